```python
import jax, jax.numpy as jnp
from jax import lax
import numpy as np

D_MODEL = 1024
BATCH = 8
SEQ = 4096
DEPTH = 2

N_EVEN = (DEPTH + 1) // 2
N_ODD = DEPTH // 2
RG_WIDTH = D_MODEL // 2
RG_BLOCKS = 8
RG_BLOCK_DIM = RG_WIDTH // RG_BLOCKS
CONV_WIDTH = 4
CONV_LEFT = 2
CONV_RIGHT = 1
RG_C = 8.0
HG_WIDTH = D_MODEL // 2
HG_HEADS = 4
HG_HEAD_DIM = HG_WIDTH // HG_HEADS
GLA_HEADS = 4
GLA_KEY_DIM = D_MODEL // 2
GLA_VALUE_DIM = D_MODEL
GLA_HEAD_K = GLA_KEY_DIM // GLA_HEADS
GLA_HEAD_V = GLA_VALUE_DIM // GLA_HEADS
GLA_LOWRANK = 16
GLA_GATE_NORMALIZER = 16.0
D_FF = 4 * D_MODEL
CHUNK = 64
EPS = 1e-6
AB_IN_SIZES = (RG_WIDTH, RG_WIDTH, HG_WIDTH, HG_WIDTH, HG_WIDTH, HG_WIDTH, HG_WIDTH)
AB_IN_WIDTH = 2 * RG_WIDTH + 5 * HG_WIDTH
AB_OUT_WIDTH = RG_WIDTH + HG_WIDTH
GLA_IN_SIZES = (GLA_KEY_DIM, GLA_KEY_DIM, GLA_VALUE_DIM, GLA_VALUE_DIM, GLA_LOWRANK, GLA_LOWRANK)
GLA_IN_WIDTH = 2 * GLA_KEY_DIM + 2 * GLA_VALUE_DIM + 2 * GLA_LOWRANK

kernel_name = "bidir_hybrid_rglru_hgrn2_gla_trunk"


def split_cols(t, sizes):
    offsets = []
    acc = 0
    for s in sizes[:-1]:
        acc += s
        offsets.append(acc)
    return jnp.split(t, offsets, axis=-1)


def rmsnorm(x, gain):
    x32 = x.astype(jnp.float32)
    y = x32 * lax.rsqrt(jnp.mean(x32 * x32, axis=-1, keepdims=True) + EPS)
    return (y * gain.astype(jnp.float32)).astype(x.dtype)


def to_heads(t, n_heads):
    b, s, w = t.shape
    return t.reshape(b, s, n_heads, w // n_heads).transpose(0, 2, 1, 3)


def head_rmsnorm(o, gain):
    b, h, s, d = o.shape
    o32 = o.astype(jnp.float32)
    o32 = o32 * lax.rsqrt(jnp.mean(o32 * o32, axis=-1, keepdims=True) + EPS)
    o32 = o32.transpose(0, 2, 1, 3).reshape(b, s, h * d)
    return (o32 * gain.astype(jnp.float32)).astype(o.dtype)


def chunk_gated_linear_attn(q, k, v, logf):
    b, h, s, dk = q.shape
    dv = v.shape[-1]
    n = s // CHUNK
    qc = q.reshape(b, h, n, CHUNK, dk)
    kc = k.reshape(b, h, n, CHUNK, dk)
    vc = v.reshape(b, h, n, CHUNK, dv)
    cum = jnp.cumsum(logf.astype(jnp.float32).reshape(b, h, n, CHUNK, dk), axis=3)
    ref = cum[:, :, :, CHUNK // 2:CHUNK // 2 + 1, :]
    last = cum[:, :, :, CHUNK - 1:, :]
    q_in = qc * jnp.exp(cum - ref)
    k_in = kc * jnp.exp(ref - cum)
    scores = jnp.einsum('bhnld,bhnmd->bhnlm', q_in, k_in)
    mask = jnp.tril(jnp.ones((CHUNK, CHUNK), dtype=bool))
    scores = jnp.where(mask, scores, 0.0)
    o_intra = jnp.einsum('bhnlm,bhnmv->bhnlv', scores, vc)
    k_state = kc * jnp.exp(last - cum)
    upd = jnp.einsum('bhnld,bhnlv->bhndv', k_state, vc)
    decay = jnp.exp(last[:, :, :, 0, :])

    def step(state, inp):
        d, u = inp
        return d[..., None] * state + u, state

    init = jnp.zeros((b, h, dk, dv), dtype=upd.dtype)
    _, s_prev = lax.scan(step, init, (jnp.moveaxis(decay, 2, 0), jnp.moveaxis(upd, 2, 0)))
    o_inter = jnp.einsum('bhnld,nbhdv->bhnlv', qc * jnp.exp(cum), s_prev)
    return (o_intra + o_inter).reshape(b, h, s, dv)


def bidir_gated_linear_attn(q, k_f, k_b, v, logf_f, logf_b):
    rev = lambda t: jnp.flip(t, axis=2)
    fwd = chunk_gated_linear_attn(q, k_f, v, logf_f)
    bwd = rev(chunk_gated_linear_attn(rev(q), rev(k_b), rev(v), rev(logf_b)))
    return fwd + bwd


def linear_scan_combine(left, right):
    a1, b1 = left
    a2, b2 = right
    return a1 * a2, a2 * b1 + b2


def rglru_branch(xa, ga, conv_w, conv_b, w_a, b_a, w_x, b_x, lam):
    bsz, s, _ = xa.shape
    xc = lax.conv_general_dilated(
        xa, conv_w[:, None, :], window_strides=(1,), padding=[(CONV_LEFT, CONV_RIGHT)],
        dimension_numbers=('NWC', 'WIO', 'NWC'), feature_group_count=RG_WIDTH) + conv_b
    xb = xc.reshape(bsz, s, RG_BLOCKS, RG_BLOCK_DIM)
    r = jax.nn.sigmoid(jnp.einsum('bsgi,dgij->dbsgj', xb, w_a).reshape(2, bsz, s, RG_WIDTH)
                       + b_a[:, None, None, :])
    i = jax.nn.sigmoid(jnp.einsum('bsgi,dgij->dbsgj', xb, w_x).reshape(2, bsz, s, RG_WIDTH)
                       + b_x[:, None, None, :])
    log_a = -RG_C * r * jax.nn.softplus(-lam)[:, None, None, :]
    a = jnp.exp(log_a)
    u = jnp.sqrt(-jnp.expm1(2.0 * log_a)) * (i * xc[None])
    _, h_f = lax.associative_scan(linear_scan_combine, (a[0], u[0]), axis=1)
    _, h_b = lax.associative_scan(linear_scan_combine, (a[1], u[1]), axis=1, reverse=True)
    return (h_f + h_b) * jax.nn.gelu(ga)


def hgrn2_branch(q, f_f, f_b, iv, g, lb, norm_gain):
    qh = to_heads(jax.nn.silu(q), HG_HEADS)
    vh = to_heads(iv, HG_HEADS)

    def gates(fpre):
        fpre32 = fpre.astype(jnp.float32)
        logf = jnp.log(lb + (1.0 - lb) * jax.nn.sigmoid(fpre32))
        k = (1.0 - lb) * jax.nn.sigmoid(-fpre32)
        return to_heads(k, HG_HEADS), to_heads(logf, HG_HEADS)

    k_f, logf_f = gates(f_f)
    k_b, logf_b = gates(f_b)
    o = bidir_gated_linear_attn(qh, k_f, k_b, vh, logf_f, logf_b)
    return head_rmsnorm(o, norm_gain) * jax.nn.silu(g)


def gla_mixer(y, w_in, w_gate_up, b_gate, norm_gain):
    proj = y @ w_in
    q, k, v, r, lr_f, lr_b = split_cols(proj, GLA_IN_SIZES)
    qh = to_heads(q, GLA_HEADS) * (GLA_HEAD_K ** -0.5)
    kh = to_heads(k, GLA_HEADS)
    vh = to_heads(v, GLA_HEADS)

    def log_gate(lr, w_up, bias):
        z = jnp.einsum('bsr,rk->bsk', lr, w_up) + bias
        return to_heads(jax.nn.log_sigmoid(z.astype(jnp.float32)) / GLA_GATE_NORMALIZER, GLA_HEADS)

    logf_f = log_gate(lr_f, w_gate_up[0], b_gate[0])
    logf_b = log_gate(lr_b, w_gate_up[1], b_gate[1])
    o = bidir_gated_linear_attn(qh, kh, kh, vh, logf_f, logf_b)
    return head_rmsnorm(o, norm_gain) * jax.nn.silu(r)


def _fwd_setup_inputs(seed: int = 0) -> dict:
    key = jax.random.key(seed)
    ks = jax.random.split(key, 24)
    nrm = lambda k, shape, scale: jax.random.normal(k, shape, jnp.float32) * scale
    u = jax.random.uniform(ks[10], (N_EVEN, 2, RG_WIDTH), jnp.float32, minval=0.9, maxval=0.999)
    p = u ** (1.0 / RG_C)
    rg_lambda = jnp.log(p) - jnp.log1p(-p)
    return {
        "x": nrm(ks[0], (BATCH, SEQ, D_MODEL), 1.0),
        "norm_mix": 1.0 + nrm(ks[1], (DEPTH, D_MODEL), 0.02),
        "norm_mlp": 1.0 + nrm(ks[2], (DEPTH, D_MODEL), 0.02),
        "norm_final": 1.0 + nrm(ks[3], (D_MODEL,), 0.02),
        "mlp_w1": nrm(ks[4], (DEPTH, D_MODEL, D_FF), D_MODEL ** -0.5),
        "mlp_w2": nrm(ks[5], (DEPTH, D_FF, D_MODEL), D_FF ** -0.5),
        "ab_w_in": nrm(ks[6], (N_EVEN, D_MODEL, AB_IN_WIDTH), D_MODEL ** -0.5),
        "ab_w_out": nrm(ks[7], (N_EVEN, AB_OUT_WIDTH, D_MODEL), AB_OUT_WIDTH ** -0.5),
        "rg_conv_w": nrm(ks[8], (N_EVEN, CONV_WIDTH, RG_WIDTH), CONV_WIDTH ** -0.5),
        "rg_conv_b": nrm(ks[9], (N_EVEN, RG_WIDTH), 0.01),
        "rg_w_a": nrm(ks[11], (N_EVEN, 2, RG_BLOCKS, RG_BLOCK_DIM, RG_BLOCK_DIM), RG_BLOCK_DIM ** -0.5),
        "rg_b_a": nrm(ks[12], (N_EVEN, 2, RG_WIDTH), 0.01),
        "rg_w_x": nrm(ks[13], (N_EVEN, 2, RG_BLOCKS, RG_BLOCK_DIM, RG_BLOCK_DIM), RG_BLOCK_DIM ** -0.5),
        "rg_b_x": nrm(ks[14], (N_EVEN, 2, RG_WIDTH), 0.01),
        "rg_lambda": rg_lambda,
        "hg_lb_logits": nrm(ks[15], (N_EVEN + 1, HG_WIDTH), 0.1),
        "hg_norm": 1.0 + nrm(ks[16], (N_EVEN, HG_WIDTH), 0.02),
        "gla_w_in": nrm(ks[17], (N_ODD, D_MODEL, GLA_IN_WIDTH), D_MODEL ** -0.5),
        "gla_w_out": nrm(ks[18], (N_ODD, GLA_VALUE_DIM, D_MODEL), GLA_VALUE_DIM ** -0.5),
        "gla_w_gate_up": nrm(ks[19], (N_ODD, 2, GLA_LOWRANK, GLA_KEY_DIM), GLA_LOWRANK ** -0.5),
        "gla_b_gate": nrm(ks[20], (N_ODD, 2, GLA_KEY_DIM), 0.01),
        "gla_norm": 1.0 + nrm(ks[21], (N_ODD, GLA_VALUE_DIM), 0.02),
    }


def _fwd_reference(x, norm_mix, norm_mlp, norm_final, mlp_w1, mlp_w2, ab_w_in, ab_w_out,
              rg_conv_w, rg_conv_b, rg_w_a, rg_b_a, rg_w_x, rg_b_x, rg_lambda,
              hg_lb_logits, hg_norm, gla_w_in, gla_w_out, gla_w_gate_up, gla_b_gate, gla_norm):
    lbs = jnp.cumsum(jax.nn.softmax(hg_lb_logits.astype(jnp.float32), axis=0), axis=0)
    h = x
    for layer in range(DEPTH):
        j = layer // 2
        y = rmsnorm(h, norm_mix[layer])
        if layer % 2 == 0:
            proj = y @ ab_w_in[j]
            xa, ga, q, f_f, f_b, iv, g = split_cols(proj, AB_IN_SIZES)
            ya = rglru_branch(xa, ga, rg_conv_w[j], rg_conv_b[j], rg_w_a[j], rg_b_a[j],
                              rg_w_x[j], rg_b_x[j], rg_lambda[j])
            yb = hgrn2_branch(q, f_f, f_b, iv, g, lbs[j], hg_norm[j])
            mix = jnp.concatenate([ya, yb], axis=-1) @ ab_w_out[j]
        else:
            mix = gla_mixer(y, gla_w_in[j], gla_w_gate_up[j], gla_b_gate[j], gla_norm[j]) @ gla_w_out[j]
        h = h + mix
        y = rmsnorm(h, norm_mlp[layer])
        h = h + jnp.square(jax.nn.relu(y @ mlp_w1[layer])) @ mlp_w2[layer]
    return rmsnorm(h, norm_final)


import jax as _jax
import jax.numpy as _jnp

TWIN_FORMAT = 'train_step'
FWD_PARAMS = ['x', 'norm_mix', 'norm_mlp', 'norm_final', 'mlp_w1', 'mlp_w2', 'ab_w_in', 'ab_w_out', 'rg_conv_w', 'rg_conv_b', 'rg_w_a', 'rg_b_a', 'rg_w_x', 'rg_b_x', 'rg_lambda', 'hg_lb_logits', 'hg_norm', 'gla_w_in', 'gla_w_out', 'gla_w_gate_up', 'gla_b_gate', 'gla_norm']
TWIN_WEIGHTS = ['norm_mix', 'norm_mlp', 'norm_final', 'mlp_w1', 'mlp_w2', 'ab_w_in', 'ab_w_out', 'rg_conv_w', 'rg_conv_b', 'rg_w_a', 'rg_b_a', 'rg_w_x', 'rg_b_x', 'rg_lambda', 'hg_lb_logits', 'hg_norm', 'gla_w_in', 'gla_w_out', 'gla_w_gate_up', 'gla_b_gate', 'gla_norm']
TWIN_DIFF_INPUT = 'x'
TWIN_INPUTS = ['x', 'norm_mix', 'norm_mlp', 'norm_final', 'mlp_w1', 'mlp_w2', 'ab_w_in', 'ab_w_out', 'rg_conv_w', 'rg_conv_b', 'rg_w_a', 'rg_b_a', 'rg_w_x', 'rg_b_x', 'rg_lambda', 'hg_lb_logits', 'hg_norm', 'gla_w_in', 'gla_w_out', 'gla_w_gate_up', 'gla_b_gate', 'gla_norm', 'loss_target', 'm_norm_mix', 'm_norm_mlp', 'm_norm_final', 'm_mlp_w1', 'm_mlp_w2', 'm_ab_w_in', 'm_ab_w_out', 'm_rg_conv_w', 'm_rg_conv_b', 'm_rg_w_a', 'm_rg_b_a', 'm_rg_w_x', 'm_rg_b_x', 'm_rg_lambda', 'm_hg_lb_logits', 'm_hg_norm', 'm_gla_w_in', 'm_gla_w_out', 'm_gla_w_gate_up', 'm_gla_b_gate', 'm_gla_norm', 'v_norm_mix', 'v_norm_mlp', 'v_norm_final', 'v_mlp_w1', 'v_mlp_w2', 'v_ab_w_in', 'v_ab_w_out', 'v_rg_conv_w', 'v_rg_conv_b', 'v_rg_w_a', 'v_rg_b_a', 'v_rg_w_x', 'v_rg_b_x', 'v_rg_lambda', 'v_hg_lb_logits', 'v_hg_norm', 'v_gla_w_in', 'v_gla_w_out', 'v_gla_w_gate_up', 'v_gla_b_gate', 'v_gla_norm']
TWIN_OUTPUTS = ['loss', 'grad_x', 'grad_norm_mix', 'grad_norm_mlp', 'grad_norm_final', 'grad_mlp_w1', 'grad_mlp_w2', 'grad_ab_w_in', 'grad_ab_w_out', 'grad_rg_conv_w', 'grad_rg_conv_b', 'grad_rg_w_a', 'grad_rg_b_a', 'grad_rg_w_x', 'grad_rg_b_x', 'grad_rg_lambda', 'grad_hg_lb_logits', 'grad_hg_norm', 'grad_gla_w_in', 'grad_gla_w_out', 'grad_gla_w_gate_up', 'grad_gla_b_gate', 'grad_gla_norm', 'delta_norm_mix', 'delta_norm_mlp', 'delta_norm_final', 'delta_mlp_w1', 'delta_mlp_w2', 'delta_ab_w_in', 'delta_ab_w_out', 'delta_rg_conv_w', 'delta_rg_conv_b', 'delta_rg_w_a', 'delta_rg_b_a', 'delta_rg_w_x', 'delta_rg_b_x', 'delta_rg_lambda', 'delta_hg_lb_logits', 'delta_hg_norm', 'delta_gla_w_in', 'delta_gla_w_out', 'delta_gla_w_gate_up', 'delta_gla_b_gate', 'delta_gla_norm', 'new_m_norm_mix', 'new_m_norm_mlp', 'new_m_norm_final', 'new_m_mlp_w1', 'new_m_mlp_w2', 'new_m_ab_w_in', 'new_m_ab_w_out', 'new_m_rg_conv_w', 'new_m_rg_conv_b', 'new_m_rg_w_a', 'new_m_rg_b_a', 'new_m_rg_w_x', 'new_m_rg_b_x', 'new_m_rg_lambda', 'new_m_hg_lb_logits', 'new_m_hg_norm', 'new_m_gla_w_in', 'new_m_gla_w_out', 'new_m_gla_w_gate_up', 'new_m_gla_b_gate', 'new_m_gla_norm', 'new_v_norm_mix', 'new_v_norm_mlp', 'new_v_norm_final', 'new_v_mlp_w1', 'new_v_mlp_w2', 'new_v_ab_w_in', 'new_v_ab_w_out', 'new_v_rg_conv_w', 'new_v_rg_conv_b', 'new_v_rg_w_a', 'new_v_rg_b_a', 'new_v_rg_w_x', 'new_v_rg_b_x', 'new_v_rg_lambda', 'new_v_hg_lb_logits', 'new_v_hg_norm', 'new_v_gla_w_in', 'new_v_gla_w_out', 'new_v_gla_w_gate_up', 'new_v_gla_b_gate', 'new_v_gla_norm']
TWIN_LEAF_KINDS = {'loss': 'loss', 'grad_x': 'grad_x', 'grad_norm_mix': 'grad_w', 'grad_norm_mlp': 'grad_w', 'grad_norm_final': 'grad_w', 'grad_mlp_w1': 'grad_w', 'grad_mlp_w2': 'grad_w', 'grad_ab_w_in': 'grad_w', 'grad_ab_w_out': 'grad_w', 'grad_rg_conv_w': 'grad_w', 'grad_rg_conv_b': 'grad_w', 'grad_rg_w_a': 'grad_w', 'grad_rg_b_a': 'grad_w', 'grad_rg_w_x': 'grad_w', 'grad_rg_b_x': 'grad_w', 'grad_rg_lambda': 'grad_w', 'grad_hg_lb_logits': 'grad_w', 'grad_hg_norm': 'grad_w', 'grad_gla_w_in': 'grad_w', 'grad_gla_w_out': 'grad_w', 'grad_gla_w_gate_up': 'grad_w', 'grad_gla_b_gate': 'grad_w', 'grad_gla_norm': 'grad_w', 'delta_norm_mix': 'delta_w', 'delta_norm_mlp': 'delta_w', 'delta_norm_final': 'delta_w', 'delta_mlp_w1': 'delta_w', 'delta_mlp_w2': 'delta_w', 'delta_ab_w_in': 'delta_w', 'delta_ab_w_out': 'delta_w', 'delta_rg_conv_w': 'delta_w', 'delta_rg_conv_b': 'delta_w', 'delta_rg_w_a': 'delta_w', 'delta_rg_b_a': 'delta_w', 'delta_rg_w_x': 'delta_w', 'delta_rg_b_x': 'delta_w', 'delta_rg_lambda': 'delta_w', 'delta_hg_lb_logits': 'delta_w', 'delta_hg_norm': 'delta_w', 'delta_gla_w_in': 'delta_w', 'delta_gla_w_out': 'delta_w', 'delta_gla_w_gate_up': 'delta_w', 'delta_gla_b_gate': 'delta_w', 'delta_gla_norm': 'delta_w', 'new_m_norm_mix': 'new_m', 'new_m_norm_mlp': 'new_m', 'new_m_norm_final': 'new_m', 'new_m_mlp_w1': 'new_m', 'new_m_mlp_w2': 'new_m', 'new_m_ab_w_in': 'new_m', 'new_m_ab_w_out': 'new_m', 'new_m_rg_conv_w': 'new_m', 'new_m_rg_conv_b': 'new_m', 'new_m_rg_w_a': 'new_m', 'new_m_rg_b_a': 'new_m', 'new_m_rg_w_x': 'new_m', 'new_m_rg_b_x': 'new_m', 'new_m_rg_lambda': 'new_m', 'new_m_hg_lb_logits': 'new_m', 'new_m_hg_norm': 'new_m', 'new_m_gla_w_in': 'new_m', 'new_m_gla_w_out': 'new_m', 'new_m_gla_w_gate_up': 'new_m', 'new_m_gla_b_gate': 'new_m', 'new_m_gla_norm': 'new_m', 'new_v_norm_mix': 'new_v', 'new_v_norm_mlp': 'new_v', 'new_v_norm_final': 'new_v', 'new_v_mlp_w1': 'new_v', 'new_v_mlp_w2': 'new_v', 'new_v_ab_w_in': 'new_v', 'new_v_ab_w_out': 'new_v', 'new_v_rg_conv_w': 'new_v', 'new_v_rg_conv_b': 'new_v', 'new_v_rg_w_a': 'new_v', 'new_v_rg_b_a': 'new_v', 'new_v_rg_w_x': 'new_v', 'new_v_rg_b_x': 'new_v', 'new_v_rg_lambda': 'new_v', 'new_v_hg_lb_logits': 'new_v', 'new_v_hg_norm': 'new_v', 'new_v_gla_w_in': 'new_v', 'new_v_gla_w_out': 'new_v', 'new_v_gla_w_gate_up': 'new_v', 'new_v_gla_b_gate': 'new_v', 'new_v_gla_norm': 'new_v'}


def _forward(args):
    return _fwd_reference(*[args[k] for k in FWD_PARAMS])


def _output_shape():
    out = _jax.eval_shape(lambda: _forward(_fwd_setup_inputs(0)))
    return out.shape, out.dtype

N_MICROBATCH = 1
ADAM_LR = 0.001
ADAM_B1 = 0.9
ADAM_B2 = 0.999
ADAM_EPS = 1e-08
ADAM_WD = 0.01
ADAM_STEP = 10
PER_EXAMPLE_BATCH_AXIS = {'x': 0, 'loss_target': 0}
SHARED_INPUTS = []
_WEIGHT_DTYPES = {'norm_mix': _jnp.float32, 'norm_mlp': _jnp.float32, 'norm_final': _jnp.float32, 'mlp_w1': _jnp.float32, 'mlp_w2': _jnp.float32, 'ab_w_in': _jnp.float32, 'ab_w_out': _jnp.float32, 'rg_conv_w': _jnp.float32, 'rg_conv_b': _jnp.float32, 'rg_w_a': _jnp.float32, 'rg_b_a': _jnp.float32, 'rg_w_x': _jnp.float32, 'rg_b_x': _jnp.float32, 'rg_lambda': _jnp.float32, 'hg_lb_logits': _jnp.float32, 'hg_norm': _jnp.float32, 'gla_w_in': _jnp.float32, 'gla_w_out': _jnp.float32, 'gla_w_gate_up': _jnp.float32, 'gla_b_gate': _jnp.float32, 'gla_norm': _jnp.float32}
MOMENT_SCALE = {'norm_mix': 1.723874e-01, 'norm_mlp': 1.541888e-01, 'norm_final': 3.241887e+01, 'mlp_w1': 7.591523e-02, 'mlp_w2': 1.467467e-01, 'ab_w_in': 9.247644e-02, 'ab_w_out': 1.263222e-01, 'rg_conv_w': 1.512815e-01, 'rg_conv_b': 2.274538e+00, 'rg_w_a': 3.666935e-02, 'rg_b_a': 3.637787e-02, 'rg_w_x': 6.841646e-02, 'rg_b_x': 3.464598e-02, 'rg_lambda': 4.662301e-02, 'hg_lb_logits': 1.012799e-02, 'hg_norm': 1.256180e-01, 'gla_w_in': 7.843209e-02, 'gla_w_out': 6.260772e-02, 'gla_w_gate_up': 1.125682e-02, 'gla_b_gate': 3.117248e-02, 'gla_norm': 6.862004e-02}


def _to_microbatches(a, axis):
    t = _jnp.moveaxis(a, axis, 0)
    t = t.reshape((N_MICROBATCH, t.shape[0] // N_MICROBATCH) + t.shape[1:])
    return _jnp.moveaxis(t, 1, axis + 1)


def setup_inputs(seed: int = 0) -> dict:
    inp = _fwd_setup_inputs(seed)
    key = _jax.random.fold_in(_jax.random.key(seed), 7919)
    shape, _ = _output_shape()
    out = dict(inp)
    out["loss_target"] = _jax.random.normal(_jax.random.fold_in(key, 0), shape, _jnp.float32)
    for i, name in enumerate(TWIN_WEIGHTS):
        w = inp[name].astype(_jnp.float32)
        if MOMENT_SCALE is None:
            s = _jnp.sqrt(_jnp.mean(_jnp.square(w)) + 1e-30)
        else:
            s = MOMENT_SCALE[name]
        km, kv = _jax.random.split(_jax.random.fold_in(key, i + 1))
        out[name] = w
        out["m_" + name] = s * _jax.random.normal(km, w.shape, _jnp.float32)
        out["v_" + name] = (s * s) * _jax.random.uniform(kv, w.shape, _jnp.float32, 0.5, 1.5)
    if N_MICROBATCH > 1:
        for name, axis in PER_EXAMPLE_BATCH_AXIS.items():
            out[name] = _to_microbatches(out[name], axis)
    return {'x': out['x'], 'norm_mix': out['norm_mix'], 'norm_mlp': out['norm_mlp'], 'norm_final': out['norm_final'], 'mlp_w1': out['mlp_w1'], 'mlp_w2': out['mlp_w2'], 'ab_w_in': out['ab_w_in'], 'ab_w_out': out['ab_w_out'], 'rg_conv_w': out['rg_conv_w'], 'rg_conv_b': out['rg_conv_b'], 'rg_w_a': out['rg_w_a'], 'rg_b_a': out['rg_b_a'], 'rg_w_x': out['rg_w_x'], 'rg_b_x': out['rg_b_x'], 'rg_lambda': out['rg_lambda'], 'hg_lb_logits': out['hg_lb_logits'], 'hg_norm': out['hg_norm'], 'gla_w_in': out['gla_w_in'], 'gla_w_out': out['gla_w_out'], 'gla_w_gate_up': out['gla_w_gate_up'], 'gla_b_gate': out['gla_b_gate'], 'gla_norm': out['gla_norm'], 'loss_target': out['loss_target'], 'm_norm_mix': out['m_norm_mix'], 'm_norm_mlp': out['m_norm_mlp'], 'm_norm_final': out['m_norm_final'], 'm_mlp_w1': out['m_mlp_w1'], 'm_mlp_w2': out['m_mlp_w2'], 'm_ab_w_in': out['m_ab_w_in'], 'm_ab_w_out': out['m_ab_w_out'], 'm_rg_conv_w': out['m_rg_conv_w'], 'm_rg_conv_b': out['m_rg_conv_b'], 'm_rg_w_a': out['m_rg_w_a'], 'm_rg_b_a': out['m_rg_b_a'], 'm_rg_w_x': out['m_rg_w_x'], 'm_rg_b_x': out['m_rg_b_x'], 'm_rg_lambda': out['m_rg_lambda'], 'm_hg_lb_logits': out['m_hg_lb_logits'], 'm_hg_norm': out['m_hg_norm'], 'm_gla_w_in': out['m_gla_w_in'], 'm_gla_w_out': out['m_gla_w_out'], 'm_gla_w_gate_up': out['m_gla_w_gate_up'], 'm_gla_b_gate': out['m_gla_b_gate'], 'm_gla_norm': out['m_gla_norm'], 'v_norm_mix': out['v_norm_mix'], 'v_norm_mlp': out['v_norm_mlp'], 'v_norm_final': out['v_norm_final'], 'v_mlp_w1': out['v_mlp_w1'], 'v_mlp_w2': out['v_mlp_w2'], 'v_ab_w_in': out['v_ab_w_in'], 'v_ab_w_out': out['v_ab_w_out'], 'v_rg_conv_w': out['v_rg_conv_w'], 'v_rg_conv_b': out['v_rg_conv_b'], 'v_rg_w_a': out['v_rg_w_a'], 'v_rg_b_a': out['v_rg_b_a'], 'v_rg_w_x': out['v_rg_w_x'], 'v_rg_b_x': out['v_rg_b_x'], 'v_rg_lambda': out['v_rg_lambda'], 'v_hg_lb_logits': out['v_hg_lb_logits'], 'v_hg_norm': out['v_hg_norm'], 'v_gla_w_in': out['v_gla_w_in'], 'v_gla_w_out': out['v_gla_w_out'], 'v_gla_w_gate_up': out['v_gla_w_gate_up'], 'v_gla_b_gate': out['v_gla_b_gate'], 'v_gla_norm': out['v_gla_norm']}


def _loss(weights, diff, rest, loss_target):
    with _jax.named_scope("forward"):
        args = {**rest, TWIN_DIFF_INPUT: diff, **{k: w.astype(_WEIGHT_DTYPES[k]) for k, w in weights.items()}}
        y = _forward(args)
    with _jax.named_scope("loss_head"):
        err = _jnp.square(y.astype(_jnp.float32) - loss_target)
        return 0.5 * _jnp.sum(_jnp.mean(err, axis=-1)) if err.ndim else 0.5 * err


def _adamw(w, g, m, v):
    m = ADAM_B1 * m + (1.0 - ADAM_B1) * g
    v = ADAM_B2 * v + (1.0 - ADAM_B2) * _jnp.square(g)
    m_hat = m / (1.0 - ADAM_B1 ** ADAM_STEP)
    v_hat = v / (1.0 - ADAM_B2 ** ADAM_STEP)
    delta = -ADAM_LR * (m_hat / (_jnp.sqrt(v_hat) + ADAM_EPS) + ADAM_WD * w)
    return delta, m, v


def reference(x, norm_mix, norm_mlp, norm_final, mlp_w1, mlp_w2, ab_w_in, ab_w_out, rg_conv_w, rg_conv_b, rg_w_a, rg_b_a, rg_w_x, rg_b_x, rg_lambda, hg_lb_logits, hg_norm, gla_w_in, gla_w_out, gla_w_gate_up, gla_b_gate, gla_norm, loss_target, m_norm_mix, m_norm_mlp, m_norm_final, m_mlp_w1, m_mlp_w2, m_ab_w_in, m_ab_w_out, m_rg_conv_w, m_rg_conv_b, m_rg_w_a, m_rg_b_a, m_rg_w_x, m_rg_b_x, m_rg_lambda, m_hg_lb_logits, m_hg_norm, m_gla_w_in, m_gla_w_out, m_gla_w_gate_up, m_gla_b_gate, m_gla_norm, v_norm_mix, v_norm_mlp, v_norm_final, v_mlp_w1, v_mlp_w2, v_ab_w_in, v_ab_w_out, v_rg_conv_w, v_rg_conv_b, v_rg_w_a, v_rg_b_a, v_rg_w_x, v_rg_b_x, v_rg_lambda, v_hg_lb_logits, v_hg_norm, v_gla_w_in, v_gla_w_out, v_gla_w_gate_up, v_gla_b_gate, v_gla_norm):
    given = dict(x=x, norm_mix=norm_mix, norm_mlp=norm_mlp, norm_final=norm_final, mlp_w1=mlp_w1, mlp_w2=mlp_w2, ab_w_in=ab_w_in, ab_w_out=ab_w_out, rg_conv_w=rg_conv_w, rg_conv_b=rg_conv_b, rg_w_a=rg_w_a, rg_b_a=rg_b_a, rg_w_x=rg_w_x, rg_b_x=rg_b_x, rg_lambda=rg_lambda, hg_lb_logits=hg_lb_logits, hg_norm=hg_norm, gla_w_in=gla_w_in, gla_w_out=gla_w_out, gla_w_gate_up=gla_w_gate_up, gla_b_gate=gla_b_gate, gla_norm=gla_norm, loss_target=loss_target, m_norm_mix=m_norm_mix, m_norm_mlp=m_norm_mlp, m_norm_final=m_norm_final, m_mlp_w1=m_mlp_w1, m_mlp_w2=m_mlp_w2, m_ab_w_in=m_ab_w_in, m_ab_w_out=m_ab_w_out, m_rg_conv_w=m_rg_conv_w, m_rg_conv_b=m_rg_conv_b, m_rg_w_a=m_rg_w_a, m_rg_b_a=m_rg_b_a, m_rg_w_x=m_rg_w_x, m_rg_b_x=m_rg_b_x, m_rg_lambda=m_rg_lambda, m_hg_lb_logits=m_hg_lb_logits, m_hg_norm=m_hg_norm, m_gla_w_in=m_gla_w_in, m_gla_w_out=m_gla_w_out, m_gla_w_gate_up=m_gla_w_gate_up, m_gla_b_gate=m_gla_b_gate, m_gla_norm=m_gla_norm, v_norm_mix=v_norm_mix, v_norm_mlp=v_norm_mlp, v_norm_final=v_norm_final, v_mlp_w1=v_mlp_w1, v_mlp_w2=v_mlp_w2, v_ab_w_in=v_ab_w_in, v_ab_w_out=v_ab_w_out, v_rg_conv_w=v_rg_conv_w, v_rg_conv_b=v_rg_conv_b, v_rg_w_a=v_rg_w_a, v_rg_b_a=v_rg_b_a, v_rg_w_x=v_rg_w_x, v_rg_b_x=v_rg_b_x, v_rg_lambda=v_rg_lambda, v_hg_lb_logits=v_hg_lb_logits, v_hg_norm=v_hg_norm, v_gla_w_in=v_gla_w_in, v_gla_w_out=v_gla_w_out, v_gla_w_gate_up=v_gla_w_gate_up, v_gla_b_gate=v_gla_b_gate, v_gla_norm=v_gla_norm)
    weights = {n: given[n] for n in TWIN_WEIGHTS}
    shared = {n: given[n] for n in SHARED_INPUTS}
    per_example = {n: given[n] for n in ['x']}
    grad_fn = _jax.value_and_grad(_loss, argnums=(0, 1))

    def one_microbatch(ex, loss_target):
        ex = dict(ex)
        diff = ex.pop(TWIN_DIFF_INPUT)
        return grad_fn(weights, diff, {**shared, **ex}, loss_target)

    if N_MICROBATCH == 1:
        loss, (grad_w, grad_x) = one_microbatch(per_example, given["loss_target"])
    else:
        def body(carry, xs):
            loss_sum, grad_sum = carry
            l_k, (gw_k, gx_k) = one_microbatch(xs[0], xs[1])
            with _jax.named_scope("update"):
                return (loss_sum + l_k, _jax.tree.map(_jnp.add, grad_sum, gw_k)), gx_k

        init = (_jnp.zeros((), _jnp.float32), _jax.tree.map(_jnp.zeros_like, weights))
        (loss, grad_w), grad_x = _jax.lax.scan(body, init, (per_example, given["loss_target"]))
    with _jax.named_scope("update"):
        delta_w, new_m, new_v = {}, {}, {}
        for n in TWIN_WEIGHTS:
            delta_w[n], new_m[n], new_v[n] = _adamw(weights[n], grad_w[n], given["m_" + n], given["v_" + n])
    return (loss, grad_x, *[grad_w[n] for n in TWIN_WEIGHTS], *[delta_w[n] for n in TWIN_WEIGHTS],
            *[new_m[n] for n in TWIN_WEIGHTS], *[new_v[n] for n in TWIN_WEIGHTS])
```

```python
import functools

import jax
import jax.numpy as jnp
from jax import lax
from jax.experimental import pallas as pl
from jax.experimental.pallas import tpu as pltpu

F32 = jnp.float32
BF16 = jnp.bfloat16
MESH = pl.DeviceIdType.MESH

LANES = 128
CHUNK = 64
EPS = 1e-6
RG_C = 8.0
N_CHIPS = 4
N_DEV = 8
VMEM_LIMIT = 56 * 1024 * 1024

ADAM_LR = 0.001
ADAM_B1 = 0.9
ADAM_B2 = 0.999
ADAM_EPS = 1e-08
ADAM_WD = 0.01
ADAM_STEP = 10


def _raw_dot(a, b, ca, cb):
    return lax.dot_general(a.astype(BF16), b.astype(BF16), (((ca,), (cb,)), ((), ())),
                           preferred_element_type=F32)


def _raw_nn(a, b):
    return _raw_dot(a, b, 1, 0)


def _raw_nt(a, b):
    return _raw_dot(a, b, 1, 1)


def _raw_tn(a, b):
    return _raw_dot(a, b, 0, 0)


@jax.custom_vjp
def _dot_nn(a, b):
    return _raw_nn(a, b)


def _dot_nn_fwd(a, b):
    return _raw_nn(a, b), (a, b)


def _dot_nn_bwd(res, g):
    a, b = res
    return _raw_nt(g, b), _raw_tn(a, g)


_dot_nn.defvjp(_dot_nn_fwd, _dot_nn_bwd)


@jax.custom_vjp
def _dot_nt(a, b):
    return _raw_nt(a, b)


def _dot_nt_fwd(a, b):
    return _raw_nt(a, b), (a, b)


def _dot_nt_bwd(res, g):
    a, b = res
    return _raw_nn(g, b), _raw_tn(g, a)


_dot_nt.defvjp(_dot_nt_fwd, _dot_nt_bwd)


@jax.custom_vjp
def _dot_tn(a, b):
    return _raw_tn(a, b)


def _dot_tn_fwd(a, b):
    return _raw_tn(a, b), (a, b)


def _dot_tn_bwd(res, g):
    a, b = res
    return _raw_nt(b, g), _raw_nn(a, g)


_dot_tn.defvjp(_dot_tn_fwd, _dot_tn_bwd)


def _tile(n, pref):
    if n <= pref:
        return n
    t = (pref // LANES) * LANES
    while t > LANES and n % t:
        t -= LANES
    assert n % t == 0, (n, pref)
    return t


def _params(sem):
    return pltpu.CompilerParams(dimension_semantics=sem, vmem_limit_bytes=VMEM_LIMIT)


def _rowcall(name, fn, rows, pars, row_outs, par_outs=(), tm=256):
    n_rows = rows[0][0].shape[0]
    tm = min(tm, n_rows)
    assert n_rows % tm == 0
    n_r, n_p, n_ro = len(rows), len(pars), len(row_outs)

    def body(*refs):
        vals = [r[...].astype(F32) for r in refs[:n_r + n_p]]
        outs = fn(*vals)
        o_refs = refs[n_r + n_p:n_r + n_p + n_ro]
        po_refs = refs[n_r + n_p + n_ro:]
        for o_ref, val in zip(o_refs, outs[:n_ro]):
            o_ref[...] = val.astype(o_ref.dtype)
        first = pl.program_id(0) == 0
        for po_ref, val in zip(po_refs, outs[n_ro:]):
            @pl.when(first)
            def _():
                po_ref[...] = val

            @pl.when(jnp.logical_not(first))
            def _():
                po_ref[...] += val

    def const_map(nd):
        return lambda i: (0,) * nd

    in_specs = [pl.BlockSpec((tm, w), functools.partial(lambda i, cb: (i, cb), cb=cb)) for _, w, cb in rows]
    in_specs += [pl.BlockSpec(p.shape, const_map(p.ndim)) for p in pars]
    out_specs = [pl.BlockSpec((tm, w), lambda i: (i, 0)) for w, _ in row_outs]
    out_specs += [pl.BlockSpec(tuple(s), const_map(len(s))) for s in par_outs]
    out_shape = [jax.ShapeDtypeStruct((n_rows, w), dt) for w, dt in row_outs]
    out_shape += [jax.ShapeDtypeStruct(tuple(s), F32) for s in par_outs]
    return pl.pallas_call(
        body, name=name, grid=(n_rows // tm,), in_specs=in_specs, out_specs=out_specs, out_shape=out_shape,
        compiler_params=_params(("arbitrary",) if par_outs else ("parallel",)),
    )(*[r[0] for r in rows], *pars)


def _vjp_of(fn, n_prim, n_out, n_par, n_pass=0):
    def bwd(*args):
        prim = args[:n_prim]
        cts = args[n_prim:n_prim + n_out]
        passes = args[n_prim + n_out:n_prim + n_out + 2 * n_pass]
        pars = args[n_prim + n_out + 2 * n_pass:]
        _, vjp = jax.vjp(fn, *prim, *pars)
        grads = vjp(tuple(cts))
        sums = tuple(passes[2 * i] + passes[2 * i + 1] for i in range(n_pass))
        return tuple(grads[:n_prim]) + sums + tuple(grads[n_prim:])
    return bwd


def _mm(name, a, b, mode="nn", extras=(), epi=None, out_dtypes=(F32,), tm=512, tn=512, tk=1024):
    if mode == "nn":
        (m, k), n = a.shape, b.shape[1]
    elif mode == "nt":
        (m, k), n = a.shape, b.shape[0]
    else:
        (k, m), n = a.shape, b.shape[1]
    tm, tn, tk = _tile(m, tm), _tile(n, tn), _tile(k, tk)
    nk = k // tk
    raw = {"nn": _raw_nn, "nt": _raw_nt, "tn": _raw_tn}[mode]
    n_e, n_o = len(extras), len(out_dtypes)
    if epi is None:
        epi = lambda acc: (acc,)

    def body(a_ref, b_ref, *rest):
        e_refs, o_refs, acc = rest[:n_e], rest[n_e:n_e + n_o], rest[-1]
        kk = pl.program_id(2)

        @pl.when(kk == 0)
        def _():
            acc[...] = jnp.zeros_like(acc)

        acc[...] += raw(a_ref[...], b_ref[...])

        @pl.when(kk == nk - 1)
        def _():
            res = epi(acc[...], *[e[...].astype(F32) for e in e_refs])
            for o_ref, r in zip(o_refs, res):
                o_ref[...] = r.astype(o_ref.dtype)

    a_spec = pl.BlockSpec((tk, tm), lambda i, j, kk: (kk, i)) if mode == "tn" else pl.BlockSpec((tm, tk), lambda i, j, kk: (i, kk))
    b_spec = pl.BlockSpec((tn, tk), lambda i, j, kk: (j, kk)) if mode == "nt" else pl.BlockSpec((tk, tn), lambda i, j, kk: (kk, j))
    mn_spec = pl.BlockSpec((tm, tn), lambda i, j, kk: (i, j))
    outs = pl.pallas_call(
        body, name=name, grid=(m // tm, n // tn, nk),
        in_specs=[a_spec, b_spec] + [mn_spec] * n_e, out_specs=[mn_spec] * n_o,
        out_shape=[jax.ShapeDtypeStruct((m, n), dt) for dt in out_dtypes],
        scratch_shapes=[pltpu.VMEM((tm, tn), F32)],
        compiler_params=_params(("parallel", "parallel", "arbitrary")),
    )(a, b, *extras)
    return outs[0] if n_o == 1 else outs


def _sigmoid(x):
    return jax.nn.sigmoid(x)


def _silu(x):
    return x * _sigmoid(x)


def _softplus(x):
    return jnp.maximum(x, 0.0) + jnp.log1p(jnp.exp(-jnp.abs(x)))


def _rmsnorm_fn(x, gain):
    return (x * lax.rsqrt(jnp.mean(x * x, axis=-1, keepdims=True) + EPS) * gain,)


def _head_norm(o, gain, n_heads):
    w = o.shape[-1] // n_heads
    parts = []
    for h in range(n_heads):
        oh = o[:, h * w:(h + 1) * w]
        parts.append(oh * lax.rsqrt(jnp.mean(oh * oh, axis=-1, keepdims=True) + EPS))
    return jnp.concatenate(parts, axis=-1) * gain


@jax.custom_jvp
def _neg_expm1(x):
    u = jnp.exp(x)
    is_one = u == 1.0
    return jnp.where(is_one, -x, (1.0 - u) * x / jnp.log(jnp.where(is_one, 2.0, u)))


@_neg_expm1.defjvp
def _neg_expm1_jvp(primals, tangents):
    (x,), (t,) = primals, tangents
    return _neg_expm1(x), -jnp.exp(x) * t


def _rg_gates_fn(xc, wa, wx, ba, bx, lam):
    outs = []
    for d in range(2):
        r = _sigmoid(_dot_nn(xc, wa[d]) + ba[d:d + 1])
        i = _sigmoid(_dot_nn(xc, wx[d]) + bx[d:d + 1])
        log_a = -RG_C * r * _softplus(-lam[d:d + 1])
        outs.append(jnp.exp(log_a))
        outs.append(jnp.sqrt(_neg_expm1(2.0 * log_a)) * (i * xc))
    return tuple(outs)


def _hg_pre_fn(q, f_f, f_b, logits):
    mx = jnp.maximum(logits[0:1], logits[1:2])
    e0 = jnp.exp(logits[0:1] - mx)
    e1 = jnp.exp(logits[1:2] - mx)
    lb = e0 / (e0 + e1)
    outs = [_silu(q)]
    for f in (f_f, f_b):
        outs.append((1.0 - lb) * _sigmoid(-f))
        outs.append(jnp.log(lb + (1.0 - lb) * _sigmoid(f)))
    return tuple(outs)


def _post0_fn(h_f, h_b, ga, o_f, o_b, g, gain):
    ya = (h_f + h_b) * jax.nn.gelu(ga, approximate=True)
    yb = _head_norm(o_f + o_b, gain, 4) * _silu(g)
    return (jnp.concatenate([ya, yb], axis=-1),)


def _gla_pre_fn(q, lr, w_up, b_gate):
    outs = [q * (128.0 ** -0.5)]
    for d in range(2):
        z = _dot_nn(lr, w_up[d]) + b_gate[d:d + 1]
        outs.append(-_softplus(-z) * (1.0 / 16.0))
    return tuple(outs)


def _gla_post_fn(o_f, o_b, r, gain):
    return (_head_norm(o_f + o_b, gain, 4) * _silu(r),)


def _relu2_bwd_epi(acc, hid):
    return (acc * 2.0 * jnp.maximum(hid, 0.0),)


def _relu2_epi(acc):
    r = jnp.maximum(acc, 0.0)
    return acc, r * r


def _add_epi(acc, res):
    return (acc + res,)


def _loss_head_fn(h, target, gain):
    def f(h, gain):
        y = _rmsnorm_fn(h, gain)[0]
        err = y - target
        return 0.5 * jnp.sum(jnp.mean(err * err, axis=-1, keepdims=True))
    loss, (dh, dgain) = jax.value_and_grad(f, argnums=(0, 1))(h, gain)
    return dh, jnp.full((1, LANES), loss, F32), dgain


def _adam_fn(w, g, m, v):
    m2 = ADAM_B1 * m + (1.0 - ADAM_B1) * g
    v2 = ADAM_B2 * v + (1.0 - ADAM_B2) * (g * g)
    m_hat = m2 / (1.0 - ADAM_B1 ** ADAM_STEP)
    v_hat = v2 / (1.0 - ADAM_B2 ** ADAM_STEP)
    delta = -ADAM_LR * (m_hat / (jnp.sqrt(v_hat) + ADAM_EPS) + ADAM_WD * w)
    return delta, m2, v2


def _shifted(x, t_idx, off):
    n = x.shape[0]
    rolled = pltpu.roll(x, (-off) % n, 0)
    valid = (t_idx + off >= 0) & (t_idx + off < n)
    return jnp.where(valid, rolled, 0.0)


def _conv_fwd(name, src, colblock, w, b):
    n_rows, width = src.shape[0], w.shape[1]

    def body(x_ref, w_ref, b_ref, o_ref):
        x = x_ref[...]
        t_idx = lax.broadcasted_iota(jnp.int32, x.shape, 0)
        acc = b_ref[...] + w_ref[2:3, :] * x
        acc += w_ref[0:1, :] * _shifted(x, t_idx, -2)
        acc += w_ref[1:2, :] * _shifted(x, t_idx, -1)
        acc += w_ref[3:4, :] * _shifted(x, t_idx, 1)
        o_ref[...] = acc

    nb = width // LANES
    return pl.pallas_call(
        body, name=name, grid=(nb,),
        in_specs=[pl.BlockSpec((n_rows, LANES), lambda j: (0, colblock * nb + j)),
                  pl.BlockSpec((4, LANES), lambda j: (0, j)), pl.BlockSpec((1, LANES), lambda j: (0, j))],
        out_specs=pl.BlockSpec((n_rows, LANES), lambda j: (0, j)),
        out_shape=jax.ShapeDtypeStruct((n_rows, width), F32),
        compiler_params=_params(("parallel",)),
    )(src, w, b)


def _conv_bwd(name, src, colblock, w, d):
    n_rows, width = src.shape[0], w.shape[1]

    def body(x_ref, w_ref, d_ref, dx_ref, dw_ref, db_ref):
        x = x_ref[...]
        g = d_ref[...]
        t_idx = lax.broadcasted_iota(jnp.int32, x.shape, 0)
        dx = w_ref[2:3, :] * g
        dx += w_ref[0:1, :] * _shifted(g, t_idx, 2)
        dx += w_ref[1:2, :] * _shifted(g, t_idx, 1)
        dx += w_ref[3:4, :] * _shifted(g, t_idx, -1)
        dx_ref[...] = dx
        dw_ref[0:1, :] = jnp.sum(g * _shifted(x, t_idx, -2), axis=0, keepdims=True)
        dw_ref[1:2, :] = jnp.sum(g * _shifted(x, t_idx, -1), axis=0, keepdims=True)
        dw_ref[2:3, :] = jnp.sum(g * x, axis=0, keepdims=True)
        dw_ref[3:4, :] = jnp.sum(g * _shifted(x, t_idx, 1), axis=0, keepdims=True)
        db_ref[...] = jnp.sum(g, axis=0, keepdims=True)

    nb = width // LANES
    return pl.pallas_call(
        body, name=name, grid=(nb,),
        in_specs=[pl.BlockSpec((n_rows, LANES), lambda j: (0, colblock * nb + j)),
                  pl.BlockSpec((4, LANES), lambda j: (0, j)),
                  pl.BlockSpec((n_rows, LANES), lambda j: (0, j))],
        out_specs=[pl.BlockSpec((n_rows, LANES), lambda j: (0, j)), pl.BlockSpec((4, LANES), lambda j: (0, j)),
                   pl.BlockSpec((1, LANES), lambda j: (0, j))],
        out_shape=[jax.ShapeDtypeStruct((n_rows, width), F32), jax.ShapeDtypeStruct((4, width), F32),
                   jax.ShapeDtypeStruct((1, width), F32)],
        compiler_params=_params(("parallel",)),
    )(src, w, d)


_WHOLE = pl.BlockSpec(memory_space=pltpu.VMEM)
SCAN_UNROLL = 8


def _scan_fwd(name, a, u, reverse):
    n_rows, width = a.shape

    def body(a_ref, u_ref, h_ref):
        def step(i, h):
            t = (n_rows - 1 - i) if reverse else i
            h = a_ref[pl.ds(t, 1), :] * h + u_ref[pl.ds(t, 1), :]
            h_ref[pl.ds(t, 1), :] = h
            return h
        lax.fori_loop(0, n_rows, step, jnp.zeros((1, width), F32), unroll=SCAN_UNROLL)

    return pl.pallas_call(
        body, name=name, in_specs=[_WHOLE, _WHOLE], out_specs=_WHOLE,
        out_shape=jax.ShapeDtypeStruct((n_rows, width), F32),
        compiler_params=pltpu.CompilerParams(vmem_limit_bytes=VMEM_LIMIT),
    )(a, u)


def _scan_bwd(name, a, h, dh, reverse):
    n_rows, width = a.shape

    def body(a_ref, h_ref, dh_ref, du_ref, da_ref):
        def step(i, carry):
            t = i if reverse else (n_rows - 1 - i)
            g = dh_ref[pl.ds(t, 1), :] + carry
            du_ref[pl.ds(t, 1), :] = g
            tp = t + 1 if reverse else t - 1
            valid = (tp >= 0) & (tp < n_rows)
            h_prev = h_ref[pl.ds(jnp.clip(tp, 0, n_rows - 1), 1), :]
            da_ref[pl.ds(t, 1), :] = jnp.where(valid, g * h_prev, 0.0)
            return a_ref[pl.ds(t, 1), :] * g
        lax.fori_loop(0, n_rows, step, jnp.zeros((1, width), F32), unroll=SCAN_UNROLL)

    return pl.pallas_call(
        body, name=name, in_specs=[_WHOLE] * 3, out_specs=[_WHOLE] * 2,
        out_shape=[jax.ShapeDtypeStruct((n_rows, width), F32)] * 2,
        compiler_params=pltpu.CompilerParams(vmem_limit_bytes=VMEM_LIMIT),
    )(a, h, dh)


def _chunk_fn(q, k, v, lf, st, reverse):
    c = q.shape[0]
    row = lax.broadcasted_iota(jnp.int32, (c, c), 0)
    col = lax.broadcasted_iota(jnp.int32, (c, c), 1)
    tri = (col >= row) if reverse else (col <= row)
    cum = jnp.dot(tri.astype(F32), lf, precision=lax.Precision.HIGHEST, preferred_element_type=F32)
    rid = lax.broadcasted_iota(jnp.int32, cum.shape, 0)
    ref_row = (c - 1 - c // 2) if reverse else c // 2
    last_row = 0 if reverse else c - 1
    ref = jnp.sum(jnp.where(rid == ref_row, cum, 0.0), axis=0, keepdims=True)
    last = jnp.sum(jnp.where(rid == last_row, cum, 0.0), axis=0, keepdims=True)
    q_in = q * jnp.exp(cum - ref)
    k_in = k * jnp.exp(ref - cum)
    scores = jnp.where(tri, _dot_nt(q_in, k_in), 0.0)
    o = _dot_nn(scores, v) + _dot_nt(q * jnp.exp(cum), st)
    k_state = k * jnp.exp(last - cum)
    st_new = st * jnp.exp(last) + _dot_tn(v, k_state)
    return o, st_new


def _attn_specs(n_chunks, reverse_order, q, k, v, lf, dk, dv):
    def cidx(n):
        return (n_chunks - 1 - n) if reverse_order else n

    def spec(width, off):
        return pl.BlockSpec((CHUNK, width), lambda h, n: (cidx(n), off + h))

    return cidx, [spec(dk, q[1]), spec(dk, k[1]), spec(dv, v[1]), spec(dk, lf[1])]


def _attn_fwd(name, q, k, v, lf, n_heads, dk, dv, reverse):
    n_rows = q[0].shape[0]
    n_chunks = n_rows // CHUNK
    cidx, in_specs = _attn_specs(n_chunks, reverse, q, k, v, lf, dk, dv)

    def body(q_ref, k_ref, v_ref, lf_ref, o_ref, s_ref, st):
        @pl.when(pl.program_id(1) == 0)
        def _():
            st[...] = jnp.zeros_like(st)

        s_ref[...] = st[...]
        o, st_new = _chunk_fn(q_ref[...], k_ref[...], v_ref[...], lf_ref[...], st[...], reverse)
        o_ref[...] = o
        st[...] = st_new

    return pl.pallas_call(
        body, name=name, grid=(n_heads, n_chunks), in_specs=in_specs,
        out_specs=[pl.BlockSpec((CHUNK, dv), lambda h, n: (cidx(n), h)),
                   pl.BlockSpec((None, None, dv, dk), lambda h, n: (h, cidx(n), 0, 0))],
        out_shape=[jax.ShapeDtypeStruct((n_rows, n_heads * dv), F32),
                   jax.ShapeDtypeStruct((n_heads, n_chunks, dv, dk), F32)],
        scratch_shapes=[pltpu.VMEM((dv, dk), F32)],
        compiler_params=_params(("parallel", "arbitrary")),
    )(q[0], k[0], v[0], lf[0])


def _attn_bwd(name, q, k, v, lf, states, do, n_heads, dk, dv, reverse):
    n_rows = q[0].shape[0]
    n_chunks = n_rows // CHUNK
    cidx, in_specs = _attn_specs(n_chunks, not reverse, q, k, v, lf, dk, dv)
    in_specs += [pl.BlockSpec((None, None, dv, dk), lambda h, n: (h, cidx(n), 0, 0)),
                 pl.BlockSpec((CHUNK, dv), lambda h, n: (cidx(n), h))]

    def body(q_ref, k_ref, v_ref, lf_ref, s_ref, do_ref, dq_ref, dk_ref, dv_ref, dlf_ref, dst):
        @pl.when(pl.program_id(1) == 0)
        def _():
            dst[...] = jnp.zeros_like(dst)

        fn = functools.partial(_chunk_fn, reverse=reverse)
        _, vjp = jax.vjp(fn, q_ref[...], k_ref[...], v_ref[...], lf_ref[...], s_ref[...])
        dq, dkk, dvv, dlf, dst_prev = vjp((do_ref[...], dst[...]))
        dq_ref[...] = dq
        dk_ref[...] = dkk
        dv_ref[...] = dvv
        dlf_ref[...] = dlf
        dst[...] = dst_prev

    def ospec(width):
        return pl.BlockSpec((CHUNK, width), lambda h, n: (cidx(n), h))

    return pl.pallas_call(
        body, name=name, grid=(n_heads, n_chunks), in_specs=in_specs,
        out_specs=[ospec(dk), ospec(dk), ospec(dv), ospec(dk)],
        out_shape=[jax.ShapeDtypeStruct((n_rows, n_heads * dk), F32), jax.ShapeDtypeStruct((n_rows, n_heads * dk), F32),
                   jax.ShapeDtypeStruct((n_rows, n_heads * dv), F32), jax.ShapeDtypeStruct((n_rows, n_heads * dk), F32)],
        scratch_shapes=[pltpu.VMEM((dv, dk), F32)],
        compiler_params=_params(("parallel", "arbitrary")),
    )(q[0], k[0], v[0], lf[0], states, do)


def _row2(v):
    return v.reshape(1, -1)


def _mlp_fwd(tag, h, gain, w1, w2):
    y = _rowcall(f"{tag}_norm", _rmsnorm_fn, [(h, h.shape[1], 0)], [gain], [(h.shape[1], BF16)], tm=512)[0]
    hid, act = _mm(f"{tag}_up", y, w1, epi=_relu2_epi, out_dtypes=(F32, BF16))
    h_out = _mm(f"{tag}_down", act, w2, extras=(h,), epi=_add_epi)
    return h_out, (y, hid, act)


def _mlp_bwd(tag, h, gain, w1, w2, saved, dh_out):
    y, hid, act = saved
    d = h.shape[1]
    dhid = _mm(f"{tag}_dact", dh_out, w2, mode="nt", extras=(hid,), epi=_relu2_bwd_epi, out_dtypes=(BF16,))
    dw2 = _mm(f"{tag}_dw2", act, dh_out, mode="tn")
    dw1 = _mm(f"{tag}_dw1", y, dhid, mode="tn")
    dy = _mm(f"{tag}_dy", dhid, w1, mode="nt")
    dh, dgain = _norm_bwd(f"{tag}_dnorm", h, gain, dy, dh_out)
    return dh, dgain, dw1, dw2


def _norm_bwd(name, h, gain, dy, dres):
    d = h.shape[1]

    def fn(h, dy, dres, gain):
        _, vjp = jax.vjp(lambda a, b: _rmsnorm_fn(a, b)[0], h, gain)
        dh, dgain = vjp(dy)
        return dh + dres, dgain

    dh, dgain = _rowcall(name, fn, [(h, d, 0), (dy, d, 0), (dres, d, 0)], [gain], [(d, F32)], [(1, d)], tm=512)
    return dh, dgain


def _local_step(x, target, w):
    g = {}
    d_model = x.shape[1]
    rg_w = hg_w = d_model // 2

    h_a0 = x
    gain = _row2(w["norm_mix"][0])
    y0 = _rowcall("l0_norm", _rmsnorm_fn, [(h_a0, d_model, 0)], [gain], [(d_model, BF16)], tm=512)[0]
    proj0 = _mm("l0_in", y0, w["ab_w_in"])
    conv_w, conv_b = w["rg_conv_w"], _row2(w["rg_conv_b"])
    xc = _conv_fwd("rg_conv", proj0, 0, conv_w, conv_b)
    gate_pars = [w["rg_wa_bd"], w["rg_wx_bd"], w["rg_b_a"], w["rg_b_x"], w["rg_lambda"]]
    a_f, u_f, a_b, u_b = _rowcall("rg_gates", _rg_gates_fn, [(xc, rg_w, 0)], gate_pars, [(rg_w, F32)] * 4)
    hs_f = _scan_fwd("rg_scan_f", a_f, u_f, False)
    hs_b = _scan_fwd("rg_scan_b", a_b, u_b, True)
    hg_rows = [(proj0, hg_w, 2), (proj0, hg_w, 3), (proj0, hg_w, 4)]
    qh, k_f, lf_f, k_b, lf_b = _rowcall("hg_pre", _hg_pre_fn, hg_rows, [w["hg_lb_logits"]], [(hg_w, F32)] * 5)
    iv = (proj0, 5 * 4)
    o_f, st_f = _attn_fwd("hg_attn_f", (qh, 0), (k_f, 0), iv, (lf_f, 0), 4, 128, 128, False)
    o_b, st_b = _attn_fwd("hg_attn_b", (qh, 0), (k_b, 0), iv, (lf_b, 0), 4, 128, 128, True)
    post0_rows = [(hs_f, rg_w, 0), (hs_b, rg_w, 0), (proj0, rg_w, 1), (o_f, hg_w, 0), (o_b, hg_w, 0), (proj0, hg_w, 6)]
    hg_gain = _row2(w["hg_norm"])
    mix_in0 = _rowcall("l0_post", _post0_fn, post0_rows, [hg_gain], [(d_model, BF16)])[0]
    h_b0 = _mm("l0_out", mix_in0, w["ab_w_out"], extras=(h_a0,), epi=_add_epi)
    h_c0, mlp0 = _mlp_fwd("mlp0", h_b0, _row2(w["norm_mlp"][0]), w["mlp_w1"][0], w["mlp_w2"][0])

    h_a1 = h_c0
    gain1 = _row2(w["norm_mix"][1])
    y1 = _rowcall("l1_norm", _rmsnorm_fn, [(h_a1, d_model, 0)], [gain1], [(d_model, BF16)], tm=512)[0]
    proj1 = _mm("l1_in", y1, w["gla_w_in_pad"], tn=640)
    gla_pars = [w["gla_w_up_pad"], w["gla_b_gate"]]
    gq, glf_f, glf_b = _rowcall("gla_pre", _gla_pre_fn, [(proj1, 512, 0), (proj1, LANES, 24)], gla_pars, [(512, F32)] * 3)
    gk, gv = (proj1, 4), (proj1, 4)
    go_f, gst_f = _attn_fwd("gla_attn_f", (gq, 0), gk, gv, (glf_f, 0), 4, 128, 256, False)
    go_b, gst_b = _attn_fwd("gla_attn_b", (gq, 0), gk, gv, (glf_b, 0), 4, 128, 256, True)
    gla_gain = _row2(w["gla_norm"])
    post1_rows = [(go_f, d_model, 0), (go_b, d_model, 0), (proj1, d_model, 2)]
    mix_in1 = _rowcall("l1_post", _gla_post_fn, post1_rows, [gla_gain], [(d_model, BF16)])[0]
    h_b1 = _mm("l1_out", mix_in1, w["gla_w_out"], extras=(h_a1,), epi=_add_epi)
    h_c1, mlp1 = _mlp_fwd("mlp1", h_b1, _row2(w["norm_mlp"][1]), w["mlp_w1"][1], w["mlp_w2"][1])

    dh, loss, g["norm_final"] = _rowcall(
        "loss_head", _loss_head_fn, [(h_c1, d_model, 0), (target, d_model, 0)], [_row2(w["norm_final"])],
        [(d_model, F32)], [(1, LANES), (1, d_model)], tm=512)

    dh, g_nmlp1, g_w1_1, g_w2_1 = _mlp_bwd("mlp1", h_b1, _row2(w["norm_mlp"][1]), w["mlp_w1"][1], w["mlp_w2"][1], mlp1, dh)
    dmix1 = _mm("l1_dout", dh, w["gla_w_out"], mode="nt")
    g["gla_w_out"] = _mm("l1_dwout", mix_in1, dh, mode="tn")
    post1_bwd = _vjp_of(_gla_post_fn, 3, 1, 1)
    dgo_f, dgo_b, dr, g["gla_norm"] = _rowcall(
        "l1_dpost", post1_bwd, post1_rows + [(dmix1, d_model, 0)], [gla_gain], [(d_model, F32)] * 3, [(1, d_model)])
    dq_f, dk_f, dv_f, dlf_f = _attn_bwd("gla_dattn_f", (gq, 0), gk, gv, (glf_f, 0), gst_f, dgo_f, 4, 128, 256, False)
    dq_b, dk_b, dv_b, dlf_b = _attn_bwd("gla_dattn_b", (gq, 0), gk, gv, (glf_b, 0), gst_b, dgo_b, 4, 128, 256, True)

    def gla_pre_bwd(q, lr, dq1, dq2, dlf1, dlf2, dk1, dk2, dv1, dv2, w_up, b_gate):
        dlr = jnp.zeros_like(lr)
        dws, dbs = [], []
        for d, dlf in enumerate((dlf1, dlf2)):
            z = _raw_nn(lr, w_up[d]) + b_gate[d:d + 1]
            dz = dlf * _sigmoid(-z) * (1.0 / 16.0)
            dlr = dlr + _raw_nt(dz, w_up[d])
            dws.append(_raw_tn(dz, lr))
            dbs.append(jnp.sum(dz, axis=0, keepdims=True))
        return ((dq1 + dq2) * (128.0 ** -0.5), dk1 + dk2, dv1 + dv2, dlr, dws[0], dws[1], dbs[0], dbs[1])

    rows = [(proj1, 512, 0), (proj1, LANES, 24), (dq_f, 512, 0), (dq_b, 512, 0), (dlf_f, 512, 0), (dlf_b, 512, 0),
            (dk_f, 512, 0), (dk_b, 512, 0), (dv_f, d_model, 0), (dv_b, d_model, 0)]
    dq, dk, dv, dlr, dwt_f, dwt_b, db_f, db_b = _rowcall(
        "gla_dpre", gla_pre_bwd, rows, gla_pars, [(512, F32), (512, F32), (d_model, F32), (LANES, F32)],
        [(512, LANES), (512, LANES), (1, 512), (1, 512)])
    g["gla_w_up_pad"] = jnp.stack([dwt_f.T, dwt_b.T])
    g["gla_b_gate"] = jnp.concatenate([db_f, db_b], axis=0)
    dproj1 = jnp.concatenate([dq, dk, dv, dr, dlr], axis=1).astype(BF16)
    g["gla_w_in_pad"] = _mm("l1_dwin", y1, dproj1, mode="tn", tn=640)
    dy1 = _mm("l1_dy", dproj1, w["gla_w_in_pad"], mode="nt", tk=640)
    dh, g_nmix1 = _norm_bwd("l1_dnorm", h_a1, gain1, dy1, dh)

    dh, g_nmlp0, g_w1_0, g_w2_0 = _mlp_bwd("mlp0", h_b0, _row2(w["norm_mlp"][0]), w["mlp_w1"][0], w["mlp_w2"][0], mlp0, dh)
    dmix0 = _mm("l0_dout", dh, w["ab_w_out"], mode="nt")
    g["ab_w_out"] = _mm("l0_dwout", mix_in0, dh, mode="tn")
    post0_bwd = _vjp_of(_post0_fn, 6, 1, 1)
    dhs_f, dhs_b, dga, do_f, do_b, dg, g["hg_norm"] = _rowcall(
        "l0_dpost", post0_bwd, post0_rows + [(dmix0, d_model, 0)], [hg_gain], [(rg_w, F32)] * 6, [(1, hg_w)])
    dqh_f, dk_f, div_f, dlf_f = _attn_bwd("hg_dattn_f", (qh, 0), (k_f, 0), iv, (lf_f, 0), st_f, do_f, 4, 128, 128, False)
    dqh_b, dk_b, div_b, dlf_b = _attn_bwd("hg_dattn_b", (qh, 0), (k_b, 0), iv, (lf_b, 0), st_b, do_b, 4, 128, 128, True)

    def hg_pre_bwd(q, f_f, f_b, dq1, dq2, dk1, dlf1, dk2, dlf2, dv1, dv2, logits):
        _, vjp = jax.vjp(_hg_pre_fn, q, f_f, f_b, logits)
        dq, df_f, df_b, dlogits = vjp((dq1 + dq2, dk1, dlf1, dk2, dlf2))
        return dq, df_f, df_b, dv1 + dv2, dlogits

    rows = hg_rows + [(t, hg_w, 0) for t in (dqh_f, dqh_b, dk_f, dlf_f, dk_b, dlf_b, div_f, div_b)]
    dq, df_f, df_b, div, g["hg_lb_logits"] = _rowcall(
        "hg_dpre", hg_pre_bwd, rows, [w["hg_lb_logits"]], [(hg_w, F32)] * 4, [(2, hg_w)])
    du_f, da_f = _scan_bwd("rg_dscan_f", a_f, hs_f, dhs_f, False)
    du_b, da_b = _scan_bwd("rg_dscan_b", a_b, hs_b, dhs_b, True)
    gates_bwd = _vjp_of(_rg_gates_fn, 1, 4, 5)
    rows = [(xc, rg_w, 0), (da_f, rg_w, 0), (du_f, rg_w, 0), (da_b, rg_w, 0), (du_b, rg_w, 0)]
    dxc, g["rg_wa_bd"], g["rg_wx_bd"], g["rg_b_a"], g["rg_b_x"], g["rg_lambda"] = _rowcall(
        "rg_dgates", gates_bwd, rows, gate_pars, [(rg_w, F32)],
        [(2, rg_w, rg_w), (2, rg_w, rg_w), (2, rg_w), (2, rg_w), (2, rg_w)])
    dxa, g["rg_conv_w"], g["rg_conv_b"] = _conv_bwd("rg_dconv", proj0, 0, conv_w, dxc)
    dproj0 = jnp.concatenate([dxa, dga, dq, df_f, df_b, div, dg], axis=1).astype(BF16)
    g["ab_w_in"] = _mm("l0_dwin", y0, dproj0, mode="tn")
    dy0 = _mm("l0_dy", dproj0, w["ab_w_in"], mode="nt")
    grad_x, g_nmix0 = _norm_bwd("l0_dnorm", h_a0, gain, dy0, dh)

    g["norm_mix"] = jnp.concatenate([g_nmix0, g_nmix1], axis=0)
    g["norm_mlp"] = jnp.concatenate([g_nmlp0, g_nmlp1], axis=0)
    g["mlp_w1"] = jnp.stack([g_w1_0, g_w1_1])
    g["mlp_w2"] = jnp.stack([g_w2_0, g_w2_1])
    return loss, grad_x, g


def _block_diag(w):
    d, g, n, _ = w.shape
    eye = jnp.eye(g, dtype=w.dtype)
    return (w[:, :, :, None, :] * eye[None, :, None, :, None]).reshape(d, g * n, g * n)


def _block_diag_extract(wbd, g):
    d, gn, _ = wbd.shape
    n = gn // g
    blocks = wbd.reshape(d, g, n, g, n)
    return jnp.stack([blocks[:, i, :, i, :] for i in range(g)], axis=1)


def _prepare_weights(full):
    w = {k: full[k] for k in ("norm_mix", "norm_mlp", "norm_final", "hg_lb_logits")}
    for k in ("mlp_w1", "mlp_w2"):
        w[k] = full[k].astype(BF16)
    for k in ("ab_w_in", "ab_w_out", "gla_w_out"):
        w[k] = full[k][0].astype(BF16)
    for k in ("rg_conv_w", "rg_conv_b", "rg_b_a", "rg_b_x", "rg_lambda", "hg_norm", "gla_b_gate", "gla_norm"):
        w[k] = full[k][0]
    w["rg_wa_bd"] = _block_diag(full["rg_w_a"][0])
    w["rg_wx_bd"] = _block_diag(full["rg_w_x"][0])
    gla_in = full["gla_w_in"][0].astype(BF16)
    w["gla_w_in_pad"] = jnp.pad(gla_in, ((0, 0), (0, 3200 - gla_in.shape[1])))
    up = full["gla_w_gate_up"][0]
    rank = up.shape[1]
    pad = jnp.zeros((2, LANES, up.shape[2]), F32)
    w["gla_w_up_pad"] = pad.at[0, 0:rank].set(up[0]).at[1, rank:2 * rank].set(up[1])
    return w


def _finish_grads(g, rank=16, gla_in_width=3104, rg_blocks=8):
    out = {
        "norm_mix": g["norm_mix"], "norm_mlp": g["norm_mlp"], "norm_final": g["norm_final"][0],
        "mlp_w1": g["mlp_w1"], "mlp_w2": g["mlp_w2"],
        "ab_w_in": g["ab_w_in"][None], "ab_w_out": g["ab_w_out"][None],
        "rg_conv_w": g["rg_conv_w"][None], "rg_conv_b": g["rg_conv_b"],
        "rg_w_a": _block_diag_extract(g["rg_wa_bd"], rg_blocks)[None], "rg_b_a": g["rg_b_a"][None],
        "rg_w_x": _block_diag_extract(g["rg_wx_bd"], rg_blocks)[None], "rg_b_x": g["rg_b_x"][None],
        "rg_lambda": g["rg_lambda"][None], "hg_lb_logits": g["hg_lb_logits"], "hg_norm": g["hg_norm"],
        "gla_w_in": g["gla_w_in_pad"][None, :, :gla_in_width], "gla_w_out": g["gla_w_out"][None],
        "gla_w_gate_up": jnp.stack([g["gla_w_up_pad"][0, 0:rank], g["gla_w_up_pad"][1, rank:2 * rank]])[None],
        "gla_b_gate": g["gla_b_gate"][None], "gla_norm": g["gla_norm"],
    }
    return out


BIG = (("mlp_w1", 2), ("mlp_w2", 1), ("ab_w_in", 2), ("ab_w_out", 1), ("gla_w_in", 2), ("gla_w_out", 1))
SMALL_SHARDED = ("rg_conv_w", "rg_b_a", "rg_b_x", "rg_lambda", "gla_w_gate_up", "gla_b_gate", "gla_norm")
SMALL_REPLICATED = ("norm_mix", "norm_mlp", "norm_final", "rg_conv_b", "rg_w_a", "rg_w_x", "hg_lb_logits", "hg_norm")
WEIGHTS = ("norm_mix", "norm_mlp", "norm_final", "mlp_w1", "mlp_w2", "ab_w_in", "ab_w_out", "rg_conv_w", "rg_conv_b",
           "rg_w_a", "rg_b_a", "rg_w_x", "rg_b_x", "rg_lambda", "hg_lb_logits", "hg_norm", "gla_w_in", "gla_w_out",
           "gla_w_gate_up", "gla_b_gate", "gla_norm")
ROW_ALIGN = 16


def _pack(arrays, lead=0):
    head = arrays[0].shape[:lead]
    flat = jnp.concatenate([a.reshape(head + (-1,)) for a in arrays], axis=lead)
    n = flat.shape[-1]
    quantum = LANES * ROW_ALIGN
    padded = -(-n // quantum) * quantum
    if padded != n:
        flat = jnp.pad(flat, [(0, 0)] * lead + [(0, padded - n)])
    return flat.reshape(head + (padded // LANES, LANES))


def _unpack(buf, shapes, lead=0):
    head = buf.shape[:lead]
    flat = buf.reshape(head + (-1,))
    out, off = [], 0
    for s in shapes:
        n = 1
        for v in s:
            n *= v
        out.append(lax.slice_in_dim(flat, off, off + n, axis=lead).reshape(head + tuple(s)))
        off += n
    return out


def _join_chips(gathered, axis):
    t = jnp.moveaxis(gathered, 0, axis)
    return t.reshape(t.shape[:axis] + (t.shape[axis] * t.shape[axis + 1],) + t.shape[axis + 2:])


def _split_chips(full, axis):
    s = full.shape
    t = full.reshape(s[:axis] + (N_CHIPS, s[axis] // N_CHIPS) + s[axis + 1:])
    return jnp.moveaxis(t, axis, 0)


_ANY = pl.BlockSpec(memory_space=pl.ANY)


def _place():
    return lax.axis_index("x"), lax.axis_index("y"), lax.axis_index("c")


def _gather_chips(name, shard):
    def body(in_ref, out_ref, send_sems, recv_sems, local_sem):
        x, y, c = _place()
        me = 2 * x + y
        peers = [(1 - x, y), (x, 1 - y), (1 - x, 1 - y)]

        def copy(j, block):
            px, py = peers[j]
            return pltpu.make_async_remote_copy(
                src_ref=in_ref, dst_ref=out_ref.at[block], send_sem=send_sems.at[j], recv_sem=recv_sems.at[j],
                device_id=(px, py, c), device_id_type=MESH)

        local = pltpu.make_async_copy(in_ref, out_ref.at[me], local_sem)
        local.start()
        sends = [copy(j, me) for j in range(3)]
        for cp in sends:
            cp.start()
        for j, (px, py) in enumerate(peers):
            copy(j, 2 * px + py).wait_recv()
        for cp in sends:
            cp.wait_send()
        local.wait()

    return pl.pallas_call(
        body, name=name, in_specs=[_ANY], out_specs=_ANY,
        out_shape=jax.ShapeDtypeStruct((N_CHIPS,) + shard.shape, shard.dtype),
        scratch_shapes=[pltpu.SemaphoreType.DMA((3,)), pltpu.SemaphoreType.DMA((3,)), pltpu.SemaphoreType.DMA],
    )(shard)


def _pair_exchange(name, g):
    n, _, rh, lanes = g.shape

    def body(g_ref, out_ref, send_sem, recv_sem):
        x, y, c = _place()
        cp = pltpu.make_async_remote_copy(
            src_ref=g_ref.at[:, 1 - c], dst_ref=out_ref, send_sem=send_sem, recv_sem=recv_sem,
            device_id=(x, y, 1 - c), device_id_type=MESH)
        cp.start()
        cp.wait()

    return pl.pallas_call(
        body, name=name, in_specs=[_ANY], out_specs=_ANY,
        out_shape=jax.ShapeDtypeStruct((n, rh, lanes), g.dtype),
        scratch_shapes=[pltpu.SemaphoreType.DMA, pltpu.SemaphoreType.DMA],
    )(g)


def _pair_add(name, g, got, c, tm):
    n, _, rh, lanes = g.shape

    def body(c_ref, g_ref, got_ref, o_ref):
        o_ref[...] = g_ref[...] + got_ref[...]

    grid_spec = pltpu.PrefetchScalarGridSpec(
        num_scalar_prefetch=1, grid=(n, rh // tm),
        in_specs=[pl.BlockSpec((None, None, tm, lanes), lambda s, i, c_ref: (s, c_ref[0], i, 0)),
                  pl.BlockSpec((None, tm, lanes), lambda s, i, c_ref: (s, i, 0))],
        out_specs=pl.BlockSpec((None, tm, lanes), lambda s, i, c_ref: (s, i, 0)))
    return pl.pallas_call(
        body, name=name, grid_spec=grid_spec, out_shape=jax.ShapeDtypeStruct((n, rh, lanes), F32),
        compiler_params=_params(("parallel", "parallel")),
    )(c.reshape(1).astype(jnp.int32), g, got)


def _chip_scatter(name, p):
    def body(p_ref, out_ref, send_sems, recv_sems, local_sem):
        x, y, c = _place()
        me = 2 * x + y
        peers = [(1 - x, y), (x, 1 - y), (1 - x, 1 - y)]

        def copy(j, src_block, dst_block):
            px, py = peers[j]
            return pltpu.make_async_remote_copy(
                src_ref=p_ref.at[src_block], dst_ref=out_ref.at[dst_block], send_sem=send_sems.at[j],
                recv_sem=recv_sems.at[j], device_id=(px, py, c), device_id_type=MESH)

        local = pltpu.make_async_copy(p_ref.at[me], out_ref.at[me], local_sem)
        local.start()
        sends = [copy(j, 2 * px + py, me) for j, (px, py) in enumerate(peers)]
        for cp in sends:
            cp.start()
        for j, (px, py) in enumerate(peers):
            copy(j, me, 2 * px + py).wait_recv()
        for cp in sends:
            cp.wait_send()
        local.wait()

    return pl.pallas_call(
        body, name=name, in_specs=[_ANY], out_specs=_ANY, out_shape=jax.ShapeDtypeStruct(p.shape, p.dtype),
        scratch_shapes=[pltpu.SemaphoreType.DMA((3,)), pltpu.SemaphoreType.DMA((3,)), pltpu.SemaphoreType.DMA],
    )(p)


def _pair_gather(name, r):
    def body(r_ref, out_ref, send_sem, recv_sem, local_sem):
        x, y, c = _place()
        local = pltpu.make_async_copy(r_ref, out_ref.at[c], local_sem)
        local.start()
        send = pltpu.make_async_remote_copy(
            src_ref=r_ref, dst_ref=out_ref.at[c], send_sem=send_sem, recv_sem=recv_sem,
            device_id=(x, y, 1 - c), device_id_type=MESH)
        send.start()
        pltpu.make_async_remote_copy(
            src_ref=r_ref, dst_ref=out_ref.at[1 - c], send_sem=send_sem, recv_sem=recv_sem,
            device_id=(x, y, 1 - c), device_id_type=MESH).wait_recv()
        send.wait_send()
        local.wait()

    return pl.pallas_call(
        body, name=name, in_specs=[_ANY], out_specs=_ANY, out_shape=jax.ShapeDtypeStruct((2,) + r.shape, r.dtype),
        scratch_shapes=[pltpu.SemaphoreType.DMA, pltpu.SemaphoreType.DMA, pltpu.SemaphoreType.DMA],
    )(r)


def _gather_all(name, s):
    def body(in_ref, out_ref, send_sems, recv_sems, local_sem):
        x, y, c = _place()
        me = 4 * x + 2 * y + c
        peers = []
        for mask in range(1, N_DEV):
            fx, fy, fc = (mask >> 2) & 1, (mask >> 1) & 1, mask & 1
            peers.append((jnp.where(fx, 1 - x, x), jnp.where(fy, 1 - y, y), jnp.where(fc, 1 - c, c)))

        def copy(j, block):
            return pltpu.make_async_remote_copy(
                src_ref=in_ref, dst_ref=out_ref.at[block], send_sem=send_sems.at[j], recv_sem=recv_sems.at[j],
                device_id=peers[j], device_id_type=MESH)

        local = pltpu.make_async_copy(in_ref, out_ref.at[me], local_sem)
        local.start()
        sends = [copy(j, me) for j in range(N_DEV - 1)]
        for cp in sends:
            cp.start()
        for j, (px, py, pc) in enumerate(peers):
            copy(j, 4 * px + 2 * py + pc).wait_recv()
        for cp in sends:
            cp.wait_send()
        local.wait()

    return pl.pallas_call(
        body, name=name, in_specs=[_ANY], out_specs=_ANY,
        out_shape=jax.ShapeDtypeStruct((N_DEV,) + s.shape, s.dtype),
        scratch_shapes=[pltpu.SemaphoreType.DMA((N_DEV - 1,)), pltpu.SemaphoreType.DMA((N_DEV - 1,)),
                        pltpu.SemaphoreType.DMA],
    )(s)


def _sum_blocks(name, stacked, tm):
    n, r, lanes = stacked.shape

    def body(in_ref, o_ref):
        acc = in_ref[0]
        for j in range(1, n):
            acc = acc + in_ref[j]
        o_ref[...] = acc

    return pl.pallas_call(
        body, name=name, grid=(r // tm,), in_specs=[pl.BlockSpec((n, tm, lanes), lambda i: (0, i, 0))],
        out_specs=pl.BlockSpec((tm, lanes), lambda i: (i, 0)), out_shape=jax.ShapeDtypeStruct((r, lanes), F32),
        compiler_params=_params(("parallel",)),
    )(stacked)


def _row_tile(rows, pref, align):
    best = None
    for t in range(align, min(rows, pref) + 1, align):
        if rows % t == 0:
            best = t
    assert best is not None, (rows, pref, align)
    return best


def _adam(name, w, g, m, v):
    rows = w.shape[0]
    tm = _row_tile(rows, 4096, 8)
    args = [(t, LANES, 0) for t in (w, g, m, v)]
    return _rowcall(name, _adam_fn, args, [], [(LANES, F32)] * 3, tm=tm)


def kernel(x, norm_mix, norm_mlp, norm_final, mlp_w1, mlp_w2, ab_w_in, ab_w_out, rg_conv_w, rg_conv_b, rg_w_a, rg_b_a, rg_w_x, rg_b_x, rg_lambda, hg_lb_logits, hg_norm, gla_w_in, gla_w_out, gla_w_gate_up, gla_b_gate, gla_norm, loss_target, m_norm_mix, m_norm_mlp, m_norm_final, m_mlp_w1, m_mlp_w2, m_ab_w_in, m_ab_w_out, m_rg_conv_w, m_rg_conv_b, m_rg_w_a, m_rg_b_a, m_rg_w_x, m_rg_b_x, m_rg_lambda, m_hg_lb_logits, m_hg_norm, m_gla_w_in, m_gla_w_out, m_gla_w_gate_up, m_gla_b_gate, m_gla_norm, v_norm_mix, v_norm_mlp, v_norm_final, v_mlp_w1, v_mlp_w2, v_ab_w_in, v_ab_w_out, v_rg_conv_w, v_rg_conv_b, v_rg_w_a, v_rg_b_a, v_rg_w_x, v_rg_b_x, v_rg_lambda, v_hg_lb_logits, v_hg_norm, v_gla_w_in, v_gla_w_out, v_gla_w_gate_up, v_gla_b_gate, v_gla_norm):
    w = dict(norm_mix=norm_mix, norm_mlp=norm_mlp, norm_final=norm_final, mlp_w1=mlp_w1, mlp_w2=mlp_w2, ab_w_in=ab_w_in, ab_w_out=ab_w_out, rg_conv_w=rg_conv_w, rg_conv_b=rg_conv_b, rg_w_a=rg_w_a, rg_b_a=rg_b_a, rg_w_x=rg_w_x, rg_b_x=rg_b_x, rg_lambda=rg_lambda, hg_lb_logits=hg_lb_logits, hg_norm=hg_norm, gla_w_in=gla_w_in, gla_w_out=gla_w_out, gla_w_gate_up=gla_w_gate_up, gla_b_gate=gla_b_gate, gla_norm=gla_norm)
    m = dict(norm_mix=m_norm_mix, norm_mlp=m_norm_mlp, norm_final=m_norm_final, mlp_w1=m_mlp_w1, mlp_w2=m_mlp_w2, ab_w_in=m_ab_w_in, ab_w_out=m_ab_w_out, rg_conv_w=m_rg_conv_w, rg_conv_b=m_rg_conv_b, rg_w_a=m_rg_w_a, rg_b_a=m_rg_b_a, rg_w_x=m_rg_w_x, rg_b_x=m_rg_b_x, rg_lambda=m_rg_lambda, hg_lb_logits=m_hg_lb_logits, hg_norm=m_hg_norm, gla_w_in=m_gla_w_in, gla_w_out=m_gla_w_out, gla_w_gate_up=m_gla_w_gate_up, gla_b_gate=m_gla_b_gate, gla_norm=m_gla_norm)
    v = dict(norm_mix=v_norm_mix, norm_mlp=v_norm_mlp, norm_final=v_norm_final, mlp_w1=v_mlp_w1, mlp_w2=v_mlp_w2, ab_w_in=v_ab_w_in, ab_w_out=v_ab_w_out, rg_conv_w=v_rg_conv_w, rg_conv_b=v_rg_conv_b, rg_w_a=v_rg_w_a, rg_b_a=v_rg_b_a, rg_w_x=v_rg_w_x, rg_b_x=v_rg_b_x, rg_lambda=v_rg_lambda, hg_lb_logits=v_hg_lb_logits, hg_norm=v_hg_norm, gla_w_in=v_gla_w_in, gla_w_out=v_gla_w_out, gla_w_gate_up=v_gla_w_gate_up, gla_b_gate=v_gla_b_gate, gla_norm=v_gla_norm)
    chip = 2 * lax.axis_index("x") + lax.axis_index("y")
    core = lax.axis_index("c")
    big_names = [n for n, _ in BIG]
    big_shapes = [w[n].shape for n in big_names]
    sharded_shapes = [w[n].shape for n in SMALL_SHARDED]

    w_big = _pack([w[n] for n in big_names])
    cast_tm = _row_tile(w_big.shape[0], 4096, ROW_ALIGN)
    w_big16 = _rowcall("cast_weights", lambda t: (t,), [(w_big, LANES, 0)], [], [(LANES, BF16)], tm=cast_tm)[0]
    big_all = _unpack(_gather_chips("gather_weights", w_big16), big_shapes, lead=1)
    small_all = _unpack(_gather_chips("gather_vectors", _pack([w[n] for n in SMALL_SHARDED])), sharded_shapes, lead=1)
    full = {n: w[n] for n in SMALL_REPLICATED}
    for (n, axis), t in zip(BIG, big_all):
        full[n] = _join_chips(t, axis)
    for n, t in zip(SMALL_SHARDED, small_all):
        full[n] = _join_chips(t, t.ndim - 2)

    loss_part, grad_x, g_kernel = _local_step(x[0], loss_target[0], _prepare_weights(full))
    g_full = _finish_grads(g_kernel)
    loss = lax.psum(loss_part[0, 0], ("x", "y", "c"))

    g_big = _pack([_split_chips(g_full[n], axis) for n, axis in BIG], lead=1)
    rows = g_big.shape[1]
    half = rows // 2
    add_tm = _row_tile(half, 4096, 8)
    g_halves = g_big.reshape(N_CHIPS, 2, half, LANES)
    from_sibling = _pair_exchange("reduce_pair", g_halves)
    chip_part = _pair_add("reduce_pair_add", g_halves, from_sibling, core, add_tm)
    from_chips = _chip_scatter("reduce_chips", chip_part)
    mine = _sum_blocks("reduce_chips_add", from_chips, add_tm)
    g_big_red = _pair_gather("reduce_share", mine).reshape(rows, LANES)

    small_names = SMALL_REPLICATED + SMALL_SHARDED
    g_small = _pack([g_full[n] for n in small_names])
    g_small_all = _gather_all("reduce_small", g_small)
    g_small_red = _sum_blocks("reduce_small_add", g_small_all, g_small.shape[0])
    g_small_full = dict(zip(small_names, _unpack(g_small_red, [g_full[n].shape for n in small_names])))
    grads = {n: g_small_full[n] for n in SMALL_REPLICATED}
    for n in SMALL_SHARDED:
        width = w[n].shape[-1]
        grads[n] = lax.dynamic_slice_in_dim(g_small_full[n], chip * width, width, axis=g_small_full[n].ndim - 1)
    for n, t in zip(big_names, _unpack(g_big_red, big_shapes)):
        grads[n] = t

    d_big, m_big, v_big = _adam("adam_big", w_big, g_big_red, _pack([m[n] for n in big_names]), _pack([v[n] for n in big_names]))
    delta, new_m, new_v = {}, {}, {}
    for dst, buf in ((delta, d_big), (new_m, m_big), (new_v, v_big)):
        dst.update(zip(big_names, _unpack(buf, big_shapes)))
    small_shapes = [w[n].shape for n in small_names]
    packs = [_pack([src[n] for n in small_names]) for src in (w, grads, m, v)]
    d_small, m_small, v_small = _adam("adam_small", *packs)
    for dst, buf in ((delta, d_small), (new_m, m_small), (new_v, v_small)):
        dst.update(zip(small_names, _unpack(buf, small_shapes)))

    return (loss, grad_x[None], *[grads[n] for n in WEIGHTS], *[delta[n] for n in WEIGHTS],
            *[new_m[n] for n in WEIGHTS], *[new_v[n] for n in WEIGHTS])
```

```python
import functools

import jax
import jax.numpy as jnp
from jax import lax
from jax.experimental import pallas as pl
from jax.experimental.pallas import tpu as pltpu

F32 = jnp.float32
BF16 = jnp.bfloat16
MESH = pl.DeviceIdType.MESH

LANES = 128
CHUNK = 64
EPS = 1e-6
RG_C = 8.0
N_CHIPS = 4
N_DEV = 8
VMEM_LIMIT = 56 * 1024 * 1024

ADAM_LR = 0.001
ADAM_B1 = 0.9
ADAM_B2 = 0.999
ADAM_EPS = 1e-08
ADAM_WD = 0.01
ADAM_STEP = 10


def _raw_dot(a, b, ca, cb):
    return lax.dot_general(a.astype(BF16), b.astype(BF16), (((ca,), (cb,)), ((), ())),
                           preferred_element_type=F32)


def _raw_nn(a, b):
    return _raw_dot(a, b, 1, 0)


def _raw_nt(a, b):
    return _raw_dot(a, b, 1, 1)


def _raw_tn(a, b):
    return _raw_dot(a, b, 0, 0)


@jax.custom_vjp
def _dot_nn(a, b):
    return _raw_nn(a, b)


def _dot_nn_fwd(a, b):
    return _raw_nn(a, b), (a, b)


def _dot_nn_bwd(res, g):
    a, b = res
    return _raw_nt(g, b), _raw_tn(a, g)


_dot_nn.defvjp(_dot_nn_fwd, _dot_nn_bwd)


@jax.custom_vjp
def _dot_nt(a, b):
    return _raw_nt(a, b)


def _dot_nt_fwd(a, b):
    return _raw_nt(a, b), (a, b)


def _dot_nt_bwd(res, g):
    a, b = res
    return _raw_nn(g, b), _raw_tn(g, a)


_dot_nt.defvjp(_dot_nt_fwd, _dot_nt_bwd)


@jax.custom_vjp
def _dot_tn(a, b):
    return _raw_tn(a, b)


def _dot_tn_fwd(a, b):
    return _raw_tn(a, b), (a, b)


def _dot_tn_bwd(res, g):
    a, b = res
    return _raw_nt(b, g), _raw_nn(a, g)


_dot_tn.defvjp(_dot_tn_fwd, _dot_tn_bwd)


def _tile(n, pref):
    if n <= pref:
        return n
    t = (pref // LANES) * LANES
    while t > LANES and n % t:
        t -= LANES
    assert n % t == 0, (n, pref)
    return t


def _params(sem):
    return pltpu.CompilerParams(dimension_semantics=sem, vmem_limit_bytes=VMEM_LIMIT)


def _rowcall(name, fn, rows, pars, row_outs, par_outs=(), tm=256):
    n_rows = rows[0][0].shape[0]
    tm = min(tm, n_rows)
    assert n_rows % tm == 0
    n_r, n_p, n_ro = len(rows), len(pars), len(row_outs)

    def body(*refs):
        vals = [r[...].astype(F32) for r in refs[:n_r + n_p]]
        outs = fn(*vals)
        o_refs = refs[n_r + n_p:n_r + n_p + n_ro]
        po_refs = refs[n_r + n_p + n_ro:]
        for o_ref, val in zip(o_refs, outs[:n_ro]):
            o_ref[...] = val.astype(o_ref.dtype)
        first = pl.program_id(0) == 0
        for po_ref, val in zip(po_refs, outs[n_ro:]):
            @pl.when(first)
            def _():
                po_ref[...] = val

            @pl.when(jnp.logical_not(first))
            def _():
                po_ref[...] += val

    def const_map(nd):
        return lambda i: (0,) * nd

    in_specs = [pl.BlockSpec((tm, w), functools.partial(lambda i, cb: (i, cb), cb=cb)) for _, w, cb in rows]
    in_specs += [pl.BlockSpec(p.shape, const_map(p.ndim)) for p in pars]
    out_specs = [pl.BlockSpec((tm, w), lambda i: (i, 0)) for w, _ in row_outs]
    out_specs += [pl.BlockSpec(tuple(s), const_map(len(s))) for s in par_outs]
    out_shape = [jax.ShapeDtypeStruct((n_rows, w), dt) for w, dt in row_outs]
    out_shape += [jax.ShapeDtypeStruct(tuple(s), F32) for s in par_outs]
    return pl.pallas_call(
        body, name=name, grid=(n_rows // tm,), in_specs=in_specs, out_specs=out_specs, out_shape=out_shape,
        compiler_params=_params(("arbitrary",) if par_outs else ("parallel",)),
    )(*[r[0] for r in rows], *pars)


def _vjp_of(fn, n_prim, n_out, n_par, n_pass=0):
    def bwd(*args):
        prim = args[:n_prim]
        cts = args[n_prim:n_prim + n_out]
        passes = args[n_prim + n_out:n_prim + n_out + 2 * n_pass]
        pars = args[n_prim + n_out + 2 * n_pass:]
        _, vjp = jax.vjp(fn, *prim, *pars)
        grads = vjp(tuple(cts))
        sums = tuple(passes[2 * i] + passes[2 * i + 1] for i in range(n_pass))
        return tuple(grads[:n_prim]) + sums + tuple(grads[n_prim:])
    return bwd


def _mm(name, a, b, mode="nn", extras=(), epi=None, out_dtypes=(F32,), a_pro=None, tm=1024, tn=1024, tk=1024):
    if mode == "nn":
        (m, k), n = a.shape, b.shape[1]
    elif mode == "nt":
        (m, k), n = a.shape, b.shape[0]
    else:
        (k, m), n = a.shape, b.shape[1]
    tm, tn, tk = _tile(m, tm), _tile(n, tn), _tile(k, tk)
    nk = k // tk
    raw = {"nn": _raw_nn, "nt": _raw_nt, "tn": _raw_tn}[mode]
    n_e, n_o = len(extras), len(out_dtypes)
    if epi is None:
        epi = lambda acc: (acc,)

    def body(a_ref, b_ref, *rest):
        e_refs, o_refs, acc = rest[:n_e], rest[n_e:n_e + n_o], rest[-1]
        kk = pl.program_id(2)

        @pl.when(kk == 0)
        def _():
            acc[...] = jnp.zeros_like(acc)

        a_tile = a_ref[...] if a_pro is None else a_pro(a_ref[...].astype(F32))
        acc[...] += raw(a_tile, b_ref[...])

        @pl.when(kk == nk - 1)
        def _():
            res = epi(acc[...], *[e[...].astype(F32) for e in e_refs])
            for o_ref, r in zip(o_refs, res):
                o_ref[...] = r.astype(o_ref.dtype)

    a_spec = pl.BlockSpec((tk, tm), lambda i, j, kk: (kk, i)) if mode == "tn" else pl.BlockSpec((tm, tk), lambda i, j, kk: (i, kk))
    b_spec = pl.BlockSpec((tn, tk), lambda i, j, kk: (j, kk)) if mode == "nt" else pl.BlockSpec((tk, tn), lambda i, j, kk: (kk, j))
    mn_spec = pl.BlockSpec((tm, tn), lambda i, j, kk: (i, j))
    outs = pl.pallas_call(
        body, name=name, grid=(m // tm, n // tn, nk),
        in_specs=[a_spec, b_spec] + [mn_spec] * n_e, out_specs=[mn_spec] * n_o,
        out_shape=[jax.ShapeDtypeStruct((m, n), dt) for dt in out_dtypes],
        scratch_shapes=[pltpu.VMEM((tm, tn), F32)],
        compiler_params=_params(("parallel", "parallel", "arbitrary")),
    )(a, b, *extras)
    return outs[0] if n_o == 1 else outs


def _sigmoid(x):
    return jax.nn.sigmoid(x)


def _silu(x):
    return x * _sigmoid(x)


def _softplus(x):
    return jnp.maximum(x, 0.0) + jnp.log1p(jnp.exp(-jnp.abs(x)))


def _rmsnorm_fn(x, gain):
    return (x * lax.rsqrt(jnp.mean(x * x, axis=-1, keepdims=True) + EPS) * gain,)


def _head_norm(o, gain, n_heads):
    w = o.shape[-1] // n_heads
    parts = []
    for h in range(n_heads):
        oh = o[:, h * w:(h + 1) * w]
        parts.append(oh * lax.rsqrt(jnp.mean(oh * oh, axis=-1, keepdims=True) + EPS))
    return jnp.concatenate(parts, axis=-1) * gain


@jax.custom_jvp
def _neg_expm1(x):
    u = jnp.exp(x)
    is_one = u == 1.0
    return jnp.where(is_one, -x, (1.0 - u) * x / jnp.log(jnp.where(is_one, 2.0, u)))


@_neg_expm1.defjvp
def _neg_expm1_jvp(primals, tangents):
    (x,), (t,) = primals, tangents
    return _neg_expm1(x), -jnp.exp(x) * t


def _rg_gates_fn(xc, wa, wx, ba, bx, lam):
    outs = []
    for d in range(2):
        r = _sigmoid(_dot_nn(xc, wa[d]) + ba[d:d + 1])
        i = _sigmoid(_dot_nn(xc, wx[d]) + bx[d:d + 1])
        log_a = -RG_C * r * _softplus(-lam[d:d + 1])
        outs.append(jnp.exp(log_a))
        outs.append(jnp.sqrt(_neg_expm1(2.0 * log_a)) * (i * xc))
    return tuple(outs)


def _hg_pre_fn(q, f_f, f_b, logits):
    mx = jnp.maximum(logits[0:1], logits[1:2])
    e0 = jnp.exp(logits[0:1] - mx)
    e1 = jnp.exp(logits[1:2] - mx)
    lb = e0 / (e0 + e1)
    outs = [_silu(q)]
    for f in (f_f, f_b):
        outs.append((1.0 - lb) * _sigmoid(-f))
        outs.append(jnp.log(lb + (1.0 - lb) * _sigmoid(f)))
    return tuple(outs)


def _post0_fn(hs, ga, o, g, gain):
    ya = hs * jax.nn.gelu(ga, approximate=True)
    yb = _head_norm(o, gain, 4) * _silu(g)
    return (jnp.concatenate([ya, yb], axis=-1),)


def _post0_fwd_fn(h_f, h_b, ga, o_f, o_b, g, gain):
    return _post0_fn(h_f + h_b, ga, o_f + o_b, g, gain)


def _post0_bwd_fn(h_f, h_b, ga, o_f, o_b, g, dmix, gain):
    _, vjp = jax.vjp(_post0_fn, h_f + h_b, ga, o_f + o_b, g, gain)
    return vjp((dmix,))


def _gla_pre_fn(q, lr, w_up, b_gate):
    outs = [q * (128.0 ** -0.5)]
    for d in range(2):
        z = _dot_nn(lr, w_up[d]) + b_gate[d:d + 1]
        outs.append(-_softplus(-z) * (1.0 / 16.0))
    return tuple(outs)


def _gla_post_fn(o, r, gain):
    return (_head_norm(o, gain, 4) * _silu(r),)


def _gla_post_fwd_fn(o_f, o_b, r, gain):
    return _gla_post_fn(o_f + o_b, r, gain)


def _gla_post_bwd_fn(o_f, o_b, r, dmix, gain):
    _, vjp = jax.vjp(_gla_post_fn, o_f + o_b, r, gain)
    return vjp((dmix,))


def _relu2_bwd_epi(acc, hid):
    return (acc * 2.0 * jnp.maximum(hid, 0.0),)


def _relu2(x):
    r = jnp.maximum(x, 0.0)
    return r * r


def _add_epi(acc, res):
    return (acc + res,)


def _loss_head_fn(h, target, gain):
    def f(h, gain):
        y = _rmsnorm_fn(h, gain)[0]
        err = y - target
        return 0.5 * jnp.sum(jnp.mean(err * err, axis=-1, keepdims=True))
    loss, (dh, dgain) = jax.value_and_grad(f, argnums=(0, 1))(h, gain)
    return dh, jnp.full((1, LANES), loss, F32), dgain


def _adam_fn(w, g, m, v):
    m2 = ADAM_B1 * m + (1.0 - ADAM_B1) * g
    v2 = ADAM_B2 * v + (1.0 - ADAM_B2) * (g * g)
    m_hat = m2 / (1.0 - ADAM_B1 ** ADAM_STEP)
    v_hat = v2 / (1.0 - ADAM_B2 ** ADAM_STEP)
    delta = -ADAM_LR * (m_hat / (jnp.sqrt(v_hat) + ADAM_EPS) + ADAM_WD * w)
    return delta, m2, v2


def _shifted(x, t_idx, off):
    n = x.shape[0]
    rolled = pltpu.roll(x, (-off) % n, 0)
    valid = (t_idx + off >= 0) & (t_idx + off < n)
    return jnp.where(valid, rolled, 0.0)


def _conv_fwd(name, src, colblock, w, b):
    n_rows, width = src.shape[0], w.shape[1]

    def body(x_ref, w_ref, b_ref, o_ref):
        x = x_ref[...]
        t_idx = lax.broadcasted_iota(jnp.int32, x.shape, 0)
        acc = b_ref[...] + w_ref[2:3, :] * x
        acc += w_ref[0:1, :] * _shifted(x, t_idx, -2)
        acc += w_ref[1:2, :] * _shifted(x, t_idx, -1)
        acc += w_ref[3:4, :] * _shifted(x, t_idx, 1)
        o_ref[...] = acc

    nb = width // LANES
    return pl.pallas_call(
        body, name=name, grid=(nb,),
        in_specs=[pl.BlockSpec((n_rows, LANES), lambda j: (0, colblock * nb + j)),
                  pl.BlockSpec((4, LANES), lambda j: (0, j)), pl.BlockSpec((1, LANES), lambda j: (0, j))],
        out_specs=pl.BlockSpec((n_rows, LANES), lambda j: (0, j)),
        out_shape=jax.ShapeDtypeStruct((n_rows, width), F32),
        compiler_params=_params(("parallel",)),
    )(src, w, b)


def _conv_bwd(name, src, colblock, w, d):
    n_rows, width = src.shape[0], w.shape[1]

    def body(x_ref, w_ref, d_ref, dx_ref, dw_ref, db_ref):
        x = x_ref[...]
        g = d_ref[...]
        t_idx = lax.broadcasted_iota(jnp.int32, x.shape, 0)
        dx = w_ref[2:3, :] * g
        dx += w_ref[0:1, :] * _shifted(g, t_idx, 2)
        dx += w_ref[1:2, :] * _shifted(g, t_idx, 1)
        dx += w_ref[3:4, :] * _shifted(g, t_idx, -1)
        dx_ref[...] = dx.astype(dx_ref.dtype)
        dw_ref[0:1, :] = jnp.sum(g * _shifted(x, t_idx, -2), axis=0, keepdims=True)
        dw_ref[1:2, :] = jnp.sum(g * _shifted(x, t_idx, -1), axis=0, keepdims=True)
        dw_ref[2:3, :] = jnp.sum(g * x, axis=0, keepdims=True)
        dw_ref[3:4, :] = jnp.sum(g * _shifted(x, t_idx, 1), axis=0, keepdims=True)
        db_ref[...] = jnp.sum(g, axis=0, keepdims=True)

    nb = width // LANES
    return pl.pallas_call(
        body, name=name, grid=(nb,),
        in_specs=[pl.BlockSpec((n_rows, LANES), lambda j: (0, colblock * nb + j)),
                  pl.BlockSpec((4, LANES), lambda j: (0, j)),
                  pl.BlockSpec((n_rows, LANES), lambda j: (0, j))],
        out_specs=[pl.BlockSpec((n_rows, LANES), lambda j: (0, j)), pl.BlockSpec((4, LANES), lambda j: (0, j)),
                   pl.BlockSpec((1, LANES), lambda j: (0, j))],
        out_shape=[jax.ShapeDtypeStruct((n_rows, width), BF16), jax.ShapeDtypeStruct((4, width), F32),
                   jax.ShapeDtypeStruct((1, width), F32)],
        compiler_params=_params(("parallel",)),
    )(src, w, d)


_WHOLE = pl.BlockSpec(memory_space=pltpu.VMEM)
SCAN_UNROLL = 8


def _scan_fwd(name, a, u, reverse):
    n_rows, width = a.shape

    def body(a_ref, u_ref, h_ref):
        def step(i, h):
            t = (n_rows - 1 - i) if reverse else i
            h = a_ref[pl.ds(t, 1), :] * h + u_ref[pl.ds(t, 1), :]
            h_ref[pl.ds(t, 1), :] = h
            return h
        lax.fori_loop(0, n_rows, step, jnp.zeros((1, width), F32), unroll=SCAN_UNROLL)

    return pl.pallas_call(
        body, name=name, in_specs=[_WHOLE, _WHOLE], out_specs=_WHOLE,
        out_shape=jax.ShapeDtypeStruct((n_rows, width), F32),
        compiler_params=pltpu.CompilerParams(vmem_limit_bytes=VMEM_LIMIT),
    )(a, u)


def _scan_bwd(name, a, h, dh, reverse):
    n_rows, width = a.shape

    def body(a_ref, h_ref, dh_ref, du_ref, da_ref):
        def step(i, carry):
            t = i if reverse else (n_rows - 1 - i)
            g = dh_ref[pl.ds(t, 1), :] + carry
            du_ref[pl.ds(t, 1), :] = g
            tp = t + 1 if reverse else t - 1
            valid = (tp >= 0) & (tp < n_rows)
            h_prev = h_ref[pl.ds(jnp.clip(tp, 0, n_rows - 1), 1), :]
            da_ref[pl.ds(t, 1), :] = jnp.where(valid, g * h_prev, 0.0)
            return a_ref[pl.ds(t, 1), :] * g
        lax.fori_loop(0, n_rows, step, jnp.zeros((1, width), F32), unroll=SCAN_UNROLL)

    return pl.pallas_call(
        body, name=name, in_specs=[_WHOLE] * 3, out_specs=[_WHOLE] * 2,
        out_shape=[jax.ShapeDtypeStruct((n_rows, width), F32)] * 2,
        compiler_params=pltpu.CompilerParams(vmem_limit_bytes=VMEM_LIMIT),
    )(a, h, dh)


def _chunk_fn(q, k, v, lf, st, reverse):
    c = q.shape[0]
    row = lax.broadcasted_iota(jnp.int32, (c, c), 0)
    col = lax.broadcasted_iota(jnp.int32, (c, c), 1)
    tri = (col >= row) if reverse else (col <= row)
    cum = jnp.dot(tri.astype(F32), lf, precision=lax.Precision.HIGHEST, preferred_element_type=F32)
    rid = lax.broadcasted_iota(jnp.int32, cum.shape, 0)
    ref_row = (c - 1 - c // 2) if reverse else c // 2
    last_row = 0 if reverse else c - 1
    ref = jnp.sum(jnp.where(rid == ref_row, cum, 0.0), axis=0, keepdims=True)
    last = jnp.sum(jnp.where(rid == last_row, cum, 0.0), axis=0, keepdims=True)
    q_in = q * jnp.exp(cum - ref)
    k_in = k * jnp.exp(ref - cum)
    scores = jnp.where(tri, _dot_nt(q_in, k_in), 0.0)
    o = _dot_nn(scores, v) + _dot_nt(q * jnp.exp(cum), st)
    k_state = k * jnp.exp(last - cum)
    st_new = st * jnp.exp(last) + _dot_tn(v, k_state)
    return o, st_new


def _attn_fwd(name, q, k_f, k_b, v, lf_f, lf_b, n_heads, dk, dv):
    n_rows = q[0].shape[0]
    n_chunks = n_rows // CHUNK
    wk, wv = n_heads * dk, n_heads * dv

    def spec(width, off, rev):
        return pl.BlockSpec((CHUNK, width), lambda n: ((n_chunks - 1 - n) if rev else n, off))

    def sspec(rev):
        return pl.BlockSpec((None, n_heads, dv, dk), lambda n: ((n_chunks - 1 - n) if rev else n, 0, 0, 0))

    def body(qf, kf, vf, lff, qb, kb, vb, lfb, of_ref, ob_ref, sf_ref, sb_ref, st):
        @pl.when(pl.program_id(0) == 0)
        def _():
            st[...] = jnp.zeros_like(st)

        for d, (q_r, k_r, v_r, lf_r, o_r, s_r) in enumerate(((qf, kf, vf, lff, of_ref, sf_ref), (qb, kb, vb, lfb, ob_ref, sb_ref))):
            for h in range(n_heads):
                ck, cv = slice(h * dk, (h + 1) * dk), slice(h * dv, (h + 1) * dv)
                s_prev = st[d, h]
                s_r[h] = s_prev
                o, s_new = _chunk_fn(q_r[:, ck], k_r[:, ck], v_r[:, cv], lf_r[:, ck], s_prev, d == 1)
                o_r[:, cv] = o
                st[d, h] = s_new

    in_specs = [spec(wk, q[1], False), spec(wk, k_f[1], False), spec(wv, v[1], False), spec(wk, lf_f[1], False),
                spec(wk, q[1], True), spec(wk, k_b[1], True), spec(wv, v[1], True), spec(wk, lf_b[1], True)]
    return pl.pallas_call(
        body, name=name, grid=(n_chunks,), in_specs=in_specs,
        out_specs=[spec(wv, 0, False), spec(wv, 0, True), sspec(False), sspec(True)],
        out_shape=[jax.ShapeDtypeStruct((n_rows, wv), F32)] * 2
        + [jax.ShapeDtypeStruct((n_chunks, n_heads, dv, dk), F32)] * 2,
        scratch_shapes=[pltpu.VMEM((2, n_heads, dv, dk), F32)],
        compiler_params=_params(("arbitrary",)),
    )(q[0], k_f[0], v[0], lf_f[0], q[0], k_b[0], v[0], lf_b[0])


def _attn_bwd(name, q, k_f, k_b, v, lf_f, lf_b, st_f, st_b, do, n_heads, dk, dv, out_dtype=F32):
    n_rows = q[0].shape[0]
    n_chunks = n_rows // CHUNK
    wk, wv = n_heads * dk, n_heads * dv

    def spec(width, off, rev):
        return pl.BlockSpec((CHUNK, width), lambda n: (n if rev else (n_chunks - 1 - n), off))

    def sspec(rev):
        return pl.BlockSpec((None, n_heads, dv, dk), lambda n: (n if rev else (n_chunks - 1 - n), 0, 0, 0))

    def body(qf, kf, vf, lff, sf, dof, qb, kb, vb, lfb, sb, dob,
             dqf, dkf, dvf, dlff, dqb, dkb, dvb, dlfb, dst):
        @pl.when(pl.program_id(0) == 0)
        def _():
            dst[...] = jnp.zeros_like(dst)

        groups = ((qf, kf, vf, lff, sf, dof, dqf, dkf, dvf, dlff), (qb, kb, vb, lfb, sb, dob, dqb, dkb, dvb, dlfb))
        for d, (q_r, k_r, v_r, lf_r, s_r, do_r, dq_r, dk_r, dv_r, dlf_r) in enumerate(groups):
            fn = functools.partial(_chunk_fn, reverse=(d == 1))
            for h in range(n_heads):
                ck, cv = slice(h * dk, (h + 1) * dk), slice(h * dv, (h + 1) * dv)
                _, vjp = jax.vjp(fn, q_r[:, ck], k_r[:, ck], v_r[:, cv], lf_r[:, ck], s_r[h])
                dq, dkk, dvv, dlf, dst_prev = vjp((do_r[:, cv], dst[d, h]))
                dq_r[:, ck] = dq.astype(dq_r.dtype)
                dk_r[:, ck] = dkk.astype(dk_r.dtype)
                dv_r[:, cv] = dvv.astype(dv_r.dtype)
                dlf_r[:, ck] = dlf.astype(dlf_r.dtype)
                dst[d, h] = dst_prev

    def dir_specs(kk, lf, rev):
        return [spec(wk, q[1], rev), spec(wk, kk[1], rev), spec(wv, v[1], rev), spec(wk, lf[1], rev), sspec(rev),
                spec(wv, 0, rev)]

    def dir_out_specs(rev):
        return [spec(wk, 0, rev), spec(wk, 0, rev), spec(wv, 0, rev), spec(wk, 0, rev)]

    shapes = [jax.ShapeDtypeStruct((n_rows, wk), out_dtype), jax.ShapeDtypeStruct((n_rows, wk), out_dtype),
              jax.ShapeDtypeStruct((n_rows, wv), out_dtype), jax.ShapeDtypeStruct((n_rows, wk), F32)]
    outs = pl.pallas_call(
        body, name=name, grid=(n_chunks,), in_specs=dir_specs(k_f, lf_f, False) + dir_specs(k_b, lf_b, True),
        out_specs=dir_out_specs(False) + dir_out_specs(True), out_shape=shapes + shapes,
        scratch_shapes=[pltpu.VMEM((2, n_heads, dv, dk), F32)],
        compiler_params=_params(("arbitrary",)),
    )(q[0], k_f[0], v[0], lf_f[0], st_f, do, q[0], k_b[0], v[0], lf_b[0], st_b, do)
    return outs[:4], outs[4:]


def _row2(v):
    return v.reshape(1, -1)


def _mlp_fwd(tag, h, gain, w1, w2):
    y = _rowcall(f"{tag}_norm", _rmsnorm_fn, [(h, h.shape[1], 0)], [gain], [(h.shape[1], BF16)], tm=512)[0]
    hid = _mm(f"{tag}_up", y, w1, out_dtypes=(BF16,))
    h_out = _mm(f"{tag}_down", hid, w2, a_pro=_relu2, extras=(h,), epi=_add_epi)
    return h_out, (y, hid)


def _mlp_bwd(tag, h, gain, w1, w2, saved, dh_out):
    y, hid = saved
    dhid = _mm(f"{tag}_dact", dh_out, w2, mode="nt", extras=(hid,), epi=_relu2_bwd_epi, out_dtypes=(BF16,))
    dw2 = _mm(f"{tag}_dw2", hid, dh_out, mode="tn", a_pro=_relu2)
    dw1 = _mm(f"{tag}_dw1", y, dhid, mode="tn")
    dy = _mm(f"{tag}_dy", dhid, w1, mode="nt")
    dh, dgain = _norm_bwd(f"{tag}_dnorm", h, gain, dy, dh_out)
    return dh, dgain, dw1, dw2


def _norm_bwd(name, h, gain, dy, dres):
    d = h.shape[1]

    def fn(h, dy, dres, gain):
        _, vjp = jax.vjp(lambda a, b: _rmsnorm_fn(a, b)[0], h, gain)
        dh, dgain = vjp(dy)
        return dh + dres, dgain

    dh, dgain = _rowcall(name, fn, [(h, d, 0), (dy, d, 0), (dres, d, 0)], [gain], [(d, F32)], [(1, d)], tm=512)
    return dh, dgain


def _local_step(x, target, w):
    g = {}
    d_model = x.shape[1]
    rg_w = hg_w = d_model // 2

    h_a0 = x
    gain = _row2(w["norm_mix"][0])
    y0 = _rowcall("l0_norm", _rmsnorm_fn, [(h_a0, d_model, 0)], [gain], [(d_model, BF16)], tm=512)[0]
    proj0 = _mm("l0_in", y0, w["ab_w_in"])
    conv_w, conv_b = w["rg_conv_w"], _row2(w["rg_conv_b"])
    xc = _conv_fwd("rg_conv", proj0, 0, conv_w, conv_b)
    gate_pars = [w["rg_wa_bd"], w["rg_wx_bd"], w["rg_b_a"], w["rg_b_x"], w["rg_lambda"]]
    a_f, u_f, a_b, u_b = _rowcall("rg_gates", _rg_gates_fn, [(xc, rg_w, 0)], gate_pars, [(rg_w, F32)] * 4)
    hs_f = _scan_fwd("rg_scan_f", a_f, u_f, False)
    hs_b = _scan_fwd("rg_scan_b", a_b, u_b, True)
    hg_rows = [(proj0, hg_w, 2), (proj0, hg_w, 3), (proj0, hg_w, 4)]
    qh, k_f, lf_f, k_b, lf_b = _rowcall("hg_pre", _hg_pre_fn, hg_rows, [w["hg_lb_logits"]], [(hg_w, F32)] * 5)
    iv = (proj0, 5)
    o_f, o_b, st_f, st_b = _attn_fwd("hg_attn", (qh, 0), (k_f, 0), (k_b, 0), iv, (lf_f, 0), (lf_b, 0), 4, 128, 128)
    post0_rows = [(hs_f, rg_w, 0), (hs_b, rg_w, 0), (proj0, rg_w, 1), (o_f, hg_w, 0), (o_b, hg_w, 0), (proj0, hg_w, 6)]
    hg_gain = _row2(w["hg_norm"])
    mix_in0 = _rowcall("l0_post", _post0_fwd_fn, post0_rows, [hg_gain], [(d_model, BF16)])[0]
    h_b0 = _mm("l0_out", mix_in0, w["ab_w_out"], extras=(h_a0,), epi=_add_epi)
    h_c0, mlp0 = _mlp_fwd("mlp0", h_b0, _row2(w["norm_mlp"][0]), w["mlp_w1"][0], w["mlp_w2"][0])

    h_a1 = h_c0
    gain1 = _row2(w["norm_mix"][1])
    y1 = _rowcall("l1_norm", _rmsnorm_fn, [(h_a1, d_model, 0)], [gain1], [(d_model, BF16)], tm=512)[0]
    proj1 = _mm("l1_in", y1, w["gla_w_in_pad"], tn=640)
    gla_pars = [w["gla_w_up_pad"], w["gla_b_gate"]]
    gq, glf_f, glf_b = _rowcall("gla_pre", _gla_pre_fn, [(proj1, 512, 0), (proj1, LANES, 24)], gla_pars, [(512, F32)] * 3)
    gk, gv = (proj1, 1), (proj1, 1)
    go_f, go_b, gst_f, gst_b = _attn_fwd("gla_attn", (gq, 0), gk, gk, gv, (glf_f, 0), (glf_b, 0), 4, 128, 256)
    gla_gain = _row2(w["gla_norm"])
    post1_rows = [(go_f, d_model, 0), (go_b, d_model, 0), (proj1, d_model, 2)]
    mix_in1 = _rowcall("l1_post", _gla_post_fwd_fn, post1_rows, [gla_gain], [(d_model, BF16)])[0]
    h_b1 = _mm("l1_out", mix_in1, w["gla_w_out"], extras=(h_a1,), epi=_add_epi)
    h_c1, mlp1 = _mlp_fwd("mlp1", h_b1, _row2(w["norm_mlp"][1]), w["mlp_w1"][1], w["mlp_w2"][1])

    dh, loss, g["norm_final"] = _rowcall(
        "loss_head", _loss_head_fn, [(h_c1, d_model, 0), (target, d_model, 0)], [_row2(w["norm_final"])],
        [(d_model, F32)], [(1, LANES), (1, d_model)], tm=512)

    dh, g_nmlp1, g_w1_1, g_w2_1 = _mlp_bwd("mlp1", h_b1, _row2(w["norm_mlp"][1]), w["mlp_w1"][1], w["mlp_w2"][1], mlp1, dh)
    dmix1 = _mm("l1_dout", dh, w["gla_w_out"], mode="nt")
    g["gla_w_out"] = _mm("l1_dwout", mix_in1, dh, mode="tn")
    dgo, dr, g["gla_norm"] = _rowcall(
        "l1_dpost", _gla_post_bwd_fn, post1_rows + [(dmix1, d_model, 0)], [gla_gain],
        [(d_model, F32), (d_model, BF16)], [(1, d_model)])
    (dq_f, dk_f, dv_f, dlf_f), (dq_b, dk_b, dv_b, dlf_b) = _attn_bwd(
        "gla_dattn", (gq, 0), gk, gk, gv, (glf_f, 0), (glf_b, 0), gst_f, gst_b, dgo, 4, 128, 256)

    def gla_pre_bwd(q, lr, dq1, dq2, dlf1, dlf2, dk1, dk2, dv1, dv2, w_up, b_gate):
        dlr = jnp.zeros_like(lr)
        dws, dbs = [], []
        for d, dlf in enumerate((dlf1, dlf2)):
            z = _raw_nn(lr, w_up[d]) + b_gate[d:d + 1]
            dz = dlf * _sigmoid(-z) * (1.0 / 16.0)
            dlr = dlr + _raw_nt(dz, w_up[d])
            dws.append(_raw_tn(dz, lr))
            dbs.append(jnp.sum(dz, axis=0, keepdims=True))
        return ((dq1 + dq2) * (128.0 ** -0.5), dk1 + dk2, dv1 + dv2, dlr, dws[0], dws[1], dbs[0], dbs[1])

    rows = [(proj1, 512, 0), (proj1, LANES, 24), (dq_f, 512, 0), (dq_b, 512, 0), (dlf_f, 512, 0), (dlf_b, 512, 0),
            (dk_f, 512, 0), (dk_b, 512, 0), (dv_f, d_model, 0), (dv_b, d_model, 0)]
    dq, dk, dv, dlr, dwt_f, dwt_b, db_f, db_b = _rowcall(
        "gla_dpre", gla_pre_bwd, rows, gla_pars, [(512, BF16), (512, BF16), (d_model, BF16), (LANES, BF16)],
        [(512, LANES), (512, LANES), (1, 512), (1, 512)])
    g["gla_w_up_pad"] = jnp.stack([dwt_f.T, dwt_b.T])
    g["gla_b_gate"] = jnp.concatenate([db_f, db_b], axis=0)
    dproj1 = jnp.concatenate([dq, dk, dv, dr, dlr], axis=1)
    g["gla_w_in_pad"] = _mm("l1_dwin", y1, dproj1, mode="tn", tn=640)
    dy1 = _mm("l1_dy", dproj1, w["gla_w_in_pad"], mode="nt", tk=640)
    dh, g_nmix1 = _norm_bwd("l1_dnorm", h_a1, gain1, dy1, dh)

    dh, g_nmlp0, g_w1_0, g_w2_0 = _mlp_bwd("mlp0", h_b0, _row2(w["norm_mlp"][0]), w["mlp_w1"][0], w["mlp_w2"][0], mlp0, dh)
    dmix0 = _mm("l0_dout", dh, w["ab_w_out"], mode="nt")
    g["ab_w_out"] = _mm("l0_dwout", mix_in0, dh, mode="tn")
    dhs, dga, do, dg, g["hg_norm"] = _rowcall(
        "l0_dpost", _post0_bwd_fn, post0_rows + [(dmix0, d_model, 0)], [hg_gain],
        [(rg_w, F32), (rg_w, BF16), (hg_w, F32), (hg_w, BF16)], [(1, hg_w)])
    (dqh_f, dk_f, div_f, dlf_f), (dqh_b, dk_b, div_b, dlf_b) = _attn_bwd(
        "hg_dattn", (qh, 0), (k_f, 0), (k_b, 0), iv, (lf_f, 0), (lf_b, 0), st_f, st_b, do, 4, 128, 128)

    def hg_pre_bwd(q, f_f, f_b, dq1, dq2, dk1, dlf1, dk2, dlf2, dv1, dv2, logits):
        _, vjp = jax.vjp(_hg_pre_fn, q, f_f, f_b, logits)
        dq, df_f, df_b, dlogits = vjp((dq1 + dq2, dk1, dlf1, dk2, dlf2))
        return dq, df_f, df_b, dv1 + dv2, dlogits

    rows = hg_rows + [(t, hg_w, 0) for t in (dqh_f, dqh_b, dk_f, dlf_f, dk_b, dlf_b, div_f, div_b)]
    dq, df_f, df_b, div, g["hg_lb_logits"] = _rowcall(
        "hg_dpre", hg_pre_bwd, rows, [w["hg_lb_logits"]], [(hg_w, BF16)] * 4, [(2, hg_w)])
    du_f, da_f = _scan_bwd("rg_dscan_f", a_f, hs_f, dhs, False)
    du_b, da_b = _scan_bwd("rg_dscan_b", a_b, hs_b, dhs, True)
    gates_bwd = _vjp_of(_rg_gates_fn, 1, 4, 5)
    rows = [(xc, rg_w, 0), (da_f, rg_w, 0), (du_f, rg_w, 0), (da_b, rg_w, 0), (du_b, rg_w, 0)]
    dxc, g["rg_wa_bd"], g["rg_wx_bd"], g["rg_b_a"], g["rg_b_x"], g["rg_lambda"] = _rowcall(
        "rg_dgates", gates_bwd, rows, gate_pars, [(rg_w, F32)],
        [(2, rg_w, rg_w), (2, rg_w, rg_w), (2, rg_w), (2, rg_w), (2, rg_w)])
    dxa, g["rg_conv_w"], g["rg_conv_b"] = _conv_bwd("rg_dconv", proj0, 0, conv_w, dxc)
    dproj0 = jnp.concatenate([dxa, dga, dq, df_f, df_b, div, dg], axis=1)
    g["ab_w_in"] = _mm("l0_dwin", y0, dproj0, mode="tn")
    dy0 = _mm("l0_dy", dproj0, w["ab_w_in"], mode="nt")
    grad_x, g_nmix0 = _norm_bwd("l0_dnorm", h_a0, gain, dy0, dh)

    g["norm_mix"] = jnp.concatenate([g_nmix0, g_nmix1], axis=0)
    g["norm_mlp"] = jnp.concatenate([g_nmlp0, g_nmlp1], axis=0)
    g["mlp_w1"] = jnp.stack([g_w1_0, g_w1_1])
    g["mlp_w2"] = jnp.stack([g_w2_0, g_w2_1])
    return loss, grad_x, g


def _block_diag(w):
    d, g, n, _ = w.shape
    eye = jnp.eye(g, dtype=w.dtype)
    return (w[:, :, :, None, :] * eye[None, :, None, :, None]).reshape(d, g * n, g * n)


def _block_diag_extract(wbd, g):
    d, gn, _ = wbd.shape
    n = gn // g
    blocks = wbd.reshape(d, g, n, g, n)
    return jnp.stack([blocks[:, i, :, i, :] for i in range(g)], axis=1)


def _prepare_weights(full):
    w = {k: full[k] for k in ("norm_mix", "norm_mlp", "norm_final", "hg_lb_logits")}
    for k in ("mlp_w1", "mlp_w2"):
        w[k] = full[k].astype(BF16)
    for k in ("ab_w_in", "ab_w_out", "gla_w_out"):
        w[k] = full[k][0].astype(BF16)
    for k in ("rg_conv_w", "rg_conv_b", "rg_b_a", "rg_b_x", "rg_lambda", "hg_norm", "gla_b_gate", "gla_norm"):
        w[k] = full[k][0]
    w["rg_wa_bd"] = _block_diag(full["rg_w_a"][0])
    w["rg_wx_bd"] = _block_diag(full["rg_w_x"][0])
    gla_in = full["gla_w_in"][0].astype(BF16)
    w["gla_w_in_pad"] = jnp.pad(gla_in, ((0, 0), (0, 3200 - gla_in.shape[1])))
    up = full["gla_w_gate_up"][0]
    rank = up.shape[1]
    pad = jnp.zeros((2, LANES, up.shape[2]), F32)
    w["gla_w_up_pad"] = pad.at[0, 0:rank].set(up[0]).at[1, rank:2 * rank].set(up[1])
    return w


def _finish_grads(g, rank=16, gla_in_width=3104, rg_blocks=8):
    out = {
        "norm_mix": g["norm_mix"], "norm_mlp": g["norm_mlp"], "norm_final": g["norm_final"][0],
        "mlp_w1": g["mlp_w1"], "mlp_w2": g["mlp_w2"],
        "ab_w_in": g["ab_w_in"][None], "ab_w_out": g["ab_w_out"][None],
        "rg_conv_w": g["rg_conv_w"][None], "rg_conv_b": g["rg_conv_b"],
        "rg_w_a": _block_diag_extract(g["rg_wa_bd"], rg_blocks)[None], "rg_b_a": g["rg_b_a"][None],
        "rg_w_x": _block_diag_extract(g["rg_wx_bd"], rg_blocks)[None], "rg_b_x": g["rg_b_x"][None],
        "rg_lambda": g["rg_lambda"][None], "hg_lb_logits": g["hg_lb_logits"], "hg_norm": g["hg_norm"],
        "gla_w_in": g["gla_w_in_pad"][None, :, :gla_in_width], "gla_w_out": g["gla_w_out"][None],
        "gla_w_gate_up": jnp.stack([g["gla_w_up_pad"][0, 0:rank], g["gla_w_up_pad"][1, rank:2 * rank]])[None],
        "gla_b_gate": g["gla_b_gate"][None], "gla_norm": g["gla_norm"],
    }
    return out


BIG = (("mlp_w1", 2), ("mlp_w2", 1), ("ab_w_in", 2), ("ab_w_out", 1), ("gla_w_in", 2), ("gla_w_out", 1))
SMALL_SHARDED = ("rg_conv_w", "rg_b_a", "rg_b_x", "rg_lambda", "gla_w_gate_up", "gla_b_gate", "gla_norm")
SMALL_REPLICATED = ("norm_mix", "norm_mlp", "norm_final", "rg_conv_b", "rg_w_a", "rg_w_x", "hg_lb_logits", "hg_norm")
WEIGHTS = ("norm_mix", "norm_mlp", "norm_final", "mlp_w1", "mlp_w2", "ab_w_in", "ab_w_out", "rg_conv_w", "rg_conv_b",
           "rg_w_a", "rg_b_a", "rg_w_x", "rg_b_x", "rg_lambda", "hg_lb_logits", "hg_norm", "gla_w_in", "gla_w_out",
           "gla_w_gate_up", "gla_b_gate", "gla_norm")
ROW_ALIGN = 16


def _pack(arrays, lead=0):
    head = arrays[0].shape[:lead]
    flat = jnp.concatenate([a.reshape(head + (-1,)) for a in arrays], axis=lead)
    n = flat.shape[-1]
    quantum = LANES * ROW_ALIGN
    padded = -(-n // quantum) * quantum
    if padded != n:
        flat = jnp.pad(flat, [(0, 0)] * lead + [(0, padded - n)])
    return flat.reshape(head + (padded // LANES, LANES))


def _unpack(buf, shapes, lead=0):
    head = buf.shape[:lead]
    flat = buf.reshape(head + (-1,))
    out, off = [], 0
    for s in shapes:
        n = 1
        for v in s:
            n *= v
        out.append(lax.slice_in_dim(flat, off, off + n, axis=lead).reshape(head + tuple(s)))
        off += n
    return out


def _join_chips(gathered, axis):
    t = jnp.moveaxis(gathered, 0, axis)
    return t.reshape(t.shape[:axis] + (t.shape[axis] * t.shape[axis + 1],) + t.shape[axis + 2:])


def _split_chips(full, axis):
    s = full.shape
    t = full.reshape(s[:axis] + (N_CHIPS, s[axis] // N_CHIPS) + s[axis + 1:])
    return jnp.moveaxis(t, axis, 0)


_ANY = pl.BlockSpec(memory_space=pl.ANY)


def _place():
    return lax.axis_index("x"), lax.axis_index("y"), lax.axis_index("c")


def _gather_chips(name, shard):
    def body(in_ref, out_ref, send_sems, recv_sems, local_sem):
        x, y, c = _place()
        me = 2 * x + y
        peers = [(1 - x, y), (x, 1 - y), (1 - x, 1 - y)]

        def copy(j, block):
            px, py = peers[j]
            return pltpu.make_async_remote_copy(
                src_ref=in_ref, dst_ref=out_ref.at[block], send_sem=send_sems.at[j], recv_sem=recv_sems.at[j],
                device_id=(px, py, c), device_id_type=MESH)

        local = pltpu.make_async_copy(in_ref, out_ref.at[me], local_sem)
        local.start()
        sends = [copy(j, me) for j in range(3)]
        for cp in sends:
            cp.start()
        for j, (px, py) in enumerate(peers):
            copy(j, 2 * px + py).wait_recv()
        for cp in sends:
            cp.wait_send()
        local.wait()

    return pl.pallas_call(
        body, name=name, in_specs=[_ANY], out_specs=_ANY,
        out_shape=jax.ShapeDtypeStruct((N_CHIPS,) + shard.shape, shard.dtype),
        scratch_shapes=[pltpu.SemaphoreType.DMA((3,)), pltpu.SemaphoreType.DMA((3,)), pltpu.SemaphoreType.DMA],
    )(shard)


def _pair_exchange(name, g):
    n, _, rh, lanes = g.shape

    def body(g_ref, out_ref, send_sem, recv_sem):
        x, y, c = _place()
        cp = pltpu.make_async_remote_copy(
            src_ref=g_ref.at[:, 1 - c], dst_ref=out_ref, send_sem=send_sem, recv_sem=recv_sem,
            device_id=(x, y, 1 - c), device_id_type=MESH)
        cp.start()
        cp.wait()

    return pl.pallas_call(
        body, name=name, in_specs=[_ANY], out_specs=_ANY,
        out_shape=jax.ShapeDtypeStruct((n, rh, lanes), g.dtype),
        scratch_shapes=[pltpu.SemaphoreType.DMA, pltpu.SemaphoreType.DMA],
    )(g)


def _pair_add(name, g, got, c, tm):
    n, _, rh, lanes = g.shape

    def body(c_ref, g_ref, got_ref, o_ref):
        o_ref[...] = g_ref[...] + got_ref[...]

    grid_spec = pltpu.PrefetchScalarGridSpec(
        num_scalar_prefetch=1, grid=(n, rh // tm),
        in_specs=[pl.BlockSpec((None, None, tm, lanes), lambda s, i, c_ref: (s, c_ref[0], i, 0)),
                  pl.BlockSpec((None, tm, lanes), lambda s, i, c_ref: (s, i, 0))],
        out_specs=pl.BlockSpec((None, tm, lanes), lambda s, i, c_ref: (s, i, 0)))
    return pl.pallas_call(
        body, name=name, grid_spec=grid_spec, out_shape=jax.ShapeDtypeStruct((n, rh, lanes), F32),
        compiler_params=_params(("parallel", "parallel")),
    )(c.reshape(1).astype(jnp.int32), g, got)


def _chip_scatter(name, p):
    def body(p_ref, out_ref, send_sems, recv_sems, local_sem):
        x, y, c = _place()
        me = 2 * x + y
        peers = [(1 - x, y), (x, 1 - y), (1 - x, 1 - y)]

        def copy(j, src_block, dst_block):
            px, py = peers[j]
            return pltpu.make_async_remote_copy(
                src_ref=p_ref.at[src_block], dst_ref=out_ref.at[dst_block], send_sem=send_sems.at[j],
                recv_sem=recv_sems.at[j], device_id=(px, py, c), device_id_type=MESH)

        local = pltpu.make_async_copy(p_ref.at[me], out_ref.at[me], local_sem)
        local.start()
        sends = [copy(j, 2 * px + py, me) for j, (px, py) in enumerate(peers)]
        for cp in sends:
            cp.start()
        for j, (px, py) in enumerate(peers):
            copy(j, me, 2 * px + py).wait_recv()
        for cp in sends:
            cp.wait_send()
        local.wait()

    return pl.pallas_call(
        body, name=name, in_specs=[_ANY], out_specs=_ANY, out_shape=jax.ShapeDtypeStruct(p.shape, p.dtype),
        scratch_shapes=[pltpu.SemaphoreType.DMA((3,)), pltpu.SemaphoreType.DMA((3,)), pltpu.SemaphoreType.DMA],
    )(p)


def _pair_gather(name, r):
    def body(r_ref, out_ref, send_sem, recv_sem, local_sem):
        x, y, c = _place()
        local = pltpu.make_async_copy(r_ref, out_ref.at[c], local_sem)
        local.start()
        send = pltpu.make_async_remote_copy(
            src_ref=r_ref, dst_ref=out_ref.at[c], send_sem=send_sem, recv_sem=recv_sem,
            device_id=(x, y, 1 - c), device_id_type=MESH)
        send.start()
        pltpu.make_async_remote_copy(
            src_ref=r_ref, dst_ref=out_ref.at[1 - c], send_sem=send_sem, recv_sem=recv_sem,
            device_id=(x, y, 1 - c), device_id_type=MESH).wait_recv()
        send.wait_send()
        local.wait()

    return pl.pallas_call(
        body, name=name, in_specs=[_ANY], out_specs=_ANY, out_shape=jax.ShapeDtypeStruct((2,) + r.shape, r.dtype),
        scratch_shapes=[pltpu.SemaphoreType.DMA, pltpu.SemaphoreType.DMA, pltpu.SemaphoreType.DMA],
    )(r)


def _gather_all(name, s):
    def body(in_ref, out_ref, send_sems, recv_sems, local_sem):
        x, y, c = _place()
        me = 4 * x + 2 * y + c
        peers = []
        for mask in range(1, N_DEV):
            fx, fy, fc = (mask >> 2) & 1, (mask >> 1) & 1, mask & 1
            peers.append((jnp.where(fx, 1 - x, x), jnp.where(fy, 1 - y, y), jnp.where(fc, 1 - c, c)))

        def copy(j, block):
            return pltpu.make_async_remote_copy(
                src_ref=in_ref, dst_ref=out_ref.at[block], send_sem=send_sems.at[j], recv_sem=recv_sems.at[j],
                device_id=peers[j], device_id_type=MESH)

        local = pltpu.make_async_copy(in_ref, out_ref.at[me], local_sem)
        local.start()
        sends = [copy(j, me) for j in range(N_DEV - 1)]
        for cp in sends:
            cp.start()
        for j, (px, py, pc) in enumerate(peers):
            copy(j, 4 * px + 2 * py + pc).wait_recv()
        for cp in sends:
            cp.wait_send()
        local.wait()

    return pl.pallas_call(
        body, name=name, in_specs=[_ANY], out_specs=_ANY,
        out_shape=jax.ShapeDtypeStruct((N_DEV,) + s.shape, s.dtype),
        scratch_shapes=[pltpu.SemaphoreType.DMA((N_DEV - 1,)), pltpu.SemaphoreType.DMA((N_DEV - 1,)),
                        pltpu.SemaphoreType.DMA],
    )(s)


def _sum_blocks(name, stacked, tm):
    n, r, lanes = stacked.shape

    def body(in_ref, o_ref):
        acc = in_ref[0]
        for j in range(1, n):
            acc = acc + in_ref[j]
        o_ref[...] = acc

    return pl.pallas_call(
        body, name=name, grid=(r // tm,), in_specs=[pl.BlockSpec((n, tm, lanes), lambda i: (0, i, 0))],
        out_specs=pl.BlockSpec((tm, lanes), lambda i: (i, 0)), out_shape=jax.ShapeDtypeStruct((r, lanes), F32),
        compiler_params=_params(("parallel",)),
    )(stacked)


def _row_tile(rows, pref, align):
    best = None
    for t in range(align, min(rows, pref) + 1, align):
        if rows % t == 0:
            best = t
    assert best is not None, (rows, pref, align)
    return best


def _adam(name, w, g, m, v):
    rows = w.shape[0]
    tm = _row_tile(rows, 4096, 8)
    args = [(t, LANES, 0) for t in (w, g, m, v)]
    return _rowcall(name, _adam_fn, args, [], [(LANES, F32)] * 3, tm=tm)


def kernel(x, norm_mix, norm_mlp, norm_final, mlp_w1, mlp_w2, ab_w_in, ab_w_out, rg_conv_w, rg_conv_b, rg_w_a, rg_b_a, rg_w_x, rg_b_x, rg_lambda, hg_lb_logits, hg_norm, gla_w_in, gla_w_out, gla_w_gate_up, gla_b_gate, gla_norm, loss_target, m_norm_mix, m_norm_mlp, m_norm_final, m_mlp_w1, m_mlp_w2, m_ab_w_in, m_ab_w_out, m_rg_conv_w, m_rg_conv_b, m_rg_w_a, m_rg_b_a, m_rg_w_x, m_rg_b_x, m_rg_lambda, m_hg_lb_logits, m_hg_norm, m_gla_w_in, m_gla_w_out, m_gla_w_gate_up, m_gla_b_gate, m_gla_norm, v_norm_mix, v_norm_mlp, v_norm_final, v_mlp_w1, v_mlp_w2, v_ab_w_in, v_ab_w_out, v_rg_conv_w, v_rg_conv_b, v_rg_w_a, v_rg_b_a, v_rg_w_x, v_rg_b_x, v_rg_lambda, v_hg_lb_logits, v_hg_norm, v_gla_w_in, v_gla_w_out, v_gla_w_gate_up, v_gla_b_gate, v_gla_norm):
    w = dict(norm_mix=norm_mix, norm_mlp=norm_mlp, norm_final=norm_final, mlp_w1=mlp_w1, mlp_w2=mlp_w2, ab_w_in=ab_w_in, ab_w_out=ab_w_out, rg_conv_w=rg_conv_w, rg_conv_b=rg_conv_b, rg_w_a=rg_w_a, rg_b_a=rg_b_a, rg_w_x=rg_w_x, rg_b_x=rg_b_x, rg_lambda=rg_lambda, hg_lb_logits=hg_lb_logits, hg_norm=hg_norm, gla_w_in=gla_w_in, gla_w_out=gla_w_out, gla_w_gate_up=gla_w_gate_up, gla_b_gate=gla_b_gate, gla_norm=gla_norm)
    m = dict(norm_mix=m_norm_mix, norm_mlp=m_norm_mlp, norm_final=m_norm_final, mlp_w1=m_mlp_w1, mlp_w2=m_mlp_w2, ab_w_in=m_ab_w_in, ab_w_out=m_ab_w_out, rg_conv_w=m_rg_conv_w, rg_conv_b=m_rg_conv_b, rg_w_a=m_rg_w_a, rg_b_a=m_rg_b_a, rg_w_x=m_rg_w_x, rg_b_x=m_rg_b_x, rg_lambda=m_rg_lambda, hg_lb_logits=m_hg_lb_logits, hg_norm=m_hg_norm, gla_w_in=m_gla_w_in, gla_w_out=m_gla_w_out, gla_w_gate_up=m_gla_w_gate_up, gla_b_gate=m_gla_b_gate, gla_norm=m_gla_norm)
    v = dict(norm_mix=v_norm_mix, norm_mlp=v_norm_mlp, norm_final=v_norm_final, mlp_w1=v_mlp_w1, mlp_w2=v_mlp_w2, ab_w_in=v_ab_w_in, ab_w_out=v_ab_w_out, rg_conv_w=v_rg_conv_w, rg_conv_b=v_rg_conv_b, rg_w_a=v_rg_w_a, rg_b_a=v_rg_b_a, rg_w_x=v_rg_w_x, rg_b_x=v_rg_b_x, rg_lambda=v_rg_lambda, hg_lb_logits=v_hg_lb_logits, hg_norm=v_hg_norm, gla_w_in=v_gla_w_in, gla_w_out=v_gla_w_out, gla_w_gate_up=v_gla_w_gate_up, gla_b_gate=v_gla_b_gate, gla_norm=v_gla_norm)
    chip = 2 * lax.axis_index("x") + lax.axis_index("y")
    core = lax.axis_index("c")
    big_names = [n for n, _ in BIG]
    big_shapes = [w[n].shape for n in big_names]
    sharded_shapes = [w[n].shape for n in SMALL_SHARDED]

    w_big = _pack([w[n] for n in big_names])
    cast_tm = _row_tile(w_big.shape[0], 4096, ROW_ALIGN)
    w_big16 = _rowcall("cast_weights", lambda t: (t,), [(w_big, LANES, 0)], [], [(LANES, BF16)], tm=cast_tm)[0]
    big_all = _unpack(_gather_chips("gather_weights", w_big16), big_shapes, lead=1)
    small_all = _unpack(_gather_chips("gather_vectors", _pack([w[n] for n in SMALL_SHARDED])), sharded_shapes, lead=1)
    full = {n: w[n] for n in SMALL_REPLICATED}
    for (n, axis), t in zip(BIG, big_all):
        full[n] = _join_chips(t, axis)
    for n, t in zip(SMALL_SHARDED, small_all):
        full[n] = _join_chips(t, t.ndim - 2)

    loss_part, grad_x, g_kernel = _local_step(x[0], loss_target[0], _prepare_weights(full))
    g_full = _finish_grads(g_kernel)
    loss = lax.psum(loss_part[0, 0], ("x", "y", "c"))

    g_big = _pack([_split_chips(g_full[n], axis) for n, axis in BIG], lead=1)
    rows = g_big.shape[1]
    half = rows // 2
    add_tm = _row_tile(half, 4096, 8)
    g_halves = g_big.reshape(N_CHIPS, 2, half, LANES)
    from_sibling = _pair_exchange("reduce_pair", g_halves)
    chip_part = _pair_add("reduce_pair_add", g_halves, from_sibling, core, add_tm)
    from_chips = _chip_scatter("reduce_chips", chip_part)
    mine = _sum_blocks("reduce_chips_add", from_chips, add_tm)
    g_big_red = _pair_gather("reduce_share", mine).reshape(rows, LANES)

    small_names = SMALL_REPLICATED + SMALL_SHARDED
    g_small = _pack([g_full[n] for n in small_names])
    g_small_all = _gather_all("reduce_small", g_small)
    g_small_red = _sum_blocks("reduce_small_add", g_small_all, g_small.shape[0])
    g_small_full = dict(zip(small_names, _unpack(g_small_red, [g_full[n].shape for n in small_names])))
    grads = {n: g_small_full[n] for n in SMALL_REPLICATED}
    for n in SMALL_SHARDED:
        width = w[n].shape[-1]
        grads[n] = lax.dynamic_slice_in_dim(g_small_full[n], chip * width, width, axis=g_small_full[n].ndim - 1)
    for n, t in zip(big_names, _unpack(g_big_red, big_shapes)):
        grads[n] = t

    d_big, m_big, v_big = _adam("adam_big", w_big, g_big_red, _pack([m[n] for n in big_names]), _pack([v[n] for n in big_names]))
    delta, new_m, new_v = {}, {}, {}
    for dst, buf in ((delta, d_big), (new_m, m_big), (new_v, v_big)):
        dst.update(zip(big_names, _unpack(buf, big_shapes)))
    small_shapes = [w[n].shape for n in small_names]
    packs = [_pack([src[n] for n in small_names]) for src in (w, grads, m, v)]
    d_small, m_small, v_small = _adam("adam_small", *packs)
    for dst, buf in ((delta, d_small), (new_m, m_small), (new_v, v_small)):
        dst.update(zip(small_names, _unpack(buf, small_shapes)))

    return (loss, grad_x[None], *[grads[n] for n in WEIGHTS], *[delta[n] for n in WEIGHTS],
            *[new_m[n] for n in WEIGHTS], *[new_v[n] for n in WEIGHTS])
```

```python
import functools

import jax
import jax.numpy as jnp
from jax import lax
from jax.experimental import pallas as pl
from jax.experimental.pallas import tpu as pltpu

F32 = jnp.float32
BF16 = jnp.bfloat16
MESH = pl.DeviceIdType.MESH

LANES = 128
CHUNK = 64
EPS = 1e-6
RG_C = 8.0
N_CHIPS = 4
N_DEV = 8
VMEM_LIMIT = 56 * 1024 * 1024

ADAM_LR = 0.001
ADAM_B1 = 0.9
ADAM_B2 = 0.999
ADAM_EPS = 1e-08
ADAM_WD = 0.01
ADAM_STEP = 10


def _raw_dot(a, b, ca, cb):
    return lax.dot_general(a.astype(BF16), b.astype(BF16), (((ca,), (cb,)), ((), ())),
                           preferred_element_type=F32)


def _raw_nn(a, b):
    return _raw_dot(a, b, 1, 0)


def _raw_nt(a, b):
    return _raw_dot(a, b, 1, 1)


def _raw_tn(a, b):
    return _raw_dot(a, b, 0, 0)


@jax.custom_vjp
def _dot_nn(a, b):
    return _raw_nn(a, b)


def _dot_nn_fwd(a, b):
    return _raw_nn(a, b), (a, b)


def _dot_nn_bwd(res, g):
    a, b = res
    return _raw_nt(g, b), _raw_tn(a, g)


_dot_nn.defvjp(_dot_nn_fwd, _dot_nn_bwd)


@jax.custom_vjp
def _dot_nt(a, b):
    return _raw_nt(a, b)


def _dot_nt_fwd(a, b):
    return _raw_nt(a, b), (a, b)


def _dot_nt_bwd(res, g):
    a, b = res
    return _raw_nn(g, b), _raw_tn(g, a)


_dot_nt.defvjp(_dot_nt_fwd, _dot_nt_bwd)


@jax.custom_vjp
def _dot_tn(a, b):
    return _raw_tn(a, b)


def _dot_tn_fwd(a, b):
    return _raw_tn(a, b), (a, b)


def _dot_tn_bwd(res, g):
    a, b = res
    return _raw_nt(b, g), _raw_nn(a, g)


_dot_tn.defvjp(_dot_tn_fwd, _dot_tn_bwd)


def _tile(n, pref):
    if n <= pref:
        return n
    t = (pref // LANES) * LANES
    while t > LANES and n % t:
        t -= LANES
    assert n % t == 0, (n, pref)
    return t


def _params(sem):
    return pltpu.CompilerParams(dimension_semantics=sem, vmem_limit_bytes=VMEM_LIMIT)


def _rowcall(name, fn, rows, pars, row_outs, par_outs=(), tm=256):
    n_rows = rows[0][0].shape[0]
    tm = min(tm, n_rows)
    assert n_rows % tm == 0
    n_r, n_p, n_ro = len(rows), len(pars), len(row_outs)

    def body(*refs):
        vals = [r[...].astype(F32) for r in refs[:n_r + n_p]]
        outs = fn(*vals)
        o_refs = refs[n_r + n_p:n_r + n_p + n_ro]
        po_refs = refs[n_r + n_p + n_ro:]
        for o_ref, val in zip(o_refs, outs[:n_ro]):
            o_ref[...] = val.astype(o_ref.dtype)
        first = pl.program_id(0) == 0
        for po_ref, val in zip(po_refs, outs[n_ro:]):
            @pl.when(first)
            def _():
                po_ref[...] = val

            @pl.when(jnp.logical_not(first))
            def _():
                po_ref[...] += val

    def const_map(nd):
        return lambda i: (0,) * nd

    in_specs = [pl.BlockSpec((tm, w), functools.partial(lambda i, cb: (i, cb), cb=cb)) for _, w, cb in rows]
    in_specs += [pl.BlockSpec(p.shape, const_map(p.ndim)) for p in pars]
    out_specs = [pl.BlockSpec((tm, w), lambda i: (i, 0)) for w, _ in row_outs]
    out_specs += [pl.BlockSpec(tuple(s), const_map(len(s))) for s in par_outs]
    out_shape = [jax.ShapeDtypeStruct((n_rows, w), dt) for w, dt in row_outs]
    out_shape += [jax.ShapeDtypeStruct(tuple(s), F32) for s in par_outs]
    return pl.pallas_call(
        body, name=name, grid=(n_rows // tm,), in_specs=in_specs, out_specs=out_specs, out_shape=out_shape,
        compiler_params=_params(("arbitrary",) if par_outs else ("parallel",)),
    )(*[r[0] for r in rows], *pars)


def _vjp_of(fn, n_prim, n_out, n_par, n_pass=0):
    def bwd(*args):
        prim = args[:n_prim]
        cts = args[n_prim:n_prim + n_out]
        passes = args[n_prim + n_out:n_prim + n_out + 2 * n_pass]
        pars = args[n_prim + n_out + 2 * n_pass:]
        _, vjp = jax.vjp(fn, *prim, *pars)
        grads = vjp(tuple(cts))
        sums = tuple(passes[2 * i] + passes[2 * i + 1] for i in range(n_pass))
        return tuple(grads[:n_prim]) + sums + tuple(grads[n_prim:])
    return bwd


def _mm(name, a, b, mode="nn", extras=(), epi=None, out_dtypes=(F32,), a_pro=None, tm=1024, tn=1024, tk=1024):
    if mode == "nn":
        (m, k), n = a.shape, b.shape[1]
    elif mode == "nt":
        (m, k), n = a.shape, b.shape[0]
    else:
        (k, m), n = a.shape, b.shape[1]
    tm, tn, tk = _tile(m, tm), _tile(n, tn), _tile(k, tk)
    nk = k // tk
    raw = {"nn": _raw_nn, "nt": _raw_nt, "tn": _raw_tn}[mode]
    n_e, n_o = len(extras), len(out_dtypes)
    if epi is None:
        epi = lambda acc: (acc,)

    def body(a_ref, b_ref, *rest):
        e_refs, o_refs, acc = rest[:n_e], rest[n_e:n_e + n_o], rest[-1]
        kk = pl.program_id(2)

        @pl.when(kk == 0)
        def _():
            acc[...] = jnp.zeros_like(acc)

        a_tile = a_ref[...] if a_pro is None else a_pro(a_ref[...].astype(F32))
        acc[...] += raw(a_tile, b_ref[...])

        @pl.when(kk == nk - 1)
        def _():
            res = epi(acc[...], *[e[...].astype(F32) for e in e_refs])
            for o_ref, r in zip(o_refs, res):
                o_ref[...] = r.astype(o_ref.dtype)

    a_spec = pl.BlockSpec((tk, tm), lambda i, j, kk: (kk, i)) if mode == "tn" else pl.BlockSpec((tm, tk), lambda i, j, kk: (i, kk))
    b_spec = pl.BlockSpec((tn, tk), lambda i, j, kk: (j, kk)) if mode == "nt" else pl.BlockSpec((tk, tn), lambda i, j, kk: (kk, j))
    mn_spec = pl.BlockSpec((tm, tn), lambda i, j, kk: (i, j))
    outs = pl.pallas_call(
        body, name=name, grid=(m // tm, n // tn, nk),
        in_specs=[a_spec, b_spec] + [mn_spec] * n_e, out_specs=[mn_spec] * n_o,
        out_shape=[jax.ShapeDtypeStruct((m, n), dt) for dt in out_dtypes],
        scratch_shapes=[pltpu.VMEM((tm, tn), F32)],
        compiler_params=_params(("parallel", "parallel", "arbitrary")),
    )(a, b, *extras)
    return outs[0] if n_o == 1 else outs


def _sigmoid(x):
    return jax.nn.sigmoid(x)


def _silu(x):
    return x * _sigmoid(x)


def _softplus(x):
    return jnp.maximum(x, 0.0) + jnp.log1p(jnp.exp(-jnp.abs(x)))


def _rmsnorm_fn(x, gain):
    return (x * lax.rsqrt(jnp.mean(x * x, axis=-1, keepdims=True) + EPS) * gain,)


def _head_norm(o, gain, n_heads):
    w = o.shape[-1] // n_heads
    parts = []
    for h in range(n_heads):
        oh = o[:, h * w:(h + 1) * w]
        parts.append(oh * lax.rsqrt(jnp.mean(oh * oh, axis=-1, keepdims=True) + EPS))
    return jnp.concatenate(parts, axis=-1) * gain


@jax.custom_jvp
def _neg_expm1(x):
    u = jnp.exp(x)
    is_one = u == 1.0
    return jnp.where(is_one, -x, (1.0 - u) * x / jnp.log(jnp.where(is_one, 2.0, u)))


@_neg_expm1.defjvp
def _neg_expm1_jvp(primals, tangents):
    (x,), (t,) = primals, tangents
    return _neg_expm1(x), -jnp.exp(x) * t


def _rg_gates_fn(xc, wa, wx, ba, bx, lam):
    outs = []
    for d in range(2):
        r = _sigmoid(_dot_nn(xc, wa[d]) + ba[d:d + 1])
        i = _sigmoid(_dot_nn(xc, wx[d]) + bx[d:d + 1])
        log_a = -RG_C * r * _softplus(-lam[d:d + 1])
        outs.append(jnp.exp(log_a))
        outs.append(jnp.sqrt(_neg_expm1(2.0 * log_a)) * (i * xc))
    return tuple(outs)


def _hg_pre_fn(q, f_f, f_b, logits):
    mx = jnp.maximum(logits[0:1], logits[1:2])
    e0 = jnp.exp(logits[0:1] - mx)
    e1 = jnp.exp(logits[1:2] - mx)
    lb = e0 / (e0 + e1)
    outs = [_silu(q)]
    for f in (f_f, f_b):
        outs.append((1.0 - lb) * _sigmoid(-f))
        outs.append(jnp.log(lb + (1.0 - lb) * _sigmoid(f)))
    return tuple(outs)


def _post0_fn(hs, ga, o, g, gain):
    ya = hs * jax.nn.gelu(ga, approximate=True)
    yb = _head_norm(o, gain, 4) * _silu(g)
    return (jnp.concatenate([ya, yb], axis=-1),)


def _post0_fwd_fn(h_f, h_b, ga, o_f, o_b, g, gain):
    return _post0_fn(h_f + h_b, ga, o_f + o_b, g, gain)


def _post0_bwd_fn(h_f, h_b, ga, o_f, o_b, g, dmix, gain):
    _, vjp = jax.vjp(_post0_fn, h_f + h_b, ga, o_f + o_b, g, gain)
    return vjp((dmix,))


def _gla_pre_fn(q, lr, w_up, b_gate):
    outs = [q * (128.0 ** -0.5)]
    for d in range(2):
        z = _dot_nn(lr, w_up[d]) + b_gate[d:d + 1]
        outs.append(-_softplus(-z) * (1.0 / 16.0))
    return tuple(outs)


def _gla_post_fn(o, r, gain):
    return (_head_norm(o, gain, 4) * _silu(r),)


def _gla_post_fwd_fn(o_f, o_b, r, gain):
    return _gla_post_fn(o_f + o_b, r, gain)


def _gla_post_bwd_fn(o_f, o_b, r, dmix, gain):
    _, vjp = jax.vjp(_gla_post_fn, o_f + o_b, r, gain)
    return vjp((dmix,))


def _relu2_bwd_epi(acc, hid):
    return (acc * 2.0 * jnp.maximum(hid, 0.0),)


def _relu2(x):
    r = jnp.maximum(x, 0.0)
    return r * r


def _add_epi(acc, res):
    return (acc + res,)


def _loss_head_fn(h, target, gain):
    def f(h, gain):
        y = _rmsnorm_fn(h, gain)[0]
        err = y - target
        return 0.5 * jnp.sum(jnp.mean(err * err, axis=-1, keepdims=True))
    loss, (dh, dgain) = jax.value_and_grad(f, argnums=(0, 1))(h, gain)
    return dh, jnp.full((1, LANES), loss, F32), dgain


def _adam_fn(w, g, m, v):
    m2 = ADAM_B1 * m + (1.0 - ADAM_B1) * g
    v2 = ADAM_B2 * v + (1.0 - ADAM_B2) * (g * g)
    m_hat = m2 / (1.0 - ADAM_B1 ** ADAM_STEP)
    v_hat = v2 / (1.0 - ADAM_B2 ** ADAM_STEP)
    delta = -ADAM_LR * (m_hat / (jnp.sqrt(v_hat) + ADAM_EPS) + ADAM_WD * w)
    return delta, m2, v2


def _shifted(x, t_idx, off):
    n = x.shape[0]
    rolled = pltpu.roll(x, (-off) % n, 0)
    valid = (t_idx + off >= 0) & (t_idx + off < n)
    return jnp.where(valid, rolled, 0.0)


def _conv_fwd(name, src, colblock, w, b):
    n_rows, width = src.shape[0], w.shape[1]

    def body(x_ref, w_ref, b_ref, o_ref):
        x = x_ref[...]
        t_idx = lax.broadcasted_iota(jnp.int32, x.shape, 0)
        acc = b_ref[...] + w_ref[2:3, :] * x
        acc += w_ref[0:1, :] * _shifted(x, t_idx, -2)
        acc += w_ref[1:2, :] * _shifted(x, t_idx, -1)
        acc += w_ref[3:4, :] * _shifted(x, t_idx, 1)
        o_ref[...] = acc

    nb = width // LANES
    return pl.pallas_call(
        body, name=name, grid=(nb,),
        in_specs=[pl.BlockSpec((n_rows, LANES), lambda j: (0, colblock * nb + j)),
                  pl.BlockSpec((4, LANES), lambda j: (0, j)), pl.BlockSpec((1, LANES), lambda j: (0, j))],
        out_specs=pl.BlockSpec((n_rows, LANES), lambda j: (0, j)),
        out_shape=jax.ShapeDtypeStruct((n_rows, width), F32),
        compiler_params=_params(("parallel",)),
    )(src, w, b)


def _conv_bwd(name, src, colblock, w, d):
    n_rows, width = src.shape[0], w.shape[1]

    def body(x_ref, w_ref, d_ref, dx_ref, dw_ref, db_ref):
        x = x_ref[...]
        g = d_ref[...]
        t_idx = lax.broadcasted_iota(jnp.int32, x.shape, 0)
        dx = w_ref[2:3, :] * g
        dx += w_ref[0:1, :] * _shifted(g, t_idx, 2)
        dx += w_ref[1:2, :] * _shifted(g, t_idx, 1)
        dx += w_ref[3:4, :] * _shifted(g, t_idx, -1)
        dx_ref[...] = dx.astype(dx_ref.dtype)
        dw_ref[0:1, :] = jnp.sum(g * _shifted(x, t_idx, -2), axis=0, keepdims=True)
        dw_ref[1:2, :] = jnp.sum(g * _shifted(x, t_idx, -1), axis=0, keepdims=True)
        dw_ref[2:3, :] = jnp.sum(g * x, axis=0, keepdims=True)
        dw_ref[3:4, :] = jnp.sum(g * _shifted(x, t_idx, 1), axis=0, keepdims=True)
        db_ref[...] = jnp.sum(g, axis=0, keepdims=True)

    nb = width // LANES
    return pl.pallas_call(
        body, name=name, grid=(nb,),
        in_specs=[pl.BlockSpec((n_rows, LANES), lambda j: (0, colblock * nb + j)),
                  pl.BlockSpec((4, LANES), lambda j: (0, j)),
                  pl.BlockSpec((n_rows, LANES), lambda j: (0, j))],
        out_specs=[pl.BlockSpec((n_rows, LANES), lambda j: (0, j)), pl.BlockSpec((4, LANES), lambda j: (0, j)),
                   pl.BlockSpec((1, LANES), lambda j: (0, j))],
        out_shape=[jax.ShapeDtypeStruct((n_rows, width), BF16), jax.ShapeDtypeStruct((4, width), F32),
                   jax.ShapeDtypeStruct((1, width), F32)],
        compiler_params=_params(("parallel",)),
    )(src, w, d)


_WHOLE = pl.BlockSpec(memory_space=pltpu.VMEM)
SCAN_UNROLL = 8


def _scan_fwd(name, a, u, reverse):
    n_rows, width = a.shape

    def body(a_ref, u_ref, h_ref):
        def step(i, h):
            t = (n_rows - 1 - i) if reverse else i
            h = a_ref[pl.ds(t, 1), :] * h + u_ref[pl.ds(t, 1), :]
            h_ref[pl.ds(t, 1), :] = h
            return h
        lax.fori_loop(0, n_rows, step, jnp.zeros((1, width), F32), unroll=SCAN_UNROLL)

    return pl.pallas_call(
        body, name=name, in_specs=[_WHOLE, _WHOLE], out_specs=_WHOLE,
        out_shape=jax.ShapeDtypeStruct((n_rows, width), F32),
        compiler_params=pltpu.CompilerParams(vmem_limit_bytes=VMEM_LIMIT),
    )(a, u)


def _scan_bwd(name, a, h, dh, reverse):
    n_rows, width = a.shape

    def body(a_ref, h_ref, dh_ref, du_ref, da_ref):
        def step(i, carry):
            t = i if reverse else (n_rows - 1 - i)
            g = dh_ref[pl.ds(t, 1), :] + carry
            du_ref[pl.ds(t, 1), :] = g
            tp = t + 1 if reverse else t - 1
            valid = (tp >= 0) & (tp < n_rows)
            h_prev = h_ref[pl.ds(jnp.clip(tp, 0, n_rows - 1), 1), :]
            da_ref[pl.ds(t, 1), :] = jnp.where(valid, g * h_prev, 0.0)
            return a_ref[pl.ds(t, 1), :] * g
        lax.fori_loop(0, n_rows, step, jnp.zeros((1, width), F32), unroll=SCAN_UNROLL)

    return pl.pallas_call(
        body, name=name, in_specs=[_WHOLE] * 3, out_specs=[_WHOLE] * 2,
        out_shape=[jax.ShapeDtypeStruct((n_rows, width), F32)] * 2,
        compiler_params=pltpu.CompilerParams(vmem_limit_bytes=VMEM_LIMIT),
    )(a, h, dh)


def _tri_mask(c, reverse):
    row = lax.broadcasted_iota(jnp.int32, (c, c), 0)
    col = lax.broadcasted_iota(jnp.int32, (c, c), 1)
    return (col >= row) if reverse else (col <= row)


def _cumsum_rows(x, reverse):
    tri = _tri_mask(x.shape[0], reverse).astype(BF16)
    hi = x.astype(BF16)
    rest = x - hi.astype(F32)
    mid = rest.astype(BF16)
    lo = (rest - mid.astype(F32)).astype(BF16)
    return _raw_nn(tri, hi) + _raw_nn(tri, mid) + _raw_nn(tri, lo)


@functools.partial(jax.custom_vjp, nondiff_argnums=(1,))
def _cumsum(x, reverse):
    return _cumsum_rows(x, reverse)


def _cumsum_fwd(x, reverse):
    return _cumsum_rows(x, reverse), None


def _cumsum_bwd(reverse, _, g):
    return (_cumsum_rows(g, not reverse),)


_cumsum.defvjp(_cumsum_fwd, _cumsum_bwd)


def _chunks_fn(qs, ks, vs, lfs, sts, reverses):
    n, c = len(qs), qs[0].shape[0]
    every = range(n)
    tris = [_tri_mask(c, r) for r in reverses]
    cums = [_cumsum(lfs[i], reverses[i]) for i in every]
    rid = lax.broadcasted_iota(jnp.int32, cums[0].shape, 0)

    def pick(cum, r):
        return jnp.sum(jnp.where(rid == r, cum, 0.0), axis=0, keepdims=True)

    refs = [pick(cums[i], (c - 1 - c // 2) if reverses[i] else c // 2) for i in every]
    lasts = [pick(cums[i], 0 if reverses[i] else c - 1) for i in every]
    q_in = [qs[i] * jnp.exp(cums[i] - refs[i]) for i in every]
    k_in = [ks[i] * jnp.exp(refs[i] - cums[i]) for i in every]
    scores = [jnp.where(tris[i], _dot_nt(q_in[i], k_in[i]), 0.0) for i in every]
    o_intra = [_dot_nn(scores[i], vs[i]) for i in every]
    q_out = [qs[i] * jnp.exp(cums[i]) for i in every]
    o_inter = [_dot_nt(q_out[i], sts[i]) for i in every]
    k_state = [ks[i] * jnp.exp(lasts[i] - cums[i]) for i in every]
    upd = [_dot_tn(vs[i], k_state[i]) for i in every]
    st_new = [sts[i] * jnp.exp(lasts[i]) + upd[i] for i in every]
    return [o_intra[i] + o_inter[i] for i in every], st_new


def _attn_fwd(name, q, k_f, k_b, v, lf_f, lf_b, n_heads, dk, dv):
    n_rows = q[0].shape[0]
    n_chunks = n_rows // CHUNK
    wk, wv = n_heads * dk, n_heads * dv

    def spec(width, off, rev):
        return pl.BlockSpec((CHUNK, width), lambda n: ((n_chunks - 1 - n) if rev else n, off))

    def sspec(rev):
        return pl.BlockSpec((None, n_heads, dv, dk), lambda n: ((n_chunks - 1 - n) if rev else n, 0, 0, 0))

    def body(qf, kf, vf, lff, qb, kb, vb, lfb, of_ref, ob_ref, sf_ref, sb_ref, st):
        @pl.when(pl.program_id(0) == 0)
        def _():
            st[...] = jnp.zeros_like(st)

        ins = ((qf, kf, vf, lff), (qb, kb, vb, lfb))
        chains = [(d, h) for d in range(2) for h in range(n_heads)]
        ck = [slice(h * dk, (h + 1) * dk) for h in range(n_heads)]
        cv = [slice(h * dv, (h + 1) * dv) for h in range(n_heads)]
        qs = [ins[d][0][:, ck[h]] for d, h in chains]
        ks = [ins[d][1][:, ck[h]] for d, h in chains]
        vs = [ins[d][2][:, cv[h]] for d, h in chains]
        lfs = [ins[d][3][:, ck[h]] for d, h in chains]
        sts = [st[d, h] for d, h in chains]
        os_, st_new = _chunks_fn(qs, ks, vs, lfs, sts, [d == 1 for d, _ in chains])
        for i, (d, h) in enumerate(chains):
            (sf_ref, sb_ref)[d][h] = sts[i]
            (of_ref, ob_ref)[d][:, cv[h]] = os_[i]
            st[d, h] = st_new[i]

    in_specs = [spec(wk, q[1], False), spec(wk, k_f[1], False), spec(wv, v[1], False), spec(wk, lf_f[1], False),
                spec(wk, q[1], True), spec(wk, k_b[1], True), spec(wv, v[1], True), spec(wk, lf_b[1], True)]
    return pl.pallas_call(
        body, name=name, grid=(n_chunks,), in_specs=in_specs,
        out_specs=[spec(wv, 0, False), spec(wv, 0, True), sspec(False), sspec(True)],
        out_shape=[jax.ShapeDtypeStruct((n_rows, wv), F32)] * 2
        + [jax.ShapeDtypeStruct((n_chunks, n_heads, dv, dk), F32)] * 2,
        scratch_shapes=[pltpu.VMEM((2, n_heads, dv, dk), F32)],
        compiler_params=_params(("arbitrary",)),
    )(q[0], k_f[0], v[0], lf_f[0], q[0], k_b[0], v[0], lf_b[0])


def _attn_bwd(name, q, k_f, k_b, v, lf_f, lf_b, st_f, st_b, do, n_heads, dk, dv, out_dtype=F32):
    n_rows = q[0].shape[0]
    n_chunks = n_rows // CHUNK
    wk, wv = n_heads * dk, n_heads * dv

    def spec(width, off, rev):
        return pl.BlockSpec((CHUNK, width), lambda n: (n if rev else (n_chunks - 1 - n), off))

    def sspec(rev):
        return pl.BlockSpec((None, n_heads, dv, dk), lambda n: (n if rev else (n_chunks - 1 - n), 0, 0, 0))

    def body(qf, kf, vf, lff, sf, dof, qb, kb, vb, lfb, sb, dob,
             dqf, dkf, dvf, dlff, dqb, dkb, dvb, dlfb, dst):
        @pl.when(pl.program_id(0) == 0)
        def _():
            dst[...] = jnp.zeros_like(dst)

        ins = ((qf, kf, vf, lff, sf, dof), (qb, kb, vb, lfb, sb, dob))
        outs = ((dqf, dkf, dvf, dlff), (dqb, dkb, dvb, dlfb))
        chains = [(d, h) for d in range(2) for h in range(n_heads)]
        ck = [slice(h * dk, (h + 1) * dk) for h in range(n_heads)]
        cv = [slice(h * dv, (h + 1) * dv) for h in range(n_heads)]
        qs = [ins[d][0][:, ck[h]] for d, h in chains]
        ks = [ins[d][1][:, ck[h]] for d, h in chains]
        vs = [ins[d][2][:, cv[h]] for d, h in chains]
        lfs = [ins[d][3][:, ck[h]] for d, h in chains]
        sts = [ins[d][4][h] for d, h in chains]
        dos = [ins[d][5][:, cv[h]] for d, h in chains]
        dsts = [dst[d, h] for d, h in chains]
        fn = functools.partial(_chunks_fn, reverses=[d == 1 for d, _ in chains])
        _, vjp = jax.vjp(fn, qs, ks, vs, lfs, sts)
        dqs, dks, dvs, dlfs, dst_prev = vjp((dos, dsts))
        for i, (d, h) in enumerate(chains):
            dq_r, dk_r, dv_r, dlf_r = outs[d]
            dq_r[:, ck[h]] = dqs[i].astype(dq_r.dtype)
            dk_r[:, ck[h]] = dks[i].astype(dk_r.dtype)
            dv_r[:, cv[h]] = dvs[i].astype(dv_r.dtype)
            dlf_r[:, ck[h]] = dlfs[i].astype(dlf_r.dtype)
            dst[d, h] = dst_prev[i]

    def dir_specs(kk, lf, rev):
        return [spec(wk, q[1], rev), spec(wk, kk[1], rev), spec(wv, v[1], rev), spec(wk, lf[1], rev), sspec(rev),
                spec(wv, 0, rev)]

    def dir_out_specs(rev):
        return [spec(wk, 0, rev), spec(wk, 0, rev), spec(wv, 0, rev), spec(wk, 0, rev)]

    shapes = [jax.ShapeDtypeStruct((n_rows, wk), out_dtype), jax.ShapeDtypeStruct((n_rows, wk), out_dtype),
              jax.ShapeDtypeStruct((n_rows, wv), out_dtype), jax.ShapeDtypeStruct((n_rows, wk), F32)]
    outs = pl.pallas_call(
        body, name=name, grid=(n_chunks,), in_specs=dir_specs(k_f, lf_f, False) + dir_specs(k_b, lf_b, True),
        out_specs=dir_out_specs(False) + dir_out_specs(True), out_shape=shapes + shapes,
        scratch_shapes=[pltpu.VMEM((2, n_heads, dv, dk), F32)],
        compiler_params=_params(("arbitrary",)),
    )(q[0], k_f[0], v[0], lf_f[0], st_f, do, q[0], k_b[0], v[0], lf_b[0], st_b, do)
    return outs[:4], outs[4:]


def _row2(v):
    return v.reshape(1, -1)


def _mlp_fwd(tag, h, gain, w1, w2):
    y = _rowcall(f"{tag}_norm", _rmsnorm_fn, [(h, h.shape[1], 0)], [gain], [(h.shape[1], BF16)], tm=512)[0]
    hid = _mm(f"{tag}_up", y, w1, out_dtypes=(BF16,))
    h_out = _mm(f"{tag}_down", hid, w2, a_pro=_relu2, extras=(h,), epi=_add_epi)
    return h_out, (y, hid)


def _mlp_bwd(tag, h, gain, w1, w2, saved, dh_out):
    y, hid = saved
    dhid = _mm(f"{tag}_dact", dh_out, w2, mode="nt", extras=(hid,), epi=_relu2_bwd_epi, out_dtypes=(BF16,))
    dw2 = _mm(f"{tag}_dw2", hid, dh_out, mode="tn", a_pro=_relu2)
    dw1 = _mm(f"{tag}_dw1", y, dhid, mode="tn")
    dy = _mm(f"{tag}_dy", dhid, w1, mode="nt")
    dh, dgain = _norm_bwd(f"{tag}_dnorm", h, gain, dy, dh_out)
    return dh, dgain, dw1, dw2


def _norm_bwd(name, h, gain, dy, dres):
    d = h.shape[1]

    def fn(h, dy, dres, gain):
        _, vjp = jax.vjp(lambda a, b: _rmsnorm_fn(a, b)[0], h, gain)
        dh, dgain = vjp(dy)
        return dh + dres, dgain

    dh, dgain = _rowcall(name, fn, [(h, d, 0), (dy, d, 0), (dres, d, 0)], [gain], [(d, F32)], [(1, d)], tm=512)
    return dh, dgain


def _local_step(x, target, w):
    g = {}
    d_model = x.shape[1]
    rg_w = hg_w = d_model // 2

    h_a0 = x
    gain = _row2(w["norm_mix"][0])
    y0 = _rowcall("l0_norm", _rmsnorm_fn, [(h_a0, d_model, 0)], [gain], [(d_model, BF16)], tm=512)[0]
    proj0 = _mm("l0_in", y0, w["ab_w_in"])
    conv_w, conv_b = w["rg_conv_w"], _row2(w["rg_conv_b"])
    xc = _conv_fwd("rg_conv", proj0, 0, conv_w, conv_b)
    gate_pars = [w["rg_wa_bd"], w["rg_wx_bd"], w["rg_b_a"], w["rg_b_x"], w["rg_lambda"]]
    a_f, u_f, a_b, u_b = _rowcall("rg_gates", _rg_gates_fn, [(xc, rg_w, 0)], gate_pars, [(rg_w, F32)] * 4)
    hs_f = _scan_fwd("rg_scan_f", a_f, u_f, False)
    hs_b = _scan_fwd("rg_scan_b", a_b, u_b, True)
    hg_rows = [(proj0, hg_w, 2), (proj0, hg_w, 3), (proj0, hg_w, 4)]
    qh, k_f, lf_f, k_b, lf_b = _rowcall("hg_pre", _hg_pre_fn, hg_rows, [w["hg_lb_logits"]], [(hg_w, F32)] * 5)
    iv = (proj0, 5)
    o_f, o_b, st_f, st_b = _attn_fwd("hg_attn", (qh, 0), (k_f, 0), (k_b, 0), iv, (lf_f, 0), (lf_b, 0), 4, 128, 128)
    post0_rows = [(hs_f, rg_w, 0), (hs_b, rg_w, 0), (proj0, rg_w, 1), (o_f, hg_w, 0), (o_b, hg_w, 0), (proj0, hg_w, 6)]
    hg_gain = _row2(w["hg_norm"])
    mix_in0 = _rowcall("l0_post", _post0_fwd_fn, post0_rows, [hg_gain], [(d_model, BF16)])[0]
    h_b0 = _mm("l0_out", mix_in0, w["ab_w_out"], extras=(h_a0,), epi=_add_epi)
    h_c0, mlp0 = _mlp_fwd("mlp0", h_b0, _row2(w["norm_mlp"][0]), w["mlp_w1"][0], w["mlp_w2"][0])

    h_a1 = h_c0
    gain1 = _row2(w["norm_mix"][1])
    y1 = _rowcall("l1_norm", _rmsnorm_fn, [(h_a1, d_model, 0)], [gain1], [(d_model, BF16)], tm=512)[0]
    proj1 = _mm("l1_in", y1, w["gla_w_in_pad"], tn=640)
    gla_pars = [w["gla_w_up_pad"], w["gla_b_gate"]]
    gq, glf_f, glf_b = _rowcall("gla_pre", _gla_pre_fn, [(proj1, 512, 0), (proj1, LANES, 24)], gla_pars, [(512, F32)] * 3)
    gk, gv = (proj1, 1), (proj1, 1)
    go_f, go_b, gst_f, gst_b = _attn_fwd("gla_attn", (gq, 0), gk, gk, gv, (glf_f, 0), (glf_b, 0), 4, 128, 256)
    gla_gain = _row2(w["gla_norm"])
    post1_rows = [(go_f, d_model, 0), (go_b, d_model, 0), (proj1, d_model, 2)]
    mix_in1 = _rowcall("l1_post", _gla_post_fwd_fn, post1_rows, [gla_gain], [(d_model, BF16)])[0]
    h_b1 = _mm("l1_out", mix_in1, w["gla_w_out"], extras=(h_a1,), epi=_add_epi)
    h_c1, mlp1 = _mlp_fwd("mlp1", h_b1, _row2(w["norm_mlp"][1]), w["mlp_w1"][1], w["mlp_w2"][1])

    dh, loss, g["norm_final"] = _rowcall(
        "loss_head", _loss_head_fn, [(h_c1, d_model, 0), (target, d_model, 0)], [_row2(w["norm_final"])],
        [(d_model, F32)], [(1, LANES), (1, d_model)], tm=512)

    dh, g_nmlp1, g_w1_1, g_w2_1 = _mlp_bwd("mlp1", h_b1, _row2(w["norm_mlp"][1]), w["mlp_w1"][1], w["mlp_w2"][1], mlp1, dh)
    dmix1 = _mm("l1_dout", dh, w["gla_w_out"], mode="nt")
    g["gla_w_out"] = _mm("l1_dwout", mix_in1, dh, mode="tn")
    dgo, dr, g["gla_norm"] = _rowcall(
        "l1_dpost", _gla_post_bwd_fn, post1_rows + [(dmix1, d_model, 0)], [gla_gain],
        [(d_model, F32), (d_model, BF16)], [(1, d_model)])
    (dq_f, dk_f, dv_f, dlf_f), (dq_b, dk_b, dv_b, dlf_b) = _attn_bwd(
        "gla_dattn", (gq, 0), gk, gk, gv, (glf_f, 0), (glf_b, 0), gst_f, gst_b, dgo, 4, 128, 256)

    def gla_pre_bwd(q, lr, dq1, dq2, dlf1, dlf2, dk1, dk2, dv1, dv2, w_up, b_gate):
        dlr = jnp.zeros_like(lr)
        dws, dbs = [], []
        for d, dlf in enumerate((dlf1, dlf2)):
            z = _raw_nn(lr, w_up[d]) + b_gate[d:d + 1]
            dz = dlf * _sigmoid(-z) * (1.0 / 16.0)
            dlr = dlr + _raw_nt(dz, w_up[d])
            dws.append(_raw_tn(dz, lr))
            dbs.append(jnp.sum(dz, axis=0, keepdims=True))
        return ((dq1 + dq2) * (128.0 ** -0.5), dk1 + dk2, dv1 + dv2, dlr, dws[0], dws[1], dbs[0], dbs[1])

    rows = [(proj1, 512, 0), (proj1, LANES, 24), (dq_f, 512, 0), (dq_b, 512, 0), (dlf_f, 512, 0), (dlf_b, 512, 0),
            (dk_f, 512, 0), (dk_b, 512, 0), (dv_f, d_model, 0), (dv_b, d_model, 0)]
    dq, dk, dv, dlr, dwt_f, dwt_b, db_f, db_b = _rowcall(
        "gla_dpre", gla_pre_bwd, rows, gla_pars, [(512, BF16), (512, BF16), (d_model, BF16), (LANES, BF16)],
        [(512, LANES), (512, LANES), (1, 512), (1, 512)])
    g["gla_w_up_pad"] = jnp.stack([dwt_f.T, dwt_b.T])
    g["gla_b_gate"] = jnp.concatenate([db_f, db_b], axis=0)
    dproj1 = jnp.concatenate([dq, dk, dv, dr, dlr], axis=1)
    g["gla_w_in_pad"] = _mm("l1_dwin", y1, dproj1, mode="tn", tn=640)
    dy1 = _mm("l1_dy", dproj1, w["gla_w_in_pad"], mode="nt", tk=640)
    dh, g_nmix1 = _norm_bwd("l1_dnorm", h_a1, gain1, dy1, dh)

    dh, g_nmlp0, g_w1_0, g_w2_0 = _mlp_bwd("mlp0", h_b0, _row2(w["norm_mlp"][0]), w["mlp_w1"][0], w["mlp_w2"][0], mlp0, dh)
    dmix0 = _mm("l0_dout", dh, w["ab_w_out"], mode="nt")
    g["ab_w_out"] = _mm("l0_dwout", mix_in0, dh, mode="tn")
    dhs, dga, do, dg, g["hg_norm"] = _rowcall(
        "l0_dpost", _post0_bwd_fn, post0_rows + [(dmix0, d_model, 0)], [hg_gain],
        [(rg_w, F32), (rg_w, BF16), (hg_w, F32), (hg_w, BF16)], [(1, hg_w)])
    (dqh_f, dk_f, div_f, dlf_f), (dqh_b, dk_b, div_b, dlf_b) = _attn_bwd(
        "hg_dattn", (qh, 0), (k_f, 0), (k_b, 0), iv, (lf_f, 0), (lf_b, 0), st_f, st_b, do, 4, 128, 128)

    def hg_pre_bwd(q, f_f, f_b, dq1, dq2, dk1, dlf1, dk2, dlf2, dv1, dv2, logits):
        _, vjp = jax.vjp(_hg_pre_fn, q, f_f, f_b, logits)
        dq, df_f, df_b, dlogits = vjp((dq1 + dq2, dk1, dlf1, dk2, dlf2))
        return dq, df_f, df_b, dv1 + dv2, dlogits

    rows = hg_rows + [(t, hg_w, 0) for t in (dqh_f, dqh_b, dk_f, dlf_f, dk_b, dlf_b, div_f, div_b)]
    dq, df_f, df_b, div, g["hg_lb_logits"] = _rowcall(
        "hg_dpre", hg_pre_bwd, rows, [w["hg_lb_logits"]], [(hg_w, BF16)] * 4, [(2, hg_w)])
    du_f, da_f = _scan_bwd("rg_dscan_f", a_f, hs_f, dhs, False)
    du_b, da_b = _scan_bwd("rg_dscan_b", a_b, hs_b, dhs, True)
    gates_bwd = _vjp_of(_rg_gates_fn, 1, 4, 5)
    rows = [(xc, rg_w, 0), (da_f, rg_w, 0), (du_f, rg_w, 0), (da_b, rg_w, 0), (du_b, rg_w, 0)]
    dxc, g["rg_wa_bd"], g["rg_wx_bd"], g["rg_b_a"], g["rg_b_x"], g["rg_lambda"] = _rowcall(
        "rg_dgates", gates_bwd, rows, gate_pars, [(rg_w, F32)],
        [(2, rg_w, rg_w), (2, rg_w, rg_w), (2, rg_w), (2, rg_w), (2, rg_w)])
    dxa, g["rg_conv_w"], g["rg_conv_b"] = _conv_bwd("rg_dconv", proj0, 0, conv_w, dxc)
    dproj0 = jnp.concatenate([dxa, dga, dq, df_f, df_b, div, dg], axis=1)
    g["ab_w_in"] = _mm("l0_dwin", y0, dproj0, mode="tn")
    dy0 = _mm("l0_dy", dproj0, w["ab_w_in"], mode="nt")
    grad_x, g_nmix0 = _norm_bwd("l0_dnorm", h_a0, gain, dy0, dh)

    g["norm_mix"] = jnp.concatenate([g_nmix0, g_nmix1], axis=0)
    g["norm_mlp"] = jnp.concatenate([g_nmlp0, g_nmlp1], axis=0)
    g["mlp_w1"] = jnp.stack([g_w1_0, g_w1_1])
    g["mlp_w2"] = jnp.stack([g_w2_0, g_w2_1])
    return loss, grad_x, g


def _block_diag(w):
    d, g, n, _ = w.shape
    eye = jnp.eye(g, dtype=w.dtype)
    return (w[:, :, :, None, :] * eye[None, :, None, :, None]).reshape(d, g * n, g * n)


def _block_diag_extract(wbd, g):
    d, gn, _ = wbd.shape
    n = gn // g
    blocks = wbd.reshape(d, g, n, g, n)
    return jnp.stack([blocks[:, i, :, i, :] for i in range(g)], axis=1)


def _prepare_weights(full):
    w = {k: full[k] for k in ("norm_mix", "norm_mlp", "norm_final", "hg_lb_logits")}
    for k in ("mlp_w1", "mlp_w2"):
        w[k] = full[k].astype(BF16)
    for k in ("ab_w_in", "ab_w_out", "gla_w_out"):
        w[k] = full[k][0].astype(BF16)
    for k in ("rg_conv_w", "rg_conv_b", "rg_b_a", "rg_b_x", "rg_lambda", "hg_norm", "gla_b_gate", "gla_norm"):
        w[k] = full[k][0]
    w["rg_wa_bd"] = _block_diag(full["rg_w_a"][0])
    w["rg_wx_bd"] = _block_diag(full["rg_w_x"][0])
    gla_in = full["gla_w_in"][0].astype(BF16)
    w["gla_w_in_pad"] = jnp.pad(gla_in, ((0, 0), (0, 3200 - gla_in.shape[1])))
    up = full["gla_w_gate_up"][0]
    rank = up.shape[1]
    pad = jnp.zeros((2, LANES, up.shape[2]), F32)
    w["gla_w_up_pad"] = pad.at[0, 0:rank].set(up[0]).at[1, rank:2 * rank].set(up[1])
    return w


def _finish_grads(g, rank=16, gla_in_width=3104, rg_blocks=8):
    out = {
        "norm_mix": g["norm_mix"], "norm_mlp": g["norm_mlp"], "norm_final": g["norm_final"][0],
        "mlp_w1": g["mlp_w1"], "mlp_w2": g["mlp_w2"],
        "ab_w_in": g["ab_w_in"][None], "ab_w_out": g["ab_w_out"][None],
        "rg_conv_w": g["rg_conv_w"][None], "rg_conv_b": g["rg_conv_b"],
        "rg_w_a": _block_diag_extract(g["rg_wa_bd"], rg_blocks)[None], "rg_b_a": g["rg_b_a"][None],
        "rg_w_x": _block_diag_extract(g["rg_wx_bd"], rg_blocks)[None], "rg_b_x": g["rg_b_x"][None],
        "rg_lambda": g["rg_lambda"][None], "hg_lb_logits": g["hg_lb_logits"], "hg_norm": g["hg_norm"],
        "gla_w_in": g["gla_w_in_pad"][None, :, :gla_in_width], "gla_w_out": g["gla_w_out"][None],
        "gla_w_gate_up": jnp.stack([g["gla_w_up_pad"][0, 0:rank], g["gla_w_up_pad"][1, rank:2 * rank]])[None],
        "gla_b_gate": g["gla_b_gate"][None], "gla_norm": g["gla_norm"],
    }
    return out


BIG = (("mlp_w1", 2), ("mlp_w2", 1), ("ab_w_in", 2), ("ab_w_out", 1), ("gla_w_in", 2), ("gla_w_out", 1))
SMALL_SHARDED = ("rg_conv_w", "rg_b_a", "rg_b_x", "rg_lambda", "gla_w_gate_up", "gla_b_gate", "gla_norm")
SMALL_REPLICATED = ("norm_mix", "norm_mlp", "norm_final", "rg_conv_b", "rg_w_a", "rg_w_x", "hg_lb_logits", "hg_norm")
WEIGHTS = ("norm_mix", "norm_mlp", "norm_final", "mlp_w1", "mlp_w2", "ab_w_in", "ab_w_out", "rg_conv_w", "rg_conv_b",
           "rg_w_a", "rg_b_a", "rg_w_x", "rg_b_x", "rg_lambda", "hg_lb_logits", "hg_norm", "gla_w_in", "gla_w_out",
           "gla_w_gate_up", "gla_b_gate", "gla_norm")
ROW_ALIGN = 16


def _pack(arrays, lead=0):
    head = arrays[0].shape[:lead]
    flat = jnp.concatenate([a.reshape(head + (-1,)) for a in arrays], axis=lead)
    n = flat.shape[-1]
    quantum = LANES * ROW_ALIGN
    padded = -(-n // quantum) * quantum
    if padded != n:
        flat = jnp.pad(flat, [(0, 0)] * lead + [(0, padded - n)])
    return flat.reshape(head + (padded // LANES, LANES))


def _unpack(buf, shapes, lead=0):
    head = buf.shape[:lead]
    flat = buf.reshape(head + (-1,))
    out, off = [], 0
    for s in shapes:
        n = 1
        for v in s:
            n *= v
        out.append(lax.slice_in_dim(flat, off, off + n, axis=lead).reshape(head + tuple(s)))
        off += n
    return out


def _join_chips(gathered, axis):
    t = jnp.moveaxis(gathered, 0, axis)
    return t.reshape(t.shape[:axis] + (t.shape[axis] * t.shape[axis + 1],) + t.shape[axis + 2:])


def _split_chips(full, axis):
    s = full.shape
    t = full.reshape(s[:axis] + (N_CHIPS, s[axis] // N_CHIPS) + s[axis + 1:])
    return jnp.moveaxis(t, axis, 0)


_ANY = pl.BlockSpec(memory_space=pl.ANY)


def _place():
    return lax.axis_index("x"), lax.axis_index("y"), lax.axis_index("c")


def _gather_chips(name, shard):
    def body(in_ref, out_ref, send_sems, recv_sems, local_sem):
        x, y, c = _place()
        me = 2 * x + y
        peers = [(1 - x, y), (x, 1 - y), (1 - x, 1 - y)]

        def copy(j, block):
            px, py = peers[j]
            return pltpu.make_async_remote_copy(
                src_ref=in_ref, dst_ref=out_ref.at[block], send_sem=send_sems.at[j], recv_sem=recv_sems.at[j],
                device_id=(px, py, c), device_id_type=MESH)

        local = pltpu.make_async_copy(in_ref, out_ref.at[me], local_sem)
        local.start()
        sends = [copy(j, me) for j in range(3)]
        for cp in sends:
            cp.start()
        for j, (px, py) in enumerate(peers):
            copy(j, 2 * px + py).wait_recv()
        for cp in sends:
            cp.wait_send()
        local.wait()

    return pl.pallas_call(
        body, name=name, in_specs=[_ANY], out_specs=_ANY,
        out_shape=jax.ShapeDtypeStruct((N_CHIPS,) + shard.shape, shard.dtype),
        scratch_shapes=[pltpu.SemaphoreType.DMA((3,)), pltpu.SemaphoreType.DMA((3,)), pltpu.SemaphoreType.DMA],
    )(shard)


def _pair_exchange(name, g):
    n, _, rh, lanes = g.shape

    def body(g_ref, out_ref, send_sem, recv_sem):
        x, y, c = _place()
        cp = pltpu.make_async_remote_copy(
            src_ref=g_ref.at[:, 1 - c], dst_ref=out_ref, send_sem=send_sem, recv_sem=recv_sem,
            device_id=(x, y, 1 - c), device_id_type=MESH)
        cp.start()
        cp.wait()

    return pl.pallas_call(
        body, name=name, in_specs=[_ANY], out_specs=_ANY,
        out_shape=jax.ShapeDtypeStruct((n, rh, lanes), g.dtype),
        scratch_shapes=[pltpu.SemaphoreType.DMA, pltpu.SemaphoreType.DMA],
    )(g)


def _pair_add(name, g, got, c, tm):
    n, _, rh, lanes = g.shape

    def body(c_ref, g_ref, got_ref, o_ref):
        o_ref[...] = g_ref[...] + got_ref[...]

    grid_spec = pltpu.PrefetchScalarGridSpec(
        num_scalar_prefetch=1, grid=(n, rh // tm),
        in_specs=[pl.BlockSpec((None, None, tm, lanes), lambda s, i, c_ref: (s, c_ref[0], i, 0)),
                  pl.BlockSpec((None, tm, lanes), lambda s, i, c_ref: (s, i, 0))],
        out_specs=pl.BlockSpec((None, tm, lanes), lambda s, i, c_ref: (s, i, 0)))
    return pl.pallas_call(
        body, name=name, grid_spec=grid_spec, out_shape=jax.ShapeDtypeStruct((n, rh, lanes), F32),
        compiler_params=_params(("parallel", "parallel")),
    )(c.reshape(1).astype(jnp.int32), g, got)


def _chip_scatter(name, p):
    def body(p_ref, out_ref, send_sems, recv_sems, local_sem):
        x, y, c = _place()
        me = 2 * x + y
        peers = [(1 - x, y), (x, 1 - y), (1 - x, 1 - y)]

        def copy(j, src_block, dst_block):
            px, py = peers[j]
            return pltpu.make_async_remote_copy(
                src_ref=p_ref.at[src_block], dst_ref=out_ref.at[dst_block], send_sem=send_sems.at[j],
                recv_sem=recv_sems.at[j], device_id=(px, py, c), device_id_type=MESH)

        local = pltpu.make_async_copy(p_ref.at[me], out_ref.at[me], local_sem)
        local.start()
        sends = [copy(j, 2 * px + py, me) for j, (px, py) in enumerate(peers)]
        for cp in sends:
            cp.start()
        for j, (px, py) in enumerate(peers):
            copy(j, me, 2 * px + py).wait_recv()
        for cp in sends:
            cp.wait_send()
        local.wait()

    return pl.pallas_call(
        body, name=name, in_specs=[_ANY], out_specs=_ANY, out_shape=jax.ShapeDtypeStruct(p.shape, p.dtype),
        scratch_shapes=[pltpu.SemaphoreType.DMA((3,)), pltpu.SemaphoreType.DMA((3,)), pltpu.SemaphoreType.DMA],
    )(p)


def _pair_gather(name, r):
    def body(r_ref, out_ref, send_sem, recv_sem, local_sem):
        x, y, c = _place()
        local = pltpu.make_async_copy(r_ref, out_ref.at[c], local_sem)
        local.start()
        send = pltpu.make_async_remote_copy(
            src_ref=r_ref, dst_ref=out_ref.at[c], send_sem=send_sem, recv_sem=recv_sem,
            device_id=(x, y, 1 - c), device_id_type=MESH)
        send.start()
        pltpu.make_async_remote_copy(
            src_ref=r_ref, dst_ref=out_ref.at[1 - c], send_sem=send_sem, recv_sem=recv_sem,
            device_id=(x, y, 1 - c), device_id_type=MESH).wait_recv()
        send.wait_send()
        local.wait()

    return pl.pallas_call(
        body, name=name, in_specs=[_ANY], out_specs=_ANY, out_shape=jax.ShapeDtypeStruct((2,) + r.shape, r.dtype),
        scratch_shapes=[pltpu.SemaphoreType.DMA, pltpu.SemaphoreType.DMA, pltpu.SemaphoreType.DMA],
    )(r)


def _gather_all(name, s):
    def body(in_ref, out_ref, send_sems, recv_sems, local_sem):
        x, y, c = _place()
        me = 4 * x + 2 * y + c
        peers = []
        for mask in range(1, N_DEV):
            fx, fy, fc = (mask >> 2) & 1, (mask >> 1) & 1, mask & 1
            peers.append((jnp.where(fx, 1 - x, x), jnp.where(fy, 1 - y, y), jnp.where(fc, 1 - c, c)))

        def copy(j, block):
            return pltpu.make_async_remote_copy(
                src_ref=in_ref, dst_ref=out_ref.at[block], send_sem=send_sems.at[j], recv_sem=recv_sems.at[j],
                device_id=peers[j], device_id_type=MESH)

        local = pltpu.make_async_copy(in_ref, out_ref.at[me], local_sem)
        local.start()
        sends = [copy(j, me) for j in range(N_DEV - 1)]
        for cp in sends:
            cp.start()
        for j, (px, py, pc) in enumerate(peers):
            copy(j, 4 * px + 2 * py + pc).wait_recv()
        for cp in sends:
            cp.wait_send()
        local.wait()

    return pl.pallas_call(
        body, name=name, in_specs=[_ANY], out_specs=_ANY,
        out_shape=jax.ShapeDtypeStruct((N_DEV,) + s.shape, s.dtype),
        scratch_shapes=[pltpu.SemaphoreType.DMA((N_DEV - 1,)), pltpu.SemaphoreType.DMA((N_DEV - 1,)),
                        pltpu.SemaphoreType.DMA],
    )(s)


def _sum_blocks(name, stacked, tm):
    n, r, lanes = stacked.shape

    def body(in_ref, o_ref):
        acc = in_ref[0]
        for j in range(1, n):
            acc = acc + in_ref[j]
        o_ref[...] = acc

    return pl.pallas_call(
        body, name=name, grid=(r // tm,), in_specs=[pl.BlockSpec((n, tm, lanes), lambda i: (0, i, 0))],
        out_specs=pl.BlockSpec((tm, lanes), lambda i: (i, 0)), out_shape=jax.ShapeDtypeStruct((r, lanes), F32),
        compiler_params=_params(("parallel",)),
    )(stacked)


def _row_tile(rows, pref, align):
    best = None
    for t in range(align, min(rows, pref) + 1, align):
        if rows % t == 0:
            best = t
    assert best is not None, (rows, pref, align)
    return best


def _adam(name, w, g, m, v):
    rows = w.shape[0]
    tm = _row_tile(rows, 4096, 8)
    args = [(t, LANES, 0) for t in (w, g, m, v)]
    return _rowcall(name, _adam_fn, args, [], [(LANES, F32)] * 3, tm=tm)


def kernel(x, norm_mix, norm_mlp, norm_final, mlp_w1, mlp_w2, ab_w_in, ab_w_out, rg_conv_w, rg_conv_b, rg_w_a, rg_b_a, rg_w_x, rg_b_x, rg_lambda, hg_lb_logits, hg_norm, gla_w_in, gla_w_out, gla_w_gate_up, gla_b_gate, gla_norm, loss_target, m_norm_mix, m_norm_mlp, m_norm_final, m_mlp_w1, m_mlp_w2, m_ab_w_in, m_ab_w_out, m_rg_conv_w, m_rg_conv_b, m_rg_w_a, m_rg_b_a, m_rg_w_x, m_rg_b_x, m_rg_lambda, m_hg_lb_logits, m_hg_norm, m_gla_w_in, m_gla_w_out, m_gla_w_gate_up, m_gla_b_gate, m_gla_norm, v_norm_mix, v_norm_mlp, v_norm_final, v_mlp_w1, v_mlp_w2, v_ab_w_in, v_ab_w_out, v_rg_conv_w, v_rg_conv_b, v_rg_w_a, v_rg_b_a, v_rg_w_x, v_rg_b_x, v_rg_lambda, v_hg_lb_logits, v_hg_norm, v_gla_w_in, v_gla_w_out, v_gla_w_gate_up, v_gla_b_gate, v_gla_norm):
    w = dict(norm_mix=norm_mix, norm_mlp=norm_mlp, norm_final=norm_final, mlp_w1=mlp_w1, mlp_w2=mlp_w2, ab_w_in=ab_w_in, ab_w_out=ab_w_out, rg_conv_w=rg_conv_w, rg_conv_b=rg_conv_b, rg_w_a=rg_w_a, rg_b_a=rg_b_a, rg_w_x=rg_w_x, rg_b_x=rg_b_x, rg_lambda=rg_lambda, hg_lb_logits=hg_lb_logits, hg_norm=hg_norm, gla_w_in=gla_w_in, gla_w_out=gla_w_out, gla_w_gate_up=gla_w_gate_up, gla_b_gate=gla_b_gate, gla_norm=gla_norm)
    m = dict(norm_mix=m_norm_mix, norm_mlp=m_norm_mlp, norm_final=m_norm_final, mlp_w1=m_mlp_w1, mlp_w2=m_mlp_w2, ab_w_in=m_ab_w_in, ab_w_out=m_ab_w_out, rg_conv_w=m_rg_conv_w, rg_conv_b=m_rg_conv_b, rg_w_a=m_rg_w_a, rg_b_a=m_rg_b_a, rg_w_x=m_rg_w_x, rg_b_x=m_rg_b_x, rg_lambda=m_rg_lambda, hg_lb_logits=m_hg_lb_logits, hg_norm=m_hg_norm, gla_w_in=m_gla_w_in, gla_w_out=m_gla_w_out, gla_w_gate_up=m_gla_w_gate_up, gla_b_gate=m_gla_b_gate, gla_norm=m_gla_norm)
    v = dict(norm_mix=v_norm_mix, norm_mlp=v_norm_mlp, norm_final=v_norm_final, mlp_w1=v_mlp_w1, mlp_w2=v_mlp_w2, ab_w_in=v_ab_w_in, ab_w_out=v_ab_w_out, rg_conv_w=v_rg_conv_w, rg_conv_b=v_rg_conv_b, rg_w_a=v_rg_w_a, rg_b_a=v_rg_b_a, rg_w_x=v_rg_w_x, rg_b_x=v_rg_b_x, rg_lambda=v_rg_lambda, hg_lb_logits=v_hg_lb_logits, hg_norm=v_hg_norm, gla_w_in=v_gla_w_in, gla_w_out=v_gla_w_out, gla_w_gate_up=v_gla_w_gate_up, gla_b_gate=v_gla_b_gate, gla_norm=v_gla_norm)
    chip = 2 * lax.axis_index("x") + lax.axis_index("y")
    core = lax.axis_index("c")
    big_names = [n for n, _ in BIG]
    big_shapes = [w[n].shape for n in big_names]
    sharded_shapes = [w[n].shape for n in SMALL_SHARDED]

    w_big = _pack([w[n] for n in big_names])
    cast_tm = _row_tile(w_big.shape[0], 4096, ROW_ALIGN)
    w_big16 = _rowcall("cast_weights", lambda t: (t,), [(w_big, LANES, 0)], [], [(LANES, BF16)], tm=cast_tm)[0]
    big_all = _unpack(_gather_chips("gather_weights", w_big16), big_shapes, lead=1)
    small_all = _unpack(_gather_chips("gather_vectors", _pack([w[n] for n in SMALL_SHARDED])), sharded_shapes, lead=1)
    full = {n: w[n] for n in SMALL_REPLICATED}
    for (n, axis), t in zip(BIG, big_all):
        full[n] = _join_chips(t, axis)
    for n, t in zip(SMALL_SHARDED, small_all):
        full[n] = _join_chips(t, t.ndim - 2)

    loss_part, grad_x, g_kernel = _local_step(x[0], loss_target[0], _prepare_weights(full))
    g_full = _finish_grads(g_kernel)
    loss = lax.psum(loss_part[0, 0], ("x", "y", "c"))

    g_big = _pack([_split_chips(g_full[n], axis) for n, axis in BIG], lead=1)
    rows = g_big.shape[1]
    half = rows // 2
    add_tm = _row_tile(half, 4096, 8)
    g_halves = g_big.reshape(N_CHIPS, 2, half, LANES)
    from_sibling = _pair_exchange("reduce_pair", g_halves)
    chip_part = _pair_add("reduce_pair_add", g_halves, from_sibling, core, add_tm)
    from_chips = _chip_scatter("reduce_chips", chip_part)
    mine = _sum_blocks("reduce_chips_add", from_chips, add_tm)
    g_big_red = _pair_gather("reduce_share", mine).reshape(rows, LANES)

    small_names = SMALL_REPLICATED + SMALL_SHARDED
    g_small = _pack([g_full[n] for n in small_names])
    g_small_all = _gather_all("reduce_small", g_small)
    g_small_red = _sum_blocks("reduce_small_add", g_small_all, g_small.shape[0])
    g_small_full = dict(zip(small_names, _unpack(g_small_red, [g_full[n].shape for n in small_names])))
    grads = {n: g_small_full[n] for n in SMALL_REPLICATED}
    for n in SMALL_SHARDED:
        width = w[n].shape[-1]
        grads[n] = lax.dynamic_slice_in_dim(g_small_full[n], chip * width, width, axis=g_small_full[n].ndim - 1)
    for n, t in zip(big_names, _unpack(g_big_red, big_shapes)):
        grads[n] = t

    d_big, m_big, v_big = _adam("adam_big", w_big, g_big_red, _pack([m[n] for n in big_names]), _pack([v[n] for n in big_names]))
    delta, new_m, new_v = {}, {}, {}
    for dst, buf in ((delta, d_big), (new_m, m_big), (new_v, v_big)):
        dst.update(zip(big_names, _unpack(buf, big_shapes)))
    small_shapes = [w[n].shape for n in small_names]
    packs = [_pack([src[n] for n in small_names]) for src in (w, grads, m, v)]
    d_small, m_small, v_small = _adam("adam_small", *packs)
    for dst, buf in ((delta, d_small), (new_m, m_small), (new_v, v_small)):
        dst.update(zip(small_names, _unpack(buf, small_shapes)))

    return (loss, grad_x[None], *[grads[n] for n in WEIGHTS], *[delta[n] for n in WEIGHTS],
            *[new_m[n] for n in WEIGHTS], *[new_v[n] for n in WEIGHTS])
```

```python
import functools

import jax
import jax.numpy as jnp
from jax import lax
from jax.experimental import pallas as pl
from jax.experimental.pallas import tpu as pltpu

F32 = jnp.float32
BF16 = jnp.bfloat16
MESH = pl.DeviceIdType.MESH

LANES = 128
CHUNK = 64
EPS = 1e-6
RG_C = 8.0
N_CHIPS = 4
N_DEV = 8
VMEM_LIMIT = 56 * 1024 * 1024

ADAM_LR = 0.001
ADAM_B1 = 0.9
ADAM_B2 = 0.999
ADAM_EPS = 1e-08
ADAM_WD = 0.01
ADAM_STEP = 10


def _raw_dot(a, b, ca, cb):
    return lax.dot_general(a.astype(BF16), b.astype(BF16), (((ca,), (cb,)), ((), ())),
                           preferred_element_type=F32)


def _raw_nn(a, b):
    return _raw_dot(a, b, 1, 0)


def _raw_nt(a, b):
    return _raw_dot(a, b, 1, 1)


def _raw_tn(a, b):
    return _raw_dot(a, b, 0, 0)


@jax.custom_vjp
def _dot_nn(a, b):
    return _raw_nn(a, b)


def _dot_nn_fwd(a, b):
    return _raw_nn(a, b), (a, b)


def _dot_nn_bwd(res, g):
    a, b = res
    return _raw_nt(g, b), _raw_tn(a, g)


_dot_nn.defvjp(_dot_nn_fwd, _dot_nn_bwd)


@jax.custom_vjp
def _dot_nt(a, b):
    return _raw_nt(a, b)


def _dot_nt_fwd(a, b):
    return _raw_nt(a, b), (a, b)


def _dot_nt_bwd(res, g):
    a, b = res
    return _raw_nn(g, b), _raw_tn(g, a)


_dot_nt.defvjp(_dot_nt_fwd, _dot_nt_bwd)


@jax.custom_vjp
def _dot_tn(a, b):
    return _raw_tn(a, b)


def _dot_tn_fwd(a, b):
    return _raw_tn(a, b), (a, b)


def _dot_tn_bwd(res, g):
    a, b = res
    return _raw_nt(b, g), _raw_nn(a, g)


_dot_tn.defvjp(_dot_tn_fwd, _dot_tn_bwd)


def _tile(n, pref):
    if n <= pref:
        return n
    t = (pref // LANES) * LANES
    while t > LANES and n % t:
        t -= LANES
    assert n % t == 0, (n, pref)
    return t


def _params(sem):
    return pltpu.CompilerParams(dimension_semantics=sem, vmem_limit_bytes=VMEM_LIMIT)


def _rowcall(name, fn, rows, pars, row_outs, par_outs=(), tm=256):
    n_rows = rows[0][0].shape[0]
    tm = min(tm, n_rows)
    assert n_rows % tm == 0
    n_r, n_p, n_ro = len(rows), len(pars), len(row_outs)

    def body(*refs):
        vals = [r[...].astype(F32) for r in refs[:n_r + n_p]]
        outs = fn(*vals)
        o_refs = refs[n_r + n_p:n_r + n_p + n_ro]
        po_refs = refs[n_r + n_p + n_ro:]
        for o_ref, val in zip(o_refs, outs[:n_ro]):
            o_ref[...] = val.astype(o_ref.dtype)
        first = pl.program_id(0) == 0
        for po_ref, val in zip(po_refs, outs[n_ro:]):
            @pl.when(first)
            def _():
                po_ref[...] = val

            @pl.when(jnp.logical_not(first))
            def _():
                po_ref[...] += val

    def const_map(nd):
        return lambda i: (0,) * nd

    in_specs = [pl.BlockSpec((tm, w), functools.partial(lambda i, cb: (i, cb), cb=cb)) for _, w, cb in rows]
    in_specs += [pl.BlockSpec(p.shape, const_map(p.ndim)) for p in pars]
    out_specs = [pl.BlockSpec((tm, w), lambda i: (i, 0)) for w, _ in row_outs]
    out_specs += [pl.BlockSpec(tuple(s), const_map(len(s))) for s in par_outs]
    out_shape = [jax.ShapeDtypeStruct((n_rows, w), dt) for w, dt in row_outs]
    out_shape += [jax.ShapeDtypeStruct(tuple(s), F32) for s in par_outs]
    return pl.pallas_call(
        body, name=name, grid=(n_rows // tm,), in_specs=in_specs, out_specs=out_specs, out_shape=out_shape,
        compiler_params=_params(("arbitrary",) if par_outs else ("parallel",)),
    )(*[r[0] for r in rows], *pars)


def _vjp_of(fn, n_prim, n_out, n_par, n_pass=0):
    def bwd(*args):
        prim = args[:n_prim]
        cts = args[n_prim:n_prim + n_out]
        passes = args[n_prim + n_out:n_prim + n_out + 2 * n_pass]
        pars = args[n_prim + n_out + 2 * n_pass:]
        _, vjp = jax.vjp(fn, *prim, *pars)
        grads = vjp(tuple(cts))
        sums = tuple(passes[2 * i] + passes[2 * i + 1] for i in range(n_pass))
        return tuple(grads[:n_prim]) + sums + tuple(grads[n_prim:])
    return bwd


def _mm(name, a, b, mode="nn", extras=(), epi=None, out_dtypes=(F32,), a_pro=None, tm=1024, tn=1024, tk=1024):
    if mode == "nn":
        (m, k), n = a.shape, b.shape[1]
    elif mode == "nt":
        (m, k), n = a.shape, b.shape[0]
    else:
        (k, m), n = a.shape, b.shape[1]
    tm, tn, tk = _tile(m, tm), _tile(n, tn), _tile(k, tk)
    nk = k // tk
    raw = {"nn": _raw_nn, "nt": _raw_nt, "tn": _raw_tn}[mode]
    n_e, n_o = len(extras), len(out_dtypes)
    if epi is None:
        epi = lambda acc: (acc,)

    def body(a_ref, b_ref, *rest):
        e_refs, o_refs, acc = rest[:n_e], rest[n_e:n_e + n_o], rest[-1]
        kk = pl.program_id(2)

        @pl.when(kk == 0)
        def _():
            acc[...] = jnp.zeros_like(acc)

        a_tile = a_ref[...] if a_pro is None else a_pro(a_ref[...].astype(F32))
        acc[...] += raw(a_tile, b_ref[...])

        @pl.when(kk == nk - 1)
        def _():
            res = epi(acc[...], *[e[...].astype(F32) for e in e_refs])
            for o_ref, r in zip(o_refs, res):
                o_ref[...] = r.astype(o_ref.dtype)

    a_spec = pl.BlockSpec((tk, tm), lambda i, j, kk: (kk, i)) if mode == "tn" else pl.BlockSpec((tm, tk), lambda i, j, kk: (i, kk))
    b_spec = pl.BlockSpec((tn, tk), lambda i, j, kk: (j, kk)) if mode == "nt" else pl.BlockSpec((tk, tn), lambda i, j, kk: (kk, j))
    mn_spec = pl.BlockSpec((tm, tn), lambda i, j, kk: (i, j))
    outs = pl.pallas_call(
        body, name=name, grid=(m // tm, n // tn, nk),
        in_specs=[a_spec, b_spec] + [mn_spec] * n_e, out_specs=[mn_spec] * n_o,
        out_shape=[jax.ShapeDtypeStruct((m, n), dt) for dt in out_dtypes],
        scratch_shapes=[pltpu.VMEM((tm, tn), F32)],
        compiler_params=_params(("parallel", "parallel", "arbitrary")),
    )(a, b, *extras)
    return outs[0] if n_o == 1 else outs


def _sigmoid(x):
    return jax.nn.sigmoid(x)


def _silu(x):
    return x * _sigmoid(x)


def _softplus(x):
    return jnp.maximum(x, 0.0) + jnp.log1p(jnp.exp(-jnp.abs(x)))


def _rmsnorm_fn(x, gain):
    return (x * lax.rsqrt(jnp.mean(x * x, axis=-1, keepdims=True) + EPS) * gain,)


def _head_norm(o, gain, n_heads):
    w = o.shape[-1] // n_heads
    parts = []
    for h in range(n_heads):
        oh = o[:, h * w:(h + 1) * w]
        parts.append(oh * lax.rsqrt(jnp.mean(oh * oh, axis=-1, keepdims=True) + EPS))
    return jnp.concatenate(parts, axis=-1) * gain


@jax.custom_jvp
def _neg_expm1(x):
    u = jnp.exp(x)
    is_one = u == 1.0
    return jnp.where(is_one, -x, (1.0 - u) * x / jnp.log(jnp.where(is_one, 2.0, u)))


@_neg_expm1.defjvp
def _neg_expm1_jvp(primals, tangents):
    (x,), (t,) = primals, tangents
    return _neg_expm1(x), -jnp.exp(x) * t


def _rg_gates_fn(xc, wa, wx, ba, bx, lam):
    outs = []
    for d in range(2):
        r = _sigmoid(_dot_nn(xc, wa[d]) + ba[d:d + 1])
        i = _sigmoid(_dot_nn(xc, wx[d]) + bx[d:d + 1])
        log_a = -RG_C * r * _softplus(-lam[d:d + 1])
        outs.append(jnp.exp(log_a))
        outs.append(jnp.sqrt(_neg_expm1(2.0 * log_a)) * (i * xc))
    return tuple(outs)


def _hg_pre_fn(q, f_f, f_b, logits):
    mx = jnp.maximum(logits[0:1], logits[1:2])
    e0 = jnp.exp(logits[0:1] - mx)
    e1 = jnp.exp(logits[1:2] - mx)
    lb = e0 / (e0 + e1)
    outs = [_silu(q)]
    for f in (f_f, f_b):
        outs.append((1.0 - lb) * _sigmoid(-f))
        outs.append(jnp.log(lb + (1.0 - lb) * _sigmoid(f)))
    return tuple(outs)


def _post0_fn(hs, ga, o, g, gain):
    ya = hs * jax.nn.gelu(ga, approximate=True)
    yb = _head_norm(o, gain, 4) * _silu(g)
    return (jnp.concatenate([ya, yb], axis=-1),)


def _post0_fwd_fn(h_f, h_b, ga, o_f, o_b, g, gain):
    return _post0_fn(h_f + h_b, ga, o_f + o_b, g, gain)


def _post0_bwd_fn(h_f, h_b, ga, o_f, o_b, g, dmix, gain):
    _, vjp = jax.vjp(_post0_fn, h_f + h_b, ga, o_f + o_b, g, gain)
    return vjp((dmix,))


def _gla_pre_fn(q, lr, w_up, b_gate):
    outs = [q * (128.0 ** -0.5)]
    for d in range(2):
        z = _dot_nn(lr, w_up[d]) + b_gate[d:d + 1]
        outs.append(-_softplus(-z) * (1.0 / 16.0))
    return tuple(outs)


def _gla_post_fn(o, r, gain):
    return (_head_norm(o, gain, 4) * _silu(r),)


def _gla_post_fwd_fn(o_f, o_b, r, gain):
    return _gla_post_fn(o_f + o_b, r, gain)


def _gla_post_bwd_fn(o_f, o_b, r, dmix, gain):
    _, vjp = jax.vjp(_gla_post_fn, o_f + o_b, r, gain)
    return vjp((dmix,))


def _relu2_bwd_epi(acc, hid):
    return (acc * 2.0 * jnp.maximum(hid, 0.0),)


def _relu2(x):
    r = jnp.maximum(x, 0.0)
    return r * r


def _add_epi(acc, res):
    return (acc + res,)


def _loss_head_fn(h, target, gain):
    def f(h, gain):
        y = _rmsnorm_fn(h, gain)[0]
        err = y - target
        return 0.5 * jnp.sum(jnp.mean(err * err, axis=-1, keepdims=True))
    loss, (dh, dgain) = jax.value_and_grad(f, argnums=(0, 1))(h, gain)
    return dh, jnp.full((1, LANES), loss, F32), dgain


def _adam_fn(w, g, m, v):
    m2 = ADAM_B1 * m + (1.0 - ADAM_B1) * g
    v2 = ADAM_B2 * v + (1.0 - ADAM_B2) * (g * g)
    m_hat = m2 / (1.0 - ADAM_B1 ** ADAM_STEP)
    v_hat = v2 / (1.0 - ADAM_B2 ** ADAM_STEP)
    delta = -ADAM_LR * (m_hat / (jnp.sqrt(v_hat) + ADAM_EPS) + ADAM_WD * w)
    return delta, m2, v2


def _shifted(x, t_idx, off):
    n = x.shape[0]
    rolled = pltpu.roll(x, (-off) % n, 0)
    valid = (t_idx + off >= 0) & (t_idx + off < n)
    return jnp.where(valid, rolled, 0.0)


def _conv_fwd(name, src, colblock, w, b):
    n_rows, width = src.shape[0], w.shape[1]

    def body(x_ref, w_ref, b_ref, o_ref):
        x = x_ref[...]
        t_idx = lax.broadcasted_iota(jnp.int32, x.shape, 0)
        acc = b_ref[...] + w_ref[2:3, :] * x
        acc += w_ref[0:1, :] * _shifted(x, t_idx, -2)
        acc += w_ref[1:2, :] * _shifted(x, t_idx, -1)
        acc += w_ref[3:4, :] * _shifted(x, t_idx, 1)
        o_ref[...] = acc

    nb = width // LANES
    return pl.pallas_call(
        body, name=name, grid=(nb,),
        in_specs=[pl.BlockSpec((n_rows, LANES), lambda j: (0, colblock * nb + j)),
                  pl.BlockSpec((4, LANES), lambda j: (0, j)), pl.BlockSpec((1, LANES), lambda j: (0, j))],
        out_specs=pl.BlockSpec((n_rows, LANES), lambda j: (0, j)),
        out_shape=jax.ShapeDtypeStruct((n_rows, width), F32),
        compiler_params=_params(("parallel",)),
    )(src, w, b)


def _conv_bwd(name, src, colblock, w, d):
    n_rows, width = src.shape[0], w.shape[1]

    def body(x_ref, w_ref, d_ref, dx_ref, dw_ref, db_ref):
        x = x_ref[...]
        g = d_ref[...]
        t_idx = lax.broadcasted_iota(jnp.int32, x.shape, 0)
        dx = w_ref[2:3, :] * g
        dx += w_ref[0:1, :] * _shifted(g, t_idx, 2)
        dx += w_ref[1:2, :] * _shifted(g, t_idx, 1)
        dx += w_ref[3:4, :] * _shifted(g, t_idx, -1)
        dx_ref[...] = dx.astype(dx_ref.dtype)
        dw_ref[0:1, :] = jnp.sum(g * _shifted(x, t_idx, -2), axis=0, keepdims=True)
        dw_ref[1:2, :] = jnp.sum(g * _shifted(x, t_idx, -1), axis=0, keepdims=True)
        dw_ref[2:3, :] = jnp.sum(g * x, axis=0, keepdims=True)
        dw_ref[3:4, :] = jnp.sum(g * _shifted(x, t_idx, 1), axis=0, keepdims=True)
        db_ref[...] = jnp.sum(g, axis=0, keepdims=True)

    nb = width // LANES
    return pl.pallas_call(
        body, name=name, grid=(nb,),
        in_specs=[pl.BlockSpec((n_rows, LANES), lambda j: (0, colblock * nb + j)),
                  pl.BlockSpec((4, LANES), lambda j: (0, j)),
                  pl.BlockSpec((n_rows, LANES), lambda j: (0, j))],
        out_specs=[pl.BlockSpec((n_rows, LANES), lambda j: (0, j)), pl.BlockSpec((4, LANES), lambda j: (0, j)),
                   pl.BlockSpec((1, LANES), lambda j: (0, j))],
        out_shape=[jax.ShapeDtypeStruct((n_rows, width), BF16), jax.ShapeDtypeStruct((4, width), F32),
                   jax.ShapeDtypeStruct((1, width), F32)],
        compiler_params=_params(("parallel",)),
    )(src, w, d)


_WHOLE = pl.BlockSpec(memory_space=pltpu.VMEM)
SCAN_UNROLL = 8


def _scan_fwd(name, a, u, reverse):
    n_rows, width = a.shape

    def body(a_ref, u_ref, h_ref):
        def step(i, h):
            t = (n_rows - 1 - i) if reverse else i
            h = a_ref[pl.ds(t, 1), :] * h + u_ref[pl.ds(t, 1), :]
            h_ref[pl.ds(t, 1), :] = h
            return h
        lax.fori_loop(0, n_rows, step, jnp.zeros((1, width), F32), unroll=SCAN_UNROLL)

    return pl.pallas_call(
        body, name=name, in_specs=[_WHOLE, _WHOLE], out_specs=_WHOLE,
        out_shape=jax.ShapeDtypeStruct((n_rows, width), F32),
        compiler_params=pltpu.CompilerParams(vmem_limit_bytes=VMEM_LIMIT),
    )(a, u)


def _scan_bwd(name, a, h, dh, reverse):
    n_rows, width = a.shape

    def body(a_ref, h_ref, dh_ref, du_ref, da_ref):
        def step(i, carry):
            t = i if reverse else (n_rows - 1 - i)
            g = dh_ref[pl.ds(t, 1), :] + carry
            du_ref[pl.ds(t, 1), :] = g
            tp = t + 1 if reverse else t - 1
            valid = (tp >= 0) & (tp < n_rows)
            h_prev = h_ref[pl.ds(jnp.clip(tp, 0, n_rows - 1), 1), :]
            da_ref[pl.ds(t, 1), :] = jnp.where(valid, g * h_prev, 0.0)
            return a_ref[pl.ds(t, 1), :] * g
        lax.fori_loop(0, n_rows, step, jnp.zeros((1, width), F32), unroll=SCAN_UNROLL)

    return pl.pallas_call(
        body, name=name, in_specs=[_WHOLE] * 3, out_specs=[_WHOLE] * 2,
        out_shape=[jax.ShapeDtypeStruct((n_rows, width), F32)] * 2,
        compiler_params=pltpu.CompilerParams(vmem_limit_bytes=VMEM_LIMIT),
    )(a, h, dh)


def _tri_mask(c, reverse):
    row = lax.broadcasted_iota(jnp.int32, (c, c), 0)
    col = lax.broadcasted_iota(jnp.int32, (c, c), 1)
    return (col >= row) if reverse else (col <= row)


def _cumsum_rows(x, reverse):
    tri = _tri_mask(x.shape[0], reverse).astype(BF16)
    hi = x.astype(BF16)
    rest = x - hi.astype(F32)
    mid = rest.astype(BF16)
    lo = (rest - mid.astype(F32)).astype(BF16)
    return _raw_nn(tri, hi) + _raw_nn(tri, mid) + _raw_nn(tri, lo)


@functools.partial(jax.custom_vjp, nondiff_argnums=(1,))
def _cumsum(x, reverse):
    return _cumsum_rows(x, reverse)


def _cumsum_fwd(x, reverse):
    return _cumsum_rows(x, reverse), None


def _cumsum_bwd(reverse, _, g):
    return (_cumsum_rows(g, not reverse),)


_cumsum.defvjp(_cumsum_fwd, _cumsum_bwd)


def _chunks_fn(qs, ks, vs, lfs, sts, reverses):
    n, c = len(qs), qs[0].shape[0]
    every = range(n)
    tris = [_tri_mask(c, r) for r in reverses]
    cums = [_cumsum(lfs[i], reverses[i]) for i in every]
    rid = lax.broadcasted_iota(jnp.int32, cums[0].shape, 0)

    def pick(cum, r):
        return jnp.sum(jnp.where(rid == r, cum, 0.0), axis=0, keepdims=True)

    refs = [pick(cums[i], (c - 1 - c // 2) if reverses[i] else c // 2) for i in every]
    lasts = [pick(cums[i], 0 if reverses[i] else c - 1) for i in every]
    q_in = [qs[i] * jnp.exp(cums[i] - refs[i]) for i in every]
    k_in = [ks[i] * jnp.exp(refs[i] - cums[i]) for i in every]
    scores = [jnp.where(tris[i], _dot_nt(q_in[i], k_in[i]), 0.0) for i in every]
    o_intra = [_dot_nn(scores[i], vs[i]) for i in every]
    q_out = [qs[i] * jnp.exp(cums[i]) for i in every]
    o_inter = [_dot_nt(q_out[i], sts[i]) for i in every]
    k_state = [ks[i] * jnp.exp(lasts[i] - cums[i]) for i in every]
    upd = [_dot_tn(vs[i], k_state[i]) for i in every]
    st_new = [sts[i] * jnp.exp(lasts[i]) + upd[i] for i in every]
    return [o_intra[i] + o_inter[i] for i in every], st_new


def _attn_fwd(name, q, k_f, k_b, v, lf_f, lf_b, n_heads, dk, dv):
    n_rows = q[0].shape[0]
    n_chunks = n_rows // CHUNK
    wk, wv = n_heads * dk, n_heads * dv

    def spec(width, off, rev):
        return pl.BlockSpec((CHUNK, width), lambda n: ((n_chunks - 1 - n) if rev else n, off))

    def sspec(rev):
        return pl.BlockSpec((None, n_heads, dv, dk), lambda n: ((n_chunks - 1 - n) if rev else n, 0, 0, 0))

    def body(qf, kf, vf, lff, qb, kb, vb, lfb, of_ref, ob_ref, sf_ref, sb_ref, st):
        @pl.when(pl.program_id(0) == 0)
        def _():
            st[...] = jnp.zeros_like(st)

        ins = ((qf, kf, vf, lff), (qb, kb, vb, lfb))
        chains = [(d, h) for d in range(2) for h in range(n_heads)]
        ck = [slice(h * dk, (h + 1) * dk) for h in range(n_heads)]
        cv = [slice(h * dv, (h + 1) * dv) for h in range(n_heads)]
        qs = [ins[d][0][:, ck[h]] for d, h in chains]
        ks = [ins[d][1][:, ck[h]] for d, h in chains]
        vs = [ins[d][2][:, cv[h]] for d, h in chains]
        lfs = [ins[d][3][:, ck[h]] for d, h in chains]
        sts = [st[d, h] for d, h in chains]
        os_, st_new = _chunks_fn(qs, ks, vs, lfs, sts, [d == 1 for d, _ in chains])
        for i, (d, h) in enumerate(chains):
            (sf_ref, sb_ref)[d][h] = sts[i]
            (of_ref, ob_ref)[d][:, cv[h]] = os_[i]
            st[d, h] = st_new[i]

    in_specs = [spec(wk, q[1], False), spec(wk, k_f[1], False), spec(wv, v[1], False), spec(wk, lf_f[1], False),
                spec(wk, q[1], True), spec(wk, k_b[1], True), spec(wv, v[1], True), spec(wk, lf_b[1], True)]
    return pl.pallas_call(
        body, name=name, grid=(n_chunks,), in_specs=in_specs,
        out_specs=[spec(wv, 0, False), spec(wv, 0, True), sspec(False), sspec(True)],
        out_shape=[jax.ShapeDtypeStruct((n_rows, wv), F32)] * 2
        + [jax.ShapeDtypeStruct((n_chunks, n_heads, dv, dk), F32)] * 2,
        scratch_shapes=[pltpu.VMEM((2, n_heads, dv, dk), F32)],
        compiler_params=_params(("arbitrary",)),
    )(q[0], k_f[0], v[0], lf_f[0], q[0], k_b[0], v[0], lf_b[0])


def _attn_bwd(name, q, k_f, k_b, v, lf_f, lf_b, st_f, st_b, do, n_heads, dk, dv, out_dtype=F32):
    n_rows = q[0].shape[0]
    n_chunks = n_rows // CHUNK
    wk, wv = n_heads * dk, n_heads * dv

    def spec(width, off, rev):
        return pl.BlockSpec((CHUNK, width), lambda n: (n if rev else (n_chunks - 1 - n), off))

    def sspec(rev):
        return pl.BlockSpec((None, n_heads, dv, dk), lambda n: (n if rev else (n_chunks - 1 - n), 0, 0, 0))

    def body(qf, kf, vf, lff, sf, dof, qb, kb, vb, lfb, sb, dob,
             dqf, dkf, dvf, dlff, dqb, dkb, dvb, dlfb, dst):
        @pl.when(pl.program_id(0) == 0)
        def _():
            dst[...] = jnp.zeros_like(dst)

        ins = ((qf, kf, vf, lff, sf, dof), (qb, kb, vb, lfb, sb, dob))
        outs = ((dqf, dkf, dvf, dlff), (dqb, dkb, dvb, dlfb))
        chains = [(d, h) for d in range(2) for h in range(n_heads)]
        ck = [slice(h * dk, (h + 1) * dk) for h in range(n_heads)]
        cv = [slice(h * dv, (h + 1) * dv) for h in range(n_heads)]
        qs = [ins[d][0][:, ck[h]] for d, h in chains]
        ks = [ins[d][1][:, ck[h]] for d, h in chains]
        vs = [ins[d][2][:, cv[h]] for d, h in chains]
        lfs = [ins[d][3][:, ck[h]] for d, h in chains]
        sts = [ins[d][4][h] for d, h in chains]
        dos = [ins[d][5][:, cv[h]] for d, h in chains]
        dsts = [dst[d, h] for d, h in chains]
        fn = functools.partial(_chunks_fn, reverses=[d == 1 for d, _ in chains])
        _, vjp = jax.vjp(fn, qs, ks, vs, lfs, sts)
        dqs, dks, dvs, dlfs, dst_prev = vjp((dos, dsts))
        for i, (d, h) in enumerate(chains):
            dq_r, dk_r, dv_r, dlf_r = outs[d]
            dq_r[:, ck[h]] = dqs[i].astype(dq_r.dtype)
            dk_r[:, ck[h]] = dks[i].astype(dk_r.dtype)
            dv_r[:, cv[h]] = dvs[i].astype(dv_r.dtype)
            dlf_r[:, ck[h]] = dlfs[i].astype(dlf_r.dtype)
            dst[d, h] = dst_prev[i]

    def dir_specs(kk, lf, rev):
        return [spec(wk, q[1], rev), spec(wk, kk[1], rev), spec(wv, v[1], rev), spec(wk, lf[1], rev), sspec(rev),
                spec(wv, 0, rev)]

    def dir_out_specs(rev):
        return [spec(wk, 0, rev), spec(wk, 0, rev), spec(wv, 0, rev), spec(wk, 0, rev)]

    shapes = [jax.ShapeDtypeStruct((n_rows, wk), out_dtype), jax.ShapeDtypeStruct((n_rows, wk), out_dtype),
              jax.ShapeDtypeStruct((n_rows, wv), out_dtype), jax.ShapeDtypeStruct((n_rows, wk), F32)]
    outs = pl.pallas_call(
        body, name=name, grid=(n_chunks,), in_specs=dir_specs(k_f, lf_f, False) + dir_specs(k_b, lf_b, True),
        out_specs=dir_out_specs(False) + dir_out_specs(True), out_shape=shapes + shapes,
        scratch_shapes=[pltpu.VMEM((2, n_heads, dv, dk), F32)],
        compiler_params=_params(("arbitrary",)),
    )(q[0], k_f[0], v[0], lf_f[0], st_f, do, q[0], k_b[0], v[0], lf_b[0], st_b, do)
    return outs[:4], outs[4:]


def _row2(v):
    return v.reshape(1, -1)


def _mlp_fwd(tag, h, gain, w1, w2):
    y = _rowcall(f"{tag}_norm", _rmsnorm_fn, [(h, h.shape[1], 0)], [gain], [(h.shape[1], BF16)], tm=512)[0]
    hid = _mm(f"{tag}_up", y, w1, out_dtypes=(BF16,))
    h_out = _mm(f"{tag}_down", hid, w2, a_pro=_relu2, extras=(h,), epi=_add_epi)
    return h_out, (y, hid)


def _mlp_bwd(tag, h, gain, w1, w2, saved, dh_out):
    y, hid = saved
    dhid = _mm(f"{tag}_dact", dh_out, w2, mode="nt", extras=(hid,), epi=_relu2_bwd_epi, out_dtypes=(BF16,))
    dw2 = _mm(f"{tag}_dw2", hid, dh_out, mode="tn", a_pro=_relu2)
    dw1 = _mm(f"{tag}_dw1", y, dhid, mode="tn")
    dy = _mm(f"{tag}_dy", dhid, w1, mode="nt")
    dh, dgain = _norm_bwd(f"{tag}_dnorm", h, gain, dy, dh_out)
    return dh, dgain, dw1, dw2


def _norm_bwd(name, h, gain, dy, dres):
    d = h.shape[1]

    def fn(h, dy, dres, gain):
        _, vjp = jax.vjp(lambda a, b: _rmsnorm_fn(a, b)[0], h, gain)
        dh, dgain = vjp(dy)
        return dh + dres, dgain

    dh, dgain = _rowcall(name, fn, [(h, d, 0), (dy, d, 0), (dres, d, 0)], [gain], [(d, F32)], [(1, d)], tm=512)
    return dh, dgain


def _local_step(x, target, w):
    g = {}
    d_model = x.shape[1]
    rg_w = hg_w = d_model // 2

    h_a0 = x
    gain = _row2(w["norm_mix"][0])
    y0 = _rowcall("l0_norm", _rmsnorm_fn, [(h_a0, d_model, 0)], [gain], [(d_model, BF16)], tm=512)[0]
    proj0 = _mm("l0_in", y0, w["ab_w_in"])
    conv_w, conv_b = w["rg_conv_w"], _row2(w["rg_conv_b"])
    xc = _conv_fwd("rg_conv", proj0, 0, conv_w, conv_b)
    gate_pars = [w["rg_wa_bd"], w["rg_wx_bd"], w["rg_b_a"], w["rg_b_x"], w["rg_lambda"]]
    a_f, u_f, a_b, u_b = _rowcall("rg_gates", _rg_gates_fn, [(xc, rg_w, 0)], gate_pars, [(rg_w, F32)] * 4)
    hs_f = _scan_fwd("rg_scan_f", a_f, u_f, False)
    hs_b = _scan_fwd("rg_scan_b", a_b, u_b, True)
    hg_rows = [(proj0, hg_w, 2), (proj0, hg_w, 3), (proj0, hg_w, 4)]
    qh, k_f, lf_f, k_b, lf_b = _rowcall("hg_pre", _hg_pre_fn, hg_rows, [w["hg_lb_logits"]], [(hg_w, F32)] * 5)
    iv = (proj0, 5)
    o_f, o_b, st_f, st_b = _attn_fwd("hg_attn", (qh, 0), (k_f, 0), (k_b, 0), iv, (lf_f, 0), (lf_b, 0), 4, 128, 128)
    post0_rows = [(hs_f, rg_w, 0), (hs_b, rg_w, 0), (proj0, rg_w, 1), (o_f, hg_w, 0), (o_b, hg_w, 0), (proj0, hg_w, 6)]
    hg_gain = _row2(w["hg_norm"])
    mix_in0 = _rowcall("l0_post", _post0_fwd_fn, post0_rows, [hg_gain], [(d_model, BF16)])[0]
    h_b0 = _mm("l0_out", mix_in0, w["ab_w_out"], extras=(h_a0,), epi=_add_epi)
    h_c0, mlp0 = _mlp_fwd("mlp0", h_b0, _row2(w["norm_mlp"][0]), w["mlp_w1"][0], w["mlp_w2"][0])

    h_a1 = h_c0
    gain1 = _row2(w["norm_mix"][1])
    y1 = _rowcall("l1_norm", _rmsnorm_fn, [(h_a1, d_model, 0)], [gain1], [(d_model, BF16)], tm=512)[0]
    proj1 = _mm("l1_in", y1, w["gla_w_in_pad"], tn=640)
    gla_pars = [w["gla_w_up_pad"], w["gla_b_gate"]]
    gq, glf_f, glf_b = _rowcall("gla_pre", _gla_pre_fn, [(proj1, 512, 0), (proj1, LANES, 24)], gla_pars, [(512, F32)] * 3)
    gk, gv = (proj1, 1), (proj1, 1)
    go_f, go_b, gst_f, gst_b = _attn_fwd("gla_attn", (gq, 0), gk, gk, gv, (glf_f, 0), (glf_b, 0), 4, 128, 256)
    gla_gain = _row2(w["gla_norm"])
    post1_rows = [(go_f, d_model, 0), (go_b, d_model, 0), (proj1, d_model, 2)]
    mix_in1 = _rowcall("l1_post", _gla_post_fwd_fn, post1_rows, [gla_gain], [(d_model, BF16)])[0]
    h_b1 = _mm("l1_out", mix_in1, w["gla_w_out"], extras=(h_a1,), epi=_add_epi)
    h_c1, mlp1 = _mlp_fwd("mlp1", h_b1, _row2(w["norm_mlp"][1]), w["mlp_w1"][1], w["mlp_w2"][1])

    dh, loss, g["norm_final"] = _rowcall(
        "loss_head", _loss_head_fn, [(h_c1, d_model, 0), (target, d_model, 0)], [_row2(w["norm_final"])],
        [(d_model, F32)], [(1, LANES), (1, d_model)], tm=512)

    dh, g_nmlp1, g_w1_1, g_w2_1 = _mlp_bwd("mlp1", h_b1, _row2(w["norm_mlp"][1]), w["mlp_w1"][1], w["mlp_w2"][1], mlp1, dh)
    dmix1 = _mm("l1_dout", dh, w["gla_w_out"], mode="nt")
    g["gla_w_out"] = _mm("l1_dwout", mix_in1, dh, mode="tn")
    dgo, dr, g["gla_norm"] = _rowcall(
        "l1_dpost", _gla_post_bwd_fn, post1_rows + [(dmix1, d_model, 0)], [gla_gain],
        [(d_model, F32), (d_model, BF16)], [(1, d_model)])
    (dq_f, dk_f, dv_f, dlf_f), (dq_b, dk_b, dv_b, dlf_b) = _attn_bwd(
        "gla_dattn", (gq, 0), gk, gk, gv, (glf_f, 0), (glf_b, 0), gst_f, gst_b, dgo, 4, 128, 256)

    def gla_pre_bwd(q, lr, dq1, dq2, dlf1, dlf2, dk1, dk2, dv1, dv2, w_up, b_gate):
        dlr = jnp.zeros_like(lr)
        dws, dbs = [], []
        for d, dlf in enumerate((dlf1, dlf2)):
            z = _raw_nn(lr, w_up[d]) + b_gate[d:d + 1]
            dz = dlf * _sigmoid(-z) * (1.0 / 16.0)
            dlr = dlr + _raw_nt(dz, w_up[d])
            dws.append(_raw_tn(dz, lr))
            dbs.append(jnp.sum(dz, axis=0, keepdims=True))
        return ((dq1 + dq2) * (128.0 ** -0.5), dk1 + dk2, dv1 + dv2, dlr, dws[0], dws[1], dbs[0], dbs[1])

    rows = [(proj1, 512, 0), (proj1, LANES, 24), (dq_f, 512, 0), (dq_b, 512, 0), (dlf_f, 512, 0), (dlf_b, 512, 0),
            (dk_f, 512, 0), (dk_b, 512, 0), (dv_f, d_model, 0), (dv_b, d_model, 0)]
    dq, dk, dv, dlr, dwt_f, dwt_b, db_f, db_b = _rowcall(
        "gla_dpre", gla_pre_bwd, rows, gla_pars, [(512, BF16), (512, BF16), (d_model, BF16), (LANES, BF16)],
        [(512, LANES), (512, LANES), (1, 512), (1, 512)])
    g["gla_w_up_pad"] = jnp.stack([dwt_f.T, dwt_b.T])
    g["gla_b_gate"] = jnp.concatenate([db_f, db_b], axis=0)
    dproj1 = jnp.concatenate([dq, dk, dv, dr, dlr], axis=1)
    g["gla_w_in_pad"] = _mm("l1_dwin", y1, dproj1, mode="tn", tn=640)
    dy1 = _mm("l1_dy", dproj1, w["gla_w_in_pad"], mode="nt", tk=640)
    dh, g_nmix1 = _norm_bwd("l1_dnorm", h_a1, gain1, dy1, dh)

    dh, g_nmlp0, g_w1_0, g_w2_0 = _mlp_bwd("mlp0", h_b0, _row2(w["norm_mlp"][0]), w["mlp_w1"][0], w["mlp_w2"][0], mlp0, dh)
    dmix0 = _mm("l0_dout", dh, w["ab_w_out"], mode="nt")
    g["ab_w_out"] = _mm("l0_dwout", mix_in0, dh, mode="tn")
    dhs, dga, do, dg, g["hg_norm"] = _rowcall(
        "l0_dpost", _post0_bwd_fn, post0_rows + [(dmix0, d_model, 0)], [hg_gain],
        [(rg_w, F32), (rg_w, BF16), (hg_w, F32), (hg_w, BF16)], [(1, hg_w)])
    (dqh_f, dk_f, div_f, dlf_f), (dqh_b, dk_b, div_b, dlf_b) = _attn_bwd(
        "hg_dattn", (qh, 0), (k_f, 0), (k_b, 0), iv, (lf_f, 0), (lf_b, 0), st_f, st_b, do, 4, 128, 128)

    def hg_pre_bwd(q, f_f, f_b, dq1, dq2, dk1, dlf1, dk2, dlf2, dv1, dv2, logits):
        _, vjp = jax.vjp(_hg_pre_fn, q, f_f, f_b, logits)
        dq, df_f, df_b, dlogits = vjp((dq1 + dq2, dk1, dlf1, dk2, dlf2))
        return dq, df_f, df_b, dv1 + dv2, dlogits

    rows = hg_rows + [(t, hg_w, 0) for t in (dqh_f, dqh_b, dk_f, dlf_f, dk_b, dlf_b, div_f, div_b)]
    dq, df_f, df_b, div, g["hg_lb_logits"] = _rowcall(
        "hg_dpre", hg_pre_bwd, rows, [w["hg_lb_logits"]], [(hg_w, BF16)] * 4, [(2, hg_w)])
    du_f, da_f = _scan_bwd("rg_dscan_f", a_f, hs_f, dhs, False)
    du_b, da_b = _scan_bwd("rg_dscan_b", a_b, hs_b, dhs, True)
    gates_bwd = _vjp_of(_rg_gates_fn, 1, 4, 5)
    rows = [(xc, rg_w, 0), (da_f, rg_w, 0), (du_f, rg_w, 0), (da_b, rg_w, 0), (du_b, rg_w, 0)]
    dxc, g["rg_wa_bd"], g["rg_wx_bd"], g["rg_b_a"], g["rg_b_x"], g["rg_lambda"] = _rowcall(
        "rg_dgates", gates_bwd, rows, gate_pars, [(rg_w, F32)],
        [(2, rg_w, rg_w), (2, rg_w, rg_w), (2, rg_w), (2, rg_w), (2, rg_w)])
    dxa, g["rg_conv_w"], g["rg_conv_b"] = _conv_bwd("rg_dconv", proj0, 0, conv_w, dxc)
    dproj0 = jnp.concatenate([dxa, dga, dq, df_f, df_b, div, dg], axis=1)
    g["ab_w_in"] = _mm("l0_dwin", y0, dproj0, mode="tn")
    dy0 = _mm("l0_dy", dproj0, w["ab_w_in"], mode="nt")
    grad_x, g_nmix0 = _norm_bwd("l0_dnorm", h_a0, gain, dy0, dh)

    g["norm_mix"] = jnp.concatenate([g_nmix0, g_nmix1], axis=0)
    g["norm_mlp"] = jnp.concatenate([g_nmlp0, g_nmlp1], axis=0)
    g["mlp_w1"] = jnp.stack([g_w1_0, g_w1_1])
    g["mlp_w2"] = jnp.stack([g_w2_0, g_w2_1])
    return loss, grad_x, g


def _block_diag(w):
    d, g, n, _ = w.shape
    eye = jnp.eye(g, dtype=w.dtype)
    return (w[:, :, :, None, :] * eye[None, :, None, :, None]).reshape(d, g * n, g * n)


def _block_diag_extract(wbd, g):
    d, gn, _ = wbd.shape
    n = gn // g
    blocks = wbd.reshape(d, g, n, g, n)
    return jnp.stack([blocks[:, i, :, i, :] for i in range(g)], axis=1)


def _prepare_weights(full):
    w = {k: full[k] for k in ("norm_mix", "norm_mlp", "norm_final", "hg_lb_logits")}
    for k in ("mlp_w1", "mlp_w2"):
        w[k] = full[k].astype(BF16)
    for k in ("ab_w_in", "ab_w_out", "gla_w_out"):
        w[k] = full[k][0].astype(BF16)
    for k in ("rg_conv_w", "rg_conv_b", "rg_b_a", "rg_b_x", "rg_lambda", "hg_norm", "gla_b_gate", "gla_norm"):
        w[k] = full[k][0]
    w["rg_wa_bd"] = _block_diag(full["rg_w_a"][0])
    w["rg_wx_bd"] = _block_diag(full["rg_w_x"][0])
    gla_in = full["gla_w_in"][0].astype(BF16)
    w["gla_w_in_pad"] = jnp.pad(gla_in, ((0, 0), (0, 3200 - gla_in.shape[1])))
    up = full["gla_w_gate_up"][0]
    rank = up.shape[1]
    pad = jnp.zeros((2, LANES, up.shape[2]), F32)
    w["gla_w_up_pad"] = pad.at[0, 0:rank].set(up[0]).at[1, rank:2 * rank].set(up[1])
    return w


def _finish_grads(g, rank=16, gla_in_width=3104, rg_blocks=8):
    out = {
        "norm_mix": g["norm_mix"], "norm_mlp": g["norm_mlp"], "norm_final": g["norm_final"][0],
        "mlp_w1": g["mlp_w1"], "mlp_w2": g["mlp_w2"],
        "ab_w_in": g["ab_w_in"][None], "ab_w_out": g["ab_w_out"][None],
        "rg_conv_w": g["rg_conv_w"][None], "rg_conv_b": g["rg_conv_b"],
        "rg_w_a": _block_diag_extract(g["rg_wa_bd"], rg_blocks)[None], "rg_b_a": g["rg_b_a"][None],
        "rg_w_x": _block_diag_extract(g["rg_wx_bd"], rg_blocks)[None], "rg_b_x": g["rg_b_x"][None],
        "rg_lambda": g["rg_lambda"][None], "hg_lb_logits": g["hg_lb_logits"], "hg_norm": g["hg_norm"],
        "gla_w_in": g["gla_w_in_pad"][None, :, :gla_in_width], "gla_w_out": g["gla_w_out"][None],
        "gla_w_gate_up": jnp.stack([g["gla_w_up_pad"][0, 0:rank], g["gla_w_up_pad"][1, rank:2 * rank]])[None],
        "gla_b_gate": g["gla_b_gate"][None], "gla_norm": g["gla_norm"],
    }
    return out


BIG = (("mlp_w1", 2), ("mlp_w2", 1), ("ab_w_in", 2), ("ab_w_out", 1), ("gla_w_in", 2), ("gla_w_out", 1))
SMALL_SHARDED = ("rg_conv_w", "rg_b_a", "rg_b_x", "rg_lambda", "gla_w_gate_up", "gla_b_gate", "gla_norm")
SMALL_REPLICATED = ("norm_mix", "norm_mlp", "norm_final", "rg_conv_b", "rg_w_a", "rg_w_x", "hg_lb_logits", "hg_norm")
WEIGHTS = ("norm_mix", "norm_mlp", "norm_final", "mlp_w1", "mlp_w2", "ab_w_in", "ab_w_out", "rg_conv_w", "rg_conv_b",
           "rg_w_a", "rg_b_a", "rg_w_x", "rg_b_x", "rg_lambda", "hg_lb_logits", "hg_norm", "gla_w_in", "gla_w_out",
           "gla_w_gate_up", "gla_b_gate", "gla_norm")
ROW_ALIGN = 16


def _pack(arrays, lead=0):
    head = arrays[0].shape[:lead]
    flat = jnp.concatenate([a.reshape(head + (-1,)) for a in arrays], axis=lead)
    n = flat.shape[-1]
    quantum = LANES * ROW_ALIGN
    padded = -(-n // quantum) * quantum
    if padded != n:
        flat = jnp.pad(flat, [(0, 0)] * lead + [(0, padded - n)])
    return flat.reshape(head + (padded // LANES, LANES))


def _unpack(buf, shapes, lead=0):
    head = buf.shape[:lead]
    flat = buf.reshape(head + (-1,))
    out, off = [], 0
    for s in shapes:
        n = 1
        for v in s:
            n *= v
        out.append(lax.slice_in_dim(flat, off, off + n, axis=lead).reshape(head + tuple(s)))
        off += n
    return out


def _join_chips(gathered, axis):
    t = jnp.moveaxis(gathered, 0, axis)
    return t.reshape(t.shape[:axis] + (t.shape[axis] * t.shape[axis + 1],) + t.shape[axis + 2:])


def _split_chips(full, axis):
    s = full.shape
    t = full.reshape(s[:axis] + (N_CHIPS, s[axis] // N_CHIPS) + s[axis + 1:])
    return jnp.moveaxis(t, axis, 0)


_ANY = pl.BlockSpec(memory_space=pl.ANY)


def _place():
    return lax.axis_index("x"), lax.axis_index("y"), lax.axis_index("c")


def _into_slot(name, src, slot, n_slots, dtype, tm):
    r, lanes = src.shape

    def body(slot_ref, in_ref, o_ref):
        o_ref[...] = in_ref[...].astype(o_ref.dtype)

    grid_spec = pltpu.PrefetchScalarGridSpec(
        num_scalar_prefetch=1, grid=(r // tm,),
        in_specs=[pl.BlockSpec((tm, lanes), lambda i, slot_ref: (i, 0))],
        out_specs=pl.BlockSpec((None, tm, lanes), lambda i, slot_ref: (slot_ref[0], i, 0)))
    return pl.pallas_call(
        body, name=name, grid_spec=grid_spec, out_shape=jax.ShapeDtypeStruct((n_slots, r, lanes), dtype),
        compiler_params=_params(("parallel",)),
    )(slot.reshape(1).astype(jnp.int32), src)


def _gather_chips(name, buf):
    def body(in_ref, out_ref, send_sems, recv_sems):
        x, y, c = _place()
        me = 2 * x + y
        peers = [(1 - x, y), (x, 1 - y), (1 - x, 1 - y)]

        def copy(j, block):
            px, py = peers[j]
            return pltpu.make_async_remote_copy(
                src_ref=in_ref.at[block], dst_ref=out_ref.at[block], send_sem=send_sems.at[j],
                recv_sem=recv_sems.at[j], device_id=(px, py, c), device_id_type=MESH)

        sends = [copy(j, me) for j in range(3)]
        for cp in sends:
            cp.start()
        for j, (px, py) in enumerate(peers):
            copy(j, 2 * px + py).wait_recv()
        for cp in sends:
            cp.wait_send()

    return pl.pallas_call(
        body, name=name, in_specs=[_ANY], out_specs=_ANY, out_shape=jax.ShapeDtypeStruct(buf.shape, buf.dtype),
        input_output_aliases={0: 0},
        scratch_shapes=[pltpu.SemaphoreType.DMA((3,)), pltpu.SemaphoreType.DMA((3,))],
    )(buf)


def _pair_exchange(name, g):
    n, _, rh, lanes = g.shape

    def body(g_ref, out_ref, send_sem, recv_sem):
        x, y, c = _place()
        cp = pltpu.make_async_remote_copy(
            src_ref=g_ref.at[:, 1 - c], dst_ref=out_ref, send_sem=send_sem, recv_sem=recv_sem,
            device_id=(x, y, 1 - c), device_id_type=MESH)
        cp.start()
        cp.wait()

    return pl.pallas_call(
        body, name=name, in_specs=[_ANY], out_specs=_ANY,
        out_shape=jax.ShapeDtypeStruct((n, rh, lanes), g.dtype),
        scratch_shapes=[pltpu.SemaphoreType.DMA, pltpu.SemaphoreType.DMA],
    )(g)


def _pair_add(name, g, got, c, tm):
    n, _, rh, lanes = g.shape

    def body(c_ref, g_ref, got_ref, o_ref):
        o_ref[...] = g_ref[...] + got_ref[...]

    grid_spec = pltpu.PrefetchScalarGridSpec(
        num_scalar_prefetch=1, grid=(n, rh // tm),
        in_specs=[pl.BlockSpec((None, None, tm, lanes), lambda s, i, c_ref: (s, c_ref[0], i, 0)),
                  pl.BlockSpec((None, tm, lanes), lambda s, i, c_ref: (s, i, 0))],
        out_specs=pl.BlockSpec((None, tm, lanes), lambda s, i, c_ref: (s, i, 0)))
    return pl.pallas_call(
        body, name=name, grid_spec=grid_spec, out_shape=jax.ShapeDtypeStruct((n, rh, lanes), F32),
        compiler_params=_params(("parallel", "parallel")),
    )(c.reshape(1).astype(jnp.int32), g, got)


def _chip_scatter(name, p):
    def body(p_ref, out_ref, send_sems, recv_sems):
        x, y, c = _place()
        me = 2 * x + y
        peers = [(1 - x, y), (x, 1 - y), (1 - x, 1 - y)]

        def copy(j, src_block, dst_block):
            px, py = peers[j]
            return pltpu.make_async_remote_copy(
                src_ref=p_ref.at[src_block], dst_ref=out_ref.at[dst_block], send_sem=send_sems.at[j],
                recv_sem=recv_sems.at[j], device_id=(px, py, c), device_id_type=MESH)

        sends = [copy(j, 2 * px + py, me) for j, (px, py) in enumerate(peers)]
        for cp in sends:
            cp.start()
        for j, (px, py) in enumerate(peers):
            copy(j, me, 2 * px + py).wait_recv()
        for cp in sends:
            cp.wait_send()

    return pl.pallas_call(
        body, name=name, in_specs=[_ANY], out_specs=_ANY, out_shape=jax.ShapeDtypeStruct(p.shape, p.dtype),
        scratch_shapes=[pltpu.SemaphoreType.DMA((3,)), pltpu.SemaphoreType.DMA((3,))],
    )(p)


def _sum_ring(name, own, got, chip, core, tm):
    n, rh, lanes = own.shape

    def body(idx_ref, own_ref, g1_ref, g2_ref, g3_ref, o_ref):
        o_ref[...] = ((own_ref[...] + g1_ref[...]) + g2_ref[...]) + g3_ref[...]

    def block(k):
        return pl.BlockSpec((None, tm, lanes), lambda i, idx_ref: ((idx_ref[0] + k) % n, i, 0))

    grid_spec = pltpu.PrefetchScalarGridSpec(
        num_scalar_prefetch=1, grid=(rh // tm,), in_specs=[block(0), block(1), block(2), block(3)],
        out_specs=pl.BlockSpec((None, tm, lanes), lambda i, idx_ref: (idx_ref[1], i, 0)))
    return pl.pallas_call(
        body, name=name, grid_spec=grid_spec, out_shape=jax.ShapeDtypeStruct((2, rh, lanes), F32),
        compiler_params=_params(("parallel",)),
    )(jnp.stack([chip, core]).astype(jnp.int32), own, got, got, got)


def _pair_gather(name, buf):
    def body(in_ref, out_ref, send_sem, recv_sem):
        x, y, c = _place()
        send = pltpu.make_async_remote_copy(
            src_ref=in_ref.at[c], dst_ref=out_ref.at[c], send_sem=send_sem, recv_sem=recv_sem,
            device_id=(x, y, 1 - c), device_id_type=MESH)
        send.start()
        pltpu.make_async_remote_copy(
            src_ref=in_ref.at[1 - c], dst_ref=out_ref.at[1 - c], send_sem=send_sem, recv_sem=recv_sem,
            device_id=(x, y, 1 - c), device_id_type=MESH).wait_recv()
        send.wait_send()

    return pl.pallas_call(
        body, name=name, in_specs=[_ANY], out_specs=_ANY, out_shape=jax.ShapeDtypeStruct(buf.shape, buf.dtype),
        input_output_aliases={0: 0},
        scratch_shapes=[pltpu.SemaphoreType.DMA, pltpu.SemaphoreType.DMA],
    )(buf)


def _gather_all(name, s):
    def body(in_ref, out_ref, send_sems, recv_sems, local_sem):
        x, y, c = _place()
        me = 4 * x + 2 * y + c
        peers = []
        for mask in range(1, N_DEV):
            fx, fy, fc = (mask >> 2) & 1, (mask >> 1) & 1, mask & 1
            peers.append((jnp.where(fx, 1 - x, x), jnp.where(fy, 1 - y, y), jnp.where(fc, 1 - c, c)))

        def copy(j, block):
            return pltpu.make_async_remote_copy(
                src_ref=in_ref, dst_ref=out_ref.at[block], send_sem=send_sems.at[j], recv_sem=recv_sems.at[j],
                device_id=peers[j], device_id_type=MESH)

        local = pltpu.make_async_copy(in_ref, out_ref.at[me], local_sem)
        local.start()
        sends = [copy(j, me) for j in range(N_DEV - 1)]
        for cp in sends:
            cp.start()
        for j, (px, py, pc) in enumerate(peers):
            copy(j, 4 * px + 2 * py + pc).wait_recv()
        for cp in sends:
            cp.wait_send()
        local.wait()

    return pl.pallas_call(
        body, name=name, in_specs=[_ANY], out_specs=_ANY,
        out_shape=jax.ShapeDtypeStruct((N_DEV,) + s.shape, s.dtype),
        scratch_shapes=[pltpu.SemaphoreType.DMA((N_DEV - 1,)), pltpu.SemaphoreType.DMA((N_DEV - 1,)),
                        pltpu.SemaphoreType.DMA],
    )(s)


def _sum_blocks(name, stacked, tm):
    n, r, lanes = stacked.shape

    def body(in_ref, o_ref):
        acc = in_ref[0]
        for j in range(1, n):
            acc = acc + in_ref[j]
        o_ref[...] = acc

    return pl.pallas_call(
        body, name=name, grid=(r // tm,), in_specs=[pl.BlockSpec((n, tm, lanes), lambda i: (0, i, 0))],
        out_specs=pl.BlockSpec((tm, lanes), lambda i: (i, 0)), out_shape=jax.ShapeDtypeStruct((r, lanes), F32),
        compiler_params=_params(("parallel",)),
    )(stacked)


def _row_tile(rows, pref, align):
    best = None
    for t in range(align, min(rows, pref) + 1, align):
        if rows % t == 0:
            best = t
    assert best is not None, (rows, pref, align)
    return best


def _adam(name, w, g, m, v):
    rows = w.shape[0]
    tm = _row_tile(rows, 4096, 8)
    args = [(t, LANES, 0) for t in (w, g, m, v)]
    return _rowcall(name, _adam_fn, args, [], [(LANES, F32)] * 3, tm=tm)


def kernel(x, norm_mix, norm_mlp, norm_final, mlp_w1, mlp_w2, ab_w_in, ab_w_out, rg_conv_w, rg_conv_b, rg_w_a, rg_b_a, rg_w_x, rg_b_x, rg_lambda, hg_lb_logits, hg_norm, gla_w_in, gla_w_out, gla_w_gate_up, gla_b_gate, gla_norm, loss_target, m_norm_mix, m_norm_mlp, m_norm_final, m_mlp_w1, m_mlp_w2, m_ab_w_in, m_ab_w_out, m_rg_conv_w, m_rg_conv_b, m_rg_w_a, m_rg_b_a, m_rg_w_x, m_rg_b_x, m_rg_lambda, m_hg_lb_logits, m_hg_norm, m_gla_w_in, m_gla_w_out, m_gla_w_gate_up, m_gla_b_gate, m_gla_norm, v_norm_mix, v_norm_mlp, v_norm_final, v_mlp_w1, v_mlp_w2, v_ab_w_in, v_ab_w_out, v_rg_conv_w, v_rg_conv_b, v_rg_w_a, v_rg_b_a, v_rg_w_x, v_rg_b_x, v_rg_lambda, v_hg_lb_logits, v_hg_norm, v_gla_w_in, v_gla_w_out, v_gla_w_gate_up, v_gla_b_gate, v_gla_norm):
    w = dict(norm_mix=norm_mix, norm_mlp=norm_mlp, norm_final=norm_final, mlp_w1=mlp_w1, mlp_w2=mlp_w2, ab_w_in=ab_w_in, ab_w_out=ab_w_out, rg_conv_w=rg_conv_w, rg_conv_b=rg_conv_b, rg_w_a=rg_w_a, rg_b_a=rg_b_a, rg_w_x=rg_w_x, rg_b_x=rg_b_x, rg_lambda=rg_lambda, hg_lb_logits=hg_lb_logits, hg_norm=hg_norm, gla_w_in=gla_w_in, gla_w_out=gla_w_out, gla_w_gate_up=gla_w_gate_up, gla_b_gate=gla_b_gate, gla_norm=gla_norm)
    m = dict(norm_mix=m_norm_mix, norm_mlp=m_norm_mlp, norm_final=m_norm_final, mlp_w1=m_mlp_w1, mlp_w2=m_mlp_w2, ab_w_in=m_ab_w_in, ab_w_out=m_ab_w_out, rg_conv_w=m_rg_conv_w, rg_conv_b=m_rg_conv_b, rg_w_a=m_rg_w_a, rg_b_a=m_rg_b_a, rg_w_x=m_rg_w_x, rg_b_x=m_rg_b_x, rg_lambda=m_rg_lambda, hg_lb_logits=m_hg_lb_logits, hg_norm=m_hg_norm, gla_w_in=m_gla_w_in, gla_w_out=m_gla_w_out, gla_w_gate_up=m_gla_w_gate_up, gla_b_gate=m_gla_b_gate, gla_norm=m_gla_norm)
    v = dict(norm_mix=v_norm_mix, norm_mlp=v_norm_mlp, norm_final=v_norm_final, mlp_w1=v_mlp_w1, mlp_w2=v_mlp_w2, ab_w_in=v_ab_w_in, ab_w_out=v_ab_w_out, rg_conv_w=v_rg_conv_w, rg_conv_b=v_rg_conv_b, rg_w_a=v_rg_w_a, rg_b_a=v_rg_b_a, rg_w_x=v_rg_w_x, rg_b_x=v_rg_b_x, rg_lambda=v_rg_lambda, hg_lb_logits=v_hg_lb_logits, hg_norm=v_hg_norm, gla_w_in=v_gla_w_in, gla_w_out=v_gla_w_out, gla_w_gate_up=v_gla_w_gate_up, gla_b_gate=v_gla_b_gate, gla_norm=v_gla_norm)
    chip = 2 * lax.axis_index("x") + lax.axis_index("y")
    core = lax.axis_index("c")
    big_names = [n for n, _ in BIG]
    big_shapes = [w[n].shape for n in big_names]
    sharded_shapes = [w[n].shape for n in SMALL_SHARDED]

    w_big = _pack([w[n] for n in big_names])
    cast_tm = _row_tile(w_big.shape[0], 4096, ROW_ALIGN)
    w_big16 = _into_slot("cast_weights", w_big, chip, N_CHIPS, BF16, cast_tm)
    big_all = _unpack(_gather_chips("gather_weights", w_big16), big_shapes, lead=1)
    vectors = _pack([w[n] for n in SMALL_SHARDED])
    vectors = _into_slot("place_vectors", vectors, chip, N_CHIPS, F32, vectors.shape[0])
    small_all = _unpack(_gather_chips("gather_vectors", vectors), sharded_shapes, lead=1)
    full = {n: w[n] for n in SMALL_REPLICATED}
    for (n, axis), t in zip(BIG, big_all):
        full[n] = _join_chips(t, axis)
    for n, t in zip(SMALL_SHARDED, small_all):
        full[n] = _join_chips(t, t.ndim - 2)

    loss_part, grad_x, g_kernel = _local_step(x[0], loss_target[0], _prepare_weights(full))
    g_full = _finish_grads(g_kernel)
    loss = lax.psum(loss_part[0, 0], ("x", "y", "c"))

    g_big = _pack([_split_chips(g_full[n], axis) for n, axis in BIG], lead=1)
    rows = g_big.shape[1]
    half = rows // 2
    add_tm = _row_tile(half, 4096, 8)
    g_halves = g_big.reshape(N_CHIPS, 2, half, LANES)
    from_sibling = _pair_exchange("reduce_pair", g_halves)
    chip_part = _pair_add("reduce_pair_add", g_halves, from_sibling, core, add_tm)
    from_chips = _chip_scatter("reduce_chips", chip_part)
    mine = _sum_ring("reduce_chips_add", chip_part, from_chips, chip, core, add_tm)
    g_big_red = _pair_gather("reduce_share", mine).reshape(rows, LANES)

    small_names = SMALL_REPLICATED + SMALL_SHARDED
    g_small = _pack([g_full[n] for n in small_names])
    g_small_all = _gather_all("reduce_small", g_small)
    g_small_red = _sum_blocks("reduce_small_add", g_small_all, g_small.shape[0])
    g_small_full = dict(zip(small_names, _unpack(g_small_red, [g_full[n].shape for n in small_names])))
    grads = {n: g_small_full[n] for n in SMALL_REPLICATED}
    for n in SMALL_SHARDED:
        width = w[n].shape[-1]
        grads[n] = lax.dynamic_slice_in_dim(g_small_full[n], chip * width, width, axis=g_small_full[n].ndim - 1)
    for n, t in zip(big_names, _unpack(g_big_red, big_shapes)):
        grads[n] = t

    d_big, m_big, v_big = _adam("adam_big", w_big, g_big_red, _pack([m[n] for n in big_names]), _pack([v[n] for n in big_names]))
    delta, new_m, new_v = {}, {}, {}
    for dst, buf in ((delta, d_big), (new_m, m_big), (new_v, v_big)):
        dst.update(zip(big_names, _unpack(buf, big_shapes)))
    small_shapes = [w[n].shape for n in small_names]
    packs = [_pack([src[n] for n in small_names]) for src in (w, grads, m, v)]
    d_small, m_small, v_small = _adam("adam_small", *packs)
    for dst, buf in ((delta, d_small), (new_m, m_small), (new_v, v_small)):
        dst.update(zip(small_names, _unpack(buf, small_shapes)))

    return (loss, grad_x[None], *[grads[n] for n in WEIGHTS], *[delta[n] for n in WEIGHTS],
            *[new_m[n] for n in WEIGHTS], *[new_v[n] for n in WEIGHTS])
```

```python
import functools

import jax
import jax.numpy as jnp
from jax import lax
from jax.experimental import pallas as pl
from jax.experimental.pallas import tpu as pltpu

F32 = jnp.float32
BF16 = jnp.bfloat16
MESH = pl.DeviceIdType.MESH

LANES = 128
CHUNK = 64
EPS = 1e-6
RG_C = 8.0
N_CHIPS = 4
N_DEV = 8
GLA_IN_PAD = 3200
VMEM_LIMIT = 56 * 1024 * 1024

ADAM_LR = 0.001
ADAM_B1 = 0.9
ADAM_B2 = 0.999
ADAM_EPS = 1e-08
ADAM_WD = 0.01
ADAM_STEP = 10


def _raw_dot(a, b, ca, cb):
    return lax.dot_general(a.astype(BF16), b.astype(BF16), (((ca,), (cb,)), ((), ())),
                           preferred_element_type=F32)


def _raw_nn(a, b):
    return _raw_dot(a, b, 1, 0)


def _raw_nt(a, b):
    return _raw_dot(a, b, 1, 1)


def _raw_tn(a, b):
    return _raw_dot(a, b, 0, 0)


@jax.custom_vjp
def _dot_nn(a, b):
    return _raw_nn(a, b)


def _dot_nn_fwd(a, b):
    return _raw_nn(a, b), (a, b)


def _dot_nn_bwd(res, g):
    a, b = res
    return _raw_nt(g, b), _raw_tn(a, g)


_dot_nn.defvjp(_dot_nn_fwd, _dot_nn_bwd)


@jax.custom_vjp
def _dot_nt(a, b):
    return _raw_nt(a, b)


def _dot_nt_fwd(a, b):
    return _raw_nt(a, b), (a, b)


def _dot_nt_bwd(res, g):
    a, b = res
    return _raw_nn(g, b), _raw_tn(g, a)


_dot_nt.defvjp(_dot_nt_fwd, _dot_nt_bwd)


@jax.custom_vjp
def _dot_tn(a, b):
    return _raw_tn(a, b)


def _dot_tn_fwd(a, b):
    return _raw_tn(a, b), (a, b)


def _dot_tn_bwd(res, g):
    a, b = res
    return _raw_nt(b, g), _raw_nn(a, g)


_dot_tn.defvjp(_dot_tn_fwd, _dot_tn_bwd)


def _tile(n, pref):
    if n <= pref:
        return n
    t = (pref // LANES) * LANES
    while t > LANES and n % t:
        t -= LANES
    assert n % t == 0, (n, pref)
    return t


def _params(sem):
    return pltpu.CompilerParams(dimension_semantics=sem, vmem_limit_bytes=VMEM_LIMIT)


def _rowcall(name, fn, rows, pars, row_outs, par_outs=(), tm=256):
    n_rows = rows[0][0].shape[0]
    tm = min(tm, n_rows)
    assert n_rows % tm == 0
    n_r, n_p, n_ro = len(rows), len(pars), len(row_outs)

    def body(*refs):
        vals = [r[...].astype(F32) for r in refs[:n_r + n_p]]
        outs = fn(*vals)
        o_refs = refs[n_r + n_p:n_r + n_p + n_ro]
        po_refs = refs[n_r + n_p + n_ro:]
        for o_ref, val in zip(o_refs, outs[:n_ro]):
            o_ref[...] = val.astype(o_ref.dtype)
        first = pl.program_id(0) == 0
        for po_ref, val in zip(po_refs, outs[n_ro:]):
            @pl.when(first)
            def _():
                po_ref[...] = val

            @pl.when(jnp.logical_not(first))
            def _():
                po_ref[...] += val

    def const_map(nd):
        return lambda i: (0,) * nd

    def row_spec(w, cb):
        return pl.BlockSpec((tm, w), lambda i: (i, cb))

    in_specs = [row_spec(w, cb) for _, w, cb in rows]
    in_specs += [pl.BlockSpec(p.shape, const_map(p.ndim)) for p in pars]
    out_specs = [pl.BlockSpec((tm, w), lambda i: (i, 0)) for w, _ in row_outs]
    out_specs += [pl.BlockSpec(tuple(s), const_map(len(s))) for s in par_outs]
    out_shape = [jax.ShapeDtypeStruct((n_rows, w), dt) for w, dt in row_outs]
    out_shape += [jax.ShapeDtypeStruct(tuple(s), F32) for s in par_outs]
    return pl.pallas_call(
        body, name=name, grid=(n_rows // tm,), in_specs=in_specs, out_specs=out_specs, out_shape=out_shape,
        compiler_params=_params(("arbitrary",) if par_outs else ("parallel",)),
    )(*[r[0] for r in rows], *pars)


def _vjp_of(fn, n_prim, n_out, n_par, n_pass=0):
    def bwd(*args):
        prim = args[:n_prim]
        cts = args[n_prim:n_prim + n_out]
        passes = args[n_prim + n_out:n_prim + n_out + 2 * n_pass]
        pars = args[n_prim + n_out + 2 * n_pass:]
        _, vjp = jax.vjp(fn, *prim, *pars)
        grads = vjp(tuple(cts))
        sums = tuple(passes[2 * i] + passes[2 * i + 1] for i in range(n_pass))
        return tuple(grads[:n_prim]) + sums + tuple(grads[n_prim:])
    return bwd


def _mm(name, a, b, mode="nn", extras=(), epi=None, out_dtypes=(F32,), a_pro=None, out_split=None,
        tm=1024, tn=1024, tk=1024):
    split = b.shape[0] if b.ndim == 3 else None
    b_rows, b_cols = b.shape[-2:]
    if mode == "nn":
        (m, k), n = a.shape, b_cols * (split or 1)
    elif mode == "nt":
        (m, k), n = a.shape, b_rows
        assert k == b_cols * (split or 1)
    else:
        assert split is None
        (k, m), n = a.shape, b_cols
    tm, tk = _tile(m, tm), _tile(k, tk)
    tn = _tile(n // out_split, tn) if out_split else _tile(n, tn)
    if split and mode == "nn":
        tn = _tile(b_cols, tn)
    if split and mode == "nt":
        tk = _tile(b_cols, tk)
    nk = k // tk
    raw = {"nn": _raw_nn, "nt": _raw_nt, "tn": _raw_tn}[mode]
    n_e, n_o = len(extras), len(out_dtypes)
    if epi is None:
        epi = lambda acc: (acc,)

    def body(a_ref, b_ref, *rest):
        e_refs, o_refs, acc = rest[:n_e], rest[n_e:n_e + n_o], rest[-1]
        kk = pl.program_id(2)

        @pl.when(kk == 0)
        def _():
            acc[...] = jnp.zeros_like(acc)

        a_tile = a_ref[...] if a_pro is None else a_pro(a_ref[...].astype(F32))
        acc[...] += raw(a_tile, b_ref[...])

        @pl.when(kk == nk - 1)
        def _():
            res = epi(acc[...], *[e[...].astype(F32) for e in e_refs])
            for o_ref, r in zip(o_refs, res):
                o_ref[...] = r.astype(o_ref.dtype)

    a_spec = pl.BlockSpec((tk, tm), lambda i, j, kk: (kk, i)) if mode == "tn" else pl.BlockSpec((tm, tk), lambda i, j, kk: (i, kk))
    if split and mode == "nn":
        per = b_cols // tn
        b_spec = pl.BlockSpec((None, tk, tn), lambda i, j, kk: (j // per, kk, j % per))
    elif split:
        per = b_cols // tk
        b_spec = pl.BlockSpec((None, tn, tk), lambda i, j, kk: (kk // per, j, kk % per))
    elif mode == "nt":
        b_spec = pl.BlockSpec((tn, tk), lambda i, j, kk: (j, kk))
    else:
        b_spec = pl.BlockSpec((tk, tn), lambda i, j, kk: (kk, j))
    mn_spec = pl.BlockSpec((tm, tn), lambda i, j, kk: (i, j))
    if out_split:
        assert not extras
        per_out = n // out_split // tn
        out_spec = pl.BlockSpec((None, tm, tn), lambda i, j, kk: (j // per_out, i, j % per_out))
        out_shapes = [jax.ShapeDtypeStruct((out_split, m, n // out_split), dt) for dt in out_dtypes]
    else:
        out_spec = mn_spec
        out_shapes = [jax.ShapeDtypeStruct((m, n), dt) for dt in out_dtypes]
    outs = pl.pallas_call(
        body, name=name, grid=(m // tm, n // tn, nk),
        in_specs=[a_spec, b_spec] + [mn_spec] * n_e, out_specs=[out_spec] * n_o,
        out_shape=out_shapes,
        scratch_shapes=[pltpu.VMEM((tm, tn), F32)],
        compiler_params=_params(("parallel", "parallel", "arbitrary")),
    )(a, b, *extras)
    return outs[0] if n_o == 1 else outs


def _sigmoid(x):
    return jax.nn.sigmoid(x)


def _silu(x):
    return x * _sigmoid(x)


def _softplus(x):
    return jnp.maximum(x, 0.0) + jnp.log1p(jnp.exp(-jnp.abs(x)))


def _rmsnorm_fn(x, gain):
    return (x * lax.rsqrt(jnp.mean(x * x, axis=-1, keepdims=True) + EPS) * gain,)


def _head_norm(o, gain, n_heads):
    w = o.shape[-1] // n_heads
    parts = []
    for h in range(n_heads):
        oh = o[:, h * w:(h + 1) * w]
        parts.append(oh * lax.rsqrt(jnp.mean(oh * oh, axis=-1, keepdims=True) + EPS))
    return jnp.concatenate(parts, axis=-1) * gain


@jax.custom_jvp
def _neg_expm1(x):
    u = jnp.exp(x)
    is_one = u == 1.0
    return jnp.where(is_one, -x, (1.0 - u) * x / jnp.log(jnp.where(is_one, 2.0, u)))


@_neg_expm1.defjvp
def _neg_expm1_jvp(primals, tangents):
    (x,), (t,) = primals, tangents
    return _neg_expm1(x), -jnp.exp(x) * t


def _rg_gates_fn(xc, wa, wx, ba, bx, lam):
    outs = []
    for d in range(2):
        r = _sigmoid(_dot_nn(xc, wa[d]) + ba[d:d + 1])
        i = _sigmoid(_dot_nn(xc, wx[d]) + bx[d:d + 1])
        log_a = -RG_C * r * _softplus(-lam[d:d + 1])
        outs.append(jnp.exp(log_a))
        outs.append(jnp.sqrt(_neg_expm1(2.0 * log_a)) * (i * xc))
    return tuple(outs)


def _hg_pre_fn(q, f_f, f_b, logits):
    mx = jnp.maximum(logits[0:1], logits[1:2])
    e0 = jnp.exp(logits[0:1] - mx)
    e1 = jnp.exp(logits[1:2] - mx)
    lb = e0 / (e0 + e1)
    outs = [_silu(q)]
    for f in (f_f, f_b):
        outs.append((1.0 - lb) * _sigmoid(-f))
        outs.append(jnp.log(lb + (1.0 - lb) * _sigmoid(f)))
    return tuple(outs)


def _post0_fn(hs, ga, o, g, gain):
    ya = hs * jax.nn.gelu(ga, approximate=True)
    yb = _head_norm(o, gain, 4) * _silu(g)
    return (jnp.concatenate([ya, yb], axis=-1),)


def _post0_fwd_fn(h_f, h_b, ga, o_f, o_b, g, gain):
    return _post0_fn(h_f + h_b, ga, o_f + o_b, g, gain)


def _post0_bwd_fn(h_f, h_b, ga, o_f, o_b, g, dmix, gain):
    _, vjp = jax.vjp(_post0_fn, h_f + h_b, ga, o_f + o_b, g, gain)
    return vjp((dmix,))


def _gla_pre_fn(q, lr, w_up, b_gate):
    outs = [q * (128.0 ** -0.5)]
    for d in range(2):
        z = _dot_nn(lr, w_up[d]) + b_gate[d:d + 1]
        outs.append(-_softplus(-z) * (1.0 / 16.0))
    return tuple(outs)


def _gla_post_fn(o, r, gain):
    return (_head_norm(o, gain, 4) * _silu(r),)


def _gla_post_fwd_fn(o_f, o_b, r, gain):
    return _gla_post_fn(o_f + o_b, r, gain)


def _gla_post_bwd_fn(o_f, o_b, r, dmix, gain):
    _, vjp = jax.vjp(_gla_post_fn, o_f + o_b, r, gain)
    return vjp((dmix,))


def _relu2_bwd_epi(acc, hid):
    return (acc * 2.0 * jnp.maximum(hid, 0.0),)


def _relu2(x):
    r = jnp.maximum(x, 0.0)
    return r * r


def _add_epi(acc, res):
    return (acc + res,)


def _loss_head_fn(h, target, gain):
    def f(h, gain):
        y = _rmsnorm_fn(h, gain)[0]
        err = y - target
        return 0.5 * jnp.sum(jnp.mean(err * err, axis=-1, keepdims=True))
    loss, (dh, dgain) = jax.value_and_grad(f, argnums=(0, 1))(h, gain)
    return dh, jnp.full((1, LANES), loss, F32), dgain


def _adam_fn(w, g, m, v):
    m2 = ADAM_B1 * m + (1.0 - ADAM_B1) * g
    v2 = ADAM_B2 * v + (1.0 - ADAM_B2) * (g * g)
    m_hat = m2 / (1.0 - ADAM_B1 ** ADAM_STEP)
    v_hat = v2 / (1.0 - ADAM_B2 ** ADAM_STEP)
    delta = -ADAM_LR * (m_hat / (jnp.sqrt(v_hat) + ADAM_EPS) + ADAM_WD * w)
    return delta, m2, v2


def _shifted(x, t_idx, off):
    n = x.shape[0]
    rolled = pltpu.roll(x, (-off) % n, 0)
    valid = (t_idx + off >= 0) & (t_idx + off < n)
    return jnp.where(valid, rolled, 0.0)


def _conv_fwd(name, src, colblock, w, b):
    n_rows, width = src.shape[0], w.shape[1]

    def body(x_ref, w_ref, b_ref, o_ref):
        x = x_ref[...]
        t_idx = lax.broadcasted_iota(jnp.int32, x.shape, 0)
        acc = b_ref[...] + w_ref[2:3, :] * x
        acc += w_ref[0:1, :] * _shifted(x, t_idx, -2)
        acc += w_ref[1:2, :] * _shifted(x, t_idx, -1)
        acc += w_ref[3:4, :] * _shifted(x, t_idx, 1)
        o_ref[...] = acc

    nb = width // LANES
    return pl.pallas_call(
        body, name=name, grid=(nb,),
        in_specs=[pl.BlockSpec((n_rows, LANES), lambda j: (0, colblock * nb + j)),
                  pl.BlockSpec((4, LANES), lambda j: (0, j)), pl.BlockSpec((1, LANES), lambda j: (0, j))],
        out_specs=pl.BlockSpec((n_rows, LANES), lambda j: (0, j)),
        out_shape=jax.ShapeDtypeStruct((n_rows, width), F32),
        compiler_params=_params(("parallel",)),
    )(src, w, b)


def _conv_bwd(name, src, colblock, w, d):
    n_rows, width = src.shape[0], w.shape[1]

    def body(x_ref, w_ref, d_ref, dx_ref, dw_ref, db_ref):
        x = x_ref[...]
        g = d_ref[...]
        t_idx = lax.broadcasted_iota(jnp.int32, x.shape, 0)
        dx = w_ref[2:3, :] * g
        dx += w_ref[0:1, :] * _shifted(g, t_idx, 2)
        dx += w_ref[1:2, :] * _shifted(g, t_idx, 1)
        dx += w_ref[3:4, :] * _shifted(g, t_idx, -1)
        dx_ref[...] = dx.astype(dx_ref.dtype)
        dw_ref[0:1, :] = jnp.sum(g * _shifted(x, t_idx, -2), axis=0, keepdims=True)
        dw_ref[1:2, :] = jnp.sum(g * _shifted(x, t_idx, -1), axis=0, keepdims=True)
        dw_ref[2:3, :] = jnp.sum(g * x, axis=0, keepdims=True)
        dw_ref[3:4, :] = jnp.sum(g * _shifted(x, t_idx, 1), axis=0, keepdims=True)
        db_ref[...] = jnp.sum(g, axis=0, keepdims=True)

    nb = width // LANES
    return pl.pallas_call(
        body, name=name, grid=(nb,),
        in_specs=[pl.BlockSpec((n_rows, LANES), lambda j: (0, colblock * nb + j)),
                  pl.BlockSpec((4, LANES), lambda j: (0, j)),
                  pl.BlockSpec((n_rows, LANES), lambda j: (0, j))],
        out_specs=[pl.BlockSpec((n_rows, LANES), lambda j: (0, j)), pl.BlockSpec((4, LANES), lambda j: (0, j)),
                   pl.BlockSpec((1, LANES), lambda j: (0, j))],
        out_shape=[jax.ShapeDtypeStruct((n_rows, width), BF16), jax.ShapeDtypeStruct((4, width), F32),
                   jax.ShapeDtypeStruct((1, width), F32)],
        compiler_params=_params(("parallel",)),
    )(src, w, d)


_WHOLE = pl.BlockSpec(memory_space=pltpu.VMEM)
SCAN_UNROLL = 8


def _scan_fwd(name, a, u, reverse):
    n_rows, width = a.shape

    def body(a_ref, u_ref, h_ref):
        def step(i, h):
            t = (n_rows - 1 - i) if reverse else i
            h = a_ref[pl.ds(t, 1), :] * h + u_ref[pl.ds(t, 1), :]
            h_ref[pl.ds(t, 1), :] = h
            return h
        lax.fori_loop(0, n_rows, step, jnp.zeros((1, width), F32), unroll=SCAN_UNROLL)

    return pl.pallas_call(
        body, name=name, in_specs=[_WHOLE, _WHOLE], out_specs=_WHOLE,
        out_shape=jax.ShapeDtypeStruct((n_rows, width), F32),
        compiler_params=pltpu.CompilerParams(vmem_limit_bytes=VMEM_LIMIT),
    )(a, u)


def _scan_bwd(name, a, h, dh, reverse):
    n_rows, width = a.shape

    def body(a_ref, h_ref, dh_ref, du_ref, da_ref):
        def step(i, carry):
            t = i if reverse else (n_rows - 1 - i)
            g = dh_ref[pl.ds(t, 1), :] + carry
            du_ref[pl.ds(t, 1), :] = g
            tp = t + 1 if reverse else t - 1
            valid = (tp >= 0) & (tp < n_rows)
            h_prev = h_ref[pl.ds(jnp.clip(tp, 0, n_rows - 1), 1), :]
            da_ref[pl.ds(t, 1), :] = jnp.where(valid, g * h_prev, 0.0)
            return a_ref[pl.ds(t, 1), :] * g
        lax.fori_loop(0, n_rows, step, jnp.zeros((1, width), F32), unroll=SCAN_UNROLL)

    return pl.pallas_call(
        body, name=name, in_specs=[_WHOLE] * 3, out_specs=[_WHOLE] * 2,
        out_shape=[jax.ShapeDtypeStruct((n_rows, width), F32)] * 2,
        compiler_params=pltpu.CompilerParams(vmem_limit_bytes=VMEM_LIMIT),
    )(a, h, dh)


def _tri_mask(c, reverse):
    row = lax.broadcasted_iota(jnp.int32, (c, c), 0)
    col = lax.broadcasted_iota(jnp.int32, (c, c), 1)
    return (col >= row) if reverse else (col <= row)


def _cumsum_rows(x, reverse):
    tri = _tri_mask(x.shape[0], reverse).astype(BF16)
    hi = x.astype(BF16)
    rest = x - hi.astype(F32)
    mid = rest.astype(BF16)
    lo = (rest - mid.astype(F32)).astype(BF16)
    return _raw_nn(tri, hi) + _raw_nn(tri, mid) + _raw_nn(tri, lo)


@functools.partial(jax.custom_vjp, nondiff_argnums=(1,))
def _cumsum(x, reverse):
    return _cumsum_rows(x, reverse)


def _cumsum_fwd(x, reverse):
    return _cumsum_rows(x, reverse), None


def _cumsum_bwd(reverse, _, g):
    return (_cumsum_rows(g, not reverse),)


_cumsum.defvjp(_cumsum_fwd, _cumsum_bwd)


def _chunks_fn(qs, ks, vs, lfs, sts, reverses):
    n, c = len(qs), qs[0].shape[0]
    every = range(n)
    tris = [_tri_mask(c, r) for r in reverses]
    cums = [_cumsum(lfs[i], reverses[i]) for i in every]
    rid = lax.broadcasted_iota(jnp.int32, cums[0].shape, 0)

    def pick(cum, r):
        return jnp.sum(jnp.where(rid == r, cum, 0.0), axis=0, keepdims=True)

    refs = [pick(cums[i], (c - 1 - c // 2) if reverses[i] else c // 2) for i in every]
    lasts = [pick(cums[i], 0 if reverses[i] else c - 1) for i in every]
    q_in = [qs[i] * jnp.exp(cums[i] - refs[i]) for i in every]
    k_in = [ks[i] * jnp.exp(refs[i] - cums[i]) for i in every]
    scores = [jnp.where(tris[i], _dot_nt(q_in[i], k_in[i]), 0.0) for i in every]
    o_intra = [_dot_nn(scores[i], vs[i]) for i in every]
    q_out = [qs[i] * jnp.exp(cums[i]) for i in every]
    o_inter = [_dot_nt(q_out[i], sts[i]) for i in every]
    k_state = [ks[i] * jnp.exp(lasts[i] - cums[i]) for i in every]
    upd = [_dot_tn(vs[i], k_state[i]) for i in every]
    st_new = [sts[i] * jnp.exp(lasts[i]) + upd[i] for i in every]
    return [o_intra[i] + o_inter[i] for i in every], st_new


def _attn_fwd(name, q, k_f, k_b, v, lf_f, lf_b, n_heads, dk, dv):
    n_rows = q[0].shape[0]
    n_chunks = n_rows // CHUNK
    wk, wv = n_heads * dk, n_heads * dv

    def spec(width, off, rev):
        return pl.BlockSpec((CHUNK, width), lambda n: ((n_chunks - 1 - n) if rev else n, off))

    def sspec(rev):
        return pl.BlockSpec((None, n_heads, dv, dk), lambda n: ((n_chunks - 1 - n) if rev else n, 0, 0, 0))

    def body(qf, kf, vf, lff, qb, kb, vb, lfb, of_ref, ob_ref, sf_ref, sb_ref, st):
        @pl.when(pl.program_id(0) == 0)
        def _():
            st[...] = jnp.zeros_like(st)

        ins = ((qf, kf, vf, lff), (qb, kb, vb, lfb))
        chains = [(d, h) for d in range(2) for h in range(n_heads)]
        ck = [slice(h * dk, (h + 1) * dk) for h in range(n_heads)]
        cv = [slice(h * dv, (h + 1) * dv) for h in range(n_heads)]
        qs = [ins[d][0][:, ck[h]] for d, h in chains]
        ks = [ins[d][1][:, ck[h]] for d, h in chains]
        vs = [ins[d][2][:, cv[h]] for d, h in chains]
        lfs = [ins[d][3][:, ck[h]] for d, h in chains]
        sts = [st[d, h] for d, h in chains]
        os_, st_new = _chunks_fn(qs, ks, vs, lfs, sts, [d == 1 for d, _ in chains])
        for i, (d, h) in enumerate(chains):
            (sf_ref, sb_ref)[d][h] = sts[i]
            (of_ref, ob_ref)[d][:, cv[h]] = os_[i]
            st[d, h] = st_new[i]

    in_specs = [spec(wk, q[1], False), spec(wk, k_f[1], False), spec(wv, v[1], False), spec(wk, lf_f[1], False),
                spec(wk, q[1], True), spec(wk, k_b[1], True), spec(wv, v[1], True), spec(wk, lf_b[1], True)]
    return pl.pallas_call(
        body, name=name, grid=(n_chunks,), in_specs=in_specs,
        out_specs=[spec(wv, 0, False), spec(wv, 0, True), sspec(False), sspec(True)],
        out_shape=[jax.ShapeDtypeStruct((n_rows, wv), F32)] * 2
        + [jax.ShapeDtypeStruct((n_chunks, n_heads, dv, dk), F32)] * 2,
        scratch_shapes=[pltpu.VMEM((2, n_heads, dv, dk), F32)],
        compiler_params=_params(("arbitrary",)),
    )(q[0], k_f[0], v[0], lf_f[0], q[0], k_b[0], v[0], lf_b[0])


def _attn_bwd(name, q, k_f, k_b, v, lf_f, lf_b, st_f, st_b, do, n_heads, dk, dv, out_dtype=F32):
    n_rows = q[0].shape[0]
    n_chunks = n_rows // CHUNK
    wk, wv = n_heads * dk, n_heads * dv

    def spec(width, off, rev):
        return pl.BlockSpec((CHUNK, width), lambda n: (n if rev else (n_chunks - 1 - n), off))

    def sspec(rev):
        return pl.BlockSpec((None, n_heads, dv, dk), lambda n: (n if rev else (n_chunks - 1 - n), 0, 0, 0))

    def body(qf, kf, vf, lff, sf, dof, qb, kb, vb, lfb, sb, dob,
             dqf, dkf, dvf, dlff, dqb, dkb, dvb, dlfb, dst):
        @pl.when(pl.program_id(0) == 0)
        def _():
            dst[...] = jnp.zeros_like(dst)

        ins = ((qf, kf, vf, lff, sf, dof), (qb, kb, vb, lfb, sb, dob))
        outs = ((dqf, dkf, dvf, dlff), (dqb, dkb, dvb, dlfb))
        chains = [(d, h) for d in range(2) for h in range(n_heads)]
        ck = [slice(h * dk, (h + 1) * dk) for h in range(n_heads)]
        cv = [slice(h * dv, (h + 1) * dv) for h in range(n_heads)]
        qs = [ins[d][0][:, ck[h]] for d, h in chains]
        ks = [ins[d][1][:, ck[h]] for d, h in chains]
        vs = [ins[d][2][:, cv[h]] for d, h in chains]
        lfs = [ins[d][3][:, ck[h]] for d, h in chains]
        sts = [ins[d][4][h] for d, h in chains]
        dos = [ins[d][5][:, cv[h]] for d, h in chains]
        dsts = [dst[d, h] for d, h in chains]
        fn = functools.partial(_chunks_fn, reverses=[d == 1 for d, _ in chains])
        _, vjp = jax.vjp(fn, qs, ks, vs, lfs, sts)
        dqs, dks, dvs, dlfs, dst_prev = vjp((dos, dsts))
        for i, (d, h) in enumerate(chains):
            dq_r, dk_r, dv_r, dlf_r = outs[d]
            dq_r[:, ck[h]] = dqs[i].astype(dq_r.dtype)
            dk_r[:, ck[h]] = dks[i].astype(dk_r.dtype)
            dv_r[:, cv[h]] = dvs[i].astype(dv_r.dtype)
            dlf_r[:, ck[h]] = dlfs[i].astype(dlf_r.dtype)
            dst[d, h] = dst_prev[i]

    def dir_specs(kk, lf, rev):
        return [spec(wk, q[1], rev), spec(wk, kk[1], rev), spec(wv, v[1], rev), spec(wk, lf[1], rev), sspec(rev),
                spec(wv, 0, rev)]

    def dir_out_specs(rev):
        return [spec(wk, 0, rev), spec(wk, 0, rev), spec(wv, 0, rev), spec(wk, 0, rev)]

    shapes = [jax.ShapeDtypeStruct((n_rows, wk), out_dtype), jax.ShapeDtypeStruct((n_rows, wk), out_dtype),
              jax.ShapeDtypeStruct((n_rows, wv), out_dtype), jax.ShapeDtypeStruct((n_rows, wk), F32)]
    outs = pl.pallas_call(
        body, name=name, grid=(n_chunks,), in_specs=dir_specs(k_f, lf_f, False) + dir_specs(k_b, lf_b, True),
        out_specs=dir_out_specs(False) + dir_out_specs(True), out_shape=shapes + shapes,
        scratch_shapes=[pltpu.VMEM((2, n_heads, dv, dk), F32)],
        compiler_params=_params(("arbitrary",)),
    )(q[0], k_f[0], v[0], lf_f[0], st_f, do, q[0], k_b[0], v[0], lf_b[0], st_b, do)
    return outs[:4], outs[4:]


def _row2(v):
    return v.reshape(1, -1)


def _mlp_fwd(tag, h, gain, w1, w2):
    y = _rowcall(f"{tag}_norm", _rmsnorm_fn, [(h, h.shape[1], 0)], [gain], [(h.shape[1], BF16)], tm=512)[0]
    hid = _mm(f"{tag}_up", y, w1, out_dtypes=(BF16,))
    h_out = _mm(f"{tag}_down", hid, w2, a_pro=_relu2, extras=(h,), epi=_add_epi)
    return h_out, (y, hid)


def _mlp_bwd(tag, h, gain, w1, w2, saved, dh_out):
    y, hid = saved
    dhid = _mm(f"{tag}_dact", dh_out, w2, mode="nt", extras=(hid,), epi=_relu2_bwd_epi, out_dtypes=(BF16,))
    dw2 = _mm(f"{tag}_dw2", hid, dh_out, mode="tn", a_pro=_relu2)
    dw1 = _mm(f"{tag}_dw1", y, dhid, mode="tn", out_split=N_CHIPS)
    dy = _mm(f"{tag}_dy", dhid, w1, mode="nt")
    dh, dgain = _norm_bwd(f"{tag}_dnorm", h, gain, dy, dh_out)
    return dh, dgain, dw1, dw2


def _norm_bwd(name, h, gain, dy, dres):
    d = h.shape[1]

    def fn(h, dy, dres, gain):
        _, vjp = jax.vjp(lambda a, b: _rmsnorm_fn(a, b)[0], h, gain)
        dh, dgain = vjp(dy)
        return dh + dres, dgain

    dh, dgain = _rowcall(name, fn, [(h, d, 0), (dy, d, 0), (dres, d, 0)], [gain], [(d, F32)], [(1, d)], tm=512)
    return dh, dgain


def _local_step(x, target, w):
    g = {}
    d_model = x.shape[1]
    rg_w = hg_w = d_model // 2

    h_a0 = x
    gain = _row2(w["norm_mix"][0])
    y0 = _rowcall("l0_norm", _rmsnorm_fn, [(h_a0, d_model, 0)], [gain], [(d_model, BF16)], tm=512)[0]
    proj0 = _mm("l0_in", y0, w["ab_w_in"])
    conv_w, conv_b = w["rg_conv_w"], _row2(w["rg_conv_b"])
    xc = _conv_fwd("rg_conv", proj0, 0, conv_w, conv_b)
    gate_pars = [w["rg_wa_bd"], w["rg_wx_bd"], w["rg_b_a"], w["rg_b_x"], w["rg_lambda"]]
    a_f, u_f, a_b, u_b = _rowcall("rg_gates", _rg_gates_fn, [(xc, rg_w, 0)], gate_pars, [(rg_w, F32)] * 4)
    hs_f = _scan_fwd("rg_scan_f", a_f, u_f, False)
    hs_b = _scan_fwd("rg_scan_b", a_b, u_b, True)
    hg_rows = [(proj0, hg_w, 2), (proj0, hg_w, 3), (proj0, hg_w, 4)]
    qh, k_f, lf_f, k_b, lf_b = _rowcall("hg_pre", _hg_pre_fn, hg_rows, [w["hg_lb_logits"]], [(hg_w, F32)] * 5)
    iv = (proj0, 5)
    o_f, o_b, st_f, st_b = _attn_fwd("hg_attn", (qh, 0), (k_f, 0), (k_b, 0), iv, (lf_f, 0), (lf_b, 0), 4, 128, 128)
    post0_rows = [(hs_f, rg_w, 0), (hs_b, rg_w, 0), (proj0, rg_w, 1), (o_f, hg_w, 0), (o_b, hg_w, 0), (proj0, hg_w, 6)]
    hg_gain = _row2(w["hg_norm"])
    mix_in0 = _rowcall("l0_post", _post0_fwd_fn, post0_rows, [hg_gain], [(d_model, BF16)])[0]
    h_b0 = _mm("l0_out", mix_in0, w["ab_w_out"], extras=(h_a0,), epi=_add_epi)
    h_c0, mlp0 = _mlp_fwd("mlp0", h_b0, _row2(w["norm_mlp"][0]), w["mlp_w1"][0], w["mlp_w2"][0])

    h_a1 = h_c0
    gain1 = _row2(w["norm_mix"][1])
    y1 = _rowcall("l1_norm", _rmsnorm_fn, [(h_a1, d_model, 0)], [gain1], [(d_model, BF16)], tm=512)[0]
    proj1 = _mm("l1_in", y1, w["gla_w_in_pad"], tn=640)
    gla_pars = [w["gla_w_up_pad"], w["gla_b_gate"]]
    gq, glf_f, glf_b = _rowcall("gla_pre", _gla_pre_fn, [(proj1, 512, 0), (proj1, LANES, 24)], gla_pars, [(512, F32)] * 3)
    gk, gv = (proj1, 1), (proj1, 1)
    go_f, go_b, gst_f, gst_b = _attn_fwd("gla_attn", (gq, 0), gk, gk, gv, (glf_f, 0), (glf_b, 0), 4, 128, 256)
    gla_gain = _row2(w["gla_norm"])
    post1_rows = [(go_f, d_model, 0), (go_b, d_model, 0), (proj1, d_model, 2)]
    mix_in1 = _rowcall("l1_post", _gla_post_fwd_fn, post1_rows, [gla_gain], [(d_model, BF16)])[0]
    h_b1 = _mm("l1_out", mix_in1, w["gla_w_out"], extras=(h_a1,), epi=_add_epi)
    h_c1, mlp1 = _mlp_fwd("mlp1", h_b1, _row2(w["norm_mlp"][1]), w["mlp_w1"][1], w["mlp_w2"][1])

    dh, loss, g["norm_final"] = _rowcall(
        "loss_head", _loss_head_fn, [(h_c1, d_model, 0), (target, d_model, 0)], [_row2(w["norm_final"])],
        [(d_model, F32)], [(1, LANES), (1, d_model)], tm=512)

    dh, g_nmlp1, g_w1_1, g_w2_1 = _mlp_bwd("mlp1", h_b1, _row2(w["norm_mlp"][1]), w["mlp_w1"][1], w["mlp_w2"][1], mlp1, dh)
    dmix1 = _mm("l1_dout", dh, w["gla_w_out"], mode="nt")
    g["gla_w_out"] = _mm("l1_dwout", mix_in1, dh, mode="tn")
    dgo, dr, g["gla_norm"] = _rowcall(
        "l1_dpost", _gla_post_bwd_fn, post1_rows + [(dmix1, d_model, 0)], [gla_gain],
        [(d_model, F32), (d_model, BF16)], [(1, d_model)])
    (dq_f, dk_f, dv_f, dlf_f), (dq_b, dk_b, dv_b, dlf_b) = _attn_bwd(
        "gla_dattn", (gq, 0), gk, gk, gv, (glf_f, 0), (glf_b, 0), gst_f, gst_b, dgo, 4, 128, 256)

    def gla_pre_bwd(q, lr, dq1, dq2, dlf1, dlf2, dk1, dk2, dv1, dv2, w_up, b_gate):
        dlr = jnp.zeros_like(lr)
        dws, dbs = [], []
        for d, dlf in enumerate((dlf1, dlf2)):
            z = _raw_nn(lr, w_up[d]) + b_gate[d:d + 1]
            dz = dlf * _sigmoid(-z) * (1.0 / 16.0)
            dlr = dlr + _raw_nt(dz, w_up[d])
            dws.append(_raw_tn(dz, lr))
            dbs.append(jnp.sum(dz, axis=0, keepdims=True))
        return ((dq1 + dq2) * (128.0 ** -0.5), dk1 + dk2, dv1 + dv2, dlr, dws[0], dws[1], dbs[0], dbs[1])

    rows = [(proj1, 512, 0), (proj1, LANES, 24), (dq_f, 512, 0), (dq_b, 512, 0), (dlf_f, 512, 0), (dlf_b, 512, 0),
            (dk_f, 512, 0), (dk_b, 512, 0), (dv_f, d_model, 0), (dv_b, d_model, 0)]
    dq, dk, dv, dlr, dwt_f, dwt_b, db_f, db_b = _rowcall(
        "gla_dpre", gla_pre_bwd, rows, gla_pars, [(512, BF16), (512, BF16), (d_model, BF16), (LANES, BF16)],
        [(512, LANES), (512, LANES), (1, 512), (1, 512)])
    g["gla_w_up_pad"] = jnp.stack([dwt_f.T, dwt_b.T])
    g["gla_b_gate"] = jnp.concatenate([db_f, db_b], axis=0)
    dproj1 = jnp.concatenate([dq, dk, dv, dr, dlr], axis=1)
    g["gla_w_in_pad"] = _mm("l1_dwin", y1, dproj1, mode="tn", tn=640)
    dy1 = _mm("l1_dy", dproj1, w["gla_w_in_pad"], mode="nt", tk=640)
    dh, g_nmix1 = _norm_bwd("l1_dnorm", h_a1, gain1, dy1, dh)

    dh, g_nmlp0, g_w1_0, g_w2_0 = _mlp_bwd("mlp0", h_b0, _row2(w["norm_mlp"][0]), w["mlp_w1"][0], w["mlp_w2"][0], mlp0, dh)
    dmix0 = _mm("l0_dout", dh, w["ab_w_out"], mode="nt")
    g["ab_w_out"] = _mm("l0_dwout", mix_in0, dh, mode="tn")
    dhs, dga, do, dg, g["hg_norm"] = _rowcall(
        "l0_dpost", _post0_bwd_fn, post0_rows + [(dmix0, d_model, 0)], [hg_gain],
        [(rg_w, F32), (rg_w, BF16), (hg_w, F32), (hg_w, BF16)], [(1, hg_w)])
    (dqh_f, dk_f, div_f, dlf_f), (dqh_b, dk_b, div_b, dlf_b) = _attn_bwd(
        "hg_dattn", (qh, 0), (k_f, 0), (k_b, 0), iv, (lf_f, 0), (lf_b, 0), st_f, st_b, do, 4, 128, 128)

    def hg_pre_bwd(q, f_f, f_b, dq1, dq2, dk1, dlf1, dk2, dlf2, dv1, dv2, logits):
        _, vjp = jax.vjp(_hg_pre_fn, q, f_f, f_b, logits)
        dq, df_f, df_b, dlogits = vjp((dq1 + dq2, dk1, dlf1, dk2, dlf2))
        return dq, df_f, df_b, dv1 + dv2, dlogits

    rows = hg_rows + [(t, hg_w, 0) for t in (dqh_f, dqh_b, dk_f, dlf_f, dk_b, dlf_b, div_f, div_b)]
    dq, df_f, df_b, div, g["hg_lb_logits"] = _rowcall(
        "hg_dpre", hg_pre_bwd, rows, [w["hg_lb_logits"]], [(hg_w, BF16)] * 4, [(2, hg_w)])
    du_f, da_f = _scan_bwd("rg_dscan_f", a_f, hs_f, dhs, False)
    du_b, da_b = _scan_bwd("rg_dscan_b", a_b, hs_b, dhs, True)
    gates_bwd = _vjp_of(_rg_gates_fn, 1, 4, 5)
    rows = [(xc, rg_w, 0), (da_f, rg_w, 0), (du_f, rg_w, 0), (da_b, rg_w, 0), (du_b, rg_w, 0)]
    dxc, g["rg_wa_bd"], g["rg_wx_bd"], g["rg_b_a"], g["rg_b_x"], g["rg_lambda"] = _rowcall(
        "rg_dgates", gates_bwd, rows, gate_pars, [(rg_w, F32)],
        [(2, rg_w, rg_w), (2, rg_w, rg_w), (2, rg_w), (2, rg_w), (2, rg_w)])
    dxa, g["rg_conv_w"], g["rg_conv_b"] = _conv_bwd("rg_dconv", proj0, 0, conv_w, dxc)
    dproj0 = jnp.concatenate([dxa, dga, dq, df_f, df_b, div, dg], axis=1)
    g["ab_w_in"] = _mm("l0_dwin", y0, dproj0, mode="tn", out_split=N_CHIPS)
    dy0 = _mm("l0_dy", dproj0, w["ab_w_in"], mode="nt")
    grad_x, g_nmix0 = _norm_bwd("l0_dnorm", h_a0, gain, dy0, dh)

    g["norm_mix"] = jnp.concatenate([g_nmix0, g_nmix1], axis=0)
    g["norm_mlp"] = jnp.concatenate([g_nmlp0, g_nmlp1], axis=0)
    g["mlp_w1"] = [g_w1_0, g_w1_1]
    g["mlp_w2"] = [g_w2_0, g_w2_1]
    return loss, grad_x, g


def _block_diag(w):
    d, g, n, _ = w.shape
    eye = jnp.eye(g, dtype=w.dtype)
    return (w[:, :, :, None, :] * eye[None, :, None, :, None]).reshape(d, g * n, g * n)


def _block_diag_extract(wbd, g):
    d, gn, _ = wbd.shape
    n = gn // g
    blocks = wbd.reshape(d, g, n, g, n)
    return jnp.stack([blocks[:, i, :, i, :] for i in range(g)], axis=1)


def _prepare_weights(big, full):
    w = {k: full[k] for k in ("norm_mix", "norm_mlp", "norm_final", "hg_lb_logits")}
    for k in ("rg_conv_w", "rg_conv_b", "rg_b_a", "rg_b_x", "rg_lambda", "hg_norm", "gla_b_gate", "gla_norm"):
        w[k] = full[k][0]
    w["rg_wa_bd"] = _block_diag(full["rg_w_a"][0])
    w["rg_wx_bd"] = _block_diag(full["rg_w_x"][0])
    up = full["gla_w_gate_up"][0]
    rank = up.shape[1]
    pad = jnp.zeros((2, LANES, up.shape[2]), F32)
    w["gla_w_up_pad"] = pad.at[0, 0:rank].set(up[0]).at[1, rank:2 * rank].set(up[1])
    w["mlp_w1"] = list(big["mlp_w1"])
    w["mlp_w2"] = [t.reshape(-1, t.shape[-1]) for t in big["mlp_w2"]]
    w["ab_w_in"] = big["ab_w_in"]
    w["ab_w_out"] = big["ab_w_out"].reshape(-1, big["ab_w_out"].shape[-1])
    w["gla_w_out"] = big["gla_w_out"].reshape(-1, big["gla_w_out"].shape[-1])
    gla_in = _join_chips(big["gla_w_in"], 1)
    w["gla_w_in_pad"] = jnp.pad(gla_in, ((0, 0), (0, GLA_IN_PAD - gla_in.shape[1])))
    return w


def _finish_grads(g, rank=16, gla_in_width=3104, rg_blocks=8):
    def chip_major(t):
        return t.reshape(N_CHIPS, t.shape[0] // N_CHIPS, t.shape[1])

    big = {
        "mlp_w1": list(g["mlp_w1"]), "mlp_w2": [chip_major(t) for t in g["mlp_w2"]],
        "ab_w_in": g["ab_w_in"], "ab_w_out": chip_major(g["ab_w_out"]),
        "gla_w_in": _split_chips(g["gla_w_in_pad"][:, :gla_in_width], 1), "gla_w_out": chip_major(g["gla_w_out"]),
    }
    small = {
        "norm_mix": g["norm_mix"], "norm_mlp": g["norm_mlp"], "norm_final": g["norm_final"][0],
        "rg_conv_w": g["rg_conv_w"][None], "rg_conv_b": g["rg_conv_b"],
        "rg_w_a": _block_diag_extract(g["rg_wa_bd"], rg_blocks)[None], "rg_b_a": g["rg_b_a"][None],
        "rg_w_x": _block_diag_extract(g["rg_wx_bd"], rg_blocks)[None], "rg_b_x": g["rg_b_x"][None],
        "rg_lambda": g["rg_lambda"][None], "hg_lb_logits": g["hg_lb_logits"], "hg_norm": g["hg_norm"],
        "gla_w_gate_up": jnp.stack([g["gla_w_up_pad"][0, 0:rank], g["gla_w_up_pad"][1, rank:2 * rank]])[None],
        "gla_b_gate": g["gla_b_gate"][None], "gla_norm": g["gla_norm"],
    }
    return big, small


MATRICES = (("mlp_w1", 0), ("mlp_w1", 1), ("mlp_w2", 0), ("mlp_w2", 1), ("ab_w_in", 0), ("ab_w_out", 0),
            ("gla_w_in", 0), ("gla_w_out", 0))
SMALL_SHARDED = ("rg_conv_w", "rg_b_a", "rg_b_x", "rg_lambda", "gla_w_gate_up", "gla_b_gate", "gla_norm")
SMALL_REPLICATED = ("norm_mix", "norm_mlp", "norm_final", "rg_conv_b", "rg_w_a", "rg_w_x", "hg_lb_logits", "hg_norm")
WEIGHTS = ("norm_mix", "norm_mlp", "norm_final", "mlp_w1", "mlp_w2", "ab_w_in", "ab_w_out", "rg_conv_w", "rg_conv_b",
           "rg_w_a", "rg_b_a", "rg_w_x", "rg_b_x", "rg_lambda", "hg_lb_logits", "hg_norm", "gla_w_in", "gla_w_out",
           "gla_w_gate_up", "gla_b_gate", "gla_norm")
ROW_ALIGN = 16


def _pack(arrays, lead=0):
    head = arrays[0].shape[:lead]
    flat = jnp.concatenate([a.reshape(head + (-1,)) for a in arrays], axis=lead)
    n = flat.shape[-1]
    quantum = LANES * ROW_ALIGN
    padded = -(-n // quantum) * quantum
    if padded != n:
        flat = jnp.pad(flat, [(0, 0)] * lead + [(0, padded - n)])
    return flat.reshape(head + (padded // LANES, LANES))


def _unpack(buf, shapes, lead=0):
    head = buf.shape[:lead]
    flat = buf.reshape(head + (-1,))
    out, off = [], 0
    for s in shapes:
        n = 1
        for v in s:
            n *= v
        out.append(lax.slice_in_dim(flat, off, off + n, axis=lead).reshape(head + tuple(s)))
        off += n
    return out


def _join_chips(gathered, axis):
    t = jnp.moveaxis(gathered, 0, axis)
    return t.reshape(t.shape[:axis] + (t.shape[axis] * t.shape[axis + 1],) + t.shape[axis + 2:])


def _split_chips(full, axis):
    s = full.shape
    t = full.reshape(s[:axis] + (N_CHIPS, s[axis] // N_CHIPS) + s[axis + 1:])
    return jnp.moveaxis(t, axis, 0)


_ANY = pl.BlockSpec(memory_space=pl.ANY)


def _place():
    return lax.axis_index("x"), lax.axis_index("y"), lax.axis_index("c")


def _into_slot(name, src, slot, n_slots, dtype, tm, layer=None):
    r, lanes = src.shape[-2:]
    tm = _row_tile(r, tm, ROW_ALIGN)

    def body(slot_ref, in_ref, o_ref):
        o_ref[...] = in_ref[...].astype(o_ref.dtype)

    if layer is None:
        in_spec = pl.BlockSpec((tm, lanes), lambda i, slot_ref: (i, 0))
    else:
        in_spec = pl.BlockSpec((None, tm, lanes), lambda i, slot_ref: (layer, i, 0))
    grid_spec = pltpu.PrefetchScalarGridSpec(
        num_scalar_prefetch=1, grid=(r // tm,), in_specs=[in_spec],
        out_specs=pl.BlockSpec((None, tm, lanes), lambda i, slot_ref: (slot_ref[0], i, 0)))
    return pl.pallas_call(
        body, name=name, grid_spec=grid_spec, out_shape=jax.ShapeDtypeStruct((n_slots, r, lanes), dtype),
        compiler_params=_params(("parallel",)),
    )(slot.reshape(1).astype(jnp.int32), src)


def _chip_peers():
    x, y, c = _place()
    return 2 * x + y, c, [(1 - x, y), (x, 1 - y), (1 - x, 1 - y)]


def _comm_call(name, body, ins, out_shapes, n_sems, aliases=None):
    return pl.pallas_call(
        body, name=name, in_specs=[_ANY] * len(ins), out_specs=[_ANY] * len(out_shapes), out_shape=out_shapes,
        input_output_aliases=aliases or {},
        scratch_shapes=[pltpu.SemaphoreType.DMA((n_sems,)), pltpu.SemaphoreType.DMA((n_sems,))],
    )(*ins)


def _gather_chips(name, bufs):
    n = len(bufs)

    def body(*refs):
        ins, outs, send_sems, recv_sems = refs[:n], refs[n:2 * n], refs[2 * n], refs[2 * n + 1]
        me, c, peers = _chip_peers()

        def copy(a, j, block):
            px, py = peers[j]
            return pltpu.make_async_remote_copy(
                src_ref=ins[a].at[block], dst_ref=outs[a].at[block], send_sem=send_sems.at[3 * a + j],
                recv_sem=recv_sems.at[3 * a + j], device_id=(px, py, c), device_id_type=MESH)

        sends = [copy(a, j, me) for a in range(n) for j in range(3)]
        for cp in sends:
            cp.start()
        for a in range(n):
            for j, (px, py) in enumerate(peers):
                copy(a, j, 2 * px + py).wait_recv()
        for cp in sends:
            cp.wait_send()

    shapes = [jax.ShapeDtypeStruct(b.shape, b.dtype) for b in bufs]
    return _comm_call(name, body, bufs, shapes, 3 * n, {a: a for a in range(n)})


def _pair_exchange(name, gs):
    n = len(gs)

    def body(*refs):
        ins, outs, send_sems, recv_sems = refs[:n], refs[n:2 * n], refs[2 * n], refs[2 * n + 1]
        x, y, c = _place()
        copies = [pltpu.make_async_remote_copy(
            src_ref=ins[a].at[:, 1 - c], dst_ref=outs[a], send_sem=send_sems.at[a], recv_sem=recv_sems.at[a],
            device_id=(x, y, 1 - c), device_id_type=MESH) for a in range(n)]
        for cp in copies:
            cp.start()
        for cp in copies:
            cp.wait()

    shapes = [jax.ShapeDtypeStruct((g.shape[0],) + g.shape[2:], g.dtype) for g in gs]
    return _comm_call(name, body, gs, shapes, n)


def _pair_add(name, g, got, c):
    n, _, rh, lanes = g.shape
    tm = _row_tile(rh, 2048, 8)

    def body(c_ref, g_ref, got_ref, o_ref):
        o_ref[...] = g_ref[...] + got_ref[...]

    grid_spec = pltpu.PrefetchScalarGridSpec(
        num_scalar_prefetch=1, grid=(n, rh // tm),
        in_specs=[pl.BlockSpec((None, None, tm, lanes), lambda s, i, c_ref: (s, c_ref[0], i, 0)),
                  pl.BlockSpec((None, tm, lanes), lambda s, i, c_ref: (s, i, 0))],
        out_specs=pl.BlockSpec((None, tm, lanes), lambda s, i, c_ref: (s, i, 0)))
    return pl.pallas_call(
        body, name=name, grid_spec=grid_spec, out_shape=jax.ShapeDtypeStruct((n, rh, lanes), F32),
        compiler_params=_params(("parallel", "parallel")),
    )(c.reshape(1).astype(jnp.int32), g, got)


def _chip_scatter(name, ps):
    n = len(ps)

    def body(*refs):
        ins, outs, send_sems, recv_sems = refs[:n], refs[n:2 * n], refs[2 * n], refs[2 * n + 1]
        me, c, peers = _chip_peers()

        def copy(a, j, src_block, dst_block):
            px, py = peers[j]
            return pltpu.make_async_remote_copy(
                src_ref=ins[a].at[src_block], dst_ref=outs[a].at[dst_block], send_sem=send_sems.at[3 * a + j],
                recv_sem=recv_sems.at[3 * a + j], device_id=(px, py, c), device_id_type=MESH)

        sends = [copy(a, j, 2 * px + py, me) for a in range(n) for j, (px, py) in enumerate(peers)]
        for cp in sends:
            cp.start()
        for a in range(n):
            for j, (px, py) in enumerate(peers):
                copy(a, j, me, 2 * px + py).wait_recv()
        for cp in sends:
            cp.wait_send()

    shapes = [jax.ShapeDtypeStruct(p.shape, p.dtype) for p in ps]
    return _comm_call(name, body, ps, shapes, 3 * n)


def _sum_ring(name, own, got, chip, core):
    n, rh, lanes = own.shape
    tm = _row_tile(rh, 2048, 8)

    def body(idx_ref, own_ref, g1_ref, g2_ref, g3_ref, o_ref):
        o_ref[...] = ((own_ref[...] + g1_ref[...]) + g2_ref[...]) + g3_ref[...]

    def block(k):
        return pl.BlockSpec((None, tm, lanes), lambda i, idx_ref: ((idx_ref[0] + k) % n, i, 0))

    grid_spec = pltpu.PrefetchScalarGridSpec(
        num_scalar_prefetch=1, grid=(rh // tm,), in_specs=[block(0), block(1), block(2), block(3)],
        out_specs=pl.BlockSpec((None, tm, lanes), lambda i, idx_ref: (idx_ref[1], i, 0)))
    return pl.pallas_call(
        body, name=name, grid_spec=grid_spec, out_shape=jax.ShapeDtypeStruct((2, rh, lanes), F32),
        compiler_params=_params(("parallel",)),
    )(jnp.stack([chip, core]).astype(jnp.int32), own, got, got, got)


def _pair_gather(name, bufs):
    n = len(bufs)

    def body(*refs):
        ins, outs, send_sems, recv_sems = refs[:n], refs[n:2 * n], refs[2 * n], refs[2 * n + 1]
        x, y, c = _place()

        def copy(a, block):
            return pltpu.make_async_remote_copy(
                src_ref=ins[a].at[block], dst_ref=outs[a].at[block], send_sem=send_sems.at[a],
                recv_sem=recv_sems.at[a], device_id=(x, y, 1 - c), device_id_type=MESH)

        sends = [copy(a, c) for a in range(n)]
        for cp in sends:
            cp.start()
        for a in range(n):
            copy(a, 1 - c).wait_recv()
        for cp in sends:
            cp.wait_send()

    shapes = [jax.ShapeDtypeStruct(b.shape, b.dtype) for b in bufs]
    return _comm_call(name, body, bufs, shapes, n, {a: a for a in range(n)})


def _gather_all(name, s):
    def body(in_ref, out_ref, send_sems, recv_sems, local_sem):
        x, y, c = _place()
        me = 4 * x + 2 * y + c
        peers = []
        for mask in range(1, N_DEV):
            fx, fy, fc = (mask >> 2) & 1, (mask >> 1) & 1, mask & 1
            peers.append((jnp.where(fx, 1 - x, x), jnp.where(fy, 1 - y, y), jnp.where(fc, 1 - c, c)))

        def copy(j, block):
            return pltpu.make_async_remote_copy(
                src_ref=in_ref, dst_ref=out_ref.at[block], send_sem=send_sems.at[j], recv_sem=recv_sems.at[j],
                device_id=peers[j], device_id_type=MESH)

        local = pltpu.make_async_copy(in_ref, out_ref.at[me], local_sem)
        local.start()
        sends = [copy(j, me) for j in range(N_DEV - 1)]
        for cp in sends:
            cp.start()
        for j, (px, py, pc) in enumerate(peers):
            copy(j, 4 * px + 2 * py + pc).wait_recv()
        for cp in sends:
            cp.wait_send()
        local.wait()

    return pl.pallas_call(
        body, name=name, in_specs=[_ANY], out_specs=_ANY,
        out_shape=jax.ShapeDtypeStruct((N_DEV,) + s.shape, s.dtype),
        scratch_shapes=[pltpu.SemaphoreType.DMA((N_DEV - 1,)), pltpu.SemaphoreType.DMA((N_DEV - 1,)),
                        pltpu.SemaphoreType.DMA],
    )(s)


def _sum_blocks(name, stacked, tm):
    n, r, lanes = stacked.shape

    def body(in_ref, o_ref):
        acc = in_ref[0]
        for j in range(1, n):
            acc = acc + in_ref[j]
        o_ref[...] = acc

    return pl.pallas_call(
        body, name=name, grid=(r // tm,), in_specs=[pl.BlockSpec((n, tm, lanes), lambda i: (0, i, 0))],
        out_specs=pl.BlockSpec((tm, lanes), lambda i: (i, 0)), out_shape=jax.ShapeDtypeStruct((r, lanes), F32),
        compiler_params=_params(("parallel",)),
    )(stacked)


def _row_tile(rows, pref, align):
    best = None
    for t in range(align, min(rows, pref) + 1, align):
        if rows % t == 0:
            best = t
    assert best is not None, (rows, pref, align)
    return best


def _adam(name, w, g, m, v):
    rows, width = w.shape
    tm = _row_tile(rows, max(8, 4096 * LANES // width), 8)
    args = [(t, width, 0) for t in (w, g, m, v)]
    return _rowcall(name, _adam_fn, args, [], [(width, F32)] * 3, tm=tm)


def kernel(x, norm_mix, norm_mlp, norm_final, mlp_w1, mlp_w2, ab_w_in, ab_w_out, rg_conv_w, rg_conv_b, rg_w_a, rg_b_a, rg_w_x, rg_b_x, rg_lambda, hg_lb_logits, hg_norm, gla_w_in, gla_w_out, gla_w_gate_up, gla_b_gate, gla_norm, loss_target, m_norm_mix, m_norm_mlp, m_norm_final, m_mlp_w1, m_mlp_w2, m_ab_w_in, m_ab_w_out, m_rg_conv_w, m_rg_conv_b, m_rg_w_a, m_rg_b_a, m_rg_w_x, m_rg_b_x, m_rg_lambda, m_hg_lb_logits, m_hg_norm, m_gla_w_in, m_gla_w_out, m_gla_w_gate_up, m_gla_b_gate, m_gla_norm, v_norm_mix, v_norm_mlp, v_norm_final, v_mlp_w1, v_mlp_w2, v_ab_w_in, v_ab_w_out, v_rg_conv_w, v_rg_conv_b, v_rg_w_a, v_rg_b_a, v_rg_w_x, v_rg_b_x, v_rg_lambda, v_hg_lb_logits, v_hg_norm, v_gla_w_in, v_gla_w_out, v_gla_w_gate_up, v_gla_b_gate, v_gla_norm):
    w = dict(norm_mix=norm_mix, norm_mlp=norm_mlp, norm_final=norm_final, mlp_w1=mlp_w1, mlp_w2=mlp_w2, ab_w_in=ab_w_in, ab_w_out=ab_w_out, rg_conv_w=rg_conv_w, rg_conv_b=rg_conv_b, rg_w_a=rg_w_a, rg_b_a=rg_b_a, rg_w_x=rg_w_x, rg_b_x=rg_b_x, rg_lambda=rg_lambda, hg_lb_logits=hg_lb_logits, hg_norm=hg_norm, gla_w_in=gla_w_in, gla_w_out=gla_w_out, gla_w_gate_up=gla_w_gate_up, gla_b_gate=gla_b_gate, gla_norm=gla_norm)
    m = dict(norm_mix=m_norm_mix, norm_mlp=m_norm_mlp, norm_final=m_norm_final, mlp_w1=m_mlp_w1, mlp_w2=m_mlp_w2, ab_w_in=m_ab_w_in, ab_w_out=m_ab_w_out, rg_conv_w=m_rg_conv_w, rg_conv_b=m_rg_conv_b, rg_w_a=m_rg_w_a, rg_b_a=m_rg_b_a, rg_w_x=m_rg_w_x, rg_b_x=m_rg_b_x, rg_lambda=m_rg_lambda, hg_lb_logits=m_hg_lb_logits, hg_norm=m_hg_norm, gla_w_in=m_gla_w_in, gla_w_out=m_gla_w_out, gla_w_gate_up=m_gla_w_gate_up, gla_b_gate=m_gla_b_gate, gla_norm=m_gla_norm)
    v = dict(norm_mix=v_norm_mix, norm_mlp=v_norm_mlp, norm_final=v_norm_final, mlp_w1=v_mlp_w1, mlp_w2=v_mlp_w2, ab_w_in=v_ab_w_in, ab_w_out=v_ab_w_out, rg_conv_w=v_rg_conv_w, rg_conv_b=v_rg_conv_b, rg_w_a=v_rg_w_a, rg_b_a=v_rg_b_a, rg_w_x=v_rg_w_x, rg_b_x=v_rg_b_x, rg_lambda=v_rg_lambda, hg_lb_logits=v_hg_lb_logits, hg_norm=v_hg_norm, gla_w_in=v_gla_w_in, gla_w_out=v_gla_w_out, gla_w_gate_up=v_gla_w_gate_up, gla_b_gate=v_gla_b_gate, gla_norm=v_gla_norm)
    chip = 2 * lax.axis_index("x") + lax.axis_index("y")
    core = lax.axis_index("c")
    sharded_shapes = [w[n].shape for n in SMALL_SHARDED]

    slots = [_into_slot(f"cast_{n}{layer}", w[n], chip, N_CHIPS, BF16, 512, layer) for n, layer in MATRICES]
    gathered = _gather_chips("gather_weights", slots)
    big = {n: [] for n, _ in MATRICES}
    for (n, _), t in zip(MATRICES, gathered):
        big[n].append(t)
    big = {n: (v if n in ("mlp_w1", "mlp_w2") else v[0]) for n, v in big.items()}
    vectors = _pack([w[n] for n in SMALL_SHARDED])
    vectors = _into_slot("place_vectors", vectors, chip, N_CHIPS, F32, vectors.shape[0])
    small_all = _unpack(_gather_chips("gather_vectors", [vectors])[0], sharded_shapes, lead=1)
    full = {n: w[n] for n in SMALL_REPLICATED}
    for n, t in zip(SMALL_SHARDED, small_all):
        full[n] = _join_chips(t, t.ndim - 2)

    loss_part, grad_x, g_kernel = _local_step(x[0], loss_target[0], _prepare_weights(big, full))
    g_big, g_full = _finish_grads(g_kernel)
    loss = lax.psum(loss_part[0, 0], ("x", "y", "c"))

    g_list = [g_big[n][layer] if n in ("mlp_w1", "mlp_w2") else g_big[n] for n, layer in MATRICES]
    halves = [t.reshape(N_CHIPS, 2, t.shape[1] // 2, t.shape[2]) for t in g_list]
    from_sibling = _pair_exchange("reduce_pair", halves)
    chip_part = [_pair_add(f"reduce_pair_add{i}", h, s, core) for i, (h, s) in enumerate(zip(halves, from_sibling))]
    from_chips = _chip_scatter("reduce_chips", chip_part)
    mine = [_sum_ring(f"reduce_chips_add{i}", p, f, chip, core) for i, (p, f) in enumerate(zip(chip_part, from_chips))]
    reduced = [t.reshape(2 * t.shape[1], t.shape[2]) for t in _pair_gather("reduce_share", mine)]
    by_name = {n: [] for n, _ in MATRICES}
    for (n, _), t in zip(MATRICES, reduced):
        by_name[n].append(t)
    grads = {n: jnp.stack(v) for n, v in by_name.items()}

    small_names = SMALL_REPLICATED + SMALL_SHARDED
    g_small = _pack([g_full[n] for n in small_names])
    g_small_all = _gather_all("reduce_small", g_small)
    g_small_red = _sum_blocks("reduce_small_add", g_small_all, g_small.shape[0])
    g_small_full = dict(zip(small_names, _unpack(g_small_red, [g_full[n].shape for n in small_names])))
    for n in SMALL_REPLICATED:
        grads[n] = g_small_full[n]
    for n in SMALL_SHARDED:
        width = w[n].shape[-1]
        grads[n] = lax.dynamic_slice_in_dim(g_small_full[n], chip * width, width, axis=g_small_full[n].ndim - 1)

    delta, new_m, new_v = {}, {}, {}
    for n in by_name:
        flat = [t.reshape(-1, t.shape[-1]) for t in (w[n], grads[n], m[n], v[n])]
        for dst, t in zip((delta, new_m, new_v), _adam(f"adam_{n}", *flat)):
            dst[n] = t.reshape(w[n].shape)
    small_shapes = [w[n].shape for n in small_names]
    packs = [_pack([src[n] for n in small_names]) for src in (w, grads, m, v)]
    d_small, m_small, v_small = _adam("adam_small", *packs)
    for dst, buf in ((delta, d_small), (new_m, m_small), (new_v, v_small)):
        dst.update(zip(small_names, _unpack(buf, small_shapes)))

    return (loss, grad_x[None], *[grads[n] for n in WEIGHTS], *[delta[n] for n in WEIGHTS],
            *[new_m[n] for n in WEIGHTS], *[new_v[n] for n in WEIGHTS])
```

```python
import functools

import jax
import jax.numpy as jnp
from jax import lax
from jax.experimental import pallas as pl
from jax.experimental.pallas import tpu as pltpu

F32 = jnp.float32
BF16 = jnp.bfloat16
MESH = pl.DeviceIdType.MESH

LANES = 128
CHUNK = 64
EPS = 1e-6
RG_C = 8.0
N_CHIPS = 4
N_DEV = 8
GLA_IN_PAD = 3200
VMEM_LIMIT = 56 * 1024 * 1024

ADAM_LR = 0.001
ADAM_B1 = 0.9
ADAM_B2 = 0.999
ADAM_EPS = 1e-08
ADAM_WD = 0.01
ADAM_STEP = 10


def _raw_dot(a, b, ca, cb):
    return lax.dot_general(a.astype(BF16), b.astype(BF16), (((ca,), (cb,)), ((), ())),
                           preferred_element_type=F32)


def _raw_nn(a, b):
    return _raw_dot(a, b, 1, 0)


def _raw_nt(a, b):
    return _raw_dot(a, b, 1, 1)


def _raw_tn(a, b):
    return _raw_dot(a, b, 0, 0)


@jax.custom_vjp
def _dot_nn(a, b):
    return _raw_nn(a, b)


def _dot_nn_fwd(a, b):
    return _raw_nn(a, b), (a, b)


def _dot_nn_bwd(res, g):
    a, b = res
    return _raw_nt(g, b), _raw_tn(a, g)


_dot_nn.defvjp(_dot_nn_fwd, _dot_nn_bwd)


@jax.custom_vjp
def _dot_nt(a, b):
    return _raw_nt(a, b)


def _dot_nt_fwd(a, b):
    return _raw_nt(a, b), (a, b)


def _dot_nt_bwd(res, g):
    a, b = res
    return _raw_nn(g, b), _raw_tn(g, a)


_dot_nt.defvjp(_dot_nt_fwd, _dot_nt_bwd)


@jax.custom_vjp
def _dot_tn(a, b):
    return _raw_tn(a, b)


def _dot_tn_fwd(a, b):
    return _raw_tn(a, b), (a, b)


def _dot_tn_bwd(res, g):
    a, b = res
    return _raw_nt(b, g), _raw_nn(a, g)


_dot_tn.defvjp(_dot_tn_fwd, _dot_tn_bwd)


def _tile(n, pref):
    if n <= pref:
        return n
    t = (pref // LANES) * LANES
    while t > LANES and n % t:
        t -= LANES
    assert n % t == 0, (n, pref)
    return t


def _params(sem):
    return pltpu.CompilerParams(dimension_semantics=sem, vmem_limit_bytes=VMEM_LIMIT)


def _rowcall(name, fn, rows, pars, row_outs, par_outs=(), tm=256):
    n_rows = rows[0][0].shape[0]
    tm = min(tm, n_rows)
    assert n_rows % tm == 0
    n_r, n_p, n_ro = len(rows), len(pars), len(row_outs)

    def body(*refs):
        vals = [r[...].astype(F32) for r in refs[:n_r + n_p]]
        outs = fn(*vals)
        o_refs = refs[n_r + n_p:n_r + n_p + n_ro]
        po_refs = refs[n_r + n_p + n_ro:]
        for o_ref, val in zip(o_refs, outs[:n_ro]):
            o_ref[...] = val.astype(o_ref.dtype)
        first = pl.program_id(0) == 0
        for po_ref, val in zip(po_refs, outs[n_ro:]):
            @pl.when(first)
            def _():
                po_ref[...] = val

            @pl.when(jnp.logical_not(first))
            def _():
                po_ref[...] += val

    def const_map(nd):
        return lambda i: (0,) * nd

    def row_spec(w, cb):
        return pl.BlockSpec((tm, w), lambda i: (i, cb))

    in_specs = [row_spec(w, cb) for _, w, cb in rows]
    in_specs += [pl.BlockSpec(p.shape, const_map(p.ndim)) for p in pars]
    out_specs = [pl.BlockSpec((tm, w), lambda i: (i, 0)) for w, _ in row_outs]
    out_specs += [pl.BlockSpec(tuple(s), const_map(len(s))) for s in par_outs]
    out_shape = [jax.ShapeDtypeStruct((n_rows, w), dt) for w, dt in row_outs]
    out_shape += [jax.ShapeDtypeStruct(tuple(s), F32) for s in par_outs]
    return pl.pallas_call(
        body, name=name, grid=(n_rows // tm,), in_specs=in_specs, out_specs=out_specs, out_shape=out_shape,
        compiler_params=_params(("arbitrary",) if par_outs else ("parallel",)),
    )(*[r[0] for r in rows], *pars)


def _vjp_of(fn, n_prim, n_out, n_par, n_pass=0):
    def bwd(*args):
        prim = args[:n_prim]
        cts = args[n_prim:n_prim + n_out]
        passes = args[n_prim + n_out:n_prim + n_out + 2 * n_pass]
        pars = args[n_prim + n_out + 2 * n_pass:]
        _, vjp = jax.vjp(fn, *prim, *pars)
        grads = vjp(tuple(cts))
        sums = tuple(passes[2 * i] + passes[2 * i + 1] for i in range(n_pass))
        return tuple(grads[:n_prim]) + sums + tuple(grads[n_prim:])
    return bwd


def _mm(name, a, b, mode="nn", extras=(), epi=None, out_dtypes=(F32,), a_pro=None, out_split=None,
        tm=1024, tn=1024, tk=1024):
    split = b.shape[0] if b.ndim == 3 else None
    b_rows, b_cols = b.shape[-2:]
    if mode == "nn":
        (m, k), n = a.shape, b_cols * (split or 1)
    elif mode == "nt":
        (m, k), n = a.shape, b_rows
        assert k == b_cols * (split or 1)
    else:
        assert split is None
        (k, m), n = a.shape, b_cols
    tm, tk = _tile(m, tm), _tile(k, tk)
    tn = _tile(n // out_split, tn) if out_split else _tile(n, tn)
    if split and mode == "nn":
        tn = _tile(b_cols, tn)
    if split and mode == "nt":
        tk = _tile(b_cols, tk)
    nk = k // tk
    raw = {"nn": _raw_nn, "nt": _raw_nt, "tn": _raw_tn}[mode]
    n_e, n_o = len(extras), len(out_dtypes)
    if epi is None:
        epi = lambda acc: (acc,)

    def body(a_ref, b_ref, *rest):
        e_refs, o_refs, acc = rest[:n_e], rest[n_e:n_e + n_o], rest[-1]
        kk = pl.program_id(2)

        @pl.when(kk == 0)
        def _():
            acc[...] = jnp.zeros_like(acc)

        a_tile = a_ref[...] if a_pro is None else a_pro(a_ref[...].astype(F32))
        acc[...] += raw(a_tile, b_ref[...])

        @pl.when(kk == nk - 1)
        def _():
            res = epi(acc[...], *[e[...].astype(F32) for e in e_refs])
            for o_ref, r in zip(o_refs, res):
                o_ref[...] = r.astype(o_ref.dtype)

    a_spec = pl.BlockSpec((tk, tm), lambda i, j, kk: (kk, i)) if mode == "tn" else pl.BlockSpec((tm, tk), lambda i, j, kk: (i, kk))
    if split and mode == "nn":
        per = b_cols // tn
        b_spec = pl.BlockSpec((None, tk, tn), lambda i, j, kk: (j // per, kk, j % per))
    elif split:
        per = b_cols // tk
        b_spec = pl.BlockSpec((None, tn, tk), lambda i, j, kk: (kk // per, j, kk % per))
    elif mode == "nt":
        b_spec = pl.BlockSpec((tn, tk), lambda i, j, kk: (j, kk))
    else:
        b_spec = pl.BlockSpec((tk, tn), lambda i, j, kk: (kk, j))
    mn_spec = pl.BlockSpec((tm, tn), lambda i, j, kk: (i, j))
    if out_split:
        assert not extras
        per_out = n // out_split // tn
        out_spec = pl.BlockSpec((None, tm, tn), lambda i, j, kk: (j // per_out, i, j % per_out))
        out_shapes = [jax.ShapeDtypeStruct((out_split, m, n // out_split), dt) for dt in out_dtypes]
    else:
        out_spec = mn_spec
        out_shapes = [jax.ShapeDtypeStruct((m, n), dt) for dt in out_dtypes]
    outs = pl.pallas_call(
        body, name=name, grid=(m // tm, n // tn, nk),
        in_specs=[a_spec, b_spec] + [mn_spec] * n_e, out_specs=[out_spec] * n_o,
        out_shape=out_shapes,
        scratch_shapes=[pltpu.VMEM((tm, tn), F32)],
        compiler_params=_params(("parallel", "parallel", "arbitrary")),
    )(a, b, *extras)
    return outs[0] if n_o == 1 else outs


def _sigmoid(x):
    return jax.nn.sigmoid(x)


def _silu(x):
    return x * _sigmoid(x)


def _softplus(x):
    return jnp.maximum(x, 0.0) + jnp.log1p(jnp.exp(-jnp.abs(x)))


def _rmsnorm_fn(x, gain):
    return (x * lax.rsqrt(jnp.mean(x * x, axis=-1, keepdims=True) + EPS) * gain,)


def _head_norm(o, gain, n_heads):
    w = o.shape[-1] // n_heads
    parts = []
    for h in range(n_heads):
        oh = o[:, h * w:(h + 1) * w]
        parts.append(oh * lax.rsqrt(jnp.mean(oh * oh, axis=-1, keepdims=True) + EPS))
    return jnp.concatenate(parts, axis=-1) * gain


@jax.custom_jvp
def _neg_expm1(x):
    u = jnp.exp(x)
    is_one = u == 1.0
    return jnp.where(is_one, -x, (1.0 - u) * x / jnp.log(jnp.where(is_one, 2.0, u)))


@_neg_expm1.defjvp
def _neg_expm1_jvp(primals, tangents):
    (x,), (t,) = primals, tangents
    return _neg_expm1(x), -jnp.exp(x) * t


def _rg_gates_fn(xc, wa, wx, ba, bx, lam):
    outs = []
    for d in range(2):
        r = _sigmoid(_dot_nn(xc, wa[d]) + ba[d:d + 1])
        i = _sigmoid(_dot_nn(xc, wx[d]) + bx[d:d + 1])
        log_a = -RG_C * r * _softplus(-lam[d:d + 1])
        outs.append(jnp.exp(log_a))
        outs.append(jnp.sqrt(_neg_expm1(2.0 * log_a)) * (i * xc))
    return tuple(outs)


def _hg_pre_fn(q, f_f, f_b, logits):
    mx = jnp.maximum(logits[0:1], logits[1:2])
    e0 = jnp.exp(logits[0:1] - mx)
    e1 = jnp.exp(logits[1:2] - mx)
    lb = e0 / (e0 + e1)
    outs = [_silu(q)]
    for f in (f_f, f_b):
        outs.append((1.0 - lb) * _sigmoid(-f))
        outs.append(jnp.log(lb + (1.0 - lb) * _sigmoid(f)))
    return tuple(outs)


def _post0_fn(hs, ga, o, g, gain):
    ya = hs * jax.nn.gelu(ga, approximate=True)
    yb = _head_norm(o, gain, 4) * _silu(g)
    return (jnp.concatenate([ya, yb], axis=-1),)


def _post0_fwd_fn(h_f, h_b, ga, o_f, o_b, g, gain):
    return _post0_fn(h_f + h_b, ga, o_f + o_b, g, gain)


def _post0_bwd_fn(h_f, h_b, ga, o_f, o_b, g, dmix, gain):
    _, vjp = jax.vjp(_post0_fn, h_f + h_b, ga, o_f + o_b, g, gain)
    return vjp((dmix,))


def _gla_pre_fn(q, lr, w_up, b_gate):
    outs = [q * (128.0 ** -0.5)]
    for d in range(2):
        z = _dot_nn(lr, w_up[d]) + b_gate[d:d + 1]
        outs.append(-_softplus(-z) * (1.0 / 16.0))
    return tuple(outs)


def _gla_post_fn(o, r, gain):
    return (_head_norm(o, gain, 4) * _silu(r),)


def _gla_post_fwd_fn(o_f, o_b, r, gain):
    return _gla_post_fn(o_f + o_b, r, gain)


def _gla_post_bwd_fn(o_f, o_b, r, dmix, gain):
    _, vjp = jax.vjp(_gla_post_fn, o_f + o_b, r, gain)
    return vjp((dmix,))


def _relu2_bwd_epi(acc, hid):
    return (acc * 2.0 * jnp.maximum(hid, 0.0),)


def _relu2(x):
    r = jnp.maximum(x, 0.0)
    return r * r


def _add_epi(acc, res):
    return (acc + res,)


def _loss_head_fn(h, target, gain):
    def f(h, gain):
        y = _rmsnorm_fn(h, gain)[0]
        err = y - target
        return 0.5 * jnp.sum(jnp.mean(err * err, axis=-1, keepdims=True))
    loss, (dh, dgain) = jax.value_and_grad(f, argnums=(0, 1))(h, gain)
    return dh, jnp.full((1, LANES), loss, F32), dgain


def _adam_fn(w, g, m, v):
    m2 = ADAM_B1 * m + (1.0 - ADAM_B1) * g
    v2 = ADAM_B2 * v + (1.0 - ADAM_B2) * (g * g)
    m_hat = m2 / (1.0 - ADAM_B1 ** ADAM_STEP)
    v_hat = v2 / (1.0 - ADAM_B2 ** ADAM_STEP)
    delta = -ADAM_LR * (m_hat / (jnp.sqrt(v_hat) + ADAM_EPS) + ADAM_WD * w)
    return delta, m2, v2


def _shifted(x, t_idx, off):
    n = x.shape[0]
    rolled = pltpu.roll(x, (-off) % n, 0)
    valid = (t_idx + off >= 0) & (t_idx + off < n)
    return jnp.where(valid, rolled, 0.0)


def _conv_fwd(name, src, colblock, w, b):
    n_rows, width = src.shape[0], w.shape[1]

    def body(x_ref, w_ref, b_ref, o_ref):
        x = x_ref[...]
        t_idx = lax.broadcasted_iota(jnp.int32, x.shape, 0)
        acc = b_ref[...] + w_ref[2:3, :] * x
        acc += w_ref[0:1, :] * _shifted(x, t_idx, -2)
        acc += w_ref[1:2, :] * _shifted(x, t_idx, -1)
        acc += w_ref[3:4, :] * _shifted(x, t_idx, 1)
        o_ref[...] = acc

    nb = width // LANES
    return pl.pallas_call(
        body, name=name, grid=(nb,),
        in_specs=[pl.BlockSpec((n_rows, LANES), lambda j: (0, colblock * nb + j)),
                  pl.BlockSpec((4, LANES), lambda j: (0, j)), pl.BlockSpec((1, LANES), lambda j: (0, j))],
        out_specs=pl.BlockSpec((n_rows, LANES), lambda j: (0, j)),
        out_shape=jax.ShapeDtypeStruct((n_rows, width), F32),
        compiler_params=_params(("parallel",)),
    )(src, w, b)


def _conv_bwd(name, src, colblock, w, d):
    n_rows, width = src.shape[0], w.shape[1]

    def body(x_ref, w_ref, d_ref, dx_ref, dw_ref, db_ref):
        x = x_ref[...]
        g = d_ref[...]
        t_idx = lax.broadcasted_iota(jnp.int32, x.shape, 0)
        dx = w_ref[2:3, :] * g
        dx += w_ref[0:1, :] * _shifted(g, t_idx, 2)
        dx += w_ref[1:2, :] * _shifted(g, t_idx, 1)
        dx += w_ref[3:4, :] * _shifted(g, t_idx, -1)
        dx_ref[...] = dx.astype(dx_ref.dtype)
        dw_ref[0:1, :] = jnp.sum(g * _shifted(x, t_idx, -2), axis=0, keepdims=True)
        dw_ref[1:2, :] = jnp.sum(g * _shifted(x, t_idx, -1), axis=0, keepdims=True)
        dw_ref[2:3, :] = jnp.sum(g * x, axis=0, keepdims=True)
        dw_ref[3:4, :] = jnp.sum(g * _shifted(x, t_idx, 1), axis=0, keepdims=True)
        db_ref[...] = jnp.sum(g, axis=0, keepdims=True)

    nb = width // LANES
    return pl.pallas_call(
        body, name=name, grid=(nb,),
        in_specs=[pl.BlockSpec((n_rows, LANES), lambda j: (0, colblock * nb + j)),
                  pl.BlockSpec((4, LANES), lambda j: (0, j)),
                  pl.BlockSpec((n_rows, LANES), lambda j: (0, j))],
        out_specs=[pl.BlockSpec((n_rows, LANES), lambda j: (0, j)), pl.BlockSpec((4, LANES), lambda j: (0, j)),
                   pl.BlockSpec((1, LANES), lambda j: (0, j))],
        out_shape=[jax.ShapeDtypeStruct((n_rows, width), BF16), jax.ShapeDtypeStruct((4, width), F32),
                   jax.ShapeDtypeStruct((1, width), F32)],
        compiler_params=_params(("parallel",)),
    )(src, w, d)


_WHOLE = pl.BlockSpec(memory_space=pltpu.VMEM)
SCAN_UNROLL = 8


def _scan_fwd(name, a, u, reverse):
    n_rows, width = a.shape

    def body(a_ref, u_ref, h_ref):
        def step(i, h):
            t = (n_rows - 1 - i) if reverse else i
            h = a_ref[pl.ds(t, 1), :] * h + u_ref[pl.ds(t, 1), :]
            h_ref[pl.ds(t, 1), :] = h
            return h
        lax.fori_loop(0, n_rows, step, jnp.zeros((1, width), F32), unroll=SCAN_UNROLL)

    return pl.pallas_call(
        body, name=name, in_specs=[_WHOLE, _WHOLE], out_specs=_WHOLE,
        out_shape=jax.ShapeDtypeStruct((n_rows, width), F32),
        compiler_params=pltpu.CompilerParams(vmem_limit_bytes=VMEM_LIMIT),
    )(a, u)


def _scan_bwd(name, a, h, dh, reverse):
    n_rows, width = a.shape

    def body(a_ref, h_ref, dh_ref, du_ref, da_ref):
        def step(i, carry):
            t = i if reverse else (n_rows - 1 - i)
            g = dh_ref[pl.ds(t, 1), :] + carry
            du_ref[pl.ds(t, 1), :] = g
            tp = t + 1 if reverse else t - 1
            valid = (tp >= 0) & (tp < n_rows)
            h_prev = h_ref[pl.ds(jnp.clip(tp, 0, n_rows - 1), 1), :]
            da_ref[pl.ds(t, 1), :] = jnp.where(valid, g * h_prev, 0.0)
            return a_ref[pl.ds(t, 1), :] * g
        lax.fori_loop(0, n_rows, step, jnp.zeros((1, width), F32), unroll=SCAN_UNROLL)

    return pl.pallas_call(
        body, name=name, in_specs=[_WHOLE] * 3, out_specs=[_WHOLE] * 2,
        out_shape=[jax.ShapeDtypeStruct((n_rows, width), F32)] * 2,
        compiler_params=pltpu.CompilerParams(vmem_limit_bytes=VMEM_LIMIT),
    )(a, h, dh)


def _tri_mask(c, reverse):
    row = lax.broadcasted_iota(jnp.int32, (c, c), 0)
    col = lax.broadcasted_iota(jnp.int32, (c, c), 1)
    return (col >= row) if reverse else (col <= row)


def _cumsum_rows(x, reverse):
    tri = _tri_mask(x.shape[0], reverse).astype(BF16)
    hi = x.astype(BF16)
    rest = x - hi.astype(F32)
    mid = rest.astype(BF16)
    lo = (rest - mid.astype(F32)).astype(BF16)
    return _raw_nn(tri, hi) + _raw_nn(tri, mid) + _raw_nn(tri, lo)


@functools.partial(jax.custom_vjp, nondiff_argnums=(1,))
def _cumsum(x, reverse):
    return _cumsum_rows(x, reverse)


def _cumsum_fwd(x, reverse):
    return _cumsum_rows(x, reverse), None


def _cumsum_bwd(reverse, _, g):
    return (_cumsum_rows(g, not reverse),)


_cumsum.defvjp(_cumsum_fwd, _cumsum_bwd)


def _chunks_fn(qs, ks, vs, lfs, sts, reverses):
    n, c = len(qs), qs[0].shape[0]
    every = range(n)
    tris = [_tri_mask(c, r) for r in reverses]
    cums = [_cumsum(lfs[i], reverses[i]) for i in every]
    rid = lax.broadcasted_iota(jnp.int32, cums[0].shape, 0)

    def pick(cum, r):
        return jnp.sum(jnp.where(rid == r, cum, 0.0), axis=0, keepdims=True)

    refs = [pick(cums[i], (c - 1 - c // 2) if reverses[i] else c // 2) for i in every]
    lasts = [pick(cums[i], 0 if reverses[i] else c - 1) for i in every]
    q_in = [qs[i] * jnp.exp(cums[i] - refs[i]) for i in every]
    k_in = [ks[i] * jnp.exp(refs[i] - cums[i]) for i in every]
    scores = [jnp.where(tris[i], _dot_nt(q_in[i], k_in[i]), 0.0) for i in every]
    o_intra = [_dot_nn(scores[i], vs[i]) for i in every]
    q_out = [qs[i] * jnp.exp(cums[i]) for i in every]
    o_inter = [_dot_nt(q_out[i], sts[i]) for i in every]
    k_state = [ks[i] * jnp.exp(lasts[i] - cums[i]) for i in every]
    upd = [_dot_tn(vs[i], k_state[i]) for i in every]
    st_new = [sts[i] * jnp.exp(lasts[i]) + upd[i] for i in every]
    return [o_intra[i] + o_inter[i] for i in every], st_new


def _attn_fwd(name, q, k_f, k_b, v, lf_f, lf_b, n_heads, dk, dv):
    n_rows = q[0].shape[0]
    n_chunks = n_rows // CHUNK
    wk, wv = n_heads * dk, n_heads * dv

    def spec(width, off, rev):
        return pl.BlockSpec((CHUNK, width), lambda n: ((n_chunks - 1 - n) if rev else n, off))

    def sspec(rev):
        return pl.BlockSpec((None, n_heads, dv, dk), lambda n: ((n_chunks - 1 - n) if rev else n, 0, 0, 0))

    def body(qf, kf, vf, lff, qb, kb, vb, lfb, of_ref, ob_ref, sf_ref, sb_ref, st):
        @pl.when(pl.program_id(0) == 0)
        def _():
            st[...] = jnp.zeros_like(st)

        ins = ((qf, kf, vf, lff), (qb, kb, vb, lfb))
        chains = [(d, h) for d in range(2) for h in range(n_heads)]
        ck = [slice(h * dk, (h + 1) * dk) for h in range(n_heads)]
        cv = [slice(h * dv, (h + 1) * dv) for h in range(n_heads)]
        qs = [ins[d][0][:, ck[h]] for d, h in chains]
        ks = [ins[d][1][:, ck[h]] for d, h in chains]
        vs = [ins[d][2][:, cv[h]] for d, h in chains]
        lfs = [ins[d][3][:, ck[h]] for d, h in chains]
        sts = [st[d, h] for d, h in chains]
        os_, st_new = _chunks_fn(qs, ks, vs, lfs, sts, [d == 1 for d, _ in chains])
        for i, (d, h) in enumerate(chains):
            (sf_ref, sb_ref)[d][h] = sts[i]
            (of_ref, ob_ref)[d][:, cv[h]] = os_[i]
            st[d, h] = st_new[i]

    in_specs = [spec(wk, q[1], False), spec(wk, k_f[1], False), spec(wv, v[1], False), spec(wk, lf_f[1], False),
                spec(wk, q[1], True), spec(wk, k_b[1], True), spec(wv, v[1], True), spec(wk, lf_b[1], True)]
    return pl.pallas_call(
        body, name=name, grid=(n_chunks,), in_specs=in_specs,
        out_specs=[spec(wv, 0, False), spec(wv, 0, True), sspec(False), sspec(True)],
        out_shape=[jax.ShapeDtypeStruct((n_rows, wv), F32)] * 2
        + [jax.ShapeDtypeStruct((n_chunks, n_heads, dv, dk), F32)] * 2,
        scratch_shapes=[pltpu.VMEM((2, n_heads, dv, dk), F32)],
        compiler_params=_params(("arbitrary",)),
    )(q[0], k_f[0], v[0], lf_f[0], q[0], k_b[0], v[0], lf_b[0])


def _attn_bwd(name, q, k_f, k_b, v, lf_f, lf_b, st_f, st_b, do, n_heads, dk, dv, out_dtype=F32):
    n_rows = q[0].shape[0]
    n_chunks = n_rows // CHUNK
    wk, wv = n_heads * dk, n_heads * dv

    def spec(width, off, rev):
        return pl.BlockSpec((CHUNK, width), lambda n: (n if rev else (n_chunks - 1 - n), off))

    def sspec(rev):
        return pl.BlockSpec((None, n_heads, dv, dk), lambda n: (n if rev else (n_chunks - 1 - n), 0, 0, 0))

    def body(qf, kf, vf, lff, sf, dof, qb, kb, vb, lfb, sb, dob,
             dqf, dkf, dvf, dlff, dqb, dkb, dvb, dlfb, dst):
        @pl.when(pl.program_id(0) == 0)
        def _():
            dst[...] = jnp.zeros_like(dst)

        ins = ((qf, kf, vf, lff, sf, dof), (qb, kb, vb, lfb, sb, dob))
        outs = ((dqf, dkf, dvf, dlff), (dqb, dkb, dvb, dlfb))
        chains = [(d, h) for d in range(2) for h in range(n_heads)]
        ck = [slice(h * dk, (h + 1) * dk) for h in range(n_heads)]
        cv = [slice(h * dv, (h + 1) * dv) for h in range(n_heads)]
        qs = [ins[d][0][:, ck[h]] for d, h in chains]
        ks = [ins[d][1][:, ck[h]] for d, h in chains]
        vs = [ins[d][2][:, cv[h]] for d, h in chains]
        lfs = [ins[d][3][:, ck[h]] for d, h in chains]
        sts = [ins[d][4][h] for d, h in chains]
        dos = [ins[d][5][:, cv[h]] for d, h in chains]
        dsts = [dst[d, h] for d, h in chains]
        fn = functools.partial(_chunks_fn, reverses=[d == 1 for d, _ in chains])
        _, vjp = jax.vjp(fn, qs, ks, vs, lfs, sts)
        dqs, dks, dvs, dlfs, dst_prev = vjp((dos, dsts))
        for i, (d, h) in enumerate(chains):
            dq_r, dk_r, dv_r, dlf_r = outs[d]
            dq_r[:, ck[h]] = dqs[i].astype(dq_r.dtype)
            dk_r[:, ck[h]] = dks[i].astype(dk_r.dtype)
            dv_r[:, cv[h]] = dvs[i].astype(dv_r.dtype)
            dlf_r[:, ck[h]] = dlfs[i].astype(dlf_r.dtype)
            dst[d, h] = dst_prev[i]

    def dir_specs(kk, lf, rev):
        return [spec(wk, q[1], rev), spec(wk, kk[1], rev), spec(wv, v[1], rev), spec(wk, lf[1], rev), sspec(rev),
                spec(wv, 0, rev)]

    def dir_out_specs(rev):
        return [spec(wk, 0, rev), spec(wk, 0, rev), spec(wv, 0, rev), spec(wk, 0, rev)]

    shapes = [jax.ShapeDtypeStruct((n_rows, wk), out_dtype), jax.ShapeDtypeStruct((n_rows, wk), out_dtype),
              jax.ShapeDtypeStruct((n_rows, wv), out_dtype), jax.ShapeDtypeStruct((n_rows, wk), F32)]
    outs = pl.pallas_call(
        body, name=name, grid=(n_chunks,), in_specs=dir_specs(k_f, lf_f, False) + dir_specs(k_b, lf_b, True),
        out_specs=dir_out_specs(False) + dir_out_specs(True), out_shape=shapes + shapes,
        scratch_shapes=[pltpu.VMEM((2, n_heads, dv, dk), F32)],
        compiler_params=_params(("arbitrary",)),
    )(q[0], k_f[0], v[0], lf_f[0], st_f, do, q[0], k_b[0], v[0], lf_b[0], st_b, do)
    return outs[:4], outs[4:]


def _row2(v):
    return v.reshape(1, -1)


def _mlp_fwd(tag, h, gain, w1, w2):
    y = _rowcall(f"{tag}_norm", _rmsnorm_fn, [(h, h.shape[1], 0)], [gain], [(h.shape[1], BF16)], tm=512)[0]
    hid = _mm(f"{tag}_up", y, w1, out_dtypes=(BF16,))
    h_out = _mm(f"{tag}_down", hid, w2, a_pro=_relu2, extras=(h,), epi=_add_epi)
    return h_out, (y, hid)


def _mlp_bwd(tag, h, gain, w1, w2, saved, dh_out):
    y, hid = saved
    dhid = _mm(f"{tag}_dact", dh_out, w2, mode="nt", extras=(hid,), epi=_relu2_bwd_epi, out_dtypes=(BF16,))
    dw2 = _mm(f"{tag}_dw2", hid, dh_out, mode="tn", a_pro=_relu2)
    dw1 = _mm(f"{tag}_dw1", y, dhid, mode="tn", out_split=N_CHIPS)
    dy = _mm(f"{tag}_dy", dhid, w1, mode="nt")
    dh, dgain = _norm_bwd(f"{tag}_dnorm", h, gain, dy, dh_out)
    return dh, dgain, dw1, dw2


def _norm_bwd(name, h, gain, dy, dres):
    d = h.shape[1]

    def fn(h, dy, dres, gain):
        _, vjp = jax.vjp(lambda a, b: _rmsnorm_fn(a, b)[0], h, gain)
        dh, dgain = vjp(dy)
        return dh + dres, dgain

    dh, dgain = _rowcall(name, fn, [(h, d, 0), (dy, d, 0), (dres, d, 0)], [gain], [(d, F32)], [(1, d)], tm=512)
    return dh, dgain


def _local_step(x, target, w):
    g = {}
    d_model = x.shape[1]
    rg_w = hg_w = d_model // 2

    h_a0 = x
    gain = _row2(w["norm_mix"][0])
    y0 = _rowcall("l0_norm", _rmsnorm_fn, [(h_a0, d_model, 0)], [gain], [(d_model, BF16)], tm=512)[0]
    proj0 = _mm("l0_in", y0, w["ab_w_in"])
    conv_w, conv_b = w["rg_conv_w"], _row2(w["rg_conv_b"])
    xc = _conv_fwd("rg_conv", proj0, 0, conv_w, conv_b)
    gate_pars = [w["rg_wa_bd"], w["rg_wx_bd"], w["rg_b_a"], w["rg_b_x"], w["rg_lambda"]]
    a_f, u_f, a_b, u_b = _rowcall("rg_gates", _rg_gates_fn, [(xc, rg_w, 0)], gate_pars, [(rg_w, F32)] * 4)
    hs_f = _scan_fwd("rg_scan_f", a_f, u_f, False)
    hs_b = _scan_fwd("rg_scan_b", a_b, u_b, True)
    hg_rows = [(proj0, hg_w, 2), (proj0, hg_w, 3), (proj0, hg_w, 4)]
    qh, k_f, lf_f, k_b, lf_b = _rowcall("hg_pre", _hg_pre_fn, hg_rows, [w["hg_lb_logits"]], [(hg_w, F32)] * 5)
    iv = (proj0, 5)
    o_f, o_b, st_f, st_b = _attn_fwd("hg_attn", (qh, 0), (k_f, 0), (k_b, 0), iv, (lf_f, 0), (lf_b, 0), 4, 128, 128)
    post0_rows = [(hs_f, rg_w, 0), (hs_b, rg_w, 0), (proj0, rg_w, 1), (o_f, hg_w, 0), (o_b, hg_w, 0), (proj0, hg_w, 6)]
    hg_gain = _row2(w["hg_norm"])
    mix_in0 = _rowcall("l0_post", _post0_fwd_fn, post0_rows, [hg_gain], [(d_model, BF16)])[0]
    h_b0 = _mm("l0_out", mix_in0, w["ab_w_out"], extras=(h_a0,), epi=_add_epi)
    h_c0, mlp0 = _mlp_fwd("mlp0", h_b0, _row2(w["norm_mlp"][0]), w["mlp_w1"][0], w["mlp_w2"][0])

    h_a1 = h_c0
    gain1 = _row2(w["norm_mix"][1])
    y1 = _rowcall("l1_norm", _rmsnorm_fn, [(h_a1, d_model, 0)], [gain1], [(d_model, BF16)], tm=512)[0]
    proj1 = _mm("l1_in", y1, w["gla_w_in_pad"], tn=640)
    gla_pars = [w["gla_w_up_pad"], w["gla_b_gate"]]
    gq, glf_f, glf_b = _rowcall("gla_pre", _gla_pre_fn, [(proj1, 512, 0), (proj1, LANES, 24)], gla_pars, [(512, F32)] * 3)
    gk, gv = (proj1, 1), (proj1, 1)
    go_f, go_b, gst_f, gst_b = _attn_fwd("gla_attn", (gq, 0), gk, gk, gv, (glf_f, 0), (glf_b, 0), 4, 128, 256)
    gla_gain = _row2(w["gla_norm"])
    post1_rows = [(go_f, d_model, 0), (go_b, d_model, 0), (proj1, d_model, 2)]
    mix_in1 = _rowcall("l1_post", _gla_post_fwd_fn, post1_rows, [gla_gain], [(d_model, BF16)])[0]
    h_b1 = _mm("l1_out", mix_in1, w["gla_w_out"], extras=(h_a1,), epi=_add_epi)
    h_c1, mlp1 = _mlp_fwd("mlp1", h_b1, _row2(w["norm_mlp"][1]), w["mlp_w1"][1], w["mlp_w2"][1])

    dh, loss, g["norm_final"] = _rowcall(
        "loss_head", _loss_head_fn, [(h_c1, d_model, 0), (target, d_model, 0)], [_row2(w["norm_final"])],
        [(d_model, F32)], [(1, LANES), (1, d_model)], tm=512)

    dh, g_nmlp1, g_w1_1, g_w2_1 = _mlp_bwd("mlp1", h_b1, _row2(w["norm_mlp"][1]), w["mlp_w1"][1], w["mlp_w2"][1], mlp1, dh)
    dmix1 = _mm("l1_dout", dh, w["gla_w_out"], mode="nt")
    g["gla_w_out"] = _mm("l1_dwout", mix_in1, dh, mode="tn")
    dgo, dr, g["gla_norm"] = _rowcall(
        "l1_dpost", _gla_post_bwd_fn, post1_rows + [(dmix1, d_model, 0)], [gla_gain],
        [(d_model, F32), (d_model, BF16)], [(1, d_model)])
    (dq_f, dk_f, dv_f, dlf_f), (dq_b, dk_b, dv_b, dlf_b) = _attn_bwd(
        "gla_dattn", (gq, 0), gk, gk, gv, (glf_f, 0), (glf_b, 0), gst_f, gst_b, dgo, 4, 128, 256)

    def gla_pre_bwd(q, lr, dq1, dq2, dlf1, dlf2, dk1, dk2, dv1, dv2, w_up, b_gate):
        dlr = jnp.zeros_like(lr)
        dws, dbs = [], []
        for d, dlf in enumerate((dlf1, dlf2)):
            z = _raw_nn(lr, w_up[d]) + b_gate[d:d + 1]
            dz = dlf * _sigmoid(-z) * (1.0 / 16.0)
            dlr = dlr + _raw_nt(dz, w_up[d])
            dws.append(_raw_tn(dz, lr))
            dbs.append(jnp.sum(dz, axis=0, keepdims=True))
        return ((dq1 + dq2) * (128.0 ** -0.5), dk1 + dk2, dv1 + dv2, dlr, dws[0], dws[1], dbs[0], dbs[1])

    rows = [(proj1, 512, 0), (proj1, LANES, 24), (dq_f, 512, 0), (dq_b, 512, 0), (dlf_f, 512, 0), (dlf_b, 512, 0),
            (dk_f, 512, 0), (dk_b, 512, 0), (dv_f, d_model, 0), (dv_b, d_model, 0)]
    dq, dk, dv, dlr, dwt_f, dwt_b, db_f, db_b = _rowcall(
        "gla_dpre", gla_pre_bwd, rows, gla_pars, [(512, BF16), (512, BF16), (d_model, BF16), (LANES, BF16)],
        [(512, LANES), (512, LANES), (1, 512), (1, 512)])
    g["gla_w_up_pad"] = jnp.stack([dwt_f.T, dwt_b.T])
    g["gla_b_gate"] = jnp.concatenate([db_f, db_b], axis=0)
    dproj1 = jnp.concatenate([dq, dk, dv, dr, dlr], axis=1)
    g["gla_w_in_pad"] = _mm("l1_dwin", y1, dproj1, mode="tn", tn=640)
    dy1 = _mm("l1_dy", dproj1, w["gla_w_in_pad"], mode="nt", tk=640)
    dh, g_nmix1 = _norm_bwd("l1_dnorm", h_a1, gain1, dy1, dh)

    dh, g_nmlp0, g_w1_0, g_w2_0 = _mlp_bwd("mlp0", h_b0, _row2(w["norm_mlp"][0]), w["mlp_w1"][0], w["mlp_w2"][0], mlp0, dh)
    dmix0 = _mm("l0_dout", dh, w["ab_w_out"], mode="nt")
    g["ab_w_out"] = _mm("l0_dwout", mix_in0, dh, mode="tn")
    dhs, dga, do, dg, g["hg_norm"] = _rowcall(
        "l0_dpost", _post0_bwd_fn, post0_rows + [(dmix0, d_model, 0)], [hg_gain],
        [(rg_w, F32), (rg_w, BF16), (hg_w, F32), (hg_w, BF16)], [(1, hg_w)])
    (dqh_f, dk_f, div_f, dlf_f), (dqh_b, dk_b, div_b, dlf_b) = _attn_bwd(
        "hg_dattn", (qh, 0), (k_f, 0), (k_b, 0), iv, (lf_f, 0), (lf_b, 0), st_f, st_b, do, 4, 128, 128)

    def hg_pre_bwd(q, f_f, f_b, dq1, dq2, dk1, dlf1, dk2, dlf2, dv1, dv2, logits):
        _, vjp = jax.vjp(_hg_pre_fn, q, f_f, f_b, logits)
        dq, df_f, df_b, dlogits = vjp((dq1 + dq2, dk1, dlf1, dk2, dlf2))
        return dq, df_f, df_b, dv1 + dv2, dlogits

    rows = hg_rows + [(t, hg_w, 0) for t in (dqh_f, dqh_b, dk_f, dlf_f, dk_b, dlf_b, div_f, div_b)]
    dq, df_f, df_b, div, g["hg_lb_logits"] = _rowcall(
        "hg_dpre", hg_pre_bwd, rows, [w["hg_lb_logits"]], [(hg_w, BF16)] * 4, [(2, hg_w)])
    du_f, da_f = _scan_bwd("rg_dscan_f", a_f, hs_f, dhs, False)
    du_b, da_b = _scan_bwd("rg_dscan_b", a_b, hs_b, dhs, True)
    gates_bwd = _vjp_of(_rg_gates_fn, 1, 4, 5)
    rows = [(xc, rg_w, 0), (da_f, rg_w, 0), (du_f, rg_w, 0), (da_b, rg_w, 0), (du_b, rg_w, 0)]
    dxc, g["rg_wa_bd"], g["rg_wx_bd"], g["rg_b_a"], g["rg_b_x"], g["rg_lambda"] = _rowcall(
        "rg_dgates", gates_bwd, rows, gate_pars, [(rg_w, F32)],
        [(2, rg_w, rg_w), (2, rg_w, rg_w), (2, rg_w), (2, rg_w), (2, rg_w)])
    dxa, g["rg_conv_w"], g["rg_conv_b"] = _conv_bwd("rg_dconv", proj0, 0, conv_w, dxc)
    dproj0 = jnp.concatenate([dxa, dga, dq, df_f, df_b, div, dg], axis=1)
    g["ab_w_in"] = _mm("l0_dwin", y0, dproj0, mode="tn", out_split=N_CHIPS)
    dy0 = _mm("l0_dy", dproj0, w["ab_w_in"], mode="nt")
    grad_x, g_nmix0 = _norm_bwd("l0_dnorm", h_a0, gain, dy0, dh)

    g["norm_mix"] = jnp.concatenate([g_nmix0, g_nmix1], axis=0)
    g["norm_mlp"] = jnp.concatenate([g_nmlp0, g_nmlp1], axis=0)
    g["mlp_w1"] = [g_w1_0, g_w1_1]
    g["mlp_w2"] = [g_w2_0, g_w2_1]
    return loss, grad_x, g


def _block_diag(w):
    d, g, n, _ = w.shape
    eye = jnp.eye(g, dtype=w.dtype)
    return (w[:, :, :, None, :] * eye[None, :, None, :, None]).reshape(d, g * n, g * n)


def _block_diag_extract(wbd, g):
    d, gn, _ = wbd.shape
    n = gn // g
    blocks = wbd.reshape(d, g, n, g, n)
    return jnp.stack([blocks[:, i, :, i, :] for i in range(g)], axis=1)


def _prepare_weights(big, full):
    w = {k: full[k] for k in ("norm_mix", "norm_mlp", "norm_final", "hg_lb_logits")}
    for k in ("rg_conv_w", "rg_conv_b", "rg_b_a", "rg_b_x", "rg_lambda", "hg_norm", "gla_b_gate", "gla_norm"):
        w[k] = full[k][0]
    w["rg_wa_bd"] = _block_diag(full["rg_w_a"][0])
    w["rg_wx_bd"] = _block_diag(full["rg_w_x"][0])
    up = full["gla_w_gate_up"][0]
    rank = up.shape[1]
    pad = jnp.zeros((2, LANES, up.shape[2]), F32)
    w["gla_w_up_pad"] = pad.at[0, 0:rank].set(up[0]).at[1, rank:2 * rank].set(up[1])
    w["mlp_w1"] = list(big["mlp_w1"])
    w["mlp_w2"] = [t.reshape(-1, t.shape[-1]) for t in big["mlp_w2"]]
    w["ab_w_in"] = big["ab_w_in"]
    w["ab_w_out"] = big["ab_w_out"].reshape(-1, big["ab_w_out"].shape[-1])
    w["gla_w_out"] = big["gla_w_out"].reshape(-1, big["gla_w_out"].shape[-1])
    gla_in = _join_chips(big["gla_w_in"], 1)
    w["gla_w_in_pad"] = jnp.pad(gla_in, ((0, 0), (0, GLA_IN_PAD - gla_in.shape[1])))
    return w


def _finish_grads(g, rank=16, gla_in_width=3104, rg_blocks=8):
    def chip_major(t):
        return t.reshape(N_CHIPS, t.shape[0] // N_CHIPS, t.shape[1])

    big = {
        "mlp_w1": list(g["mlp_w1"]), "mlp_w2": [chip_major(t) for t in g["mlp_w2"]],
        "ab_w_in": g["ab_w_in"], "ab_w_out": chip_major(g["ab_w_out"]),
        "gla_w_in": _split_chips(g["gla_w_in_pad"][:, :gla_in_width], 1), "gla_w_out": chip_major(g["gla_w_out"]),
    }
    small = {
        "norm_mix": g["norm_mix"], "norm_mlp": g["norm_mlp"], "norm_final": g["norm_final"][0],
        "rg_conv_w": g["rg_conv_w"][None], "rg_conv_b": g["rg_conv_b"],
        "rg_w_a": _block_diag_extract(g["rg_wa_bd"], rg_blocks)[None], "rg_b_a": g["rg_b_a"][None],
        "rg_w_x": _block_diag_extract(g["rg_wx_bd"], rg_blocks)[None], "rg_b_x": g["rg_b_x"][None],
        "rg_lambda": g["rg_lambda"][None], "hg_lb_logits": g["hg_lb_logits"], "hg_norm": g["hg_norm"],
        "gla_w_gate_up": jnp.stack([g["gla_w_up_pad"][0, 0:rank], g["gla_w_up_pad"][1, rank:2 * rank]])[None],
        "gla_b_gate": g["gla_b_gate"][None], "gla_norm": g["gla_norm"],
    }
    return big, small


MATRICES = (("mlp_w1", 0), ("mlp_w1", 1), ("mlp_w2", 0), ("mlp_w2", 1), ("ab_w_in", 0), ("ab_w_out", 0),
            ("gla_w_in", 0), ("gla_w_out", 0))
SMALL_SHARDED = ("rg_conv_w", "rg_b_a", "rg_b_x", "rg_lambda", "gla_w_gate_up", "gla_b_gate", "gla_norm")
SMALL_REPLICATED = ("norm_mix", "norm_mlp", "norm_final", "rg_conv_b", "rg_w_a", "rg_w_x", "hg_lb_logits", "hg_norm")
WEIGHTS = ("norm_mix", "norm_mlp", "norm_final", "mlp_w1", "mlp_w2", "ab_w_in", "ab_w_out", "rg_conv_w", "rg_conv_b",
           "rg_w_a", "rg_b_a", "rg_w_x", "rg_b_x", "rg_lambda", "hg_lb_logits", "hg_norm", "gla_w_in", "gla_w_out",
           "gla_w_gate_up", "gla_b_gate", "gla_norm")
ROW_ALIGN = 16


def _pack(arrays, lead=0):
    head = arrays[0].shape[:lead]
    flat = jnp.concatenate([a.reshape(head + (-1,)) for a in arrays], axis=lead)
    n = flat.shape[-1]
    quantum = LANES * ROW_ALIGN
    padded = -(-n // quantum) * quantum
    if padded != n:
        flat = jnp.pad(flat, [(0, 0)] * lead + [(0, padded - n)])
    return flat.reshape(head + (padded // LANES, LANES))


def _unpack(buf, shapes, lead=0):
    head = buf.shape[:lead]
    flat = buf.reshape(head + (-1,))
    out, off = [], 0
    for s in shapes:
        n = 1
        for v in s:
            n *= v
        out.append(lax.slice_in_dim(flat, off, off + n, axis=lead).reshape(head + tuple(s)))
        off += n
    return out


def _join_chips(gathered, axis):
    t = jnp.moveaxis(gathered, 0, axis)
    return t.reshape(t.shape[:axis] + (t.shape[axis] * t.shape[axis + 1],) + t.shape[axis + 2:])


def _split_chips(full, axis):
    s = full.shape
    t = full.reshape(s[:axis] + (N_CHIPS, s[axis] // N_CHIPS) + s[axis + 1:])
    return jnp.moveaxis(t, axis, 0)


_ANY = pl.BlockSpec(memory_space=pl.ANY)


def _place():
    return lax.axis_index("x"), lax.axis_index("y"), lax.axis_index("c")


def _into_slot(name, src, slot, n_slots, dtype, tm, layer=None):
    r, lanes = src.shape[-2:]
    tm = _row_tile(r, tm, ROW_ALIGN)

    def body(slot_ref, in_ref, o_ref):
        o_ref[...] = in_ref[...].astype(o_ref.dtype)

    if layer is None:
        in_spec = pl.BlockSpec((tm, lanes), lambda i, slot_ref: (i, 0))
    else:
        in_spec = pl.BlockSpec((None, tm, lanes), lambda i, slot_ref: (layer, i, 0))
    grid_spec = pltpu.PrefetchScalarGridSpec(
        num_scalar_prefetch=1, grid=(r // tm,), in_specs=[in_spec],
        out_specs=pl.BlockSpec((None, tm, lanes), lambda i, slot_ref: (slot_ref[0], i, 0)))
    return pl.pallas_call(
        body, name=name, grid_spec=grid_spec, out_shape=jax.ShapeDtypeStruct((n_slots, r, lanes), dtype),
        compiler_params=_params(("parallel",)),
    )(slot.reshape(1).astype(jnp.int32), src)


def _chip_peers():
    x, y, c = _place()
    return 2 * x + y, c, [(1 - x, y), (x, 1 - y), (1 - x, 1 - y)]


def _comm_call(name, body, ins, out_shapes, n_sems, aliases=None):
    return pl.pallas_call(
        body, name=name, in_specs=[_ANY] * len(ins), out_specs=[_ANY] * len(out_shapes), out_shape=out_shapes,
        input_output_aliases=aliases or {},
        scratch_shapes=[pltpu.SemaphoreType.DMA((n_sems,)), pltpu.SemaphoreType.DMA((n_sems,))],
    )(*ins)


def _gather_chips(name, bufs):
    n = len(bufs)

    def body(*refs):
        outs, send_sems, recv_sems = refs[n:2 * n], refs[2 * n], refs[2 * n + 1]
        x, y, c = _place()
        me, _, peers = _chip_peers()

        def rows(a, block, half):
            rh = outs[a].shape[1] // 2
            return outs[a].at[block, pl.ds(half * rh, rh)]

        def copy(a, j, block, half, to, sem):
            return pltpu.make_async_remote_copy(
                src_ref=rows(a, block, half), dst_ref=rows(a, block, half), send_sem=send_sems.at[sem],
                recv_sem=recv_sems.at[sem], device_id=to, device_id_type=MESH)

        def over_ici(a, j, block):
            px, py = peers[j]
            return copy(a, j, block, c, (px, py, c), 6 * a + j)

        def to_sibling(a, j, block, half):
            return copy(a, j, block, half, (x, y, 1 - c), 6 * a + 3 + j)

        sends = [over_ici(a, j, me) for a in range(n) for j in range(3)]
        for cp in sends:
            cp.start()
        for a in range(n):
            for j, (px, py) in enumerate(peers):
                over_ici(a, j, 2 * px + py).wait_recv()
                handed = to_sibling(a, j, 2 * px + py, c)
                handed.start()
                sends.append(handed)
        for a in range(n):
            for j, (px, py) in enumerate(peers):
                to_sibling(a, j, 2 * px + py, 1 - c).wait_recv()
        for cp in sends:
            cp.wait_send()

    shapes = [jax.ShapeDtypeStruct(b.shape, b.dtype) for b in bufs]
    return _comm_call(name, body, bufs, shapes, 6 * n, {a: a for a in range(n)})


def _pair_exchange(name, gs):
    n = len(gs)

    def body(*refs):
        ins, outs, send_sems, recv_sems = refs[:n], refs[n:2 * n], refs[2 * n], refs[2 * n + 1]
        x, y, c = _place()
        copies = [pltpu.make_async_remote_copy(
            src_ref=ins[a].at[:, 1 - c], dst_ref=outs[a], send_sem=send_sems.at[a], recv_sem=recv_sems.at[a],
            device_id=(x, y, 1 - c), device_id_type=MESH) for a in range(n)]
        for cp in copies:
            cp.start()
        for cp in copies:
            cp.wait()

    shapes = [jax.ShapeDtypeStruct((g.shape[0],) + g.shape[2:], g.dtype) for g in gs]
    return _comm_call(name, body, gs, shapes, n)


def _pair_add(name, g, got, c):
    n, _, rh, lanes = g.shape
    tm = _row_tile(rh, 2048, ROW_ALIGN)

    def body(c_ref, g_ref, got_ref, o_ref, o16_ref):
        s = g_ref[...] + got_ref[...]
        o_ref[...] = s
        o16_ref[...] = s.astype(BF16)

    out_spec = pl.BlockSpec((None, tm, lanes), lambda s, i, c_ref: (s, i, 0))
    grid_spec = pltpu.PrefetchScalarGridSpec(
        num_scalar_prefetch=1, grid=(n, rh // tm),
        in_specs=[pl.BlockSpec((None, None, tm, lanes), lambda s, i, c_ref: (s, c_ref[0], i, 0)),
                  pl.BlockSpec((None, tm, lanes), lambda s, i, c_ref: (s, i, 0))],
        out_specs=[out_spec, out_spec])
    return pl.pallas_call(
        body, name=name, grid_spec=grid_spec,
        out_shape=[jax.ShapeDtypeStruct((n, rh, lanes), F32), jax.ShapeDtypeStruct((n, rh, lanes), BF16)],
        compiler_params=_params(("parallel", "parallel")),
    )(c.reshape(1).astype(jnp.int32), g, got)


def _chip_scatter(name, ps):
    n = len(ps)

    def body(*refs):
        ins, outs, send_sems, recv_sems = refs[:n], refs[n:2 * n], refs[2 * n], refs[2 * n + 1]
        me, c, peers = _chip_peers()

        def copy(a, j, src_block, dst_block):
            px, py = peers[j]
            return pltpu.make_async_remote_copy(
                src_ref=ins[a].at[src_block], dst_ref=outs[a].at[dst_block], send_sem=send_sems.at[3 * a + j],
                recv_sem=recv_sems.at[3 * a + j], device_id=(px, py, c), device_id_type=MESH)

        sends = [copy(a, j, 2 * px + py, me) for a in range(n) for j, (px, py) in enumerate(peers)]
        for cp in sends:
            cp.start()
        for a in range(n):
            for j, (px, py) in enumerate(peers):
                copy(a, j, me, 2 * px + py).wait_recv()
        for cp in sends:
            cp.wait_send()

    shapes = [jax.ShapeDtypeStruct(p.shape, p.dtype) for p in ps]
    return _comm_call(name, body, ps, shapes, 3 * n)


def _sum_ring(name, own, got, chip, core):
    n, rh, lanes = own.shape
    tm = _row_tile(rh, 2048, ROW_ALIGN)

    def body(idx_ref, own_ref, g1_ref, g2_ref, g3_ref, o_ref):
        o_ref[...] = ((own_ref[...] + g1_ref[...].astype(F32)) + g2_ref[...].astype(F32)) + g3_ref[...].astype(F32)

    def block(k):
        return pl.BlockSpec((None, tm, lanes), lambda i, idx_ref: ((idx_ref[0] + k) % n, i, 0))

    grid_spec = pltpu.PrefetchScalarGridSpec(
        num_scalar_prefetch=1, grid=(rh // tm,), in_specs=[block(0), block(1), block(2), block(3)],
        out_specs=pl.BlockSpec((None, tm, lanes), lambda i, idx_ref: (idx_ref[1], i, 0)))
    return pl.pallas_call(
        body, name=name, grid_spec=grid_spec, out_shape=jax.ShapeDtypeStruct((2, rh, lanes), F32),
        compiler_params=_params(("parallel",)),
    )(jnp.stack([chip, core]).astype(jnp.int32), own, got, got, got)


def _pair_gather(name, bufs):
    n = len(bufs)

    def body(*refs):
        ins, outs, send_sems, recv_sems = refs[:n], refs[n:2 * n], refs[2 * n], refs[2 * n + 1]
        x, y, c = _place()

        def copy(a, block):
            return pltpu.make_async_remote_copy(
                src_ref=ins[a].at[block], dst_ref=outs[a].at[block], send_sem=send_sems.at[a],
                recv_sem=recv_sems.at[a], device_id=(x, y, 1 - c), device_id_type=MESH)

        sends = [copy(a, c) for a in range(n)]
        for cp in sends:
            cp.start()
        for a in range(n):
            copy(a, 1 - c).wait_recv()
        for cp in sends:
            cp.wait_send()

    shapes = [jax.ShapeDtypeStruct(b.shape, b.dtype) for b in bufs]
    return _comm_call(name, body, bufs, shapes, n, {a: a for a in range(n)})


def _gather_all(name, s):
    def body(in_ref, out_ref, send_sems, recv_sems, local_sem):
        x, y, c = _place()
        me = 4 * x + 2 * y + c
        peers = []
        for mask in range(1, N_DEV):
            fx, fy, fc = (mask >> 2) & 1, (mask >> 1) & 1, mask & 1
            peers.append((jnp.where(fx, 1 - x, x), jnp.where(fy, 1 - y, y), jnp.where(fc, 1 - c, c)))

        def copy(j, block):
            return pltpu.make_async_remote_copy(
                src_ref=in_ref, dst_ref=out_ref.at[block], send_sem=send_sems.at[j], recv_sem=recv_sems.at[j],
                device_id=peers[j], device_id_type=MESH)

        local = pltpu.make_async_copy(in_ref, out_ref.at[me], local_sem)
        local.start()
        sends = [copy(j, me) for j in range(N_DEV - 1)]
        for cp in sends:
            cp.start()
        for j, (px, py, pc) in enumerate(peers):
            copy(j, 4 * px + 2 * py + pc).wait_recv()
        for cp in sends:
            cp.wait_send()
        local.wait()

    return pl.pallas_call(
        body, name=name, in_specs=[_ANY], out_specs=_ANY,
        out_shape=jax.ShapeDtypeStruct((N_DEV,) + s.shape, s.dtype),
        scratch_shapes=[pltpu.SemaphoreType.DMA((N_DEV - 1,)), pltpu.SemaphoreType.DMA((N_DEV - 1,)),
                        pltpu.SemaphoreType.DMA],
    )(s)


def _sum_blocks(name, stacked, tm):
    n, r, lanes = stacked.shape

    def body(in_ref, o_ref):
        acc = in_ref[0]
        for j in range(1, n):
            acc = acc + in_ref[j]
        o_ref[...] = acc

    return pl.pallas_call(
        body, name=name, grid=(r // tm,), in_specs=[pl.BlockSpec((n, tm, lanes), lambda i: (0, i, 0))],
        out_specs=pl.BlockSpec((tm, lanes), lambda i: (i, 0)), out_shape=jax.ShapeDtypeStruct((r, lanes), F32),
        compiler_params=_params(("parallel",)),
    )(stacked)


def _row_tile(rows, pref, align):
    best = None
    for t in range(align, min(rows, pref) + 1, align):
        if rows % t == 0:
            best = t
    assert best is not None, (rows, pref, align)
    return best


def _adam(name, w, g, m, v):
    rows, width = w.shape
    tm = _row_tile(rows, max(8, 4096 * LANES // width), 8)
    args = [(t, width, 0) for t in (w, g, m, v)]
    return _rowcall(name, _adam_fn, args, [], [(width, F32)] * 3, tm=tm)


def kernel(x, norm_mix, norm_mlp, norm_final, mlp_w1, mlp_w2, ab_w_in, ab_w_out, rg_conv_w, rg_conv_b, rg_w_a, rg_b_a, rg_w_x, rg_b_x, rg_lambda, hg_lb_logits, hg_norm, gla_w_in, gla_w_out, gla_w_gate_up, gla_b_gate, gla_norm, loss_target, m_norm_mix, m_norm_mlp, m_norm_final, m_mlp_w1, m_mlp_w2, m_ab_w_in, m_ab_w_out, m_rg_conv_w, m_rg_conv_b, m_rg_w_a, m_rg_b_a, m_rg_w_x, m_rg_b_x, m_rg_lambda, m_hg_lb_logits, m_hg_norm, m_gla_w_in, m_gla_w_out, m_gla_w_gate_up, m_gla_b_gate, m_gla_norm, v_norm_mix, v_norm_mlp, v_norm_final, v_mlp_w1, v_mlp_w2, v_ab_w_in, v_ab_w_out, v_rg_conv_w, v_rg_conv_b, v_rg_w_a, v_rg_b_a, v_rg_w_x, v_rg_b_x, v_rg_lambda, v_hg_lb_logits, v_hg_norm, v_gla_w_in, v_gla_w_out, v_gla_w_gate_up, v_gla_b_gate, v_gla_norm):
    w = dict(norm_mix=norm_mix, norm_mlp=norm_mlp, norm_final=norm_final, mlp_w1=mlp_w1, mlp_w2=mlp_w2, ab_w_in=ab_w_in, ab_w_out=ab_w_out, rg_conv_w=rg_conv_w, rg_conv_b=rg_conv_b, rg_w_a=rg_w_a, rg_b_a=rg_b_a, rg_w_x=rg_w_x, rg_b_x=rg_b_x, rg_lambda=rg_lambda, hg_lb_logits=hg_lb_logits, hg_norm=hg_norm, gla_w_in=gla_w_in, gla_w_out=gla_w_out, gla_w_gate_up=gla_w_gate_up, gla_b_gate=gla_b_gate, gla_norm=gla_norm)
    m = dict(norm_mix=m_norm_mix, norm_mlp=m_norm_mlp, norm_final=m_norm_final, mlp_w1=m_mlp_w1, mlp_w2=m_mlp_w2, ab_w_in=m_ab_w_in, ab_w_out=m_ab_w_out, rg_conv_w=m_rg_conv_w, rg_conv_b=m_rg_conv_b, rg_w_a=m_rg_w_a, rg_b_a=m_rg_b_a, rg_w_x=m_rg_w_x, rg_b_x=m_rg_b_x, rg_lambda=m_rg_lambda, hg_lb_logits=m_hg_lb_logits, hg_norm=m_hg_norm, gla_w_in=m_gla_w_in, gla_w_out=m_gla_w_out, gla_w_gate_up=m_gla_w_gate_up, gla_b_gate=m_gla_b_gate, gla_norm=m_gla_norm)
    v = dict(norm_mix=v_norm_mix, norm_mlp=v_norm_mlp, norm_final=v_norm_final, mlp_w1=v_mlp_w1, mlp_w2=v_mlp_w2, ab_w_in=v_ab_w_in, ab_w_out=v_ab_w_out, rg_conv_w=v_rg_conv_w, rg_conv_b=v_rg_conv_b, rg_w_a=v_rg_w_a, rg_b_a=v_rg_b_a, rg_w_x=v_rg_w_x, rg_b_x=v_rg_b_x, rg_lambda=v_rg_lambda, hg_lb_logits=v_hg_lb_logits, hg_norm=v_hg_norm, gla_w_in=v_gla_w_in, gla_w_out=v_gla_w_out, gla_w_gate_up=v_gla_w_gate_up, gla_b_gate=v_gla_b_gate, gla_norm=v_gla_norm)
    chip = 2 * lax.axis_index("x") + lax.axis_index("y")
    core = lax.axis_index("c")
    sharded_shapes = [w[n].shape for n in SMALL_SHARDED]

    slots = [_into_slot(f"cast_{n}{layer}", w[n], chip, N_CHIPS, BF16, 512, layer) for n, layer in MATRICES]
    gathered = _gather_chips("gather_weights", slots)
    big = {n: [] for n, _ in MATRICES}
    for (n, _), t in zip(MATRICES, gathered):
        big[n].append(t)
    big = {n: (v if n in ("mlp_w1", "mlp_w2") else v[0]) for n, v in big.items()}
    vectors = _pack([w[n] for n in SMALL_SHARDED])
    vectors = _into_slot("place_vectors", vectors, chip, N_CHIPS, F32, vectors.shape[0])
    small_all = _unpack(_gather_chips("gather_vectors", [vectors])[0], sharded_shapes, lead=1)
    full = {n: w[n] for n in SMALL_REPLICATED}
    for n, t in zip(SMALL_SHARDED, small_all):
        full[n] = _join_chips(t, t.ndim - 2)

    loss_part, grad_x, g_kernel = _local_step(x[0], loss_target[0], _prepare_weights(big, full))
    g_big, g_full = _finish_grads(g_kernel)
    loss = lax.psum(loss_part[0, 0], ("x", "y", "c"))

    g_list = [g_big[n][layer] if n in ("mlp_w1", "mlp_w2") else g_big[n] for n, layer in MATRICES]
    halves = [t.reshape(N_CHIPS, 2, t.shape[1] // 2, t.shape[2]) for t in g_list]
    from_sibling = _pair_exchange("reduce_pair", halves)
    chip_part = [_pair_add(f"reduce_pair_add{i}", h, s, core) for i, (h, s) in enumerate(zip(halves, from_sibling))]
    from_chips = _chip_scatter("reduce_chips", [p16 for _, p16 in chip_part])
    mine = [_sum_ring(f"reduce_chips_add{i}", p32, f, chip, core)
            for i, ((p32, _), f) in enumerate(zip(chip_part, from_chips))]
    reduced = [t.reshape(2 * t.shape[1], t.shape[2]) for t in _pair_gather("reduce_share", mine)]
    by_name = {n: [] for n, _ in MATRICES}
    for (n, _), t in zip(MATRICES, reduced):
        by_name[n].append(t)
    grads = {n: jnp.stack(v) for n, v in by_name.items()}

    small_names = SMALL_REPLICATED + SMALL_SHARDED
    g_small = _pack([g_full[n] for n in small_names])
    g_small_all = _gather_all("reduce_small", g_small)
    g_small_red = _sum_blocks("reduce_small_add", g_small_all, g_small.shape[0])
    g_small_full = dict(zip(small_names, _unpack(g_small_red, [g_full[n].shape for n in small_names])))
    for n in SMALL_REPLICATED:
        grads[n] = g_small_full[n]
    for n in SMALL_SHARDED:
        width = w[n].shape[-1]
        grads[n] = lax.dynamic_slice_in_dim(g_small_full[n], chip * width, width, axis=g_small_full[n].ndim - 1)

    delta, new_m, new_v = {}, {}, {}
    for n in by_name:
        flat = [t.reshape(-1, t.shape[-1]) for t in (w[n], grads[n], m[n], v[n])]
        for dst, t in zip((delta, new_m, new_v), _adam(f"adam_{n}", *flat)):
            dst[n] = t.reshape(w[n].shape)
    small_shapes = [w[n].shape for n in small_names]
    packs = [_pack([src[n] for n in small_names]) for src in (w, grads, m, v)]
    d_small, m_small, v_small = _adam("adam_small", *packs)
    for dst, buf in ((delta, d_small), (new_m, m_small), (new_v, v_small)):
        dst.update(zip(small_names, _unpack(buf, small_shapes)))

    return (loss, grad_x[None], *[grads[n] for n in WEIGHTS], *[delta[n] for n in WEIGHTS],
            *[new_m[n] for n in WEIGHTS], *[new_v[n] for n in WEIGHTS])
```

```python
import functools

import jax
import jax.numpy as jnp
from jax import lax
from jax.experimental import pallas as pl
from jax.experimental.pallas import tpu as pltpu

F32 = jnp.float32
BF16 = jnp.bfloat16
MESH = pl.DeviceIdType.MESH

LANES = 128
CHUNK = 64
EPS = 1e-6
RG_C = 8.0
N_CHIPS = 4
N_DEV = 8
GLA_IN_PAD = 3200
VMEM_LIMIT = 56 * 1024 * 1024

ADAM_LR = 0.001
ADAM_B1 = 0.9
ADAM_B2 = 0.999
ADAM_EPS = 1e-08
ADAM_WD = 0.01
ADAM_STEP = 10


def _raw_dot(a, b, ca, cb):
    return lax.dot_general(a.astype(BF16), b.astype(BF16), (((ca,), (cb,)), ((), ())),
                           preferred_element_type=F32)


def _raw_nn(a, b):
    return _raw_dot(a, b, 1, 0)


def _raw_nt(a, b):
    return _raw_dot(a, b, 1, 1)


def _raw_tn(a, b):
    return _raw_dot(a, b, 0, 0)


@jax.custom_vjp
def _dot_nn(a, b):
    return _raw_nn(a, b)


def _dot_nn_fwd(a, b):
    return _raw_nn(a, b), (a, b)


def _dot_nn_bwd(res, g):
    a, b = res
    return _raw_nt(g, b), _raw_tn(a, g)


_dot_nn.defvjp(_dot_nn_fwd, _dot_nn_bwd)


@jax.custom_vjp
def _dot_nt(a, b):
    return _raw_nt(a, b)


def _dot_nt_fwd(a, b):
    return _raw_nt(a, b), (a, b)


def _dot_nt_bwd(res, g):
    a, b = res
    return _raw_nn(g, b), _raw_tn(g, a)


_dot_nt.defvjp(_dot_nt_fwd, _dot_nt_bwd)


@jax.custom_vjp
def _dot_tn(a, b):
    return _raw_tn(a, b)


def _dot_tn_fwd(a, b):
    return _raw_tn(a, b), (a, b)


def _dot_tn_bwd(res, g):
    a, b = res
    return _raw_nt(b, g), _raw_nn(a, g)


_dot_tn.defvjp(_dot_tn_fwd, _dot_tn_bwd)


def _tile(n, pref):
    if n <= pref:
        return n
    t = (pref // LANES) * LANES
    while t > LANES and n % t:
        t -= LANES
    assert n % t == 0, (n, pref)
    return t


def _params(sem):
    return pltpu.CompilerParams(dimension_semantics=sem, vmem_limit_bytes=VMEM_LIMIT)


def _rowcall(name, fn, rows, pars, row_outs, par_outs=(), tm=256):
    n_rows = rows[0][0].shape[0]
    tm = min(tm, n_rows)
    assert n_rows % tm == 0
    n_r, n_p, n_ro = len(rows), len(pars), len(row_outs)

    def body(*refs):
        vals = [r[...].astype(F32) for r in refs[:n_r + n_p]]
        outs = fn(*vals)
        o_refs = refs[n_r + n_p:n_r + n_p + n_ro]
        po_refs = refs[n_r + n_p + n_ro:]
        for o_ref, val in zip(o_refs, outs[:n_ro]):
            o_ref[...] = val.astype(o_ref.dtype)
        first = pl.program_id(0) == 0
        for po_ref, val in zip(po_refs, outs[n_ro:]):
            @pl.when(first)
            def _():
                po_ref[...] = val

            @pl.when(jnp.logical_not(first))
            def _():
                po_ref[...] += val

    def const_map(nd):
        return lambda i: (0,) * nd

    def row_spec(w, cb):
        return pl.BlockSpec((tm, w), lambda i: (i, cb))

    in_specs = [row_spec(w, cb) for _, w, cb in rows]
    in_specs += [pl.BlockSpec(p.shape, const_map(p.ndim)) for p in pars]
    out_specs = [pl.BlockSpec((tm, w), lambda i: (i, 0)) for w, _ in row_outs]
    out_specs += [pl.BlockSpec(tuple(s), const_map(len(s))) for s in par_outs]
    out_shape = [jax.ShapeDtypeStruct((n_rows, w), dt) for w, dt in row_outs]
    out_shape += [jax.ShapeDtypeStruct(tuple(s), F32) for s in par_outs]
    return pl.pallas_call(
        body, name=name, grid=(n_rows // tm,), in_specs=in_specs, out_specs=out_specs, out_shape=out_shape,
        compiler_params=_params(("arbitrary",) if par_outs else ("parallel",)),
    )(*[r[0] for r in rows], *pars)


def _vjp_of(fn, n_prim, n_out, n_par, n_pass=0):
    def bwd(*args):
        prim = args[:n_prim]
        cts = args[n_prim:n_prim + n_out]
        passes = args[n_prim + n_out:n_prim + n_out + 2 * n_pass]
        pars = args[n_prim + n_out + 2 * n_pass:]
        _, vjp = jax.vjp(fn, *prim, *pars)
        grads = vjp(tuple(cts))
        sums = tuple(passes[2 * i] + passes[2 * i + 1] for i in range(n_pass))
        return tuple(grads[:n_prim]) + sums + tuple(grads[n_prim:])
    return bwd


def _mm(name, a, b, mode="nn", extras=(), epi=None, out_dtypes=(F32,), a_pro=None, out_split=None,
        tm=1024, tn=1024, tk=1024):
    split = b.shape[0] if b.ndim == 3 else None
    b_rows, b_cols = b.shape[-2:]
    if mode == "nn":
        (m, k), n = a.shape, b_cols * (split or 1)
    elif mode == "nt":
        (m, k), n = a.shape, b_rows
        assert k == b_cols * (split or 1)
    else:
        assert split is None
        (k, m), n = a.shape, b_cols
    tm, tk = _tile(m, tm), _tile(k, tk)
    tn = _tile(n // out_split, tn) if out_split else _tile(n, tn)
    if split and mode == "nn":
        tn = _tile(b_cols, tn)
    if split and mode == "nt":
        tk = _tile(b_cols, tk)
    nk = k // tk
    raw = {"nn": _raw_nn, "nt": _raw_nt, "tn": _raw_tn}[mode]
    n_e, n_o = len(extras), len(out_dtypes)
    if epi is None:
        epi = lambda acc: (acc,)

    def body(a_ref, b_ref, *rest):
        e_refs, o_refs, acc = rest[:n_e], rest[n_e:n_e + n_o], rest[-1]
        kk = pl.program_id(2)

        @pl.when(kk == 0)
        def _():
            acc[...] = jnp.zeros_like(acc)

        a_tile = a_ref[...] if a_pro is None else a_pro(a_ref[...].astype(F32))
        acc[...] += raw(a_tile, b_ref[...])

        @pl.when(kk == nk - 1)
        def _():
            res = epi(acc[...], *[e[...].astype(F32) for e in e_refs])
            for o_ref, r in zip(o_refs, res):
                o_ref[...] = r.astype(o_ref.dtype)

    a_spec = pl.BlockSpec((tk, tm), lambda i, j, kk: (kk, i)) if mode == "tn" else pl.BlockSpec((tm, tk), lambda i, j, kk: (i, kk))
    if split and mode == "nn":
        per = b_cols // tn
        b_spec = pl.BlockSpec((None, tk, tn), lambda i, j, kk: (j // per, kk, j % per))
    elif split:
        per = b_cols // tk
        b_spec = pl.BlockSpec((None, tn, tk), lambda i, j, kk: (kk // per, j, kk % per))
    elif mode == "nt":
        b_spec = pl.BlockSpec((tn, tk), lambda i, j, kk: (j, kk))
    else:
        b_spec = pl.BlockSpec((tk, tn), lambda i, j, kk: (kk, j))
    mn_spec = pl.BlockSpec((tm, tn), lambda i, j, kk: (i, j))
    if out_split:
        assert not extras
        per_out = n // out_split // tn
        out_spec = pl.BlockSpec((None, tm, tn), lambda i, j, kk: (j // per_out, i, j % per_out))
        out_shapes = [jax.ShapeDtypeStruct((out_split, m, n // out_split), dt) for dt in out_dtypes]
    else:
        out_spec = mn_spec
        out_shapes = [jax.ShapeDtypeStruct((m, n), dt) for dt in out_dtypes]
    outs = pl.pallas_call(
        body, name=name, grid=(m // tm, n // tn, nk),
        in_specs=[a_spec, b_spec] + [mn_spec] * n_e, out_specs=[out_spec] * n_o,
        out_shape=out_shapes,
        scratch_shapes=[pltpu.VMEM((tm, tn), F32)],
        compiler_params=_params(("parallel", "parallel", "arbitrary")),
    )(a, b, *extras)
    return outs[0] if n_o == 1 else outs


def _sigmoid(x):
    return jax.nn.sigmoid(x)


def _silu(x):
    return x * _sigmoid(x)


def _softplus(x):
    return jnp.maximum(x, 0.0) + jnp.log1p(jnp.exp(-jnp.abs(x)))


def _rmsnorm_fn(x, gain):
    return (x * lax.rsqrt(jnp.mean(x * x, axis=-1, keepdims=True) + EPS) * gain,)


def _head_norm(o, gain, n_heads):
    w = o.shape[-1] // n_heads
    parts = []
    for h in range(n_heads):
        oh = o[:, h * w:(h + 1) * w]
        parts.append(oh * lax.rsqrt(jnp.mean(oh * oh, axis=-1, keepdims=True) + EPS))
    return jnp.concatenate(parts, axis=-1) * gain


@jax.custom_jvp
def _neg_expm1(x):
    u = jnp.exp(x)
    is_one = u == 1.0
    return jnp.where(is_one, -x, (1.0 - u) * x / jnp.log(jnp.where(is_one, 2.0, u)))


@_neg_expm1.defjvp
def _neg_expm1_jvp(primals, tangents):
    (x,), (t,) = primals, tangents
    return _neg_expm1(x), -jnp.exp(x) * t


def _rg_gates_fn(xc, wa, wx, ba, bx, lam):
    outs = []
    for d in range(2):
        r = _sigmoid(_dot_nn(xc, wa[d]) + ba[d:d + 1])
        i = _sigmoid(_dot_nn(xc, wx[d]) + bx[d:d + 1])
        log_a = -RG_C * r * _softplus(-lam[d:d + 1])
        outs.append(jnp.exp(log_a))
        outs.append(jnp.sqrt(_neg_expm1(2.0 * log_a)) * (i * xc))
    return tuple(outs)


def _hg_pre_fn(q, f_f, f_b, logits):
    mx = jnp.maximum(logits[0:1], logits[1:2])
    e0 = jnp.exp(logits[0:1] - mx)
    e1 = jnp.exp(logits[1:2] - mx)
    lb = e0 / (e0 + e1)
    outs = [_silu(q)]
    for f in (f_f, f_b):
        outs.append((1.0 - lb) * _sigmoid(-f))
        outs.append(jnp.log(lb + (1.0 - lb) * _sigmoid(f)))
    return tuple(outs)


def _post0_fn(hs, ga, o, g, gain):
    ya = hs * jax.nn.gelu(ga, approximate=True)
    yb = _head_norm(o, gain, 4) * _silu(g)
    return (jnp.concatenate([ya, yb], axis=-1),)


def _post0_fwd_fn(h_f, h_b, ga, o_f, o_b, g, gain):
    return _post0_fn(h_f + h_b, ga, o_f + o_b, g, gain)


def _post0_bwd_fn(h_f, h_b, ga, o_f, o_b, g, dmix, gain):
    _, vjp = jax.vjp(_post0_fn, h_f + h_b, ga, o_f + o_b, g, gain)
    return vjp((dmix,))


def _gla_pre_fn(q, lr, w_up, b_gate):
    outs = [q * (128.0 ** -0.5)]
    for d in range(2):
        z = _dot_nn(lr, w_up[d]) + b_gate[d:d + 1]
        outs.append(-_softplus(-z) * (1.0 / 16.0))
    return tuple(outs)


def _gla_post_fn(o, r, gain):
    return (_head_norm(o, gain, 4) * _silu(r),)


def _gla_post_fwd_fn(o_f, o_b, r, gain):
    return _gla_post_fn(o_f + o_b, r, gain)


def _gla_post_bwd_fn(o_f, o_b, r, dmix, gain):
    _, vjp = jax.vjp(_gla_post_fn, o_f + o_b, r, gain)
    return vjp((dmix,))


def _relu2_bwd_epi(acc, hid):
    return (acc * 2.0 * jnp.maximum(hid, 0.0),)


def _relu2(x):
    r = jnp.maximum(x, 0.0)
    return r * r


def _add_epi(acc, res):
    return (acc + res,)


def _loss_head_fn(h, target, gain):
    def f(h, gain):
        y = _rmsnorm_fn(h, gain)[0]
        err = y - target
        return 0.5 * jnp.sum(jnp.mean(err * err, axis=-1, keepdims=True))
    loss, (dh, dgain) = jax.value_and_grad(f, argnums=(0, 1))(h, gain)
    return dh, jnp.full((1, LANES), loss, F32), dgain


def _adam_fn(w, g, m, v):
    m2 = ADAM_B1 * m + (1.0 - ADAM_B1) * g
    v2 = ADAM_B2 * v + (1.0 - ADAM_B2) * (g * g)
    m_hat = m2 / (1.0 - ADAM_B1 ** ADAM_STEP)
    v_hat = v2 / (1.0 - ADAM_B2 ** ADAM_STEP)
    delta = -ADAM_LR * (m_hat / (jnp.sqrt(v_hat) + ADAM_EPS) + ADAM_WD * w)
    return delta, m2, v2


def _shifted(x, t_idx, off):
    n = x.shape[0]
    rolled = pltpu.roll(x, (-off) % n, 0)
    valid = (t_idx + off >= 0) & (t_idx + off < n)
    return jnp.where(valid, rolled, 0.0)


def _conv_fwd(name, src, colblock, w, b):
    n_rows, width = src.shape[0], w.shape[1]

    def body(x_ref, w_ref, b_ref, o_ref):
        x = x_ref[...]
        t_idx = lax.broadcasted_iota(jnp.int32, x.shape, 0)
        acc = b_ref[...] + w_ref[2:3, :] * x
        acc += w_ref[0:1, :] * _shifted(x, t_idx, -2)
        acc += w_ref[1:2, :] * _shifted(x, t_idx, -1)
        acc += w_ref[3:4, :] * _shifted(x, t_idx, 1)
        o_ref[...] = acc

    nb = width // LANES
    return pl.pallas_call(
        body, name=name, grid=(nb,),
        in_specs=[pl.BlockSpec((n_rows, LANES), lambda j: (0, colblock * nb + j)),
                  pl.BlockSpec((4, LANES), lambda j: (0, j)), pl.BlockSpec((1, LANES), lambda j: (0, j))],
        out_specs=pl.BlockSpec((n_rows, LANES), lambda j: (0, j)),
        out_shape=jax.ShapeDtypeStruct((n_rows, width), F32),
        compiler_params=_params(("parallel",)),
    )(src, w, b)


def _conv_bwd(name, src, colblock, w, d):
    n_rows, width = src.shape[0], w.shape[1]

    def body(x_ref, w_ref, d_ref, dx_ref, dw_ref, db_ref):
        x = x_ref[...]
        g = d_ref[...]
        t_idx = lax.broadcasted_iota(jnp.int32, x.shape, 0)
        dx = w_ref[2:3, :] * g
        dx += w_ref[0:1, :] * _shifted(g, t_idx, 2)
        dx += w_ref[1:2, :] * _shifted(g, t_idx, 1)
        dx += w_ref[3:4, :] * _shifted(g, t_idx, -1)
        dx_ref[...] = dx.astype(dx_ref.dtype)
        dw_ref[0:1, :] = jnp.sum(g * _shifted(x, t_idx, -2), axis=0, keepdims=True)
        dw_ref[1:2, :] = jnp.sum(g * _shifted(x, t_idx, -1), axis=0, keepdims=True)
        dw_ref[2:3, :] = jnp.sum(g * x, axis=0, keepdims=True)
        dw_ref[3:4, :] = jnp.sum(g * _shifted(x, t_idx, 1), axis=0, keepdims=True)
        db_ref[...] = jnp.sum(g, axis=0, keepdims=True)

    nb = width // LANES
    return pl.pallas_call(
        body, name=name, grid=(nb,),
        in_specs=[pl.BlockSpec((n_rows, LANES), lambda j: (0, colblock * nb + j)),
                  pl.BlockSpec((4, LANES), lambda j: (0, j)),
                  pl.BlockSpec((n_rows, LANES), lambda j: (0, j))],
        out_specs=[pl.BlockSpec((n_rows, LANES), lambda j: (0, j)), pl.BlockSpec((4, LANES), lambda j: (0, j)),
                   pl.BlockSpec((1, LANES), lambda j: (0, j))],
        out_shape=[jax.ShapeDtypeStruct((n_rows, width), BF16), jax.ShapeDtypeStruct((4, width), F32),
                   jax.ShapeDtypeStruct((1, width), F32)],
        compiler_params=_params(("parallel",)),
    )(src, w, d)


_WHOLE = pl.BlockSpec(memory_space=pltpu.VMEM)
SCAN_UNROLL = 8


def _scan_fwd(name, a, u, reverse):
    n_rows, width = a.shape

    def body(a_ref, u_ref, h_ref):
        def step(i, h):
            t = (n_rows - 1 - i) if reverse else i
            h = a_ref[pl.ds(t, 1), :] * h + u_ref[pl.ds(t, 1), :]
            h_ref[pl.ds(t, 1), :] = h
            return h
        lax.fori_loop(0, n_rows, step, jnp.zeros((1, width), F32), unroll=SCAN_UNROLL)

    return pl.pallas_call(
        body, name=name, in_specs=[_WHOLE, _WHOLE], out_specs=_WHOLE,
        out_shape=jax.ShapeDtypeStruct((n_rows, width), F32),
        compiler_params=pltpu.CompilerParams(vmem_limit_bytes=VMEM_LIMIT),
    )(a, u)


def _scan_bwd(name, a, h, dh, reverse):
    n_rows, width = a.shape

    def body(a_ref, h_ref, dh_ref, du_ref, da_ref):
        def step(i, carry):
            t = i if reverse else (n_rows - 1 - i)
            g = dh_ref[pl.ds(t, 1), :] + carry
            du_ref[pl.ds(t, 1), :] = g
            tp = t + 1 if reverse else t - 1
            valid = (tp >= 0) & (tp < n_rows)
            h_prev = h_ref[pl.ds(jnp.clip(tp, 0, n_rows - 1), 1), :]
            da_ref[pl.ds(t, 1), :] = jnp.where(valid, g * h_prev, 0.0)
            return a_ref[pl.ds(t, 1), :] * g
        lax.fori_loop(0, n_rows, step, jnp.zeros((1, width), F32), unroll=SCAN_UNROLL)

    return pl.pallas_call(
        body, name=name, in_specs=[_WHOLE] * 3, out_specs=[_WHOLE] * 2,
        out_shape=[jax.ShapeDtypeStruct((n_rows, width), F32)] * 2,
        compiler_params=pltpu.CompilerParams(vmem_limit_bytes=VMEM_LIMIT),
    )(a, h, dh)


def _tri_mask(c, reverse):
    row = lax.broadcasted_iota(jnp.int32, (c, c), 0)
    col = lax.broadcasted_iota(jnp.int32, (c, c), 1)
    return (col >= row) if reverse else (col <= row)


def _cumsum_rows(x, reverse):
    tri = _tri_mask(x.shape[0], reverse).astype(BF16)
    hi = x.astype(BF16)
    rest = x - hi.astype(F32)
    mid = rest.astype(BF16)
    lo = (rest - mid.astype(F32)).astype(BF16)
    return _raw_nn(tri, hi) + _raw_nn(tri, mid) + _raw_nn(tri, lo)


@functools.partial(jax.custom_vjp, nondiff_argnums=(1,))
def _cumsum(x, reverse):
    return _cumsum_rows(x, reverse)


def _cumsum_fwd(x, reverse):
    return _cumsum_rows(x, reverse), None


def _cumsum_bwd(reverse, _, g):
    return (_cumsum_rows(g, not reverse),)


_cumsum.defvjp(_cumsum_fwd, _cumsum_bwd)


def _chunks_fn(qs, ks, vs, lfs, sts, reverses):
    n, c = len(qs), qs[0].shape[0]
    every = range(n)
    tris = [_tri_mask(c, r) for r in reverses]
    cums = [_cumsum(lfs[i], reverses[i]) for i in every]
    rid = lax.broadcasted_iota(jnp.int32, cums[0].shape, 0)

    def pick(cum, r):
        return jnp.sum(jnp.where(rid == r, cum, 0.0), axis=0, keepdims=True)

    refs = [pick(cums[i], (c - 1 - c // 2) if reverses[i] else c // 2) for i in every]
    lasts = [pick(cums[i], 0 if reverses[i] else c - 1) for i in every]
    q_in = [qs[i] * jnp.exp(cums[i] - refs[i]) for i in every]
    k_in = [ks[i] * jnp.exp(refs[i] - cums[i]) for i in every]
    scores = [jnp.where(tris[i], _dot_nt(q_in[i], k_in[i]), 0.0) for i in every]
    o_intra = [_dot_nn(scores[i], vs[i]) for i in every]
    q_out = [qs[i] * jnp.exp(cums[i]) for i in every]
    o_inter = [_dot_nt(q_out[i], sts[i]) for i in every]
    k_state = [ks[i] * jnp.exp(lasts[i] - cums[i]) for i in every]
    upd = [_dot_tn(vs[i], k_state[i]) for i in every]
    st_new = [sts[i] * jnp.exp(lasts[i]) + upd[i] for i in every]
    return [o_intra[i] + o_inter[i] for i in every], st_new


def _attn_fwd(name, q, k_f, k_b, v, lf_f, lf_b, n_heads, dk, dv):
    n_rows = q[0].shape[0]
    n_chunks = n_rows // CHUNK
    wk, wv = n_heads * dk, n_heads * dv

    def spec(width, off, rev):
        return pl.BlockSpec((CHUNK, width), lambda n: ((n_chunks - 1 - n) if rev else n, off))

    def sspec(rev):
        return pl.BlockSpec((None, n_heads, dv, dk), lambda n: ((n_chunks - 1 - n) if rev else n, 0, 0, 0))

    def body(qf, kf, vf, lff, qb, kb, vb, lfb, of_ref, ob_ref, sf_ref, sb_ref, st):
        @pl.when(pl.program_id(0) == 0)
        def _():
            st[...] = jnp.zeros_like(st)

        ins = ((qf, kf, vf, lff), (qb, kb, vb, lfb))
        chains = [(d, h) for d in range(2) for h in range(n_heads)]
        ck = [slice(h * dk, (h + 1) * dk) for h in range(n_heads)]
        cv = [slice(h * dv, (h + 1) * dv) for h in range(n_heads)]
        qs = [ins[d][0][:, ck[h]] for d, h in chains]
        ks = [ins[d][1][:, ck[h]] for d, h in chains]
        vs = [ins[d][2][:, cv[h]] for d, h in chains]
        lfs = [ins[d][3][:, ck[h]] for d, h in chains]
        sts = [st[d, h] for d, h in chains]
        os_, st_new = _chunks_fn(qs, ks, vs, lfs, sts, [d == 1 for d, _ in chains])
        for i, (d, h) in enumerate(chains):
            (sf_ref, sb_ref)[d][h] = sts[i]
            (of_ref, ob_ref)[d][:, cv[h]] = os_[i]
            st[d, h] = st_new[i]

    in_specs = [spec(wk, q[1], False), spec(wk, k_f[1], False), spec(wv, v[1], False), spec(wk, lf_f[1], False),
                spec(wk, q[1], True), spec(wk, k_b[1], True), spec(wv, v[1], True), spec(wk, lf_b[1], True)]
    return pl.pallas_call(
        body, name=name, grid=(n_chunks,), in_specs=in_specs,
        out_specs=[spec(wv, 0, False), spec(wv, 0, True), sspec(False), sspec(True)],
        out_shape=[jax.ShapeDtypeStruct((n_rows, wv), F32)] * 2
        + [jax.ShapeDtypeStruct((n_chunks, n_heads, dv, dk), F32)] * 2,
        scratch_shapes=[pltpu.VMEM((2, n_heads, dv, dk), F32)],
        compiler_params=_params(("arbitrary",)),
    )(q[0], k_f[0], v[0], lf_f[0], q[0], k_b[0], v[0], lf_b[0])


def _attn_bwd(name, q, k_f, k_b, v, lf_f, lf_b, st_f, st_b, do, n_heads, dk, dv, out_dtype=F32):
    n_rows = q[0].shape[0]
    n_chunks = n_rows // CHUNK
    wk, wv = n_heads * dk, n_heads * dv

    def spec(width, off, rev):
        return pl.BlockSpec((CHUNK, width), lambda n: (n if rev else (n_chunks - 1 - n), off))

    def sspec(rev):
        return pl.BlockSpec((None, n_heads, dv, dk), lambda n: (n if rev else (n_chunks - 1 - n), 0, 0, 0))

    def body(qf, kf, vf, lff, sf, dof, qb, kb, vb, lfb, sb, dob,
             dqf, dkf, dvf, dlff, dqb, dkb, dvb, dlfb, dst):
        @pl.when(pl.program_id(0) == 0)
        def _():
            dst[...] = jnp.zeros_like(dst)

        ins = ((qf, kf, vf, lff, sf, dof), (qb, kb, vb, lfb, sb, dob))
        outs = ((dqf, dkf, dvf, dlff), (dqb, dkb, dvb, dlfb))
        chains = [(d, h) for d in range(2) for h in range(n_heads)]
        ck = [slice(h * dk, (h + 1) * dk) for h in range(n_heads)]
        cv = [slice(h * dv, (h + 1) * dv) for h in range(n_heads)]
        qs = [ins[d][0][:, ck[h]] for d, h in chains]
        ks = [ins[d][1][:, ck[h]] for d, h in chains]
        vs = [ins[d][2][:, cv[h]] for d, h in chains]
        lfs = [ins[d][3][:, ck[h]] for d, h in chains]
        sts = [ins[d][4][h] for d, h in chains]
        dos = [ins[d][5][:, cv[h]] for d, h in chains]
        dsts = [dst[d, h] for d, h in chains]
        fn = functools.partial(_chunks_fn, reverses=[d == 1 for d, _ in chains])
        _, vjp = jax.vjp(fn, qs, ks, vs, lfs, sts)
        dqs, dks, dvs, dlfs, dst_prev = vjp((dos, dsts))
        for i, (d, h) in enumerate(chains):
            dq_r, dk_r, dv_r, dlf_r = outs[d]
            dq_r[:, ck[h]] = dqs[i].astype(dq_r.dtype)
            dk_r[:, ck[h]] = dks[i].astype(dk_r.dtype)
            dv_r[:, cv[h]] = dvs[i].astype(dv_r.dtype)
            dlf_r[:, ck[h]] = dlfs[i].astype(dlf_r.dtype)
            dst[d, h] = dst_prev[i]

    def dir_specs(kk, lf, rev):
        return [spec(wk, q[1], rev), spec(wk, kk[1], rev), spec(wv, v[1], rev), spec(wk, lf[1], rev), sspec(rev),
                spec(wv, 0, rev)]

    def dir_out_specs(rev):
        return [spec(wk, 0, rev), spec(wk, 0, rev), spec(wv, 0, rev), spec(wk, 0, rev)]

    shapes = [jax.ShapeDtypeStruct((n_rows, wk), out_dtype), jax.ShapeDtypeStruct((n_rows, wk), out_dtype),
              jax.ShapeDtypeStruct((n_rows, wv), out_dtype), jax.ShapeDtypeStruct((n_rows, wk), F32)]
    outs = pl.pallas_call(
        body, name=name, grid=(n_chunks,), in_specs=dir_specs(k_f, lf_f, False) + dir_specs(k_b, lf_b, True),
        out_specs=dir_out_specs(False) + dir_out_specs(True), out_shape=shapes + shapes,
        scratch_shapes=[pltpu.VMEM((2, n_heads, dv, dk), F32)],
        compiler_params=_params(("arbitrary",)),
    )(q[0], k_f[0], v[0], lf_f[0], st_f, do, q[0], k_b[0], v[0], lf_b[0], st_b, do)
    return outs[:4], outs[4:]


def _row2(v):
    return v.reshape(1, -1)


def _mlp_fwd(tag, h, gain, w1, w2):
    y = _rowcall(f"{tag}_norm", _rmsnorm_fn, [(h, h.shape[1], 0)], [gain], [(h.shape[1], BF16)], tm=512)[0]
    hid = _mm(f"{tag}_up", y, w1, out_dtypes=(BF16,))
    h_out = _mm(f"{tag}_down", hid, w2, a_pro=_relu2, extras=(h,), epi=_add_epi)
    return h_out, (y, hid)


def _mlp_bwd(tag, h, gain, w1, w2, saved, dh_out):
    y, hid = saved
    dhid = _mm(f"{tag}_dact", dh_out, w2, mode="nt", extras=(hid,), epi=_relu2_bwd_epi, out_dtypes=(BF16,))
    dw2 = _mm(f"{tag}_dw2", hid, dh_out, mode="tn", a_pro=_relu2)
    dw1 = _mm(f"{tag}_dw1", y, dhid, mode="tn", out_split=N_CHIPS)
    dy = _mm(f"{tag}_dy", dhid, w1, mode="nt")
    dh, dgain = _norm_bwd(f"{tag}_dnorm", h, gain, dy, dh_out)
    return dh, dgain, dw1, dw2


def _norm_bwd(name, h, gain, dy, dres):
    d = h.shape[1]

    def fn(h, dy, dres, gain):
        _, vjp = jax.vjp(lambda a, b: _rmsnorm_fn(a, b)[0], h, gain)
        dh, dgain = vjp(dy)
        return dh + dres, dgain

    dh, dgain = _rowcall(name, fn, [(h, d, 0), (dy, d, 0), (dres, d, 0)], [gain], [(d, F32)], [(1, d)], tm=512)
    return dh, dgain


def _local_step(x, target, w, pin=None, late=None):
    g = {}
    d_model = x.shape[1]
    rg_w = hg_w = d_model // 2

    h_a0 = x
    gain = _row2(w["norm_mix"][0])
    first_pars = [gain] if pin is None else [gain, pin]
    y0 = _rowcall("l0_norm", lambda h, gn, *_: _rmsnorm_fn(h, gn), [(h_a0, d_model, 0)], first_pars, [(d_model, BF16)], tm=512)[0]
    proj0 = _mm("l0_in", y0, w["ab_w_in"])
    conv_w, conv_b = w["rg_conv_w"], _row2(w["rg_conv_b"])
    xc = _conv_fwd("rg_conv", proj0, 0, conv_w, conv_b)
    gate_pars = [w["rg_wa_bd"], w["rg_wx_bd"], w["rg_b_a"], w["rg_b_x"], w["rg_lambda"]]
    a_f, u_f, a_b, u_b = _rowcall("rg_gates", _rg_gates_fn, [(xc, rg_w, 0)], gate_pars, [(rg_w, F32)] * 4)
    hs_f = _scan_fwd("rg_scan_f", a_f, u_f, False)
    hs_b = _scan_fwd("rg_scan_b", a_b, u_b, True)
    hg_rows = [(proj0, hg_w, 2), (proj0, hg_w, 3), (proj0, hg_w, 4)]
    qh, k_f, lf_f, k_b, lf_b = _rowcall("hg_pre", _hg_pre_fn, hg_rows, [w["hg_lb_logits"]], [(hg_w, F32)] * 5)
    iv = (proj0, 5)
    o_f, o_b, st_f, st_b = _attn_fwd("hg_attn", (qh, 0), (k_f, 0), (k_b, 0), iv, (lf_f, 0), (lf_b, 0), 4, 128, 128)
    post0_rows = [(hs_f, rg_w, 0), (hs_b, rg_w, 0), (proj0, rg_w, 1), (o_f, hg_w, 0), (o_b, hg_w, 0), (proj0, hg_w, 6)]
    hg_gain = _row2(w["hg_norm"])
    mix_in0 = _rowcall("l0_post", _post0_fwd_fn, post0_rows, [hg_gain], [(d_model, BF16)])[0]
    h_b0 = _mm("l0_out", mix_in0, w["ab_w_out"], extras=(h_a0,), epi=_add_epi)
    if late is not None:
        w = {**w, **late(h_b0)}
    h_c0, mlp0 = _mlp_fwd("mlp0", h_b0, _row2(w["norm_mlp"][0]), w["mlp_w1"][0], w["mlp_w2"][0])

    h_a1 = h_c0
    gain1 = _row2(w["norm_mix"][1])
    y1 = _rowcall("l1_norm", _rmsnorm_fn, [(h_a1, d_model, 0)], [gain1], [(d_model, BF16)], tm=512)[0]
    proj1 = _mm("l1_in", y1, w["gla_w_in_pad"], tn=640)
    gla_pars = [w["gla_w_up_pad"], w["gla_b_gate"]]
    gq, glf_f, glf_b = _rowcall("gla_pre", _gla_pre_fn, [(proj1, 512, 0), (proj1, LANES, 24)], gla_pars, [(512, F32)] * 3)
    gk, gv = (proj1, 1), (proj1, 1)
    go_f, go_b, gst_f, gst_b = _attn_fwd("gla_attn", (gq, 0), gk, gk, gv, (glf_f, 0), (glf_b, 0), 4, 128, 256)
    gla_gain = _row2(w["gla_norm"])
    post1_rows = [(go_f, d_model, 0), (go_b, d_model, 0), (proj1, d_model, 2)]
    mix_in1 = _rowcall("l1_post", _gla_post_fwd_fn, post1_rows, [gla_gain], [(d_model, BF16)])[0]
    h_b1 = _mm("l1_out", mix_in1, w["gla_w_out"], extras=(h_a1,), epi=_add_epi)
    h_c1, mlp1 = _mlp_fwd("mlp1", h_b1, _row2(w["norm_mlp"][1]), w["mlp_w1"][1], w["mlp_w2"][1])

    dh, loss, g["norm_final"] = _rowcall(
        "loss_head", _loss_head_fn, [(h_c1, d_model, 0), (target, d_model, 0)], [_row2(w["norm_final"])],
        [(d_model, F32)], [(1, LANES), (1, d_model)], tm=512)

    dh, g_nmlp1, g_w1_1, g_w2_1 = _mlp_bwd("mlp1", h_b1, _row2(w["norm_mlp"][1]), w["mlp_w1"][1], w["mlp_w2"][1], mlp1, dh)
    dmix1 = _mm("l1_dout", dh, w["gla_w_out"], mode="nt")
    g["gla_w_out"] = _mm("l1_dwout", mix_in1, dh, mode="tn")
    dgo, dr, g["gla_norm"] = _rowcall(
        "l1_dpost", _gla_post_bwd_fn, post1_rows + [(dmix1, d_model, 0)], [gla_gain],
        [(d_model, F32), (d_model, BF16)], [(1, d_model)])
    (dq_f, dk_f, dv_f, dlf_f), (dq_b, dk_b, dv_b, dlf_b) = _attn_bwd(
        "gla_dattn", (gq, 0), gk, gk, gv, (glf_f, 0), (glf_b, 0), gst_f, gst_b, dgo, 4, 128, 256)

    def gla_pre_bwd(q, lr, dq1, dq2, dlf1, dlf2, dk1, dk2, dv1, dv2, w_up, b_gate):
        dlr = jnp.zeros_like(lr)
        dws, dbs = [], []
        for d, dlf in enumerate((dlf1, dlf2)):
            z = _raw_nn(lr, w_up[d]) + b_gate[d:d + 1]
            dz = dlf * _sigmoid(-z) * (1.0 / 16.0)
            dlr = dlr + _raw_nt(dz, w_up[d])
            dws.append(_raw_tn(dz, lr))
            dbs.append(jnp.sum(dz, axis=0, keepdims=True))
        return ((dq1 + dq2) * (128.0 ** -0.5), dk1 + dk2, dv1 + dv2, dlr, dws[0], dws[1], dbs[0], dbs[1])

    rows = [(proj1, 512, 0), (proj1, LANES, 24), (dq_f, 512, 0), (dq_b, 512, 0), (dlf_f, 512, 0), (dlf_b, 512, 0),
            (dk_f, 512, 0), (dk_b, 512, 0), (dv_f, d_model, 0), (dv_b, d_model, 0)]
    dq, dk, dv, dlr, dwt_f, dwt_b, db_f, db_b = _rowcall(
        "gla_dpre", gla_pre_bwd, rows, gla_pars, [(512, BF16), (512, BF16), (d_model, BF16), (LANES, BF16)],
        [(512, LANES), (512, LANES), (1, 512), (1, 512)])
    g["gla_w_up_pad"] = jnp.stack([dwt_f.T, dwt_b.T])
    g["gla_b_gate"] = jnp.concatenate([db_f, db_b], axis=0)
    dproj1 = jnp.concatenate([dq, dk, dv, dr, dlr], axis=1)
    g["gla_w_in_pad"] = _mm("l1_dwin", y1, dproj1, mode="tn", tn=640)
    dy1 = _mm("l1_dy", dproj1, w["gla_w_in_pad"], mode="nt", tk=640)
    dh, g_nmix1 = _norm_bwd("l1_dnorm", h_a1, gain1, dy1, dh)

    dh, g_nmlp0, g_w1_0, g_w2_0 = _mlp_bwd("mlp0", h_b0, _row2(w["norm_mlp"][0]), w["mlp_w1"][0], w["mlp_w2"][0], mlp0, dh)
    dmix0 = _mm("l0_dout", dh, w["ab_w_out"], mode="nt")
    g["ab_w_out"] = _mm("l0_dwout", mix_in0, dh, mode="tn")
    dhs, dga, do, dg, g["hg_norm"] = _rowcall(
        "l0_dpost", _post0_bwd_fn, post0_rows + [(dmix0, d_model, 0)], [hg_gain],
        [(rg_w, F32), (rg_w, BF16), (hg_w, F32), (hg_w, BF16)], [(1, hg_w)])
    (dqh_f, dk_f, div_f, dlf_f), (dqh_b, dk_b, div_b, dlf_b) = _attn_bwd(
        "hg_dattn", (qh, 0), (k_f, 0), (k_b, 0), iv, (lf_f, 0), (lf_b, 0), st_f, st_b, do, 4, 128, 128)

    def hg_pre_bwd(q, f_f, f_b, dq1, dq2, dk1, dlf1, dk2, dlf2, dv1, dv2, logits):
        _, vjp = jax.vjp(_hg_pre_fn, q, f_f, f_b, logits)
        dq, df_f, df_b, dlogits = vjp((dq1 + dq2, dk1, dlf1, dk2, dlf2))
        return dq, df_f, df_b, dv1 + dv2, dlogits

    rows = hg_rows + [(t, hg_w, 0) for t in (dqh_f, dqh_b, dk_f, dlf_f, dk_b, dlf_b, div_f, div_b)]
    dq, df_f, df_b, div, g["hg_lb_logits"] = _rowcall(
        "hg_dpre", hg_pre_bwd, rows, [w["hg_lb_logits"]], [(hg_w, BF16)] * 4, [(2, hg_w)])
    du_f, da_f = _scan_bwd("rg_dscan_f", a_f, hs_f, dhs, False)
    du_b, da_b = _scan_bwd("rg_dscan_b", a_b, hs_b, dhs, True)
    gates_bwd = _vjp_of(_rg_gates_fn, 1, 4, 5)
    rows = [(xc, rg_w, 0), (da_f, rg_w, 0), (du_f, rg_w, 0), (da_b, rg_w, 0), (du_b, rg_w, 0)]
    dxc, g["rg_wa_bd"], g["rg_wx_bd"], g["rg_b_a"], g["rg_b_x"], g["rg_lambda"] = _rowcall(
        "rg_dgates", gates_bwd, rows, gate_pars, [(rg_w, F32)],
        [(2, rg_w, rg_w), (2, rg_w, rg_w), (2, rg_w), (2, rg_w), (2, rg_w)])
    dxa, g["rg_conv_w"], g["rg_conv_b"] = _conv_bwd("rg_dconv", proj0, 0, conv_w, dxc)
    dproj0 = jnp.concatenate([dxa, dga, dq, df_f, df_b, div, dg], axis=1)
    g["ab_w_in"] = _mm("l0_dwin", y0, dproj0, mode="tn", out_split=N_CHIPS)
    dy0 = _mm("l0_dy", dproj0, w["ab_w_in"], mode="nt")
    grad_x, g_nmix0 = _norm_bwd("l0_dnorm", h_a0, gain, dy0, dh)

    g["norm_mix"] = jnp.concatenate([g_nmix0, g_nmix1], axis=0)
    g["norm_mlp"] = jnp.concatenate([g_nmlp0, g_nmlp1], axis=0)
    g["mlp_w1"] = [g_w1_0, g_w1_1]
    g["mlp_w2"] = [g_w2_0, g_w2_1]
    return loss, grad_x, g


def _block_diag(w):
    d, g, n, _ = w.shape
    eye = jnp.eye(g, dtype=w.dtype)
    return (w[:, :, :, None, :] * eye[None, :, None, :, None]).reshape(d, g * n, g * n)


def _block_diag_extract(wbd, g):
    d, gn, _ = wbd.shape
    n = gn // g
    blocks = wbd.reshape(d, g, n, g, n)
    return jnp.stack([blocks[:, i, :, i, :] for i in range(g)], axis=1)


def _prepare_weights(big, full):
    w = {k: full[k] for k in ("norm_mix", "norm_mlp", "norm_final", "hg_lb_logits")}
    for k in ("rg_conv_w", "rg_conv_b", "rg_b_a", "rg_b_x", "rg_lambda", "hg_norm", "gla_b_gate", "gla_norm"):
        w[k] = full[k][0]
    w["rg_wa_bd"] = _block_diag(full["rg_w_a"][0])
    w["rg_wx_bd"] = _block_diag(full["rg_w_x"][0])
    up = full["gla_w_gate_up"][0]
    rank = up.shape[1]
    pad = jnp.zeros((2, LANES, up.shape[2]), F32)
    w["gla_w_up_pad"] = pad.at[0, 0:rank].set(up[0]).at[1, rank:2 * rank].set(up[1])
    w.update(_prepare_matrices(big))
    return w


def _prepare_matrices(big):
    w = {}
    if "mlp_w1" in big:
        w["mlp_w1"] = list(big["mlp_w1"])
        w["mlp_w2"] = [t.reshape(-1, t.shape[-1]) for t in big["mlp_w2"]]
    if "ab_w_in" in big:
        w["ab_w_in"] = big["ab_w_in"]
        w["ab_w_out"] = big["ab_w_out"].reshape(-1, big["ab_w_out"].shape[-1])
    if "gla_w_in" in big:
        w["gla_w_out"] = big["gla_w_out"].reshape(-1, big["gla_w_out"].shape[-1])
        gla_in = _join_chips(big["gla_w_in"], 1)
        w["gla_w_in_pad"] = jnp.pad(gla_in, ((0, 0), (0, GLA_IN_PAD - gla_in.shape[1])))
    return w


def _finish_grads(g, rank=16, gla_in_width=3104, rg_blocks=8):
    def chip_major(t):
        return t.reshape(N_CHIPS, t.shape[0] // N_CHIPS, t.shape[1])

    big = {
        "mlp_w1": list(g["mlp_w1"]), "mlp_w2": [chip_major(t) for t in g["mlp_w2"]],
        "ab_w_in": g["ab_w_in"], "ab_w_out": chip_major(g["ab_w_out"]),
        "gla_w_in": _split_chips(g["gla_w_in_pad"][:, :gla_in_width], 1), "gla_w_out": chip_major(g["gla_w_out"]),
    }
    small = {
        "norm_mix": g["norm_mix"], "norm_mlp": g["norm_mlp"], "norm_final": g["norm_final"][0],
        "rg_conv_w": g["rg_conv_w"][None], "rg_conv_b": g["rg_conv_b"],
        "rg_w_a": _block_diag_extract(g["rg_wa_bd"], rg_blocks)[None], "rg_b_a": g["rg_b_a"][None],
        "rg_w_x": _block_diag_extract(g["rg_wx_bd"], rg_blocks)[None], "rg_b_x": g["rg_b_x"][None],
        "rg_lambda": g["rg_lambda"][None], "hg_lb_logits": g["hg_lb_logits"], "hg_norm": g["hg_norm"],
        "gla_w_gate_up": jnp.stack([g["gla_w_up_pad"][0, 0:rank], g["gla_w_up_pad"][1, rank:2 * rank]])[None],
        "gla_b_gate": g["gla_b_gate"][None], "gla_norm": g["gla_norm"],
    }
    return big, small


MATRICES = (("mlp_w1", 0), ("mlp_w1", 1), ("mlp_w2", 0), ("mlp_w2", 1), ("ab_w_in", 0), ("ab_w_out", 0),
            ("gla_w_in", 0), ("gla_w_out", 0))
EARLY_MATRICES = ("ab_w_in", "ab_w_out")
SMALL_SHARDED = ("rg_conv_w", "rg_b_a", "rg_b_x", "rg_lambda", "gla_w_gate_up", "gla_b_gate", "gla_norm")
SMALL_REPLICATED = ("norm_mix", "norm_mlp", "norm_final", "rg_conv_b", "rg_w_a", "rg_w_x", "hg_lb_logits", "hg_norm")
WEIGHTS = ("norm_mix", "norm_mlp", "norm_final", "mlp_w1", "mlp_w2", "ab_w_in", "ab_w_out", "rg_conv_w", "rg_conv_b",
           "rg_w_a", "rg_b_a", "rg_w_x", "rg_b_x", "rg_lambda", "hg_lb_logits", "hg_norm", "gla_w_in", "gla_w_out",
           "gla_w_gate_up", "gla_b_gate", "gla_norm")
ROW_ALIGN = 16


def _pack(arrays, lead=0):
    head = arrays[0].shape[:lead]
    flat = jnp.concatenate([a.reshape(head + (-1,)) for a in arrays], axis=lead)
    n = flat.shape[-1]
    quantum = LANES * ROW_ALIGN
    padded = -(-n // quantum) * quantum
    if padded != n:
        flat = jnp.pad(flat, [(0, 0)] * lead + [(0, padded - n)])
    return flat.reshape(head + (padded // LANES, LANES))


def _unpack(buf, shapes, lead=0):
    head = buf.shape[:lead]
    flat = buf.reshape(head + (-1,))
    out, off = [], 0
    for s in shapes:
        n = 1
        for v in s:
            n *= v
        out.append(lax.slice_in_dim(flat, off, off + n, axis=lead).reshape(head + tuple(s)))
        off += n
    return out


def _join_chips(gathered, axis):
    t = jnp.moveaxis(gathered, 0, axis)
    return t.reshape(t.shape[:axis] + (t.shape[axis] * t.shape[axis + 1],) + t.shape[axis + 2:])


def _split_chips(full, axis):
    s = full.shape
    t = full.reshape(s[:axis] + (N_CHIPS, s[axis] // N_CHIPS) + s[axis + 1:])
    return jnp.moveaxis(t, axis, 0)


_ANY = pl.BlockSpec(memory_space=pl.ANY)


def _place():
    return lax.axis_index("x"), lax.axis_index("y"), lax.axis_index("c")


def _into_slot(name, src, slot, n_slots, dtype, tm, layer=None):
    r, lanes = src.shape[-2:]
    tm = _row_tile(r, tm, ROW_ALIGN)

    def body(slot_ref, in_ref, o_ref):
        o_ref[...] = in_ref[...].astype(o_ref.dtype)

    if layer is None:
        in_spec = pl.BlockSpec((tm, lanes), lambda i, slot_ref: (i, 0))
    else:
        in_spec = pl.BlockSpec((None, tm, lanes), lambda i, slot_ref: (layer, i, 0))
    grid_spec = pltpu.PrefetchScalarGridSpec(
        num_scalar_prefetch=1, grid=(r // tm,), in_specs=[in_spec],
        out_specs=pl.BlockSpec((None, tm, lanes), lambda i, slot_ref: (slot_ref[0], i, 0)))
    return pl.pallas_call(
        body, name=name, grid_spec=grid_spec, out_shape=jax.ShapeDtypeStruct((n_slots, r, lanes), dtype),
        compiler_params=_params(("parallel",)),
    )(slot.reshape(1).astype(jnp.int32), src)


def _chip_peers():
    x, y, c = _place()
    return 2 * x + y, c, [(1 - x, y), (x, 1 - y), (1 - x, 1 - y)]


def _comm_call(name, body, ins, out_shapes, n_sems, aliases=None):
    return pl.pallas_call(
        body, name=name, in_specs=[_ANY] * len(ins), out_specs=[_ANY] * len(out_shapes), out_shape=out_shapes,
        input_output_aliases=aliases or {},
        scratch_shapes=[pltpu.SemaphoreType.DMA((n_sems,)), pltpu.SemaphoreType.DMA((n_sems,))],
    )(*ins)


def _gather_chips(name, bufs):
    n = len(bufs)

    def body(*refs):
        outs, send_sems, recv_sems = refs[n:2 * n], refs[2 * n], refs[2 * n + 1]
        x, y, c = _place()
        me, _, peers = _chip_peers()

        def rows(a, block, half):
            rh = outs[a].shape[1] // 2
            return outs[a].at[block, pl.ds(half * rh, rh)]

        def copy(a, j, block, half, to, sem):
            return pltpu.make_async_remote_copy(
                src_ref=rows(a, block, half), dst_ref=rows(a, block, half), send_sem=send_sems.at[sem],
                recv_sem=recv_sems.at[sem], device_id=to, device_id_type=MESH)

        def over_ici(a, j, block):
            px, py = peers[j]
            return copy(a, j, block, c, (px, py, c), 6 * a + j)

        def to_sibling(a, j, block, half):
            return copy(a, j, block, half, (x, y, 1 - c), 6 * a + 3 + j)

        sends = [over_ici(a, j, me) for a in range(n) for j in range(3)]
        for cp in sends:
            cp.start()
        for a in range(n):
            for j, (px, py) in enumerate(peers):
                over_ici(a, j, 2 * px + py).wait_recv()
                handed = to_sibling(a, j, 2 * px + py, c)
                handed.start()
                sends.append(handed)
        for a in range(n):
            for j, (px, py) in enumerate(peers):
                to_sibling(a, j, 2 * px + py, 1 - c).wait_recv()
        for cp in sends:
            cp.wait_send()

    shapes = [jax.ShapeDtypeStruct(b.shape, b.dtype) for b in bufs]
    return _comm_call(name, body, bufs, shapes, 6 * n, {a: a for a in range(n)})


_HBM = pl.BlockSpec(memory_space=pltpu.HBM)
_SEM = pl.BlockSpec(memory_space=pltpu.SEMAPHORE)
_EFFECT = pltpu.SideEffectType.DATAFLOW_SIDE_EFFECTING


def _half_rows(ref, block, half):
    rh = ref.shape[1] // 2
    return ref.at[block, pl.ds(half * rh, rh)]


def _gather_start(name, bufs, after):
    n = len(bufs)

    def body(*refs):
        ins, send_sems, recv_sems, token = refs[:n], refs[n + 1], refs[n + 2], refs[-1]
        me, c, peers = _chip_peers()
        for a in range(n):
            mine = _half_rows(ins[a], me, c)
            for j, (px, py) in enumerate(peers):
                pltpu.make_async_remote_copy(
                    src_ref=mine, dst_ref=mine, send_sem=send_sems.at[3 * a + j], recv_sem=recv_sems.at[3 * a + j],
                    device_id=(px, py, c), device_id_type=MESH).start()
        token[...] = jnp.zeros_like(token)

    out_shape = (pltpu.SemaphoreType.DMA((3 * n,)), pltpu.SemaphoreType.DMA((3 * n,)),
                 *[pltpu.HBM(b.shape, b.dtype) for b in bufs], jax.ShapeDtypeStruct((8, LANES), F32))
    return pl.pallas_call(
        body, name=name, out_shape=out_shape, in_specs=[_HBM] * n + [_ANY],
        out_specs=(_SEM, _SEM, *[_HBM] * n, pl.BlockSpec(memory_space=pltpu.VMEM)),
        input_output_aliases={a: 2 + a for a in range(n)},
        compiler_params=pltpu.CompilerParams(has_side_effects=_EFFECT),
    )(*[pltpu.with_memory_space_constraint(b, pltpu.HBM) for b in bufs], after)


def _gather_wait(name, bufs, send_sems, recv_sems, after):
    n = len(bufs)

    def body(*refs):
        ins, send_sems, recv_sems = refs[:n], refs[n], refs[n + 1]
        me, c, peers = _chip_peers()
        for a in range(n):
            for j, (px, py) in enumerate(peers):
                copy = pltpu.make_async_remote_copy(
                    src_ref=_half_rows(ins[a], me, c), dst_ref=_half_rows(ins[a], 2 * px + py, c),
                    send_sem=send_sems.at[3 * a + j], recv_sem=recv_sems.at[3 * a + j],
                    device_id=(px, py, c), device_id_type=MESH)
                copy.wait_send()
                copy.wait_recv()

    return pl.pallas_call(
        body, name=name, out_shape=tuple(pltpu.HBM(b.shape, b.dtype) for b in bufs),
        in_specs=[_HBM] * n + [_SEM, _SEM, _ANY], out_specs=tuple([_HBM] * n),
        input_output_aliases={a: a for a in range(n)},
        compiler_params=pltpu.CompilerParams(has_side_effects=_EFFECT),
    )(*bufs, send_sems, recv_sems, after)


def _hand_over(name, bufs):
    n = len(bufs)

    def body(*refs):
        outs, send_sems, recv_sems = refs[n:2 * n], refs[2 * n], refs[2 * n + 1]
        x, y, c = _place()
        _, _, peers = _chip_peers()

        def copy(a, j, half):
            px, py = peers[j]
            rows = _half_rows(outs[a], 2 * px + py, half)
            return pltpu.make_async_remote_copy(
                src_ref=rows, dst_ref=rows, send_sem=send_sems.at[3 * a + j], recv_sem=recv_sems.at[3 * a + j],
                device_id=(x, y, 1 - c), device_id_type=MESH)

        sends = [copy(a, j, c) for a in range(n) for j in range(3)]
        for cp in sends:
            cp.start()
        for a in range(n):
            for j in range(3):
                copy(a, j, 1 - c).wait_recv()
        for cp in sends:
            cp.wait_send()

    shapes = [jax.ShapeDtypeStruct(b.shape, b.dtype) for b in bufs]
    return _comm_call(name, body, bufs, shapes, 3 * n, {a: a for a in range(n)})


def _pair_exchange(name, gs):
    n = len(gs)

    def body(*refs):
        ins, outs, send_sems, recv_sems = refs[:n], refs[n:2 * n], refs[2 * n], refs[2 * n + 1]
        x, y, c = _place()
        copies = [pltpu.make_async_remote_copy(
            src_ref=ins[a].at[:, 1 - c], dst_ref=outs[a], send_sem=send_sems.at[a], recv_sem=recv_sems.at[a],
            device_id=(x, y, 1 - c), device_id_type=MESH) for a in range(n)]
        for cp in copies:
            cp.start()
        for cp in copies:
            cp.wait()

    shapes = [jax.ShapeDtypeStruct((g.shape[0],) + g.shape[2:], g.dtype) for g in gs]
    return _comm_call(name, body, gs, shapes, n)


def _pair_add(name, g, got, c):
    n, _, rh, lanes = g.shape
    tm = _row_tile(rh, 2048, ROW_ALIGN)

    def body(c_ref, g_ref, got_ref, o_ref, o16_ref):
        s = g_ref[...] + got_ref[...]
        o_ref[...] = s
        o16_ref[...] = s.astype(BF16)

    out_spec = pl.BlockSpec((None, tm, lanes), lambda s, i, c_ref: (s, i, 0))
    grid_spec = pltpu.PrefetchScalarGridSpec(
        num_scalar_prefetch=1, grid=(n, rh // tm),
        in_specs=[pl.BlockSpec((None, None, tm, lanes), lambda s, i, c_ref: (s, c_ref[0], i, 0)),
                  pl.BlockSpec((None, tm, lanes), lambda s, i, c_ref: (s, i, 0))],
        out_specs=[out_spec, out_spec])
    return pl.pallas_call(
        body, name=name, grid_spec=grid_spec,
        out_shape=[jax.ShapeDtypeStruct((n, rh, lanes), F32), jax.ShapeDtypeStruct((n, rh, lanes), BF16)],
        compiler_params=_params(("parallel", "parallel")),
    )(c.reshape(1).astype(jnp.int32), g, got)


def _chip_scatter(name, ps):
    n = len(ps)

    def body(*refs):
        ins, outs, send_sems, recv_sems = refs[:n], refs[n:2 * n], refs[2 * n], refs[2 * n + 1]
        me, c, peers = _chip_peers()

        def copy(a, j, src_block, dst_block):
            px, py = peers[j]
            return pltpu.make_async_remote_copy(
                src_ref=ins[a].at[src_block], dst_ref=outs[a].at[dst_block], send_sem=send_sems.at[3 * a + j],
                recv_sem=recv_sems.at[3 * a + j], device_id=(px, py, c), device_id_type=MESH)

        sends = [copy(a, j, 2 * px + py, me) for a in range(n) for j, (px, py) in enumerate(peers)]
        for cp in sends:
            cp.start()
        for a in range(n):
            for j, (px, py) in enumerate(peers):
                copy(a, j, me, 2 * px + py).wait_recv()
        for cp in sends:
            cp.wait_send()

    shapes = [jax.ShapeDtypeStruct(p.shape, p.dtype) for p in ps]
    return _comm_call(name, body, ps, shapes, 3 * n)


def _sum_ring(name, own, got, chip, core):
    n, rh, lanes = own.shape
    tm = _row_tile(rh, 2048, ROW_ALIGN)

    def body(idx_ref, own_ref, g1_ref, g2_ref, g3_ref, o_ref):
        o_ref[...] = ((own_ref[...] + g1_ref[...].astype(F32)) + g2_ref[...].astype(F32)) + g3_ref[...].astype(F32)

    def block(k):
        return pl.BlockSpec((None, tm, lanes), lambda i, idx_ref: ((idx_ref[0] + k) % n, i, 0))

    grid_spec = pltpu.PrefetchScalarGridSpec(
        num_scalar_prefetch=1, grid=(rh // tm,), in_specs=[block(0), block(1), block(2), block(3)],
        out_specs=pl.BlockSpec((None, tm, lanes), lambda i, idx_ref: (idx_ref[1], i, 0)))
    return pl.pallas_call(
        body, name=name, grid_spec=grid_spec, out_shape=jax.ShapeDtypeStruct((2, rh, lanes), F32),
        compiler_params=_params(("parallel",)),
    )(jnp.stack([chip, core]).astype(jnp.int32), own, got, got, got)


def _pair_gather(name, bufs):
    n = len(bufs)

    def body(*refs):
        ins, outs, send_sems, recv_sems = refs[:n], refs[n:2 * n], refs[2 * n], refs[2 * n + 1]
        x, y, c = _place()

        def copy(a, block):
            return pltpu.make_async_remote_copy(
                src_ref=ins[a].at[block], dst_ref=outs[a].at[block], send_sem=send_sems.at[a],
                recv_sem=recv_sems.at[a], device_id=(x, y, 1 - c), device_id_type=MESH)

        sends = [copy(a, c) for a in range(n)]
        for cp in sends:
            cp.start()
        for a in range(n):
            copy(a, 1 - c).wait_recv()
        for cp in sends:
            cp.wait_send()

    shapes = [jax.ShapeDtypeStruct(b.shape, b.dtype) for b in bufs]
    return _comm_call(name, body, bufs, shapes, n, {a: a for a in range(n)})


def _gather_all(name, s):
    def body(in_ref, out_ref, send_sems, recv_sems, local_sem):
        x, y, c = _place()
        me = 4 * x + 2 * y + c
        peers = []
        for mask in range(1, N_DEV):
            fx, fy, fc = (mask >> 2) & 1, (mask >> 1) & 1, mask & 1
            peers.append((jnp.where(fx, 1 - x, x), jnp.where(fy, 1 - y, y), jnp.where(fc, 1 - c, c)))

        def copy(j, block):
            return pltpu.make_async_remote_copy(
                src_ref=in_ref, dst_ref=out_ref.at[block], send_sem=send_sems.at[j], recv_sem=recv_sems.at[j],
                device_id=peers[j], device_id_type=MESH)

        local = pltpu.make_async_copy(in_ref, out_ref.at[me], local_sem)
        local.start()
        sends = [copy(j, me) for j in range(N_DEV - 1)]
        for cp in sends:
            cp.start()
        for j, (px, py, pc) in enumerate(peers):
            copy(j, 4 * px + 2 * py + pc).wait_recv()
        for cp in sends:
            cp.wait_send()
        local.wait()

    return pl.pallas_call(
        body, name=name, in_specs=[_ANY], out_specs=_ANY,
        out_shape=jax.ShapeDtypeStruct((N_DEV,) + s.shape, s.dtype),
        scratch_shapes=[pltpu.SemaphoreType.DMA((N_DEV - 1,)), pltpu.SemaphoreType.DMA((N_DEV - 1,)),
                        pltpu.SemaphoreType.DMA],
    )(s)


def _sum_blocks(name, stacked, tm):
    n, r, lanes = stacked.shape

    def body(in_ref, o_ref):
        acc = in_ref[0]
        for j in range(1, n):
            acc = acc + in_ref[j]
        o_ref[...] = acc

    return pl.pallas_call(
        body, name=name, grid=(r // tm,), in_specs=[pl.BlockSpec((n, tm, lanes), lambda i: (0, i, 0))],
        out_specs=pl.BlockSpec((tm, lanes), lambda i: (i, 0)), out_shape=jax.ShapeDtypeStruct((r, lanes), F32),
        compiler_params=_params(("parallel",)),
    )(stacked)


def _row_tile(rows, pref, align):
    best = None
    for t in range(align, min(rows, pref) + 1, align):
        if rows % t == 0:
            best = t
    assert best is not None, (rows, pref, align)
    return best


def _adam(name, w, g, m, v):
    rows, width = w.shape
    tm = _row_tile(rows, max(8, 4096 * LANES // width), 8)
    args = [(t, width, 0) for t in (w, g, m, v)]
    return _rowcall(name, _adam_fn, args, [], [(width, F32)] * 3, tm=tm)


def kernel(x, norm_mix, norm_mlp, norm_final, mlp_w1, mlp_w2, ab_w_in, ab_w_out, rg_conv_w, rg_conv_b, rg_w_a, rg_b_a, rg_w_x, rg_b_x, rg_lambda, hg_lb_logits, hg_norm, gla_w_in, gla_w_out, gla_w_gate_up, gla_b_gate, gla_norm, loss_target, m_norm_mix, m_norm_mlp, m_norm_final, m_mlp_w1, m_mlp_w2, m_ab_w_in, m_ab_w_out, m_rg_conv_w, m_rg_conv_b, m_rg_w_a, m_rg_b_a, m_rg_w_x, m_rg_b_x, m_rg_lambda, m_hg_lb_logits, m_hg_norm, m_gla_w_in, m_gla_w_out, m_gla_w_gate_up, m_gla_b_gate, m_gla_norm, v_norm_mix, v_norm_mlp, v_norm_final, v_mlp_w1, v_mlp_w2, v_ab_w_in, v_ab_w_out, v_rg_conv_w, v_rg_conv_b, v_rg_w_a, v_rg_b_a, v_rg_w_x, v_rg_b_x, v_rg_lambda, v_hg_lb_logits, v_hg_norm, v_gla_w_in, v_gla_w_out, v_gla_w_gate_up, v_gla_b_gate, v_gla_norm):
    w = dict(norm_mix=norm_mix, norm_mlp=norm_mlp, norm_final=norm_final, mlp_w1=mlp_w1, mlp_w2=mlp_w2, ab_w_in=ab_w_in, ab_w_out=ab_w_out, rg_conv_w=rg_conv_w, rg_conv_b=rg_conv_b, rg_w_a=rg_w_a, rg_b_a=rg_b_a, rg_w_x=rg_w_x, rg_b_x=rg_b_x, rg_lambda=rg_lambda, hg_lb_logits=hg_lb_logits, hg_norm=hg_norm, gla_w_in=gla_w_in, gla_w_out=gla_w_out, gla_w_gate_up=gla_w_gate_up, gla_b_gate=gla_b_gate, gla_norm=gla_norm)
    m = dict(norm_mix=m_norm_mix, norm_mlp=m_norm_mlp, norm_final=m_norm_final, mlp_w1=m_mlp_w1, mlp_w2=m_mlp_w2, ab_w_in=m_ab_w_in, ab_w_out=m_ab_w_out, rg_conv_w=m_rg_conv_w, rg_conv_b=m_rg_conv_b, rg_w_a=m_rg_w_a, rg_b_a=m_rg_b_a, rg_w_x=m_rg_w_x, rg_b_x=m_rg_b_x, rg_lambda=m_rg_lambda, hg_lb_logits=m_hg_lb_logits, hg_norm=m_hg_norm, gla_w_in=m_gla_w_in, gla_w_out=m_gla_w_out, gla_w_gate_up=m_gla_w_gate_up, gla_b_gate=m_gla_b_gate, gla_norm=m_gla_norm)
    v = dict(norm_mix=v_norm_mix, norm_mlp=v_norm_mlp, norm_final=v_norm_final, mlp_w1=v_mlp_w1, mlp_w2=v_mlp_w2, ab_w_in=v_ab_w_in, ab_w_out=v_ab_w_out, rg_conv_w=v_rg_conv_w, rg_conv_b=v_rg_conv_b, rg_w_a=v_rg_w_a, rg_b_a=v_rg_b_a, rg_w_x=v_rg_w_x, rg_b_x=v_rg_b_x, rg_lambda=v_rg_lambda, hg_lb_logits=v_hg_lb_logits, hg_norm=v_hg_norm, gla_w_in=v_gla_w_in, gla_w_out=v_gla_w_out, gla_w_gate_up=v_gla_w_gate_up, gla_b_gate=v_gla_b_gate, gla_norm=v_gla_norm)
    chip = 2 * lax.axis_index("x") + lax.axis_index("y")
    core = lax.axis_index("c")
    sharded_shapes = [w[n].shape for n in SMALL_SHARDED]

    slots = [_into_slot(f"cast_{n}{layer}", w[n], chip, N_CHIPS, BF16, 512, layer) for n, layer in MATRICES]
    early = [i for i, (n, _) in enumerate(MATRICES) if n in EARLY_MATRICES]
    rest = [i for i in range(len(MATRICES)) if i not in early]

    def named(indices, arrays):
        big = {}
        for i, t in zip(indices, arrays):
            big.setdefault(MATRICES[i][0], []).append(t)
        return {n: (v if n in ("mlp_w1", "mlp_w2") else v[0]) for n, v in big.items()}

    gathered = _gather_chips("gather_early", [slots[i] for i in early])
    send_sems, recv_sems, *in_flight, token = _gather_start("gather_rest_start", [slots[i] for i in rest], gathered[0])

    def late_weights(after):
        landed = _gather_wait("gather_rest_wait", in_flight, send_sems, recv_sems, after)
        return _prepare_matrices(named(rest, _hand_over("gather_rest_share", list(landed))))

    big = named(early, gathered)
    vectors = _pack([w[n] for n in SMALL_SHARDED])
    vectors = _into_slot("place_vectors", vectors, chip, N_CHIPS, F32, vectors.shape[0])
    small_all = _unpack(_gather_chips("gather_vectors", [vectors])[0], sharded_shapes, lead=1)
    full = {n: w[n] for n in SMALL_REPLICATED}
    for n, t in zip(SMALL_SHARDED, small_all):
        full[n] = _join_chips(t, t.ndim - 2)

    loss_part, grad_x, g_kernel = _local_step(x[0], loss_target[0], _prepare_weights(big, full), token, late_weights)
    g_big, g_full = _finish_grads(g_kernel)
    loss = lax.psum(loss_part[0, 0], ("x", "y", "c"))

    g_list = [g_big[n][layer] if n in ("mlp_w1", "mlp_w2") else g_big[n] for n, layer in MATRICES]
    halves = [t.reshape(N_CHIPS, 2, t.shape[1] // 2, t.shape[2]) for t in g_list]
    from_sibling = _pair_exchange("reduce_pair", halves)
    chip_part = [_pair_add(f"reduce_pair_add{i}", h, s, core) for i, (h, s) in enumerate(zip(halves, from_sibling))]
    from_chips = _chip_scatter("reduce_chips", [p16 for _, p16 in chip_part])
    mine = [_sum_ring(f"reduce_chips_add{i}", p32, f, chip, core)
            for i, ((p32, _), f) in enumerate(zip(chip_part, from_chips))]
    reduced = [t.reshape(2 * t.shape[1], t.shape[2]) for t in _pair_gather("reduce_share", mine)]
    by_name = {n: [] for n, _ in MATRICES}
    for (n, _), t in zip(MATRICES, reduced):
        by_name[n].append(t)
    grads = {n: jnp.stack(v) for n, v in by_name.items()}

    small_names = SMALL_REPLICATED + SMALL_SHARDED
    g_small = _pack([g_full[n] for n in small_names])
    g_small_all = _gather_all("reduce_small", g_small)
    g_small_red = _sum_blocks("reduce_small_add", g_small_all, g_small.shape[0])
    g_small_full = dict(zip(small_names, _unpack(g_small_red, [g_full[n].shape for n in small_names])))
    for n in SMALL_REPLICATED:
        grads[n] = g_small_full[n]
    for n in SMALL_SHARDED:
        width = w[n].shape[-1]
        grads[n] = lax.dynamic_slice_in_dim(g_small_full[n], chip * width, width, axis=g_small_full[n].ndim - 1)

    delta, new_m, new_v = {}, {}, {}
    for n in by_name:
        flat = [t.reshape(-1, t.shape[-1]) for t in (w[n], grads[n], m[n], v[n])]
        for dst, t in zip((delta, new_m, new_v), _adam(f"adam_{n}", *flat)):
            dst[n] = t.reshape(w[n].shape)
    small_shapes = [w[n].shape for n in small_names]
    packs = [_pack([src[n] for n in small_names]) for src in (w, grads, m, v)]
    d_small, m_small, v_small = _adam("adam_small", *packs)
    for dst, buf in ((delta, d_small), (new_m, m_small), (new_v, v_small)):
        dst.update(zip(small_names, _unpack(buf, small_shapes)))

    return (loss, grad_x[None], *[grads[n] for n in WEIGHTS], *[delta[n] for n in WEIGHTS],
            *[new_m[n] for n in WEIGHTS], *[new_v[n] for n in WEIGHTS])
```

```python
import functools

import jax
import jax.numpy as jnp
from jax import lax
from jax.experimental import pallas as pl
from jax.experimental.pallas import tpu as pltpu

F32 = jnp.float32
BF16 = jnp.bfloat16
MESH = pl.DeviceIdType.MESH

LANES = 128
CHUNK = 64
EPS = 1e-6
RG_C = 8.0
N_CHIPS = 4
N_DEV = 8
GLA_IN_WIDTH = 3104
GLA_IN_PAD = 3200
VMEM_LIMIT = 56 * 1024 * 1024

ADAM_LR = 0.001
ADAM_B1 = 0.9
ADAM_B2 = 0.999
ADAM_EPS = 1e-08
ADAM_WD = 0.01
ADAM_STEP = 10


def _raw_dot(a, b, ca, cb):
    return lax.dot_general(a.astype(BF16), b.astype(BF16), (((ca,), (cb,)), ((), ())),
                           preferred_element_type=F32)


def _raw_nn(a, b):
    return _raw_dot(a, b, 1, 0)


def _raw_nt(a, b):
    return _raw_dot(a, b, 1, 1)


def _raw_tn(a, b):
    return _raw_dot(a, b, 0, 0)


@jax.custom_vjp
def _dot_nn(a, b):
    return _raw_nn(a, b)


def _dot_nn_fwd(a, b):
    return _raw_nn(a, b), (a, b)


def _dot_nn_bwd(res, g):
    a, b = res
    return _raw_nt(g, b), _raw_tn(a, g)


_dot_nn.defvjp(_dot_nn_fwd, _dot_nn_bwd)


@jax.custom_vjp
def _dot_nt(a, b):
    return _raw_nt(a, b)


def _dot_nt_fwd(a, b):
    return _raw_nt(a, b), (a, b)


def _dot_nt_bwd(res, g):
    a, b = res
    return _raw_nn(g, b), _raw_tn(g, a)


_dot_nt.defvjp(_dot_nt_fwd, _dot_nt_bwd)


@jax.custom_vjp
def _dot_tn(a, b):
    return _raw_tn(a, b)


def _dot_tn_fwd(a, b):
    return _raw_tn(a, b), (a, b)


def _dot_tn_bwd(res, g):
    a, b = res
    return _raw_nt(b, g), _raw_nn(a, g)


_dot_tn.defvjp(_dot_tn_fwd, _dot_tn_bwd)


def _tile(n, pref):
    if n <= pref:
        return n
    t = (pref // LANES) * LANES
    while t > LANES and n % t:
        t -= LANES
    assert n % t == 0, (n, pref)
    return t


def _params(sem):
    return pltpu.CompilerParams(dimension_semantics=sem, vmem_limit_bytes=VMEM_LIMIT)


def _rowcall(name, fn, rows, pars, row_outs, par_outs=(), tm=256, pin=None):
    if pin is not None:
        inner, pars = fn, list(pars) + [pin]
        fn = lambda *vals: inner(*vals[:-1])
    n_rows = rows[0][0].shape[0]
    tm = min(tm, n_rows)
    assert n_rows % tm == 0
    n_r, n_p, n_ro = len(rows), len(pars), len(row_outs)

    def body(*refs):
        vals = [r[...].astype(F32) for r in refs[:n_r + n_p]]
        outs = fn(*vals)
        o_refs = refs[n_r + n_p:n_r + n_p + n_ro]
        po_refs = refs[n_r + n_p + n_ro:]
        for o_ref, val in zip(o_refs, outs[:n_ro]):
            o_ref[...] = val.astype(o_ref.dtype)
        first = pl.program_id(0) == 0
        for po_ref, val in zip(po_refs, outs[n_ro:]):
            @pl.when(first)
            def _():
                po_ref[...] = val

            @pl.when(jnp.logical_not(first))
            def _():
                po_ref[...] += val

    def const_map(nd):
        return lambda i: (0,) * nd

    def row_spec(w, cb):
        return pl.BlockSpec((tm, w), lambda i: (i, cb))

    in_specs = [row_spec(w, cb) for _, w, cb in rows]
    in_specs += [pl.BlockSpec(p.shape, const_map(p.ndim)) for p in pars]
    out_specs = [pl.BlockSpec((tm, w), lambda i: (i, 0)) for w, _ in row_outs]
    out_specs += [pl.BlockSpec(tuple(s), const_map(len(s))) for s in par_outs]
    out_shape = [jax.ShapeDtypeStruct((n_rows, w), dt) for w, dt in row_outs]
    out_shape += [jax.ShapeDtypeStruct(tuple(s), F32) for s in par_outs]
    return pl.pallas_call(
        body, name=name, grid=(n_rows // tm,), in_specs=in_specs, out_specs=out_specs, out_shape=out_shape,
        compiler_params=_params(("arbitrary",) if par_outs else ("parallel",)),
    )(*[r[0] for r in rows], *pars)


def _vjp_of(fn, n_prim, n_out, n_par, n_pass=0):
    def bwd(*args):
        prim = args[:n_prim]
        cts = args[n_prim:n_prim + n_out]
        passes = args[n_prim + n_out:n_prim + n_out + 2 * n_pass]
        pars = args[n_prim + n_out + 2 * n_pass:]
        _, vjp = jax.vjp(fn, *prim, *pars)
        grads = vjp(tuple(cts))
        sums = tuple(passes[2 * i] + passes[2 * i + 1] for i in range(n_pass))
        return tuple(grads[:n_prim]) + sums + tuple(grads[n_prim:])
    return bwd


def _mm(name, a, b, mode="nn", extras=(), epi=None, out_dtypes=(F32,), a_pro=None, out_split=None,
        tm=1024, tn=1024, tk=1024):
    split = b.shape[0] if b.ndim == 3 else None
    b_rows, b_cols = b.shape[-2:]
    if mode == "nn":
        (m, k), n = a.shape, b_cols * (split or 1)
    elif mode == "nt":
        (m, k), n = a.shape, b_rows
        assert k == b_cols * (split or 1)
    else:
        assert split is None
        (k, m), n = a.shape, b_cols
    tm, tk = _tile(m, tm), _tile(k, tk)
    tn = _tile(n // out_split, tn) if out_split else _tile(n, tn)
    if split and mode == "nn":
        tn = _tile(b_cols, tn)
    if split and mode == "nt":
        tk = _tile(b_cols, tk)
    nk = k // tk
    raw = {"nn": _raw_nn, "nt": _raw_nt, "tn": _raw_tn}[mode]
    n_e, n_o = len(extras), len(out_dtypes)
    if epi is None:
        epi = lambda acc: (acc,)

    def body(a_ref, b_ref, *rest):
        e_refs, o_refs, acc = rest[:n_e], rest[n_e:n_e + n_o], rest[-1]
        kk = pl.program_id(2)

        @pl.when(kk == 0)
        def _():
            acc[...] = jnp.zeros_like(acc)

        a_tile = a_ref[...] if a_pro is None else a_pro(a_ref[...].astype(F32))
        acc[...] += raw(a_tile, b_ref[...])

        @pl.when(kk == nk - 1)
        def _():
            res = epi(acc[...], *[e[...].astype(F32) for e in e_refs])
            for o_ref, r in zip(o_refs, res):
                o_ref[...] = r.astype(o_ref.dtype)

    a_spec = pl.BlockSpec((tk, tm), lambda i, j, kk: (kk, i)) if mode == "tn" else pl.BlockSpec((tm, tk), lambda i, j, kk: (i, kk))
    if split and mode == "nn":
        per = b_cols // tn
        b_spec = pl.BlockSpec((None, tk, tn), lambda i, j, kk: (j // per, kk, j % per))
    elif split:
        per = b_cols // tk
        b_spec = pl.BlockSpec((None, tn, tk), lambda i, j, kk: (kk // per, j, kk % per))
    elif mode == "nt":
        b_spec = pl.BlockSpec((tn, tk), lambda i, j, kk: (j, kk))
    else:
        b_spec = pl.BlockSpec((tk, tn), lambda i, j, kk: (kk, j))
    mn_spec = pl.BlockSpec((tm, tn), lambda i, j, kk: (i, j))
    if out_split:
        assert not extras
        per_out = n // out_split // tn
        out_spec = pl.BlockSpec((None, tm, tn), lambda i, j, kk: (j // per_out, i, j % per_out))
        out_shapes = [jax.ShapeDtypeStruct((out_split, m, n // out_split), dt) for dt in out_dtypes]
    else:
        out_spec = mn_spec
        out_shapes = [jax.ShapeDtypeStruct((m, n), dt) for dt in out_dtypes]
    outs = pl.pallas_call(
        body, name=name, grid=(m // tm, n // tn, nk),
        in_specs=[a_spec, b_spec] + [mn_spec] * n_e, out_specs=[out_spec] * n_o,
        out_shape=out_shapes,
        scratch_shapes=[pltpu.VMEM((tm, tn), F32)],
        compiler_params=_params(("parallel", "parallel", "arbitrary")),
    )(a, b, *extras)
    return outs[0] if n_o == 1 else outs


def _sigmoid(x):
    return jax.nn.sigmoid(x)


def _silu(x):
    return x * _sigmoid(x)


def _softplus(x):
    return jnp.maximum(x, 0.0) + jnp.log1p(jnp.exp(-jnp.abs(x)))


def _rmsnorm_fn(x, gain):
    return (x * lax.rsqrt(jnp.mean(x * x, axis=-1, keepdims=True) + EPS) * gain,)


def _head_norm(o, gain, n_heads):
    w = o.shape[-1] // n_heads
    parts = []
    for h in range(n_heads):
        oh = o[:, h * w:(h + 1) * w]
        parts.append(oh * lax.rsqrt(jnp.mean(oh * oh, axis=-1, keepdims=True) + EPS))
    return jnp.concatenate(parts, axis=-1) * gain


@jax.custom_jvp
def _neg_expm1(x):
    u = jnp.exp(x)
    is_one = u == 1.0
    return jnp.where(is_one, -x, (1.0 - u) * x / jnp.log(jnp.where(is_one, 2.0, u)))


@_neg_expm1.defjvp
def _neg_expm1_jvp(primals, tangents):
    (x,), (t,) = primals, tangents
    return _neg_expm1(x), -jnp.exp(x) * t


def _rg_gates_fn(xc, wa, wx, ba, bx, lam):
    outs = []
    for d in range(2):
        r = _sigmoid(_dot_nn(xc, wa[d]) + ba[d:d + 1])
        i = _sigmoid(_dot_nn(xc, wx[d]) + bx[d:d + 1])
        log_a = -RG_C * r * _softplus(-lam[d:d + 1])
        outs.append(jnp.exp(log_a))
        outs.append(jnp.sqrt(_neg_expm1(2.0 * log_a)) * (i * xc))
    return tuple(outs)


def _hg_pre_fn(q, f_f, f_b, logits):
    mx = jnp.maximum(logits[0:1], logits[1:2])
    e0 = jnp.exp(logits[0:1] - mx)
    e1 = jnp.exp(logits[1:2] - mx)
    lb = e0 / (e0 + e1)
    outs = [_silu(q)]
    for f in (f_f, f_b):
        outs.append((1.0 - lb) * _sigmoid(-f))
        outs.append(jnp.log(lb + (1.0 - lb) * _sigmoid(f)))
    return tuple(outs)


def _post0_fn(hs, ga, o, g, gain):
    ya = hs * jax.nn.gelu(ga, approximate=True)
    yb = _head_norm(o, gain, 4) * _silu(g)
    return (jnp.concatenate([ya, yb], axis=-1),)


def _post0_fwd_fn(h_f, h_b, ga, o_f, o_b, g, gain):
    return _post0_fn(h_f + h_b, ga, o_f + o_b, g, gain)


def _post0_bwd_fn(h_f, h_b, ga, o_f, o_b, g, dmix, gain):
    _, vjp = jax.vjp(_post0_fn, h_f + h_b, ga, o_f + o_b, g, gain)
    return vjp((dmix,))


def _gla_pre_fn(q, lr, w_up, b_gate):
    outs = [q * (128.0 ** -0.5)]
    for d in range(2):
        z = _dot_nn(lr, w_up[d]) + b_gate[d:d + 1]
        outs.append(-_softplus(-z) * (1.0 / 16.0))
    return tuple(outs)


def _gla_post_fn(o, r, gain):
    return (_head_norm(o, gain, 4) * _silu(r),)


def _gla_post_fwd_fn(o_f, o_b, r, gain):
    return _gla_post_fn(o_f + o_b, r, gain)


def _gla_post_bwd_fn(o_f, o_b, r, dmix, gain):
    _, vjp = jax.vjp(_gla_post_fn, o_f + o_b, r, gain)
    return vjp((dmix,))


def _relu2_bwd_epi(acc, hid):
    return (acc * 2.0 * jnp.maximum(hid, 0.0),)


def _relu2(x):
    r = jnp.maximum(x, 0.0)
    return r * r


def _add_epi(acc, res):
    return (acc + res,)


def _loss_head_fn(h, target, gain):
    def f(h, gain):
        y = _rmsnorm_fn(h, gain)[0]
        err = y - target
        return 0.5 * jnp.sum(jnp.mean(err * err, axis=-1, keepdims=True))
    loss, (dh, dgain) = jax.value_and_grad(f, argnums=(0, 1))(h, gain)
    return dh, jnp.full((1, LANES), loss, F32), dgain


def _adam_fn(w, g, m, v):
    m2 = ADAM_B1 * m + (1.0 - ADAM_B1) * g
    v2 = ADAM_B2 * v + (1.0 - ADAM_B2) * (g * g)
    m_hat = m2 / (1.0 - ADAM_B1 ** ADAM_STEP)
    v_hat = v2 / (1.0 - ADAM_B2 ** ADAM_STEP)
    delta = -ADAM_LR * (m_hat / (jnp.sqrt(v_hat) + ADAM_EPS) + ADAM_WD * w)
    return delta, m2, v2


def _shifted(x, t_idx, off):
    n = x.shape[0]
    rolled = pltpu.roll(x, (-off) % n, 0)
    valid = (t_idx + off >= 0) & (t_idx + off < n)
    return jnp.where(valid, rolled, 0.0)


def _conv_fwd(name, src, colblock, w, b):
    n_rows, width = src.shape[0], w.shape[1]

    def body(x_ref, w_ref, b_ref, o_ref):
        x = x_ref[...]
        t_idx = lax.broadcasted_iota(jnp.int32, x.shape, 0)
        acc = b_ref[...] + w_ref[2:3, :] * x
        acc += w_ref[0:1, :] * _shifted(x, t_idx, -2)
        acc += w_ref[1:2, :] * _shifted(x, t_idx, -1)
        acc += w_ref[3:4, :] * _shifted(x, t_idx, 1)
        o_ref[...] = acc

    nb = width // LANES
    return pl.pallas_call(
        body, name=name, grid=(nb,),
        in_specs=[pl.BlockSpec((n_rows, LANES), lambda j: (0, colblock * nb + j)),
                  pl.BlockSpec((4, LANES), lambda j: (0, j)), pl.BlockSpec((1, LANES), lambda j: (0, j))],
        out_specs=pl.BlockSpec((n_rows, LANES), lambda j: (0, j)),
        out_shape=jax.ShapeDtypeStruct((n_rows, width), F32),
        compiler_params=_params(("parallel",)),
    )(src, w, b)


def _conv_bwd(name, src, colblock, w, d):
    n_rows, width = src.shape[0], w.shape[1]

    def body(x_ref, w_ref, d_ref, dx_ref, dw_ref, db_ref):
        x = x_ref[...]
        g = d_ref[...]
        t_idx = lax.broadcasted_iota(jnp.int32, x.shape, 0)
        dx = w_ref[2:3, :] * g
        dx += w_ref[0:1, :] * _shifted(g, t_idx, 2)
        dx += w_ref[1:2, :] * _shifted(g, t_idx, 1)
        dx += w_ref[3:4, :] * _shifted(g, t_idx, -1)
        dx_ref[...] = dx.astype(dx_ref.dtype)
        dw_ref[0:1, :] = jnp.sum(g * _shifted(x, t_idx, -2), axis=0, keepdims=True)
        dw_ref[1:2, :] = jnp.sum(g * _shifted(x, t_idx, -1), axis=0, keepdims=True)
        dw_ref[2:3, :] = jnp.sum(g * x, axis=0, keepdims=True)
        dw_ref[3:4, :] = jnp.sum(g * _shifted(x, t_idx, 1), axis=0, keepdims=True)
        db_ref[...] = jnp.sum(g, axis=0, keepdims=True)

    nb = width // LANES
    return pl.pallas_call(
        body, name=name, grid=(nb,),
        in_specs=[pl.BlockSpec((n_rows, LANES), lambda j: (0, colblock * nb + j)),
                  pl.BlockSpec((4, LANES), lambda j: (0, j)),
                  pl.BlockSpec((n_rows, LANES), lambda j: (0, j))],
        out_specs=[pl.BlockSpec((n_rows, LANES), lambda j: (0, j)), pl.BlockSpec((4, LANES), lambda j: (0, j)),
                   pl.BlockSpec((1, LANES), lambda j: (0, j))],
        out_shape=[jax.ShapeDtypeStruct((n_rows, width), BF16), jax.ShapeDtypeStruct((4, width), F32),
                   jax.ShapeDtypeStruct((1, width), F32)],
        compiler_params=_params(("parallel",)),
    )(src, w, d)


_WHOLE = pl.BlockSpec(memory_space=pltpu.VMEM)
SCAN_UNROLL = 8


def _scan_fwd(name, a, u, reverse):
    n_rows, width = a.shape

    def body(a_ref, u_ref, h_ref):
        def step(i, h):
            t = (n_rows - 1 - i) if reverse else i
            h = a_ref[pl.ds(t, 1), :] * h + u_ref[pl.ds(t, 1), :]
            h_ref[pl.ds(t, 1), :] = h
            return h
        lax.fori_loop(0, n_rows, step, jnp.zeros((1, width), F32), unroll=SCAN_UNROLL)

    return pl.pallas_call(
        body, name=name, in_specs=[_WHOLE, _WHOLE], out_specs=_WHOLE,
        out_shape=jax.ShapeDtypeStruct((n_rows, width), F32),
        compiler_params=pltpu.CompilerParams(vmem_limit_bytes=VMEM_LIMIT),
    )(a, u)


def _scan_bwd(name, a, h, dh, reverse):
    n_rows, width = a.shape

    def body(a_ref, h_ref, dh_ref, du_ref, da_ref):
        def step(i, carry):
            t = i if reverse else (n_rows - 1 - i)
            g = dh_ref[pl.ds(t, 1), :] + carry
            du_ref[pl.ds(t, 1), :] = g
            tp = t + 1 if reverse else t - 1
            valid = (tp >= 0) & (tp < n_rows)
            h_prev = h_ref[pl.ds(jnp.clip(tp, 0, n_rows - 1), 1), :]
            da_ref[pl.ds(t, 1), :] = jnp.where(valid, g * h_prev, 0.0)
            return a_ref[pl.ds(t, 1), :] * g
        lax.fori_loop(0, n_rows, step, jnp.zeros((1, width), F32), unroll=SCAN_UNROLL)

    return pl.pallas_call(
        body, name=name, in_specs=[_WHOLE] * 3, out_specs=[_WHOLE] * 2,
        out_shape=[jax.ShapeDtypeStruct((n_rows, width), F32)] * 2,
        compiler_params=pltpu.CompilerParams(vmem_limit_bytes=VMEM_LIMIT),
    )(a, h, dh)


def _tri_mask(c, reverse):
    row = lax.broadcasted_iota(jnp.int32, (c, c), 0)
    col = lax.broadcasted_iota(jnp.int32, (c, c), 1)
    return (col >= row) if reverse else (col <= row)


def _cumsum_rows(x, reverse):
    tri = _tri_mask(x.shape[0], reverse).astype(BF16)
    hi = x.astype(BF16)
    rest = x - hi.astype(F32)
    mid = rest.astype(BF16)
    lo = (rest - mid.astype(F32)).astype(BF16)
    return _raw_nn(tri, hi) + _raw_nn(tri, mid) + _raw_nn(tri, lo)


@functools.partial(jax.custom_vjp, nondiff_argnums=(1,))
def _cumsum(x, reverse):
    return _cumsum_rows(x, reverse)


def _cumsum_fwd(x, reverse):
    return _cumsum_rows(x, reverse), None


def _cumsum_bwd(reverse, _, g):
    return (_cumsum_rows(g, not reverse),)


_cumsum.defvjp(_cumsum_fwd, _cumsum_bwd)


def _chunks_fn(qs, ks, vs, lfs, sts, reverses):
    n, c = len(qs), qs[0].shape[0]
    every = range(n)
    tris = [_tri_mask(c, r) for r in reverses]
    cums = [_cumsum(lfs[i], reverses[i]) for i in every]
    rid = lax.broadcasted_iota(jnp.int32, cums[0].shape, 0)

    def pick(cum, r):
        return jnp.sum(jnp.where(rid == r, cum, 0.0), axis=0, keepdims=True)

    refs = [pick(cums[i], (c - 1 - c // 2) if reverses[i] else c // 2) for i in every]
    lasts = [pick(cums[i], 0 if reverses[i] else c - 1) for i in every]
    q_in = [qs[i] * jnp.exp(cums[i] - refs[i]) for i in every]
    k_in = [ks[i] * jnp.exp(refs[i] - cums[i]) for i in every]
    scores = [jnp.where(tris[i], _dot_nt(q_in[i], k_in[i]), 0.0) for i in every]
    o_intra = [_dot_nn(scores[i], vs[i]) for i in every]
    q_out = [qs[i] * jnp.exp(cums[i]) for i in every]
    o_inter = [_dot_nt(q_out[i], sts[i]) for i in every]
    k_state = [ks[i] * jnp.exp(lasts[i] - cums[i]) for i in every]
    upd = [_dot_tn(vs[i], k_state[i]) for i in every]
    st_new = [sts[i] * jnp.exp(lasts[i]) + upd[i] for i in every]
    return [o_intra[i] + o_inter[i] for i in every], st_new


def _attn_fwd(name, q, k_f, k_b, v, lf_f, lf_b, n_heads, dk, dv):
    n_rows = q[0].shape[0]
    n_chunks = n_rows // CHUNK
    wk, wv = n_heads * dk, n_heads * dv

    def spec(width, off, rev):
        return pl.BlockSpec((CHUNK, width), lambda n: ((n_chunks - 1 - n) if rev else n, off))

    def sspec(rev):
        return pl.BlockSpec((None, n_heads, dv, dk), lambda n: ((n_chunks - 1 - n) if rev else n, 0, 0, 0))

    def body(qf, kf, vf, lff, qb, kb, vb, lfb, of_ref, ob_ref, sf_ref, sb_ref, st):
        @pl.when(pl.program_id(0) == 0)
        def _():
            st[...] = jnp.zeros_like(st)

        ins = ((qf, kf, vf, lff), (qb, kb, vb, lfb))
        chains = [(d, h) for d in range(2) for h in range(n_heads)]
        ck = [slice(h * dk, (h + 1) * dk) for h in range(n_heads)]
        cv = [slice(h * dv, (h + 1) * dv) for h in range(n_heads)]
        qs = [ins[d][0][:, ck[h]] for d, h in chains]
        ks = [ins[d][1][:, ck[h]] for d, h in chains]
        vs = [ins[d][2][:, cv[h]] for d, h in chains]
        lfs = [ins[d][3][:, ck[h]] for d, h in chains]
        sts = [st[d, h] for d, h in chains]
        os_, st_new = _chunks_fn(qs, ks, vs, lfs, sts, [d == 1 for d, _ in chains])
        for i, (d, h) in enumerate(chains):
            (sf_ref, sb_ref)[d][h] = sts[i]
            (of_ref, ob_ref)[d][:, cv[h]] = os_[i]
            st[d, h] = st_new[i]

    in_specs = [spec(wk, q[1], False), spec(wk, k_f[1], False), spec(wv, v[1], False), spec(wk, lf_f[1], False),
                spec(wk, q[1], True), spec(wk, k_b[1], True), spec(wv, v[1], True), spec(wk, lf_b[1], True)]
    return pl.pallas_call(
        body, name=name, grid=(n_chunks,), in_specs=in_specs,
        out_specs=[spec(wv, 0, False), spec(wv, 0, True), sspec(False), sspec(True)],
        out_shape=[jax.ShapeDtypeStruct((n_rows, wv), F32)] * 2
        + [jax.ShapeDtypeStruct((n_chunks, n_heads, dv, dk), F32)] * 2,
        scratch_shapes=[pltpu.VMEM((2, n_heads, dv, dk), F32)],
        compiler_params=_params(("arbitrary",)),
    )(q[0], k_f[0], v[0], lf_f[0], q[0], k_b[0], v[0], lf_b[0])


def _attn_bwd(name, q, k_f, k_b, v, lf_f, lf_b, st_f, st_b, do, n_heads, dk, dv, out_dtype=F32):
    n_rows = q[0].shape[0]
    n_chunks = n_rows // CHUNK
    wk, wv = n_heads * dk, n_heads * dv

    def spec(width, off, rev):
        return pl.BlockSpec((CHUNK, width), lambda n: (n if rev else (n_chunks - 1 - n), off))

    def sspec(rev):
        return pl.BlockSpec((None, n_heads, dv, dk), lambda n: (n if rev else (n_chunks - 1 - n), 0, 0, 0))

    def body(qf, kf, vf, lff, sf, dof, qb, kb, vb, lfb, sb, dob,
             dqf, dkf, dvf, dlff, dqb, dkb, dvb, dlfb, dst):
        @pl.when(pl.program_id(0) == 0)
        def _():
            dst[...] = jnp.zeros_like(dst)

        ins = ((qf, kf, vf, lff, sf, dof), (qb, kb, vb, lfb, sb, dob))
        outs = ((dqf, dkf, dvf, dlff), (dqb, dkb, dvb, dlfb))
        chains = [(d, h) for d in range(2) for h in range(n_heads)]
        ck = [slice(h * dk, (h + 1) * dk) for h in range(n_heads)]
        cv = [slice(h * dv, (h + 1) * dv) for h in range(n_heads)]
        qs = [ins[d][0][:, ck[h]] for d, h in chains]
        ks = [ins[d][1][:, ck[h]] for d, h in chains]
        vs = [ins[d][2][:, cv[h]] for d, h in chains]
        lfs = [ins[d][3][:, ck[h]] for d, h in chains]
        sts = [ins[d][4][h] for d, h in chains]
        dos = [ins[d][5][:, cv[h]] for d, h in chains]
        dsts = [dst[d, h] for d, h in chains]
        fn = functools.partial(_chunks_fn, reverses=[d == 1 for d, _ in chains])
        _, vjp = jax.vjp(fn, qs, ks, vs, lfs, sts)
        dqs, dks, dvs, dlfs, dst_prev = vjp((dos, dsts))
        for i, (d, h) in enumerate(chains):
            dq_r, dk_r, dv_r, dlf_r = outs[d]
            dq_r[:, ck[h]] = dqs[i].astype(dq_r.dtype)
            dk_r[:, ck[h]] = dks[i].astype(dk_r.dtype)
            dv_r[:, cv[h]] = dvs[i].astype(dv_r.dtype)
            dlf_r[:, ck[h]] = dlfs[i].astype(dlf_r.dtype)
            dst[d, h] = dst_prev[i]

    def dir_specs(kk, lf, rev):
        return [spec(wk, q[1], rev), spec(wk, kk[1], rev), spec(wv, v[1], rev), spec(wk, lf[1], rev), sspec(rev),
                spec(wv, 0, rev)]

    def dir_out_specs(rev):
        return [spec(wk, 0, rev), spec(wk, 0, rev), spec(wv, 0, rev), spec(wk, 0, rev)]

    shapes = [jax.ShapeDtypeStruct((n_rows, wk), out_dtype), jax.ShapeDtypeStruct((n_rows, wk), out_dtype),
              jax.ShapeDtypeStruct((n_rows, wv), out_dtype), jax.ShapeDtypeStruct((n_rows, wk), F32)]
    outs = pl.pallas_call(
        body, name=name, grid=(n_chunks,), in_specs=dir_specs(k_f, lf_f, False) + dir_specs(k_b, lf_b, True),
        out_specs=dir_out_specs(False) + dir_out_specs(True), out_shape=shapes + shapes,
        scratch_shapes=[pltpu.VMEM((2, n_heads, dv, dk), F32)],
        compiler_params=_params(("arbitrary",)),
    )(q[0], k_f[0], v[0], lf_f[0], st_f, do, q[0], k_b[0], v[0], lf_b[0], st_b, do)
    return outs[:4], outs[4:]


def _row2(v):
    return v.reshape(1, -1)


def _mlp_fwd(tag, h, gain, w1, w2):
    y = _rowcall(f"{tag}_norm", _rmsnorm_fn, [(h, h.shape[1], 0)], [gain], [(h.shape[1], BF16)], tm=512)[0]
    hid = _mm(f"{tag}_up", y, w1, out_dtypes=(BF16,))
    h_out = _mm(f"{tag}_down", hid, w2, a_pro=_relu2, extras=(h,), epi=_add_epi)
    return h_out, (y, hid)


def _mlp_bwd(tag, h, gain, w1, w2, saved, dh_out):
    y, hid = saved
    dhid = _mm(f"{tag}_dact", dh_out, w2, mode="nt", extras=(hid,), epi=_relu2_bwd_epi, out_dtypes=(BF16,))
    dw2 = _mm(f"{tag}_dw2", hid, dh_out, mode="tn", a_pro=_relu2)
    dw1 = _mm(f"{tag}_dw1", y, dhid, mode="tn", out_split=N_CHIPS)
    dy = _mm(f"{tag}_dy", dhid, w1, mode="nt")
    dh, dgain = _norm_bwd(f"{tag}_dnorm", h, gain, dy, dh_out)
    return dh, dgain, dw1, dw2


def _norm_bwd(name, h, gain, dy, dres, pin=None):
    d = h.shape[1]

    def fn(h, dy, dres, gain):
        _, vjp = jax.vjp(lambda a, b: _rmsnorm_fn(a, b)[0], h, gain)
        dh, dgain = vjp(dy)
        return dh + dres, dgain

    dh, dgain = _rowcall(name, fn, [(h, d, 0), (dy, d, 0), (dres, d, 0)], [gain], [(d, F32)], [(1, d)], tm=512, pin=pin)
    return dh, dgain


def _local_step(x, target, w, pin=None, late=None, emit=None):
    g = {}
    d_model = x.shape[1]
    rg_w = hg_w = d_model // 2
    pins = []

    def send_off(tag, arrays):
        if emit is not None:
            pins.append(emit(tag, arrays))

    def chip_major(t):
        return t.reshape(N_CHIPS, t.shape[0] // N_CHIPS, t.shape[1])

    h_a0 = x
    gain = _row2(w["norm_mix"][0])
    y0 = _rowcall("l0_norm", _rmsnorm_fn, [(h_a0, d_model, 0)], [gain], [(d_model, BF16)], tm=512, pin=pin)[0]
    proj0 = _mm("l0_in", y0, w["ab_w_in"])
    conv_w, conv_b = w["rg_conv_w"], _row2(w["rg_conv_b"])
    xc = _conv_fwd("rg_conv", proj0, 0, conv_w, conv_b)
    gate_pars = [w["rg_wa_bd"], w["rg_wx_bd"], w["rg_b_a"], w["rg_b_x"], w["rg_lambda"]]
    a_f, u_f, a_b, u_b = _rowcall("rg_gates", _rg_gates_fn, [(xc, rg_w, 0)], gate_pars, [(rg_w, F32)] * 4)
    hs_f = _scan_fwd("rg_scan_f", a_f, u_f, False)
    hs_b = _scan_fwd("rg_scan_b", a_b, u_b, True)
    hg_rows = [(proj0, hg_w, 2), (proj0, hg_w, 3), (proj0, hg_w, 4)]
    qh, k_f, lf_f, k_b, lf_b = _rowcall("hg_pre", _hg_pre_fn, hg_rows, [w["hg_lb_logits"]], [(hg_w, F32)] * 5)
    iv = (proj0, 5)
    o_f, o_b, st_f, st_b = _attn_fwd("hg_attn", (qh, 0), (k_f, 0), (k_b, 0), iv, (lf_f, 0), (lf_b, 0), 4, 128, 128)
    post0_rows = [(hs_f, rg_w, 0), (hs_b, rg_w, 0), (proj0, rg_w, 1), (o_f, hg_w, 0), (o_b, hg_w, 0), (proj0, hg_w, 6)]
    hg_gain = _row2(w["hg_norm"])
    mix_in0 = _rowcall("l0_post", _post0_fwd_fn, post0_rows, [hg_gain], [(d_model, BF16)])[0]
    h_b0 = _mm("l0_out", mix_in0, w["ab_w_out"], extras=(h_a0,), epi=_add_epi)
    if late is not None:
        w = {**w, **late(h_b0)}
    h_c0, mlp0 = _mlp_fwd("mlp0", h_b0, _row2(w["norm_mlp"][0]), w["mlp_w1"][0], w["mlp_w2"][0])

    h_a1 = h_c0
    gain1 = _row2(w["norm_mix"][1])
    y1 = _rowcall("l1_norm", _rmsnorm_fn, [(h_a1, d_model, 0)], [gain1], [(d_model, BF16)], tm=512)[0]
    proj1 = _mm("l1_in", y1, w["gla_w_in_pad"], tn=640)
    gla_pars = [w["gla_w_up_pad"], w["gla_b_gate"]]
    gq, glf_f, glf_b = _rowcall("gla_pre", _gla_pre_fn, [(proj1, 512, 0), (proj1, LANES, 24)], gla_pars, [(512, F32)] * 3)
    gk, gv = (proj1, 1), (proj1, 1)
    go_f, go_b, gst_f, gst_b = _attn_fwd("gla_attn", (gq, 0), gk, gk, gv, (glf_f, 0), (glf_b, 0), 4, 128, 256)
    gla_gain = _row2(w["gla_norm"])
    post1_rows = [(go_f, d_model, 0), (go_b, d_model, 0), (proj1, d_model, 2)]
    mix_in1 = _rowcall("l1_post", _gla_post_fwd_fn, post1_rows, [gla_gain], [(d_model, BF16)])[0]
    h_b1 = _mm("l1_out", mix_in1, w["gla_w_out"], extras=(h_a1,), epi=_add_epi)
    h_c1, mlp1 = _mlp_fwd("mlp1", h_b1, _row2(w["norm_mlp"][1]), w["mlp_w1"][1], w["mlp_w2"][1])

    dh, loss, g["norm_final"] = _rowcall(
        "loss_head", _loss_head_fn, [(h_c1, d_model, 0), (target, d_model, 0)], [_row2(w["norm_final"])],
        [(d_model, F32)], [(1, LANES), (1, d_model)], tm=512)

    dh, g_nmlp1, g_w1_1, g_w2_1 = _mlp_bwd("mlp1", h_b1, _row2(w["norm_mlp"][1]), w["mlp_w1"][1], w["mlp_w2"][1], mlp1, dh)
    send_off("mlp1", [g_w1_1, chip_major(g_w2_1)])
    dmix1 = _mm("l1_dout", dh, w["gla_w_out"], mode="nt")
    g["gla_w_out"] = _mm("l1_dwout", mix_in1, dh, mode="tn")
    dgo, dr, g["gla_norm"] = _rowcall(
        "l1_dpost", _gla_post_bwd_fn, post1_rows + [(dmix1, d_model, 0)], [gla_gain],
        [(d_model, F32), (d_model, BF16)], [(1, d_model)], pin=pins.pop() if pins else None)
    (dq_f, dk_f, dv_f, dlf_f), (dq_b, dk_b, dv_b, dlf_b) = _attn_bwd(
        "gla_dattn", (gq, 0), gk, gk, gv, (glf_f, 0), (glf_b, 0), gst_f, gst_b, dgo, 4, 128, 256)

    def gla_pre_bwd(q, lr, dq1, dq2, dlf1, dlf2, dk1, dk2, dv1, dv2, w_up, b_gate):
        dlr = jnp.zeros_like(lr)
        dws, dbs = [], []
        for d, dlf in enumerate((dlf1, dlf2)):
            z = _raw_nn(lr, w_up[d]) + b_gate[d:d + 1]
            dz = dlf * _sigmoid(-z) * (1.0 / 16.0)
            dlr = dlr + _raw_nt(dz, w_up[d])
            dws.append(_raw_tn(dz, lr))
            dbs.append(jnp.sum(dz, axis=0, keepdims=True))
        return ((dq1 + dq2) * (128.0 ** -0.5), dk1 + dk2, dv1 + dv2, dlr, dws[0], dws[1], dbs[0], dbs[1])

    rows = [(proj1, 512, 0), (proj1, LANES, 24), (dq_f, 512, 0), (dq_b, 512, 0), (dlf_f, 512, 0), (dlf_b, 512, 0),
            (dk_f, 512, 0), (dk_b, 512, 0), (dv_f, d_model, 0), (dv_b, d_model, 0)]
    dq, dk, dv, dlr, dwt_f, dwt_b, db_f, db_b = _rowcall(
        "gla_dpre", gla_pre_bwd, rows, gla_pars, [(512, BF16), (512, BF16), (d_model, BF16), (LANES, BF16)],
        [(512, LANES), (512, LANES), (1, 512), (1, 512)])
    g["gla_w_up_pad"] = jnp.stack([dwt_f.T, dwt_b.T])
    g["gla_b_gate"] = jnp.concatenate([db_f, db_b], axis=0)
    dproj1 = jnp.concatenate([dq, dk, dv, dr, dlr], axis=1)
    g_gla_in = _mm("l1_dwin", y1, dproj1, mode="tn", tn=640)
    g["gla_w_in"] = _split_chips(g_gla_in[:, :GLA_IN_WIDTH], 1)
    send_off("gla", [g["gla_w_in"], chip_major(g["gla_w_out"])])
    dy1 = _mm("l1_dy", dproj1, w["gla_w_in_pad"], mode="nt", tk=640)
    dh, g_nmix1 = _norm_bwd("l1_dnorm", h_a1, gain1, dy1, dh, pin=pins.pop() if pins else None)

    dh, g_nmlp0, g_w1_0, g_w2_0 = _mlp_bwd("mlp0", h_b0, _row2(w["norm_mlp"][0]), w["mlp_w1"][0], w["mlp_w2"][0], mlp0, dh)
    send_off("mlp0", [g_w1_0, chip_major(g_w2_0)])
    dmix0 = _mm("l0_dout", dh, w["ab_w_out"], mode="nt")
    g["ab_w_out"] = _mm("l0_dwout", mix_in0, dh, mode="tn")
    dhs, dga, do, dg, g["hg_norm"] = _rowcall(
        "l0_dpost", _post0_bwd_fn, post0_rows + [(dmix0, d_model, 0)], [hg_gain],
        [(rg_w, F32), (rg_w, BF16), (hg_w, F32), (hg_w, BF16)], [(1, hg_w)], pin=pins.pop() if pins else None)
    (dqh_f, dk_f, div_f, dlf_f), (dqh_b, dk_b, div_b, dlf_b) = _attn_bwd(
        "hg_dattn", (qh, 0), (k_f, 0), (k_b, 0), iv, (lf_f, 0), (lf_b, 0), st_f, st_b, do, 4, 128, 128)

    def hg_pre_bwd(q, f_f, f_b, dq1, dq2, dk1, dlf1, dk2, dlf2, dv1, dv2, logits):
        _, vjp = jax.vjp(_hg_pre_fn, q, f_f, f_b, logits)
        dq, df_f, df_b, dlogits = vjp((dq1 + dq2, dk1, dlf1, dk2, dlf2))
        return dq, df_f, df_b, dv1 + dv2, dlogits

    rows = hg_rows + [(t, hg_w, 0) for t in (dqh_f, dqh_b, dk_f, dlf_f, dk_b, dlf_b, div_f, div_b)]
    dq, df_f, df_b, div, g["hg_lb_logits"] = _rowcall(
        "hg_dpre", hg_pre_bwd, rows, [w["hg_lb_logits"]], [(hg_w, BF16)] * 4, [(2, hg_w)])
    du_f, da_f = _scan_bwd("rg_dscan_f", a_f, hs_f, dhs, False)
    du_b, da_b = _scan_bwd("rg_dscan_b", a_b, hs_b, dhs, True)
    gates_bwd = _vjp_of(_rg_gates_fn, 1, 4, 5)
    rows = [(xc, rg_w, 0), (da_f, rg_w, 0), (du_f, rg_w, 0), (da_b, rg_w, 0), (du_b, rg_w, 0)]
    dxc, g["rg_wa_bd"], g["rg_wx_bd"], g["rg_b_a"], g["rg_b_x"], g["rg_lambda"] = _rowcall(
        "rg_dgates", gates_bwd, rows, gate_pars, [(rg_w, F32)],
        [(2, rg_w, rg_w), (2, rg_w, rg_w), (2, rg_w), (2, rg_w), (2, rg_w)])
    dxa, g["rg_conv_w"], g["rg_conv_b"] = _conv_bwd("rg_dconv", proj0, 0, conv_w, dxc)
    dproj0 = jnp.concatenate([dxa, dga, dq, df_f, df_b, div, dg], axis=1)
    g["ab_w_in"] = _mm("l0_dwin", y0, dproj0, mode="tn", out_split=N_CHIPS)
    dy0 = _mm("l0_dy", dproj0, w["ab_w_in"], mode="nt")
    grad_x, g_nmix0 = _norm_bwd("l0_dnorm", h_a0, gain, dy0, dh)

    g["norm_mix"] = jnp.concatenate([g_nmix0, g_nmix1], axis=0)
    g["norm_mlp"] = jnp.concatenate([g_nmlp0, g_nmlp1], axis=0)
    g["mlp_w1"] = [g_w1_0, g_w1_1]
    g["mlp_w2"] = [g_w2_0, g_w2_1]
    return loss, grad_x, g


def _block_diag(w):
    d, g, n, _ = w.shape
    eye = jnp.eye(g, dtype=w.dtype)
    return (w[:, :, :, None, :] * eye[None, :, None, :, None]).reshape(d, g * n, g * n)


def _block_diag_extract(wbd, g):
    d, gn, _ = wbd.shape
    n = gn // g
    blocks = wbd.reshape(d, g, n, g, n)
    return jnp.stack([blocks[:, i, :, i, :] for i in range(g)], axis=1)


def _prepare_weights(big, full):
    w = {k: full[k] for k in ("norm_mix", "norm_mlp", "norm_final", "hg_lb_logits")}
    for k in ("rg_conv_w", "rg_conv_b", "rg_b_a", "rg_b_x", "rg_lambda", "hg_norm", "gla_b_gate", "gla_norm"):
        w[k] = full[k][0]
    w["rg_wa_bd"] = _block_diag(full["rg_w_a"][0])
    w["rg_wx_bd"] = _block_diag(full["rg_w_x"][0])
    up = full["gla_w_gate_up"][0]
    rank = up.shape[1]
    pad = jnp.zeros((2, LANES, up.shape[2]), F32)
    w["gla_w_up_pad"] = pad.at[0, 0:rank].set(up[0]).at[1, rank:2 * rank].set(up[1])
    w.update(_prepare_matrices(big))
    return w


def _prepare_matrices(big):
    w = {}
    if "mlp_w1" in big:
        w["mlp_w1"] = list(big["mlp_w1"])
        w["mlp_w2"] = [t.reshape(-1, t.shape[-1]) for t in big["mlp_w2"]]
    if "ab_w_in" in big:
        w["ab_w_in"] = big["ab_w_in"]
        w["ab_w_out"] = big["ab_w_out"].reshape(-1, big["ab_w_out"].shape[-1])
    if "gla_w_in" in big:
        w["gla_w_out"] = big["gla_w_out"].reshape(-1, big["gla_w_out"].shape[-1])
        gla_in = _join_chips(big["gla_w_in"], 1)
        w["gla_w_in_pad"] = jnp.pad(gla_in, ((0, 0), (0, GLA_IN_PAD - gla_in.shape[1])))
    return w


def _finish_grads(g, rank=16, rg_blocks=8):
    def chip_major(t):
        return t.reshape(N_CHIPS, t.shape[0] // N_CHIPS, t.shape[1])

    big = {
        "mlp_w1": list(g["mlp_w1"]), "mlp_w2": [chip_major(t) for t in g["mlp_w2"]],
        "ab_w_in": g["ab_w_in"], "ab_w_out": chip_major(g["ab_w_out"]),
        "gla_w_in": g["gla_w_in"], "gla_w_out": chip_major(g["gla_w_out"]),
    }
    small = {
        "norm_mix": g["norm_mix"], "norm_mlp": g["norm_mlp"], "norm_final": g["norm_final"][0],
        "rg_conv_w": g["rg_conv_w"][None], "rg_conv_b": g["rg_conv_b"],
        "rg_w_a": _block_diag_extract(g["rg_wa_bd"], rg_blocks)[None], "rg_b_a": g["rg_b_a"][None],
        "rg_w_x": _block_diag_extract(g["rg_wx_bd"], rg_blocks)[None], "rg_b_x": g["rg_b_x"][None],
        "rg_lambda": g["rg_lambda"][None], "hg_lb_logits": g["hg_lb_logits"], "hg_norm": g["hg_norm"],
        "gla_w_gate_up": jnp.stack([g["gla_w_up_pad"][0, 0:rank], g["gla_w_up_pad"][1, rank:2 * rank]])[None],
        "gla_b_gate": g["gla_b_gate"][None], "gla_norm": g["gla_norm"],
    }
    return big, small


MATRICES = (("mlp_w1", 0), ("mlp_w1", 1), ("mlp_w2", 0), ("mlp_w2", 1), ("ab_w_in", 0), ("ab_w_out", 0),
            ("gla_w_in", 0), ("gla_w_out", 0))
EARLY_MATRICES = ("ab_w_in", "ab_w_out")
SMALL_SHARDED = ("rg_conv_w", "rg_b_a", "rg_b_x", "rg_lambda", "gla_w_gate_up", "gla_b_gate", "gla_norm")
SMALL_REPLICATED = ("norm_mix", "norm_mlp", "norm_final", "rg_conv_b", "rg_w_a", "rg_w_x", "hg_lb_logits", "hg_norm")
WEIGHTS = ("norm_mix", "norm_mlp", "norm_final", "mlp_w1", "mlp_w2", "ab_w_in", "ab_w_out", "rg_conv_w", "rg_conv_b",
           "rg_w_a", "rg_b_a", "rg_w_x", "rg_b_x", "rg_lambda", "hg_lb_logits", "hg_norm", "gla_w_in", "gla_w_out",
           "gla_w_gate_up", "gla_b_gate", "gla_norm")
ROW_ALIGN = 16


def _pack(arrays, lead=0):
    head = arrays[0].shape[:lead]
    flat = jnp.concatenate([a.reshape(head + (-1,)) for a in arrays], axis=lead)
    n = flat.shape[-1]
    quantum = LANES * ROW_ALIGN
    padded = -(-n // quantum) * quantum
    if padded != n:
        flat = jnp.pad(flat, [(0, 0)] * lead + [(0, padded - n)])
    return flat.reshape(head + (padded // LANES, LANES))


def _unpack(buf, shapes, lead=0):
    head = buf.shape[:lead]
    flat = buf.reshape(head + (-1,))
    out, off = [], 0
    for s in shapes:
        n = 1
        for v in s:
            n *= v
        out.append(lax.slice_in_dim(flat, off, off + n, axis=lead).reshape(head + tuple(s)))
        off += n
    return out


def _join_chips(gathered, axis):
    t = jnp.moveaxis(gathered, 0, axis)
    return t.reshape(t.shape[:axis] + (t.shape[axis] * t.shape[axis + 1],) + t.shape[axis + 2:])


def _split_chips(full, axis):
    s = full.shape
    t = full.reshape(s[:axis] + (N_CHIPS, s[axis] // N_CHIPS) + s[axis + 1:])
    return jnp.moveaxis(t, axis, 0)


_ANY = pl.BlockSpec(memory_space=pl.ANY)


def _place():
    return lax.axis_index("x"), lax.axis_index("y"), lax.axis_index("c")


def _into_slot(name, src, slot, n_slots, dtype, tm, layer=None):
    r, lanes = src.shape[-2:]
    tm = _row_tile(r, tm, ROW_ALIGN)

    def body(slot_ref, in_ref, o_ref):
        o_ref[...] = in_ref[...].astype(o_ref.dtype)

    if layer is None:
        in_spec = pl.BlockSpec((tm, lanes), lambda i, slot_ref: (i, 0))
    else:
        in_spec = pl.BlockSpec((None, tm, lanes), lambda i, slot_ref: (layer, i, 0))
    grid_spec = pltpu.PrefetchScalarGridSpec(
        num_scalar_prefetch=1, grid=(r // tm,), in_specs=[in_spec],
        out_specs=pl.BlockSpec((None, tm, lanes), lambda i, slot_ref: (slot_ref[0], i, 0)))
    return pl.pallas_call(
        body, name=name, grid_spec=grid_spec, out_shape=jax.ShapeDtypeStruct((n_slots, r, lanes), dtype),
        compiler_params=_params(("parallel",)),
    )(slot.reshape(1).astype(jnp.int32), src)


def _chip_peers():
    x, y, c = _place()
    return 2 * x + y, c, [(1 - x, y), (x, 1 - y), (1 - x, 1 - y)]


def _comm_call(name, body, ins, out_shapes, n_sems, aliases=None):
    return pl.pallas_call(
        body, name=name, in_specs=[_ANY] * len(ins), out_specs=[_ANY] * len(out_shapes), out_shape=out_shapes,
        input_output_aliases=aliases or {},
        scratch_shapes=[pltpu.SemaphoreType.DMA((n_sems,)), pltpu.SemaphoreType.DMA((n_sems,))],
    )(*ins)


def _gather_chips(name, bufs):
    n = len(bufs)

    def body(*refs):
        outs, send_sems, recv_sems = refs[n:2 * n], refs[2 * n], refs[2 * n + 1]
        x, y, c = _place()
        me, _, peers = _chip_peers()

        def rows(a, block, half):
            rh = outs[a].shape[1] // 2
            return outs[a].at[block, pl.ds(half * rh, rh)]

        def copy(a, j, block, half, to, sem):
            return pltpu.make_async_remote_copy(
                src_ref=rows(a, block, half), dst_ref=rows(a, block, half), send_sem=send_sems.at[sem],
                recv_sem=recv_sems.at[sem], device_id=to, device_id_type=MESH)

        def over_ici(a, j, block):
            px, py = peers[j]
            return copy(a, j, block, c, (px, py, c), 6 * a + j)

        def to_sibling(a, j, block, half):
            return copy(a, j, block, half, (x, y, 1 - c), 6 * a + 3 + j)

        sends = [over_ici(a, j, me) for a in range(n) for j in range(3)]
        for cp in sends:
            cp.start()
        for a in range(n):
            for j, (px, py) in enumerate(peers):
                over_ici(a, j, 2 * px + py).wait_recv()
                handed = to_sibling(a, j, 2 * px + py, c)
                handed.start()
                sends.append(handed)
        for a in range(n):
            for j, (px, py) in enumerate(peers):
                to_sibling(a, j, 2 * px + py, 1 - c).wait_recv()
        for cp in sends:
            cp.wait_send()

    shapes = [jax.ShapeDtypeStruct(b.shape, b.dtype) for b in bufs]
    return _comm_call(name, body, bufs, shapes, 6 * n, {a: a for a in range(n)})


_HBM = pl.BlockSpec(memory_space=pltpu.HBM)
_SEM = pl.BlockSpec(memory_space=pltpu.SEMAPHORE)
_EFFECT = pltpu.SideEffectType.DATAFLOW_SIDE_EFFECTING


def _half_rows(ref, block, half):
    rh = ref.shape[1] // 2
    return ref.at[block, pl.ds(half * rh, rh)]


def _gather_start(name, bufs, after):
    n = len(bufs)

    def body(*refs):
        ins, send_sems, recv_sems, token = refs[:n], refs[n + 1], refs[n + 2], refs[-1]
        me, c, peers = _chip_peers()
        for a in range(n):
            mine = _half_rows(ins[a], me, c)
            for j, (px, py) in enumerate(peers):
                pltpu.make_async_remote_copy(
                    src_ref=mine, dst_ref=mine, send_sem=send_sems.at[3 * a + j], recv_sem=recv_sems.at[3 * a + j],
                    device_id=(px, py, c), device_id_type=MESH).start()
        token[...] = jnp.zeros_like(token)

    out_shape = (pltpu.SemaphoreType.DMA((3 * n,)), pltpu.SemaphoreType.DMA((3 * n,)),
                 *[pltpu.HBM(b.shape, b.dtype) for b in bufs], jax.ShapeDtypeStruct((8, LANES), F32))
    return pl.pallas_call(
        body, name=name, out_shape=out_shape, in_specs=[_HBM] * n + [_ANY],
        out_specs=(_SEM, _SEM, *[_HBM] * n, pl.BlockSpec(memory_space=pltpu.VMEM)),
        input_output_aliases={a: 2 + a for a in range(n)},
        compiler_params=pltpu.CompilerParams(has_side_effects=_EFFECT),
    )(*[pltpu.with_memory_space_constraint(b, pltpu.HBM) for b in bufs], after)


def _gather_wait(name, bufs, send_sems, recv_sems, after):
    n = len(bufs)

    def body(*refs):
        ins, send_sems, recv_sems = refs[:n], refs[n], refs[n + 1]
        me, c, peers = _chip_peers()
        for a in range(n):
            for j, (px, py) in enumerate(peers):
                copy = pltpu.make_async_remote_copy(
                    src_ref=_half_rows(ins[a], me, c), dst_ref=_half_rows(ins[a], 2 * px + py, c),
                    send_sem=send_sems.at[3 * a + j], recv_sem=recv_sems.at[3 * a + j],
                    device_id=(px, py, c), device_id_type=MESH)
                copy.wait_send()
                copy.wait_recv()

    return pl.pallas_call(
        body, name=name, out_shape=tuple(pltpu.HBM(b.shape, b.dtype) for b in bufs),
        in_specs=[_HBM] * n + [_SEM, _SEM, _ANY], out_specs=tuple([_HBM] * n),
        input_output_aliases={a: a for a in range(n)},
        compiler_params=pltpu.CompilerParams(has_side_effects=_EFFECT),
    )(*bufs, send_sems, recv_sems, after)


def _hand_over(name, bufs):
    n = len(bufs)

    def body(*refs):
        outs, send_sems, recv_sems = refs[n:2 * n], refs[2 * n], refs[2 * n + 1]
        x, y, c = _place()
        _, _, peers = _chip_peers()

        def copy(a, j, half):
            px, py = peers[j]
            rows = _half_rows(outs[a], 2 * px + py, half)
            return pltpu.make_async_remote_copy(
                src_ref=rows, dst_ref=rows, send_sem=send_sems.at[3 * a + j], recv_sem=recv_sems.at[3 * a + j],
                device_id=(x, y, 1 - c), device_id_type=MESH)

        sends = [copy(a, j, c) for a in range(n) for j in range(3)]
        for cp in sends:
            cp.start()
        for a in range(n):
            for j in range(3):
                copy(a, j, 1 - c).wait_recv()
        for cp in sends:
            cp.wait_send()

    shapes = [jax.ShapeDtypeStruct(b.shape, b.dtype) for b in bufs]
    return _comm_call(name, body, bufs, shapes, 3 * n, {a: a for a in range(n)})


def _pair_exchange(name, gs):
    n = len(gs)

    def body(*refs):
        ins, outs, send_sems, recv_sems = refs[:n], refs[n:2 * n], refs[2 * n], refs[2 * n + 1]
        x, y, c = _place()
        copies = [pltpu.make_async_remote_copy(
            src_ref=ins[a].at[:, 1 - c], dst_ref=outs[a], send_sem=send_sems.at[a], recv_sem=recv_sems.at[a],
            device_id=(x, y, 1 - c), device_id_type=MESH) for a in range(n)]
        for cp in copies:
            cp.start()
        for cp in copies:
            cp.wait()

    shapes = [jax.ShapeDtypeStruct((g.shape[0],) + g.shape[2:], g.dtype) for g in gs]
    return _comm_call(name, body, gs, shapes, n)


def _pair_add(name, g, got, c):
    n, _, rh, lanes = g.shape
    tm = _row_tile(rh, 2048, ROW_ALIGN)

    def body(c_ref, g_ref, got_ref, o_ref, o16_ref):
        s = g_ref[...] + got_ref[...]
        o_ref[...] = s
        o16_ref[...] = s.astype(BF16)

    out_spec = pl.BlockSpec((None, tm, lanes), lambda s, i, c_ref: (s, i, 0))
    grid_spec = pltpu.PrefetchScalarGridSpec(
        num_scalar_prefetch=1, grid=(n, rh // tm),
        in_specs=[pl.BlockSpec((None, None, tm, lanes), lambda s, i, c_ref: (s, c_ref[0], i, 0)),
                  pl.BlockSpec((None, tm, lanes), lambda s, i, c_ref: (s, i, 0))],
        out_specs=[out_spec, out_spec])
    return pl.pallas_call(
        body, name=name, grid_spec=grid_spec,
        out_shape=[jax.ShapeDtypeStruct((n, rh, lanes), F32), jax.ShapeDtypeStruct((n, rh, lanes), BF16)],
        compiler_params=_params(("parallel", "parallel")),
    )(c.reshape(1).astype(jnp.int32), g, got)


def _chip_scatter(name, ps):
    n = len(ps)

    def body(*refs):
        ins, outs, send_sems, recv_sems = refs[:n], refs[n:2 * n], refs[2 * n], refs[2 * n + 1]
        me, c, peers = _chip_peers()

        def copy(a, j, src_block, dst_block):
            px, py = peers[j]
            return pltpu.make_async_remote_copy(
                src_ref=ins[a].at[src_block], dst_ref=outs[a].at[dst_block], send_sem=send_sems.at[3 * a + j],
                recv_sem=recv_sems.at[3 * a + j], device_id=(px, py, c), device_id_type=MESH)

        sends = [copy(a, j, 2 * px + py, me) for a in range(n) for j, (px, py) in enumerate(peers)]
        for cp in sends:
            cp.start()
        for a in range(n):
            for j, (px, py) in enumerate(peers):
                copy(a, j, me, 2 * px + py).wait_recv()
        for cp in sends:
            cp.wait_send()

    shapes = [jax.ShapeDtypeStruct(p.shape, p.dtype) for p in ps]
    return _comm_call(name, body, ps, shapes, 3 * n)


def _scatter_start(name, ps, lands):
    n = len(ps)

    def body(*refs):
        srcs, dsts, send_sems, recv_sems, token = refs[:n], refs[n:2 * n], refs[2 * n], refs[2 * n + 1], refs[-1]
        me, c, peers = _chip_peers()
        for a in range(n):
            for j, (px, py) in enumerate(peers):
                pltpu.make_async_remote_copy(
                    src_ref=srcs[a].at[2 * px + py], dst_ref=dsts[a].at[me], send_sem=send_sems.at[3 * a + j],
                    recv_sem=recv_sems.at[3 * a + j], device_id=(px, py, c), device_id_type=MESH).start()
        token[...] = jnp.zeros_like(token)

    bufs = list(ps) + list(lands)
    out_shape = (pltpu.SemaphoreType.DMA((3 * n,)), pltpu.SemaphoreType.DMA((3 * n,)),
                 *[pltpu.HBM(b.shape, b.dtype) for b in bufs], jax.ShapeDtypeStruct((8, LANES), F32))
    return pl.pallas_call(
        body, name=name, out_shape=out_shape, in_specs=[_HBM] * (2 * n),
        out_specs=(_SEM, _SEM, *[_HBM] * (2 * n), pl.BlockSpec(memory_space=pltpu.VMEM)),
        input_output_aliases={a: 2 + a for a in range(2 * n)},
        compiler_params=pltpu.CompilerParams(has_side_effects=_EFFECT),
    )(*[pltpu.with_memory_space_constraint(b, pltpu.HBM) for b in bufs])


def _scatter_wait(name, ps, lands, send_sems, recv_sems, after):
    n = len(ps)

    def body(*refs):
        srcs, dsts, send_sems, recv_sems = refs[:n], refs[n:2 * n], refs[2 * n], refs[2 * n + 1]
        me, c, peers = _chip_peers()
        for a in range(n):
            for j, (px, py) in enumerate(peers):
                copy = pltpu.make_async_remote_copy(
                    src_ref=srcs[a].at[2 * px + py], dst_ref=dsts[a].at[2 * px + py],
                    send_sem=send_sems.at[3 * a + j], recv_sem=recv_sems.at[3 * a + j],
                    device_id=(px, py, c), device_id_type=MESH)
                copy.wait_send()
                copy.wait_recv()

    bufs = list(ps) + list(lands)
    outs = pl.pallas_call(
        body, name=name, out_shape=tuple(pltpu.HBM(b.shape, b.dtype) for b in bufs),
        in_specs=[_HBM] * (2 * n) + [_SEM, _SEM, _ANY], out_specs=tuple([_HBM] * (2 * n)),
        input_output_aliases={a: a for a in range(2 * n)},
        compiler_params=pltpu.CompilerParams(has_side_effects=_EFFECT),
    )(*bufs, send_sems, recv_sems, after)
    return list(outs[n:])


def _sum_ring(name, own, got, chip, core):
    n, rh, lanes = own.shape
    tm = _row_tile(rh, 2048, ROW_ALIGN)

    def body(idx_ref, own_ref, g1_ref, g2_ref, g3_ref, o_ref):
        o_ref[...] = ((own_ref[...] + g1_ref[...].astype(F32)) + g2_ref[...].astype(F32)) + g3_ref[...].astype(F32)

    def block(k):
        return pl.BlockSpec((None, tm, lanes), lambda i, idx_ref: ((idx_ref[0] + k) % n, i, 0))

    grid_spec = pltpu.PrefetchScalarGridSpec(
        num_scalar_prefetch=1, grid=(rh // tm,), in_specs=[block(0), block(1), block(2), block(3)],
        out_specs=pl.BlockSpec((None, tm, lanes), lambda i, idx_ref: (idx_ref[1], i, 0)))
    return pl.pallas_call(
        body, name=name, grid_spec=grid_spec, out_shape=jax.ShapeDtypeStruct((2, rh, lanes), F32),
        compiler_params=_params(("parallel",)),
    )(jnp.stack([chip, core]).astype(jnp.int32), own, got, got, got)


def _pair_gather(name, bufs):
    n = len(bufs)

    def body(*refs):
        ins, outs, send_sems, recv_sems = refs[:n], refs[n:2 * n], refs[2 * n], refs[2 * n + 1]
        x, y, c = _place()

        def copy(a, block):
            return pltpu.make_async_remote_copy(
                src_ref=ins[a].at[block], dst_ref=outs[a].at[block], send_sem=send_sems.at[a],
                recv_sem=recv_sems.at[a], device_id=(x, y, 1 - c), device_id_type=MESH)

        sends = [copy(a, c) for a in range(n)]
        for cp in sends:
            cp.start()
        for a in range(n):
            copy(a, 1 - c).wait_recv()
        for cp in sends:
            cp.wait_send()

    shapes = [jax.ShapeDtypeStruct(b.shape, b.dtype) for b in bufs]
    return _comm_call(name, body, bufs, shapes, n, {a: a for a in range(n)})


def _gather_all(name, s):
    def body(in_ref, out_ref, send_sems, recv_sems, local_sem):
        x, y, c = _place()
        me = 4 * x + 2 * y + c
        peers = []
        for mask in range(1, N_DEV):
            fx, fy, fc = (mask >> 2) & 1, (mask >> 1) & 1, mask & 1
            peers.append((jnp.where(fx, 1 - x, x), jnp.where(fy, 1 - y, y), jnp.where(fc, 1 - c, c)))

        def copy(j, block):
            return pltpu.make_async_remote_copy(
                src_ref=in_ref, dst_ref=out_ref.at[block], send_sem=send_sems.at[j], recv_sem=recv_sems.at[j],
                device_id=peers[j], device_id_type=MESH)

        local = pltpu.make_async_copy(in_ref, out_ref.at[me], local_sem)
        local.start()
        sends = [copy(j, me) for j in range(N_DEV - 1)]
        for cp in sends:
            cp.start()
        for j, (px, py, pc) in enumerate(peers):
            copy(j, 4 * px + 2 * py + pc).wait_recv()
        for cp in sends:
            cp.wait_send()
        local.wait()

    return pl.pallas_call(
        body, name=name, in_specs=[_ANY], out_specs=_ANY,
        out_shape=jax.ShapeDtypeStruct((N_DEV,) + s.shape, s.dtype),
        scratch_shapes=[pltpu.SemaphoreType.DMA((N_DEV - 1,)), pltpu.SemaphoreType.DMA((N_DEV - 1,)),
                        pltpu.SemaphoreType.DMA],
    )(s)


def _sum_blocks(name, stacked, tm):
    n, r, lanes = stacked.shape

    def body(in_ref, o_ref):
        acc = in_ref[0]
        for j in range(1, n):
            acc = acc + in_ref[j]
        o_ref[...] = acc

    return pl.pallas_call(
        body, name=name, grid=(r // tm,), in_specs=[pl.BlockSpec((n, tm, lanes), lambda i: (0, i, 0))],
        out_specs=pl.BlockSpec((tm, lanes), lambda i: (i, 0)), out_shape=jax.ShapeDtypeStruct((r, lanes), F32),
        compiler_params=_params(("parallel",)),
    )(stacked)


def _row_tile(rows, pref, align):
    best = None
    for t in range(align, min(rows, pref) + 1, align):
        if rows % t == 0:
            best = t
    assert best is not None, (rows, pref, align)
    return best


def _adam(name, w, g, m, v):
    rows, width = w.shape
    tm = _row_tile(rows, max(8, 4096 * LANES // width), 8)
    args = [(t, width, 0) for t in (w, g, m, v)]
    return _rowcall(name, _adam_fn, args, [], [(width, F32)] * 3, tm=tm)


def kernel(x, norm_mix, norm_mlp, norm_final, mlp_w1, mlp_w2, ab_w_in, ab_w_out, rg_conv_w, rg_conv_b, rg_w_a, rg_b_a, rg_w_x, rg_b_x, rg_lambda, hg_lb_logits, hg_norm, gla_w_in, gla_w_out, gla_w_gate_up, gla_b_gate, gla_norm, loss_target, m_norm_mix, m_norm_mlp, m_norm_final, m_mlp_w1, m_mlp_w2, m_ab_w_in, m_ab_w_out, m_rg_conv_w, m_rg_conv_b, m_rg_w_a, m_rg_b_a, m_rg_w_x, m_rg_b_x, m_rg_lambda, m_hg_lb_logits, m_hg_norm, m_gla_w_in, m_gla_w_out, m_gla_w_gate_up, m_gla_b_gate, m_gla_norm, v_norm_mix, v_norm_mlp, v_norm_final, v_mlp_w1, v_mlp_w2, v_ab_w_in, v_ab_w_out, v_rg_conv_w, v_rg_conv_b, v_rg_w_a, v_rg_b_a, v_rg_w_x, v_rg_b_x, v_rg_lambda, v_hg_lb_logits, v_hg_norm, v_gla_w_in, v_gla_w_out, v_gla_w_gate_up, v_gla_b_gate, v_gla_norm):
    w = dict(norm_mix=norm_mix, norm_mlp=norm_mlp, norm_final=norm_final, mlp_w1=mlp_w1, mlp_w2=mlp_w2, ab_w_in=ab_w_in, ab_w_out=ab_w_out, rg_conv_w=rg_conv_w, rg_conv_b=rg_conv_b, rg_w_a=rg_w_a, rg_b_a=rg_b_a, rg_w_x=rg_w_x, rg_b_x=rg_b_x, rg_lambda=rg_lambda, hg_lb_logits=hg_lb_logits, hg_norm=hg_norm, gla_w_in=gla_w_in, gla_w_out=gla_w_out, gla_w_gate_up=gla_w_gate_up, gla_b_gate=gla_b_gate, gla_norm=gla_norm)
    m = dict(norm_mix=m_norm_mix, norm_mlp=m_norm_mlp, norm_final=m_norm_final, mlp_w1=m_mlp_w1, mlp_w2=m_mlp_w2, ab_w_in=m_ab_w_in, ab_w_out=m_ab_w_out, rg_conv_w=m_rg_conv_w, rg_conv_b=m_rg_conv_b, rg_w_a=m_rg_w_a, rg_b_a=m_rg_b_a, rg_w_x=m_rg_w_x, rg_b_x=m_rg_b_x, rg_lambda=m_rg_lambda, hg_lb_logits=m_hg_lb_logits, hg_norm=m_hg_norm, gla_w_in=m_gla_w_in, gla_w_out=m_gla_w_out, gla_w_gate_up=m_gla_w_gate_up, gla_b_gate=m_gla_b_gate, gla_norm=m_gla_norm)
    v = dict(norm_mix=v_norm_mix, norm_mlp=v_norm_mlp, norm_final=v_norm_final, mlp_w1=v_mlp_w1, mlp_w2=v_mlp_w2, ab_w_in=v_ab_w_in, ab_w_out=v_ab_w_out, rg_conv_w=v_rg_conv_w, rg_conv_b=v_rg_conv_b, rg_w_a=v_rg_w_a, rg_b_a=v_rg_b_a, rg_w_x=v_rg_w_x, rg_b_x=v_rg_b_x, rg_lambda=v_rg_lambda, hg_lb_logits=v_hg_lb_logits, hg_norm=v_hg_norm, gla_w_in=v_gla_w_in, gla_w_out=v_gla_w_out, gla_w_gate_up=v_gla_w_gate_up, gla_b_gate=v_gla_b_gate, gla_norm=v_gla_norm)
    chip = 2 * lax.axis_index("x") + lax.axis_index("y")
    core = lax.axis_index("c")
    sharded_shapes = [w[n].shape for n in SMALL_SHARDED]

    slots = [_into_slot(f"cast_{n}{layer}", w[n], chip, N_CHIPS, BF16, 512, layer) for n, layer in MATRICES]
    early = [i for i, (n, _) in enumerate(MATRICES) if n in EARLY_MATRICES]
    rest = [i for i in range(len(MATRICES)) if i not in early]

    def named(indices, arrays):
        big = {}
        for i, t in zip(indices, arrays):
            big.setdefault(MATRICES[i][0], []).append(t)
        return {n: (v if n in ("mlp_w1", "mlp_w2") else v[0]) for n, v in big.items()}

    gathered = _gather_chips("gather_early", [slots[i] for i in early])
    send_sems, recv_sems, *in_flight, token = _gather_start("gather_rest_start", [slots[i] for i in rest], gathered[0])

    def late_weights(after):
        landed = _gather_wait("gather_rest_wait", in_flight, send_sems, recv_sems, after)
        return _prepare_matrices(named(rest, _hand_over("gather_rest_share", list(landed))))

    big = named(early, gathered)
    vectors = _pack([w[n] for n in SMALL_SHARDED])
    vectors = _into_slot("place_vectors", vectors, chip, N_CHIPS, F32, vectors.shape[0])
    small_all = _unpack(_gather_chips("gather_vectors", [vectors])[0], sharded_shapes, lead=1)
    full = {n: w[n] for n in SMALL_REPLICATED}
    for n, t in zip(SMALL_SHARDED, small_all):
        full[n] = _join_chips(t, t.ndim - 2)

    def pair_sums(tag, arrays):
        halves = [t.reshape(N_CHIPS, 2, t.shape[1] // 2, t.shape[2]) for t in arrays]
        from_sibling = _pair_exchange(f"reduce_pair_{tag}", halves)
        return [_pair_add(f"reduce_pair_add_{tag}{i}", h, s, core) for i, (h, s) in enumerate(zip(halves, from_sibling))]

    in_flight_grads = {}

    def emit(tag, arrays):
        parts = pair_sums(tag, arrays)
        p16 = [p for _, p in parts]
        send, recv, *rest = _scatter_start(f"reduce_chips_{tag}_start", p16, [lax.empty(p.shape, p.dtype) for p in p16])
        in_flight_grads[tag] = ([p for p, _ in parts], rest[:len(p16)], rest[len(p16):-1], send, recv)
        return rest[-1]

    loss_part, grad_x, g_kernel = _local_step(
        x[0], loss_target[0], _prepare_weights(big, full), token, late_weights, emit)
    g_big, g_full = _finish_grads(g_kernel)
    loss = lax.psum(loss_part[0, 0], ("x", "y", "c"))

    mine = {}
    last = pair_sums("ab", [g_big["ab_w_in"], g_big["ab_w_out"]])
    from_chips = _chip_scatter("reduce_chips_ab", [p for _, p in last])
    mine["ab"] = [_sum_ring(f"reduce_chips_add_ab{i}", p32, f, chip, core)
                  for i, ((p32, _), f) in enumerate(zip(last, from_chips))]
    for tag, (p32s, p16s, lands, send, recv) in in_flight_grads.items():
        landed = _scatter_wait(f"reduce_chips_{tag}_wait", p16s, lands, send, recv, mine["ab"][0])
        mine[tag] = [_sum_ring(f"reduce_chips_add_{tag}{i}", p32, f, chip, core)
                     for i, (p32, f) in enumerate(zip(p32s, landed))]
    ordered = [mine["mlp0"][0], mine["mlp1"][0], mine["mlp0"][1], mine["mlp1"][1], *mine["ab"], *mine["gla"]]
    reduced = [t.reshape(2 * t.shape[1], t.shape[2]) for t in _pair_gather("reduce_share", ordered)]
    by_name = {n: [] for n, _ in MATRICES}
    for (n, _), t in zip(MATRICES, reduced):
        by_name[n].append(t)
    grads = {n: jnp.stack(v) for n, v in by_name.items()}

    small_names = SMALL_REPLICATED + SMALL_SHARDED
    g_small = _pack([g_full[n] for n in small_names])
    g_small_all = _gather_all("reduce_small", g_small)
    g_small_red = _sum_blocks("reduce_small_add", g_small_all, g_small.shape[0])
    g_small_full = dict(zip(small_names, _unpack(g_small_red, [g_full[n].shape for n in small_names])))
    for n in SMALL_REPLICATED:
        grads[n] = g_small_full[n]
    for n in SMALL_SHARDED:
        width = w[n].shape[-1]
        grads[n] = lax.dynamic_slice_in_dim(g_small_full[n], chip * width, width, axis=g_small_full[n].ndim - 1)

    delta, new_m, new_v = {}, {}, {}
    for n in by_name:
        flat = [t.reshape(-1, t.shape[-1]) for t in (w[n], grads[n], m[n], v[n])]
        for dst, t in zip((delta, new_m, new_v), _adam(f"adam_{n}", *flat)):
            dst[n] = t.reshape(w[n].shape)
    small_shapes = [w[n].shape for n in small_names]
    packs = [_pack([src[n] for n in small_names]) for src in (w, grads, m, v)]
    d_small, m_small, v_small = _adam("adam_small", *packs)
    for dst, buf in ((delta, d_small), (new_m, m_small), (new_v, v_small)):
        dst.update(zip(small_names, _unpack(buf, small_shapes)))

    return (loss, grad_x[None], *[grads[n] for n in WEIGHTS], *[delta[n] for n in WEIGHTS],
            *[new_m[n] for n in WEIGHTS], *[new_v[n] for n in WEIGHTS])
```

```python
import functools

import jax
import jax.numpy as jnp
from jax import lax
from jax.experimental import pallas as pl
from jax.experimental.pallas import tpu as pltpu

F32 = jnp.float32
BF16 = jnp.bfloat16
MESH = pl.DeviceIdType.MESH

LANES = 128
CHUNK = 64
EPS = 1e-6
RG_C = 8.0
N_CHIPS = 4
N_DEV = 8
GLA_IN_WIDTH = 3104
GLA_IN_PAD = 3200
VMEM_LIMIT = 56 * 1024 * 1024

ADAM_LR = 0.001
ADAM_B1 = 0.9
ADAM_B2 = 0.999
ADAM_EPS = 1e-08
ADAM_WD = 0.01
ADAM_STEP = 10


def _raw_dot(a, b, ca, cb):
    return lax.dot_general(a.astype(BF16), b.astype(BF16), (((ca,), (cb,)), ((), ())),
                           preferred_element_type=F32)


def _raw_nn(a, b):
    return _raw_dot(a, b, 1, 0)


def _raw_nt(a, b):
    return _raw_dot(a, b, 1, 1)


def _raw_tn(a, b):
    return _raw_dot(a, b, 0, 0)


@jax.custom_vjp
def _dot_nn(a, b):
    return _raw_nn(a, b)


def _dot_nn_fwd(a, b):
    return _raw_nn(a, b), (a, b)


def _dot_nn_bwd(res, g):
    a, b = res
    return _raw_nt(g, b), _raw_tn(a, g)


_dot_nn.defvjp(_dot_nn_fwd, _dot_nn_bwd)


@jax.custom_vjp
def _dot_nt(a, b):
    return _raw_nt(a, b)


def _dot_nt_fwd(a, b):
    return _raw_nt(a, b), (a, b)


def _dot_nt_bwd(res, g):
    a, b = res
    return _raw_nn(g, b), _raw_tn(g, a)


_dot_nt.defvjp(_dot_nt_fwd, _dot_nt_bwd)


@jax.custom_vjp
def _dot_tn(a, b):
    return _raw_tn(a, b)


def _dot_tn_fwd(a, b):
    return _raw_tn(a, b), (a, b)


def _dot_tn_bwd(res, g):
    a, b = res
    return _raw_nt(b, g), _raw_nn(a, g)


_dot_tn.defvjp(_dot_tn_fwd, _dot_tn_bwd)


def _tile(n, pref):
    if n <= pref:
        return n
    t = (pref // LANES) * LANES
    while t > LANES and n % t:
        t -= LANES
    assert n % t == 0, (n, pref)
    return t


def _params(sem):
    return pltpu.CompilerParams(dimension_semantics=sem, vmem_limit_bytes=VMEM_LIMIT)


def _rowcall(name, fn, rows, pars, row_outs, par_outs=(), tm=256, pin=None):
    if pin is not None:
        inner, pars = fn, list(pars) + [pin]
        fn = lambda *vals: inner(*vals[:-1])
    n_rows = rows[0][0].shape[0]
    tm = min(tm, n_rows)
    assert n_rows % tm == 0
    n_r, n_p, n_ro = len(rows), len(pars), len(row_outs)

    def body(*refs):
        vals = [r[...].astype(F32) for r in refs[:n_r + n_p]]
        outs = fn(*vals)
        o_refs = refs[n_r + n_p:n_r + n_p + n_ro]
        po_refs = refs[n_r + n_p + n_ro:]
        for o_ref, val in zip(o_refs, outs[:n_ro]):
            o_ref[...] = val.astype(o_ref.dtype)
        first = pl.program_id(0) == 0
        for po_ref, val in zip(po_refs, outs[n_ro:]):
            @pl.when(first)
            def _():
                po_ref[...] = val

            @pl.when(jnp.logical_not(first))
            def _():
                po_ref[...] += val

    def const_map(nd):
        return lambda i: (0,) * nd

    def row_spec(w, cb):
        return pl.BlockSpec((tm, w), lambda i: (i, cb))

    in_specs = [row_spec(w, cb) for _, w, cb in rows]
    in_specs += [pl.BlockSpec(p.shape, const_map(p.ndim)) for p in pars]
    out_specs = [pl.BlockSpec((tm, w), lambda i: (i, 0)) for w, _ in row_outs]
    out_specs += [pl.BlockSpec(tuple(s), const_map(len(s))) for s in par_outs]
    out_shape = [jax.ShapeDtypeStruct((n_rows, w), dt) for w, dt in row_outs]
    out_shape += [jax.ShapeDtypeStruct(tuple(s), F32) for s in par_outs]
    return pl.pallas_call(
        body, name=name, grid=(n_rows // tm,), in_specs=in_specs, out_specs=out_specs, out_shape=out_shape,
        compiler_params=_params(("arbitrary",) if par_outs else ("parallel",)),
    )(*[r[0] for r in rows], *pars)


def _vjp_of(fn, n_prim, n_out, n_par, n_pass=0):
    def bwd(*args):
        prim = args[:n_prim]
        cts = args[n_prim:n_prim + n_out]
        passes = args[n_prim + n_out:n_prim + n_out + 2 * n_pass]
        pars = args[n_prim + n_out + 2 * n_pass:]
        _, vjp = jax.vjp(fn, *prim, *pars)
        grads = vjp(tuple(cts))
        sums = tuple(passes[2 * i] + passes[2 * i + 1] for i in range(n_pass))
        return tuple(grads[:n_prim]) + sums + tuple(grads[n_prim:])
    return bwd


def _mm(name, a, b, mode="nn", extras=(), epi=None, out_dtypes=(F32,), a_pro=None, out_split=None,
        tm=1024, tn=1024, tk=1024):
    split = b.shape[0] if b.ndim == 3 else None
    b_rows, b_cols = b.shape[-2:]
    if mode == "nn":
        (m, k), n = a.shape, b_cols * (split or 1)
    elif mode == "nt":
        (m, k), n = a.shape, b_rows
        assert k == b_cols * (split or 1)
    else:
        assert split is None
        (k, m), n = a.shape, b_cols
    tm, tk = _tile(m, tm), _tile(k, tk)
    tn = _tile(n // out_split, tn) if out_split else _tile(n, tn)
    if split and mode == "nn":
        tn = _tile(b_cols, tn)
    if split and mode == "nt":
        tk = _tile(b_cols, tk)
    nk = k // tk
    raw = {"nn": _raw_nn, "nt": _raw_nt, "tn": _raw_tn}[mode]
    n_e, n_o = len(extras), len(out_dtypes)
    if epi is None:
        epi = lambda acc: (acc,)

    def body(a_ref, b_ref, *rest):
        e_refs, o_refs, acc = rest[:n_e], rest[n_e:n_e + n_o], rest[-1]
        kk = pl.program_id(2)

        @pl.when(kk == 0)
        def _():
            acc[...] = jnp.zeros_like(acc)

        a_tile = a_ref[...] if a_pro is None else a_pro(a_ref[...].astype(F32))
        acc[...] += raw(a_tile, b_ref[...])

        @pl.when(kk == nk - 1)
        def _():
            res = epi(acc[...], *[e[...].astype(F32) for e in e_refs])
            for o_ref, r in zip(o_refs, res):
                o_ref[...] = r.astype(o_ref.dtype)

    a_spec = pl.BlockSpec((tk, tm), lambda i, j, kk: (kk, i)) if mode == "tn" else pl.BlockSpec((tm, tk), lambda i, j, kk: (i, kk))
    if split and mode == "nn":
        per = b_cols // tn
        b_spec = pl.BlockSpec((None, tk, tn), lambda i, j, kk: (j // per, kk, j % per))
    elif split:
        per = b_cols // tk
        b_spec = pl.BlockSpec((None, tn, tk), lambda i, j, kk: (kk // per, j, kk % per))
    elif mode == "nt":
        b_spec = pl.BlockSpec((tn, tk), lambda i, j, kk: (j, kk))
    else:
        b_spec = pl.BlockSpec((tk, tn), lambda i, j, kk: (kk, j))
    mn_spec = pl.BlockSpec((tm, tn), lambda i, j, kk: (i, j))
    if out_split:
        assert not extras
        per_out = n // out_split // tn
        out_spec = pl.BlockSpec((None, tm, tn), lambda i, j, kk: (j // per_out, i, j % per_out))
        out_shapes = [jax.ShapeDtypeStruct((out_split, m, n // out_split), dt) for dt in out_dtypes]
    else:
        out_spec = mn_spec
        out_shapes = [jax.ShapeDtypeStruct((m, n), dt) for dt in out_dtypes]
    outs = pl.pallas_call(
        body, name=name, grid=(m // tm, n // tn, nk),
        in_specs=[a_spec, b_spec] + [mn_spec] * n_e, out_specs=[out_spec] * n_o,
        out_shape=out_shapes,
        scratch_shapes=[pltpu.VMEM((tm, tn), F32)],
        compiler_params=_params(("parallel", "parallel", "arbitrary")),
    )(a, b, *extras)
    return outs[0] if n_o == 1 else outs


def _sigmoid(x):
    return jax.nn.sigmoid(x)


def _silu(x):
    return x * _sigmoid(x)


def _softplus(x):
    return jnp.maximum(x, 0.0) + jnp.log1p(jnp.exp(-jnp.abs(x)))


def _rmsnorm_fn(x, gain):
    return (x * lax.rsqrt(jnp.mean(x * x, axis=-1, keepdims=True) + EPS) * gain,)


def _head_norm(o, gain, n_heads):
    w = o.shape[-1] // n_heads
    parts = []
    for h in range(n_heads):
        oh = o[:, h * w:(h + 1) * w]
        parts.append(oh * lax.rsqrt(jnp.mean(oh * oh, axis=-1, keepdims=True) + EPS))
    return jnp.concatenate(parts, axis=-1) * gain


@jax.custom_jvp
def _neg_expm1(x):
    u = jnp.exp(x)
    is_one = u == 1.0
    return jnp.where(is_one, -x, (1.0 - u) * x / jnp.log(jnp.where(is_one, 2.0, u)))


@_neg_expm1.defjvp
def _neg_expm1_jvp(primals, tangents):
    (x,), (t,) = primals, tangents
    return _neg_expm1(x), -jnp.exp(x) * t


def _rg_gates_fn(xc, wa, wx, ba, bx, lam):
    outs = []
    for d in range(2):
        r = _sigmoid(_dot_nn(xc, wa[d]) + ba[d:d + 1])
        i = _sigmoid(_dot_nn(xc, wx[d]) + bx[d:d + 1])
        log_a = -RG_C * r * _softplus(-lam[d:d + 1])
        outs.append(jnp.exp(log_a))
        outs.append(jnp.sqrt(_neg_expm1(2.0 * log_a)) * (i * xc))
    return tuple(outs)


def _hg_pre_fn(q, f_f, f_b, logits):
    mx = jnp.maximum(logits[0:1], logits[1:2])
    e0 = jnp.exp(logits[0:1] - mx)
    e1 = jnp.exp(logits[1:2] - mx)
    lb = e0 / (e0 + e1)
    outs = [_silu(q)]
    for f in (f_f, f_b):
        outs.append((1.0 - lb) * _sigmoid(-f))
        outs.append(jnp.log(lb + (1.0 - lb) * _sigmoid(f)))
    return tuple(outs)


def _post0_fn(hs, ga, o, g, gain):
    ya = hs * jax.nn.gelu(ga, approximate=True)
    yb = _head_norm(o, gain, 4) * _silu(g)
    return (jnp.concatenate([ya, yb], axis=-1),)


def _post0_fwd_fn(h_f, h_b, ga, o_f, o_b, g, gain):
    return _post0_fn(h_f + h_b, ga, o_f + o_b, g, gain)


def _post0_bwd_fn(h_f, h_b, ga, o_f, o_b, g, dmix, gain):
    _, vjp = jax.vjp(_post0_fn, h_f + h_b, ga, o_f + o_b, g, gain)
    return vjp((dmix,))


def _gla_pre_fn(q, lr, w_up, b_gate):
    outs = [q * (128.0 ** -0.5)]
    for d in range(2):
        z = _dot_nn(lr, w_up[d]) + b_gate[d:d + 1]
        outs.append(-_softplus(-z) * (1.0 / 16.0))
    return tuple(outs)


def _gla_post_fn(o, r, gain):
    return (_head_norm(o, gain, 4) * _silu(r),)


def _gla_post_fwd_fn(o_f, o_b, r, gain):
    return _gla_post_fn(o_f + o_b, r, gain)


def _gla_post_bwd_fn(o_f, o_b, r, dmix, gain):
    _, vjp = jax.vjp(_gla_post_fn, o_f + o_b, r, gain)
    return vjp((dmix,))


def _relu2_bwd_epi(acc, hid):
    return (acc * 2.0 * jnp.maximum(hid, 0.0),)


def _relu2(x):
    r = jnp.maximum(x, 0.0)
    return r * r


def _add_epi(acc, res):
    return (acc + res,)


def _loss_head_fn(h, target, gain):
    def f(h, gain):
        y = _rmsnorm_fn(h, gain)[0]
        err = y - target
        return 0.5 * jnp.sum(jnp.mean(err * err, axis=-1, keepdims=True))
    loss, (dh, dgain) = jax.value_and_grad(f, argnums=(0, 1))(h, gain)
    return dh, jnp.full((1, LANES), loss, F32), dgain


def _adam_fn(w, g, m, v):
    m2 = ADAM_B1 * m + (1.0 - ADAM_B1) * g
    v2 = ADAM_B2 * v + (1.0 - ADAM_B2) * (g * g)
    m_hat = m2 / (1.0 - ADAM_B1 ** ADAM_STEP)
    v_hat = v2 / (1.0 - ADAM_B2 ** ADAM_STEP)
    delta = -ADAM_LR * (m_hat / (jnp.sqrt(v_hat) + ADAM_EPS) + ADAM_WD * w)
    return delta, m2, v2


def _shifted(x, t_idx, off):
    n = x.shape[0]
    rolled = pltpu.roll(x, (-off) % n, 0)
    valid = (t_idx + off >= 0) & (t_idx + off < n)
    return jnp.where(valid, rolled, 0.0)


def _conv_fwd(name, src, colblock, w, b):
    n_rows, width = src.shape[0], w.shape[1]

    def body(x_ref, w_ref, b_ref, o_ref):
        x = x_ref[...]
        t_idx = lax.broadcasted_iota(jnp.int32, x.shape, 0)
        acc = b_ref[...] + w_ref[2:3, :] * x
        acc += w_ref[0:1, :] * _shifted(x, t_idx, -2)
        acc += w_ref[1:2, :] * _shifted(x, t_idx, -1)
        acc += w_ref[3:4, :] * _shifted(x, t_idx, 1)
        o_ref[...] = acc

    nb = width // LANES
    return pl.pallas_call(
        body, name=name, grid=(nb,),
        in_specs=[pl.BlockSpec((n_rows, LANES), lambda j: (0, colblock * nb + j)),
                  pl.BlockSpec((4, LANES), lambda j: (0, j)), pl.BlockSpec((1, LANES), lambda j: (0, j))],
        out_specs=pl.BlockSpec((n_rows, LANES), lambda j: (0, j)),
        out_shape=jax.ShapeDtypeStruct((n_rows, width), F32),
        compiler_params=_params(("parallel",)),
    )(src, w, b)


def _conv_bwd(name, src, colblock, w, d):
    n_rows, width = src.shape[0], w.shape[1]

    def body(x_ref, w_ref, d_ref, dx_ref, dw_ref, db_ref):
        x = x_ref[...]
        g = d_ref[...]
        t_idx = lax.broadcasted_iota(jnp.int32, x.shape, 0)
        dx = w_ref[2:3, :] * g
        dx += w_ref[0:1, :] * _shifted(g, t_idx, 2)
        dx += w_ref[1:2, :] * _shifted(g, t_idx, 1)
        dx += w_ref[3:4, :] * _shifted(g, t_idx, -1)
        dx_ref[...] = dx.astype(dx_ref.dtype)
        dw_ref[0:1, :] = jnp.sum(g * _shifted(x, t_idx, -2), axis=0, keepdims=True)
        dw_ref[1:2, :] = jnp.sum(g * _shifted(x, t_idx, -1), axis=0, keepdims=True)
        dw_ref[2:3, :] = jnp.sum(g * x, axis=0, keepdims=True)
        dw_ref[3:4, :] = jnp.sum(g * _shifted(x, t_idx, 1), axis=0, keepdims=True)
        db_ref[...] = jnp.sum(g, axis=0, keepdims=True)

    nb = width // LANES
    return pl.pallas_call(
        body, name=name, grid=(nb,),
        in_specs=[pl.BlockSpec((n_rows, LANES), lambda j: (0, colblock * nb + j)),
                  pl.BlockSpec((4, LANES), lambda j: (0, j)),
                  pl.BlockSpec((n_rows, LANES), lambda j: (0, j))],
        out_specs=[pl.BlockSpec((n_rows, LANES), lambda j: (0, j)), pl.BlockSpec((4, LANES), lambda j: (0, j)),
                   pl.BlockSpec((1, LANES), lambda j: (0, j))],
        out_shape=[jax.ShapeDtypeStruct((n_rows, width), BF16), jax.ShapeDtypeStruct((4, width), F32),
                   jax.ShapeDtypeStruct((1, width), F32)],
        compiler_params=_params(("parallel",)),
    )(src, w, d)


_WHOLE = pl.BlockSpec(memory_space=pltpu.VMEM)
SCAN_UNROLL = 8


def _scan_fwd(name, a, u, reverse):
    n_rows, width = a.shape

    def body(a_ref, u_ref, h_ref):
        def step(i, h):
            t = (n_rows - 1 - i) if reverse else i
            h = a_ref[pl.ds(t, 1), :] * h + u_ref[pl.ds(t, 1), :]
            h_ref[pl.ds(t, 1), :] = h
            return h
        lax.fori_loop(0, n_rows, step, jnp.zeros((1, width), F32), unroll=SCAN_UNROLL)

    return pl.pallas_call(
        body, name=name, in_specs=[_WHOLE, _WHOLE], out_specs=_WHOLE,
        out_shape=jax.ShapeDtypeStruct((n_rows, width), F32),
        compiler_params=pltpu.CompilerParams(vmem_limit_bytes=VMEM_LIMIT),
    )(a, u)


def _scan_bwd(name, a, h, dh, reverse):
    n_rows, width = a.shape

    def body(a_ref, h_ref, dh_ref, du_ref, da_ref):
        def step(i, carry):
            t = i if reverse else (n_rows - 1 - i)
            g = dh_ref[pl.ds(t, 1), :] + carry
            du_ref[pl.ds(t, 1), :] = g
            tp = t + 1 if reverse else t - 1
            valid = (tp >= 0) & (tp < n_rows)
            h_prev = h_ref[pl.ds(jnp.clip(tp, 0, n_rows - 1), 1), :]
            da_ref[pl.ds(t, 1), :] = jnp.where(valid, g * h_prev, 0.0)
            return a_ref[pl.ds(t, 1), :] * g
        lax.fori_loop(0, n_rows, step, jnp.zeros((1, width), F32), unroll=SCAN_UNROLL)

    return pl.pallas_call(
        body, name=name, in_specs=[_WHOLE] * 3, out_specs=[_WHOLE] * 2,
        out_shape=[jax.ShapeDtypeStruct((n_rows, width), F32)] * 2,
        compiler_params=pltpu.CompilerParams(vmem_limit_bytes=VMEM_LIMIT),
    )(a, h, dh)


def _tri_mask(c, reverse):
    row = lax.broadcasted_iota(jnp.int32, (c, c), 0)
    col = lax.broadcasted_iota(jnp.int32, (c, c), 1)
    return (col >= row) if reverse else (col <= row)


def _cumsum_rows(x, reverse):
    tri = _tri_mask(x.shape[0], reverse).astype(BF16)
    hi = x.astype(BF16)
    rest = x - hi.astype(F32)
    mid = rest.astype(BF16)
    lo = (rest - mid.astype(F32)).astype(BF16)
    return _raw_nn(tri, hi) + _raw_nn(tri, mid) + _raw_nn(tri, lo)


@functools.partial(jax.custom_vjp, nondiff_argnums=(1,))
def _cumsum(x, reverse):
    return _cumsum_rows(x, reverse)


def _cumsum_fwd(x, reverse):
    return _cumsum_rows(x, reverse), None


def _cumsum_bwd(reverse, _, g):
    return (_cumsum_rows(g, not reverse),)


_cumsum.defvjp(_cumsum_fwd, _cumsum_bwd)


def _chunks_fn(qs, ks, vs, lfs, sts, reverses):
    n, c = len(qs), qs[0].shape[0]
    every = range(n)
    tris = [_tri_mask(c, r) for r in reverses]
    cums = [_cumsum(lfs[i], reverses[i]) for i in every]
    rid = lax.broadcasted_iota(jnp.int32, cums[0].shape, 0)

    def pick(cum, r):
        return jnp.sum(jnp.where(rid == r, cum, 0.0), axis=0, keepdims=True)

    refs = [pick(cums[i], (c - 1 - c // 2) if reverses[i] else c // 2) for i in every]
    lasts = [pick(cums[i], 0 if reverses[i] else c - 1) for i in every]
    q_in = [qs[i] * jnp.exp(cums[i] - refs[i]) for i in every]
    k_in = [ks[i] * jnp.exp(refs[i] - cums[i]) for i in every]
    scores = [jnp.where(tris[i], _dot_nt(q_in[i], k_in[i]), 0.0) for i in every]
    o_intra = [_dot_nn(scores[i], vs[i]) for i in every]
    q_out = [qs[i] * jnp.exp(cums[i]) for i in every]
    o_inter = [_dot_nt(q_out[i], sts[i]) for i in every]
    k_state = [ks[i] * jnp.exp(lasts[i] - cums[i]) for i in every]
    upd = [_dot_tn(vs[i], k_state[i]) for i in every]
    st_new = [sts[i] * jnp.exp(lasts[i]) + upd[i] for i in every]
    return [o_intra[i] + o_inter[i] for i in every], st_new


def _attn_fwd(name, q, k_f, k_b, v, lf_f, lf_b, n_heads, dk, dv):
    n_rows = q[0].shape[0]
    n_chunks = n_rows // CHUNK
    wk, wv = n_heads * dk, n_heads * dv

    def spec(width, off, rev):
        return pl.BlockSpec((CHUNK, width), lambda n: ((n_chunks - 1 - n) if rev else n, off))

    def sspec(rev):
        return pl.BlockSpec((None, n_heads, dv, dk), lambda n: ((n_chunks - 1 - n) if rev else n, 0, 0, 0))

    def body(qf, kf, vf, lff, qb, kb, vb, lfb, of_ref, ob_ref, sf_ref, sb_ref, st):
        @pl.when(pl.program_id(0) == 0)
        def _():
            st[...] = jnp.zeros_like(st)

        ins = ((qf, kf, vf, lff), (qb, kb, vb, lfb))
        chains = [(d, h) for d in range(2) for h in range(n_heads)]
        ck = [slice(h * dk, (h + 1) * dk) for h in range(n_heads)]
        cv = [slice(h * dv, (h + 1) * dv) for h in range(n_heads)]
        qs = [ins[d][0][:, ck[h]] for d, h in chains]
        ks = [ins[d][1][:, ck[h]] for d, h in chains]
        vs = [ins[d][2][:, cv[h]] for d, h in chains]
        lfs = [ins[d][3][:, ck[h]] for d, h in chains]
        sts = [st[d, h] for d, h in chains]
        os_, st_new = _chunks_fn(qs, ks, vs, lfs, sts, [d == 1 for d, _ in chains])
        for i, (d, h) in enumerate(chains):
            (sf_ref, sb_ref)[d][h] = sts[i].astype(BF16)
            (of_ref, ob_ref)[d][:, cv[h]] = os_[i]
            st[d, h] = st_new[i]

    in_specs = [spec(wk, q[1], False), spec(wk, k_f[1], False), spec(wv, v[1], False), spec(wk, lf_f[1], False),
                spec(wk, q[1], True), spec(wk, k_b[1], True), spec(wv, v[1], True), spec(wk, lf_b[1], True)]
    return pl.pallas_call(
        body, name=name, grid=(n_chunks,), in_specs=in_specs,
        out_specs=[spec(wv, 0, False), spec(wv, 0, True), sspec(False), sspec(True)],
        out_shape=[jax.ShapeDtypeStruct((n_rows, wv), F32)] * 2
        + [jax.ShapeDtypeStruct((n_chunks, n_heads, dv, dk), BF16)] * 2,
        scratch_shapes=[pltpu.VMEM((2, n_heads, dv, dk), F32)],
        compiler_params=_params(("arbitrary",)),
    )(q[0], k_f[0], v[0], lf_f[0], q[0], k_b[0], v[0], lf_b[0])


def _attn_bwd(name, q, k_f, k_b, v, lf_f, lf_b, st_f, st_b, do, n_heads, dk, dv, out_dtype=F32):
    n_rows = q[0].shape[0]
    n_chunks = n_rows // CHUNK
    wk, wv = n_heads * dk, n_heads * dv

    def spec(width, off, rev):
        return pl.BlockSpec((CHUNK, width), lambda n: (n if rev else (n_chunks - 1 - n), off))

    def sspec(rev):
        return pl.BlockSpec((None, n_heads, dv, dk), lambda n: (n if rev else (n_chunks - 1 - n), 0, 0, 0))

    def body(qf, kf, vf, lff, sf, dof, qb, kb, vb, lfb, sb, dob,
             dqf, dkf, dvf, dlff, dqb, dkb, dvb, dlfb, dst):
        @pl.when(pl.program_id(0) == 0)
        def _():
            dst[...] = jnp.zeros_like(dst)

        ins = ((qf, kf, vf, lff, sf, dof), (qb, kb, vb, lfb, sb, dob))
        outs = ((dqf, dkf, dvf, dlff), (dqb, dkb, dvb, dlfb))
        chains = [(d, h) for d in range(2) for h in range(n_heads)]
        ck = [slice(h * dk, (h + 1) * dk) for h in range(n_heads)]
        cv = [slice(h * dv, (h + 1) * dv) for h in range(n_heads)]
        qs = [ins[d][0][:, ck[h]] for d, h in chains]
        ks = [ins[d][1][:, ck[h]] for d, h in chains]
        vs = [ins[d][2][:, cv[h]] for d, h in chains]
        lfs = [ins[d][3][:, ck[h]] for d, h in chains]
        sts = [ins[d][4][h].astype(F32) for d, h in chains]
        dos = [ins[d][5][:, cv[h]] for d, h in chains]
        dsts = [dst[d, h] for d, h in chains]
        fn = functools.partial(_chunks_fn, reverses=[d == 1 for d, _ in chains])
        _, vjp = jax.vjp(fn, qs, ks, vs, lfs, sts)
        dqs, dks, dvs, dlfs, dst_prev = vjp((dos, dsts))
        for i, (d, h) in enumerate(chains):
            dq_r, dk_r, dv_r, dlf_r = outs[d]
            dq_r[:, ck[h]] = dqs[i].astype(dq_r.dtype)
            dk_r[:, ck[h]] = dks[i].astype(dk_r.dtype)
            dv_r[:, cv[h]] = dvs[i].astype(dv_r.dtype)
            dlf_r[:, ck[h]] = dlfs[i].astype(dlf_r.dtype)
            dst[d, h] = dst_prev[i]

    def dir_specs(kk, lf, rev):
        return [spec(wk, q[1], rev), spec(wk, kk[1], rev), spec(wv, v[1], rev), spec(wk, lf[1], rev), sspec(rev),
                spec(wv, 0, rev)]

    def dir_out_specs(rev):
        return [spec(wk, 0, rev), spec(wk, 0, rev), spec(wv, 0, rev), spec(wk, 0, rev)]

    shapes = [jax.ShapeDtypeStruct((n_rows, wk), out_dtype), jax.ShapeDtypeStruct((n_rows, wk), out_dtype),
              jax.ShapeDtypeStruct((n_rows, wv), out_dtype), jax.ShapeDtypeStruct((n_rows, wk), F32)]
    outs = pl.pallas_call(
        body, name=name, grid=(n_chunks,), in_specs=dir_specs(k_f, lf_f, False) + dir_specs(k_b, lf_b, True),
        out_specs=dir_out_specs(False) + dir_out_specs(True), out_shape=shapes + shapes,
        scratch_shapes=[pltpu.VMEM((2, n_heads, dv, dk), F32)],
        compiler_params=_params(("arbitrary",)),
    )(q[0], k_f[0], v[0], lf_f[0], st_f, do, q[0], k_b[0], v[0], lf_b[0], st_b, do)
    return outs[:4], outs[4:]


def _row2(v):
    return v.reshape(1, -1)


def _mlp_fwd(tag, h, gain, w1, w2):
    y = _rowcall(f"{tag}_norm", _rmsnorm_fn, [(h, h.shape[1], 0)], [gain], [(h.shape[1], BF16)], tm=512)[0]
    hid = _mm(f"{tag}_up", y, w1, out_dtypes=(BF16,))
    h_out = _mm(f"{tag}_down", hid, w2, a_pro=_relu2, extras=(h,), epi=_add_epi)
    return h_out, (y, hid)


def _mlp_bwd(tag, h, gain, w1, w2, saved, dh_out):
    y, hid = saved
    dhid = _mm(f"{tag}_dact", dh_out, w2, mode="nt", extras=(hid,), epi=_relu2_bwd_epi, out_dtypes=(BF16,))
    dw2 = _mm(f"{tag}_dw2", hid, dh_out, mode="tn", a_pro=_relu2)
    dw1 = _mm(f"{tag}_dw1", y, dhid, mode="tn", out_split=N_CHIPS)
    dy = _mm(f"{tag}_dy", dhid, w1, mode="nt")
    dh, dgain = _norm_bwd(f"{tag}_dnorm", h, gain, dy, dh_out)
    return dh, dgain, dw1, dw2


def _norm_bwd(name, h, gain, dy, dres, pin=None):
    d = h.shape[1]

    def fn(h, dy, dres, gain):
        _, vjp = jax.vjp(lambda a, b: _rmsnorm_fn(a, b)[0], h, gain)
        dh, dgain = vjp(dy)
        return dh + dres, dgain

    dh, dgain = _rowcall(name, fn, [(h, d, 0), (dy, d, 0), (dres, d, 0)], [gain], [(d, F32)], [(1, d)], tm=512, pin=pin)
    return dh, dgain


def _local_step(x, target, w, pin=None, late=None, emit=None):
    g = {}
    d_model = x.shape[1]
    rg_w = hg_w = d_model // 2
    pins = []

    def send_off(tag, arrays):
        if emit is not None:
            pins.append(emit(tag, arrays))

    def chip_major(t):
        return t.reshape(N_CHIPS, t.shape[0] // N_CHIPS, t.shape[1])

    h_a0 = x
    gain = _row2(w["norm_mix"][0])
    y0 = _rowcall("l0_norm", _rmsnorm_fn, [(h_a0, d_model, 0)], [gain], [(d_model, BF16)], tm=512, pin=pin)[0]
    proj0 = _mm("l0_in", y0, w["ab_w_in"])
    conv_w, conv_b = w["rg_conv_w"], _row2(w["rg_conv_b"])
    xc = _conv_fwd("rg_conv", proj0, 0, conv_w, conv_b)
    gate_pars = [w["rg_wa_bd"], w["rg_wx_bd"], w["rg_b_a"], w["rg_b_x"], w["rg_lambda"]]
    a_f, u_f, a_b, u_b = _rowcall("rg_gates", _rg_gates_fn, [(xc, rg_w, 0)], gate_pars, [(rg_w, F32)] * 4)
    hs_f = _scan_fwd("rg_scan_f", a_f, u_f, False)
    hs_b = _scan_fwd("rg_scan_b", a_b, u_b, True)
    hg_rows = [(proj0, hg_w, 2), (proj0, hg_w, 3), (proj0, hg_w, 4)]
    qh, k_f, lf_f, k_b, lf_b = _rowcall("hg_pre", _hg_pre_fn, hg_rows, [w["hg_lb_logits"]], [(hg_w, F32)] * 5)
    iv = (proj0, 5)
    o_f, o_b, st_f, st_b = _attn_fwd("hg_attn", (qh, 0), (k_f, 0), (k_b, 0), iv, (lf_f, 0), (lf_b, 0), 4, 128, 128)
    post0_rows = [(hs_f, rg_w, 0), (hs_b, rg_w, 0), (proj0, rg_w, 1), (o_f, hg_w, 0), (o_b, hg_w, 0), (proj0, hg_w, 6)]
    hg_gain = _row2(w["hg_norm"])
    mix_in0 = _rowcall("l0_post", _post0_fwd_fn, post0_rows, [hg_gain], [(d_model, BF16)])[0]
    h_b0 = _mm("l0_out", mix_in0, w["ab_w_out"], extras=(h_a0,), epi=_add_epi)
    if late is not None:
        w = {**w, **late(h_b0)}
    h_c0, mlp0 = _mlp_fwd("mlp0", h_b0, _row2(w["norm_mlp"][0]), w["mlp_w1"][0], w["mlp_w2"][0])

    h_a1 = h_c0
    gain1 = _row2(w["norm_mix"][1])
    y1 = _rowcall("l1_norm", _rmsnorm_fn, [(h_a1, d_model, 0)], [gain1], [(d_model, BF16)], tm=512)[0]
    proj1 = _mm("l1_in", y1, w["gla_w_in_pad"], tn=640)
    gla_pars = [w["gla_w_up_pad"], w["gla_b_gate"]]
    gq, glf_f, glf_b = _rowcall("gla_pre", _gla_pre_fn, [(proj1, 512, 0), (proj1, LANES, 24)], gla_pars, [(512, F32)] * 3)
    gk, gv = (proj1, 1), (proj1, 1)
    go_f, go_b, gst_f, gst_b = _attn_fwd("gla_attn", (gq, 0), gk, gk, gv, (glf_f, 0), (glf_b, 0), 4, 128, 256)
    gla_gain = _row2(w["gla_norm"])
    post1_rows = [(go_f, d_model, 0), (go_b, d_model, 0), (proj1, d_model, 2)]
    mix_in1 = _rowcall("l1_post", _gla_post_fwd_fn, post1_rows, [gla_gain], [(d_model, BF16)])[0]
    h_b1 = _mm("l1_out", mix_in1, w["gla_w_out"], extras=(h_a1,), epi=_add_epi)
    h_c1, mlp1 = _mlp_fwd("mlp1", h_b1, _row2(w["norm_mlp"][1]), w["mlp_w1"][1], w["mlp_w2"][1])

    dh, loss, g["norm_final"] = _rowcall(
        "loss_head", _loss_head_fn, [(h_c1, d_model, 0), (target, d_model, 0)], [_row2(w["norm_final"])],
        [(d_model, F32)], [(1, LANES), (1, d_model)], tm=512)

    dh, g_nmlp1, g_w1_1, g_w2_1 = _mlp_bwd("mlp1", h_b1, _row2(w["norm_mlp"][1]), w["mlp_w1"][1], w["mlp_w2"][1], mlp1, dh)
    send_off("mlp1", [g_w1_1, chip_major(g_w2_1)])
    dmix1 = _mm("l1_dout", dh, w["gla_w_out"], mode="nt")
    g["gla_w_out"] = _mm("l1_dwout", mix_in1, dh, mode="tn")
    dgo, dr, g["gla_norm"] = _rowcall(
        "l1_dpost", _gla_post_bwd_fn, post1_rows + [(dmix1, d_model, 0)], [gla_gain],
        [(d_model, F32), (d_model, BF16)], [(1, d_model)], pin=pins.pop() if pins else None)
    (dq_f, dk_f, dv_f, dlf_f), (dq_b, dk_b, dv_b, dlf_b) = _attn_bwd(
        "gla_dattn", (gq, 0), gk, gk, gv, (glf_f, 0), (glf_b, 0), gst_f, gst_b, dgo, 4, 128, 256)

    def gla_pre_bwd(q, lr, dq1, dq2, dlf1, dlf2, dk1, dk2, dv1, dv2, w_up, b_gate):
        dlr = jnp.zeros_like(lr)
        dws, dbs = [], []
        for d, dlf in enumerate((dlf1, dlf2)):
            z = _raw_nn(lr, w_up[d]) + b_gate[d:d + 1]
            dz = dlf * _sigmoid(-z) * (1.0 / 16.0)
            dlr = dlr + _raw_nt(dz, w_up[d])
            dws.append(_raw_tn(dz, lr))
            dbs.append(jnp.sum(dz, axis=0, keepdims=True))
        return ((dq1 + dq2) * (128.0 ** -0.5), dk1 + dk2, dv1 + dv2, dlr, dws[0], dws[1], dbs[0], dbs[1])

    rows = [(proj1, 512, 0), (proj1, LANES, 24), (dq_f, 512, 0), (dq_b, 512, 0), (dlf_f, 512, 0), (dlf_b, 512, 0),
            (dk_f, 512, 0), (dk_b, 512, 0), (dv_f, d_model, 0), (dv_b, d_model, 0)]
    dq, dk, dv, dlr, dwt_f, dwt_b, db_f, db_b = _rowcall(
        "gla_dpre", gla_pre_bwd, rows, gla_pars, [(512, BF16), (512, BF16), (d_model, BF16), (LANES, BF16)],
        [(512, LANES), (512, LANES), (1, 512), (1, 512)])
    g["gla_w_up_pad"] = jnp.stack([dwt_f.T, dwt_b.T])
    g["gla_b_gate"] = jnp.concatenate([db_f, db_b], axis=0)
    dproj1 = jnp.concatenate([dq, dk, dv, dr, dlr], axis=1)
    g_gla_in = _mm("l1_dwin", y1, dproj1, mode="tn", tn=640)
    g["gla_w_in"] = _split_chips(g_gla_in[:, :GLA_IN_WIDTH], 1)
    send_off("gla", [g["gla_w_in"], chip_major(g["gla_w_out"])])
    dy1 = _mm("l1_dy", dproj1, w["gla_w_in_pad"], mode="nt", tk=640)
    dh, g_nmix1 = _norm_bwd("l1_dnorm", h_a1, gain1, dy1, dh, pin=pins.pop() if pins else None)

    dh, g_nmlp0, g_w1_0, g_w2_0 = _mlp_bwd("mlp0", h_b0, _row2(w["norm_mlp"][0]), w["mlp_w1"][0], w["mlp_w2"][0], mlp0, dh)
    send_off("mlp0", [g_w1_0, chip_major(g_w2_0)])
    dmix0 = _mm("l0_dout", dh, w["ab_w_out"], mode="nt")
    g["ab_w_out"] = _mm("l0_dwout", mix_in0, dh, mode="tn")
    dhs, dga, do, dg, g["hg_norm"] = _rowcall(
        "l0_dpost", _post0_bwd_fn, post0_rows + [(dmix0, d_model, 0)], [hg_gain],
        [(rg_w, F32), (rg_w, BF16), (hg_w, F32), (hg_w, BF16)], [(1, hg_w)], pin=pins.pop() if pins else None)
    (dqh_f, dk_f, div_f, dlf_f), (dqh_b, dk_b, div_b, dlf_b) = _attn_bwd(
        "hg_dattn", (qh, 0), (k_f, 0), (k_b, 0), iv, (lf_f, 0), (lf_b, 0), st_f, st_b, do, 4, 128, 128)

    def hg_pre_bwd(q, f_f, f_b, dq1, dq2, dk1, dlf1, dk2, dlf2, dv1, dv2, logits):
        _, vjp = jax.vjp(_hg_pre_fn, q, f_f, f_b, logits)
        dq, df_f, df_b, dlogits = vjp((dq1 + dq2, dk1, dlf1, dk2, dlf2))
        return dq, df_f, df_b, dv1 + dv2, dlogits

    rows = hg_rows + [(t, hg_w, 0) for t in (dqh_f, dqh_b, dk_f, dlf_f, dk_b, dlf_b, div_f, div_b)]
    dq, df_f, df_b, div, g["hg_lb_logits"] = _rowcall(
        "hg_dpre", hg_pre_bwd, rows, [w["hg_lb_logits"]], [(hg_w, BF16)] * 4, [(2, hg_w)])
    du_f, da_f = _scan_bwd("rg_dscan_f", a_f, hs_f, dhs, False)
    du_b, da_b = _scan_bwd("rg_dscan_b", a_b, hs_b, dhs, True)
    gates_bwd = _vjp_of(_rg_gates_fn, 1, 4, 5)
    rows = [(xc, rg_w, 0), (da_f, rg_w, 0), (du_f, rg_w, 0), (da_b, rg_w, 0), (du_b, rg_w, 0)]
    dxc, g["rg_wa_bd"], g["rg_wx_bd"], g["rg_b_a"], g["rg_b_x"], g["rg_lambda"] = _rowcall(
        "rg_dgates", gates_bwd, rows, gate_pars, [(rg_w, F32)],
        [(2, rg_w, rg_w), (2, rg_w, rg_w), (2, rg_w), (2, rg_w), (2, rg_w)])
    dxa, g["rg_conv_w"], g["rg_conv_b"] = _conv_bwd("rg_dconv", proj0, 0, conv_w, dxc)
    dproj0 = jnp.concatenate([dxa, dga, dq, df_f, df_b, div, dg], axis=1)
    g["ab_w_in"] = _mm("l0_dwin", y0, dproj0, mode="tn", out_split=N_CHIPS)
    dy0 = _mm("l0_dy", dproj0, w["ab_w_in"], mode="nt")
    grad_x, g_nmix0 = _norm_bwd("l0_dnorm", h_a0, gain, dy0, dh)

    g["norm_mix"] = jnp.concatenate([g_nmix0, g_nmix1], axis=0)
    g["norm_mlp"] = jnp.concatenate([g_nmlp0, g_nmlp1], axis=0)
    g["mlp_w1"] = [g_w1_0, g_w1_1]
    g["mlp_w2"] = [g_w2_0, g_w2_1]
    return loss, grad_x, g


def _block_diag(w):
    d, g, n, _ = w.shape
    eye = jnp.eye(g, dtype=w.dtype)
    return (w[:, :, :, None, :] * eye[None, :, None, :, None]).reshape(d, g * n, g * n)


def _block_diag_extract(wbd, g):
    d, gn, _ = wbd.shape
    n = gn // g
    blocks = wbd.reshape(d, g, n, g, n)
    return jnp.stack([blocks[:, i, :, i, :] for i in range(g)], axis=1)


def _prepare_weights(big, full):
    w = {k: full[k] for k in ("norm_mix", "norm_mlp", "norm_final", "hg_lb_logits")}
    for k in ("rg_conv_w", "rg_conv_b", "rg_b_a", "rg_b_x", "rg_lambda", "hg_norm", "gla_b_gate", "gla_norm"):
        w[k] = full[k][0]
    w["rg_wa_bd"] = _block_diag(full["rg_w_a"][0])
    w["rg_wx_bd"] = _block_diag(full["rg_w_x"][0])
    up = full["gla_w_gate_up"][0]
    rank = up.shape[1]
    pad = jnp.zeros((2, LANES, up.shape[2]), F32)
    w["gla_w_up_pad"] = pad.at[0, 0:rank].set(up[0]).at[1, rank:2 * rank].set(up[1])
    w.update(_prepare_matrices(big))
    return w


def _prepare_matrices(big):
    w = {}
    if "mlp_w1" in big:
        w["mlp_w1"] = list(big["mlp_w1"])
        w["mlp_w2"] = [t.reshape(-1, t.shape[-1]) for t in big["mlp_w2"]]
    if "ab_w_in" in big:
        w["ab_w_in"] = big["ab_w_in"]
        w["ab_w_out"] = big["ab_w_out"].reshape(-1, big["ab_w_out"].shape[-1])
    if "gla_w_in" in big:
        w["gla_w_out"] = big["gla_w_out"].reshape(-1, big["gla_w_out"].shape[-1])
        gla_in = _join_chips(big["gla_w_in"], 1)
        w["gla_w_in_pad"] = jnp.pad(gla_in, ((0, 0), (0, GLA_IN_PAD - gla_in.shape[1])))
    return w


def _finish_grads(g, rank=16, rg_blocks=8):
    def chip_major(t):
        return t.reshape(N_CHIPS, t.shape[0] // N_CHIPS, t.shape[1])

    big = {
        "mlp_w1": list(g["mlp_w1"]), "mlp_w2": [chip_major(t) for t in g["mlp_w2"]],
        "ab_w_in": g["ab_w_in"], "ab_w_out": chip_major(g["ab_w_out"]),
        "gla_w_in": g["gla_w_in"], "gla_w_out": chip_major(g["gla_w_out"]),
    }
    small = {
        "norm_mix": g["norm_mix"], "norm_mlp": g["norm_mlp"], "norm_final": g["norm_final"][0],
        "rg_conv_w": g["rg_conv_w"][None], "rg_conv_b": g["rg_conv_b"],
        "rg_w_a": _block_diag_extract(g["rg_wa_bd"], rg_blocks)[None], "rg_b_a": g["rg_b_a"][None],
        "rg_w_x": _block_diag_extract(g["rg_wx_bd"], rg_blocks)[None], "rg_b_x": g["rg_b_x"][None],
        "rg_lambda": g["rg_lambda"][None], "hg_lb_logits": g["hg_lb_logits"], "hg_norm": g["hg_norm"],
        "gla_w_gate_up": jnp.stack([g["gla_w_up_pad"][0, 0:rank], g["gla_w_up_pad"][1, rank:2 * rank]])[None],
        "gla_b_gate": g["gla_b_gate"][None], "gla_norm": g["gla_norm"],
    }
    return big, small


MATRICES = (("mlp_w1", 0), ("mlp_w1", 1), ("mlp_w2", 0), ("mlp_w2", 1), ("ab_w_in", 0), ("ab_w_out", 0),
            ("gla_w_in", 0), ("gla_w_out", 0))
EARLY_MATRICES = ("ab_w_in", "ab_w_out")
SMALL_SHARDED = ("rg_conv_w", "rg_b_a", "rg_b_x", "rg_lambda", "gla_w_gate_up", "gla_b_gate", "gla_norm")
SMALL_REPLICATED = ("norm_mix", "norm_mlp", "norm_final", "rg_conv_b", "rg_w_a", "rg_w_x", "hg_lb_logits", "hg_norm")
WEIGHTS = ("norm_mix", "norm_mlp", "norm_final", "mlp_w1", "mlp_w2", "ab_w_in", "ab_w_out", "rg_conv_w", "rg_conv_b",
           "rg_w_a", "rg_b_a", "rg_w_x", "rg_b_x", "rg_lambda", "hg_lb_logits", "hg_norm", "gla_w_in", "gla_w_out",
           "gla_w_gate_up", "gla_b_gate", "gla_norm")
ROW_ALIGN = 16


def _pack(arrays, lead=0):
    head = arrays[0].shape[:lead]
    flat = jnp.concatenate([a.reshape(head + (-1,)) for a in arrays], axis=lead)
    n = flat.shape[-1]
    quantum = LANES * ROW_ALIGN
    padded = -(-n // quantum) * quantum
    if padded != n:
        flat = jnp.pad(flat, [(0, 0)] * lead + [(0, padded - n)])
    return flat.reshape(head + (padded // LANES, LANES))


def _unpack(buf, shapes, lead=0):
    head = buf.shape[:lead]
    flat = buf.reshape(head + (-1,))
    out, off = [], 0
    for s in shapes:
        n = 1
        for v in s:
            n *= v
        out.append(lax.slice_in_dim(flat, off, off + n, axis=lead).reshape(head + tuple(s)))
        off += n
    return out


def _join_chips(gathered, axis):
    t = jnp.moveaxis(gathered, 0, axis)
    return t.reshape(t.shape[:axis] + (t.shape[axis] * t.shape[axis + 1],) + t.shape[axis + 2:])


def _split_chips(full, axis):
    s = full.shape
    t = full.reshape(s[:axis] + (N_CHIPS, s[axis] // N_CHIPS) + s[axis + 1:])
    return jnp.moveaxis(t, axis, 0)


_ANY = pl.BlockSpec(memory_space=pl.ANY)


def _place():
    return lax.axis_index("x"), lax.axis_index("y"), lax.axis_index("c")


def _into_slot(name, src, slot, n_slots, dtype, tm, layer=None):
    r, lanes = src.shape[-2:]
    tm = _row_tile(r, tm, ROW_ALIGN)

    def body(slot_ref, in_ref, o_ref):
        o_ref[...] = in_ref[...].astype(o_ref.dtype)

    if layer is None:
        in_spec = pl.BlockSpec((tm, lanes), lambda i, slot_ref: (i, 0))
    else:
        in_spec = pl.BlockSpec((None, tm, lanes), lambda i, slot_ref: (layer, i, 0))
    grid_spec = pltpu.PrefetchScalarGridSpec(
        num_scalar_prefetch=1, grid=(r // tm,), in_specs=[in_spec],
        out_specs=pl.BlockSpec((None, tm, lanes), lambda i, slot_ref: (slot_ref[0], i, 0)))
    return pl.pallas_call(
        body, name=name, grid_spec=grid_spec, out_shape=jax.ShapeDtypeStruct((n_slots, r, lanes), dtype),
        compiler_params=_params(("parallel",)),
    )(slot.reshape(1).astype(jnp.int32), src)


def _chip_peers():
    x, y, c = _place()
    return 2 * x + y, c, [(1 - x, y), (x, 1 - y), (1 - x, 1 - y)]


def _comm_call(name, body, ins, out_shapes, n_sems, aliases=None):
    return pl.pallas_call(
        body, name=name, in_specs=[_ANY] * len(ins), out_specs=[_ANY] * len(out_shapes), out_shape=out_shapes,
        input_output_aliases=aliases or {},
        scratch_shapes=[pltpu.SemaphoreType.DMA((n_sems,)), pltpu.SemaphoreType.DMA((n_sems,))],
    )(*ins)


def _gather_chips(name, bufs):
    n = len(bufs)

    def body(*refs):
        outs, send_sems, recv_sems = refs[n:2 * n], refs[2 * n], refs[2 * n + 1]
        x, y, c = _place()
        me, _, peers = _chip_peers()

        def rows(a, block, half):
            rh = outs[a].shape[1] // 2
            return outs[a].at[block, pl.ds(half * rh, rh)]

        def copy(a, j, block, half, to, sem):
            return pltpu.make_async_remote_copy(
                src_ref=rows(a, block, half), dst_ref=rows(a, block, half), send_sem=send_sems.at[sem],
                recv_sem=recv_sems.at[sem], device_id=to, device_id_type=MESH)

        def over_ici(a, j, block):
            px, py = peers[j]
            return copy(a, j, block, c, (px, py, c), 6 * a + j)

        def to_sibling(a, j, block, half):
            return copy(a, j, block, half, (x, y, 1 - c), 6 * a + 3 + j)

        sends = [over_ici(a, j, me) for a in range(n) for j in range(3)]
        for cp in sends:
            cp.start()
        for a in range(n):
            for j, (px, py) in enumerate(peers):
                over_ici(a, j, 2 * px + py).wait_recv()
                handed = to_sibling(a, j, 2 * px + py, c)
                handed.start()
                sends.append(handed)
        for a in range(n):
            for j, (px, py) in enumerate(peers):
                to_sibling(a, j, 2 * px + py, 1 - c).wait_recv()
        for cp in sends:
            cp.wait_send()

    shapes = [jax.ShapeDtypeStruct(b.shape, b.dtype) for b in bufs]
    return _comm_call(name, body, bufs, shapes, 6 * n, {a: a for a in range(n)})


_HBM = pl.BlockSpec(memory_space=pltpu.HBM)
_SEM = pl.BlockSpec(memory_space=pltpu.SEMAPHORE)
_EFFECT = pltpu.SideEffectType.DATAFLOW_SIDE_EFFECTING


def _half_rows(ref, block, half):
    rh = ref.shape[1] // 2
    return ref.at[block, pl.ds(half * rh, rh)]


def _gather_start(name, bufs, after):
    n = len(bufs)

    def body(*refs):
        ins, send_sems, recv_sems, token = refs[:n], refs[n + 1], refs[n + 2], refs[-1]
        me, c, peers = _chip_peers()
        for a in range(n):
            mine = _half_rows(ins[a], me, c)
            for j, (px, py) in enumerate(peers):
                pltpu.make_async_remote_copy(
                    src_ref=mine, dst_ref=mine, send_sem=send_sems.at[3 * a + j], recv_sem=recv_sems.at[3 * a + j],
                    device_id=(px, py, c), device_id_type=MESH).start()
        token[...] = jnp.zeros_like(token)

    out_shape = (pltpu.SemaphoreType.DMA((3 * n,)), pltpu.SemaphoreType.DMA((3 * n,)),
                 *[pltpu.HBM(b.shape, b.dtype) for b in bufs], jax.ShapeDtypeStruct((8, LANES), F32))
    return pl.pallas_call(
        body, name=name, out_shape=out_shape, in_specs=[_HBM] * n + [_ANY],
        out_specs=(_SEM, _SEM, *[_HBM] * n, pl.BlockSpec(memory_space=pltpu.VMEM)),
        input_output_aliases={a: 2 + a for a in range(n)},
        compiler_params=pltpu.CompilerParams(has_side_effects=_EFFECT),
    )(*[pltpu.with_memory_space_constraint(b, pltpu.HBM) for b in bufs], after)


def _gather_wait(name, bufs, send_sems, recv_sems, after):
    n = len(bufs)

    def body(*refs):
        ins, send_sems, recv_sems = refs[:n], refs[n], refs[n + 1]
        me, c, peers = _chip_peers()
        for a in range(n):
            for j, (px, py) in enumerate(peers):
                copy = pltpu.make_async_remote_copy(
                    src_ref=_half_rows(ins[a], me, c), dst_ref=_half_rows(ins[a], 2 * px + py, c),
                    send_sem=send_sems.at[3 * a + j], recv_sem=recv_sems.at[3 * a + j],
                    device_id=(px, py, c), device_id_type=MESH)
                copy.wait_send()
                copy.wait_recv()

    return pl.pallas_call(
        body, name=name, out_shape=tuple(pltpu.HBM(b.shape, b.dtype) for b in bufs),
        in_specs=[_HBM] * n + [_SEM, _SEM, _ANY], out_specs=tuple([_HBM] * n),
        input_output_aliases={a: a for a in range(n)},
        compiler_params=pltpu.CompilerParams(has_side_effects=_EFFECT),
    )(*bufs, send_sems, recv_sems, after)


def _hand_over(name, bufs):
    n = len(bufs)

    def body(*refs):
        outs, send_sems, recv_sems = refs[n:2 * n], refs[2 * n], refs[2 * n + 1]
        x, y, c = _place()
        _, _, peers = _chip_peers()

        def copy(a, j, half):
            px, py = peers[j]
            rows = _half_rows(outs[a], 2 * px + py, half)
            return pltpu.make_async_remote_copy(
                src_ref=rows, dst_ref=rows, send_sem=send_sems.at[3 * a + j], recv_sem=recv_sems.at[3 * a + j],
                device_id=(x, y, 1 - c), device_id_type=MESH)

        sends = [copy(a, j, c) for a in range(n) for j in range(3)]
        for cp in sends:
            cp.start()
        for a in range(n):
            for j in range(3):
                copy(a, j, 1 - c).wait_recv()
        for cp in sends:
            cp.wait_send()

    shapes = [jax.ShapeDtypeStruct(b.shape, b.dtype) for b in bufs]
    return _comm_call(name, body, bufs, shapes, 3 * n, {a: a for a in range(n)})


def _pair_exchange(name, gs):
    n = len(gs)

    def body(*refs):
        ins, outs, send_sems, recv_sems = refs[:n], refs[n:2 * n], refs[2 * n], refs[2 * n + 1]
        x, y, c = _place()
        copies = [pltpu.make_async_remote_copy(
            src_ref=ins[a].at[:, 1 - c], dst_ref=outs[a], send_sem=send_sems.at[a], recv_sem=recv_sems.at[a],
            device_id=(x, y, 1 - c), device_id_type=MESH) for a in range(n)]
        for cp in copies:
            cp.start()
        for cp in copies:
            cp.wait()

    shapes = [jax.ShapeDtypeStruct((g.shape[0],) + g.shape[2:], g.dtype) for g in gs]
    return _comm_call(name, body, gs, shapes, n)


def _pair_add(name, g, got, chip, core):
    n, _, rh, lanes = g.shape
    tm = _row_tile(rh, 1024, ROW_ALIGN)

    def body(idx_ref, g_ref, got_ref, own_ref, o16_ref):
        s = g_ref[...] + got_ref[...]
        o16_ref[...] = s.astype(BF16)

        @pl.when(pl.program_id(1) == idx_ref[1])
        def _():
            own_ref[...] = s

    grid_spec = pltpu.PrefetchScalarGridSpec(
        num_scalar_prefetch=1, grid=(rh // tm, n),
        in_specs=[pl.BlockSpec((None, None, tm, lanes), lambda i, s, idx_ref: (s, idx_ref[0], i, 0)),
                  pl.BlockSpec((None, tm, lanes), lambda i, s, idx_ref: (s, i, 0))],
        out_specs=[pl.BlockSpec((tm, lanes), lambda i, s, idx_ref: (i, 0)),
                   pl.BlockSpec((None, tm, lanes), lambda i, s, idx_ref: (s, i, 0))])
    return pl.pallas_call(
        body, name=name, grid_spec=grid_spec,
        out_shape=[jax.ShapeDtypeStruct((rh, lanes), F32), jax.ShapeDtypeStruct((n, rh, lanes), BF16)],
        compiler_params=_params(("parallel", "arbitrary")),
    )(jnp.stack([core, chip]).astype(jnp.int32), g, got)


def _chip_scatter(name, ps):
    n = len(ps)

    def body(*refs):
        ins, outs, send_sems, recv_sems = refs[:n], refs[n:2 * n], refs[2 * n], refs[2 * n + 1]
        me, c, peers = _chip_peers()

        def copy(a, j, src_block, dst_block):
            px, py = peers[j]
            return pltpu.make_async_remote_copy(
                src_ref=ins[a].at[src_block], dst_ref=outs[a].at[dst_block], send_sem=send_sems.at[3 * a + j],
                recv_sem=recv_sems.at[3 * a + j], device_id=(px, py, c), device_id_type=MESH)

        sends = [copy(a, j, 2 * px + py, me) for a in range(n) for j, (px, py) in enumerate(peers)]
        for cp in sends:
            cp.start()
        for a in range(n):
            for j, (px, py) in enumerate(peers):
                copy(a, j, me, 2 * px + py).wait_recv()
        for cp in sends:
            cp.wait_send()

    shapes = [jax.ShapeDtypeStruct(p.shape, p.dtype) for p in ps]
    return _comm_call(name, body, ps, shapes, 3 * n)


def _scatter_start(name, ps, lands):
    n = len(ps)

    def body(*refs):
        srcs, dsts, send_sems, recv_sems, token = refs[:n], refs[n:2 * n], refs[2 * n], refs[2 * n + 1], refs[-1]
        me, c, peers = _chip_peers()
        for a in range(n):
            for j, (px, py) in enumerate(peers):
                pltpu.make_async_remote_copy(
                    src_ref=srcs[a].at[2 * px + py], dst_ref=dsts[a].at[me], send_sem=send_sems.at[3 * a + j],
                    recv_sem=recv_sems.at[3 * a + j], device_id=(px, py, c), device_id_type=MESH).start()
        token[...] = jnp.zeros_like(token)

    bufs = list(ps) + list(lands)
    out_shape = (pltpu.SemaphoreType.DMA((3 * n,)), pltpu.SemaphoreType.DMA((3 * n,)),
                 *[pltpu.HBM(b.shape, b.dtype) for b in bufs], jax.ShapeDtypeStruct((8, LANES), F32))
    return pl.pallas_call(
        body, name=name, out_shape=out_shape, in_specs=[_HBM] * (2 * n),
        out_specs=(_SEM, _SEM, *[_HBM] * (2 * n), pl.BlockSpec(memory_space=pltpu.VMEM)),
        input_output_aliases={a: 2 + a for a in range(2 * n)},
        compiler_params=pltpu.CompilerParams(has_side_effects=_EFFECT),
    )(*[pltpu.with_memory_space_constraint(b, pltpu.HBM) for b in bufs])


def _scatter_wait(name, ps, lands, send_sems, recv_sems, after):
    n = len(ps)

    def body(*refs):
        srcs, dsts, send_sems, recv_sems = refs[:n], refs[n:2 * n], refs[2 * n], refs[2 * n + 1]
        me, c, peers = _chip_peers()
        for a in range(n):
            for j, (px, py) in enumerate(peers):
                copy = pltpu.make_async_remote_copy(
                    src_ref=srcs[a].at[2 * px + py], dst_ref=dsts[a].at[2 * px + py],
                    send_sem=send_sems.at[3 * a + j], recv_sem=recv_sems.at[3 * a + j],
                    device_id=(px, py, c), device_id_type=MESH)
                copy.wait_send()
                copy.wait_recv()

    bufs = list(ps) + list(lands)
    outs = pl.pallas_call(
        body, name=name, out_shape=tuple(pltpu.HBM(b.shape, b.dtype) for b in bufs),
        in_specs=[_HBM] * (2 * n) + [_SEM, _SEM, _ANY], out_specs=tuple([_HBM] * (2 * n)),
        input_output_aliases={a: a for a in range(2 * n)},
        compiler_params=pltpu.CompilerParams(has_side_effects=_EFFECT),
    )(*bufs, send_sems, recv_sems, after)
    return list(outs[n:])


def _sum_ring(name, own, got, chip, core):
    n, rh, lanes = got.shape
    tm = _row_tile(rh, 2048, ROW_ALIGN)

    def body(idx_ref, own_ref, g1_ref, g2_ref, g3_ref, o_ref):
        o_ref[...] = ((own_ref[...] + g1_ref[...].astype(F32)) + g2_ref[...].astype(F32)) + g3_ref[...].astype(F32)

    def block(k):
        return pl.BlockSpec((None, tm, lanes), lambda i, idx_ref: ((idx_ref[0] + k) % n, i, 0))

    grid_spec = pltpu.PrefetchScalarGridSpec(
        num_scalar_prefetch=1, grid=(rh // tm,),
        in_specs=[pl.BlockSpec((tm, lanes), lambda i, idx_ref: (i, 0)), block(1), block(2), block(3)],
        out_specs=pl.BlockSpec((None, tm, lanes), lambda i, idx_ref: (idx_ref[1], i, 0)))
    return pl.pallas_call(
        body, name=name, grid_spec=grid_spec, out_shape=jax.ShapeDtypeStruct((2, rh, lanes), F32),
        compiler_params=_params(("parallel",)),
    )(jnp.stack([chip, core]).astype(jnp.int32), own, got, got, got)


def _pair_gather(name, bufs):
    n = len(bufs)

    def body(*refs):
        ins, outs, send_sems, recv_sems = refs[:n], refs[n:2 * n], refs[2 * n], refs[2 * n + 1]
        x, y, c = _place()

        def copy(a, block):
            return pltpu.make_async_remote_copy(
                src_ref=ins[a].at[block], dst_ref=outs[a].at[block], send_sem=send_sems.at[a],
                recv_sem=recv_sems.at[a], device_id=(x, y, 1 - c), device_id_type=MESH)

        sends = [copy(a, c) for a in range(n)]
        for cp in sends:
            cp.start()
        for a in range(n):
            copy(a, 1 - c).wait_recv()
        for cp in sends:
            cp.wait_send()

    shapes = [jax.ShapeDtypeStruct(b.shape, b.dtype) for b in bufs]
    return _comm_call(name, body, bufs, shapes, n, {a: a for a in range(n)})


def _gather_all(name, s):
    def body(in_ref, out_ref, send_sems, recv_sems, local_sem):
        x, y, c = _place()
        me = 4 * x + 2 * y + c
        peers = []
        for mask in range(1, N_DEV):
            fx, fy, fc = (mask >> 2) & 1, (mask >> 1) & 1, mask & 1
            peers.append((jnp.where(fx, 1 - x, x), jnp.where(fy, 1 - y, y), jnp.where(fc, 1 - c, c)))

        def copy(j, block):
            return pltpu.make_async_remote_copy(
                src_ref=in_ref, dst_ref=out_ref.at[block], send_sem=send_sems.at[j], recv_sem=recv_sems.at[j],
                device_id=peers[j], device_id_type=MESH)

        local = pltpu.make_async_copy(in_ref, out_ref.at[me], local_sem)
        local.start()
        sends = [copy(j, me) for j in range(N_DEV - 1)]
        for cp in sends:
            cp.start()
        for j, (px, py, pc) in enumerate(peers):
            copy(j, 4 * px + 2 * py + pc).wait_recv()
        for cp in sends:
            cp.wait_send()
        local.wait()

    return pl.pallas_call(
        body, name=name, in_specs=[_ANY], out_specs=_ANY,
        out_shape=jax.ShapeDtypeStruct((N_DEV,) + s.shape, s.dtype),
        scratch_shapes=[pltpu.SemaphoreType.DMA((N_DEV - 1,)), pltpu.SemaphoreType.DMA((N_DEV - 1,)),
                        pltpu.SemaphoreType.DMA],
    )(s)


def _sum_blocks(name, stacked, tm):
    n, r, lanes = stacked.shape

    def body(in_ref, o_ref):
        acc = in_ref[0]
        for j in range(1, n):
            acc = acc + in_ref[j]
        o_ref[...] = acc

    return pl.pallas_call(
        body, name=name, grid=(r // tm,), in_specs=[pl.BlockSpec((n, tm, lanes), lambda i: (0, i, 0))],
        out_specs=pl.BlockSpec((tm, lanes), lambda i: (i, 0)), out_shape=jax.ShapeDtypeStruct((r, lanes), F32),
        compiler_params=_params(("parallel",)),
    )(stacked)


def _row_tile(rows, pref, align):
    best = None
    for t in range(align, min(rows, pref) + 1, align):
        if rows % t == 0:
            best = t
    assert best is not None, (rows, pref, align)
    return best


def _adam(name, w, g, m, v):
    rows, width = w.shape
    tm = _row_tile(rows, max(8, 4096 * LANES // width), 8)
    args = [(t, width, 0) for t in (w, g, m, v)]
    return _rowcall(name, _adam_fn, args, [], [(width, F32)] * 3, tm=tm)


def kernel(x, norm_mix, norm_mlp, norm_final, mlp_w1, mlp_w2, ab_w_in, ab_w_out, rg_conv_w, rg_conv_b, rg_w_a, rg_b_a, rg_w_x, rg_b_x, rg_lambda, hg_lb_logits, hg_norm, gla_w_in, gla_w_out, gla_w_gate_up, gla_b_gate, gla_norm, loss_target, m_norm_mix, m_norm_mlp, m_norm_final, m_mlp_w1, m_mlp_w2, m_ab_w_in, m_ab_w_out, m_rg_conv_w, m_rg_conv_b, m_rg_w_a, m_rg_b_a, m_rg_w_x, m_rg_b_x, m_rg_lambda, m_hg_lb_logits, m_hg_norm, m_gla_w_in, m_gla_w_out, m_gla_w_gate_up, m_gla_b_gate, m_gla_norm, v_norm_mix, v_norm_mlp, v_norm_final, v_mlp_w1, v_mlp_w2, v_ab_w_in, v_ab_w_out, v_rg_conv_w, v_rg_conv_b, v_rg_w_a, v_rg_b_a, v_rg_w_x, v_rg_b_x, v_rg_lambda, v_hg_lb_logits, v_hg_norm, v_gla_w_in, v_gla_w_out, v_gla_w_gate_up, v_gla_b_gate, v_gla_norm):
    w = dict(norm_mix=norm_mix, norm_mlp=norm_mlp, norm_final=norm_final, mlp_w1=mlp_w1, mlp_w2=mlp_w2, ab_w_in=ab_w_in, ab_w_out=ab_w_out, rg_conv_w=rg_conv_w, rg_conv_b=rg_conv_b, rg_w_a=rg_w_a, rg_b_a=rg_b_a, rg_w_x=rg_w_x, rg_b_x=rg_b_x, rg_lambda=rg_lambda, hg_lb_logits=hg_lb_logits, hg_norm=hg_norm, gla_w_in=gla_w_in, gla_w_out=gla_w_out, gla_w_gate_up=gla_w_gate_up, gla_b_gate=gla_b_gate, gla_norm=gla_norm)
    m = dict(norm_mix=m_norm_mix, norm_mlp=m_norm_mlp, norm_final=m_norm_final, mlp_w1=m_mlp_w1, mlp_w2=m_mlp_w2, ab_w_in=m_ab_w_in, ab_w_out=m_ab_w_out, rg_conv_w=m_rg_conv_w, rg_conv_b=m_rg_conv_b, rg_w_a=m_rg_w_a, rg_b_a=m_rg_b_a, rg_w_x=m_rg_w_x, rg_b_x=m_rg_b_x, rg_lambda=m_rg_lambda, hg_lb_logits=m_hg_lb_logits, hg_norm=m_hg_norm, gla_w_in=m_gla_w_in, gla_w_out=m_gla_w_out, gla_w_gate_up=m_gla_w_gate_up, gla_b_gate=m_gla_b_gate, gla_norm=m_gla_norm)
    v = dict(norm_mix=v_norm_mix, norm_mlp=v_norm_mlp, norm_final=v_norm_final, mlp_w1=v_mlp_w1, mlp_w2=v_mlp_w2, ab_w_in=v_ab_w_in, ab_w_out=v_ab_w_out, rg_conv_w=v_rg_conv_w, rg_conv_b=v_rg_conv_b, rg_w_a=v_rg_w_a, rg_b_a=v_rg_b_a, rg_w_x=v_rg_w_x, rg_b_x=v_rg_b_x, rg_lambda=v_rg_lambda, hg_lb_logits=v_hg_lb_logits, hg_norm=v_hg_norm, gla_w_in=v_gla_w_in, gla_w_out=v_gla_w_out, gla_w_gate_up=v_gla_w_gate_up, gla_b_gate=v_gla_b_gate, gla_norm=v_gla_norm)
    chip = 2 * lax.axis_index("x") + lax.axis_index("y")
    core = lax.axis_index("c")
    sharded_shapes = [w[n].shape for n in SMALL_SHARDED]

    slots = [_into_slot(f"cast_{n}{layer}", w[n], chip, N_CHIPS, BF16, 512, layer) for n, layer in MATRICES]
    early = [i for i, (n, _) in enumerate(MATRICES) if n in EARLY_MATRICES]
    rest = [i for i in range(len(MATRICES)) if i not in early]

    def named(indices, arrays):
        big = {}
        for i, t in zip(indices, arrays):
            big.setdefault(MATRICES[i][0], []).append(t)
        return {n: (v if n in ("mlp_w1", "mlp_w2") else v[0]) for n, v in big.items()}

    gathered = _gather_chips("gather_early", [slots[i] for i in early])
    send_sems, recv_sems, *in_flight, token = _gather_start("gather_rest_start", [slots[i] for i in rest], gathered[0])

    def late_weights(after):
        landed = _gather_wait("gather_rest_wait", in_flight, send_sems, recv_sems, after)
        return _prepare_matrices(named(rest, _hand_over("gather_rest_share", list(landed))))

    big = named(early, gathered)
    vectors = _pack([w[n] for n in SMALL_SHARDED])
    vectors = _into_slot("place_vectors", vectors, chip, N_CHIPS, F32, vectors.shape[0])
    small_all = _unpack(_gather_chips("gather_vectors", [vectors])[0], sharded_shapes, lead=1)
    full = {n: w[n] for n in SMALL_REPLICATED}
    for n, t in zip(SMALL_SHARDED, small_all):
        full[n] = _join_chips(t, t.ndim - 2)

    def pair_sums(tag, arrays):
        halves = [t.reshape(N_CHIPS, 2, t.shape[1] // 2, t.shape[2]) for t in arrays]
        from_sibling = _pair_exchange(f"reduce_pair_{tag}", halves)
        return [_pair_add(f"reduce_pair_add_{tag}{i}", h, s, chip, core)
                for i, (h, s) in enumerate(zip(halves, from_sibling))]

    in_flight_grads = {}

    def emit(tag, arrays):
        parts = pair_sums(tag, arrays)
        p16 = [p for _, p in parts]
        send, recv, *rest = _scatter_start(f"reduce_chips_{tag}_start", p16, [lax.empty(p.shape, p.dtype) for p in p16])
        in_flight_grads[tag] = ([p for p, _ in parts], rest[:len(p16)], rest[len(p16):-1], send, recv)
        return rest[-1]

    loss_part, grad_x, g_kernel = _local_step(
        x[0], loss_target[0], _prepare_weights(big, full), token, late_weights, emit)
    g_big, g_full = _finish_grads(g_kernel)
    loss = lax.psum(loss_part[0, 0], ("x", "y", "c"))

    mine = {}
    last = pair_sums("ab", [g_big["ab_w_in"], g_big["ab_w_out"]])
    from_chips = _chip_scatter("reduce_chips_ab", [p for _, p in last])
    mine["ab"] = [_sum_ring(f"reduce_chips_add_ab{i}", p32, f, chip, core)
                  for i, ((p32, _), f) in enumerate(zip(last, from_chips))]
    for tag, (p32s, p16s, lands, send, recv) in in_flight_grads.items():
        landed = _scatter_wait(f"reduce_chips_{tag}_wait", p16s, lands, send, recv, mine["ab"][0])
        mine[tag] = [_sum_ring(f"reduce_chips_add_{tag}{i}", p32, f, chip, core)
                     for i, (p32, f) in enumerate(zip(p32s, landed))]
    ordered = [mine["mlp0"][0], mine["mlp1"][0], mine["mlp0"][1], mine["mlp1"][1], *mine["ab"], *mine["gla"]]
    reduced = [t.reshape(2 * t.shape[1], t.shape[2]) for t in _pair_gather("reduce_share", ordered)]
    by_name = {n: [] for n, _ in MATRICES}
    for (n, _), t in zip(MATRICES, reduced):
        by_name[n].append(t)
    grads = {n: jnp.stack(v) for n, v in by_name.items()}

    small_names = SMALL_REPLICATED + SMALL_SHARDED
    g_small = _pack([g_full[n] for n in small_names])
    g_small_all = _gather_all("reduce_small", g_small)
    g_small_red = _sum_blocks("reduce_small_add", g_small_all, g_small.shape[0])
    g_small_full = dict(zip(small_names, _unpack(g_small_red, [g_full[n].shape for n in small_names])))
    for n in SMALL_REPLICATED:
        grads[n] = g_small_full[n]
    for n in SMALL_SHARDED:
        width = w[n].shape[-1]
        grads[n] = lax.dynamic_slice_in_dim(g_small_full[n], chip * width, width, axis=g_small_full[n].ndim - 1)

    delta, new_m, new_v = {}, {}, {}
    for n in by_name:
        flat = [t.reshape(-1, t.shape[-1]) for t in (w[n], grads[n], m[n], v[n])]
        for dst, t in zip((delta, new_m, new_v), _adam(f"adam_{n}", *flat)):
            dst[n] = t.reshape(w[n].shape)
    small_shapes = [w[n].shape for n in small_names]
    packs = [_pack([src[n] for n in small_names]) for src in (w, grads, m, v)]
    d_small, m_small, v_small = _adam("adam_small", *packs)
    for dst, buf in ((delta, d_small), (new_m, m_small), (new_v, v_small)):
        dst.update(zip(small_names, _unpack(buf, small_shapes)))

    return (loss, grad_x[None], *[grads[n] for n in WEIGHTS], *[delta[n] for n in WEIGHTS],
            *[new_m[n] for n in WEIGHTS], *[new_v[n] for n in WEIGHTS])
```

```python
import functools

import jax
import jax.numpy as jnp
from jax import lax
from jax.experimental import pallas as pl
from jax.experimental.pallas import tpu as pltpu

F32 = jnp.float32
BF16 = jnp.bfloat16
MESH = pl.DeviceIdType.MESH

LANES = 128
CHUNK = 64
EPS = 1e-6
RG_C = 8.0
N_CHIPS = 4
N_DEV = 8
GLA_IN_WIDTH = 3104
GLA_IN_PAD = 3200
VMEM_LIMIT = 56 * 1024 * 1024

ADAM_LR = 0.001
ADAM_B1 = 0.9
ADAM_B2 = 0.999
ADAM_EPS = 1e-08
ADAM_WD = 0.01
ADAM_STEP = 10


def _raw_dot(a, b, ca, cb):
    return lax.dot_general(a.astype(BF16), b.astype(BF16), (((ca,), (cb,)), ((), ())),
                           preferred_element_type=F32)


def _raw_nn(a, b):
    return _raw_dot(a, b, 1, 0)


def _raw_nt(a, b):
    return _raw_dot(a, b, 1, 1)


def _raw_tn(a, b):
    return _raw_dot(a, b, 0, 0)


@jax.custom_vjp
def _dot_nn(a, b):
    return _raw_nn(a, b)


def _dot_nn_fwd(a, b):
    return _raw_nn(a, b), (a, b)


def _dot_nn_bwd(res, g):
    a, b = res
    return _raw_nt(g, b), _raw_tn(a, g)


_dot_nn.defvjp(_dot_nn_fwd, _dot_nn_bwd)


@jax.custom_vjp
def _dot_nt(a, b):
    return _raw_nt(a, b)


def _dot_nt_fwd(a, b):
    return _raw_nt(a, b), (a, b)


def _dot_nt_bwd(res, g):
    a, b = res
    return _raw_nn(g, b), _raw_tn(g, a)


_dot_nt.defvjp(_dot_nt_fwd, _dot_nt_bwd)


@jax.custom_vjp
def _dot_tn(a, b):
    return _raw_tn(a, b)


def _dot_tn_fwd(a, b):
    return _raw_tn(a, b), (a, b)


def _dot_tn_bwd(res, g):
    a, b = res
    return _raw_nt(b, g), _raw_nn(a, g)


_dot_tn.defvjp(_dot_tn_fwd, _dot_tn_bwd)


def _tile(n, pref):
    if n <= pref:
        return n
    t = (pref // LANES) * LANES
    while t > LANES and n % t:
        t -= LANES
    assert n % t == 0, (n, pref)
    return t


def _params(sem):
    return pltpu.CompilerParams(dimension_semantics=sem, vmem_limit_bytes=VMEM_LIMIT)


def _rowcall(name, fn, rows, pars, row_outs, par_outs=(), tm=256, pin=None):
    if pin is not None:
        inner, pars = fn, list(pars) + [pin]
        fn = lambda *vals: inner(*vals[:-1])
    n_rows = rows[0][0].shape[0]
    tm = min(tm, n_rows)
    assert n_rows % tm == 0
    n_r, n_p, n_ro = len(rows), len(pars), len(row_outs)

    def body(*refs):
        vals = [r[...].astype(F32) for r in refs[:n_r + n_p]]
        outs = fn(*vals)
        o_refs = refs[n_r + n_p:n_r + n_p + n_ro]
        po_refs = refs[n_r + n_p + n_ro:]
        for o_ref, val in zip(o_refs, outs[:n_ro]):
            o_ref[...] = val.astype(o_ref.dtype)
        first = pl.program_id(0) == 0
        for po_ref, val in zip(po_refs, outs[n_ro:]):
            @pl.when(first)
            def _():
                po_ref[...] = val

            @pl.when(jnp.logical_not(first))
            def _():
                po_ref[...] += val

    def const_map(nd):
        return lambda i: (0,) * nd

    def row_spec(w, cb):
        return pl.BlockSpec((tm, w), lambda i: (i, cb))

    in_specs = [row_spec(w, cb) for _, w, cb in rows]
    in_specs += [pl.BlockSpec(p.shape, const_map(p.ndim)) for p in pars]
    out_specs = [pl.BlockSpec((tm, w), lambda i: (i, 0)) for w, _ in row_outs]
    out_specs += [pl.BlockSpec(tuple(s), const_map(len(s))) for s in par_outs]
    out_shape = [jax.ShapeDtypeStruct((n_rows, w), dt) for w, dt in row_outs]
    out_shape += [jax.ShapeDtypeStruct(tuple(s), F32) for s in par_outs]
    return pl.pallas_call(
        body, name=name, grid=(n_rows // tm,), in_specs=in_specs, out_specs=out_specs, out_shape=out_shape,
        compiler_params=_params(("arbitrary",) if par_outs else ("parallel",)),
    )(*[r[0] for r in rows], *pars)


def _vjp_of(fn, n_prim, n_out, n_par, n_pass=0):
    def bwd(*args):
        prim = args[:n_prim]
        cts = args[n_prim:n_prim + n_out]
        passes = args[n_prim + n_out:n_prim + n_out + 2 * n_pass]
        pars = args[n_prim + n_out + 2 * n_pass:]
        _, vjp = jax.vjp(fn, *prim, *pars)
        grads = vjp(tuple(cts))
        sums = tuple(passes[2 * i] + passes[2 * i + 1] for i in range(n_pass))
        return tuple(grads[:n_prim]) + sums + tuple(grads[n_prim:])
    return bwd


def _mm(name, a, b, mode="nn", extras=(), epi=None, out_dtypes=(F32,), a_pro=None, out_split=None,
        tm=1024, tn=1024, tk=1024):
    split = b.shape[0] if b.ndim == 3 else None
    b_rows, b_cols = b.shape[-2:]
    if mode == "nn":
        (m, k), n = a.shape, b_cols * (split or 1)
    elif mode == "nt":
        (m, k), n = a.shape, b_rows
        assert k == b_cols * (split or 1)
    else:
        assert split is None
        (k, m), n = a.shape, b_cols
    tm, tk = _tile(m, tm), _tile(k, tk)
    tn = _tile(n // out_split, tn) if out_split else _tile(n, tn)
    if split and mode == "nn":
        tn = _tile(b_cols, tn)
    if split and mode == "nt":
        tk = _tile(b_cols, tk)
    nk = k // tk
    raw = {"nn": _raw_nn, "nt": _raw_nt, "tn": _raw_tn}[mode]
    n_e, n_o = len(extras), len(out_dtypes)
    if epi is None:
        epi = lambda acc: (acc,)

    def body(a_ref, b_ref, *rest):
        e_refs, o_refs = rest[:n_e], rest[n_e:n_e + n_o]
        kk = pl.program_id(2)
        a_tile = a_ref[...] if a_pro is None else a_pro(a_ref[...].astype(F32))
        part = raw(a_tile, b_ref[...])

        def finish(total):
            res = epi(total, *[e[...].astype(F32) for e in e_refs])
            for o_ref, r in zip(o_refs, res):
                o_ref[...] = r.astype(o_ref.dtype)

        if nk == 1:
            finish(part)
            return
        acc = rest[-1]

        @pl.when(kk == 0)
        def _():
            acc[...] = part

        @pl.when((kk > 0) & (kk < nk - 1))
        def _():
            acc[...] += part

        @pl.when(kk == nk - 1)
        def _():
            finish(acc[...] + part)

    a_spec = pl.BlockSpec((tk, tm), lambda i, j, kk: (kk, i)) if mode == "tn" else pl.BlockSpec((tm, tk), lambda i, j, kk: (i, kk))
    if split and mode == "nn":
        per = b_cols // tn
        b_spec = pl.BlockSpec((None, tk, tn), lambda i, j, kk: (j // per, kk, j % per))
    elif split:
        per = b_cols // tk
        b_spec = pl.BlockSpec((None, tn, tk), lambda i, j, kk: (kk // per, j, kk % per))
    elif mode == "nt":
        b_spec = pl.BlockSpec((tn, tk), lambda i, j, kk: (j, kk))
    else:
        b_spec = pl.BlockSpec((tk, tn), lambda i, j, kk: (kk, j))
    mn_spec = pl.BlockSpec((tm, tn), lambda i, j, kk: (i, j))
    if out_split:
        assert not extras
        per_out = n // out_split // tn
        out_spec = pl.BlockSpec((None, tm, tn), lambda i, j, kk: (j // per_out, i, j % per_out))
        out_shapes = [jax.ShapeDtypeStruct((out_split, m, n // out_split), dt) for dt in out_dtypes]
    else:
        out_spec = mn_spec
        out_shapes = [jax.ShapeDtypeStruct((m, n), dt) for dt in out_dtypes]
    outs = pl.pallas_call(
        body, name=name, grid=(m // tm, n // tn, nk),
        in_specs=[a_spec, b_spec] + [mn_spec] * n_e, out_specs=[out_spec] * n_o,
        out_shape=out_shapes,
        scratch_shapes=[pltpu.VMEM((tm, tn), F32)] if nk > 1 else [],
        compiler_params=_params(("parallel", "parallel", "arbitrary")),
    )(a, b, *extras)
    return outs[0] if n_o == 1 else outs


def _sigmoid(x):
    return jax.nn.sigmoid(x)


def _silu(x):
    return x * _sigmoid(x)


def _softplus(x):
    return jnp.maximum(x, 0.0) + jnp.log1p(jnp.exp(-jnp.abs(x)))


def _rmsnorm_fn(x, gain):
    return (x * lax.rsqrt(jnp.mean(x * x, axis=-1, keepdims=True) + EPS) * gain,)


def _head_norm(o, gain, n_heads):
    w = o.shape[-1] // n_heads
    parts = []
    for h in range(n_heads):
        oh = o[:, h * w:(h + 1) * w]
        parts.append(oh * lax.rsqrt(jnp.mean(oh * oh, axis=-1, keepdims=True) + EPS))
    return jnp.concatenate(parts, axis=-1) * gain


@jax.custom_jvp
def _neg_expm1(x):
    u = jnp.exp(x)
    is_one = u == 1.0
    return jnp.where(is_one, -x, (1.0 - u) * x / jnp.log(jnp.where(is_one, 2.0, u)))


@_neg_expm1.defjvp
def _neg_expm1_jvp(primals, tangents):
    (x,), (t,) = primals, tangents
    return _neg_expm1(x), -jnp.exp(x) * t


def _rg_gates_fn(xc, wa, wx, ba, bx, lam):
    outs = []
    for d in range(2):
        r = _sigmoid(_dot_nn(xc, wa[d]) + ba[d:d + 1])
        i = _sigmoid(_dot_nn(xc, wx[d]) + bx[d:d + 1])
        log_a = -RG_C * r * _softplus(-lam[d:d + 1])
        outs.append(jnp.exp(log_a))
        outs.append(jnp.sqrt(_neg_expm1(2.0 * log_a)) * (i * xc))
    return tuple(outs)


def _hg_pre_fn(q, f_f, f_b, logits):
    mx = jnp.maximum(logits[0:1], logits[1:2])
    e0 = jnp.exp(logits[0:1] - mx)
    e1 = jnp.exp(logits[1:2] - mx)
    lb = e0 / (e0 + e1)
    outs = [_silu(q)]
    for f in (f_f, f_b):
        outs.append((1.0 - lb) * _sigmoid(-f))
        outs.append(jnp.log(lb + (1.0 - lb) * _sigmoid(f)))
    return tuple(outs)


def _post0_fn(hs, ga, o, g, gain):
    ya = hs * jax.nn.gelu(ga, approximate=True)
    yb = _head_norm(o, gain, 4) * _silu(g)
    return (jnp.concatenate([ya, yb], axis=-1),)


def _post0_fwd_fn(h_f, h_b, ga, o_f, o_b, g, gain):
    return _post0_fn(h_f + h_b, ga, o_f + o_b, g, gain)


def _post0_bwd_fn(h_f, h_b, ga, o_f, o_b, g, dmix, gain):
    _, vjp = jax.vjp(_post0_fn, h_f + h_b, ga, o_f + o_b, g, gain)
    return vjp((dmix,))


def _gla_pre_fn(q, lr, w_up, b_gate):
    outs = [q * (128.0 ** -0.5)]
    for d in range(2):
        z = _dot_nn(lr, w_up[d]) + b_gate[d:d + 1]
        outs.append(-_softplus(-z) * (1.0 / 16.0))
    return tuple(outs)


def _gla_post_fn(o, r, gain):
    return (_head_norm(o, gain, 4) * _silu(r),)


def _gla_post_fwd_fn(o_f, o_b, r, gain):
    return _gla_post_fn(o_f + o_b, r, gain)


def _gla_post_bwd_fn(o_f, o_b, r, dmix, gain):
    _, vjp = jax.vjp(_gla_post_fn, o_f + o_b, r, gain)
    return vjp((dmix,))


def _relu2_bwd_epi(acc, hid):
    return (acc * 2.0 * jnp.maximum(hid, 0.0),)


def _relu2(x):
    r = jnp.maximum(x, 0.0)
    return r * r


def _add_epi(acc, res):
    return (acc + res,)


def _loss_head_fn(h, target, gain):
    def f(h, gain):
        y = _rmsnorm_fn(h, gain)[0]
        err = y - target
        return 0.5 * jnp.sum(jnp.mean(err * err, axis=-1, keepdims=True))
    loss, (dh, dgain) = jax.value_and_grad(f, argnums=(0, 1))(h, gain)
    return dh, jnp.full((1, LANES), loss, F32), dgain


def _adam_fn(w, g, m, v):
    m2 = ADAM_B1 * m + (1.0 - ADAM_B1) * g
    v2 = ADAM_B2 * v + (1.0 - ADAM_B2) * (g * g)
    m_hat = m2 / (1.0 - ADAM_B1 ** ADAM_STEP)
    v_hat = v2 / (1.0 - ADAM_B2 ** ADAM_STEP)
    delta = -ADAM_LR * (m_hat / (jnp.sqrt(v_hat) + ADAM_EPS) + ADAM_WD * w)
    return delta, m2, v2


def _shifted(x, t_idx, off):
    n = x.shape[0]
    rolled = pltpu.roll(x, (-off) % n, 0)
    valid = (t_idx + off >= 0) & (t_idx + off < n)
    return jnp.where(valid, rolled, 0.0)


def _conv_fwd(name, src, colblock, w, b):
    n_rows, width = src.shape[0], w.shape[1]

    def body(x_ref, w_ref, b_ref, o_ref):
        x = x_ref[...]
        t_idx = lax.broadcasted_iota(jnp.int32, x.shape, 0)
        acc = b_ref[...] + w_ref[2:3, :] * x
        acc += w_ref[0:1, :] * _shifted(x, t_idx, -2)
        acc += w_ref[1:2, :] * _shifted(x, t_idx, -1)
        acc += w_ref[3:4, :] * _shifted(x, t_idx, 1)
        o_ref[...] = acc

    nb = width // LANES
    return pl.pallas_call(
        body, name=name, grid=(nb,),
        in_specs=[pl.BlockSpec((n_rows, LANES), lambda j: (0, colblock * nb + j)),
                  pl.BlockSpec((4, LANES), lambda j: (0, j)), pl.BlockSpec((1, LANES), lambda j: (0, j))],
        out_specs=pl.BlockSpec((n_rows, LANES), lambda j: (0, j)),
        out_shape=jax.ShapeDtypeStruct((n_rows, width), F32),
        compiler_params=_params(("parallel",)),
    )(src, w, b)


def _conv_bwd(name, src, colblock, w, d):
    n_rows, width = src.shape[0], w.shape[1]

    def body(x_ref, w_ref, d_ref, dx_ref, dw_ref, db_ref):
        x = x_ref[...]
        g = d_ref[...]
        t_idx = lax.broadcasted_iota(jnp.int32, x.shape, 0)
        dx = w_ref[2:3, :] * g
        dx += w_ref[0:1, :] * _shifted(g, t_idx, 2)
        dx += w_ref[1:2, :] * _shifted(g, t_idx, 1)
        dx += w_ref[3:4, :] * _shifted(g, t_idx, -1)
        dx_ref[...] = dx.astype(dx_ref.dtype)
        dw_ref[0:1, :] = jnp.sum(g * _shifted(x, t_idx, -2), axis=0, keepdims=True)
        dw_ref[1:2, :] = jnp.sum(g * _shifted(x, t_idx, -1), axis=0, keepdims=True)
        dw_ref[2:3, :] = jnp.sum(g * x, axis=0, keepdims=True)
        dw_ref[3:4, :] = jnp.sum(g * _shifted(x, t_idx, 1), axis=0, keepdims=True)
        db_ref[...] = jnp.sum(g, axis=0, keepdims=True)

    nb = width // LANES
    return pl.pallas_call(
        body, name=name, grid=(nb,),
        in_specs=[pl.BlockSpec((n_rows, LANES), lambda j: (0, colblock * nb + j)),
                  pl.BlockSpec((4, LANES), lambda j: (0, j)),
                  pl.BlockSpec((n_rows, LANES), lambda j: (0, j))],
        out_specs=[pl.BlockSpec((n_rows, LANES), lambda j: (0, j)), pl.BlockSpec((4, LANES), lambda j: (0, j)),
                   pl.BlockSpec((1, LANES), lambda j: (0, j))],
        out_shape=[jax.ShapeDtypeStruct((n_rows, width), BF16), jax.ShapeDtypeStruct((4, width), F32),
                   jax.ShapeDtypeStruct((1, width), F32)],
        compiler_params=_params(("parallel",)),
    )(src, w, d)


SUBLANES = 8
SCAN_UNROLL = 8


def _shift_rows(x, d, fill):
    n = x.shape[0]
    t = lax.broadcasted_iota(jnp.int32, x.shape, 0)
    valid = (t >= d) if d > 0 else (t < n + d)
    return jnp.where(valid, pltpu.roll(x, d % n, 0), fill)


def _tile_scan(a, u, reverse):
    d = 1
    while d < a.shape[0]:
        s = -d if reverse else d
        a_sh, u_sh = _shift_rows(a, s, 1.0), _shift_rows(u, s, 0.0)
        u = u + a * u_sh
        a = a * a_sh
        d *= 2
    return a, u


def _edge_row(x, reverse):
    return x[0:1, :] if reverse else x[SUBLANES - 1:SUBLANES, :]


def _scan_specs(n_rows, n):
    return [pl.BlockSpec((n_rows, LANES), lambda j: (0, j))] * n


def _scan_fwd(name, a, u, reverse):
    n_rows, width = a.shape
    n_tiles = n_rows // SUBLANES

    def body(a_ref, u_ref, h_ref):
        def step(i, carry):
            tile = (n_tiles - 1 - i) if reverse else i
            rows = pl.ds(pl.multiple_of(tile * SUBLANES, SUBLANES), SUBLANES)
            acc_a, acc_u = _tile_scan(a_ref[rows, :], u_ref[rows, :], reverse)
            h = acc_u + acc_a * carry
            h_ref[rows, :] = h
            return _edge_row(h, reverse)
        lax.fori_loop(0, n_tiles, step, jnp.zeros((1, LANES), F32), unroll=SCAN_UNROLL)

    return pl.pallas_call(
        body, name=name, grid=(width // LANES,), in_specs=_scan_specs(n_rows, 2), out_specs=_scan_specs(n_rows, 1)[0],
        out_shape=jax.ShapeDtypeStruct((n_rows, width), F32), compiler_params=_params(("parallel",)),
    )(a, u)


def _scan_bwd(name, a, h, dh, reverse):
    n_rows, width = a.shape
    n_tiles = n_rows // SUBLANES
    against = not reverse
    one = -1 if against else 1

    def body(a_ref, h_ref, dh_ref, du_ref, da_ref):
        def step(i, carry):
            g_in, a_edge = carry
            tile = (n_tiles - 1 - i) if against else i
            start = pl.multiple_of(tile * SUBLANES, SUBLANES)
            rows = pl.ds(start, SUBLANES)
            a_tile = a_ref[rows, :]
            coeff = _shift_rows(a_tile, one, a_edge)
            acc_a, acc_u = _tile_scan(coeff, dh_ref[rows, :], against)
            g = acc_u + acc_a * g_in
            du_ref[rows, :] = g
            outside = (start + SUBLANES) if reverse else (start - 1)
            inside = (outside >= 0) & (outside < n_rows)
            h_edge = jnp.where(inside, h_ref[pl.ds(jnp.clip(outside, 0, n_rows - 1), 1), :], 0.0)
            da_ref[rows, :] = g * _shift_rows(h_ref[rows, :], -one, h_edge)
            return _edge_row(g, against), _edge_row(a_tile, against)
        zero = jnp.zeros((1, LANES), F32)
        lax.fori_loop(0, n_tiles, step, (zero, zero), unroll=SCAN_UNROLL)

    return pl.pallas_call(
        body, name=name, grid=(width // LANES,), in_specs=_scan_specs(n_rows, 3), out_specs=_scan_specs(n_rows, 2),
        out_shape=[jax.ShapeDtypeStruct((n_rows, width), F32)] * 2, compiler_params=_params(("parallel",)),
    )(a, h, dh)


def _tri_mask(c, reverse):
    row = lax.broadcasted_iota(jnp.int32, (c, c), 0)
    col = lax.broadcasted_iota(jnp.int32, (c, c), 1)
    return (col >= row) if reverse else (col <= row)


def _cumsum_rows(x, reverse):
    tri = _tri_mask(x.shape[0], reverse).astype(BF16)
    hi = x.astype(BF16)
    rest = x - hi.astype(F32)
    mid = rest.astype(BF16)
    lo = (rest - mid.astype(F32)).astype(BF16)
    return _raw_nn(tri, hi) + _raw_nn(tri, mid) + _raw_nn(tri, lo)


@functools.partial(jax.custom_vjp, nondiff_argnums=(1,))
def _cumsum(x, reverse):
    return _cumsum_rows(x, reverse)


def _cumsum_fwd(x, reverse):
    return _cumsum_rows(x, reverse), None


def _cumsum_bwd(reverse, _, g):
    return (_cumsum_rows(g, not reverse),)


_cumsum.defvjp(_cumsum_fwd, _cumsum_bwd)


def _chunks_fn(qs, ks, vs, lfs, sts, reverses):
    n, c = len(qs), qs[0].shape[0]
    every = range(n)
    tris = [_tri_mask(c, r) for r in reverses]
    cums = [_cumsum(lfs[i], reverses[i]) for i in every]
    rid = lax.broadcasted_iota(jnp.int32, cums[0].shape, 0)

    def pick(cum, r):
        return jnp.sum(jnp.where(rid == r, cum, 0.0), axis=0, keepdims=True)

    refs = [pick(cums[i], (c - 1 - c // 2) if reverses[i] else c // 2) for i in every]
    lasts = [pick(cums[i], 0 if reverses[i] else c - 1) for i in every]
    q_in = [qs[i] * jnp.exp(cums[i] - refs[i]) for i in every]
    k_in = [ks[i] * jnp.exp(refs[i] - cums[i]) for i in every]
    scores = [jnp.where(tris[i], _dot_nt(q_in[i], k_in[i]), 0.0) for i in every]
    o_intra = [_dot_nn(scores[i], vs[i]) for i in every]
    q_out = [qs[i] * jnp.exp(cums[i]) for i in every]
    o_inter = [_dot_nt(q_out[i], sts[i]) for i in every]
    k_state = [ks[i] * jnp.exp(lasts[i] - cums[i]) for i in every]
    upd = [_dot_tn(vs[i], k_state[i]) for i in every]
    st_new = [sts[i] * jnp.exp(lasts[i]) + upd[i] for i in every]
    return [o_intra[i] + o_inter[i] for i in every], st_new


def _attn_fwd(name, q, k_f, k_b, v, lf_f, lf_b, n_heads, dk, dv):
    n_rows = q[0].shape[0]
    n_chunks = n_rows // CHUNK
    wk, wv = n_heads * dk, n_heads * dv

    def spec(width, off, rev):
        return pl.BlockSpec((CHUNK, width), lambda n: ((n_chunks - 1 - n) if rev else n, off))

    def sspec(rev):
        return pl.BlockSpec((None, n_heads, dv, dk), lambda n: ((n_chunks - 1 - n) if rev else n, 0, 0, 0))

    def body(qf, kf, vf, lff, qb, kb, vb, lfb, of_ref, ob_ref, sf_ref, sb_ref, st):
        @pl.when(pl.program_id(0) == 0)
        def _():
            st[...] = jnp.zeros_like(st)

        ins = ((qf, kf, vf, lff), (qb, kb, vb, lfb))
        chains = [(d, h) for d in range(2) for h in range(n_heads)]
        ck = [slice(h * dk, (h + 1) * dk) for h in range(n_heads)]
        cv = [slice(h * dv, (h + 1) * dv) for h in range(n_heads)]
        qs = [ins[d][0][:, ck[h]] for d, h in chains]
        ks = [ins[d][1][:, ck[h]] for d, h in chains]
        vs = [ins[d][2][:, cv[h]] for d, h in chains]
        lfs = [ins[d][3][:, ck[h]] for d, h in chains]
        sts = [st[d, h] for d, h in chains]
        os_, st_new = _chunks_fn(qs, ks, vs, lfs, sts, [d == 1 for d, _ in chains])
        for i, (d, h) in enumerate(chains):
            (sf_ref, sb_ref)[d][h] = sts[i].astype(BF16)
            (of_ref, ob_ref)[d][:, cv[h]] = os_[i]
            st[d, h] = st_new[i]

    in_specs = [spec(wk, q[1], False), spec(wk, k_f[1], False), spec(wv, v[1], False), spec(wk, lf_f[1], False),
                spec(wk, q[1], True), spec(wk, k_b[1], True), spec(wv, v[1], True), spec(wk, lf_b[1], True)]
    return pl.pallas_call(
        body, name=name, grid=(n_chunks,), in_specs=in_specs,
        out_specs=[spec(wv, 0, False), spec(wv, 0, True), sspec(False), sspec(True)],
        out_shape=[jax.ShapeDtypeStruct((n_rows, wv), F32)] * 2
        + [jax.ShapeDtypeStruct((n_chunks, n_heads, dv, dk), BF16)] * 2,
        scratch_shapes=[pltpu.VMEM((2, n_heads, dv, dk), F32)],
        compiler_params=_params(("arbitrary",)),
    )(q[0], k_f[0], v[0], lf_f[0], q[0], k_b[0], v[0], lf_b[0])


def _attn_bwd(name, q, k_f, k_b, v, lf_f, lf_b, st_f, st_b, do, n_heads, dk, dv, out_dtype=F32):
    n_rows = q[0].shape[0]
    n_chunks = n_rows // CHUNK
    wk, wv = n_heads * dk, n_heads * dv

    def spec(width, off, rev):
        return pl.BlockSpec((CHUNK, width), lambda n: (n if rev else (n_chunks - 1 - n), off))

    def sspec(rev):
        return pl.BlockSpec((None, n_heads, dv, dk), lambda n: (n if rev else (n_chunks - 1 - n), 0, 0, 0))

    def body(qf, kf, vf, lff, sf, dof, qb, kb, vb, lfb, sb, dob,
             dqf, dkf, dvf, dlff, dqb, dkb, dvb, dlfb, dst):
        @pl.when(pl.program_id(0) == 0)
        def _():
            dst[...] = jnp.zeros_like(dst)

        ins = ((qf, kf, vf, lff, sf, dof), (qb, kb, vb, lfb, sb, dob))
        outs = ((dqf, dkf, dvf, dlff), (dqb, dkb, dvb, dlfb))
        chains = [(d, h) for d in range(2) for h in range(n_heads)]
        ck = [slice(h * dk, (h + 1) * dk) for h in range(n_heads)]
        cv = [slice(h * dv, (h + 1) * dv) for h in range(n_heads)]
        qs = [ins[d][0][:, ck[h]] for d, h in chains]
        ks = [ins[d][1][:, ck[h]] for d, h in chains]
        vs = [ins[d][2][:, cv[h]] for d, h in chains]
        lfs = [ins[d][3][:, ck[h]] for d, h in chains]
        sts = [ins[d][4][h].astype(F32) for d, h in chains]
        dos = [ins[d][5][:, cv[h]] for d, h in chains]
        dsts = [dst[d, h] for d, h in chains]
        fn = functools.partial(_chunks_fn, reverses=[d == 1 for d, _ in chains])
        _, vjp = jax.vjp(fn, qs, ks, vs, lfs, sts)
        dqs, dks, dvs, dlfs, dst_prev = vjp((dos, dsts))
        for i, (d, h) in enumerate(chains):
            dq_r, dk_r, dv_r, dlf_r = outs[d]
            dq_r[:, ck[h]] = dqs[i].astype(dq_r.dtype)
            dk_r[:, ck[h]] = dks[i].astype(dk_r.dtype)
            dv_r[:, cv[h]] = dvs[i].astype(dv_r.dtype)
            dlf_r[:, ck[h]] = dlfs[i].astype(dlf_r.dtype)
            dst[d, h] = dst_prev[i]

    def dir_specs(kk, lf, rev):
        return [spec(wk, q[1], rev), spec(wk, kk[1], rev), spec(wv, v[1], rev), spec(wk, lf[1], rev), sspec(rev),
                spec(wv, 0, rev)]

    def dir_out_specs(rev):
        return [spec(wk, 0, rev), spec(wk, 0, rev), spec(wv, 0, rev), spec(wk, 0, rev)]

    shapes = [jax.ShapeDtypeStruct((n_rows, wk), out_dtype), jax.ShapeDtypeStruct((n_rows, wk), out_dtype),
              jax.ShapeDtypeStruct((n_rows, wv), out_dtype), jax.ShapeDtypeStruct((n_rows, wk), F32)]
    outs = pl.pallas_call(
        body, name=name, grid=(n_chunks,), in_specs=dir_specs(k_f, lf_f, False) + dir_specs(k_b, lf_b, True),
        out_specs=dir_out_specs(False) + dir_out_specs(True), out_shape=shapes + shapes,
        scratch_shapes=[pltpu.VMEM((2, n_heads, dv, dk), F32)],
        compiler_params=_params(("arbitrary",)),
    )(q[0], k_f[0], v[0], lf_f[0], st_f, do, q[0], k_b[0], v[0], lf_b[0], st_b, do)
    return outs[:4], outs[4:]


def _row2(v):
    return v.reshape(1, -1)


def _mlp_fwd(tag, h, gain, w1, w2):
    y = _rowcall(f"{tag}_norm", _rmsnorm_fn, [(h, h.shape[1], 0)], [gain], [(h.shape[1], BF16)], tm=512)[0]
    hid = _mm(f"{tag}_up", y, w1, out_dtypes=(BF16,))
    h_out = _mm(f"{tag}_down", hid, w2, a_pro=_relu2, extras=(h,), epi=_add_epi)
    return h_out, (y, hid)


def _mlp_bwd(tag, h, gain, w1, w2, saved, dh_out):
    y, hid = saved
    dhid = _mm(f"{tag}_dact", dh_out, w2, mode="nt", extras=(hid,), epi=_relu2_bwd_epi, out_dtypes=(BF16,))
    dw2 = _mm(f"{tag}_dw2", hid, dh_out, mode="tn", a_pro=_relu2)
    dw1 = _mm(f"{tag}_dw1", y, dhid, mode="tn", out_split=N_CHIPS)
    dy = _mm(f"{tag}_dy", dhid, w1, mode="nt")
    dh, dgain = _norm_bwd(f"{tag}_dnorm", h, gain, dy, dh_out)
    return dh, dgain, dw1, dw2


def _norm_bwd(name, h, gain, dy, dres, pin=None):
    d = h.shape[1]

    def fn(h, dy, dres, gain):
        _, vjp = jax.vjp(lambda a, b: _rmsnorm_fn(a, b)[0], h, gain)
        dh, dgain = vjp(dy)
        return dh + dres, dgain

    dh, dgain = _rowcall(name, fn, [(h, d, 0), (dy, d, 0), (dres, d, 0)], [gain], [(d, F32)], [(1, d)], tm=512, pin=pin)
    return dh, dgain


def _local_step(x, target, w, pin=None, late=None, emit=None):
    g = {}
    d_model = x.shape[1]
    rg_w = hg_w = d_model // 2
    pins = []

    def send_off(tag, arrays):
        if emit is not None:
            pins.append(emit(tag, arrays))

    def chip_major(t):
        return t.reshape(N_CHIPS, t.shape[0] // N_CHIPS, t.shape[1])

    h_a0 = x
    gain = _row2(w["norm_mix"][0])
    y0 = _rowcall("l0_norm", _rmsnorm_fn, [(h_a0, d_model, 0)], [gain], [(d_model, BF16)], tm=512, pin=pin)[0]
    proj0 = _mm("l0_in", y0, w["ab_w_in"])
    conv_w, conv_b = w["rg_conv_w"], _row2(w["rg_conv_b"])
    xc = _conv_fwd("rg_conv", proj0, 0, conv_w, conv_b)
    gate_pars = [w["rg_wa_bd"], w["rg_wx_bd"], w["rg_b_a"], w["rg_b_x"], w["rg_lambda"]]
    a_f, u_f, a_b, u_b = _rowcall("rg_gates", _rg_gates_fn, [(xc, rg_w, 0)], gate_pars, [(rg_w, F32)] * 4)
    hs_f = _scan_fwd("rg_scan_f", a_f, u_f, False)
    hs_b = _scan_fwd("rg_scan_b", a_b, u_b, True)
    hg_rows = [(proj0, hg_w, 2), (proj0, hg_w, 3), (proj0, hg_w, 4)]
    qh, k_f, lf_f, k_b, lf_b = _rowcall("hg_pre", _hg_pre_fn, hg_rows, [w["hg_lb_logits"]], [(hg_w, F32)] * 5)
    iv = (proj0, 5)
    o_f, o_b, st_f, st_b = _attn_fwd("hg_attn", (qh, 0), (k_f, 0), (k_b, 0), iv, (lf_f, 0), (lf_b, 0), 4, 128, 128)
    post0_rows = [(hs_f, rg_w, 0), (hs_b, rg_w, 0), (proj0, rg_w, 1), (o_f, hg_w, 0), (o_b, hg_w, 0), (proj0, hg_w, 6)]
    hg_gain = _row2(w["hg_norm"])
    mix_in0 = _rowcall("l0_post", _post0_fwd_fn, post0_rows, [hg_gain], [(d_model, BF16)])[0]
    h_b0 = _mm("l0_out", mix_in0, w["ab_w_out"], extras=(h_a0,), epi=_add_epi)
    if late is not None:
        w = {**w, **late(h_b0)}
    h_c0, mlp0 = _mlp_fwd("mlp0", h_b0, _row2(w["norm_mlp"][0]), w["mlp_w1"][0], w["mlp_w2"][0])

    h_a1 = h_c0
    gain1 = _row2(w["norm_mix"][1])
    y1 = _rowcall("l1_norm", _rmsnorm_fn, [(h_a1, d_model, 0)], [gain1], [(d_model, BF16)], tm=512)[0]
    proj1 = _mm("l1_in", y1, w["gla_w_in_pad"], tn=640)
    gla_pars = [w["gla_w_up_pad"], w["gla_b_gate"]]
    gq, glf_f, glf_b = _rowcall("gla_pre", _gla_pre_fn, [(proj1, 512, 0), (proj1, LANES, 24)], gla_pars, [(512, F32)] * 3)
    gk, gv = (proj1, 1), (proj1, 1)
    go_f, go_b, gst_f, gst_b = _attn_fwd("gla_attn", (gq, 0), gk, gk, gv, (glf_f, 0), (glf_b, 0), 4, 128, 256)
    gla_gain = _row2(w["gla_norm"])
    post1_rows = [(go_f, d_model, 0), (go_b, d_model, 0), (proj1, d_model, 2)]
    mix_in1 = _rowcall("l1_post", _gla_post_fwd_fn, post1_rows, [gla_gain], [(d_model, BF16)])[0]
    h_b1 = _mm("l1_out", mix_in1, w["gla_w_out"], extras=(h_a1,), epi=_add_epi)
    h_c1, mlp1 = _mlp_fwd("mlp1", h_b1, _row2(w["norm_mlp"][1]), w["mlp_w1"][1], w["mlp_w2"][1])

    dh, loss, g["norm_final"] = _rowcall(
        "loss_head", _loss_head_fn, [(h_c1, d_model, 0), (target, d_model, 0)], [_row2(w["norm_final"])],
        [(d_model, F32)], [(1, LANES), (1, d_model)], tm=512)

    dh, g_nmlp1, g_w1_1, g_w2_1 = _mlp_bwd("mlp1", h_b1, _row2(w["norm_mlp"][1]), w["mlp_w1"][1], w["mlp_w2"][1], mlp1, dh)
    send_off("mlp1", [g_w1_1, chip_major(g_w2_1)])
    dmix1 = _mm("l1_dout", dh, w["gla_w_out"], mode="nt")
    g["gla_w_out"] = _mm("l1_dwout", mix_in1, dh, mode="tn")
    dgo, dr, g["gla_norm"] = _rowcall(
        "l1_dpost", _gla_post_bwd_fn, post1_rows + [(dmix1, d_model, 0)], [gla_gain],
        [(d_model, F32), (d_model, BF16)], [(1, d_model)], pin=pins.pop() if pins else None)
    (dq_f, dk_f, dv_f, dlf_f), (dq_b, dk_b, dv_b, dlf_b) = _attn_bwd(
        "gla_dattn", (gq, 0), gk, gk, gv, (glf_f, 0), (glf_b, 0), gst_f, gst_b, dgo, 4, 128, 256)

    def gla_pre_bwd(q, lr, dq1, dq2, dlf1, dlf2, dk1, dk2, dv1, dv2, w_up, b_gate):
        dlr = jnp.zeros_like(lr)
        dws, dbs = [], []
        for d, dlf in enumerate((dlf1, dlf2)):
            z = _raw_nn(lr, w_up[d]) + b_gate[d:d + 1]
            dz = dlf * _sigmoid(-z) * (1.0 / 16.0)
            dlr = dlr + _raw_nt(dz, w_up[d])
            dws.append(_raw_tn(dz, lr))
            dbs.append(jnp.sum(dz, axis=0, keepdims=True))
        return ((dq1 + dq2) * (128.0 ** -0.5), dk1 + dk2, dv1 + dv2, dlr, dws[0], dws[1], dbs[0], dbs[1])

    rows = [(proj1, 512, 0), (proj1, LANES, 24), (dq_f, 512, 0), (dq_b, 512, 0), (dlf_f, 512, 0), (dlf_b, 512, 0),
            (dk_f, 512, 0), (dk_b, 512, 0), (dv_f, d_model, 0), (dv_b, d_model, 0)]
    dq, dk, dv, dlr, dwt_f, dwt_b, db_f, db_b = _rowcall(
        "gla_dpre", gla_pre_bwd, rows, gla_pars, [(512, BF16), (512, BF16), (d_model, BF16), (LANES, BF16)],
        [(512, LANES), (512, LANES), (1, 512), (1, 512)])
    g["gla_w_up_pad"] = jnp.stack([dwt_f.T, dwt_b.T])
    g["gla_b_gate"] = jnp.concatenate([db_f, db_b], axis=0)
    dproj1 = jnp.concatenate([dq, dk, dv, dr, dlr], axis=1)
    g_gla_in = _mm("l1_dwin", y1, dproj1, mode="tn", tn=640)
    g["gla_w_in"] = _split_chips(g_gla_in[:, :GLA_IN_WIDTH], 1)
    send_off("gla", [g["gla_w_in"], chip_major(g["gla_w_out"])])
    dy1 = _mm("l1_dy", dproj1, w["gla_w_in_pad"], mode="nt", tk=640)
    dh, g_nmix1 = _norm_bwd("l1_dnorm", h_a1, gain1, dy1, dh, pin=pins.pop() if pins else None)

    dh, g_nmlp0, g_w1_0, g_w2_0 = _mlp_bwd("mlp0", h_b0, _row2(w["norm_mlp"][0]), w["mlp_w1"][0], w["mlp_w2"][0], mlp0, dh)
    send_off("mlp0", [g_w1_0, chip_major(g_w2_0)])
    dmix0 = _mm("l0_dout", dh, w["ab_w_out"], mode="nt")
    g["ab_w_out"] = _mm("l0_dwout", mix_in0, dh, mode="tn")
    dhs, dga, do, dg, g["hg_norm"] = _rowcall(
        "l0_dpost", _post0_bwd_fn, post0_rows + [(dmix0, d_model, 0)], [hg_gain],
        [(rg_w, F32), (rg_w, BF16), (hg_w, F32), (hg_w, BF16)], [(1, hg_w)], pin=pins.pop() if pins else None)
    (dqh_f, dk_f, div_f, dlf_f), (dqh_b, dk_b, div_b, dlf_b) = _attn_bwd(
        "hg_dattn", (qh, 0), (k_f, 0), (k_b, 0), iv, (lf_f, 0), (lf_b, 0), st_f, st_b, do, 4, 128, 128)

    def hg_pre_bwd(q, f_f, f_b, dq1, dq2, dk1, dlf1, dk2, dlf2, dv1, dv2, logits):
        _, vjp = jax.vjp(_hg_pre_fn, q, f_f, f_b, logits)
        dq, df_f, df_b, dlogits = vjp((dq1 + dq2, dk1, dlf1, dk2, dlf2))
        return dq, df_f, df_b, dv1 + dv2, dlogits

    rows = hg_rows + [(t, hg_w, 0) for t in (dqh_f, dqh_b, dk_f, dlf_f, dk_b, dlf_b, div_f, div_b)]
    dq, df_f, df_b, div, g["hg_lb_logits"] = _rowcall(
        "hg_dpre", hg_pre_bwd, rows, [w["hg_lb_logits"]], [(hg_w, BF16)] * 4, [(2, hg_w)])
    du_f, da_f = _scan_bwd("rg_dscan_f", a_f, hs_f, dhs, False)
    du_b, da_b = _scan_bwd("rg_dscan_b", a_b, hs_b, dhs, True)
    gates_bwd = _vjp_of(_rg_gates_fn, 1, 4, 5)
    rows = [(xc, rg_w, 0), (da_f, rg_w, 0), (du_f, rg_w, 0), (da_b, rg_w, 0), (du_b, rg_w, 0)]
    dxc, g["rg_wa_bd"], g["rg_wx_bd"], g["rg_b_a"], g["rg_b_x"], g["rg_lambda"] = _rowcall(
        "rg_dgates", gates_bwd, rows, gate_pars, [(rg_w, F32)],
        [(2, rg_w, rg_w), (2, rg_w, rg_w), (2, rg_w), (2, rg_w), (2, rg_w)])
    dxa, g["rg_conv_w"], g["rg_conv_b"] = _conv_bwd("rg_dconv", proj0, 0, conv_w, dxc)
    dproj0 = jnp.concatenate([dxa, dga, dq, df_f, df_b, div, dg], axis=1)
    g["ab_w_in"] = _mm("l0_dwin", y0, dproj0, mode="tn", out_split=N_CHIPS)
    dy0 = _mm("l0_dy", dproj0, w["ab_w_in"], mode="nt")
    grad_x, g_nmix0 = _norm_bwd("l0_dnorm", h_a0, gain, dy0, dh)

    g["norm_mix"] = jnp.concatenate([g_nmix0, g_nmix1], axis=0)
    g["norm_mlp"] = jnp.concatenate([g_nmlp0, g_nmlp1], axis=0)
    g["mlp_w1"] = [g_w1_0, g_w1_1]
    g["mlp_w2"] = [g_w2_0, g_w2_1]
    return loss, grad_x, g


def _block_diag(w):
    d, g, n, _ = w.shape
    eye = jnp.eye(g, dtype=w.dtype)
    return (w[:, :, :, None, :] * eye[None, :, None, :, None]).reshape(d, g * n, g * n)


def _block_diag_extract(wbd, g):
    d, gn, _ = wbd.shape
    n = gn // g
    blocks = wbd.reshape(d, g, n, g, n)
    return jnp.stack([blocks[:, i, :, i, :] for i in range(g)], axis=1)


def _prepare_weights(big, full):
    w = {k: full[k] for k in ("norm_mix", "norm_mlp", "norm_final", "hg_lb_logits")}
    for k in ("rg_conv_w", "rg_conv_b", "rg_b_a", "rg_b_x", "rg_lambda", "hg_norm", "gla_b_gate", "gla_norm"):
        w[k] = full[k][0]
    w["rg_wa_bd"] = _block_diag(full["rg_w_a"][0])
    w["rg_wx_bd"] = _block_diag(full["rg_w_x"][0])
    up = full["gla_w_gate_up"][0]
    rank = up.shape[1]
    pad = jnp.zeros((2, LANES, up.shape[2]), F32)
    w["gla_w_up_pad"] = pad.at[0, 0:rank].set(up[0]).at[1, rank:2 * rank].set(up[1])
    w.update(_prepare_matrices(big))
    return w


def _prepare_matrices(big):
    w = {}
    if "mlp_w1" in big:
        w["mlp_w1"] = list(big["mlp_w1"])
        w["mlp_w2"] = [t.reshape(-1, t.shape[-1]) for t in big["mlp_w2"]]
    if "ab_w_in" in big:
        w["ab_w_in"] = big["ab_w_in"]
        w["ab_w_out"] = big["ab_w_out"].reshape(-1, big["ab_w_out"].shape[-1])
    if "gla_w_in" in big:
        w["gla_w_out"] = big["gla_w_out"].reshape(-1, big["gla_w_out"].shape[-1])
        gla_in = _join_chips(big["gla_w_in"], 1)
        w["gla_w_in_pad"] = jnp.pad(gla_in, ((0, 0), (0, GLA_IN_PAD - gla_in.shape[1])))
    return w


def _finish_grads(g, rank=16, rg_blocks=8):
    def chip_major(t):
        return t.reshape(N_CHIPS, t.shape[0] // N_CHIPS, t.shape[1])

    big = {
        "mlp_w1": list(g["mlp_w1"]), "mlp_w2": [chip_major(t) for t in g["mlp_w2"]],
        "ab_w_in": g["ab_w_in"], "ab_w_out": chip_major(g["ab_w_out"]),
        "gla_w_in": g["gla_w_in"], "gla_w_out": chip_major(g["gla_w_out"]),
    }
    small = {
        "norm_mix": g["norm_mix"], "norm_mlp": g["norm_mlp"], "norm_final": g["norm_final"][0],
        "rg_conv_w": g["rg_conv_w"][None], "rg_conv_b": g["rg_conv_b"],
        "rg_w_a": _block_diag_extract(g["rg_wa_bd"], rg_blocks)[None], "rg_b_a": g["rg_b_a"][None],
        "rg_w_x": _block_diag_extract(g["rg_wx_bd"], rg_blocks)[None], "rg_b_x": g["rg_b_x"][None],
        "rg_lambda": g["rg_lambda"][None], "hg_lb_logits": g["hg_lb_logits"], "hg_norm": g["hg_norm"],
        "gla_w_gate_up": jnp.stack([g["gla_w_up_pad"][0, 0:rank], g["gla_w_up_pad"][1, rank:2 * rank]])[None],
        "gla_b_gate": g["gla_b_gate"][None], "gla_norm": g["gla_norm"],
    }
    return big, small


MATRICES = (("mlp_w1", 0), ("mlp_w1", 1), ("mlp_w2", 0), ("mlp_w2", 1), ("ab_w_in", 0), ("ab_w_out", 0),
            ("gla_w_in", 0), ("gla_w_out", 0))
EARLY_MATRICES = ("ab_w_in", "ab_w_out")
SMALL_SHARDED = ("rg_conv_w", "rg_b_a", "rg_b_x", "rg_lambda", "gla_w_gate_up", "gla_b_gate", "gla_norm")
SMALL_REPLICATED = ("norm_mix", "norm_mlp", "norm_final", "rg_conv_b", "rg_w_a", "rg_w_x", "hg_lb_logits", "hg_norm")
WEIGHTS = ("norm_mix", "norm_mlp", "norm_final", "mlp_w1", "mlp_w2", "ab_w_in", "ab_w_out", "rg_conv_w", "rg_conv_b",
           "rg_w_a", "rg_b_a", "rg_w_x", "rg_b_x", "rg_lambda", "hg_lb_logits", "hg_norm", "gla_w_in", "gla_w_out",
           "gla_w_gate_up", "gla_b_gate", "gla_norm")
ROW_ALIGN = 16


def _pack(arrays, lead=0):
    head = arrays[0].shape[:lead]
    flat = jnp.concatenate([a.reshape(head + (-1,)) for a in arrays], axis=lead)
    n = flat.shape[-1]
    quantum = LANES * ROW_ALIGN
    padded = -(-n // quantum) * quantum
    if padded != n:
        flat = jnp.pad(flat, [(0, 0)] * lead + [(0, padded - n)])
    return flat.reshape(head + (padded // LANES, LANES))


def _unpack(buf, shapes, lead=0):
    head = buf.shape[:lead]
    flat = buf.reshape(head + (-1,))
    out, off = [], 0
    for s in shapes:
        n = 1
        for v in s:
            n *= v
        out.append(lax.slice_in_dim(flat, off, off + n, axis=lead).reshape(head + tuple(s)))
        off += n
    return out


def _join_chips(gathered, axis):
    t = jnp.moveaxis(gathered, 0, axis)
    return t.reshape(t.shape[:axis] + (t.shape[axis] * t.shape[axis + 1],) + t.shape[axis + 2:])


def _split_chips(full, axis):
    s = full.shape
    t = full.reshape(s[:axis] + (N_CHIPS, s[axis] // N_CHIPS) + s[axis + 1:])
    return jnp.moveaxis(t, axis, 0)


_ANY = pl.BlockSpec(memory_space=pl.ANY)


def _place():
    return lax.axis_index("x"), lax.axis_index("y"), lax.axis_index("c")


def _into_slot(name, src, slot, n_slots, dtype, tm, layer=None):
    r, lanes = src.shape[-2:]
    tm = _row_tile(r, tm, ROW_ALIGN)

    def body(slot_ref, in_ref, o_ref):
        o_ref[...] = in_ref[...].astype(o_ref.dtype)

    if layer is None:
        in_spec = pl.BlockSpec((tm, lanes), lambda i, slot_ref: (i, 0))
    else:
        in_spec = pl.BlockSpec((None, tm, lanes), lambda i, slot_ref: (layer, i, 0))
    grid_spec = pltpu.PrefetchScalarGridSpec(
        num_scalar_prefetch=1, grid=(r // tm,), in_specs=[in_spec],
        out_specs=pl.BlockSpec((None, tm, lanes), lambda i, slot_ref: (slot_ref[0], i, 0)))
    return pl.pallas_call(
        body, name=name, grid_spec=grid_spec, out_shape=jax.ShapeDtypeStruct((n_slots, r, lanes), dtype),
        compiler_params=_params(("parallel",)),
    )(slot.reshape(1).astype(jnp.int32), src)


def _chip_peers():
    x, y, c = _place()
    return 2 * x + y, c, [(1 - x, y), (x, 1 - y), (1 - x, 1 - y)]


def _comm_call(name, body, ins, out_shapes, n_sems, aliases=None):
    return pl.pallas_call(
        body, name=name, in_specs=[_ANY] * len(ins), out_specs=[_ANY] * len(out_shapes), out_shape=out_shapes,
        input_output_aliases=aliases or {},
        scratch_shapes=[pltpu.SemaphoreType.DMA((n_sems,)), pltpu.SemaphoreType.DMA((n_sems,))],
    )(*ins)


def _gather_chips(name, bufs):
    n = len(bufs)

    def body(*refs):
        outs, send_sems, recv_sems = refs[n:2 * n], refs[2 * n], refs[2 * n + 1]
        x, y, c = _place()
        me, _, peers = _chip_peers()

        def rows(a, block, half):
            rh = outs[a].shape[1] // 2
            return outs[a].at[block, pl.ds(half * rh, rh)]

        def copy(a, j, block, half, to, sem):
            return pltpu.make_async_remote_copy(
                src_ref=rows(a, block, half), dst_ref=rows(a, block, half), send_sem=send_sems.at[sem],
                recv_sem=recv_sems.at[sem], device_id=to, device_id_type=MESH)

        def over_ici(a, j, block):
            px, py = peers[j]
            return copy(a, j, block, c, (px, py, c), 6 * a + j)

        def to_sibling(a, j, block, half):
            return copy(a, j, block, half, (x, y, 1 - c), 6 * a + 3 + j)

        sends = [over_ici(a, j, me) for a in range(n) for j in range(3)]
        for cp in sends:
            cp.start()
        for a in range(n):
            for j, (px, py) in enumerate(peers):
                over_ici(a, j, 2 * px + py).wait_recv()
                handed = to_sibling(a, j, 2 * px + py, c)
                handed.start()
                sends.append(handed)
        for a in range(n):
            for j, (px, py) in enumerate(peers):
                to_sibling(a, j, 2 * px + py, 1 - c).wait_recv()
        for cp in sends:
            cp.wait_send()

    shapes = [jax.ShapeDtypeStruct(b.shape, b.dtype) for b in bufs]
    return _comm_call(name, body, bufs, shapes, 6 * n, {a: a for a in range(n)})


_HBM = pl.BlockSpec(memory_space=pltpu.HBM)
_SEM = pl.BlockSpec(memory_space=pltpu.SEMAPHORE)
_EFFECT = pltpu.SideEffectType.DATAFLOW_SIDE_EFFECTING


def _half_rows(ref, block, half):
    rh = ref.shape[1] // 2
    return ref.at[block, pl.ds(half * rh, rh)]


def _gather_start(name, bufs, after):
    n = len(bufs)

    def body(*refs):
        ins, send_sems, recv_sems, token = refs[:n], refs[n + 1], refs[n + 2], refs[-1]
        me, c, peers = _chip_peers()
        for a in range(n):
            mine = _half_rows(ins[a], me, c)
            for j, (px, py) in enumerate(peers):
                pltpu.make_async_remote_copy(
                    src_ref=mine, dst_ref=mine, send_sem=send_sems.at[3 * a + j], recv_sem=recv_sems.at[3 * a + j],
                    device_id=(px, py, c), device_id_type=MESH).start()
        token[...] = jnp.zeros_like(token)

    out_shape = (pltpu.SemaphoreType.DMA((3 * n,)), pltpu.SemaphoreType.DMA((3 * n,)),
                 *[pltpu.HBM(b.shape, b.dtype) for b in bufs], jax.ShapeDtypeStruct((8, LANES), F32))
    return pl.pallas_call(
        body, name=name, out_shape=out_shape, in_specs=[_HBM] * n + [_ANY],
        out_specs=(_SEM, _SEM, *[_HBM] * n, pl.BlockSpec(memory_space=pltpu.VMEM)),
        input_output_aliases={a: 2 + a for a in range(n)},
        compiler_params=pltpu.CompilerParams(has_side_effects=_EFFECT),
    )(*[pltpu.with_memory_space_constraint(b, pltpu.HBM) for b in bufs], after)


def _gather_wait(name, bufs, send_sems, recv_sems, after):
    n = len(bufs)

    def body(*refs):
        ins, send_sems, recv_sems = refs[:n], refs[n], refs[n + 1]
        me, c, peers = _chip_peers()
        for a in range(n):
            for j, (px, py) in enumerate(peers):
                copy = pltpu.make_async_remote_copy(
                    src_ref=_half_rows(ins[a], me, c), dst_ref=_half_rows(ins[a], 2 * px + py, c),
                    send_sem=send_sems.at[3 * a + j], recv_sem=recv_sems.at[3 * a + j],
                    device_id=(px, py, c), device_id_type=MESH)
                copy.wait_send()
                copy.wait_recv()

    return pl.pallas_call(
        body, name=name, out_shape=tuple(pltpu.HBM(b.shape, b.dtype) for b in bufs),
        in_specs=[_HBM] * n + [_SEM, _SEM, _ANY], out_specs=tuple([_HBM] * n),
        input_output_aliases={a: a for a in range(n)},
        compiler_params=pltpu.CompilerParams(has_side_effects=_EFFECT),
    )(*bufs, send_sems, recv_sems, after)


def _hand_over(name, bufs):
    n = len(bufs)

    def body(*refs):
        outs, send_sems, recv_sems = refs[n:2 * n], refs[2 * n], refs[2 * n + 1]
        x, y, c = _place()
        _, _, peers = _chip_peers()

        def copy(a, j, half):
            px, py = peers[j]
            rows = _half_rows(outs[a], 2 * px + py, half)
            return pltpu.make_async_remote_copy(
                src_ref=rows, dst_ref=rows, send_sem=send_sems.at[3 * a + j], recv_sem=recv_sems.at[3 * a + j],
                device_id=(x, y, 1 - c), device_id_type=MESH)

        sends = [copy(a, j, c) for a in range(n) for j in range(3)]
        for cp in sends:
            cp.start()
        for a in range(n):
            for j in range(3):
                copy(a, j, 1 - c).wait_recv()
        for cp in sends:
            cp.wait_send()

    shapes = [jax.ShapeDtypeStruct(b.shape, b.dtype) for b in bufs]
    return _comm_call(name, body, bufs, shapes, 3 * n, {a: a for a in range(n)})


def _pair_exchange(name, gs):
    n = len(gs)

    def body(*refs):
        ins, outs, send_sems, recv_sems = refs[:n], refs[n:2 * n], refs[2 * n], refs[2 * n + 1]
        x, y, c = _place()
        copies = [pltpu.make_async_remote_copy(
            src_ref=ins[a].at[:, 1 - c], dst_ref=outs[a], send_sem=send_sems.at[a], recv_sem=recv_sems.at[a],
            device_id=(x, y, 1 - c), device_id_type=MESH) for a in range(n)]
        for cp in copies:
            cp.start()
        for cp in copies:
            cp.wait()

    shapes = [jax.ShapeDtypeStruct((g.shape[0],) + g.shape[2:], g.dtype) for g in gs]
    return _comm_call(name, body, gs, shapes, n)


def _pair_add(name, g, got, chip, core):
    n, _, rh, lanes = g.shape
    tm = _row_tile(rh, 1024, ROW_ALIGN)

    def body(idx_ref, g_ref, got_ref, own_ref, o16_ref):
        s = g_ref[...] + got_ref[...]
        o16_ref[...] = s.astype(BF16)

        @pl.when(pl.program_id(1) == idx_ref[1])
        def _():
            own_ref[...] = s

    grid_spec = pltpu.PrefetchScalarGridSpec(
        num_scalar_prefetch=1, grid=(rh // tm, n),
        in_specs=[pl.BlockSpec((None, None, tm, lanes), lambda i, s, idx_ref: (s, idx_ref[0], i, 0)),
                  pl.BlockSpec((None, tm, lanes), lambda i, s, idx_ref: (s, i, 0))],
        out_specs=[pl.BlockSpec((tm, lanes), lambda i, s, idx_ref: (i, 0)),
                   pl.BlockSpec((None, tm, lanes), lambda i, s, idx_ref: (s, i, 0))])
    return pl.pallas_call(
        body, name=name, grid_spec=grid_spec,
        out_shape=[jax.ShapeDtypeStruct((rh, lanes), F32), jax.ShapeDtypeStruct((n, rh, lanes), BF16)],
        compiler_params=_params(("parallel", "arbitrary")),
    )(jnp.stack([core, chip]).astype(jnp.int32), g, got)


def _chip_scatter(name, ps):
    n = len(ps)

    def body(*refs):
        ins, outs, send_sems, recv_sems = refs[:n], refs[n:2 * n], refs[2 * n], refs[2 * n + 1]
        me, c, peers = _chip_peers()

        def copy(a, j, src_block, dst_block):
            px, py = peers[j]
            return pltpu.make_async_remote_copy(
                src_ref=ins[a].at[src_block], dst_ref=outs[a].at[dst_block], send_sem=send_sems.at[3 * a + j],
                recv_sem=recv_sems.at[3 * a + j], device_id=(px, py, c), device_id_type=MESH)

        sends = [copy(a, j, 2 * px + py, me) for a in range(n) for j, (px, py) in enumerate(peers)]
        for cp in sends:
            cp.start()
        for a in range(n):
            for j, (px, py) in enumerate(peers):
                copy(a, j, me, 2 * px + py).wait_recv()
        for cp in sends:
            cp.wait_send()

    shapes = [jax.ShapeDtypeStruct(p.shape, p.dtype) for p in ps]
    return _comm_call(name, body, ps, shapes, 3 * n)


def _scatter_start(name, ps, lands):
    n = len(ps)

    def body(*refs):
        srcs, dsts, send_sems, recv_sems, token = refs[:n], refs[n:2 * n], refs[2 * n], refs[2 * n + 1], refs[-1]
        me, c, peers = _chip_peers()
        for a in range(n):
            for j, (px, py) in enumerate(peers):
                pltpu.make_async_remote_copy(
                    src_ref=srcs[a].at[2 * px + py], dst_ref=dsts[a].at[me], send_sem=send_sems.at[3 * a + j],
                    recv_sem=recv_sems.at[3 * a + j], device_id=(px, py, c), device_id_type=MESH).start()
        token[...] = jnp.zeros_like(token)

    bufs = list(ps) + list(lands)
    out_shape = (pltpu.SemaphoreType.DMA((3 * n,)), pltpu.SemaphoreType.DMA((3 * n,)),
                 *[pltpu.HBM(b.shape, b.dtype) for b in bufs], jax.ShapeDtypeStruct((8, LANES), F32))
    return pl.pallas_call(
        body, name=name, out_shape=out_shape, in_specs=[_HBM] * (2 * n),
        out_specs=(_SEM, _SEM, *[_HBM] * (2 * n), pl.BlockSpec(memory_space=pltpu.VMEM)),
        input_output_aliases={a: 2 + a for a in range(2 * n)},
        compiler_params=pltpu.CompilerParams(has_side_effects=_EFFECT),
    )(*[pltpu.with_memory_space_constraint(b, pltpu.HBM) for b in bufs])


def _scatter_wait(name, ps, lands, send_sems, recv_sems, after):
    n = len(ps)

    def body(*refs):
        srcs, dsts, send_sems, recv_sems = refs[:n], refs[n:2 * n], refs[2 * n], refs[2 * n + 1]
        me, c, peers = _chip_peers()
        for a in range(n):
            for j, (px, py) in enumerate(peers):
                copy = pltpu.make_async_remote_copy(
                    src_ref=srcs[a].at[2 * px + py], dst_ref=dsts[a].at[2 * px + py],
                    send_sem=send_sems.at[3 * a + j], recv_sem=recv_sems.at[3 * a + j],
                    device_id=(px, py, c), device_id_type=MESH)
                copy.wait_send()
                copy.wait_recv()

    bufs = list(ps) + list(lands)
    outs = pl.pallas_call(
        body, name=name, out_shape=tuple(pltpu.HBM(b.shape, b.dtype) for b in bufs),
        in_specs=[_HBM] * (2 * n) + [_SEM, _SEM, _ANY], out_specs=tuple([_HBM] * (2 * n)),
        input_output_aliases={a: a for a in range(2 * n)},
        compiler_params=pltpu.CompilerParams(has_side_effects=_EFFECT),
    )(*bufs, send_sems, recv_sems, after)
    return list(outs[n:])


def _sum_ring(name, own, got, chip, core):
    n, rh, lanes = got.shape
    tm = _row_tile(rh, 2048, ROW_ALIGN)

    def body(idx_ref, own_ref, g1_ref, g2_ref, g3_ref, o_ref):
        o_ref[...] = ((own_ref[...] + g1_ref[...].astype(F32)) + g2_ref[...].astype(F32)) + g3_ref[...].astype(F32)

    def block(k):
        return pl.BlockSpec((None, tm, lanes), lambda i, idx_ref: ((idx_ref[0] + k) % n, i, 0))

    grid_spec = pltpu.PrefetchScalarGridSpec(
        num_scalar_prefetch=1, grid=(rh // tm,),
        in_specs=[pl.BlockSpec((tm, lanes), lambda i, idx_ref: (i, 0)), block(1), block(2), block(3)],
        out_specs=pl.BlockSpec((None, tm, lanes), lambda i, idx_ref: (idx_ref[1], i, 0)))
    return pl.pallas_call(
        body, name=name, grid_spec=grid_spec, out_shape=jax.ShapeDtypeStruct((2, rh, lanes), F32),
        compiler_params=_params(("parallel",)),
    )(jnp.stack([chip, core]).astype(jnp.int32), own, got, got, got)


def _pair_gather(name, bufs):
    n = len(bufs)

    def body(*refs):
        ins, outs, send_sems, recv_sems = refs[:n], refs[n:2 * n], refs[2 * n], refs[2 * n + 1]
        x, y, c = _place()

        def copy(a, block):
            return pltpu.make_async_remote_copy(
                src_ref=ins[a].at[block], dst_ref=outs[a].at[block], send_sem=send_sems.at[a],
                recv_sem=recv_sems.at[a], device_id=(x, y, 1 - c), device_id_type=MESH)

        sends = [copy(a, c) for a in range(n)]
        for cp in sends:
            cp.start()
        for a in range(n):
            copy(a, 1 - c).wait_recv()
        for cp in sends:
            cp.wait_send()

    shapes = [jax.ShapeDtypeStruct(b.shape, b.dtype) for b in bufs]
    return _comm_call(name, body, bufs, shapes, n, {a: a for a in range(n)})


def _gather_all(name, s):
    def body(in_ref, out_ref, send_sems, recv_sems, local_sem):
        x, y, c = _place()
        me = 4 * x + 2 * y + c
        peers = []
        for mask in range(1, N_DEV):
            fx, fy, fc = (mask >> 2) & 1, (mask >> 1) & 1, mask & 1
            peers.append((jnp.where(fx, 1 - x, x), jnp.where(fy, 1 - y, y), jnp.where(fc, 1 - c, c)))

        def copy(j, block):
            return pltpu.make_async_remote_copy(
                src_ref=in_ref, dst_ref=out_ref.at[block], send_sem=send_sems.at[j], recv_sem=recv_sems.at[j],
                device_id=peers[j], device_id_type=MESH)

        local = pltpu.make_async_copy(in_ref, out_ref.at[me], local_sem)
        local.start()
        sends = [copy(j, me) for j in range(N_DEV - 1)]
        for cp in sends:
            cp.start()
        for j, (px, py, pc) in enumerate(peers):
            copy(j, 4 * px + 2 * py + pc).wait_recv()
        for cp in sends:
            cp.wait_send()
        local.wait()

    return pl.pallas_call(
        body, name=name, in_specs=[_ANY], out_specs=_ANY,
        out_shape=jax.ShapeDtypeStruct((N_DEV,) + s.shape, s.dtype),
        scratch_shapes=[pltpu.SemaphoreType.DMA((N_DEV - 1,)), pltpu.SemaphoreType.DMA((N_DEV - 1,)),
                        pltpu.SemaphoreType.DMA],
    )(s)


def _sum_blocks(name, stacked, tm):
    n, r, lanes = stacked.shape

    def body(in_ref, o_ref):
        acc = in_ref[0]
        for j in range(1, n):
            acc = acc + in_ref[j]
        o_ref[...] = acc

    return pl.pallas_call(
        body, name=name, grid=(r // tm,), in_specs=[pl.BlockSpec((n, tm, lanes), lambda i: (0, i, 0))],
        out_specs=pl.BlockSpec((tm, lanes), lambda i: (i, 0)), out_shape=jax.ShapeDtypeStruct((r, lanes), F32),
        compiler_params=_params(("parallel",)),
    )(stacked)


def _row_tile(rows, pref, align):
    best = None
    for t in range(align, min(rows, pref) + 1, align):
        if rows % t == 0:
            best = t
    assert best is not None, (rows, pref, align)
    return best


def _adam(name, w, g, m, v):
    rows, width = w.shape
    tm = _row_tile(rows, max(8, 4096 * LANES // width), 8)
    args = [(t, width, 0) for t in (w, g, m, v)]
    return _rowcall(name, _adam_fn, args, [], [(width, F32)] * 3, tm=tm)


def kernel(x, norm_mix, norm_mlp, norm_final, mlp_w1, mlp_w2, ab_w_in, ab_w_out, rg_conv_w, rg_conv_b, rg_w_a, rg_b_a, rg_w_x, rg_b_x, rg_lambda, hg_lb_logits, hg_norm, gla_w_in, gla_w_out, gla_w_gate_up, gla_b_gate, gla_norm, loss_target, m_norm_mix, m_norm_mlp, m_norm_final, m_mlp_w1, m_mlp_w2, m_ab_w_in, m_ab_w_out, m_rg_conv_w, m_rg_conv_b, m_rg_w_a, m_rg_b_a, m_rg_w_x, m_rg_b_x, m_rg_lambda, m_hg_lb_logits, m_hg_norm, m_gla_w_in, m_gla_w_out, m_gla_w_gate_up, m_gla_b_gate, m_gla_norm, v_norm_mix, v_norm_mlp, v_norm_final, v_mlp_w1, v_mlp_w2, v_ab_w_in, v_ab_w_out, v_rg_conv_w, v_rg_conv_b, v_rg_w_a, v_rg_b_a, v_rg_w_x, v_rg_b_x, v_rg_lambda, v_hg_lb_logits, v_hg_norm, v_gla_w_in, v_gla_w_out, v_gla_w_gate_up, v_gla_b_gate, v_gla_norm):
    w = dict(norm_mix=norm_mix, norm_mlp=norm_mlp, norm_final=norm_final, mlp_w1=mlp_w1, mlp_w2=mlp_w2, ab_w_in=ab_w_in, ab_w_out=ab_w_out, rg_conv_w=rg_conv_w, rg_conv_b=rg_conv_b, rg_w_a=rg_w_a, rg_b_a=rg_b_a, rg_w_x=rg_w_x, rg_b_x=rg_b_x, rg_lambda=rg_lambda, hg_lb_logits=hg_lb_logits, hg_norm=hg_norm, gla_w_in=gla_w_in, gla_w_out=gla_w_out, gla_w_gate_up=gla_w_gate_up, gla_b_gate=gla_b_gate, gla_norm=gla_norm)
    m = dict(norm_mix=m_norm_mix, norm_mlp=m_norm_mlp, norm_final=m_norm_final, mlp_w1=m_mlp_w1, mlp_w2=m_mlp_w2, ab_w_in=m_ab_w_in, ab_w_out=m_ab_w_out, rg_conv_w=m_rg_conv_w, rg_conv_b=m_rg_conv_b, rg_w_a=m_rg_w_a, rg_b_a=m_rg_b_a, rg_w_x=m_rg_w_x, rg_b_x=m_rg_b_x, rg_lambda=m_rg_lambda, hg_lb_logits=m_hg_lb_logits, hg_norm=m_hg_norm, gla_w_in=m_gla_w_in, gla_w_out=m_gla_w_out, gla_w_gate_up=m_gla_w_gate_up, gla_b_gate=m_gla_b_gate, gla_norm=m_gla_norm)
    v = dict(norm_mix=v_norm_mix, norm_mlp=v_norm_mlp, norm_final=v_norm_final, mlp_w1=v_mlp_w1, mlp_w2=v_mlp_w2, ab_w_in=v_ab_w_in, ab_w_out=v_ab_w_out, rg_conv_w=v_rg_conv_w, rg_conv_b=v_rg_conv_b, rg_w_a=v_rg_w_a, rg_b_a=v_rg_b_a, rg_w_x=v_rg_w_x, rg_b_x=v_rg_b_x, rg_lambda=v_rg_lambda, hg_lb_logits=v_hg_lb_logits, hg_norm=v_hg_norm, gla_w_in=v_gla_w_in, gla_w_out=v_gla_w_out, gla_w_gate_up=v_gla_w_gate_up, gla_b_gate=v_gla_b_gate, gla_norm=v_gla_norm)
    chip = 2 * lax.axis_index("x") + lax.axis_index("y")
    core = lax.axis_index("c")
    sharded_shapes = [w[n].shape for n in SMALL_SHARDED]

    slots = [_into_slot(f"cast_{n}{layer}", w[n], chip, N_CHIPS, BF16, 512, layer) for n, layer in MATRICES]
    early = [i for i, (n, _) in enumerate(MATRICES) if n in EARLY_MATRICES]
    rest = [i for i in range(len(MATRICES)) if i not in early]

    def named(indices, arrays):
        big = {}
        for i, t in zip(indices, arrays):
            big.setdefault(MATRICES[i][0], []).append(t)
        return {n: (v if n in ("mlp_w1", "mlp_w2") else v[0]) for n, v in big.items()}

    gathered = _gather_chips("gather_early", [slots[i] for i in early])
    send_sems, recv_sems, *in_flight, token = _gather_start("gather_rest_start", [slots[i] for i in rest], gathered[0])

    def late_weights(after):
        landed = _gather_wait("gather_rest_wait", in_flight, send_sems, recv_sems, after)
        return _prepare_matrices(named(rest, _hand_over("gather_rest_share", list(landed))))

    big = named(early, gathered)
    vectors = _pack([w[n] for n in SMALL_SHARDED])
    vectors = _into_slot("place_vectors", vectors, chip, N_CHIPS, F32, vectors.shape[0])
    small_all = _unpack(_gather_chips("gather_vectors", [vectors])[0], sharded_shapes, lead=1)
    full = {n: w[n] for n in SMALL_REPLICATED}
    for n, t in zip(SMALL_SHARDED, small_all):
        full[n] = _join_chips(t, t.ndim - 2)

    def pair_sums(tag, arrays):
        halves = [t.reshape(N_CHIPS, 2, t.shape[1] // 2, t.shape[2]) for t in arrays]
        from_sibling = _pair_exchange(f"reduce_pair_{tag}", halves)
        return [_pair_add(f"reduce_pair_add_{tag}{i}", h, s, chip, core)
                for i, (h, s) in enumerate(zip(halves, from_sibling))]

    in_flight_grads = {}

    def emit(tag, arrays):
        parts = pair_sums(tag, arrays)
        p16 = [p for _, p in parts]
        send, recv, *rest = _scatter_start(f"reduce_chips_{tag}_start", p16, [lax.empty(p.shape, p.dtype) for p in p16])
        in_flight_grads[tag] = ([p for p, _ in parts], rest[:len(p16)], rest[len(p16):-1], send, recv)
        return rest[-1]

    loss_part, grad_x, g_kernel = _local_step(
        x[0], loss_target[0], _prepare_weights(big, full), token, late_weights, emit)
    g_big, g_full = _finish_grads(g_kernel)
    loss = lax.psum(loss_part[0, 0], ("x", "y", "c"))

    mine = {}
    last = pair_sums("ab", [g_big["ab_w_in"], g_big["ab_w_out"]])
    from_chips = _chip_scatter("reduce_chips_ab", [p for _, p in last])
    mine["ab"] = [_sum_ring(f"reduce_chips_add_ab{i}", p32, f, chip, core)
                  for i, ((p32, _), f) in enumerate(zip(last, from_chips))]
    for tag, (p32s, p16s, lands, send, recv) in in_flight_grads.items():
        landed = _scatter_wait(f"reduce_chips_{tag}_wait", p16s, lands, send, recv, mine["ab"][0])
        mine[tag] = [_sum_ring(f"reduce_chips_add_{tag}{i}", p32, f, chip, core)
                     for i, (p32, f) in enumerate(zip(p32s, landed))]
    ordered = [mine["mlp0"][0], mine["mlp1"][0], mine["mlp0"][1], mine["mlp1"][1], *mine["ab"], *mine["gla"]]
    reduced = [t.reshape(2 * t.shape[1], t.shape[2]) for t in _pair_gather("reduce_share", ordered)]
    by_name = {n: [] for n, _ in MATRICES}
    for (n, _), t in zip(MATRICES, reduced):
        by_name[n].append(t)
    grads = {n: jnp.stack(v) for n, v in by_name.items()}

    small_names = SMALL_REPLICATED + SMALL_SHARDED
    g_small = _pack([g_full[n] for n in small_names])
    g_small_all = _gather_all("reduce_small", g_small)
    g_small_red = _sum_blocks("reduce_small_add", g_small_all, g_small.shape[0])
    g_small_full = dict(zip(small_names, _unpack(g_small_red, [g_full[n].shape for n in small_names])))
    for n in SMALL_REPLICATED:
        grads[n] = g_small_full[n]
    for n in SMALL_SHARDED:
        width = w[n].shape[-1]
        grads[n] = lax.dynamic_slice_in_dim(g_small_full[n], chip * width, width, axis=g_small_full[n].ndim - 1)

    delta, new_m, new_v = {}, {}, {}
    for n in by_name:
        flat = [t.reshape(-1, t.shape[-1]) for t in (w[n], grads[n], m[n], v[n])]
        for dst, t in zip((delta, new_m, new_v), _adam(f"adam_{n}", *flat)):
            dst[n] = t.reshape(w[n].shape)
    small_shapes = [w[n].shape for n in small_names]
    packs = [_pack([src[n] for n in small_names]) for src in (w, grads, m, v)]
    d_small, m_small, v_small = _adam("adam_small", *packs)
    for dst, buf in ((delta, d_small), (new_m, m_small), (new_v, v_small)):
        dst.update(zip(small_names, _unpack(buf, small_shapes)))

    return (loss, grad_x[None], *[grads[n] for n in WEIGHTS], *[delta[n] for n in WEIGHTS],
            *[new_m[n] for n in WEIGHTS], *[new_v[n] for n in WEIGHTS])
```

```python
import functools

import jax
import jax.numpy as jnp
from jax import lax
from jax.experimental import pallas as pl
from jax.experimental.pallas import tpu as pltpu

F32 = jnp.float32
BF16 = jnp.bfloat16
MESH = pl.DeviceIdType.MESH

LANES = 128
CHUNK = 64
EPS = 1e-6
RG_C = 8.0
N_CHIPS = 4
N_DEV = 8
GLA_IN_WIDTH = 3104
GLA_IN_PAD = 3200
VMEM_LIMIT = 56 * 1024 * 1024

ADAM_LR = 0.001
ADAM_B1 = 0.9
ADAM_B2 = 0.999
ADAM_EPS = 1e-08
ADAM_WD = 0.01
ADAM_STEP = 10


def _raw_dot(a, b, ca, cb):
    return lax.dot_general(a.astype(BF16), b.astype(BF16), (((ca,), (cb,)), ((), ())),
                           preferred_element_type=F32)


def _raw_nn(a, b):
    return _raw_dot(a, b, 1, 0)


def _raw_nt(a, b):
    return _raw_dot(a, b, 1, 1)


def _raw_tn(a, b):
    return _raw_dot(a, b, 0, 0)


@jax.custom_vjp
def _dot_nn(a, b):
    return _raw_nn(a, b)


def _dot_nn_fwd(a, b):
    return _raw_nn(a, b), (a, b)


def _dot_nn_bwd(res, g):
    a, b = res
    return _raw_nt(g, b), _raw_tn(a, g)


_dot_nn.defvjp(_dot_nn_fwd, _dot_nn_bwd)


@jax.custom_vjp
def _dot_nt(a, b):
    return _raw_nt(a, b)


def _dot_nt_fwd(a, b):
    return _raw_nt(a, b), (a, b)


def _dot_nt_bwd(res, g):
    a, b = res
    return _raw_nn(g, b), _raw_tn(g, a)


_dot_nt.defvjp(_dot_nt_fwd, _dot_nt_bwd)


@jax.custom_vjp
def _dot_tn(a, b):
    return _raw_tn(a, b)


def _dot_tn_fwd(a, b):
    return _raw_tn(a, b), (a, b)


def _dot_tn_bwd(res, g):
    a, b = res
    return _raw_nt(b, g), _raw_nn(a, g)


_dot_tn.defvjp(_dot_tn_fwd, _dot_tn_bwd)


def _tile(n, pref):
    if n <= pref:
        return n
    t = (pref // LANES) * LANES
    while t > LANES and n % t:
        t -= LANES
    assert n % t == 0, (n, pref)
    return t


def _params(sem):
    return pltpu.CompilerParams(dimension_semantics=sem, vmem_limit_bytes=VMEM_LIMIT)


def _rowcall(name, fn, rows, pars, row_outs, par_outs=(), tm=256, pin=None):
    if pin is not None:
        inner, pars = fn, list(pars) + [pin]
        fn = lambda *vals: inner(*vals[:-1])
    n_rows = rows[0][0].shape[0]
    tm = min(tm, n_rows)
    assert n_rows % tm == 0
    n_r, n_p, n_ro = len(rows), len(pars), len(row_outs)

    def body(*refs):
        vals = [r[...].astype(F32) for r in refs[:n_r + n_p]]
        outs = fn(*vals)
        o_refs = refs[n_r + n_p:n_r + n_p + n_ro]
        po_refs = refs[n_r + n_p + n_ro:]
        for o_ref, val in zip(o_refs, outs[:n_ro]):
            o_ref[...] = val.astype(o_ref.dtype)
        first = pl.program_id(0) == 0
        for po_ref, val in zip(po_refs, outs[n_ro:]):
            @pl.when(first)
            def _():
                po_ref[...] = val

            @pl.when(jnp.logical_not(first))
            def _():
                po_ref[...] += val

    def const_map(nd):
        return lambda i: (0,) * nd

    def row_spec(w, cb):
        return pl.BlockSpec((tm, w), lambda i: (i, cb))

    in_specs = [row_spec(w, cb) for _, w, cb in rows]
    in_specs += [pl.BlockSpec(p.shape, const_map(p.ndim)) for p in pars]
    out_specs = [pl.BlockSpec((tm, w), lambda i: (i, 0)) for w, _ in row_outs]
    out_specs += [pl.BlockSpec(tuple(s), const_map(len(s))) for s in par_outs]
    out_shape = [jax.ShapeDtypeStruct((n_rows, w), dt) for w, dt in row_outs]
    out_shape += [jax.ShapeDtypeStruct(tuple(s), F32) for s in par_outs]
    return pl.pallas_call(
        body, name=name, grid=(n_rows // tm,), in_specs=in_specs, out_specs=out_specs, out_shape=out_shape,
        compiler_params=_params(("arbitrary",) if par_outs else ("parallel",)),
    )(*[r[0] for r in rows], *pars)


def _vjp_of(fn, n_prim, n_out, n_par, n_pass=0):
    def bwd(*args):
        prim = args[:n_prim]
        cts = args[n_prim:n_prim + n_out]
        passes = args[n_prim + n_out:n_prim + n_out + 2 * n_pass]
        pars = args[n_prim + n_out + 2 * n_pass:]
        _, vjp = jax.vjp(fn, *prim, *pars)
        grads = vjp(tuple(cts))
        sums = tuple(passes[2 * i] + passes[2 * i + 1] for i in range(n_pass))
        return tuple(grads[:n_prim]) + sums + tuple(grads[n_prim:])
    return bwd


def _mm(name, a, b, mode="nn", extras=(), epi=None, out_dtypes=(F32,), a_pro=None, out_split=None,
        tm=1024, tn=1024, tk=1024):
    split = b.shape[0] if b.ndim == 3 else None
    b_rows, b_cols = b.shape[-2:]
    if mode == "nn":
        (m, k), n = a.shape, b_cols * (split or 1)
    elif mode == "nt":
        (m, k), n = a.shape, b_rows
        assert k == b_cols * (split or 1)
    else:
        assert split is None
        (k, m), n = a.shape, b_cols
    tm, tk = _tile(m, tm), _tile(k, tk)
    tn = _tile(n // out_split, tn) if out_split else _tile(n, tn)
    if split and mode == "nn":
        tn = _tile(b_cols, tn)
    if split and mode == "nt":
        tk = _tile(b_cols, tk)
    nk = k // tk
    raw = {"nn": _raw_nn, "nt": _raw_nt, "tn": _raw_tn}[mode]
    n_e, n_o = len(extras), len(out_dtypes)
    if epi is None:
        epi = lambda acc: (acc,)

    def body(a_ref, b_ref, *rest):
        e_refs, o_refs = rest[:n_e], rest[n_e:n_e + n_o]
        kk = pl.program_id(2)
        a_tile = a_ref[...] if a_pro is None else a_pro(a_ref[...].astype(F32))
        part = raw(a_tile, b_ref[...])

        def finish(total):
            res = epi(total, *[e[...].astype(F32) for e in e_refs])
            for o_ref, r in zip(o_refs, res):
                o_ref[...] = r.astype(o_ref.dtype)

        if nk == 1:
            finish(part)
            return
        acc = rest[-1]

        @pl.when(kk == 0)
        def _():
            acc[...] = part

        @pl.when((kk > 0) & (kk < nk - 1))
        def _():
            acc[...] += part

        @pl.when(kk == nk - 1)
        def _():
            finish(acc[...] + part)

    a_spec = pl.BlockSpec((tk, tm), lambda i, j, kk: (kk, i)) if mode == "tn" else pl.BlockSpec((tm, tk), lambda i, j, kk: (i, kk))
    if split and mode == "nn":
        per = b_cols // tn
        b_spec = pl.BlockSpec((None, tk, tn), lambda i, j, kk: (j // per, kk, j % per))
    elif split:
        per = b_cols // tk
        b_spec = pl.BlockSpec((None, tn, tk), lambda i, j, kk: (kk // per, j, kk % per))
    elif mode == "nt":
        b_spec = pl.BlockSpec((tn, tk), lambda i, j, kk: (j, kk))
    else:
        b_spec = pl.BlockSpec((tk, tn), lambda i, j, kk: (kk, j))
    mn_spec = pl.BlockSpec((tm, tn), lambda i, j, kk: (i, j))
    if out_split:
        assert not extras
        per_out = n // out_split // tn
        out_spec = pl.BlockSpec((None, tm, tn), lambda i, j, kk: (j // per_out, i, j % per_out))
        out_shapes = [jax.ShapeDtypeStruct((out_split, m, n // out_split), dt) for dt in out_dtypes]
    else:
        out_spec = mn_spec
        out_shapes = [jax.ShapeDtypeStruct((m, n), dt) for dt in out_dtypes]
    outs = pl.pallas_call(
        body, name=name, grid=(m // tm, n // tn, nk),
        in_specs=[a_spec, b_spec] + [mn_spec] * n_e, out_specs=[out_spec] * n_o,
        out_shape=out_shapes,
        scratch_shapes=[pltpu.VMEM((tm, tn), F32)] if nk > 1 else [],
        compiler_params=_params(("parallel", "parallel", "arbitrary")),
    )(a, b, *extras)
    return outs[0] if n_o == 1 else outs


def _sigmoid(x):
    return jax.nn.sigmoid(x)


def _silu(x):
    return x * _sigmoid(x)


def _softplus(x):
    return jnp.maximum(x, 0.0) + jnp.log1p(jnp.exp(-jnp.abs(x)))


def _rmsnorm_fn(x, gain):
    return (x * lax.rsqrt(jnp.mean(x * x, axis=-1, keepdims=True) + EPS) * gain,)


def _head_norm(o, gain, n_heads):
    w = o.shape[-1] // n_heads
    parts = []
    for h in range(n_heads):
        oh = o[:, h * w:(h + 1) * w]
        parts.append(oh * lax.rsqrt(jnp.mean(oh * oh, axis=-1, keepdims=True) + EPS))
    return jnp.concatenate(parts, axis=-1) * gain


@jax.custom_jvp
def _neg_expm1(x):
    u = jnp.exp(x)
    is_one = u == 1.0
    return jnp.where(is_one, -x, (1.0 - u) * x / jnp.log(jnp.where(is_one, 2.0, u)))


@_neg_expm1.defjvp
def _neg_expm1_jvp(primals, tangents):
    (x,), (t,) = primals, tangents
    return _neg_expm1(x), -jnp.exp(x) * t


def _rg_gates_fn(xc, wa, wx, ba, bx, lam):
    outs = []
    for d in range(2):
        r = _sigmoid(_dot_nn(xc, wa[d]) + ba[d:d + 1])
        i = _sigmoid(_dot_nn(xc, wx[d]) + bx[d:d + 1])
        log_a = -RG_C * r * _softplus(-lam[d:d + 1])
        outs.append(jnp.exp(log_a))
        outs.append(jnp.sqrt(_neg_expm1(2.0 * log_a)) * (i * xc))
    return tuple(outs)


def _hg_pre_fn(q, f_f, f_b, logits):
    mx = jnp.maximum(logits[0:1], logits[1:2])
    e0 = jnp.exp(logits[0:1] - mx)
    e1 = jnp.exp(logits[1:2] - mx)
    lb = e0 / (e0 + e1)
    outs = [_silu(q)]
    for f in (f_f, f_b):
        outs.append((1.0 - lb) * _sigmoid(-f))
        outs.append(jnp.log(lb + (1.0 - lb) * _sigmoid(f)))
    return tuple(outs)


def _post0_fn(hs, ga, o, g, gain):
    ya = hs * jax.nn.gelu(ga, approximate=True)
    yb = _head_norm(o, gain, 4) * _silu(g)
    return (jnp.concatenate([ya, yb], axis=-1),)


def _post0_fwd_fn(h_f, h_b, ga, o_f, o_b, g, gain):
    return _post0_fn(h_f + h_b, ga, o_f + o_b, g, gain)


def _post0_bwd_fn(h_f, h_b, ga, o_f, o_b, g, dmix, gain):
    _, vjp = jax.vjp(_post0_fn, h_f + h_b, ga, o_f + o_b, g, gain)
    return vjp((dmix,))


def _gla_pre_fn(q, lr, w_up, b_gate):
    outs = [q * (128.0 ** -0.5)]
    for d in range(2):
        z = _dot_nn(lr, w_up[d]) + b_gate[d:d + 1]
        outs.append(-_softplus(-z) * (1.0 / 16.0))
    return tuple(outs)


def _gla_post_fn(o, r, gain):
    return (_head_norm(o, gain, 4) * _silu(r),)


def _gla_post_fwd_fn(o_f, o_b, r, gain):
    return _gla_post_fn(o_f + o_b, r, gain)


def _gla_post_bwd_fn(o_f, o_b, r, dmix, gain):
    _, vjp = jax.vjp(_gla_post_fn, o_f + o_b, r, gain)
    return vjp((dmix,))


def _relu2_bwd_epi(acc, hid):
    return (acc * 2.0 * jnp.maximum(hid, 0.0),)


def _relu2(x):
    r = jnp.maximum(x, 0.0)
    return r * r


def _add_epi(acc, res):
    return (acc + res,)


def _loss_head_fn(h, target, gain):
    def f(h, gain):
        y = _rmsnorm_fn(h, gain)[0]
        err = y - target
        return 0.5 * jnp.sum(jnp.mean(err * err, axis=-1, keepdims=True))
    loss, (dh, dgain) = jax.value_and_grad(f, argnums=(0, 1))(h, gain)
    return dh, jnp.full((1, LANES), loss, F32), dgain


def _adam_fn(w, g, m, v):
    m2 = ADAM_B1 * m + (1.0 - ADAM_B1) * g
    v2 = ADAM_B2 * v + (1.0 - ADAM_B2) * (g * g)
    m_hat = m2 / (1.0 - ADAM_B1 ** ADAM_STEP)
    v_hat = v2 / (1.0 - ADAM_B2 ** ADAM_STEP)
    delta = -ADAM_LR * (m_hat / (jnp.sqrt(v_hat) + ADAM_EPS) + ADAM_WD * w)
    return delta, m2, v2


def _shifted(x, t_idx, off):
    n = x.shape[0]
    rolled = pltpu.roll(x, (-off) % n, 0)
    valid = (t_idx + off >= 0) & (t_idx + off < n)
    return jnp.where(valid, rolled, 0.0)


def _conv_fwd(name, src, colblock, w, b):
    n_rows, width = src.shape[0], w.shape[1]

    def body(x_ref, w_ref, b_ref, o_ref):
        x = x_ref[...]
        t_idx = lax.broadcasted_iota(jnp.int32, x.shape, 0)
        acc = b_ref[...] + w_ref[2:3, :] * x
        acc += w_ref[0:1, :] * _shifted(x, t_idx, -2)
        acc += w_ref[1:2, :] * _shifted(x, t_idx, -1)
        acc += w_ref[3:4, :] * _shifted(x, t_idx, 1)
        o_ref[...] = acc

    nb = width // LANES
    return pl.pallas_call(
        body, name=name, grid=(nb,),
        in_specs=[pl.BlockSpec((n_rows, LANES), lambda j: (0, colblock * nb + j)),
                  pl.BlockSpec((4, LANES), lambda j: (0, j)), pl.BlockSpec((1, LANES), lambda j: (0, j))],
        out_specs=pl.BlockSpec((n_rows, LANES), lambda j: (0, j)),
        out_shape=jax.ShapeDtypeStruct((n_rows, width), F32),
        compiler_params=_params(("parallel",)),
    )(src, w, b)


def _conv_bwd(name, src, colblock, w, d):
    n_rows, width = src.shape[0], w.shape[1]

    def body(x_ref, w_ref, d_ref, dx_ref, dw_ref, db_ref):
        x = x_ref[...]
        g = d_ref[...]
        t_idx = lax.broadcasted_iota(jnp.int32, x.shape, 0)
        dx = w_ref[2:3, :] * g
        dx += w_ref[0:1, :] * _shifted(g, t_idx, 2)
        dx += w_ref[1:2, :] * _shifted(g, t_idx, 1)
        dx += w_ref[3:4, :] * _shifted(g, t_idx, -1)
        dx_ref[...] = dx.astype(dx_ref.dtype)
        dw_ref[0:1, :] = jnp.sum(g * _shifted(x, t_idx, -2), axis=0, keepdims=True)
        dw_ref[1:2, :] = jnp.sum(g * _shifted(x, t_idx, -1), axis=0, keepdims=True)
        dw_ref[2:3, :] = jnp.sum(g * x, axis=0, keepdims=True)
        dw_ref[3:4, :] = jnp.sum(g * _shifted(x, t_idx, 1), axis=0, keepdims=True)
        db_ref[...] = jnp.sum(g, axis=0, keepdims=True)

    nb = width // LANES
    return pl.pallas_call(
        body, name=name, grid=(nb,),
        in_specs=[pl.BlockSpec((n_rows, LANES), lambda j: (0, colblock * nb + j)),
                  pl.BlockSpec((4, LANES), lambda j: (0, j)),
                  pl.BlockSpec((n_rows, LANES), lambda j: (0, j))],
        out_specs=[pl.BlockSpec((n_rows, LANES), lambda j: (0, j)), pl.BlockSpec((4, LANES), lambda j: (0, j)),
                   pl.BlockSpec((1, LANES), lambda j: (0, j))],
        out_shape=[jax.ShapeDtypeStruct((n_rows, width), BF16), jax.ShapeDtypeStruct((4, width), F32),
                   jax.ShapeDtypeStruct((1, width), F32)],
        compiler_params=_params(("parallel",)),
    )(src, w, d)


SUBLANES = 8
SCAN_UNROLL = 8


def _shift_rows(x, d, fill):
    n = x.shape[0]
    t = lax.broadcasted_iota(jnp.int32, x.shape, 0)
    valid = (t >= d) if d > 0 else (t < n + d)
    return jnp.where(valid, pltpu.roll(x, d % n, 0), fill)


def _tile_scan(a, u, reverse):
    d = 1
    while d < a.shape[0]:
        s = -d if reverse else d
        a_sh, u_sh = _shift_rows(a, s, 1.0), _shift_rows(u, s, 0.0)
        u = u + a * u_sh
        a = a * a_sh
        d *= 2
    return a, u


def _edge_row(x, reverse):
    return x[0:1, :] if reverse else x[SUBLANES - 1:SUBLANES, :]


def _scan_specs(n_rows, n):
    return [pl.BlockSpec((n_rows, LANES), lambda j: (0, j))] * n


def _scan_fwd(name, a, u, reverse):
    n_rows, width = a.shape
    n_tiles = n_rows // SUBLANES

    def body(a_ref, u_ref, h_ref):
        def step(i, carry):
            tile = (n_tiles - 1 - i) if reverse else i
            rows = pl.ds(pl.multiple_of(tile * SUBLANES, SUBLANES), SUBLANES)
            acc_a, acc_u = _tile_scan(a_ref[rows, :], u_ref[rows, :], reverse)
            h = acc_u + acc_a * carry
            h_ref[rows, :] = h
            return _edge_row(h, reverse)
        lax.fori_loop(0, n_tiles, step, jnp.zeros((1, LANES), F32), unroll=SCAN_UNROLL)

    return pl.pallas_call(
        body, name=name, grid=(width // LANES,), in_specs=_scan_specs(n_rows, 2), out_specs=_scan_specs(n_rows, 1)[0],
        out_shape=jax.ShapeDtypeStruct((n_rows, width), F32), compiler_params=_params(("parallel",)),
    )(a, u)


def _scan_bwd(name, a, h, dh, reverse):
    n_rows, width = a.shape
    n_tiles = n_rows // SUBLANES
    against = not reverse
    one = -1 if against else 1

    def body(a_ref, h_ref, dh_ref, du_ref, da_ref):
        def step(i, carry):
            g_in, a_edge = carry
            tile = (n_tiles - 1 - i) if against else i
            start = pl.multiple_of(tile * SUBLANES, SUBLANES)
            rows = pl.ds(start, SUBLANES)
            a_tile = a_ref[rows, :]
            coeff = _shift_rows(a_tile, one, a_edge)
            acc_a, acc_u = _tile_scan(coeff, dh_ref[rows, :], against)
            g = acc_u + acc_a * g_in
            du_ref[rows, :] = g
            outside = (start + SUBLANES) if reverse else (start - 1)
            inside = (outside >= 0) & (outside < n_rows)
            h_edge = jnp.where(inside, h_ref[pl.ds(jnp.clip(outside, 0, n_rows - 1), 1), :], 0.0)
            da_ref[rows, :] = g * _shift_rows(h_ref[rows, :], -one, h_edge)
            return _edge_row(g, against), _edge_row(a_tile, against)
        zero = jnp.zeros((1, LANES), F32)
        lax.fori_loop(0, n_tiles, step, (zero, zero), unroll=SCAN_UNROLL)

    return pl.pallas_call(
        body, name=name, grid=(width // LANES,), in_specs=_scan_specs(n_rows, 3), out_specs=_scan_specs(n_rows, 2),
        out_shape=[jax.ShapeDtypeStruct((n_rows, width), F32)] * 2, compiler_params=_params(("parallel",)),
    )(a, h, dh)


def _tri_mask(c, reverse):
    row = lax.broadcasted_iota(jnp.int32, (c, c), 0)
    col = lax.broadcasted_iota(jnp.int32, (c, c), 1)
    return (col >= row) if reverse else (col <= row)


def _cumsum_rows(x, reverse):
    tri = _tri_mask(x.shape[0], reverse).astype(BF16)
    hi = x.astype(BF16)
    rest = x - hi.astype(F32)
    mid = rest.astype(BF16)
    lo = (rest - mid.astype(F32)).astype(BF16)
    return _raw_nn(tri, hi) + _raw_nn(tri, mid) + _raw_nn(tri, lo)


@functools.partial(jax.custom_vjp, nondiff_argnums=(1,))
def _cumsum(x, reverse):
    return _cumsum_rows(x, reverse)


def _cumsum_fwd(x, reverse):
    return _cumsum_rows(x, reverse), None


def _cumsum_bwd(reverse, _, g):
    return (_cumsum_rows(g, not reverse),)


_cumsum.defvjp(_cumsum_fwd, _cumsum_bwd)


def _chunks_fn(qs, ks, vs, lfs, sts, reverses):
    n, c = len(qs), qs[0].shape[0]
    every = range(n)
    tris = [_tri_mask(c, r) for r in reverses]
    cums = [_cumsum(lfs[i], reverses[i]) for i in every]
    rid = lax.broadcasted_iota(jnp.int32, cums[0].shape, 0)

    def pick(cum, r):
        return jnp.sum(jnp.where(rid == r, cum, 0.0), axis=0, keepdims=True)

    refs = [pick(cums[i], (c - 1 - c // 2) if reverses[i] else c // 2) for i in every]
    lasts = [pick(cums[i], 0 if reverses[i] else c - 1) for i in every]
    q_in = [qs[i] * jnp.exp(cums[i] - refs[i]) for i in every]
    k_in = [ks[i] * jnp.exp(refs[i] - cums[i]) for i in every]
    scores = [jnp.where(tris[i], _dot_nt(q_in[i], k_in[i]), 0.0) for i in every]
    o_intra = [_dot_nn(scores[i], vs[i]) for i in every]
    q_out = [qs[i] * jnp.exp(cums[i]) for i in every]
    o_inter = [_dot_nt(q_out[i], sts[i]) for i in every]
    k_state = [ks[i] * jnp.exp(lasts[i] - cums[i]) for i in every]
    upd = [_dot_tn(vs[i], k_state[i]) for i in every]
    st_new = [sts[i] * jnp.exp(lasts[i]) + upd[i] for i in every]
    return [o_intra[i] + o_inter[i] for i in every], st_new


def _attn_fwd(name, q, k_f, k_b, v, lf_f, lf_b, n_heads, dk, dv):
    n_rows = q[0].shape[0]
    n_chunks = n_rows // CHUNK
    wk, wv = n_heads * dk, n_heads * dv

    def spec(width, off, rev):
        return pl.BlockSpec((CHUNK, width), lambda n: ((n_chunks - 1 - n) if rev else n, off))

    def sspec(rev):
        return pl.BlockSpec((None, n_heads, dv, dk), lambda n: ((n_chunks - 1 - n) if rev else n, 0, 0, 0))

    def body(qf, kf, vf, lff, qb, kb, vb, lfb, of_ref, ob_ref, sf_ref, sb_ref, st):
        @pl.when(pl.program_id(0) == 0)
        def _():
            st[...] = jnp.zeros_like(st)

        ins = ((qf, kf, vf, lff), (qb, kb, vb, lfb))
        chains = [(d, h) for d in range(2) for h in range(n_heads)]
        ck = [slice(h * dk, (h + 1) * dk) for h in range(n_heads)]
        cv = [slice(h * dv, (h + 1) * dv) for h in range(n_heads)]
        qs = [ins[d][0][:, ck[h]] for d, h in chains]
        ks = [ins[d][1][:, ck[h]] for d, h in chains]
        vs = [ins[d][2][:, cv[h]] for d, h in chains]
        lfs = [ins[d][3][:, ck[h]] for d, h in chains]
        sts = [st[d, h] for d, h in chains]
        os_, st_new = _chunks_fn(qs, ks, vs, lfs, sts, [d == 1 for d, _ in chains])
        for i, (d, h) in enumerate(chains):
            (sf_ref, sb_ref)[d][h] = sts[i].astype(BF16)
            (of_ref, ob_ref)[d][:, cv[h]] = os_[i]
            st[d, h] = st_new[i]

    in_specs = [spec(wk, q[1], False), spec(wk, k_f[1], False), spec(wv, v[1], False), spec(wk, lf_f[1], False),
                spec(wk, q[1], True), spec(wk, k_b[1], True), spec(wv, v[1], True), spec(wk, lf_b[1], True)]
    return pl.pallas_call(
        body, name=name, grid=(n_chunks,), in_specs=in_specs,
        out_specs=[spec(wv, 0, False), spec(wv, 0, True), sspec(False), sspec(True)],
        out_shape=[jax.ShapeDtypeStruct((n_rows, wv), F32)] * 2
        + [jax.ShapeDtypeStruct((n_chunks, n_heads, dv, dk), BF16)] * 2,
        scratch_shapes=[pltpu.VMEM((2, n_heads, dv, dk), F32)],
        compiler_params=_params(("arbitrary",)),
    )(q[0], k_f[0], v[0], lf_f[0], q[0], k_b[0], v[0], lf_b[0])


def _attn_bwd(name, q, k_f, k_b, v, lf_f, lf_b, st_f, st_b, do, n_heads, dk, dv, out_dtype=F32):
    n_rows = q[0].shape[0]
    n_chunks = n_rows // CHUNK
    wk, wv = n_heads * dk, n_heads * dv

    def spec(width, off, rev):
        return pl.BlockSpec((CHUNK, width), lambda n: (n if rev else (n_chunks - 1 - n), off))

    def sspec(rev):
        return pl.BlockSpec((None, n_heads, dv, dk), lambda n: (n if rev else (n_chunks - 1 - n), 0, 0, 0))

    def body(qf, kf, vf, lff, sf, dof, qb, kb, vb, lfb, sb, dob,
             dqf, dkf, dvf, dlff, dqb, dkb, dvb, dlfb, dst):
        @pl.when(pl.program_id(0) == 0)
        def _():
            dst[...] = jnp.zeros_like(dst)

        ins = ((qf, kf, vf, lff, sf, dof), (qb, kb, vb, lfb, sb, dob))
        outs = ((dqf, dkf, dvf, dlff), (dqb, dkb, dvb, dlfb))
        chains = [(d, h) for d in range(2) for h in range(n_heads)]
        ck = [slice(h * dk, (h + 1) * dk) for h in range(n_heads)]
        cv = [slice(h * dv, (h + 1) * dv) for h in range(n_heads)]
        qs = [ins[d][0][:, ck[h]] for d, h in chains]
        ks = [ins[d][1][:, ck[h]] for d, h in chains]
        vs = [ins[d][2][:, cv[h]] for d, h in chains]
        lfs = [ins[d][3][:, ck[h]] for d, h in chains]
        sts = [ins[d][4][h].astype(F32) for d, h in chains]
        dos = [ins[d][5][:, cv[h]] for d, h in chains]
        dsts = [dst[d, h] for d, h in chains]
        fn = functools.partial(_chunks_fn, reverses=[d == 1 for d, _ in chains])
        _, vjp = jax.vjp(fn, qs, ks, vs, lfs, sts)
        dqs, dks, dvs, dlfs, dst_prev = vjp((dos, dsts))
        for i, (d, h) in enumerate(chains):
            dq_r, dk_r, dv_r, dlf_r = outs[d]
            dq_r[:, ck[h]] = dqs[i].astype(dq_r.dtype)
            dk_r[:, ck[h]] = dks[i].astype(dk_r.dtype)
            dv_r[:, cv[h]] = dvs[i].astype(dv_r.dtype)
            dlf_r[:, ck[h]] = dlfs[i].astype(dlf_r.dtype)
            dst[d, h] = dst_prev[i]

    def dir_specs(kk, lf, rev):
        return [spec(wk, q[1], rev), spec(wk, kk[1], rev), spec(wv, v[1], rev), spec(wk, lf[1], rev), sspec(rev),
                spec(wv, 0, rev)]

    def dir_out_specs(rev):
        return [spec(wk, 0, rev), spec(wk, 0, rev), spec(wv, 0, rev), spec(wk, 0, rev)]

    shapes = [jax.ShapeDtypeStruct((n_rows, wk), out_dtype), jax.ShapeDtypeStruct((n_rows, wk), out_dtype),
              jax.ShapeDtypeStruct((n_rows, wv), out_dtype), jax.ShapeDtypeStruct((n_rows, wk), F32)]
    outs = pl.pallas_call(
        body, name=name, grid=(n_chunks,), in_specs=dir_specs(k_f, lf_f, False) + dir_specs(k_b, lf_b, True),
        out_specs=dir_out_specs(False) + dir_out_specs(True), out_shape=shapes + shapes,
        scratch_shapes=[pltpu.VMEM((2, n_heads, dv, dk), F32)],
        compiler_params=_params(("arbitrary",)),
    )(q[0], k_f[0], v[0], lf_f[0], st_f, do, q[0], k_b[0], v[0], lf_b[0], st_b, do)
    return outs[:4], outs[4:]


def _row2(v):
    return v.reshape(1, -1)


def _mlp_fwd(tag, h, gain, w1, w2):
    y = _rowcall(f"{tag}_norm", _rmsnorm_fn, [(h, h.shape[1], 0)], [gain], [(h.shape[1], BF16)], tm=512)[0]
    hid = _mm(f"{tag}_up", y, w1, out_dtypes=(BF16,))
    h_out = _mm(f"{tag}_down", hid, w2, a_pro=_relu2, extras=(h,), epi=_add_epi)
    return h_out, (y, hid)


def _mlp_bwd(tag, h, gain, w1, w2, saved, dh_out):
    y, hid = saved
    dhid = _mm(f"{tag}_dact", dh_out, w2, mode="nt", extras=(hid,), epi=_relu2_bwd_epi, out_dtypes=(BF16,))
    dw2 = _mm(f"{tag}_dw2", hid, dh_out, mode="tn", a_pro=_relu2)
    dw1 = _mm(f"{tag}_dw1", y, dhid, mode="tn", out_split=N_CHIPS)
    dy = _mm(f"{tag}_dy", dhid, w1, mode="nt")
    dh, dgain = _norm_bwd(f"{tag}_dnorm", h, gain, dy, dh_out)
    return dh, dgain, dw1, dw2


def _norm_bwd(name, h, gain, dy, dres, pin=None):
    d = h.shape[1]

    def fn(h, dy, dres, gain):
        _, vjp = jax.vjp(lambda a, b: _rmsnorm_fn(a, b)[0], h, gain)
        dh, dgain = vjp(dy)
        return dh + dres, dgain

    dh, dgain = _rowcall(name, fn, [(h, d, 0), (dy, d, 0), (dres, d, 0)], [gain], [(d, F32)], [(1, d)], tm=512, pin=pin)
    return dh, dgain


def _local_step(x, target, w, pin=None, late=None, emit=None):
    g = {}
    d_model = x.shape[1]
    rg_w = hg_w = d_model // 2
    pins = []

    def send_off(tag, arrays):
        if emit is not None:
            pins.append(emit(tag, arrays))

    def chip_major(t):
        return t.reshape(N_CHIPS, t.shape[0] // N_CHIPS, t.shape[1])

    h_a0 = x
    gain = _row2(w["norm_mix"][0])
    y0 = _rowcall("l0_norm", _rmsnorm_fn, [(h_a0, d_model, 0)], [gain], [(d_model, BF16)], tm=512, pin=pin)[0]
    proj0 = _mm("l0_in", y0, w["ab_w_in"])
    conv_w, conv_b = w["rg_conv_w"], _row2(w["rg_conv_b"])
    xc = _conv_fwd("rg_conv", proj0, 0, conv_w, conv_b)
    gate_pars = [w["rg_wa_bd"], w["rg_wx_bd"], w["rg_b_a"], w["rg_b_x"], w["rg_lambda"]]
    a_f, u_f, a_b, u_b = _rowcall("rg_gates", _rg_gates_fn, [(xc, rg_w, 0)], gate_pars, [(rg_w, F32)] * 4)
    hs_f = _scan_fwd("rg_scan_f", a_f, u_f, False)
    hs_b = _scan_fwd("rg_scan_b", a_b, u_b, True)
    hg_rows = [(proj0, hg_w, 2), (proj0, hg_w, 3), (proj0, hg_w, 4)]
    qh, k_f, lf_f, k_b, lf_b = _rowcall("hg_pre", _hg_pre_fn, hg_rows, [w["hg_lb_logits"]], [(hg_w, F32)] * 5)
    iv = (proj0, 5)
    o_f, o_b, st_f, st_b = _attn_fwd("hg_attn", (qh, 0), (k_f, 0), (k_b, 0), iv, (lf_f, 0), (lf_b, 0), 4, 128, 128)
    post0_rows = [(hs_f, rg_w, 0), (hs_b, rg_w, 0), (proj0, rg_w, 1), (o_f, hg_w, 0), (o_b, hg_w, 0), (proj0, hg_w, 6)]
    hg_gain = _row2(w["hg_norm"])
    mix_in0 = _rowcall("l0_post", _post0_fwd_fn, post0_rows, [hg_gain], [(d_model, BF16)])[0]
    if late is not None:
        w = {**w, **late(mix_in0)}
    h_b0 = _mm("l0_out", mix_in0, w["ab_w_out"], extras=(h_a0,), epi=_add_epi)
    h_c0, mlp0 = _mlp_fwd("mlp0", h_b0, _row2(w["norm_mlp"][0]), w["mlp_w1"][0], w["mlp_w2"][0])

    h_a1 = h_c0
    gain1 = _row2(w["norm_mix"][1])
    y1 = _rowcall("l1_norm", _rmsnorm_fn, [(h_a1, d_model, 0)], [gain1], [(d_model, BF16)], tm=512)[0]
    proj1 = _mm("l1_in", y1, w["gla_w_in_pad"], tn=640)
    gla_pars = [w["gla_w_up_pad"], w["gla_b_gate"]]
    gq, glf_f, glf_b = _rowcall("gla_pre", _gla_pre_fn, [(proj1, 512, 0), (proj1, LANES, 24)], gla_pars, [(512, F32)] * 3)
    gk, gv = (proj1, 1), (proj1, 1)
    go_f, go_b, gst_f, gst_b = _attn_fwd("gla_attn", (gq, 0), gk, gk, gv, (glf_f, 0), (glf_b, 0), 4, 128, 256)
    gla_gain = _row2(w["gla_norm"])
    post1_rows = [(go_f, d_model, 0), (go_b, d_model, 0), (proj1, d_model, 2)]
    mix_in1 = _rowcall("l1_post", _gla_post_fwd_fn, post1_rows, [gla_gain], [(d_model, BF16)])[0]
    h_b1 = _mm("l1_out", mix_in1, w["gla_w_out"], extras=(h_a1,), epi=_add_epi)
    h_c1, mlp1 = _mlp_fwd("mlp1", h_b1, _row2(w["norm_mlp"][1]), w["mlp_w1"][1], w["mlp_w2"][1])

    dh, loss, g["norm_final"] = _rowcall(
        "loss_head", _loss_head_fn, [(h_c1, d_model, 0), (target, d_model, 0)], [_row2(w["norm_final"])],
        [(d_model, F32)], [(1, LANES), (1, d_model)], tm=512)

    dh, g_nmlp1, g_w1_1, g_w2_1 = _mlp_bwd("mlp1", h_b1, _row2(w["norm_mlp"][1]), w["mlp_w1"][1], w["mlp_w2"][1], mlp1, dh)
    send_off("mlp1", [g_w1_1, chip_major(g_w2_1)])
    dmix1 = _mm("l1_dout", dh, w["gla_w_out"], mode="nt")
    g["gla_w_out"] = _mm("l1_dwout", mix_in1, dh, mode="tn")
    dgo, dr, g["gla_norm"] = _rowcall(
        "l1_dpost", _gla_post_bwd_fn, post1_rows + [(dmix1, d_model, 0)], [gla_gain],
        [(d_model, F32), (d_model, BF16)], [(1, d_model)], pin=pins.pop() if pins else None)
    (dq_f, dk_f, dv_f, dlf_f), (dq_b, dk_b, dv_b, dlf_b) = _attn_bwd(
        "gla_dattn", (gq, 0), gk, gk, gv, (glf_f, 0), (glf_b, 0), gst_f, gst_b, dgo, 4, 128, 256)

    def gla_pre_bwd(q, lr, dq1, dq2, dlf1, dlf2, dk1, dk2, dv1, dv2, w_up, b_gate):
        dlr = jnp.zeros_like(lr)
        dws, dbs = [], []
        for d, dlf in enumerate((dlf1, dlf2)):
            z = _raw_nn(lr, w_up[d]) + b_gate[d:d + 1]
            dz = dlf * _sigmoid(-z) * (1.0 / 16.0)
            dlr = dlr + _raw_nt(dz, w_up[d])
            dws.append(_raw_tn(dz, lr))
            dbs.append(jnp.sum(dz, axis=0, keepdims=True))
        return ((dq1 + dq2) * (128.0 ** -0.5), dk1 + dk2, dv1 + dv2, dlr, dws[0], dws[1], dbs[0], dbs[1])

    rows = [(proj1, 512, 0), (proj1, LANES, 24), (dq_f, 512, 0), (dq_b, 512, 0), (dlf_f, 512, 0), (dlf_b, 512, 0),
            (dk_f, 512, 0), (dk_b, 512, 0), (dv_f, d_model, 0), (dv_b, d_model, 0)]
    dq, dk, dv, dlr, dwt_f, dwt_b, db_f, db_b = _rowcall(
        "gla_dpre", gla_pre_bwd, rows, gla_pars, [(512, BF16), (512, BF16), (d_model, BF16), (LANES, BF16)],
        [(512, LANES), (512, LANES), (1, 512), (1, 512)])
    g["gla_w_up_pad"] = jnp.stack([dwt_f.T, dwt_b.T])
    g["gla_b_gate"] = jnp.concatenate([db_f, db_b], axis=0)
    dproj1 = jnp.concatenate([dq, dk, dv, dr, dlr], axis=1)
    g_gla_in = _mm("l1_dwin", y1, dproj1, mode="tn", tn=640)
    g["gla_w_in"] = _split_chips(g_gla_in[:, :GLA_IN_WIDTH], 1)
    send_off("gla", [g["gla_w_in"], chip_major(g["gla_w_out"])])
    dy1 = _mm("l1_dy", dproj1, w["gla_w_in_pad"], mode="nt", tk=640)
    dh, g_nmix1 = _norm_bwd("l1_dnorm", h_a1, gain1, dy1, dh, pin=pins.pop() if pins else None)

    dh, g_nmlp0, g_w1_0, g_w2_0 = _mlp_bwd("mlp0", h_b0, _row2(w["norm_mlp"][0]), w["mlp_w1"][0], w["mlp_w2"][0], mlp0, dh)
    send_off("mlp0", [g_w1_0, chip_major(g_w2_0)])
    dmix0 = _mm("l0_dout", dh, w["ab_w_out"], mode="nt")
    g["ab_w_out"] = _mm("l0_dwout", mix_in0, dh, mode="tn")
    dhs, dga, do, dg, g["hg_norm"] = _rowcall(
        "l0_dpost", _post0_bwd_fn, post0_rows + [(dmix0, d_model, 0)], [hg_gain],
        [(rg_w, F32), (rg_w, BF16), (hg_w, F32), (hg_w, BF16)], [(1, hg_w)], pin=pins.pop() if pins else None)
    (dqh_f, dk_f, div_f, dlf_f), (dqh_b, dk_b, div_b, dlf_b) = _attn_bwd(
        "hg_dattn", (qh, 0), (k_f, 0), (k_b, 0), iv, (lf_f, 0), (lf_b, 0), st_f, st_b, do, 4, 128, 128)

    def hg_pre_bwd(q, f_f, f_b, dq1, dq2, dk1, dlf1, dk2, dlf2, dv1, dv2, logits):
        _, vjp = jax.vjp(_hg_pre_fn, q, f_f, f_b, logits)
        dq, df_f, df_b, dlogits = vjp((dq1 + dq2, dk1, dlf1, dk2, dlf2))
        return dq, df_f, df_b, dv1 + dv2, dlogits

    rows = hg_rows + [(t, hg_w, 0) for t in (dqh_f, dqh_b, dk_f, dlf_f, dk_b, dlf_b, div_f, div_b)]
    dq, df_f, df_b, div, g["hg_lb_logits"] = _rowcall(
        "hg_dpre", hg_pre_bwd, rows, [w["hg_lb_logits"]], [(hg_w, BF16)] * 4, [(2, hg_w)])
    du_f, da_f = _scan_bwd("rg_dscan_f", a_f, hs_f, dhs, False)
    du_b, da_b = _scan_bwd("rg_dscan_b", a_b, hs_b, dhs, True)
    gates_bwd = _vjp_of(_rg_gates_fn, 1, 4, 5)
    rows = [(xc, rg_w, 0), (da_f, rg_w, 0), (du_f, rg_w, 0), (da_b, rg_w, 0), (du_b, rg_w, 0)]
    dxc, g["rg_wa_bd"], g["rg_wx_bd"], g["rg_b_a"], g["rg_b_x"], g["rg_lambda"] = _rowcall(
        "rg_dgates", gates_bwd, rows, gate_pars, [(rg_w, F32)],
        [(2, rg_w, rg_w), (2, rg_w, rg_w), (2, rg_w), (2, rg_w), (2, rg_w)])
    dxa, g["rg_conv_w"], g["rg_conv_b"] = _conv_bwd("rg_dconv", proj0, 0, conv_w, dxc)
    dproj0 = jnp.concatenate([dxa, dga, dq, df_f, df_b, div, dg], axis=1)
    g["ab_w_in"] = _mm("l0_dwin", y0, dproj0, mode="tn", out_split=N_CHIPS)
    dy0 = _mm("l0_dy", dproj0, w["ab_w_in"], mode="nt")
    grad_x, g_nmix0 = _norm_bwd("l0_dnorm", h_a0, gain, dy0, dh)

    g["norm_mix"] = jnp.concatenate([g_nmix0, g_nmix1], axis=0)
    g["norm_mlp"] = jnp.concatenate([g_nmlp0, g_nmlp1], axis=0)
    g["mlp_w1"] = [g_w1_0, g_w1_1]
    g["mlp_w2"] = [g_w2_0, g_w2_1]
    return loss, grad_x, g


def _block_diag(w):
    d, g, n, _ = w.shape
    eye = jnp.eye(g, dtype=w.dtype)
    return (w[:, :, :, None, :] * eye[None, :, None, :, None]).reshape(d, g * n, g * n)


def _block_diag_extract(wbd, g):
    d, gn, _ = wbd.shape
    n = gn // g
    blocks = wbd.reshape(d, g, n, g, n)
    return jnp.stack([blocks[:, i, :, i, :] for i in range(g)], axis=1)


def _prepare_weights(big, full):
    w = {k: full[k] for k in ("norm_mix", "norm_mlp", "norm_final", "hg_lb_logits")}
    for k in ("rg_conv_w", "rg_conv_b", "rg_b_a", "rg_b_x", "rg_lambda", "hg_norm", "gla_b_gate", "gla_norm"):
        w[k] = full[k][0]
    w["rg_wa_bd"] = _block_diag(full["rg_w_a"][0])
    w["rg_wx_bd"] = _block_diag(full["rg_w_x"][0])
    up = full["gla_w_gate_up"][0]
    rank = up.shape[1]
    pad = jnp.zeros((2, LANES, up.shape[2]), F32)
    w["gla_w_up_pad"] = pad.at[0, 0:rank].set(up[0]).at[1, rank:2 * rank].set(up[1])
    w.update(_prepare_matrices(big))
    return w


def _prepare_matrices(big):
    w = {}
    if "mlp_w1" in big:
        w["mlp_w1"] = list(big["mlp_w1"])
        w["mlp_w2"] = [t.reshape(-1, t.shape[-1]) for t in big["mlp_w2"]]
    if "ab_w_in" in big:
        w["ab_w_in"] = big["ab_w_in"]
    if "ab_w_out" in big:
        w["ab_w_out"] = big["ab_w_out"].reshape(-1, big["ab_w_out"].shape[-1])
    if "gla_w_in" in big:
        w["gla_w_out"] = big["gla_w_out"].reshape(-1, big["gla_w_out"].shape[-1])
        gla_in = _join_chips(big["gla_w_in"], 1)
        w["gla_w_in_pad"] = jnp.pad(gla_in, ((0, 0), (0, GLA_IN_PAD - gla_in.shape[1])))
    return w


def _finish_grads(g, rank=16, rg_blocks=8):
    def chip_major(t):
        return t.reshape(N_CHIPS, t.shape[0] // N_CHIPS, t.shape[1])

    big = {
        "mlp_w1": list(g["mlp_w1"]), "mlp_w2": [chip_major(t) for t in g["mlp_w2"]],
        "ab_w_in": g["ab_w_in"], "ab_w_out": chip_major(g["ab_w_out"]),
        "gla_w_in": g["gla_w_in"], "gla_w_out": chip_major(g["gla_w_out"]),
    }
    small = {
        "norm_mix": g["norm_mix"], "norm_mlp": g["norm_mlp"], "norm_final": g["norm_final"][0],
        "rg_conv_w": g["rg_conv_w"][None], "rg_conv_b": g["rg_conv_b"],
        "rg_w_a": _block_diag_extract(g["rg_wa_bd"], rg_blocks)[None], "rg_b_a": g["rg_b_a"][None],
        "rg_w_x": _block_diag_extract(g["rg_wx_bd"], rg_blocks)[None], "rg_b_x": g["rg_b_x"][None],
        "rg_lambda": g["rg_lambda"][None], "hg_lb_logits": g["hg_lb_logits"], "hg_norm": g["hg_norm"],
        "gla_w_gate_up": jnp.stack([g["gla_w_up_pad"][0, 0:rank], g["gla_w_up_pad"][1, rank:2 * rank]])[None],
        "gla_b_gate": g["gla_b_gate"][None], "gla_norm": g["gla_norm"],
    }
    return big, small


MATRICES = (("mlp_w1", 0), ("mlp_w1", 1), ("mlp_w2", 0), ("mlp_w2", 1), ("ab_w_in", 0), ("ab_w_out", 0),
            ("gla_w_in", 0), ("gla_w_out", 0))
EARLY_MATRICES = ("ab_w_in",)
SMALL_SHARDED = ("rg_conv_w", "rg_b_a", "rg_b_x", "rg_lambda", "gla_w_gate_up", "gla_b_gate", "gla_norm")
SMALL_REPLICATED = ("norm_mix", "norm_mlp", "norm_final", "rg_conv_b", "rg_w_a", "rg_w_x", "hg_lb_logits", "hg_norm")
WEIGHTS = ("norm_mix", "norm_mlp", "norm_final", "mlp_w1", "mlp_w2", "ab_w_in", "ab_w_out", "rg_conv_w", "rg_conv_b",
           "rg_w_a", "rg_b_a", "rg_w_x", "rg_b_x", "rg_lambda", "hg_lb_logits", "hg_norm", "gla_w_in", "gla_w_out",
           "gla_w_gate_up", "gla_b_gate", "gla_norm")
ROW_ALIGN = 16


def _pack(arrays, lead=0):
    head = arrays[0].shape[:lead]
    flat = jnp.concatenate([a.reshape(head + (-1,)) for a in arrays], axis=lead)
    n = flat.shape[-1]
    quantum = LANES * ROW_ALIGN
    padded = -(-n // quantum) * quantum
    if padded != n:
        flat = jnp.pad(flat, [(0, 0)] * lead + [(0, padded - n)])
    return flat.reshape(head + (padded // LANES, LANES))


def _unpack(buf, shapes, lead=0):
    head = buf.shape[:lead]
    flat = buf.reshape(head + (-1,))
    out, off = [], 0
    for s in shapes:
        n = 1
        for v in s:
            n *= v
        out.append(lax.slice_in_dim(flat, off, off + n, axis=lead).reshape(head + tuple(s)))
        off += n
    return out


def _join_chips(gathered, axis):
    t = jnp.moveaxis(gathered, 0, axis)
    return t.reshape(t.shape[:axis] + (t.shape[axis] * t.shape[axis + 1],) + t.shape[axis + 2:])


def _split_chips(full, axis):
    s = full.shape
    t = full.reshape(s[:axis] + (N_CHIPS, s[axis] // N_CHIPS) + s[axis + 1:])
    return jnp.moveaxis(t, axis, 0)


_ANY = pl.BlockSpec(memory_space=pl.ANY)


def _place():
    return lax.axis_index("x"), lax.axis_index("y"), lax.axis_index("c")


def _into_slot(name, src, slot, n_slots, dtype, tm, layer=None):
    r, lanes = src.shape[-2:]
    tm = _row_tile(r, tm, ROW_ALIGN)

    def body(slot_ref, in_ref, o_ref):
        o_ref[...] = in_ref[...].astype(o_ref.dtype)

    if layer is None:
        in_spec = pl.BlockSpec((tm, lanes), lambda i, slot_ref: (i, 0))
    else:
        in_spec = pl.BlockSpec((None, tm, lanes), lambda i, slot_ref: (layer, i, 0))
    grid_spec = pltpu.PrefetchScalarGridSpec(
        num_scalar_prefetch=1, grid=(r // tm,), in_specs=[in_spec],
        out_specs=pl.BlockSpec((None, tm, lanes), lambda i, slot_ref: (slot_ref[0], i, 0)))
    return pl.pallas_call(
        body, name=name, grid_spec=grid_spec, out_shape=jax.ShapeDtypeStruct((n_slots, r, lanes), dtype),
        compiler_params=_params(("parallel",)),
    )(slot.reshape(1).astype(jnp.int32), src)


def _chip_peers():
    x, y, c = _place()
    return 2 * x + y, c, [(1 - x, y), (x, 1 - y), (1 - x, 1 - y)]


def _comm_call(name, body, ins, out_shapes, n_sems, aliases=None):
    return pl.pallas_call(
        body, name=name, in_specs=[_ANY] * len(ins), out_specs=[_ANY] * len(out_shapes), out_shape=out_shapes,
        input_output_aliases=aliases or {},
        scratch_shapes=[pltpu.SemaphoreType.DMA((n_sems,)), pltpu.SemaphoreType.DMA((n_sems,))],
    )(*ins)


def _gather_chips(name, bufs):
    n = len(bufs)

    def body(*refs):
        outs, send_sems, recv_sems = refs[n:2 * n], refs[2 * n], refs[2 * n + 1]
        x, y, c = _place()
        me, _, peers = _chip_peers()

        def rows(a, block, half):
            rh = outs[a].shape[1] // 2
            return outs[a].at[block, pl.ds(half * rh, rh)]

        def copy(a, j, block, half, to, sem):
            return pltpu.make_async_remote_copy(
                src_ref=rows(a, block, half), dst_ref=rows(a, block, half), send_sem=send_sems.at[sem],
                recv_sem=recv_sems.at[sem], device_id=to, device_id_type=MESH)

        def over_ici(a, j, block):
            px, py = peers[j]
            return copy(a, j, block, c, (px, py, c), 6 * a + j)

        def to_sibling(a, j, block, half):
            return copy(a, j, block, half, (x, y, 1 - c), 6 * a + 3 + j)

        sends = [over_ici(a, j, me) for a in range(n) for j in range(3)]
        for cp in sends:
            cp.start()
        for a in range(n):
            for j, (px, py) in enumerate(peers):
                over_ici(a, j, 2 * px + py).wait_recv()
                handed = to_sibling(a, j, 2 * px + py, c)
                handed.start()
                sends.append(handed)
        for a in range(n):
            for j, (px, py) in enumerate(peers):
                to_sibling(a, j, 2 * px + py, 1 - c).wait_recv()
        for cp in sends:
            cp.wait_send()

    shapes = [jax.ShapeDtypeStruct(b.shape, b.dtype) for b in bufs]
    return _comm_call(name, body, bufs, shapes, 6 * n, {a: a for a in range(n)})


_HBM = pl.BlockSpec(memory_space=pltpu.HBM)
_SEM = pl.BlockSpec(memory_space=pltpu.SEMAPHORE)
_EFFECT = pltpu.SideEffectType.DATAFLOW_SIDE_EFFECTING


def _half_rows(ref, block, half):
    rh = ref.shape[1] // 2
    return ref.at[block, pl.ds(half * rh, rh)]


def _gather_start(name, bufs, after):
    n = len(bufs)

    def body(*refs):
        ins, send_sems, recv_sems, token = refs[:n], refs[n + 1], refs[n + 2], refs[-1]
        me, c, peers = _chip_peers()
        for a in range(n):
            mine = _half_rows(ins[a], me, c)
            for j, (px, py) in enumerate(peers):
                pltpu.make_async_remote_copy(
                    src_ref=mine, dst_ref=mine, send_sem=send_sems.at[3 * a + j], recv_sem=recv_sems.at[3 * a + j],
                    device_id=(px, py, c), device_id_type=MESH).start()
        token[...] = jnp.zeros_like(token)

    out_shape = (pltpu.SemaphoreType.DMA((3 * n,)), pltpu.SemaphoreType.DMA((3 * n,)),
                 *[pltpu.HBM(b.shape, b.dtype) for b in bufs], jax.ShapeDtypeStruct((8, LANES), F32))
    return pl.pallas_call(
        body, name=name, out_shape=out_shape, in_specs=[_HBM] * n + [_ANY],
        out_specs=(_SEM, _SEM, *[_HBM] * n, pl.BlockSpec(memory_space=pltpu.VMEM)),
        input_output_aliases={a: 2 + a for a in range(n)},
        compiler_params=pltpu.CompilerParams(has_side_effects=_EFFECT),
    )(*[pltpu.with_memory_space_constraint(b, pltpu.HBM) for b in bufs], after)


def _gather_wait(name, bufs, send_sems, recv_sems, after):
    n = len(bufs)

    def body(*refs):
        ins, send_sems, recv_sems = refs[:n], refs[n], refs[n + 1]
        me, c, peers = _chip_peers()
        for a in range(n):
            for j, (px, py) in enumerate(peers):
                copy = pltpu.make_async_remote_copy(
                    src_ref=_half_rows(ins[a], me, c), dst_ref=_half_rows(ins[a], 2 * px + py, c),
                    send_sem=send_sems.at[3 * a + j], recv_sem=recv_sems.at[3 * a + j],
                    device_id=(px, py, c), device_id_type=MESH)
                copy.wait_send()
                copy.wait_recv()

    return pl.pallas_call(
        body, name=name, out_shape=tuple(pltpu.HBM(b.shape, b.dtype) for b in bufs),
        in_specs=[_HBM] * n + [_SEM, _SEM, _ANY], out_specs=tuple([_HBM] * n),
        input_output_aliases={a: a for a in range(n)},
        compiler_params=pltpu.CompilerParams(has_side_effects=_EFFECT),
    )(*bufs, send_sems, recv_sems, after)


def _hand_over(name, bufs):
    n = len(bufs)

    def body(*refs):
        outs, send_sems, recv_sems = refs[n:2 * n], refs[2 * n], refs[2 * n + 1]
        x, y, c = _place()
        _, _, peers = _chip_peers()

        def copy(a, j, half):
            px, py = peers[j]
            rows = _half_rows(outs[a], 2 * px + py, half)
            return pltpu.make_async_remote_copy(
                src_ref=rows, dst_ref=rows, send_sem=send_sems.at[3 * a + j], recv_sem=recv_sems.at[3 * a + j],
                device_id=(x, y, 1 - c), device_id_type=MESH)

        sends = [copy(a, j, c) for a in range(n) for j in range(3)]
        for cp in sends:
            cp.start()
        for a in range(n):
            for j in range(3):
                copy(a, j, 1 - c).wait_recv()
        for cp in sends:
            cp.wait_send()

    shapes = [jax.ShapeDtypeStruct(b.shape, b.dtype) for b in bufs]
    return _comm_call(name, body, bufs, shapes, 3 * n, {a: a for a in range(n)})


def _pair_exchange(name, gs, pin=None):
    n = len(gs)
    extra = [] if pin is None else [pin]
    k = n + len(extra)

    def body(*refs):
        ins, outs, send_sems, recv_sems = refs[:n], refs[k:k + n], refs[k + n], refs[k + n + 1]
        x, y, c = _place()
        copies = [pltpu.make_async_remote_copy(
            src_ref=ins[a].at[:, 1 - c], dst_ref=outs[a], send_sem=send_sems.at[a], recv_sem=recv_sems.at[a],
            device_id=(x, y, 1 - c), device_id_type=MESH) for a in range(n)]
        for cp in copies:
            cp.start()
        for cp in copies:
            cp.wait()

    shapes = [jax.ShapeDtypeStruct((g.shape[0],) + g.shape[2:], g.dtype) for g in gs]
    return _comm_call(name, body, list(gs) + extra, shapes, n)


def _pair_add(name, g, got, chip, core):
    n, _, rh, lanes = g.shape
    tm = _row_tile(rh, 1024, ROW_ALIGN)

    def body(idx_ref, g_ref, got_ref, own_ref, o16_ref):
        s = g_ref[...] + got_ref[...]
        o16_ref[...] = s.astype(BF16)

        @pl.when(pl.program_id(1) == idx_ref[1])
        def _():
            own_ref[...] = s

    grid_spec = pltpu.PrefetchScalarGridSpec(
        num_scalar_prefetch=1, grid=(rh // tm, n),
        in_specs=[pl.BlockSpec((None, None, tm, lanes), lambda i, s, idx_ref: (s, idx_ref[0], i, 0)),
                  pl.BlockSpec((None, tm, lanes), lambda i, s, idx_ref: (s, i, 0))],
        out_specs=[pl.BlockSpec((tm, lanes), lambda i, s, idx_ref: (i, 0)),
                   pl.BlockSpec((None, tm, lanes), lambda i, s, idx_ref: (s, i, 0))])
    return pl.pallas_call(
        body, name=name, grid_spec=grid_spec,
        out_shape=[jax.ShapeDtypeStruct((rh, lanes), F32), jax.ShapeDtypeStruct((n, rh, lanes), BF16)],
        compiler_params=_params(("parallel", "arbitrary")),
    )(jnp.stack([core, chip]).astype(jnp.int32), g, got)


def _chip_scatter(name, ps):
    n = len(ps)

    def body(*refs):
        ins, outs, send_sems, recv_sems = refs[:n], refs[n:2 * n], refs[2 * n], refs[2 * n + 1]
        me, c, peers = _chip_peers()

        def copy(a, j, src_block, dst_block):
            px, py = peers[j]
            return pltpu.make_async_remote_copy(
                src_ref=ins[a].at[src_block], dst_ref=outs[a].at[dst_block], send_sem=send_sems.at[3 * a + j],
                recv_sem=recv_sems.at[3 * a + j], device_id=(px, py, c), device_id_type=MESH)

        sends = [copy(a, j, 2 * px + py, me) for a in range(n) for j, (px, py) in enumerate(peers)]
        for cp in sends:
            cp.start()
        for a in range(n):
            for j, (px, py) in enumerate(peers):
                copy(a, j, me, 2 * px + py).wait_recv()
        for cp in sends:
            cp.wait_send()

    shapes = [jax.ShapeDtypeStruct(p.shape, p.dtype) for p in ps]
    return _comm_call(name, body, ps, shapes, 3 * n)


def _scatter_start(name, ps, lands):
    n = len(ps)

    def body(*refs):
        srcs, dsts, send_sems, recv_sems, token = refs[:n], refs[n:2 * n], refs[2 * n], refs[2 * n + 1], refs[-1]
        me, c, peers = _chip_peers()
        for a in range(n):
            for j, (px, py) in enumerate(peers):
                pltpu.make_async_remote_copy(
                    src_ref=srcs[a].at[2 * px + py], dst_ref=dsts[a].at[me], send_sem=send_sems.at[3 * a + j],
                    recv_sem=recv_sems.at[3 * a + j], device_id=(px, py, c), device_id_type=MESH).start()
        token[...] = jnp.zeros_like(token)

    bufs = list(ps) + list(lands)
    out_shape = (pltpu.SemaphoreType.DMA((3 * n,)), pltpu.SemaphoreType.DMA((3 * n,)),
                 *[pltpu.HBM(b.shape, b.dtype) for b in bufs], jax.ShapeDtypeStruct((8, LANES), F32))
    return pl.pallas_call(
        body, name=name, out_shape=out_shape, in_specs=[_HBM] * (2 * n),
        out_specs=(_SEM, _SEM, *[_HBM] * (2 * n), pl.BlockSpec(memory_space=pltpu.VMEM)),
        input_output_aliases={a: 2 + a for a in range(2 * n)},
        compiler_params=pltpu.CompilerParams(has_side_effects=_EFFECT),
    )(*[pltpu.with_memory_space_constraint(b, pltpu.HBM) for b in bufs])


def _scatter_wait(name, ps, lands, send_sems, recv_sems, after):
    n = len(ps)

    def body(*refs):
        srcs, dsts, send_sems, recv_sems = refs[:n], refs[n:2 * n], refs[2 * n], refs[2 * n + 1]
        me, c, peers = _chip_peers()
        for a in range(n):
            for j, (px, py) in enumerate(peers):
                copy = pltpu.make_async_remote_copy(
                    src_ref=srcs[a].at[2 * px + py], dst_ref=dsts[a].at[2 * px + py],
                    send_sem=send_sems.at[3 * a + j], recv_sem=recv_sems.at[3 * a + j],
                    device_id=(px, py, c), device_id_type=MESH)
                copy.wait_send()
                copy.wait_recv()

    bufs = list(ps) + list(lands)
    outs = pl.pallas_call(
        body, name=name, out_shape=tuple(pltpu.HBM(b.shape, b.dtype) for b in bufs),
        in_specs=[_HBM] * (2 * n) + [_SEM, _SEM, _ANY], out_specs=tuple([_HBM] * (2 * n)),
        input_output_aliases={a: a for a in range(2 * n)},
        compiler_params=pltpu.CompilerParams(has_side_effects=_EFFECT),
    )(*bufs, send_sems, recv_sems, after)
    return list(outs[n:])


def _sum_ring(name, own, got, chip, core):
    n, rh, lanes = got.shape
    tm = _row_tile(rh, 2048, ROW_ALIGN)

    def body(idx_ref, own_ref, g1_ref, g2_ref, g3_ref, o_ref):
        o_ref[...] = ((own_ref[...] + g1_ref[...].astype(F32)) + g2_ref[...].astype(F32)) + g3_ref[...].astype(F32)

    def block(k):
        return pl.BlockSpec((None, tm, lanes), lambda i, idx_ref: ((idx_ref[0] + k) % n, i, 0))

    grid_spec = pltpu.PrefetchScalarGridSpec(
        num_scalar_prefetch=1, grid=(rh // tm,),
        in_specs=[pl.BlockSpec((tm, lanes), lambda i, idx_ref: (i, 0)), block(1), block(2), block(3)],
        out_specs=pl.BlockSpec((None, tm, lanes), lambda i, idx_ref: (idx_ref[1], i, 0)))
    return pl.pallas_call(
        body, name=name, grid_spec=grid_spec, out_shape=jax.ShapeDtypeStruct((2, rh, lanes), F32),
        compiler_params=_params(("parallel",)),
    )(jnp.stack([chip, core]).astype(jnp.int32), own, got, got, got)


def _pair_gather(name, bufs):
    n = len(bufs)

    def body(*refs):
        ins, outs, send_sems, recv_sems = refs[:n], refs[n:2 * n], refs[2 * n], refs[2 * n + 1]
        x, y, c = _place()

        def copy(a, block):
            return pltpu.make_async_remote_copy(
                src_ref=ins[a].at[block], dst_ref=outs[a].at[block], send_sem=send_sems.at[a],
                recv_sem=recv_sems.at[a], device_id=(x, y, 1 - c), device_id_type=MESH)

        sends = [copy(a, c) for a in range(n)]
        for cp in sends:
            cp.start()
        for a in range(n):
            copy(a, 1 - c).wait_recv()
        for cp in sends:
            cp.wait_send()

    shapes = [jax.ShapeDtypeStruct(b.shape, b.dtype) for b in bufs]
    return _comm_call(name, body, bufs, shapes, n, {a: a for a in range(n)})


def _all_peers():
    x, y, c = _place()
    peers = []
    for mask in range(1, N_DEV):
        fx, fy, fc = (mask >> 2) & 1, (mask >> 1) & 1, mask & 1
        peers.append((jnp.where(fx, 1 - x, x), jnp.where(fy, 1 - y, y), jnp.where(fc, 1 - c, c)))
    return 4 * x + 2 * y + c, peers


def _gather_all_start(name, buf):
    def body(in_ref, send_sems, recv_sems, out_ref, token):
        me, peers = _all_peers()
        for j, peer in enumerate(peers):
            pltpu.make_async_remote_copy(
                src_ref=in_ref.at[me], dst_ref=in_ref.at[me], send_sem=send_sems.at[j], recv_sem=recv_sems.at[j],
                device_id=peer, device_id_type=MESH).start()
        token[...] = jnp.zeros_like(token)

    n = N_DEV - 1
    return pl.pallas_call(
        body, name=name, in_specs=[_HBM],
        out_shape=(pltpu.SemaphoreType.DMA((n,)), pltpu.SemaphoreType.DMA((n,)), pltpu.HBM(buf.shape, buf.dtype),
                   jax.ShapeDtypeStruct((8, LANES), F32)),
        out_specs=(_SEM, _SEM, _HBM, pl.BlockSpec(memory_space=pltpu.VMEM)), input_output_aliases={0: 2},
        compiler_params=pltpu.CompilerParams(has_side_effects=_EFFECT),
    )(pltpu.with_memory_space_constraint(buf, pltpu.HBM))


def _gather_all_wait(name, buf, send_sems, recv_sems, after):
    def body(in_ref, send_sems, recv_sems, after_ref, out_ref):
        me, peers = _all_peers()
        for j, (px, py, pc) in enumerate(peers):
            copy = pltpu.make_async_remote_copy(
                src_ref=in_ref.at[me], dst_ref=in_ref.at[4 * px + 2 * py + pc], send_sem=send_sems.at[j],
                recv_sem=recv_sems.at[j], device_id=(px, py, pc), device_id_type=MESH)
            copy.wait_send()
            copy.wait_recv()

    return pl.pallas_call(
        body, name=name, in_specs=[_HBM, _SEM, _SEM, _ANY], out_shape=pltpu.HBM(buf.shape, buf.dtype),
        out_specs=_HBM, input_output_aliases={0: 0},
        compiler_params=pltpu.CompilerParams(has_side_effects=_EFFECT),
    )(buf, send_sems, recv_sems, after)


def _sum_blocks(name, stacked, tm):
    n, r, lanes = stacked.shape

    def body(in_ref, o_ref):
        acc = in_ref[0]
        for j in range(1, n):
            acc = acc + in_ref[j]
        o_ref[...] = acc

    return pl.pallas_call(
        body, name=name, grid=(r // tm,), in_specs=[pl.BlockSpec((n, tm, lanes), lambda i: (0, i, 0))],
        out_specs=pl.BlockSpec((tm, lanes), lambda i: (i, 0)), out_shape=jax.ShapeDtypeStruct((r, lanes), F32),
        compiler_params=_params(("parallel",)),
    )(stacked)


def _row_tile(rows, pref, align):
    best = None
    for t in range(align, min(rows, pref) + 1, align):
        if rows % t == 0:
            best = t
    assert best is not None, (rows, pref, align)
    return best


def _adam(name, w, g, m, v):
    rows, width = w.shape
    tm = _row_tile(rows, max(8, 4096 * LANES // width), 8)
    args = [(t, width, 0) for t in (w, g, m, v)]
    return _rowcall(name, _adam_fn, args, [], [(width, F32)] * 3, tm=tm)


def kernel(x, norm_mix, norm_mlp, norm_final, mlp_w1, mlp_w2, ab_w_in, ab_w_out, rg_conv_w, rg_conv_b, rg_w_a, rg_b_a, rg_w_x, rg_b_x, rg_lambda, hg_lb_logits, hg_norm, gla_w_in, gla_w_out, gla_w_gate_up, gla_b_gate, gla_norm, loss_target, m_norm_mix, m_norm_mlp, m_norm_final, m_mlp_w1, m_mlp_w2, m_ab_w_in, m_ab_w_out, m_rg_conv_w, m_rg_conv_b, m_rg_w_a, m_rg_b_a, m_rg_w_x, m_rg_b_x, m_rg_lambda, m_hg_lb_logits, m_hg_norm, m_gla_w_in, m_gla_w_out, m_gla_w_gate_up, m_gla_b_gate, m_gla_norm, v_norm_mix, v_norm_mlp, v_norm_final, v_mlp_w1, v_mlp_w2, v_ab_w_in, v_ab_w_out, v_rg_conv_w, v_rg_conv_b, v_rg_w_a, v_rg_b_a, v_rg_w_x, v_rg_b_x, v_rg_lambda, v_hg_lb_logits, v_hg_norm, v_gla_w_in, v_gla_w_out, v_gla_w_gate_up, v_gla_b_gate, v_gla_norm):
    w = dict(norm_mix=norm_mix, norm_mlp=norm_mlp, norm_final=norm_final, mlp_w1=mlp_w1, mlp_w2=mlp_w2, ab_w_in=ab_w_in, ab_w_out=ab_w_out, rg_conv_w=rg_conv_w, rg_conv_b=rg_conv_b, rg_w_a=rg_w_a, rg_b_a=rg_b_a, rg_w_x=rg_w_x, rg_b_x=rg_b_x, rg_lambda=rg_lambda, hg_lb_logits=hg_lb_logits, hg_norm=hg_norm, gla_w_in=gla_w_in, gla_w_out=gla_w_out, gla_w_gate_up=gla_w_gate_up, gla_b_gate=gla_b_gate, gla_norm=gla_norm)
    m = dict(norm_mix=m_norm_mix, norm_mlp=m_norm_mlp, norm_final=m_norm_final, mlp_w1=m_mlp_w1, mlp_w2=m_mlp_w2, ab_w_in=m_ab_w_in, ab_w_out=m_ab_w_out, rg_conv_w=m_rg_conv_w, rg_conv_b=m_rg_conv_b, rg_w_a=m_rg_w_a, rg_b_a=m_rg_b_a, rg_w_x=m_rg_w_x, rg_b_x=m_rg_b_x, rg_lambda=m_rg_lambda, hg_lb_logits=m_hg_lb_logits, hg_norm=m_hg_norm, gla_w_in=m_gla_w_in, gla_w_out=m_gla_w_out, gla_w_gate_up=m_gla_w_gate_up, gla_b_gate=m_gla_b_gate, gla_norm=m_gla_norm)
    v = dict(norm_mix=v_norm_mix, norm_mlp=v_norm_mlp, norm_final=v_norm_final, mlp_w1=v_mlp_w1, mlp_w2=v_mlp_w2, ab_w_in=v_ab_w_in, ab_w_out=v_ab_w_out, rg_conv_w=v_rg_conv_w, rg_conv_b=v_rg_conv_b, rg_w_a=v_rg_w_a, rg_b_a=v_rg_b_a, rg_w_x=v_rg_w_x, rg_b_x=v_rg_b_x, rg_lambda=v_rg_lambda, hg_lb_logits=v_hg_lb_logits, hg_norm=v_hg_norm, gla_w_in=v_gla_w_in, gla_w_out=v_gla_w_out, gla_w_gate_up=v_gla_w_gate_up, gla_b_gate=v_gla_b_gate, gla_norm=v_gla_norm)
    chip = 2 * lax.axis_index("x") + lax.axis_index("y")
    core = lax.axis_index("c")
    sharded_shapes = [w[n].shape for n in SMALL_SHARDED]

    slots = [_into_slot(f"cast_{n}{layer}", w[n], chip, N_CHIPS, BF16, 512, layer) for n, layer in MATRICES]
    early = [i for i, (n, _) in enumerate(MATRICES) if n in EARLY_MATRICES]
    rest = [i for i in range(len(MATRICES)) if i not in early]

    def named(indices, arrays):
        big = {}
        for i, t in zip(indices, arrays):
            big.setdefault(MATRICES[i][0], []).append(t)
        return {n: (v if n in ("mlp_w1", "mlp_w2") else v[0]) for n, v in big.items()}

    gathered = _gather_chips("gather_early", [slots[i] for i in early])
    send_sems, recv_sems, *in_flight, token = _gather_start("gather_rest_start", [slots[i] for i in rest], gathered[0])

    def late_weights(after):
        landed = _gather_wait("gather_rest_wait", in_flight, send_sems, recv_sems, after)
        return _prepare_matrices(named(rest, _hand_over("gather_rest_share", list(landed))))

    big = named(early, gathered)
    vectors = _pack([w[n] for n in SMALL_SHARDED])
    vectors = _into_slot("place_vectors", vectors, chip, N_CHIPS, F32, vectors.shape[0])
    small_all = _unpack(_gather_chips("gather_vectors", [vectors])[0], sharded_shapes, lead=1)
    full = {n: w[n] for n in SMALL_REPLICATED}
    for n, t in zip(SMALL_SHARDED, small_all):
        full[n] = _join_chips(t, t.ndim - 2)

    def pair_sums(tag, arrays, pin=None):
        halves = [t.reshape(N_CHIPS, 2, t.shape[1] // 2, t.shape[2]) for t in arrays]
        from_sibling = _pair_exchange(f"reduce_pair_{tag}", halves, pin)
        return [_pair_add(f"reduce_pair_add_{tag}{i}", h, s, chip, core)
                for i, (h, s) in enumerate(zip(halves, from_sibling))]

    in_flight_grads = {}

    def emit(tag, arrays):
        parts = pair_sums(tag, arrays)
        p16 = [p for _, p in parts]
        send, recv, *rest = _scatter_start(f"reduce_chips_{tag}_start", p16, [lax.empty(p.shape, p.dtype) for p in p16])
        in_flight_grads[tag] = ([p for p, _ in parts], rest[:len(p16)], rest[len(p16):-1], send, recv)
        return rest[-1]

    loss_part, grad_x, g_kernel = _local_step(
        x[0], loss_target[0], _prepare_weights(big, full), token, late_weights, emit)
    g_big, g_full = _finish_grads(g_kernel)

    small_names = SMALL_REPLICATED + SMALL_SHARDED
    reduced_shapes = [g_full[n].shape for n in small_names] + [loss_part.shape]
    g_small = _pack([g_full[n] for n in small_names] + [loss_part])
    device = 2 * chip + core
    g_small = _into_slot("place_small", g_small, device, N_DEV, F32, g_small.shape[0])
    small_send, small_recv, small_in_flight, small_token = _gather_all_start("reduce_small_start", g_small)

    mine = {}
    last = pair_sums("ab", [g_big["ab_w_in"], g_big["ab_w_out"]], small_token)
    from_chips = _chip_scatter("reduce_chips_ab", [p for _, p in last])
    mine["ab"] = [_sum_ring(f"reduce_chips_add_ab{i}", p32, f, chip, core)
                  for i, ((p32, _), f) in enumerate(zip(last, from_chips))]
    for tag, (p32s, p16s, lands, send, recv) in in_flight_grads.items():
        landed = _scatter_wait(f"reduce_chips_{tag}_wait", p16s, lands, send, recv, mine["ab"][0])
        mine[tag] = [_sum_ring(f"reduce_chips_add_{tag}{i}", p32, f, chip, core)
                     for i, (p32, f) in enumerate(zip(p32s, landed))]
    ordered = [mine["mlp0"][0], mine["mlp1"][0], mine["mlp0"][1], mine["mlp1"][1], *mine["ab"], *mine["gla"]]
    reduced = [t.reshape(2 * t.shape[1], t.shape[2]) for t in _pair_gather("reduce_share", ordered)]
    by_name = {n: [] for n, _ in MATRICES}
    for (n, _), t in zip(MATRICES, reduced):
        by_name[n].append(t)
    grads = {n: jnp.stack(v) for n, v in by_name.items()}

    g_small_all = _gather_all_wait("reduce_small_wait", small_in_flight, small_send, small_recv, reduced[0])
    g_small_red = _sum_blocks("reduce_small_add", g_small_all, g_small_all.shape[1])
    *small_red, loss_sum = _unpack(g_small_red, reduced_shapes)
    loss = loss_sum[0, 0]
    g_small_full = dict(zip(small_names, small_red))
    for n in SMALL_REPLICATED:
        grads[n] = g_small_full[n]
    for n in SMALL_SHARDED:
        width = w[n].shape[-1]
        grads[n] = lax.dynamic_slice_in_dim(g_small_full[n], chip * width, width, axis=g_small_full[n].ndim - 1)

    delta, new_m, new_v = {}, {}, {}
    for n in by_name:
        flat = [t.reshape(-1, t.shape[-1]) for t in (w[n], grads[n], m[n], v[n])]
        for dst, t in zip((delta, new_m, new_v), _adam(f"adam_{n}", *flat)):
            dst[n] = t.reshape(w[n].shape)
    small_shapes = [w[n].shape for n in small_names]
    packs = [_pack([src[n] for n in small_names]) for src in (w, grads, m, v)]
    d_small, m_small, v_small = _adam("adam_small", *packs)
    for dst, buf in ((delta, d_small), (new_m, m_small), (new_v, v_small)):
        dst.update(zip(small_names, _unpack(buf, small_shapes)))

    return (loss, grad_x[None], *[grads[n] for n in WEIGHTS], *[delta[n] for n in WEIGHTS],
            *[new_m[n] for n in WEIGHTS], *[new_v[n] for n in WEIGHTS])
```

```python
import functools

import jax
import jax.numpy as jnp
from jax import lax
from jax.experimental import pallas as pl
from jax.experimental.pallas import tpu as pltpu

F32 = jnp.float32
BF16 = jnp.bfloat16
MESH = pl.DeviceIdType.MESH

LANES = 128
CHUNK = 64
ATTN_SUB = 2
EPS = 1e-6
RG_C = 8.0
N_CHIPS = 4
N_DEV = 8
GLA_IN_WIDTH = 3104
GLA_IN_PAD = 3200
VMEM_LIMIT = 56 * 1024 * 1024

ADAM_LR = 0.001
ADAM_B1 = 0.9
ADAM_B2 = 0.999
ADAM_EPS = 1e-08
ADAM_WD = 0.01
ADAM_STEP = 10


def _raw_dot(a, b, ca, cb):
    return lax.dot_general(a.astype(BF16), b.astype(BF16), (((ca,), (cb,)), ((), ())),
                           preferred_element_type=F32)


def _raw_nn(a, b):
    return _raw_dot(a, b, 1, 0)


def _raw_nt(a, b):
    return _raw_dot(a, b, 1, 1)


def _raw_tn(a, b):
    return _raw_dot(a, b, 0, 0)


@jax.custom_vjp
def _dot_nn(a, b):
    return _raw_nn(a, b)


def _dot_nn_fwd(a, b):
    return _raw_nn(a, b), (a, b)


def _dot_nn_bwd(res, g):
    a, b = res
    return _raw_nt(g, b), _raw_tn(a, g)


_dot_nn.defvjp(_dot_nn_fwd, _dot_nn_bwd)


@jax.custom_vjp
def _dot_nt(a, b):
    return _raw_nt(a, b)


def _dot_nt_fwd(a, b):
    return _raw_nt(a, b), (a, b)


def _dot_nt_bwd(res, g):
    a, b = res
    return _raw_nn(g, b), _raw_tn(g, a)


_dot_nt.defvjp(_dot_nt_fwd, _dot_nt_bwd)


@jax.custom_vjp
def _dot_tn(a, b):
    return _raw_tn(a, b)


def _dot_tn_fwd(a, b):
    return _raw_tn(a, b), (a, b)


def _dot_tn_bwd(res, g):
    a, b = res
    return _raw_nt(b, g), _raw_nn(a, g)


_dot_tn.defvjp(_dot_tn_fwd, _dot_tn_bwd)


def _tile(n, pref):
    if n <= pref:
        return n
    t = (pref // LANES) * LANES
    while t > LANES and n % t:
        t -= LANES
    assert n % t == 0, (n, pref)
    return t


def _params(sem):
    return pltpu.CompilerParams(dimension_semantics=sem, vmem_limit_bytes=VMEM_LIMIT)


def _rowcall(name, fn, rows, pars, row_outs, par_outs=(), tm=256, pin=None):
    if pin is not None:
        inner, pars = fn, list(pars) + [pin]
        fn = lambda *vals: inner(*vals[:-1])
    n_rows = rows[0][0].shape[0]
    tm = min(tm, n_rows)
    assert n_rows % tm == 0
    n_r, n_p, n_ro = len(rows), len(pars), len(row_outs)

    def body(*refs):
        vals = [r[...].astype(F32) for r in refs[:n_r + n_p]]
        outs = fn(*vals)
        o_refs = refs[n_r + n_p:n_r + n_p + n_ro]
        po_refs = refs[n_r + n_p + n_ro:]
        for o_ref, val in zip(o_refs, outs[:n_ro]):
            o_ref[...] = val.astype(o_ref.dtype)
        first = pl.program_id(0) == 0
        for po_ref, val in zip(po_refs, outs[n_ro:]):
            @pl.when(first)
            def _():
                po_ref[...] = val

            @pl.when(jnp.logical_not(first))
            def _():
                po_ref[...] += val

    def const_map(nd):
        return lambda i: (0,) * nd

    def row_spec(w, cb):
        return pl.BlockSpec((tm, w), lambda i: (i, cb))

    in_specs = [row_spec(w, cb) for _, w, cb in rows]
    in_specs += [pl.BlockSpec(p.shape, const_map(p.ndim)) for p in pars]
    out_specs = [pl.BlockSpec((tm, w), lambda i: (i, 0)) for w, _ in row_outs]
    out_specs += [pl.BlockSpec(tuple(s), const_map(len(s))) for s in par_outs]
    out_shape = [jax.ShapeDtypeStruct((n_rows, w), dt) for w, dt in row_outs]
    out_shape += [jax.ShapeDtypeStruct(tuple(s), F32) for s in par_outs]
    return pl.pallas_call(
        body, name=name, grid=(n_rows // tm,), in_specs=in_specs, out_specs=out_specs, out_shape=out_shape,
        compiler_params=_params(("arbitrary",) if par_outs else ("parallel",)),
    )(*[r[0] for r in rows], *pars)


def _vjp_of(fn, n_prim, n_out, n_par, n_pass=0):
    def bwd(*args):
        prim = args[:n_prim]
        cts = args[n_prim:n_prim + n_out]
        passes = args[n_prim + n_out:n_prim + n_out + 2 * n_pass]
        pars = args[n_prim + n_out + 2 * n_pass:]
        _, vjp = jax.vjp(fn, *prim, *pars)
        grads = vjp(tuple(cts))
        sums = tuple(passes[2 * i] + passes[2 * i + 1] for i in range(n_pass))
        return tuple(grads[:n_prim]) + sums + tuple(grads[n_prim:])
    return bwd


def _mm(name, a, b, mode="nn", extras=(), epi=None, out_dtypes=(F32,), a_pro=None, out_split=None,
        tm=1024, tn=1024, tk=1024):
    split = b.shape[0] if b.ndim == 3 else None
    b_rows, b_cols = b.shape[-2:]
    if mode == "nn":
        (m, k), n = a.shape, b_cols * (split or 1)
    elif mode == "nt":
        (m, k), n = a.shape, b_rows
        assert k == b_cols * (split or 1)
    else:
        assert split is None
        (k, m), n = a.shape, b_cols
    tm, tk = _tile(m, tm), _tile(k, tk)
    tn = _tile(n // out_split, tn) if out_split else _tile(n, tn)
    if split and mode == "nn":
        tn = _tile(b_cols, tn)
    if split and mode == "nt":
        tk = _tile(b_cols, tk)
    nk = k // tk
    raw = {"nn": _raw_nn, "nt": _raw_nt, "tn": _raw_tn}[mode]
    n_e, n_o = len(extras), len(out_dtypes)
    if epi is None:
        epi = lambda acc: (acc,)

    def body(a_ref, b_ref, *rest):
        e_refs, o_refs = rest[:n_e], rest[n_e:n_e + n_o]
        kk = pl.program_id(2)
        a_tile = a_ref[...] if a_pro is None else a_pro(a_ref[...].astype(F32))
        part = raw(a_tile, b_ref[...])

        def finish(total):
            res = epi(total, *[e[...].astype(F32) for e in e_refs])
            for o_ref, r in zip(o_refs, res):
                o_ref[...] = r.astype(o_ref.dtype)

        if nk == 1:
            finish(part)
            return
        acc = rest[-1]

        @pl.when(kk == 0)
        def _():
            acc[...] = part

        @pl.when((kk > 0) & (kk < nk - 1))
        def _():
            acc[...] += part

        @pl.when(kk == nk - 1)
        def _():
            finish(acc[...] + part)

    a_spec = pl.BlockSpec((tk, tm), lambda i, j, kk: (kk, i)) if mode == "tn" else pl.BlockSpec((tm, tk), lambda i, j, kk: (i, kk))
    if split and mode == "nn":
        per = b_cols // tn
        b_spec = pl.BlockSpec((None, tk, tn), lambda i, j, kk: (j // per, kk, j % per))
    elif split:
        per = b_cols // tk
        b_spec = pl.BlockSpec((None, tn, tk), lambda i, j, kk: (kk // per, j, kk % per))
    elif mode == "nt":
        b_spec = pl.BlockSpec((tn, tk), lambda i, j, kk: (j, kk))
    else:
        b_spec = pl.BlockSpec((tk, tn), lambda i, j, kk: (kk, j))
    mn_spec = pl.BlockSpec((tm, tn), lambda i, j, kk: (i, j))
    if out_split:
        assert not extras
        per_out = n // out_split // tn
        out_spec = pl.BlockSpec((None, tm, tn), lambda i, j, kk: (j // per_out, i, j % per_out))
        out_shapes = [jax.ShapeDtypeStruct((out_split, m, n // out_split), dt) for dt in out_dtypes]
    else:
        out_spec = mn_spec
        out_shapes = [jax.ShapeDtypeStruct((m, n), dt) for dt in out_dtypes]
    outs = pl.pallas_call(
        body, name=name, grid=(m // tm, n // tn, nk),
        in_specs=[a_spec, b_spec] + [mn_spec] * n_e, out_specs=[out_spec] * n_o,
        out_shape=out_shapes,
        scratch_shapes=[pltpu.VMEM((tm, tn), F32)] if nk > 1 else [],
        compiler_params=_params(("parallel", "parallel", "arbitrary")),
    )(a, b, *extras)
    return outs[0] if n_o == 1 else outs


def _sigmoid(x):
    return jax.nn.sigmoid(x)


def _silu(x):
    return x * _sigmoid(x)


def _softplus(x):
    return jnp.maximum(x, 0.0) + jnp.log1p(jnp.exp(-jnp.abs(x)))


def _rmsnorm_fn(x, gain):
    return (x * lax.rsqrt(jnp.mean(x * x, axis=-1, keepdims=True) + EPS) * gain,)


def _head_norm(o, gain, n_heads):
    w = o.shape[-1] // n_heads
    parts = []
    for h in range(n_heads):
        oh = o[:, h * w:(h + 1) * w]
        parts.append(oh * lax.rsqrt(jnp.mean(oh * oh, axis=-1, keepdims=True) + EPS))
    return jnp.concatenate(parts, axis=-1) * gain


@jax.custom_jvp
def _neg_expm1(x):
    u = jnp.exp(x)
    is_one = u == 1.0
    return jnp.where(is_one, -x, (1.0 - u) * x / jnp.log(jnp.where(is_one, 2.0, u)))


@_neg_expm1.defjvp
def _neg_expm1_jvp(primals, tangents):
    (x,), (t,) = primals, tangents
    return _neg_expm1(x), -jnp.exp(x) * t


def _rg_gates_fn(xc, wa, wx, ba, bx, lam):
    outs = []
    for d in range(2):
        r = _sigmoid(_dot_nn(xc, wa[d]) + ba[d:d + 1])
        i = _sigmoid(_dot_nn(xc, wx[d]) + bx[d:d + 1])
        log_a = -RG_C * r * _softplus(-lam[d:d + 1])
        outs.append(jnp.exp(log_a))
        outs.append(jnp.sqrt(_neg_expm1(2.0 * log_a)) * (i * xc))
    return tuple(outs)


def _hg_pre_fn(q, f_f, f_b, logits):
    mx = jnp.maximum(logits[0:1], logits[1:2])
    e0 = jnp.exp(logits[0:1] - mx)
    e1 = jnp.exp(logits[1:2] - mx)
    lb = e0 / (e0 + e1)
    outs = [_silu(q)]
    for f in (f_f, f_b):
        outs.append((1.0 - lb) * _sigmoid(-f))
        outs.append(jnp.log(lb + (1.0 - lb) * _sigmoid(f)))
    return tuple(outs)


def _post0_fn(hs, ga, o, g, gain):
    ya = hs * jax.nn.gelu(ga, approximate=True)
    yb = _head_norm(o, gain, 4) * _silu(g)
    return (jnp.concatenate([ya, yb], axis=-1),)


def _post0_fwd_fn(h_f, h_b, ga, o_f, o_b, g, gain):
    return _post0_fn(h_f + h_b, ga, o_f + o_b, g, gain)


def _post0_bwd_fn(h_f, h_b, ga, o_f, o_b, g, dmix, gain):
    _, vjp = jax.vjp(_post0_fn, h_f + h_b, ga, o_f + o_b, g, gain)
    return vjp((dmix,))


def _gla_pre_fn(q, lr, w_up, b_gate):
    outs = [q * (128.0 ** -0.5)]
    for d in range(2):
        z = _dot_nn(lr, w_up[d]) + b_gate[d:d + 1]
        outs.append(-_softplus(-z) * (1.0 / 16.0))
    return tuple(outs)


def _gla_post_fn(o, r, gain):
    return (_head_norm(o, gain, 4) * _silu(r),)


def _gla_post_fwd_fn(o_f, o_b, r, gain):
    return _gla_post_fn(o_f + o_b, r, gain)


def _gla_post_bwd_fn(o_f, o_b, r, dmix, gain):
    _, vjp = jax.vjp(_gla_post_fn, o_f + o_b, r, gain)
    return vjp((dmix,))


def _relu2_bwd_epi(acc, hid):
    return (acc * 2.0 * jnp.maximum(hid, 0.0),)


def _relu2(x):
    r = jnp.maximum(x, 0.0)
    return r * r


def _add_epi(acc, res):
    return (acc + res,)


def _loss_head_fn(h, target, gain):
    def f(h, gain):
        y = _rmsnorm_fn(h, gain)[0]
        err = y - target
        return 0.5 * jnp.sum(jnp.mean(err * err, axis=-1, keepdims=True))
    loss, (dh, dgain) = jax.value_and_grad(f, argnums=(0, 1))(h, gain)
    return dh, jnp.full((1, LANES), loss, F32), dgain


def _adam_fn(w, g, m, v):
    m2 = ADAM_B1 * m + (1.0 - ADAM_B1) * g
    v2 = ADAM_B2 * v + (1.0 - ADAM_B2) * (g * g)
    m_hat = m2 / (1.0 - ADAM_B1 ** ADAM_STEP)
    v_hat = v2 / (1.0 - ADAM_B2 ** ADAM_STEP)
    delta = -ADAM_LR * (m_hat / (jnp.sqrt(v_hat) + ADAM_EPS) + ADAM_WD * w)
    return delta, m2, v2


def _shifted(x, t_idx, off):
    n = x.shape[0]
    rolled = pltpu.roll(x, (-off) % n, 0)
    valid = (t_idx + off >= 0) & (t_idx + off < n)
    return jnp.where(valid, rolled, 0.0)


def _conv_fwd(name, src, colblock, w, b):
    n_rows, width = src.shape[0], w.shape[1]

    def body(x_ref, w_ref, b_ref, o_ref):
        x = x_ref[...]
        t_idx = lax.broadcasted_iota(jnp.int32, x.shape, 0)
        acc = b_ref[...] + w_ref[2:3, :] * x
        acc += w_ref[0:1, :] * _shifted(x, t_idx, -2)
        acc += w_ref[1:2, :] * _shifted(x, t_idx, -1)
        acc += w_ref[3:4, :] * _shifted(x, t_idx, 1)
        o_ref[...] = acc

    nb = width // LANES
    return pl.pallas_call(
        body, name=name, grid=(nb,),
        in_specs=[pl.BlockSpec((n_rows, LANES), lambda j: (0, colblock * nb + j)),
                  pl.BlockSpec((4, LANES), lambda j: (0, j)), pl.BlockSpec((1, LANES), lambda j: (0, j))],
        out_specs=pl.BlockSpec((n_rows, LANES), lambda j: (0, j)),
        out_shape=jax.ShapeDtypeStruct((n_rows, width), F32),
        compiler_params=_params(("parallel",)),
    )(src, w, b)


def _conv_bwd(name, src, colblock, w, d):
    n_rows, width = src.shape[0], w.shape[1]

    def body(x_ref, w_ref, d_ref, dx_ref, dw_ref, db_ref):
        x = x_ref[...]
        g = d_ref[...]
        t_idx = lax.broadcasted_iota(jnp.int32, x.shape, 0)
        dx = w_ref[2:3, :] * g
        dx += w_ref[0:1, :] * _shifted(g, t_idx, 2)
        dx += w_ref[1:2, :] * _shifted(g, t_idx, 1)
        dx += w_ref[3:4, :] * _shifted(g, t_idx, -1)
        dx_ref[...] = dx.astype(dx_ref.dtype)
        dw_ref[0:1, :] = jnp.sum(g * _shifted(x, t_idx, -2), axis=0, keepdims=True)
        dw_ref[1:2, :] = jnp.sum(g * _shifted(x, t_idx, -1), axis=0, keepdims=True)
        dw_ref[2:3, :] = jnp.sum(g * x, axis=0, keepdims=True)
        dw_ref[3:4, :] = jnp.sum(g * _shifted(x, t_idx, 1), axis=0, keepdims=True)
        db_ref[...] = jnp.sum(g, axis=0, keepdims=True)

    nb = width // LANES
    return pl.pallas_call(
        body, name=name, grid=(nb,),
        in_specs=[pl.BlockSpec((n_rows, LANES), lambda j: (0, colblock * nb + j)),
                  pl.BlockSpec((4, LANES), lambda j: (0, j)),
                  pl.BlockSpec((n_rows, LANES), lambda j: (0, j))],
        out_specs=[pl.BlockSpec((n_rows, LANES), lambda j: (0, j)), pl.BlockSpec((4, LANES), lambda j: (0, j)),
                   pl.BlockSpec((1, LANES), lambda j: (0, j))],
        out_shape=[jax.ShapeDtypeStruct((n_rows, width), BF16), jax.ShapeDtypeStruct((4, width), F32),
                   jax.ShapeDtypeStruct((1, width), F32)],
        compiler_params=_params(("parallel",)),
    )(src, w, d)


SUBLANES = 8
SCAN_UNROLL = 8


def _shift_rows(x, d, fill):
    n = x.shape[0]
    t = lax.broadcasted_iota(jnp.int32, x.shape, 0)
    valid = (t >= d) if d > 0 else (t < n + d)
    return jnp.where(valid, pltpu.roll(x, d % n, 0), fill)


def _tile_scan(a, u, reverse):
    d = 1
    while d < a.shape[0]:
        s = -d if reverse else d
        a_sh, u_sh = _shift_rows(a, s, 1.0), _shift_rows(u, s, 0.0)
        u = u + a * u_sh
        a = a * a_sh
        d *= 2
    return a, u


def _edge_row(x, reverse):
    return x[0:1, :] if reverse else x[SUBLANES - 1:SUBLANES, :]


def _scan_specs(n_rows, n):
    return [pl.BlockSpec((n_rows, LANES), lambda j: (0, j))] * n


def _scan_fwd(name, a, u, reverse):
    n_rows, width = a.shape
    n_tiles = n_rows // SUBLANES

    def body(a_ref, u_ref, h_ref):
        def step(i, carry):
            tile = (n_tiles - 1 - i) if reverse else i
            rows = pl.ds(pl.multiple_of(tile * SUBLANES, SUBLANES), SUBLANES)
            acc_a, acc_u = _tile_scan(a_ref[rows, :], u_ref[rows, :], reverse)
            h = acc_u + acc_a * carry
            h_ref[rows, :] = h
            return _edge_row(h, reverse)
        lax.fori_loop(0, n_tiles, step, jnp.zeros((1, LANES), F32), unroll=SCAN_UNROLL)

    return pl.pallas_call(
        body, name=name, grid=(width // LANES,), in_specs=_scan_specs(n_rows, 2), out_specs=_scan_specs(n_rows, 1)[0],
        out_shape=jax.ShapeDtypeStruct((n_rows, width), F32), compiler_params=_params(("parallel",)),
    )(a, u)


def _scan_bwd(name, a, h, dh, reverse):
    n_rows, width = a.shape
    n_tiles = n_rows // SUBLANES
    against = not reverse
    one = -1 if against else 1

    def body(a_ref, h_ref, dh_ref, du_ref, da_ref):
        def step(i, carry):
            g_in, a_edge = carry
            tile = (n_tiles - 1 - i) if against else i
            start = pl.multiple_of(tile * SUBLANES, SUBLANES)
            rows = pl.ds(start, SUBLANES)
            a_tile = a_ref[rows, :]
            coeff = _shift_rows(a_tile, one, a_edge)
            acc_a, acc_u = _tile_scan(coeff, dh_ref[rows, :], against)
            g = acc_u + acc_a * g_in
            du_ref[rows, :] = g
            outside = (start + SUBLANES) if reverse else (start - 1)
            inside = (outside >= 0) & (outside < n_rows)
            h_edge = jnp.where(inside, h_ref[pl.ds(jnp.clip(outside, 0, n_rows - 1), 1), :], 0.0)
            da_ref[rows, :] = g * _shift_rows(h_ref[rows, :], -one, h_edge)
            return _edge_row(g, against), _edge_row(a_tile, against)
        zero = jnp.zeros((1, LANES), F32)
        lax.fori_loop(0, n_tiles, step, (zero, zero), unroll=SCAN_UNROLL)

    return pl.pallas_call(
        body, name=name, grid=(width // LANES,), in_specs=_scan_specs(n_rows, 3), out_specs=_scan_specs(n_rows, 2),
        out_shape=[jax.ShapeDtypeStruct((n_rows, width), F32)] * 2, compiler_params=_params(("parallel",)),
    )(a, h, dh)


def _tri_mask(c, reverse):
    row = lax.broadcasted_iota(jnp.int32, (c, c), 0)
    col = lax.broadcasted_iota(jnp.int32, (c, c), 1)
    return (col >= row) if reverse else (col <= row)


def _cumsum_rows(x, reverse):
    tri = _tri_mask(x.shape[0], reverse).astype(BF16)
    hi = x.astype(BF16)
    rest = x - hi.astype(F32)
    mid = rest.astype(BF16)
    lo = (rest - mid.astype(F32)).astype(BF16)
    return _raw_nn(tri, hi) + _raw_nn(tri, mid) + _raw_nn(tri, lo)


@functools.partial(jax.custom_vjp, nondiff_argnums=(1,))
def _cumsum(x, reverse):
    return _cumsum_rows(x, reverse)


def _cumsum_fwd(x, reverse):
    return _cumsum_rows(x, reverse), None


def _cumsum_bwd(reverse, _, g):
    return (_cumsum_rows(g, not reverse),)


_cumsum.defvjp(_cumsum_fwd, _cumsum_bwd)


def _chunks_fn(qs, ks, vs, lfs, sts, reverses):
    n, c = len(qs), qs[0].shape[0]
    every = range(n)
    tris = [_tri_mask(c, r) for r in reverses]
    cums = [_cumsum(lfs[i], reverses[i]) for i in every]
    rid = lax.broadcasted_iota(jnp.int32, cums[0].shape, 0)

    def pick(cum, r):
        return jnp.sum(jnp.where(rid == r, cum, 0.0), axis=0, keepdims=True)

    refs = [pick(cums[i], (c - 1 - c // 2) if reverses[i] else c // 2) for i in every]
    lasts = [pick(cums[i], 0 if reverses[i] else c - 1) for i in every]
    q_in = [qs[i] * jnp.exp(cums[i] - refs[i]) for i in every]
    k_in = [ks[i] * jnp.exp(refs[i] - cums[i]) for i in every]
    scores = [jnp.where(tris[i], _dot_nt(q_in[i], k_in[i]), 0.0) for i in every]
    o_intra = [_dot_nn(scores[i], vs[i]) for i in every]
    q_out = [qs[i] * jnp.exp(cums[i]) for i in every]
    o_inter = [_dot_nt(q_out[i], sts[i]) for i in every]
    k_state = [ks[i] * jnp.exp(lasts[i] - cums[i]) for i in every]
    upd = [_dot_tn(vs[i], k_state[i]) for i in every]
    st_new = [sts[i] * jnp.exp(lasts[i]) + upd[i] for i in every]
    return [o_intra[i] + o_inter[i] for i in every], st_new


def _attn_fwd(name, q, k_f, k_b, v, lf_f, lf_b, n_heads, dk, dv):
    n_rows = q[0].shape[0]
    n_chunks = n_rows // CHUNK
    n_steps = n_chunks // ATTN_SUB
    wk, wv = n_heads * dk, n_heads * dv

    def spec(width, off, rev):
        return pl.BlockSpec((CHUNK * ATTN_SUB, width), lambda n: ((n_steps - 1 - n) if rev else n, off))

    def sspec(rev):
        return pl.BlockSpec((ATTN_SUB, n_heads, dv, dk), lambda n: ((n_steps - 1 - n) if rev else n, 0, 0, 0))

    def body(qf, kf, vf, lff, qb, kb, vb, lfb, of_ref, ob_ref, sf_ref, sb_ref, st):
        @pl.when(pl.program_id(0) == 0)
        def _():
            st[...] = jnp.zeros_like(st)

        ins = ((qf, kf, vf, lff), (qb, kb, vb, lfb))
        chains = [(d, h) for d in range(2) for h in range(n_heads)]
        ck = [slice(h * dk, (h + 1) * dk) for h in range(n_heads)]
        cv = [slice(h * dv, (h + 1) * dv) for h in range(n_heads)]
        sts = [st[d, h] for d, h in chains]
        done = []
        for sub in range(ATTN_SUB):
            local = (sub, ATTN_SUB - 1 - sub)
            rows = [slice(local[d] * CHUNK, (local[d] + 1) * CHUNK) for d in range(2)]
            qs = [ins[d][0][rows[d], ck[h]] for d, h in chains]
            ks = [ins[d][1][rows[d], ck[h]] for d, h in chains]
            vs = [ins[d][2][rows[d], cv[h]] for d, h in chains]
            lfs = [ins[d][3][rows[d], ck[h]] for d, h in chains]
            os_, st_new = _chunks_fn(qs, ks, vs, lfs, sts, [d == 1 for d, _ in chains])
            done.append((local, rows, sts, os_))
            sts = st_new
        for local, rows, entered, os_ in done:
            for i, (d, h) in enumerate(chains):
                (sf_ref, sb_ref)[d][local[d], h] = entered[i].astype(BF16)
                (of_ref, ob_ref)[d][rows[d], cv[h]] = os_[i]
        for i, (d, h) in enumerate(chains):
            st[d, h] = sts[i]

    in_specs = [spec(wk, q[1], False), spec(wk, k_f[1], False), spec(wv, v[1], False), spec(wk, lf_f[1], False),
                spec(wk, q[1], True), spec(wk, k_b[1], True), spec(wv, v[1], True), spec(wk, lf_b[1], True)]
    return pl.pallas_call(
        body, name=name, grid=(n_steps,), in_specs=in_specs,
        out_specs=[spec(wv, 0, False), spec(wv, 0, True), sspec(False), sspec(True)],
        out_shape=[jax.ShapeDtypeStruct((n_rows, wv), F32)] * 2
        + [jax.ShapeDtypeStruct((n_chunks, n_heads, dv, dk), BF16)] * 2,
        scratch_shapes=[pltpu.VMEM((2, n_heads, dv, dk), F32)],
        compiler_params=_params(("arbitrary",)),
    )(q[0], k_f[0], v[0], lf_f[0], q[0], k_b[0], v[0], lf_b[0])


def _attn_bwd(name, q, k_f, k_b, v, lf_f, lf_b, st_f, st_b, do, n_heads, dk, dv, out_dtype=F32):
    n_rows = q[0].shape[0]
    n_chunks = n_rows // CHUNK
    n_steps = n_chunks // ATTN_SUB
    wk, wv = n_heads * dk, n_heads * dv

    def spec(width, off, rev):
        return pl.BlockSpec((CHUNK * ATTN_SUB, width), lambda n: (n if rev else (n_steps - 1 - n), off))

    def sspec(rev):
        return pl.BlockSpec((ATTN_SUB, n_heads, dv, dk), lambda n: (n if rev else (n_steps - 1 - n), 0, 0, 0))

    def body(qf, kf, vf, lff, sf, dof, qb, kb, vb, lfb, sb, dob,
             dqf, dkf, dvf, dlff, dqb, dkb, dvb, dlfb, dst):
        @pl.when(pl.program_id(0) == 0)
        def _():
            dst[...] = jnp.zeros_like(dst)

        ins = ((qf, kf, vf, lff, sf, dof), (qb, kb, vb, lfb, sb, dob))
        outs = ((dqf, dkf, dvf, dlff), (dqb, dkb, dvb, dlfb))
        chains = [(d, h) for d in range(2) for h in range(n_heads)]
        ck = [slice(h * dk, (h + 1) * dk) for h in range(n_heads)]
        cv = [slice(h * dv, (h + 1) * dv) for h in range(n_heads)]
        fn = functools.partial(_chunks_fn, reverses=[d == 1 for d, _ in chains])
        dsts = [dst[d, h] for d, h in chains]
        done = []
        for sub in range(ATTN_SUB):
            local = (ATTN_SUB - 1 - sub, sub)
            rows = [slice(local[d] * CHUNK, (local[d] + 1) * CHUNK) for d in range(2)]
            qs = [ins[d][0][rows[d], ck[h]] for d, h in chains]
            ks = [ins[d][1][rows[d], ck[h]] for d, h in chains]
            vs = [ins[d][2][rows[d], cv[h]] for d, h in chains]
            lfs = [ins[d][3][rows[d], ck[h]] for d, h in chains]
            sts = [ins[d][4][local[d], h].astype(F32) for d, h in chains]
            dos = [ins[d][5][rows[d], cv[h]] for d, h in chains]
            _, vjp = jax.vjp(fn, qs, ks, vs, lfs, sts)
            dqs, dks, dvs, dlfs, dsts = vjp((dos, dsts))
            done.append((rows, dqs, dks, dvs, dlfs))
        for rows, dqs, dks, dvs, dlfs in done:
            for i, (d, h) in enumerate(chains):
                dq_r, dk_r, dv_r, dlf_r = outs[d]
                dq_r[rows[d], ck[h]] = dqs[i].astype(dq_r.dtype)
                dk_r[rows[d], ck[h]] = dks[i].astype(dk_r.dtype)
                dv_r[rows[d], cv[h]] = dvs[i].astype(dv_r.dtype)
                dlf_r[rows[d], ck[h]] = dlfs[i].astype(dlf_r.dtype)
        for i, (d, h) in enumerate(chains):
            dst[d, h] = dsts[i]

    def dir_specs(kk, lf, rev):
        return [spec(wk, q[1], rev), spec(wk, kk[1], rev), spec(wv, v[1], rev), spec(wk, lf[1], rev), sspec(rev),
                spec(wv, 0, rev)]

    def dir_out_specs(rev):
        return [spec(wk, 0, rev), spec(wk, 0, rev), spec(wv, 0, rev), spec(wk, 0, rev)]

    shapes = [jax.ShapeDtypeStruct((n_rows, wk), out_dtype), jax.ShapeDtypeStruct((n_rows, wk), out_dtype),
              jax.ShapeDtypeStruct((n_rows, wv), out_dtype), jax.ShapeDtypeStruct((n_rows, wk), F32)]
    outs = pl.pallas_call(
        body, name=name, grid=(n_steps,), in_specs=dir_specs(k_f, lf_f, False) + dir_specs(k_b, lf_b, True),
        out_specs=dir_out_specs(False) + dir_out_specs(True), out_shape=shapes + shapes,
        scratch_shapes=[pltpu.VMEM((2, n_heads, dv, dk), F32)],
        compiler_params=_params(("arbitrary",)),
    )(q[0], k_f[0], v[0], lf_f[0], st_f, do, q[0], k_b[0], v[0], lf_b[0], st_b, do)
    return outs[:4], outs[4:]


def _row2(v):
    return v.reshape(1, -1)


def _mlp_fwd(tag, h, gain, w1, w2):
    y = _rowcall(f"{tag}_norm", _rmsnorm_fn, [(h, h.shape[1], 0)], [gain], [(h.shape[1], BF16)], tm=512)[0]
    hid = _mm(f"{tag}_up", y, w1, out_dtypes=(BF16,))
    h_out = _mm(f"{tag}_down", hid, w2, a_pro=_relu2, extras=(h,), epi=_add_epi)
    return h_out, (y, hid)


def _mlp_bwd(tag, h, gain, w1, w2, saved, dh_out):
    y, hid = saved
    dhid = _mm(f"{tag}_dact", dh_out, w2, mode="nt", extras=(hid,), epi=_relu2_bwd_epi, out_dtypes=(BF16,))
    dw2 = _mm(f"{tag}_dw2", hid, dh_out, mode="tn", a_pro=_relu2)
    dw1 = _mm(f"{tag}_dw1", y, dhid, mode="tn", out_split=N_CHIPS)
    dy = _mm(f"{tag}_dy", dhid, w1, mode="nt")
    dh, dgain = _norm_bwd(f"{tag}_dnorm", h, gain, dy, dh_out)
    return dh, dgain, dw1, dw2


def _norm_bwd(name, h, gain, dy, dres, pin=None):
    d = h.shape[1]

    def fn(h, dy, dres, gain):
        _, vjp = jax.vjp(lambda a, b: _rmsnorm_fn(a, b)[0], h, gain)
        dh, dgain = vjp(dy)
        return dh + dres, dgain

    dh, dgain = _rowcall(name, fn, [(h, d, 0), (dy, d, 0), (dres, d, 0)], [gain], [(d, F32)], [(1, d)], tm=512, pin=pin)
    return dh, dgain


def _local_step(x, target, w, pin=None, late=None, emit=None):
    g = {}
    d_model = x.shape[1]
    rg_w = hg_w = d_model // 2
    pins = []

    def send_off(tag, arrays):
        if emit is not None:
            pins.append(emit(tag, arrays))

    def chip_major(t):
        return t.reshape(N_CHIPS, t.shape[0] // N_CHIPS, t.shape[1])

    h_a0 = x
    gain = _row2(w["norm_mix"][0])
    y0 = _rowcall("l0_norm", _rmsnorm_fn, [(h_a0, d_model, 0)], [gain], [(d_model, BF16)], tm=512, pin=pin)[0]
    proj0 = _mm("l0_in", y0, w["ab_w_in"])
    conv_w, conv_b = w["rg_conv_w"], _row2(w["rg_conv_b"])
    xc = _conv_fwd("rg_conv", proj0, 0, conv_w, conv_b)
    gate_pars = [w["rg_wa_bd"], w["rg_wx_bd"], w["rg_b_a"], w["rg_b_x"], w["rg_lambda"]]
    a_f, u_f, a_b, u_b = _rowcall("rg_gates", _rg_gates_fn, [(xc, rg_w, 0)], gate_pars, [(rg_w, F32)] * 4)
    hs_f = _scan_fwd("rg_scan_f", a_f, u_f, False)
    hs_b = _scan_fwd("rg_scan_b", a_b, u_b, True)
    hg_rows = [(proj0, hg_w, 2), (proj0, hg_w, 3), (proj0, hg_w, 4)]
    qh, k_f, lf_f, k_b, lf_b = _rowcall("hg_pre", _hg_pre_fn, hg_rows, [w["hg_lb_logits"]], [(hg_w, F32)] * 5)
    iv = (proj0, 5)
    o_f, o_b, st_f, st_b = _attn_fwd("hg_attn", (qh, 0), (k_f, 0), (k_b, 0), iv, (lf_f, 0), (lf_b, 0), 4, 128, 128)
    post0_rows = [(hs_f, rg_w, 0), (hs_b, rg_w, 0), (proj0, rg_w, 1), (o_f, hg_w, 0), (o_b, hg_w, 0), (proj0, hg_w, 6)]
    hg_gain = _row2(w["hg_norm"])
    mix_in0 = _rowcall("l0_post", _post0_fwd_fn, post0_rows, [hg_gain], [(d_model, BF16)])[0]
    if late is not None:
        w = {**w, **late(mix_in0)}
    h_b0 = _mm("l0_out", mix_in0, w["ab_w_out"], extras=(h_a0,), epi=_add_epi)
    h_c0, mlp0 = _mlp_fwd("mlp0", h_b0, _row2(w["norm_mlp"][0]), w["mlp_w1"][0], w["mlp_w2"][0])

    h_a1 = h_c0
    gain1 = _row2(w["norm_mix"][1])
    y1 = _rowcall("l1_norm", _rmsnorm_fn, [(h_a1, d_model, 0)], [gain1], [(d_model, BF16)], tm=512)[0]
    proj1 = _mm("l1_in", y1, w["gla_w_in_pad"], tn=640)
    gla_pars = [w["gla_w_up_pad"], w["gla_b_gate"]]
    gq, glf_f, glf_b = _rowcall("gla_pre", _gla_pre_fn, [(proj1, 512, 0), (proj1, LANES, 24)], gla_pars, [(512, F32)] * 3)
    gk, gv = (proj1, 1), (proj1, 1)
    go_f, go_b, gst_f, gst_b = _attn_fwd("gla_attn", (gq, 0), gk, gk, gv, (glf_f, 0), (glf_b, 0), 4, 128, 256)
    gla_gain = _row2(w["gla_norm"])
    post1_rows = [(go_f, d_model, 0), (go_b, d_model, 0), (proj1, d_model, 2)]
    mix_in1 = _rowcall("l1_post", _gla_post_fwd_fn, post1_rows, [gla_gain], [(d_model, BF16)])[0]
    h_b1 = _mm("l1_out", mix_in1, w["gla_w_out"], extras=(h_a1,), epi=_add_epi)
    h_c1, mlp1 = _mlp_fwd("mlp1", h_b1, _row2(w["norm_mlp"][1]), w["mlp_w1"][1], w["mlp_w2"][1])

    dh, loss, g["norm_final"] = _rowcall(
        "loss_head", _loss_head_fn, [(h_c1, d_model, 0), (target, d_model, 0)], [_row2(w["norm_final"])],
        [(d_model, F32)], [(1, LANES), (1, d_model)], tm=512)

    dh, g_nmlp1, g_w1_1, g_w2_1 = _mlp_bwd("mlp1", h_b1, _row2(w["norm_mlp"][1]), w["mlp_w1"][1], w["mlp_w2"][1], mlp1, dh)
    send_off("mlp1", [g_w1_1, chip_major(g_w2_1)])
    dmix1 = _mm("l1_dout", dh, w["gla_w_out"], mode="nt")
    g["gla_w_out"] = _mm("l1_dwout", mix_in1, dh, mode="tn")
    dgo, dr, g["gla_norm"] = _rowcall(
        "l1_dpost", _gla_post_bwd_fn, post1_rows + [(dmix1, d_model, 0)], [gla_gain],
        [(d_model, F32), (d_model, BF16)], [(1, d_model)], pin=pins.pop() if pins else None)
    (dq_f, dk_f, dv_f, dlf_f), (dq_b, dk_b, dv_b, dlf_b) = _attn_bwd(
        "gla_dattn", (gq, 0), gk, gk, gv, (glf_f, 0), (glf_b, 0), gst_f, gst_b, dgo, 4, 128, 256)

    def gla_pre_bwd(q, lr, dq1, dq2, dlf1, dlf2, dk1, dk2, dv1, dv2, w_up, b_gate):
        dlr = jnp.zeros_like(lr)
        dws, dbs = [], []
        for d, dlf in enumerate((dlf1, dlf2)):
            z = _raw_nn(lr, w_up[d]) + b_gate[d:d + 1]
            dz = dlf * _sigmoid(-z) * (1.0 / 16.0)
            dlr = dlr + _raw_nt(dz, w_up[d])
            dws.append(_raw_tn(dz, lr))
            dbs.append(jnp.sum(dz, axis=0, keepdims=True))
        return ((dq1 + dq2) * (128.0 ** -0.5), dk1 + dk2, dv1 + dv2, dlr, dws[0], dws[1], dbs[0], dbs[1])

    rows = [(proj1, 512, 0), (proj1, LANES, 24), (dq_f, 512, 0), (dq_b, 512, 0), (dlf_f, 512, 0), (dlf_b, 512, 0),
            (dk_f, 512, 0), (dk_b, 512, 0), (dv_f, d_model, 0), (dv_b, d_model, 0)]
    dq, dk, dv, dlr, dwt_f, dwt_b, db_f, db_b = _rowcall(
        "gla_dpre", gla_pre_bwd, rows, gla_pars, [(512, BF16), (512, BF16), (d_model, BF16), (LANES, BF16)],
        [(512, LANES), (512, LANES), (1, 512), (1, 512)])
    g["gla_w_up_pad"] = jnp.stack([dwt_f.T, dwt_b.T])
    g["gla_b_gate"] = jnp.concatenate([db_f, db_b], axis=0)
    dproj1 = jnp.concatenate([dq, dk, dv, dr, dlr], axis=1)
    g_gla_in = _mm("l1_dwin", y1, dproj1, mode="tn", tn=640)
    g["gla_w_in"] = _split_chips(g_gla_in[:, :GLA_IN_WIDTH], 1)
    send_off("gla", [g["gla_w_in"], chip_major(g["gla_w_out"])])
    dy1 = _mm("l1_dy", dproj1, w["gla_w_in_pad"], mode="nt", tk=640)
    dh, g_nmix1 = _norm_bwd("l1_dnorm", h_a1, gain1, dy1, dh, pin=pins.pop() if pins else None)

    dh, g_nmlp0, g_w1_0, g_w2_0 = _mlp_bwd("mlp0", h_b0, _row2(w["norm_mlp"][0]), w["mlp_w1"][0], w["mlp_w2"][0], mlp0, dh)
    g["ab_w_out"] = _mm("l0_dwout", mix_in0, dh, mode="tn")
    send_off("mlp0", [g_w1_0, chip_major(g_w2_0), chip_major(g["ab_w_out"])])
    dmix0 = _mm("l0_dout", dh, w["ab_w_out"], mode="nt")
    dhs, dga, do, dg, g["hg_norm"] = _rowcall(
        "l0_dpost", _post0_bwd_fn, post0_rows + [(dmix0, d_model, 0)], [hg_gain],
        [(rg_w, F32), (rg_w, BF16), (hg_w, F32), (hg_w, BF16)], [(1, hg_w)], pin=pins.pop() if pins else None)
    (dqh_f, dk_f, div_f, dlf_f), (dqh_b, dk_b, div_b, dlf_b) = _attn_bwd(
        "hg_dattn", (qh, 0), (k_f, 0), (k_b, 0), iv, (lf_f, 0), (lf_b, 0), st_f, st_b, do, 4, 128, 128)

    def hg_pre_bwd(q, f_f, f_b, dq1, dq2, dk1, dlf1, dk2, dlf2, dv1, dv2, logits):
        _, vjp = jax.vjp(_hg_pre_fn, q, f_f, f_b, logits)
        dq, df_f, df_b, dlogits = vjp((dq1 + dq2, dk1, dlf1, dk2, dlf2))
        return dq, df_f, df_b, dv1 + dv2, dlogits

    rows = hg_rows + [(t, hg_w, 0) for t in (dqh_f, dqh_b, dk_f, dlf_f, dk_b, dlf_b, div_f, div_b)]
    dq, df_f, df_b, div, g["hg_lb_logits"] = _rowcall(
        "hg_dpre", hg_pre_bwd, rows, [w["hg_lb_logits"]], [(hg_w, BF16)] * 4, [(2, hg_w)])
    du_f, da_f = _scan_bwd("rg_dscan_f", a_f, hs_f, dhs, False)
    du_b, da_b = _scan_bwd("rg_dscan_b", a_b, hs_b, dhs, True)
    gates_bwd = _vjp_of(_rg_gates_fn, 1, 4, 5)
    rows = [(xc, rg_w, 0), (da_f, rg_w, 0), (du_f, rg_w, 0), (da_b, rg_w, 0), (du_b, rg_w, 0)]
    dxc, g["rg_wa_bd"], g["rg_wx_bd"], g["rg_b_a"], g["rg_b_x"], g["rg_lambda"] = _rowcall(
        "rg_dgates", gates_bwd, rows, gate_pars, [(rg_w, F32)],
        [(2, rg_w, rg_w), (2, rg_w, rg_w), (2, rg_w), (2, rg_w), (2, rg_w)])
    dxa, g["rg_conv_w"], g["rg_conv_b"] = _conv_bwd("rg_dconv", proj0, 0, conv_w, dxc)
    dproj0 = jnp.concatenate([dxa, dga, dq, df_f, df_b, div, dg], axis=1)
    g["ab_w_in"] = _mm("l0_dwin", y0, dproj0, mode="tn", out_split=N_CHIPS)
    dy0 = _mm("l0_dy", dproj0, w["ab_w_in"], mode="nt")
    grad_x, g_nmix0 = _norm_bwd("l0_dnorm", h_a0, gain, dy0, dh)

    g["norm_mix"] = jnp.concatenate([g_nmix0, g_nmix1], axis=0)
    g["norm_mlp"] = jnp.concatenate([g_nmlp0, g_nmlp1], axis=0)
    g["mlp_w1"] = [g_w1_0, g_w1_1]
    g["mlp_w2"] = [g_w2_0, g_w2_1]
    return loss, grad_x, g


def _block_diag(w):
    d, g, n, _ = w.shape
    eye = jnp.eye(g, dtype=w.dtype)
    return (w[:, :, :, None, :] * eye[None, :, None, :, None]).reshape(d, g * n, g * n)


def _block_diag_extract(wbd, g):
    d, gn, _ = wbd.shape
    n = gn // g
    blocks = wbd.reshape(d, g, n, g, n)
    return jnp.stack([blocks[:, i, :, i, :] for i in range(g)], axis=1)


def _prepare_weights(big, full):
    w = {k: full[k] for k in ("norm_mix", "norm_mlp", "norm_final", "hg_lb_logits")}
    for k in ("rg_conv_w", "rg_conv_b", "rg_b_a", "rg_b_x", "rg_lambda", "hg_norm", "gla_b_gate", "gla_norm"):
        w[k] = full[k][0]
    w["rg_wa_bd"] = _block_diag(full["rg_w_a"][0])
    w["rg_wx_bd"] = _block_diag(full["rg_w_x"][0])
    up = full["gla_w_gate_up"][0]
    rank = up.shape[1]
    pad = jnp.zeros((2, LANES, up.shape[2]), F32)
    w["gla_w_up_pad"] = pad.at[0, 0:rank].set(up[0]).at[1, rank:2 * rank].set(up[1])
    w.update(_prepare_matrices(big))
    return w


def _prepare_matrices(big):
    w = {}
    if "mlp_w1" in big:
        w["mlp_w1"] = list(big["mlp_w1"])
        w["mlp_w2"] = [t.reshape(-1, t.shape[-1]) for t in big["mlp_w2"]]
    if "ab_w_in" in big:
        w["ab_w_in"] = big["ab_w_in"]
    if "ab_w_out" in big:
        w["ab_w_out"] = big["ab_w_out"].reshape(-1, big["ab_w_out"].shape[-1])
    if "gla_w_in" in big:
        w["gla_w_out"] = big["gla_w_out"].reshape(-1, big["gla_w_out"].shape[-1])
        gla_in = _join_chips(big["gla_w_in"], 1)
        w["gla_w_in_pad"] = jnp.pad(gla_in, ((0, 0), (0, GLA_IN_PAD - gla_in.shape[1])))
    return w


def _finish_grads(g, rank=16, rg_blocks=8):
    def chip_major(t):
        return t.reshape(N_CHIPS, t.shape[0] // N_CHIPS, t.shape[1])

    big = {
        "mlp_w1": list(g["mlp_w1"]), "mlp_w2": [chip_major(t) for t in g["mlp_w2"]],
        "ab_w_in": g["ab_w_in"], "ab_w_out": chip_major(g["ab_w_out"]),
        "gla_w_in": g["gla_w_in"], "gla_w_out": chip_major(g["gla_w_out"]),
    }
    small = {
        "norm_mix": g["norm_mix"], "norm_mlp": g["norm_mlp"], "norm_final": g["norm_final"][0],
        "rg_conv_w": g["rg_conv_w"][None], "rg_conv_b": g["rg_conv_b"],
        "rg_w_a": _block_diag_extract(g["rg_wa_bd"], rg_blocks)[None], "rg_b_a": g["rg_b_a"][None],
        "rg_w_x": _block_diag_extract(g["rg_wx_bd"], rg_blocks)[None], "rg_b_x": g["rg_b_x"][None],
        "rg_lambda": g["rg_lambda"][None], "hg_lb_logits": g["hg_lb_logits"], "hg_norm": g["hg_norm"],
        "gla_w_gate_up": jnp.stack([g["gla_w_up_pad"][0, 0:rank], g["gla_w_up_pad"][1, rank:2 * rank]])[None],
        "gla_b_gate": g["gla_b_gate"][None], "gla_norm": g["gla_norm"],
    }
    return big, small


MATRICES = (("mlp_w1", 0), ("mlp_w1", 1), ("mlp_w2", 0), ("mlp_w2", 1), ("ab_w_in", 0), ("ab_w_out", 0),
            ("gla_w_in", 0), ("gla_w_out", 0))
EARLY_MATRICES = ("ab_w_in",)
SMALL_SHARDED = ("rg_conv_w", "rg_b_a", "rg_b_x", "rg_lambda", "gla_w_gate_up", "gla_b_gate", "gla_norm")
SMALL_REPLICATED = ("norm_mix", "norm_mlp", "norm_final", "rg_conv_b", "rg_w_a", "rg_w_x", "hg_lb_logits", "hg_norm")
WEIGHTS = ("norm_mix", "norm_mlp", "norm_final", "mlp_w1", "mlp_w2", "ab_w_in", "ab_w_out", "rg_conv_w", "rg_conv_b",
           "rg_w_a", "rg_b_a", "rg_w_x", "rg_b_x", "rg_lambda", "hg_lb_logits", "hg_norm", "gla_w_in", "gla_w_out",
           "gla_w_gate_up", "gla_b_gate", "gla_norm")
ROW_ALIGN = 16


def _pack(arrays, lead=0):
    head = arrays[0].shape[:lead]
    flat = jnp.concatenate([a.reshape(head + (-1,)) for a in arrays], axis=lead)
    n = flat.shape[-1]
    quantum = LANES * ROW_ALIGN
    padded = -(-n // quantum) * quantum
    if padded != n:
        flat = jnp.pad(flat, [(0, 0)] * lead + [(0, padded - n)])
    return flat.reshape(head + (padded // LANES, LANES))


def _unpack(buf, shapes, lead=0):
    head = buf.shape[:lead]
    flat = buf.reshape(head + (-1,))
    out, off = [], 0
    for s in shapes:
        n = 1
        for v in s:
            n *= v
        out.append(lax.slice_in_dim(flat, off, off + n, axis=lead).reshape(head + tuple(s)))
        off += n
    return out


def _join_chips(gathered, axis):
    t = jnp.moveaxis(gathered, 0, axis)
    return t.reshape(t.shape[:axis] + (t.shape[axis] * t.shape[axis + 1],) + t.shape[axis + 2:])


def _split_chips(full, axis):
    s = full.shape
    t = full.reshape(s[:axis] + (N_CHIPS, s[axis] // N_CHIPS) + s[axis + 1:])
    return jnp.moveaxis(t, axis, 0)


_ANY = pl.BlockSpec(memory_space=pl.ANY)


def _place():
    return lax.axis_index("x"), lax.axis_index("y"), lax.axis_index("c")


def _into_slot(name, src, slot, n_slots, dtype, tm, layer=None):
    r, lanes = src.shape[-2:]
    tm = _row_tile(r, tm, ROW_ALIGN)

    def body(slot_ref, in_ref, o_ref):
        o_ref[...] = in_ref[...].astype(o_ref.dtype)

    if layer is None:
        in_spec = pl.BlockSpec((tm, lanes), lambda i, slot_ref: (i, 0))
    else:
        in_spec = pl.BlockSpec((None, tm, lanes), lambda i, slot_ref: (layer, i, 0))
    grid_spec = pltpu.PrefetchScalarGridSpec(
        num_scalar_prefetch=1, grid=(r // tm,), in_specs=[in_spec],
        out_specs=pl.BlockSpec((None, tm, lanes), lambda i, slot_ref: (slot_ref[0], i, 0)))
    return pl.pallas_call(
        body, name=name, grid_spec=grid_spec, out_shape=jax.ShapeDtypeStruct((n_slots, r, lanes), dtype),
        compiler_params=_params(("parallel",)),
    )(slot.reshape(1).astype(jnp.int32), src)


def _chip_peers():
    x, y, c = _place()
    return 2 * x + y, c, [(1 - x, y), (x, 1 - y), (1 - x, 1 - y)]


def _comm_call(name, body, ins, out_shapes, n_sems, aliases=None):
    return pl.pallas_call(
        body, name=name, in_specs=[_ANY] * len(ins), out_specs=[_ANY] * len(out_shapes), out_shape=out_shapes,
        input_output_aliases=aliases or {},
        scratch_shapes=[pltpu.SemaphoreType.DMA((n_sems,)), pltpu.SemaphoreType.DMA((n_sems,))],
    )(*ins)


def _gather_chips(name, bufs):
    n = len(bufs)

    def body(*refs):
        outs, send_sems, recv_sems = refs[n:2 * n], refs[2 * n], refs[2 * n + 1]
        x, y, c = _place()
        me, _, peers = _chip_peers()

        def rows(a, block, half):
            rh = outs[a].shape[1] // 2
            return outs[a].at[block, pl.ds(half * rh, rh)]

        def copy(a, j, block, half, to, sem):
            return pltpu.make_async_remote_copy(
                src_ref=rows(a, block, half), dst_ref=rows(a, block, half), send_sem=send_sems.at[sem],
                recv_sem=recv_sems.at[sem], device_id=to, device_id_type=MESH)

        def over_ici(a, j, block):
            px, py = peers[j]
            return copy(a, j, block, c, (px, py, c), 6 * a + j)

        def to_sibling(a, j, block, half):
            return copy(a, j, block, half, (x, y, 1 - c), 6 * a + 3 + j)

        sends = [over_ici(a, j, me) for a in range(n) for j in range(3)]
        for cp in sends:
            cp.start()
        for a in range(n):
            for j, (px, py) in enumerate(peers):
                over_ici(a, j, 2 * px + py).wait_recv()
                handed = to_sibling(a, j, 2 * px + py, c)
                handed.start()
                sends.append(handed)
        for a in range(n):
            for j, (px, py) in enumerate(peers):
                to_sibling(a, j, 2 * px + py, 1 - c).wait_recv()
        for cp in sends:
            cp.wait_send()

    shapes = [jax.ShapeDtypeStruct(b.shape, b.dtype) for b in bufs]
    return _comm_call(name, body, bufs, shapes, 6 * n, {a: a for a in range(n)})


_HBM = pl.BlockSpec(memory_space=pltpu.HBM)
_SEM = pl.BlockSpec(memory_space=pltpu.SEMAPHORE)
_EFFECT = pltpu.SideEffectType.DATAFLOW_SIDE_EFFECTING


def _half_rows(ref, block, half):
    rh = ref.shape[1] // 2
    return ref.at[block, pl.ds(half * rh, rh)]


def _gather_start(name, bufs, after):
    n = len(bufs)

    def body(*refs):
        ins, send_sems, recv_sems, token = refs[:n], refs[n + 1], refs[n + 2], refs[-1]
        me, c, peers = _chip_peers()
        for a in range(n):
            mine = _half_rows(ins[a], me, c)
            for j, (px, py) in enumerate(peers):
                pltpu.make_async_remote_copy(
                    src_ref=mine, dst_ref=mine, send_sem=send_sems.at[3 * a + j], recv_sem=recv_sems.at[3 * a + j],
                    device_id=(px, py, c), device_id_type=MESH).start()
        token[...] = jnp.zeros_like(token)

    out_shape = (pltpu.SemaphoreType.DMA((3 * n,)), pltpu.SemaphoreType.DMA((3 * n,)),
                 *[pltpu.HBM(b.shape, b.dtype) for b in bufs], jax.ShapeDtypeStruct((8, LANES), F32))
    return pl.pallas_call(
        body, name=name, out_shape=out_shape, in_specs=[_HBM] * n + [_ANY],
        out_specs=(_SEM, _SEM, *[_HBM] * n, pl.BlockSpec(memory_space=pltpu.VMEM)),
        input_output_aliases={a: 2 + a for a in range(n)},
        compiler_params=pltpu.CompilerParams(has_side_effects=_EFFECT),
    )(*[pltpu.with_memory_space_constraint(b, pltpu.HBM) for b in bufs], after)


def _gather_wait(name, bufs, send_sems, recv_sems, after):
    n = len(bufs)

    def body(*refs):
        ins, send_sems, recv_sems = refs[:n], refs[n], refs[n + 1]
        me, c, peers = _chip_peers()
        for a in range(n):
            for j, (px, py) in enumerate(peers):
                copy = pltpu.make_async_remote_copy(
                    src_ref=_half_rows(ins[a], me, c), dst_ref=_half_rows(ins[a], 2 * px + py, c),
                    send_sem=send_sems.at[3 * a + j], recv_sem=recv_sems.at[3 * a + j],
                    device_id=(px, py, c), device_id_type=MESH)
                copy.wait_send()
                copy.wait_recv()

    return pl.pallas_call(
        body, name=name, out_shape=tuple(pltpu.HBM(b.shape, b.dtype) for b in bufs),
        in_specs=[_HBM] * n + [_SEM, _SEM, _ANY], out_specs=tuple([_HBM] * n),
        input_output_aliases={a: a for a in range(n)},
        compiler_params=pltpu.CompilerParams(has_side_effects=_EFFECT),
    )(*bufs, send_sems, recv_sems, after)


def _hand_over(name, bufs):
    n = len(bufs)

    def body(*refs):
        outs, send_sems, recv_sems = refs[n:2 * n], refs[2 * n], refs[2 * n + 1]
        x, y, c = _place()
        _, _, peers = _chip_peers()

        def copy(a, j, half):
            px, py = peers[j]
            rows = _half_rows(outs[a], 2 * px + py, half)
            return pltpu.make_async_remote_copy(
                src_ref=rows, dst_ref=rows, send_sem=send_sems.at[3 * a + j], recv_sem=recv_sems.at[3 * a + j],
                device_id=(x, y, 1 - c), device_id_type=MESH)

        sends = [copy(a, j, c) for a in range(n) for j in range(3)]
        for cp in sends:
            cp.start()
        for a in range(n):
            for j in range(3):
                copy(a, j, 1 - c).wait_recv()
        for cp in sends:
            cp.wait_send()

    shapes = [jax.ShapeDtypeStruct(b.shape, b.dtype) for b in bufs]
    return _comm_call(name, body, bufs, shapes, 3 * n, {a: a for a in range(n)})


def _pair_exchange(name, gs, pin=None):
    n = len(gs)
    extra = [] if pin is None else [pin]
    k = n + len(extra)

    def body(*refs):
        ins, outs, send_sems, recv_sems = refs[:n], refs[k:k + n], refs[k + n], refs[k + n + 1]
        x, y, c = _place()
        copies = [pltpu.make_async_remote_copy(
            src_ref=ins[a].at[:, 1 - c], dst_ref=outs[a], send_sem=send_sems.at[a], recv_sem=recv_sems.at[a],
            device_id=(x, y, 1 - c), device_id_type=MESH) for a in range(n)]
        for cp in copies:
            cp.start()
        for cp in copies:
            cp.wait()

    shapes = [jax.ShapeDtypeStruct((g.shape[0],) + g.shape[2:], g.dtype) for g in gs]
    return _comm_call(name, body, list(gs) + extra, shapes, n)


def _pair_add(name, g, got, chip, core):
    n, _, rh, lanes = g.shape
    tm = _row_tile(rh, 1024, ROW_ALIGN)

    def body(idx_ref, g_ref, got_ref, own_ref, o16_ref):
        s = g_ref[...] + got_ref[...]
        o16_ref[...] = s.astype(BF16)

        @pl.when(pl.program_id(1) == idx_ref[1])
        def _():
            own_ref[...] = s

    grid_spec = pltpu.PrefetchScalarGridSpec(
        num_scalar_prefetch=1, grid=(rh // tm, n),
        in_specs=[pl.BlockSpec((None, None, tm, lanes), lambda i, s, idx_ref: (s, idx_ref[0], i, 0)),
                  pl.BlockSpec((None, tm, lanes), lambda i, s, idx_ref: (s, i, 0))],
        out_specs=[pl.BlockSpec((tm, lanes), lambda i, s, idx_ref: (i, 0)),
                   pl.BlockSpec((None, tm, lanes), lambda i, s, idx_ref: (s, i, 0))])
    return pl.pallas_call(
        body, name=name, grid_spec=grid_spec,
        out_shape=[jax.ShapeDtypeStruct((rh, lanes), F32), jax.ShapeDtypeStruct((n, rh, lanes), BF16)],
        compiler_params=_params(("parallel", "arbitrary")),
    )(jnp.stack([core, chip]).astype(jnp.int32), g, got)


def _chip_scatter(name, ps):
    n = len(ps)

    def body(*refs):
        ins, outs, send_sems, recv_sems = refs[:n], refs[n:2 * n], refs[2 * n], refs[2 * n + 1]
        me, c, peers = _chip_peers()

        def copy(a, j, src_block, dst_block):
            px, py = peers[j]
            return pltpu.make_async_remote_copy(
                src_ref=ins[a].at[src_block], dst_ref=outs[a].at[dst_block], send_sem=send_sems.at[3 * a + j],
                recv_sem=recv_sems.at[3 * a + j], device_id=(px, py, c), device_id_type=MESH)

        sends = [copy(a, j, 2 * px + py, me) for a in range(n) for j, (px, py) in enumerate(peers)]
        for cp in sends:
            cp.start()
        for a in range(n):
            for j, (px, py) in enumerate(peers):
                copy(a, j, me, 2 * px + py).wait_recv()
        for cp in sends:
            cp.wait_send()

    shapes = [jax.ShapeDtypeStruct(p.shape, p.dtype) for p in ps]
    return _comm_call(name, body, ps, shapes, 3 * n)


def _scatter_start(name, ps, lands):
    n = len(ps)

    def body(*refs):
        srcs, dsts, send_sems, recv_sems, token = refs[:n], refs[n:2 * n], refs[2 * n], refs[2 * n + 1], refs[-1]
        me, c, peers = _chip_peers()
        for a in range(n):
            for j, (px, py) in enumerate(peers):
                pltpu.make_async_remote_copy(
                    src_ref=srcs[a].at[2 * px + py], dst_ref=dsts[a].at[me], send_sem=send_sems.at[3 * a + j],
                    recv_sem=recv_sems.at[3 * a + j], device_id=(px, py, c), device_id_type=MESH).start()
        token[...] = jnp.zeros_like(token)

    bufs = list(ps) + list(lands)
    out_shape = (pltpu.SemaphoreType.DMA((3 * n,)), pltpu.SemaphoreType.DMA((3 * n,)),
                 *[pltpu.HBM(b.shape, b.dtype) for b in bufs], jax.ShapeDtypeStruct((8, LANES), F32))
    return pl.pallas_call(
        body, name=name, out_shape=out_shape, in_specs=[_HBM] * (2 * n),
        out_specs=(_SEM, _SEM, *[_HBM] * (2 * n), pl.BlockSpec(memory_space=pltpu.VMEM)),
        input_output_aliases={a: 2 + a for a in range(2 * n)},
        compiler_params=pltpu.CompilerParams(has_side_effects=_EFFECT),
    )(*[pltpu.with_memory_space_constraint(b, pltpu.HBM) for b in bufs])


def _scatter_wait(name, ps, lands, send_sems, recv_sems, after):
    n = len(ps)

    def body(*refs):
        srcs, dsts, send_sems, recv_sems = refs[:n], refs[n:2 * n], refs[2 * n], refs[2 * n + 1]
        me, c, peers = _chip_peers()
        for a in range(n):
            for j, (px, py) in enumerate(peers):
                copy = pltpu.make_async_remote_copy(
                    src_ref=srcs[a].at[2 * px + py], dst_ref=dsts[a].at[2 * px + py],
                    send_sem=send_sems.at[3 * a + j], recv_sem=recv_sems.at[3 * a + j],
                    device_id=(px, py, c), device_id_type=MESH)
                copy.wait_send()
                copy.wait_recv()

    bufs = list(ps) + list(lands)
    outs = pl.pallas_call(
        body, name=name, out_shape=tuple(pltpu.HBM(b.shape, b.dtype) for b in bufs),
        in_specs=[_HBM] * (2 * n) + [_SEM, _SEM, _ANY], out_specs=tuple([_HBM] * (2 * n)),
        input_output_aliases={a: a for a in range(2 * n)},
        compiler_params=pltpu.CompilerParams(has_side_effects=_EFFECT),
    )(*bufs, send_sems, recv_sems, after)
    return list(outs[n:])


def _sum_ring(name, own, got, chip, core):
    n, rh, lanes = got.shape
    tm = _row_tile(rh, 2048, ROW_ALIGN)

    def body(idx_ref, own_ref, g1_ref, g2_ref, g3_ref, o_ref):
        o_ref[...] = ((own_ref[...] + g1_ref[...].astype(F32)) + g2_ref[...].astype(F32)) + g3_ref[...].astype(F32)

    def block(k):
        return pl.BlockSpec((None, tm, lanes), lambda i, idx_ref: ((idx_ref[0] + k) % n, i, 0))

    grid_spec = pltpu.PrefetchScalarGridSpec(
        num_scalar_prefetch=1, grid=(rh // tm,),
        in_specs=[pl.BlockSpec((tm, lanes), lambda i, idx_ref: (i, 0)), block(1), block(2), block(3)],
        out_specs=pl.BlockSpec((None, tm, lanes), lambda i, idx_ref: (idx_ref[1], i, 0)))
    return pl.pallas_call(
        body, name=name, grid_spec=grid_spec, out_shape=jax.ShapeDtypeStruct((2, rh, lanes), F32),
        compiler_params=_params(("parallel",)),
    )(jnp.stack([chip, core]).astype(jnp.int32), own, got, got, got)


def _pair_gather(name, bufs):
    n = len(bufs)

    def body(*refs):
        ins, outs, send_sems, recv_sems = refs[:n], refs[n:2 * n], refs[2 * n], refs[2 * n + 1]
        x, y, c = _place()

        def copy(a, block):
            return pltpu.make_async_remote_copy(
                src_ref=ins[a].at[block], dst_ref=outs[a].at[block], send_sem=send_sems.at[a],
                recv_sem=recv_sems.at[a], device_id=(x, y, 1 - c), device_id_type=MESH)

        sends = [copy(a, c) for a in range(n)]
        for cp in sends:
            cp.start()
        for a in range(n):
            copy(a, 1 - c).wait_recv()
        for cp in sends:
            cp.wait_send()

    shapes = [jax.ShapeDtypeStruct(b.shape, b.dtype) for b in bufs]
    return _comm_call(name, body, bufs, shapes, n, {a: a for a in range(n)})


def _all_peers():
    x, y, c = _place()
    peers = []
    for mask in range(1, N_DEV):
        fx, fy, fc = (mask >> 2) & 1, (mask >> 1) & 1, mask & 1
        peers.append((jnp.where(fx, 1 - x, x), jnp.where(fy, 1 - y, y), jnp.where(fc, 1 - c, c)))
    return 4 * x + 2 * y + c, peers


def _gather_all_start(name, buf):
    def body(in_ref, send_sems, recv_sems, out_ref, token):
        me, peers = _all_peers()
        for j, peer in enumerate(peers):
            pltpu.make_async_remote_copy(
                src_ref=in_ref.at[me], dst_ref=in_ref.at[me], send_sem=send_sems.at[j], recv_sem=recv_sems.at[j],
                device_id=peer, device_id_type=MESH).start()
        token[...] = jnp.zeros_like(token)

    n = N_DEV - 1
    return pl.pallas_call(
        body, name=name, in_specs=[_HBM],
        out_shape=(pltpu.SemaphoreType.DMA((n,)), pltpu.SemaphoreType.DMA((n,)), pltpu.HBM(buf.shape, buf.dtype),
                   jax.ShapeDtypeStruct((8, LANES), F32)),
        out_specs=(_SEM, _SEM, _HBM, pl.BlockSpec(memory_space=pltpu.VMEM)), input_output_aliases={0: 2},
        compiler_params=pltpu.CompilerParams(has_side_effects=_EFFECT),
    )(pltpu.with_memory_space_constraint(buf, pltpu.HBM))


def _gather_all_wait(name, buf, send_sems, recv_sems, after):
    def body(in_ref, send_sems, recv_sems, after_ref, out_ref):
        me, peers = _all_peers()
        for j, (px, py, pc) in enumerate(peers):
            copy = pltpu.make_async_remote_copy(
                src_ref=in_ref.at[me], dst_ref=in_ref.at[4 * px + 2 * py + pc], send_sem=send_sems.at[j],
                recv_sem=recv_sems.at[j], device_id=(px, py, pc), device_id_type=MESH)
            copy.wait_send()
            copy.wait_recv()

    return pl.pallas_call(
        body, name=name, in_specs=[_HBM, _SEM, _SEM, _ANY], out_shape=pltpu.HBM(buf.shape, buf.dtype),
        out_specs=_HBM, input_output_aliases={0: 0},
        compiler_params=pltpu.CompilerParams(has_side_effects=_EFFECT),
    )(buf, send_sems, recv_sems, after)


def _sum_blocks(name, stacked, tm):
    n, r, lanes = stacked.shape

    def body(in_ref, o_ref):
        acc = in_ref[0]
        for j in range(1, n):
            acc = acc + in_ref[j]
        o_ref[...] = acc

    return pl.pallas_call(
        body, name=name, grid=(r // tm,), in_specs=[pl.BlockSpec((n, tm, lanes), lambda i: (0, i, 0))],
        out_specs=pl.BlockSpec((tm, lanes), lambda i: (i, 0)), out_shape=jax.ShapeDtypeStruct((r, lanes), F32),
        compiler_params=_params(("parallel",)),
    )(stacked)


def _row_tile(rows, pref, align):
    best = None
    for t in range(align, min(rows, pref) + 1, align):
        if rows % t == 0:
            best = t
    assert best is not None, (rows, pref, align)
    return best


def _adam(name, w, g, m, v):
    rows, width = w.shape
    tm = _row_tile(rows, max(8, 4096 * LANES // width), 8)
    args = [(t, width, 0) for t in (w, g, m, v)]
    return _rowcall(name, _adam_fn, args, [], [(width, F32)] * 3, tm=tm)


def kernel(x, norm_mix, norm_mlp, norm_final, mlp_w1, mlp_w2, ab_w_in, ab_w_out, rg_conv_w, rg_conv_b, rg_w_a, rg_b_a, rg_w_x, rg_b_x, rg_lambda, hg_lb_logits, hg_norm, gla_w_in, gla_w_out, gla_w_gate_up, gla_b_gate, gla_norm, loss_target, m_norm_mix, m_norm_mlp, m_norm_final, m_mlp_w1, m_mlp_w2, m_ab_w_in, m_ab_w_out, m_rg_conv_w, m_rg_conv_b, m_rg_w_a, m_rg_b_a, m_rg_w_x, m_rg_b_x, m_rg_lambda, m_hg_lb_logits, m_hg_norm, m_gla_w_in, m_gla_w_out, m_gla_w_gate_up, m_gla_b_gate, m_gla_norm, v_norm_mix, v_norm_mlp, v_norm_final, v_mlp_w1, v_mlp_w2, v_ab_w_in, v_ab_w_out, v_rg_conv_w, v_rg_conv_b, v_rg_w_a, v_rg_b_a, v_rg_w_x, v_rg_b_x, v_rg_lambda, v_hg_lb_logits, v_hg_norm, v_gla_w_in, v_gla_w_out, v_gla_w_gate_up, v_gla_b_gate, v_gla_norm):
    w = dict(norm_mix=norm_mix, norm_mlp=norm_mlp, norm_final=norm_final, mlp_w1=mlp_w1, mlp_w2=mlp_w2, ab_w_in=ab_w_in, ab_w_out=ab_w_out, rg_conv_w=rg_conv_w, rg_conv_b=rg_conv_b, rg_w_a=rg_w_a, rg_b_a=rg_b_a, rg_w_x=rg_w_x, rg_b_x=rg_b_x, rg_lambda=rg_lambda, hg_lb_logits=hg_lb_logits, hg_norm=hg_norm, gla_w_in=gla_w_in, gla_w_out=gla_w_out, gla_w_gate_up=gla_w_gate_up, gla_b_gate=gla_b_gate, gla_norm=gla_norm)
    m = dict(norm_mix=m_norm_mix, norm_mlp=m_norm_mlp, norm_final=m_norm_final, mlp_w1=m_mlp_w1, mlp_w2=m_mlp_w2, ab_w_in=m_ab_w_in, ab_w_out=m_ab_w_out, rg_conv_w=m_rg_conv_w, rg_conv_b=m_rg_conv_b, rg_w_a=m_rg_w_a, rg_b_a=m_rg_b_a, rg_w_x=m_rg_w_x, rg_b_x=m_rg_b_x, rg_lambda=m_rg_lambda, hg_lb_logits=m_hg_lb_logits, hg_norm=m_hg_norm, gla_w_in=m_gla_w_in, gla_w_out=m_gla_w_out, gla_w_gate_up=m_gla_w_gate_up, gla_b_gate=m_gla_b_gate, gla_norm=m_gla_norm)
    v = dict(norm_mix=v_norm_mix, norm_mlp=v_norm_mlp, norm_final=v_norm_final, mlp_w1=v_mlp_w1, mlp_w2=v_mlp_w2, ab_w_in=v_ab_w_in, ab_w_out=v_ab_w_out, rg_conv_w=v_rg_conv_w, rg_conv_b=v_rg_conv_b, rg_w_a=v_rg_w_a, rg_b_a=v_rg_b_a, rg_w_x=v_rg_w_x, rg_b_x=v_rg_b_x, rg_lambda=v_rg_lambda, hg_lb_logits=v_hg_lb_logits, hg_norm=v_hg_norm, gla_w_in=v_gla_w_in, gla_w_out=v_gla_w_out, gla_w_gate_up=v_gla_w_gate_up, gla_b_gate=v_gla_b_gate, gla_norm=v_gla_norm)
    chip = 2 * lax.axis_index("x") + lax.axis_index("y")
    core = lax.axis_index("c")
    sharded_shapes = [w[n].shape for n in SMALL_SHARDED]

    slots = [_into_slot(f"cast_{n}{layer}", w[n], chip, N_CHIPS, BF16, 512, layer) for n, layer in MATRICES]
    early = [i for i, (n, _) in enumerate(MATRICES) if n in EARLY_MATRICES]
    rest = [i for i in range(len(MATRICES)) if i not in early]

    def named(indices, arrays):
        big = {}
        for i, t in zip(indices, arrays):
            big.setdefault(MATRICES[i][0], []).append(t)
        return {n: (v if n in ("mlp_w1", "mlp_w2") else v[0]) for n, v in big.items()}

    gathered = _gather_chips("gather_early", [slots[i] for i in early])
    send_sems, recv_sems, *in_flight, token = _gather_start("gather_rest_start", [slots[i] for i in rest], gathered[0])

    def late_weights(after):
        landed = _gather_wait("gather_rest_wait", in_flight, send_sems, recv_sems, after)
        return _prepare_matrices(named(rest, _hand_over("gather_rest_share", list(landed))))

    big = named(early, gathered)
    vectors = _pack([w[n] for n in SMALL_SHARDED])
    vectors = _into_slot("place_vectors", vectors, chip, N_CHIPS, F32, vectors.shape[0])
    small_all = _unpack(_gather_chips("gather_vectors", [vectors])[0], sharded_shapes, lead=1)
    full = {n: w[n] for n in SMALL_REPLICATED}
    for n, t in zip(SMALL_SHARDED, small_all):
        full[n] = _join_chips(t, t.ndim - 2)

    def pair_sums(tag, arrays, pin=None):
        halves = [t.reshape(N_CHIPS, 2, t.shape[1] // 2, t.shape[2]) for t in arrays]
        from_sibling = _pair_exchange(f"reduce_pair_{tag}", halves, pin)
        return [_pair_add(f"reduce_pair_add_{tag}{i}", h, s, chip, core)
                for i, (h, s) in enumerate(zip(halves, from_sibling))]

    in_flight_grads = {}

    def emit(tag, arrays):
        parts = pair_sums(tag, arrays)
        p16 = [p for _, p in parts]
        send, recv, *rest = _scatter_start(f"reduce_chips_{tag}_start", p16, [lax.empty(p.shape, p.dtype) for p in p16])
        in_flight_grads[tag] = ([p for p, _ in parts], rest[:len(p16)], rest[len(p16):-1], send, recv)
        return rest[-1]

    loss_part, grad_x, g_kernel = _local_step(
        x[0], loss_target[0], _prepare_weights(big, full), token, late_weights, emit)
    g_big, g_full = _finish_grads(g_kernel)

    small_names = SMALL_REPLICATED + SMALL_SHARDED
    reduced_shapes = [g_full[n].shape for n in small_names] + [loss_part.shape]
    g_small = _pack([g_full[n] for n in small_names] + [loss_part])
    device = 2 * chip + core
    g_small = _into_slot("place_small", g_small, device, N_DEV, F32, g_small.shape[0])
    small_send, small_recv, small_in_flight, small_token = _gather_all_start("reduce_small_start", g_small)

    mine = {}
    last = pair_sums("ab", [g_big["ab_w_in"]], small_token)
    from_chips = _chip_scatter("reduce_chips_ab", [p for _, p in last])
    mine["ab"] = [_sum_ring(f"reduce_chips_add_ab{i}", p32, f, chip, core)
                  for i, ((p32, _), f) in enumerate(zip(last, from_chips))]
    for tag, (p32s, p16s, lands, send, recv) in in_flight_grads.items():
        landed = _scatter_wait(f"reduce_chips_{tag}_wait", p16s, lands, send, recv, mine["ab"][0])
        mine[tag] = [_sum_ring(f"reduce_chips_add_{tag}{i}", p32, f, chip, core)
                     for i, (p32, f) in enumerate(zip(p32s, landed))]
    ordered = [mine["mlp0"][0], mine["mlp1"][0], mine["mlp0"][1], mine["mlp1"][1], mine["ab"][0], mine["mlp0"][2],
               *mine["gla"]]
    reduced = [t.reshape(2 * t.shape[1], t.shape[2]) for t in _pair_gather("reduce_share", ordered)]
    by_name = {n: [] for n, _ in MATRICES}
    for (n, _), t in zip(MATRICES, reduced):
        by_name[n].append(t)
    grads = {n: jnp.stack(v) for n, v in by_name.items()}

    g_small_all = _gather_all_wait("reduce_small_wait", small_in_flight, small_send, small_recv, reduced[0])
    g_small_red = _sum_blocks("reduce_small_add", g_small_all, g_small_all.shape[1])
    *small_red, loss_sum = _unpack(g_small_red, reduced_shapes)
    loss = loss_sum[0, 0]
    g_small_full = dict(zip(small_names, small_red))
    for n in SMALL_REPLICATED:
        grads[n] = g_small_full[n]
    for n in SMALL_SHARDED:
        width = w[n].shape[-1]
        grads[n] = lax.dynamic_slice_in_dim(g_small_full[n], chip * width, width, axis=g_small_full[n].ndim - 1)

    delta, new_m, new_v = {}, {}, {}
    for n in by_name:
        flat = [t.reshape(-1, t.shape[-1]) for t in (w[n], grads[n], m[n], v[n])]
        for dst, t in zip((delta, new_m, new_v), _adam(f"adam_{n}", *flat)):
            dst[n] = t.reshape(w[n].shape)
    small_shapes = [w[n].shape for n in small_names]
    packs = [_pack([src[n] for n in small_names]) for src in (w, grads, m, v)]
    d_small, m_small, v_small = _adam("adam_small", *packs)
    for dst, buf in ((delta, d_small), (new_m, m_small), (new_v, v_small)):
        dst.update(zip(small_names, _unpack(buf, small_shapes)))

    return (loss, grad_x[None], *[grads[n] for n in WEIGHTS], *[delta[n] for n in WEIGHTS],
            *[new_m[n] for n in WEIGHTS], *[new_v[n] for n in WEIGHTS])
```

```python
import functools

import jax
import jax.numpy as jnp
from jax import lax
from jax.experimental import pallas as pl
from jax.experimental.pallas import tpu as pltpu

F32 = jnp.float32
BF16 = jnp.bfloat16
MESH = pl.DeviceIdType.MESH

LANES = 128
CHUNK = 64
ATTN_SUB = 2
EPS = 1e-6
RG_C = 8.0
N_CHIPS = 4
N_DEV = 8
GLA_IN_WIDTH = 3104
GLA_IN_PAD = 3200
VMEM_LIMIT = 56 * 1024 * 1024

ADAM_LR = 0.001
ADAM_B1 = 0.9
ADAM_B2 = 0.999
ADAM_EPS = 1e-08
ADAM_WD = 0.01
ADAM_STEP = 10


def _raw_dot(a, b, ca, cb):
    return lax.dot_general(a.astype(BF16), b.astype(BF16), (((ca,), (cb,)), ((), ())),
                           preferred_element_type=F32)


def _raw_nn(a, b):
    return _raw_dot(a, b, 1, 0)


def _raw_nt(a, b):
    return _raw_dot(a, b, 1, 1)


def _raw_tn(a, b):
    return _raw_dot(a, b, 0, 0)


@jax.custom_vjp
def _dot_nn(a, b):
    return _raw_nn(a, b)


def _dot_nn_fwd(a, b):
    return _raw_nn(a, b), (a, b)


def _dot_nn_bwd(res, g):
    a, b = res
    return _raw_nt(g, b), _raw_tn(a, g)


_dot_nn.defvjp(_dot_nn_fwd, _dot_nn_bwd)


@jax.custom_vjp
def _dot_nt(a, b):
    return _raw_nt(a, b)


def _dot_nt_fwd(a, b):
    return _raw_nt(a, b), (a, b)


def _dot_nt_bwd(res, g):
    a, b = res
    return _raw_nn(g, b), _raw_tn(g, a)


_dot_nt.defvjp(_dot_nt_fwd, _dot_nt_bwd)


@jax.custom_vjp
def _dot_tn(a, b):
    return _raw_tn(a, b)


def _dot_tn_fwd(a, b):
    return _raw_tn(a, b), (a, b)


def _dot_tn_bwd(res, g):
    a, b = res
    return _raw_nt(b, g), _raw_nn(a, g)


_dot_tn.defvjp(_dot_tn_fwd, _dot_tn_bwd)


def _tile(n, pref):
    if n <= pref:
        return n
    t = (pref // LANES) * LANES
    while t > LANES and n % t:
        t -= LANES
    assert n % t == 0, (n, pref)
    return t


def _params(sem):
    return pltpu.CompilerParams(dimension_semantics=sem, vmem_limit_bytes=VMEM_LIMIT)


def _rowcall(name, fn, rows, pars, row_outs, par_outs=(), tm=256, pin=None):
    if pin is not None:
        inner, pars = fn, list(pars) + [pin]
        fn = lambda *vals: inner(*vals[:-1])
    n_rows = rows[0][0].shape[0]
    tm = min(tm, n_rows)
    assert n_rows % tm == 0
    n_r, n_p, n_ro = len(rows), len(pars), len(row_outs)

    def body(*refs):
        vals = [r[...].astype(F32) for r in refs[:n_r + n_p]]
        outs = fn(*vals)
        o_refs = refs[n_r + n_p:n_r + n_p + n_ro]
        po_refs = refs[n_r + n_p + n_ro:]
        for o_ref, val in zip(o_refs, outs[:n_ro]):
            o_ref[...] = val.astype(o_ref.dtype)
        first = pl.program_id(0) == 0
        for po_ref, val in zip(po_refs, outs[n_ro:]):
            @pl.when(first)
            def _():
                po_ref[...] = val

            @pl.when(jnp.logical_not(first))
            def _():
                po_ref[...] += val

    def const_map(nd):
        return lambda i: (0,) * nd

    def row_spec(w, cb):
        return pl.BlockSpec((tm, w), lambda i: (i, cb))

    in_specs = [row_spec(w, cb) for _, w, cb in rows]
    in_specs += [pl.BlockSpec(p.shape, const_map(p.ndim)) for p in pars]
    out_specs = [pl.BlockSpec((tm, w), lambda i: (i, 0)) for w, _ in row_outs]
    out_specs += [pl.BlockSpec(tuple(s), const_map(len(s))) for s in par_outs]
    out_shape = [jax.ShapeDtypeStruct((n_rows, w), dt) for w, dt in row_outs]
    out_shape += [jax.ShapeDtypeStruct(tuple(s), F32) for s in par_outs]
    return pl.pallas_call(
        body, name=name, grid=(n_rows // tm,), in_specs=in_specs, out_specs=out_specs, out_shape=out_shape,
        compiler_params=_params(("arbitrary",) if par_outs else ("parallel",)),
    )(*[r[0] for r in rows], *pars)


def _vjp_of(fn, n_prim, n_out, n_par, n_pass=0):
    def bwd(*args):
        prim = args[:n_prim]
        cts = args[n_prim:n_prim + n_out]
        passes = args[n_prim + n_out:n_prim + n_out + 2 * n_pass]
        pars = args[n_prim + n_out + 2 * n_pass:]
        _, vjp = jax.vjp(fn, *prim, *pars)
        grads = vjp(tuple(cts))
        sums = tuple(passes[2 * i] + passes[2 * i + 1] for i in range(n_pass))
        return tuple(grads[:n_prim]) + sums + tuple(grads[n_prim:])
    return bwd


def _mm(name, a, b, mode="nn", extras=(), epi=None, out_dtypes=(F32,), a_pro=None, out_split=None,
        tm=1024, tn=1024, tk=1024):
    split = b.shape[0] if b.ndim == 3 else None
    b_rows, b_cols = b.shape[-2:]
    if mode == "nn":
        (m, k), n = a.shape, b_cols * (split or 1)
    elif mode == "nt":
        (m, k), n = a.shape, b_rows
        assert k == b_cols * (split or 1)
    else:
        assert split is None
        (k, m), n = a.shape, b_cols
    tm, tk = _tile(m, tm), _tile(k, tk)
    tn = _tile(n // out_split, tn) if out_split else _tile(n, tn)
    if split and mode == "nn":
        tn = _tile(b_cols, tn)
    if split and mode == "nt":
        tk = _tile(b_cols, tk)
    nk = k // tk
    raw = {"nn": _raw_nn, "nt": _raw_nt, "tn": _raw_tn}[mode]
    n_e, n_o = len(extras), len(out_dtypes)
    if epi is None:
        epi = lambda acc: (acc,)

    def body(a_ref, b_ref, *rest):
        e_refs, o_refs = rest[:n_e], rest[n_e:n_e + n_o]
        kk = pl.program_id(2)
        a_tile = a_ref[...] if a_pro is None else a_pro(a_ref[...].astype(F32))
        part = raw(a_tile, b_ref[...])

        def finish(total):
            res = epi(total, *[e[...].astype(F32) for e in e_refs])
            for o_ref, r in zip(o_refs, res):
                o_ref[...] = r.astype(o_ref.dtype)

        if nk == 1:
            finish(part)
            return
        acc = rest[-1]

        @pl.when(kk == 0)
        def _():
            acc[...] = part

        @pl.when((kk > 0) & (kk < nk - 1))
        def _():
            acc[...] += part

        @pl.when(kk == nk - 1)
        def _():
            finish(acc[...] + part)

    a_spec = pl.BlockSpec((tk, tm), lambda i, j, kk: (kk, i)) if mode == "tn" else pl.BlockSpec((tm, tk), lambda i, j, kk: (i, kk))
    if split and mode == "nn":
        per = b_cols // tn
        b_spec = pl.BlockSpec((None, tk, tn), lambda i, j, kk: (j // per, kk, j % per))
    elif split:
        per = b_cols // tk
        b_spec = pl.BlockSpec((None, tn, tk), lambda i, j, kk: (kk // per, j, kk % per))
    elif mode == "nt":
        b_spec = pl.BlockSpec((tn, tk), lambda i, j, kk: (j, kk))
    else:
        b_spec = pl.BlockSpec((tk, tn), lambda i, j, kk: (kk, j))
    mn_spec = pl.BlockSpec((tm, tn), lambda i, j, kk: (i, j))
    if out_split:
        assert not extras
        per_out = n // out_split // tn
        out_spec = pl.BlockSpec((None, tm, tn), lambda i, j, kk: (j // per_out, i, j % per_out))
        out_shapes = [jax.ShapeDtypeStruct((out_split, m, n // out_split), dt) for dt in out_dtypes]
    else:
        out_spec = mn_spec
        out_shapes = [jax.ShapeDtypeStruct((m, n), dt) for dt in out_dtypes]
    outs = pl.pallas_call(
        body, name=name, grid=(m // tm, n // tn, nk),
        in_specs=[a_spec, b_spec] + [mn_spec] * n_e, out_specs=[out_spec] * n_o,
        out_shape=out_shapes,
        scratch_shapes=[pltpu.VMEM((tm, tn), F32)] if nk > 1 else [],
        compiler_params=_params(("parallel", "parallel", "arbitrary")),
    )(a, b, *extras)
    return outs[0] if n_o == 1 else outs


def _sigmoid(x):
    return jax.nn.sigmoid(x)


def _silu(x):
    return x * _sigmoid(x)


def _softplus(x):
    return jnp.maximum(x, 0.0) + jnp.log1p(jnp.exp(-jnp.abs(x)))


def _rmsnorm_fn(x, gain):
    return (x * lax.rsqrt(jnp.mean(x * x, axis=-1, keepdims=True) + EPS) * gain,)


def _head_norm(o, gain, n_heads):
    w = o.shape[-1] // n_heads
    parts = []
    for h in range(n_heads):
        oh = o[:, h * w:(h + 1) * w]
        parts.append(oh * lax.rsqrt(jnp.mean(oh * oh, axis=-1, keepdims=True) + EPS))
    return jnp.concatenate(parts, axis=-1) * gain


@jax.custom_jvp
def _neg_expm1(x):
    u = jnp.exp(x)
    is_one = u == 1.0
    return jnp.where(is_one, -x, (1.0 - u) * x / jnp.log(jnp.where(is_one, 2.0, u)))


@_neg_expm1.defjvp
def _neg_expm1_jvp(primals, tangents):
    (x,), (t,) = primals, tangents
    return _neg_expm1(x), -jnp.exp(x) * t


def _rg_gates_fn(xc, wa, wx, ba, bx, lam):
    outs = []
    for d in range(2):
        r = _sigmoid(_dot_nn(xc, wa[d]) + ba[d:d + 1])
        i = _sigmoid(_dot_nn(xc, wx[d]) + bx[d:d + 1])
        log_a = -RG_C * r * _softplus(-lam[d:d + 1])
        outs.append(jnp.exp(log_a))
        outs.append(jnp.sqrt(_neg_expm1(2.0 * log_a)) * (i * xc))
    return tuple(outs)


def _hg_pre_fn(q, f_f, f_b, logits):
    mx = jnp.maximum(logits[0:1], logits[1:2])
    e0 = jnp.exp(logits[0:1] - mx)
    e1 = jnp.exp(logits[1:2] - mx)
    lb = e0 / (e0 + e1)
    outs = [_silu(q)]
    for f in (f_f, f_b):
        outs.append((1.0 - lb) * _sigmoid(-f))
        outs.append(jnp.log(lb + (1.0 - lb) * _sigmoid(f)))
    return tuple(outs)


def _post0_fn(hs, ga, o, g, gain):
    ya = hs * jax.nn.gelu(ga, approximate=True)
    yb = _head_norm(o, gain, 4) * _silu(g)
    return (jnp.concatenate([ya, yb], axis=-1),)


def _post0_fwd_fn(h_f, h_b, ga, o_f, o_b, g, gain):
    return _post0_fn(h_f + h_b, ga, o_f + o_b, g, gain)


def _post0_bwd_fn(h_f, h_b, ga, o_f, o_b, g, dmix, gain):
    _, vjp = jax.vjp(_post0_fn, h_f + h_b, ga, o_f + o_b, g, gain)
    return vjp((dmix,))


def _gla_pre_fn(q, lr, w_up, b_gate):
    outs = [q * (128.0 ** -0.5)]
    for d in range(2):
        z = _dot_nn(lr, w_up[d]) + b_gate[d:d + 1]
        outs.append(-_softplus(-z) * (1.0 / 16.0))
    return tuple(outs)


def _gla_post_fn(o, r, gain):
    return (_head_norm(o, gain, 4) * _silu(r),)


def _gla_post_fwd_fn(o_f, o_b, r, gain):
    return _gla_post_fn(o_f + o_b, r, gain)


def _gla_post_bwd_fn(o_f, o_b, r, dmix, gain):
    _, vjp = jax.vjp(_gla_post_fn, o_f + o_b, r, gain)
    return vjp((dmix,))


def _relu2_bwd_epi(acc, hid):
    return (acc * 2.0 * jnp.maximum(hid, 0.0),)


def _relu2(x):
    r = jnp.maximum(x, 0.0)
    return r * r


def _add_epi(acc, res):
    return (acc + res,)


def _loss_head_fn(h, target, gain):
    def f(h, gain):
        y = _rmsnorm_fn(h, gain)[0]
        err = y - target
        return 0.5 * jnp.sum(jnp.mean(err * err, axis=-1, keepdims=True))
    loss, (dh, dgain) = jax.value_and_grad(f, argnums=(0, 1))(h, gain)
    return dh, jnp.full((1, LANES), loss, F32), dgain


def _adam_fn(w, g, m, v):
    m2 = ADAM_B1 * m + (1.0 - ADAM_B1) * g
    v2 = ADAM_B2 * v + (1.0 - ADAM_B2) * (g * g)
    m_hat = m2 / (1.0 - ADAM_B1 ** ADAM_STEP)
    v_hat = v2 / (1.0 - ADAM_B2 ** ADAM_STEP)
    delta = -ADAM_LR * (m_hat / (jnp.sqrt(v_hat) + ADAM_EPS) + ADAM_WD * w)
    return delta, m2, v2


def _shifted(x, t_idx, off):
    n = x.shape[0]
    rolled = pltpu.roll(x, (-off) % n, 0)
    valid = (t_idx + off >= 0) & (t_idx + off < n)
    return jnp.where(valid, rolled, 0.0)


def _conv_fwd(name, src, colblock, w, b):
    n_rows, width = src.shape[0], w.shape[1]

    def body(x_ref, w_ref, b_ref, o_ref):
        x = x_ref[...]
        t_idx = lax.broadcasted_iota(jnp.int32, x.shape, 0)
        acc = b_ref[...] + w_ref[2:3, :] * x
        acc += w_ref[0:1, :] * _shifted(x, t_idx, -2)
        acc += w_ref[1:2, :] * _shifted(x, t_idx, -1)
        acc += w_ref[3:4, :] * _shifted(x, t_idx, 1)
        o_ref[...] = acc

    nb = width // LANES
    return pl.pallas_call(
        body, name=name, grid=(nb,),
        in_specs=[pl.BlockSpec((n_rows, LANES), lambda j: (0, colblock * nb + j)),
                  pl.BlockSpec((4, LANES), lambda j: (0, j)), pl.BlockSpec((1, LANES), lambda j: (0, j))],
        out_specs=pl.BlockSpec((n_rows, LANES), lambda j: (0, j)),
        out_shape=jax.ShapeDtypeStruct((n_rows, width), F32),
        compiler_params=_params(("parallel",)),
    )(src, w, b)


def _conv_bwd(name, src, colblock, w, d):
    n_rows, width = src.shape[0], w.shape[1]

    def body(x_ref, w_ref, d_ref, dx_ref, dw_ref, db_ref):
        x = x_ref[...]
        g = d_ref[...]
        t_idx = lax.broadcasted_iota(jnp.int32, x.shape, 0)
        dx = w_ref[2:3, :] * g
        dx += w_ref[0:1, :] * _shifted(g, t_idx, 2)
        dx += w_ref[1:2, :] * _shifted(g, t_idx, 1)
        dx += w_ref[3:4, :] * _shifted(g, t_idx, -1)
        dx_ref[...] = dx.astype(dx_ref.dtype)
        dw_ref[0:1, :] = jnp.sum(g * _shifted(x, t_idx, -2), axis=0, keepdims=True)
        dw_ref[1:2, :] = jnp.sum(g * _shifted(x, t_idx, -1), axis=0, keepdims=True)
        dw_ref[2:3, :] = jnp.sum(g * x, axis=0, keepdims=True)
        dw_ref[3:4, :] = jnp.sum(g * _shifted(x, t_idx, 1), axis=0, keepdims=True)
        db_ref[...] = jnp.sum(g, axis=0, keepdims=True)

    nb = width // LANES
    return pl.pallas_call(
        body, name=name, grid=(nb,),
        in_specs=[pl.BlockSpec((n_rows, LANES), lambda j: (0, colblock * nb + j)),
                  pl.BlockSpec((4, LANES), lambda j: (0, j)),
                  pl.BlockSpec((n_rows, LANES), lambda j: (0, j))],
        out_specs=[pl.BlockSpec((n_rows, LANES), lambda j: (0, j)), pl.BlockSpec((4, LANES), lambda j: (0, j)),
                   pl.BlockSpec((1, LANES), lambda j: (0, j))],
        out_shape=[jax.ShapeDtypeStruct((n_rows, width), BF16), jax.ShapeDtypeStruct((4, width), F32),
                   jax.ShapeDtypeStruct((1, width), F32)],
        compiler_params=_params(("parallel",)),
    )(src, w, d)


SUBLANES = 8
SCAN_UNROLL = 8


def _shift_rows(x, d, fill):
    n = x.shape[0]
    t = lax.broadcasted_iota(jnp.int32, x.shape, 0)
    valid = (t >= d) if d > 0 else (t < n + d)
    return jnp.where(valid, pltpu.roll(x, d % n, 0), fill)


def _tile_scan(a, u, reverse):
    d = 1
    while d < a.shape[0]:
        s = -d if reverse else d
        a_sh, u_sh = _shift_rows(a, s, 1.0), _shift_rows(u, s, 0.0)
        u = u + a * u_sh
        a = a * a_sh
        d *= 2
    return a, u


def _edge_row(x, reverse):
    return x[0:1, :] if reverse else x[SUBLANES - 1:SUBLANES, :]


def _scan_specs(n_rows, n):
    return [pl.BlockSpec((n_rows, LANES), lambda j: (0, j))] * n


def _scan_fwd(name, a, u, reverse):
    n_rows, width = a.shape
    n_tiles = n_rows // SUBLANES

    def body(a_ref, u_ref, h_ref):
        def step(i, carry):
            tile = (n_tiles - 1 - i) if reverse else i
            rows = pl.ds(pl.multiple_of(tile * SUBLANES, SUBLANES), SUBLANES)
            acc_a, acc_u = _tile_scan(a_ref[rows, :], u_ref[rows, :], reverse)
            h = acc_u + acc_a * carry
            h_ref[rows, :] = h
            return _edge_row(h, reverse)
        lax.fori_loop(0, n_tiles, step, jnp.zeros((1, LANES), F32), unroll=SCAN_UNROLL)

    return pl.pallas_call(
        body, name=name, grid=(width // LANES,), in_specs=_scan_specs(n_rows, 2), out_specs=_scan_specs(n_rows, 1)[0],
        out_shape=jax.ShapeDtypeStruct((n_rows, width), F32), compiler_params=_params(("parallel",)),
    )(a, u)


def _scan_bwd(name, a, h, dh, reverse):
    n_rows, width = a.shape
    n_tiles = n_rows // SUBLANES
    against = not reverse
    one = -1 if against else 1

    def body(a_ref, h_ref, dh_ref, du_ref, da_ref):
        def step(i, carry):
            g_in, a_edge = carry
            tile = (n_tiles - 1 - i) if against else i
            start = pl.multiple_of(tile * SUBLANES, SUBLANES)
            rows = pl.ds(start, SUBLANES)
            a_tile = a_ref[rows, :]
            coeff = _shift_rows(a_tile, one, a_edge)
            acc_a, acc_u = _tile_scan(coeff, dh_ref[rows, :], against)
            g = acc_u + acc_a * g_in
            du_ref[rows, :] = g
            outside = (start + SUBLANES) if reverse else (start - 1)
            inside = (outside >= 0) & (outside < n_rows)
            h_edge = jnp.where(inside, h_ref[pl.ds(jnp.clip(outside, 0, n_rows - 1), 1), :], 0.0)
            da_ref[rows, :] = g * _shift_rows(h_ref[rows, :], -one, h_edge)
            return _edge_row(g, against), _edge_row(a_tile, against)
        zero = jnp.zeros((1, LANES), F32)
        lax.fori_loop(0, n_tiles, step, (zero, zero), unroll=SCAN_UNROLL)

    return pl.pallas_call(
        body, name=name, grid=(width // LANES,), in_specs=_scan_specs(n_rows, 3), out_specs=_scan_specs(n_rows, 2),
        out_shape=[jax.ShapeDtypeStruct((n_rows, width), F32)] * 2, compiler_params=_params(("parallel",)),
    )(a, h, dh)


def _tri_mask(c, reverse):
    row = lax.broadcasted_iota(jnp.int32, (c, c), 0)
    col = lax.broadcasted_iota(jnp.int32, (c, c), 1)
    return (col >= row) if reverse else (col <= row)


def _cumsum_rows(x, reverse):
    tri = _tri_mask(x.shape[0], reverse).astype(BF16)
    hi = x.astype(BF16)
    rest = x - hi.astype(F32)
    mid = rest.astype(BF16)
    lo = (rest - mid.astype(F32)).astype(BF16)
    return _raw_nn(tri, hi) + _raw_nn(tri, mid) + _raw_nn(tri, lo)


@functools.partial(jax.custom_vjp, nondiff_argnums=(1,))
def _cumsum(x, reverse):
    return _cumsum_rows(x, reverse)


def _cumsum_fwd(x, reverse):
    return _cumsum_rows(x, reverse), None


def _cumsum_bwd(reverse, _, g):
    return (_cumsum_rows(g, not reverse),)


_cumsum.defvjp(_cumsum_fwd, _cumsum_bwd)


def _chunks_fn(qs, ks, vs, lfs, sts, reverses):
    n, c = len(qs), qs[0].shape[0]
    every = range(n)
    tris = [_tri_mask(c, r) for r in reverses]
    cums = [_cumsum(lfs[i], reverses[i]) for i in every]
    rid = lax.broadcasted_iota(jnp.int32, cums[0].shape, 0)

    def pick(cum, r):
        return jnp.sum(jnp.where(rid == r, cum, 0.0), axis=0, keepdims=True)

    refs = [pick(cums[i], (c - 1 - c // 2) if reverses[i] else c // 2) for i in every]
    lasts = [pick(cums[i], 0 if reverses[i] else c - 1) for i in every]
    q_in = [qs[i] * jnp.exp(cums[i] - refs[i]) for i in every]
    k_in = [ks[i] * jnp.exp(refs[i] - cums[i]) for i in every]
    scores = [jnp.where(tris[i], _dot_nt(q_in[i], k_in[i]), 0.0) for i in every]
    o_intra = [_dot_nn(scores[i], vs[i]) for i in every]
    q_out = [qs[i] * jnp.exp(cums[i]) for i in every]
    o_inter = [_dot_nt(q_out[i], sts[i]) for i in every]
    k_state = [ks[i] * jnp.exp(lasts[i] - cums[i]) for i in every]
    upd = [_dot_tn(vs[i], k_state[i]) for i in every]
    st_new = [sts[i] * jnp.exp(lasts[i]) + upd[i] for i in every]
    return [o_intra[i] + o_inter[i] for i in every], st_new


def _attn_fwd(name, q, k_f, k_b, v, lf_f, lf_b, n_heads, dk, dv):
    n_rows = q[0].shape[0]
    n_chunks = n_rows // CHUNK
    n_steps = n_chunks // ATTN_SUB
    wk, wv = n_heads * dk, n_heads * dv

    def spec(width, off, rev):
        return pl.BlockSpec((CHUNK * ATTN_SUB, width), lambda n: ((n_steps - 1 - n) if rev else n, off))

    def sspec(rev):
        return pl.BlockSpec((ATTN_SUB, n_heads, dv, dk), lambda n: ((n_steps - 1 - n) if rev else n, 0, 0, 0))

    def body(qf, kf, vf, lff, qb, kb, vb, lfb, of_ref, ob_ref, sf_ref, sb_ref, st):
        @pl.when(pl.program_id(0) == 0)
        def _():
            st[...] = jnp.zeros_like(st)

        ins = ((qf, kf, vf, lff), (qb, kb, vb, lfb))
        chains = [(d, h) for d in range(2) for h in range(n_heads)]
        ck = [slice(h * dk, (h + 1) * dk) for h in range(n_heads)]
        cv = [slice(h * dv, (h + 1) * dv) for h in range(n_heads)]
        sts = [st[d, h] for d, h in chains]
        done = []
        for sub in range(ATTN_SUB):
            local = (sub, ATTN_SUB - 1 - sub)
            rows = [slice(local[d] * CHUNK, (local[d] + 1) * CHUNK) for d in range(2)]
            qs = [ins[d][0][rows[d], ck[h]] for d, h in chains]
            ks = [ins[d][1][rows[d], ck[h]] for d, h in chains]
            vs = [ins[d][2][rows[d], cv[h]] for d, h in chains]
            lfs = [ins[d][3][rows[d], ck[h]] for d, h in chains]
            os_, st_new = _chunks_fn(qs, ks, vs, lfs, sts, [d == 1 for d, _ in chains])
            done.append((local, rows, sts, os_))
            sts = st_new
        for local, rows, entered, os_ in done:
            for i, (d, h) in enumerate(chains):
                (sf_ref, sb_ref)[d][local[d], h] = entered[i].astype(BF16)
                (of_ref, ob_ref)[d][rows[d], cv[h]] = os_[i]
        for i, (d, h) in enumerate(chains):
            st[d, h] = sts[i]

    in_specs = [spec(wk, q[1], False), spec(wk, k_f[1], False), spec(wv, v[1], False), spec(wk, lf_f[1], False),
                spec(wk, q[1], True), spec(wk, k_b[1], True), spec(wv, v[1], True), spec(wk, lf_b[1], True)]
    return pl.pallas_call(
        body, name=name, grid=(n_steps,), in_specs=in_specs,
        out_specs=[spec(wv, 0, False), spec(wv, 0, True), sspec(False), sspec(True)],
        out_shape=[jax.ShapeDtypeStruct((n_rows, wv), F32)] * 2
        + [jax.ShapeDtypeStruct((n_chunks, n_heads, dv, dk), BF16)] * 2,
        scratch_shapes=[pltpu.VMEM((2, n_heads, dv, dk), F32)],
        compiler_params=_params(("arbitrary",)),
    )(q[0], k_f[0], v[0], lf_f[0], q[0], k_b[0], v[0], lf_b[0])


def _attn_bwd(name, q, k_f, k_b, v, lf_f, lf_b, st_f, st_b, do, n_heads, dk, dv, out_dtype=F32):
    n_rows = q[0].shape[0]
    n_chunks = n_rows // CHUNK
    n_steps = n_chunks // ATTN_SUB
    wk, wv = n_heads * dk, n_heads * dv

    def spec(width, off, rev):
        return pl.BlockSpec((CHUNK * ATTN_SUB, width), lambda n: (n if rev else (n_steps - 1 - n), off))

    def sspec(rev):
        return pl.BlockSpec((ATTN_SUB, n_heads, dv, dk), lambda n: (n if rev else (n_steps - 1 - n), 0, 0, 0))

    def body(qf, kf, vf, lff, sf, dof, qb, kb, vb, lfb, sb, dob,
             dqf, dkf, dvf, dlff, dqb, dkb, dvb, dlfb, dst):
        @pl.when(pl.program_id(0) == 0)
        def _():
            dst[...] = jnp.zeros_like(dst)

        ins = ((qf, kf, vf, lff, sf, dof), (qb, kb, vb, lfb, sb, dob))
        outs = ((dqf, dkf, dvf, dlff), (dqb, dkb, dvb, dlfb))
        chains = [(d, h) for d in range(2) for h in range(n_heads)]
        ck = [slice(h * dk, (h + 1) * dk) for h in range(n_heads)]
        cv = [slice(h * dv, (h + 1) * dv) for h in range(n_heads)]
        fn = functools.partial(_chunks_fn, reverses=[d == 1 for d, _ in chains])
        dsts = [dst[d, h] for d, h in chains]
        done = []
        for sub in range(ATTN_SUB):
            local = (ATTN_SUB - 1 - sub, sub)
            rows = [slice(local[d] * CHUNK, (local[d] + 1) * CHUNK) for d in range(2)]
            qs = [ins[d][0][rows[d], ck[h]] for d, h in chains]
            ks = [ins[d][1][rows[d], ck[h]] for d, h in chains]
            vs = [ins[d][2][rows[d], cv[h]] for d, h in chains]
            lfs = [ins[d][3][rows[d], ck[h]] for d, h in chains]
            sts = [ins[d][4][local[d], h].astype(F32) for d, h in chains]
            dos = [ins[d][5][rows[d], cv[h]] for d, h in chains]
            _, vjp = jax.vjp(fn, qs, ks, vs, lfs, sts)
            dqs, dks, dvs, dlfs, dsts = vjp((dos, dsts))
            done.append((rows, dqs, dks, dvs, dlfs))
        for rows, dqs, dks, dvs, dlfs in done:
            for i, (d, h) in enumerate(chains):
                dq_r, dk_r, dv_r, dlf_r = outs[d]
                dq_r[rows[d], ck[h]] = dqs[i].astype(dq_r.dtype)
                dk_r[rows[d], ck[h]] = dks[i].astype(dk_r.dtype)
                dv_r[rows[d], cv[h]] = dvs[i].astype(dv_r.dtype)
                dlf_r[rows[d], ck[h]] = dlfs[i].astype(dlf_r.dtype)
        for i, (d, h) in enumerate(chains):
            dst[d, h] = dsts[i]

    def dir_specs(kk, lf, rev):
        return [spec(wk, q[1], rev), spec(wk, kk[1], rev), spec(wv, v[1], rev), spec(wk, lf[1], rev), sspec(rev),
                spec(wv, 0, rev)]

    def dir_out_specs(rev):
        return [spec(wk, 0, rev), spec(wk, 0, rev), spec(wv, 0, rev), spec(wk, 0, rev)]

    shapes = [jax.ShapeDtypeStruct((n_rows, wk), out_dtype), jax.ShapeDtypeStruct((n_rows, wk), out_dtype),
              jax.ShapeDtypeStruct((n_rows, wv), out_dtype), jax.ShapeDtypeStruct((n_rows, wk), F32)]
    outs = pl.pallas_call(
        body, name=name, grid=(n_steps,), in_specs=dir_specs(k_f, lf_f, False) + dir_specs(k_b, lf_b, True),
        out_specs=dir_out_specs(False) + dir_out_specs(True), out_shape=shapes + shapes,
        scratch_shapes=[pltpu.VMEM((2, n_heads, dv, dk), F32)],
        compiler_params=_params(("arbitrary",)),
    )(q[0], k_f[0], v[0], lf_f[0], st_f, do, q[0], k_b[0], v[0], lf_b[0], st_b, do)
    return outs[:4], outs[4:]


def _row2(v):
    return v.reshape(1, -1)


def _mlp_fwd(tag, h, gain, w1, w2):
    y = _rowcall(f"{tag}_norm", _rmsnorm_fn, [(h, h.shape[1], 0)], [gain], [(h.shape[1], BF16)], tm=512)[0]
    hid = _mm(f"{tag}_up", y, w1, out_dtypes=(BF16,))
    h_out = _mm(f"{tag}_down", hid, w2, a_pro=_relu2, extras=(h,), epi=_add_epi)
    return h_out, (y, hid)


def _dw(name, a, b, **kw):
    return _mm(name, a, b, mode="tn", epi=lambda acc: (acc, acc), out_dtypes=(F32, BF16), **kw)


def _mlp_bwd(tag, h, gain, w1, w2, saved, dh_out):
    y, hid = saved
    dhid = _mm(f"{tag}_dact", dh_out, w2, mode="nt", extras=(hid,), epi=_relu2_bwd_epi, out_dtypes=(BF16,))
    dw2 = _dw(f"{tag}_dw2", hid, dh_out, a_pro=_relu2)
    dw1 = _dw(f"{tag}_dw1", y, dhid, out_split=N_CHIPS)
    dy = _mm(f"{tag}_dy", dhid, w1, mode="nt")
    dh, dgain = _norm_bwd(f"{tag}_dnorm", h, gain, dy, dh_out)
    return dh, dgain, dw1, dw2


def _norm_bwd(name, h, gain, dy, dres, pin=None):
    d = h.shape[1]

    def fn(h, dy, dres, gain):
        _, vjp = jax.vjp(lambda a, b: _rmsnorm_fn(a, b)[0], h, gain)
        dh, dgain = vjp(dy)
        return dh + dres, dgain

    dh, dgain = _rowcall(name, fn, [(h, d, 0), (dy, d, 0), (dres, d, 0)], [gain], [(d, F32)], [(1, d)], tm=512, pin=pin)
    return dh, dgain


def _local_step(x, target, w, pin=None, late=None, emit=None):
    g = {}
    d_model = x.shape[1]
    rg_w = hg_w = d_model // 2
    pins = []

    def send_off(tag, pairs):
        if emit is not None:
            pins.append(emit(tag, [p[0] for p in pairs], [p[1] for p in pairs]))

    def both(fn, pair):
        return [fn(t) for t in pair]

    def chip_major(t):
        return t.reshape(N_CHIPS, t.shape[0] // N_CHIPS, t.shape[1])

    h_a0 = x
    gain = _row2(w["norm_mix"][0])
    y0 = _rowcall("l0_norm", _rmsnorm_fn, [(h_a0, d_model, 0)], [gain], [(d_model, BF16)], tm=512, pin=pin)[0]
    proj0 = _mm("l0_in", y0, w["ab_w_in"])
    conv_w, conv_b = w["rg_conv_w"], _row2(w["rg_conv_b"])
    xc = _conv_fwd("rg_conv", proj0, 0, conv_w, conv_b)
    gate_pars = [w["rg_wa_bd"], w["rg_wx_bd"], w["rg_b_a"], w["rg_b_x"], w["rg_lambda"]]
    a_f, u_f, a_b, u_b = _rowcall("rg_gates", _rg_gates_fn, [(xc, rg_w, 0)], gate_pars, [(rg_w, F32)] * 4)
    hs_f = _scan_fwd("rg_scan_f", a_f, u_f, False)
    hs_b = _scan_fwd("rg_scan_b", a_b, u_b, True)
    hg_rows = [(proj0, hg_w, 2), (proj0, hg_w, 3), (proj0, hg_w, 4)]
    qh, k_f, lf_f, k_b, lf_b = _rowcall("hg_pre", _hg_pre_fn, hg_rows, [w["hg_lb_logits"]], [(hg_w, F32)] * 5)
    iv = (proj0, 5)
    o_f, o_b, st_f, st_b = _attn_fwd("hg_attn", (qh, 0), (k_f, 0), (k_b, 0), iv, (lf_f, 0), (lf_b, 0), 4, 128, 128)
    post0_rows = [(hs_f, rg_w, 0), (hs_b, rg_w, 0), (proj0, rg_w, 1), (o_f, hg_w, 0), (o_b, hg_w, 0), (proj0, hg_w, 6)]
    hg_gain = _row2(w["hg_norm"])
    mix_in0 = _rowcall("l0_post", _post0_fwd_fn, post0_rows, [hg_gain], [(d_model, BF16)])[0]
    if late is not None:
        w = {**w, **late(mix_in0)}
    h_b0 = _mm("l0_out", mix_in0, w["ab_w_out"], extras=(h_a0,), epi=_add_epi)
    h_c0, mlp0 = _mlp_fwd("mlp0", h_b0, _row2(w["norm_mlp"][0]), w["mlp_w1"][0], w["mlp_w2"][0])

    h_a1 = h_c0
    gain1 = _row2(w["norm_mix"][1])
    y1 = _rowcall("l1_norm", _rmsnorm_fn, [(h_a1, d_model, 0)], [gain1], [(d_model, BF16)], tm=512)[0]
    proj1 = _mm("l1_in", y1, w["gla_w_in_pad"], tn=640)
    gla_pars = [w["gla_w_up_pad"], w["gla_b_gate"]]
    gq, glf_f, glf_b = _rowcall("gla_pre", _gla_pre_fn, [(proj1, 512, 0), (proj1, LANES, 24)], gla_pars, [(512, F32)] * 3)
    gk, gv = (proj1, 1), (proj1, 1)
    go_f, go_b, gst_f, gst_b = _attn_fwd("gla_attn", (gq, 0), gk, gk, gv, (glf_f, 0), (glf_b, 0), 4, 128, 256)
    gla_gain = _row2(w["gla_norm"])
    post1_rows = [(go_f, d_model, 0), (go_b, d_model, 0), (proj1, d_model, 2)]
    mix_in1 = _rowcall("l1_post", _gla_post_fwd_fn, post1_rows, [gla_gain], [(d_model, BF16)])[0]
    h_b1 = _mm("l1_out", mix_in1, w["gla_w_out"], extras=(h_a1,), epi=_add_epi)
    h_c1, mlp1 = _mlp_fwd("mlp1", h_b1, _row2(w["norm_mlp"][1]), w["mlp_w1"][1], w["mlp_w2"][1])

    dh, loss, g["norm_final"] = _rowcall(
        "loss_head", _loss_head_fn, [(h_c1, d_model, 0), (target, d_model, 0)], [_row2(w["norm_final"])],
        [(d_model, F32)], [(1, LANES), (1, d_model)], tm=512)

    dh, g_nmlp1, g_w1_1, g_w2_1 = _mlp_bwd("mlp1", h_b1, _row2(w["norm_mlp"][1]), w["mlp_w1"][1], w["mlp_w2"][1], mlp1, dh)
    send_off("mlp1", [g_w1_1, both(chip_major, g_w2_1)])
    dmix1 = _mm("l1_dout", dh, w["gla_w_out"], mode="nt")
    g_gla_out = _dw("l1_dwout", mix_in1, dh)
    g["gla_w_out"] = g_gla_out[0]
    dgo, dr, g["gla_norm"] = _rowcall(
        "l1_dpost", _gla_post_bwd_fn, post1_rows + [(dmix1, d_model, 0)], [gla_gain],
        [(d_model, F32), (d_model, BF16)], [(1, d_model)], pin=pins.pop() if pins else None)
    (dq_f, dk_f, dv_f, dlf_f), (dq_b, dk_b, dv_b, dlf_b) = _attn_bwd(
        "gla_dattn", (gq, 0), gk, gk, gv, (glf_f, 0), (glf_b, 0), gst_f, gst_b, dgo, 4, 128, 256)

    def gla_pre_bwd(q, lr, dq1, dq2, dlf1, dlf2, dk1, dk2, dv1, dv2, w_up, b_gate):
        dlr = jnp.zeros_like(lr)
        dws, dbs = [], []
        for d, dlf in enumerate((dlf1, dlf2)):
            z = _raw_nn(lr, w_up[d]) + b_gate[d:d + 1]
            dz = dlf * _sigmoid(-z) * (1.0 / 16.0)
            dlr = dlr + _raw_nt(dz, w_up[d])
            dws.append(_raw_tn(dz, lr))
            dbs.append(jnp.sum(dz, axis=0, keepdims=True))
        return ((dq1 + dq2) * (128.0 ** -0.5), dk1 + dk2, dv1 + dv2, dlr, dws[0], dws[1], dbs[0], dbs[1])

    rows = [(proj1, 512, 0), (proj1, LANES, 24), (dq_f, 512, 0), (dq_b, 512, 0), (dlf_f, 512, 0), (dlf_b, 512, 0),
            (dk_f, 512, 0), (dk_b, 512, 0), (dv_f, d_model, 0), (dv_b, d_model, 0)]
    dq, dk, dv, dlr, dwt_f, dwt_b, db_f, db_b = _rowcall(
        "gla_dpre", gla_pre_bwd, rows, gla_pars, [(512, BF16), (512, BF16), (d_model, BF16), (LANES, BF16)],
        [(512, LANES), (512, LANES), (1, 512), (1, 512)])
    g["gla_w_up_pad"] = jnp.stack([dwt_f.T, dwt_b.T])
    g["gla_b_gate"] = jnp.concatenate([db_f, db_b], axis=0)
    dproj1 = jnp.concatenate([dq, dk, dv, dr, dlr], axis=1)
    g_gla_in = both(lambda t: _split_chips(t[:, :GLA_IN_WIDTH], 1), _dw("l1_dwin", y1, dproj1, tn=640))
    g["gla_w_in"] = g_gla_in[0]
    send_off("gla", [g_gla_in, both(chip_major, g_gla_out)])
    dy1 = _mm("l1_dy", dproj1, w["gla_w_in_pad"], mode="nt", tk=640)
    dh, g_nmix1 = _norm_bwd("l1_dnorm", h_a1, gain1, dy1, dh, pin=pins.pop() if pins else None)

    dh, g_nmlp0, g_w1_0, g_w2_0 = _mlp_bwd("mlp0", h_b0, _row2(w["norm_mlp"][0]), w["mlp_w1"][0], w["mlp_w2"][0], mlp0, dh)
    g_ab_out = _dw("l0_dwout", mix_in0, dh)
    g["ab_w_out"] = g_ab_out[0]
    send_off("mlp0", [g_w1_0, both(chip_major, g_w2_0), both(chip_major, g_ab_out)])
    dmix0 = _mm("l0_dout", dh, w["ab_w_out"], mode="nt")
    dhs, dga, do, dg, g["hg_norm"] = _rowcall(
        "l0_dpost", _post0_bwd_fn, post0_rows + [(dmix0, d_model, 0)], [hg_gain],
        [(rg_w, F32), (rg_w, BF16), (hg_w, F32), (hg_w, BF16)], [(1, hg_w)], pin=pins.pop() if pins else None)
    (dqh_f, dk_f, div_f, dlf_f), (dqh_b, dk_b, div_b, dlf_b) = _attn_bwd(
        "hg_dattn", (qh, 0), (k_f, 0), (k_b, 0), iv, (lf_f, 0), (lf_b, 0), st_f, st_b, do, 4, 128, 128)

    def hg_pre_bwd(q, f_f, f_b, dq1, dq2, dk1, dlf1, dk2, dlf2, dv1, dv2, logits):
        _, vjp = jax.vjp(_hg_pre_fn, q, f_f, f_b, logits)
        dq, df_f, df_b, dlogits = vjp((dq1 + dq2, dk1, dlf1, dk2, dlf2))
        return dq, df_f, df_b, dv1 + dv2, dlogits

    rows = hg_rows + [(t, hg_w, 0) for t in (dqh_f, dqh_b, dk_f, dlf_f, dk_b, dlf_b, div_f, div_b)]
    dq, df_f, df_b, div, g["hg_lb_logits"] = _rowcall(
        "hg_dpre", hg_pre_bwd, rows, [w["hg_lb_logits"]], [(hg_w, BF16)] * 4, [(2, hg_w)])
    du_f, da_f = _scan_bwd("rg_dscan_f", a_f, hs_f, dhs, False)
    du_b, da_b = _scan_bwd("rg_dscan_b", a_b, hs_b, dhs, True)
    gates_bwd = _vjp_of(_rg_gates_fn, 1, 4, 5)
    rows = [(xc, rg_w, 0), (da_f, rg_w, 0), (du_f, rg_w, 0), (da_b, rg_w, 0), (du_b, rg_w, 0)]
    dxc, g["rg_wa_bd"], g["rg_wx_bd"], g["rg_b_a"], g["rg_b_x"], g["rg_lambda"] = _rowcall(
        "rg_dgates", gates_bwd, rows, gate_pars, [(rg_w, F32)],
        [(2, rg_w, rg_w), (2, rg_w, rg_w), (2, rg_w), (2, rg_w), (2, rg_w)])
    dxa, g["rg_conv_w"], g["rg_conv_b"] = _conv_bwd("rg_dconv", proj0, 0, conv_w, dxc)
    dproj0 = jnp.concatenate([dxa, dga, dq, df_f, df_b, div, dg], axis=1)
    g["ab_w_in"], g["ab_w_in_bf16"] = _dw("l0_dwin", y0, dproj0, out_split=N_CHIPS)
    dy0 = _mm("l0_dy", dproj0, w["ab_w_in"], mode="nt")
    grad_x, g_nmix0 = _norm_bwd("l0_dnorm", h_a0, gain, dy0, dh)

    g["norm_mix"] = jnp.concatenate([g_nmix0, g_nmix1], axis=0)
    g["norm_mlp"] = jnp.concatenate([g_nmlp0, g_nmlp1], axis=0)
    g["mlp_w1"] = [g_w1_0[0], g_w1_1[0]]
    g["mlp_w2"] = [g_w2_0[0], g_w2_1[0]]
    return loss, grad_x, g


def _block_diag(w):
    d, g, n, _ = w.shape
    eye = jnp.eye(g, dtype=w.dtype)
    return (w[:, :, :, None, :] * eye[None, :, None, :, None]).reshape(d, g * n, g * n)


def _block_diag_extract(wbd, g):
    d, gn, _ = wbd.shape
    n = gn // g
    blocks = wbd.reshape(d, g, n, g, n)
    return jnp.stack([blocks[:, i, :, i, :] for i in range(g)], axis=1)


def _prepare_weights(big, full):
    w = {k: full[k] for k in ("norm_mix", "norm_mlp", "norm_final", "hg_lb_logits")}
    for k in ("rg_conv_w", "rg_conv_b", "rg_b_a", "rg_b_x", "rg_lambda", "hg_norm", "gla_b_gate", "gla_norm"):
        w[k] = full[k][0]
    w["rg_wa_bd"] = _block_diag(full["rg_w_a"][0])
    w["rg_wx_bd"] = _block_diag(full["rg_w_x"][0])
    up = full["gla_w_gate_up"][0]
    rank = up.shape[1]
    pad = jnp.zeros((2, LANES, up.shape[2]), F32)
    w["gla_w_up_pad"] = pad.at[0, 0:rank].set(up[0]).at[1, rank:2 * rank].set(up[1])
    w.update(_prepare_matrices(big))
    return w


def _prepare_matrices(big):
    w = {}
    if "mlp_w1" in big:
        w["mlp_w1"] = list(big["mlp_w1"])
        w["mlp_w2"] = [t.reshape(-1, t.shape[-1]) for t in big["mlp_w2"]]
    if "ab_w_in" in big:
        w["ab_w_in"] = big["ab_w_in"]
    if "ab_w_out" in big:
        w["ab_w_out"] = big["ab_w_out"].reshape(-1, big["ab_w_out"].shape[-1])
    if "gla_w_in" in big:
        w["gla_w_out"] = big["gla_w_out"].reshape(-1, big["gla_w_out"].shape[-1])
        gla_in = _join_chips(big["gla_w_in"], 1)
        w["gla_w_in_pad"] = jnp.pad(gla_in, ((0, 0), (0, GLA_IN_PAD - gla_in.shape[1])))
    return w


def _finish_grads(g, rank=16, rg_blocks=8):
    def chip_major(t):
        return t.reshape(N_CHIPS, t.shape[0] // N_CHIPS, t.shape[1])

    big = {
        "mlp_w1": list(g["mlp_w1"]), "mlp_w2": [chip_major(t) for t in g["mlp_w2"]],
        "ab_w_in": g["ab_w_in"], "ab_w_out": chip_major(g["ab_w_out"]),
        "gla_w_in": g["gla_w_in"], "gla_w_out": chip_major(g["gla_w_out"]),
    }
    small = {
        "norm_mix": g["norm_mix"], "norm_mlp": g["norm_mlp"], "norm_final": g["norm_final"][0],
        "rg_conv_w": g["rg_conv_w"][None], "rg_conv_b": g["rg_conv_b"],
        "rg_w_a": _block_diag_extract(g["rg_wa_bd"], rg_blocks)[None], "rg_b_a": g["rg_b_a"][None],
        "rg_w_x": _block_diag_extract(g["rg_wx_bd"], rg_blocks)[None], "rg_b_x": g["rg_b_x"][None],
        "rg_lambda": g["rg_lambda"][None], "hg_lb_logits": g["hg_lb_logits"], "hg_norm": g["hg_norm"],
        "gla_w_gate_up": jnp.stack([g["gla_w_up_pad"][0, 0:rank], g["gla_w_up_pad"][1, rank:2 * rank]])[None],
        "gla_b_gate": g["gla_b_gate"][None], "gla_norm": g["gla_norm"],
    }
    return big, small


MATRICES = (("mlp_w1", 0), ("mlp_w1", 1), ("mlp_w2", 0), ("mlp_w2", 1), ("ab_w_in", 0), ("ab_w_out", 0),
            ("gla_w_in", 0), ("gla_w_out", 0))
EARLY_MATRICES = ("ab_w_in",)
SMALL_SHARDED = ("rg_conv_w", "rg_b_a", "rg_b_x", "rg_lambda", "gla_w_gate_up", "gla_b_gate", "gla_norm")
SMALL_REPLICATED = ("norm_mix", "norm_mlp", "norm_final", "rg_conv_b", "rg_w_a", "rg_w_x", "hg_lb_logits", "hg_norm")
WEIGHTS = ("norm_mix", "norm_mlp", "norm_final", "mlp_w1", "mlp_w2", "ab_w_in", "ab_w_out", "rg_conv_w", "rg_conv_b",
           "rg_w_a", "rg_b_a", "rg_w_x", "rg_b_x", "rg_lambda", "hg_lb_logits", "hg_norm", "gla_w_in", "gla_w_out",
           "gla_w_gate_up", "gla_b_gate", "gla_norm")
ROW_ALIGN = 16


def _pack(arrays, lead=0):
    head = arrays[0].shape[:lead]
    flat = jnp.concatenate([a.reshape(head + (-1,)) for a in arrays], axis=lead)
    n = flat.shape[-1]
    quantum = LANES * ROW_ALIGN
    padded = -(-n // quantum) * quantum
    if padded != n:
        flat = jnp.pad(flat, [(0, 0)] * lead + [(0, padded - n)])
    return flat.reshape(head + (padded // LANES, LANES))


def _unpack(buf, shapes, lead=0):
    head = buf.shape[:lead]
    flat = buf.reshape(head + (-1,))
    out, off = [], 0
    for s in shapes:
        n = 1
        for v in s:
            n *= v
        out.append(lax.slice_in_dim(flat, off, off + n, axis=lead).reshape(head + tuple(s)))
        off += n
    return out


def _join_chips(gathered, axis):
    t = jnp.moveaxis(gathered, 0, axis)
    return t.reshape(t.shape[:axis] + (t.shape[axis] * t.shape[axis + 1],) + t.shape[axis + 2:])


def _split_chips(full, axis):
    s = full.shape
    t = full.reshape(s[:axis] + (N_CHIPS, s[axis] // N_CHIPS) + s[axis + 1:])
    return jnp.moveaxis(t, axis, 0)


_ANY = pl.BlockSpec(memory_space=pl.ANY)


def _place():
    return lax.axis_index("x"), lax.axis_index("y"), lax.axis_index("c")


def _into_slot(name, src, slot, n_slots, dtype, tm, layer=None):
    r, lanes = src.shape[-2:]
    tm = _row_tile(r, tm, ROW_ALIGN)

    def body(slot_ref, in_ref, o_ref):
        o_ref[...] = in_ref[...].astype(o_ref.dtype)

    if layer is None:
        in_spec = pl.BlockSpec((tm, lanes), lambda i, slot_ref: (i, 0))
    else:
        in_spec = pl.BlockSpec((None, tm, lanes), lambda i, slot_ref: (layer, i, 0))
    grid_spec = pltpu.PrefetchScalarGridSpec(
        num_scalar_prefetch=1, grid=(r // tm,), in_specs=[in_spec],
        out_specs=pl.BlockSpec((None, tm, lanes), lambda i, slot_ref: (slot_ref[0], i, 0)))
    return pl.pallas_call(
        body, name=name, grid_spec=grid_spec, out_shape=jax.ShapeDtypeStruct((n_slots, r, lanes), dtype),
        compiler_params=_params(("parallel",)),
    )(slot.reshape(1).astype(jnp.int32), src)


def _chip_peers():
    x, y, c = _place()
    return 2 * x + y, c, [(1 - x, y), (x, 1 - y), (1 - x, 1 - y)]


def _comm_call(name, body, ins, out_shapes, n_sems, aliases=None):
    return pl.pallas_call(
        body, name=name, in_specs=[_ANY] * len(ins), out_specs=[_ANY] * len(out_shapes), out_shape=out_shapes,
        input_output_aliases=aliases or {},
        scratch_shapes=[pltpu.SemaphoreType.DMA((n_sems,)), pltpu.SemaphoreType.DMA((n_sems,))],
    )(*ins)


def _gather_chips(name, bufs):
    n = len(bufs)

    def body(*refs):
        outs, send_sems, recv_sems = refs[n:2 * n], refs[2 * n], refs[2 * n + 1]
        x, y, c = _place()
        me, _, peers = _chip_peers()

        def rows(a, block, half):
            rh = outs[a].shape[1] // 2
            return outs[a].at[block, pl.ds(half * rh, rh)]

        def copy(a, j, block, half, to, sem):
            return pltpu.make_async_remote_copy(
                src_ref=rows(a, block, half), dst_ref=rows(a, block, half), send_sem=send_sems.at[sem],
                recv_sem=recv_sems.at[sem], device_id=to, device_id_type=MESH)

        def over_ici(a, j, block):
            px, py = peers[j]
            return copy(a, j, block, c, (px, py, c), 6 * a + j)

        def to_sibling(a, j, block, half):
            return copy(a, j, block, half, (x, y, 1 - c), 6 * a + 3 + j)

        sends = [over_ici(a, j, me) for a in range(n) for j in range(3)]
        for cp in sends:
            cp.start()
        for a in range(n):
            for j, (px, py) in enumerate(peers):
                over_ici(a, j, 2 * px + py).wait_recv()
                handed = to_sibling(a, j, 2 * px + py, c)
                handed.start()
                sends.append(handed)
        for a in range(n):
            for j, (px, py) in enumerate(peers):
                to_sibling(a, j, 2 * px + py, 1 - c).wait_recv()
        for cp in sends:
            cp.wait_send()

    shapes = [jax.ShapeDtypeStruct(b.shape, b.dtype) for b in bufs]
    return _comm_call(name, body, bufs, shapes, 6 * n, {a: a for a in range(n)})


_HBM = pl.BlockSpec(memory_space=pltpu.HBM)
_SEM = pl.BlockSpec(memory_space=pltpu.SEMAPHORE)
_EFFECT = pltpu.SideEffectType.DATAFLOW_SIDE_EFFECTING


def _half_rows(ref, block, half):
    rh = ref.shape[1] // 2
    return ref.at[block, pl.ds(half * rh, rh)]


def _gather_start(name, bufs, after):
    n = len(bufs)

    def body(*refs):
        ins, send_sems, recv_sems, token = refs[:n], refs[n + 1], refs[n + 2], refs[-1]
        me, c, peers = _chip_peers()
        for a in range(n):
            mine = _half_rows(ins[a], me, c)
            for j, (px, py) in enumerate(peers):
                pltpu.make_async_remote_copy(
                    src_ref=mine, dst_ref=mine, send_sem=send_sems.at[3 * a + j], recv_sem=recv_sems.at[3 * a + j],
                    device_id=(px, py, c), device_id_type=MESH).start()
        token[...] = jnp.zeros_like(token)

    out_shape = (pltpu.SemaphoreType.DMA((3 * n,)), pltpu.SemaphoreType.DMA((3 * n,)),
                 *[pltpu.HBM(b.shape, b.dtype) for b in bufs], jax.ShapeDtypeStruct((8, LANES), F32))
    return pl.pallas_call(
        body, name=name, out_shape=out_shape, in_specs=[_HBM] * n + [_ANY],
        out_specs=(_SEM, _SEM, *[_HBM] * n, pl.BlockSpec(memory_space=pltpu.VMEM)),
        input_output_aliases={a: 2 + a for a in range(n)},
        compiler_params=pltpu.CompilerParams(has_side_effects=_EFFECT),
    )(*[pltpu.with_memory_space_constraint(b, pltpu.HBM) for b in bufs], after)


def _gather_wait(name, bufs, send_sems, recv_sems, after):
    n = len(bufs)

    def body(*refs):
        ins, send_sems, recv_sems = refs[:n], refs[n], refs[n + 1]
        me, c, peers = _chip_peers()
        for a in range(n):
            for j, (px, py) in enumerate(peers):
                copy = pltpu.make_async_remote_copy(
                    src_ref=_half_rows(ins[a], me, c), dst_ref=_half_rows(ins[a], 2 * px + py, c),
                    send_sem=send_sems.at[3 * a + j], recv_sem=recv_sems.at[3 * a + j],
                    device_id=(px, py, c), device_id_type=MESH)
                copy.wait_send()
                copy.wait_recv()

    return pl.pallas_call(
        body, name=name, out_shape=tuple(pltpu.HBM(b.shape, b.dtype) for b in bufs),
        in_specs=[_HBM] * n + [_SEM, _SEM, _ANY], out_specs=tuple([_HBM] * n),
        input_output_aliases={a: a for a in range(n)},
        compiler_params=pltpu.CompilerParams(has_side_effects=_EFFECT),
    )(*bufs, send_sems, recv_sems, after)


def _hand_over(name, bufs):
    n = len(bufs)

    def body(*refs):
        outs, send_sems, recv_sems = refs[n:2 * n], refs[2 * n], refs[2 * n + 1]
        x, y, c = _place()
        _, _, peers = _chip_peers()

        def copy(a, j, half):
            px, py = peers[j]
            rows = _half_rows(outs[a], 2 * px + py, half)
            return pltpu.make_async_remote_copy(
                src_ref=rows, dst_ref=rows, send_sem=send_sems.at[3 * a + j], recv_sem=recv_sems.at[3 * a + j],
                device_id=(x, y, 1 - c), device_id_type=MESH)

        sends = [copy(a, j, c) for a in range(n) for j in range(3)]
        for cp in sends:
            cp.start()
        for a in range(n):
            for j in range(3):
                copy(a, j, 1 - c).wait_recv()
        for cp in sends:
            cp.wait_send()

    shapes = [jax.ShapeDtypeStruct(b.shape, b.dtype) for b in bufs]
    return _comm_call(name, body, bufs, shapes, 3 * n, {a: a for a in range(n)})


def _pair_gather(name, bufs):
    n = len(bufs)

    def body(*refs):
        ins, outs, send_sems, recv_sems = refs[:n], refs[n:2 * n], refs[2 * n], refs[2 * n + 1]
        x, y, c = _place()

        def copy(a, block):
            return pltpu.make_async_remote_copy(
                src_ref=ins[a].at[block], dst_ref=outs[a].at[block], send_sem=send_sems.at[a],
                recv_sem=recv_sems.at[a], device_id=(x, y, 1 - c), device_id_type=MESH)

        sends = [copy(a, c) for a in range(n)]
        for cp in sends:
            cp.start()
        for a in range(n):
            copy(a, 1 - c).wait_recv()
        for cp in sends:
            cp.wait_send()

    shapes = [jax.ShapeDtypeStruct(b.shape, b.dtype) for b in bufs]
    return _comm_call(name, body, bufs, shapes, n, {a: a for a in range(n)})


def _all_peers():
    x, y, c = _place()
    peers = []
    for mask in range(1, N_DEV):
        fx, fy, fc = (mask >> 2) & 1, (mask >> 1) & 1, mask & 1
        peers.append((jnp.where(fx, 1 - x, x), jnp.where(fy, 1 - y, y), jnp.where(fc, 1 - c, c)))
    return 4 * x + 2 * y + c, peers


def _reduce_copies(srcs, lands, send_sems, recv_sems):
    me, peers = _all_peers()
    sends, arrivals = [], []
    for a in range(len(srcs)):
        for j, (px, py, pc) in enumerate(peers):
            k = (N_DEV - 1) * a + j
            sends.append(pltpu.make_async_remote_copy(
                src_ref=srcs[a].at[2 * px + py, pc], dst_ref=lands[a].at[me], send_sem=send_sems.at[k],
                recv_sem=recv_sems.at[k], device_id=(px, py, pc), device_id_type=MESH))
            arrivals.append(pltpu.make_async_remote_copy(
                src_ref=srcs[a].at[2 * px + py, pc], dst_ref=lands[a].at[4 * px + 2 * py + pc],
                send_sem=send_sems.at[k], recv_sem=recv_sems.at[k], device_id=(px, py, pc), device_id_type=MESH))
    return sends, arrivals


def _reduce_direct(name, srcs, pin=None):
    n = len(srcs)
    extra = [] if pin is None else [pin]

    def body(*refs):
        ins, outs = refs[:n], refs[n + len(extra):2 * n + len(extra)]
        sends, arrivals = _reduce_copies(ins, outs, refs[-2], refs[-1])
        for cp in sends:
            cp.start()
        for cp in arrivals:
            cp.wait_recv()
        for cp in sends:
            cp.wait_send()

    shapes = [jax.ShapeDtypeStruct((N_DEV,) + s.shape[2:], s.dtype) for s in srcs]
    return _comm_call(name, body, list(srcs) + extra, shapes, (N_DEV - 1) * n)


def _reduce_start(name, srcs):
    n = len(srcs)
    lands = [lax.empty((N_DEV,) + s.shape[2:], s.dtype) for s in srcs]

    def body(*refs):
        sends, _ = _reduce_copies(refs[:n], refs[n:2 * n], refs[2 * n], refs[2 * n + 1])
        for cp in sends:
            cp.start()
        refs[-1][...] = jnp.zeros_like(refs[-1])

    bufs = list(srcs) + lands
    n_sems = (N_DEV - 1) * n
    out_shape = (pltpu.SemaphoreType.DMA((n_sems,)), pltpu.SemaphoreType.DMA((n_sems,)),
                 *[pltpu.HBM(b.shape, b.dtype) for b in bufs], jax.ShapeDtypeStruct((8, LANES), F32))
    return pl.pallas_call(
        body, name=name, out_shape=out_shape, in_specs=[_HBM] * (2 * n),
        out_specs=(_SEM, _SEM, *[_HBM] * (2 * n), pl.BlockSpec(memory_space=pltpu.VMEM)),
        input_output_aliases={a: 2 + a for a in range(2 * n)},
        compiler_params=pltpu.CompilerParams(has_side_effects=_EFFECT),
    )(*[pltpu.with_memory_space_constraint(b, pltpu.HBM) for b in bufs])


def _reduce_wait(name, srcs, lands, send_sems, recv_sems, after):
    n = len(srcs)

    def body(*refs):
        sends, arrivals = _reduce_copies(refs[:n], refs[n:2 * n], refs[2 * n], refs[2 * n + 1])
        for cp in sends:
            cp.wait_send()
        for cp in arrivals:
            cp.wait_recv()

    bufs = list(srcs) + list(lands)
    outs = pl.pallas_call(
        body, name=name, out_shape=tuple(pltpu.HBM(b.shape, b.dtype) for b in bufs),
        in_specs=[_HBM] * (2 * n) + [_SEM, _SEM, _ANY], out_specs=tuple([_HBM] * (2 * n)),
        input_output_aliases={a: a for a in range(2 * n)},
        compiler_params=pltpu.CompilerParams(has_side_effects=_EFFECT),
    )(*bufs, send_sems, recv_sems, after)
    return list(outs[n:])


def _reduce_sum(name, own, land, chip, core):
    n, rh, lanes = land.shape
    tm = _row_tile(rh, 1024, ROW_ALIGN)

    def body(idx_ref, own_ref, *rest):
        total = own_ref[...]
        for g_ref in rest[:-1]:
            total = total + g_ref[...].astype(F32)
        rest[-1][...] = total

    def block(k):
        return pl.BlockSpec((None, tm, lanes), lambda i, idx_ref: ((2 * idx_ref[0] + idx_ref[1] + k) % n, i, 0))

    grid_spec = pltpu.PrefetchScalarGridSpec(
        num_scalar_prefetch=1, grid=(rh // tm,),
        in_specs=[pl.BlockSpec((None, None, tm, lanes), lambda i, idx_ref: (idx_ref[0], idx_ref[1], i, 0))]
        + [block(k) for k in range(1, n)],
        out_specs=pl.BlockSpec((None, tm, lanes), lambda i, idx_ref: (idx_ref[1], i, 0)))
    return pl.pallas_call(
        body, name=name, grid_spec=grid_spec, out_shape=jax.ShapeDtypeStruct((2, rh, lanes), F32),
        compiler_params=_params(("parallel",)),
    )(jnp.stack([chip, core]).astype(jnp.int32), own, *[land] * (n - 1))


def _gather_all_start(name, buf):
    def body(in_ref, send_sems, recv_sems, out_ref, token):
        me, peers = _all_peers()
        for j, peer in enumerate(peers):
            pltpu.make_async_remote_copy(
                src_ref=in_ref.at[me], dst_ref=in_ref.at[me], send_sem=send_sems.at[j], recv_sem=recv_sems.at[j],
                device_id=peer, device_id_type=MESH).start()
        token[...] = jnp.zeros_like(token)

    n = N_DEV - 1
    return pl.pallas_call(
        body, name=name, in_specs=[_HBM],
        out_shape=(pltpu.SemaphoreType.DMA((n,)), pltpu.SemaphoreType.DMA((n,)), pltpu.HBM(buf.shape, buf.dtype),
                   jax.ShapeDtypeStruct((8, LANES), F32)),
        out_specs=(_SEM, _SEM, _HBM, pl.BlockSpec(memory_space=pltpu.VMEM)), input_output_aliases={0: 2},
        compiler_params=pltpu.CompilerParams(has_side_effects=_EFFECT),
    )(pltpu.with_memory_space_constraint(buf, pltpu.HBM))


def _gather_all_wait(name, buf, send_sems, recv_sems, after):
    def body(in_ref, send_sems, recv_sems, after_ref, out_ref):
        me, peers = _all_peers()
        for j, (px, py, pc) in enumerate(peers):
            copy = pltpu.make_async_remote_copy(
                src_ref=in_ref.at[me], dst_ref=in_ref.at[4 * px + 2 * py + pc], send_sem=send_sems.at[j],
                recv_sem=recv_sems.at[j], device_id=(px, py, pc), device_id_type=MESH)
            copy.wait_send()
            copy.wait_recv()

    return pl.pallas_call(
        body, name=name, in_specs=[_HBM, _SEM, _SEM, _ANY], out_shape=pltpu.HBM(buf.shape, buf.dtype),
        out_specs=_HBM, input_output_aliases={0: 0},
        compiler_params=pltpu.CompilerParams(has_side_effects=_EFFECT),
    )(buf, send_sems, recv_sems, after)


def _sum_blocks(name, stacked, tm):
    n, r, lanes = stacked.shape

    def body(in_ref, o_ref):
        acc = in_ref[0]
        for j in range(1, n):
            acc = acc + in_ref[j]
        o_ref[...] = acc

    return pl.pallas_call(
        body, name=name, grid=(r // tm,), in_specs=[pl.BlockSpec((n, tm, lanes), lambda i: (0, i, 0))],
        out_specs=pl.BlockSpec((tm, lanes), lambda i: (i, 0)), out_shape=jax.ShapeDtypeStruct((r, lanes), F32),
        compiler_params=_params(("parallel",)),
    )(stacked)


def _row_tile(rows, pref, align):
    best = None
    for t in range(align, min(rows, pref) + 1, align):
        if rows % t == 0:
            best = t
    assert best is not None, (rows, pref, align)
    return best


def _adam(name, w, g, m, v):
    rows, width = w.shape
    tm = _row_tile(rows, max(8, 4096 * LANES // width), 8)
    args = [(t, width, 0) for t in (w, g, m, v)]
    return _rowcall(name, _adam_fn, args, [], [(width, F32)] * 3, tm=tm)


def kernel(x, norm_mix, norm_mlp, norm_final, mlp_w1, mlp_w2, ab_w_in, ab_w_out, rg_conv_w, rg_conv_b, rg_w_a, rg_b_a, rg_w_x, rg_b_x, rg_lambda, hg_lb_logits, hg_norm, gla_w_in, gla_w_out, gla_w_gate_up, gla_b_gate, gla_norm, loss_target, m_norm_mix, m_norm_mlp, m_norm_final, m_mlp_w1, m_mlp_w2, m_ab_w_in, m_ab_w_out, m_rg_conv_w, m_rg_conv_b, m_rg_w_a, m_rg_b_a, m_rg_w_x, m_rg_b_x, m_rg_lambda, m_hg_lb_logits, m_hg_norm, m_gla_w_in, m_gla_w_out, m_gla_w_gate_up, m_gla_b_gate, m_gla_norm, v_norm_mix, v_norm_mlp, v_norm_final, v_mlp_w1, v_mlp_w2, v_ab_w_in, v_ab_w_out, v_rg_conv_w, v_rg_conv_b, v_rg_w_a, v_rg_b_a, v_rg_w_x, v_rg_b_x, v_rg_lambda, v_hg_lb_logits, v_hg_norm, v_gla_w_in, v_gla_w_out, v_gla_w_gate_up, v_gla_b_gate, v_gla_norm):
    w = dict(norm_mix=norm_mix, norm_mlp=norm_mlp, norm_final=norm_final, mlp_w1=mlp_w1, mlp_w2=mlp_w2, ab_w_in=ab_w_in, ab_w_out=ab_w_out, rg_conv_w=rg_conv_w, rg_conv_b=rg_conv_b, rg_w_a=rg_w_a, rg_b_a=rg_b_a, rg_w_x=rg_w_x, rg_b_x=rg_b_x, rg_lambda=rg_lambda, hg_lb_logits=hg_lb_logits, hg_norm=hg_norm, gla_w_in=gla_w_in, gla_w_out=gla_w_out, gla_w_gate_up=gla_w_gate_up, gla_b_gate=gla_b_gate, gla_norm=gla_norm)
    m = dict(norm_mix=m_norm_mix, norm_mlp=m_norm_mlp, norm_final=m_norm_final, mlp_w1=m_mlp_w1, mlp_w2=m_mlp_w2, ab_w_in=m_ab_w_in, ab_w_out=m_ab_w_out, rg_conv_w=m_rg_conv_w, rg_conv_b=m_rg_conv_b, rg_w_a=m_rg_w_a, rg_b_a=m_rg_b_a, rg_w_x=m_rg_w_x, rg_b_x=m_rg_b_x, rg_lambda=m_rg_lambda, hg_lb_logits=m_hg_lb_logits, hg_norm=m_hg_norm, gla_w_in=m_gla_w_in, gla_w_out=m_gla_w_out, gla_w_gate_up=m_gla_w_gate_up, gla_b_gate=m_gla_b_gate, gla_norm=m_gla_norm)
    v = dict(norm_mix=v_norm_mix, norm_mlp=v_norm_mlp, norm_final=v_norm_final, mlp_w1=v_mlp_w1, mlp_w2=v_mlp_w2, ab_w_in=v_ab_w_in, ab_w_out=v_ab_w_out, rg_conv_w=v_rg_conv_w, rg_conv_b=v_rg_conv_b, rg_w_a=v_rg_w_a, rg_b_a=v_rg_b_a, rg_w_x=v_rg_w_x, rg_b_x=v_rg_b_x, rg_lambda=v_rg_lambda, hg_lb_logits=v_hg_lb_logits, hg_norm=v_hg_norm, gla_w_in=v_gla_w_in, gla_w_out=v_gla_w_out, gla_w_gate_up=v_gla_w_gate_up, gla_b_gate=v_gla_b_gate, gla_norm=v_gla_norm)
    chip = 2 * lax.axis_index("x") + lax.axis_index("y")
    core = lax.axis_index("c")
    sharded_shapes = [w[n].shape for n in SMALL_SHARDED]

    slots = [_into_slot(f"cast_{n}{layer}", w[n], chip, N_CHIPS, BF16, 512, layer) for n, layer in MATRICES]
    early = [i for i, (n, _) in enumerate(MATRICES) if n in EARLY_MATRICES]
    rest = [i for i in range(len(MATRICES)) if i not in early]

    def named(indices, arrays):
        big = {}
        for i, t in zip(indices, arrays):
            big.setdefault(MATRICES[i][0], []).append(t)
        return {n: (v if n in ("mlp_w1", "mlp_w2") else v[0]) for n, v in big.items()}

    gathered = _gather_chips("gather_early", [slots[i] for i in early])
    send_sems, recv_sems, *in_flight, token = _gather_start("gather_rest_start", [slots[i] for i in rest], gathered[0])

    def late_weights(after):
        landed = _gather_wait("gather_rest_wait", in_flight, send_sems, recv_sems, after)
        return _prepare_matrices(named(rest, _hand_over("gather_rest_share", list(landed))))

    big = named(early, gathered)
    vectors = _pack([w[n] for n in SMALL_SHARDED])
    vectors = _into_slot("place_vectors", vectors, chip, N_CHIPS, F32, vectors.shape[0])
    small_all = _unpack(_gather_chips("gather_vectors", [vectors])[0], sharded_shapes, lead=1)
    full = {n: w[n] for n in SMALL_REPLICATED}
    for n, t in zip(SMALL_SHARDED, small_all):
        full[n] = _join_chips(t, t.ndim - 2)

    def halves(t):
        return t.reshape(N_CHIPS, 2, t.shape[1] // 2, t.shape[2])

    in_flight_grads = {}

    def emit(tag, arrays32, arrays16):
        n = len(arrays16)
        send, recv, *rest = _reduce_start(f"reduce_{tag}_start", [halves(t) for t in arrays16])
        in_flight_grads[tag] = ([halves(t) for t in arrays32], rest[:n], rest[n:2 * n], send, recv)
        return rest[-1]

    loss_part, grad_x, g_kernel = _local_step(
        x[0], loss_target[0], _prepare_weights(big, full), token, late_weights, emit)
    g_big, g_full = _finish_grads(g_kernel)

    small_names = SMALL_REPLICATED + SMALL_SHARDED
    reduced_shapes = [g_full[n].shape for n in small_names] + [loss_part.shape]
    g_small = _pack([g_full[n] for n in small_names] + [loss_part])
    device = 2 * chip + core
    g_small = _into_slot("place_small", g_small, device, N_DEV, F32, g_small.shape[0])
    small_send, small_recv, small_in_flight, small_token = _gather_all_start("reduce_small_start", g_small)

    mine = {}
    landed = _reduce_direct("reduce_ab", [halves(g_kernel["ab_w_in_bf16"])], small_token)
    mine["ab"] = [_reduce_sum("reduce_add_ab", halves(g_big["ab_w_in"]), landed[0], chip, core)]
    for tag, (own, srcs, lands, send, recv) in in_flight_grads.items():
        landed = _reduce_wait(f"reduce_{tag}_wait", srcs, lands, send, recv, mine["ab"][0])
        mine[tag] = [_reduce_sum(f"reduce_add_{tag}{i}", o, f, chip, core) for i, (o, f) in enumerate(zip(own, landed))]
    ordered = [mine["mlp0"][0], mine["mlp1"][0], mine["mlp0"][1], mine["mlp1"][1], mine["ab"][0], mine["mlp0"][2],
               *mine["gla"]]
    reduced = [t.reshape(2 * t.shape[1], t.shape[2]) for t in _pair_gather("reduce_share", ordered)]
    by_name = {n: [] for n, _ in MATRICES}
    for (n, _), t in zip(MATRICES, reduced):
        by_name[n].append(t)
    grads = {n: jnp.stack(v) for n, v in by_name.items()}

    g_small_all = _gather_all_wait("reduce_small_wait", small_in_flight, small_send, small_recv, reduced[0])
    g_small_red = _sum_blocks("reduce_small_add", g_small_all, g_small_all.shape[1])
    *small_red, loss_sum = _unpack(g_small_red, reduced_shapes)
    loss = loss_sum[0, 0]
    g_small_full = dict(zip(small_names, small_red))
    for n in SMALL_REPLICATED:
        grads[n] = g_small_full[n]
    for n in SMALL_SHARDED:
        width = w[n].shape[-1]
        grads[n] = lax.dynamic_slice_in_dim(g_small_full[n], chip * width, width, axis=g_small_full[n].ndim - 1)

    delta, new_m, new_v = {}, {}, {}
    for n in by_name:
        flat = [t.reshape(-1, t.shape[-1]) for t in (w[n], grads[n], m[n], v[n])]
        for dst, t in zip((delta, new_m, new_v), _adam(f"adam_{n}", *flat)):
            dst[n] = t.reshape(w[n].shape)
    small_shapes = [w[n].shape for n in small_names]
    packs = [_pack([src[n] for n in small_names]) for src in (w, grads, m, v)]
    d_small, m_small, v_small = _adam("adam_small", *packs)
    for dst, buf in ((delta, d_small), (new_m, m_small), (new_v, v_small)):
        dst.update(zip(small_names, _unpack(buf, small_shapes)))

    return (loss, grad_x[None], *[grads[n] for n in WEIGHTS], *[delta[n] for n in WEIGHTS],
            *[new_m[n] for n in WEIGHTS], *[new_v[n] for n in WEIGHTS])
```

```python
import functools

import jax
import jax.numpy as jnp
from jax import lax
from jax.experimental import pallas as pl
from jax.experimental.pallas import tpu as pltpu

F32 = jnp.float32
BF16 = jnp.bfloat16
MESH = pl.DeviceIdType.MESH

LANES = 128
CHUNK = 64
ATTN_SUB = 2
EPS = 1e-6
RG_C = 8.0
N_CHIPS = 4
N_DEV = 8
GLA_IN_WIDTH = 3104
GLA_IN_PAD = 3200
VMEM_LIMIT = 56 * 1024 * 1024

ADAM_LR = 0.001
ADAM_B1 = 0.9
ADAM_B2 = 0.999
ADAM_EPS = 1e-08
ADAM_WD = 0.01
ADAM_STEP = 10


def _raw_dot(a, b, ca, cb):
    return lax.dot_general(a.astype(BF16), b.astype(BF16), (((ca,), (cb,)), ((), ())),
                           preferred_element_type=F32)


def _raw_nn(a, b):
    return _raw_dot(a, b, 1, 0)


def _raw_nt(a, b):
    return _raw_dot(a, b, 1, 1)


def _raw_tn(a, b):
    return _raw_dot(a, b, 0, 0)


@jax.custom_vjp
def _dot_nn(a, b):
    return _raw_nn(a, b)


def _dot_nn_fwd(a, b):
    return _raw_nn(a, b), (a, b)


def _dot_nn_bwd(res, g):
    a, b = res
    return _raw_nt(g, b), _raw_tn(a, g)


_dot_nn.defvjp(_dot_nn_fwd, _dot_nn_bwd)


@jax.custom_vjp
def _dot_nt(a, b):
    return _raw_nt(a, b)


def _dot_nt_fwd(a, b):
    return _raw_nt(a, b), (a, b)


def _dot_nt_bwd(res, g):
    a, b = res
    return _raw_nn(g, b), _raw_tn(g, a)


_dot_nt.defvjp(_dot_nt_fwd, _dot_nt_bwd)


@jax.custom_vjp
def _dot_tn(a, b):
    return _raw_tn(a, b)


def _dot_tn_fwd(a, b):
    return _raw_tn(a, b), (a, b)


def _dot_tn_bwd(res, g):
    a, b = res
    return _raw_nt(b, g), _raw_nn(a, g)


_dot_tn.defvjp(_dot_tn_fwd, _dot_tn_bwd)


def _tile(n, pref):
    if n <= pref:
        return n
    t = (pref // LANES) * LANES
    while t > LANES and n % t:
        t -= LANES
    assert n % t == 0, (n, pref)
    return t


def _params(sem):
    return pltpu.CompilerParams(dimension_semantics=sem, vmem_limit_bytes=VMEM_LIMIT)


def _rowcall(name, fn, rows, pars, row_outs, par_outs=(), tm=256, pin=None):
    if pin is not None:
        inner, pars = fn, list(pars) + [pin]
        fn = lambda *vals: inner(*vals[:-1])
    n_rows = rows[0][0].shape[0]
    tm = min(tm, n_rows)
    assert n_rows % tm == 0
    n_r, n_p, n_ro = len(rows), len(pars), len(row_outs)

    def body(*refs):
        vals = [r[...].astype(F32) for r in refs[:n_r + n_p]]
        outs = fn(*vals)
        o_refs = refs[n_r + n_p:n_r + n_p + n_ro]
        po_refs = refs[n_r + n_p + n_ro:]
        for o_ref, val in zip(o_refs, outs[:n_ro]):
            o_ref[...] = val.astype(o_ref.dtype)
        first = pl.program_id(0) == 0
        for po_ref, val in zip(po_refs, outs[n_ro:]):
            @pl.when(first)
            def _():
                po_ref[...] = val

            @pl.when(jnp.logical_not(first))
            def _():
                po_ref[...] += val

    def const_map(nd):
        return lambda i: (0,) * nd

    def row_spec(w, cb):
        return pl.BlockSpec((tm, w), lambda i: (i, cb))

    in_specs = [row_spec(w, cb) for _, w, cb in rows]
    in_specs += [pl.BlockSpec(p.shape, const_map(p.ndim)) for p in pars]
    out_specs = [pl.BlockSpec((tm, w), lambda i: (i, 0)) for w, _ in row_outs]
    out_specs += [pl.BlockSpec(tuple(s), const_map(len(s))) for s in par_outs]
    out_shape = [jax.ShapeDtypeStruct((n_rows, w), dt) for w, dt in row_outs]
    out_shape += [jax.ShapeDtypeStruct(tuple(s), F32) for s in par_outs]
    return pl.pallas_call(
        body, name=name, grid=(n_rows // tm,), in_specs=in_specs, out_specs=out_specs, out_shape=out_shape,
        compiler_params=_params(("arbitrary",) if par_outs else ("parallel",)),
    )(*[r[0] for r in rows], *pars)


def _vjp_of(fn, n_prim, n_out, n_par, n_pass=0):
    def bwd(*args):
        prim = args[:n_prim]
        cts = args[n_prim:n_prim + n_out]
        passes = args[n_prim + n_out:n_prim + n_out + 2 * n_pass]
        pars = args[n_prim + n_out + 2 * n_pass:]
        _, vjp = jax.vjp(fn, *prim, *pars)
        grads = vjp(tuple(cts))
        sums = tuple(passes[2 * i] + passes[2 * i + 1] for i in range(n_pass))
        return tuple(grads[:n_prim]) + sums + tuple(grads[n_prim:])
    return bwd


def _mm(name, a, b, mode="nn", extras=(), epi=None, out_dtypes=(F32,), a_pro=None, out_split=None,
        tm=1024, tn=1024, tk=1024):
    split = b.shape[0] if b.ndim == 3 else None
    b_rows, b_cols = b.shape[-2:]
    if mode == "nn":
        (m, k), n = a.shape, b_cols * (split or 1)
    elif mode == "nt":
        (m, k), n = a.shape, b_rows
        assert k == b_cols * (split or 1)
    else:
        assert split is None
        (k, m), n = a.shape, b_cols
    tm, tk = _tile(m, tm), _tile(k, tk)
    tn = _tile(n // out_split, tn) if out_split else _tile(n, tn)
    if split and mode == "nn":
        tn = _tile(b_cols, tn)
    if split and mode == "nt":
        tk = _tile(b_cols, tk)
    nk = k // tk
    raw = {"nn": _raw_nn, "nt": _raw_nt, "tn": _raw_tn}[mode]
    n_e, n_o = len(extras), len(out_dtypes)
    if epi is None:
        epi = lambda acc: (acc,)

    def body(a_ref, b_ref, *rest):
        e_refs, o_refs = rest[:n_e], rest[n_e:n_e + n_o]
        kk = pl.program_id(2)
        a_tile = a_ref[...] if a_pro is None else a_pro(a_ref[...].astype(F32))
        part = raw(a_tile, b_ref[...])

        def finish(total):
            res = epi(total, *[e[...].astype(F32) for e in e_refs])
            for o_ref, r in zip(o_refs, res):
                o_ref[...] = r.astype(o_ref.dtype)

        if nk == 1:
            finish(part)
            return
        acc = rest[-1]

        @pl.when(kk == 0)
        def _():
            acc[...] = part

        @pl.when((kk > 0) & (kk < nk - 1))
        def _():
            acc[...] += part

        @pl.when(kk == nk - 1)
        def _():
            finish(acc[...] + part)

    a_spec = pl.BlockSpec((tk, tm), lambda i, j, kk: (kk, i)) if mode == "tn" else pl.BlockSpec((tm, tk), lambda i, j, kk: (i, kk))
    if split and mode == "nn":
        per = b_cols // tn
        b_spec = pl.BlockSpec((None, tk, tn), lambda i, j, kk: (j // per, kk, j % per))
    elif split:
        per = b_cols // tk
        b_spec = pl.BlockSpec((None, tn, tk), lambda i, j, kk: (kk // per, j, kk % per))
    elif mode == "nt":
        b_spec = pl.BlockSpec((tn, tk), lambda i, j, kk: (j, kk))
    else:
        b_spec = pl.BlockSpec((tk, tn), lambda i, j, kk: (kk, j))
    mn_spec = pl.BlockSpec((tm, tn), lambda i, j, kk: (i, j))
    if out_split:
        assert not extras
        per_out = n // out_split // tn
        out_spec = pl.BlockSpec((None, tm, tn), lambda i, j, kk: (j // per_out, i, j % per_out))
        out_shapes = [jax.ShapeDtypeStruct((out_split, m, n // out_split), dt) for dt in out_dtypes]
    else:
        out_spec = mn_spec
        out_shapes = [jax.ShapeDtypeStruct((m, n), dt) for dt in out_dtypes]
    outs = pl.pallas_call(
        body, name=name, grid=(m // tm, n // tn, nk),
        in_specs=[a_spec, b_spec] + [mn_spec] * n_e, out_specs=[out_spec] * n_o,
        out_shape=out_shapes,
        scratch_shapes=[pltpu.VMEM((tm, tn), F32)] if nk > 1 else [],
        compiler_params=_params(("parallel", "parallel", "arbitrary")),
    )(a, b, *extras)
    return outs[0] if n_o == 1 else outs


def _sigmoid(x):
    return jax.nn.sigmoid(x)


def _silu(x):
    return x * _sigmoid(x)


def _softplus(x):
    return jnp.maximum(x, 0.0) + jnp.log1p(jnp.exp(-jnp.abs(x)))


def _rmsnorm_fn(x, gain):
    return (x * lax.rsqrt(jnp.mean(x * x, axis=-1, keepdims=True) + EPS) * gain,)


def _head_norm(o, gain, n_heads):
    w = o.shape[-1] // n_heads
    parts = []
    for h in range(n_heads):
        oh = o[:, h * w:(h + 1) * w]
        parts.append(oh * lax.rsqrt(jnp.mean(oh * oh, axis=-1, keepdims=True) + EPS))
    return jnp.concatenate(parts, axis=-1) * gain


@jax.custom_jvp
def _neg_expm1(x):
    u = jnp.exp(x)
    is_one = u == 1.0
    return jnp.where(is_one, -x, (1.0 - u) * x / jnp.log(jnp.where(is_one, 2.0, u)))


@_neg_expm1.defjvp
def _neg_expm1_jvp(primals, tangents):
    (x,), (t,) = primals, tangents
    return _neg_expm1(x), -jnp.exp(x) * t


def _rg_gates_fn(xc, wa, wx, ba, bx, lam):
    outs = []
    for d in range(2):
        r = _sigmoid(_dot_nn(xc, wa[d]) + ba[d:d + 1])
        i = _sigmoid(_dot_nn(xc, wx[d]) + bx[d:d + 1])
        log_a = -RG_C * r * _softplus(-lam[d:d + 1])
        outs.append(jnp.exp(log_a))
        outs.append(jnp.sqrt(_neg_expm1(2.0 * log_a)) * (i * xc))
    return tuple(outs)


def _hg_pre_fn(q, f_f, f_b, logits):
    mx = jnp.maximum(logits[0:1], logits[1:2])
    e0 = jnp.exp(logits[0:1] - mx)
    e1 = jnp.exp(logits[1:2] - mx)
    lb = e0 / (e0 + e1)
    outs = [_silu(q)]
    for f in (f_f, f_b):
        outs.append((1.0 - lb) * _sigmoid(-f))
        outs.append(jnp.log(lb + (1.0 - lb) * _sigmoid(f)))
    return tuple(outs)


def _post0_fn(hs, ga, o, g, gain):
    ya = hs * jax.nn.gelu(ga, approximate=True)
    yb = _head_norm(o, gain, 4) * _silu(g)
    return (jnp.concatenate([ya, yb], axis=-1),)


def _post0_fwd_fn(h_f, h_b, ga, o_f, o_b, g, gain):
    return _post0_fn(h_f + h_b, ga, o_f + o_b, g, gain)


def _post0_bwd_fn(h_f, h_b, ga, o_f, o_b, g, dmix, gain):
    _, vjp = jax.vjp(_post0_fn, h_f + h_b, ga, o_f + o_b, g, gain)
    return vjp((dmix,))


def _gla_pre_fn(q, lr, w_up, b_gate):
    outs = [q * (128.0 ** -0.5)]
    for d in range(2):
        z = _dot_nn(lr, w_up[d]) + b_gate[d:d + 1]
        outs.append(-_softplus(-z) * (1.0 / 16.0))
    return tuple(outs)


def _gla_post_fn(o, r, gain):
    return (_head_norm(o, gain, 4) * _silu(r),)


def _gla_post_fwd_fn(o_f, o_b, r, gain):
    return _gla_post_fn(o_f + o_b, r, gain)


def _gla_post_bwd_fn(o_f, o_b, r, dmix, gain):
    _, vjp = jax.vjp(_gla_post_fn, o_f + o_b, r, gain)
    return vjp((dmix,))


def _relu2_bwd_epi(acc, hid):
    return (acc * 2.0 * jnp.maximum(hid, 0.0),)


def _relu2(x):
    r = jnp.maximum(x, 0.0)
    return r * r


def _add_epi(acc, res):
    return (acc + res,)


def _loss_head_fn(h, target, gain):
    def f(h, gain):
        y = _rmsnorm_fn(h, gain)[0]
        err = y - target
        return 0.5 * jnp.sum(jnp.mean(err * err, axis=-1, keepdims=True))
    loss, (dh, dgain) = jax.value_and_grad(f, argnums=(0, 1))(h, gain)
    return dh, jnp.full((1, LANES), loss, F32), dgain


def _adam_fn(w, g, m, v):
    m2 = ADAM_B1 * m + (1.0 - ADAM_B1) * g
    v2 = ADAM_B2 * v + (1.0 - ADAM_B2) * (g * g)
    m_hat = m2 / (1.0 - ADAM_B1 ** ADAM_STEP)
    v_hat = v2 / (1.0 - ADAM_B2 ** ADAM_STEP)
    delta = -ADAM_LR * (m_hat / (jnp.sqrt(v_hat) + ADAM_EPS) + ADAM_WD * w)
    return delta, m2, v2


def _shifted(x, t_idx, off):
    n = x.shape[0]
    rolled = pltpu.roll(x, (-off) % n, 0)
    valid = (t_idx + off >= 0) & (t_idx + off < n)
    return jnp.where(valid, rolled, 0.0)


def _conv_fwd(name, src, colblock, w, b):
    n_rows, width = src.shape[0], w.shape[1]

    def body(x_ref, w_ref, b_ref, o_ref):
        x = x_ref[...]
        t_idx = lax.broadcasted_iota(jnp.int32, x.shape, 0)
        acc = b_ref[...] + w_ref[2:3, :] * x
        acc += w_ref[0:1, :] * _shifted(x, t_idx, -2)
        acc += w_ref[1:2, :] * _shifted(x, t_idx, -1)
        acc += w_ref[3:4, :] * _shifted(x, t_idx, 1)
        o_ref[...] = acc

    nb = width // LANES
    return pl.pallas_call(
        body, name=name, grid=(nb,),
        in_specs=[pl.BlockSpec((n_rows, LANES), lambda j: (0, colblock * nb + j)),
                  pl.BlockSpec((4, LANES), lambda j: (0, j)), pl.BlockSpec((1, LANES), lambda j: (0, j))],
        out_specs=pl.BlockSpec((n_rows, LANES), lambda j: (0, j)),
        out_shape=jax.ShapeDtypeStruct((n_rows, width), F32),
        compiler_params=_params(("parallel",)),
    )(src, w, b)


def _conv_bwd(name, src, colblock, w, d):
    n_rows, width = src.shape[0], w.shape[1]

    def body(x_ref, w_ref, d_ref, dx_ref, dw_ref, db_ref):
        x = x_ref[...]
        g = d_ref[...]
        t_idx = lax.broadcasted_iota(jnp.int32, x.shape, 0)
        dx = w_ref[2:3, :] * g
        dx += w_ref[0:1, :] * _shifted(g, t_idx, 2)
        dx += w_ref[1:2, :] * _shifted(g, t_idx, 1)
        dx += w_ref[3:4, :] * _shifted(g, t_idx, -1)
        dx_ref[...] = dx.astype(dx_ref.dtype)
        dw_ref[0:1, :] = jnp.sum(g * _shifted(x, t_idx, -2), axis=0, keepdims=True)
        dw_ref[1:2, :] = jnp.sum(g * _shifted(x, t_idx, -1), axis=0, keepdims=True)
        dw_ref[2:3, :] = jnp.sum(g * x, axis=0, keepdims=True)
        dw_ref[3:4, :] = jnp.sum(g * _shifted(x, t_idx, 1), axis=0, keepdims=True)
        db_ref[...] = jnp.sum(g, axis=0, keepdims=True)

    nb = width // LANES
    return pl.pallas_call(
        body, name=name, grid=(nb,),
        in_specs=[pl.BlockSpec((n_rows, LANES), lambda j: (0, colblock * nb + j)),
                  pl.BlockSpec((4, LANES), lambda j: (0, j)),
                  pl.BlockSpec((n_rows, LANES), lambda j: (0, j))],
        out_specs=[pl.BlockSpec((n_rows, LANES), lambda j: (0, j)), pl.BlockSpec((4, LANES), lambda j: (0, j)),
                   pl.BlockSpec((1, LANES), lambda j: (0, j))],
        out_shape=[jax.ShapeDtypeStruct((n_rows, width), BF16), jax.ShapeDtypeStruct((4, width), F32),
                   jax.ShapeDtypeStruct((1, width), F32)],
        compiler_params=_params(("parallel",)),
    )(src, w, d)


SUBLANES = 8
SCAN_UNROLL = 8


def _shift_rows(x, d, fill):
    n = x.shape[0]
    t = lax.broadcasted_iota(jnp.int32, x.shape, 0)
    valid = (t >= d) if d > 0 else (t < n + d)
    return jnp.where(valid, pltpu.roll(x, d % n, 0), fill)


def _tile_scan(a, u, reverse):
    d = 1
    while d < a.shape[0]:
        s = -d if reverse else d
        a_sh, u_sh = _shift_rows(a, s, 1.0), _shift_rows(u, s, 0.0)
        u = u + a * u_sh
        a = a * a_sh
        d *= 2
    return a, u


def _edge_row(x, reverse):
    return x[0:1, :] if reverse else x[SUBLANES - 1:SUBLANES, :]


def _scan_specs(n_rows, n):
    return [pl.BlockSpec((n_rows, LANES), lambda j: (0, j))] * n


def _scan_fwd(name, a, u, reverse):
    n_rows, width = a.shape
    n_tiles = n_rows // SUBLANES

    def body(a_ref, u_ref, h_ref):
        def step(i, carry):
            tile = (n_tiles - 1 - i) if reverse else i
            rows = pl.ds(pl.multiple_of(tile * SUBLANES, SUBLANES), SUBLANES)
            acc_a, acc_u = _tile_scan(a_ref[rows, :], u_ref[rows, :], reverse)
            h = acc_u + acc_a * carry
            h_ref[rows, :] = h
            return _edge_row(h, reverse)
        lax.fori_loop(0, n_tiles, step, jnp.zeros((1, LANES), F32), unroll=SCAN_UNROLL)

    return pl.pallas_call(
        body, name=name, grid=(width // LANES,), in_specs=_scan_specs(n_rows, 2), out_specs=_scan_specs(n_rows, 1)[0],
        out_shape=jax.ShapeDtypeStruct((n_rows, width), F32), compiler_params=_params(("parallel",)),
    )(a, u)


def _scan_bwd(name, a, h, dh, reverse):
    n_rows, width = a.shape
    n_tiles = n_rows // SUBLANES
    against = not reverse
    one = -1 if against else 1

    def body(a_ref, h_ref, dh_ref, du_ref, da_ref):
        def step(i, carry):
            g_in, a_edge = carry
            tile = (n_tiles - 1 - i) if against else i
            start = pl.multiple_of(tile * SUBLANES, SUBLANES)
            rows = pl.ds(start, SUBLANES)
            a_tile = a_ref[rows, :]
            coeff = _shift_rows(a_tile, one, a_edge)
            acc_a, acc_u = _tile_scan(coeff, dh_ref[rows, :], against)
            g = acc_u + acc_a * g_in
            du_ref[rows, :] = g
            outside = (start + SUBLANES) if reverse else (start - 1)
            inside = (outside >= 0) & (outside < n_rows)
            h_edge = jnp.where(inside, h_ref[pl.ds(jnp.clip(outside, 0, n_rows - 1), 1), :], 0.0)
            da_ref[rows, :] = g * _shift_rows(h_ref[rows, :], -one, h_edge)
            return _edge_row(g, against), _edge_row(a_tile, against)
        zero = jnp.zeros((1, LANES), F32)
        lax.fori_loop(0, n_tiles, step, (zero, zero), unroll=SCAN_UNROLL)

    return pl.pallas_call(
        body, name=name, grid=(width // LANES,), in_specs=_scan_specs(n_rows, 3), out_specs=_scan_specs(n_rows, 2),
        out_shape=[jax.ShapeDtypeStruct((n_rows, width), F32)] * 2, compiler_params=_params(("parallel",)),
    )(a, h, dh)


def _tri_mask(c, reverse):
    row = lax.broadcasted_iota(jnp.int32, (c, c), 0)
    col = lax.broadcasted_iota(jnp.int32, (c, c), 1)
    return (col >= row) if reverse else (col <= row)


def _cumsum_rows(x, reverse):
    tri = _tri_mask(x.shape[0], reverse).astype(BF16)
    hi = x.astype(BF16)
    rest = x - hi.astype(F32)
    mid = rest.astype(BF16)
    lo = (rest - mid.astype(F32)).astype(BF16)
    return _raw_nn(tri, hi) + _raw_nn(tri, mid) + _raw_nn(tri, lo)


@functools.partial(jax.custom_vjp, nondiff_argnums=(1,))
def _cumsum(x, reverse):
    return _cumsum_rows(x, reverse)


def _cumsum_fwd(x, reverse):
    return _cumsum_rows(x, reverse), None


def _cumsum_bwd(reverse, _, g):
    return (_cumsum_rows(g, not reverse),)


_cumsum.defvjp(_cumsum_fwd, _cumsum_bwd)


def _chunks_fn(qs, ks, vs, lfs, sts, reverses):
    n, c = len(qs), qs[0].shape[0]
    every = range(n)
    tris = [_tri_mask(c, r) for r in reverses]
    cums = [_cumsum(lfs[i], reverses[i]) for i in every]
    rid = lax.broadcasted_iota(jnp.int32, cums[0].shape, 0)

    def pick(cum, r):
        return jnp.sum(jnp.where(rid == r, cum, 0.0), axis=0, keepdims=True)

    refs = [pick(cums[i], (c - 1 - c // 2) if reverses[i] else c // 2) for i in every]
    lasts = [pick(cums[i], 0 if reverses[i] else c - 1) for i in every]
    q_in = [qs[i] * jnp.exp(cums[i] - refs[i]) for i in every]
    k_in = [ks[i] * jnp.exp(refs[i] - cums[i]) for i in every]
    scores = [jnp.where(tris[i], _dot_nt(q_in[i], k_in[i]), 0.0) for i in every]
    o_intra = [_dot_nn(scores[i], vs[i]) for i in every]
    q_out = [qs[i] * jnp.exp(cums[i]) for i in every]
    o_inter = [_dot_nt(q_out[i], sts[i]) for i in every]
    k_state = [ks[i] * jnp.exp(lasts[i] - cums[i]) for i in every]
    upd = [_dot_tn(vs[i], k_state[i]) for i in every]
    st_new = [sts[i] * jnp.exp(lasts[i]) + upd[i] for i in every]
    return [o_intra[i] + o_inter[i] for i in every], st_new


def _attn_fwd(name, q, k_f, k_b, v, lf_f, lf_b, n_heads, dk, dv):
    n_rows = q[0].shape[0]
    n_chunks = n_rows // CHUNK
    n_steps = n_chunks // ATTN_SUB
    wk, wv = n_heads * dk, n_heads * dv

    def spec(width, off, rev):
        return pl.BlockSpec((CHUNK * ATTN_SUB, width), lambda n: ((n_steps - 1 - n) if rev else n, off))

    def sspec(rev):
        return pl.BlockSpec((ATTN_SUB, n_heads, dv, dk), lambda n: ((n_steps - 1 - n) if rev else n, 0, 0, 0))

    def body(qf, kf, vf, lff, qb, kb, vb, lfb, of_ref, ob_ref, sf_ref, sb_ref, st):
        @pl.when(pl.program_id(0) == 0)
        def _():
            st[...] = jnp.zeros_like(st)

        ins = ((qf, kf, vf, lff), (qb, kb, vb, lfb))
        chains = [(d, h) for d in range(2) for h in range(n_heads)]
        ck = [slice(h * dk, (h + 1) * dk) for h in range(n_heads)]
        cv = [slice(h * dv, (h + 1) * dv) for h in range(n_heads)]
        sts = [st[d, h] for d, h in chains]
        done = []
        for sub in range(ATTN_SUB):
            local = (sub, ATTN_SUB - 1 - sub)
            rows = [slice(local[d] * CHUNK, (local[d] + 1) * CHUNK) for d in range(2)]
            qs = [ins[d][0][rows[d], ck[h]] for d, h in chains]
            ks = [ins[d][1][rows[d], ck[h]] for d, h in chains]
            vs = [ins[d][2][rows[d], cv[h]] for d, h in chains]
            lfs = [ins[d][3][rows[d], ck[h]] for d, h in chains]
            os_, st_new = _chunks_fn(qs, ks, vs, lfs, sts, [d == 1 for d, _ in chains])
            done.append((local, rows, sts, os_))
            sts = st_new
        for local, rows, entered, os_ in done:
            for i, (d, h) in enumerate(chains):
                (sf_ref, sb_ref)[d][local[d], h] = entered[i].astype(BF16)
                (of_ref, ob_ref)[d][rows[d], cv[h]] = os_[i]
        for i, (d, h) in enumerate(chains):
            st[d, h] = sts[i]

    in_specs = [spec(wk, q[1], False), spec(wk, k_f[1], False), spec(wv, v[1], False), spec(wk, lf_f[1], False),
                spec(wk, q[1], True), spec(wk, k_b[1], True), spec(wv, v[1], True), spec(wk, lf_b[1], True)]
    return pl.pallas_call(
        body, name=name, grid=(n_steps,), in_specs=in_specs,
        out_specs=[spec(wv, 0, False), spec(wv, 0, True), sspec(False), sspec(True)],
        out_shape=[jax.ShapeDtypeStruct((n_rows, wv), F32)] * 2
        + [jax.ShapeDtypeStruct((n_chunks, n_heads, dv, dk), BF16)] * 2,
        scratch_shapes=[pltpu.VMEM((2, n_heads, dv, dk), F32)],
        compiler_params=_params(("arbitrary",)),
    )(q[0], k_f[0], v[0], lf_f[0], q[0], k_b[0], v[0], lf_b[0])


def _attn_bwd(name, q, k_f, k_b, v, lf_f, lf_b, st_f, st_b, do, n_heads, dk, dv, out_dtype=F32):
    n_rows = q[0].shape[0]
    n_chunks = n_rows // CHUNK
    n_steps = n_chunks // ATTN_SUB
    wk, wv = n_heads * dk, n_heads * dv

    def spec(width, off, rev):
        return pl.BlockSpec((CHUNK * ATTN_SUB, width), lambda n: (n if rev else (n_steps - 1 - n), off))

    def sspec(rev):
        return pl.BlockSpec((ATTN_SUB, n_heads, dv, dk), lambda n: (n if rev else (n_steps - 1 - n), 0, 0, 0))

    def body(qf, kf, vf, lff, sf, dof, qb, kb, vb, lfb, sb, dob,
             dqf, dkf, dvf, dlff, dqb, dkb, dvb, dlfb, dst):
        @pl.when(pl.program_id(0) == 0)
        def _():
            dst[...] = jnp.zeros_like(dst)

        ins = ((qf, kf, vf, lff, sf, dof), (qb, kb, vb, lfb, sb, dob))
        outs = ((dqf, dkf, dvf, dlff), (dqb, dkb, dvb, dlfb))
        chains = [(d, h) for d in range(2) for h in range(n_heads)]
        ck = [slice(h * dk, (h + 1) * dk) for h in range(n_heads)]
        cv = [slice(h * dv, (h + 1) * dv) for h in range(n_heads)]
        fn = functools.partial(_chunks_fn, reverses=[d == 1 for d, _ in chains])
        dsts = [dst[d, h] for d, h in chains]
        done = []
        for sub in range(ATTN_SUB):
            local = (ATTN_SUB - 1 - sub, sub)
            rows = [slice(local[d] * CHUNK, (local[d] + 1) * CHUNK) for d in range(2)]
            qs = [ins[d][0][rows[d], ck[h]] for d, h in chains]
            ks = [ins[d][1][rows[d], ck[h]] for d, h in chains]
            vs = [ins[d][2][rows[d], cv[h]] for d, h in chains]
            lfs = [ins[d][3][rows[d], ck[h]] for d, h in chains]
            sts = [ins[d][4][local[d], h].astype(F32) for d, h in chains]
            dos = [ins[d][5][rows[d], cv[h]] for d, h in chains]
            _, vjp = jax.vjp(fn, qs, ks, vs, lfs, sts)
            dqs, dks, dvs, dlfs, dsts = vjp((dos, dsts))
            done.append((rows, dqs, dks, dvs, dlfs))
        for rows, dqs, dks, dvs, dlfs in done:
            for i, (d, h) in enumerate(chains):
                dq_r, dk_r, dv_r, dlf_r = outs[d]
                dq_r[rows[d], ck[h]] = dqs[i].astype(dq_r.dtype)
                dk_r[rows[d], ck[h]] = dks[i].astype(dk_r.dtype)
                dv_r[rows[d], cv[h]] = dvs[i].astype(dv_r.dtype)
                dlf_r[rows[d], ck[h]] = dlfs[i].astype(dlf_r.dtype)
        for i, (d, h) in enumerate(chains):
            dst[d, h] = dsts[i]

    def dir_specs(kk, lf, rev):
        return [spec(wk, q[1], rev), spec(wk, kk[1], rev), spec(wv, v[1], rev), spec(wk, lf[1], rev), sspec(rev),
                spec(wv, 0, rev)]

    def dir_out_specs(rev):
        return [spec(wk, 0, rev), spec(wk, 0, rev), spec(wv, 0, rev), spec(wk, 0, rev)]

    shapes = [jax.ShapeDtypeStruct((n_rows, wk), out_dtype), jax.ShapeDtypeStruct((n_rows, wk), out_dtype),
              jax.ShapeDtypeStruct((n_rows, wv), out_dtype), jax.ShapeDtypeStruct((n_rows, wk), F32)]
    outs = pl.pallas_call(
        body, name=name, grid=(n_steps,), in_specs=dir_specs(k_f, lf_f, False) + dir_specs(k_b, lf_b, True),
        out_specs=dir_out_specs(False) + dir_out_specs(True), out_shape=shapes + shapes,
        scratch_shapes=[pltpu.VMEM((2, n_heads, dv, dk), F32)],
        compiler_params=_params(("arbitrary",)),
    )(q[0], k_f[0], v[0], lf_f[0], st_f, do, q[0], k_b[0], v[0], lf_b[0], st_b, do)
    return outs[:4], outs[4:]


def _row2(v):
    return v.reshape(1, -1)


def _mlp_fwd(tag, h, gain, w1, w2):
    y = _rowcall(f"{tag}_norm", _rmsnorm_fn, [(h, h.shape[1], 0)], [gain], [(h.shape[1], BF16)], tm=512)[0]
    hid = _mm(f"{tag}_up", y, w1, out_dtypes=(BF16,))
    h_out = _mm(f"{tag}_down", hid, w2, a_pro=_relu2, extras=(h,), epi=_add_epi)
    return h_out, (y, hid)


def _dw(name, a, b, **kw):
    return _mm(name, a, b, mode="tn", epi=lambda acc: (acc, acc), out_dtypes=(F32, BF16), **kw)


def _mlp_bwd(tag, h, gain, w1, w2, saved, dh_out):
    y, hid = saved
    dhid = _mm(f"{tag}_dact", dh_out, w2, mode="nt", extras=(hid,), epi=_relu2_bwd_epi, out_dtypes=(BF16,))
    dw2 = _dw(f"{tag}_dw2", hid, dh_out, a_pro=_relu2)
    dw1 = _dw(f"{tag}_dw1", y, dhid, out_split=N_CHIPS)
    dy = _mm(f"{tag}_dy", dhid, w1, mode="nt")
    dh, dgain = _norm_bwd(f"{tag}_dnorm", h, gain, dy, dh_out)
    return dh, dgain, dw1, dw2


def _norm_bwd(name, h, gain, dy, dres, pin=None):
    d = h.shape[1]

    def fn(h, dy, dres, gain):
        _, vjp = jax.vjp(lambda a, b: _rmsnorm_fn(a, b)[0], h, gain)
        dh, dgain = vjp(dy)
        return dh + dres, dgain

    dh, dgain = _rowcall(name, fn, [(h, d, 0), (dy, d, 0), (dres, d, 0)], [gain], [(d, F32)], [(1, d)], tm=512, pin=pin)
    return dh, dgain


def _local_step(x, target, w, pin=None, late=None, emit=None):
    g = {}
    d_model = x.shape[1]
    rg_w = hg_w = d_model // 2
    pins = []

    def send_off(tag, pairs):
        if emit is not None:
            pins.append(emit(tag, [p[0] for p in pairs], [p[1] for p in pairs]))

    def both(fn, pair):
        return [fn(t) for t in pair]

    def chip_major(t):
        return t.reshape(N_CHIPS, t.shape[0] // N_CHIPS, t.shape[1])

    h_a0 = x
    gain = _row2(w["norm_mix"][0])
    y0 = _rowcall("l0_norm", _rmsnorm_fn, [(h_a0, d_model, 0)], [gain], [(d_model, BF16)], tm=512, pin=pin)[0]
    proj0 = _mm("l0_in", y0, w["ab_w_in"])
    conv_w, conv_b = w["rg_conv_w"], _row2(w["rg_conv_b"])
    xc = _conv_fwd("rg_conv", proj0, 0, conv_w, conv_b)
    gate_pars = [w["rg_wa_bd"], w["rg_wx_bd"], w["rg_b_a"], w["rg_b_x"], w["rg_lambda"]]
    a_f, u_f, a_b, u_b = _rowcall("rg_gates", _rg_gates_fn, [(xc, rg_w, 0)], gate_pars, [(rg_w, F32)] * 4)
    hs_f = _scan_fwd("rg_scan_f", a_f, u_f, False)
    hs_b = _scan_fwd("rg_scan_b", a_b, u_b, True)
    hg_rows = [(proj0, hg_w, 2), (proj0, hg_w, 3), (proj0, hg_w, 4)]
    qh, k_f, lf_f, k_b, lf_b = _rowcall("hg_pre", _hg_pre_fn, hg_rows, [w["hg_lb_logits"]], [(hg_w, F32)] * 5)
    iv = (proj0, 5)
    o_f, o_b, st_f, st_b = _attn_fwd("hg_attn", (qh, 0), (k_f, 0), (k_b, 0), iv, (lf_f, 0), (lf_b, 0), 4, 128, 128)
    post0_rows = [(hs_f, rg_w, 0), (hs_b, rg_w, 0), (proj0, rg_w, 1), (o_f, hg_w, 0), (o_b, hg_w, 0), (proj0, hg_w, 6)]
    hg_gain = _row2(w["hg_norm"])
    mix_in0 = _rowcall("l0_post", _post0_fwd_fn, post0_rows, [hg_gain], [(d_model, BF16)])[0]
    if late is not None:
        w = {**w, **late(mix_in0)}
    h_b0 = _mm("l0_out", mix_in0, w["ab_w_out"], extras=(h_a0,), epi=_add_epi)
    h_c0, mlp0 = _mlp_fwd("mlp0", h_b0, _row2(w["norm_mlp"][0]), w["mlp_w1"][0], w["mlp_w2"][0])

    h_a1 = h_c0
    gain1 = _row2(w["norm_mix"][1])
    y1 = _rowcall("l1_norm", _rmsnorm_fn, [(h_a1, d_model, 0)], [gain1], [(d_model, BF16)], tm=512)[0]
    proj1 = _mm("l1_in", y1, w["gla_w_in_pad"], tn=640)
    gla_pars = [w["gla_w_up_pad"], w["gla_b_gate"]]
    gq, glf_f, glf_b = _rowcall("gla_pre", _gla_pre_fn, [(proj1, 512, 0), (proj1, LANES, 24)], gla_pars, [(512, F32)] * 3)
    gk, gv = (proj1, 1), (proj1, 1)
    go_f, go_b, gst_f, gst_b = _attn_fwd("gla_attn", (gq, 0), gk, gk, gv, (glf_f, 0), (glf_b, 0), 4, 128, 256)
    gla_gain = _row2(w["gla_norm"])
    post1_rows = [(go_f, d_model, 0), (go_b, d_model, 0), (proj1, d_model, 2)]
    mix_in1 = _rowcall("l1_post", _gla_post_fwd_fn, post1_rows, [gla_gain], [(d_model, BF16)])[0]
    h_b1 = _mm("l1_out", mix_in1, w["gla_w_out"], extras=(h_a1,), epi=_add_epi)
    h_c1, mlp1 = _mlp_fwd("mlp1", h_b1, _row2(w["norm_mlp"][1]), w["mlp_w1"][1], w["mlp_w2"][1])

    dh, loss, g["norm_final"] = _rowcall(
        "loss_head", _loss_head_fn, [(h_c1, d_model, 0), (target, d_model, 0)], [_row2(w["norm_final"])],
        [(d_model, F32)], [(1, LANES), (1, d_model)], tm=512)

    dh, g_nmlp1, g_w1_1, g_w2_1 = _mlp_bwd("mlp1", h_b1, _row2(w["norm_mlp"][1]), w["mlp_w1"][1], w["mlp_w2"][1], mlp1, dh)
    send_off("mlp1", [g_w1_1, both(chip_major, g_w2_1)])
    dmix1 = _mm("l1_dout", dh, w["gla_w_out"], mode="nt")
    g_gla_out = _dw("l1_dwout", mix_in1, dh)
    g["gla_w_out"] = g_gla_out[0]
    dgo, dr, g["gla_norm"] = _rowcall(
        "l1_dpost", _gla_post_bwd_fn, post1_rows + [(dmix1, d_model, 0)], [gla_gain],
        [(d_model, F32), (d_model, BF16)], [(1, d_model)], pin=pins.pop() if pins else None)
    (dq_f, dk_f, dv_f, dlf_f), (dq_b, dk_b, dv_b, dlf_b) = _attn_bwd(
        "gla_dattn", (gq, 0), gk, gk, gv, (glf_f, 0), (glf_b, 0), gst_f, gst_b, dgo, 4, 128, 256)

    def gla_pre_bwd(q, lr, dq1, dq2, dlf1, dlf2, dk1, dk2, dv1, dv2, w_up, b_gate):
        dlr = jnp.zeros_like(lr)
        dws, dbs = [], []
        for d, dlf in enumerate((dlf1, dlf2)):
            z = _raw_nn(lr, w_up[d]) + b_gate[d:d + 1]
            dz = dlf * _sigmoid(-z) * (1.0 / 16.0)
            dlr = dlr + _raw_nt(dz, w_up[d])
            dws.append(_raw_tn(dz, lr))
            dbs.append(jnp.sum(dz, axis=0, keepdims=True))
        return ((dq1 + dq2) * (128.0 ** -0.5), dk1 + dk2, dv1 + dv2, dlr, dws[0], dws[1], dbs[0], dbs[1])

    rows = [(proj1, 512, 0), (proj1, LANES, 24), (dq_f, 512, 0), (dq_b, 512, 0), (dlf_f, 512, 0), (dlf_b, 512, 0),
            (dk_f, 512, 0), (dk_b, 512, 0), (dv_f, d_model, 0), (dv_b, d_model, 0)]
    dq, dk, dv, dlr, dwt_f, dwt_b, db_f, db_b = _rowcall(
        "gla_dpre", gla_pre_bwd, rows, gla_pars, [(512, BF16), (512, BF16), (d_model, BF16), (LANES, BF16)],
        [(512, LANES), (512, LANES), (1, 512), (1, 512)])
    g["gla_w_up_pad"] = jnp.stack([dwt_f.T, dwt_b.T])
    g["gla_b_gate"] = jnp.concatenate([db_f, db_b], axis=0)
    dproj1 = jnp.concatenate([dq, dk, dv, dr, dlr], axis=1)
    g_gla_in = both(lambda t: _split_chips(t[:, :GLA_IN_WIDTH], 1), _dw("l1_dwin", y1, dproj1, tn=640))
    g["gla_w_in"] = g_gla_in[0]
    send_off("gla", [g_gla_in, both(chip_major, g_gla_out)])
    dy1 = _mm("l1_dy", dproj1, w["gla_w_in_pad"], mode="nt", tk=640)
    dh, g_nmix1 = _norm_bwd("l1_dnorm", h_a1, gain1, dy1, dh, pin=pins.pop() if pins else None)

    dh, g_nmlp0, g_w1_0, g_w2_0 = _mlp_bwd("mlp0", h_b0, _row2(w["norm_mlp"][0]), w["mlp_w1"][0], w["mlp_w2"][0], mlp0, dh)
    g_ab_out = _dw("l0_dwout", mix_in0, dh)
    g["ab_w_out"] = g_ab_out[0]
    send_off("mlp0", [g_w1_0, both(chip_major, g_w2_0), both(chip_major, g_ab_out)])
    dmix0 = _mm("l0_dout", dh, w["ab_w_out"], mode="nt")
    dhs, dga, do, dg, g["hg_norm"] = _rowcall(
        "l0_dpost", _post0_bwd_fn, post0_rows + [(dmix0, d_model, 0)], [hg_gain],
        [(rg_w, F32), (rg_w, BF16), (hg_w, F32), (hg_w, BF16)], [(1, hg_w)], pin=pins.pop() if pins else None)
    (dqh_f, dk_f, div_f, dlf_f), (dqh_b, dk_b, div_b, dlf_b) = _attn_bwd(
        "hg_dattn", (qh, 0), (k_f, 0), (k_b, 0), iv, (lf_f, 0), (lf_b, 0), st_f, st_b, do, 4, 128, 128)

    def hg_pre_bwd(q, f_f, f_b, dq1, dq2, dk1, dlf1, dk2, dlf2, dv1, dv2, logits):
        _, vjp = jax.vjp(_hg_pre_fn, q, f_f, f_b, logits)
        dq, df_f, df_b, dlogits = vjp((dq1 + dq2, dk1, dlf1, dk2, dlf2))
        return dq, df_f, df_b, dv1 + dv2, dlogits

    rows = hg_rows + [(t, hg_w, 0) for t in (dqh_f, dqh_b, dk_f, dlf_f, dk_b, dlf_b, div_f, div_b)]
    dq, df_f, df_b, div, g["hg_lb_logits"] = _rowcall(
        "hg_dpre", hg_pre_bwd, rows, [w["hg_lb_logits"]], [(hg_w, BF16)] * 4, [(2, hg_w)])
    du_f, da_f = _scan_bwd("rg_dscan_f", a_f, hs_f, dhs, False)
    du_b, da_b = _scan_bwd("rg_dscan_b", a_b, hs_b, dhs, True)
    gates_bwd = _vjp_of(_rg_gates_fn, 1, 4, 5)
    rows = [(xc, rg_w, 0), (da_f, rg_w, 0), (du_f, rg_w, 0), (da_b, rg_w, 0), (du_b, rg_w, 0)]
    dxc, g["rg_wa_bd"], g["rg_wx_bd"], g["rg_b_a"], g["rg_b_x"], g["rg_lambda"] = _rowcall(
        "rg_dgates", gates_bwd, rows, gate_pars, [(rg_w, F32)],
        [(2, rg_w, rg_w), (2, rg_w, rg_w), (2, rg_w), (2, rg_w), (2, rg_w)])
    dxa, g["rg_conv_w"], g["rg_conv_b"] = _conv_bwd("rg_dconv", proj0, 0, conv_w, dxc)
    dproj0 = jnp.concatenate([dxa, dga, dq, df_f, df_b, div, dg], axis=1)
    g_ab_in = _dw("l0_dwin", y0, dproj0, out_split=N_CHIPS)
    g["ab_w_in"] = g_ab_in[0]
    send_off("ab", [g_ab_in])
    dy0 = _mm("l0_dy", dproj0, w["ab_w_in"], mode="nt")
    grad_x, g_nmix0 = _norm_bwd("l0_dnorm", h_a0, gain, dy0, dh, pin=pins.pop() if pins else None)

    g["norm_mix"] = jnp.concatenate([g_nmix0, g_nmix1], axis=0)
    g["norm_mlp"] = jnp.concatenate([g_nmlp0, g_nmlp1], axis=0)
    g["mlp_w1"] = [g_w1_0[0], g_w1_1[0]]
    g["mlp_w2"] = [g_w2_0[0], g_w2_1[0]]
    return loss, grad_x, g


def _block_diag(w):
    d, g, n, _ = w.shape
    eye = jnp.eye(g, dtype=w.dtype)
    return (w[:, :, :, None, :] * eye[None, :, None, :, None]).reshape(d, g * n, g * n)


def _block_diag_extract(wbd, g):
    d, gn, _ = wbd.shape
    n = gn // g
    blocks = wbd.reshape(d, g, n, g, n)
    return jnp.stack([blocks[:, i, :, i, :] for i in range(g)], axis=1)


def _prepare_weights(big, full):
    w = {k: full[k] for k in ("norm_mix", "norm_mlp", "norm_final", "hg_lb_logits")}
    for k in ("rg_conv_w", "rg_conv_b", "rg_b_a", "rg_b_x", "rg_lambda", "hg_norm", "gla_b_gate", "gla_norm"):
        w[k] = full[k][0]
    w["rg_wa_bd"] = _block_diag(full["rg_w_a"][0])
    w["rg_wx_bd"] = _block_diag(full["rg_w_x"][0])
    up = full["gla_w_gate_up"][0]
    rank = up.shape[1]
    pad = jnp.zeros((2, LANES, up.shape[2]), F32)
    w["gla_w_up_pad"] = pad.at[0, 0:rank].set(up[0]).at[1, rank:2 * rank].set(up[1])
    w.update(_prepare_matrices(big))
    return w


def _prepare_matrices(big):
    w = {}
    if "mlp_w1" in big:
        w["mlp_w1"] = list(big["mlp_w1"])
        w["mlp_w2"] = [t.reshape(-1, t.shape[-1]) for t in big["mlp_w2"]]
    if "ab_w_in" in big:
        w["ab_w_in"] = big["ab_w_in"]
    if "ab_w_out" in big:
        w["ab_w_out"] = big["ab_w_out"].reshape(-1, big["ab_w_out"].shape[-1])
    if "gla_w_in" in big:
        w["gla_w_out"] = big["gla_w_out"].reshape(-1, big["gla_w_out"].shape[-1])
        gla_in = _join_chips(big["gla_w_in"], 1)
        w["gla_w_in_pad"] = jnp.pad(gla_in, ((0, 0), (0, GLA_IN_PAD - gla_in.shape[1])))
    return w


def _finish_grads(g, rank=16, rg_blocks=8):
    def chip_major(t):
        return t.reshape(N_CHIPS, t.shape[0] // N_CHIPS, t.shape[1])

    big = {
        "mlp_w1": list(g["mlp_w1"]), "mlp_w2": [chip_major(t) for t in g["mlp_w2"]],
        "ab_w_in": g["ab_w_in"], "ab_w_out": chip_major(g["ab_w_out"]),
        "gla_w_in": g["gla_w_in"], "gla_w_out": chip_major(g["gla_w_out"]),
    }
    small = {
        "norm_mix": g["norm_mix"], "norm_mlp": g["norm_mlp"], "norm_final": g["norm_final"][0],
        "rg_conv_w": g["rg_conv_w"][None], "rg_conv_b": g["rg_conv_b"],
        "rg_w_a": _block_diag_extract(g["rg_wa_bd"], rg_blocks)[None], "rg_b_a": g["rg_b_a"][None],
        "rg_w_x": _block_diag_extract(g["rg_wx_bd"], rg_blocks)[None], "rg_b_x": g["rg_b_x"][None],
        "rg_lambda": g["rg_lambda"][None], "hg_lb_logits": g["hg_lb_logits"], "hg_norm": g["hg_norm"],
        "gla_w_gate_up": jnp.stack([g["gla_w_up_pad"][0, 0:rank], g["gla_w_up_pad"][1, rank:2 * rank]])[None],
        "gla_b_gate": g["gla_b_gate"][None], "gla_norm": g["gla_norm"],
    }
    return big, small


MATRICES = (("mlp_w1", 0), ("mlp_w1", 1), ("mlp_w2", 0), ("mlp_w2", 1), ("ab_w_in", 0), ("ab_w_out", 0),
            ("gla_w_in", 0), ("gla_w_out", 0))
EARLY_MATRICES = ("ab_w_in",)
SMALL_SHARDED = ("rg_conv_w", "rg_b_a", "rg_b_x", "rg_lambda", "gla_w_gate_up", "gla_b_gate", "gla_norm")
SMALL_REPLICATED = ("norm_mix", "norm_mlp", "norm_final", "rg_conv_b", "rg_w_a", "rg_w_x", "hg_lb_logits", "hg_norm")
WEIGHTS = ("norm_mix", "norm_mlp", "norm_final", "mlp_w1", "mlp_w2", "ab_w_in", "ab_w_out", "rg_conv_w", "rg_conv_b",
           "rg_w_a", "rg_b_a", "rg_w_x", "rg_b_x", "rg_lambda", "hg_lb_logits", "hg_norm", "gla_w_in", "gla_w_out",
           "gla_w_gate_up", "gla_b_gate", "gla_norm")
ROW_ALIGN = 16


def _pack(arrays, lead=0):
    head = arrays[0].shape[:lead]
    flat = jnp.concatenate([a.reshape(head + (-1,)) for a in arrays], axis=lead)
    n = flat.shape[-1]
    quantum = LANES * ROW_ALIGN
    padded = -(-n // quantum) * quantum
    if padded != n:
        flat = jnp.pad(flat, [(0, 0)] * lead + [(0, padded - n)])
    return flat.reshape(head + (padded // LANES, LANES))


def _unpack(buf, shapes, lead=0):
    head = buf.shape[:lead]
    flat = buf.reshape(head + (-1,))
    out, off = [], 0
    for s in shapes:
        n = 1
        for v in s:
            n *= v
        out.append(lax.slice_in_dim(flat, off, off + n, axis=lead).reshape(head + tuple(s)))
        off += n
    return out


def _join_chips(gathered, axis):
    t = jnp.moveaxis(gathered, 0, axis)
    return t.reshape(t.shape[:axis] + (t.shape[axis] * t.shape[axis + 1],) + t.shape[axis + 2:])


def _split_chips(full, axis):
    s = full.shape
    t = full.reshape(s[:axis] + (N_CHIPS, s[axis] // N_CHIPS) + s[axis + 1:])
    return jnp.moveaxis(t, axis, 0)


_ANY = pl.BlockSpec(memory_space=pl.ANY)


def _place():
    return lax.axis_index("x"), lax.axis_index("y"), lax.axis_index("c")


def _into_slot(name, src, slot, n_slots, dtype, tm, layer=None):
    r, lanes = src.shape[-2:]
    tm = _row_tile(r, tm, ROW_ALIGN)

    def body(slot_ref, in_ref, o_ref):
        o_ref[...] = in_ref[...].astype(o_ref.dtype)

    if layer is None:
        in_spec = pl.BlockSpec((tm, lanes), lambda i, slot_ref: (i, 0))
    else:
        in_spec = pl.BlockSpec((None, tm, lanes), lambda i, slot_ref: (layer, i, 0))
    grid_spec = pltpu.PrefetchScalarGridSpec(
        num_scalar_prefetch=1, grid=(r // tm,), in_specs=[in_spec],
        out_specs=pl.BlockSpec((None, tm, lanes), lambda i, slot_ref: (slot_ref[0], i, 0)))
    return pl.pallas_call(
        body, name=name, grid_spec=grid_spec, out_shape=jax.ShapeDtypeStruct((n_slots, r, lanes), dtype),
        compiler_params=_params(("parallel",)),
    )(slot.reshape(1).astype(jnp.int32), src)


def _chip_peers():
    x, y, c = _place()
    return 2 * x + y, c, [(1 - x, y), (x, 1 - y), (1 - x, 1 - y)]


def _comm_call(name, body, ins, out_shapes, n_sems, aliases=None):
    return pl.pallas_call(
        body, name=name, in_specs=[_ANY] * len(ins), out_specs=[_ANY] * len(out_shapes), out_shape=out_shapes,
        input_output_aliases=aliases or {},
        scratch_shapes=[pltpu.SemaphoreType.DMA((n_sems,)), pltpu.SemaphoreType.DMA((n_sems,))],
    )(*ins)


def _gather_chips(name, bufs):
    n = len(bufs)

    def body(*refs):
        outs, send_sems, recv_sems = refs[n:2 * n], refs[2 * n], refs[2 * n + 1]
        x, y, c = _place()
        me, _, peers = _chip_peers()

        def rows(a, block, half):
            rh = outs[a].shape[1] // 2
            return outs[a].at[block, pl.ds(half * rh, rh)]

        def copy(a, j, block, half, to, sem):
            return pltpu.make_async_remote_copy(
                src_ref=rows(a, block, half), dst_ref=rows(a, block, half), send_sem=send_sems.at[sem],
                recv_sem=recv_sems.at[sem], device_id=to, device_id_type=MESH)

        def over_ici(a, j, block):
            px, py = peers[j]
            return copy(a, j, block, c, (px, py, c), 6 * a + j)

        def to_sibling(a, j, block, half):
            return copy(a, j, block, half, (x, y, 1 - c), 6 * a + 3 + j)

        sends = [over_ici(a, j, me) for a in range(n) for j in range(3)]
        for cp in sends:
            cp.start()
        for a in range(n):
            for j, (px, py) in enumerate(peers):
                over_ici(a, j, 2 * px + py).wait_recv()
                handed = to_sibling(a, j, 2 * px + py, c)
                handed.start()
                sends.append(handed)
        for a in range(n):
            for j, (px, py) in enumerate(peers):
                to_sibling(a, j, 2 * px + py, 1 - c).wait_recv()
        for cp in sends:
            cp.wait_send()

    shapes = [jax.ShapeDtypeStruct(b.shape, b.dtype) for b in bufs]
    return _comm_call(name, body, bufs, shapes, 6 * n, {a: a for a in range(n)})


_HBM = pl.BlockSpec(memory_space=pltpu.HBM)
_SEM = pl.BlockSpec(memory_space=pltpu.SEMAPHORE)
_EFFECT = pltpu.SideEffectType.DATAFLOW_SIDE_EFFECTING


def _half_rows(ref, block, half):
    rh = ref.shape[1] // 2
    return ref.at[block, pl.ds(half * rh, rh)]


def _gather_start(name, bufs, after):
    n = len(bufs)

    def body(*refs):
        ins, send_sems, recv_sems, token = refs[:n], refs[n + 1], refs[n + 2], refs[-1]
        me, c, peers = _chip_peers()
        for a in range(n):
            mine = _half_rows(ins[a], me, c)
            for j, (px, py) in enumerate(peers):
                pltpu.make_async_remote_copy(
                    src_ref=mine, dst_ref=mine, send_sem=send_sems.at[3 * a + j], recv_sem=recv_sems.at[3 * a + j],
                    device_id=(px, py, c), device_id_type=MESH).start()
        token[...] = jnp.zeros_like(token)

    out_shape = (pltpu.SemaphoreType.DMA((3 * n,)), pltpu.SemaphoreType.DMA((3 * n,)),
                 *[pltpu.HBM(b.shape, b.dtype) for b in bufs], jax.ShapeDtypeStruct((8, LANES), F32))
    return pl.pallas_call(
        body, name=name, out_shape=out_shape, in_specs=[_HBM] * n + [_ANY],
        out_specs=(_SEM, _SEM, *[_HBM] * n, pl.BlockSpec(memory_space=pltpu.VMEM)),
        input_output_aliases={a: 2 + a for a in range(n)},
        compiler_params=pltpu.CompilerParams(has_side_effects=_EFFECT),
    )(*[pltpu.with_memory_space_constraint(b, pltpu.HBM) for b in bufs], after)


def _gather_wait(name, bufs, send_sems, recv_sems, after):
    n = len(bufs)

    def body(*refs):
        ins, send_sems, recv_sems = refs[:n], refs[n], refs[n + 1]
        me, c, peers = _chip_peers()
        for a in range(n):
            for j, (px, py) in enumerate(peers):
                copy = pltpu.make_async_remote_copy(
                    src_ref=_half_rows(ins[a], me, c), dst_ref=_half_rows(ins[a], 2 * px + py, c),
                    send_sem=send_sems.at[3 * a + j], recv_sem=recv_sems.at[3 * a + j],
                    device_id=(px, py, c), device_id_type=MESH)
                copy.wait_send()
                copy.wait_recv()

    return pl.pallas_call(
        body, name=name, out_shape=tuple(pltpu.HBM(b.shape, b.dtype) for b in bufs),
        in_specs=[_HBM] * n + [_SEM, _SEM, _ANY], out_specs=tuple([_HBM] * n),
        input_output_aliases={a: a for a in range(n)},
        compiler_params=pltpu.CompilerParams(has_side_effects=_EFFECT),
    )(*bufs, send_sems, recv_sems, after)


def _hand_over(name, bufs):
    n = len(bufs)

    def body(*refs):
        outs, send_sems, recv_sems = refs[n:2 * n], refs[2 * n], refs[2 * n + 1]
        x, y, c = _place()
        _, _, peers = _chip_peers()

        def copy(a, j, half):
            px, py = peers[j]
            rows = _half_rows(outs[a], 2 * px + py, half)
            return pltpu.make_async_remote_copy(
                src_ref=rows, dst_ref=rows, send_sem=send_sems.at[3 * a + j], recv_sem=recv_sems.at[3 * a + j],
                device_id=(x, y, 1 - c), device_id_type=MESH)

        sends = [copy(a, j, c) for a in range(n) for j in range(3)]
        for cp in sends:
            cp.start()
        for a in range(n):
            for j in range(3):
                copy(a, j, 1 - c).wait_recv()
        for cp in sends:
            cp.wait_send()

    shapes = [jax.ShapeDtypeStruct(b.shape, b.dtype) for b in bufs]
    return _comm_call(name, body, bufs, shapes, 3 * n, {a: a for a in range(n)})


def _pair_gather(name, bufs):
    n = len(bufs)

    def body(*refs):
        ins, outs, send_sems, recv_sems = refs[:n], refs[n:2 * n], refs[2 * n], refs[2 * n + 1]
        x, y, c = _place()

        def copy(a, block):
            return pltpu.make_async_remote_copy(
                src_ref=ins[a].at[block], dst_ref=outs[a].at[block], send_sem=send_sems.at[a],
                recv_sem=recv_sems.at[a], device_id=(x, y, 1 - c), device_id_type=MESH)

        sends = [copy(a, c) for a in range(n)]
        for cp in sends:
            cp.start()
        for a in range(n):
            copy(a, 1 - c).wait_recv()
        for cp in sends:
            cp.wait_send()

    shapes = [jax.ShapeDtypeStruct(b.shape, b.dtype) for b in bufs]
    return _comm_call(name, body, bufs, shapes, n, {a: a for a in range(n)})


def _all_peers():
    x, y, c = _place()
    peers = []
    for mask in range(1, N_DEV):
        fx, fy, fc = (mask >> 2) & 1, (mask >> 1) & 1, mask & 1
        peers.append((jnp.where(fx, 1 - x, x), jnp.where(fy, 1 - y, y), jnp.where(fc, 1 - c, c)))
    return 4 * x + 2 * y + c, peers


def _reduce_copies(srcs, lands, send_sems, recv_sems):
    me, peers = _all_peers()
    sends, arrivals = [], []
    for a in range(len(srcs)):
        for j, (px, py, pc) in enumerate(peers):
            k = (N_DEV - 1) * a + j
            sends.append(pltpu.make_async_remote_copy(
                src_ref=srcs[a].at[2 * px + py, pc], dst_ref=lands[a].at[me], send_sem=send_sems.at[k],
                recv_sem=recv_sems.at[k], device_id=(px, py, pc), device_id_type=MESH))
            arrivals.append(pltpu.make_async_remote_copy(
                src_ref=srcs[a].at[2 * px + py, pc], dst_ref=lands[a].at[4 * px + 2 * py + pc],
                send_sem=send_sems.at[k], recv_sem=recv_sems.at[k], device_id=(px, py, pc), device_id_type=MESH))
    return sends, arrivals


def _reduce_direct(name, srcs, pin=None):
    n = len(srcs)
    extra = [] if pin is None else [pin]

    def body(*refs):
        ins, outs = refs[:n], refs[n + len(extra):2 * n + len(extra)]
        sends, arrivals = _reduce_copies(ins, outs, refs[-2], refs[-1])
        for cp in sends:
            cp.start()
        for cp in arrivals:
            cp.wait_recv()
        for cp in sends:
            cp.wait_send()

    shapes = [jax.ShapeDtypeStruct((N_DEV,) + s.shape[2:], s.dtype) for s in srcs]
    return _comm_call(name, body, list(srcs) + extra, shapes, (N_DEV - 1) * n)


def _reduce_start(name, srcs):
    n = len(srcs)
    lands = [lax.empty((N_DEV,) + s.shape[2:], s.dtype) for s in srcs]

    def body(*refs):
        sends, _ = _reduce_copies(refs[:n], refs[n:2 * n], refs[2 * n], refs[2 * n + 1])
        for cp in sends:
            cp.start()
        refs[-1][...] = jnp.zeros_like(refs[-1])

    bufs = list(srcs) + lands
    n_sems = (N_DEV - 1) * n
    out_shape = (pltpu.SemaphoreType.DMA((n_sems,)), pltpu.SemaphoreType.DMA((n_sems,)),
                 *[pltpu.HBM(b.shape, b.dtype) for b in bufs], jax.ShapeDtypeStruct((8, LANES), F32))
    return pl.pallas_call(
        body, name=name, out_shape=out_shape, in_specs=[_HBM] * (2 * n),
        out_specs=(_SEM, _SEM, *[_HBM] * (2 * n), pl.BlockSpec(memory_space=pltpu.VMEM)),
        input_output_aliases={a: 2 + a for a in range(2 * n)},
        compiler_params=pltpu.CompilerParams(has_side_effects=_EFFECT),
    )(*[pltpu.with_memory_space_constraint(b, pltpu.HBM) for b in bufs])


def _reduce_wait(name, srcs, lands, send_sems, recv_sems, after):
    n = len(srcs)

    def body(*refs):
        sends, arrivals = _reduce_copies(refs[:n], refs[n:2 * n], refs[2 * n], refs[2 * n + 1])
        for cp in sends:
            cp.wait_send()
        for cp in arrivals:
            cp.wait_recv()

    bufs = list(srcs) + list(lands)
    outs = pl.pallas_call(
        body, name=name, out_shape=tuple(pltpu.HBM(b.shape, b.dtype) for b in bufs),
        in_specs=[_HBM] * (2 * n) + [_SEM, _SEM, _ANY], out_specs=tuple([_HBM] * (2 * n)),
        input_output_aliases={a: a for a in range(2 * n)},
        compiler_params=pltpu.CompilerParams(has_side_effects=_EFFECT),
    )(*bufs, send_sems, recv_sems, after)
    return list(outs[n:])


def _reduce_sum(name, own, land, chip, core):
    n, rh, lanes = land.shape
    tm = _row_tile(rh, 1024, ROW_ALIGN)

    def body(idx_ref, own_ref, *rest):
        total = own_ref[...]
        for g_ref in rest[:-1]:
            total = total + g_ref[...].astype(F32)
        rest[-1][...] = total

    def block(k):
        return pl.BlockSpec((None, tm, lanes), lambda i, idx_ref: ((2 * idx_ref[0] + idx_ref[1] + k) % n, i, 0))

    grid_spec = pltpu.PrefetchScalarGridSpec(
        num_scalar_prefetch=1, grid=(rh // tm,),
        in_specs=[pl.BlockSpec((None, None, tm, lanes), lambda i, idx_ref: (idx_ref[0], idx_ref[1], i, 0))]
        + [block(k) for k in range(1, n)],
        out_specs=pl.BlockSpec((None, tm, lanes), lambda i, idx_ref: (idx_ref[1], i, 0)))
    return pl.pallas_call(
        body, name=name, grid_spec=grid_spec, out_shape=jax.ShapeDtypeStruct((2, rh, lanes), F32),
        compiler_params=_params(("parallel",)),
    )(jnp.stack([chip, core]).astype(jnp.int32), own, *[land] * (n - 1))


def _gather_all_start(name, buf):
    def body(in_ref, send_sems, recv_sems, out_ref, token):
        me, peers = _all_peers()
        for j, peer in enumerate(peers):
            pltpu.make_async_remote_copy(
                src_ref=in_ref.at[me], dst_ref=in_ref.at[me], send_sem=send_sems.at[j], recv_sem=recv_sems.at[j],
                device_id=peer, device_id_type=MESH).start()
        token[...] = jnp.zeros_like(token)

    n = N_DEV - 1
    return pl.pallas_call(
        body, name=name, in_specs=[_HBM],
        out_shape=(pltpu.SemaphoreType.DMA((n,)), pltpu.SemaphoreType.DMA((n,)), pltpu.HBM(buf.shape, buf.dtype),
                   jax.ShapeDtypeStruct((8, LANES), F32)),
        out_specs=(_SEM, _SEM, _HBM, pl.BlockSpec(memory_space=pltpu.VMEM)), input_output_aliases={0: 2},
        compiler_params=pltpu.CompilerParams(has_side_effects=_EFFECT),
    )(pltpu.with_memory_space_constraint(buf, pltpu.HBM))


def _gather_all_wait(name, buf, send_sems, recv_sems, after):
    def body(in_ref, send_sems, recv_sems, after_ref, out_ref):
        me, peers = _all_peers()
        for j, (px, py, pc) in enumerate(peers):
            copy = pltpu.make_async_remote_copy(
                src_ref=in_ref.at[me], dst_ref=in_ref.at[4 * px + 2 * py + pc], send_sem=send_sems.at[j],
                recv_sem=recv_sems.at[j], device_id=(px, py, pc), device_id_type=MESH)
            copy.wait_send()
            copy.wait_recv()

    return pl.pallas_call(
        body, name=name, in_specs=[_HBM, _SEM, _SEM, _ANY], out_shape=pltpu.HBM(buf.shape, buf.dtype),
        out_specs=_HBM, input_output_aliases={0: 0},
        compiler_params=pltpu.CompilerParams(has_side_effects=_EFFECT),
    )(buf, send_sems, recv_sems, after)


def _sum_blocks(name, stacked, tm):
    n, r, lanes = stacked.shape

    def body(in_ref, o_ref):
        acc = in_ref[0]
        for j in range(1, n):
            acc = acc + in_ref[j]
        o_ref[...] = acc

    return pl.pallas_call(
        body, name=name, grid=(r // tm,), in_specs=[pl.BlockSpec((n, tm, lanes), lambda i: (0, i, 0))],
        out_specs=pl.BlockSpec((tm, lanes), lambda i: (i, 0)), out_shape=jax.ShapeDtypeStruct((r, lanes), F32),
        compiler_params=_params(("parallel",)),
    )(stacked)


def _row_tile(rows, pref, align):
    best = None
    for t in range(align, min(rows, pref) + 1, align):
        if rows % t == 0:
            best = t
    assert best is not None, (rows, pref, align)
    return best


def _adam(name, w, g, m, v):
    rows, width = w.shape
    tm = _row_tile(rows, max(8, 4096 * LANES // width), 8)
    args = [(t, width, 0) for t in (w, g, m, v)]
    return _rowcall(name, _adam_fn, args, [], [(width, F32)] * 3, tm=tm)


def kernel(x, norm_mix, norm_mlp, norm_final, mlp_w1, mlp_w2, ab_w_in, ab_w_out, rg_conv_w, rg_conv_b, rg_w_a, rg_b_a, rg_w_x, rg_b_x, rg_lambda, hg_lb_logits, hg_norm, gla_w_in, gla_w_out, gla_w_gate_up, gla_b_gate, gla_norm, loss_target, m_norm_mix, m_norm_mlp, m_norm_final, m_mlp_w1, m_mlp_w2, m_ab_w_in, m_ab_w_out, m_rg_conv_w, m_rg_conv_b, m_rg_w_a, m_rg_b_a, m_rg_w_x, m_rg_b_x, m_rg_lambda, m_hg_lb_logits, m_hg_norm, m_gla_w_in, m_gla_w_out, m_gla_w_gate_up, m_gla_b_gate, m_gla_norm, v_norm_mix, v_norm_mlp, v_norm_final, v_mlp_w1, v_mlp_w2, v_ab_w_in, v_ab_w_out, v_rg_conv_w, v_rg_conv_b, v_rg_w_a, v_rg_b_a, v_rg_w_x, v_rg_b_x, v_rg_lambda, v_hg_lb_logits, v_hg_norm, v_gla_w_in, v_gla_w_out, v_gla_w_gate_up, v_gla_b_gate, v_gla_norm):
    w = dict(norm_mix=norm_mix, norm_mlp=norm_mlp, norm_final=norm_final, mlp_w1=mlp_w1, mlp_w2=mlp_w2, ab_w_in=ab_w_in, ab_w_out=ab_w_out, rg_conv_w=rg_conv_w, rg_conv_b=rg_conv_b, rg_w_a=rg_w_a, rg_b_a=rg_b_a, rg_w_x=rg_w_x, rg_b_x=rg_b_x, rg_lambda=rg_lambda, hg_lb_logits=hg_lb_logits, hg_norm=hg_norm, gla_w_in=gla_w_in, gla_w_out=gla_w_out, gla_w_gate_up=gla_w_gate_up, gla_b_gate=gla_b_gate, gla_norm=gla_norm)
    m = dict(norm_mix=m_norm_mix, norm_mlp=m_norm_mlp, norm_final=m_norm_final, mlp_w1=m_mlp_w1, mlp_w2=m_mlp_w2, ab_w_in=m_ab_w_in, ab_w_out=m_ab_w_out, rg_conv_w=m_rg_conv_w, rg_conv_b=m_rg_conv_b, rg_w_a=m_rg_w_a, rg_b_a=m_rg_b_a, rg_w_x=m_rg_w_x, rg_b_x=m_rg_b_x, rg_lambda=m_rg_lambda, hg_lb_logits=m_hg_lb_logits, hg_norm=m_hg_norm, gla_w_in=m_gla_w_in, gla_w_out=m_gla_w_out, gla_w_gate_up=m_gla_w_gate_up, gla_b_gate=m_gla_b_gate, gla_norm=m_gla_norm)
    v = dict(norm_mix=v_norm_mix, norm_mlp=v_norm_mlp, norm_final=v_norm_final, mlp_w1=v_mlp_w1, mlp_w2=v_mlp_w2, ab_w_in=v_ab_w_in, ab_w_out=v_ab_w_out, rg_conv_w=v_rg_conv_w, rg_conv_b=v_rg_conv_b, rg_w_a=v_rg_w_a, rg_b_a=v_rg_b_a, rg_w_x=v_rg_w_x, rg_b_x=v_rg_b_x, rg_lambda=v_rg_lambda, hg_lb_logits=v_hg_lb_logits, hg_norm=v_hg_norm, gla_w_in=v_gla_w_in, gla_w_out=v_gla_w_out, gla_w_gate_up=v_gla_w_gate_up, gla_b_gate=v_gla_b_gate, gla_norm=v_gla_norm)
    chip = 2 * lax.axis_index("x") + lax.axis_index("y")
    core = lax.axis_index("c")
    sharded_shapes = [w[n].shape for n in SMALL_SHARDED]

    slots = [_into_slot(f"cast_{n}{layer}", w[n], chip, N_CHIPS, BF16, 512, layer) for n, layer in MATRICES]
    early = [i for i, (n, _) in enumerate(MATRICES) if n in EARLY_MATRICES]
    rest = [i for i in range(len(MATRICES)) if i not in early]

    def named(indices, arrays):
        big = {}
        for i, t in zip(indices, arrays):
            big.setdefault(MATRICES[i][0], []).append(t)
        return {n: (v if n in ("mlp_w1", "mlp_w2") else v[0]) for n, v in big.items()}

    gathered = _gather_chips("gather_early", [slots[i] for i in early])
    send_sems, recv_sems, *in_flight, token = _gather_start("gather_rest_start", [slots[i] for i in rest], gathered[0])

    def late_weights(after):
        landed = _gather_wait("gather_rest_wait", in_flight, send_sems, recv_sems, after)
        return _prepare_matrices(named(rest, _hand_over("gather_rest_share", list(landed))))

    big = named(early, gathered)
    vectors = _pack([w[n] for n in SMALL_SHARDED])
    vectors = _into_slot("place_vectors", vectors, chip, N_CHIPS, F32, vectors.shape[0])
    small_all = _unpack(_gather_chips("gather_vectors", [vectors])[0], sharded_shapes, lead=1)
    full = {n: w[n] for n in SMALL_REPLICATED}
    for n, t in zip(SMALL_SHARDED, small_all):
        full[n] = _join_chips(t, t.ndim - 2)

    def halves(t):
        return t.reshape(N_CHIPS, 2, t.shape[1] // 2, t.shape[2])

    in_flight_grads = {}

    def emit(tag, arrays32, arrays16):
        n = len(arrays16)
        send, recv, *rest = _reduce_start(f"reduce_{tag}_start", [halves(t) for t in arrays16])
        in_flight_grads[tag] = ([halves(t) for t in arrays32], rest[:n], rest[n:2 * n], send, recv)
        return rest[-1]

    loss_part, grad_x, g_kernel = _local_step(
        x[0], loss_target[0], _prepare_weights(big, full), token, late_weights, emit)
    g_big, g_full = _finish_grads(g_kernel)

    small_names = SMALL_REPLICATED + SMALL_SHARDED
    reduced_shapes = [g_full[n].shape for n in small_names] + [loss_part.shape]
    g_small = _pack([g_full[n] for n in small_names] + [loss_part])
    device = 2 * chip + core
    g_small = _into_slot("place_small", g_small, device, N_DEV, F32, g_small.shape[0])
    small_send, small_recv, small_in_flight, small_token = _gather_all_start("reduce_small_start", g_small)

    mine = {}
    for tag, (own, srcs, lands, send, recv) in in_flight_grads.items():
        landed = _reduce_wait(f"reduce_{tag}_wait", srcs, lands, send, recv, small_token)
        mine[tag] = [_reduce_sum(f"reduce_add_{tag}{i}", o, f, chip, core) for i, (o, f) in enumerate(zip(own, landed))]
    ordered = [mine["mlp0"][0], mine["mlp1"][0], mine["mlp0"][1], mine["mlp1"][1], mine["ab"][0], mine["mlp0"][2],
               *mine["gla"]]
    reduced = [t.reshape(2 * t.shape[1], t.shape[2]) for t in _pair_gather("reduce_share", ordered)]
    by_name = {n: [] for n, _ in MATRICES}
    for (n, _), t in zip(MATRICES, reduced):
        by_name[n].append(t)
    grads = {n: jnp.stack(v) for n, v in by_name.items()}

    g_small_all = _gather_all_wait("reduce_small_wait", small_in_flight, small_send, small_recv, reduced[0])
    g_small_red = _sum_blocks("reduce_small_add", g_small_all, g_small_all.shape[1])
    *small_red, loss_sum = _unpack(g_small_red, reduced_shapes)
    loss = loss_sum[0, 0]
    g_small_full = dict(zip(small_names, small_red))
    for n in SMALL_REPLICATED:
        grads[n] = g_small_full[n]
    for n in SMALL_SHARDED:
        width = w[n].shape[-1]
        grads[n] = lax.dynamic_slice_in_dim(g_small_full[n], chip * width, width, axis=g_small_full[n].ndim - 1)

    delta, new_m, new_v = {}, {}, {}
    for n in by_name:
        flat = [t.reshape(-1, t.shape[-1]) for t in (w[n], grads[n], m[n], v[n])]
        for dst, t in zip((delta, new_m, new_v), _adam(f"adam_{n}", *flat)):
            dst[n] = t.reshape(w[n].shape)
    small_shapes = [w[n].shape for n in small_names]
    packs = [_pack([src[n] for n in small_names]) for src in (w, grads, m, v)]
    d_small, m_small, v_small = _adam("adam_small", *packs)
    for dst, buf in ((delta, d_small), (new_m, m_small), (new_v, v_small)):
        dst.update(zip(small_names, _unpack(buf, small_shapes)))

    return (loss, grad_x[None], *[grads[n] for n in WEIGHTS], *[delta[n] for n in WEIGHTS],
            *[new_m[n] for n in WEIGHTS], *[new_v[n] for n in WEIGHTS])
```

```python
import functools

import jax
import jax.numpy as jnp
from jax import lax
from jax.experimental import pallas as pl
from jax.experimental.pallas import tpu as pltpu

F32 = jnp.float32
BF16 = jnp.bfloat16
MESH = pl.DeviceIdType.MESH

LANES = 128
CHUNK = 64
ATTN_SUB = 4
EPS = 1e-6
RG_C = 8.0
N_CHIPS = 4
N_DEV = 8
GLA_IN_WIDTH = 3104
GLA_IN_PAD = 3200
VMEM_LIMIT = 56 * 1024 * 1024

ADAM_LR = 0.001
ADAM_B1 = 0.9
ADAM_B2 = 0.999
ADAM_EPS = 1e-08
ADAM_WD = 0.01
ADAM_STEP = 10


def _raw_dot(a, b, ca, cb):
    return lax.dot_general(a.astype(BF16), b.astype(BF16), (((ca,), (cb,)), ((), ())),
                           preferred_element_type=F32)


def _raw_nn(a, b):
    return _raw_dot(a, b, 1, 0)


def _raw_nt(a, b):
    return _raw_dot(a, b, 1, 1)


def _raw_tn(a, b):
    return _raw_dot(a, b, 0, 0)


@jax.custom_vjp
def _dot_nn(a, b):
    return _raw_nn(a, b)


def _dot_nn_fwd(a, b):
    return _raw_nn(a, b), (a, b)


def _dot_nn_bwd(res, g):
    a, b = res
    return _raw_nt(g, b), _raw_tn(a, g)


_dot_nn.defvjp(_dot_nn_fwd, _dot_nn_bwd)


@jax.custom_vjp
def _dot_nt(a, b):
    return _raw_nt(a, b)


def _dot_nt_fwd(a, b):
    return _raw_nt(a, b), (a, b)


def _dot_nt_bwd(res, g):
    a, b = res
    return _raw_nn(g, b), _raw_tn(g, a)


_dot_nt.defvjp(_dot_nt_fwd, _dot_nt_bwd)


@jax.custom_vjp
def _dot_tn(a, b):
    return _raw_tn(a, b)


def _dot_tn_fwd(a, b):
    return _raw_tn(a, b), (a, b)


def _dot_tn_bwd(res, g):
    a, b = res
    return _raw_nt(b, g), _raw_nn(a, g)


_dot_tn.defvjp(_dot_tn_fwd, _dot_tn_bwd)


def _tile(n, pref):
    if n <= pref:
        return n
    t = (pref // LANES) * LANES
    while t > LANES and n % t:
        t -= LANES
    assert n % t == 0, (n, pref)
    return t


def _params(sem):
    return pltpu.CompilerParams(dimension_semantics=sem, vmem_limit_bytes=VMEM_LIMIT)


def _rowcall(name, fn, rows, pars, row_outs, par_outs=(), tm=256, pin=None):
    if pin is not None:
        inner, pars = fn, list(pars) + [pin]
        fn = lambda *vals: inner(*vals[:-1])
    n_rows = rows[0][0].shape[0]
    tm = min(tm, n_rows)
    assert n_rows % tm == 0
    n_r, n_p, n_ro = len(rows), len(pars), len(row_outs)

    def body(*refs):
        vals = [r[...].astype(F32) for r in refs[:n_r + n_p]]
        outs = fn(*vals)
        o_refs = refs[n_r + n_p:n_r + n_p + n_ro]
        po_refs = refs[n_r + n_p + n_ro:]
        for o_ref, val in zip(o_refs, outs[:n_ro]):
            o_ref[...] = val.astype(o_ref.dtype)
        first = pl.program_id(0) == 0
        for po_ref, val in zip(po_refs, outs[n_ro:]):
            @pl.when(first)
            def _():
                po_ref[...] = val

            @pl.when(jnp.logical_not(first))
            def _():
                po_ref[...] += val

    def const_map(nd):
        return lambda i: (0,) * nd

    def row_spec(w, cb):
        return pl.BlockSpec((tm, w), lambda i: (i, cb))

    in_specs = [row_spec(w, cb) for _, w, cb in rows]
    in_specs += [pl.BlockSpec(p.shape, const_map(p.ndim)) for p in pars]
    out_specs = [pl.BlockSpec((tm, w), lambda i: (i, 0)) for w, _ in row_outs]
    out_specs += [pl.BlockSpec(tuple(s), const_map(len(s))) for s in par_outs]
    out_shape = [jax.ShapeDtypeStruct((n_rows, w), dt) for w, dt in row_outs]
    out_shape += [jax.ShapeDtypeStruct(tuple(s), F32) for s in par_outs]
    return pl.pallas_call(
        body, name=name, grid=(n_rows // tm,), in_specs=in_specs, out_specs=out_specs, out_shape=out_shape,
        compiler_params=_params(("arbitrary",) if par_outs else ("parallel",)),
    )(*[r[0] for r in rows], *pars)


def _vjp_of(fn, n_prim, n_out, n_par, n_pass=0):
    def bwd(*args):
        prim = args[:n_prim]
        cts = args[n_prim:n_prim + n_out]
        passes = args[n_prim + n_out:n_prim + n_out + 2 * n_pass]
        pars = args[n_prim + n_out + 2 * n_pass:]
        _, vjp = jax.vjp(fn, *prim, *pars)
        grads = vjp(tuple(cts))
        sums = tuple(passes[2 * i] + passes[2 * i + 1] for i in range(n_pass))
        return tuple(grads[:n_prim]) + sums + tuple(grads[n_prim:])
    return bwd


def _mm(name, a, b, mode="nn", extras=(), epi=None, out_dtypes=(F32,), a_pro=None, out_split=None,
        tm=1024, tn=1024, tk=1024):
    split = b.shape[0] if b.ndim == 3 else None
    b_rows, b_cols = b.shape[-2:]
    if mode == "nn":
        (m, k), n = a.shape, b_cols * (split or 1)
    elif mode == "nt":
        (m, k), n = a.shape, b_rows
        assert k == b_cols * (split or 1)
    else:
        assert split is None
        (k, m), n = a.shape, b_cols
    tm, tk = _tile(m, tm), _tile(k, tk)
    tn = _tile(n // out_split, tn) if out_split else _tile(n, tn)
    if split and mode == "nn":
        tn = _tile(b_cols, tn)
    if split and mode == "nt":
        tk = _tile(b_cols, tk)
    nk = k // tk
    raw = {"nn": _raw_nn, "nt": _raw_nt, "tn": _raw_tn}[mode]
    n_e, n_o = len(extras), len(out_dtypes)
    if epi is None:
        epi = lambda acc: (acc,)

    def body(a_ref, b_ref, *rest):
        e_refs, o_refs = rest[:n_e], rest[n_e:n_e + n_o]
        kk = pl.program_id(2)
        a_tile = a_ref[...] if a_pro is None else a_pro(a_ref[...].astype(F32))
        part = raw(a_tile, b_ref[...])

        def finish(total):
            res = epi(total, *[e[...].astype(F32) for e in e_refs])
            for o_ref, r in zip(o_refs, res):
                o_ref[...] = r.astype(o_ref.dtype)

        if nk == 1:
            finish(part)
            return
        acc = rest[-1]

        @pl.when(kk == 0)
        def _():
            acc[...] = part

        @pl.when((kk > 0) & (kk < nk - 1))
        def _():
            acc[...] += part

        @pl.when(kk == nk - 1)
        def _():
            finish(acc[...] + part)

    a_spec = pl.BlockSpec((tk, tm), lambda i, j, kk: (kk, i)) if mode == "tn" else pl.BlockSpec((tm, tk), lambda i, j, kk: (i, kk))
    if split and mode == "nn":
        per = b_cols // tn
        b_spec = pl.BlockSpec((None, tk, tn), lambda i, j, kk: (j // per, kk, j % per))
    elif split:
        per = b_cols // tk
        b_spec = pl.BlockSpec((None, tn, tk), lambda i, j, kk: (kk // per, j, kk % per))
    elif mode == "nt":
        b_spec = pl.BlockSpec((tn, tk), lambda i, j, kk: (j, kk))
    else:
        b_spec = pl.BlockSpec((tk, tn), lambda i, j, kk: (kk, j))
    mn_spec = pl.BlockSpec((tm, tn), lambda i, j, kk: (i, j))
    if out_split:
        assert not extras
        per_out = n // out_split // tn
        out_spec = pl.BlockSpec((None, tm, tn), lambda i, j, kk: (j // per_out, i, j % per_out))
        out_shapes = [jax.ShapeDtypeStruct((out_split, m, n // out_split), dt) for dt in out_dtypes]
    else:
        out_spec = mn_spec
        out_shapes = [jax.ShapeDtypeStruct((m, n), dt) for dt in out_dtypes]
    outs = pl.pallas_call(
        body, name=name, grid=(m // tm, n // tn, nk),
        in_specs=[a_spec, b_spec] + [mn_spec] * n_e, out_specs=[out_spec] * n_o,
        out_shape=out_shapes,
        scratch_shapes=[pltpu.VMEM((tm, tn), F32)] if nk > 1 else [],
        compiler_params=_params(("parallel", "parallel", "arbitrary")),
    )(a, b, *extras)
    return outs[0] if n_o == 1 else outs


def _sigmoid(x):
    return jax.nn.sigmoid(x)


def _silu(x):
    return x * _sigmoid(x)


def _softplus(x):
    return jnp.maximum(x, 0.0) + jnp.log1p(jnp.exp(-jnp.abs(x)))


def _rmsnorm_fn(x, gain):
    return (x * lax.rsqrt(jnp.mean(x * x, axis=-1, keepdims=True) + EPS) * gain,)


def _head_norm(o, gain, n_heads):
    w = o.shape[-1] // n_heads
    parts = []
    for h in range(n_heads):
        oh = o[:, h * w:(h + 1) * w]
        parts.append(oh * lax.rsqrt(jnp.mean(oh * oh, axis=-1, keepdims=True) + EPS))
    return jnp.concatenate(parts, axis=-1) * gain


@jax.custom_jvp
def _neg_expm1(x):
    u = jnp.exp(x)
    is_one = u == 1.0
    return jnp.where(is_one, -x, (1.0 - u) * x / jnp.log(jnp.where(is_one, 2.0, u)))


@_neg_expm1.defjvp
def _neg_expm1_jvp(primals, tangents):
    (x,), (t,) = primals, tangents
    return _neg_expm1(x), -jnp.exp(x) * t


def _rg_gates_fn(xc, wa, wx, ba, bx, lam):
    outs = []
    for d in range(2):
        r = _sigmoid(_dot_nn(xc, wa[d]) + ba[d:d + 1])
        i = _sigmoid(_dot_nn(xc, wx[d]) + bx[d:d + 1])
        log_a = -RG_C * r * _softplus(-lam[d:d + 1])
        outs.append(jnp.exp(log_a))
        outs.append(jnp.sqrt(_neg_expm1(2.0 * log_a)) * (i * xc))
    return tuple(outs)


def _hg_pre_fn(q, f_f, f_b, logits):
    mx = jnp.maximum(logits[0:1], logits[1:2])
    e0 = jnp.exp(logits[0:1] - mx)
    e1 = jnp.exp(logits[1:2] - mx)
    lb = e0 / (e0 + e1)
    outs = [_silu(q)]
    for f in (f_f, f_b):
        outs.append((1.0 - lb) * _sigmoid(-f))
        outs.append(jnp.log(lb + (1.0 - lb) * _sigmoid(f)))
    return tuple(outs)


def _post0_fn(hs, ga, o, g, gain):
    ya = hs * jax.nn.gelu(ga, approximate=True)
    yb = _head_norm(o, gain, 4) * _silu(g)
    return (jnp.concatenate([ya, yb], axis=-1),)


def _post0_fwd_fn(h_f, h_b, ga, o_f, o_b, g, gain):
    return _post0_fn(h_f + h_b, ga, o_f + o_b, g, gain)


def _post0_bwd_fn(h_f, h_b, ga, o_f, o_b, g, dmix, gain):
    _, vjp = jax.vjp(_post0_fn, h_f + h_b, ga, o_f + o_b, g, gain)
    return vjp((dmix,))


def _gla_pre_fn(q, lr, w_up, b_gate):
    outs = [q * (128.0 ** -0.5)]
    for d in range(2):
        z = _dot_nn(lr, w_up[d]) + b_gate[d:d + 1]
        outs.append(-_softplus(-z) * (1.0 / 16.0))
    return tuple(outs)


def _gla_post_fn(o, r, gain):
    return (_head_norm(o, gain, 4) * _silu(r),)


def _gla_post_fwd_fn(o_f, o_b, r, gain):
    return _gla_post_fn(o_f + o_b, r, gain)


def _gla_post_bwd_fn(o_f, o_b, r, dmix, gain):
    _, vjp = jax.vjp(_gla_post_fn, o_f + o_b, r, gain)
    return vjp((dmix,))


def _relu2_bwd_epi(acc, hid):
    return (acc * 2.0 * jnp.maximum(hid, 0.0),)


def _relu2(x):
    r = jnp.maximum(x, 0.0)
    return r * r


def _add_epi(acc, res):
    return (acc + res,)


def _loss_head_fn(h, target, gain):
    def f(h, gain):
        y = _rmsnorm_fn(h, gain)[0]
        err = y - target
        return 0.5 * jnp.sum(jnp.mean(err * err, axis=-1, keepdims=True))
    loss, (dh, dgain) = jax.value_and_grad(f, argnums=(0, 1))(h, gain)
    return dh, jnp.full((1, LANES), loss, F32), dgain


def _adam_fn(w, g, m, v):
    m2 = ADAM_B1 * m + (1.0 - ADAM_B1) * g
    v2 = ADAM_B2 * v + (1.0 - ADAM_B2) * (g * g)
    m_hat = m2 / (1.0 - ADAM_B1 ** ADAM_STEP)
    v_hat = v2 / (1.0 - ADAM_B2 ** ADAM_STEP)
    delta = -ADAM_LR * (m_hat / (jnp.sqrt(v_hat) + ADAM_EPS) + ADAM_WD * w)
    return delta, m2, v2


def _shifted(x, t_idx, off):
    n = x.shape[0]
    rolled = pltpu.roll(x, (-off) % n, 0)
    valid = (t_idx + off >= 0) & (t_idx + off < n)
    return jnp.where(valid, rolled, 0.0)


def _conv_fwd(name, src, colblock, w, b):
    n_rows, width = src.shape[0], w.shape[1]

    def body(x_ref, w_ref, b_ref, o_ref):
        x = x_ref[...]
        t_idx = lax.broadcasted_iota(jnp.int32, x.shape, 0)
        acc = b_ref[...] + w_ref[2:3, :] * x
        acc += w_ref[0:1, :] * _shifted(x, t_idx, -2)
        acc += w_ref[1:2, :] * _shifted(x, t_idx, -1)
        acc += w_ref[3:4, :] * _shifted(x, t_idx, 1)
        o_ref[...] = acc

    nb = width // LANES
    return pl.pallas_call(
        body, name=name, grid=(nb,),
        in_specs=[pl.BlockSpec((n_rows, LANES), lambda j: (0, colblock * nb + j)),
                  pl.BlockSpec((4, LANES), lambda j: (0, j)), pl.BlockSpec((1, LANES), lambda j: (0, j))],
        out_specs=pl.BlockSpec((n_rows, LANES), lambda j: (0, j)),
        out_shape=jax.ShapeDtypeStruct((n_rows, width), F32),
        compiler_params=_params(("parallel",)),
    )(src, w, b)


def _conv_bwd(name, src, colblock, w, d):
    n_rows, width = src.shape[0], w.shape[1]

    def body(x_ref, w_ref, d_ref, dx_ref, dw_ref, db_ref):
        x = x_ref[...]
        g = d_ref[...]
        t_idx = lax.broadcasted_iota(jnp.int32, x.shape, 0)
        dx = w_ref[2:3, :] * g
        dx += w_ref[0:1, :] * _shifted(g, t_idx, 2)
        dx += w_ref[1:2, :] * _shifted(g, t_idx, 1)
        dx += w_ref[3:4, :] * _shifted(g, t_idx, -1)
        dx_ref[...] = dx.astype(dx_ref.dtype)
        dw_ref[0:1, :] = jnp.sum(g * _shifted(x, t_idx, -2), axis=0, keepdims=True)
        dw_ref[1:2, :] = jnp.sum(g * _shifted(x, t_idx, -1), axis=0, keepdims=True)
        dw_ref[2:3, :] = jnp.sum(g * x, axis=0, keepdims=True)
        dw_ref[3:4, :] = jnp.sum(g * _shifted(x, t_idx, 1), axis=0, keepdims=True)
        db_ref[...] = jnp.sum(g, axis=0, keepdims=True)

    nb = width // LANES
    return pl.pallas_call(
        body, name=name, grid=(nb,),
        in_specs=[pl.BlockSpec((n_rows, LANES), lambda j: (0, colblock * nb + j)),
                  pl.BlockSpec((4, LANES), lambda j: (0, j)),
                  pl.BlockSpec((n_rows, LANES), lambda j: (0, j))],
        out_specs=[pl.BlockSpec((n_rows, LANES), lambda j: (0, j)), pl.BlockSpec((4, LANES), lambda j: (0, j)),
                   pl.BlockSpec((1, LANES), lambda j: (0, j))],
        out_shape=[jax.ShapeDtypeStruct((n_rows, width), BF16), jax.ShapeDtypeStruct((4, width), F32),
                   jax.ShapeDtypeStruct((1, width), F32)],
        compiler_params=_params(("parallel",)),
    )(src, w, d)


SUBLANES = 8
SCAN_UNROLL = 8


def _shift_rows(x, d, fill):
    n = x.shape[0]
    t = lax.broadcasted_iota(jnp.int32, x.shape, 0)
    valid = (t >= d) if d > 0 else (t < n + d)
    return jnp.where(valid, pltpu.roll(x, d % n, 0), fill)


def _tile_scan(a, u, reverse):
    d = 1
    while d < a.shape[0]:
        s = -d if reverse else d
        a_sh, u_sh = _shift_rows(a, s, 1.0), _shift_rows(u, s, 0.0)
        u = u + a * u_sh
        a = a * a_sh
        d *= 2
    return a, u


def _edge_row(x, reverse):
    return x[0:1, :] if reverse else x[SUBLANES - 1:SUBLANES, :]


def _scan_specs(n_rows, n):
    return [pl.BlockSpec((n_rows, LANES), lambda j: (0, j))] * n


def _scan_fwd(name, a, u, reverse):
    n_rows, width = a.shape
    n_tiles = n_rows // SUBLANES

    def body(a_ref, u_ref, h_ref):
        def step(i, carry):
            tile = (n_tiles - 1 - i) if reverse else i
            rows = pl.ds(pl.multiple_of(tile * SUBLANES, SUBLANES), SUBLANES)
            acc_a, acc_u = _tile_scan(a_ref[rows, :], u_ref[rows, :], reverse)
            h = acc_u + acc_a * carry
            h_ref[rows, :] = h
            return _edge_row(h, reverse)
        lax.fori_loop(0, n_tiles, step, jnp.zeros((1, LANES), F32), unroll=SCAN_UNROLL)

    return pl.pallas_call(
        body, name=name, grid=(width // LANES,), in_specs=_scan_specs(n_rows, 2), out_specs=_scan_specs(n_rows, 1)[0],
        out_shape=jax.ShapeDtypeStruct((n_rows, width), F32), compiler_params=_params(("parallel",)),
    )(a, u)


def _scan_bwd(name, a, h, dh, reverse):
    n_rows, width = a.shape
    n_tiles = n_rows // SUBLANES
    against = not reverse
    one = -1 if against else 1

    def body(a_ref, h_ref, dh_ref, du_ref, da_ref):
        def step(i, carry):
            g_in, a_edge = carry
            tile = (n_tiles - 1 - i) if against else i
            start = pl.multiple_of(tile * SUBLANES, SUBLANES)
            rows = pl.ds(start, SUBLANES)
            a_tile = a_ref[rows, :]
            coeff = _shift_rows(a_tile, one, a_edge)
            acc_a, acc_u = _tile_scan(coeff, dh_ref[rows, :], against)
            g = acc_u + acc_a * g_in
            du_ref[rows, :] = g
            outside = (start + SUBLANES) if reverse else (start - 1)
            inside = (outside >= 0) & (outside < n_rows)
            h_edge = jnp.where(inside, h_ref[pl.ds(jnp.clip(outside, 0, n_rows - 1), 1), :], 0.0)
            da_ref[rows, :] = g * _shift_rows(h_ref[rows, :], -one, h_edge)
            return _edge_row(g, against), _edge_row(a_tile, against)
        zero = jnp.zeros((1, LANES), F32)
        lax.fori_loop(0, n_tiles, step, (zero, zero), unroll=SCAN_UNROLL)

    return pl.pallas_call(
        body, name=name, grid=(width // LANES,), in_specs=_scan_specs(n_rows, 3), out_specs=_scan_specs(n_rows, 2),
        out_shape=[jax.ShapeDtypeStruct((n_rows, width), F32)] * 2, compiler_params=_params(("parallel",)),
    )(a, h, dh)


def _tri_mask(c, reverse):
    row = lax.broadcasted_iota(jnp.int32, (c, c), 0)
    col = lax.broadcasted_iota(jnp.int32, (c, c), 1)
    return (col >= row) if reverse else (col <= row)


def _cumsum_rows(x, reverse):
    tri = _tri_mask(x.shape[0], reverse).astype(BF16)
    hi = x.astype(BF16)
    rest = x - hi.astype(F32)
    mid = rest.astype(BF16)
    lo = (rest - mid.astype(F32)).astype(BF16)
    return _raw_nn(tri, hi) + _raw_nn(tri, mid) + _raw_nn(tri, lo)


@functools.partial(jax.custom_vjp, nondiff_argnums=(1,))
def _cumsum(x, reverse):
    return _cumsum_rows(x, reverse)


def _cumsum_fwd(x, reverse):
    return _cumsum_rows(x, reverse), None


def _cumsum_bwd(reverse, _, g):
    return (_cumsum_rows(g, not reverse),)


_cumsum.defvjp(_cumsum_fwd, _cumsum_bwd)


def _chunks_fn(qs, ks, vs, lfs, sts, reverses):
    n, c = len(qs), qs[0].shape[0]
    every = range(n)
    tris = [_tri_mask(c, r) for r in reverses]
    cums = [_cumsum(lfs[i], reverses[i]) for i in every]
    rid = lax.broadcasted_iota(jnp.int32, cums[0].shape, 0)

    def pick(cum, r):
        return jnp.sum(jnp.where(rid == r, cum, 0.0), axis=0, keepdims=True)

    refs = [pick(cums[i], (c - 1 - c // 2) if reverses[i] else c // 2) for i in every]
    lasts = [pick(cums[i], 0 if reverses[i] else c - 1) for i in every]
    q_in = [qs[i] * jnp.exp(cums[i] - refs[i]) for i in every]
    k_in = [ks[i] * jnp.exp(refs[i] - cums[i]) for i in every]
    scores = [jnp.where(tris[i], _dot_nt(q_in[i], k_in[i]), 0.0) for i in every]
    o_intra = [_dot_nn(scores[i], vs[i]) for i in every]
    q_out = [qs[i] * jnp.exp(cums[i]) for i in every]
    o_inter = [_dot_nt(q_out[i], sts[i]) for i in every]
    k_state = [ks[i] * jnp.exp(lasts[i] - cums[i]) for i in every]
    upd = [_dot_tn(vs[i], k_state[i]) for i in every]
    st_new = [sts[i] * jnp.exp(lasts[i]) + upd[i] for i in every]
    return [o_intra[i] + o_inter[i] for i in every], st_new


def _attn_fwd(name, q, k_f, k_b, v, lf_f, lf_b, n_heads, dk, dv):
    n_rows = q[0].shape[0]
    n_chunks = n_rows // CHUNK
    n_steps = n_chunks // ATTN_SUB
    wk, wv = n_heads * dk, n_heads * dv

    def spec(width, off, rev):
        return pl.BlockSpec((CHUNK * ATTN_SUB, width), lambda n: ((n_steps - 1 - n) if rev else n, off))

    def sspec(rev):
        return pl.BlockSpec((ATTN_SUB, n_heads, dv, dk), lambda n: ((n_steps - 1 - n) if rev else n, 0, 0, 0))

    def body(qf, kf, vf, lff, qb, kb, vb, lfb, of_ref, ob_ref, sf_ref, sb_ref, st):
        @pl.when(pl.program_id(0) == 0)
        def _():
            st[...] = jnp.zeros_like(st)

        ins = ((qf, kf, vf, lff), (qb, kb, vb, lfb))
        chains = [(d, h) for d in range(2) for h in range(n_heads)]
        ck = [slice(h * dk, (h + 1) * dk) for h in range(n_heads)]
        cv = [slice(h * dv, (h + 1) * dv) for h in range(n_heads)]
        sts = [st[d, h] for d, h in chains]
        done = []
        for sub in range(ATTN_SUB):
            local = (sub, ATTN_SUB - 1 - sub)
            rows = [slice(local[d] * CHUNK, (local[d] + 1) * CHUNK) for d in range(2)]
            qs = [ins[d][0][rows[d], ck[h]] for d, h in chains]
            ks = [ins[d][1][rows[d], ck[h]] for d, h in chains]
            vs = [ins[d][2][rows[d], cv[h]] for d, h in chains]
            lfs = [ins[d][3][rows[d], ck[h]] for d, h in chains]
            os_, st_new = _chunks_fn(qs, ks, vs, lfs, sts, [d == 1 for d, _ in chains])
            done.append((local, rows, sts, os_))
            sts = st_new
        for local, rows, entered, os_ in done:
            for i, (d, h) in enumerate(chains):
                (sf_ref, sb_ref)[d][local[d], h] = entered[i].astype(BF16)
                (of_ref, ob_ref)[d][rows[d], cv[h]] = os_[i]
        for i, (d, h) in enumerate(chains):
            st[d, h] = sts[i]

    in_specs = [spec(wk, q[1], False), spec(wk, k_f[1], False), spec(wv, v[1], False), spec(wk, lf_f[1], False),
                spec(wk, q[1], True), spec(wk, k_b[1], True), spec(wv, v[1], True), spec(wk, lf_b[1], True)]
    return pl.pallas_call(
        body, name=name, grid=(n_steps,), in_specs=in_specs,
        out_specs=[spec(wv, 0, False), spec(wv, 0, True), sspec(False), sspec(True)],
        out_shape=[jax.ShapeDtypeStruct((n_rows, wv), F32)] * 2
        + [jax.ShapeDtypeStruct((n_chunks, n_heads, dv, dk), BF16)] * 2,
        scratch_shapes=[pltpu.VMEM((2, n_heads, dv, dk), F32)],
        compiler_params=_params(("arbitrary",)),
    )(q[0], k_f[0], v[0], lf_f[0], q[0], k_b[0], v[0], lf_b[0])


def _attn_bwd(name, q, k_f, k_b, v, lf_f, lf_b, st_f, st_b, do, n_heads, dk, dv, out_dtype=F32):
    n_rows = q[0].shape[0]
    n_chunks = n_rows // CHUNK
    n_steps = n_chunks // ATTN_SUB
    wk, wv = n_heads * dk, n_heads * dv

    def spec(width, off, rev):
        return pl.BlockSpec((CHUNK * ATTN_SUB, width), lambda n: (n if rev else (n_steps - 1 - n), off))

    def sspec(rev):
        return pl.BlockSpec((ATTN_SUB, n_heads, dv, dk), lambda n: (n if rev else (n_steps - 1 - n), 0, 0, 0))

    def body(qf, kf, vf, lff, sf, dof, qb, kb, vb, lfb, sb, dob,
             dqf, dkf, dvf, dlff, dqb, dkb, dvb, dlfb, dst):
        @pl.when(pl.program_id(0) == 0)
        def _():
            dst[...] = jnp.zeros_like(dst)

        ins = ((qf, kf, vf, lff, sf, dof), (qb, kb, vb, lfb, sb, dob))
        outs = ((dqf, dkf, dvf, dlff), (dqb, dkb, dvb, dlfb))
        chains = [(d, h) for d in range(2) for h in range(n_heads)]
        ck = [slice(h * dk, (h + 1) * dk) for h in range(n_heads)]
        cv = [slice(h * dv, (h + 1) * dv) for h in range(n_heads)]
        fn = functools.partial(_chunks_fn, reverses=[d == 1 for d, _ in chains])
        dsts = [dst[d, h] for d, h in chains]
        done = []
        for sub in range(ATTN_SUB):
            local = (ATTN_SUB - 1 - sub, sub)
            rows = [slice(local[d] * CHUNK, (local[d] + 1) * CHUNK) for d in range(2)]
            qs = [ins[d][0][rows[d], ck[h]] for d, h in chains]
            ks = [ins[d][1][rows[d], ck[h]] for d, h in chains]
            vs = [ins[d][2][rows[d], cv[h]] for d, h in chains]
            lfs = [ins[d][3][rows[d], ck[h]] for d, h in chains]
            sts = [ins[d][4][local[d], h].astype(F32) for d, h in chains]
            dos = [ins[d][5][rows[d], cv[h]] for d, h in chains]
            _, vjp = jax.vjp(fn, qs, ks, vs, lfs, sts)
            dqs, dks, dvs, dlfs, dsts = vjp((dos, dsts))
            done.append((rows, dqs, dks, dvs, dlfs))
        for rows, dqs, dks, dvs, dlfs in done:
            for i, (d, h) in enumerate(chains):
                dq_r, dk_r, dv_r, dlf_r = outs[d]
                dq_r[rows[d], ck[h]] = dqs[i].astype(dq_r.dtype)
                dk_r[rows[d], ck[h]] = dks[i].astype(dk_r.dtype)
                dv_r[rows[d], cv[h]] = dvs[i].astype(dv_r.dtype)
                dlf_r[rows[d], ck[h]] = dlfs[i].astype(dlf_r.dtype)
        for i, (d, h) in enumerate(chains):
            dst[d, h] = dsts[i]

    def dir_specs(kk, lf, rev):
        return [spec(wk, q[1], rev), spec(wk, kk[1], rev), spec(wv, v[1], rev), spec(wk, lf[1], rev), sspec(rev),
                spec(wv, 0, rev)]

    def dir_out_specs(rev):
        return [spec(wk, 0, rev), spec(wk, 0, rev), spec(wv, 0, rev), spec(wk, 0, rev)]

    shapes = [jax.ShapeDtypeStruct((n_rows, wk), out_dtype), jax.ShapeDtypeStruct((n_rows, wk), out_dtype),
              jax.ShapeDtypeStruct((n_rows, wv), out_dtype), jax.ShapeDtypeStruct((n_rows, wk), F32)]
    outs = pl.pallas_call(
        body, name=name, grid=(n_steps,), in_specs=dir_specs(k_f, lf_f, False) + dir_specs(k_b, lf_b, True),
        out_specs=dir_out_specs(False) + dir_out_specs(True), out_shape=shapes + shapes,
        scratch_shapes=[pltpu.VMEM((2, n_heads, dv, dk), F32)],
        compiler_params=_params(("arbitrary",)),
    )(q[0], k_f[0], v[0], lf_f[0], st_f, do, q[0], k_b[0], v[0], lf_b[0], st_b, do)
    return outs[:4], outs[4:]


def _row2(v):
    return v.reshape(1, -1)


def _mlp_fwd(tag, h, gain, w1, w2):
    y = _rowcall(f"{tag}_norm", _rmsnorm_fn, [(h, h.shape[1], 0)], [gain], [(h.shape[1], BF16)], tm=512)[0]
    hid = _mm(f"{tag}_up", y, w1, out_dtypes=(BF16,))
    h_out = _mm(f"{tag}_down", hid, w2, a_pro=_relu2, extras=(h,), epi=_add_epi)
    return h_out, (y, hid)


def _dw(name, a, b, **kw):
    return _mm(name, a, b, mode="tn", epi=lambda acc: (acc, acc), out_dtypes=(F32, BF16), **kw)


def _mlp_bwd(tag, h, gain, w1, w2, saved, dh_out):
    y, hid = saved
    dhid = _mm(f"{tag}_dact", dh_out, w2, mode="nt", extras=(hid,), epi=_relu2_bwd_epi, out_dtypes=(BF16,))
    dw2 = _dw(f"{tag}_dw2", hid, dh_out, a_pro=_relu2)
    dw1 = _dw(f"{tag}_dw1", y, dhid, out_split=N_CHIPS)
    dy = _mm(f"{tag}_dy", dhid, w1, mode="nt")
    dh, dgain = _norm_bwd(f"{tag}_dnorm", h, gain, dy, dh_out)
    return dh, dgain, dw1, dw2


def _norm_bwd(name, h, gain, dy, dres, pin=None):
    d = h.shape[1]

    def fn(h, dy, dres, gain):
        _, vjp = jax.vjp(lambda a, b: _rmsnorm_fn(a, b)[0], h, gain)
        dh, dgain = vjp(dy)
        return dh + dres, dgain

    dh, dgain = _rowcall(name, fn, [(h, d, 0), (dy, d, 0), (dres, d, 0)], [gain], [(d, F32)], [(1, d)], tm=512, pin=pin)
    return dh, dgain


def _local_step(x, target, w, pin=None, late=None, emit=None):
    g = {}
    d_model = x.shape[1]
    rg_w = hg_w = d_model // 2
    pins = []

    def send_off(tag, pairs):
        if emit is not None:
            pins.append(emit(tag, [p[0] for p in pairs], [p[1] for p in pairs]))

    def both(fn, pair):
        return [fn(t) for t in pair]

    def chip_major(t):
        return t.reshape(N_CHIPS, t.shape[0] // N_CHIPS, t.shape[1])

    h_a0 = x
    gain = _row2(w["norm_mix"][0])
    y0 = _rowcall("l0_norm", _rmsnorm_fn, [(h_a0, d_model, 0)], [gain], [(d_model, BF16)], tm=512, pin=pin)[0]
    proj0 = _mm("l0_in", y0, w["ab_w_in"])
    conv_w, conv_b = w["rg_conv_w"], _row2(w["rg_conv_b"])
    xc = _conv_fwd("rg_conv", proj0, 0, conv_w, conv_b)
    gate_pars = [w["rg_wa_bd"], w["rg_wx_bd"], w["rg_b_a"], w["rg_b_x"], w["rg_lambda"]]
    a_f, u_f, a_b, u_b = _rowcall("rg_gates", _rg_gates_fn, [(xc, rg_w, 0)], gate_pars, [(rg_w, F32)] * 4)
    hs_f = _scan_fwd("rg_scan_f", a_f, u_f, False)
    hs_b = _scan_fwd("rg_scan_b", a_b, u_b, True)
    hg_rows = [(proj0, hg_w, 2), (proj0, hg_w, 3), (proj0, hg_w, 4)]
    qh, k_f, lf_f, k_b, lf_b = _rowcall("hg_pre", _hg_pre_fn, hg_rows, [w["hg_lb_logits"]], [(hg_w, F32)] * 5)
    iv = (proj0, 5)
    o_f, o_b, st_f, st_b = _attn_fwd("hg_attn", (qh, 0), (k_f, 0), (k_b, 0), iv, (lf_f, 0), (lf_b, 0), 4, 128, 128)
    post0_rows = [(hs_f, rg_w, 0), (hs_b, rg_w, 0), (proj0, rg_w, 1), (o_f, hg_w, 0), (o_b, hg_w, 0), (proj0, hg_w, 6)]
    hg_gain = _row2(w["hg_norm"])
    mix_in0 = _rowcall("l0_post", _post0_fwd_fn, post0_rows, [hg_gain], [(d_model, BF16)])[0]
    if late is not None:
        w = {**w, **late(mix_in0)}
    h_b0 = _mm("l0_out", mix_in0, w["ab_w_out"], extras=(h_a0,), epi=_add_epi)
    h_c0, mlp0 = _mlp_fwd("mlp0", h_b0, _row2(w["norm_mlp"][0]), w["mlp_w1"][0], w["mlp_w2"][0])

    h_a1 = h_c0
    gain1 = _row2(w["norm_mix"][1])
    y1 = _rowcall("l1_norm", _rmsnorm_fn, [(h_a1, d_model, 0)], [gain1], [(d_model, BF16)], tm=512)[0]
    proj1 = _mm("l1_in", y1, w["gla_w_in_pad"], tn=640)
    gla_pars = [w["gla_w_up_pad"], w["gla_b_gate"]]
    gq, glf_f, glf_b = _rowcall("gla_pre", _gla_pre_fn, [(proj1, 512, 0), (proj1, LANES, 24)], gla_pars, [(512, F32)] * 3)
    gk, gv = (proj1, 1), (proj1, 1)
    go_f, go_b, gst_f, gst_b = _attn_fwd("gla_attn", (gq, 0), gk, gk, gv, (glf_f, 0), (glf_b, 0), 4, 128, 256)
    gla_gain = _row2(w["gla_norm"])
    post1_rows = [(go_f, d_model, 0), (go_b, d_model, 0), (proj1, d_model, 2)]
    mix_in1 = _rowcall("l1_post", _gla_post_fwd_fn, post1_rows, [gla_gain], [(d_model, BF16)])[0]
    h_b1 = _mm("l1_out", mix_in1, w["gla_w_out"], extras=(h_a1,), epi=_add_epi)
    h_c1, mlp1 = _mlp_fwd("mlp1", h_b1, _row2(w["norm_mlp"][1]), w["mlp_w1"][1], w["mlp_w2"][1])

    dh, loss, g["norm_final"] = _rowcall(
        "loss_head", _loss_head_fn, [(h_c1, d_model, 0), (target, d_model, 0)], [_row2(w["norm_final"])],
        [(d_model, F32)], [(1, LANES), (1, d_model)], tm=512)

    dh, g_nmlp1, g_w1_1, g_w2_1 = _mlp_bwd("mlp1", h_b1, _row2(w["norm_mlp"][1]), w["mlp_w1"][1], w["mlp_w2"][1], mlp1, dh)
    send_off("mlp1", [g_w1_1, both(chip_major, g_w2_1)])
    dmix1 = _mm("l1_dout", dh, w["gla_w_out"], mode="nt")
    g_gla_out = _dw("l1_dwout", mix_in1, dh)
    g["gla_w_out"] = g_gla_out[0]
    dgo, dr, g["gla_norm"] = _rowcall(
        "l1_dpost", _gla_post_bwd_fn, post1_rows + [(dmix1, d_model, 0)], [gla_gain],
        [(d_model, F32), (d_model, BF16)], [(1, d_model)], pin=pins.pop() if pins else None)
    (dq_f, dk_f, dv_f, dlf_f), (dq_b, dk_b, dv_b, dlf_b) = _attn_bwd(
        "gla_dattn", (gq, 0), gk, gk, gv, (glf_f, 0), (glf_b, 0), gst_f, gst_b, dgo, 4, 128, 256)

    def gla_pre_bwd(q, lr, dq1, dq2, dlf1, dlf2, dk1, dk2, dv1, dv2, w_up, b_gate):
        dlr = jnp.zeros_like(lr)
        dws, dbs = [], []
        for d, dlf in enumerate((dlf1, dlf2)):
            z = _raw_nn(lr, w_up[d]) + b_gate[d:d + 1]
            dz = dlf * _sigmoid(-z) * (1.0 / 16.0)
            dlr = dlr + _raw_nt(dz, w_up[d])
            dws.append(_raw_tn(dz, lr))
            dbs.append(jnp.sum(dz, axis=0, keepdims=True))
        return ((dq1 + dq2) * (128.0 ** -0.5), dk1 + dk2, dv1 + dv2, dlr, dws[0], dws[1], dbs[0], dbs[1])

    rows = [(proj1, 512, 0), (proj1, LANES, 24), (dq_f, 512, 0), (dq_b, 512, 0), (dlf_f, 512, 0), (dlf_b, 512, 0),
            (dk_f, 512, 0), (dk_b, 512, 0), (dv_f, d_model, 0), (dv_b, d_model, 0)]
    dq, dk, dv, dlr, dwt_f, dwt_b, db_f, db_b = _rowcall(
        "gla_dpre", gla_pre_bwd, rows, gla_pars, [(512, BF16), (512, BF16), (d_model, BF16), (LANES, BF16)],
        [(512, LANES), (512, LANES), (1, 512), (1, 512)])
    g["gla_w_up_pad"] = jnp.stack([dwt_f.T, dwt_b.T])
    g["gla_b_gate"] = jnp.concatenate([db_f, db_b], axis=0)
    dproj1 = jnp.concatenate([dq, dk, dv, dr, dlr], axis=1)
    g_gla_in = both(lambda t: _split_chips(t[:, :GLA_IN_WIDTH], 1), _dw("l1_dwin", y1, dproj1, tn=640))
    g["gla_w_in"] = g_gla_in[0]
    send_off("gla", [g_gla_in, both(chip_major, g_gla_out)])
    dy1 = _mm("l1_dy", dproj1, w["gla_w_in_pad"], mode="nt", tk=640)
    dh, g_nmix1 = _norm_bwd("l1_dnorm", h_a1, gain1, dy1, dh, pin=pins.pop() if pins else None)

    dh, g_nmlp0, g_w1_0, g_w2_0 = _mlp_bwd("mlp0", h_b0, _row2(w["norm_mlp"][0]), w["mlp_w1"][0], w["mlp_w2"][0], mlp0, dh)
    g_ab_out = _dw("l0_dwout", mix_in0, dh)
    g["ab_w_out"] = g_ab_out[0]
    send_off("mlp0", [g_w1_0, both(chip_major, g_w2_0), both(chip_major, g_ab_out)])
    dmix0 = _mm("l0_dout", dh, w["ab_w_out"], mode="nt")
    dhs, dga, do, dg, g["hg_norm"] = _rowcall(
        "l0_dpost", _post0_bwd_fn, post0_rows + [(dmix0, d_model, 0)], [hg_gain],
        [(rg_w, F32), (rg_w, BF16), (hg_w, F32), (hg_w, BF16)], [(1, hg_w)], pin=pins.pop() if pins else None)
    (dqh_f, dk_f, div_f, dlf_f), (dqh_b, dk_b, div_b, dlf_b) = _attn_bwd(
        "hg_dattn", (qh, 0), (k_f, 0), (k_b, 0), iv, (lf_f, 0), (lf_b, 0), st_f, st_b, do, 4, 128, 128)

    def hg_pre_bwd(q, f_f, f_b, dq1, dq2, dk1, dlf1, dk2, dlf2, dv1, dv2, logits):
        _, vjp = jax.vjp(_hg_pre_fn, q, f_f, f_b, logits)
        dq, df_f, df_b, dlogits = vjp((dq1 + dq2, dk1, dlf1, dk2, dlf2))
        return dq, df_f, df_b, dv1 + dv2, dlogits

    rows = hg_rows + [(t, hg_w, 0) for t in (dqh_f, dqh_b, dk_f, dlf_f, dk_b, dlf_b, div_f, div_b)]
    dq, df_f, df_b, div, g["hg_lb_logits"] = _rowcall(
        "hg_dpre", hg_pre_bwd, rows, [w["hg_lb_logits"]], [(hg_w, BF16)] * 4, [(2, hg_w)])
    du_f, da_f = _scan_bwd("rg_dscan_f", a_f, hs_f, dhs, False)
    du_b, da_b = _scan_bwd("rg_dscan_b", a_b, hs_b, dhs, True)
    gates_bwd = _vjp_of(_rg_gates_fn, 1, 4, 5)
    rows = [(xc, rg_w, 0), (da_f, rg_w, 0), (du_f, rg_w, 0), (da_b, rg_w, 0), (du_b, rg_w, 0)]
    dxc, g["rg_wa_bd"], g["rg_wx_bd"], g["rg_b_a"], g["rg_b_x"], g["rg_lambda"] = _rowcall(
        "rg_dgates", gates_bwd, rows, gate_pars, [(rg_w, F32)],
        [(2, rg_w, rg_w), (2, rg_w, rg_w), (2, rg_w), (2, rg_w), (2, rg_w)])
    dxa, g["rg_conv_w"], g["rg_conv_b"] = _conv_bwd("rg_dconv", proj0, 0, conv_w, dxc)
    dproj0 = jnp.concatenate([dxa, dga, dq, df_f, df_b, div, dg], axis=1)
    g_ab_in = _dw("l0_dwin", y0, dproj0, out_split=N_CHIPS)
    g["ab_w_in"] = g_ab_in[0]
    send_off("ab", [g_ab_in])
    dy0 = _mm("l0_dy", dproj0, w["ab_w_in"], mode="nt")
    grad_x, g_nmix0 = _norm_bwd("l0_dnorm", h_a0, gain, dy0, dh, pin=pins.pop() if pins else None)

    g["norm_mix"] = jnp.concatenate([g_nmix0, g_nmix1], axis=0)
    g["norm_mlp"] = jnp.concatenate([g_nmlp0, g_nmlp1], axis=0)
    g["mlp_w1"] = [g_w1_0[0], g_w1_1[0]]
    g["mlp_w2"] = [g_w2_0[0], g_w2_1[0]]
    return loss, grad_x, g


def _block_diag(w):
    d, g, n, _ = w.shape
    eye = jnp.eye(g, dtype=w.dtype)
    return (w[:, :, :, None, :] * eye[None, :, None, :, None]).reshape(d, g * n, g * n)


def _block_diag_extract(wbd, g):
    d, gn, _ = wbd.shape
    n = gn // g
    blocks = wbd.reshape(d, g, n, g, n)
    return jnp.stack([blocks[:, i, :, i, :] for i in range(g)], axis=1)


def _prepare_weights(big, full):
    w = {k: full[k] for k in ("norm_mix", "norm_mlp", "norm_final", "hg_lb_logits")}
    for k in ("rg_conv_w", "rg_conv_b", "rg_b_a", "rg_b_x", "rg_lambda", "hg_norm", "gla_b_gate", "gla_norm"):
        w[k] = full[k][0]
    w["rg_wa_bd"] = _block_diag(full["rg_w_a"][0])
    w["rg_wx_bd"] = _block_diag(full["rg_w_x"][0])
    up = full["gla_w_gate_up"][0]
    rank = up.shape[1]
    pad = jnp.zeros((2, LANES, up.shape[2]), F32)
    w["gla_w_up_pad"] = pad.at[0, 0:rank].set(up[0]).at[1, rank:2 * rank].set(up[1])
    w.update(_prepare_matrices(big))
    return w


def _prepare_matrices(big):
    w = {}
    if "mlp_w1" in big:
        w["mlp_w1"] = list(big["mlp_w1"])
        w["mlp_w2"] = [t.reshape(-1, t.shape[-1]) for t in big["mlp_w2"]]
    if "ab_w_in" in big:
        w["ab_w_in"] = big["ab_w_in"]
    if "ab_w_out" in big:
        w["ab_w_out"] = big["ab_w_out"].reshape(-1, big["ab_w_out"].shape[-1])
    if "gla_w_in" in big:
        w["gla_w_out"] = big["gla_w_out"].reshape(-1, big["gla_w_out"].shape[-1])
        gla_in = _join_chips(big["gla_w_in"], 1)
        w["gla_w_in_pad"] = jnp.pad(gla_in, ((0, 0), (0, GLA_IN_PAD - gla_in.shape[1])))
    return w


def _finish_grads(g, rank=16, rg_blocks=8):
    def chip_major(t):
        return t.reshape(N_CHIPS, t.shape[0] // N_CHIPS, t.shape[1])

    big = {
        "mlp_w1": list(g["mlp_w1"]), "mlp_w2": [chip_major(t) for t in g["mlp_w2"]],
        "ab_w_in": g["ab_w_in"], "ab_w_out": chip_major(g["ab_w_out"]),
        "gla_w_in": g["gla_w_in"], "gla_w_out": chip_major(g["gla_w_out"]),
    }
    small = {
        "norm_mix": g["norm_mix"], "norm_mlp": g["norm_mlp"], "norm_final": g["norm_final"][0],
        "rg_conv_w": g["rg_conv_w"][None], "rg_conv_b": g["rg_conv_b"],
        "rg_w_a": _block_diag_extract(g["rg_wa_bd"], rg_blocks)[None], "rg_b_a": g["rg_b_a"][None],
        "rg_w_x": _block_diag_extract(g["rg_wx_bd"], rg_blocks)[None], "rg_b_x": g["rg_b_x"][None],
        "rg_lambda": g["rg_lambda"][None], "hg_lb_logits": g["hg_lb_logits"], "hg_norm": g["hg_norm"],
        "gla_w_gate_up": jnp.stack([g["gla_w_up_pad"][0, 0:rank], g["gla_w_up_pad"][1, rank:2 * rank]])[None],
        "gla_b_gate": g["gla_b_gate"][None], "gla_norm": g["gla_norm"],
    }
    return big, small


MATRICES = (("mlp_w1", 0), ("mlp_w1", 1), ("mlp_w2", 0), ("mlp_w2", 1), ("ab_w_in", 0), ("ab_w_out", 0),
            ("gla_w_in", 0), ("gla_w_out", 0))
EARLY_MATRICES = ("ab_w_in",)
SMALL_SHARDED = ("rg_conv_w", "rg_b_a", "rg_b_x", "rg_lambda", "gla_w_gate_up", "gla_b_gate", "gla_norm")
SMALL_REPLICATED = ("norm_mix", "norm_mlp", "norm_final", "rg_conv_b", "rg_w_a", "rg_w_x", "hg_lb_logits", "hg_norm")
WEIGHTS = ("norm_mix", "norm_mlp", "norm_final", "mlp_w1", "mlp_w2", "ab_w_in", "ab_w_out", "rg_conv_w", "rg_conv_b",
           "rg_w_a", "rg_b_a", "rg_w_x", "rg_b_x", "rg_lambda", "hg_lb_logits", "hg_norm", "gla_w_in", "gla_w_out",
           "gla_w_gate_up", "gla_b_gate", "gla_norm")
ROW_ALIGN = 16


def _pack(arrays, lead=0):
    head = arrays[0].shape[:lead]
    flat = jnp.concatenate([a.reshape(head + (-1,)) for a in arrays], axis=lead)
    n = flat.shape[-1]
    quantum = LANES * ROW_ALIGN
    padded = -(-n // quantum) * quantum
    if padded != n:
        flat = jnp.pad(flat, [(0, 0)] * lead + [(0, padded - n)])
    return flat.reshape(head + (padded // LANES, LANES))


def _unpack(buf, shapes, lead=0):
    head = buf.shape[:lead]
    flat = buf.reshape(head + (-1,))
    out, off = [], 0
    for s in shapes:
        n = 1
        for v in s:
            n *= v
        out.append(lax.slice_in_dim(flat, off, off + n, axis=lead).reshape(head + tuple(s)))
        off += n
    return out


def _join_chips(gathered, axis):
    t = jnp.moveaxis(gathered, 0, axis)
    return t.reshape(t.shape[:axis] + (t.shape[axis] * t.shape[axis + 1],) + t.shape[axis + 2:])


def _split_chips(full, axis):
    s = full.shape
    t = full.reshape(s[:axis] + (N_CHIPS, s[axis] // N_CHIPS) + s[axis + 1:])
    return jnp.moveaxis(t, axis, 0)


_ANY = pl.BlockSpec(memory_space=pl.ANY)


def _place():
    return lax.axis_index("x"), lax.axis_index("y"), lax.axis_index("c")


def _into_slot(name, src, slot, n_slots, dtype, tm, layer=None):
    r, lanes = src.shape[-2:]
    tm = _row_tile(r, tm, ROW_ALIGN)

    def body(slot_ref, in_ref, o_ref):
        o_ref[...] = in_ref[...].astype(o_ref.dtype)

    if layer is None:
        in_spec = pl.BlockSpec((tm, lanes), lambda i, slot_ref: (i, 0))
    else:
        in_spec = pl.BlockSpec((None, tm, lanes), lambda i, slot_ref: (layer, i, 0))
    grid_spec = pltpu.PrefetchScalarGridSpec(
        num_scalar_prefetch=1, grid=(r // tm,), in_specs=[in_spec],
        out_specs=pl.BlockSpec((None, tm, lanes), lambda i, slot_ref: (slot_ref[0], i, 0)))
    return pl.pallas_call(
        body, name=name, grid_spec=grid_spec, out_shape=jax.ShapeDtypeStruct((n_slots, r, lanes), dtype),
        compiler_params=_params(("parallel",)),
    )(slot.reshape(1).astype(jnp.int32), src)


def _chip_peers():
    x, y, c = _place()
    return 2 * x + y, c, [(1 - x, y), (x, 1 - y), (1 - x, 1 - y)]


def _comm_call(name, body, ins, out_shapes, n_sems, aliases=None):
    return pl.pallas_call(
        body, name=name, in_specs=[_ANY] * len(ins), out_specs=[_ANY] * len(out_shapes), out_shape=out_shapes,
        input_output_aliases=aliases or {},
        scratch_shapes=[pltpu.SemaphoreType.DMA((n_sems,)), pltpu.SemaphoreType.DMA((n_sems,))],
    )(*ins)


def _gather_chips(name, bufs):
    n = len(bufs)

    def body(*refs):
        outs, send_sems, recv_sems = refs[n:2 * n], refs[2 * n], refs[2 * n + 1]
        x, y, c = _place()
        me, _, peers = _chip_peers()

        def rows(a, block, half):
            rh = outs[a].shape[1] // 2
            return outs[a].at[block, pl.ds(half * rh, rh)]

        def copy(a, j, block, half, to, sem):
            return pltpu.make_async_remote_copy(
                src_ref=rows(a, block, half), dst_ref=rows(a, block, half), send_sem=send_sems.at[sem],
                recv_sem=recv_sems.at[sem], device_id=to, device_id_type=MESH)

        def over_ici(a, j, block):
            px, py = peers[j]
            return copy(a, j, block, c, (px, py, c), 6 * a + j)

        def to_sibling(a, j, block, half):
            return copy(a, j, block, half, (x, y, 1 - c), 6 * a + 3 + j)

        sends = [over_ici(a, j, me) for a in range(n) for j in range(3)]
        for cp in sends:
            cp.start()
        for a in range(n):
            for j, (px, py) in enumerate(peers):
                over_ici(a, j, 2 * px + py).wait_recv()
                handed = to_sibling(a, j, 2 * px + py, c)
                handed.start()
                sends.append(handed)
        for a in range(n):
            for j, (px, py) in enumerate(peers):
                to_sibling(a, j, 2 * px + py, 1 - c).wait_recv()
        for cp in sends:
            cp.wait_send()

    shapes = [jax.ShapeDtypeStruct(b.shape, b.dtype) for b in bufs]
    return _comm_call(name, body, bufs, shapes, 6 * n, {a: a for a in range(n)})


_HBM = pl.BlockSpec(memory_space=pltpu.HBM)
_SEM = pl.BlockSpec(memory_space=pltpu.SEMAPHORE)
_EFFECT = pltpu.SideEffectType.DATAFLOW_SIDE_EFFECTING


def _half_rows(ref, block, half):
    rh = ref.shape[1] // 2
    return ref.at[block, pl.ds(half * rh, rh)]


def _gather_start(name, bufs, after):
    n = len(bufs)

    def body(*refs):
        ins, send_sems, recv_sems, token = refs[:n], refs[n + 1], refs[n + 2], refs[-1]
        me, c, peers = _chip_peers()
        for a in range(n):
            mine = _half_rows(ins[a], me, c)
            for j, (px, py) in enumerate(peers):
                for pc in range(2):
                    pltpu.make_async_remote_copy(
                        src_ref=mine, dst_ref=mine, send_sem=send_sems.at[6 * a + 2 * j + pc],
                        recv_sem=recv_sems.at[6 * a + 2 * j + c], device_id=(px, py, pc), device_id_type=MESH).start()
        token[...] = jnp.zeros_like(token)

    out_shape = (pltpu.SemaphoreType.DMA((6 * n,)), pltpu.SemaphoreType.DMA((6 * n,)),
                 *[pltpu.HBM(b.shape, b.dtype) for b in bufs], jax.ShapeDtypeStruct((8, LANES), F32))
    return pl.pallas_call(
        body, name=name, out_shape=out_shape, in_specs=[_HBM] * n + [_ANY],
        out_specs=(_SEM, _SEM, *[_HBM] * n, pl.BlockSpec(memory_space=pltpu.VMEM)),
        input_output_aliases={a: 2 + a for a in range(n)},
        compiler_params=pltpu.CompilerParams(has_side_effects=_EFFECT),
    )(*[pltpu.with_memory_space_constraint(b, pltpu.HBM) for b in bufs], after)


def _gather_wait(name, bufs, send_sems, recv_sems, after):
    n = len(bufs)

    def body(*refs):
        ins, send_sems, recv_sems = refs[:n], refs[n], refs[n + 1]
        me, c, peers = _chip_peers()
        for a in range(n):
            for j, (px, py) in enumerate(peers):
                for pc in range(2):
                    copy = pltpu.make_async_remote_copy(
                        src_ref=_half_rows(ins[a], me, c), dst_ref=_half_rows(ins[a], 2 * px + py, pc),
                        send_sem=send_sems.at[6 * a + 2 * j + pc], recv_sem=recv_sems.at[6 * a + 2 * j + pc],
                        device_id=(px, py, pc), device_id_type=MESH)
                    copy.wait_send()
                    copy.wait_recv()

    return pl.pallas_call(
        body, name=name, out_shape=tuple(pltpu.HBM(b.shape, b.dtype) for b in bufs),
        in_specs=[_HBM] * n + [_SEM, _SEM, _ANY], out_specs=tuple([_HBM] * n),
        input_output_aliases={a: a for a in range(n)},
        compiler_params=pltpu.CompilerParams(has_side_effects=_EFFECT),
    )(*bufs, send_sems, recv_sems, after)


def _pair_gather(name, bufs):
    n = len(bufs)

    def body(*refs):
        ins, outs, send_sems, recv_sems = refs[:n], refs[n:2 * n], refs[2 * n], refs[2 * n + 1]
        x, y, c = _place()

        def copy(a, block):
            return pltpu.make_async_remote_copy(
                src_ref=ins[a].at[block], dst_ref=outs[a].at[block], send_sem=send_sems.at[a],
                recv_sem=recv_sems.at[a], device_id=(x, y, 1 - c), device_id_type=MESH)

        sends = [copy(a, c) for a in range(n)]
        for cp in sends:
            cp.start()
        for a in range(n):
            copy(a, 1 - c).wait_recv()
        for cp in sends:
            cp.wait_send()

    shapes = [jax.ShapeDtypeStruct(b.shape, b.dtype) for b in bufs]
    return _comm_call(name, body, bufs, shapes, n, {a: a for a in range(n)})


def _all_peers():
    x, y, c = _place()
    peers = []
    for mask in range(1, N_DEV):
        fx, fy, fc = (mask >> 2) & 1, (mask >> 1) & 1, mask & 1
        peers.append((jnp.where(fx, 1 - x, x), jnp.where(fy, 1 - y, y), jnp.where(fc, 1 - c, c)))
    return 4 * x + 2 * y + c, peers


def _reduce_copies(srcs, lands, send_sems, recv_sems):
    me, peers = _all_peers()
    sends, arrivals = [], []
    for a in range(len(srcs)):
        for j, (px, py, pc) in enumerate(peers):
            k = (N_DEV - 1) * a + j
            sends.append(pltpu.make_async_remote_copy(
                src_ref=srcs[a].at[2 * px + py, pc], dst_ref=lands[a].at[me], send_sem=send_sems.at[k],
                recv_sem=recv_sems.at[k], device_id=(px, py, pc), device_id_type=MESH))
            arrivals.append(pltpu.make_async_remote_copy(
                src_ref=srcs[a].at[2 * px + py, pc], dst_ref=lands[a].at[4 * px + 2 * py + pc],
                send_sem=send_sems.at[k], recv_sem=recv_sems.at[k], device_id=(px, py, pc), device_id_type=MESH))
    return sends, arrivals


def _reduce_direct(name, srcs, pin=None):
    n = len(srcs)
    extra = [] if pin is None else [pin]

    def body(*refs):
        ins, outs = refs[:n], refs[n + len(extra):2 * n + len(extra)]
        sends, arrivals = _reduce_copies(ins, outs, refs[-2], refs[-1])
        for cp in sends:
            cp.start()
        for cp in arrivals:
            cp.wait_recv()
        for cp in sends:
            cp.wait_send()

    shapes = [jax.ShapeDtypeStruct((N_DEV,) + s.shape[2:], s.dtype) for s in srcs]
    return _comm_call(name, body, list(srcs) + extra, shapes, (N_DEV - 1) * n)


def _reduce_start(name, srcs):
    n = len(srcs)
    lands = [lax.empty((N_DEV,) + s.shape[2:], s.dtype) for s in srcs]

    def body(*refs):
        sends, _ = _reduce_copies(refs[:n], refs[n:2 * n], refs[2 * n], refs[2 * n + 1])
        for cp in sends:
            cp.start()
        refs[-1][...] = jnp.zeros_like(refs[-1])

    bufs = list(srcs) + lands
    n_sems = (N_DEV - 1) * n
    out_shape = (pltpu.SemaphoreType.DMA((n_sems,)), pltpu.SemaphoreType.DMA((n_sems,)),
                 *[pltpu.HBM(b.shape, b.dtype) for b in bufs], jax.ShapeDtypeStruct((8, LANES), F32))
    return pl.pallas_call(
        body, name=name, out_shape=out_shape, in_specs=[_HBM] * (2 * n),
        out_specs=(_SEM, _SEM, *[_HBM] * (2 * n), pl.BlockSpec(memory_space=pltpu.VMEM)),
        input_output_aliases={a: 2 + a for a in range(2 * n)},
        compiler_params=pltpu.CompilerParams(has_side_effects=_EFFECT),
    )(*[pltpu.with_memory_space_constraint(b, pltpu.HBM) for b in bufs])


def _reduce_wait(name, srcs, lands, send_sems, recv_sems, after):
    n = len(srcs)

    def body(*refs):
        sends, arrivals = _reduce_copies(refs[:n], refs[n:2 * n], refs[2 * n], refs[2 * n + 1])
        for cp in sends:
            cp.wait_send()
        for cp in arrivals:
            cp.wait_recv()

    bufs = list(srcs) + list(lands)
    outs = pl.pallas_call(
        body, name=name, out_shape=tuple(pltpu.HBM(b.shape, b.dtype) for b in bufs),
        in_specs=[_HBM] * (2 * n) + [_SEM, _SEM, _ANY], out_specs=tuple([_HBM] * (2 * n)),
        input_output_aliases={a: a for a in range(2 * n)},
        compiler_params=pltpu.CompilerParams(has_side_effects=_EFFECT),
    )(*bufs, send_sems, recv_sems, after)
    return list(outs[n:])


def _reduce_sum(name, own, land, chip, core):
    n, rh, lanes = land.shape
    tm = _row_tile(rh, 1024, ROW_ALIGN)

    def body(idx_ref, own_ref, *rest):
        total = own_ref[...]
        for g_ref in rest[:-1]:
            total = total + g_ref[...].astype(F32)
        rest[-1][...] = total

    def block(k):
        return pl.BlockSpec((None, tm, lanes), lambda i, idx_ref: ((2 * idx_ref[0] + idx_ref[1] + k) % n, i, 0))

    grid_spec = pltpu.PrefetchScalarGridSpec(
        num_scalar_prefetch=1, grid=(rh // tm,),
        in_specs=[pl.BlockSpec((None, None, tm, lanes), lambda i, idx_ref: (idx_ref[0], idx_ref[1], i, 0))]
        + [block(k) for k in range(1, n)],
        out_specs=pl.BlockSpec((None, tm, lanes), lambda i, idx_ref: (idx_ref[1], i, 0)))
    return pl.pallas_call(
        body, name=name, grid_spec=grid_spec, out_shape=jax.ShapeDtypeStruct((2, rh, lanes), F32),
        compiler_params=_params(("parallel",)),
    )(jnp.stack([chip, core]).astype(jnp.int32), own, *[land] * (n - 1))


def _gather_all_start(name, buf):
    def body(in_ref, send_sems, recv_sems, out_ref, token):
        me, peers = _all_peers()
        for j, peer in enumerate(peers):
            pltpu.make_async_remote_copy(
                src_ref=in_ref.at[me], dst_ref=in_ref.at[me], send_sem=send_sems.at[j], recv_sem=recv_sems.at[j],
                device_id=peer, device_id_type=MESH).start()
        token[...] = jnp.zeros_like(token)

    n = N_DEV - 1
    return pl.pallas_call(
        body, name=name, in_specs=[_HBM],
        out_shape=(pltpu.SemaphoreType.DMA((n,)), pltpu.SemaphoreType.DMA((n,)), pltpu.HBM(buf.shape, buf.dtype),
                   jax.ShapeDtypeStruct((8, LANES), F32)),
        out_specs=(_SEM, _SEM, _HBM, pl.BlockSpec(memory_space=pltpu.VMEM)), input_output_aliases={0: 2},
        compiler_params=pltpu.CompilerParams(has_side_effects=_EFFECT),
    )(pltpu.with_memory_space_constraint(buf, pltpu.HBM))


def _gather_all_wait(name, buf, send_sems, recv_sems, after):
    def body(in_ref, send_sems, recv_sems, after_ref, out_ref):
        me, peers = _all_peers()
        for j, (px, py, pc) in enumerate(peers):
            copy = pltpu.make_async_remote_copy(
                src_ref=in_ref.at[me], dst_ref=in_ref.at[4 * px + 2 * py + pc], send_sem=send_sems.at[j],
                recv_sem=recv_sems.at[j], device_id=(px, py, pc), device_id_type=MESH)
            copy.wait_send()
            copy.wait_recv()

    return pl.pallas_call(
        body, name=name, in_specs=[_HBM, _SEM, _SEM, _ANY], out_shape=pltpu.HBM(buf.shape, buf.dtype),
        out_specs=_HBM, input_output_aliases={0: 0},
        compiler_params=pltpu.CompilerParams(has_side_effects=_EFFECT),
    )(buf, send_sems, recv_sems, after)


def _sum_blocks(name, stacked, tm):
    n, r, lanes = stacked.shape

    def body(in_ref, o_ref):
        acc = in_ref[0]
        for j in range(1, n):
            acc = acc + in_ref[j]
        o_ref[...] = acc

    return pl.pallas_call(
        body, name=name, grid=(r // tm,), in_specs=[pl.BlockSpec((n, tm, lanes), lambda i: (0, i, 0))],
        out_specs=pl.BlockSpec((tm, lanes), lambda i: (i, 0)), out_shape=jax.ShapeDtypeStruct((r, lanes), F32),
        compiler_params=_params(("parallel",)),
    )(stacked)


def _row_tile(rows, pref, align):
    best = None
    for t in range(align, min(rows, pref) + 1, align):
        if rows % t == 0:
            best = t
    assert best is not None, (rows, pref, align)
    return best


def _adam(name, w, g, m, v):
    rows, width = w.shape
    tm = _row_tile(rows, max(8, 4096 * LANES // width), 8)
    args = [(t, width, 0) for t in (w, g, m, v)]
    return _rowcall(name, _adam_fn, args, [], [(width, F32)] * 3, tm=tm)


def kernel(x, norm_mix, norm_mlp, norm_final, mlp_w1, mlp_w2, ab_w_in, ab_w_out, rg_conv_w, rg_conv_b, rg_w_a, rg_b_a, rg_w_x, rg_b_x, rg_lambda, hg_lb_logits, hg_norm, gla_w_in, gla_w_out, gla_w_gate_up, gla_b_gate, gla_norm, loss_target, m_norm_mix, m_norm_mlp, m_norm_final, m_mlp_w1, m_mlp_w2, m_ab_w_in, m_ab_w_out, m_rg_conv_w, m_rg_conv_b, m_rg_w_a, m_rg_b_a, m_rg_w_x, m_rg_b_x, m_rg_lambda, m_hg_lb_logits, m_hg_norm, m_gla_w_in, m_gla_w_out, m_gla_w_gate_up, m_gla_b_gate, m_gla_norm, v_norm_mix, v_norm_mlp, v_norm_final, v_mlp_w1, v_mlp_w2, v_ab_w_in, v_ab_w_out, v_rg_conv_w, v_rg_conv_b, v_rg_w_a, v_rg_b_a, v_rg_w_x, v_rg_b_x, v_rg_lambda, v_hg_lb_logits, v_hg_norm, v_gla_w_in, v_gla_w_out, v_gla_w_gate_up, v_gla_b_gate, v_gla_norm):
    w = dict(norm_mix=norm_mix, norm_mlp=norm_mlp, norm_final=norm_final, mlp_w1=mlp_w1, mlp_w2=mlp_w2, ab_w_in=ab_w_in, ab_w_out=ab_w_out, rg_conv_w=rg_conv_w, rg_conv_b=rg_conv_b, rg_w_a=rg_w_a, rg_b_a=rg_b_a, rg_w_x=rg_w_x, rg_b_x=rg_b_x, rg_lambda=rg_lambda, hg_lb_logits=hg_lb_logits, hg_norm=hg_norm, gla_w_in=gla_w_in, gla_w_out=gla_w_out, gla_w_gate_up=gla_w_gate_up, gla_b_gate=gla_b_gate, gla_norm=gla_norm)
    m = dict(norm_mix=m_norm_mix, norm_mlp=m_norm_mlp, norm_final=m_norm_final, mlp_w1=m_mlp_w1, mlp_w2=m_mlp_w2, ab_w_in=m_ab_w_in, ab_w_out=m_ab_w_out, rg_conv_w=m_rg_conv_w, rg_conv_b=m_rg_conv_b, rg_w_a=m_rg_w_a, rg_b_a=m_rg_b_a, rg_w_x=m_rg_w_x, rg_b_x=m_rg_b_x, rg_lambda=m_rg_lambda, hg_lb_logits=m_hg_lb_logits, hg_norm=m_hg_norm, gla_w_in=m_gla_w_in, gla_w_out=m_gla_w_out, gla_w_gate_up=m_gla_w_gate_up, gla_b_gate=m_gla_b_gate, gla_norm=m_gla_norm)
    v = dict(norm_mix=v_norm_mix, norm_mlp=v_norm_mlp, norm_final=v_norm_final, mlp_w1=v_mlp_w1, mlp_w2=v_mlp_w2, ab_w_in=v_ab_w_in, ab_w_out=v_ab_w_out, rg_conv_w=v_rg_conv_w, rg_conv_b=v_rg_conv_b, rg_w_a=v_rg_w_a, rg_b_a=v_rg_b_a, rg_w_x=v_rg_w_x, rg_b_x=v_rg_b_x, rg_lambda=v_rg_lambda, hg_lb_logits=v_hg_lb_logits, hg_norm=v_hg_norm, gla_w_in=v_gla_w_in, gla_w_out=v_gla_w_out, gla_w_gate_up=v_gla_w_gate_up, gla_b_gate=v_gla_b_gate, gla_norm=v_gla_norm)
    chip = 2 * lax.axis_index("x") + lax.axis_index("y")
    core = lax.axis_index("c")
    sharded_shapes = [w[n].shape for n in SMALL_SHARDED]

    slots = [_into_slot(f"cast_{n}{layer}", w[n], chip, N_CHIPS, BF16, 512, layer) for n, layer in MATRICES]
    early = [i for i, (n, _) in enumerate(MATRICES) if n in EARLY_MATRICES]
    rest = [i for i in range(len(MATRICES)) if i not in early]

    def named(indices, arrays):
        big = {}
        for i, t in zip(indices, arrays):
            big.setdefault(MATRICES[i][0], []).append(t)
        return {n: (v if n in ("mlp_w1", "mlp_w2") else v[0]) for n, v in big.items()}

    gathered = _gather_chips("gather_early", [slots[i] for i in early])
    send_sems, recv_sems, *in_flight, token = _gather_start("gather_rest_start", [slots[i] for i in rest], gathered[0])

    def late_weights(after):
        landed = _gather_wait("gather_rest_wait", in_flight, send_sems, recv_sems, after)
        return _prepare_matrices(named(rest, list(landed)))

    big = named(early, gathered)
    vectors = _pack([w[n] for n in SMALL_SHARDED])
    vectors = _into_slot("place_vectors", vectors, chip, N_CHIPS, F32, vectors.shape[0])
    small_all = _unpack(_gather_chips("gather_vectors", [vectors])[0], sharded_shapes, lead=1)
    full = {n: w[n] for n in SMALL_REPLICATED}
    for n, t in zip(SMALL_SHARDED, small_all):
        full[n] = _join_chips(t, t.ndim - 2)

    def halves(t):
        return t.reshape(N_CHIPS, 2, t.shape[1] // 2, t.shape[2])

    in_flight_grads = {}

    def emit(tag, arrays32, arrays16):
        n = len(arrays16)
        send, recv, *rest = _reduce_start(f"reduce_{tag}_start", [halves(t) for t in arrays16])
        in_flight_grads[tag] = ([halves(t) for t in arrays32], rest[:n], rest[n:2 * n], send, recv)
        return rest[-1]

    loss_part, grad_x, g_kernel = _local_step(
        x[0], loss_target[0], _prepare_weights(big, full), token, late_weights, emit)
    g_big, g_full = _finish_grads(g_kernel)

    small_names = SMALL_REPLICATED + SMALL_SHARDED
    reduced_shapes = [g_full[n].shape for n in small_names] + [loss_part.shape]
    g_small = _pack([g_full[n] for n in small_names] + [loss_part])
    device = 2 * chip + core
    g_small = _into_slot("place_small", g_small, device, N_DEV, F32, g_small.shape[0])
    small_send, small_recv, small_in_flight, small_token = _gather_all_start("reduce_small_start", g_small)

    mine = {}
    for tag, (own, srcs, lands, send, recv) in in_flight_grads.items():
        landed = _reduce_wait(f"reduce_{tag}_wait", srcs, lands, send, recv, small_token)
        mine[tag] = [_reduce_sum(f"reduce_add_{tag}{i}", o, f, chip, core) for i, (o, f) in enumerate(zip(own, landed))]
    ordered = [mine["mlp0"][0], mine["mlp1"][0], mine["mlp0"][1], mine["mlp1"][1], mine["ab"][0], mine["mlp0"][2],
               *mine["gla"]]
    reduced = [t.reshape(2 * t.shape[1], t.shape[2]) for t in _pair_gather("reduce_share", ordered)]
    by_name = {n: [] for n, _ in MATRICES}
    for (n, _), t in zip(MATRICES, reduced):
        by_name[n].append(t)
    grads = {n: jnp.stack(v) for n, v in by_name.items()}

    g_small_all = _gather_all_wait("reduce_small_wait", small_in_flight, small_send, small_recv, reduced[0])
    g_small_red = _sum_blocks("reduce_small_add", g_small_all, g_small_all.shape[1])
    *small_red, loss_sum = _unpack(g_small_red, reduced_shapes)
    loss = loss_sum[0, 0]
    g_small_full = dict(zip(small_names, small_red))
    for n in SMALL_REPLICATED:
        grads[n] = g_small_full[n]
    for n in SMALL_SHARDED:
        width = w[n].shape[-1]
        grads[n] = lax.dynamic_slice_in_dim(g_small_full[n], chip * width, width, axis=g_small_full[n].ndim - 1)

    delta, new_m, new_v = {}, {}, {}
    for n in by_name:
        flat = [t.reshape(-1, t.shape[-1]) for t in (w[n], grads[n], m[n], v[n])]
        for dst, t in zip((delta, new_m, new_v), _adam(f"adam_{n}", *flat)):
            dst[n] = t.reshape(w[n].shape)
    small_shapes = [w[n].shape for n in small_names]
    packs = [_pack([src[n] for n in small_names]) for src in (w, grads, m, v)]
    d_small, m_small, v_small = _adam("adam_small", *packs)
    for dst, buf in ((delta, d_small), (new_m, m_small), (new_v, v_small)):
        dst.update(zip(small_names, _unpack(buf, small_shapes)))

    return (loss, grad_x[None], *[grads[n] for n in WEIGHTS], *[delta[n] for n in WEIGHTS],
            *[new_m[n] for n in WEIGHTS], *[new_v[n] for n in WEIGHTS])
```

```python
import functools

import jax
import jax.numpy as jnp
from jax import lax
from jax.experimental import pallas as pl
from jax.experimental.pallas import tpu as pltpu

F32 = jnp.float32
BF16 = jnp.bfloat16
MESH = pl.DeviceIdType.MESH

LANES = 128
CHUNK = 64
ATTN_SUB = 4
EPS = 1e-6
RG_C = 8.0
N_CHIPS = 4
N_DEV = 8
GLA_IN_WIDTH = 3104
GLA_IN_PAD = 3200
VMEM_LIMIT = 56 * 1024 * 1024

ADAM_LR = 0.001
ADAM_B1 = 0.9
ADAM_B2 = 0.999
ADAM_EPS = 1e-08
ADAM_WD = 0.01
ADAM_STEP = 10


def _raw_dot(a, b, ca, cb):
    return lax.dot_general(a.astype(BF16), b.astype(BF16), (((ca,), (cb,)), ((), ())),
                           preferred_element_type=F32)


def _raw_nn(a, b):
    return _raw_dot(a, b, 1, 0)


def _raw_nt(a, b):
    return _raw_dot(a, b, 1, 1)


def _raw_tn(a, b):
    return _raw_dot(a, b, 0, 0)


@jax.custom_vjp
def _dot_nn(a, b):
    return _raw_nn(a, b)


def _dot_nn_fwd(a, b):
    return _raw_nn(a, b), (a, b)


def _dot_nn_bwd(res, g):
    a, b = res
    return _raw_nt(g, b), _raw_tn(a, g)


_dot_nn.defvjp(_dot_nn_fwd, _dot_nn_bwd)


@jax.custom_vjp
def _dot_nt(a, b):
    return _raw_nt(a, b)


def _dot_nt_fwd(a, b):
    return _raw_nt(a, b), (a, b)


def _dot_nt_bwd(res, g):
    a, b = res
    return _raw_nn(g, b), _raw_tn(g, a)


_dot_nt.defvjp(_dot_nt_fwd, _dot_nt_bwd)


@jax.custom_vjp
def _dot_tn(a, b):
    return _raw_tn(a, b)


def _dot_tn_fwd(a, b):
    return _raw_tn(a, b), (a, b)


def _dot_tn_bwd(res, g):
    a, b = res
    return _raw_nt(b, g), _raw_nn(a, g)


_dot_tn.defvjp(_dot_tn_fwd, _dot_tn_bwd)


def _tile(n, pref):
    if n <= pref:
        return n
    t = (pref // LANES) * LANES
    while t > LANES and n % t:
        t -= LANES
    assert n % t == 0, (n, pref)
    return t


def _params(sem):
    return pltpu.CompilerParams(dimension_semantics=sem, vmem_limit_bytes=VMEM_LIMIT)


def _rowcall(name, fn, rows, pars, row_outs, par_outs=(), tm=256, pin=None):
    if pin is not None:
        inner, pars = fn, list(pars) + [pin]
        fn = lambda *vals: inner(*vals[:-1])
    n_rows = rows[0][0].shape[0]
    tm = min(tm, n_rows)
    assert n_rows % tm == 0
    n_r, n_p, n_ro = len(rows), len(pars), len(row_outs)

    def body(*refs):
        vals = [r[...].astype(F32) for r in refs[:n_r + n_p]]
        outs = fn(*vals)
        o_refs = refs[n_r + n_p:n_r + n_p + n_ro]
        po_refs = refs[n_r + n_p + n_ro:]
        for o_ref, val in zip(o_refs, outs[:n_ro]):
            o_ref[...] = val.astype(o_ref.dtype)
        first = pl.program_id(0) == 0
        for po_ref, val in zip(po_refs, outs[n_ro:]):
            @pl.when(first)
            def _():
                po_ref[...] = val

            @pl.when(jnp.logical_not(first))
            def _():
                po_ref[...] += val

    def const_map(nd):
        return lambda i: (0,) * nd

    def row_spec(w, cb):
        return pl.BlockSpec((tm, w), lambda i: (i, cb))

    in_specs = [row_spec(w, cb) for _, w, cb in rows]
    in_specs += [pl.BlockSpec(p.shape, const_map(p.ndim)) for p in pars]
    out_specs = [pl.BlockSpec((tm, w), lambda i: (i, 0)) for w, _ in row_outs]
    out_specs += [pl.BlockSpec(tuple(s), const_map(len(s))) for s in par_outs]
    out_shape = [jax.ShapeDtypeStruct((n_rows, w), dt) for w, dt in row_outs]
    out_shape += [jax.ShapeDtypeStruct(tuple(s), F32) for s in par_outs]
    return pl.pallas_call(
        body, name=name, grid=(n_rows // tm,), in_specs=in_specs, out_specs=out_specs, out_shape=out_shape,
        compiler_params=_params(("arbitrary",) if par_outs else ("parallel",)),
    )(*[r[0] for r in rows], *pars)


def _vjp_of(fn, n_prim, n_out, n_par, n_pass=0):
    def bwd(*args):
        prim = args[:n_prim]
        cts = args[n_prim:n_prim + n_out]
        passes = args[n_prim + n_out:n_prim + n_out + 2 * n_pass]
        pars = args[n_prim + n_out + 2 * n_pass:]
        _, vjp = jax.vjp(fn, *prim, *pars)
        grads = vjp(tuple(cts))
        sums = tuple(passes[2 * i] + passes[2 * i + 1] for i in range(n_pass))
        return tuple(grads[:n_prim]) + sums + tuple(grads[n_prim:])
    return bwd


def _mm(name, a, b, mode="nn", extras=(), epi=None, out_dtypes=(F32,), a_pro=None, out_split=None,
        epi_pars=(), row_sum=False, pin=None, tm=1024, tn=1024, tk=1024):
    split = b.shape[0] if b.ndim == 3 else None
    b_rows, b_cols = b.shape[-2:]
    if mode == "nn":
        (m, k), n = a.shape, b_cols * (split or 1)
    elif mode == "nt":
        (m, k), n = a.shape, b_rows
        assert k == b_cols * (split or 1)
    else:
        assert split is None
        (k, m), n = a.shape, b_cols
    tm, tk = _tile(m, tm), _tile(k, tk)
    tn = _tile(n // out_split, tn) if out_split else _tile(n, tn)
    if split and mode == "nn":
        tn = _tile(b_cols, tn)
    if split and mode == "nt":
        tk = _tile(b_cols, tk)
    nk = k // tk
    raw = {"nn": _raw_nn, "nt": _raw_nt, "tn": _raw_tn}[mode]
    n_e, n_p, n_o = len(extras), len(epi_pars), len(out_dtypes)
    n_in = n_e + n_p + (0 if pin is None else 1)
    if epi is None:
        epi = lambda acc: (acc,)

    def body(a_ref, b_ref, *rest):
        e_refs, p_refs, o_refs = rest[:n_e], rest[n_e:n_e + n_p], rest[n_in:n_in + n_o]
        kk = pl.program_id(2)
        a_tile = a_ref[...] if a_pro is None else a_pro(a_ref[...].astype(F32))
        part = raw(a_tile, b_ref[...])

        def finish(total):
            res = epi(total, *[e[...].astype(F32) for e in e_refs], *[p[...] for p in p_refs])
            for o_ref, r in zip(o_refs, res):
                o_ref[...] = r.astype(o_ref.dtype)
            if row_sum:
                rest[n_in + n_o][...] = res[n_o]

        if nk == 1:
            finish(part)
            return
        acc = rest[-1]

        @pl.when(kk == 0)
        def _():
            acc[...] = part

        @pl.when((kk > 0) & (kk < nk - 1))
        def _():
            acc[...] += part

        @pl.when(kk == nk - 1)
        def _():
            finish(acc[...] + part)

    a_spec = pl.BlockSpec((tk, tm), lambda i, j, kk: (kk, i)) if mode == "tn" else pl.BlockSpec((tm, tk), lambda i, j, kk: (i, kk))
    if split and mode == "nn":
        per = b_cols // tn
        b_spec = pl.BlockSpec((None, tk, tn), lambda i, j, kk: (j // per, kk, j % per))
    elif split:
        per = b_cols // tk
        b_spec = pl.BlockSpec((None, tn, tk), lambda i, j, kk: (kk // per, j, kk % per))
    elif mode == "nt":
        b_spec = pl.BlockSpec((tn, tk), lambda i, j, kk: (j, kk))
    else:
        b_spec = pl.BlockSpec((tk, tn), lambda i, j, kk: (kk, j))
    mn_spec = pl.BlockSpec((tm, tn), lambda i, j, kk: (i, j))
    if out_split:
        assert not extras
        per_out = n // out_split // tn
        out_spec = pl.BlockSpec((None, tm, tn), lambda i, j, kk: (j // per_out, i, j % per_out))
        out_shapes = [jax.ShapeDtypeStruct((out_split, m, n // out_split), dt) for dt in out_dtypes]
    else:
        out_spec = mn_spec
        out_shapes = [jax.ShapeDtypeStruct((m, n), dt) for dt in out_dtypes]
    out_specs = [out_spec] * n_o
    if row_sum:
        out_specs.append(pl.BlockSpec((None, 1, tn), lambda i, j, kk: (i, 0, j)))
        out_shapes.append(jax.ShapeDtypeStruct((m // tm, 1, n), F32))
    in_specs = [a_spec, b_spec] + [mn_spec] * n_e
    in_specs += [pl.BlockSpec(p.shape, functools.partial(lambda i, j, kk, nd: (0,) * nd, nd=p.ndim)) for p in epi_pars]
    in_specs += [] if pin is None else [pl.BlockSpec(memory_space=pl.ANY)]
    outs = pl.pallas_call(
        body, name=name, grid=(m // tm, n // tn, nk), in_specs=in_specs, out_specs=out_specs, out_shape=out_shapes,
        scratch_shapes=[pltpu.VMEM((tm, tn), F32)] if nk > 1 else [],
        compiler_params=_params(("parallel", "parallel", "arbitrary")),
    )(a, b, *extras, *epi_pars, *([] if pin is None else [pin]))
    return outs[0] if len(outs) == 1 else outs


def _sigmoid(x):
    return jax.nn.sigmoid(x)


def _silu(x):
    return x * _sigmoid(x)


def _softplus(x):
    return jnp.maximum(x, 0.0) + jnp.log1p(jnp.exp(-jnp.abs(x)))


def _rmsnorm_fn(x, gain):
    return (x * lax.rsqrt(jnp.mean(x * x, axis=-1, keepdims=True) + EPS) * gain,)


def _head_norm(o, gain, n_heads):
    w = o.shape[-1] // n_heads
    parts = []
    for h in range(n_heads):
        oh = o[:, h * w:(h + 1) * w]
        parts.append(oh * lax.rsqrt(jnp.mean(oh * oh, axis=-1, keepdims=True) + EPS))
    return jnp.concatenate(parts, axis=-1) * gain


@jax.custom_jvp
def _neg_expm1(x):
    u = jnp.exp(x)
    is_one = u == 1.0
    return jnp.where(is_one, -x, (1.0 - u) * x / jnp.log(jnp.where(is_one, 2.0, u)))


@_neg_expm1.defjvp
def _neg_expm1_jvp(primals, tangents):
    (x,), (t,) = primals, tangents
    return _neg_expm1(x), -jnp.exp(x) * t


def _rg_gates_fn(xc, wa, wx, ba, bx, lam):
    outs = []
    for d in range(2):
        r = _sigmoid(_dot_nn(xc, wa[d]) + ba[d:d + 1])
        i = _sigmoid(_dot_nn(xc, wx[d]) + bx[d:d + 1])
        log_a = -RG_C * r * _softplus(-lam[d:d + 1])
        outs.append(jnp.exp(log_a))
        outs.append(jnp.sqrt(_neg_expm1(2.0 * log_a)) * (i * xc))
    return tuple(outs)


def _hg_pre_fn(q, f_f, f_b, logits):
    mx = jnp.maximum(logits[0:1], logits[1:2])
    e0 = jnp.exp(logits[0:1] - mx)
    e1 = jnp.exp(logits[1:2] - mx)
    lb = e0 / (e0 + e1)
    outs = [_silu(q)]
    for f in (f_f, f_b):
        outs.append((1.0 - lb) * _sigmoid(-f))
        outs.append(jnp.log(lb + (1.0 - lb) * _sigmoid(f)))
    return tuple(outs)


def _post0_fn(hs, ga, o, g, gain):
    ya = hs * jax.nn.gelu(ga, approximate=True)
    yb = _head_norm(o, gain, 4) * _silu(g)
    return (jnp.concatenate([ya, yb], axis=-1),)


def _post0_fwd_fn(h_f, h_b, ga, o_f, o_b, g, gain):
    return _post0_fn(h_f + h_b, ga, o_f + o_b, g, gain)


def _post0_bwd_fn(h_f, h_b, ga, o_f, o_b, g, dmix, gain):
    _, vjp = jax.vjp(_post0_fn, h_f + h_b, ga, o_f + o_b, g, gain)
    return vjp((dmix,))


def _gla_pre_fn(q, lr, w_up, b_gate):
    outs = [q * (128.0 ** -0.5)]
    for d in range(2):
        z = _dot_nn(lr, w_up[d]) + b_gate[d:d + 1]
        outs.append(-_softplus(-z) * (1.0 / 16.0))
    return tuple(outs)


def _gla_post_fn(o, r, gain):
    return (_head_norm(o, gain, 4) * _silu(r),)


def _gla_post_fwd_fn(o_f, o_b, r, gain):
    return _gla_post_fn(o_f + o_b, r, gain)


def _gla_post_bwd_fn(o_f, o_b, r, dmix, gain):
    _, vjp = jax.vjp(_gla_post_fn, o_f + o_b, r, gain)
    return vjp((dmix,))


def _relu2_bwd_epi(acc, hid):
    return (acc * 2.0 * jnp.maximum(hid, 0.0),)


def _relu2(x):
    r = jnp.maximum(x, 0.0)
    return r * r


def _add_epi(acc, res):
    return (acc + res,)


def _loss_head_fn(h, target, gain):
    def f(h, gain):
        y = _rmsnorm_fn(h, gain)[0]
        err = y - target
        return 0.5 * jnp.sum(jnp.mean(err * err, axis=-1, keepdims=True))
    loss, (dh, dgain) = jax.value_and_grad(f, argnums=(0, 1))(h, gain)
    return dh, jnp.full((1, LANES), loss, F32), dgain


def _adam_fn(w, g, m, v):
    m2 = ADAM_B1 * m + (1.0 - ADAM_B1) * g
    v2 = ADAM_B2 * v + (1.0 - ADAM_B2) * (g * g)
    m_hat = m2 / (1.0 - ADAM_B1 ** ADAM_STEP)
    v_hat = v2 / (1.0 - ADAM_B2 ** ADAM_STEP)
    delta = -ADAM_LR * (m_hat / (jnp.sqrt(v_hat) + ADAM_EPS) + ADAM_WD * w)
    return delta, m2, v2


def _shifted(x, t_idx, off):
    n = x.shape[0]
    rolled = pltpu.roll(x, (-off) % n, 0)
    valid = (t_idx + off >= 0) & (t_idx + off < n)
    return jnp.where(valid, rolled, 0.0)


def _conv_fwd(name, src, colblock, w, b):
    n_rows, width = src.shape[0], w.shape[1]

    def body(x_ref, w_ref, b_ref, o_ref):
        x = x_ref[...]
        t_idx = lax.broadcasted_iota(jnp.int32, x.shape, 0)
        acc = b_ref[...] + w_ref[2:3, :] * x
        acc += w_ref[0:1, :] * _shifted(x, t_idx, -2)
        acc += w_ref[1:2, :] * _shifted(x, t_idx, -1)
        acc += w_ref[3:4, :] * _shifted(x, t_idx, 1)
        o_ref[...] = acc

    nb = width // LANES
    return pl.pallas_call(
        body, name=name, grid=(nb,),
        in_specs=[pl.BlockSpec((n_rows, LANES), lambda j: (0, colblock * nb + j)),
                  pl.BlockSpec((4, LANES), lambda j: (0, j)), pl.BlockSpec((1, LANES), lambda j: (0, j))],
        out_specs=pl.BlockSpec((n_rows, LANES), lambda j: (0, j)),
        out_shape=jax.ShapeDtypeStruct((n_rows, width), F32),
        compiler_params=_params(("parallel",)),
    )(src, w, b)


def _conv_bwd(name, src, colblock, w, d):
    n_rows, width = src.shape[0], w.shape[1]

    def body(x_ref, w_ref, d_ref, dx_ref, dw_ref, db_ref):
        x = x_ref[...]
        g = d_ref[...]
        t_idx = lax.broadcasted_iota(jnp.int32, x.shape, 0)
        dx = w_ref[2:3, :] * g
        dx += w_ref[0:1, :] * _shifted(g, t_idx, 2)
        dx += w_ref[1:2, :] * _shifted(g, t_idx, 1)
        dx += w_ref[3:4, :] * _shifted(g, t_idx, -1)
        dx_ref[...] = dx.astype(dx_ref.dtype)
        dw_ref[0:1, :] = jnp.sum(g * _shifted(x, t_idx, -2), axis=0, keepdims=True)
        dw_ref[1:2, :] = jnp.sum(g * _shifted(x, t_idx, -1), axis=0, keepdims=True)
        dw_ref[2:3, :] = jnp.sum(g * x, axis=0, keepdims=True)
        dw_ref[3:4, :] = jnp.sum(g * _shifted(x, t_idx, 1), axis=0, keepdims=True)
        db_ref[...] = jnp.sum(g, axis=0, keepdims=True)

    nb = width // LANES
    return pl.pallas_call(
        body, name=name, grid=(nb,),
        in_specs=[pl.BlockSpec((n_rows, LANES), lambda j: (0, colblock * nb + j)),
                  pl.BlockSpec((4, LANES), lambda j: (0, j)),
                  pl.BlockSpec((n_rows, LANES), lambda j: (0, j))],
        out_specs=[pl.BlockSpec((n_rows, LANES), lambda j: (0, j)), pl.BlockSpec((4, LANES), lambda j: (0, j)),
                   pl.BlockSpec((1, LANES), lambda j: (0, j))],
        out_shape=[jax.ShapeDtypeStruct((n_rows, width), BF16), jax.ShapeDtypeStruct((4, width), F32),
                   jax.ShapeDtypeStruct((1, width), F32)],
        compiler_params=_params(("parallel",)),
    )(src, w, d)


SUBLANES = 8
SCAN_UNROLL = 8


def _shift_rows(x, d, fill):
    n = x.shape[0]
    t = lax.broadcasted_iota(jnp.int32, x.shape, 0)
    valid = (t >= d) if d > 0 else (t < n + d)
    return jnp.where(valid, pltpu.roll(x, d % n, 0), fill)


def _tile_scan(a, u, reverse):
    d = 1
    while d < a.shape[0]:
        s = -d if reverse else d
        a_sh, u_sh = _shift_rows(a, s, 1.0), _shift_rows(u, s, 0.0)
        u = u + a * u_sh
        a = a * a_sh
        d *= 2
    return a, u


def _edge_row(x, reverse):
    return x[0:1, :] if reverse else x[SUBLANES - 1:SUBLANES, :]


def _scan_specs(n_rows, n):
    return [pl.BlockSpec((n_rows, LANES), lambda j: (0, j))] * n


def _scan_fwd(name, a, u, reverse):
    n_rows, width = a.shape
    n_tiles = n_rows // SUBLANES

    def body(a_ref, u_ref, h_ref):
        def step(i, carry):
            tile = (n_tiles - 1 - i) if reverse else i
            rows = pl.ds(pl.multiple_of(tile * SUBLANES, SUBLANES), SUBLANES)
            acc_a, acc_u = _tile_scan(a_ref[rows, :], u_ref[rows, :], reverse)
            h = acc_u + acc_a * carry
            h_ref[rows, :] = h
            return _edge_row(h, reverse)
        lax.fori_loop(0, n_tiles, step, jnp.zeros((1, LANES), F32), unroll=SCAN_UNROLL)

    return pl.pallas_call(
        body, name=name, grid=(width // LANES,), in_specs=_scan_specs(n_rows, 2), out_specs=_scan_specs(n_rows, 1)[0],
        out_shape=jax.ShapeDtypeStruct((n_rows, width), F32), compiler_params=_params(("parallel",)),
    )(a, u)


def _scan_bwd(name, a, h, dh, reverse):
    n_rows, width = a.shape
    n_tiles = n_rows // SUBLANES
    against = not reverse
    one = -1 if against else 1

    def body(a_ref, h_ref, dh_ref, du_ref, da_ref):
        def step(i, carry):
            g_in, a_edge = carry
            tile = (n_tiles - 1 - i) if against else i
            start = pl.multiple_of(tile * SUBLANES, SUBLANES)
            rows = pl.ds(start, SUBLANES)
            a_tile = a_ref[rows, :]
            coeff = _shift_rows(a_tile, one, a_edge)
            acc_a, acc_u = _tile_scan(coeff, dh_ref[rows, :], against)
            g = acc_u + acc_a * g_in
            du_ref[rows, :] = g
            outside = (start + SUBLANES) if reverse else (start - 1)
            inside = (outside >= 0) & (outside < n_rows)
            h_edge = jnp.where(inside, h_ref[pl.ds(jnp.clip(outside, 0, n_rows - 1), 1), :], 0.0)
            da_ref[rows, :] = g * _shift_rows(h_ref[rows, :], -one, h_edge)
            return _edge_row(g, against), _edge_row(a_tile, against)
        zero = jnp.zeros((1, LANES), F32)
        lax.fori_loop(0, n_tiles, step, (zero, zero), unroll=SCAN_UNROLL)

    return pl.pallas_call(
        body, name=name, grid=(width // LANES,), in_specs=_scan_specs(n_rows, 3), out_specs=_scan_specs(n_rows, 2),
        out_shape=[jax.ShapeDtypeStruct((n_rows, width), F32)] * 2, compiler_params=_params(("parallel",)),
    )(a, h, dh)


def _tri_mask(c, reverse):
    row = lax.broadcasted_iota(jnp.int32, (c, c), 0)
    col = lax.broadcasted_iota(jnp.int32, (c, c), 1)
    return (col >= row) if reverse else (col <= row)


def _cumsum_rows(x, reverse):
    tri = _tri_mask(x.shape[0], reverse).astype(BF16)
    hi = x.astype(BF16)
    rest = x - hi.astype(F32)
    mid = rest.astype(BF16)
    lo = (rest - mid.astype(F32)).astype(BF16)
    return _raw_nn(tri, hi) + _raw_nn(tri, mid) + _raw_nn(tri, lo)


@functools.partial(jax.custom_vjp, nondiff_argnums=(1,))
def _cumsum(x, reverse):
    return _cumsum_rows(x, reverse)


def _cumsum_fwd(x, reverse):
    return _cumsum_rows(x, reverse), None


def _cumsum_bwd(reverse, _, g):
    return (_cumsum_rows(g, not reverse),)


_cumsum.defvjp(_cumsum_fwd, _cumsum_bwd)


def _chunks_fn(qs, ks, vs, lfs, sts, reverses):
    n, c = len(qs), qs[0].shape[0]
    every = range(n)
    tris = [_tri_mask(c, r) for r in reverses]
    cums = [_cumsum(lfs[i], reverses[i]) for i in every]
    rid = lax.broadcasted_iota(jnp.int32, cums[0].shape, 0)

    def pick(cum, r):
        return jnp.sum(jnp.where(rid == r, cum, 0.0), axis=0, keepdims=True)

    refs = [pick(cums[i], (c - 1 - c // 2) if reverses[i] else c // 2) for i in every]
    lasts = [pick(cums[i], 0 if reverses[i] else c - 1) for i in every]
    q_in = [qs[i] * jnp.exp(cums[i] - refs[i]) for i in every]
    k_in = [ks[i] * jnp.exp(refs[i] - cums[i]) for i in every]
    scores = [jnp.where(tris[i], _dot_nt(q_in[i], k_in[i]), 0.0) for i in every]
    o_intra = [_dot_nn(scores[i], vs[i]) for i in every]
    q_out = [qs[i] * jnp.exp(cums[i]) for i in every]
    o_inter = [_dot_nt(q_out[i], sts[i]) for i in every]
    k_state = [ks[i] * jnp.exp(lasts[i] - cums[i]) for i in every]
    upd = [_dot_tn(vs[i], k_state[i]) for i in every]
    st_new = [sts[i] * jnp.exp(lasts[i]) + upd[i] for i in every]
    return [o_intra[i] + o_inter[i] for i in every], st_new


def _attn_fwd(name, q, k_f, k_b, v, lf_f, lf_b, n_heads, dk, dv):
    n_rows = q[0].shape[0]
    n_chunks = n_rows // CHUNK
    n_steps = n_chunks // ATTN_SUB
    wk, wv = n_heads * dk, n_heads * dv

    def spec(width, off, rev):
        return pl.BlockSpec((CHUNK * ATTN_SUB, width), lambda n: ((n_steps - 1 - n) if rev else n, off))

    def sspec(rev):
        return pl.BlockSpec((ATTN_SUB, n_heads, dv, dk), lambda n: ((n_steps - 1 - n) if rev else n, 0, 0, 0))

    def body(qf, kf, vf, lff, qb, kb, vb, lfb, of_ref, ob_ref, sf_ref, sb_ref, st):
        @pl.when(pl.program_id(0) == 0)
        def _():
            st[...] = jnp.zeros_like(st)

        ins = ((qf, kf, vf, lff), (qb, kb, vb, lfb))
        chains = [(d, h) for d in range(2) for h in range(n_heads)]
        ck = [slice(h * dk, (h + 1) * dk) for h in range(n_heads)]
        cv = [slice(h * dv, (h + 1) * dv) for h in range(n_heads)]
        sts = [st[d, h] for d, h in chains]
        done = []
        for sub in range(ATTN_SUB):
            local = (sub, ATTN_SUB - 1 - sub)
            rows = [slice(local[d] * CHUNK, (local[d] + 1) * CHUNK) for d in range(2)]
            qs = [ins[d][0][rows[d], ck[h]] for d, h in chains]
            ks = [ins[d][1][rows[d], ck[h]] for d, h in chains]
            vs = [ins[d][2][rows[d], cv[h]] for d, h in chains]
            lfs = [ins[d][3][rows[d], ck[h]] for d, h in chains]
            os_, st_new = _chunks_fn(qs, ks, vs, lfs, sts, [d == 1 for d, _ in chains])
            done.append((local, rows, sts, os_))
            sts = st_new
        for local, rows, entered, os_ in done:
            for i, (d, h) in enumerate(chains):
                (sf_ref, sb_ref)[d][local[d], h] = entered[i].astype(BF16)
                (of_ref, ob_ref)[d][rows[d], cv[h]] = os_[i]
        for i, (d, h) in enumerate(chains):
            st[d, h] = sts[i]

    in_specs = [spec(wk, q[1], False), spec(wk, k_f[1], False), spec(wv, v[1], False), spec(wk, lf_f[1], False),
                spec(wk, q[1], True), spec(wk, k_b[1], True), spec(wv, v[1], True), spec(wk, lf_b[1], True)]
    return pl.pallas_call(
        body, name=name, grid=(n_steps,), in_specs=in_specs,
        out_specs=[spec(wv, 0, False), spec(wv, 0, True), sspec(False), sspec(True)],
        out_shape=[jax.ShapeDtypeStruct((n_rows, wv), F32)] * 2
        + [jax.ShapeDtypeStruct((n_chunks, n_heads, dv, dk), BF16)] * 2,
        scratch_shapes=[pltpu.VMEM((2, n_heads, dv, dk), F32)],
        compiler_params=_params(("arbitrary",)),
    )(q[0], k_f[0], v[0], lf_f[0], q[0], k_b[0], v[0], lf_b[0])


def _attn_bwd(name, q, k_f, k_b, v, lf_f, lf_b, st_f, st_b, do, n_heads, dk, dv, out_dtype=F32):
    n_rows = q[0].shape[0]
    n_chunks = n_rows // CHUNK
    n_steps = n_chunks // ATTN_SUB
    wk, wv = n_heads * dk, n_heads * dv

    def spec(width, off, rev):
        return pl.BlockSpec((CHUNK * ATTN_SUB, width), lambda n: (n if rev else (n_steps - 1 - n), off))

    def sspec(rev):
        return pl.BlockSpec((ATTN_SUB, n_heads, dv, dk), lambda n: (n if rev else (n_steps - 1 - n), 0, 0, 0))

    def body(qf, kf, vf, lff, sf, dof, qb, kb, vb, lfb, sb, dob,
             dqf, dkf, dvf, dlff, dqb, dkb, dvb, dlfb, dst):
        @pl.when(pl.program_id(0) == 0)
        def _():
            dst[...] = jnp.zeros_like(dst)

        ins = ((qf, kf, vf, lff, sf, dof), (qb, kb, vb, lfb, sb, dob))
        outs = ((dqf, dkf, dvf, dlff), (dqb, dkb, dvb, dlfb))
        chains = [(d, h) for d in range(2) for h in range(n_heads)]
        ck = [slice(h * dk, (h + 1) * dk) for h in range(n_heads)]
        cv = [slice(h * dv, (h + 1) * dv) for h in range(n_heads)]
        fn = functools.partial(_chunks_fn, reverses=[d == 1 for d, _ in chains])
        dsts = [dst[d, h] for d, h in chains]
        done = []
        for sub in range(ATTN_SUB):
            local = (ATTN_SUB - 1 - sub, sub)
            rows = [slice(local[d] * CHUNK, (local[d] + 1) * CHUNK) for d in range(2)]
            qs = [ins[d][0][rows[d], ck[h]] for d, h in chains]
            ks = [ins[d][1][rows[d], ck[h]] for d, h in chains]
            vs = [ins[d][2][rows[d], cv[h]] for d, h in chains]
            lfs = [ins[d][3][rows[d], ck[h]] for d, h in chains]
            sts = [ins[d][4][local[d], h].astype(F32) for d, h in chains]
            dos = [ins[d][5][rows[d], cv[h]] for d, h in chains]
            _, vjp = jax.vjp(fn, qs, ks, vs, lfs, sts)
            dqs, dks, dvs, dlfs, dsts = vjp((dos, dsts))
            done.append((rows, dqs, dks, dvs, dlfs))
        for rows, dqs, dks, dvs, dlfs in done:
            for i, (d, h) in enumerate(chains):
                dq_r, dk_r, dv_r, dlf_r = outs[d]
                dq_r[rows[d], ck[h]] = dqs[i].astype(dq_r.dtype)
                dk_r[rows[d], ck[h]] = dks[i].astype(dk_r.dtype)
                dv_r[rows[d], cv[h]] = dvs[i].astype(dv_r.dtype)
                dlf_r[rows[d], ck[h]] = dlfs[i].astype(dlf_r.dtype)
        for i, (d, h) in enumerate(chains):
            dst[d, h] = dsts[i]

    def dir_specs(kk, lf, rev):
        return [spec(wk, q[1], rev), spec(wk, kk[1], rev), spec(wv, v[1], rev), spec(wk, lf[1], rev), sspec(rev),
                spec(wv, 0, rev)]

    def dir_out_specs(rev):
        return [spec(wk, 0, rev), spec(wk, 0, rev), spec(wv, 0, rev), spec(wk, 0, rev)]

    shapes = [jax.ShapeDtypeStruct((n_rows, wk), out_dtype), jax.ShapeDtypeStruct((n_rows, wk), out_dtype),
              jax.ShapeDtypeStruct((n_rows, wv), out_dtype), jax.ShapeDtypeStruct((n_rows, wk), F32)]
    outs = pl.pallas_call(
        body, name=name, grid=(n_steps,), in_specs=dir_specs(k_f, lf_f, False) + dir_specs(k_b, lf_b, True),
        out_specs=dir_out_specs(False) + dir_out_specs(True), out_shape=shapes + shapes,
        scratch_shapes=[pltpu.VMEM((2, n_heads, dv, dk), F32)],
        compiler_params=_params(("arbitrary",)),
    )(q[0], k_f[0], v[0], lf_f[0], st_f, do, q[0], k_b[0], v[0], lf_b[0], st_b, do)
    return outs[:4], outs[4:]


def _row2(v):
    return v.reshape(1, -1)


def _mlp_fwd(tag, h, gain, w1, w2):
    y = _rowcall(f"{tag}_norm", _rmsnorm_fn, [(h, h.shape[1], 0)], [gain], [(h.shape[1], BF16)], tm=512)[0]
    hid = _mm(f"{tag}_up", y, w1, out_dtypes=(BF16,))
    h_out = _mm(f"{tag}_down", hid, w2, a_pro=_relu2, extras=(h,), epi=_add_epi)
    return h_out, (y, hid)


def _dw(name, a, b, **kw):
    return _mm(name, a, b, mode="tn", epi=lambda acc: (acc, acc), out_dtypes=(F32, BF16), **kw)


def _mlp_bwd(tag, h, gain, w1, w2, saved, dh_out):
    y, hid = saved
    dhid = _mm(f"{tag}_dact", dh_out, w2, mode="nt", extras=(hid,), epi=_relu2_bwd_epi, out_dtypes=(BF16,))
    dw2 = _dw(f"{tag}_dw2", hid, dh_out, a_pro=_relu2)
    dw1 = _dw(f"{tag}_dw1", y, dhid, out_split=N_CHIPS)
    dh, dgain = _dy_norm_bwd(f"{tag}_dy", dhid, w1, h, gain, dh_out)
    return dh, dgain, dw1, dw2


def _dy_norm_bwd(name, dz, w, h, gain, dres, pin=None, **tiles):
    def epi(dy, h_tile, dres_tile, gain_row):
        _, vjp = jax.vjp(lambda u, v: _rmsnorm_fn(u, v)[0], h_tile, gain_row)
        dh, dgain = vjp(dy)
        return dh + dres_tile, dgain

    assert h.shape[1] <= 1024
    tiles.setdefault("tm", 512)
    dh, dgain_parts = _mm(name, dz, w, mode="nt", extras=(h, dres), epi=epi, epi_pars=(gain,), row_sum=True,
                          pin=pin, **tiles)
    return dh, jnp.sum(dgain_parts, axis=0)


def _local_step(x, target, w, pin=None, late=None, emit=None):
    g = {}
    d_model = x.shape[1]
    rg_w = hg_w = d_model // 2
    pins = []

    def send_off(tag, pairs):
        if emit is not None:
            pins.append(emit(tag, [p[0] for p in pairs], [p[1] for p in pairs]))

    def both(fn, pair):
        return [fn(t) for t in pair]

    def chip_major(t):
        return t.reshape(N_CHIPS, t.shape[0] // N_CHIPS, t.shape[1])

    h_a0 = x
    gain = _row2(w["norm_mix"][0])
    y0 = _rowcall("l0_norm", _rmsnorm_fn, [(h_a0, d_model, 0)], [gain], [(d_model, BF16)], tm=512, pin=pin)[0]
    proj0 = _mm("l0_in", y0, w["ab_w_in"])
    conv_w, conv_b = w["rg_conv_w"], _row2(w["rg_conv_b"])
    xc = _conv_fwd("rg_conv", proj0, 0, conv_w, conv_b)
    gate_pars = [w["rg_wa_bd"], w["rg_wx_bd"], w["rg_b_a"], w["rg_b_x"], w["rg_lambda"]]
    a_f, u_f, a_b, u_b = _rowcall("rg_gates", _rg_gates_fn, [(xc, rg_w, 0)], gate_pars, [(rg_w, F32)] * 4)
    hs_f = _scan_fwd("rg_scan_f", a_f, u_f, False)
    hs_b = _scan_fwd("rg_scan_b", a_b, u_b, True)
    hg_rows = [(proj0, hg_w, 2), (proj0, hg_w, 3), (proj0, hg_w, 4)]
    qh, k_f, lf_f, k_b, lf_b = _rowcall("hg_pre", _hg_pre_fn, hg_rows, [w["hg_lb_logits"]], [(hg_w, F32)] * 5)
    iv = (proj0, 5)
    o_f, o_b, st_f, st_b = _attn_fwd("hg_attn", (qh, 0), (k_f, 0), (k_b, 0), iv, (lf_f, 0), (lf_b, 0), 4, 128, 128)
    post0_rows = [(hs_f, rg_w, 0), (hs_b, rg_w, 0), (proj0, rg_w, 1), (o_f, hg_w, 0), (o_b, hg_w, 0), (proj0, hg_w, 6)]
    hg_gain = _row2(w["hg_norm"])
    mix_in0 = _rowcall("l0_post", _post0_fwd_fn, post0_rows, [hg_gain], [(d_model, BF16)])[0]
    if late is not None:
        w = {**w, **late(mix_in0)}
    h_b0 = _mm("l0_out", mix_in0, w["ab_w_out"], extras=(h_a0,), epi=_add_epi)
    h_c0, mlp0 = _mlp_fwd("mlp0", h_b0, _row2(w["norm_mlp"][0]), w["mlp_w1"][0], w["mlp_w2"][0])

    h_a1 = h_c0
    gain1 = _row2(w["norm_mix"][1])
    y1 = _rowcall("l1_norm", _rmsnorm_fn, [(h_a1, d_model, 0)], [gain1], [(d_model, BF16)], tm=512)[0]
    proj1 = _mm("l1_in", y1, w["gla_w_in_pad"], tn=640)
    gla_pars = [w["gla_w_up_pad"], w["gla_b_gate"]]
    gq, glf_f, glf_b = _rowcall("gla_pre", _gla_pre_fn, [(proj1, 512, 0), (proj1, LANES, 24)], gla_pars, [(512, F32)] * 3)
    gk, gv = (proj1, 1), (proj1, 1)
    go_f, go_b, gst_f, gst_b = _attn_fwd("gla_attn", (gq, 0), gk, gk, gv, (glf_f, 0), (glf_b, 0), 4, 128, 256)
    gla_gain = _row2(w["gla_norm"])
    post1_rows = [(go_f, d_model, 0), (go_b, d_model, 0), (proj1, d_model, 2)]
    mix_in1 = _rowcall("l1_post", _gla_post_fwd_fn, post1_rows, [gla_gain], [(d_model, BF16)])[0]
    h_b1 = _mm("l1_out", mix_in1, w["gla_w_out"], extras=(h_a1,), epi=_add_epi)
    h_c1, mlp1 = _mlp_fwd("mlp1", h_b1, _row2(w["norm_mlp"][1]), w["mlp_w1"][1], w["mlp_w2"][1])

    dh, loss, g["norm_final"] = _rowcall(
        "loss_head", _loss_head_fn, [(h_c1, d_model, 0), (target, d_model, 0)], [_row2(w["norm_final"])],
        [(d_model, F32)], [(1, LANES), (1, d_model)], tm=512)

    dh, g_nmlp1, g_w1_1, g_w2_1 = _mlp_bwd("mlp1", h_b1, _row2(w["norm_mlp"][1]), w["mlp_w1"][1], w["mlp_w2"][1], mlp1, dh)
    send_off("mlp1", [g_w1_1, both(chip_major, g_w2_1)])
    dmix1 = _mm("l1_dout", dh, w["gla_w_out"], mode="nt")
    g_gla_out = _dw("l1_dwout", mix_in1, dh)
    g["gla_w_out"] = g_gla_out[0]
    dgo, dr, g["gla_norm"] = _rowcall(
        "l1_dpost", _gla_post_bwd_fn, post1_rows + [(dmix1, d_model, 0)], [gla_gain],
        [(d_model, F32), (d_model, BF16)], [(1, d_model)], pin=pins.pop() if pins else None)
    (dq_f, dk_f, dv_f, dlf_f), (dq_b, dk_b, dv_b, dlf_b) = _attn_bwd(
        "gla_dattn", (gq, 0), gk, gk, gv, (glf_f, 0), (glf_b, 0), gst_f, gst_b, dgo, 4, 128, 256)

    def gla_pre_bwd(q, lr, dq1, dq2, dlf1, dlf2, dk1, dk2, dv1, dv2, w_up, b_gate):
        dlr = jnp.zeros_like(lr)
        dws, dbs = [], []
        for d, dlf in enumerate((dlf1, dlf2)):
            z = _raw_nn(lr, w_up[d]) + b_gate[d:d + 1]
            dz = dlf * _sigmoid(-z) * (1.0 / 16.0)
            dlr = dlr + _raw_nt(dz, w_up[d])
            dws.append(_raw_tn(dz, lr))
            dbs.append(jnp.sum(dz, axis=0, keepdims=True))
        return ((dq1 + dq2) * (128.0 ** -0.5), dk1 + dk2, dv1 + dv2, dlr, dws[0], dws[1], dbs[0], dbs[1])

    rows = [(proj1, 512, 0), (proj1, LANES, 24), (dq_f, 512, 0), (dq_b, 512, 0), (dlf_f, 512, 0), (dlf_b, 512, 0),
            (dk_f, 512, 0), (dk_b, 512, 0), (dv_f, d_model, 0), (dv_b, d_model, 0)]
    dq, dk, dv, dlr, dwt_f, dwt_b, db_f, db_b = _rowcall(
        "gla_dpre", gla_pre_bwd, rows, gla_pars, [(512, BF16), (512, BF16), (d_model, BF16), (LANES, BF16)],
        [(512, LANES), (512, LANES), (1, 512), (1, 512)])
    g["gla_w_up_pad"] = jnp.stack([dwt_f.T, dwt_b.T])
    g["gla_b_gate"] = jnp.concatenate([db_f, db_b], axis=0)
    dproj1 = jnp.concatenate([dq, dk, dv, dr, dlr], axis=1)
    g_gla_in = both(lambda t: _split_chips(t[:, :GLA_IN_WIDTH], 1), _dw("l1_dwin", y1, dproj1, tn=640))
    g["gla_w_in"] = g_gla_in[0]
    send_off("gla", [g_gla_in, both(chip_major, g_gla_out)])
    dh, g_nmix1 = _dy_norm_bwd("l1_dy", dproj1, w["gla_w_in_pad"], h_a1, gain1, dh,
                               pin=pins.pop() if pins else None, tk=640)

    dh, g_nmlp0, g_w1_0, g_w2_0 = _mlp_bwd("mlp0", h_b0, _row2(w["norm_mlp"][0]), w["mlp_w1"][0], w["mlp_w2"][0], mlp0, dh)
    g_ab_out = _dw("l0_dwout", mix_in0, dh)
    g["ab_w_out"] = g_ab_out[0]
    send_off("mlp0", [g_w1_0, both(chip_major, g_w2_0), both(chip_major, g_ab_out)])
    dmix0 = _mm("l0_dout", dh, w["ab_w_out"], mode="nt")
    dhs, dga, do, dg, g["hg_norm"] = _rowcall(
        "l0_dpost", _post0_bwd_fn, post0_rows + [(dmix0, d_model, 0)], [hg_gain],
        [(rg_w, F32), (rg_w, BF16), (hg_w, F32), (hg_w, BF16)], [(1, hg_w)], pin=pins.pop() if pins else None)
    (dqh_f, dk_f, div_f, dlf_f), (dqh_b, dk_b, div_b, dlf_b) = _attn_bwd(
        "hg_dattn", (qh, 0), (k_f, 0), (k_b, 0), iv, (lf_f, 0), (lf_b, 0), st_f, st_b, do, 4, 128, 128)

    def hg_pre_bwd(q, f_f, f_b, dq1, dq2, dk1, dlf1, dk2, dlf2, dv1, dv2, logits):
        _, vjp = jax.vjp(_hg_pre_fn, q, f_f, f_b, logits)
        dq, df_f, df_b, dlogits = vjp((dq1 + dq2, dk1, dlf1, dk2, dlf2))
        return dq, df_f, df_b, dv1 + dv2, dlogits

    rows = hg_rows + [(t, hg_w, 0) for t in (dqh_f, dqh_b, dk_f, dlf_f, dk_b, dlf_b, div_f, div_b)]
    dq, df_f, df_b, div, g["hg_lb_logits"] = _rowcall(
        "hg_dpre", hg_pre_bwd, rows, [w["hg_lb_logits"]], [(hg_w, BF16)] * 4, [(2, hg_w)])
    du_f, da_f = _scan_bwd("rg_dscan_f", a_f, hs_f, dhs, False)
    du_b, da_b = _scan_bwd("rg_dscan_b", a_b, hs_b, dhs, True)
    gates_bwd = _vjp_of(_rg_gates_fn, 1, 4, 5)
    rows = [(xc, rg_w, 0), (da_f, rg_w, 0), (du_f, rg_w, 0), (da_b, rg_w, 0), (du_b, rg_w, 0)]
    dxc, g["rg_wa_bd"], g["rg_wx_bd"], g["rg_b_a"], g["rg_b_x"], g["rg_lambda"] = _rowcall(
        "rg_dgates", gates_bwd, rows, gate_pars, [(rg_w, F32)],
        [(2, rg_w, rg_w), (2, rg_w, rg_w), (2, rg_w), (2, rg_w), (2, rg_w)])
    dxa, g["rg_conv_w"], g["rg_conv_b"] = _conv_bwd("rg_dconv", proj0, 0, conv_w, dxc)
    dproj0 = jnp.concatenate([dxa, dga, dq, df_f, df_b, div, dg], axis=1)
    g_ab_in = _dw("l0_dwin", y0, dproj0, out_split=N_CHIPS)
    g["ab_w_in"] = g_ab_in[0]
    send_off("ab", [g_ab_in])
    grad_x, g_nmix0 = _dy_norm_bwd("l0_dy", dproj0, w["ab_w_in"], h_a0, gain, dh, pin=pins.pop() if pins else None)

    g["norm_mix"] = jnp.concatenate([g_nmix0, g_nmix1], axis=0)
    g["norm_mlp"] = jnp.concatenate([g_nmlp0, g_nmlp1], axis=0)
    g["mlp_w1"] = [g_w1_0[0], g_w1_1[0]]
    g["mlp_w2"] = [g_w2_0[0], g_w2_1[0]]
    return loss, grad_x, g


def _block_diag(w):
    d, g, n, _ = w.shape
    eye = jnp.eye(g, dtype=w.dtype)
    return (w[:, :, :, None, :] * eye[None, :, None, :, None]).reshape(d, g * n, g * n)


def _block_diag_extract(wbd, g):
    d, gn, _ = wbd.shape
    n = gn // g
    blocks = wbd.reshape(d, g, n, g, n)
    return jnp.stack([blocks[:, i, :, i, :] for i in range(g)], axis=1)


def _prepare_weights(big, full):
    w = {k: full[k] for k in ("norm_mix", "norm_mlp", "norm_final", "hg_lb_logits")}
    for k in ("rg_conv_w", "rg_conv_b", "rg_b_a", "rg_b_x", "rg_lambda", "hg_norm", "gla_b_gate", "gla_norm"):
        w[k] = full[k][0]
    w["rg_wa_bd"] = _block_diag(full["rg_w_a"][0])
    w["rg_wx_bd"] = _block_diag(full["rg_w_x"][0])
    up = full["gla_w_gate_up"][0]
    rank = up.shape[1]
    pad = jnp.zeros((2, LANES, up.shape[2]), F32)
    w["gla_w_up_pad"] = pad.at[0, 0:rank].set(up[0]).at[1, rank:2 * rank].set(up[1])
    w.update(_prepare_matrices(big))
    return w


def _prepare_matrices(big):
    w = {}
    if "mlp_w1" in big:
        w["mlp_w1"] = list(big["mlp_w1"])
        w["mlp_w2"] = [t.reshape(-1, t.shape[-1]) for t in big["mlp_w2"]]
    if "ab_w_in" in big:
        w["ab_w_in"] = big["ab_w_in"]
    if "ab_w_out" in big:
        w["ab_w_out"] = big["ab_w_out"].reshape(-1, big["ab_w_out"].shape[-1])
    if "gla_w_in" in big:
        w["gla_w_out"] = big["gla_w_out"].reshape(-1, big["gla_w_out"].shape[-1])
        gla_in = _join_chips(big["gla_w_in"], 1)
        w["gla_w_in_pad"] = jnp.pad(gla_in, ((0, 0), (0, GLA_IN_PAD - gla_in.shape[1])))
    return w


def _finish_grads(g, rank=16, rg_blocks=8):
    def chip_major(t):
        return t.reshape(N_CHIPS, t.shape[0] // N_CHIPS, t.shape[1])

    big = {
        "mlp_w1": list(g["mlp_w1"]), "mlp_w2": [chip_major(t) for t in g["mlp_w2"]],
        "ab_w_in": g["ab_w_in"], "ab_w_out": chip_major(g["ab_w_out"]),
        "gla_w_in": g["gla_w_in"], "gla_w_out": chip_major(g["gla_w_out"]),
    }
    small = {
        "norm_mix": g["norm_mix"], "norm_mlp": g["norm_mlp"], "norm_final": g["norm_final"][0],
        "rg_conv_w": g["rg_conv_w"][None], "rg_conv_b": g["rg_conv_b"],
        "rg_w_a": _block_diag_extract(g["rg_wa_bd"], rg_blocks)[None], "rg_b_a": g["rg_b_a"][None],
        "rg_w_x": _block_diag_extract(g["rg_wx_bd"], rg_blocks)[None], "rg_b_x": g["rg_b_x"][None],
        "rg_lambda": g["rg_lambda"][None], "hg_lb_logits": g["hg_lb_logits"], "hg_norm": g["hg_norm"],
        "gla_w_gate_up": jnp.stack([g["gla_w_up_pad"][0, 0:rank], g["gla_w_up_pad"][1, rank:2 * rank]])[None],
        "gla_b_gate": g["gla_b_gate"][None], "gla_norm": g["gla_norm"],
    }
    return big, small


MATRICES = (("mlp_w1", 0), ("mlp_w1", 1), ("mlp_w2", 0), ("mlp_w2", 1), ("ab_w_in", 0), ("ab_w_out", 0),
            ("gla_w_in", 0), ("gla_w_out", 0))
EARLY_MATRICES = ("ab_w_in",)
SMALL_SHARDED = ("rg_conv_w", "rg_b_a", "rg_b_x", "rg_lambda", "gla_w_gate_up", "gla_b_gate", "gla_norm")
SMALL_REPLICATED = ("norm_mix", "norm_mlp", "norm_final", "rg_conv_b", "rg_w_a", "rg_w_x", "hg_lb_logits", "hg_norm")
WEIGHTS = ("norm_mix", "norm_mlp", "norm_final", "mlp_w1", "mlp_w2", "ab_w_in", "ab_w_out", "rg_conv_w", "rg_conv_b",
           "rg_w_a", "rg_b_a", "rg_w_x", "rg_b_x", "rg_lambda", "hg_lb_logits", "hg_norm", "gla_w_in", "gla_w_out",
           "gla_w_gate_up", "gla_b_gate", "gla_norm")
ROW_ALIGN = 16


def _pack(arrays, lead=0):
    head = arrays[0].shape[:lead]
    flat = jnp.concatenate([a.reshape(head + (-1,)) for a in arrays], axis=lead)
    n = flat.shape[-1]
    quantum = LANES * ROW_ALIGN
    padded = -(-n // quantum) * quantum
    if padded != n:
        flat = jnp.pad(flat, [(0, 0)] * lead + [(0, padded - n)])
    return flat.reshape(head + (padded // LANES, LANES))


def _unpack(buf, shapes, lead=0):
    head = buf.shape[:lead]
    flat = buf.reshape(head + (-1,))
    out, off = [], 0
    for s in shapes:
        n = 1
        for v in s:
            n *= v
        out.append(lax.slice_in_dim(flat, off, off + n, axis=lead).reshape(head + tuple(s)))
        off += n
    return out


def _join_chips(gathered, axis):
    t = jnp.moveaxis(gathered, 0, axis)
    return t.reshape(t.shape[:axis] + (t.shape[axis] * t.shape[axis + 1],) + t.shape[axis + 2:])


def _split_chips(full, axis):
    s = full.shape
    t = full.reshape(s[:axis] + (N_CHIPS, s[axis] // N_CHIPS) + s[axis + 1:])
    return jnp.moveaxis(t, axis, 0)


_ANY = pl.BlockSpec(memory_space=pl.ANY)


def _place():
    return lax.axis_index("x"), lax.axis_index("y"), lax.axis_index("c")


def _into_slot(name, src, slot, n_slots, dtype, tm, layer=None):
    r, lanes = src.shape[-2:]
    tm = _row_tile(r, tm, ROW_ALIGN)

    def body(slot_ref, in_ref, o_ref):
        o_ref[...] = in_ref[...].astype(o_ref.dtype)

    if layer is None:
        in_spec = pl.BlockSpec((tm, lanes), lambda i, slot_ref: (i, 0))
    else:
        in_spec = pl.BlockSpec((None, tm, lanes), lambda i, slot_ref: (layer, i, 0))
    grid_spec = pltpu.PrefetchScalarGridSpec(
        num_scalar_prefetch=1, grid=(r // tm,), in_specs=[in_spec],
        out_specs=pl.BlockSpec((None, tm, lanes), lambda i, slot_ref: (slot_ref[0], i, 0)))
    return pl.pallas_call(
        body, name=name, grid_spec=grid_spec, out_shape=jax.ShapeDtypeStruct((n_slots, r, lanes), dtype),
        compiler_params=_params(("parallel",)),
    )(slot.reshape(1).astype(jnp.int32), src)


def _chip_peers():
    x, y, c = _place()
    return 2 * x + y, c, [(1 - x, y), (x, 1 - y), (1 - x, 1 - y)]


def _comm_call(name, body, ins, out_shapes, n_sems, aliases=None):
    return pl.pallas_call(
        body, name=name, in_specs=[_ANY] * len(ins), out_specs=[_ANY] * len(out_shapes), out_shape=out_shapes,
        input_output_aliases=aliases or {},
        scratch_shapes=[pltpu.SemaphoreType.DMA((n_sems,)), pltpu.SemaphoreType.DMA((n_sems,))],
    )(*ins)


def _gather_chips(name, bufs):
    n = len(bufs)

    def body(*refs):
        outs, send_sems, recv_sems = refs[n:2 * n], refs[2 * n], refs[2 * n + 1]
        x, y, c = _place()
        me, _, peers = _chip_peers()

        def rows(a, block, half):
            rh = outs[a].shape[1] // 2
            return outs[a].at[block, pl.ds(half * rh, rh)]

        def copy(a, j, block, half, to, sem):
            return pltpu.make_async_remote_copy(
                src_ref=rows(a, block, half), dst_ref=rows(a, block, half), send_sem=send_sems.at[sem],
                recv_sem=recv_sems.at[sem], device_id=to, device_id_type=MESH)

        def over_ici(a, j, block):
            px, py = peers[j]
            return copy(a, j, block, c, (px, py, c), 6 * a + j)

        def to_sibling(a, j, block, half):
            return copy(a, j, block, half, (x, y, 1 - c), 6 * a + 3 + j)

        sends = [over_ici(a, j, me) for a in range(n) for j in range(3)]
        for cp in sends:
            cp.start()
        for a in range(n):
            for j, (px, py) in enumerate(peers):
                over_ici(a, j, 2 * px + py).wait_recv()
                handed = to_sibling(a, j, 2 * px + py, c)
                handed.start()
                sends.append(handed)
        for a in range(n):
            for j, (px, py) in enumerate(peers):
                to_sibling(a, j, 2 * px + py, 1 - c).wait_recv()
        for cp in sends:
            cp.wait_send()

    shapes = [jax.ShapeDtypeStruct(b.shape, b.dtype) for b in bufs]
    return _comm_call(name, body, bufs, shapes, 6 * n, {a: a for a in range(n)})


_HBM = pl.BlockSpec(memory_space=pltpu.HBM)
_SEM = pl.BlockSpec(memory_space=pltpu.SEMAPHORE)
_EFFECT = pltpu.SideEffectType.DATAFLOW_SIDE_EFFECTING


def _half_rows(ref, block, half):
    rh = ref.shape[1] // 2
    return ref.at[block, pl.ds(half * rh, rh)]


def _gather_start(name, bufs, after):
    n = len(bufs)

    def body(*refs):
        ins, send_sems, recv_sems, token = refs[:n], refs[n + 1], refs[n + 2], refs[-1]
        me, c, peers = _chip_peers()
        for a in range(n):
            mine = _half_rows(ins[a], me, c)
            for j, (px, py) in enumerate(peers):
                pltpu.make_async_remote_copy(
                    src_ref=mine, dst_ref=mine, send_sem=send_sems.at[3 * a + j], recv_sem=recv_sems.at[3 * a + j],
                    device_id=(px, py, c), device_id_type=MESH).start()
        token[...] = jnp.zeros_like(token)

    out_shape = (pltpu.SemaphoreType.DMA((3 * n,)), pltpu.SemaphoreType.DMA((3 * n,)),
                 *[pltpu.HBM(b.shape, b.dtype) for b in bufs], jax.ShapeDtypeStruct((8, LANES), F32))
    return pl.pallas_call(
        body, name=name, out_shape=out_shape, in_specs=[_HBM] * n + [_ANY],
        out_specs=(_SEM, _SEM, *[_HBM] * n, pl.BlockSpec(memory_space=pltpu.VMEM)),
        input_output_aliases={a: 2 + a for a in range(n)},
        compiler_params=pltpu.CompilerParams(has_side_effects=_EFFECT),
    )(*[pltpu.with_memory_space_constraint(b, pltpu.HBM) for b in bufs], after)


def _gather_wait(name, bufs, send_sems, recv_sems, after):
    n = len(bufs)

    def body(*refs):
        ins, send_sems, recv_sems = refs[:n], refs[n], refs[n + 1]
        me, c, peers = _chip_peers()
        for a in range(n):
            for j, (px, py) in enumerate(peers):
                copy = pltpu.make_async_remote_copy(
                    src_ref=_half_rows(ins[a], me, c), dst_ref=_half_rows(ins[a], 2 * px + py, c),
                    send_sem=send_sems.at[3 * a + j], recv_sem=recv_sems.at[3 * a + j],
                    device_id=(px, py, c), device_id_type=MESH)
                copy.wait_send()
                copy.wait_recv()

    return pl.pallas_call(
        body, name=name, out_shape=tuple(pltpu.HBM(b.shape, b.dtype) for b in bufs),
        in_specs=[_HBM] * n + [_SEM, _SEM, _ANY], out_specs=tuple([_HBM] * n),
        input_output_aliases={a: a for a in range(n)},
        compiler_params=pltpu.CompilerParams(has_side_effects=_EFFECT),
    )(*bufs, send_sems, recv_sems, after)


def _hand_over(name, bufs):
    n = len(bufs)

    def body(*refs):
        outs, send_sems, recv_sems = refs[n:2 * n], refs[2 * n], refs[2 * n + 1]
        x, y, c = _place()
        _, _, peers = _chip_peers()

        def copy(a, j, half):
            px, py = peers[j]
            rows = _half_rows(outs[a], 2 * px + py, half)
            return pltpu.make_async_remote_copy(
                src_ref=rows, dst_ref=rows, send_sem=send_sems.at[3 * a + j], recv_sem=recv_sems.at[3 * a + j],
                device_id=(x, y, 1 - c), device_id_type=MESH)

        sends = [copy(a, j, c) for a in range(n) for j in range(3)]
        for cp in sends:
            cp.start()
        for a in range(n):
            for j in range(3):
                copy(a, j, 1 - c).wait_recv()
        for cp in sends:
            cp.wait_send()

    shapes = [jax.ShapeDtypeStruct(b.shape, b.dtype) for b in bufs]
    return _comm_call(name, body, bufs, shapes, 3 * n, {a: a for a in range(n)})


def _pair_gather(name, bufs):
    n = len(bufs)

    def body(*refs):
        ins, outs, send_sems, recv_sems = refs[:n], refs[n:2 * n], refs[2 * n], refs[2 * n + 1]
        x, y, c = _place()

        def copy(a, block):
            return pltpu.make_async_remote_copy(
                src_ref=ins[a].at[block], dst_ref=outs[a].at[block], send_sem=send_sems.at[a],
                recv_sem=recv_sems.at[a], device_id=(x, y, 1 - c), device_id_type=MESH)

        sends = [copy(a, c) for a in range(n)]
        for cp in sends:
            cp.start()
        for a in range(n):
            copy(a, 1 - c).wait_recv()
        for cp in sends:
            cp.wait_send()

    shapes = [jax.ShapeDtypeStruct(b.shape, b.dtype) for b in bufs]
    return _comm_call(name, body, bufs, shapes, n, {a: a for a in range(n)})


def _all_peers():
    x, y, c = _place()
    peers = []
    for mask in range(1, N_DEV):
        fx, fy, fc = (mask >> 2) & 1, (mask >> 1) & 1, mask & 1
        peers.append((jnp.where(fx, 1 - x, x), jnp.where(fy, 1 - y, y), jnp.where(fc, 1 - c, c)))
    return 4 * x + 2 * y + c, peers


def _reduce_copies(srcs, lands, send_sems, recv_sems):
    me, peers = _all_peers()
    sends, arrivals = [], []
    for a in range(len(srcs)):
        for j, (px, py, pc) in enumerate(peers):
            k = (N_DEV - 1) * a + j
            sends.append(pltpu.make_async_remote_copy(
                src_ref=srcs[a].at[2 * px + py, pc], dst_ref=lands[a].at[me], send_sem=send_sems.at[k],
                recv_sem=recv_sems.at[k], device_id=(px, py, pc), device_id_type=MESH))
            arrivals.append(pltpu.make_async_remote_copy(
                src_ref=srcs[a].at[2 * px + py, pc], dst_ref=lands[a].at[4 * px + 2 * py + pc],
                send_sem=send_sems.at[k], recv_sem=recv_sems.at[k], device_id=(px, py, pc), device_id_type=MESH))
    return sends, arrivals


def _reduce_direct(name, srcs, pin=None):
    n = len(srcs)
    extra = [] if pin is None else [pin]

    def body(*refs):
        ins, outs = refs[:n], refs[n + len(extra):2 * n + len(extra)]
        sends, arrivals = _reduce_copies(ins, outs, refs[-2], refs[-1])
        for cp in sends:
            cp.start()
        for cp in arrivals:
            cp.wait_recv()
        for cp in sends:
            cp.wait_send()

    shapes = [jax.ShapeDtypeStruct((N_DEV,) + s.shape[2:], s.dtype) for s in srcs]
    return _comm_call(name, body, list(srcs) + extra, shapes, (N_DEV - 1) * n)


def _reduce_start(name, srcs):
    n = len(srcs)
    lands = [lax.empty((N_DEV,) + s.shape[2:], s.dtype) for s in srcs]

    def body(*refs):
        sends, _ = _reduce_copies(refs[:n], refs[n:2 * n], refs[2 * n], refs[2 * n + 1])
        for cp in sends:
            cp.start()
        refs[-1][...] = jnp.zeros_like(refs[-1])

    bufs = list(srcs) + lands
    n_sems = (N_DEV - 1) * n
    out_shape = (pltpu.SemaphoreType.DMA((n_sems,)), pltpu.SemaphoreType.DMA((n_sems,)),
                 *[pltpu.HBM(b.shape, b.dtype) for b in bufs], jax.ShapeDtypeStruct((8, LANES), F32))
    return pl.pallas_call(
        body, name=name, out_shape=out_shape, in_specs=[_HBM] * (2 * n),
        out_specs=(_SEM, _SEM, *[_HBM] * (2 * n), pl.BlockSpec(memory_space=pltpu.VMEM)),
        input_output_aliases={a: 2 + a for a in range(2 * n)},
        compiler_params=pltpu.CompilerParams(has_side_effects=_EFFECT),
    )(*[pltpu.with_memory_space_constraint(b, pltpu.HBM) for b in bufs])


def _reduce_wait(name, srcs, lands, send_sems, recv_sems, after):
    n = len(srcs)

    def body(*refs):
        sends, arrivals = _reduce_copies(refs[:n], refs[n:2 * n], refs[2 * n], refs[2 * n + 1])
        for cp in sends:
            cp.wait_send()
        for cp in arrivals:
            cp.wait_recv()

    bufs = list(srcs) + list(lands)
    outs = pl.pallas_call(
        body, name=name, out_shape=tuple(pltpu.HBM(b.shape, b.dtype) for b in bufs),
        in_specs=[_HBM] * (2 * n) + [_SEM, _SEM, _ANY], out_specs=tuple([_HBM] * (2 * n)),
        input_output_aliases={a: a for a in range(2 * n)},
        compiler_params=pltpu.CompilerParams(has_side_effects=_EFFECT),
    )(*bufs, send_sems, recv_sems, after)
    return list(outs[n:])


def _reduce_sum(name, own, land, chip, core):
    n, rh, lanes = land.shape
    tm = _row_tile(rh, 1024, ROW_ALIGN)

    def body(idx_ref, own_ref, *rest):
        total = own_ref[...]
        for g_ref in rest[:-1]:
            total = total + g_ref[...].astype(F32)
        rest[-1][...] = total

    def block(k):
        return pl.BlockSpec((None, tm, lanes), lambda i, idx_ref: ((2 * idx_ref[0] + idx_ref[1] + k) % n, i, 0))

    grid_spec = pltpu.PrefetchScalarGridSpec(
        num_scalar_prefetch=1, grid=(rh // tm,),
        in_specs=[pl.BlockSpec((None, None, tm, lanes), lambda i, idx_ref: (idx_ref[0], idx_ref[1], i, 0))]
        + [block(k) for k in range(1, n)],
        out_specs=pl.BlockSpec((None, tm, lanes), lambda i, idx_ref: (idx_ref[1], i, 0)))
    return pl.pallas_call(
        body, name=name, grid_spec=grid_spec, out_shape=jax.ShapeDtypeStruct((2, rh, lanes), F32),
        compiler_params=_params(("parallel",)),
    )(jnp.stack([chip, core]).astype(jnp.int32), own, *[land] * (n - 1))


def _gather_all_start(name, buf):
    def body(in_ref, send_sems, recv_sems, out_ref, token):
        me, peers = _all_peers()
        for j, peer in enumerate(peers):
            pltpu.make_async_remote_copy(
                src_ref=in_ref.at[me], dst_ref=in_ref.at[me], send_sem=send_sems.at[j], recv_sem=recv_sems.at[j],
                device_id=peer, device_id_type=MESH).start()
        token[...] = jnp.zeros_like(token)

    n = N_DEV - 1
    return pl.pallas_call(
        body, name=name, in_specs=[_HBM],
        out_shape=(pltpu.SemaphoreType.DMA((n,)), pltpu.SemaphoreType.DMA((n,)), pltpu.HBM(buf.shape, buf.dtype),
                   jax.ShapeDtypeStruct((8, LANES), F32)),
        out_specs=(_SEM, _SEM, _HBM, pl.BlockSpec(memory_space=pltpu.VMEM)), input_output_aliases={0: 2},
        compiler_params=pltpu.CompilerParams(has_side_effects=_EFFECT),
    )(pltpu.with_memory_space_constraint(buf, pltpu.HBM))


def _gather_all_wait(name, buf, send_sems, recv_sems, after):
    def body(in_ref, send_sems, recv_sems, after_ref, out_ref):
        me, peers = _all_peers()
        for j, (px, py, pc) in enumerate(peers):
            copy = pltpu.make_async_remote_copy(
                src_ref=in_ref.at[me], dst_ref=in_ref.at[4 * px + 2 * py + pc], send_sem=send_sems.at[j],
                recv_sem=recv_sems.at[j], device_id=(px, py, pc), device_id_type=MESH)
            copy.wait_send()
            copy.wait_recv()

    return pl.pallas_call(
        body, name=name, in_specs=[_HBM, _SEM, _SEM, _ANY], out_shape=pltpu.HBM(buf.shape, buf.dtype),
        out_specs=_HBM, input_output_aliases={0: 0},
        compiler_params=pltpu.CompilerParams(has_side_effects=_EFFECT),
    )(buf, send_sems, recv_sems, after)


def _sum_blocks(name, stacked, tm):
    n, r, lanes = stacked.shape

    def body(in_ref, o_ref):
        acc = in_ref[0]
        for j in range(1, n):
            acc = acc + in_ref[j]
        o_ref[...] = acc

    return pl.pallas_call(
        body, name=name, grid=(r // tm,), in_specs=[pl.BlockSpec((n, tm, lanes), lambda i: (0, i, 0))],
        out_specs=pl.BlockSpec((tm, lanes), lambda i: (i, 0)), out_shape=jax.ShapeDtypeStruct((r, lanes), F32),
        compiler_params=_params(("parallel",)),
    )(stacked)


def _row_tile(rows, pref, align):
    best = None
    for t in range(align, min(rows, pref) + 1, align):
        if rows % t == 0:
            best = t
    assert best is not None, (rows, pref, align)
    return best


def _adam(name, w, g, m, v):
    rows, width = w.shape
    tm = _row_tile(rows, max(8, 4096 * LANES // width), 8)
    args = [(t, width, 0) for t in (w, g, m, v)]
    return _rowcall(name, _adam_fn, args, [], [(width, F32)] * 3, tm=tm)


def kernel(x, norm_mix, norm_mlp, norm_final, mlp_w1, mlp_w2, ab_w_in, ab_w_out, rg_conv_w, rg_conv_b, rg_w_a, rg_b_a, rg_w_x, rg_b_x, rg_lambda, hg_lb_logits, hg_norm, gla_w_in, gla_w_out, gla_w_gate_up, gla_b_gate, gla_norm, loss_target, m_norm_mix, m_norm_mlp, m_norm_final, m_mlp_w1, m_mlp_w2, m_ab_w_in, m_ab_w_out, m_rg_conv_w, m_rg_conv_b, m_rg_w_a, m_rg_b_a, m_rg_w_x, m_rg_b_x, m_rg_lambda, m_hg_lb_logits, m_hg_norm, m_gla_w_in, m_gla_w_out, m_gla_w_gate_up, m_gla_b_gate, m_gla_norm, v_norm_mix, v_norm_mlp, v_norm_final, v_mlp_w1, v_mlp_w2, v_ab_w_in, v_ab_w_out, v_rg_conv_w, v_rg_conv_b, v_rg_w_a, v_rg_b_a, v_rg_w_x, v_rg_b_x, v_rg_lambda, v_hg_lb_logits, v_hg_norm, v_gla_w_in, v_gla_w_out, v_gla_w_gate_up, v_gla_b_gate, v_gla_norm):
    w = dict(norm_mix=norm_mix, norm_mlp=norm_mlp, norm_final=norm_final, mlp_w1=mlp_w1, mlp_w2=mlp_w2, ab_w_in=ab_w_in, ab_w_out=ab_w_out, rg_conv_w=rg_conv_w, rg_conv_b=rg_conv_b, rg_w_a=rg_w_a, rg_b_a=rg_b_a, rg_w_x=rg_w_x, rg_b_x=rg_b_x, rg_lambda=rg_lambda, hg_lb_logits=hg_lb_logits, hg_norm=hg_norm, gla_w_in=gla_w_in, gla_w_out=gla_w_out, gla_w_gate_up=gla_w_gate_up, gla_b_gate=gla_b_gate, gla_norm=gla_norm)
    m = dict(norm_mix=m_norm_mix, norm_mlp=m_norm_mlp, norm_final=m_norm_final, mlp_w1=m_mlp_w1, mlp_w2=m_mlp_w2, ab_w_in=m_ab_w_in, ab_w_out=m_ab_w_out, rg_conv_w=m_rg_conv_w, rg_conv_b=m_rg_conv_b, rg_w_a=m_rg_w_a, rg_b_a=m_rg_b_a, rg_w_x=m_rg_w_x, rg_b_x=m_rg_b_x, rg_lambda=m_rg_lambda, hg_lb_logits=m_hg_lb_logits, hg_norm=m_hg_norm, gla_w_in=m_gla_w_in, gla_w_out=m_gla_w_out, gla_w_gate_up=m_gla_w_gate_up, gla_b_gate=m_gla_b_gate, gla_norm=m_gla_norm)
    v = dict(norm_mix=v_norm_mix, norm_mlp=v_norm_mlp, norm_final=v_norm_final, mlp_w1=v_mlp_w1, mlp_w2=v_mlp_w2, ab_w_in=v_ab_w_in, ab_w_out=v_ab_w_out, rg_conv_w=v_rg_conv_w, rg_conv_b=v_rg_conv_b, rg_w_a=v_rg_w_a, rg_b_a=v_rg_b_a, rg_w_x=v_rg_w_x, rg_b_x=v_rg_b_x, rg_lambda=v_rg_lambda, hg_lb_logits=v_hg_lb_logits, hg_norm=v_hg_norm, gla_w_in=v_gla_w_in, gla_w_out=v_gla_w_out, gla_w_gate_up=v_gla_w_gate_up, gla_b_gate=v_gla_b_gate, gla_norm=v_gla_norm)
    chip = 2 * lax.axis_index("x") + lax.axis_index("y")
    core = lax.axis_index("c")
    sharded_shapes = [w[n].shape for n in SMALL_SHARDED]

    slots = [_into_slot(f"cast_{n}{layer}", w[n], chip, N_CHIPS, BF16, 512, layer) for n, layer in MATRICES]
    early = [i for i, (n, _) in enumerate(MATRICES) if n in EARLY_MATRICES]
    rest = [i for i in range(len(MATRICES)) if i not in early]

    def named(indices, arrays):
        big = {}
        for i, t in zip(indices, arrays):
            big.setdefault(MATRICES[i][0], []).append(t)
        return {n: (v if n in ("mlp_w1", "mlp_w2") else v[0]) for n, v in big.items()}

    gathered = _gather_chips("gather_early", [slots[i] for i in early])
    send_sems, recv_sems, *in_flight, token = _gather_start("gather_rest_start", [slots[i] for i in rest], gathered[0])

    def late_weights(after):
        landed = _gather_wait("gather_rest_wait", in_flight, send_sems, recv_sems, after)
        return _prepare_matrices(named(rest, _hand_over("gather_rest_share", list(landed))))

    big = named(early, gathered)
    vectors = _pack([w[n] for n in SMALL_SHARDED])
    vectors = _into_slot("place_vectors", vectors, chip, N_CHIPS, F32, vectors.shape[0])
    small_all = _unpack(_gather_chips("gather_vectors", [vectors])[0], sharded_shapes, lead=1)
    full = {n: w[n] for n in SMALL_REPLICATED}
    for n, t in zip(SMALL_SHARDED, small_all):
        full[n] = _join_chips(t, t.ndim - 2)

    def halves(t):
        return t.reshape(N_CHIPS, 2, t.shape[1] // 2, t.shape[2])

    in_flight_grads = {}

    def emit(tag, arrays32, arrays16):
        n = len(arrays16)
        send, recv, *rest = _reduce_start(f"reduce_{tag}_start", [halves(t) for t in arrays16])
        in_flight_grads[tag] = ([halves(t) for t in arrays32], rest[:n], rest[n:2 * n], send, recv)
        return rest[-1]

    loss_part, grad_x, g_kernel = _local_step(
        x[0], loss_target[0], _prepare_weights(big, full), token, late_weights, emit)
    g_big, g_full = _finish_grads(g_kernel)

    small_names = SMALL_REPLICATED + SMALL_SHARDED
    reduced_shapes = [g_full[n].shape for n in small_names] + [loss_part.shape]
    g_small = _pack([g_full[n] for n in small_names] + [loss_part])
    device = 2 * chip + core
    g_small = _into_slot("place_small", g_small, device, N_DEV, F32, g_small.shape[0])
    small_send, small_recv, small_in_flight, small_token = _gather_all_start("reduce_small_start", g_small)

    mine = {}
    for tag, (own, srcs, lands, send, recv) in in_flight_grads.items():
        landed = _reduce_wait(f"reduce_{tag}_wait", srcs, lands, send, recv, small_token)
        mine[tag] = [_reduce_sum(f"reduce_add_{tag}{i}", o, f, chip, core) for i, (o, f) in enumerate(zip(own, landed))]
    ordered = [mine["mlp0"][0], mine["mlp1"][0], mine["mlp0"][1], mine["mlp1"][1], mine["ab"][0], mine["mlp0"][2],
               *mine["gla"]]
    reduced = [t.reshape(2 * t.shape[1], t.shape[2]) for t in _pair_gather("reduce_share", ordered)]
    by_name = {n: [] for n, _ in MATRICES}
    for (n, _), t in zip(MATRICES, reduced):
        by_name[n].append(t)
    grads = {n: jnp.stack(v) for n, v in by_name.items()}

    g_small_all = _gather_all_wait("reduce_small_wait", small_in_flight, small_send, small_recv, reduced[0])
    g_small_red = _sum_blocks("reduce_small_add", g_small_all, g_small_all.shape[1])
    *small_red, loss_sum = _unpack(g_small_red, reduced_shapes)
    loss = loss_sum[0, 0]
    g_small_full = dict(zip(small_names, small_red))
    for n in SMALL_REPLICATED:
        grads[n] = g_small_full[n]
    for n in SMALL_SHARDED:
        width = w[n].shape[-1]
        grads[n] = lax.dynamic_slice_in_dim(g_small_full[n], chip * width, width, axis=g_small_full[n].ndim - 1)

    delta, new_m, new_v = {}, {}, {}
    for n in by_name:
        flat = [t.reshape(-1, t.shape[-1]) for t in (w[n], grads[n], m[n], v[n])]
        for dst, t in zip((delta, new_m, new_v), _adam(f"adam_{n}", *flat)):
            dst[n] = t.reshape(w[n].shape)
    small_shapes = [w[n].shape for n in small_names]
    packs = [_pack([src[n] for n in small_names]) for src in (w, grads, m, v)]
    d_small, m_small, v_small = _adam("adam_small", *packs)
    for dst, buf in ((delta, d_small), (new_m, m_small), (new_v, v_small)):
        dst.update(zip(small_names, _unpack(buf, small_shapes)))

    return (loss, grad_x[None], *[grads[n] for n in WEIGHTS], *[delta[n] for n in WEIGHTS],
            *[new_m[n] for n in WEIGHTS], *[new_v[n] for n in WEIGHTS])
```

```python
import functools

import jax
import jax.numpy as jnp
from jax import lax
from jax.experimental import pallas as pl
from jax.experimental.pallas import tpu as pltpu

F32 = jnp.float32
BF16 = jnp.bfloat16
MESH = pl.DeviceIdType.MESH

LANES = 128
CHUNK = 64
ATTN_SUB = 4
EPS = 1e-6
RG_C = 8.0
N_CHIPS = 4
N_DEV = 8
GLA_IN_WIDTH = 3104
GLA_IN_PAD = 3200
VMEM_LIMIT = 56 * 1024 * 1024

ADAM_LR = 0.001
ADAM_B1 = 0.9
ADAM_B2 = 0.999
ADAM_EPS = 1e-08
ADAM_WD = 0.01
ADAM_STEP = 10


def _raw_dot(a, b, ca, cb):
    return lax.dot_general(a.astype(BF16), b.astype(BF16), (((ca,), (cb,)), ((), ())),
                           preferred_element_type=F32)


def _raw_nn(a, b):
    return _raw_dot(a, b, 1, 0)


def _raw_nt(a, b):
    return _raw_dot(a, b, 1, 1)


def _raw_tn(a, b):
    return _raw_dot(a, b, 0, 0)


@jax.custom_vjp
def _dot_nn(a, b):
    return _raw_nn(a, b)


def _dot_nn_fwd(a, b):
    return _raw_nn(a, b), (a, b)


def _dot_nn_bwd(res, g):
    a, b = res
    return _raw_nt(g, b), _raw_tn(a, g)


_dot_nn.defvjp(_dot_nn_fwd, _dot_nn_bwd)


@jax.custom_vjp
def _dot_nt(a, b):
    return _raw_nt(a, b)


def _dot_nt_fwd(a, b):
    return _raw_nt(a, b), (a, b)


def _dot_nt_bwd(res, g):
    a, b = res
    return _raw_nn(g, b), _raw_tn(g, a)


_dot_nt.defvjp(_dot_nt_fwd, _dot_nt_bwd)


@jax.custom_vjp
def _dot_tn(a, b):
    return _raw_tn(a, b)


def _dot_tn_fwd(a, b):
    return _raw_tn(a, b), (a, b)


def _dot_tn_bwd(res, g):
    a, b = res
    return _raw_nt(b, g), _raw_nn(a, g)


_dot_tn.defvjp(_dot_tn_fwd, _dot_tn_bwd)


def _tile(n, pref):
    if n <= pref:
        return n
    t = (pref // LANES) * LANES
    while t > LANES and n % t:
        t -= LANES
    assert n % t == 0, (n, pref)
    return t


def _params(sem):
    return pltpu.CompilerParams(dimension_semantics=sem, vmem_limit_bytes=VMEM_LIMIT)


def _rowcall(name, fn, rows, pars, row_outs, par_outs=(), tm=256, pin=None):
    if pin is not None:
        inner, pars = fn, list(pars) + [pin]
        fn = lambda *vals: inner(*vals[:-1])
    n_rows = rows[0][0].shape[0]
    tm = min(tm, n_rows)
    assert n_rows % tm == 0
    n_r, n_p, n_ro = len(rows), len(pars), len(row_outs)

    def body(*refs):
        vals = [r[...].astype(F32) for r in refs[:n_r + n_p]]
        outs = fn(*vals)
        o_refs = refs[n_r + n_p:n_r + n_p + n_ro]
        po_refs = refs[n_r + n_p + n_ro:]
        for o_ref, val in zip(o_refs, outs[:n_ro]):
            o_ref[...] = val.astype(o_ref.dtype)
        first = pl.program_id(0) == 0
        for po_ref, val in zip(po_refs, outs[n_ro:]):
            @pl.when(first)
            def _():
                po_ref[...] = val

            @pl.when(jnp.logical_not(first))
            def _():
                po_ref[...] += val

    def const_map(nd):
        return lambda i: (0,) * nd

    def row_spec(w, cb):
        return pl.BlockSpec((tm, w), lambda i: (i, cb))

    in_specs = [row_spec(w, cb) for _, w, cb in rows]
    in_specs += [pl.BlockSpec(p.shape, const_map(p.ndim)) for p in pars]
    out_specs = [pl.BlockSpec((tm, w), lambda i: (i, 0)) for w, _ in row_outs]
    out_specs += [pl.BlockSpec(tuple(s), const_map(len(s))) for s in par_outs]
    out_shape = [jax.ShapeDtypeStruct((n_rows, w), dt) for w, dt in row_outs]
    out_shape += [jax.ShapeDtypeStruct(tuple(s), F32) for s in par_outs]
    return pl.pallas_call(
        body, name=name, grid=(n_rows // tm,), in_specs=in_specs, out_specs=out_specs, out_shape=out_shape,
        compiler_params=_params(("arbitrary",) if par_outs else ("parallel",)),
    )(*[r[0] for r in rows], *pars)


def _vjp_of(fn, n_prim, n_out, n_par, n_pass=0):
    def bwd(*args):
        prim = args[:n_prim]
        cts = args[n_prim:n_prim + n_out]
        passes = args[n_prim + n_out:n_prim + n_out + 2 * n_pass]
        pars = args[n_prim + n_out + 2 * n_pass:]
        _, vjp = jax.vjp(fn, *prim, *pars)
        grads = vjp(tuple(cts))
        sums = tuple(passes[2 * i] + passes[2 * i + 1] for i in range(n_pass))
        return tuple(grads[:n_prim]) + sums + tuple(grads[n_prim:])
    return bwd


def _mm(name, a, b, mode="nn", extras=(), epi=None, out_dtypes=(F32,), a_pro=None, out_split=None,
        epi_pars=(), row_sum=False, pin=None, tm=1024, tn=1024, tk=1024):
    split = b.shape[0] if b.ndim == 3 else None
    b_rows, b_cols = b.shape[-2:]
    if mode == "nn":
        (m, k), n = a.shape, b_cols * (split or 1)
    elif mode == "nt":
        (m, k), n = a.shape, b_rows
        assert k == b_cols * (split or 1)
    else:
        assert split is None
        (k, m), n = a.shape, b_cols
    tm, tk = _tile(m, tm), _tile(k, tk)
    tn = _tile(n // out_split, tn) if out_split else _tile(n, tn)
    if split and mode == "nn":
        tn = _tile(b_cols, tn)
    if split and mode == "nt":
        tk = _tile(b_cols, tk)
    nk = k // tk
    raw = {"nn": _raw_nn, "nt": _raw_nt, "tn": _raw_tn}[mode]
    n_e, n_p, n_o = len(extras), len(epi_pars), len(out_dtypes)
    n_in = n_e + n_p + (0 if pin is None else 1)
    if epi is None:
        epi = lambda acc: (acc,)

    def body(a_ref, b_ref, *rest):
        e_refs, p_refs, o_refs = rest[:n_e], rest[n_e:n_e + n_p], rest[n_in:n_in + n_o]
        kk = pl.program_id(2)
        a_tile = a_ref[...] if a_pro is None else a_pro(a_ref[...].astype(F32))
        part = raw(a_tile, b_ref[...])

        def finish(total):
            res = epi(total, *[e[...].astype(F32) for e in e_refs], *[p[...] for p in p_refs])
            for o_ref, r in zip(o_refs, res):
                o_ref[...] = r.astype(o_ref.dtype)
            if row_sum:
                rest[n_in + n_o][...] = res[n_o]

        if nk == 1:
            finish(part)
            return
        acc = rest[-1]

        @pl.when(kk == 0)
        def _():
            acc[...] = part

        @pl.when((kk > 0) & (kk < nk - 1))
        def _():
            acc[...] += part

        @pl.when(kk == nk - 1)
        def _():
            finish(acc[...] + part)

    a_spec = pl.BlockSpec((tk, tm), lambda i, j, kk: (kk, i)) if mode == "tn" else pl.BlockSpec((tm, tk), lambda i, j, kk: (i, kk))
    if split and mode == "nn":
        per = b_cols // tn
        b_spec = pl.BlockSpec((None, tk, tn), lambda i, j, kk: (j // per, kk, j % per))
    elif split:
        per = b_cols // tk
        b_spec = pl.BlockSpec((None, tn, tk), lambda i, j, kk: (kk // per, j, kk % per))
    elif mode == "nt":
        b_spec = pl.BlockSpec((tn, tk), lambda i, j, kk: (j, kk))
    else:
        b_spec = pl.BlockSpec((tk, tn), lambda i, j, kk: (kk, j))
    mn_spec = pl.BlockSpec((tm, tn), lambda i, j, kk: (i, j))
    if out_split:
        assert not extras
        per_out = n // out_split // tn
        out_spec = pl.BlockSpec((None, tm, tn), lambda i, j, kk: (j // per_out, i, j % per_out))
        out_shapes = [jax.ShapeDtypeStruct((out_split, m, n // out_split), dt) for dt in out_dtypes]
    else:
        out_spec = mn_spec
        out_shapes = [jax.ShapeDtypeStruct((m, n), dt) for dt in out_dtypes]
    out_specs = [out_spec] * n_o
    if row_sum:
        out_specs.append(pl.BlockSpec((None, 1, tn), lambda i, j, kk: (i, 0, j)))
        out_shapes.append(jax.ShapeDtypeStruct((m // tm, 1, n), F32))
    in_specs = [a_spec, b_spec] + [mn_spec] * n_e
    in_specs += [pl.BlockSpec(p.shape, functools.partial(lambda i, j, kk, nd: (0,) * nd, nd=p.ndim)) for p in epi_pars]
    in_specs += [] if pin is None else [pl.BlockSpec(memory_space=pl.ANY)]
    outs = pl.pallas_call(
        body, name=name, grid=(m // tm, n // tn, nk), in_specs=in_specs, out_specs=out_specs, out_shape=out_shapes,
        scratch_shapes=[pltpu.VMEM((tm, tn), F32)] if nk > 1 else [],
        compiler_params=_params(("parallel", "parallel", "arbitrary")),
    )(a, b, *extras, *epi_pars, *([] if pin is None else [pin]))
    return outs[0] if len(outs) == 1 else outs


def _sigmoid(x):
    return jax.nn.sigmoid(x)


def _silu(x):
    return x * _sigmoid(x)


def _softplus(x):
    return jnp.maximum(x, 0.0) + jnp.log1p(jnp.exp(-jnp.abs(x)))


def _rmsnorm_fn(x, gain):
    return (x * lax.rsqrt(jnp.mean(x * x, axis=-1, keepdims=True) + EPS) * gain,)


def _head_norm(o, gain, n_heads):
    w = o.shape[-1] // n_heads
    parts = []
    for h in range(n_heads):
        oh = o[:, h * w:(h + 1) * w]
        parts.append(oh * lax.rsqrt(jnp.mean(oh * oh, axis=-1, keepdims=True) + EPS))
    return jnp.concatenate(parts, axis=-1) * gain


@jax.custom_jvp
def _neg_expm1(x):
    u = jnp.exp(x)
    is_one = u == 1.0
    return jnp.where(is_one, -x, (1.0 - u) * x / jnp.log(jnp.where(is_one, 2.0, u)))


@_neg_expm1.defjvp
def _neg_expm1_jvp(primals, tangents):
    (x,), (t,) = primals, tangents
    return _neg_expm1(x), -jnp.exp(x) * t


def _rg_gates_fn(xc, wa, wx, ba, bx, lam):
    outs = []
    for d in range(2):
        r = _sigmoid(_dot_nn(xc, wa[d]) + ba[d:d + 1])
        i = _sigmoid(_dot_nn(xc, wx[d]) + bx[d:d + 1])
        log_a = -RG_C * r * _softplus(-lam[d:d + 1])
        outs.append(jnp.exp(log_a))
        outs.append(jnp.sqrt(_neg_expm1(2.0 * log_a)) * (i * xc))
    return tuple(outs)


def _hg_pre_fn(q, f_f, f_b, logits):
    mx = jnp.maximum(logits[0:1], logits[1:2])
    e0 = jnp.exp(logits[0:1] - mx)
    e1 = jnp.exp(logits[1:2] - mx)
    lb = e0 / (e0 + e1)
    outs = [_silu(q)]
    for f in (f_f, f_b):
        outs.append((1.0 - lb) * _sigmoid(-f))
        outs.append(jnp.log(lb + (1.0 - lb) * _sigmoid(f)))
    return tuple(outs)


def _post0_fn(hs, ga, o, g, gain):
    ya = hs * jax.nn.gelu(ga, approximate=True)
    yb = _head_norm(o, gain, 4) * _silu(g)
    return (jnp.concatenate([ya, yb], axis=-1),)


def _post0_fwd_fn(h_f, h_b, ga, o_f, o_b, g, gain):
    return _post0_fn(h_f + h_b, ga, o_f + o_b, g, gain)


def _post0_bwd_fn(h_f, h_b, ga, o_f, o_b, g, dmix, gain):
    _, vjp = jax.vjp(_post0_fn, h_f + h_b, ga, o_f + o_b, g, gain)
    return vjp((dmix,))


def _gla_pre_fn(q, lr, w_up, b_gate):
    outs = [q * (128.0 ** -0.5)]
    for d in range(2):
        z = _dot_nn(lr, w_up[d]) + b_gate[d:d + 1]
        outs.append(-_softplus(-z) * (1.0 / 16.0))
    return tuple(outs)


def _gla_post_fn(o, r, gain):
    return (_head_norm(o, gain, 4) * _silu(r),)


def _gla_post_fwd_fn(o_f, o_b, r, gain):
    return _gla_post_fn(o_f + o_b, r, gain)


def _gla_post_bwd_fn(o_f, o_b, r, dmix, gain):
    _, vjp = jax.vjp(_gla_post_fn, o_f + o_b, r, gain)
    return vjp((dmix,))


def _relu2_bwd_epi(acc, hid):
    return (acc * 2.0 * jnp.maximum(hid, 0.0),)


def _relu2(x):
    r = jnp.maximum(x, 0.0)
    return r * r


def _add_epi(acc, res):
    return (acc + res,)


def _loss_head_fn(h, target, gain):
    def f(h, gain):
        y = _rmsnorm_fn(h, gain)[0]
        err = y - target
        return 0.5 * jnp.sum(jnp.mean(err * err, axis=-1, keepdims=True))
    loss, (dh, dgain) = jax.value_and_grad(f, argnums=(0, 1))(h, gain)
    return dh, jnp.full((1, LANES), loss, F32), dgain


def _adam_fn(w, g, m, v):
    m2 = ADAM_B1 * m + (1.0 - ADAM_B1) * g
    v2 = ADAM_B2 * v + (1.0 - ADAM_B2) * (g * g)
    m_hat = m2 / (1.0 - ADAM_B1 ** ADAM_STEP)
    v_hat = v2 / (1.0 - ADAM_B2 ** ADAM_STEP)
    delta = -ADAM_LR * (m_hat / (jnp.sqrt(v_hat) + ADAM_EPS) + ADAM_WD * w)
    return delta, m2, v2


def _shifted(x, t_idx, off):
    n = x.shape[0]
    rolled = pltpu.roll(x, (-off) % n, 0)
    valid = (t_idx + off >= 0) & (t_idx + off < n)
    return jnp.where(valid, rolled, 0.0)


def _conv_fwd(name, src, colblock, w, b):
    n_rows, width = src.shape[0], w.shape[1]

    def body(x_ref, w_ref, b_ref, o_ref):
        x = x_ref[...]
        t_idx = lax.broadcasted_iota(jnp.int32, x.shape, 0)
        acc = b_ref[...] + w_ref[2:3, :] * x
        acc += w_ref[0:1, :] * _shifted(x, t_idx, -2)
        acc += w_ref[1:2, :] * _shifted(x, t_idx, -1)
        acc += w_ref[3:4, :] * _shifted(x, t_idx, 1)
        o_ref[...] = acc

    nb = width // LANES
    return pl.pallas_call(
        body, name=name, grid=(nb,),
        in_specs=[pl.BlockSpec((n_rows, LANES), lambda j: (0, colblock * nb + j)),
                  pl.BlockSpec((4, LANES), lambda j: (0, j)), pl.BlockSpec((1, LANES), lambda j: (0, j))],
        out_specs=pl.BlockSpec((n_rows, LANES), lambda j: (0, j)),
        out_shape=jax.ShapeDtypeStruct((n_rows, width), F32),
        compiler_params=_params(("parallel",)),
    )(src, w, b)


def _conv_bwd(name, src, colblock, w, d):
    n_rows, width = src.shape[0], w.shape[1]

    def body(x_ref, w_ref, d_ref, dx_ref, dw_ref, db_ref):
        x = x_ref[...]
        g = d_ref[...]
        t_idx = lax.broadcasted_iota(jnp.int32, x.shape, 0)
        dx = w_ref[2:3, :] * g
        dx += w_ref[0:1, :] * _shifted(g, t_idx, 2)
        dx += w_ref[1:2, :] * _shifted(g, t_idx, 1)
        dx += w_ref[3:4, :] * _shifted(g, t_idx, -1)
        dx_ref[...] = dx.astype(dx_ref.dtype)
        dw_ref[0:1, :] = jnp.sum(g * _shifted(x, t_idx, -2), axis=0, keepdims=True)
        dw_ref[1:2, :] = jnp.sum(g * _shifted(x, t_idx, -1), axis=0, keepdims=True)
        dw_ref[2:3, :] = jnp.sum(g * x, axis=0, keepdims=True)
        dw_ref[3:4, :] = jnp.sum(g * _shifted(x, t_idx, 1), axis=0, keepdims=True)
        db_ref[...] = jnp.sum(g, axis=0, keepdims=True)

    nb = width // LANES
    return pl.pallas_call(
        body, name=name, grid=(nb,),
        in_specs=[pl.BlockSpec((n_rows, LANES), lambda j: (0, colblock * nb + j)),
                  pl.BlockSpec((4, LANES), lambda j: (0, j)),
                  pl.BlockSpec((n_rows, LANES), lambda j: (0, j))],
        out_specs=[pl.BlockSpec((n_rows, LANES), lambda j: (0, j)), pl.BlockSpec((4, LANES), lambda j: (0, j)),
                   pl.BlockSpec((1, LANES), lambda j: (0, j))],
        out_shape=[jax.ShapeDtypeStruct((n_rows, width), BF16), jax.ShapeDtypeStruct((4, width), F32),
                   jax.ShapeDtypeStruct((1, width), F32)],
        compiler_params=_params(("parallel",)),
    )(src, w, d)


SUBLANES = 8
SCAN_UNROLL = 8


def _shift_rows(x, d, fill):
    n = x.shape[0]
    t = lax.broadcasted_iota(jnp.int32, x.shape, 0)
    valid = (t >= d) if d > 0 else (t < n + d)
    return jnp.where(valid, pltpu.roll(x, d % n, 0), fill)


def _tile_scan(a, u, reverse):
    d = 1
    while d < a.shape[0]:
        s = -d if reverse else d
        a_sh, u_sh = _shift_rows(a, s, 1.0), _shift_rows(u, s, 0.0)
        u = u + a * u_sh
        a = a * a_sh
        d *= 2
    return a, u


def _edge_row(x, reverse):
    return x[0:1, :] if reverse else x[SUBLANES - 1:SUBLANES, :]


def _scan_specs(n_rows, n):
    return [pl.BlockSpec((n_rows, LANES), lambda j: (0, j))] * n


def _scan_tile(a_ref, u_ref, h_ref, i, carry, reverse):
    n_tiles = a_ref.shape[0] // SUBLANES
    tile = (n_tiles - 1 - i) if reverse else i
    rows = pl.ds(pl.multiple_of(tile * SUBLANES, SUBLANES), SUBLANES)
    acc_a, acc_u = _tile_scan(a_ref[rows, :], u_ref[rows, :], reverse)
    h = acc_u + acc_a * carry
    h_ref[rows, :] = h
    return _edge_row(h, reverse)


def _scan_fwd(name, a_f, u_f, a_b, u_b):
    n_rows, width = a_f.shape

    def body(af_ref, uf_ref, ab_ref, ub_ref, hf_ref, hb_ref):
        def step(i, carry):
            return (_scan_tile(af_ref, uf_ref, hf_ref, i, carry[0], False),
                    _scan_tile(ab_ref, ub_ref, hb_ref, i, carry[1], True))
        zero = jnp.zeros((1, LANES), F32)
        lax.fori_loop(0, n_rows // SUBLANES, step, (zero, zero), unroll=SCAN_UNROLL)

    return pl.pallas_call(
        body, name=name, grid=(width // LANES,), in_specs=_scan_specs(n_rows, 4), out_specs=_scan_specs(n_rows, 2),
        out_shape=[jax.ShapeDtypeStruct((n_rows, width), F32)] * 2, compiler_params=_params(("parallel",)),
    )(a_f, u_f, a_b, u_b)


def _scan_bwd_tile(a_ref, h_ref, dh_ref, du_ref, da_ref, i, carry, reverse):
    n_rows = a_ref.shape[0]
    n_tiles = n_rows // SUBLANES
    against = not reverse
    one = -1 if against else 1
    g_in, a_edge = carry
    tile = (n_tiles - 1 - i) if against else i
    start = pl.multiple_of(tile * SUBLANES, SUBLANES)
    rows = pl.ds(start, SUBLANES)
    a_tile = a_ref[rows, :]
    coeff = _shift_rows(a_tile, one, a_edge)
    acc_a, acc_u = _tile_scan(coeff, dh_ref[rows, :], against)
    g = acc_u + acc_a * g_in
    du_ref[rows, :] = g
    outside = (start + SUBLANES) if reverse else (start - 1)
    inside = (outside >= 0) & (outside < n_rows)
    h_edge = jnp.where(inside, h_ref[pl.ds(jnp.clip(outside, 0, n_rows - 1), 1), :], 0.0)
    da_ref[rows, :] = g * _shift_rows(h_ref[rows, :], -one, h_edge)
    return _edge_row(g, against), _edge_row(a_tile, against)


def _scan_bwd(name, a_f, h_f, a_b, h_b, dh):
    n_rows, width = a_f.shape

    def body(af_ref, hf_ref, ab_ref, hb_ref, dh_ref, duf_ref, daf_ref, dub_ref, dab_ref):
        def step(i, carry):
            return (_scan_bwd_tile(af_ref, hf_ref, dh_ref, duf_ref, daf_ref, i, carry[0], False),
                    _scan_bwd_tile(ab_ref, hb_ref, dh_ref, dub_ref, dab_ref, i, carry[1], True))
        zero = jnp.zeros((1, LANES), F32)
        lax.fori_loop(0, n_rows // SUBLANES, step, ((zero, zero), (zero, zero)), unroll=SCAN_UNROLL)

    return pl.pallas_call(
        body, name=name, grid=(width // LANES,), in_specs=_scan_specs(n_rows, 5), out_specs=_scan_specs(n_rows, 4),
        out_shape=[jax.ShapeDtypeStruct((n_rows, width), F32)] * 4, compiler_params=_params(("parallel",)),
    )(a_f, h_f, a_b, h_b, dh)


def _tri_mask(c, reverse):
    row = lax.broadcasted_iota(jnp.int32, (c, c), 0)
    col = lax.broadcasted_iota(jnp.int32, (c, c), 1)
    return (col >= row) if reverse else (col <= row)


def _cumsum_rows(x, reverse):
    tri = _tri_mask(x.shape[0], reverse).astype(BF16)
    hi = x.astype(BF16)
    rest = x - hi.astype(F32)
    mid = rest.astype(BF16)
    lo = (rest - mid.astype(F32)).astype(BF16)
    return _raw_nn(tri, hi) + _raw_nn(tri, mid) + _raw_nn(tri, lo)


@functools.partial(jax.custom_vjp, nondiff_argnums=(1,))
def _cumsum(x, reverse):
    return _cumsum_rows(x, reverse)


def _cumsum_fwd(x, reverse):
    return _cumsum_rows(x, reverse), None


def _cumsum_bwd(reverse, _, g):
    return (_cumsum_rows(g, not reverse),)


_cumsum.defvjp(_cumsum_fwd, _cumsum_bwd)


def _chunks_fn(qs, ks, vs, lfs, sts, reverses):
    n, c = len(qs), qs[0].shape[0]
    every = range(n)
    tris = [_tri_mask(c, r) for r in reverses]
    cums = [_cumsum(lfs[i], reverses[i]) for i in every]
    rid = lax.broadcasted_iota(jnp.int32, cums[0].shape, 0)

    def pick(cum, r):
        return jnp.sum(jnp.where(rid == r, cum, 0.0), axis=0, keepdims=True)

    refs = [pick(cums[i], (c - 1 - c // 2) if reverses[i] else c // 2) for i in every]
    lasts = [pick(cums[i], 0 if reverses[i] else c - 1) for i in every]
    q_in = [qs[i] * jnp.exp(cums[i] - refs[i]) for i in every]
    k_in = [ks[i] * jnp.exp(refs[i] - cums[i]) for i in every]
    scores = [jnp.where(tris[i], _dot_nt(q_in[i], k_in[i]), 0.0) for i in every]
    o_intra = [_dot_nn(scores[i], vs[i]) for i in every]
    q_out = [qs[i] * jnp.exp(cums[i]) for i in every]
    o_inter = [_dot_nt(q_out[i], sts[i]) for i in every]
    k_state = [ks[i] * jnp.exp(lasts[i] - cums[i]) for i in every]
    upd = [_dot_tn(vs[i], k_state[i]) for i in every]
    st_new = [sts[i] * jnp.exp(lasts[i]) + upd[i] for i in every]
    return [o_intra[i] + o_inter[i] for i in every], st_new


def _attn_fwd(name, q, k_f, k_b, v, lf_f, lf_b, n_heads, dk, dv):
    n_rows = q[0].shape[0]
    n_chunks = n_rows // CHUNK
    n_steps = n_chunks // ATTN_SUB
    wk, wv = n_heads * dk, n_heads * dv

    def spec(width, off, rev):
        return pl.BlockSpec((CHUNK * ATTN_SUB, width), lambda n: ((n_steps - 1 - n) if rev else n, off))

    def sspec(rev):
        return pl.BlockSpec((ATTN_SUB, n_heads, dv, dk), lambda n: ((n_steps - 1 - n) if rev else n, 0, 0, 0))

    def body(qf, kf, vf, lff, qb, kb, vb, lfb, of_ref, ob_ref, sf_ref, sb_ref, st):
        @pl.when(pl.program_id(0) == 0)
        def _():
            st[...] = jnp.zeros_like(st)

        ins = ((qf, kf, vf, lff), (qb, kb, vb, lfb))
        chains = [(d, h) for d in range(2) for h in range(n_heads)]
        ck = [slice(h * dk, (h + 1) * dk) for h in range(n_heads)]
        cv = [slice(h * dv, (h + 1) * dv) for h in range(n_heads)]
        sts = [st[d, h] for d, h in chains]
        done = []
        for sub in range(ATTN_SUB):
            local = (sub, ATTN_SUB - 1 - sub)
            rows = [slice(local[d] * CHUNK, (local[d] + 1) * CHUNK) for d in range(2)]
            qs = [ins[d][0][rows[d], ck[h]] for d, h in chains]
            ks = [ins[d][1][rows[d], ck[h]] for d, h in chains]
            vs = [ins[d][2][rows[d], cv[h]] for d, h in chains]
            lfs = [ins[d][3][rows[d], ck[h]] for d, h in chains]
            os_, st_new = _chunks_fn(qs, ks, vs, lfs, sts, [d == 1 for d, _ in chains])
            done.append((local, rows, sts, os_))
            sts = st_new
        for local, rows, entered, os_ in done:
            for i, (d, h) in enumerate(chains):
                (sf_ref, sb_ref)[d][local[d], h] = entered[i].astype(BF16)
                (of_ref, ob_ref)[d][rows[d], cv[h]] = os_[i]
        for i, (d, h) in enumerate(chains):
            st[d, h] = sts[i]

    in_specs = [spec(wk, q[1], False), spec(wk, k_f[1], False), spec(wv, v[1], False), spec(wk, lf_f[1], False),
                spec(wk, q[1], True), spec(wk, k_b[1], True), spec(wv, v[1], True), spec(wk, lf_b[1], True)]
    return pl.pallas_call(
        body, name=name, grid=(n_steps,), in_specs=in_specs,
        out_specs=[spec(wv, 0, False), spec(wv, 0, True), sspec(False), sspec(True)],
        out_shape=[jax.ShapeDtypeStruct((n_rows, wv), F32)] * 2
        + [jax.ShapeDtypeStruct((n_chunks, n_heads, dv, dk), BF16)] * 2,
        scratch_shapes=[pltpu.VMEM((2, n_heads, dv, dk), F32)],
        compiler_params=_params(("arbitrary",)),
    )(q[0], k_f[0], v[0], lf_f[0], q[0], k_b[0], v[0], lf_b[0])


def _attn_bwd(name, q, k_f, k_b, v, lf_f, lf_b, st_f, st_b, do, n_heads, dk, dv, out_dtype=F32):
    n_rows = q[0].shape[0]
    n_chunks = n_rows // CHUNK
    n_steps = n_chunks // ATTN_SUB
    wk, wv = n_heads * dk, n_heads * dv

    def spec(width, off, rev):
        return pl.BlockSpec((CHUNK * ATTN_SUB, width), lambda n: (n if rev else (n_steps - 1 - n), off))

    def sspec(rev):
        return pl.BlockSpec((ATTN_SUB, n_heads, dv, dk), lambda n: (n if rev else (n_steps - 1 - n), 0, 0, 0))

    def body(qf, kf, vf, lff, sf, dof, qb, kb, vb, lfb, sb, dob,
             dqf, dkf, dvf, dlff, dqb, dkb, dvb, dlfb, dst):
        @pl.when(pl.program_id(0) == 0)
        def _():
            dst[...] = jnp.zeros_like(dst)

        ins = ((qf, kf, vf, lff, sf, dof), (qb, kb, vb, lfb, sb, dob))
        outs = ((dqf, dkf, dvf, dlff), (dqb, dkb, dvb, dlfb))
        chains = [(d, h) for d in range(2) for h in range(n_heads)]
        ck = [slice(h * dk, (h + 1) * dk) for h in range(n_heads)]
        cv = [slice(h * dv, (h + 1) * dv) for h in range(n_heads)]
        fn = functools.partial(_chunks_fn, reverses=[d == 1 for d, _ in chains])
        dsts = [dst[d, h] for d, h in chains]
        done = []
        for sub in range(ATTN_SUB):
            local = (ATTN_SUB - 1 - sub, sub)
            rows = [slice(local[d] * CHUNK, (local[d] + 1) * CHUNK) for d in range(2)]
            qs = [ins[d][0][rows[d], ck[h]] for d, h in chains]
            ks = [ins[d][1][rows[d], ck[h]] for d, h in chains]
            vs = [ins[d][2][rows[d], cv[h]] for d, h in chains]
            lfs = [ins[d][3][rows[d], ck[h]] for d, h in chains]
            sts = [ins[d][4][local[d], h].astype(F32) for d, h in chains]
            dos = [ins[d][5][rows[d], cv[h]] for d, h in chains]
            _, vjp = jax.vjp(fn, qs, ks, vs, lfs, sts)
            dqs, dks, dvs, dlfs, dsts = vjp((dos, dsts))
            done.append((rows, dqs, dks, dvs, dlfs))
        for rows, dqs, dks, dvs, dlfs in done:
            for i, (d, h) in enumerate(chains):
                dq_r, dk_r, dv_r, dlf_r = outs[d]
                dq_r[rows[d], ck[h]] = dqs[i].astype(dq_r.dtype)
                dk_r[rows[d], ck[h]] = dks[i].astype(dk_r.dtype)
                dv_r[rows[d], cv[h]] = dvs[i].astype(dv_r.dtype)
                dlf_r[rows[d], ck[h]] = dlfs[i].astype(dlf_r.dtype)
        for i, (d, h) in enumerate(chains):
            dst[d, h] = dsts[i]

    def dir_specs(kk, lf, rev):
        return [spec(wk, q[1], rev), spec(wk, kk[1], rev), spec(wv, v[1], rev), spec(wk, lf[1], rev), sspec(rev),
                spec(wv, 0, rev)]

    def dir_out_specs(rev):
        return [spec(wk, 0, rev), spec(wk, 0, rev), spec(wv, 0, rev), spec(wk, 0, rev)]

    shapes = [jax.ShapeDtypeStruct((n_rows, wk), out_dtype), jax.ShapeDtypeStruct((n_rows, wk), out_dtype),
              jax.ShapeDtypeStruct((n_rows, wv), out_dtype), jax.ShapeDtypeStruct((n_rows, wk), F32)]
    outs = pl.pallas_call(
        body, name=name, grid=(n_steps,), in_specs=dir_specs(k_f, lf_f, False) + dir_specs(k_b, lf_b, True),
        out_specs=dir_out_specs(False) + dir_out_specs(True), out_shape=shapes + shapes,
        scratch_shapes=[pltpu.VMEM((2, n_heads, dv, dk), F32)],
        compiler_params=_params(("arbitrary",)),
    )(q[0], k_f[0], v[0], lf_f[0], st_f, do, q[0], k_b[0], v[0], lf_b[0], st_b, do)
    return outs[:4], outs[4:]


def _row2(v):
    return v.reshape(1, -1)


def _mlp_fwd(tag, h, gain, w1, w2):
    y = _rowcall(f"{tag}_norm", _rmsnorm_fn, [(h, h.shape[1], 0)], [gain], [(h.shape[1], BF16)], tm=512)[0]
    hid = _mm(f"{tag}_up", y, w1, out_dtypes=(BF16,))
    h_out = _mm(f"{tag}_down", hid, w2, a_pro=_relu2, extras=(h,), epi=_add_epi)
    return h_out, (y, hid)


def _dw(name, a, b, **kw):
    return _mm(name, a, b, mode="tn", epi=lambda acc: (acc, acc), out_dtypes=(F32, BF16), **kw)


def _mlp_bwd(tag, h, gain, w1, w2, saved, dh_out):
    y, hid = saved
    dhid = _mm(f"{tag}_dact", dh_out, w2, mode="nt", extras=(hid,), epi=_relu2_bwd_epi, out_dtypes=(BF16,))
    dw2 = _dw(f"{tag}_dw2", hid, dh_out, a_pro=_relu2)
    dw1 = _dw(f"{tag}_dw1", y, dhid, out_split=N_CHIPS)
    dh, dgain = _dy_norm_bwd(f"{tag}_dy", dhid, w1, h, gain, dh_out)
    return dh, dgain, dw1, dw2


def _dy_norm_bwd(name, dz, w, h, gain, dres, pin=None, **tiles):
    def epi(dy, h_tile, dres_tile, gain_row):
        _, vjp = jax.vjp(lambda u, v: _rmsnorm_fn(u, v)[0], h_tile, gain_row)
        dh, dgain = vjp(dy)
        return dh + dres_tile, dgain

    assert h.shape[1] <= 1024
    tiles.setdefault("tm", 512)
    dh, dgain_parts = _mm(name, dz, w, mode="nt", extras=(h, dres), epi=epi, epi_pars=(gain,), row_sum=True,
                          pin=pin, **tiles)
    return dh, jnp.sum(dgain_parts, axis=0)


def _local_step(x, target, w, pin=None, late=None, emit=None):
    g = {}
    d_model = x.shape[1]
    rg_w = hg_w = d_model // 2
    pins = []

    def send_off(tag, pairs):
        if emit is not None:
            pins.append(emit(tag, [p[0] for p in pairs], [p[1] for p in pairs]))

    def both(fn, pair):
        return [fn(t) for t in pair]

    def chip_major(t):
        return t.reshape(N_CHIPS, t.shape[0] // N_CHIPS, t.shape[1])

    h_a0 = x
    gain = _row2(w["norm_mix"][0])
    y0 = _rowcall("l0_norm", _rmsnorm_fn, [(h_a0, d_model, 0)], [gain], [(d_model, BF16)], tm=512, pin=pin)[0]
    proj0 = _mm("l0_in", y0, w["ab_w_in"])
    conv_w, conv_b = w["rg_conv_w"], _row2(w["rg_conv_b"])
    xc = _conv_fwd("rg_conv", proj0, 0, conv_w, conv_b)
    gate_pars = [w["rg_wa_bd"], w["rg_wx_bd"], w["rg_b_a"], w["rg_b_x"], w["rg_lambda"]]
    a_f, u_f, a_b, u_b = _rowcall("rg_gates", _rg_gates_fn, [(xc, rg_w, 0)], gate_pars, [(rg_w, F32)] * 4)
    hs_f, hs_b = _scan_fwd("rg_scan", a_f, u_f, a_b, u_b)
    hg_rows = [(proj0, hg_w, 2), (proj0, hg_w, 3), (proj0, hg_w, 4)]
    qh, k_f, lf_f, k_b, lf_b = _rowcall("hg_pre", _hg_pre_fn, hg_rows, [w["hg_lb_logits"]], [(hg_w, F32)] * 5)
    iv = (proj0, 5)
    o_f, o_b, st_f, st_b = _attn_fwd("hg_attn", (qh, 0), (k_f, 0), (k_b, 0), iv, (lf_f, 0), (lf_b, 0), 4, 128, 128)
    post0_rows = [(hs_f, rg_w, 0), (hs_b, rg_w, 0), (proj0, rg_w, 1), (o_f, hg_w, 0), (o_b, hg_w, 0), (proj0, hg_w, 6)]
    hg_gain = _row2(w["hg_norm"])
    mix_in0 = _rowcall("l0_post", _post0_fwd_fn, post0_rows, [hg_gain], [(d_model, BF16)])[0]
    if late is not None:
        w = {**w, **late(mix_in0)}
    h_b0 = _mm("l0_out", mix_in0, w["ab_w_out"], extras=(h_a0,), epi=_add_epi)
    h_c0, mlp0 = _mlp_fwd("mlp0", h_b0, _row2(w["norm_mlp"][0]), w["mlp_w1"][0], w["mlp_w2"][0])

    h_a1 = h_c0
    gain1 = _row2(w["norm_mix"][1])
    y1 = _rowcall("l1_norm", _rmsnorm_fn, [(h_a1, d_model, 0)], [gain1], [(d_model, BF16)], tm=512)[0]
    proj1 = _mm("l1_in", y1, w["gla_w_in_pad"], tn=640)
    gla_pars = [w["gla_w_up_pad"], w["gla_b_gate"]]
    gq, glf_f, glf_b = _rowcall("gla_pre", _gla_pre_fn, [(proj1, 512, 0), (proj1, LANES, 24)], gla_pars, [(512, F32)] * 3)
    gk, gv = (proj1, 1), (proj1, 1)
    go_f, go_b, gst_f, gst_b = _attn_fwd("gla_attn", (gq, 0), gk, gk, gv, (glf_f, 0), (glf_b, 0), 4, 128, 256)
    gla_gain = _row2(w["gla_norm"])
    post1_rows = [(go_f, d_model, 0), (go_b, d_model, 0), (proj1, d_model, 2)]
    mix_in1 = _rowcall("l1_post", _gla_post_fwd_fn, post1_rows, [gla_gain], [(d_model, BF16)])[0]
    h_b1 = _mm("l1_out", mix_in1, w["gla_w_out"], extras=(h_a1,), epi=_add_epi)
    h_c1, mlp1 = _mlp_fwd("mlp1", h_b1, _row2(w["norm_mlp"][1]), w["mlp_w1"][1], w["mlp_w2"][1])

    dh, loss, g["norm_final"] = _rowcall(
        "loss_head", _loss_head_fn, [(h_c1, d_model, 0), (target, d_model, 0)], [_row2(w["norm_final"])],
        [(d_model, F32)], [(1, LANES), (1, d_model)], tm=512)

    dh, g_nmlp1, g_w1_1, g_w2_1 = _mlp_bwd("mlp1", h_b1, _row2(w["norm_mlp"][1]), w["mlp_w1"][1], w["mlp_w2"][1], mlp1, dh)
    send_off("mlp1", [g_w1_1, both(chip_major, g_w2_1)])
    dmix1 = _mm("l1_dout", dh, w["gla_w_out"], mode="nt")
    g_gla_out = _dw("l1_dwout", mix_in1, dh)
    g["gla_w_out"] = g_gla_out[0]
    dgo, dr, g["gla_norm"] = _rowcall(
        "l1_dpost", _gla_post_bwd_fn, post1_rows + [(dmix1, d_model, 0)], [gla_gain],
        [(d_model, F32), (d_model, BF16)], [(1, d_model)], pin=pins.pop() if pins else None)
    (dq_f, dk_f, dv_f, dlf_f), (dq_b, dk_b, dv_b, dlf_b) = _attn_bwd(
        "gla_dattn", (gq, 0), gk, gk, gv, (glf_f, 0), (glf_b, 0), gst_f, gst_b, dgo, 4, 128, 256)

    def gla_pre_bwd(q, lr, dq1, dq2, dlf1, dlf2, dk1, dk2, dv1, dv2, w_up, b_gate):
        dlr = jnp.zeros_like(lr)
        dws, dbs = [], []
        for d, dlf in enumerate((dlf1, dlf2)):
            z = _raw_nn(lr, w_up[d]) + b_gate[d:d + 1]
            dz = dlf * _sigmoid(-z) * (1.0 / 16.0)
            dlr = dlr + _raw_nt(dz, w_up[d])
            dws.append(_raw_tn(dz, lr))
            dbs.append(jnp.sum(dz, axis=0, keepdims=True))
        return ((dq1 + dq2) * (128.0 ** -0.5), dk1 + dk2, dv1 + dv2, dlr, dws[0], dws[1], dbs[0], dbs[1])

    rows = [(proj1, 512, 0), (proj1, LANES, 24), (dq_f, 512, 0), (dq_b, 512, 0), (dlf_f, 512, 0), (dlf_b, 512, 0),
            (dk_f, 512, 0), (dk_b, 512, 0), (dv_f, d_model, 0), (dv_b, d_model, 0)]
    dq, dk, dv, dlr, dwt_f, dwt_b, db_f, db_b = _rowcall(
        "gla_dpre", gla_pre_bwd, rows, gla_pars, [(512, BF16), (512, BF16), (d_model, BF16), (LANES, BF16)],
        [(512, LANES), (512, LANES), (1, 512), (1, 512)])
    g["gla_w_up_pad"] = jnp.stack([dwt_f.T, dwt_b.T])
    g["gla_b_gate"] = jnp.concatenate([db_f, db_b], axis=0)
    dproj1 = jnp.concatenate([dq, dk, dv, dr, dlr], axis=1)
    g_gla_in = both(lambda t: _split_chips(t[:, :GLA_IN_WIDTH], 1), _dw("l1_dwin", y1, dproj1, tn=640))
    g["gla_w_in"] = g_gla_in[0]
    send_off("gla", [g_gla_in, both(chip_major, g_gla_out)])
    dh, g_nmix1 = _dy_norm_bwd("l1_dy", dproj1, w["gla_w_in_pad"], h_a1, gain1, dh,
                               pin=pins.pop() if pins else None, tk=640)

    dh, g_nmlp0, g_w1_0, g_w2_0 = _mlp_bwd("mlp0", h_b0, _row2(w["norm_mlp"][0]), w["mlp_w1"][0], w["mlp_w2"][0], mlp0, dh)
    g_ab_out = _dw("l0_dwout", mix_in0, dh)
    g["ab_w_out"] = g_ab_out[0]
    send_off("mlp0", [g_w1_0, both(chip_major, g_w2_0), both(chip_major, g_ab_out)])
    dmix0 = _mm("l0_dout", dh, w["ab_w_out"], mode="nt")
    dhs, dga, do, dg, g["hg_norm"] = _rowcall(
        "l0_dpost", _post0_bwd_fn, post0_rows + [(dmix0, d_model, 0)], [hg_gain],
        [(rg_w, F32), (rg_w, BF16), (hg_w, F32), (hg_w, BF16)], [(1, hg_w)], pin=pins.pop() if pins else None)
    (dqh_f, dk_f, div_f, dlf_f), (dqh_b, dk_b, div_b, dlf_b) = _attn_bwd(
        "hg_dattn", (qh, 0), (k_f, 0), (k_b, 0), iv, (lf_f, 0), (lf_b, 0), st_f, st_b, do, 4, 128, 128)

    def hg_pre_bwd(q, f_f, f_b, dq1, dq2, dk1, dlf1, dk2, dlf2, dv1, dv2, logits):
        _, vjp = jax.vjp(_hg_pre_fn, q, f_f, f_b, logits)
        dq, df_f, df_b, dlogits = vjp((dq1 + dq2, dk1, dlf1, dk2, dlf2))
        return dq, df_f, df_b, dv1 + dv2, dlogits

    rows = hg_rows + [(t, hg_w, 0) for t in (dqh_f, dqh_b, dk_f, dlf_f, dk_b, dlf_b, div_f, div_b)]
    dq, df_f, df_b, div, g["hg_lb_logits"] = _rowcall(
        "hg_dpre", hg_pre_bwd, rows, [w["hg_lb_logits"]], [(hg_w, BF16)] * 4, [(2, hg_w)])
    du_f, da_f, du_b, da_b = _scan_bwd("rg_dscan", a_f, hs_f, a_b, hs_b, dhs)
    gates_bwd = _vjp_of(_rg_gates_fn, 1, 4, 5)
    rows = [(xc, rg_w, 0), (da_f, rg_w, 0), (du_f, rg_w, 0), (da_b, rg_w, 0), (du_b, rg_w, 0)]
    dxc, g["rg_wa_bd"], g["rg_wx_bd"], g["rg_b_a"], g["rg_b_x"], g["rg_lambda"] = _rowcall(
        "rg_dgates", gates_bwd, rows, gate_pars, [(rg_w, F32)],
        [(2, rg_w, rg_w), (2, rg_w, rg_w), (2, rg_w), (2, rg_w), (2, rg_w)])
    dxa, g["rg_conv_w"], g["rg_conv_b"] = _conv_bwd("rg_dconv", proj0, 0, conv_w, dxc)
    dproj0 = jnp.concatenate([dxa, dga, dq, df_f, df_b, div, dg], axis=1)
    g_ab_in = _dw("l0_dwin", y0, dproj0, out_split=N_CHIPS)
    g["ab_w_in"] = g_ab_in[0]
    send_off("ab", [g_ab_in])
    grad_x, g_nmix0 = _dy_norm_bwd("l0_dy", dproj0, w["ab_w_in"], h_a0, gain, dh, pin=pins.pop() if pins else None)

    g["norm_mix"] = jnp.concatenate([g_nmix0, g_nmix1], axis=0)
    g["norm_mlp"] = jnp.concatenate([g_nmlp0, g_nmlp1], axis=0)
    g["mlp_w1"] = [g_w1_0[0], g_w1_1[0]]
    g["mlp_w2"] = [g_w2_0[0], g_w2_1[0]]
    return loss, grad_x, g


def _block_diag(w):
    d, g, n, _ = w.shape
    eye = jnp.eye(g, dtype=w.dtype)
    return (w[:, :, :, None, :] * eye[None, :, None, :, None]).reshape(d, g * n, g * n)


def _block_diag_extract(wbd, g):
    d, gn, _ = wbd.shape
    n = gn // g
    blocks = wbd.reshape(d, g, n, g, n)
    return jnp.stack([blocks[:, i, :, i, :] for i in range(g)], axis=1)


def _prepare_weights(big, full):
    w = {k: full[k] for k in ("norm_mix", "norm_mlp", "norm_final", "hg_lb_logits")}
    for k in ("rg_conv_w", "rg_conv_b", "rg_b_a", "rg_b_x", "rg_lambda", "hg_norm", "gla_b_gate", "gla_norm"):
        w[k] = full[k][0]
    w["rg_wa_bd"] = _block_diag(full["rg_w_a"][0])
    w["rg_wx_bd"] = _block_diag(full["rg_w_x"][0])
    up = full["gla_w_gate_up"][0]
    rank = up.shape[1]
    pad = jnp.zeros((2, LANES, up.shape[2]), F32)
    w["gla_w_up_pad"] = pad.at[0, 0:rank].set(up[0]).at[1, rank:2 * rank].set(up[1])
    w.update(_prepare_matrices(big))
    return w


def _prepare_matrices(big):
    w = {}
    if "mlp_w1" in big:
        w["mlp_w1"] = list(big["mlp_w1"])
        w["mlp_w2"] = [t.reshape(-1, t.shape[-1]) for t in big["mlp_w2"]]
    if "ab_w_in" in big:
        w["ab_w_in"] = big["ab_w_in"]
    if "ab_w_out" in big:
        w["ab_w_out"] = big["ab_w_out"].reshape(-1, big["ab_w_out"].shape[-1])
    if "gla_w_in" in big:
        w["gla_w_out"] = big["gla_w_out"].reshape(-1, big["gla_w_out"].shape[-1])
        gla_in = _join_chips(big["gla_w_in"], 1)
        w["gla_w_in_pad"] = jnp.pad(gla_in, ((0, 0), (0, GLA_IN_PAD - gla_in.shape[1])))
    return w


def _finish_grads(g, rank=16, rg_blocks=8):
    def chip_major(t):
        return t.reshape(N_CHIPS, t.shape[0] // N_CHIPS, t.shape[1])

    big = {
        "mlp_w1": list(g["mlp_w1"]), "mlp_w2": [chip_major(t) for t in g["mlp_w2"]],
        "ab_w_in": g["ab_w_in"], "ab_w_out": chip_major(g["ab_w_out"]),
        "gla_w_in": g["gla_w_in"], "gla_w_out": chip_major(g["gla_w_out"]),
    }
    small = {
        "norm_mix": g["norm_mix"], "norm_mlp": g["norm_mlp"], "norm_final": g["norm_final"][0],
        "rg_conv_w": g["rg_conv_w"][None], "rg_conv_b": g["rg_conv_b"],
        "rg_w_a": _block_diag_extract(g["rg_wa_bd"], rg_blocks)[None], "rg_b_a": g["rg_b_a"][None],
        "rg_w_x": _block_diag_extract(g["rg_wx_bd"], rg_blocks)[None], "rg_b_x": g["rg_b_x"][None],
        "rg_lambda": g["rg_lambda"][None], "hg_lb_logits": g["hg_lb_logits"], "hg_norm": g["hg_norm"],
        "gla_w_gate_up": jnp.stack([g["gla_w_up_pad"][0, 0:rank], g["gla_w_up_pad"][1, rank:2 * rank]])[None],
        "gla_b_gate": g["gla_b_gate"][None], "gla_norm": g["gla_norm"],
    }
    return big, small


MATRICES = (("mlp_w1", 0), ("mlp_w1", 1), ("mlp_w2", 0), ("mlp_w2", 1), ("ab_w_in", 0), ("ab_w_out", 0),
            ("gla_w_in", 0), ("gla_w_out", 0))
EARLY_MATRICES = ("ab_w_in",)
SMALL_SHARDED = ("rg_conv_w", "rg_b_a", "rg_b_x", "rg_lambda", "gla_w_gate_up", "gla_b_gate", "gla_norm")
SMALL_REPLICATED = ("norm_mix", "norm_mlp", "norm_final", "rg_conv_b", "rg_w_a", "rg_w_x", "hg_lb_logits", "hg_norm")
WEIGHTS = ("norm_mix", "norm_mlp", "norm_final", "mlp_w1", "mlp_w2", "ab_w_in", "ab_w_out", "rg_conv_w", "rg_conv_b",
           "rg_w_a", "rg_b_a", "rg_w_x", "rg_b_x", "rg_lambda", "hg_lb_logits", "hg_norm", "gla_w_in", "gla_w_out",
           "gla_w_gate_up", "gla_b_gate", "gla_norm")
ROW_ALIGN = 16


def _pack(arrays, lead=0):
    head = arrays[0].shape[:lead]
    flat = jnp.concatenate([a.reshape(head + (-1,)) for a in arrays], axis=lead)
    n = flat.shape[-1]
    quantum = LANES * ROW_ALIGN
    padded = -(-n // quantum) * quantum
    if padded != n:
        flat = jnp.pad(flat, [(0, 0)] * lead + [(0, padded - n)])
    return flat.reshape(head + (padded // LANES, LANES))


def _unpack(buf, shapes, lead=0):
    head = buf.shape[:lead]
    flat = buf.reshape(head + (-1,))
    out, off = [], 0
    for s in shapes:
        n = 1
        for v in s:
            n *= v
        out.append(lax.slice_in_dim(flat, off, off + n, axis=lead).reshape(head + tuple(s)))
        off += n
    return out


def _join_chips(gathered, axis):
    t = jnp.moveaxis(gathered, 0, axis)
    return t.reshape(t.shape[:axis] + (t.shape[axis] * t.shape[axis + 1],) + t.shape[axis + 2:])


def _split_chips(full, axis):
    s = full.shape
    t = full.reshape(s[:axis] + (N_CHIPS, s[axis] // N_CHIPS) + s[axis + 1:])
    return jnp.moveaxis(t, axis, 0)


_ANY = pl.BlockSpec(memory_space=pl.ANY)


def _place():
    return lax.axis_index("x"), lax.axis_index("y"), lax.axis_index("c")


def _into_slot(name, src, slot, n_slots, dtype, tm, layer=None):
    r, lanes = src.shape[-2:]
    tm = _row_tile(r, tm, ROW_ALIGN)

    def body(slot_ref, in_ref, o_ref):
        o_ref[...] = in_ref[...].astype(o_ref.dtype)

    if layer is None:
        in_spec = pl.BlockSpec((tm, lanes), lambda i, slot_ref: (i, 0))
    else:
        in_spec = pl.BlockSpec((None, tm, lanes), lambda i, slot_ref: (layer, i, 0))
    grid_spec = pltpu.PrefetchScalarGridSpec(
        num_scalar_prefetch=1, grid=(r // tm,), in_specs=[in_spec],
        out_specs=pl.BlockSpec((None, tm, lanes), lambda i, slot_ref: (slot_ref[0], i, 0)))
    return pl.pallas_call(
        body, name=name, grid_spec=grid_spec, out_shape=jax.ShapeDtypeStruct((n_slots, r, lanes), dtype),
        compiler_params=_params(("parallel",)),
    )(slot.reshape(1).astype(jnp.int32), src)


def _chip_peers():
    x, y, c = _place()
    return 2 * x + y, c, [(1 - x, y), (x, 1 - y), (1 - x, 1 - y)]


def _comm_call(name, body, ins, out_shapes, n_sems, aliases=None):
    return pl.pallas_call(
        body, name=name, in_specs=[_ANY] * len(ins), out_specs=[_ANY] * len(out_shapes), out_shape=out_shapes,
        input_output_aliases=aliases or {},
        scratch_shapes=[pltpu.SemaphoreType.DMA((n_sems,)), pltpu.SemaphoreType.DMA((n_sems,))],
    )(*ins)


def _gather_chips(name, bufs):
    n = len(bufs)

    def body(*refs):
        outs, send_sems, recv_sems = refs[n:2 * n], refs[2 * n], refs[2 * n + 1]
        x, y, c = _place()
        me, _, peers = _chip_peers()

        def rows(a, block, half):
            rh = outs[a].shape[1] // 2
            return outs[a].at[block, pl.ds(half * rh, rh)]

        def copy(a, j, block, half, to, sem):
            return pltpu.make_async_remote_copy(
                src_ref=rows(a, block, half), dst_ref=rows(a, block, half), send_sem=send_sems.at[sem],
                recv_sem=recv_sems.at[sem], device_id=to, device_id_type=MESH)

        def over_ici(a, j, block):
            px, py = peers[j]
            return copy(a, j, block, c, (px, py, c), 6 * a + j)

        def to_sibling(a, j, block, half):
            return copy(a, j, block, half, (x, y, 1 - c), 6 * a + 3 + j)

        sends = [over_ici(a, j, me) for a in range(n) for j in range(3)]
        for cp in sends:
            cp.start()
        for a in range(n):
            for j, (px, py) in enumerate(peers):
                over_ici(a, j, 2 * px + py).wait_recv()
                handed = to_sibling(a, j, 2 * px + py, c)
                handed.start()
                sends.append(handed)
        for a in range(n):
            for j, (px, py) in enumerate(peers):
                to_sibling(a, j, 2 * px + py, 1 - c).wait_recv()
        for cp in sends:
            cp.wait_send()

    shapes = [jax.ShapeDtypeStruct(b.shape, b.dtype) for b in bufs]
    return _comm_call(name, body, bufs, shapes, 6 * n, {a: a for a in range(n)})


_HBM = pl.BlockSpec(memory_space=pltpu.HBM)
_SEM = pl.BlockSpec(memory_space=pltpu.SEMAPHORE)
_EFFECT = pltpu.SideEffectType.DATAFLOW_SIDE_EFFECTING


def _half_rows(ref, block, half):
    rh = ref.shape[1] // 2
    return ref.at[block, pl.ds(half * rh, rh)]


def _gather_start(name, bufs, after):
    n = len(bufs)

    def body(*refs):
        ins, send_sems, recv_sems, token = refs[:n], refs[n + 1], refs[n + 2], refs[-1]
        me, c, peers = _chip_peers()
        for a in range(n):
            mine = _half_rows(ins[a], me, c)
            for j, (px, py) in enumerate(peers):
                pltpu.make_async_remote_copy(
                    src_ref=mine, dst_ref=mine, send_sem=send_sems.at[3 * a + j], recv_sem=recv_sems.at[3 * a + j],
                    device_id=(px, py, c), device_id_type=MESH).start()
        token[...] = jnp.zeros_like(token)

    out_shape = (pltpu.SemaphoreType.DMA((3 * n,)), pltpu.SemaphoreType.DMA((3 * n,)),
                 *[pltpu.HBM(b.shape, b.dtype) for b in bufs], jax.ShapeDtypeStruct((8, LANES), F32))
    return pl.pallas_call(
        body, name=name, out_shape=out_shape, in_specs=[_HBM] * n + [_ANY],
        out_specs=(_SEM, _SEM, *[_HBM] * n, pl.BlockSpec(memory_space=pltpu.VMEM)),
        input_output_aliases={a: 2 + a for a in range(n)},
        compiler_params=pltpu.CompilerParams(has_side_effects=_EFFECT),
    )(*[pltpu.with_memory_space_constraint(b, pltpu.HBM) for b in bufs], after)


def _gather_wait(name, bufs, send_sems, recv_sems, after):
    n = len(bufs)

    def body(*refs):
        ins, send_sems, recv_sems = refs[:n], refs[n], refs[n + 1]
        me, c, peers = _chip_peers()
        for a in range(n):
            for j, (px, py) in enumerate(peers):
                copy = pltpu.make_async_remote_copy(
                    src_ref=_half_rows(ins[a], me, c), dst_ref=_half_rows(ins[a], 2 * px + py, c),
                    send_sem=send_sems.at[3 * a + j], recv_sem=recv_sems.at[3 * a + j],
                    device_id=(px, py, c), device_id_type=MESH)
                copy.wait_send()
                copy.wait_recv()

    return pl.pallas_call(
        body, name=name, out_shape=tuple(pltpu.HBM(b.shape, b.dtype) for b in bufs),
        in_specs=[_HBM] * n + [_SEM, _SEM, _ANY], out_specs=tuple([_HBM] * n),
        input_output_aliases={a: a for a in range(n)},
        compiler_params=pltpu.CompilerParams(has_side_effects=_EFFECT),
    )(*bufs, send_sems, recv_sems, after)


def _hand_over(name, bufs):
    n = len(bufs)

    def body(*refs):
        outs, send_sems, recv_sems = refs[n:2 * n], refs[2 * n], refs[2 * n + 1]
        x, y, c = _place()
        _, _, peers = _chip_peers()

        def copy(a, j, half):
            px, py = peers[j]
            rows = _half_rows(outs[a], 2 * px + py, half)
            return pltpu.make_async_remote_copy(
                src_ref=rows, dst_ref=rows, send_sem=send_sems.at[3 * a + j], recv_sem=recv_sems.at[3 * a + j],
                device_id=(x, y, 1 - c), device_id_type=MESH)

        sends = [copy(a, j, c) for a in range(n) for j in range(3)]
        for cp in sends:
            cp.start()
        for a in range(n):
            for j in range(3):
                copy(a, j, 1 - c).wait_recv()
        for cp in sends:
            cp.wait_send()

    shapes = [jax.ShapeDtypeStruct(b.shape, b.dtype) for b in bufs]
    return _comm_call(name, body, bufs, shapes, 3 * n, {a: a for a in range(n)})


def _pair_gather(name, bufs):
    n = len(bufs)

    def body(*refs):
        ins, outs, send_sems, recv_sems = refs[:n], refs[n:2 * n], refs[2 * n], refs[2 * n + 1]
        x, y, c = _place()

        def copy(a, block):
            return pltpu.make_async_remote_copy(
                src_ref=ins[a].at[block], dst_ref=outs[a].at[block], send_sem=send_sems.at[a],
                recv_sem=recv_sems.at[a], device_id=(x, y, 1 - c), device_id_type=MESH)

        sends = [copy(a, c) for a in range(n)]
        for cp in sends:
            cp.start()
        for a in range(n):
            copy(a, 1 - c).wait_recv()
        for cp in sends:
            cp.wait_send()

    shapes = [jax.ShapeDtypeStruct(b.shape, b.dtype) for b in bufs]
    return _comm_call(name, body, bufs, shapes, n, {a: a for a in range(n)})


def _all_peers():
    x, y, c = _place()
    peers = []
    for mask in range(1, N_DEV):
        fx, fy, fc = (mask >> 2) & 1, (mask >> 1) & 1, mask & 1
        peers.append((jnp.where(fx, 1 - x, x), jnp.where(fy, 1 - y, y), jnp.where(fc, 1 - c, c)))
    return 4 * x + 2 * y + c, peers


def _reduce_copies(srcs, lands, send_sems, recv_sems):
    me, peers = _all_peers()
    sends, arrivals = [], []
    for a in range(len(srcs)):
        for j, (px, py, pc) in enumerate(peers):
            k = (N_DEV - 1) * a + j
            sends.append(pltpu.make_async_remote_copy(
                src_ref=srcs[a].at[2 * px + py, pc], dst_ref=lands[a].at[me], send_sem=send_sems.at[k],
                recv_sem=recv_sems.at[k], device_id=(px, py, pc), device_id_type=MESH))
            arrivals.append(pltpu.make_async_remote_copy(
                src_ref=srcs[a].at[2 * px + py, pc], dst_ref=lands[a].at[4 * px + 2 * py + pc],
                send_sem=send_sems.at[k], recv_sem=recv_sems.at[k], device_id=(px, py, pc), device_id_type=MESH))
    return sends, arrivals


def _reduce_direct(name, srcs, pin=None):
    n = len(srcs)
    extra = [] if pin is None else [pin]

    def body(*refs):
        ins, outs = refs[:n], refs[n + len(extra):2 * n + len(extra)]
        sends, arrivals = _reduce_copies(ins, outs, refs[-2], refs[-1])
        for cp in sends:
            cp.start()
        for cp in arrivals:
            cp.wait_recv()
        for cp in sends:
            cp.wait_send()

    shapes = [jax.ShapeDtypeStruct((N_DEV,) + s.shape[2:], s.dtype) for s in srcs]
    return _comm_call(name, body, list(srcs) + extra, shapes, (N_DEV - 1) * n)


def _reduce_start(name, srcs):
    n = len(srcs)
    lands = [lax.empty((N_DEV,) + s.shape[2:], s.dtype) for s in srcs]

    def body(*refs):
        sends, _ = _reduce_copies(refs[:n], refs[n:2 * n], refs[2 * n], refs[2 * n + 1])
        for cp in sends:
            cp.start()
        refs[-1][...] = jnp.zeros_like(refs[-1])

    bufs = list(srcs) + lands
    n_sems = (N_DEV - 1) * n
    out_shape = (pltpu.SemaphoreType.DMA((n_sems,)), pltpu.SemaphoreType.DMA((n_sems,)),
                 *[pltpu.HBM(b.shape, b.dtype) for b in bufs], jax.ShapeDtypeStruct((8, LANES), F32))
    return pl.pallas_call(
        body, name=name, out_shape=out_shape, in_specs=[_HBM] * (2 * n),
        out_specs=(_SEM, _SEM, *[_HBM] * (2 * n), pl.BlockSpec(memory_space=pltpu.VMEM)),
        input_output_aliases={a: 2 + a for a in range(2 * n)},
        compiler_params=pltpu.CompilerParams(has_side_effects=_EFFECT),
    )(*[pltpu.with_memory_space_constraint(b, pltpu.HBM) for b in bufs])


def _reduce_wait(name, srcs, lands, send_sems, recv_sems, after):
    n = len(srcs)

    def body(*refs):
        sends, arrivals = _reduce_copies(refs[:n], refs[n:2 * n], refs[2 * n], refs[2 * n + 1])
        for cp in sends:
            cp.wait_send()
        for cp in arrivals:
            cp.wait_recv()

    bufs = list(srcs) + list(lands)
    outs = pl.pallas_call(
        body, name=name, out_shape=tuple(pltpu.HBM(b.shape, b.dtype) for b in bufs),
        in_specs=[_HBM] * (2 * n) + [_SEM, _SEM, _ANY], out_specs=tuple([_HBM] * (2 * n)),
        input_output_aliases={a: a for a in range(2 * n)},
        compiler_params=pltpu.CompilerParams(has_side_effects=_EFFECT),
    )(*bufs, send_sems, recv_sems, after)
    return list(outs[n:])


def _reduce_sum(name, own, land, chip, core):
    n, rh, lanes = land.shape
    tm = _row_tile(rh, 1024, ROW_ALIGN)

    def body(idx_ref, own_ref, *rest):
        total = own_ref[...]
        for g_ref in rest[:-1]:
            total = total + g_ref[...].astype(F32)
        rest[-1][...] = total

    def block(k):
        return pl.BlockSpec((None, tm, lanes), lambda i, idx_ref: ((2 * idx_ref[0] + idx_ref[1] + k) % n, i, 0))

    grid_spec = pltpu.PrefetchScalarGridSpec(
        num_scalar_prefetch=1, grid=(rh // tm,),
        in_specs=[pl.BlockSpec((None, None, tm, lanes), lambda i, idx_ref: (idx_ref[0], idx_ref[1], i, 0))]
        + [block(k) for k in range(1, n)],
        out_specs=pl.BlockSpec((None, tm, lanes), lambda i, idx_ref: (idx_ref[1], i, 0)))
    return pl.pallas_call(
        body, name=name, grid_spec=grid_spec, out_shape=jax.ShapeDtypeStruct((2, rh, lanes), F32),
        compiler_params=_params(("parallel",)),
    )(jnp.stack([chip, core]).astype(jnp.int32), own, *[land] * (n - 1))


def _gather_all_start(name, buf):
    def body(in_ref, send_sems, recv_sems, out_ref, token):
        me, peers = _all_peers()
        for j, peer in enumerate(peers):
            pltpu.make_async_remote_copy(
                src_ref=in_ref.at[me], dst_ref=in_ref.at[me], send_sem=send_sems.at[j], recv_sem=recv_sems.at[j],
                device_id=peer, device_id_type=MESH).start()
        token[...] = jnp.zeros_like(token)

    n = N_DEV - 1
    return pl.pallas_call(
        body, name=name, in_specs=[_HBM],
        out_shape=(pltpu.SemaphoreType.DMA((n,)), pltpu.SemaphoreType.DMA((n,)), pltpu.HBM(buf.shape, buf.dtype),
                   jax.ShapeDtypeStruct((8, LANES), F32)),
        out_specs=(_SEM, _SEM, _HBM, pl.BlockSpec(memory_space=pltpu.VMEM)), input_output_aliases={0: 2},
        compiler_params=pltpu.CompilerParams(has_side_effects=_EFFECT),
    )(pltpu.with_memory_space_constraint(buf, pltpu.HBM))


def _gather_all_wait(name, buf, send_sems, recv_sems, after):
    def body(in_ref, send_sems, recv_sems, after_ref, out_ref):
        me, peers = _all_peers()
        for j, (px, py, pc) in enumerate(peers):
            copy = pltpu.make_async_remote_copy(
                src_ref=in_ref.at[me], dst_ref=in_ref.at[4 * px + 2 * py + pc], send_sem=send_sems.at[j],
                recv_sem=recv_sems.at[j], device_id=(px, py, pc), device_id_type=MESH)
            copy.wait_send()
            copy.wait_recv()

    return pl.pallas_call(
        body, name=name, in_specs=[_HBM, _SEM, _SEM, _ANY], out_shape=pltpu.HBM(buf.shape, buf.dtype),
        out_specs=_HBM, input_output_aliases={0: 0},
        compiler_params=pltpu.CompilerParams(has_side_effects=_EFFECT),
    )(buf, send_sems, recv_sems, after)


def _sum_blocks(name, stacked, tm):
    n, r, lanes = stacked.shape

    def body(in_ref, o_ref):
        acc = in_ref[0]
        for j in range(1, n):
            acc = acc + in_ref[j]
        o_ref[...] = acc

    return pl.pallas_call(
        body, name=name, grid=(r // tm,), in_specs=[pl.BlockSpec((n, tm, lanes), lambda i: (0, i, 0))],
        out_specs=pl.BlockSpec((tm, lanes), lambda i: (i, 0)), out_shape=jax.ShapeDtypeStruct((r, lanes), F32),
        compiler_params=_params(("parallel",)),
    )(stacked)


def _row_tile(rows, pref, align):
    best = None
    for t in range(align, min(rows, pref) + 1, align):
        if rows % t == 0:
            best = t
    assert best is not None, (rows, pref, align)
    return best


def _adam(name, w, g, m, v):
    rows, width = w.shape
    tm = _row_tile(rows, max(8, 4096 * LANES // width), 8)
    args = [(t, width, 0) for t in (w, g, m, v)]
    return _rowcall(name, _adam_fn, args, [], [(width, F32)] * 3, tm=tm)


def kernel(x, norm_mix, norm_mlp, norm_final, mlp_w1, mlp_w2, ab_w_in, ab_w_out, rg_conv_w, rg_conv_b, rg_w_a, rg_b_a, rg_w_x, rg_b_x, rg_lambda, hg_lb_logits, hg_norm, gla_w_in, gla_w_out, gla_w_gate_up, gla_b_gate, gla_norm, loss_target, m_norm_mix, m_norm_mlp, m_norm_final, m_mlp_w1, m_mlp_w2, m_ab_w_in, m_ab_w_out, m_rg_conv_w, m_rg_conv_b, m_rg_w_a, m_rg_b_a, m_rg_w_x, m_rg_b_x, m_rg_lambda, m_hg_lb_logits, m_hg_norm, m_gla_w_in, m_gla_w_out, m_gla_w_gate_up, m_gla_b_gate, m_gla_norm, v_norm_mix, v_norm_mlp, v_norm_final, v_mlp_w1, v_mlp_w2, v_ab_w_in, v_ab_w_out, v_rg_conv_w, v_rg_conv_b, v_rg_w_a, v_rg_b_a, v_rg_w_x, v_rg_b_x, v_rg_lambda, v_hg_lb_logits, v_hg_norm, v_gla_w_in, v_gla_w_out, v_gla_w_gate_up, v_gla_b_gate, v_gla_norm):
    w = dict(norm_mix=norm_mix, norm_mlp=norm_mlp, norm_final=norm_final, mlp_w1=mlp_w1, mlp_w2=mlp_w2, ab_w_in=ab_w_in, ab_w_out=ab_w_out, rg_conv_w=rg_conv_w, rg_conv_b=rg_conv_b, rg_w_a=rg_w_a, rg_b_a=rg_b_a, rg_w_x=rg_w_x, rg_b_x=rg_b_x, rg_lambda=rg_lambda, hg_lb_logits=hg_lb_logits, hg_norm=hg_norm, gla_w_in=gla_w_in, gla_w_out=gla_w_out, gla_w_gate_up=gla_w_gate_up, gla_b_gate=gla_b_gate, gla_norm=gla_norm)
    m = dict(norm_mix=m_norm_mix, norm_mlp=m_norm_mlp, norm_final=m_norm_final, mlp_w1=m_mlp_w1, mlp_w2=m_mlp_w2, ab_w_in=m_ab_w_in, ab_w_out=m_ab_w_out, rg_conv_w=m_rg_conv_w, rg_conv_b=m_rg_conv_b, rg_w_a=m_rg_w_a, rg_b_a=m_rg_b_a, rg_w_x=m_rg_w_x, rg_b_x=m_rg_b_x, rg_lambda=m_rg_lambda, hg_lb_logits=m_hg_lb_logits, hg_norm=m_hg_norm, gla_w_in=m_gla_w_in, gla_w_out=m_gla_w_out, gla_w_gate_up=m_gla_w_gate_up, gla_b_gate=m_gla_b_gate, gla_norm=m_gla_norm)
    v = dict(norm_mix=v_norm_mix, norm_mlp=v_norm_mlp, norm_final=v_norm_final, mlp_w1=v_mlp_w1, mlp_w2=v_mlp_w2, ab_w_in=v_ab_w_in, ab_w_out=v_ab_w_out, rg_conv_w=v_rg_conv_w, rg_conv_b=v_rg_conv_b, rg_w_a=v_rg_w_a, rg_b_a=v_rg_b_a, rg_w_x=v_rg_w_x, rg_b_x=v_rg_b_x, rg_lambda=v_rg_lambda, hg_lb_logits=v_hg_lb_logits, hg_norm=v_hg_norm, gla_w_in=v_gla_w_in, gla_w_out=v_gla_w_out, gla_w_gate_up=v_gla_w_gate_up, gla_b_gate=v_gla_b_gate, gla_norm=v_gla_norm)
    chip = 2 * lax.axis_index("x") + lax.axis_index("y")
    core = lax.axis_index("c")
    sharded_shapes = [w[n].shape for n in SMALL_SHARDED]

    slots = [_into_slot(f"cast_{n}{layer}", w[n], chip, N_CHIPS, BF16, 512, layer) for n, layer in MATRICES]
    early = [i for i, (n, _) in enumerate(MATRICES) if n in EARLY_MATRICES]
    rest = [i for i in range(len(MATRICES)) if i not in early]

    def named(indices, arrays):
        big = {}
        for i, t in zip(indices, arrays):
            big.setdefault(MATRICES[i][0], []).append(t)
        return {n: (v if n in ("mlp_w1", "mlp_w2") else v[0]) for n, v in big.items()}

    gathered = _gather_chips("gather_early", [slots[i] for i in early])
    send_sems, recv_sems, *in_flight, token = _gather_start("gather_rest_start", [slots[i] for i in rest], gathered[0])

    def late_weights(after):
        landed = _gather_wait("gather_rest_wait", in_flight, send_sems, recv_sems, after)
        return _prepare_matrices(named(rest, _hand_over("gather_rest_share", list(landed))))

    big = named(early, gathered)
    vectors = _pack([w[n] for n in SMALL_SHARDED])
    vectors = _into_slot("place_vectors", vectors, chip, N_CHIPS, F32, vectors.shape[0])
    small_all = _unpack(_gather_chips("gather_vectors", [vectors])[0], sharded_shapes, lead=1)
    full = {n: w[n] for n in SMALL_REPLICATED}
    for n, t in zip(SMALL_SHARDED, small_all):
        full[n] = _join_chips(t, t.ndim - 2)

    def halves(t):
        return t.reshape(N_CHIPS, 2, t.shape[1] // 2, t.shape[2])

    in_flight_grads = {}

    def emit(tag, arrays32, arrays16):
        n = len(arrays16)
        send, recv, *rest = _reduce_start(f"reduce_{tag}_start", [halves(t) for t in arrays16])
        in_flight_grads[tag] = ([halves(t) for t in arrays32], rest[:n], rest[n:2 * n], send, recv)
        return rest[-1]

    loss_part, grad_x, g_kernel = _local_step(
        x[0], loss_target[0], _prepare_weights(big, full), token, late_weights, emit)
    g_big, g_full = _finish_grads(g_kernel)

    small_names = SMALL_REPLICATED + SMALL_SHARDED
    reduced_shapes = [g_full[n].shape for n in small_names] + [loss_part.shape]
    g_small = _pack([g_full[n] for n in small_names] + [loss_part])
    device = 2 * chip + core
    g_small = _into_slot("place_small", g_small, device, N_DEV, F32, g_small.shape[0])
    small_send, small_recv, small_in_flight, small_token = _gather_all_start("reduce_small_start", g_small)

    mine = {}
    for tag, (own, srcs, lands, send, recv) in in_flight_grads.items():
        landed = _reduce_wait(f"reduce_{tag}_wait", srcs, lands, send, recv, small_token)
        mine[tag] = [_reduce_sum(f"reduce_add_{tag}{i}", o, f, chip, core) for i, (o, f) in enumerate(zip(own, landed))]
    ordered = [mine["mlp0"][0], mine["mlp1"][0], mine["mlp0"][1], mine["mlp1"][1], mine["ab"][0], mine["mlp0"][2],
               *mine["gla"]]
    reduced = [t.reshape(2 * t.shape[1], t.shape[2]) for t in _pair_gather("reduce_share", ordered)]
    by_name = {n: [] for n, _ in MATRICES}
    for (n, _), t in zip(MATRICES, reduced):
        by_name[n].append(t)
    grads = {n: jnp.stack(v) for n, v in by_name.items()}

    g_small_all = _gather_all_wait("reduce_small_wait", small_in_flight, small_send, small_recv, reduced[0])
    g_small_red = _sum_blocks("reduce_small_add", g_small_all, g_small_all.shape[1])
    *small_red, loss_sum = _unpack(g_small_red, reduced_shapes)
    loss = loss_sum[0, 0]
    g_small_full = dict(zip(small_names, small_red))
    for n in SMALL_REPLICATED:
        grads[n] = g_small_full[n]
    for n in SMALL_SHARDED:
        width = w[n].shape[-1]
        grads[n] = lax.dynamic_slice_in_dim(g_small_full[n], chip * width, width, axis=g_small_full[n].ndim - 1)

    delta, new_m, new_v = {}, {}, {}
    for n in by_name:
        flat = [t.reshape(-1, t.shape[-1]) for t in (w[n], grads[n], m[n], v[n])]
        for dst, t in zip((delta, new_m, new_v), _adam(f"adam_{n}", *flat)):
            dst[n] = t.reshape(w[n].shape)
    small_shapes = [w[n].shape for n in small_names]
    packs = [_pack([src[n] for n in small_names]) for src in (w, grads, m, v)]
    d_small, m_small, v_small = _adam("adam_small", *packs)
    for dst, buf in ((delta, d_small), (new_m, m_small), (new_v, v_small)):
        dst.update(zip(small_names, _unpack(buf, small_shapes)))

    return (loss, grad_x[None], *[grads[n] for n in WEIGHTS], *[delta[n] for n in WEIGHTS],
            *[new_m[n] for n in WEIGHTS], *[new_v[n] for n in WEIGHTS])
```

```python
import functools

import jax
import jax.numpy as jnp
from jax import lax
from jax.experimental import pallas as pl
from jax.experimental.pallas import tpu as pltpu

F32 = jnp.float32
BF16 = jnp.bfloat16
MESH = pl.DeviceIdType.MESH

LANES = 128
CHUNK = 64
ATTN_SUB = 4
EPS = 1e-6
RG_C = 8.0
N_CHIPS = 4
N_DEV = 8
GLA_IN_WIDTH = 3104
GLA_IN_PAD = 3200
VMEM_LIMIT = 56 * 1024 * 1024

ADAM_LR = 0.001
ADAM_B1 = 0.9
ADAM_B2 = 0.999
ADAM_EPS = 1e-08
ADAM_WD = 0.01
ADAM_STEP = 10


def _raw_dot(a, b, ca, cb):
    return lax.dot_general(a.astype(BF16), b.astype(BF16), (((ca,), (cb,)), ((), ())),
                           preferred_element_type=F32)


def _raw_nn(a, b):
    return _raw_dot(a, b, 1, 0)


def _raw_nt(a, b):
    return _raw_dot(a, b, 1, 1)


def _raw_tn(a, b):
    return _raw_dot(a, b, 0, 0)


@jax.custom_vjp
def _dot_nn(a, b):
    return _raw_nn(a, b)


def _dot_nn_fwd(a, b):
    return _raw_nn(a, b), (a, b)


def _dot_nn_bwd(res, g):
    a, b = res
    return _raw_nt(g, b), _raw_tn(a, g)


_dot_nn.defvjp(_dot_nn_fwd, _dot_nn_bwd)


@jax.custom_vjp
def _dot_nt(a, b):
    return _raw_nt(a, b)


def _dot_nt_fwd(a, b):
    return _raw_nt(a, b), (a, b)


def _dot_nt_bwd(res, g):
    a, b = res
    return _raw_nn(g, b), _raw_tn(g, a)


_dot_nt.defvjp(_dot_nt_fwd, _dot_nt_bwd)


@jax.custom_vjp
def _dot_tn(a, b):
    return _raw_tn(a, b)


def _dot_tn_fwd(a, b):
    return _raw_tn(a, b), (a, b)


def _dot_tn_bwd(res, g):
    a, b = res
    return _raw_nt(b, g), _raw_nn(a, g)


_dot_tn.defvjp(_dot_tn_fwd, _dot_tn_bwd)


def _tile(n, pref):
    if n <= pref:
        return n
    t = (pref // LANES) * LANES
    while t > LANES and n % t:
        t -= LANES
    assert n % t == 0, (n, pref)
    return t


def _params(sem):
    return pltpu.CompilerParams(dimension_semantics=sem, vmem_limit_bytes=VMEM_LIMIT)


def _rowcall(name, fn, rows, pars, row_outs, par_outs=(), tm=256, pin=None):
    if pin is not None:
        inner, pars = fn, list(pars) + [pin]
        fn = lambda *vals: inner(*vals[:-1])
    n_rows = rows[0][0].shape[0]
    tm = min(tm, n_rows)
    assert n_rows % tm == 0
    n_r, n_p, n_ro = len(rows), len(pars), len(row_outs)

    def body(*refs):
        vals = [r[...].astype(F32) for r in refs[:n_r + n_p]]
        outs = fn(*vals)
        o_refs = refs[n_r + n_p:n_r + n_p + n_ro]
        po_refs = refs[n_r + n_p + n_ro:]
        for o_ref, val in zip(o_refs, outs[:n_ro]):
            o_ref[...] = val.astype(o_ref.dtype)
        first = pl.program_id(0) == 0
        for po_ref, val in zip(po_refs, outs[n_ro:]):
            @pl.when(first)
            def _():
                po_ref[...] = val

            @pl.when(jnp.logical_not(first))
            def _():
                po_ref[...] += val

    def const_map(nd):
        return lambda i: (0,) * nd

    def row_spec(w, cb):
        return pl.BlockSpec((tm, w), lambda i: (i, cb))

    in_specs = [row_spec(w, cb) for _, w, cb in rows]
    in_specs += [pl.BlockSpec(p.shape, const_map(p.ndim)) for p in pars]
    out_specs = [pl.BlockSpec((tm, w), lambda i: (i, 0)) for w, _ in row_outs]
    out_specs += [pl.BlockSpec(tuple(s), const_map(len(s))) for s in par_outs]
    out_shape = [jax.ShapeDtypeStruct((n_rows, w), dt) for w, dt in row_outs]
    out_shape += [jax.ShapeDtypeStruct(tuple(s), F32) for s in par_outs]
    return pl.pallas_call(
        body, name=name, grid=(n_rows // tm,), in_specs=in_specs, out_specs=out_specs, out_shape=out_shape,
        compiler_params=_params(("arbitrary",) if par_outs else ("parallel",)),
    )(*[r[0] for r in rows], *pars)


def _vjp_of(fn, n_prim, n_out, n_par, n_pass=0):
    def bwd(*args):
        prim = args[:n_prim]
        cts = args[n_prim:n_prim + n_out]
        passes = args[n_prim + n_out:n_prim + n_out + 2 * n_pass]
        pars = args[n_prim + n_out + 2 * n_pass:]
        _, vjp = jax.vjp(fn, *prim, *pars)
        grads = vjp(tuple(cts))
        sums = tuple(passes[2 * i] + passes[2 * i + 1] for i in range(n_pass))
        return tuple(grads[:n_prim]) + sums + tuple(grads[n_prim:])
    return bwd


def _mm(name, a, b, mode="nn", extras=(), epi=None, out_dtypes=(F32,), a_pro=None, out_split=None,
        epi_pars=(), row_sum=False, pin=None, tm=1024, tn=1024, tk=1024):
    split = b.shape[0] if b.ndim == 3 else None
    b_rows, b_cols = b.shape[-2:]
    if mode == "nn":
        (m, k), n = a.shape, b_cols * (split or 1)
    elif mode == "nt":
        (m, k), n = a.shape, b_rows
        assert k == b_cols * (split or 1)
    else:
        assert split is None
        (k, m), n = a.shape, b_cols
    tm, tk = _tile(m, tm), _tile(k, tk)
    tn = _tile(n // out_split, tn) if out_split else _tile(n, tn)
    if split and mode == "nn":
        tn = _tile(b_cols, tn)
    if split and mode == "nt":
        tk = _tile(b_cols, tk)
    nk = k // tk
    raw = {"nn": _raw_nn, "nt": _raw_nt, "tn": _raw_tn}[mode]
    n_e, n_p, n_o = len(extras), len(epi_pars), len(out_dtypes)
    n_in = n_e + n_p + (0 if pin is None else 1)
    if epi is None:
        epi = lambda acc: (acc,)

    def body(a_ref, b_ref, *rest):
        e_refs, p_refs, o_refs = rest[:n_e], rest[n_e:n_e + n_p], rest[n_in:n_in + n_o]
        kk = pl.program_id(2)
        a_tile = a_ref[...] if a_pro is None else a_pro(a_ref[...].astype(F32))
        part = raw(a_tile, b_ref[...])

        def finish(total):
            res = epi(total, *[e[...].astype(F32) for e in e_refs], *[p[...] for p in p_refs])
            for o_ref, r in zip(o_refs, res):
                o_ref[...] = r.astype(o_ref.dtype)
            if row_sum:
                rest[n_in + n_o][...] = res[n_o]

        if nk == 1:
            finish(part)
            return
        acc = rest[-1]

        @pl.when(kk == 0)
        def _():
            acc[...] = part

        @pl.when((kk > 0) & (kk < nk - 1))
        def _():
            acc[...] += part

        @pl.when(kk == nk - 1)
        def _():
            finish(acc[...] + part)

    a_spec = pl.BlockSpec((tk, tm), lambda i, j, kk: (kk, i)) if mode == "tn" else pl.BlockSpec((tm, tk), lambda i, j, kk: (i, kk))
    if split and mode == "nn":
        per = b_cols // tn
        b_spec = pl.BlockSpec((None, tk, tn), lambda i, j, kk: (j // per, kk, j % per))
    elif split:
        per = b_cols // tk
        b_spec = pl.BlockSpec((None, tn, tk), lambda i, j, kk: (kk // per, j, kk % per))
    elif mode == "nt":
        b_spec = pl.BlockSpec((tn, tk), lambda i, j, kk: (j, kk))
    else:
        b_spec = pl.BlockSpec((tk, tn), lambda i, j, kk: (kk, j))
    mn_spec = pl.BlockSpec((tm, tn), lambda i, j, kk: (i, j))
    if out_split:
        assert not extras
        per_out = n // out_split // tn
        out_spec = pl.BlockSpec((None, tm, tn), lambda i, j, kk: (j // per_out, i, j % per_out))
        out_shapes = [jax.ShapeDtypeStruct((out_split, m, n // out_split), dt) for dt in out_dtypes]
    else:
        out_spec = mn_spec
        out_shapes = [jax.ShapeDtypeStruct((m, n), dt) for dt in out_dtypes]
    out_specs = [out_spec] * n_o
    if row_sum:
        out_specs.append(pl.BlockSpec((None, 1, tn), lambda i, j, kk: (i, 0, j)))
        out_shapes.append(jax.ShapeDtypeStruct((m // tm, 1, n), F32))
    in_specs = [a_spec, b_spec] + [mn_spec] * n_e
    in_specs += [pl.BlockSpec(p.shape, functools.partial(lambda i, j, kk, nd: (0,) * nd, nd=p.ndim)) for p in epi_pars]
    in_specs += [] if pin is None else [pl.BlockSpec(memory_space=pl.ANY)]
    outs = pl.pallas_call(
        body, name=name, grid=(m // tm, n // tn, nk), in_specs=in_specs, out_specs=out_specs, out_shape=out_shapes,
        scratch_shapes=[pltpu.VMEM((tm, tn), F32)] if nk > 1 else [],
        compiler_params=_params(("parallel", "parallel", "arbitrary")),
    )(a, b, *extras, *epi_pars, *([] if pin is None else [pin]))
    return outs[0] if len(outs) == 1 else outs


def _sigmoid(x):
    return jax.nn.sigmoid(x)


def _silu(x):
    return x * _sigmoid(x)


def _softplus(x):
    return jnp.maximum(x, 0.0) + jnp.log1p(jnp.exp(-jnp.abs(x)))


def _rmsnorm_fn(x, gain):
    return (x * lax.rsqrt(jnp.mean(x * x, axis=-1, keepdims=True) + EPS) * gain,)


def _head_norm(o, gain, n_heads):
    w = o.shape[-1] // n_heads
    parts = []
    for h in range(n_heads):
        oh = o[:, h * w:(h + 1) * w]
        parts.append(oh * lax.rsqrt(jnp.mean(oh * oh, axis=-1, keepdims=True) + EPS))
    return jnp.concatenate(parts, axis=-1) * gain


@jax.custom_jvp
def _neg_expm1(x):
    u = jnp.exp(x)
    is_one = u == 1.0
    return jnp.where(is_one, -x, (1.0 - u) * x / jnp.log(jnp.where(is_one, 2.0, u)))


@_neg_expm1.defjvp
def _neg_expm1_jvp(primals, tangents):
    (x,), (t,) = primals, tangents
    return _neg_expm1(x), -jnp.exp(x) * t


def _rg_gates_fn(xc, wa, wx, ba, bx, lam):
    outs = []
    for d in range(2):
        r = _sigmoid(_dot_nn(xc, wa[d]) + ba[d:d + 1])
        i = _sigmoid(_dot_nn(xc, wx[d]) + bx[d:d + 1])
        log_a = -RG_C * r * _softplus(-lam[d:d + 1])
        outs.append(jnp.exp(log_a))
        outs.append(jnp.sqrt(_neg_expm1(2.0 * log_a)) * (i * xc))
    return tuple(outs)


def _hg_pre_fn(q, f_f, f_b, logits):
    mx = jnp.maximum(logits[0:1], logits[1:2])
    e0 = jnp.exp(logits[0:1] - mx)
    e1 = jnp.exp(logits[1:2] - mx)
    lb = e0 / (e0 + e1)
    outs = [_silu(q)]
    for f in (f_f, f_b):
        outs.append((1.0 - lb) * _sigmoid(-f))
        outs.append(jnp.log(lb + (1.0 - lb) * _sigmoid(f)))
    return tuple(outs)


def _post0_fn(hs, ga, o, g, gain):
    ya = hs * jax.nn.gelu(ga, approximate=True)
    yb = _head_norm(o, gain, 4) * _silu(g)
    return (jnp.concatenate([ya, yb], axis=-1),)


def _post0_fwd_fn(h_f, h_b, ga, o_f, o_b, g, gain):
    return _post0_fn(h_f + h_b, ga, o_f + o_b, g, gain)


def _post0_bwd_fn(h_f, h_b, ga, o_f, o_b, g, dmix, gain):
    _, vjp = jax.vjp(_post0_fn, h_f + h_b, ga, o_f + o_b, g, gain)
    return vjp((dmix,))


def _gla_pre_fn(q, lr, w_up, b_gate):
    outs = [q * (128.0 ** -0.5)]
    for d in range(2):
        z = _dot_nn(lr, w_up[d]) + b_gate[d:d + 1]
        outs.append(-_softplus(-z) * (1.0 / 16.0))
    return tuple(outs)


def _gla_post_fn(o, r, gain):
    return (_head_norm(o, gain, 4) * _silu(r),)


def _gla_post_fwd_fn(o_f, o_b, r, gain):
    return _gla_post_fn(o_f + o_b, r, gain)


def _gla_post_bwd_fn(o_f, o_b, r, dmix, gain):
    _, vjp = jax.vjp(_gla_post_fn, o_f + o_b, r, gain)
    return vjp((dmix,))


def _relu2_bwd_epi(acc, hid):
    return (acc * 2.0 * jnp.maximum(hid, 0.0),)


def _relu2(x):
    r = jnp.maximum(x, 0.0)
    return r * r


def _add_epi(acc, res):
    return (acc + res,)


def _loss_head_fn(h, target, gain):
    def f(h, gain):
        y = _rmsnorm_fn(h, gain)[0]
        err = y - target
        return 0.5 * jnp.sum(jnp.mean(err * err, axis=-1, keepdims=True))
    loss, (dh, dgain) = jax.value_and_grad(f, argnums=(0, 1))(h, gain)
    return dh, jnp.full((1, LANES), loss, F32), dgain


def _adam_fn(w, g, m, v):
    m2 = ADAM_B1 * m + (1.0 - ADAM_B1) * g
    v2 = ADAM_B2 * v + (1.0 - ADAM_B2) * (g * g)
    m_hat = m2 / (1.0 - ADAM_B1 ** ADAM_STEP)
    v_hat = v2 / (1.0 - ADAM_B2 ** ADAM_STEP)
    delta = -ADAM_LR * (m_hat / (jnp.sqrt(v_hat) + ADAM_EPS) + ADAM_WD * w)
    return delta, m2, v2


def _shifted(x, t_idx, off):
    n = x.shape[0]
    rolled = pltpu.roll(x, (-off) % n, 0)
    valid = (t_idx + off >= 0) & (t_idx + off < n)
    return jnp.where(valid, rolled, 0.0)


def _conv_fwd(name, src, colblock, w, b):
    n_rows, width = src.shape[0], w.shape[1]

    def body(x_ref, w_ref, b_ref, o_ref):
        x = x_ref[...]
        t_idx = lax.broadcasted_iota(jnp.int32, x.shape, 0)
        acc = b_ref[...] + w_ref[2:3, :] * x
        acc += w_ref[0:1, :] * _shifted(x, t_idx, -2)
        acc += w_ref[1:2, :] * _shifted(x, t_idx, -1)
        acc += w_ref[3:4, :] * _shifted(x, t_idx, 1)
        o_ref[...] = acc

    nb = width // LANES
    return pl.pallas_call(
        body, name=name, grid=(nb,),
        in_specs=[pl.BlockSpec((n_rows, LANES), lambda j: (0, colblock * nb + j)),
                  pl.BlockSpec((4, LANES), lambda j: (0, j)), pl.BlockSpec((1, LANES), lambda j: (0, j))],
        out_specs=pl.BlockSpec((n_rows, LANES), lambda j: (0, j)),
        out_shape=jax.ShapeDtypeStruct((n_rows, width), F32),
        compiler_params=_params(("parallel",)),
    )(src, w, b)


def _conv_bwd(name, src, colblock, w, d):
    n_rows, width = src.shape[0], w.shape[1]

    def body(x_ref, w_ref, d_ref, dx_ref, dw_ref, db_ref):
        x = x_ref[...]
        g = d_ref[...]
        t_idx = lax.broadcasted_iota(jnp.int32, x.shape, 0)
        dx = w_ref[2:3, :] * g
        dx += w_ref[0:1, :] * _shifted(g, t_idx, 2)
        dx += w_ref[1:2, :] * _shifted(g, t_idx, 1)
        dx += w_ref[3:4, :] * _shifted(g, t_idx, -1)
        dx_ref[...] = dx.astype(dx_ref.dtype)
        dw_ref[0:1, :] = jnp.sum(g * _shifted(x, t_idx, -2), axis=0, keepdims=True)
        dw_ref[1:2, :] = jnp.sum(g * _shifted(x, t_idx, -1), axis=0, keepdims=True)
        dw_ref[2:3, :] = jnp.sum(g * x, axis=0, keepdims=True)
        dw_ref[3:4, :] = jnp.sum(g * _shifted(x, t_idx, 1), axis=0, keepdims=True)
        db_ref[...] = jnp.sum(g, axis=0, keepdims=True)

    nb = width // LANES
    return pl.pallas_call(
        body, name=name, grid=(nb,),
        in_specs=[pl.BlockSpec((n_rows, LANES), lambda j: (0, colblock * nb + j)),
                  pl.BlockSpec((4, LANES), lambda j: (0, j)),
                  pl.BlockSpec((n_rows, LANES), lambda j: (0, j))],
        out_specs=[pl.BlockSpec((n_rows, LANES), lambda j: (0, j)), pl.BlockSpec((4, LANES), lambda j: (0, j)),
                   pl.BlockSpec((1, LANES), lambda j: (0, j))],
        out_shape=[jax.ShapeDtypeStruct((n_rows, width), BF16), jax.ShapeDtypeStruct((4, width), F32),
                   jax.ShapeDtypeStruct((1, width), F32)],
        compiler_params=_params(("parallel",)),
    )(src, w, d)


SUBLANES = 8
SCAN_UNROLL = 8


def _shift_rows(x, d, fill):
    n = x.shape[0]
    t = lax.broadcasted_iota(jnp.int32, x.shape, 0)
    valid = (t >= d) if d > 0 else (t < n + d)
    return jnp.where(valid, pltpu.roll(x, d % n, 0), fill)


def _tile_scan(a, u, reverse):
    d = 1
    while d < a.shape[0]:
        s = -d if reverse else d
        a_sh, u_sh = _shift_rows(a, s, 1.0), _shift_rows(u, s, 0.0)
        u = u + a * u_sh
        a = a * a_sh
        d *= 2
    return a, u


def _edge_row(x, reverse):
    return x[0:1, :] if reverse else x[SUBLANES - 1:SUBLANES, :]


def _scan_specs(n_rows, n):
    return [pl.BlockSpec((n_rows, LANES), lambda j: (0, j))] * n


def _scan_tile(a_ref, u_ref, h_ref, i, carry, reverse):
    n_tiles = a_ref.shape[0] // SUBLANES
    tile = (n_tiles - 1 - i) if reverse else i
    rows = pl.ds(pl.multiple_of(tile * SUBLANES, SUBLANES), SUBLANES)
    acc_a, acc_u = _tile_scan(a_ref[rows, :], u_ref[rows, :], reverse)
    h = acc_u + acc_a * carry
    h_ref[rows, :] = h
    return _edge_row(h, reverse)


def _scan_fwd(name, a_f, u_f, a_b, u_b):
    n_rows, width = a_f.shape

    def body(af_ref, uf_ref, ab_ref, ub_ref, hf_ref, hb_ref):
        def step(i, carry):
            return (_scan_tile(af_ref, uf_ref, hf_ref, i, carry[0], False),
                    _scan_tile(ab_ref, ub_ref, hb_ref, i, carry[1], True))
        zero = jnp.zeros((1, LANES), F32)
        lax.fori_loop(0, n_rows // SUBLANES, step, (zero, zero), unroll=SCAN_UNROLL)

    return pl.pallas_call(
        body, name=name, grid=(width // LANES,), in_specs=_scan_specs(n_rows, 4), out_specs=_scan_specs(n_rows, 2),
        out_shape=[jax.ShapeDtypeStruct((n_rows, width), F32)] * 2, compiler_params=_params(("parallel",)),
    )(a_f, u_f, a_b, u_b)


def _scan_bwd_tile(a_ref, h_ref, dh_ref, du_ref, da_ref, i, carry, reverse):
    n_rows = a_ref.shape[0]
    n_tiles = n_rows // SUBLANES
    against = not reverse
    one = -1 if against else 1
    g_in, a_edge = carry
    tile = (n_tiles - 1 - i) if against else i
    start = pl.multiple_of(tile * SUBLANES, SUBLANES)
    rows = pl.ds(start, SUBLANES)
    a_tile = a_ref[rows, :]
    coeff = _shift_rows(a_tile, one, a_edge)
    acc_a, acc_u = _tile_scan(coeff, dh_ref[rows, :], against)
    g = acc_u + acc_a * g_in
    du_ref[rows, :] = g
    outside = (start + SUBLANES) if reverse else (start - 1)
    inside = (outside >= 0) & (outside < n_rows)
    h_edge = jnp.where(inside, h_ref[pl.ds(jnp.clip(outside, 0, n_rows - 1), 1), :], 0.0)
    da_ref[rows, :] = g * _shift_rows(h_ref[rows, :], -one, h_edge)
    return _edge_row(g, against), _edge_row(a_tile, against)


def _scan_bwd(name, a_f, h_f, a_b, h_b, dh):
    n_rows, width = a_f.shape

    def body(af_ref, hf_ref, ab_ref, hb_ref, dh_ref, duf_ref, daf_ref, dub_ref, dab_ref):
        def step(i, carry):
            return (_scan_bwd_tile(af_ref, hf_ref, dh_ref, duf_ref, daf_ref, i, carry[0], False),
                    _scan_bwd_tile(ab_ref, hb_ref, dh_ref, dub_ref, dab_ref, i, carry[1], True))
        zero = jnp.zeros((1, LANES), F32)
        lax.fori_loop(0, n_rows // SUBLANES, step, ((zero, zero), (zero, zero)), unroll=SCAN_UNROLL)

    return pl.pallas_call(
        body, name=name, grid=(width // LANES,), in_specs=_scan_specs(n_rows, 5), out_specs=_scan_specs(n_rows, 4),
        out_shape=[jax.ShapeDtypeStruct((n_rows, width), F32)] * 4, compiler_params=_params(("parallel",)),
    )(a_f, h_f, a_b, h_b, dh)


def _tri_mask(c, reverse):
    row = lax.broadcasted_iota(jnp.int32, (c, c), 0)
    col = lax.broadcasted_iota(jnp.int32, (c, c), 1)
    return (col >= row) if reverse else (col <= row)


def _cumsum_rows(x, reverse):
    tri = _tri_mask(x.shape[0], reverse).astype(BF16)
    hi = x.astype(BF16)
    rest = x - hi.astype(F32)
    mid = rest.astype(BF16)
    lo = (rest - mid.astype(F32)).astype(BF16)
    return _raw_nn(tri, hi) + _raw_nn(tri, mid) + _raw_nn(tri, lo)


@functools.partial(jax.custom_vjp, nondiff_argnums=(1,))
def _cumsum(x, reverse):
    return _cumsum_rows(x, reverse)


def _cumsum_fwd(x, reverse):
    return _cumsum_rows(x, reverse), None


def _cumsum_bwd(reverse, _, g):
    return (_cumsum_rows(g, not reverse),)


_cumsum.defvjp(_cumsum_fwd, _cumsum_bwd)


def _chunks_fn(qs, ks, vs, lfs, sts, reverses):
    n, c = len(qs), qs[0].shape[0]
    every = range(n)
    tris = [_tri_mask(c, r) for r in reverses]
    cums = [_cumsum(lfs[i], reverses[i]) for i in every]
    rid = lax.broadcasted_iota(jnp.int32, cums[0].shape, 0)

    def pick(cum, r):
        return jnp.sum(jnp.where(rid == r, cum, 0.0), axis=0, keepdims=True)

    refs = [pick(cums[i], (c - 1 - c // 2) if reverses[i] else c // 2) for i in every]
    lasts = [pick(cums[i], 0 if reverses[i] else c - 1) for i in every]
    q_in = [qs[i] * jnp.exp(cums[i] - refs[i]) for i in every]
    k_in = [ks[i] * jnp.exp(refs[i] - cums[i]) for i in every]
    scores = [jnp.where(tris[i], _dot_nt(q_in[i], k_in[i]), 0.0) for i in every]
    o_intra = [_dot_nn(scores[i], vs[i]) for i in every]
    q_out = [qs[i] * jnp.exp(cums[i]) for i in every]
    o_inter = [_dot_nt(q_out[i], sts[i]) for i in every]
    k_state = [ks[i] * jnp.exp(lasts[i] - cums[i]) for i in every]
    upd = [_dot_tn(vs[i], k_state[i]) for i in every]
    st_new = [sts[i] * jnp.exp(lasts[i]) + upd[i] for i in every]
    return [o_intra[i] + o_inter[i] for i in every], st_new


def _attn_fwd(name, q, k_f, k_b, v, lf_f, lf_b, n_heads, dk, dv):
    n_rows = q[0].shape[0]
    n_chunks = n_rows // CHUNK
    n_steps = n_chunks // ATTN_SUB
    wk, wv = n_heads * dk, n_heads * dv

    def spec(width, off, rev):
        return pl.BlockSpec((CHUNK * ATTN_SUB, width), lambda n: ((n_steps - 1 - n) if rev else n, off))

    def sspec(rev):
        return pl.BlockSpec((ATTN_SUB, n_heads, dv, dk), lambda n: ((n_steps - 1 - n) if rev else n, 0, 0, 0))

    def body(qf, kf, vf, lff, qb, kb, vb, lfb, of_ref, ob_ref, sf_ref, sb_ref, st):
        @pl.when(pl.program_id(0) == 0)
        def _():
            st[...] = jnp.zeros_like(st)

        ins = ((qf, kf, vf, lff), (qb, kb, vb, lfb))
        chains = [(d, h) for d in range(2) for h in range(n_heads)]
        ck = [slice(h * dk, (h + 1) * dk) for h in range(n_heads)]
        cv = [slice(h * dv, (h + 1) * dv) for h in range(n_heads)]
        sts = [st[d, h] for d, h in chains]
        done = []
        for sub in range(ATTN_SUB):
            local = (sub, ATTN_SUB - 1 - sub)
            rows = [slice(local[d] * CHUNK, (local[d] + 1) * CHUNK) for d in range(2)]
            qs = [ins[d][0][rows[d], ck[h]] for d, h in chains]
            ks = [ins[d][1][rows[d], ck[h]] for d, h in chains]
            vs = [ins[d][2][rows[d], cv[h]] for d, h in chains]
            lfs = [ins[d][3][rows[d], ck[h]] for d, h in chains]
            os_, st_new = _chunks_fn(qs, ks, vs, lfs, sts, [d == 1 for d, _ in chains])
            done.append((local, rows, sts, os_))
            sts = st_new
        for local, rows, entered, os_ in done:
            for i, (d, h) in enumerate(chains):
                (sf_ref, sb_ref)[d][local[d], h] = entered[i].astype(BF16)
                (of_ref, ob_ref)[d][rows[d], cv[h]] = os_[i]
        for i, (d, h) in enumerate(chains):
            st[d, h] = sts[i]

    in_specs = [spec(wk, q[1], False), spec(wk, k_f[1], False), spec(wv, v[1], False), spec(wk, lf_f[1], False),
                spec(wk, q[1], True), spec(wk, k_b[1], True), spec(wv, v[1], True), spec(wk, lf_b[1], True)]
    return pl.pallas_call(
        body, name=name, grid=(n_steps,), in_specs=in_specs,
        out_specs=[spec(wv, 0, False), spec(wv, 0, True), sspec(False), sspec(True)],
        out_shape=[jax.ShapeDtypeStruct((n_rows, wv), F32)] * 2
        + [jax.ShapeDtypeStruct((n_chunks, n_heads, dv, dk), BF16)] * 2,
        scratch_shapes=[pltpu.VMEM((2, n_heads, dv, dk), F32)],
        compiler_params=_params(("arbitrary",)),
    )(q[0], k_f[0], v[0], lf_f[0], q[0], k_b[0], v[0], lf_b[0])


def _attn_bwd(name, q, k_f, k_b, v, lf_f, lf_b, st_f, st_b, do, n_heads, dk, dv, out_dtype=F32):
    n_rows = q[0].shape[0]
    n_chunks = n_rows // CHUNK
    n_steps = n_chunks // ATTN_SUB
    wk, wv = n_heads * dk, n_heads * dv

    def spec(width, off, rev):
        return pl.BlockSpec((CHUNK * ATTN_SUB, width), lambda n: (n if rev else (n_steps - 1 - n), off))

    def sspec(rev):
        return pl.BlockSpec((ATTN_SUB, n_heads, dv, dk), lambda n: (n if rev else (n_steps - 1 - n), 0, 0, 0))

    def body(qf, kf, vf, lff, sf, dof, qb, kb, vb, lfb, sb, dob,
             dqf, dkf, dvf, dlff, dqb, dkb, dvb, dlfb, dst):
        @pl.when(pl.program_id(0) == 0)
        def _():
            dst[...] = jnp.zeros_like(dst)

        ins = ((qf, kf, vf, lff, sf, dof), (qb, kb, vb, lfb, sb, dob))
        outs = ((dqf, dkf, dvf, dlff), (dqb, dkb, dvb, dlfb))
        chains = [(d, h) for d in range(2) for h in range(n_heads)]
        ck = [slice(h * dk, (h + 1) * dk) for h in range(n_heads)]
        cv = [slice(h * dv, (h + 1) * dv) for h in range(n_heads)]
        fn = functools.partial(_chunks_fn, reverses=[d == 1 for d, _ in chains])
        dsts = [dst[d, h] for d, h in chains]
        done = []
        for sub in range(ATTN_SUB):
            local = (ATTN_SUB - 1 - sub, sub)
            rows = [slice(local[d] * CHUNK, (local[d] + 1) * CHUNK) for d in range(2)]
            qs = [ins[d][0][rows[d], ck[h]] for d, h in chains]
            ks = [ins[d][1][rows[d], ck[h]] for d, h in chains]
            vs = [ins[d][2][rows[d], cv[h]] for d, h in chains]
            lfs = [ins[d][3][rows[d], ck[h]] for d, h in chains]
            sts = [ins[d][4][local[d], h].astype(F32) for d, h in chains]
            dos = [ins[d][5][rows[d], cv[h]] for d, h in chains]
            _, vjp = jax.vjp(fn, qs, ks, vs, lfs, sts)
            dqs, dks, dvs, dlfs, dsts = vjp((dos, dsts))
            done.append((rows, dqs, dks, dvs, dlfs))
        for rows, dqs, dks, dvs, dlfs in done:
            for i, (d, h) in enumerate(chains):
                dq_r, dk_r, dv_r, dlf_r = outs[d]
                dq_r[rows[d], ck[h]] = dqs[i].astype(dq_r.dtype)
                dk_r[rows[d], ck[h]] = dks[i].astype(dk_r.dtype)
                dv_r[rows[d], cv[h]] = dvs[i].astype(dv_r.dtype)
                dlf_r[rows[d], ck[h]] = dlfs[i].astype(dlf_r.dtype)
        for i, (d, h) in enumerate(chains):
            dst[d, h] = dsts[i]

    def dir_specs(kk, lf, rev):
        return [spec(wk, q[1], rev), spec(wk, kk[1], rev), spec(wv, v[1], rev), spec(wk, lf[1], rev), sspec(rev),
                spec(wv, 0, rev)]

    def dir_out_specs(rev):
        return [spec(wk, 0, rev), spec(wk, 0, rev), spec(wv, 0, rev), spec(wk, 0, rev)]

    shapes = [jax.ShapeDtypeStruct((n_rows, wk), out_dtype), jax.ShapeDtypeStruct((n_rows, wk), out_dtype),
              jax.ShapeDtypeStruct((n_rows, wv), out_dtype), jax.ShapeDtypeStruct((n_rows, wk), F32)]
    outs = pl.pallas_call(
        body, name=name, grid=(n_steps,), in_specs=dir_specs(k_f, lf_f, False) + dir_specs(k_b, lf_b, True),
        out_specs=dir_out_specs(False) + dir_out_specs(True), out_shape=shapes + shapes,
        scratch_shapes=[pltpu.VMEM((2, n_heads, dv, dk), F32)],
        compiler_params=_params(("arbitrary",)),
    )(q[0], k_f[0], v[0], lf_f[0], st_f, do, q[0], k_b[0], v[0], lf_b[0], st_b, do)
    return outs[:4], outs[4:]


def _row2(v):
    return v.reshape(1, -1)


def _mlp_fwd(tag, h, gain, w1, w2):
    y = _rowcall(f"{tag}_norm", _rmsnorm_fn, [(h, h.shape[1], 0)], [gain], [(h.shape[1], BF16)], tm=512)[0]
    hid = _mm(f"{tag}_up", y, w1, out_dtypes=(BF16,))
    h_out = _mm(f"{tag}_down", hid, w2, a_pro=_relu2, extras=(h,), epi=_add_epi)
    return h_out, (y, hid)


def _dw(name, a, b, **kw):
    return _mm(name, a, b, mode="tn", epi=lambda acc: (acc, acc), out_dtypes=(F32, BF16), **kw)


def _mlp_bwd(tag, h, gain, w1, w2, saved, dh_out):
    y, hid = saved
    dhid = _mm(f"{tag}_dact", dh_out, w2, mode="nt", extras=(hid,), epi=_relu2_bwd_epi, out_dtypes=(BF16,))
    dw2 = _dw(f"{tag}_dw2", hid, dh_out, a_pro=_relu2)
    dw1 = _dw(f"{tag}_dw1", y, dhid, out_split=N_CHIPS)
    dh, dgain = _dy_norm_bwd(f"{tag}_dy", dhid, w1, h, gain, dh_out)
    return dh, dgain, dw1, dw2


def _dy_norm_bwd(name, dz, w, h, gain, dres, pin=None, **tiles):
    def epi(dy, h_tile, dres_tile, gain_row):
        _, vjp = jax.vjp(lambda u, v: _rmsnorm_fn(u, v)[0], h_tile, gain_row)
        dh, dgain = vjp(dy)
        return dh + dres_tile, dgain

    assert h.shape[1] <= 1024
    tiles.setdefault("tm", 1024)
    dh, dgain_parts = _mm(name, dz, w, mode="nt", extras=(h, dres), epi=epi, epi_pars=(gain,), row_sum=True,
                          pin=pin, **tiles)
    return dh, jnp.sum(dgain_parts, axis=0)


def _local_step(x, target, w, pin=None, late=None, emit=None):
    g = {}
    d_model = x.shape[1]
    rg_w = hg_w = d_model // 2
    pins = []

    def send_off(tag, pairs):
        if emit is not None:
            pins.append(emit(tag, [p[0] for p in pairs], [p[1] for p in pairs]))

    def both(fn, pair):
        return [fn(t) for t in pair]

    def chip_major(t):
        return t.reshape(N_CHIPS, t.shape[0] // N_CHIPS, t.shape[1])

    h_a0 = x
    gain = _row2(w["norm_mix"][0])
    y0 = _rowcall("l0_norm", _rmsnorm_fn, [(h_a0, d_model, 0)], [gain], [(d_model, BF16)], tm=512, pin=pin)[0]
    proj0 = _mm("l0_in", y0, w["ab_w_in"])
    conv_w, conv_b = w["rg_conv_w"], _row2(w["rg_conv_b"])
    xc = _conv_fwd("rg_conv", proj0, 0, conv_w, conv_b)
    gate_pars = [w["rg_wa_bd"], w["rg_wx_bd"], w["rg_b_a"], w["rg_b_x"], w["rg_lambda"]]
    a_f, u_f, a_b, u_b = _rowcall("rg_gates", _rg_gates_fn, [(xc, rg_w, 0)], gate_pars, [(rg_w, F32)] * 4)
    hs_f, hs_b = _scan_fwd("rg_scan", a_f, u_f, a_b, u_b)
    hg_rows = [(proj0, hg_w, 2), (proj0, hg_w, 3), (proj0, hg_w, 4)]
    qh, k_f, lf_f, k_b, lf_b = _rowcall("hg_pre", _hg_pre_fn, hg_rows, [w["hg_lb_logits"]], [(hg_w, F32)] * 5)
    iv = (proj0, 5)
    o_f, o_b, st_f, st_b = _attn_fwd("hg_attn", (qh, 0), (k_f, 0), (k_b, 0), iv, (lf_f, 0), (lf_b, 0), 4, 128, 128)
    post0_rows = [(hs_f, rg_w, 0), (hs_b, rg_w, 0), (proj0, rg_w, 1), (o_f, hg_w, 0), (o_b, hg_w, 0), (proj0, hg_w, 6)]
    hg_gain = _row2(w["hg_norm"])
    mix_in0 = _rowcall("l0_post", _post0_fwd_fn, post0_rows, [hg_gain], [(d_model, BF16)])[0]
    if late is not None:
        w = {**w, **late(mix_in0)}
    h_b0 = _mm("l0_out", mix_in0, w["ab_w_out"], extras=(h_a0,), epi=_add_epi)
    h_c0, mlp0 = _mlp_fwd("mlp0", h_b0, _row2(w["norm_mlp"][0]), w["mlp_w1"][0], w["mlp_w2"][0])

    h_a1 = h_c0
    gain1 = _row2(w["norm_mix"][1])
    y1 = _rowcall("l1_norm", _rmsnorm_fn, [(h_a1, d_model, 0)], [gain1], [(d_model, BF16)], tm=512)[0]
    proj1 = _mm("l1_in", y1, w["gla_w_in_pad"], tn=640)
    gla_pars = [w["gla_w_up_pad"], w["gla_b_gate"]]
    gq, glf_f, glf_b = _rowcall("gla_pre", _gla_pre_fn, [(proj1, 512, 0), (proj1, LANES, 24)], gla_pars, [(512, F32)] * 3)
    gk, gv = (proj1, 1), (proj1, 1)
    go_f, go_b, gst_f, gst_b = _attn_fwd("gla_attn", (gq, 0), gk, gk, gv, (glf_f, 0), (glf_b, 0), 4, 128, 256)
    gla_gain = _row2(w["gla_norm"])
    post1_rows = [(go_f, d_model, 0), (go_b, d_model, 0), (proj1, d_model, 2)]
    mix_in1 = _rowcall("l1_post", _gla_post_fwd_fn, post1_rows, [gla_gain], [(d_model, BF16)])[0]
    h_b1 = _mm("l1_out", mix_in1, w["gla_w_out"], extras=(h_a1,), epi=_add_epi)
    h_c1, mlp1 = _mlp_fwd("mlp1", h_b1, _row2(w["norm_mlp"][1]), w["mlp_w1"][1], w["mlp_w2"][1])

    dh, loss, g["norm_final"] = _rowcall(
        "loss_head", _loss_head_fn, [(h_c1, d_model, 0), (target, d_model, 0)], [_row2(w["norm_final"])],
        [(d_model, F32)], [(1, LANES), (1, d_model)], tm=512)

    dh, g_nmlp1, g_w1_1, g_w2_1 = _mlp_bwd("mlp1", h_b1, _row2(w["norm_mlp"][1]), w["mlp_w1"][1], w["mlp_w2"][1], mlp1, dh)
    send_off("mlp1", [g_w1_1, both(chip_major, g_w2_1)])
    dmix1 = _mm("l1_dout", dh, w["gla_w_out"], mode="nt")
    g_gla_out = _dw("l1_dwout", mix_in1, dh)
    g["gla_w_out"] = g_gla_out[0]
    dgo, dr, g["gla_norm"] = _rowcall(
        "l1_dpost", _gla_post_bwd_fn, post1_rows + [(dmix1, d_model, 0)], [gla_gain],
        [(d_model, F32), (d_model, BF16)], [(1, d_model)], pin=pins.pop() if pins else None)
    (dq_f, dk_f, dv_f, dlf_f), (dq_b, dk_b, dv_b, dlf_b) = _attn_bwd(
        "gla_dattn", (gq, 0), gk, gk, gv, (glf_f, 0), (glf_b, 0), gst_f, gst_b, dgo, 4, 128, 256)

    def gla_pre_bwd(q, lr, dq1, dq2, dlf1, dlf2, dk1, dk2, dv1, dv2, w_up, b_gate):
        dlr = jnp.zeros_like(lr)
        dws, dbs = [], []
        for d, dlf in enumerate((dlf1, dlf2)):
            z = _raw_nn(lr, w_up[d]) + b_gate[d:d + 1]
            dz = dlf * _sigmoid(-z) * (1.0 / 16.0)
            dlr = dlr + _raw_nt(dz, w_up[d])
            dws.append(_raw_tn(dz, lr))
            dbs.append(jnp.sum(dz, axis=0, keepdims=True))
        return ((dq1 + dq2) * (128.0 ** -0.5), dk1 + dk2, dv1 + dv2, dlr, dws[0], dws[1], dbs[0], dbs[1])

    rows = [(proj1, 512, 0), (proj1, LANES, 24), (dq_f, 512, 0), (dq_b, 512, 0), (dlf_f, 512, 0), (dlf_b, 512, 0),
            (dk_f, 512, 0), (dk_b, 512, 0), (dv_f, d_model, 0), (dv_b, d_model, 0)]
    dq, dk, dv, dlr, dwt_f, dwt_b, db_f, db_b = _rowcall(
        "gla_dpre", gla_pre_bwd, rows, gla_pars, [(512, BF16), (512, BF16), (d_model, BF16), (LANES, BF16)],
        [(512, LANES), (512, LANES), (1, 512), (1, 512)])
    g["gla_w_up_pad"] = jnp.stack([dwt_f.T, dwt_b.T])
    g["gla_b_gate"] = jnp.concatenate([db_f, db_b], axis=0)
    dproj1 = jnp.concatenate([dq, dk, dv, dr, dlr], axis=1)
    g_gla_in = both(lambda t: _split_chips(t[:, :GLA_IN_WIDTH], 1), _dw("l1_dwin", y1, dproj1, tn=640))
    g["gla_w_in"] = g_gla_in[0]
    send_off("gla", [g_gla_in, both(chip_major, g_gla_out)])
    dh, g_nmix1 = _dy_norm_bwd("l1_dy", dproj1, w["gla_w_in_pad"], h_a1, gain1, dh,
                               pin=pins.pop() if pins else None, tk=640)

    dh, g_nmlp0, g_w1_0, g_w2_0 = _mlp_bwd("mlp0", h_b0, _row2(w["norm_mlp"][0]), w["mlp_w1"][0], w["mlp_w2"][0], mlp0, dh)
    g_ab_out = _dw("l0_dwout", mix_in0, dh)
    g["ab_w_out"] = g_ab_out[0]
    send_off("mlp0", [g_w1_0, both(chip_major, g_w2_0), both(chip_major, g_ab_out)])
    dmix0 = _mm("l0_dout", dh, w["ab_w_out"], mode="nt")
    dhs, dga, do, dg, g["hg_norm"] = _rowcall(
        "l0_dpost", _post0_bwd_fn, post0_rows + [(dmix0, d_model, 0)], [hg_gain],
        [(rg_w, F32), (rg_w, BF16), (hg_w, F32), (hg_w, BF16)], [(1, hg_w)], pin=pins.pop() if pins else None)
    (dqh_f, dk_f, div_f, dlf_f), (dqh_b, dk_b, div_b, dlf_b) = _attn_bwd(
        "hg_dattn", (qh, 0), (k_f, 0), (k_b, 0), iv, (lf_f, 0), (lf_b, 0), st_f, st_b, do, 4, 128, 128)

    def hg_pre_bwd(q, f_f, f_b, dq1, dq2, dk1, dlf1, dk2, dlf2, dv1, dv2, logits):
        _, vjp = jax.vjp(_hg_pre_fn, q, f_f, f_b, logits)
        dq, df_f, df_b, dlogits = vjp((dq1 + dq2, dk1, dlf1, dk2, dlf2))
        return dq, df_f, df_b, dv1 + dv2, dlogits

    rows = hg_rows + [(t, hg_w, 0) for t in (dqh_f, dqh_b, dk_f, dlf_f, dk_b, dlf_b, div_f, div_b)]
    dq, df_f, df_b, div, g["hg_lb_logits"] = _rowcall(
        "hg_dpre", hg_pre_bwd, rows, [w["hg_lb_logits"]], [(hg_w, BF16)] * 4, [(2, hg_w)])
    du_f, da_f, du_b, da_b = _scan_bwd("rg_dscan", a_f, hs_f, a_b, hs_b, dhs)
    gates_bwd = _vjp_of(_rg_gates_fn, 1, 4, 5)
    rows = [(xc, rg_w, 0), (da_f, rg_w, 0), (du_f, rg_w, 0), (da_b, rg_w, 0), (du_b, rg_w, 0)]
    dxc, g["rg_wa_bd"], g["rg_wx_bd"], g["rg_b_a"], g["rg_b_x"], g["rg_lambda"] = _rowcall(
        "rg_dgates", gates_bwd, rows, gate_pars, [(rg_w, F32)],
        [(2, rg_w, rg_w), (2, rg_w, rg_w), (2, rg_w), (2, rg_w), (2, rg_w)])
    dxa, g["rg_conv_w"], g["rg_conv_b"] = _conv_bwd("rg_dconv", proj0, 0, conv_w, dxc)
    dproj0 = jnp.concatenate([dxa, dga, dq, df_f, df_b, div, dg], axis=1)
    g_ab_in = _dw("l0_dwin", y0, dproj0, out_split=N_CHIPS)
    g["ab_w_in"] = g_ab_in[0]
    send_off("ab", [g_ab_in])
    grad_x, g_nmix0 = _dy_norm_bwd("l0_dy", dproj0, w["ab_w_in"], h_a0, gain, dh, pin=pins.pop() if pins else None)

    g["norm_mix"] = jnp.concatenate([g_nmix0, g_nmix1], axis=0)
    g["norm_mlp"] = jnp.concatenate([g_nmlp0, g_nmlp1], axis=0)
    g["mlp_w1"] = [g_w1_0[0], g_w1_1[0]]
    g["mlp_w2"] = [g_w2_0[0], g_w2_1[0]]
    return loss, grad_x, g


def _block_diag(w):
    d, g, n, _ = w.shape
    eye = jnp.eye(g, dtype=w.dtype)
    return (w[:, :, :, None, :] * eye[None, :, None, :, None]).reshape(d, g * n, g * n)


def _block_diag_extract(wbd, g):
    d, gn, _ = wbd.shape
    n = gn // g
    blocks = wbd.reshape(d, g, n, g, n)
    return jnp.stack([blocks[:, i, :, i, :] for i in range(g)], axis=1)


def _prepare_weights(big, full):
    w = {k: full[k] for k in ("norm_mix", "norm_mlp", "norm_final", "hg_lb_logits")}
    for k in ("rg_conv_w", "rg_conv_b", "rg_b_a", "rg_b_x", "rg_lambda", "hg_norm", "gla_b_gate", "gla_norm"):
        w[k] = full[k][0]
    w["rg_wa_bd"] = _block_diag(full["rg_w_a"][0])
    w["rg_wx_bd"] = _block_diag(full["rg_w_x"][0])
    up = full["gla_w_gate_up"][0]
    rank = up.shape[1]
    pad = jnp.zeros((2, LANES, up.shape[2]), F32)
    w["gla_w_up_pad"] = pad.at[0, 0:rank].set(up[0]).at[1, rank:2 * rank].set(up[1])
    w.update(_prepare_matrices(big))
    return w


def _prepare_matrices(big):
    w = {}
    if "mlp_w1" in big:
        w["mlp_w1"] = list(big["mlp_w1"])
        w["mlp_w2"] = [t.reshape(-1, t.shape[-1]) for t in big["mlp_w2"]]
    if "ab_w_in" in big:
        w["ab_w_in"] = big["ab_w_in"]
    if "ab_w_out" in big:
        w["ab_w_out"] = big["ab_w_out"].reshape(-1, big["ab_w_out"].shape[-1])
    if "gla_w_in" in big:
        w["gla_w_out"] = big["gla_w_out"].reshape(-1, big["gla_w_out"].shape[-1])
        gla_in = _join_chips(big["gla_w_in"], 1)
        w["gla_w_in_pad"] = jnp.pad(gla_in, ((0, 0), (0, GLA_IN_PAD - gla_in.shape[1])))
    return w


def _finish_grads(g, rank=16, rg_blocks=8):
    def chip_major(t):
        return t.reshape(N_CHIPS, t.shape[0] // N_CHIPS, t.shape[1])

    big = {
        "mlp_w1": list(g["mlp_w1"]), "mlp_w2": [chip_major(t) for t in g["mlp_w2"]],
        "ab_w_in": g["ab_w_in"], "ab_w_out": chip_major(g["ab_w_out"]),
        "gla_w_in": g["gla_w_in"], "gla_w_out": chip_major(g["gla_w_out"]),
    }
    small = {
        "norm_mix": g["norm_mix"], "norm_mlp": g["norm_mlp"], "norm_final": g["norm_final"][0],
        "rg_conv_w": g["rg_conv_w"][None], "rg_conv_b": g["rg_conv_b"],
        "rg_w_a": _block_diag_extract(g["rg_wa_bd"], rg_blocks)[None], "rg_b_a": g["rg_b_a"][None],
        "rg_w_x": _block_diag_extract(g["rg_wx_bd"], rg_blocks)[None], "rg_b_x": g["rg_b_x"][None],
        "rg_lambda": g["rg_lambda"][None], "hg_lb_logits": g["hg_lb_logits"], "hg_norm": g["hg_norm"],
        "gla_w_gate_up": jnp.stack([g["gla_w_up_pad"][0, 0:rank], g["gla_w_up_pad"][1, rank:2 * rank]])[None],
        "gla_b_gate": g["gla_b_gate"][None], "gla_norm": g["gla_norm"],
    }
    return big, small


MATRICES = (("mlp_w1", 0), ("mlp_w1", 1), ("mlp_w2", 0), ("mlp_w2", 1), ("ab_w_in", 0), ("ab_w_out", 0),
            ("gla_w_in", 0), ("gla_w_out", 0))
EARLY_MATRICES = ("ab_w_in",)
SMALL_SHARDED = ("rg_conv_w", "rg_b_a", "rg_b_x", "rg_lambda", "gla_w_gate_up", "gla_b_gate", "gla_norm")
SMALL_REPLICATED = ("norm_mix", "norm_mlp", "norm_final", "rg_conv_b", "rg_w_a", "rg_w_x", "hg_lb_logits", "hg_norm")
WEIGHTS = ("norm_mix", "norm_mlp", "norm_final", "mlp_w1", "mlp_w2", "ab_w_in", "ab_w_out", "rg_conv_w", "rg_conv_b",
           "rg_w_a", "rg_b_a", "rg_w_x", "rg_b_x", "rg_lambda", "hg_lb_logits", "hg_norm", "gla_w_in", "gla_w_out",
           "gla_w_gate_up", "gla_b_gate", "gla_norm")
ROW_ALIGN = 16


def _pack(arrays, lead=0):
    head = arrays[0].shape[:lead]
    flat = jnp.concatenate([a.reshape(head + (-1,)) for a in arrays], axis=lead)
    n = flat.shape[-1]
    quantum = LANES * ROW_ALIGN
    padded = -(-n // quantum) * quantum
    if padded != n:
        flat = jnp.pad(flat, [(0, 0)] * lead + [(0, padded - n)])
    return flat.reshape(head + (padded // LANES, LANES))


def _unpack(buf, shapes, lead=0):
    head = buf.shape[:lead]
    flat = buf.reshape(head + (-1,))
    out, off = [], 0
    for s in shapes:
        n = 1
        for v in s:
            n *= v
        out.append(lax.slice_in_dim(flat, off, off + n, axis=lead).reshape(head + tuple(s)))
        off += n
    return out


def _join_chips(gathered, axis):
    t = jnp.moveaxis(gathered, 0, axis)
    return t.reshape(t.shape[:axis] + (t.shape[axis] * t.shape[axis + 1],) + t.shape[axis + 2:])


def _split_chips(full, axis):
    s = full.shape
    t = full.reshape(s[:axis] + (N_CHIPS, s[axis] // N_CHIPS) + s[axis + 1:])
    return jnp.moveaxis(t, axis, 0)


_ANY = pl.BlockSpec(memory_space=pl.ANY)


def _place():
    return lax.axis_index("x"), lax.axis_index("y"), lax.axis_index("c")


def _into_slot(name, src, slot, n_slots, dtype, tm, layer=None):
    r, lanes = src.shape[-2:]
    tm = _row_tile(r, tm, ROW_ALIGN)

    def body(slot_ref, in_ref, o_ref):
        o_ref[...] = in_ref[...].astype(o_ref.dtype)

    if layer is None:
        in_spec = pl.BlockSpec((tm, lanes), lambda i, slot_ref: (i, 0))
    else:
        in_spec = pl.BlockSpec((None, tm, lanes), lambda i, slot_ref: (layer, i, 0))
    grid_spec = pltpu.PrefetchScalarGridSpec(
        num_scalar_prefetch=1, grid=(r // tm,), in_specs=[in_spec],
        out_specs=pl.BlockSpec((None, tm, lanes), lambda i, slot_ref: (slot_ref[0], i, 0)))
    return pl.pallas_call(
        body, name=name, grid_spec=grid_spec, out_shape=jax.ShapeDtypeStruct((n_slots, r, lanes), dtype),
        compiler_params=_params(("parallel",)),
    )(slot.reshape(1).astype(jnp.int32), src)


def _chip_peers():
    x, y, c = _place()
    return 2 * x + y, c, [(1 - x, y), (x, 1 - y), (1 - x, 1 - y)]


def _comm_call(name, body, ins, out_shapes, n_sems, aliases=None):
    return pl.pallas_call(
        body, name=name, in_specs=[_ANY] * len(ins), out_specs=[_ANY] * len(out_shapes), out_shape=out_shapes,
        input_output_aliases=aliases or {},
        scratch_shapes=[pltpu.SemaphoreType.DMA((n_sems,)), pltpu.SemaphoreType.DMA((n_sems,))],
    )(*ins)


def _gather_chips(name, bufs):
    n = len(bufs)

    def body(*refs):
        outs, send_sems, recv_sems = refs[n:2 * n], refs[2 * n], refs[2 * n + 1]
        x, y, c = _place()
        me, _, peers = _chip_peers()

        def rows(a, block, half):
            rh = outs[a].shape[1] // 2
            return outs[a].at[block, pl.ds(half * rh, rh)]

        def copy(a, j, block, half, to, sem):
            return pltpu.make_async_remote_copy(
                src_ref=rows(a, block, half), dst_ref=rows(a, block, half), send_sem=send_sems.at[sem],
                recv_sem=recv_sems.at[sem], device_id=to, device_id_type=MESH)

        def over_ici(a, j, block):
            px, py = peers[j]
            return copy(a, j, block, c, (px, py, c), 6 * a + j)

        def to_sibling(a, j, block, half):
            return copy(a, j, block, half, (x, y, 1 - c), 6 * a + 3 + j)

        sends = [over_ici(a, j, me) for a in range(n) for j in range(3)]
        for cp in sends:
            cp.start()
        for a in range(n):
            for j, (px, py) in enumerate(peers):
                over_ici(a, j, 2 * px + py).wait_recv()
                handed = to_sibling(a, j, 2 * px + py, c)
                handed.start()
                sends.append(handed)
        for a in range(n):
            for j, (px, py) in enumerate(peers):
                to_sibling(a, j, 2 * px + py, 1 - c).wait_recv()
        for cp in sends:
            cp.wait_send()

    shapes = [jax.ShapeDtypeStruct(b.shape, b.dtype) for b in bufs]
    return _comm_call(name, body, bufs, shapes, 6 * n, {a: a for a in range(n)})


_HBM = pl.BlockSpec(memory_space=pltpu.HBM)
_SEM = pl.BlockSpec(memory_space=pltpu.SEMAPHORE)
_EFFECT = pltpu.SideEffectType.DATAFLOW_SIDE_EFFECTING


def _half_rows(ref, block, half):
    rh = ref.shape[1] // 2
    return ref.at[block, pl.ds(half * rh, rh)]


def _gather_start(name, bufs, after):
    n = len(bufs)

    def body(*refs):
        ins, send_sems, recv_sems, token = refs[:n], refs[n + 1], refs[n + 2], refs[-1]
        me, c, peers = _chip_peers()
        for a in range(n):
            mine = _half_rows(ins[a], me, c)
            for j, (px, py) in enumerate(peers):
                pltpu.make_async_remote_copy(
                    src_ref=mine, dst_ref=mine, send_sem=send_sems.at[3 * a + j], recv_sem=recv_sems.at[3 * a + j],
                    device_id=(px, py, c), device_id_type=MESH).start()
        token[...] = jnp.zeros_like(token)

    out_shape = (pltpu.SemaphoreType.DMA((3 * n,)), pltpu.SemaphoreType.DMA((3 * n,)),
                 *[pltpu.HBM(b.shape, b.dtype) for b in bufs], jax.ShapeDtypeStruct((8, LANES), F32))
    return pl.pallas_call(
        body, name=name, out_shape=out_shape, in_specs=[_HBM] * n + [_ANY],
        out_specs=(_SEM, _SEM, *[_HBM] * n, pl.BlockSpec(memory_space=pltpu.VMEM)),
        input_output_aliases={a: 2 + a for a in range(n)},
        compiler_params=pltpu.CompilerParams(has_side_effects=_EFFECT),
    )(*[pltpu.with_memory_space_constraint(b, pltpu.HBM) for b in bufs], after)


def _gather_wait(name, bufs, send_sems, recv_sems, after):
    n = len(bufs)

    def body(*refs):
        ins, send_sems, recv_sems = refs[:n], refs[n], refs[n + 1]
        me, c, peers = _chip_peers()
        for a in range(n):
            for j, (px, py) in enumerate(peers):
                copy = pltpu.make_async_remote_copy(
                    src_ref=_half_rows(ins[a], me, c), dst_ref=_half_rows(ins[a], 2 * px + py, c),
                    send_sem=send_sems.at[3 * a + j], recv_sem=recv_sems.at[3 * a + j],
                    device_id=(px, py, c), device_id_type=MESH)
                copy.wait_send()
                copy.wait_recv()

    return pl.pallas_call(
        body, name=name, out_shape=tuple(pltpu.HBM(b.shape, b.dtype) for b in bufs),
        in_specs=[_HBM] * n + [_SEM, _SEM, _ANY], out_specs=tuple([_HBM] * n),
        input_output_aliases={a: a for a in range(n)},
        compiler_params=pltpu.CompilerParams(has_side_effects=_EFFECT),
    )(*bufs, send_sems, recv_sems, after)


def _hand_over(name, bufs):
    n = len(bufs)

    def body(*refs):
        outs, send_sems, recv_sems = refs[n:2 * n], refs[2 * n], refs[2 * n + 1]
        x, y, c = _place()
        _, _, peers = _chip_peers()

        def copy(a, j, half):
            px, py = peers[j]
            rows = _half_rows(outs[a], 2 * px + py, half)
            return pltpu.make_async_remote_copy(
                src_ref=rows, dst_ref=rows, send_sem=send_sems.at[3 * a + j], recv_sem=recv_sems.at[3 * a + j],
                device_id=(x, y, 1 - c), device_id_type=MESH)

        sends = [copy(a, j, c) for a in range(n) for j in range(3)]
        for cp in sends:
            cp.start()
        for a in range(n):
            for j in range(3):
                copy(a, j, 1 - c).wait_recv()
        for cp in sends:
            cp.wait_send()

    shapes = [jax.ShapeDtypeStruct(b.shape, b.dtype) for b in bufs]
    return _comm_call(name, body, bufs, shapes, 3 * n, {a: a for a in range(n)})


def _pair_gather(name, bufs):
    n = len(bufs)

    def body(*refs):
        ins, outs, send_sems, recv_sems = refs[:n], refs[n:2 * n], refs[2 * n], refs[2 * n + 1]
        x, y, c = _place()

        def copy(a, block):
            return pltpu.make_async_remote_copy(
                src_ref=ins[a].at[block], dst_ref=outs[a].at[block], send_sem=send_sems.at[a],
                recv_sem=recv_sems.at[a], device_id=(x, y, 1 - c), device_id_type=MESH)

        sends = [copy(a, c) for a in range(n)]
        for cp in sends:
            cp.start()
        for a in range(n):
            copy(a, 1 - c).wait_recv()
        for cp in sends:
            cp.wait_send()

    shapes = [jax.ShapeDtypeStruct(b.shape, b.dtype) for b in bufs]
    return _comm_call(name, body, bufs, shapes, n, {a: a for a in range(n)})


def _all_peers():
    x, y, c = _place()
    peers = []
    for mask in range(1, N_DEV):
        fx, fy, fc = (mask >> 2) & 1, (mask >> 1) & 1, mask & 1
        peers.append((jnp.where(fx, 1 - x, x), jnp.where(fy, 1 - y, y), jnp.where(fc, 1 - c, c)))
    return 4 * x + 2 * y + c, peers


def _reduce_copies(srcs, lands, send_sems, recv_sems):
    me, peers = _all_peers()
    sends, arrivals = [], []
    for a in range(len(srcs)):
        for j, (px, py, pc) in enumerate(peers):
            k = (N_DEV - 1) * a + j
            sends.append(pltpu.make_async_remote_copy(
                src_ref=srcs[a].at[2 * px + py, pc], dst_ref=lands[a].at[me], send_sem=send_sems.at[k],
                recv_sem=recv_sems.at[k], device_id=(px, py, pc), device_id_type=MESH))
            arrivals.append(pltpu.make_async_remote_copy(
                src_ref=srcs[a].at[2 * px + py, pc], dst_ref=lands[a].at[4 * px + 2 * py + pc],
                send_sem=send_sems.at[k], recv_sem=recv_sems.at[k], device_id=(px, py, pc), device_id_type=MESH))
    return sends, arrivals


def _reduce_direct(name, srcs, pin=None):
    n = len(srcs)
    extra = [] if pin is None else [pin]

    def body(*refs):
        ins, outs = refs[:n], refs[n + len(extra):2 * n + len(extra)]
        sends, arrivals = _reduce_copies(ins, outs, refs[-2], refs[-1])
        for cp in sends:
            cp.start()
        for cp in arrivals:
            cp.wait_recv()
        for cp in sends:
            cp.wait_send()

    shapes = [jax.ShapeDtypeStruct((N_DEV,) + s.shape[2:], s.dtype) for s in srcs]
    return _comm_call(name, body, list(srcs) + extra, shapes, (N_DEV - 1) * n)


def _reduce_start(name, srcs):
    n = len(srcs)
    lands = [lax.empty((N_DEV,) + s.shape[2:], s.dtype) for s in srcs]

    def body(*refs):
        sends, _ = _reduce_copies(refs[:n], refs[n:2 * n], refs[2 * n], refs[2 * n + 1])
        for cp in sends:
            cp.start()
        refs[-1][...] = jnp.zeros_like(refs[-1])

    bufs = list(srcs) + lands
    n_sems = (N_DEV - 1) * n
    out_shape = (pltpu.SemaphoreType.DMA((n_sems,)), pltpu.SemaphoreType.DMA((n_sems,)),
                 *[pltpu.HBM(b.shape, b.dtype) for b in bufs], jax.ShapeDtypeStruct((8, LANES), F32))
    return pl.pallas_call(
        body, name=name, out_shape=out_shape, in_specs=[_HBM] * (2 * n),
        out_specs=(_SEM, _SEM, *[_HBM] * (2 * n), pl.BlockSpec(memory_space=pltpu.VMEM)),
        input_output_aliases={a: 2 + a for a in range(2 * n)},
        compiler_params=pltpu.CompilerParams(has_side_effects=_EFFECT),
    )(*[pltpu.with_memory_space_constraint(b, pltpu.HBM) for b in bufs])


def _reduce_wait(name, srcs, lands, send_sems, recv_sems, after):
    n = len(srcs)

    def body(*refs):
        sends, arrivals = _reduce_copies(refs[:n], refs[n:2 * n], refs[2 * n], refs[2 * n + 1])
        for cp in sends:
            cp.wait_send()
        for cp in arrivals:
            cp.wait_recv()

    bufs = list(srcs) + list(lands)
    outs = pl.pallas_call(
        body, name=name, out_shape=tuple(pltpu.HBM(b.shape, b.dtype) for b in bufs),
        in_specs=[_HBM] * (2 * n) + [_SEM, _SEM, _ANY], out_specs=tuple([_HBM] * (2 * n)),
        input_output_aliases={a: a for a in range(2 * n)},
        compiler_params=pltpu.CompilerParams(has_side_effects=_EFFECT),
    )(*bufs, send_sems, recv_sems, after)
    return list(outs[n:])


def _reduce_sum(name, own, land, chip, core):
    n, rh, lanes = land.shape
    tm = _row_tile(rh, 1024, ROW_ALIGN)

    def body(idx_ref, own_ref, *rest):
        total = own_ref[...]
        for g_ref in rest[:-1]:
            total = total + g_ref[...].astype(F32)
        rest[-1][...] = total

    def block(k):
        return pl.BlockSpec((None, tm, lanes), lambda i, idx_ref: ((2 * idx_ref[0] + idx_ref[1] + k) % n, i, 0))

    grid_spec = pltpu.PrefetchScalarGridSpec(
        num_scalar_prefetch=1, grid=(rh // tm,),
        in_specs=[pl.BlockSpec((None, None, tm, lanes), lambda i, idx_ref: (idx_ref[0], idx_ref[1], i, 0))]
        + [block(k) for k in range(1, n)],
        out_specs=pl.BlockSpec((None, tm, lanes), lambda i, idx_ref: (idx_ref[1], i, 0)))
    return pl.pallas_call(
        body, name=name, grid_spec=grid_spec, out_shape=jax.ShapeDtypeStruct((2, rh, lanes), F32),
        compiler_params=_params(("parallel",)),
    )(jnp.stack([chip, core]).astype(jnp.int32), own, *[land] * (n - 1))


def _gather_all_start(name, buf):
    def body(in_ref, send_sems, recv_sems, out_ref, token):
        me, peers = _all_peers()
        for j, peer in enumerate(peers):
            pltpu.make_async_remote_copy(
                src_ref=in_ref.at[me], dst_ref=in_ref.at[me], send_sem=send_sems.at[j], recv_sem=recv_sems.at[j],
                device_id=peer, device_id_type=MESH).start()
        token[...] = jnp.zeros_like(token)

    n = N_DEV - 1
    return pl.pallas_call(
        body, name=name, in_specs=[_HBM],
        out_shape=(pltpu.SemaphoreType.DMA((n,)), pltpu.SemaphoreType.DMA((n,)), pltpu.HBM(buf.shape, buf.dtype),
                   jax.ShapeDtypeStruct((8, LANES), F32)),
        out_specs=(_SEM, _SEM, _HBM, pl.BlockSpec(memory_space=pltpu.VMEM)), input_output_aliases={0: 2},
        compiler_params=pltpu.CompilerParams(has_side_effects=_EFFECT),
    )(pltpu.with_memory_space_constraint(buf, pltpu.HBM))


def _gather_all_wait(name, buf, send_sems, recv_sems, after):
    def body(in_ref, send_sems, recv_sems, after_ref, out_ref):
        me, peers = _all_peers()
        for j, (px, py, pc) in enumerate(peers):
            copy = pltpu.make_async_remote_copy(
                src_ref=in_ref.at[me], dst_ref=in_ref.at[4 * px + 2 * py + pc], send_sem=send_sems.at[j],
                recv_sem=recv_sems.at[j], device_id=(px, py, pc), device_id_type=MESH)
            copy.wait_send()
            copy.wait_recv()

    return pl.pallas_call(
        body, name=name, in_specs=[_HBM, _SEM, _SEM, _ANY], out_shape=pltpu.HBM(buf.shape, buf.dtype),
        out_specs=_HBM, input_output_aliases={0: 0},
        compiler_params=pltpu.CompilerParams(has_side_effects=_EFFECT),
    )(buf, send_sems, recv_sems, after)


def _sum_blocks(name, stacked, tm):
    n, r, lanes = stacked.shape

    def body(in_ref, o_ref):
        acc = in_ref[0]
        for j in range(1, n):
            acc = acc + in_ref[j]
        o_ref[...] = acc

    return pl.pallas_call(
        body, name=name, grid=(r // tm,), in_specs=[pl.BlockSpec((n, tm, lanes), lambda i: (0, i, 0))],
        out_specs=pl.BlockSpec((tm, lanes), lambda i: (i, 0)), out_shape=jax.ShapeDtypeStruct((r, lanes), F32),
        compiler_params=_params(("parallel",)),
    )(stacked)


def _row_tile(rows, pref, align):
    best = None
    for t in range(align, min(rows, pref) + 1, align):
        if rows % t == 0:
            best = t
    assert best is not None, (rows, pref, align)
    return best


def _adam(name, w, g, m, v):
    rows, width = w.shape
    tm = _row_tile(rows, max(8, 4096 * LANES // width), 8)
    args = [(t, width, 0) for t in (w, g, m, v)]
    return _rowcall(name, _adam_fn, args, [], [(width, F32)] * 3, tm=tm)


def kernel(x, norm_mix, norm_mlp, norm_final, mlp_w1, mlp_w2, ab_w_in, ab_w_out, rg_conv_w, rg_conv_b, rg_w_a, rg_b_a, rg_w_x, rg_b_x, rg_lambda, hg_lb_logits, hg_norm, gla_w_in, gla_w_out, gla_w_gate_up, gla_b_gate, gla_norm, loss_target, m_norm_mix, m_norm_mlp, m_norm_final, m_mlp_w1, m_mlp_w2, m_ab_w_in, m_ab_w_out, m_rg_conv_w, m_rg_conv_b, m_rg_w_a, m_rg_b_a, m_rg_w_x, m_rg_b_x, m_rg_lambda, m_hg_lb_logits, m_hg_norm, m_gla_w_in, m_gla_w_out, m_gla_w_gate_up, m_gla_b_gate, m_gla_norm, v_norm_mix, v_norm_mlp, v_norm_final, v_mlp_w1, v_mlp_w2, v_ab_w_in, v_ab_w_out, v_rg_conv_w, v_rg_conv_b, v_rg_w_a, v_rg_b_a, v_rg_w_x, v_rg_b_x, v_rg_lambda, v_hg_lb_logits, v_hg_norm, v_gla_w_in, v_gla_w_out, v_gla_w_gate_up, v_gla_b_gate, v_gla_norm):
    w = dict(norm_mix=norm_mix, norm_mlp=norm_mlp, norm_final=norm_final, mlp_w1=mlp_w1, mlp_w2=mlp_w2, ab_w_in=ab_w_in, ab_w_out=ab_w_out, rg_conv_w=rg_conv_w, rg_conv_b=rg_conv_b, rg_w_a=rg_w_a, rg_b_a=rg_b_a, rg_w_x=rg_w_x, rg_b_x=rg_b_x, rg_lambda=rg_lambda, hg_lb_logits=hg_lb_logits, hg_norm=hg_norm, gla_w_in=gla_w_in, gla_w_out=gla_w_out, gla_w_gate_up=gla_w_gate_up, gla_b_gate=gla_b_gate, gla_norm=gla_norm)
    m = dict(norm_mix=m_norm_mix, norm_mlp=m_norm_mlp, norm_final=m_norm_final, mlp_w1=m_mlp_w1, mlp_w2=m_mlp_w2, ab_w_in=m_ab_w_in, ab_w_out=m_ab_w_out, rg_conv_w=m_rg_conv_w, rg_conv_b=m_rg_conv_b, rg_w_a=m_rg_w_a, rg_b_a=m_rg_b_a, rg_w_x=m_rg_w_x, rg_b_x=m_rg_b_x, rg_lambda=m_rg_lambda, hg_lb_logits=m_hg_lb_logits, hg_norm=m_hg_norm, gla_w_in=m_gla_w_in, gla_w_out=m_gla_w_out, gla_w_gate_up=m_gla_w_gate_up, gla_b_gate=m_gla_b_gate, gla_norm=m_gla_norm)
    v = dict(norm_mix=v_norm_mix, norm_mlp=v_norm_mlp, norm_final=v_norm_final, mlp_w1=v_mlp_w1, mlp_w2=v_mlp_w2, ab_w_in=v_ab_w_in, ab_w_out=v_ab_w_out, rg_conv_w=v_rg_conv_w, rg_conv_b=v_rg_conv_b, rg_w_a=v_rg_w_a, rg_b_a=v_rg_b_a, rg_w_x=v_rg_w_x, rg_b_x=v_rg_b_x, rg_lambda=v_rg_lambda, hg_lb_logits=v_hg_lb_logits, hg_norm=v_hg_norm, gla_w_in=v_gla_w_in, gla_w_out=v_gla_w_out, gla_w_gate_up=v_gla_w_gate_up, gla_b_gate=v_gla_b_gate, gla_norm=v_gla_norm)
    chip = 2 * lax.axis_index("x") + lax.axis_index("y")
    core = lax.axis_index("c")
    sharded_shapes = [w[n].shape for n in SMALL_SHARDED]

    slots = [_into_slot(f"cast_{n}{layer}", w[n], chip, N_CHIPS, BF16, 512, layer) for n, layer in MATRICES]
    early = [i for i, (n, _) in enumerate(MATRICES) if n in EARLY_MATRICES]
    rest = [i for i in range(len(MATRICES)) if i not in early]

    def named(indices, arrays):
        big = {}
        for i, t in zip(indices, arrays):
            big.setdefault(MATRICES[i][0], []).append(t)
        return {n: (v if n in ("mlp_w1", "mlp_w2") else v[0]) for n, v in big.items()}

    vectors = _pack([w[n] for n in SMALL_SHARDED])
    vectors = _into_slot("place_vectors", vectors, chip, N_CHIPS, F32, vectors.shape[0])
    *gathered, vectors = _gather_chips("gather_early", [slots[i] for i in early] + [vectors])
    send_sems, recv_sems, *in_flight, token = _gather_start("gather_rest_start", [slots[i] for i in rest], gathered[0])

    def late_weights(after):
        landed = _gather_wait("gather_rest_wait", in_flight, send_sems, recv_sems, after)
        return _prepare_matrices(named(rest, _hand_over("gather_rest_share", list(landed))))

    big = named(early, gathered)
    small_all = _unpack(vectors, sharded_shapes, lead=1)
    full = {n: w[n] for n in SMALL_REPLICATED}
    for n, t in zip(SMALL_SHARDED, small_all):
        full[n] = _join_chips(t, t.ndim - 2)

    def halves(t):
        return t.reshape(N_CHIPS, 2, t.shape[1] // 2, t.shape[2])

    in_flight_grads = {}

    def emit(tag, arrays32, arrays16):
        n = len(arrays16)
        send, recv, *rest = _reduce_start(f"reduce_{tag}_start", [halves(t) for t in arrays16])
        in_flight_grads[tag] = ([halves(t) for t in arrays32], rest[:n], rest[n:2 * n], send, recv)
        return rest[-1]

    loss_part, grad_x, g_kernel = _local_step(
        x[0], loss_target[0], _prepare_weights(big, full), token, late_weights, emit)
    g_big, g_full = _finish_grads(g_kernel)

    small_names = SMALL_REPLICATED + SMALL_SHARDED
    reduced_shapes = [g_full[n].shape for n in small_names] + [loss_part.shape]
    g_small = _pack([g_full[n] for n in small_names] + [loss_part])
    device = 2 * chip + core
    g_small = _into_slot("place_small", g_small, device, N_DEV, F32, g_small.shape[0])
    small_send, small_recv, small_in_flight, small_token = _gather_all_start("reduce_small_start", g_small)

    mine = {}
    for tag, (own, srcs, lands, send, recv) in in_flight_grads.items():
        landed = _reduce_wait(f"reduce_{tag}_wait", srcs, lands, send, recv, small_token)
        mine[tag] = [_reduce_sum(f"reduce_add_{tag}{i}", o, f, chip, core) for i, (o, f) in enumerate(zip(own, landed))]
    ordered = [mine["mlp0"][0], mine["mlp1"][0], mine["mlp0"][1], mine["mlp1"][1], mine["ab"][0], mine["mlp0"][2],
               *mine["gla"]]
    reduced = [t.reshape(2 * t.shape[1], t.shape[2]) for t in _pair_gather("reduce_share", ordered)]
    by_name = {n: [] for n, _ in MATRICES}
    for (n, _), t in zip(MATRICES, reduced):
        by_name[n].append(t)
    grads = {n: jnp.stack(v) for n, v in by_name.items()}

    g_small_all = _gather_all_wait("reduce_small_wait", small_in_flight, small_send, small_recv, reduced[0])
    g_small_red = _sum_blocks("reduce_small_add", g_small_all, g_small_all.shape[1])
    *small_red, loss_sum = _unpack(g_small_red, reduced_shapes)
    loss = loss_sum[0, 0]
    g_small_full = dict(zip(small_names, small_red))
    for n in SMALL_REPLICATED:
        grads[n] = g_small_full[n]
    for n in SMALL_SHARDED:
        width = w[n].shape[-1]
        grads[n] = lax.dynamic_slice_in_dim(g_small_full[n], chip * width, width, axis=g_small_full[n].ndim - 1)

    delta, new_m, new_v = {}, {}, {}
    for n in by_name:
        flat = [t.reshape(-1, t.shape[-1]) for t in (w[n], grads[n], m[n], v[n])]
        for dst, t in zip((delta, new_m, new_v), _adam(f"adam_{n}", *flat)):
            dst[n] = t.reshape(w[n].shape)
    small_shapes = [w[n].shape for n in small_names]
    packs = [_pack([src[n] for n in small_names]) for src in (w, grads, m, v)]
    d_small, m_small, v_small = _adam("adam_small", *packs)
    for dst, buf in ((delta, d_small), (new_m, m_small), (new_v, v_small)):
        dst.update(zip(small_names, _unpack(buf, small_shapes)))

    return (loss, grad_x[None], *[grads[n] for n in WEIGHTS], *[delta[n] for n in WEIGHTS],
            *[new_m[n] for n in WEIGHTS], *[new_v[n] for n in WEIGHTS])
```

```python
import functools

import jax
import jax.numpy as jnp
from jax import lax
from jax.experimental import pallas as pl
from jax.experimental.pallas import tpu as pltpu

F32 = jnp.float32
BF16 = jnp.bfloat16
MESH = pl.DeviceIdType.MESH

LANES = 128
CHUNK = 64
ATTN_SUB = 4
EPS = 1e-6
RG_C = 8.0
N_CHIPS = 4
N_DEV = 8
GLA_IN_WIDTH = 3104
GLA_IN_PAD = 3200
VMEM_LIMIT = 56 * 1024 * 1024

ADAM_LR = 0.001
ADAM_B1 = 0.9
ADAM_B2 = 0.999
ADAM_EPS = 1e-08
ADAM_WD = 0.01
ADAM_STEP = 10


def _raw_dot(a, b, ca, cb):
    return lax.dot_general(a.astype(BF16), b.astype(BF16), (((ca,), (cb,)), ((), ())),
                           preferred_element_type=F32)


def _raw_nn(a, b):
    return _raw_dot(a, b, 1, 0)


def _raw_nt(a, b):
    return _raw_dot(a, b, 1, 1)


def _raw_tn(a, b):
    return _raw_dot(a, b, 0, 0)


@jax.custom_vjp
def _dot_nn(a, b):
    return _raw_nn(a, b)


def _dot_nn_fwd(a, b):
    return _raw_nn(a, b), (a, b)


def _dot_nn_bwd(res, g):
    a, b = res
    return _raw_nt(g, b), _raw_tn(a, g)


_dot_nn.defvjp(_dot_nn_fwd, _dot_nn_bwd)


@jax.custom_vjp
def _dot_nt(a, b):
    return _raw_nt(a, b)


def _dot_nt_fwd(a, b):
    return _raw_nt(a, b), (a, b)


def _dot_nt_bwd(res, g):
    a, b = res
    return _raw_nn(g, b), _raw_tn(g, a)


_dot_nt.defvjp(_dot_nt_fwd, _dot_nt_bwd)


@jax.custom_vjp
def _dot_tn(a, b):
    return _raw_tn(a, b)


def _dot_tn_fwd(a, b):
    return _raw_tn(a, b), (a, b)


def _dot_tn_bwd(res, g):
    a, b = res
    return _raw_nt(b, g), _raw_nn(a, g)


_dot_tn.defvjp(_dot_tn_fwd, _dot_tn_bwd)


def _tile(n, pref):
    if n <= pref:
        return n
    t = (pref // LANES) * LANES
    while t > LANES and n % t:
        t -= LANES
    assert n % t == 0, (n, pref)
    return t


def _params(sem):
    return pltpu.CompilerParams(dimension_semantics=sem, vmem_limit_bytes=VMEM_LIMIT)


def _rowcall(name, fn, rows, pars, row_outs, par_outs=(), tm=256, pin=None):
    if pin is not None:
        inner, pars = fn, list(pars) + [pin]
        fn = lambda *vals: inner(*vals[:-1])
    n_rows = rows[0][0].shape[0]
    tm = min(tm, n_rows)
    assert n_rows % tm == 0
    n_r, n_p, n_ro = len(rows), len(pars), len(row_outs)

    def body(*refs):
        vals = [r[...].astype(F32) for r in refs[:n_r + n_p]]
        outs = fn(*vals)
        o_refs = refs[n_r + n_p:n_r + n_p + n_ro]
        po_refs = refs[n_r + n_p + n_ro:]
        for o_ref, val in zip(o_refs, outs[:n_ro]):
            o_ref[...] = val.astype(o_ref.dtype)
        first = pl.program_id(0) == 0
        for po_ref, val in zip(po_refs, outs[n_ro:]):
            @pl.when(first)
            def _():
                po_ref[...] = val

            @pl.when(jnp.logical_not(first))
            def _():
                po_ref[...] += val

    def const_map(nd):
        return lambda i: (0,) * nd

    def row_spec(w, cb):
        return pl.BlockSpec((tm, w), lambda i: (i, cb))

    in_specs = [row_spec(w, cb) for _, w, cb in rows]
    in_specs += [pl.BlockSpec(p.shape, const_map(p.ndim)) for p in pars]
    out_specs = [pl.BlockSpec((tm, w), lambda i: (i, 0)) for w, _ in row_outs]
    out_specs += [pl.BlockSpec(tuple(s), const_map(len(s))) for s in par_outs]
    out_shape = [jax.ShapeDtypeStruct((n_rows, w), dt) for w, dt in row_outs]
    out_shape += [jax.ShapeDtypeStruct(tuple(s), F32) for s in par_outs]
    return pl.pallas_call(
        body, name=name, grid=(n_rows // tm,), in_specs=in_specs, out_specs=out_specs, out_shape=out_shape,
        compiler_params=_params(("arbitrary",) if par_outs else ("parallel",)),
    )(*[r[0] for r in rows], *pars)


def _vjp_of(fn, n_prim, n_out, n_par, n_pass=0):
    def bwd(*args):
        prim = args[:n_prim]
        cts = args[n_prim:n_prim + n_out]
        passes = args[n_prim + n_out:n_prim + n_out + 2 * n_pass]
        pars = args[n_prim + n_out + 2 * n_pass:]
        _, vjp = jax.vjp(fn, *prim, *pars)
        grads = vjp(tuple(cts))
        sums = tuple(passes[2 * i] + passes[2 * i + 1] for i in range(n_pass))
        return tuple(grads[:n_prim]) + sums + tuple(grads[n_prim:])
    return bwd


def _mm(name, a, b, mode="nn", extras=(), epi=None, out_dtypes=(F32,), a_pro=None, out_split=None,
        epi_pars=(), row_sum=False, pin=None, tm=1024, tn=1024, tk=1024):
    split = b.shape[0] if b.ndim == 3 else None
    b_rows, b_cols = b.shape[-2:]
    if mode == "nn":
        (m, k), n = a.shape, b_cols * (split or 1)
    elif mode == "nt":
        (m, k), n = a.shape, b_rows
        assert k == b_cols * (split or 1)
    else:
        assert split is None
        (k, m), n = a.shape, b_cols
    tm, tk = _tile(m, tm), _tile(k, tk)
    tn = _tile(n // out_split, tn) if out_split else _tile(n, tn)
    if split and mode == "nn":
        tn = _tile(b_cols, tn)
    if split and mode == "nt":
        tk = _tile(b_cols, tk)
    nk = k // tk
    raw = {"nn": _raw_nn, "nt": _raw_nt, "tn": _raw_tn}[mode]
    n_e, n_p, n_o = len(extras), len(epi_pars), len(out_dtypes)
    n_in = n_e + n_p + (0 if pin is None else 1)
    if epi is None:
        epi = lambda acc: (acc,)

    def body(a_ref, b_ref, *rest):
        e_refs, p_refs, o_refs = rest[:n_e], rest[n_e:n_e + n_p], rest[n_in:n_in + n_o]
        kk = pl.program_id(2)
        a_tile = a_ref[...] if a_pro is None else a_pro(a_ref[...].astype(F32))
        part = raw(a_tile, b_ref[...])

        def finish(total):
            res = epi(total, *[e[...].astype(F32) for e in e_refs], *[p[...] for p in p_refs])
            for o_ref, r in zip(o_refs, res):
                o_ref[...] = r.astype(o_ref.dtype)
            if row_sum:
                rest[n_in + n_o][...] = res[n_o]

        if nk == 1:
            finish(part)
            return
        acc = rest[-1]

        @pl.when(kk == 0)
        def _():
            acc[...] = part

        @pl.when((kk > 0) & (kk < nk - 1))
        def _():
            acc[...] += part

        @pl.when(kk == nk - 1)
        def _():
            finish(acc[...] + part)

    a_spec = pl.BlockSpec((tk, tm), lambda i, j, kk: (kk, i)) if mode == "tn" else pl.BlockSpec((tm, tk), lambda i, j, kk: (i, kk))
    if split and mode == "nn":
        per = b_cols // tn
        b_spec = pl.BlockSpec((None, tk, tn), lambda i, j, kk: (j // per, kk, j % per))
    elif split:
        per = b_cols // tk
        b_spec = pl.BlockSpec((None, tn, tk), lambda i, j, kk: (kk // per, j, kk % per))
    elif mode == "nt":
        b_spec = pl.BlockSpec((tn, tk), lambda i, j, kk: (j, kk))
    else:
        b_spec = pl.BlockSpec((tk, tn), lambda i, j, kk: (kk, j))
    mn_spec = pl.BlockSpec((tm, tn), lambda i, j, kk: (i, j))
    if out_split:
        assert not extras
        per_out = n // out_split // tn
        out_spec = pl.BlockSpec((None, tm, tn), lambda i, j, kk: (j // per_out, i, j % per_out))
        out_shapes = [jax.ShapeDtypeStruct((out_split, m, n // out_split), dt) for dt in out_dtypes]
    else:
        out_spec = mn_spec
        out_shapes = [jax.ShapeDtypeStruct((m, n), dt) for dt in out_dtypes]
    out_specs = [out_spec] * n_o
    if row_sum:
        out_specs.append(pl.BlockSpec((None, 1, tn), lambda i, j, kk: (i, 0, j)))
        out_shapes.append(jax.ShapeDtypeStruct((m // tm, 1, n), F32))
    in_specs = [a_spec, b_spec] + [mn_spec] * n_e
    in_specs += [pl.BlockSpec(p.shape, functools.partial(lambda i, j, kk, nd: (0,) * nd, nd=p.ndim)) for p in epi_pars]
    in_specs += [] if pin is None else [pl.BlockSpec(memory_space=pl.ANY)]
    outs = pl.pallas_call(
        body, name=name, grid=(m // tm, n // tn, nk), in_specs=in_specs, out_specs=out_specs, out_shape=out_shapes,
        scratch_shapes=[pltpu.VMEM((tm, tn), F32)] if nk > 1 else [],
        compiler_params=_params(("parallel", "parallel", "arbitrary")),
    )(a, b, *extras, *epi_pars, *([] if pin is None else [pin]))
    return outs[0] if len(outs) == 1 else outs


def _sigmoid(x):
    return jax.nn.sigmoid(x)


def _silu(x):
    return x * _sigmoid(x)


def _softplus(x):
    return jnp.maximum(x, 0.0) + jnp.log1p(jnp.exp(-jnp.abs(x)))


def _rmsnorm_fn(x, gain):
    return (x * lax.rsqrt(jnp.mean(x * x, axis=-1, keepdims=True) + EPS) * gain,)


def _head_norm(o, gain, n_heads):
    w = o.shape[-1] // n_heads
    parts = []
    for h in range(n_heads):
        oh = o[:, h * w:(h + 1) * w]
        parts.append(oh * lax.rsqrt(jnp.mean(oh * oh, axis=-1, keepdims=True) + EPS))
    return jnp.concatenate(parts, axis=-1) * gain


@jax.custom_jvp
def _neg_expm1(x):
    u = jnp.exp(x)
    is_one = u == 1.0
    return jnp.where(is_one, -x, (1.0 - u) * x / jnp.log(jnp.where(is_one, 2.0, u)))


@_neg_expm1.defjvp
def _neg_expm1_jvp(primals, tangents):
    (x,), (t,) = primals, tangents
    return _neg_expm1(x), -jnp.exp(x) * t


def _rg_gates_fn(xc, wa, wx, ba, bx, lam):
    outs = []
    for d in range(2):
        r = _sigmoid(_dot_nn(xc, wa[d]) + ba[d:d + 1])
        i = _sigmoid(_dot_nn(xc, wx[d]) + bx[d:d + 1])
        log_a = -RG_C * r * _softplus(-lam[d:d + 1])
        outs.append(jnp.exp(log_a))
        outs.append(jnp.sqrt(_neg_expm1(2.0 * log_a)) * (i * xc))
    return tuple(outs)


def _hg_pre_fn(q, f_f, f_b, logits):
    mx = jnp.maximum(logits[0:1], logits[1:2])
    e0 = jnp.exp(logits[0:1] - mx)
    e1 = jnp.exp(logits[1:2] - mx)
    lb = e0 / (e0 + e1)
    outs = [_silu(q)]
    for f in (f_f, f_b):
        outs.append((1.0 - lb) * _sigmoid(-f))
        outs.append(jnp.log(lb + (1.0 - lb) * _sigmoid(f)))
    return tuple(outs)


def _post0_fn(hs, ga, o, g, gain):
    ya = hs * jax.nn.gelu(ga, approximate=True)
    yb = _head_norm(o, gain, 4) * _silu(g)
    return (jnp.concatenate([ya, yb], axis=-1),)


def _post0_fwd_fn(h_f, h_b, ga, o_f, o_b, g, gain):
    return _post0_fn(h_f + h_b, ga, o_f + o_b, g, gain)


def _post0_bwd_fn(h_f, h_b, ga, o_f, o_b, g, dmix, gain):
    _, vjp = jax.vjp(_post0_fn, h_f + h_b, ga, o_f + o_b, g, gain)
    return vjp((dmix,))


def _gla_pre_fn(q, lr, w_up, b_gate):
    outs = [q * (128.0 ** -0.5)]
    for d in range(2):
        z = _dot_nn(lr, w_up[d]) + b_gate[d:d + 1]
        outs.append(-_softplus(-z) * (1.0 / 16.0))
    return tuple(outs)


def _gla_post_fn(o, r, gain):
    return (_head_norm(o, gain, 4) * _silu(r),)


def _gla_post_fwd_fn(o_f, o_b, r, gain):
    return _gla_post_fn(o_f + o_b, r, gain)


def _gla_post_bwd_fn(o_f, o_b, r, dmix, gain):
    _, vjp = jax.vjp(_gla_post_fn, o_f + o_b, r, gain)
    return vjp((dmix,))


def _relu2_bwd_epi(acc, hid):
    return (acc * 2.0 * jnp.maximum(hid, 0.0),)


def _relu2(x):
    r = jnp.maximum(x, 0.0)
    return r * r


def _add_epi(acc, res):
    return (acc + res,)


def _loss_head_fn(h, target, gain):
    def f(h, gain):
        y = _rmsnorm_fn(h, gain)[0]
        err = y - target
        return 0.5 * jnp.sum(jnp.mean(err * err, axis=-1, keepdims=True))
    loss, (dh, dgain) = jax.value_and_grad(f, argnums=(0, 1))(h, gain)
    return dh, jnp.full((1, LANES), loss, F32), dgain


def _adam_fn(w, g, m, v):
    m2 = ADAM_B1 * m + (1.0 - ADAM_B1) * g
    v2 = ADAM_B2 * v + (1.0 - ADAM_B2) * (g * g)
    m_hat = m2 / (1.0 - ADAM_B1 ** ADAM_STEP)
    v_hat = v2 / (1.0 - ADAM_B2 ** ADAM_STEP)
    delta = -ADAM_LR * (m_hat / (jnp.sqrt(v_hat) + ADAM_EPS) + ADAM_WD * w)
    return delta, m2, v2


def _shifted(x, t_idx, off):
    n = x.shape[0]
    rolled = pltpu.roll(x, (-off) % n, 0)
    valid = (t_idx + off >= 0) & (t_idx + off < n)
    return jnp.where(valid, rolled, 0.0)


def _conv_fwd(name, src, colblock, w, b):
    n_rows, width = src.shape[0], w.shape[1]

    def body(x_ref, w_ref, b_ref, o_ref):
        x = x_ref[...]
        t_idx = lax.broadcasted_iota(jnp.int32, x.shape, 0)
        acc = b_ref[...] + w_ref[2:3, :] * x
        acc += w_ref[0:1, :] * _shifted(x, t_idx, -2)
        acc += w_ref[1:2, :] * _shifted(x, t_idx, -1)
        acc += w_ref[3:4, :] * _shifted(x, t_idx, 1)
        o_ref[...] = acc

    nb = width // LANES
    return pl.pallas_call(
        body, name=name, grid=(nb,),
        in_specs=[pl.BlockSpec((n_rows, LANES), lambda j: (0, colblock * nb + j)),
                  pl.BlockSpec((4, LANES), lambda j: (0, j)), pl.BlockSpec((1, LANES), lambda j: (0, j))],
        out_specs=pl.BlockSpec((n_rows, LANES), lambda j: (0, j)),
        out_shape=jax.ShapeDtypeStruct((n_rows, width), F32),
        compiler_params=_params(("parallel",)),
    )(src, w, b)


def _conv_bwd(name, src, colblock, w, d):
    n_rows, width = src.shape[0], w.shape[1]

    def body(x_ref, w_ref, d_ref, dx_ref, dw_ref, db_ref):
        x = x_ref[...]
        g = d_ref[...]
        t_idx = lax.broadcasted_iota(jnp.int32, x.shape, 0)
        dx = w_ref[2:3, :] * g
        dx += w_ref[0:1, :] * _shifted(g, t_idx, 2)
        dx += w_ref[1:2, :] * _shifted(g, t_idx, 1)
        dx += w_ref[3:4, :] * _shifted(g, t_idx, -1)
        dx_ref[...] = dx.astype(dx_ref.dtype)
        dw_ref[0:1, :] = jnp.sum(g * _shifted(x, t_idx, -2), axis=0, keepdims=True)
        dw_ref[1:2, :] = jnp.sum(g * _shifted(x, t_idx, -1), axis=0, keepdims=True)
        dw_ref[2:3, :] = jnp.sum(g * x, axis=0, keepdims=True)
        dw_ref[3:4, :] = jnp.sum(g * _shifted(x, t_idx, 1), axis=0, keepdims=True)
        db_ref[...] = jnp.sum(g, axis=0, keepdims=True)

    nb = width // LANES
    return pl.pallas_call(
        body, name=name, grid=(nb,),
        in_specs=[pl.BlockSpec((n_rows, LANES), lambda j: (0, colblock * nb + j)),
                  pl.BlockSpec((4, LANES), lambda j: (0, j)),
                  pl.BlockSpec((n_rows, LANES), lambda j: (0, j))],
        out_specs=[pl.BlockSpec((n_rows, LANES), lambda j: (0, j)), pl.BlockSpec((4, LANES), lambda j: (0, j)),
                   pl.BlockSpec((1, LANES), lambda j: (0, j))],
        out_shape=[jax.ShapeDtypeStruct((n_rows, width), BF16), jax.ShapeDtypeStruct((4, width), F32),
                   jax.ShapeDtypeStruct((1, width), F32)],
        compiler_params=_params(("parallel",)),
    )(src, w, d)


SUBLANES = 8
SCAN_UNROLL = 8


def _shift_rows(x, d, fill):
    n = x.shape[0]
    t = lax.broadcasted_iota(jnp.int32, x.shape, 0)
    valid = (t >= d) if d > 0 else (t < n + d)
    return jnp.where(valid, pltpu.roll(x, d % n, 0), fill)


def _tile_scan(a, u, reverse):
    d = 1
    while d < a.shape[0]:
        s = -d if reverse else d
        a_sh, u_sh = _shift_rows(a, s, 1.0), _shift_rows(u, s, 0.0)
        u = u + a * u_sh
        a = a * a_sh
        d *= 2
    return a, u


def _edge_row(x, reverse):
    return x[0:1, :] if reverse else x[SUBLANES - 1:SUBLANES, :]


def _scan_specs(n_rows, n):
    return [pl.BlockSpec((n_rows, LANES), lambda j: (0, j))] * n


def _scan_tile(a_ref, u_ref, h_ref, i, carry, reverse):
    n_tiles = a_ref.shape[0] // SUBLANES
    tile = (n_tiles - 1 - i) if reverse else i
    rows = pl.ds(pl.multiple_of(tile * SUBLANES, SUBLANES), SUBLANES)
    acc_a, acc_u = _tile_scan(a_ref[rows, :], u_ref[rows, :], reverse)
    h = acc_u + acc_a * carry
    h_ref[rows, :] = h
    return _edge_row(h, reverse)


def _scan_fwd(name, a_f, u_f, a_b, u_b):
    n_rows, width = a_f.shape

    def body(af_ref, uf_ref, ab_ref, ub_ref, hf_ref, hb_ref):
        def step(i, carry):
            return (_scan_tile(af_ref, uf_ref, hf_ref, i, carry[0], False),
                    _scan_tile(ab_ref, ub_ref, hb_ref, i, carry[1], True))
        zero = jnp.zeros((1, LANES), F32)
        lax.fori_loop(0, n_rows // SUBLANES, step, (zero, zero), unroll=SCAN_UNROLL)

    return pl.pallas_call(
        body, name=name, grid=(width // LANES,), in_specs=_scan_specs(n_rows, 4), out_specs=_scan_specs(n_rows, 2),
        out_shape=[jax.ShapeDtypeStruct((n_rows, width), F32)] * 2, compiler_params=_params(("parallel",)),
    )(a_f, u_f, a_b, u_b)


def _scan_bwd_tile(a_ref, h_ref, dh_ref, du_ref, da_ref, i, carry, reverse):
    n_rows = a_ref.shape[0]
    n_tiles = n_rows // SUBLANES
    against = not reverse
    one = -1 if against else 1
    g_in, a_edge = carry
    tile = (n_tiles - 1 - i) if against else i
    start = pl.multiple_of(tile * SUBLANES, SUBLANES)
    rows = pl.ds(start, SUBLANES)
    a_tile = a_ref[rows, :]
    coeff = _shift_rows(a_tile, one, a_edge)
    acc_a, acc_u = _tile_scan(coeff, dh_ref[rows, :], against)
    g = acc_u + acc_a * g_in
    du_ref[rows, :] = g
    outside = (start + SUBLANES) if reverse else (start - 1)
    inside = (outside >= 0) & (outside < n_rows)
    h_edge = jnp.where(inside, h_ref[pl.ds(jnp.clip(outside, 0, n_rows - 1), 1), :], 0.0)
    da_ref[rows, :] = g * _shift_rows(h_ref[rows, :], -one, h_edge)
    return _edge_row(g, against), _edge_row(a_tile, against)


def _scan_bwd(name, a_f, h_f, a_b, h_b, dh):
    n_rows, width = a_f.shape

    def body(af_ref, hf_ref, ab_ref, hb_ref, dh_ref, duf_ref, daf_ref, dub_ref, dab_ref):
        def step(i, carry):
            return (_scan_bwd_tile(af_ref, hf_ref, dh_ref, duf_ref, daf_ref, i, carry[0], False),
                    _scan_bwd_tile(ab_ref, hb_ref, dh_ref, dub_ref, dab_ref, i, carry[1], True))
        zero = jnp.zeros((1, LANES), F32)
        lax.fori_loop(0, n_rows // SUBLANES, step, ((zero, zero), (zero, zero)), unroll=SCAN_UNROLL)

    return pl.pallas_call(
        body, name=name, grid=(width // LANES,), in_specs=_scan_specs(n_rows, 5), out_specs=_scan_specs(n_rows, 4),
        out_shape=[jax.ShapeDtypeStruct((n_rows, width), F32)] * 4, compiler_params=_params(("parallel",)),
    )(a_f, h_f, a_b, h_b, dh)


def _tri_mask(c, reverse):
    row = lax.broadcasted_iota(jnp.int32, (c, c), 0)
    col = lax.broadcasted_iota(jnp.int32, (c, c), 1)
    return (col >= row) if reverse else (col <= row)


def _cumsum_rows(x, reverse):
    tri = _tri_mask(x.shape[0], reverse).astype(BF16)
    hi = x.astype(BF16)
    rest = x - hi.astype(F32)
    mid = rest.astype(BF16)
    lo = (rest - mid.astype(F32)).astype(BF16)
    return _raw_nn(tri, hi) + _raw_nn(tri, mid) + _raw_nn(tri, lo)


@functools.partial(jax.custom_vjp, nondiff_argnums=(1,))
def _cumsum(x, reverse):
    return _cumsum_rows(x, reverse)


def _cumsum_fwd(x, reverse):
    return _cumsum_rows(x, reverse), None


def _cumsum_bwd(reverse, _, g):
    return (_cumsum_rows(g, not reverse),)


_cumsum.defvjp(_cumsum_fwd, _cumsum_bwd)


def _chunks_fn(qs, ks, vs, lfs, sts, reverses):
    n, c = len(qs), qs[0].shape[0]
    every = range(n)
    tris = [_tri_mask(c, r) for r in reverses]
    cums = [_cumsum(lfs[i], reverses[i]) for i in every]
    rid = lax.broadcasted_iota(jnp.int32, cums[0].shape, 0)

    def pick(cum, r):
        return jnp.sum(jnp.where(rid == r, cum, 0.0), axis=0, keepdims=True)

    refs = [pick(cums[i], (c - 1 - c // 2) if reverses[i] else c // 2) for i in every]
    lasts = [pick(cums[i], 0 if reverses[i] else c - 1) for i in every]
    q_in = [qs[i] * jnp.exp(cums[i] - refs[i]) for i in every]
    k_in = [ks[i] * jnp.exp(refs[i] - cums[i]) for i in every]
    scores = [jnp.where(tris[i], _dot_nt(q_in[i], k_in[i]), 0.0) for i in every]
    o_intra = [_dot_nn(scores[i], vs[i]) for i in every]
    q_out = [qs[i] * jnp.exp(cums[i]) for i in every]
    o_inter = [_dot_nt(q_out[i], sts[i]) for i in every]
    k_state = [ks[i] * jnp.exp(lasts[i] - cums[i]) for i in every]
    upd = [_dot_tn(vs[i], k_state[i]) for i in every]
    st_new = [sts[i] * jnp.exp(lasts[i]) + upd[i] for i in every]
    return [o_intra[i] + o_inter[i] for i in every], st_new


def _attn_fwd(name, q, k_f, k_b, v, lf_f, lf_b, n_heads, dk, dv):
    n_rows = q[0].shape[0]
    n_chunks = n_rows // CHUNK
    n_steps = n_chunks // ATTN_SUB
    wk, wv = n_heads * dk, n_heads * dv

    def spec(width, off, rev):
        return pl.BlockSpec((CHUNK * ATTN_SUB, width), lambda n: ((n_steps - 1 - n) if rev else n, off))

    def sspec(rev):
        return pl.BlockSpec((ATTN_SUB, n_heads, dv, dk), lambda n: ((n_steps - 1 - n) if rev else n, 0, 0, 0))

    def body(qf, kf, vf, lff, qb, kb, vb, lfb, of_ref, ob_ref, sf_ref, sb_ref, st):
        @pl.when(pl.program_id(0) == 0)
        def _():
            st[...] = jnp.zeros_like(st)

        ins = ((qf, kf, vf, lff), (qb, kb, vb, lfb))
        chains = [(d, h) for d in range(2) for h in range(n_heads)]
        ck = [slice(h * dk, (h + 1) * dk) for h in range(n_heads)]
        cv = [slice(h * dv, (h + 1) * dv) for h in range(n_heads)]
        sts = [st[d, h] for d, h in chains]
        done = []
        for sub in range(ATTN_SUB):
            local = (sub, ATTN_SUB - 1 - sub)
            rows = [slice(local[d] * CHUNK, (local[d] + 1) * CHUNK) for d in range(2)]
            qs = [ins[d][0][rows[d], ck[h]] for d, h in chains]
            ks = [ins[d][1][rows[d], ck[h]] for d, h in chains]
            vs = [ins[d][2][rows[d], cv[h]] for d, h in chains]
            lfs = [ins[d][3][rows[d], ck[h]] for d, h in chains]
            os_, st_new = _chunks_fn(qs, ks, vs, lfs, sts, [d == 1 for d, _ in chains])
            done.append((local, rows, sts, os_))
            sts = st_new
        for local, rows, entered, os_ in done:
            for i, (d, h) in enumerate(chains):
                (sf_ref, sb_ref)[d][local[d], h] = entered[i].astype(BF16)
                (of_ref, ob_ref)[d][rows[d], cv[h]] = os_[i]
        for i, (d, h) in enumerate(chains):
            st[d, h] = sts[i]

    in_specs = [spec(wk, q[1], False), spec(wk, k_f[1], False), spec(wv, v[1], False), spec(wk, lf_f[1], False),
                spec(wk, q[1], True), spec(wk, k_b[1], True), spec(wv, v[1], True), spec(wk, lf_b[1], True)]
    return pl.pallas_call(
        body, name=name, grid=(n_steps,), in_specs=in_specs,
        out_specs=[spec(wv, 0, False), spec(wv, 0, True), sspec(False), sspec(True)],
        out_shape=[jax.ShapeDtypeStruct((n_rows, wv), F32)] * 2
        + [jax.ShapeDtypeStruct((n_chunks, n_heads, dv, dk), BF16)] * 2,
        scratch_shapes=[pltpu.VMEM((2, n_heads, dv, dk), F32)],
        compiler_params=_params(("arbitrary",)),
    )(q[0], k_f[0], v[0], lf_f[0], q[0], k_b[0], v[0], lf_b[0])


def _attn_bwd(name, q, k_f, k_b, v, lf_f, lf_b, st_f, st_b, do, n_heads, dk, dv, out_dtype=F32):
    n_rows = q[0].shape[0]
    n_chunks = n_rows // CHUNK
    n_steps = n_chunks // ATTN_SUB
    wk, wv = n_heads * dk, n_heads * dv

    def spec(width, off, rev):
        return pl.BlockSpec((CHUNK * ATTN_SUB, width), lambda n: (n if rev else (n_steps - 1 - n), off))

    def sspec(rev):
        return pl.BlockSpec((ATTN_SUB, n_heads, dv, dk), lambda n: (n if rev else (n_steps - 1 - n), 0, 0, 0))

    def body(qf, kf, vf, lff, sf, dof, qb, kb, vb, lfb, sb, dob,
             dqf, dkf, dvf, dlff, dqb, dkb, dvb, dlfb, dst):
        @pl.when(pl.program_id(0) == 0)
        def _():
            dst[...] = jnp.zeros_like(dst)

        ins = ((qf, kf, vf, lff, sf, dof), (qb, kb, vb, lfb, sb, dob))
        outs = ((dqf, dkf, dvf, dlff), (dqb, dkb, dvb, dlfb))
        chains = [(d, h) for d in range(2) for h in range(n_heads)]
        ck = [slice(h * dk, (h + 1) * dk) for h in range(n_heads)]
        cv = [slice(h * dv, (h + 1) * dv) for h in range(n_heads)]
        fn = functools.partial(_chunks_fn, reverses=[d == 1 for d, _ in chains])
        dsts = [dst[d, h] for d, h in chains]
        done = []
        for sub in range(ATTN_SUB):
            local = (ATTN_SUB - 1 - sub, sub)
            rows = [slice(local[d] * CHUNK, (local[d] + 1) * CHUNK) for d in range(2)]
            qs = [ins[d][0][rows[d], ck[h]] for d, h in chains]
            ks = [ins[d][1][rows[d], ck[h]] for d, h in chains]
            vs = [ins[d][2][rows[d], cv[h]] for d, h in chains]
            lfs = [ins[d][3][rows[d], ck[h]] for d, h in chains]
            sts = [ins[d][4][local[d], h].astype(F32) for d, h in chains]
            dos = [ins[d][5][rows[d], cv[h]] for d, h in chains]
            _, vjp = jax.vjp(fn, qs, ks, vs, lfs, sts)
            dqs, dks, dvs, dlfs, dsts = vjp((dos, dsts))
            done.append((rows, dqs, dks, dvs, dlfs))
        for rows, dqs, dks, dvs, dlfs in done:
            for i, (d, h) in enumerate(chains):
                dq_r, dk_r, dv_r, dlf_r = outs[d]
                dq_r[rows[d], ck[h]] = dqs[i].astype(dq_r.dtype)
                dk_r[rows[d], ck[h]] = dks[i].astype(dk_r.dtype)
                dv_r[rows[d], cv[h]] = dvs[i].astype(dv_r.dtype)
                dlf_r[rows[d], ck[h]] = dlfs[i].astype(dlf_r.dtype)
        for i, (d, h) in enumerate(chains):
            dst[d, h] = dsts[i]

    def dir_specs(kk, lf, rev):
        return [spec(wk, q[1], rev), spec(wk, kk[1], rev), spec(wv, v[1], rev), spec(wk, lf[1], rev), sspec(rev),
                spec(wv, 0, rev)]

    def dir_out_specs(rev):
        return [spec(wk, 0, rev), spec(wk, 0, rev), spec(wv, 0, rev), spec(wk, 0, rev)]

    shapes = [jax.ShapeDtypeStruct((n_rows, wk), out_dtype), jax.ShapeDtypeStruct((n_rows, wk), out_dtype),
              jax.ShapeDtypeStruct((n_rows, wv), out_dtype), jax.ShapeDtypeStruct((n_rows, wk), F32)]
    outs = pl.pallas_call(
        body, name=name, grid=(n_steps,), in_specs=dir_specs(k_f, lf_f, False) + dir_specs(k_b, lf_b, True),
        out_specs=dir_out_specs(False) + dir_out_specs(True), out_shape=shapes + shapes,
        scratch_shapes=[pltpu.VMEM((2, n_heads, dv, dk), F32)],
        compiler_params=_params(("arbitrary",)),
    )(q[0], k_f[0], v[0], lf_f[0], st_f, do, q[0], k_b[0], v[0], lf_b[0], st_b, do)
    return outs[:4], outs[4:]


def _row2(v):
    return v.reshape(1, -1)


def _mlp_fwd(tag, h, gain, w1, w2):
    y = _rowcall(f"{tag}_norm", _rmsnorm_fn, [(h, h.shape[1], 0)], [gain], [(h.shape[1], BF16)], tm=512)[0]
    hid = _mm(f"{tag}_up", y, w1, out_dtypes=(BF16,))
    h_out = _mm(f"{tag}_down", hid, w2, a_pro=_relu2, extras=(h,), epi=_add_epi)
    return h_out, (y, hid)


def _dw(name, a, b, **kw):
    return _mm(name, a, b, mode="tn", epi=lambda acc: (acc, acc), out_dtypes=(F32, BF16), **kw)


def _mlp_bwd(tag, h, gain, w1, w2, saved, dh_out):
    y, hid = saved
    dhid = _mm(f"{tag}_dact", dh_out, w2, mode="nt", extras=(hid,), epi=_relu2_bwd_epi, out_dtypes=(BF16,))
    dw2 = _dw(f"{tag}_dw2", hid, dh_out, a_pro=_relu2)
    dw1 = _dw(f"{tag}_dw1", y, dhid, out_split=N_CHIPS)
    dh, dgain = _dy_norm_bwd(f"{tag}_dy", dhid, w1, h, gain, dh_out)
    return dh, dgain, dw1, dw2


def _dy_norm_bwd(name, dz, w, h, gain, dres, pin=None, **tiles):
    def epi(dy, h_tile, dres_tile, gain_row):
        _, vjp = jax.vjp(lambda u, v: _rmsnorm_fn(u, v)[0], h_tile, gain_row)
        dh, dgain = vjp(dy)
        return dh + dres_tile, dgain

    assert h.shape[1] <= 1024
    tiles.setdefault("tm", 1024)
    dh, dgain_parts = _mm(name, dz, w, mode="nt", extras=(h, dres), epi=epi, epi_pars=(gain,), row_sum=True,
                          pin=pin, **tiles)
    return dh, jnp.sum(dgain_parts, axis=0)


def _local_step(x, target, w, pin=None, late=None, emit=None):
    g = {}
    d_model = x.shape[1]
    rg_w = hg_w = d_model // 2
    pins = []

    def send_off(tag, pairs):
        if emit is not None:
            pins.append(emit(tag, [p[0] for p in pairs], [p[1] for p in pairs]))

    def both(fn, pair):
        return [fn(t) for t in pair]

    def chip_major(t):
        return t.reshape(N_CHIPS, t.shape[0] // N_CHIPS, t.shape[1])

    h_a0 = x
    gain = _row2(w["norm_mix"][0])
    y0 = _rowcall("l0_norm", _rmsnorm_fn, [(h_a0, d_model, 0)], [gain], [(d_model, BF16)], tm=512, pin=pin)[0]
    proj0 = _mm("l0_in", y0, w["ab_w_in"])
    conv_w, conv_b = w["rg_conv_w"], _row2(w["rg_conv_b"])
    xc = _conv_fwd("rg_conv", proj0, 0, conv_w, conv_b)
    gate_pars = [w["rg_wa_bd"], w["rg_wx_bd"], w["rg_b_a"], w["rg_b_x"], w["rg_lambda"]]
    a_f, u_f, a_b, u_b = _rowcall("rg_gates", _rg_gates_fn, [(xc, rg_w, 0)], gate_pars, [(rg_w, F32)] * 4)
    hs_f, hs_b = _scan_fwd("rg_scan", a_f, u_f, a_b, u_b)
    hg_rows = [(proj0, hg_w, 2), (proj0, hg_w, 3), (proj0, hg_w, 4)]
    qh, k_f, lf_f, k_b, lf_b = _rowcall("hg_pre", _hg_pre_fn, hg_rows, [w["hg_lb_logits"]], [(hg_w, F32)] * 5)
    iv = (proj0, 5)
    o_f, o_b, st_f, st_b = _attn_fwd("hg_attn", (qh, 0), (k_f, 0), (k_b, 0), iv, (lf_f, 0), (lf_b, 0), 4, 128, 128)
    post0_rows = [(hs_f, rg_w, 0), (hs_b, rg_w, 0), (proj0, rg_w, 1), (o_f, hg_w, 0), (o_b, hg_w, 0), (proj0, hg_w, 6)]
    hg_gain = _row2(w["hg_norm"])
    mix_in0 = _rowcall("l0_post", _post0_fwd_fn, post0_rows, [hg_gain], [(d_model, BF16)])[0]
    if late is not None:
        w = {**w, **late(mix_in0)}
    h_b0 = _mm("l0_out", mix_in0, w["ab_w_out"], extras=(h_a0,), epi=_add_epi)
    h_c0, mlp0 = _mlp_fwd("mlp0", h_b0, _row2(w["norm_mlp"][0]), w["mlp_w1"][0], w["mlp_w2"][0])

    h_a1 = h_c0
    gain1 = _row2(w["norm_mix"][1])
    y1 = _rowcall("l1_norm", _rmsnorm_fn, [(h_a1, d_model, 0)], [gain1], [(d_model, BF16)], tm=512)[0]
    proj1 = _mm("l1_in", y1, w["gla_w_in_pad"], tn=640)
    gla_pars = [w["gla_w_up_pad"], w["gla_b_gate"]]
    gq, glf_f, glf_b = _rowcall("gla_pre", _gla_pre_fn, [(proj1, 512, 0), (proj1, LANES, 24)], gla_pars, [(512, F32)] * 3)
    gk, gv = (proj1, 1), (proj1, 1)
    go_f, go_b, gst_f, gst_b = _attn_fwd("gla_attn", (gq, 0), gk, gk, gv, (glf_f, 0), (glf_b, 0), 4, 128, 256)
    gla_gain = _row2(w["gla_norm"])
    post1_rows = [(go_f, d_model, 0), (go_b, d_model, 0), (proj1, d_model, 2)]
    mix_in1 = _rowcall("l1_post", _gla_post_fwd_fn, post1_rows, [gla_gain], [(d_model, BF16)])[0]
    h_b1 = _mm("l1_out", mix_in1, w["gla_w_out"], extras=(h_a1,), epi=_add_epi)
    h_c1, mlp1 = _mlp_fwd("mlp1", h_b1, _row2(w["norm_mlp"][1]), w["mlp_w1"][1], w["mlp_w2"][1])

    dh, loss, g["norm_final"] = _rowcall(
        "loss_head", _loss_head_fn, [(h_c1, d_model, 0), (target, d_model, 0)], [_row2(w["norm_final"])],
        [(d_model, F32)], [(1, LANES), (1, d_model)], tm=512)

    dh, g_nmlp1, g_w1_1, g_w2_1 = _mlp_bwd("mlp1", h_b1, _row2(w["norm_mlp"][1]), w["mlp_w1"][1], w["mlp_w2"][1], mlp1, dh)
    send_off("mlp1", [g_w1_1, both(chip_major, g_w2_1)])
    dmix1 = _mm("l1_dout", dh, w["gla_w_out"], mode="nt")
    g_gla_out = _dw("l1_dwout", mix_in1, dh)
    g["gla_w_out"] = g_gla_out[0]
    dgo, dr, g["gla_norm"] = _rowcall(
        "l1_dpost", _gla_post_bwd_fn, post1_rows + [(dmix1, d_model, 0)], [gla_gain],
        [(d_model, F32), (d_model, BF16)], [(1, d_model)], pin=pins.pop() if pins else None)
    (dq_f, dk_f, dv_f, dlf_f), (dq_b, dk_b, dv_b, dlf_b) = _attn_bwd(
        "gla_dattn", (gq, 0), gk, gk, gv, (glf_f, 0), (glf_b, 0), gst_f, gst_b, dgo, 4, 128, 256)

    def gla_pre_bwd(q, lr, dq1, dq2, dlf1, dlf2, dk1, dk2, dv1, dv2, w_up, b_gate):
        dlr = jnp.zeros_like(lr)
        dws, dbs = [], []
        for d, dlf in enumerate((dlf1, dlf2)):
            z = _raw_nn(lr, w_up[d]) + b_gate[d:d + 1]
            dz = dlf * _sigmoid(-z) * (1.0 / 16.0)
            dlr = dlr + _raw_nt(dz, w_up[d])
            dws.append(_raw_tn(dz, lr))
            dbs.append(jnp.sum(dz, axis=0, keepdims=True))
        return ((dq1 + dq2) * (128.0 ** -0.5), dk1 + dk2, dv1 + dv2, dlr, dws[0], dws[1], dbs[0], dbs[1])

    rows = [(proj1, 512, 0), (proj1, LANES, 24), (dq_f, 512, 0), (dq_b, 512, 0), (dlf_f, 512, 0), (dlf_b, 512, 0),
            (dk_f, 512, 0), (dk_b, 512, 0), (dv_f, d_model, 0), (dv_b, d_model, 0)]
    dq, dk, dv, dlr, dwt_f, dwt_b, db_f, db_b = _rowcall(
        "gla_dpre", gla_pre_bwd, rows, gla_pars, [(512, BF16), (512, BF16), (d_model, BF16), (LANES, BF16)],
        [(512, LANES), (512, LANES), (1, 512), (1, 512)])
    g["gla_w_up_pad"] = jnp.stack([dwt_f.T, dwt_b.T])
    g["gla_b_gate"] = jnp.concatenate([db_f, db_b], axis=0)
    dproj1 = jnp.concatenate([dq, dk, dv, dr, dlr], axis=1)
    g_gla_in = both(lambda t: _split_chips(t[:, :GLA_IN_WIDTH], 1), _dw("l1_dwin", y1, dproj1, tn=640))
    g["gla_w_in"] = g_gla_in[0]
    send_off("gla", [g_gla_in, both(chip_major, g_gla_out)])
    dh, g_nmix1 = _dy_norm_bwd("l1_dy", dproj1, w["gla_w_in_pad"], h_a1, gain1, dh,
                               pin=pins.pop() if pins else None, tk=640)

    dh, g_nmlp0, g_w1_0, g_w2_0 = _mlp_bwd("mlp0", h_b0, _row2(w["norm_mlp"][0]), w["mlp_w1"][0], w["mlp_w2"][0], mlp0, dh)
    g_ab_out = _dw("l0_dwout", mix_in0, dh)
    g["ab_w_out"] = g_ab_out[0]
    send_off("mlp0", [g_w1_0, both(chip_major, g_w2_0), both(chip_major, g_ab_out)])
    dmix0 = _mm("l0_dout", dh, w["ab_w_out"], mode="nt")
    dhs, dga, do, dg, g["hg_norm"] = _rowcall(
        "l0_dpost", _post0_bwd_fn, post0_rows + [(dmix0, d_model, 0)], [hg_gain],
        [(rg_w, F32), (rg_w, BF16), (hg_w, F32), (hg_w, BF16)], [(1, hg_w)], pin=pins.pop() if pins else None)
    (dqh_f, dk_f, div_f, dlf_f), (dqh_b, dk_b, div_b, dlf_b) = _attn_bwd(
        "hg_dattn", (qh, 0), (k_f, 0), (k_b, 0), iv, (lf_f, 0), (lf_b, 0), st_f, st_b, do, 4, 128, 128)

    def hg_pre_bwd(q, f_f, f_b, dq1, dq2, dk1, dlf1, dk2, dlf2, dv1, dv2, logits):
        _, vjp = jax.vjp(_hg_pre_fn, q, f_f, f_b, logits)
        dq, df_f, df_b, dlogits = vjp((dq1 + dq2, dk1, dlf1, dk2, dlf2))
        return dq, df_f, df_b, dv1 + dv2, dlogits

    rows = hg_rows + [(t, hg_w, 0) for t in (dqh_f, dqh_b, dk_f, dlf_f, dk_b, dlf_b, div_f, div_b)]
    dq, df_f, df_b, div, g["hg_lb_logits"] = _rowcall(
        "hg_dpre", hg_pre_bwd, rows, [w["hg_lb_logits"]], [(hg_w, BF16)] * 4, [(2, hg_w)])
    du_f, da_f, du_b, da_b = _scan_bwd("rg_dscan", a_f, hs_f, a_b, hs_b, dhs)
    gates_bwd = _vjp_of(_rg_gates_fn, 1, 4, 5)
    rows = [(xc, rg_w, 0), (da_f, rg_w, 0), (du_f, rg_w, 0), (da_b, rg_w, 0), (du_b, rg_w, 0)]
    dxc, g["rg_wa_bd"], g["rg_wx_bd"], g["rg_b_a"], g["rg_b_x"], g["rg_lambda"] = _rowcall(
        "rg_dgates", gates_bwd, rows, gate_pars, [(rg_w, F32)],
        [(2, rg_w, rg_w), (2, rg_w, rg_w), (2, rg_w), (2, rg_w), (2, rg_w)])
    dxa, g["rg_conv_w"], g["rg_conv_b"] = _conv_bwd("rg_dconv", proj0, 0, conv_w, dxc)
    dproj0 = jnp.concatenate([dxa, dga, dq, df_f, df_b, div, dg], axis=1)
    g_ab_in = _dw("l0_dwin", y0, dproj0, out_split=N_CHIPS)
    g["ab_w_in"] = g_ab_in[0]
    send_off("ab", [g_ab_in])
    grad_x, g_nmix0 = _dy_norm_bwd("l0_dy", dproj0, w["ab_w_in"], h_a0, gain, dh, pin=pins.pop() if pins else None)

    g["norm_mix"] = jnp.concatenate([g_nmix0, g_nmix1], axis=0)
    g["norm_mlp"] = jnp.concatenate([g_nmlp0, g_nmlp1], axis=0)
    g["mlp_w1"] = [g_w1_0[0], g_w1_1[0]]
    g["mlp_w2"] = [g_w2_0[0], g_w2_1[0]]
    return loss, grad_x, g


def _block_diag(w):
    d, g, n, _ = w.shape
    eye = jnp.eye(g, dtype=w.dtype)
    return (w[:, :, :, None, :] * eye[None, :, None, :, None]).reshape(d, g * n, g * n)


def _block_diag_extract(wbd, g):
    d, gn, _ = wbd.shape
    n = gn // g
    blocks = wbd.reshape(d, g, n, g, n)
    return jnp.stack([blocks[:, i, :, i, :] for i in range(g)], axis=1)


def _prepare_weights(big, full):
    w = {k: full[k] for k in ("norm_mix", "norm_mlp", "norm_final", "hg_lb_logits")}
    for k in ("rg_conv_w", "rg_conv_b", "rg_b_a", "rg_b_x", "rg_lambda", "hg_norm", "gla_b_gate", "gla_norm"):
        w[k] = full[k][0]
    w["rg_wa_bd"] = _block_diag(full["rg_w_a"][0])
    w["rg_wx_bd"] = _block_diag(full["rg_w_x"][0])
    up = full["gla_w_gate_up"][0]
    rank = up.shape[1]
    pad = jnp.zeros((2, LANES, up.shape[2]), F32)
    w["gla_w_up_pad"] = pad.at[0, 0:rank].set(up[0]).at[1, rank:2 * rank].set(up[1])
    w.update(_prepare_matrices(big))
    return w


def _prepare_matrices(big):
    w = {}
    if "mlp_w1" in big:
        w["mlp_w1"] = list(big["mlp_w1"])
        w["mlp_w2"] = [t.reshape(-1, t.shape[-1]) for t in big["mlp_w2"]]
    if "ab_w_in" in big:
        w["ab_w_in"] = big["ab_w_in"]
    if "ab_w_out" in big:
        w["ab_w_out"] = big["ab_w_out"].reshape(-1, big["ab_w_out"].shape[-1])
    if "gla_w_in" in big:
        w["gla_w_out"] = big["gla_w_out"].reshape(-1, big["gla_w_out"].shape[-1])
        gla_in = _join_chips(big["gla_w_in"], 1)
        w["gla_w_in_pad"] = jnp.pad(gla_in, ((0, 0), (0, GLA_IN_PAD - gla_in.shape[1])))
    return w


def _finish_grads(g, rank=16, rg_blocks=8):
    def chip_major(t):
        return t.reshape(N_CHIPS, t.shape[0] // N_CHIPS, t.shape[1])

    big = {
        "mlp_w1": list(g["mlp_w1"]), "mlp_w2": [chip_major(t) for t in g["mlp_w2"]],
        "ab_w_in": g["ab_w_in"], "ab_w_out": chip_major(g["ab_w_out"]),
        "gla_w_in": g["gla_w_in"], "gla_w_out": chip_major(g["gla_w_out"]),
    }
    small = {
        "norm_mix": g["norm_mix"], "norm_mlp": g["norm_mlp"], "norm_final": g["norm_final"][0],
        "rg_conv_w": g["rg_conv_w"][None], "rg_conv_b": g["rg_conv_b"],
        "rg_w_a": _block_diag_extract(g["rg_wa_bd"], rg_blocks)[None], "rg_b_a": g["rg_b_a"][None],
        "rg_w_x": _block_diag_extract(g["rg_wx_bd"], rg_blocks)[None], "rg_b_x": g["rg_b_x"][None],
        "rg_lambda": g["rg_lambda"][None], "hg_lb_logits": g["hg_lb_logits"], "hg_norm": g["hg_norm"],
        "gla_w_gate_up": jnp.stack([g["gla_w_up_pad"][0, 0:rank], g["gla_w_up_pad"][1, rank:2 * rank]])[None],
        "gla_b_gate": g["gla_b_gate"][None], "gla_norm": g["gla_norm"],
    }
    return big, small


MATRICES = (("mlp_w1", 0), ("mlp_w1", 1), ("mlp_w2", 0), ("mlp_w2", 1), ("ab_w_in", 0), ("ab_w_out", 0),
            ("gla_w_in", 0), ("gla_w_out", 0))
EARLY_MATRICES = ("ab_w_in",)
SMALL_SHARDED = ("rg_conv_w", "rg_b_a", "rg_b_x", "rg_lambda", "gla_w_gate_up", "gla_b_gate", "gla_norm")
SMALL_REPLICATED = ("norm_mix", "norm_mlp", "norm_final", "rg_conv_b", "rg_w_a", "rg_w_x", "hg_lb_logits", "hg_norm")
WEIGHTS = ("norm_mix", "norm_mlp", "norm_final", "mlp_w1", "mlp_w2", "ab_w_in", "ab_w_out", "rg_conv_w", "rg_conv_b",
           "rg_w_a", "rg_b_a", "rg_w_x", "rg_b_x", "rg_lambda", "hg_lb_logits", "hg_norm", "gla_w_in", "gla_w_out",
           "gla_w_gate_up", "gla_b_gate", "gla_norm")
ROW_ALIGN = 16


def _pack(arrays, lead=0):
    head = arrays[0].shape[:lead]
    flat = jnp.concatenate([a.reshape(head + (-1,)) for a in arrays], axis=lead)
    n = flat.shape[-1]
    quantum = LANES * ROW_ALIGN
    padded = -(-n // quantum) * quantum
    if padded != n:
        flat = jnp.pad(flat, [(0, 0)] * lead + [(0, padded - n)])
    return flat.reshape(head + (padded // LANES, LANES))


def _unpack(buf, shapes, lead=0):
    head = buf.shape[:lead]
    flat = buf.reshape(head + (-1,))
    out, off = [], 0
    for s in shapes:
        n = 1
        for v in s:
            n *= v
        out.append(lax.slice_in_dim(flat, off, off + n, axis=lead).reshape(head + tuple(s)))
        off += n
    return out


def _join_chips(gathered, axis):
    t = jnp.moveaxis(gathered, 0, axis)
    return t.reshape(t.shape[:axis] + (t.shape[axis] * t.shape[axis + 1],) + t.shape[axis + 2:])


def _split_chips(full, axis):
    s = full.shape
    t = full.reshape(s[:axis] + (N_CHIPS, s[axis] // N_CHIPS) + s[axis + 1:])
    return jnp.moveaxis(t, axis, 0)


_ANY = pl.BlockSpec(memory_space=pl.ANY)


def _place():
    return lax.axis_index("x"), lax.axis_index("y"), lax.axis_index("c")


def _into_slot(name, src, slot, n_slots, dtype, tm, layer=None):
    r, lanes = src.shape[-2:]
    tm = _row_tile(r, tm, ROW_ALIGN)

    def body(slot_ref, in_ref, o_ref):
        o_ref[...] = in_ref[...].astype(o_ref.dtype)

    if layer is None:
        in_spec = pl.BlockSpec((tm, lanes), lambda i, slot_ref: (i, 0))
    else:
        in_spec = pl.BlockSpec((None, tm, lanes), lambda i, slot_ref: (layer, i, 0))
    grid_spec = pltpu.PrefetchScalarGridSpec(
        num_scalar_prefetch=1, grid=(r // tm,), in_specs=[in_spec],
        out_specs=pl.BlockSpec((None, tm, lanes), lambda i, slot_ref: (slot_ref[0], i, 0)))
    return pl.pallas_call(
        body, name=name, grid_spec=grid_spec, out_shape=jax.ShapeDtypeStruct((n_slots, r, lanes), dtype),
        compiler_params=_params(("parallel",)),
    )(slot.reshape(1).astype(jnp.int32), src)


def _chip_peers():
    x, y, c = _place()
    return 2 * x + y, c, [(1 - x, y), (x, 1 - y), (1 - x, 1 - y)]


def _comm_call(name, body, ins, out_shapes, n_sems, aliases=None):
    return pl.pallas_call(
        body, name=name, in_specs=[_ANY] * len(ins), out_specs=[_ANY] * len(out_shapes), out_shape=out_shapes,
        input_output_aliases=aliases or {},
        scratch_shapes=[pltpu.SemaphoreType.DMA((n_sems,)), pltpu.SemaphoreType.DMA((n_sems,))],
    )(*ins)


def _gather_chips(name, bufs):
    n = len(bufs)

    def body(*refs):
        outs, send_sems, recv_sems = refs[n:2 * n], refs[2 * n], refs[2 * n + 1]
        x, y, c = _place()
        me, _, peers = _chip_peers()

        def rows(a, block, half):
            rh = outs[a].shape[1] // 2
            return outs[a].at[block, pl.ds(half * rh, rh)]

        def copy(a, j, block, half, to, sem):
            return pltpu.make_async_remote_copy(
                src_ref=rows(a, block, half), dst_ref=rows(a, block, half), send_sem=send_sems.at[sem],
                recv_sem=recv_sems.at[sem], device_id=to, device_id_type=MESH)

        def over_ici(a, j, block):
            px, py = peers[j]
            return copy(a, j, block, c, (px, py, c), 6 * a + j)

        def to_sibling(a, j, block, half):
            return copy(a, j, block, half, (x, y, 1 - c), 6 * a + 3 + j)

        sends = [over_ici(a, j, me) for a in range(n) for j in range(3)]
        for cp in sends:
            cp.start()
        for a in range(n):
            for j, (px, py) in enumerate(peers):
                over_ici(a, j, 2 * px + py).wait_recv()
                handed = to_sibling(a, j, 2 * px + py, c)
                handed.start()
                sends.append(handed)
        for a in range(n):
            for j, (px, py) in enumerate(peers):
                to_sibling(a, j, 2 * px + py, 1 - c).wait_recv()
        for cp in sends:
            cp.wait_send()

    shapes = [jax.ShapeDtypeStruct(b.shape, b.dtype) for b in bufs]
    return _comm_call(name, body, bufs, shapes, 6 * n, {a: a for a in range(n)})


_HBM = pl.BlockSpec(memory_space=pltpu.HBM)
_SEM = pl.BlockSpec(memory_space=pltpu.SEMAPHORE)
_EFFECT = pltpu.SideEffectType.DATAFLOW_SIDE_EFFECTING


def _half_rows(ref, block, half):
    rh = ref.shape[1] // 2
    return ref.at[block, pl.ds(half * rh, rh)]


def _gather_start(name, bufs, after):
    n = len(bufs)

    def body(*refs):
        ins, send_sems, recv_sems, token = refs[:n], refs[n + 1], refs[n + 2], refs[-1]
        me, c, peers = _chip_peers()
        for a in range(n):
            mine = _half_rows(ins[a], me, c)
            for j, (px, py) in enumerate(peers):
                pltpu.make_async_remote_copy(
                    src_ref=mine, dst_ref=mine, send_sem=send_sems.at[3 * a + j], recv_sem=recv_sems.at[3 * a + j],
                    device_id=(px, py, c), device_id_type=MESH).start()
        token[...] = jnp.zeros_like(token)

    out_shape = (pltpu.SemaphoreType.DMA((3 * n,)), pltpu.SemaphoreType.DMA((3 * n,)),
                 *[pltpu.HBM(b.shape, b.dtype) for b in bufs], jax.ShapeDtypeStruct((8, LANES), F32))
    return pl.pallas_call(
        body, name=name, out_shape=out_shape, in_specs=[_HBM] * n + [_ANY],
        out_specs=(_SEM, _SEM, *[_HBM] * n, pl.BlockSpec(memory_space=pltpu.VMEM)),
        input_output_aliases={a: 2 + a for a in range(n)},
        compiler_params=pltpu.CompilerParams(has_side_effects=_EFFECT),
    )(*[pltpu.with_memory_space_constraint(b, pltpu.HBM) for b in bufs], after)


def _gather_wait(name, bufs, send_sems, recv_sems, after):
    n = len(bufs)

    def body(*refs):
        ins, send_sems, recv_sems = refs[:n], refs[n], refs[n + 1]
        me, c, peers = _chip_peers()
        for a in range(n):
            for j, (px, py) in enumerate(peers):
                copy = pltpu.make_async_remote_copy(
                    src_ref=_half_rows(ins[a], me, c), dst_ref=_half_rows(ins[a], 2 * px + py, c),
                    send_sem=send_sems.at[3 * a + j], recv_sem=recv_sems.at[3 * a + j],
                    device_id=(px, py, c), device_id_type=MESH)
                copy.wait_send()
                copy.wait_recv()

    return pl.pallas_call(
        body, name=name, out_shape=tuple(pltpu.HBM(b.shape, b.dtype) for b in bufs),
        in_specs=[_HBM] * n + [_SEM, _SEM, _ANY], out_specs=tuple([_HBM] * n),
        input_output_aliases={a: a for a in range(n)},
        compiler_params=pltpu.CompilerParams(has_side_effects=_EFFECT),
    )(*bufs, send_sems, recv_sems, after)


def _hand_over(name, bufs):
    n = len(bufs)

    def body(*refs):
        outs, send_sems, recv_sems = refs[n:2 * n], refs[2 * n], refs[2 * n + 1]
        x, y, c = _place()
        _, _, peers = _chip_peers()

        def copy(a, j, half):
            px, py = peers[j]
            rows = _half_rows(outs[a], 2 * px + py, half)
            return pltpu.make_async_remote_copy(
                src_ref=rows, dst_ref=rows, send_sem=send_sems.at[3 * a + j], recv_sem=recv_sems.at[3 * a + j],
                device_id=(x, y, 1 - c), device_id_type=MESH)

        sends = [copy(a, j, c) for a in range(n) for j in range(3)]
        for cp in sends:
            cp.start()
        for a in range(n):
            for j in range(3):
                copy(a, j, 1 - c).wait_recv()
        for cp in sends:
            cp.wait_send()

    shapes = [jax.ShapeDtypeStruct(b.shape, b.dtype) for b in bufs]
    return _comm_call(name, body, bufs, shapes, 3 * n, {a: a for a in range(n)})


def _pair_gather(name, bufs):
    n = len(bufs)

    def body(*refs):
        ins, outs, send_sems, recv_sems = refs[:n], refs[n:2 * n], refs[2 * n], refs[2 * n + 1]
        x, y, c = _place()

        def copy(a, block):
            return pltpu.make_async_remote_copy(
                src_ref=ins[a].at[block], dst_ref=outs[a].at[block], send_sem=send_sems.at[a],
                recv_sem=recv_sems.at[a], device_id=(x, y, 1 - c), device_id_type=MESH)

        sends = [copy(a, c) for a in range(n)]
        for cp in sends:
            cp.start()
        for a in range(n):
            copy(a, 1 - c).wait_recv()
        for cp in sends:
            cp.wait_send()

    shapes = [jax.ShapeDtypeStruct(b.shape, b.dtype) for b in bufs]
    return _comm_call(name, body, bufs, shapes, n, {a: a for a in range(n)})


def _all_peers():
    x, y, c = _place()
    peers = []
    for mask in range(1, N_DEV):
        fx, fy, fc = (mask >> 2) & 1, (mask >> 1) & 1, mask & 1
        peers.append((jnp.where(fx, 1 - x, x), jnp.where(fy, 1 - y, y), jnp.where(fc, 1 - c, c)))
    return 4 * x + 2 * y + c, peers


def _reduce_copies(srcs, lands, send_sems, recv_sems):
    me, peers = _all_peers()
    sends, arrivals = [], []
    for a in range(len(srcs)):
        for j, (px, py, pc) in enumerate(peers):
            k = (N_DEV - 1) * a + j
            sends.append(pltpu.make_async_remote_copy(
                src_ref=srcs[a].at[2 * px + py, pc], dst_ref=lands[a].at[me], send_sem=send_sems.at[k],
                recv_sem=recv_sems.at[k], device_id=(px, py, pc), device_id_type=MESH))
            arrivals.append(pltpu.make_async_remote_copy(
                src_ref=srcs[a].at[2 * px + py, pc], dst_ref=lands[a].at[4 * px + 2 * py + pc],
                send_sem=send_sems.at[k], recv_sem=recv_sems.at[k], device_id=(px, py, pc), device_id_type=MESH))
    return sends, arrivals


def _reduce_direct(name, srcs, pin=None):
    n = len(srcs)
    extra = [] if pin is None else [pin]

    def body(*refs):
        ins, outs = refs[:n], refs[n + len(extra):2 * n + len(extra)]
        sends, arrivals = _reduce_copies(ins, outs, refs[-2], refs[-1])
        for cp in sends:
            cp.start()
        for cp in arrivals:
            cp.wait_recv()
        for cp in sends:
            cp.wait_send()

    shapes = [jax.ShapeDtypeStruct((N_DEV,) + s.shape[2:], s.dtype) for s in srcs]
    return _comm_call(name, body, list(srcs) + extra, shapes, (N_DEV - 1) * n)


def _reduce_start(name, srcs):
    n = len(srcs)
    lands = [lax.empty((N_DEV,) + s.shape[2:], s.dtype) for s in srcs]

    def body(*refs):
        sends, _ = _reduce_copies(refs[:n], refs[n:2 * n], refs[2 * n], refs[2 * n + 1])
        for cp in sends:
            cp.start()
        refs[-1][...] = jnp.zeros_like(refs[-1])

    bufs = list(srcs) + lands
    n_sems = (N_DEV - 1) * n
    out_shape = (pltpu.SemaphoreType.DMA((n_sems,)), pltpu.SemaphoreType.DMA((n_sems,)),
                 *[pltpu.HBM(b.shape, b.dtype) for b in bufs], jax.ShapeDtypeStruct((8, LANES), F32))
    return pl.pallas_call(
        body, name=name, out_shape=out_shape, in_specs=[_HBM] * (2 * n),
        out_specs=(_SEM, _SEM, *[_HBM] * (2 * n), pl.BlockSpec(memory_space=pltpu.VMEM)),
        input_output_aliases={a: 2 + a for a in range(2 * n)},
        compiler_params=pltpu.CompilerParams(has_side_effects=_EFFECT),
    )(*[pltpu.with_memory_space_constraint(b, pltpu.HBM) for b in bufs])


def _reduce_wait(name, srcs, lands, send_sems, recv_sems, after):
    n = len(srcs)

    def body(*refs):
        sends, arrivals = _reduce_copies(refs[:n], refs[n:2 * n], refs[2 * n], refs[2 * n + 1])
        for cp in sends:
            cp.wait_send()
        for cp in arrivals:
            cp.wait_recv()

    bufs = list(srcs) + list(lands)
    outs = pl.pallas_call(
        body, name=name, out_shape=tuple(pltpu.HBM(b.shape, b.dtype) for b in bufs),
        in_specs=[_HBM] * (2 * n) + [_SEM, _SEM, _ANY], out_specs=tuple([_HBM] * (2 * n)),
        input_output_aliases={a: a for a in range(2 * n)},
        compiler_params=pltpu.CompilerParams(has_side_effects=_EFFECT),
    )(*bufs, send_sems, recv_sems, after)
    return list(outs[n:])


def _reduce_sum(name, own, land, chip, core):
    n, rh, lanes = land.shape
    tm = _row_tile(rh, 1024, ROW_ALIGN)

    def body(idx_ref, own_ref, *rest):
        total = own_ref[...]
        for g_ref in rest[:-1]:
            total = total + g_ref[...].astype(F32)
        rest[-1][...] = total

    def block(k):
        return pl.BlockSpec((None, tm, lanes), lambda i, idx_ref: ((2 * idx_ref[0] + idx_ref[1] + k) % n, i, 0))

    grid_spec = pltpu.PrefetchScalarGridSpec(
        num_scalar_prefetch=1, grid=(rh // tm,),
        in_specs=[pl.BlockSpec((None, None, tm, lanes), lambda i, idx_ref: (idx_ref[0], idx_ref[1], i, 0))]
        + [block(k) for k in range(1, n)],
        out_specs=pl.BlockSpec((None, tm, lanes), lambda i, idx_ref: (idx_ref[1], i, 0)))
    return pl.pallas_call(
        body, name=name, grid_spec=grid_spec, out_shape=jax.ShapeDtypeStruct((2, rh, lanes), F32),
        compiler_params=_params(("parallel",)),
    )(jnp.stack([chip, core]).astype(jnp.int32), own, *[land] * (n - 1))


def _gather_all_start(name, buf):
    def body(in_ref, send_sems, recv_sems, out_ref, token):
        me, peers = _all_peers()
        for j, peer in enumerate(peers):
            pltpu.make_async_remote_copy(
                src_ref=in_ref.at[me], dst_ref=in_ref.at[me], send_sem=send_sems.at[j], recv_sem=recv_sems.at[j],
                device_id=peer, device_id_type=MESH).start()
        token[...] = jnp.zeros_like(token)

    n = N_DEV - 1
    return pl.pallas_call(
        body, name=name, in_specs=[_HBM],
        out_shape=(pltpu.SemaphoreType.DMA((n,)), pltpu.SemaphoreType.DMA((n,)), pltpu.HBM(buf.shape, buf.dtype),
                   jax.ShapeDtypeStruct((8, LANES), F32)),
        out_specs=(_SEM, _SEM, _HBM, pl.BlockSpec(memory_space=pltpu.VMEM)), input_output_aliases={0: 2},
        compiler_params=pltpu.CompilerParams(has_side_effects=_EFFECT),
    )(pltpu.with_memory_space_constraint(buf, pltpu.HBM))


def _gather_all_wait(name, buf, send_sems, recv_sems, after):
    def body(in_ref, send_sems, recv_sems, after_ref, out_ref):
        me, peers = _all_peers()
        for j, (px, py, pc) in enumerate(peers):
            copy = pltpu.make_async_remote_copy(
                src_ref=in_ref.at[me], dst_ref=in_ref.at[4 * px + 2 * py + pc], send_sem=send_sems.at[j],
                recv_sem=recv_sems.at[j], device_id=(px, py, pc), device_id_type=MESH)
            copy.wait_send()
            copy.wait_recv()

    return pl.pallas_call(
        body, name=name, in_specs=[_HBM, _SEM, _SEM, _ANY], out_shape=pltpu.HBM(buf.shape, buf.dtype),
        out_specs=_HBM, input_output_aliases={0: 0},
        compiler_params=pltpu.CompilerParams(has_side_effects=_EFFECT),
    )(buf, send_sems, recv_sems, after)


def _sum_blocks(name, stacked, tm):
    n, r, lanes = stacked.shape

    def body(in_ref, o_ref):
        acc = in_ref[0]
        for j in range(1, n):
            acc = acc + in_ref[j]
        o_ref[...] = acc

    return pl.pallas_call(
        body, name=name, grid=(r // tm,), in_specs=[pl.BlockSpec((n, tm, lanes), lambda i: (0, i, 0))],
        out_specs=pl.BlockSpec((tm, lanes), lambda i: (i, 0)), out_shape=jax.ShapeDtypeStruct((r, lanes), F32),
        compiler_params=_params(("parallel",)),
    )(stacked)


def _row_tile(rows, pref, align):
    best = None
    for t in range(align, min(rows, pref) + 1, align):
        if rows % t == 0:
            best = t
    assert best is not None, (rows, pref, align)
    return best


def _adam(name, w, g, m, v):
    rows, width = w.shape
    tm = _row_tile(rows, max(8, 4096 * LANES // width), 8)
    args = [(t, width, 0) for t in (w, g, m, v)]
    return _rowcall(name, _adam_fn, args, [], [(width, F32)] * 3, tm=tm)


def kernel(x, norm_mix, norm_mlp, norm_final, mlp_w1, mlp_w2, ab_w_in, ab_w_out, rg_conv_w, rg_conv_b, rg_w_a, rg_b_a, rg_w_x, rg_b_x, rg_lambda, hg_lb_logits, hg_norm, gla_w_in, gla_w_out, gla_w_gate_up, gla_b_gate, gla_norm, loss_target, m_norm_mix, m_norm_mlp, m_norm_final, m_mlp_w1, m_mlp_w2, m_ab_w_in, m_ab_w_out, m_rg_conv_w, m_rg_conv_b, m_rg_w_a, m_rg_b_a, m_rg_w_x, m_rg_b_x, m_rg_lambda, m_hg_lb_logits, m_hg_norm, m_gla_w_in, m_gla_w_out, m_gla_w_gate_up, m_gla_b_gate, m_gla_norm, v_norm_mix, v_norm_mlp, v_norm_final, v_mlp_w1, v_mlp_w2, v_ab_w_in, v_ab_w_out, v_rg_conv_w, v_rg_conv_b, v_rg_w_a, v_rg_b_a, v_rg_w_x, v_rg_b_x, v_rg_lambda, v_hg_lb_logits, v_hg_norm, v_gla_w_in, v_gla_w_out, v_gla_w_gate_up, v_gla_b_gate, v_gla_norm):
    w = dict(norm_mix=norm_mix, norm_mlp=norm_mlp, norm_final=norm_final, mlp_w1=mlp_w1, mlp_w2=mlp_w2, ab_w_in=ab_w_in, ab_w_out=ab_w_out, rg_conv_w=rg_conv_w, rg_conv_b=rg_conv_b, rg_w_a=rg_w_a, rg_b_a=rg_b_a, rg_w_x=rg_w_x, rg_b_x=rg_b_x, rg_lambda=rg_lambda, hg_lb_logits=hg_lb_logits, hg_norm=hg_norm, gla_w_in=gla_w_in, gla_w_out=gla_w_out, gla_w_gate_up=gla_w_gate_up, gla_b_gate=gla_b_gate, gla_norm=gla_norm)
    m = dict(norm_mix=m_norm_mix, norm_mlp=m_norm_mlp, norm_final=m_norm_final, mlp_w1=m_mlp_w1, mlp_w2=m_mlp_w2, ab_w_in=m_ab_w_in, ab_w_out=m_ab_w_out, rg_conv_w=m_rg_conv_w, rg_conv_b=m_rg_conv_b, rg_w_a=m_rg_w_a, rg_b_a=m_rg_b_a, rg_w_x=m_rg_w_x, rg_b_x=m_rg_b_x, rg_lambda=m_rg_lambda, hg_lb_logits=m_hg_lb_logits, hg_norm=m_hg_norm, gla_w_in=m_gla_w_in, gla_w_out=m_gla_w_out, gla_w_gate_up=m_gla_w_gate_up, gla_b_gate=m_gla_b_gate, gla_norm=m_gla_norm)
    v = dict(norm_mix=v_norm_mix, norm_mlp=v_norm_mlp, norm_final=v_norm_final, mlp_w1=v_mlp_w1, mlp_w2=v_mlp_w2, ab_w_in=v_ab_w_in, ab_w_out=v_ab_w_out, rg_conv_w=v_rg_conv_w, rg_conv_b=v_rg_conv_b, rg_w_a=v_rg_w_a, rg_b_a=v_rg_b_a, rg_w_x=v_rg_w_x, rg_b_x=v_rg_b_x, rg_lambda=v_rg_lambda, hg_lb_logits=v_hg_lb_logits, hg_norm=v_hg_norm, gla_w_in=v_gla_w_in, gla_w_out=v_gla_w_out, gla_w_gate_up=v_gla_w_gate_up, gla_b_gate=v_gla_b_gate, gla_norm=v_gla_norm)
    chip = 2 * lax.axis_index("x") + lax.axis_index("y")
    core = lax.axis_index("c")
    sharded_shapes = [w[n].shape for n in SMALL_SHARDED]

    slots = [_into_slot(f"cast_{n}{layer}", w[n], chip, N_CHIPS, BF16, 512, layer) for n, layer in MATRICES]
    early = [i for i, (n, _) in enumerate(MATRICES) if n in EARLY_MATRICES]
    rest = [i for i in range(len(MATRICES)) if i not in early]

    def named(indices, arrays):
        big = {}
        for i, t in zip(indices, arrays):
            big.setdefault(MATRICES[i][0], []).append(t)
        return {n: (v if n in ("mlp_w1", "mlp_w2") else v[0]) for n, v in big.items()}

    vectors = _pack([w[n] for n in SMALL_SHARDED])
    vectors = _into_slot("place_vectors", vectors, chip, N_CHIPS, F32, vectors.shape[0])
    *gathered, vectors = _gather_chips("gather_early", [slots[i] for i in early] + [vectors])
    send_sems, recv_sems, *in_flight, token = _gather_start("gather_rest_start", [slots[i] for i in rest], gathered[0])

    def late_weights(after):
        landed = _gather_wait("gather_rest_wait", in_flight, send_sems, recv_sems, after)
        return _prepare_matrices(named(rest, _hand_over("gather_rest_share", list(landed))))

    big = named(early, gathered)
    small_all = _unpack(vectors, sharded_shapes, lead=1)
    full = {n: w[n] for n in SMALL_REPLICATED}
    for n, t in zip(SMALL_SHARDED, small_all):
        full[n] = _join_chips(t, t.ndim - 2)

    def halves(t):
        return t.reshape(N_CHIPS, 2, t.shape[1] // 2, t.shape[2])

    in_flight_grads = {}

    def emit(tag, arrays32, arrays16):
        n = len(arrays16)
        send, recv, *rest = _reduce_start(f"reduce_{tag}_start", [halves(t) for t in arrays16])
        in_flight_grads[tag] = ([halves(t) for t in arrays32], rest[:n], rest[n:2 * n], send, recv)
        return rest[-1]

    loss_part, grad_x, g_kernel = _local_step(
        x[0], loss_target[0], _prepare_weights(big, full), token, late_weights, emit)
    g_big, g_full = _finish_grads(g_kernel)

    small_names = SMALL_REPLICATED + SMALL_SHARDED
    reduced_shapes = [g_full[n].shape for n in small_names] + [loss_part.shape]
    g_small = _pack([g_full[n] for n in small_names] + [loss_part])
    device = 2 * chip + core
    g_small = _into_slot("place_small", g_small, device, N_DEV, F32, g_small.shape[0])
    small_send, small_recv, small_in_flight, small_token = _gather_all_start("reduce_small_start", g_small)

    def reduced_halves(tag, after):
        own, srcs, lands, send, recv = in_flight_grads[tag]
        landed = _reduce_wait(f"reduce_{tag}_wait", srcs, lands, send, recv, after)
        return [_reduce_sum(f"reduce_add_{tag}{i}", o, f, chip, core) for i, (o, f) in enumerate(zip(own, landed))]

    def shared(name, halves_):
        return [t.reshape(2 * t.shape[1], t.shape[2]) for t in _pair_gather(name, halves_)]

    delta, new_m, new_v = {}, {}, {}

    def adam(n):
        flat = [t.reshape(-1, t.shape[-1]) for t in (w[n], grads[n], m[n], v[n])]
        for dst, t in zip((delta, new_m, new_v), _adam(f"adam_{n}", *flat)):
            dst[n] = t.reshape(w[n].shape)

    mlp1, gla, mlp0 = (reduced_halves(tag, small_token) for tag in ("mlp1", "gla", "mlp0"))
    w1_0, w1_1, w2_0, w2_1, ab_out, gla_in, gla_out = shared(
        "reduce_share", [mlp0[0], mlp1[0], mlp0[1], mlp1[1], mlp0[2], *gla])
    grads = {"mlp_w1": jnp.stack([w1_0, w1_1]), "mlp_w2": jnp.stack([w2_0, w2_1]), "ab_w_out": ab_out[None],
             "gla_w_in": gla_in[None], "gla_w_out": gla_out[None]}
    for n in tuple(grads):
        adam(n)
    grads["ab_w_in"] = shared("reduce_share_ab", reduced_halves("ab", new_v["gla_w_out"]))[0][None]
    adam("ab_w_in")

    g_small_all = _gather_all_wait("reduce_small_wait", small_in_flight, small_send, small_recv, new_v["ab_w_in"])
    g_small_red = _sum_blocks("reduce_small_add", g_small_all, g_small_all.shape[1])
    *small_red, loss_sum = _unpack(g_small_red, reduced_shapes)
    loss = loss_sum[0, 0]
    g_small_full = dict(zip(small_names, small_red))
    for n in SMALL_REPLICATED:
        grads[n] = g_small_full[n]
    for n in SMALL_SHARDED:
        width = w[n].shape[-1]
        grads[n] = lax.dynamic_slice_in_dim(g_small_full[n], chip * width, width, axis=g_small_full[n].ndim - 1)

    small_shapes = [w[n].shape for n in small_names]
    packs = [_pack([src[n] for n in small_names]) for src in (w, grads, m, v)]
    d_small, m_small, v_small = _adam("adam_small", *packs)
    for dst, buf in ((delta, d_small), (new_m, m_small), (new_v, v_small)):
        dst.update(zip(small_names, _unpack(buf, small_shapes)))

    return (loss, grad_x[None], *[grads[n] for n in WEIGHTS], *[delta[n] for n in WEIGHTS],
            *[new_m[n] for n in WEIGHTS], *[new_v[n] for n in WEIGHTS])
```

```python
import functools

import jax
import jax.numpy as jnp
from jax import lax
from jax.experimental import pallas as pl
from jax.experimental.pallas import tpu as pltpu

F32 = jnp.float32
BF16 = jnp.bfloat16
MESH = pl.DeviceIdType.MESH

LANES = 128
CHUNK = 64
ATTN_SUB = 4
EPS = 1e-6
RG_C = 8.0
N_CHIPS = 4
N_DEV = 8
GLA_IN_WIDTH = 3104
GLA_IN_PAD = 3200
VMEM_LIMIT = 56 * 1024 * 1024

ADAM_LR = 0.001
ADAM_B1 = 0.9
ADAM_B2 = 0.999
ADAM_EPS = 1e-08
ADAM_WD = 0.01
ADAM_STEP = 10


def _raw_dot(a, b, ca, cb):
    return lax.dot_general(a.astype(BF16), b.astype(BF16), (((ca,), (cb,)), ((), ())),
                           preferred_element_type=F32)


def _raw_nn(a, b):
    return _raw_dot(a, b, 1, 0)


def _raw_nt(a, b):
    return _raw_dot(a, b, 1, 1)


def _raw_tn(a, b):
    return _raw_dot(a, b, 0, 0)


@jax.custom_vjp
def _dot_nn(a, b):
    return _raw_nn(a, b)


def _dot_nn_fwd(a, b):
    return _raw_nn(a, b), (a, b)


def _dot_nn_bwd(res, g):
    a, b = res
    return _raw_nt(g, b), _raw_tn(a, g)


_dot_nn.defvjp(_dot_nn_fwd, _dot_nn_bwd)


@jax.custom_vjp
def _dot_nt(a, b):
    return _raw_nt(a, b)


def _dot_nt_fwd(a, b):
    return _raw_nt(a, b), (a, b)


def _dot_nt_bwd(res, g):
    a, b = res
    return _raw_nn(g, b), _raw_tn(g, a)


_dot_nt.defvjp(_dot_nt_fwd, _dot_nt_bwd)


@jax.custom_vjp
def _dot_tn(a, b):
    return _raw_tn(a, b)


def _dot_tn_fwd(a, b):
    return _raw_tn(a, b), (a, b)


def _dot_tn_bwd(res, g):
    a, b = res
    return _raw_nt(b, g), _raw_nn(a, g)


_dot_tn.defvjp(_dot_tn_fwd, _dot_tn_bwd)


def _tile(n, pref):
    if n <= pref:
        return n
    t = (pref // LANES) * LANES
    while t > LANES and n % t:
        t -= LANES
    assert n % t == 0, (n, pref)
    return t


def _params(sem):
    return pltpu.CompilerParams(dimension_semantics=sem, vmem_limit_bytes=VMEM_LIMIT)


def _rowcall(name, fn, rows, pars, row_outs, par_outs=(), tm=512, pin=None):
    if pin is not None:
        inner, pars = fn, list(pars) + [pin]
        fn = lambda *vals: inner(*vals[:-1])
    n_rows = rows[0][0].shape[0]
    tm = min(tm, n_rows)
    assert n_rows % tm == 0
    n_r, n_p, n_ro = len(rows), len(pars), len(row_outs)

    def body(*refs):
        vals = [r[...].astype(F32) for r in refs[:n_r + n_p]]
        outs = fn(*vals)
        o_refs = refs[n_r + n_p:n_r + n_p + n_ro]
        po_refs = refs[n_r + n_p + n_ro:]
        for o_ref, val in zip(o_refs, outs[:n_ro]):
            o_ref[...] = val.astype(o_ref.dtype)
        first = pl.program_id(0) == 0
        for po_ref, val in zip(po_refs, outs[n_ro:]):
            @pl.when(first)
            def _():
                po_ref[...] = val

            @pl.when(jnp.logical_not(first))
            def _():
                po_ref[...] += val

    def const_map(nd):
        return lambda i: (0,) * nd

    def row_spec(w, cb):
        return pl.BlockSpec((tm, w), lambda i: (i, cb))

    in_specs = [row_spec(w, cb) for _, w, cb in rows]
    in_specs += [pl.BlockSpec(p.shape, const_map(p.ndim)) for p in pars]
    out_specs = [pl.BlockSpec((tm, w), lambda i: (i, 0)) for w, _ in row_outs]
    out_specs += [pl.BlockSpec(tuple(s), const_map(len(s))) for s in par_outs]
    out_shape = [jax.ShapeDtypeStruct((n_rows, w), dt) for w, dt in row_outs]
    out_shape += [jax.ShapeDtypeStruct(tuple(s), F32) for s in par_outs]
    return pl.pallas_call(
        body, name=name, grid=(n_rows // tm,), in_specs=in_specs, out_specs=out_specs, out_shape=out_shape,
        compiler_params=_params(("arbitrary",) if par_outs else ("parallel",)),
    )(*[r[0] for r in rows], *pars)


def _vjp_of(fn, n_prim, n_out, n_par, n_pass=0):
    def bwd(*args):
        prim = args[:n_prim]
        cts = args[n_prim:n_prim + n_out]
        passes = args[n_prim + n_out:n_prim + n_out + 2 * n_pass]
        pars = args[n_prim + n_out + 2 * n_pass:]
        _, vjp = jax.vjp(fn, *prim, *pars)
        grads = vjp(tuple(cts))
        sums = tuple(passes[2 * i] + passes[2 * i + 1] for i in range(n_pass))
        return tuple(grads[:n_prim]) + sums + tuple(grads[n_prim:])
    return bwd


def _mm(name, a, b, mode="nn", extras=(), epi=None, out_dtypes=(F32,), a_pro=None, out_split=None,
        epi_pars=(), row_sum=False, pin=None, tm=1024, tn=1024, tk=1024):
    split = b.shape[0] if b.ndim == 3 else None
    b_rows, b_cols = b.shape[-2:]
    if mode == "nn":
        (m, k), n = a.shape, b_cols * (split or 1)
    elif mode == "nt":
        (m, k), n = a.shape, b_rows
        assert k == b_cols * (split or 1)
    else:
        assert split is None
        (k, m), n = a.shape, b_cols
    tm, tk = _tile(m, tm), _tile(k, tk)
    tn = _tile(n // out_split, tn) if out_split else _tile(n, tn)
    if split and mode == "nn":
        tn = _tile(b_cols, tn)
    if split and mode == "nt":
        tk = _tile(b_cols, tk)
    nk = k // tk
    raw = {"nn": _raw_nn, "nt": _raw_nt, "tn": _raw_tn}[mode]
    n_e, n_p, n_o = len(extras), len(epi_pars), len(out_dtypes)
    n_in = n_e + n_p + (0 if pin is None else 1)
    if epi is None:
        epi = lambda acc: (acc,)

    def body(a_ref, b_ref, *rest):
        e_refs, p_refs, o_refs = rest[:n_e], rest[n_e:n_e + n_p], rest[n_in:n_in + n_o]
        kk = pl.program_id(2)
        a_tile = a_ref[...] if a_pro is None else a_pro(a_ref[...].astype(F32))
        part = raw(a_tile, b_ref[...])

        def finish(total):
            res = epi(total, *[e[...].astype(F32) for e in e_refs], *[p[...] for p in p_refs])
            for o_ref, r in zip(o_refs, res):
                o_ref[...] = r.astype(o_ref.dtype)
            if row_sum:
                rest[n_in + n_o][...] = res[n_o]

        if nk == 1:
            finish(part)
            return
        acc = rest[-1]

        @pl.when(kk == 0)
        def _():
            acc[...] = part

        @pl.when((kk > 0) & (kk < nk - 1))
        def _():
            acc[...] += part

        @pl.when(kk == nk - 1)
        def _():
            finish(acc[...] + part)

    a_spec = pl.BlockSpec((tk, tm), lambda i, j, kk: (kk, i)) if mode == "tn" else pl.BlockSpec((tm, tk), lambda i, j, kk: (i, kk))
    if split and mode == "nn":
        per = b_cols // tn
        b_spec = pl.BlockSpec((None, tk, tn), lambda i, j, kk: (j // per, kk, j % per))
    elif split:
        per = b_cols // tk
        b_spec = pl.BlockSpec((None, tn, tk), lambda i, j, kk: (kk // per, j, kk % per))
    elif mode == "nt":
        b_spec = pl.BlockSpec((tn, tk), lambda i, j, kk: (j, kk))
    else:
        b_spec = pl.BlockSpec((tk, tn), lambda i, j, kk: (kk, j))
    mn_spec = pl.BlockSpec((tm, tn), lambda i, j, kk: (i, j))
    if out_split:
        assert not extras
        per_out = n // out_split // tn
        out_spec = pl.BlockSpec((None, tm, tn), lambda i, j, kk: (j // per_out, i, j % per_out))
        out_shapes = [jax.ShapeDtypeStruct((out_split, m, n // out_split), dt) for dt in out_dtypes]
    else:
        out_spec = mn_spec
        out_shapes = [jax.ShapeDtypeStruct((m, n), dt) for dt in out_dtypes]
    out_specs = [out_spec] * n_o
    if row_sum:
        out_specs.append(pl.BlockSpec((None, 1, tn), lambda i, j, kk: (i, 0, j)))
        out_shapes.append(jax.ShapeDtypeStruct((m // tm, 1, n), F32))
    in_specs = [a_spec, b_spec] + [mn_spec] * n_e
    in_specs += [pl.BlockSpec(p.shape, functools.partial(lambda i, j, kk, nd: (0,) * nd, nd=p.ndim)) for p in epi_pars]
    in_specs += [] if pin is None else [pl.BlockSpec(memory_space=pl.ANY)]
    outs = pl.pallas_call(
        body, name=name, grid=(m // tm, n // tn, nk), in_specs=in_specs, out_specs=out_specs, out_shape=out_shapes,
        scratch_shapes=[pltpu.VMEM((tm, tn), F32)] if nk > 1 else [],
        compiler_params=_params(("parallel", "parallel", "arbitrary")),
    )(a, b, *extras, *epi_pars, *([] if pin is None else [pin]))
    return outs[0] if len(outs) == 1 else outs


def _sigmoid(x):
    return jax.nn.sigmoid(x)


def _silu(x):
    return x * _sigmoid(x)


def _softplus(x):
    return jnp.maximum(x, 0.0) + jnp.log1p(jnp.exp(-jnp.abs(x)))


def _rmsnorm_fn(x, gain):
    return (x * lax.rsqrt(jnp.mean(x * x, axis=-1, keepdims=True) + EPS) * gain,)


def _head_norm(o, gain, n_heads):
    w = o.shape[-1] // n_heads
    parts = []
    for h in range(n_heads):
        oh = o[:, h * w:(h + 1) * w]
        parts.append(oh * lax.rsqrt(jnp.mean(oh * oh, axis=-1, keepdims=True) + EPS))
    return jnp.concatenate(parts, axis=-1) * gain


@jax.custom_jvp
def _neg_expm1(x):
    u = jnp.exp(x)
    is_one = u == 1.0
    return jnp.where(is_one, -x, (1.0 - u) * x / jnp.log(jnp.where(is_one, 2.0, u)))


@_neg_expm1.defjvp
def _neg_expm1_jvp(primals, tangents):
    (x,), (t,) = primals, tangents
    return _neg_expm1(x), -jnp.exp(x) * t


def _rg_gates_fn(xc, wa, wx, ba, bx, lam):
    outs = []
    for d in range(2):
        r = _sigmoid(_dot_nn(xc, wa[d]) + ba[d:d + 1])
        i = _sigmoid(_dot_nn(xc, wx[d]) + bx[d:d + 1])
        log_a = -RG_C * r * _softplus(-lam[d:d + 1])
        outs.append(jnp.exp(log_a))
        outs.append(jnp.sqrt(_neg_expm1(2.0 * log_a)) * (i * xc))
    return tuple(outs)


def _hg_pre_fn(q, f_f, f_b, logits):
    mx = jnp.maximum(logits[0:1], logits[1:2])
    e0 = jnp.exp(logits[0:1] - mx)
    e1 = jnp.exp(logits[1:2] - mx)
    lb = e0 / (e0 + e1)
    outs = [_silu(q)]
    for f in (f_f, f_b):
        outs.append((1.0 - lb) * _sigmoid(-f))
        outs.append(jnp.log(lb + (1.0 - lb) * _sigmoid(f)))
    return tuple(outs)


def _post0_fn(hs, ga, o, g, gain):
    ya = hs * jax.nn.gelu(ga, approximate=True)
    yb = _head_norm(o, gain, 4) * _silu(g)
    return (jnp.concatenate([ya, yb], axis=-1),)


def _post0_fwd_fn(h_f, h_b, ga, o_f, o_b, g, gain):
    return _post0_fn(h_f + h_b, ga, o_f + o_b, g, gain)


def _post0_bwd_fn(h_f, h_b, ga, o_f, o_b, g, dmix, gain):
    _, vjp = jax.vjp(_post0_fn, h_f + h_b, ga, o_f + o_b, g, gain)
    return vjp((dmix,))


def _gla_pre_fn(q, lr, w_up, b_gate):
    outs = [q * (128.0 ** -0.5)]
    for d in range(2):
        z = _dot_nn(lr, w_up[d]) + b_gate[d:d + 1]
        outs.append(-_softplus(-z) * (1.0 / 16.0))
    return tuple(outs)


def _gla_post_fn(o, r, gain):
    return (_head_norm(o, gain, 4) * _silu(r),)


def _gla_post_fwd_fn(o_f, o_b, r, gain):
    return _gla_post_fn(o_f + o_b, r, gain)


def _gla_post_bwd_fn(o_f, o_b, r, dmix, gain):
    _, vjp = jax.vjp(_gla_post_fn, o_f + o_b, r, gain)
    return vjp((dmix,))


def _relu2_bwd_epi(acc, hid):
    return (acc * 2.0 * jnp.maximum(hid, 0.0),)


def _relu2(x):
    r = jnp.maximum(x, 0.0)
    return r * r


def _add_epi(acc, res):
    return (acc + res,)


def _loss_head_fn(h, target, gain):
    def f(h, gain):
        y = _rmsnorm_fn(h, gain)[0]
        err = y - target
        return 0.5 * jnp.sum(jnp.mean(err * err, axis=-1, keepdims=True))
    loss, (dh, dgain) = jax.value_and_grad(f, argnums=(0, 1))(h, gain)
    return dh, jnp.full((1, LANES), loss, F32), dgain


def _adam_fn(w, g, m, v):
    m2 = ADAM_B1 * m + (1.0 - ADAM_B1) * g
    v2 = ADAM_B2 * v + (1.0 - ADAM_B2) * (g * g)
    m_hat = m2 / (1.0 - ADAM_B1 ** ADAM_STEP)
    v_hat = v2 / (1.0 - ADAM_B2 ** ADAM_STEP)
    delta = -ADAM_LR * (m_hat / (jnp.sqrt(v_hat) + ADAM_EPS) + ADAM_WD * w)
    return delta, m2, v2


def _shifted(x, t_idx, off):
    n = x.shape[0]
    rolled = pltpu.roll(x, (-off) % n, 0)
    valid = (t_idx + off >= 0) & (t_idx + off < n)
    return jnp.where(valid, rolled, 0.0)


def _conv_fwd(name, src, colblock, w, b):
    n_rows, width = src.shape[0], w.shape[1]

    def body(x_ref, w_ref, b_ref, o_ref):
        x = x_ref[...]
        t_idx = lax.broadcasted_iota(jnp.int32, x.shape, 0)
        acc = b_ref[...] + w_ref[2:3, :] * x
        acc += w_ref[0:1, :] * _shifted(x, t_idx, -2)
        acc += w_ref[1:2, :] * _shifted(x, t_idx, -1)
        acc += w_ref[3:4, :] * _shifted(x, t_idx, 1)
        o_ref[...] = acc

    nb = width // LANES
    return pl.pallas_call(
        body, name=name, grid=(nb,),
        in_specs=[pl.BlockSpec((n_rows, LANES), lambda j: (0, colblock * nb + j)),
                  pl.BlockSpec((4, LANES), lambda j: (0, j)), pl.BlockSpec((1, LANES), lambda j: (0, j))],
        out_specs=pl.BlockSpec((n_rows, LANES), lambda j: (0, j)),
        out_shape=jax.ShapeDtypeStruct((n_rows, width), F32),
        compiler_params=_params(("parallel",)),
    )(src, w, b)


def _conv_bwd(name, src, colblock, w, d):
    n_rows, width = src.shape[0], w.shape[1]

    def body(x_ref, w_ref, d_ref, dx_ref, dw_ref, db_ref):
        x = x_ref[...]
        g = d_ref[...]
        t_idx = lax.broadcasted_iota(jnp.int32, x.shape, 0)
        dx = w_ref[2:3, :] * g
        dx += w_ref[0:1, :] * _shifted(g, t_idx, 2)
        dx += w_ref[1:2, :] * _shifted(g, t_idx, 1)
        dx += w_ref[3:4, :] * _shifted(g, t_idx, -1)
        dx_ref[...] = dx.astype(dx_ref.dtype)
        dw_ref[0:1, :] = jnp.sum(g * _shifted(x, t_idx, -2), axis=0, keepdims=True)
        dw_ref[1:2, :] = jnp.sum(g * _shifted(x, t_idx, -1), axis=0, keepdims=True)
        dw_ref[2:3, :] = jnp.sum(g * x, axis=0, keepdims=True)
        dw_ref[3:4, :] = jnp.sum(g * _shifted(x, t_idx, 1), axis=0, keepdims=True)
        db_ref[...] = jnp.sum(g, axis=0, keepdims=True)

    nb = width // LANES
    return pl.pallas_call(
        body, name=name, grid=(nb,),
        in_specs=[pl.BlockSpec((n_rows, LANES), lambda j: (0, colblock * nb + j)),
                  pl.BlockSpec((4, LANES), lambda j: (0, j)),
                  pl.BlockSpec((n_rows, LANES), lambda j: (0, j))],
        out_specs=[pl.BlockSpec((n_rows, LANES), lambda j: (0, j)), pl.BlockSpec((4, LANES), lambda j: (0, j)),
                   pl.BlockSpec((1, LANES), lambda j: (0, j))],
        out_shape=[jax.ShapeDtypeStruct((n_rows, width), BF16), jax.ShapeDtypeStruct((4, width), F32),
                   jax.ShapeDtypeStruct((1, width), F32)],
        compiler_params=_params(("parallel",)),
    )(src, w, d)


SUBLANES = 8
SCAN_UNROLL = 8


def _shift_rows(x, d, fill):
    n = x.shape[0]
    t = lax.broadcasted_iota(jnp.int32, x.shape, 0)
    valid = (t >= d) if d > 0 else (t < n + d)
    return jnp.where(valid, pltpu.roll(x, d % n, 0), fill)


def _tile_scan(a, u, reverse):
    d = 1
    while d < a.shape[0]:
        s = -d if reverse else d
        a_sh, u_sh = _shift_rows(a, s, 1.0), _shift_rows(u, s, 0.0)
        u = u + a * u_sh
        a = a * a_sh
        d *= 2
    return a, u


def _edge_row(x, reverse):
    return x[0:1, :] if reverse else x[SUBLANES - 1:SUBLANES, :]


def _scan_specs(n_rows, n):
    return [pl.BlockSpec((n_rows, LANES), lambda j: (0, j))] * n


def _scan_tile(a_ref, u_ref, h_ref, i, carry, reverse):
    n_tiles = a_ref.shape[0] // SUBLANES
    tile = (n_tiles - 1 - i) if reverse else i
    rows = pl.ds(pl.multiple_of(tile * SUBLANES, SUBLANES), SUBLANES)
    acc_a, acc_u = _tile_scan(a_ref[rows, :], u_ref[rows, :], reverse)
    h = acc_u + acc_a * carry
    h_ref[rows, :] = h
    return _edge_row(h, reverse)


def _scan_fwd(name, a_f, u_f, a_b, u_b):
    n_rows, width = a_f.shape

    def body(af_ref, uf_ref, ab_ref, ub_ref, hf_ref, hb_ref):
        def step(i, carry):
            return (_scan_tile(af_ref, uf_ref, hf_ref, i, carry[0], False),
                    _scan_tile(ab_ref, ub_ref, hb_ref, i, carry[1], True))
        zero = jnp.zeros((1, LANES), F32)
        lax.fori_loop(0, n_rows // SUBLANES, step, (zero, zero), unroll=SCAN_UNROLL)

    return pl.pallas_call(
        body, name=name, grid=(width // LANES,), in_specs=_scan_specs(n_rows, 4), out_specs=_scan_specs(n_rows, 2),
        out_shape=[jax.ShapeDtypeStruct((n_rows, width), F32)] * 2, compiler_params=_params(("parallel",)),
    )(a_f, u_f, a_b, u_b)


def _scan_bwd_tile(a_ref, h_ref, dh_ref, du_ref, da_ref, i, carry, reverse):
    n_rows = a_ref.shape[0]
    n_tiles = n_rows // SUBLANES
    against = not reverse
    one = -1 if against else 1
    g_in, a_edge = carry
    tile = (n_tiles - 1 - i) if against else i
    start = pl.multiple_of(tile * SUBLANES, SUBLANES)
    rows = pl.ds(start, SUBLANES)
    a_tile = a_ref[rows, :]
    coeff = _shift_rows(a_tile, one, a_edge)
    acc_a, acc_u = _tile_scan(coeff, dh_ref[rows, :], against)
    g = acc_u + acc_a * g_in
    du_ref[rows, :] = g
    outside = (start + SUBLANES) if reverse else (start - 1)
    inside = (outside >= 0) & (outside < n_rows)
    h_edge = jnp.where(inside, h_ref[pl.ds(jnp.clip(outside, 0, n_rows - 1), 1), :], 0.0)
    da_ref[rows, :] = g * _shift_rows(h_ref[rows, :], -one, h_edge)
    return _edge_row(g, against), _edge_row(a_tile, against)


def _scan_bwd(name, a_f, h_f, a_b, h_b, dh):
    n_rows, width = a_f.shape

    def body(af_ref, hf_ref, ab_ref, hb_ref, dh_ref, duf_ref, daf_ref, dub_ref, dab_ref):
        def step(i, carry):
            return (_scan_bwd_tile(af_ref, hf_ref, dh_ref, duf_ref, daf_ref, i, carry[0], False),
                    _scan_bwd_tile(ab_ref, hb_ref, dh_ref, dub_ref, dab_ref, i, carry[1], True))
        zero = jnp.zeros((1, LANES), F32)
        lax.fori_loop(0, n_rows // SUBLANES, step, ((zero, zero), (zero, zero)), unroll=SCAN_UNROLL)

    return pl.pallas_call(
        body, name=name, grid=(width // LANES,), in_specs=_scan_specs(n_rows, 5), out_specs=_scan_specs(n_rows, 4),
        out_shape=[jax.ShapeDtypeStruct((n_rows, width), F32)] * 4, compiler_params=_params(("parallel",)),
    )(a_f, h_f, a_b, h_b, dh)


def _tri_mask(c, reverse):
    row = lax.broadcasted_iota(jnp.int32, (c, c), 0)
    col = lax.broadcasted_iota(jnp.int32, (c, c), 1)
    return (col >= row) if reverse else (col <= row)


def _cumsum_rows(x, reverse):
    tri = _tri_mask(x.shape[0], reverse).astype(BF16)
    hi = x.astype(BF16)
    rest = x - hi.astype(F32)
    mid = rest.astype(BF16)
    lo = (rest - mid.astype(F32)).astype(BF16)
    return _raw_nn(tri, hi) + _raw_nn(tri, mid) + _raw_nn(tri, lo)


@functools.partial(jax.custom_vjp, nondiff_argnums=(1,))
def _cumsum(x, reverse):
    return _cumsum_rows(x, reverse)


def _cumsum_fwd(x, reverse):
    return _cumsum_rows(x, reverse), None


def _cumsum_bwd(reverse, _, g):
    return (_cumsum_rows(g, not reverse),)


_cumsum.defvjp(_cumsum_fwd, _cumsum_bwd)


def _chunks_fn(qs, ks, vs, lfs, sts, reverses):
    n, c = len(qs), qs[0].shape[0]
    every = range(n)
    tris = [_tri_mask(c, r) for r in reverses]
    cums = [_cumsum(lfs[i], reverses[i]) for i in every]
    rid = lax.broadcasted_iota(jnp.int32, cums[0].shape, 0)

    def pick(cum, r):
        return jnp.sum(jnp.where(rid == r, cum, 0.0), axis=0, keepdims=True)

    refs = [pick(cums[i], (c - 1 - c // 2) if reverses[i] else c // 2) for i in every]
    lasts = [pick(cums[i], 0 if reverses[i] else c - 1) for i in every]
    q_in = [qs[i] * jnp.exp(cums[i] - refs[i]) for i in every]
    k_in = [ks[i] * jnp.exp(refs[i] - cums[i]) for i in every]
    scores = [jnp.where(tris[i], _dot_nt(q_in[i], k_in[i]), 0.0) for i in every]
    o_intra = [_dot_nn(scores[i], vs[i]) for i in every]
    q_out = [qs[i] * jnp.exp(cums[i]) for i in every]
    o_inter = [_dot_nt(q_out[i], sts[i]) for i in every]
    k_state = [ks[i] * jnp.exp(lasts[i] - cums[i]) for i in every]
    upd = [_dot_tn(vs[i], k_state[i]) for i in every]
    st_new = [sts[i] * jnp.exp(lasts[i]) + upd[i] for i in every]
    return [o_intra[i] + o_inter[i] for i in every], st_new


def _attn_fwd(name, q, k_f, k_b, v, lf_f, lf_b, n_heads, dk, dv):
    n_rows = q[0].shape[0]
    n_chunks = n_rows // CHUNK
    n_steps = n_chunks // ATTN_SUB
    wk, wv = n_heads * dk, n_heads * dv

    def spec(width, off, rev):
        return pl.BlockSpec((CHUNK * ATTN_SUB, width), lambda n: ((n_steps - 1 - n) if rev else n, off))

    def sspec(rev):
        return pl.BlockSpec((ATTN_SUB, n_heads, dv, dk), lambda n: ((n_steps - 1 - n) if rev else n, 0, 0, 0))

    def body(qf, kf, vf, lff, qb, kb, vb, lfb, of_ref, ob_ref, sf_ref, sb_ref, st):
        @pl.when(pl.program_id(0) == 0)
        def _():
            st[...] = jnp.zeros_like(st)

        ins = ((qf, kf, vf, lff), (qb, kb, vb, lfb))
        chains = [(d, h) for d in range(2) for h in range(n_heads)]
        ck = [slice(h * dk, (h + 1) * dk) for h in range(n_heads)]
        cv = [slice(h * dv, (h + 1) * dv) for h in range(n_heads)]
        sts = [st[d, h] for d, h in chains]
        done = []
        for sub in range(ATTN_SUB):
            local = (sub, ATTN_SUB - 1 - sub)
            rows = [slice(local[d] * CHUNK, (local[d] + 1) * CHUNK) for d in range(2)]
            qs = [ins[d][0][rows[d], ck[h]] for d, h in chains]
            ks = [ins[d][1][rows[d], ck[h]] for d, h in chains]
            vs = [ins[d][2][rows[d], cv[h]] for d, h in chains]
            lfs = [ins[d][3][rows[d], ck[h]] for d, h in chains]
            os_, st_new = _chunks_fn(qs, ks, vs, lfs, sts, [d == 1 for d, _ in chains])
            done.append((local, rows, sts, os_))
            sts = st_new
        for local, rows, entered, os_ in done:
            for i, (d, h) in enumerate(chains):
                (sf_ref, sb_ref)[d][local[d], h] = entered[i].astype(BF16)
                (of_ref, ob_ref)[d][rows[d], cv[h]] = os_[i]
        for i, (d, h) in enumerate(chains):
            st[d, h] = sts[i]

    in_specs = [spec(wk, q[1], False), spec(wk, k_f[1], False), spec(wv, v[1], False), spec(wk, lf_f[1], False),
                spec(wk, q[1], True), spec(wk, k_b[1], True), spec(wv, v[1], True), spec(wk, lf_b[1], True)]
    return pl.pallas_call(
        body, name=name, grid=(n_steps,), in_specs=in_specs,
        out_specs=[spec(wv, 0, False), spec(wv, 0, True), sspec(False), sspec(True)],
        out_shape=[jax.ShapeDtypeStruct((n_rows, wv), F32)] * 2
        + [jax.ShapeDtypeStruct((n_chunks, n_heads, dv, dk), BF16)] * 2,
        scratch_shapes=[pltpu.VMEM((2, n_heads, dv, dk), F32)],
        compiler_params=_params(("arbitrary",)),
    )(q[0], k_f[0], v[0], lf_f[0], q[0], k_b[0], v[0], lf_b[0])


def _attn_bwd(name, q, k_f, k_b, v, lf_f, lf_b, st_f, st_b, do, n_heads, dk, dv, out_dtype=F32):
    n_rows = q[0].shape[0]
    n_chunks = n_rows // CHUNK
    n_steps = n_chunks // ATTN_SUB
    wk, wv = n_heads * dk, n_heads * dv

    def spec(width, off, rev):
        return pl.BlockSpec((CHUNK * ATTN_SUB, width), lambda n: (n if rev else (n_steps - 1 - n), off))

    def sspec(rev):
        return pl.BlockSpec((ATTN_SUB, n_heads, dv, dk), lambda n: (n if rev else (n_steps - 1 - n), 0, 0, 0))

    def body(qf, kf, vf, lff, sf, dof, qb, kb, vb, lfb, sb, dob,
             dqf, dkf, dvf, dlff, dqb, dkb, dvb, dlfb, dst):
        @pl.when(pl.program_id(0) == 0)
        def _():
            dst[...] = jnp.zeros_like(dst)

        ins = ((qf, kf, vf, lff, sf, dof), (qb, kb, vb, lfb, sb, dob))
        outs = ((dqf, dkf, dvf, dlff), (dqb, dkb, dvb, dlfb))
        chains = [(d, h) for d in range(2) for h in range(n_heads)]
        ck = [slice(h * dk, (h + 1) * dk) for h in range(n_heads)]
        cv = [slice(h * dv, (h + 1) * dv) for h in range(n_heads)]
        fn = functools.partial(_chunks_fn, reverses=[d == 1 for d, _ in chains])
        dsts = [dst[d, h] for d, h in chains]
        done = []
        for sub in range(ATTN_SUB):
            local = (ATTN_SUB - 1 - sub, sub)
            rows = [slice(local[d] * CHUNK, (local[d] + 1) * CHUNK) for d in range(2)]
            qs = [ins[d][0][rows[d], ck[h]] for d, h in chains]
            ks = [ins[d][1][rows[d], ck[h]] for d, h in chains]
            vs = [ins[d][2][rows[d], cv[h]] for d, h in chains]
            lfs = [ins[d][3][rows[d], ck[h]] for d, h in chains]
            sts = [ins[d][4][local[d], h].astype(F32) for d, h in chains]
            dos = [ins[d][5][rows[d], cv[h]] for d, h in chains]
            _, vjp = jax.vjp(fn, qs, ks, vs, lfs, sts)
            dqs, dks, dvs, dlfs, dsts = vjp((dos, dsts))
            done.append((rows, dqs, dks, dvs, dlfs))
        for rows, dqs, dks, dvs, dlfs in done:
            for i, (d, h) in enumerate(chains):
                dq_r, dk_r, dv_r, dlf_r = outs[d]
                dq_r[rows[d], ck[h]] = dqs[i].astype(dq_r.dtype)
                dk_r[rows[d], ck[h]] = dks[i].astype(dk_r.dtype)
                dv_r[rows[d], cv[h]] = dvs[i].astype(dv_r.dtype)
                dlf_r[rows[d], ck[h]] = dlfs[i].astype(dlf_r.dtype)
        for i, (d, h) in enumerate(chains):
            dst[d, h] = dsts[i]

    def dir_specs(kk, lf, rev):
        return [spec(wk, q[1], rev), spec(wk, kk[1], rev), spec(wv, v[1], rev), spec(wk, lf[1], rev), sspec(rev),
                spec(wv, 0, rev)]

    def dir_out_specs(rev):
        return [spec(wk, 0, rev), spec(wk, 0, rev), spec(wv, 0, rev), spec(wk, 0, rev)]

    shapes = [jax.ShapeDtypeStruct((n_rows, wk), out_dtype), jax.ShapeDtypeStruct((n_rows, wk), out_dtype),
              jax.ShapeDtypeStruct((n_rows, wv), out_dtype), jax.ShapeDtypeStruct((n_rows, wk), F32)]
    outs = pl.pallas_call(
        body, name=name, grid=(n_steps,), in_specs=dir_specs(k_f, lf_f, False) + dir_specs(k_b, lf_b, True),
        out_specs=dir_out_specs(False) + dir_out_specs(True), out_shape=shapes + shapes,
        scratch_shapes=[pltpu.VMEM((2, n_heads, dv, dk), F32)],
        compiler_params=_params(("arbitrary",)),
    )(q[0], k_f[0], v[0], lf_f[0], st_f, do, q[0], k_b[0], v[0], lf_b[0], st_b, do)
    return outs[:4], outs[4:]


def _row2(v):
    return v.reshape(1, -1)


def _mlp_fwd(tag, h, gain, w1, w2):
    y = _rowcall(f"{tag}_norm", _rmsnorm_fn, [(h, h.shape[1], 0)], [gain], [(h.shape[1], BF16)], tm=512)[0]
    hid = _mm(f"{tag}_up", y, w1, out_dtypes=(BF16,))
    h_out = _mm(f"{tag}_down", hid, w2, a_pro=_relu2, extras=(h,), epi=_add_epi)
    return h_out, (y, hid)


def _dw(name, a, b, **kw):
    return _mm(name, a, b, mode="tn", epi=lambda acc: (acc, acc), out_dtypes=(F32, BF16), **kw)


def _mlp_bwd(tag, h, gain, w1, w2, saved, dh_out):
    y, hid = saved
    dhid = _mm(f"{tag}_dact", dh_out, w2, mode="nt", extras=(hid,), epi=_relu2_bwd_epi, out_dtypes=(BF16,))
    dw2 = _dw(f"{tag}_dw2", hid, dh_out, a_pro=_relu2)
    dw1 = _dw(f"{tag}_dw1", y, dhid, out_split=N_CHIPS)
    dh, dgain = _dy_norm_bwd(f"{tag}_dy", dhid, w1, h, gain, dh_out)
    return dh, dgain, dw1, dw2


def _dy_norm_bwd(name, dz, w, h, gain, dres, pin=None, **tiles):
    def epi(dy, h_tile, dres_tile, gain_row):
        _, vjp = jax.vjp(lambda u, v: _rmsnorm_fn(u, v)[0], h_tile, gain_row)
        dh, dgain = vjp(dy)
        return dh + dres_tile, dgain

    assert h.shape[1] <= 1024
    tiles.setdefault("tm", 1024)
    dh, dgain_parts = _mm(name, dz, w, mode="nt", extras=(h, dres), epi=epi, epi_pars=(gain,), row_sum=True,
                          pin=pin, **tiles)
    return dh, jnp.sum(dgain_parts, axis=0)


def _local_step(x, target, w, pin=None, late=None, emit=None):
    g = {}
    d_model = x.shape[1]
    rg_w = hg_w = d_model // 2
    pins = []

    def send_off(tag, pairs):
        if emit is not None:
            pins.append(emit(tag, [p[0] for p in pairs], [p[1] for p in pairs]))

    def both(fn, pair):
        return [fn(t) for t in pair]

    def chip_major(t):
        return t.reshape(N_CHIPS, t.shape[0] // N_CHIPS, t.shape[1])

    h_a0 = x
    gain = _row2(w["norm_mix"][0])
    y0 = _rowcall("l0_norm", _rmsnorm_fn, [(h_a0, d_model, 0)], [gain], [(d_model, BF16)], tm=512, pin=pin)[0]
    proj0 = _mm("l0_in", y0, w["ab_w_in"])
    conv_w, conv_b = w["rg_conv_w"], _row2(w["rg_conv_b"])
    xc = _conv_fwd("rg_conv", proj0, 0, conv_w, conv_b)
    gate_pars = [w["rg_wa_bd"], w["rg_wx_bd"], w["rg_b_a"], w["rg_b_x"], w["rg_lambda"]]
    a_f, u_f, a_b, u_b = _rowcall("rg_gates", _rg_gates_fn, [(xc, rg_w, 0)], gate_pars, [(rg_w, F32)] * 4)
    hs_f, hs_b = _scan_fwd("rg_scan", a_f, u_f, a_b, u_b)
    hg_rows = [(proj0, hg_w, 2), (proj0, hg_w, 3), (proj0, hg_w, 4)]
    qh, k_f, lf_f, k_b, lf_b = _rowcall("hg_pre", _hg_pre_fn, hg_rows, [w["hg_lb_logits"]], [(hg_w, F32)] * 5)
    iv = (proj0, 5)
    o_f, o_b, st_f, st_b = _attn_fwd("hg_attn", (qh, 0), (k_f, 0), (k_b, 0), iv, (lf_f, 0), (lf_b, 0), 4, 128, 128)
    post0_rows = [(hs_f, rg_w, 0), (hs_b, rg_w, 0), (proj0, rg_w, 1), (o_f, hg_w, 0), (o_b, hg_w, 0), (proj0, hg_w, 6)]
    hg_gain = _row2(w["hg_norm"])
    mix_in0 = _rowcall("l0_post", _post0_fwd_fn, post0_rows, [hg_gain], [(d_model, BF16)])[0]
    if late is not None:
        w = {**w, **late(mix_in0)}
    h_b0 = _mm("l0_out", mix_in0, w["ab_w_out"], extras=(h_a0,), epi=_add_epi)
    h_c0, mlp0 = _mlp_fwd("mlp0", h_b0, _row2(w["norm_mlp"][0]), w["mlp_w1"][0], w["mlp_w2"][0])

    h_a1 = h_c0
    gain1 = _row2(w["norm_mix"][1])
    y1 = _rowcall("l1_norm", _rmsnorm_fn, [(h_a1, d_model, 0)], [gain1], [(d_model, BF16)], tm=512)[0]
    proj1 = _mm("l1_in", y1, w["gla_w_in_pad"], tn=640)
    gla_pars = [w["gla_w_up_pad"], w["gla_b_gate"]]
    gq, glf_f, glf_b = _rowcall("gla_pre", _gla_pre_fn, [(proj1, 512, 0), (proj1, LANES, 24)], gla_pars, [(512, F32)] * 3)
    gk, gv = (proj1, 1), (proj1, 1)
    go_f, go_b, gst_f, gst_b = _attn_fwd("gla_attn", (gq, 0), gk, gk, gv, (glf_f, 0), (glf_b, 0), 4, 128, 256)
    gla_gain = _row2(w["gla_norm"])
    post1_rows = [(go_f, d_model, 0), (go_b, d_model, 0), (proj1, d_model, 2)]
    mix_in1 = _rowcall("l1_post", _gla_post_fwd_fn, post1_rows, [gla_gain], [(d_model, BF16)])[0]
    h_b1 = _mm("l1_out", mix_in1, w["gla_w_out"], extras=(h_a1,), epi=_add_epi)
    h_c1, mlp1 = _mlp_fwd("mlp1", h_b1, _row2(w["norm_mlp"][1]), w["mlp_w1"][1], w["mlp_w2"][1])

    dh, loss, g["norm_final"] = _rowcall(
        "loss_head", _loss_head_fn, [(h_c1, d_model, 0), (target, d_model, 0)], [_row2(w["norm_final"])],
        [(d_model, F32)], [(1, LANES), (1, d_model)], tm=512)

    dh, g_nmlp1, g_w1_1, g_w2_1 = _mlp_bwd("mlp1", h_b1, _row2(w["norm_mlp"][1]), w["mlp_w1"][1], w["mlp_w2"][1], mlp1, dh)
    send_off("mlp1", [g_w1_1, both(chip_major, g_w2_1)])
    dmix1 = _mm("l1_dout", dh, w["gla_w_out"], mode="nt")
    g_gla_out = _dw("l1_dwout", mix_in1, dh)
    g["gla_w_out"] = g_gla_out[0]
    dgo, dr, g["gla_norm"] = _rowcall(
        "l1_dpost", _gla_post_bwd_fn, post1_rows + [(dmix1, d_model, 0)], [gla_gain],
        [(d_model, F32), (d_model, BF16)], [(1, d_model)], pin=pins.pop() if pins else None)
    (dq_f, dk_f, dv_f, dlf_f), (dq_b, dk_b, dv_b, dlf_b) = _attn_bwd(
        "gla_dattn", (gq, 0), gk, gk, gv, (glf_f, 0), (glf_b, 0), gst_f, gst_b, dgo, 4, 128, 256)

    def gla_pre_bwd(q, lr, dq1, dq2, dlf1, dlf2, dk1, dk2, dv1, dv2, w_up, b_gate):
        dlr = jnp.zeros_like(lr)
        dws, dbs = [], []
        for d, dlf in enumerate((dlf1, dlf2)):
            z = _raw_nn(lr, w_up[d]) + b_gate[d:d + 1]
            dz = dlf * _sigmoid(-z) * (1.0 / 16.0)
            dlr = dlr + _raw_nt(dz, w_up[d])
            dws.append(_raw_tn(dz, lr))
            dbs.append(jnp.sum(dz, axis=0, keepdims=True))
        return ((dq1 + dq2) * (128.0 ** -0.5), dk1 + dk2, dv1 + dv2, dlr, dws[0], dws[1], dbs[0], dbs[1])

    rows = [(proj1, 512, 0), (proj1, LANES, 24), (dq_f, 512, 0), (dq_b, 512, 0), (dlf_f, 512, 0), (dlf_b, 512, 0),
            (dk_f, 512, 0), (dk_b, 512, 0), (dv_f, d_model, 0), (dv_b, d_model, 0)]
    dq, dk, dv, dlr, dwt_f, dwt_b, db_f, db_b = _rowcall(
        "gla_dpre", gla_pre_bwd, rows, gla_pars, [(512, BF16), (512, BF16), (d_model, BF16), (LANES, BF16)],
        [(512, LANES), (512, LANES), (1, 512), (1, 512)])
    g["gla_w_up_pad"] = jnp.stack([dwt_f.T, dwt_b.T])
    g["gla_b_gate"] = jnp.concatenate([db_f, db_b], axis=0)
    dproj1 = jnp.concatenate([dq, dk, dv, dr, dlr], axis=1)
    g_gla_in = both(lambda t: _split_chips(t[:, :GLA_IN_WIDTH], 1), _dw("l1_dwin", y1, dproj1, tn=640))
    g["gla_w_in"] = g_gla_in[0]
    send_off("gla", [g_gla_in, both(chip_major, g_gla_out)])
    dh, g_nmix1 = _dy_norm_bwd("l1_dy", dproj1, w["gla_w_in_pad"], h_a1, gain1, dh,
                               pin=pins.pop() if pins else None, tk=640)

    dh, g_nmlp0, g_w1_0, g_w2_0 = _mlp_bwd("mlp0", h_b0, _row2(w["norm_mlp"][0]), w["mlp_w1"][0], w["mlp_w2"][0], mlp0, dh)
    g_ab_out = _dw("l0_dwout", mix_in0, dh)
    g["ab_w_out"] = g_ab_out[0]
    send_off("mlp0", [g_w1_0, both(chip_major, g_w2_0), both(chip_major, g_ab_out)])
    dmix0 = _mm("l0_dout", dh, w["ab_w_out"], mode="nt")
    dhs, dga, do, dg, g["hg_norm"] = _rowcall(
        "l0_dpost", _post0_bwd_fn, post0_rows + [(dmix0, d_model, 0)], [hg_gain],
        [(rg_w, F32), (rg_w, BF16), (hg_w, F32), (hg_w, BF16)], [(1, hg_w)], pin=pins.pop() if pins else None)
    (dqh_f, dk_f, div_f, dlf_f), (dqh_b, dk_b, div_b, dlf_b) = _attn_bwd(
        "hg_dattn", (qh, 0), (k_f, 0), (k_b, 0), iv, (lf_f, 0), (lf_b, 0), st_f, st_b, do, 4, 128, 128)

    def hg_pre_bwd(q, f_f, f_b, dq1, dq2, dk1, dlf1, dk2, dlf2, dv1, dv2, logits):
        _, vjp = jax.vjp(_hg_pre_fn, q, f_f, f_b, logits)
        dq, df_f, df_b, dlogits = vjp((dq1 + dq2, dk1, dlf1, dk2, dlf2))
        return dq, df_f, df_b, dv1 + dv2, dlogits

    rows = hg_rows + [(t, hg_w, 0) for t in (dqh_f, dqh_b, dk_f, dlf_f, dk_b, dlf_b, div_f, div_b)]
    dq, df_f, df_b, div, g["hg_lb_logits"] = _rowcall(
        "hg_dpre", hg_pre_bwd, rows, [w["hg_lb_logits"]], [(hg_w, BF16)] * 4, [(2, hg_w)])
    du_f, da_f, du_b, da_b = _scan_bwd("rg_dscan", a_f, hs_f, a_b, hs_b, dhs)
    gates_bwd = _vjp_of(_rg_gates_fn, 1, 4, 5)
    rows = [(xc, rg_w, 0), (da_f, rg_w, 0), (du_f, rg_w, 0), (da_b, rg_w, 0), (du_b, rg_w, 0)]
    dxc, g["rg_wa_bd"], g["rg_wx_bd"], g["rg_b_a"], g["rg_b_x"], g["rg_lambda"] = _rowcall(
        "rg_dgates", gates_bwd, rows, gate_pars, [(rg_w, F32)],
        [(2, rg_w, rg_w), (2, rg_w, rg_w), (2, rg_w), (2, rg_w), (2, rg_w)])
    dxa, g["rg_conv_w"], g["rg_conv_b"] = _conv_bwd("rg_dconv", proj0, 0, conv_w, dxc)
    dproj0 = jnp.concatenate([dxa, dga, dq, df_f, df_b, div, dg], axis=1)
    g_ab_in = _dw("l0_dwin", y0, dproj0, out_split=N_CHIPS)
    g["ab_w_in"] = g_ab_in[0]
    send_off("ab", [g_ab_in])
    grad_x, g_nmix0 = _dy_norm_bwd("l0_dy", dproj0, w["ab_w_in"], h_a0, gain, dh, pin=pins.pop() if pins else None)

    g["norm_mix"] = jnp.concatenate([g_nmix0, g_nmix1], axis=0)
    g["norm_mlp"] = jnp.concatenate([g_nmlp0, g_nmlp1], axis=0)
    g["mlp_w1"] = [g_w1_0[0], g_w1_1[0]]
    g["mlp_w2"] = [g_w2_0[0], g_w2_1[0]]
    return loss, grad_x, g


def _block_diag(w):
    d, g, n, _ = w.shape
    eye = jnp.eye(g, dtype=w.dtype)
    return (w[:, :, :, None, :] * eye[None, :, None, :, None]).reshape(d, g * n, g * n)


def _block_diag_extract(wbd, g):
    d, gn, _ = wbd.shape
    n = gn // g
    blocks = wbd.reshape(d, g, n, g, n)
    return jnp.stack([blocks[:, i, :, i, :] for i in range(g)], axis=1)


def _prepare_weights(big, full):
    w = {k: full[k] for k in ("norm_mix", "norm_mlp", "norm_final", "hg_lb_logits")}
    for k in ("rg_conv_w", "rg_conv_b", "rg_b_a", "rg_b_x", "rg_lambda", "hg_norm", "gla_b_gate", "gla_norm"):
        w[k] = full[k][0]
    w["rg_wa_bd"] = _block_diag(full["rg_w_a"][0])
    w["rg_wx_bd"] = _block_diag(full["rg_w_x"][0])
    up = full["gla_w_gate_up"][0]
    rank = up.shape[1]
    pad = jnp.zeros((2, LANES, up.shape[2]), F32)
    w["gla_w_up_pad"] = pad.at[0, 0:rank].set(up[0]).at[1, rank:2 * rank].set(up[1])
    w.update(_prepare_matrices(big))
    return w


def _prepare_matrices(big):
    w = {}
    if "mlp_w1" in big:
        w["mlp_w1"] = list(big["mlp_w1"])
        w["mlp_w2"] = [t.reshape(-1, t.shape[-1]) for t in big["mlp_w2"]]
    if "ab_w_in" in big:
        w["ab_w_in"] = big["ab_w_in"]
    if "ab_w_out" in big:
        w["ab_w_out"] = big["ab_w_out"].reshape(-1, big["ab_w_out"].shape[-1])
    if "gla_w_in" in big:
        w["gla_w_out"] = big["gla_w_out"].reshape(-1, big["gla_w_out"].shape[-1])
        gla_in = _join_chips(big["gla_w_in"], 1)
        w["gla_w_in_pad"] = jnp.pad(gla_in, ((0, 0), (0, GLA_IN_PAD - gla_in.shape[1])))
    return w


def _finish_grads(g, rank=16, rg_blocks=8):
    def chip_major(t):
        return t.reshape(N_CHIPS, t.shape[0] // N_CHIPS, t.shape[1])

    big = {
        "mlp_w1": list(g["mlp_w1"]), "mlp_w2": [chip_major(t) for t in g["mlp_w2"]],
        "ab_w_in": g["ab_w_in"], "ab_w_out": chip_major(g["ab_w_out"]),
        "gla_w_in": g["gla_w_in"], "gla_w_out": chip_major(g["gla_w_out"]),
    }
    small = {
        "norm_mix": g["norm_mix"], "norm_mlp": g["norm_mlp"], "norm_final": g["norm_final"][0],
        "rg_conv_w": g["rg_conv_w"][None], "rg_conv_b": g["rg_conv_b"],
        "rg_w_a": _block_diag_extract(g["rg_wa_bd"], rg_blocks)[None], "rg_b_a": g["rg_b_a"][None],
        "rg_w_x": _block_diag_extract(g["rg_wx_bd"], rg_blocks)[None], "rg_b_x": g["rg_b_x"][None],
        "rg_lambda": g["rg_lambda"][None], "hg_lb_logits": g["hg_lb_logits"], "hg_norm": g["hg_norm"],
        "gla_w_gate_up": jnp.stack([g["gla_w_up_pad"][0, 0:rank], g["gla_w_up_pad"][1, rank:2 * rank]])[None],
        "gla_b_gate": g["gla_b_gate"][None], "gla_norm": g["gla_norm"],
    }
    return big, small


MATRICES = (("mlp_w1", 0), ("mlp_w1", 1), ("mlp_w2", 0), ("mlp_w2", 1), ("ab_w_in", 0), ("ab_w_out", 0),
            ("gla_w_in", 0), ("gla_w_out", 0))
EARLY_MATRICES = ("ab_w_in",)
SMALL_SHARDED = ("rg_conv_w", "rg_b_a", "rg_b_x", "rg_lambda", "gla_w_gate_up", "gla_b_gate", "gla_norm")
SMALL_REPLICATED = ("norm_mix", "norm_mlp", "norm_final", "rg_conv_b", "rg_w_a", "rg_w_x", "hg_lb_logits", "hg_norm")
WEIGHTS = ("norm_mix", "norm_mlp", "norm_final", "mlp_w1", "mlp_w2", "ab_w_in", "ab_w_out", "rg_conv_w", "rg_conv_b",
           "rg_w_a", "rg_b_a", "rg_w_x", "rg_b_x", "rg_lambda", "hg_lb_logits", "hg_norm", "gla_w_in", "gla_w_out",
           "gla_w_gate_up", "gla_b_gate", "gla_norm")
ROW_ALIGN = 16


def _pack(arrays, lead=0):
    head = arrays[0].shape[:lead]
    flat = jnp.concatenate([a.reshape(head + (-1,)) for a in arrays], axis=lead)
    n = flat.shape[-1]
    quantum = LANES * ROW_ALIGN
    padded = -(-n // quantum) * quantum
    if padded != n:
        flat = jnp.pad(flat, [(0, 0)] * lead + [(0, padded - n)])
    return flat.reshape(head + (padded // LANES, LANES))


def _unpack(buf, shapes, lead=0):
    head = buf.shape[:lead]
    flat = buf.reshape(head + (-1,))
    out, off = [], 0
    for s in shapes:
        n = 1
        for v in s:
            n *= v
        out.append(lax.slice_in_dim(flat, off, off + n, axis=lead).reshape(head + tuple(s)))
        off += n
    return out


def _join_chips(gathered, axis):
    t = jnp.moveaxis(gathered, 0, axis)
    return t.reshape(t.shape[:axis] + (t.shape[axis] * t.shape[axis + 1],) + t.shape[axis + 2:])


def _split_chips(full, axis):
    s = full.shape
    t = full.reshape(s[:axis] + (N_CHIPS, s[axis] // N_CHIPS) + s[axis + 1:])
    return jnp.moveaxis(t, axis, 0)


_ANY = pl.BlockSpec(memory_space=pl.ANY)


def _place():
    return lax.axis_index("x"), lax.axis_index("y"), lax.axis_index("c")


def _into_slot(name, src, slot, n_slots, dtype, tm, layer=None):
    r, lanes = src.shape[-2:]
    tm = _row_tile(r, tm, ROW_ALIGN)

    def body(slot_ref, in_ref, o_ref):
        o_ref[...] = in_ref[...].astype(o_ref.dtype)

    if layer is None:
        in_spec = pl.BlockSpec((tm, lanes), lambda i, slot_ref: (i, 0))
    else:
        in_spec = pl.BlockSpec((None, tm, lanes), lambda i, slot_ref: (layer, i, 0))
    grid_spec = pltpu.PrefetchScalarGridSpec(
        num_scalar_prefetch=1, grid=(r // tm,), in_specs=[in_spec],
        out_specs=pl.BlockSpec((None, tm, lanes), lambda i, slot_ref: (slot_ref[0], i, 0)))
    return pl.pallas_call(
        body, name=name, grid_spec=grid_spec, out_shape=jax.ShapeDtypeStruct((n_slots, r, lanes), dtype),
        compiler_params=_params(("parallel",)),
    )(slot.reshape(1).astype(jnp.int32), src)


def _chip_peers():
    x, y, c = _place()
    return 2 * x + y, c, [(1 - x, y), (x, 1 - y), (1 - x, 1 - y)]


def _comm_call(name, body, ins, out_shapes, n_sems, aliases=None):
    return pl.pallas_call(
        body, name=name, in_specs=[_ANY] * len(ins), out_specs=[_ANY] * len(out_shapes), out_shape=out_shapes,
        input_output_aliases=aliases or {},
        scratch_shapes=[pltpu.SemaphoreType.DMA((n_sems,)), pltpu.SemaphoreType.DMA((n_sems,))],
    )(*ins)


def _gather_chips(name, bufs):
    n = len(bufs)

    def body(*refs):
        outs, send_sems, recv_sems = refs[n:2 * n], refs[2 * n], refs[2 * n + 1]
        x, y, c = _place()
        me, _, peers = _chip_peers()

        def rows(a, block, half):
            rh = outs[a].shape[1] // 2
            return outs[a].at[block, pl.ds(half * rh, rh)]

        def copy(a, j, block, half, to, sem):
            return pltpu.make_async_remote_copy(
                src_ref=rows(a, block, half), dst_ref=rows(a, block, half), send_sem=send_sems.at[sem],
                recv_sem=recv_sems.at[sem], device_id=to, device_id_type=MESH)

        def over_ici(a, j, block):
            px, py = peers[j]
            return copy(a, j, block, c, (px, py, c), 6 * a + j)

        def to_sibling(a, j, block, half):
            return copy(a, j, block, half, (x, y, 1 - c), 6 * a + 3 + j)

        sends = [over_ici(a, j, me) for a in range(n) for j in range(3)]
        for cp in sends:
            cp.start()
        for a in range(n):
            for j, (px, py) in enumerate(peers):
                over_ici(a, j, 2 * px + py).wait_recv()
                handed = to_sibling(a, j, 2 * px + py, c)
                handed.start()
                sends.append(handed)
        for a in range(n):
            for j, (px, py) in enumerate(peers):
                to_sibling(a, j, 2 * px + py, 1 - c).wait_recv()
        for cp in sends:
            cp.wait_send()

    shapes = [jax.ShapeDtypeStruct(b.shape, b.dtype) for b in bufs]
    return _comm_call(name, body, bufs, shapes, 6 * n, {a: a for a in range(n)})


_HBM = pl.BlockSpec(memory_space=pltpu.HBM)
_SEM = pl.BlockSpec(memory_space=pltpu.SEMAPHORE)
_EFFECT = pltpu.SideEffectType.DATAFLOW_SIDE_EFFECTING


def _half_rows(ref, block, half):
    rh = ref.shape[1] // 2
    return ref.at[block, pl.ds(half * rh, rh)]


def _gather_start(name, bufs, after):
    n = len(bufs)

    def body(*refs):
        ins, send_sems, recv_sems, token = refs[:n], refs[n + 1], refs[n + 2], refs[-1]
        me, c, peers = _chip_peers()
        for a in range(n):
            mine = _half_rows(ins[a], me, c)
            for j, (px, py) in enumerate(peers):
                pltpu.make_async_remote_copy(
                    src_ref=mine, dst_ref=mine, send_sem=send_sems.at[3 * a + j], recv_sem=recv_sems.at[3 * a + j],
                    device_id=(px, py, c), device_id_type=MESH).start()
        token[...] = jnp.zeros_like(token)

    out_shape = (pltpu.SemaphoreType.DMA((3 * n,)), pltpu.SemaphoreType.DMA((3 * n,)),
                 *[pltpu.HBM(b.shape, b.dtype) for b in bufs], jax.ShapeDtypeStruct((8, LANES), F32))
    return pl.pallas_call(
        body, name=name, out_shape=out_shape, in_specs=[_HBM] * n + [_ANY],
        out_specs=(_SEM, _SEM, *[_HBM] * n, pl.BlockSpec(memory_space=pltpu.VMEM)),
        input_output_aliases={a: 2 + a for a in range(n)},
        compiler_params=pltpu.CompilerParams(has_side_effects=_EFFECT),
    )(*[pltpu.with_memory_space_constraint(b, pltpu.HBM) for b in bufs], after)


def _gather_wait(name, bufs, send_sems, recv_sems, after):
    n = len(bufs)

    def body(*refs):
        ins, send_sems, recv_sems = refs[:n], refs[n], refs[n + 1]
        me, c, peers = _chip_peers()
        for a in range(n):
            for j, (px, py) in enumerate(peers):
                copy = pltpu.make_async_remote_copy(
                    src_ref=_half_rows(ins[a], me, c), dst_ref=_half_rows(ins[a], 2 * px + py, c),
                    send_sem=send_sems.at[3 * a + j], recv_sem=recv_sems.at[3 * a + j],
                    device_id=(px, py, c), device_id_type=MESH)
                copy.wait_send()
                copy.wait_recv()

    return pl.pallas_call(
        body, name=name, out_shape=tuple(pltpu.HBM(b.shape, b.dtype) for b in bufs),
        in_specs=[_HBM] * n + [_SEM, _SEM, _ANY], out_specs=tuple([_HBM] * n),
        input_output_aliases={a: a for a in range(n)},
        compiler_params=pltpu.CompilerParams(has_side_effects=_EFFECT),
    )(*bufs, send_sems, recv_sems, after)


def _hand_over(name, bufs):
    n = len(bufs)

    def body(*refs):
        outs, send_sems, recv_sems = refs[n:2 * n], refs[2 * n], refs[2 * n + 1]
        x, y, c = _place()
        _, _, peers = _chip_peers()

        def copy(a, j, half):
            px, py = peers[j]
            rows = _half_rows(outs[a], 2 * px + py, half)
            return pltpu.make_async_remote_copy(
                src_ref=rows, dst_ref=rows, send_sem=send_sems.at[3 * a + j], recv_sem=recv_sems.at[3 * a + j],
                device_id=(x, y, 1 - c), device_id_type=MESH)

        sends = [copy(a, j, c) for a in range(n) for j in range(3)]
        for cp in sends:
            cp.start()
        for a in range(n):
            for j in range(3):
                copy(a, j, 1 - c).wait_recv()
        for cp in sends:
            cp.wait_send()

    shapes = [jax.ShapeDtypeStruct(b.shape, b.dtype) for b in bufs]
    return _comm_call(name, body, bufs, shapes, 3 * n, {a: a for a in range(n)})


def _pair_gather(name, bufs):
    n = len(bufs)

    def body(*refs):
        ins, outs, send_sems, recv_sems = refs[:n], refs[n:2 * n], refs[2 * n], refs[2 * n + 1]
        x, y, c = _place()

        def copy(a, block):
            return pltpu.make_async_remote_copy(
                src_ref=ins[a].at[block], dst_ref=outs[a].at[block], send_sem=send_sems.at[a],
                recv_sem=recv_sems.at[a], device_id=(x, y, 1 - c), device_id_type=MESH)

        sends = [copy(a, c) for a in range(n)]
        for cp in sends:
            cp.start()
        for a in range(n):
            copy(a, 1 - c).wait_recv()
        for cp in sends:
            cp.wait_send()

    shapes = [jax.ShapeDtypeStruct(b.shape, b.dtype) for b in bufs]
    return _comm_call(name, body, bufs, shapes, n, {a: a for a in range(n)})


def _all_peers():
    x, y, c = _place()
    peers = []
    for mask in range(1, N_DEV):
        fx, fy, fc = (mask >> 2) & 1, (mask >> 1) & 1, mask & 1
        peers.append((jnp.where(fx, 1 - x, x), jnp.where(fy, 1 - y, y), jnp.where(fc, 1 - c, c)))
    return 4 * x + 2 * y + c, peers


def _reduce_copies(srcs, lands, send_sems, recv_sems):
    me, peers = _all_peers()
    sends, arrivals = [], []
    for a in range(len(srcs)):
        for j, (px, py, pc) in enumerate(peers):
            k = (N_DEV - 1) * a + j
            sends.append(pltpu.make_async_remote_copy(
                src_ref=srcs[a].at[2 * px + py, pc], dst_ref=lands[a].at[me], send_sem=send_sems.at[k],
                recv_sem=recv_sems.at[k], device_id=(px, py, pc), device_id_type=MESH))
            arrivals.append(pltpu.make_async_remote_copy(
                src_ref=srcs[a].at[2 * px + py, pc], dst_ref=lands[a].at[4 * px + 2 * py + pc],
                send_sem=send_sems.at[k], recv_sem=recv_sems.at[k], device_id=(px, py, pc), device_id_type=MESH))
    return sends, arrivals


def _reduce_direct(name, srcs, pin=None):
    n = len(srcs)
    extra = [] if pin is None else [pin]

    def body(*refs):
        ins, outs = refs[:n], refs[n + len(extra):2 * n + len(extra)]
        sends, arrivals = _reduce_copies(ins, outs, refs[-2], refs[-1])
        for cp in sends:
            cp.start()
        for cp in arrivals:
            cp.wait_recv()
        for cp in sends:
            cp.wait_send()

    shapes = [jax.ShapeDtypeStruct((N_DEV,) + s.shape[2:], s.dtype) for s in srcs]
    return _comm_call(name, body, list(srcs) + extra, shapes, (N_DEV - 1) * n)


def _reduce_start(name, srcs):
    n = len(srcs)
    lands = [lax.empty((N_DEV,) + s.shape[2:], s.dtype) for s in srcs]

    def body(*refs):
        sends, _ = _reduce_copies(refs[:n], refs[n:2 * n], refs[2 * n], refs[2 * n + 1])
        for cp in sends:
            cp.start()
        refs[-1][...] = jnp.zeros_like(refs[-1])

    bufs = list(srcs) + lands
    n_sems = (N_DEV - 1) * n
    out_shape = (pltpu.SemaphoreType.DMA((n_sems,)), pltpu.SemaphoreType.DMA((n_sems,)),
                 *[pltpu.HBM(b.shape, b.dtype) for b in bufs], jax.ShapeDtypeStruct((8, LANES), F32))
    return pl.pallas_call(
        body, name=name, out_shape=out_shape, in_specs=[_HBM] * (2 * n),
        out_specs=(_SEM, _SEM, *[_HBM] * (2 * n), pl.BlockSpec(memory_space=pltpu.VMEM)),
        input_output_aliases={a: 2 + a for a in range(2 * n)},
        compiler_params=pltpu.CompilerParams(has_side_effects=_EFFECT),
    )(*[pltpu.with_memory_space_constraint(b, pltpu.HBM) for b in bufs])


def _reduce_wait(name, srcs, lands, send_sems, recv_sems, after):
    n = len(srcs)

    def body(*refs):
        sends, arrivals = _reduce_copies(refs[:n], refs[n:2 * n], refs[2 * n], refs[2 * n + 1])
        for cp in sends:
            cp.wait_send()
        for cp in arrivals:
            cp.wait_recv()

    bufs = list(srcs) + list(lands)
    outs = pl.pallas_call(
        body, name=name, out_shape=tuple(pltpu.HBM(b.shape, b.dtype) for b in bufs),
        in_specs=[_HBM] * (2 * n) + [_SEM, _SEM, _ANY], out_specs=tuple([_HBM] * (2 * n)),
        input_output_aliases={a: a for a in range(2 * n)},
        compiler_params=pltpu.CompilerParams(has_side_effects=_EFFECT),
    )(*bufs, send_sems, recv_sems, after)
    return list(outs[n:])


def _reduce_sum(name, own, land, chip, core):
    n, rh, lanes = land.shape
    tm = _row_tile(rh, 1024, ROW_ALIGN)

    def body(idx_ref, own_ref, *rest):
        total = own_ref[...]
        for g_ref in rest[:-1]:
            total = total + g_ref[...].astype(F32)
        rest[-1][...] = total

    def block(k):
        return pl.BlockSpec((None, tm, lanes), lambda i, idx_ref: ((2 * idx_ref[0] + idx_ref[1] + k) % n, i, 0))

    grid_spec = pltpu.PrefetchScalarGridSpec(
        num_scalar_prefetch=1, grid=(rh // tm,),
        in_specs=[pl.BlockSpec((None, None, tm, lanes), lambda i, idx_ref: (idx_ref[0], idx_ref[1], i, 0))]
        + [block(k) for k in range(1, n)],
        out_specs=pl.BlockSpec((None, tm, lanes), lambda i, idx_ref: (idx_ref[1], i, 0)))
    return pl.pallas_call(
        body, name=name, grid_spec=grid_spec, out_shape=jax.ShapeDtypeStruct((2, rh, lanes), F32),
        compiler_params=_params(("parallel",)),
    )(jnp.stack([chip, core]).astype(jnp.int32), own, *[land] * (n - 1))


def _gather_all_start(name, buf):
    def body(in_ref, send_sems, recv_sems, out_ref, token):
        me, peers = _all_peers()
        for j, peer in enumerate(peers):
            pltpu.make_async_remote_copy(
                src_ref=in_ref.at[me], dst_ref=in_ref.at[me], send_sem=send_sems.at[j], recv_sem=recv_sems.at[j],
                device_id=peer, device_id_type=MESH).start()
        token[...] = jnp.zeros_like(token)

    n = N_DEV - 1
    return pl.pallas_call(
        body, name=name, in_specs=[_HBM],
        out_shape=(pltpu.SemaphoreType.DMA((n,)), pltpu.SemaphoreType.DMA((n,)), pltpu.HBM(buf.shape, buf.dtype),
                   jax.ShapeDtypeStruct((8, LANES), F32)),
        out_specs=(_SEM, _SEM, _HBM, pl.BlockSpec(memory_space=pltpu.VMEM)), input_output_aliases={0: 2},
        compiler_params=pltpu.CompilerParams(has_side_effects=_EFFECT),
    )(pltpu.with_memory_space_constraint(buf, pltpu.HBM))


def _gather_all_wait(name, buf, send_sems, recv_sems, after):
    def body(in_ref, send_sems, recv_sems, after_ref, out_ref):
        me, peers = _all_peers()
        for j, (px, py, pc) in enumerate(peers):
            copy = pltpu.make_async_remote_copy(
                src_ref=in_ref.at[me], dst_ref=in_ref.at[4 * px + 2 * py + pc], send_sem=send_sems.at[j],
                recv_sem=recv_sems.at[j], device_id=(px, py, pc), device_id_type=MESH)
            copy.wait_send()
            copy.wait_recv()

    return pl.pallas_call(
        body, name=name, in_specs=[_HBM, _SEM, _SEM, _ANY], out_shape=pltpu.HBM(buf.shape, buf.dtype),
        out_specs=_HBM, input_output_aliases={0: 0},
        compiler_params=pltpu.CompilerParams(has_side_effects=_EFFECT),
    )(buf, send_sems, recv_sems, after)


def _sum_blocks(name, stacked, tm):
    n, r, lanes = stacked.shape

    def body(in_ref, o_ref):
        acc = in_ref[0]
        for j in range(1, n):
            acc = acc + in_ref[j]
        o_ref[...] = acc

    return pl.pallas_call(
        body, name=name, grid=(r // tm,), in_specs=[pl.BlockSpec((n, tm, lanes), lambda i: (0, i, 0))],
        out_specs=pl.BlockSpec((tm, lanes), lambda i: (i, 0)), out_shape=jax.ShapeDtypeStruct((r, lanes), F32),
        compiler_params=_params(("parallel",)),
    )(stacked)


def _row_tile(rows, pref, align):
    best = None
    for t in range(align, min(rows, pref) + 1, align):
        if rows % t == 0:
            best = t
    assert best is not None, (rows, pref, align)
    return best


def _adam(name, w, g, m, v):
    rows, width = w.shape
    tm = _row_tile(rows, max(8, 4096 * LANES // width), 8)
    args = [(t, width, 0) for t in (w, g, m, v)]
    return _rowcall(name, _adam_fn, args, [], [(width, F32)] * 3, tm=tm)


def kernel(x, norm_mix, norm_mlp, norm_final, mlp_w1, mlp_w2, ab_w_in, ab_w_out, rg_conv_w, rg_conv_b, rg_w_a, rg_b_a, rg_w_x, rg_b_x, rg_lambda, hg_lb_logits, hg_norm, gla_w_in, gla_w_out, gla_w_gate_up, gla_b_gate, gla_norm, loss_target, m_norm_mix, m_norm_mlp, m_norm_final, m_mlp_w1, m_mlp_w2, m_ab_w_in, m_ab_w_out, m_rg_conv_w, m_rg_conv_b, m_rg_w_a, m_rg_b_a, m_rg_w_x, m_rg_b_x, m_rg_lambda, m_hg_lb_logits, m_hg_norm, m_gla_w_in, m_gla_w_out, m_gla_w_gate_up, m_gla_b_gate, m_gla_norm, v_norm_mix, v_norm_mlp, v_norm_final, v_mlp_w1, v_mlp_w2, v_ab_w_in, v_ab_w_out, v_rg_conv_w, v_rg_conv_b, v_rg_w_a, v_rg_b_a, v_rg_w_x, v_rg_b_x, v_rg_lambda, v_hg_lb_logits, v_hg_norm, v_gla_w_in, v_gla_w_out, v_gla_w_gate_up, v_gla_b_gate, v_gla_norm):
    w = dict(norm_mix=norm_mix, norm_mlp=norm_mlp, norm_final=norm_final, mlp_w1=mlp_w1, mlp_w2=mlp_w2, ab_w_in=ab_w_in, ab_w_out=ab_w_out, rg_conv_w=rg_conv_w, rg_conv_b=rg_conv_b, rg_w_a=rg_w_a, rg_b_a=rg_b_a, rg_w_x=rg_w_x, rg_b_x=rg_b_x, rg_lambda=rg_lambda, hg_lb_logits=hg_lb_logits, hg_norm=hg_norm, gla_w_in=gla_w_in, gla_w_out=gla_w_out, gla_w_gate_up=gla_w_gate_up, gla_b_gate=gla_b_gate, gla_norm=gla_norm)
    m = dict(norm_mix=m_norm_mix, norm_mlp=m_norm_mlp, norm_final=m_norm_final, mlp_w1=m_mlp_w1, mlp_w2=m_mlp_w2, ab_w_in=m_ab_w_in, ab_w_out=m_ab_w_out, rg_conv_w=m_rg_conv_w, rg_conv_b=m_rg_conv_b, rg_w_a=m_rg_w_a, rg_b_a=m_rg_b_a, rg_w_x=m_rg_w_x, rg_b_x=m_rg_b_x, rg_lambda=m_rg_lambda, hg_lb_logits=m_hg_lb_logits, hg_norm=m_hg_norm, gla_w_in=m_gla_w_in, gla_w_out=m_gla_w_out, gla_w_gate_up=m_gla_w_gate_up, gla_b_gate=m_gla_b_gate, gla_norm=m_gla_norm)
    v = dict(norm_mix=v_norm_mix, norm_mlp=v_norm_mlp, norm_final=v_norm_final, mlp_w1=v_mlp_w1, mlp_w2=v_mlp_w2, ab_w_in=v_ab_w_in, ab_w_out=v_ab_w_out, rg_conv_w=v_rg_conv_w, rg_conv_b=v_rg_conv_b, rg_w_a=v_rg_w_a, rg_b_a=v_rg_b_a, rg_w_x=v_rg_w_x, rg_b_x=v_rg_b_x, rg_lambda=v_rg_lambda, hg_lb_logits=v_hg_lb_logits, hg_norm=v_hg_norm, gla_w_in=v_gla_w_in, gla_w_out=v_gla_w_out, gla_w_gate_up=v_gla_w_gate_up, gla_b_gate=v_gla_b_gate, gla_norm=v_gla_norm)
    chip = 2 * lax.axis_index("x") + lax.axis_index("y")
    core = lax.axis_index("c")
    sharded_shapes = [w[n].shape for n in SMALL_SHARDED]

    slots = [_into_slot(f"cast_{n}{layer}", w[n], chip, N_CHIPS, BF16, 512, layer) for n, layer in MATRICES]
    early = [i for i, (n, _) in enumerate(MATRICES) if n in EARLY_MATRICES]
    rest = [i for i in range(len(MATRICES)) if i not in early]

    def named(indices, arrays):
        big = {}
        for i, t in zip(indices, arrays):
            big.setdefault(MATRICES[i][0], []).append(t)
        return {n: (v if n in ("mlp_w1", "mlp_w2") else v[0]) for n, v in big.items()}

    vectors = _pack([w[n] for n in SMALL_SHARDED])
    vectors = _into_slot("place_vectors", vectors, chip, N_CHIPS, F32, vectors.shape[0])
    *gathered, vectors = _gather_chips("gather_early", [slots[i] for i in early] + [vectors])
    send_sems, recv_sems, *in_flight, token = _gather_start("gather_rest_start", [slots[i] for i in rest], gathered[0])

    def late_weights(after):
        landed = _gather_wait("gather_rest_wait", in_flight, send_sems, recv_sems, after)
        return _prepare_matrices(named(rest, _hand_over("gather_rest_share", list(landed))))

    big = named(early, gathered)
    small_all = _unpack(vectors, sharded_shapes, lead=1)
    full = {n: w[n] for n in SMALL_REPLICATED}
    for n, t in zip(SMALL_SHARDED, small_all):
        full[n] = _join_chips(t, t.ndim - 2)

    def halves(t):
        return t.reshape(N_CHIPS, 2, t.shape[1] // 2, t.shape[2])

    in_flight_grads = {}

    def emit(tag, arrays32, arrays16):
        n = len(arrays16)
        send, recv, *rest = _reduce_start(f"reduce_{tag}_start", [halves(t) for t in arrays16])
        in_flight_grads[tag] = ([halves(t) for t in arrays32], rest[:n], rest[n:2 * n], send, recv)
        return rest[-1]

    loss_part, grad_x, g_kernel = _local_step(
        x[0], loss_target[0], _prepare_weights(big, full), token, late_weights, emit)
    g_big, g_full = _finish_grads(g_kernel)

    small_names = SMALL_REPLICATED + SMALL_SHARDED
    reduced_shapes = [g_full[n].shape for n in small_names] + [loss_part.shape]
    g_small = _pack([g_full[n] for n in small_names] + [loss_part])
    device = 2 * chip + core
    g_small = _into_slot("place_small", g_small, device, N_DEV, F32, g_small.shape[0])
    small_send, small_recv, small_in_flight, small_token = _gather_all_start("reduce_small_start", g_small)

    mine = {}
    for tag, (own, srcs, lands, send, recv) in in_flight_grads.items():
        landed = _reduce_wait(f"reduce_{tag}_wait", srcs, lands, send, recv, small_token)
        mine[tag] = [_reduce_sum(f"reduce_add_{tag}{i}", o, f, chip, core) for i, (o, f) in enumerate(zip(own, landed))]
    ordered = [mine["mlp0"][0], mine["mlp1"][0], mine["mlp0"][1], mine["mlp1"][1], mine["ab"][0], mine["mlp0"][2],
               *mine["gla"]]
    reduced = [t.reshape(2 * t.shape[1], t.shape[2]) for t in _pair_gather("reduce_share", ordered)]
    by_name = {n: [] for n, _ in MATRICES}
    for (n, _), t in zip(MATRICES, reduced):
        by_name[n].append(t)
    grads = {n: jnp.stack(v) for n, v in by_name.items()}

    g_small_all = _gather_all_wait("reduce_small_wait", small_in_flight, small_send, small_recv, reduced[0])
    g_small_red = _sum_blocks("reduce_small_add", g_small_all, g_small_all.shape[1])
    *small_red, loss_sum = _unpack(g_small_red, reduced_shapes)
    loss = loss_sum[0, 0]
    g_small_full = dict(zip(small_names, small_red))
    for n in SMALL_REPLICATED:
        grads[n] = g_small_full[n]
    for n in SMALL_SHARDED:
        width = w[n].shape[-1]
        grads[n] = lax.dynamic_slice_in_dim(g_small_full[n], chip * width, width, axis=g_small_full[n].ndim - 1)

    delta, new_m, new_v = {}, {}, {}
    for n in by_name:
        flat = [t.reshape(-1, t.shape[-1]) for t in (w[n], grads[n], m[n], v[n])]
        for dst, t in zip((delta, new_m, new_v), _adam(f"adam_{n}", *flat)):
            dst[n] = t.reshape(w[n].shape)
    small_shapes = [w[n].shape for n in small_names]
    packs = [_pack([src[n] for n in small_names]) for src in (w, grads, m, v)]
    d_small, m_small, v_small = _adam("adam_small", *packs)
    for dst, buf in ((delta, d_small), (new_m, m_small), (new_v, v_small)):
        dst.update(zip(small_names, _unpack(buf, small_shapes)))

    return (loss, grad_x[None], *[grads[n] for n in WEIGHTS], *[delta[n] for n in WEIGHTS],
            *[new_m[n] for n in WEIGHTS], *[new_v[n] for n in WEIGHTS])
```

```python
import functools

import jax
import jax.numpy as jnp
from jax import lax
from jax.experimental import pallas as pl
from jax.experimental.pallas import tpu as pltpu

F32 = jnp.float32
BF16 = jnp.bfloat16
MESH = pl.DeviceIdType.MESH

LANES = 128
CHUNK = 64
ATTN_SUB = 4
EPS = 1e-6
RG_C = 8.0
N_CHIPS = 4
N_DEV = 8
GLA_IN_WIDTH = 3104
GLA_IN_PAD = 3200
VMEM_LIMIT = 56 * 1024 * 1024

ADAM_LR = 0.001
ADAM_B1 = 0.9
ADAM_B2 = 0.999
ADAM_EPS = 1e-08
ADAM_WD = 0.01
ADAM_STEP = 10


def _raw_dot(a, b, ca, cb):
    return lax.dot_general(a.astype(BF16), b.astype(BF16), (((ca,), (cb,)), ((), ())),
                           preferred_element_type=F32)


def _raw_nn(a, b):
    return _raw_dot(a, b, 1, 0)


def _raw_nt(a, b):
    return _raw_dot(a, b, 1, 1)


def _raw_tn(a, b):
    return _raw_dot(a, b, 0, 0)


@jax.custom_vjp
def _dot_nn(a, b):
    return _raw_nn(a, b)


def _dot_nn_fwd(a, b):
    return _raw_nn(a, b), (a, b)


def _dot_nn_bwd(res, g):
    a, b = res
    return _raw_nt(g, b), _raw_tn(a, g)


_dot_nn.defvjp(_dot_nn_fwd, _dot_nn_bwd)


@jax.custom_vjp
def _dot_nt(a, b):
    return _raw_nt(a, b)


def _dot_nt_fwd(a, b):
    return _raw_nt(a, b), (a, b)


def _dot_nt_bwd(res, g):
    a, b = res
    return _raw_nn(g, b), _raw_tn(g, a)


_dot_nt.defvjp(_dot_nt_fwd, _dot_nt_bwd)


@jax.custom_vjp
def _dot_tn(a, b):
    return _raw_tn(a, b)


def _dot_tn_fwd(a, b):
    return _raw_tn(a, b), (a, b)


def _dot_tn_bwd(res, g):
    a, b = res
    return _raw_nt(b, g), _raw_nn(a, g)


_dot_tn.defvjp(_dot_tn_fwd, _dot_tn_bwd)


def _tile(n, pref):
    if n <= pref:
        return n
    t = (pref // LANES) * LANES
    while t > LANES and n % t:
        t -= LANES
    assert n % t == 0, (n, pref)
    return t


def _params(sem):
    return pltpu.CompilerParams(dimension_semantics=sem, vmem_limit_bytes=VMEM_LIMIT)


def _rowcall(name, fn, rows, pars, row_outs, par_outs=(), tm=512, pin=None):
    if pin is not None:
        inner, pars = fn, list(pars) + [pin]
        fn = lambda *vals: inner(*vals[:-1])
    n_rows = rows[0][0].shape[0]
    tm = min(tm, n_rows)
    assert n_rows % tm == 0
    n_r, n_p, n_ro = len(rows), len(pars), len(row_outs)

    def body(*refs):
        vals = [r[...].astype(F32) for r in refs[:n_r + n_p]]
        outs = fn(*vals)
        o_refs = refs[n_r + n_p:n_r + n_p + n_ro]
        po_refs = refs[n_r + n_p + n_ro:]
        for o_ref, val in zip(o_refs, outs[:n_ro]):
            o_ref[...] = val.astype(o_ref.dtype)
        first = pl.program_id(0) == 0
        for po_ref, val in zip(po_refs, outs[n_ro:]):
            @pl.when(first)
            def _():
                po_ref[...] = val

            @pl.when(jnp.logical_not(first))
            def _():
                po_ref[...] += val

    def const_map(nd):
        return lambda i: (0,) * nd

    def row_spec(w, cb):
        return pl.BlockSpec((tm, w), lambda i: (i, cb))

    in_specs = [row_spec(w, cb) for _, w, cb in rows]
    in_specs += [pl.BlockSpec(p.shape, const_map(p.ndim)) for p in pars]
    out_specs = [pl.BlockSpec((tm, w), lambda i: (i, 0)) for w, _ in row_outs]
    out_specs += [pl.BlockSpec(tuple(s), const_map(len(s))) for s in par_outs]
    out_shape = [jax.ShapeDtypeStruct((n_rows, w), dt) for w, dt in row_outs]
    out_shape += [jax.ShapeDtypeStruct(tuple(s), F32) for s in par_outs]
    return pl.pallas_call(
        body, name=name, grid=(n_rows // tm,), in_specs=in_specs, out_specs=out_specs, out_shape=out_shape,
        compiler_params=_params(("arbitrary",) if par_outs else ("parallel",)),
    )(*[r[0] for r in rows], *pars)


def _vjp_of(fn, n_prim, n_out, n_par, n_pass=0):
    def bwd(*args):
        prim = args[:n_prim]
        cts = args[n_prim:n_prim + n_out]
        passes = args[n_prim + n_out:n_prim + n_out + 2 * n_pass]
        pars = args[n_prim + n_out + 2 * n_pass:]
        _, vjp = jax.vjp(fn, *prim, *pars)
        grads = vjp(tuple(cts))
        sums = tuple(passes[2 * i] + passes[2 * i + 1] for i in range(n_pass))
        return tuple(grads[:n_prim]) + sums + tuple(grads[n_prim:])
    return bwd


def _mm(name, a, b, mode="nn", extras=(), epi=None, out_dtypes=(F32,), a_pro=None, out_split=None,
        epi_pars=(), row_sum=False, pin=None, tm=1024, tn=1024, tk=1024):
    split = b.shape[0] if b.ndim == 3 else None
    b_rows, b_cols = b.shape[-2:]
    if mode == "nn":
        (m, k), n = a.shape, b_cols * (split or 1)
    elif mode == "nt":
        (m, k), n = a.shape, b_rows
        assert k == b_cols * (split or 1)
    else:
        assert split is None
        (k, m), n = a.shape, b_cols
    tm, tk = _tile(m, tm), _tile(k, tk)
    tn = _tile(n // out_split, tn) if out_split else _tile(n, tn)
    if split and mode == "nn":
        tn = _tile(b_cols, tn)
    if split and mode == "nt":
        tk = _tile(b_cols, tk)
    nk = k // tk
    raw = {"nn": _raw_nn, "nt": _raw_nt, "tn": _raw_tn}[mode]
    n_e, n_p, n_o = len(extras), len(epi_pars), len(out_dtypes)
    n_in = n_e + n_p + (0 if pin is None else 1)
    if epi is None:
        epi = lambda acc: (acc,)

    def body(a_ref, b_ref, *rest):
        e_refs, p_refs, o_refs = rest[:n_e], rest[n_e:n_e + n_p], rest[n_in:n_in + n_o]
        kk = pl.program_id(2)
        a_tile = a_ref[...] if a_pro is None else a_pro(a_ref[...].astype(F32))
        part = raw(a_tile, b_ref[...])

        def finish(total):
            res = epi(total, *[e[...].astype(F32) for e in e_refs], *[p[...] for p in p_refs])
            for o_ref, r in zip(o_refs, res):
                o_ref[...] = r.astype(o_ref.dtype)
            if row_sum:
                rest[n_in + n_o][...] = res[n_o]

        if nk == 1:
            finish(part)
            return
        acc = rest[-1]

        @pl.when(kk == 0)
        def _():
            acc[...] = part

        @pl.when((kk > 0) & (kk < nk - 1))
        def _():
            acc[...] += part

        @pl.when(kk == nk - 1)
        def _():
            finish(acc[...] + part)

    a_spec = pl.BlockSpec((tk, tm), lambda i, j, kk: (kk, i)) if mode == "tn" else pl.BlockSpec((tm, tk), lambda i, j, kk: (i, kk))
    if split and mode == "nn":
        per = b_cols // tn
        b_spec = pl.BlockSpec((None, tk, tn), lambda i, j, kk: (j // per, kk, j % per))
    elif split:
        per = b_cols // tk
        b_spec = pl.BlockSpec((None, tn, tk), lambda i, j, kk: (kk // per, j, kk % per))
    elif mode == "nt":
        b_spec = pl.BlockSpec((tn, tk), lambda i, j, kk: (j, kk))
    else:
        b_spec = pl.BlockSpec((tk, tn), lambda i, j, kk: (kk, j))
    mn_spec = pl.BlockSpec((tm, tn), lambda i, j, kk: (i, j))
    if out_split:
        assert not extras
        per_out = n // out_split // tn
        out_spec = pl.BlockSpec((None, tm, tn), lambda i, j, kk: (j // per_out, i, j % per_out))
        out_shapes = [jax.ShapeDtypeStruct((out_split, m, n // out_split), dt) for dt in out_dtypes]
    else:
        out_spec = mn_spec
        out_shapes = [jax.ShapeDtypeStruct((m, n), dt) for dt in out_dtypes]
    out_specs = [out_spec] * n_o
    if row_sum:
        out_specs.append(pl.BlockSpec((None, 1, tn), lambda i, j, kk: (i, 0, j)))
        out_shapes.append(jax.ShapeDtypeStruct((m // tm, 1, n), F32))
    in_specs = [a_spec, b_spec] + [mn_spec] * n_e
    in_specs += [pl.BlockSpec(p.shape, functools.partial(lambda i, j, kk, nd: (0,) * nd, nd=p.ndim)) for p in epi_pars]
    in_specs += [] if pin is None else [pl.BlockSpec(memory_space=pl.ANY)]
    outs = pl.pallas_call(
        body, name=name, grid=(m // tm, n // tn, nk), in_specs=in_specs, out_specs=out_specs, out_shape=out_shapes,
        scratch_shapes=[pltpu.VMEM((tm, tn), F32)] if nk > 1 else [],
        compiler_params=_params(("parallel", "parallel", "arbitrary")),
    )(a, b, *extras, *epi_pars, *([] if pin is None else [pin]))
    return outs[0] if len(outs) == 1 else outs


def _sigmoid(x):
    return jax.nn.sigmoid(x)


def _silu(x):
    return x * _sigmoid(x)


def _softplus(x):
    return jnp.maximum(x, 0.0) + jnp.log1p(jnp.exp(-jnp.abs(x)))


def _rmsnorm_fn(x, gain):
    return (x * lax.rsqrt(jnp.mean(x * x, axis=-1, keepdims=True) + EPS) * gain,)


def _head_norm(o, gain, n_heads):
    w = o.shape[-1] // n_heads
    parts = []
    for h in range(n_heads):
        oh = o[:, h * w:(h + 1) * w]
        parts.append(oh * lax.rsqrt(jnp.mean(oh * oh, axis=-1, keepdims=True) + EPS))
    return jnp.concatenate(parts, axis=-1) * gain


@jax.custom_jvp
def _neg_expm1(x):
    u = jnp.exp(x)
    is_one = u == 1.0
    return jnp.where(is_one, -x, (1.0 - u) * x / jnp.log(jnp.where(is_one, 2.0, u)))


@_neg_expm1.defjvp
def _neg_expm1_jvp(primals, tangents):
    (x,), (t,) = primals, tangents
    return _neg_expm1(x), -jnp.exp(x) * t


def _rg_gates_fn(xc, wa, wx, ba, bx, lam):
    outs = []
    for d in range(2):
        r = _sigmoid(_dot_nn(xc, wa[d]) + ba[d:d + 1])
        i = _sigmoid(_dot_nn(xc, wx[d]) + bx[d:d + 1])
        log_a = -RG_C * r * _softplus(-lam[d:d + 1])
        outs.append(jnp.exp(log_a))
        outs.append(jnp.sqrt(_neg_expm1(2.0 * log_a)) * (i * xc))
    return tuple(outs)


def _hg_pre_fn(q, f_f, f_b, logits):
    mx = jnp.maximum(logits[0:1], logits[1:2])
    e0 = jnp.exp(logits[0:1] - mx)
    e1 = jnp.exp(logits[1:2] - mx)
    lb = e0 / (e0 + e1)
    outs = [_silu(q)]
    for f in (f_f, f_b):
        outs.append((1.0 - lb) * _sigmoid(-f))
        outs.append(jnp.log(lb + (1.0 - lb) * _sigmoid(f)))
    return tuple(outs)


def _post0_fn(hs, ga, o, g, gain):
    ya = hs * jax.nn.gelu(ga, approximate=True)
    yb = _head_norm(o, gain, 4) * _silu(g)
    return (jnp.concatenate([ya, yb], axis=-1),)


def _post0_fwd_fn(h_f, h_b, ga, o_f, o_b, g, gain):
    return _post0_fn(h_f + h_b, ga, o_f + o_b, g, gain)


def _post0_bwd_fn(h_f, h_b, ga, o_f, o_b, g, dmix, gain):
    _, vjp = jax.vjp(_post0_fn, h_f + h_b, ga, o_f + o_b, g, gain)
    return vjp((dmix,))


def _gla_pre_fn(q, lr, w_up, b_gate):
    outs = [q * (128.0 ** -0.5)]
    for d in range(2):
        z = _dot_nn(lr, w_up[d]) + b_gate[d:d + 1]
        outs.append(-_softplus(-z) * (1.0 / 16.0))
    return tuple(outs)


def _gla_post_fn(o, r, gain):
    return (_head_norm(o, gain, 4) * _silu(r),)


def _gla_post_fwd_fn(o_f, o_b, r, gain):
    return _gla_post_fn(o_f + o_b, r, gain)


def _gla_post_bwd_fn(o_f, o_b, r, dmix, gain):
    _, vjp = jax.vjp(_gla_post_fn, o_f + o_b, r, gain)
    return vjp((dmix,))


def _relu2_bwd_epi(acc, hid):
    return (acc * 2.0 * jnp.maximum(hid, 0.0),)


def _relu2(x):
    r = jnp.maximum(x, 0.0)
    return r * r


def _add_epi(acc, res):
    return (acc + res,)


def _loss_head_fn(h, target, gain):
    def f(h, gain):
        y = _rmsnorm_fn(h, gain)[0]
        err = y - target
        return 0.5 * jnp.sum(jnp.mean(err * err, axis=-1, keepdims=True))
    loss, (dh, dgain) = jax.value_and_grad(f, argnums=(0, 1))(h, gain)
    return dh, dh, jnp.full((1, LANES), loss, F32), dgain


def _adam_fn(w, g, m, v):
    m2 = ADAM_B1 * m + (1.0 - ADAM_B1) * g
    v2 = ADAM_B2 * v + (1.0 - ADAM_B2) * (g * g)
    m_hat = m2 / (1.0 - ADAM_B1 ** ADAM_STEP)
    v_hat = v2 / (1.0 - ADAM_B2 ** ADAM_STEP)
    delta = -ADAM_LR * (m_hat / (jnp.sqrt(v_hat) + ADAM_EPS) + ADAM_WD * w)
    return delta, m2, v2


def _shifted(x, t_idx, off):
    n = x.shape[0]
    rolled = pltpu.roll(x, (-off) % n, 0)
    valid = (t_idx + off >= 0) & (t_idx + off < n)
    return jnp.where(valid, rolled, 0.0)


def _conv_fwd(name, src, colblock, w, b):
    n_rows, width = src.shape[0], w.shape[1]

    def body(x_ref, w_ref, b_ref, o_ref):
        x = x_ref[...]
        t_idx = lax.broadcasted_iota(jnp.int32, x.shape, 0)
        acc = b_ref[...] + w_ref[2:3, :] * x
        acc += w_ref[0:1, :] * _shifted(x, t_idx, -2)
        acc += w_ref[1:2, :] * _shifted(x, t_idx, -1)
        acc += w_ref[3:4, :] * _shifted(x, t_idx, 1)
        o_ref[...] = acc

    nb = width // LANES
    return pl.pallas_call(
        body, name=name, grid=(nb,),
        in_specs=[pl.BlockSpec((n_rows, LANES), lambda j: (0, colblock * nb + j)),
                  pl.BlockSpec((4, LANES), lambda j: (0, j)), pl.BlockSpec((1, LANES), lambda j: (0, j))],
        out_specs=pl.BlockSpec((n_rows, LANES), lambda j: (0, j)),
        out_shape=jax.ShapeDtypeStruct((n_rows, width), F32),
        compiler_params=_params(("parallel",)),
    )(src, w, b)


def _conv_bwd(name, src, colblock, w, d):
    n_rows, width = src.shape[0], w.shape[1]

    def body(x_ref, w_ref, d_ref, dx_ref, dw_ref, db_ref):
        x = x_ref[...]
        g = d_ref[...]
        t_idx = lax.broadcasted_iota(jnp.int32, x.shape, 0)
        dx = w_ref[2:3, :] * g
        dx += w_ref[0:1, :] * _shifted(g, t_idx, 2)
        dx += w_ref[1:2, :] * _shifted(g, t_idx, 1)
        dx += w_ref[3:4, :] * _shifted(g, t_idx, -1)
        dx_ref[...] = dx.astype(dx_ref.dtype)
        dw_ref[0:1, :] = jnp.sum(g * _shifted(x, t_idx, -2), axis=0, keepdims=True)
        dw_ref[1:2, :] = jnp.sum(g * _shifted(x, t_idx, -1), axis=0, keepdims=True)
        dw_ref[2:3, :] = jnp.sum(g * x, axis=0, keepdims=True)
        dw_ref[3:4, :] = jnp.sum(g * _shifted(x, t_idx, 1), axis=0, keepdims=True)
        db_ref[...] = jnp.sum(g, axis=0, keepdims=True)

    nb = width // LANES
    return pl.pallas_call(
        body, name=name, grid=(nb,),
        in_specs=[pl.BlockSpec((n_rows, LANES), lambda j: (0, colblock * nb + j)),
                  pl.BlockSpec((4, LANES), lambda j: (0, j)),
                  pl.BlockSpec((n_rows, LANES), lambda j: (0, j))],
        out_specs=[pl.BlockSpec((n_rows, LANES), lambda j: (0, j)), pl.BlockSpec((4, LANES), lambda j: (0, j)),
                   pl.BlockSpec((1, LANES), lambda j: (0, j))],
        out_shape=[jax.ShapeDtypeStruct((n_rows, width), BF16), jax.ShapeDtypeStruct((4, width), F32),
                   jax.ShapeDtypeStruct((1, width), F32)],
        compiler_params=_params(("parallel",)),
    )(src, w, d)


SUBLANES = 8
SCAN_UNROLL = 8


def _shift_rows(x, d, fill):
    n = x.shape[0]
    t = lax.broadcasted_iota(jnp.int32, x.shape, 0)
    valid = (t >= d) if d > 0 else (t < n + d)
    return jnp.where(valid, pltpu.roll(x, d % n, 0), fill)


def _tile_scan(a, u, reverse):
    d = 1
    while d < a.shape[0]:
        s = -d if reverse else d
        a_sh, u_sh = _shift_rows(a, s, 1.0), _shift_rows(u, s, 0.0)
        u = u + a * u_sh
        a = a * a_sh
        d *= 2
    return a, u


def _edge_row(x, reverse):
    return x[0:1, :] if reverse else x[SUBLANES - 1:SUBLANES, :]


def _scan_specs(n_rows, n):
    return [pl.BlockSpec((n_rows, LANES), lambda j: (0, j))] * n


def _scan_tile(a_ref, u_ref, h_ref, i, carry, reverse):
    n_tiles = a_ref.shape[0] // SUBLANES
    tile = (n_tiles - 1 - i) if reverse else i
    rows = pl.ds(pl.multiple_of(tile * SUBLANES, SUBLANES), SUBLANES)
    acc_a, acc_u = _tile_scan(a_ref[rows, :], u_ref[rows, :], reverse)
    h = acc_u + acc_a * carry
    h_ref[rows, :] = h
    return _edge_row(h, reverse)


def _scan_fwd(name, a_f, u_f, a_b, u_b):
    n_rows, width = a_f.shape

    def body(af_ref, uf_ref, ab_ref, ub_ref, hf_ref, hb_ref):
        def step(i, carry):
            return (_scan_tile(af_ref, uf_ref, hf_ref, i, carry[0], False),
                    _scan_tile(ab_ref, ub_ref, hb_ref, i, carry[1], True))
        zero = jnp.zeros((1, LANES), F32)
        lax.fori_loop(0, n_rows // SUBLANES, step, (zero, zero), unroll=SCAN_UNROLL)

    return pl.pallas_call(
        body, name=name, grid=(width // LANES,), in_specs=_scan_specs(n_rows, 4), out_specs=_scan_specs(n_rows, 2),
        out_shape=[jax.ShapeDtypeStruct((n_rows, width), F32)] * 2, compiler_params=_params(("parallel",)),
    )(a_f, u_f, a_b, u_b)


def _scan_bwd_tile(a_ref, h_ref, dh_ref, du_ref, da_ref, i, carry, reverse):
    n_rows = a_ref.shape[0]
    n_tiles = n_rows // SUBLANES
    against = not reverse
    one = -1 if against else 1
    g_in, a_edge = carry
    tile = (n_tiles - 1 - i) if against else i
    start = pl.multiple_of(tile * SUBLANES, SUBLANES)
    rows = pl.ds(start, SUBLANES)
    a_tile = a_ref[rows, :]
    coeff = _shift_rows(a_tile, one, a_edge)
    acc_a, acc_u = _tile_scan(coeff, dh_ref[rows, :], against)
    g = acc_u + acc_a * g_in
    du_ref[rows, :] = g
    outside = (start + SUBLANES) if reverse else (start - 1)
    inside = (outside >= 0) & (outside < n_rows)
    h_edge = jnp.where(inside, h_ref[pl.ds(jnp.clip(outside, 0, n_rows - 1), 1), :], 0.0)
    da_ref[rows, :] = g * _shift_rows(h_ref[rows, :], -one, h_edge)
    return _edge_row(g, against), _edge_row(a_tile, against)


def _scan_bwd(name, a_f, h_f, a_b, h_b, dh):
    n_rows, width = a_f.shape

    def body(af_ref, hf_ref, ab_ref, hb_ref, dh_ref, duf_ref, daf_ref, dub_ref, dab_ref):
        def step(i, carry):
            return (_scan_bwd_tile(af_ref, hf_ref, dh_ref, duf_ref, daf_ref, i, carry[0], False),
                    _scan_bwd_tile(ab_ref, hb_ref, dh_ref, dub_ref, dab_ref, i, carry[1], True))
        zero = jnp.zeros((1, LANES), F32)
        lax.fori_loop(0, n_rows // SUBLANES, step, ((zero, zero), (zero, zero)), unroll=SCAN_UNROLL)

    return pl.pallas_call(
        body, name=name, grid=(width // LANES,), in_specs=_scan_specs(n_rows, 5), out_specs=_scan_specs(n_rows, 4),
        out_shape=[jax.ShapeDtypeStruct((n_rows, width), F32)] * 4, compiler_params=_params(("parallel",)),
    )(a_f, h_f, a_b, h_b, dh)


def _tri_mask(c, reverse):
    row = lax.broadcasted_iota(jnp.int32, (c, c), 0)
    col = lax.broadcasted_iota(jnp.int32, (c, c), 1)
    return (col >= row) if reverse else (col <= row)


def _cumsum_rows(x, reverse):
    tri = _tri_mask(x.shape[0], reverse).astype(BF16)
    hi = x.astype(BF16)
    rest = x - hi.astype(F32)
    mid = rest.astype(BF16)
    lo = (rest - mid.astype(F32)).astype(BF16)
    return _raw_nn(tri, hi) + _raw_nn(tri, mid) + _raw_nn(tri, lo)


@functools.partial(jax.custom_vjp, nondiff_argnums=(1,))
def _cumsum(x, reverse):
    return _cumsum_rows(x, reverse)


def _cumsum_fwd(x, reverse):
    return _cumsum_rows(x, reverse), None


def _cumsum_bwd(reverse, _, g):
    return (_cumsum_rows(g, not reverse),)


_cumsum.defvjp(_cumsum_fwd, _cumsum_bwd)


def _chunks_fn(qs, ks, vs, lfs, sts, reverses):
    n, c = len(qs), qs[0].shape[0]
    every = range(n)
    tris = [_tri_mask(c, r) for r in reverses]
    cums = [_cumsum(lfs[i], reverses[i]) for i in every]
    rid = lax.broadcasted_iota(jnp.int32, cums[0].shape, 0)

    def pick(cum, r):
        return jnp.sum(jnp.where(rid == r, cum, 0.0), axis=0, keepdims=True)

    refs = [pick(cums[i], (c - 1 - c // 2) if reverses[i] else c // 2) for i in every]
    lasts = [pick(cums[i], 0 if reverses[i] else c - 1) for i in every]
    q_in = [qs[i] * jnp.exp(cums[i] - refs[i]) for i in every]
    k_in = [ks[i] * jnp.exp(refs[i] - cums[i]) for i in every]
    scores = [jnp.where(tris[i], _dot_nt(q_in[i], k_in[i]), 0.0) for i in every]
    o_intra = [_dot_nn(scores[i], vs[i]) for i in every]
    q_out = [qs[i] * jnp.exp(cums[i]) for i in every]
    o_inter = [_dot_nt(q_out[i], sts[i]) for i in every]
    k_state = [ks[i] * jnp.exp(lasts[i] - cums[i]) for i in every]
    upd = [_dot_tn(vs[i], k_state[i]) for i in every]
    st_new = [sts[i] * jnp.exp(lasts[i]) + upd[i] for i in every]
    return [o_intra[i] + o_inter[i] for i in every], st_new


def _attn_fwd(name, q, k_f, k_b, v, lf_f, lf_b, n_heads, dk, dv):
    n_rows = q[0].shape[0]
    n_chunks = n_rows // CHUNK
    n_steps = n_chunks // ATTN_SUB
    wk, wv = n_heads * dk, n_heads * dv

    def spec(width, off, rev):
        return pl.BlockSpec((CHUNK * ATTN_SUB, width), lambda n: ((n_steps - 1 - n) if rev else n, off))

    def sspec(rev):
        return pl.BlockSpec((ATTN_SUB, n_heads, dv, dk), lambda n: ((n_steps - 1 - n) if rev else n, 0, 0, 0))

    def body(qf, kf, vf, lff, qb, kb, vb, lfb, of_ref, ob_ref, sf_ref, sb_ref, st):
        @pl.when(pl.program_id(0) == 0)
        def _():
            st[...] = jnp.zeros_like(st)

        ins = ((qf, kf, vf, lff), (qb, kb, vb, lfb))
        chains = [(d, h) for d in range(2) for h in range(n_heads)]
        ck = [slice(h * dk, (h + 1) * dk) for h in range(n_heads)]
        cv = [slice(h * dv, (h + 1) * dv) for h in range(n_heads)]
        sts = [st[d, h] for d, h in chains]
        done = []
        for sub in range(ATTN_SUB):
            local = (sub, ATTN_SUB - 1 - sub)
            rows = [slice(local[d] * CHUNK, (local[d] + 1) * CHUNK) for d in range(2)]
            qs = [ins[d][0][rows[d], ck[h]] for d, h in chains]
            ks = [ins[d][1][rows[d], ck[h]] for d, h in chains]
            vs = [ins[d][2][rows[d], cv[h]] for d, h in chains]
            lfs = [ins[d][3][rows[d], ck[h]] for d, h in chains]
            os_, st_new = _chunks_fn(qs, ks, vs, lfs, sts, [d == 1 for d, _ in chains])
            done.append((local, rows, sts, os_))
            sts = st_new
        for local, rows, entered, os_ in done:
            for i, (d, h) in enumerate(chains):
                (sf_ref, sb_ref)[d][local[d], h] = entered[i].astype(BF16)
                (of_ref, ob_ref)[d][rows[d], cv[h]] = os_[i]
        for i, (d, h) in enumerate(chains):
            st[d, h] = sts[i]

    in_specs = [spec(wk, q[1], False), spec(wk, k_f[1], False), spec(wv, v[1], False), spec(wk, lf_f[1], False),
                spec(wk, q[1], True), spec(wk, k_b[1], True), spec(wv, v[1], True), spec(wk, lf_b[1], True)]
    return pl.pallas_call(
        body, name=name, grid=(n_steps,), in_specs=in_specs,
        out_specs=[spec(wv, 0, False), spec(wv, 0, True), sspec(False), sspec(True)],
        out_shape=[jax.ShapeDtypeStruct((n_rows, wv), F32)] * 2
        + [jax.ShapeDtypeStruct((n_chunks, n_heads, dv, dk), BF16)] * 2,
        scratch_shapes=[pltpu.VMEM((2, n_heads, dv, dk), F32)],
        compiler_params=_params(("arbitrary",)),
    )(q[0], k_f[0], v[0], lf_f[0], q[0], k_b[0], v[0], lf_b[0])


def _attn_bwd(name, q, k_f, k_b, v, lf_f, lf_b, st_f, st_b, do, n_heads, dk, dv, out_dtype=F32):
    n_rows = q[0].shape[0]
    n_chunks = n_rows // CHUNK
    n_steps = n_chunks // ATTN_SUB
    wk, wv = n_heads * dk, n_heads * dv

    def spec(width, off, rev):
        return pl.BlockSpec((CHUNK * ATTN_SUB, width), lambda n: (n if rev else (n_steps - 1 - n), off))

    def sspec(rev):
        return pl.BlockSpec((ATTN_SUB, n_heads, dv, dk), lambda n: (n if rev else (n_steps - 1 - n), 0, 0, 0))

    def body(qf, kf, vf, lff, sf, dof, qb, kb, vb, lfb, sb, dob,
             dqf, dkf, dvf, dlff, dqb, dkb, dvb, dlfb, dst):
        @pl.when(pl.program_id(0) == 0)
        def _():
            dst[...] = jnp.zeros_like(dst)

        ins = ((qf, kf, vf, lff, sf, dof), (qb, kb, vb, lfb, sb, dob))
        outs = ((dqf, dkf, dvf, dlff), (dqb, dkb, dvb, dlfb))
        chains = [(d, h) for d in range(2) for h in range(n_heads)]
        ck = [slice(h * dk, (h + 1) * dk) for h in range(n_heads)]
        cv = [slice(h * dv, (h + 1) * dv) for h in range(n_heads)]
        fn = functools.partial(_chunks_fn, reverses=[d == 1 for d, _ in chains])
        dsts = [dst[d, h] for d, h in chains]
        done = []
        for sub in range(ATTN_SUB):
            local = (ATTN_SUB - 1 - sub, sub)
            rows = [slice(local[d] * CHUNK, (local[d] + 1) * CHUNK) for d in range(2)]
            qs = [ins[d][0][rows[d], ck[h]] for d, h in chains]
            ks = [ins[d][1][rows[d], ck[h]] for d, h in chains]
            vs = [ins[d][2][rows[d], cv[h]] for d, h in chains]
            lfs = [ins[d][3][rows[d], ck[h]] for d, h in chains]
            sts = [ins[d][4][local[d], h].astype(F32) for d, h in chains]
            dos = [ins[d][5][rows[d], cv[h]] for d, h in chains]
            _, vjp = jax.vjp(fn, qs, ks, vs, lfs, sts)
            dqs, dks, dvs, dlfs, dsts = vjp((dos, dsts))
            done.append((rows, dqs, dks, dvs, dlfs))
        for rows, dqs, dks, dvs, dlfs in done:
            for i, (d, h) in enumerate(chains):
                dq_r, dk_r, dv_r, dlf_r = outs[d]
                dq_r[rows[d], ck[h]] = dqs[i].astype(dq_r.dtype)
                dk_r[rows[d], ck[h]] = dks[i].astype(dk_r.dtype)
                dv_r[rows[d], cv[h]] = dvs[i].astype(dv_r.dtype)
                dlf_r[rows[d], ck[h]] = dlfs[i].astype(dlf_r.dtype)
        for i, (d, h) in enumerate(chains):
            dst[d, h] = dsts[i]

    def dir_specs(kk, lf, rev):
        return [spec(wk, q[1], rev), spec(wk, kk[1], rev), spec(wv, v[1], rev), spec(wk, lf[1], rev), sspec(rev),
                spec(wv, 0, rev)]

    def dir_out_specs(rev):
        return [spec(wk, 0, rev), spec(wk, 0, rev), spec(wv, 0, rev), spec(wk, 0, rev)]

    shapes = [jax.ShapeDtypeStruct((n_rows, wk), out_dtype), jax.ShapeDtypeStruct((n_rows, wk), out_dtype),
              jax.ShapeDtypeStruct((n_rows, wv), out_dtype), jax.ShapeDtypeStruct((n_rows, wk), F32)]
    outs = pl.pallas_call(
        body, name=name, grid=(n_steps,), in_specs=dir_specs(k_f, lf_f, False) + dir_specs(k_b, lf_b, True),
        out_specs=dir_out_specs(False) + dir_out_specs(True), out_shape=shapes + shapes,
        scratch_shapes=[pltpu.VMEM((2, n_heads, dv, dk), F32)],
        compiler_params=_params(("arbitrary",)),
    )(q[0], k_f[0], v[0], lf_f[0], st_f, do, q[0], k_b[0], v[0], lf_b[0], st_b, do)
    return outs[:4], outs[4:]


def _row2(v):
    return v.reshape(1, -1)


def _mlp_fwd(tag, h, gain, w1, w2):
    y = _rowcall(f"{tag}_norm", _rmsnorm_fn, [(h, h.shape[1], 0)], [gain], [(h.shape[1], BF16)], tm=512)[0]
    hid = _mm(f"{tag}_up", y, w1, out_dtypes=(BF16,))
    h_out = _mm(f"{tag}_down", hid, w2, a_pro=_relu2, extras=(h,), epi=_add_epi)
    return h_out, (y, hid)


def _dw(name, a, b, **kw):
    return _mm(name, a, b, mode="tn", epi=lambda acc: (acc, acc), out_dtypes=(F32, BF16), **kw)


def _mlp_bwd(tag, h, gain, w1, w2, saved, dh_out):
    y, hid = saved
    dhid = _mm(f"{tag}_dact", dh_out[1], w2, mode="nt", extras=(hid,), epi=_relu2_bwd_epi, out_dtypes=(BF16,))
    dw2 = _dw(f"{tag}_dw2", hid, dh_out[1], a_pro=_relu2)
    dw1 = _dw(f"{tag}_dw1", y, dhid, out_split=N_CHIPS, tk=2048)
    dh, dgain = _dy_norm_bwd(f"{tag}_dy", dhid, w1, h, gain, dh_out[0])
    return dh, dgain, dw1, dw2


def _dy_norm_bwd(name, dz, w, h, gain, dres, pin=None, twice=True, **tiles):
    n_out = 2 if twice else 1

    def epi(dy, h_tile, dres_tile, gain_row):
        _, vjp = jax.vjp(lambda u, v: _rmsnorm_fn(u, v)[0], h_tile, gain_row)
        dh, dgain = vjp(dy)
        return (dh + dres_tile,) * n_out + (dgain,)

    assert h.shape[1] <= 1024
    tiles.setdefault("tm", 1024)
    *dh, dgain_parts = _mm(name, dz, w, mode="nt", extras=(h, dres), epi=epi, epi_pars=(gain,), row_sum=True,
                           out_dtypes=(F32, BF16)[:n_out], pin=pin, **tiles)
    return dh, jnp.sum(dgain_parts, axis=0)


def _local_step(x, target, w, pin=None, late=None, emit=None):
    g = {}
    d_model = x.shape[1]
    rg_w = hg_w = d_model // 2
    pins = []

    def send_off(tag, pairs):
        if emit is not None:
            pins.append(emit(tag, [p[0] for p in pairs], [p[1] for p in pairs]))

    def both(fn, pair):
        return [fn(t) for t in pair]

    def chip_major(t):
        return t.reshape(N_CHIPS, t.shape[0] // N_CHIPS, t.shape[1])

    h_a0 = x
    gain = _row2(w["norm_mix"][0])
    y0 = _rowcall("l0_norm", _rmsnorm_fn, [(h_a0, d_model, 0)], [gain], [(d_model, BF16)], tm=512, pin=pin)[0]
    proj0 = _mm("l0_in", y0, w["ab_w_in"])
    conv_w, conv_b = w["rg_conv_w"], _row2(w["rg_conv_b"])
    xc = _conv_fwd("rg_conv", proj0, 0, conv_w, conv_b)
    gate_pars = [w["rg_wa_bd"], w["rg_wx_bd"], w["rg_b_a"], w["rg_b_x"], w["rg_lambda"]]
    a_f, u_f, a_b, u_b = _rowcall("rg_gates", _rg_gates_fn, [(xc, rg_w, 0)], gate_pars, [(rg_w, F32)] * 4)
    hs_f, hs_b = _scan_fwd("rg_scan", a_f, u_f, a_b, u_b)
    hg_rows = [(proj0, hg_w, 2), (proj0, hg_w, 3), (proj0, hg_w, 4)]
    qh, k_f, lf_f, k_b, lf_b = _rowcall("hg_pre", _hg_pre_fn, hg_rows, [w["hg_lb_logits"]], [(hg_w, F32)] * 5)
    iv = (proj0, 5)
    o_f, o_b, st_f, st_b = _attn_fwd("hg_attn", (qh, 0), (k_f, 0), (k_b, 0), iv, (lf_f, 0), (lf_b, 0), 4, 128, 128)
    post0_rows = [(hs_f, rg_w, 0), (hs_b, rg_w, 0), (proj0, rg_w, 1), (o_f, hg_w, 0), (o_b, hg_w, 0), (proj0, hg_w, 6)]
    hg_gain = _row2(w["hg_norm"])
    mix_in0 = _rowcall("l0_post", _post0_fwd_fn, post0_rows, [hg_gain], [(d_model, BF16)])[0]
    if late is not None:
        w = {**w, **late(mix_in0)}
    h_b0 = _mm("l0_out", mix_in0, w["ab_w_out"], extras=(h_a0,), epi=_add_epi)
    h_c0, mlp0 = _mlp_fwd("mlp0", h_b0, _row2(w["norm_mlp"][0]), w["mlp_w1"][0], w["mlp_w2"][0])

    h_a1 = h_c0
    gain1 = _row2(w["norm_mix"][1])
    y1 = _rowcall("l1_norm", _rmsnorm_fn, [(h_a1, d_model, 0)], [gain1], [(d_model, BF16)], tm=512)[0]
    proj1 = _mm("l1_in", y1, w["gla_w_in_pad"], tn=640)
    gla_pars = [w["gla_w_up_pad"], w["gla_b_gate"]]
    gq, glf_f, glf_b = _rowcall("gla_pre", _gla_pre_fn, [(proj1, 512, 0), (proj1, LANES, 24)], gla_pars, [(512, F32)] * 3)
    gk, gv = (proj1, 1), (proj1, 1)
    go_f, go_b, gst_f, gst_b = _attn_fwd("gla_attn", (gq, 0), gk, gk, gv, (glf_f, 0), (glf_b, 0), 4, 128, 256)
    gla_gain = _row2(w["gla_norm"])
    post1_rows = [(go_f, d_model, 0), (go_b, d_model, 0), (proj1, d_model, 2)]
    mix_in1 = _rowcall("l1_post", _gla_post_fwd_fn, post1_rows, [gla_gain], [(d_model, BF16)])[0]
    h_b1 = _mm("l1_out", mix_in1, w["gla_w_out"], extras=(h_a1,), epi=_add_epi)
    h_c1, mlp1 = _mlp_fwd("mlp1", h_b1, _row2(w["norm_mlp"][1]), w["mlp_w1"][1], w["mlp_w2"][1])

    *dh, loss, g["norm_final"] = _rowcall(
        "loss_head", _loss_head_fn, [(h_c1, d_model, 0), (target, d_model, 0)], [_row2(w["norm_final"])],
        [(d_model, F32), (d_model, BF16)], [(1, LANES), (1, d_model)], tm=512)

    dh, g_nmlp1, g_w1_1, g_w2_1 = _mlp_bwd("mlp1", h_b1, _row2(w["norm_mlp"][1]), w["mlp_w1"][1], w["mlp_w2"][1], mlp1, dh)
    send_off("mlp1", [g_w1_1, both(chip_major, g_w2_1)])
    dmix1 = _mm("l1_dout", dh[1], w["gla_w_out"], mode="nt")
    g_gla_out = _dw("l1_dwout", mix_in1, dh[1])
    g["gla_w_out"] = g_gla_out[0]
    dgo, dr, g["gla_norm"] = _rowcall(
        "l1_dpost", _gla_post_bwd_fn, post1_rows + [(dmix1, d_model, 0)], [gla_gain],
        [(d_model, F32), (d_model, BF16)], [(1, d_model)], pin=pins.pop() if pins else None)
    (dq_f, dk_f, dv_f, dlf_f), (dq_b, dk_b, dv_b, dlf_b) = _attn_bwd(
        "gla_dattn", (gq, 0), gk, gk, gv, (glf_f, 0), (glf_b, 0), gst_f, gst_b, dgo, 4, 128, 256)

    def gla_pre_bwd(q, lr, dq1, dq2, dlf1, dlf2, dk1, dk2, dv1, dv2, w_up, b_gate):
        dlr = jnp.zeros_like(lr)
        dws, dbs = [], []
        for d, dlf in enumerate((dlf1, dlf2)):
            z = _raw_nn(lr, w_up[d]) + b_gate[d:d + 1]
            dz = dlf * _sigmoid(-z) * (1.0 / 16.0)
            dlr = dlr + _raw_nt(dz, w_up[d])
            dws.append(_raw_tn(dz, lr))
            dbs.append(jnp.sum(dz, axis=0, keepdims=True))
        return ((dq1 + dq2) * (128.0 ** -0.5), dk1 + dk2, dv1 + dv2, dlr, dws[0], dws[1], dbs[0], dbs[1])

    rows = [(proj1, 512, 0), (proj1, LANES, 24), (dq_f, 512, 0), (dq_b, 512, 0), (dlf_f, 512, 0), (dlf_b, 512, 0),
            (dk_f, 512, 0), (dk_b, 512, 0), (dv_f, d_model, 0), (dv_b, d_model, 0)]
    dq, dk, dv, dlr, dwt_f, dwt_b, db_f, db_b = _rowcall(
        "gla_dpre", gla_pre_bwd, rows, gla_pars, [(512, BF16), (512, BF16), (d_model, BF16), (LANES, BF16)],
        [(512, LANES), (512, LANES), (1, 512), (1, 512)])
    g["gla_w_up_pad"] = jnp.stack([dwt_f.T, dwt_b.T])
    g["gla_b_gate"] = jnp.concatenate([db_f, db_b], axis=0)
    dproj1 = jnp.concatenate([dq, dk, dv, dr, dlr], axis=1)
    g_gla_in = both(lambda t: _split_chips(t[:, :GLA_IN_WIDTH], 1), _dw("l1_dwin", y1, dproj1, tn=640, tk=2048))
    g["gla_w_in"] = g_gla_in[0]
    send_off("gla", [g_gla_in, both(chip_major, g_gla_out)])
    dh, g_nmix1 = _dy_norm_bwd("l1_dy", dproj1, w["gla_w_in_pad"], h_a1, gain1, dh[0],
                               pin=pins.pop() if pins else None, tk=640)

    dh, g_nmlp0, g_w1_0, g_w2_0 = _mlp_bwd("mlp0", h_b0, _row2(w["norm_mlp"][0]), w["mlp_w1"][0], w["mlp_w2"][0], mlp0, dh)
    g_ab_out = _dw("l0_dwout", mix_in0, dh[1])
    g["ab_w_out"] = g_ab_out[0]
    send_off("mlp0", [g_w1_0, both(chip_major, g_w2_0), both(chip_major, g_ab_out)])
    dmix0 = _mm("l0_dout", dh[1], w["ab_w_out"], mode="nt")
    dhs, dga, do, dg, g["hg_norm"] = _rowcall(
        "l0_dpost", _post0_bwd_fn, post0_rows + [(dmix0, d_model, 0)], [hg_gain],
        [(rg_w, F32), (rg_w, BF16), (hg_w, F32), (hg_w, BF16)], [(1, hg_w)], pin=pins.pop() if pins else None)
    (dqh_f, dk_f, div_f, dlf_f), (dqh_b, dk_b, div_b, dlf_b) = _attn_bwd(
        "hg_dattn", (qh, 0), (k_f, 0), (k_b, 0), iv, (lf_f, 0), (lf_b, 0), st_f, st_b, do, 4, 128, 128)

    def hg_pre_bwd(q, f_f, f_b, dq1, dq2, dk1, dlf1, dk2, dlf2, dv1, dv2, logits):
        _, vjp = jax.vjp(_hg_pre_fn, q, f_f, f_b, logits)
        dq, df_f, df_b, dlogits = vjp((dq1 + dq2, dk1, dlf1, dk2, dlf2))
        return dq, df_f, df_b, dv1 + dv2, dlogits

    rows = hg_rows + [(t, hg_w, 0) for t in (dqh_f, dqh_b, dk_f, dlf_f, dk_b, dlf_b, div_f, div_b)]
    dq, df_f, df_b, div, g["hg_lb_logits"] = _rowcall(
        "hg_dpre", hg_pre_bwd, rows, [w["hg_lb_logits"]], [(hg_w, BF16)] * 4, [(2, hg_w)])
    du_f, da_f, du_b, da_b = _scan_bwd("rg_dscan", a_f, hs_f, a_b, hs_b, dhs)
    gates_bwd = _vjp_of(_rg_gates_fn, 1, 4, 5)
    rows = [(xc, rg_w, 0), (da_f, rg_w, 0), (du_f, rg_w, 0), (da_b, rg_w, 0), (du_b, rg_w, 0)]
    dxc, g["rg_wa_bd"], g["rg_wx_bd"], g["rg_b_a"], g["rg_b_x"], g["rg_lambda"] = _rowcall(
        "rg_dgates", gates_bwd, rows, gate_pars, [(rg_w, F32)],
        [(2, rg_w, rg_w), (2, rg_w, rg_w), (2, rg_w), (2, rg_w), (2, rg_w)])
    dxa, g["rg_conv_w"], g["rg_conv_b"] = _conv_bwd("rg_dconv", proj0, 0, conv_w, dxc)
    dproj0 = jnp.concatenate([dxa, dga, dq, df_f, df_b, div, dg], axis=1)
    g_ab_in = _dw("l0_dwin", y0, dproj0, out_split=N_CHIPS, tk=2048)
    g["ab_w_in"] = g_ab_in[0]
    send_off("ab", [g_ab_in])
    (grad_x,), g_nmix0 = _dy_norm_bwd("l0_dy", dproj0, w["ab_w_in"], h_a0, gain, dh[0],
                                      pin=pins.pop() if pins else None, twice=False)

    g["norm_mix"] = jnp.concatenate([g_nmix0, g_nmix1], axis=0)
    g["norm_mlp"] = jnp.concatenate([g_nmlp0, g_nmlp1], axis=0)
    g["mlp_w1"] = [g_w1_0[0], g_w1_1[0]]
    g["mlp_w2"] = [g_w2_0[0], g_w2_1[0]]
    return loss, grad_x, g


def _block_diag(w):
    d, g, n, _ = w.shape
    eye = jnp.eye(g, dtype=w.dtype)
    return (w[:, :, :, None, :] * eye[None, :, None, :, None]).reshape(d, g * n, g * n)


def _block_diag_extract(wbd, g):
    d, gn, _ = wbd.shape
    n = gn // g
    blocks = wbd.reshape(d, g, n, g, n)
    return jnp.stack([blocks[:, i, :, i, :] for i in range(g)], axis=1)


def _prepare_weights(big, full):
    w = {k: full[k] for k in ("norm_mix", "norm_mlp", "norm_final", "hg_lb_logits")}
    for k in ("rg_conv_w", "rg_conv_b", "rg_b_a", "rg_b_x", "rg_lambda", "hg_norm", "gla_b_gate", "gla_norm"):
        w[k] = full[k][0]
    w["rg_wa_bd"] = _block_diag(full["rg_w_a"][0])
    w["rg_wx_bd"] = _block_diag(full["rg_w_x"][0])
    up = full["gla_w_gate_up"][0]
    rank = up.shape[1]
    pad = jnp.zeros((2, LANES, up.shape[2]), F32)
    w["gla_w_up_pad"] = pad.at[0, 0:rank].set(up[0]).at[1, rank:2 * rank].set(up[1])
    w.update(_prepare_matrices(big))
    return w


def _prepare_matrices(big):
    w = {}
    if "mlp_w1" in big:
        w["mlp_w1"] = list(big["mlp_w1"])
        w["mlp_w2"] = [t.reshape(-1, t.shape[-1]) for t in big["mlp_w2"]]
    if "ab_w_in" in big:
        w["ab_w_in"] = big["ab_w_in"]
    if "ab_w_out" in big:
        w["ab_w_out"] = big["ab_w_out"].reshape(-1, big["ab_w_out"].shape[-1])
    if "gla_w_in" in big:
        w["gla_w_out"] = big["gla_w_out"].reshape(-1, big["gla_w_out"].shape[-1])
        gla_in = _join_chips(big["gla_w_in"], 1)
        w["gla_w_in_pad"] = jnp.pad(gla_in, ((0, 0), (0, GLA_IN_PAD - gla_in.shape[1])))
    return w


def _finish_grads(g, rank=16, rg_blocks=8):
    def chip_major(t):
        return t.reshape(N_CHIPS, t.shape[0] // N_CHIPS, t.shape[1])

    big = {
        "mlp_w1": list(g["mlp_w1"]), "mlp_w2": [chip_major(t) for t in g["mlp_w2"]],
        "ab_w_in": g["ab_w_in"], "ab_w_out": chip_major(g["ab_w_out"]),
        "gla_w_in": g["gla_w_in"], "gla_w_out": chip_major(g["gla_w_out"]),
    }
    small = {
        "norm_mix": g["norm_mix"], "norm_mlp": g["norm_mlp"], "norm_final": g["norm_final"][0],
        "rg_conv_w": g["rg_conv_w"][None], "rg_conv_b": g["rg_conv_b"],
        "rg_w_a": _block_diag_extract(g["rg_wa_bd"], rg_blocks)[None], "rg_b_a": g["rg_b_a"][None],
        "rg_w_x": _block_diag_extract(g["rg_wx_bd"], rg_blocks)[None], "rg_b_x": g["rg_b_x"][None],
        "rg_lambda": g["rg_lambda"][None], "hg_lb_logits": g["hg_lb_logits"], "hg_norm": g["hg_norm"],
        "gla_w_gate_up": jnp.stack([g["gla_w_up_pad"][0, 0:rank], g["gla_w_up_pad"][1, rank:2 * rank]])[None],
        "gla_b_gate": g["gla_b_gate"][None], "gla_norm": g["gla_norm"],
    }
    return big, small


MATRICES = (("mlp_w1", 0), ("mlp_w1", 1), ("mlp_w2", 0), ("mlp_w2", 1), ("ab_w_in", 0), ("ab_w_out", 0),
            ("gla_w_in", 0), ("gla_w_out", 0))
EARLY_MATRICES = ("ab_w_in",)
SMALL_SHARDED = ("rg_conv_w", "rg_b_a", "rg_b_x", "rg_lambda", "gla_w_gate_up", "gla_b_gate", "gla_norm")
SMALL_REPLICATED = ("norm_mix", "norm_mlp", "norm_final", "rg_conv_b", "rg_w_a", "rg_w_x", "hg_lb_logits", "hg_norm")
WEIGHTS = ("norm_mix", "norm_mlp", "norm_final", "mlp_w1", "mlp_w2", "ab_w_in", "ab_w_out", "rg_conv_w", "rg_conv_b",
           "rg_w_a", "rg_b_a", "rg_w_x", "rg_b_x", "rg_lambda", "hg_lb_logits", "hg_norm", "gla_w_in", "gla_w_out",
           "gla_w_gate_up", "gla_b_gate", "gla_norm")
ROW_ALIGN = 16


def _pack(arrays, lead=0):
    head = arrays[0].shape[:lead]
    flat = jnp.concatenate([a.reshape(head + (-1,)) for a in arrays], axis=lead)
    n = flat.shape[-1]
    quantum = LANES * ROW_ALIGN
    padded = -(-n // quantum) * quantum
    if padded != n:
        flat = jnp.pad(flat, [(0, 0)] * lead + [(0, padded - n)])
    return flat.reshape(head + (padded // LANES, LANES))


def _unpack(buf, shapes, lead=0):
    head = buf.shape[:lead]
    flat = buf.reshape(head + (-1,))
    out, off = [], 0
    for s in shapes:
        n = 1
        for v in s:
            n *= v
        out.append(lax.slice_in_dim(flat, off, off + n, axis=lead).reshape(head + tuple(s)))
        off += n
    return out


def _join_chips(gathered, axis):
    t = jnp.moveaxis(gathered, 0, axis)
    return t.reshape(t.shape[:axis] + (t.shape[axis] * t.shape[axis + 1],) + t.shape[axis + 2:])


def _split_chips(full, axis):
    s = full.shape
    t = full.reshape(s[:axis] + (N_CHIPS, s[axis] // N_CHIPS) + s[axis + 1:])
    return jnp.moveaxis(t, axis, 0)


_ANY = pl.BlockSpec(memory_space=pl.ANY)


def _place():
    return lax.axis_index("x"), lax.axis_index("y"), lax.axis_index("c")


def _into_slot(name, src, slot, n_slots, dtype, tm, layer=None):
    r, lanes = src.shape[-2:]
    tm = _row_tile(r, tm, ROW_ALIGN)

    def body(slot_ref, in_ref, o_ref):
        o_ref[...] = in_ref[...].astype(o_ref.dtype)

    if layer is None:
        in_spec = pl.BlockSpec((tm, lanes), lambda i, slot_ref: (i, 0))
    else:
        in_spec = pl.BlockSpec((None, tm, lanes), lambda i, slot_ref: (layer, i, 0))
    grid_spec = pltpu.PrefetchScalarGridSpec(
        num_scalar_prefetch=1, grid=(r // tm,), in_specs=[in_spec],
        out_specs=pl.BlockSpec((None, tm, lanes), lambda i, slot_ref: (slot_ref[0], i, 0)))
    return pl.pallas_call(
        body, name=name, grid_spec=grid_spec, out_shape=jax.ShapeDtypeStruct((n_slots, r, lanes), dtype),
        compiler_params=_params(("parallel",)),
    )(slot.reshape(1).astype(jnp.int32), src)


def _chip_peers():
    x, y, c = _place()
    return 2 * x + y, c, [(1 - x, y), (x, 1 - y), (1 - x, 1 - y)]


def _comm_call(name, body, ins, out_shapes, n_sems, aliases=None):
    return pl.pallas_call(
        body, name=name, in_specs=[_ANY] * len(ins), out_specs=[_ANY] * len(out_shapes), out_shape=out_shapes,
        input_output_aliases=aliases or {},
        scratch_shapes=[pltpu.SemaphoreType.DMA((n_sems,)), pltpu.SemaphoreType.DMA((n_sems,))],
    )(*ins)


def _gather_chips(name, bufs):
    n = len(bufs)

    def body(*refs):
        outs, send_sems, recv_sems = refs[n:2 * n], refs[2 * n], refs[2 * n + 1]
        x, y, c = _place()
        me, _, peers = _chip_peers()

        def rows(a, block, half):
            rh = outs[a].shape[1] // 2
            return outs[a].at[block, pl.ds(half * rh, rh)]

        def copy(a, j, block, half, to, sem):
            return pltpu.make_async_remote_copy(
                src_ref=rows(a, block, half), dst_ref=rows(a, block, half), send_sem=send_sems.at[sem],
                recv_sem=recv_sems.at[sem], device_id=to, device_id_type=MESH)

        def over_ici(a, j, block):
            px, py = peers[j]
            return copy(a, j, block, c, (px, py, c), 6 * a + j)

        def to_sibling(a, j, block, half):
            return copy(a, j, block, half, (x, y, 1 - c), 6 * a + 3 + j)

        sends = [over_ici(a, j, me) for a in range(n) for j in range(3)]
        for cp in sends:
            cp.start()
        for a in range(n):
            for j, (px, py) in enumerate(peers):
                over_ici(a, j, 2 * px + py).wait_recv()
                handed = to_sibling(a, j, 2 * px + py, c)
                handed.start()
                sends.append(handed)
        for a in range(n):
            for j, (px, py) in enumerate(peers):
                to_sibling(a, j, 2 * px + py, 1 - c).wait_recv()
        for cp in sends:
            cp.wait_send()

    shapes = [jax.ShapeDtypeStruct(b.shape, b.dtype) for b in bufs]
    return _comm_call(name, body, bufs, shapes, 6 * n, {a: a for a in range(n)})


_HBM = pl.BlockSpec(memory_space=pltpu.HBM)
_SEM = pl.BlockSpec(memory_space=pltpu.SEMAPHORE)
_EFFECT = pltpu.SideEffectType.DATAFLOW_SIDE_EFFECTING


def _half_rows(ref, block, half):
    rh = ref.shape[1] // 2
    return ref.at[block, pl.ds(half * rh, rh)]


def _gather_start(name, bufs, after):
    n = len(bufs)

    def body(*refs):
        ins, send_sems, recv_sems, token = refs[:n], refs[n + 1], refs[n + 2], refs[-1]
        me, c, peers = _chip_peers()
        for a in range(n):
            mine = _half_rows(ins[a], me, c)
            for j, (px, py) in enumerate(peers):
                pltpu.make_async_remote_copy(
                    src_ref=mine, dst_ref=mine, send_sem=send_sems.at[3 * a + j], recv_sem=recv_sems.at[3 * a + j],
                    device_id=(px, py, c), device_id_type=MESH).start()
        token[...] = jnp.zeros_like(token)

    out_shape = (pltpu.SemaphoreType.DMA((3 * n,)), pltpu.SemaphoreType.DMA((3 * n,)),
                 *[pltpu.HBM(b.shape, b.dtype) for b in bufs], jax.ShapeDtypeStruct((8, LANES), F32))
    return pl.pallas_call(
        body, name=name, out_shape=out_shape, in_specs=[_HBM] * n + [_ANY],
        out_specs=(_SEM, _SEM, *[_HBM] * n, pl.BlockSpec(memory_space=pltpu.VMEM)),
        input_output_aliases={a: 2 + a for a in range(n)},
        compiler_params=pltpu.CompilerParams(has_side_effects=_EFFECT),
    )(*[pltpu.with_memory_space_constraint(b, pltpu.HBM) for b in bufs], after)


def _gather_wait(name, bufs, send_sems, recv_sems, after):
    n = len(bufs)

    def body(*refs):
        ins, send_sems, recv_sems = refs[:n], refs[n], refs[n + 1]
        me, c, peers = _chip_peers()
        for a in range(n):
            for j, (px, py) in enumerate(peers):
                copy = pltpu.make_async_remote_copy(
                    src_ref=_half_rows(ins[a], me, c), dst_ref=_half_rows(ins[a], 2 * px + py, c),
                    send_sem=send_sems.at[3 * a + j], recv_sem=recv_sems.at[3 * a + j],
                    device_id=(px, py, c), device_id_type=MESH)
                copy.wait_send()
                copy.wait_recv()

    return pl.pallas_call(
        body, name=name, out_shape=tuple(pltpu.HBM(b.shape, b.dtype) for b in bufs),
        in_specs=[_HBM] * n + [_SEM, _SEM, _ANY], out_specs=tuple([_HBM] * n),
        input_output_aliases={a: a for a in range(n)},
        compiler_params=pltpu.CompilerParams(has_side_effects=_EFFECT),
    )(*bufs, send_sems, recv_sems, after)


def _hand_over(name, bufs):
    n = len(bufs)

    def body(*refs):
        outs, send_sems, recv_sems = refs[n:2 * n], refs[2 * n], refs[2 * n + 1]
        x, y, c = _place()
        _, _, peers = _chip_peers()

        def copy(a, j, half):
            px, py = peers[j]
            rows = _half_rows(outs[a], 2 * px + py, half)
            return pltpu.make_async_remote_copy(
                src_ref=rows, dst_ref=rows, send_sem=send_sems.at[3 * a + j], recv_sem=recv_sems.at[3 * a + j],
                device_id=(x, y, 1 - c), device_id_type=MESH)

        sends = [copy(a, j, c) for a in range(n) for j in range(3)]
        for cp in sends:
            cp.start()
        for a in range(n):
            for j in range(3):
                copy(a, j, 1 - c).wait_recv()
        for cp in sends:
            cp.wait_send()

    shapes = [jax.ShapeDtypeStruct(b.shape, b.dtype) for b in bufs]
    return _comm_call(name, body, bufs, shapes, 3 * n, {a: a for a in range(n)})


def _pair_gather(name, bufs):
    n = len(bufs)

    def body(*refs):
        ins, outs, send_sems, recv_sems = refs[:n], refs[n:2 * n], refs[2 * n], refs[2 * n + 1]
        x, y, c = _place()

        def copy(a, block):
            return pltpu.make_async_remote_copy(
                src_ref=ins[a].at[block], dst_ref=outs[a].at[block], send_sem=send_sems.at[a],
                recv_sem=recv_sems.at[a], device_id=(x, y, 1 - c), device_id_type=MESH)

        sends = [copy(a, c) for a in range(n)]
        for cp in sends:
            cp.start()
        for a in range(n):
            copy(a, 1 - c).wait_recv()
        for cp in sends:
            cp.wait_send()

    shapes = [jax.ShapeDtypeStruct(b.shape, b.dtype) for b in bufs]
    return _comm_call(name, body, bufs, shapes, n, {a: a for a in range(n)})


def _all_peers():
    x, y, c = _place()
    peers = []
    for mask in range(1, N_DEV):
        fx, fy, fc = (mask >> 2) & 1, (mask >> 1) & 1, mask & 1
        peers.append((jnp.where(fx, 1 - x, x), jnp.where(fy, 1 - y, y), jnp.where(fc, 1 - c, c)))
    return 4 * x + 2 * y + c, peers


def _reduce_copies(srcs, lands, send_sems, recv_sems):
    me, peers = _all_peers()
    sends, arrivals = [], []
    for a in range(len(srcs)):
        for j, (px, py, pc) in enumerate(peers):
            k = (N_DEV - 1) * a + j
            sends.append(pltpu.make_async_remote_copy(
                src_ref=srcs[a].at[2 * px + py, pc], dst_ref=lands[a].at[me], send_sem=send_sems.at[k],
                recv_sem=recv_sems.at[k], device_id=(px, py, pc), device_id_type=MESH))
            arrivals.append(pltpu.make_async_remote_copy(
                src_ref=srcs[a].at[2 * px + py, pc], dst_ref=lands[a].at[4 * px + 2 * py + pc],
                send_sem=send_sems.at[k], recv_sem=recv_sems.at[k], device_id=(px, py, pc), device_id_type=MESH))
    return sends, arrivals


def _reduce_direct(name, srcs, pin=None):
    n = len(srcs)
    extra = [] if pin is None else [pin]

    def body(*refs):
        ins, outs = refs[:n], refs[n + len(extra):2 * n + len(extra)]
        sends, arrivals = _reduce_copies(ins, outs, refs[-2], refs[-1])
        for cp in sends:
            cp.start()
        for cp in arrivals:
            cp.wait_recv()
        for cp in sends:
            cp.wait_send()

    shapes = [jax.ShapeDtypeStruct((N_DEV,) + s.shape[2:], s.dtype) for s in srcs]
    return _comm_call(name, body, list(srcs) + extra, shapes, (N_DEV - 1) * n)


def _reduce_start(name, srcs):
    n = len(srcs)
    lands = [lax.empty((N_DEV,) + s.shape[2:], s.dtype) for s in srcs]

    def body(*refs):
        sends, _ = _reduce_copies(refs[:n], refs[n:2 * n], refs[2 * n], refs[2 * n + 1])
        for cp in sends:
            cp.start()
        refs[-1][...] = jnp.zeros_like(refs[-1])

    bufs = list(srcs) + lands
    n_sems = (N_DEV - 1) * n
    out_shape = (pltpu.SemaphoreType.DMA((n_sems,)), pltpu.SemaphoreType.DMA((n_sems,)),
                 *[pltpu.HBM(b.shape, b.dtype) for b in bufs], jax.ShapeDtypeStruct((8, LANES), F32))
    return pl.pallas_call(
        body, name=name, out_shape=out_shape, in_specs=[_HBM] * (2 * n),
        out_specs=(_SEM, _SEM, *[_HBM] * (2 * n), pl.BlockSpec(memory_space=pltpu.VMEM)),
        input_output_aliases={a: 2 + a for a in range(2 * n)},
        compiler_params=pltpu.CompilerParams(has_side_effects=_EFFECT),
    )(*[pltpu.with_memory_space_constraint(b, pltpu.HBM) for b in bufs])


def _reduce_wait(name, srcs, lands, send_sems, recv_sems, after):
    n = len(srcs)

    def body(*refs):
        sends, arrivals = _reduce_copies(refs[:n], refs[n:2 * n], refs[2 * n], refs[2 * n + 1])
        for cp in sends:
            cp.wait_send()
        for cp in arrivals:
            cp.wait_recv()

    bufs = list(srcs) + list(lands)
    outs = pl.pallas_call(
        body, name=name, out_shape=tuple(pltpu.HBM(b.shape, b.dtype) for b in bufs),
        in_specs=[_HBM] * (2 * n) + [_SEM, _SEM, _ANY], out_specs=tuple([_HBM] * (2 * n)),
        input_output_aliases={a: a for a in range(2 * n)},
        compiler_params=pltpu.CompilerParams(has_side_effects=_EFFECT),
    )(*bufs, send_sems, recv_sems, after)
    return list(outs[n:])


def _reduce_sum(name, own, land, chip, core):
    n, rh, lanes = land.shape
    tm = _row_tile(rh, 1024, ROW_ALIGN)

    def body(idx_ref, own_ref, *rest):
        total = own_ref[...]
        for g_ref in rest[:-1]:
            total = total + g_ref[...].astype(F32)
        rest[-1][...] = total

    def block(k):
        return pl.BlockSpec((None, tm, lanes), lambda i, idx_ref: ((2 * idx_ref[0] + idx_ref[1] + k) % n, i, 0))

    grid_spec = pltpu.PrefetchScalarGridSpec(
        num_scalar_prefetch=1, grid=(rh // tm,),
        in_specs=[pl.BlockSpec((None, None, tm, lanes), lambda i, idx_ref: (idx_ref[0], idx_ref[1], i, 0))]
        + [block(k) for k in range(1, n)],
        out_specs=pl.BlockSpec((None, tm, lanes), lambda i, idx_ref: (idx_ref[1], i, 0)))
    return pl.pallas_call(
        body, name=name, grid_spec=grid_spec, out_shape=jax.ShapeDtypeStruct((2, rh, lanes), F32),
        compiler_params=_params(("parallel",)),
    )(jnp.stack([chip, core]).astype(jnp.int32), own, *[land] * (n - 1))


def _gather_all_start(name, buf):
    def body(in_ref, send_sems, recv_sems, out_ref, token):
        me, peers = _all_peers()
        for j, peer in enumerate(peers):
            pltpu.make_async_remote_copy(
                src_ref=in_ref.at[me], dst_ref=in_ref.at[me], send_sem=send_sems.at[j], recv_sem=recv_sems.at[j],
                device_id=peer, device_id_type=MESH).start()
        token[...] = jnp.zeros_like(token)

    n = N_DEV - 1
    return pl.pallas_call(
        body, name=name, in_specs=[_HBM],
        out_shape=(pltpu.SemaphoreType.DMA((n,)), pltpu.SemaphoreType.DMA((n,)), pltpu.HBM(buf.shape, buf.dtype),
                   jax.ShapeDtypeStruct((8, LANES), F32)),
        out_specs=(_SEM, _SEM, _HBM, pl.BlockSpec(memory_space=pltpu.VMEM)), input_output_aliases={0: 2},
        compiler_params=pltpu.CompilerParams(has_side_effects=_EFFECT),
    )(pltpu.with_memory_space_constraint(buf, pltpu.HBM))


def _gather_all_wait(name, buf, send_sems, recv_sems, after):
    def body(in_ref, send_sems, recv_sems, after_ref, out_ref):
        me, peers = _all_peers()
        for j, (px, py, pc) in enumerate(peers):
            copy = pltpu.make_async_remote_copy(
                src_ref=in_ref.at[me], dst_ref=in_ref.at[4 * px + 2 * py + pc], send_sem=send_sems.at[j],
                recv_sem=recv_sems.at[j], device_id=(px, py, pc), device_id_type=MESH)
            copy.wait_send()
            copy.wait_recv()

    return pl.pallas_call(
        body, name=name, in_specs=[_HBM, _SEM, _SEM, _ANY], out_shape=pltpu.HBM(buf.shape, buf.dtype),
        out_specs=_HBM, input_output_aliases={0: 0},
        compiler_params=pltpu.CompilerParams(has_side_effects=_EFFECT),
    )(buf, send_sems, recv_sems, after)


def _sum_blocks(name, stacked, tm):
    n, r, lanes = stacked.shape

    def body(in_ref, o_ref):
        acc = in_ref[0]
        for j in range(1, n):
            acc = acc + in_ref[j]
        o_ref[...] = acc

    return pl.pallas_call(
        body, name=name, grid=(r // tm,), in_specs=[pl.BlockSpec((n, tm, lanes), lambda i: (0, i, 0))],
        out_specs=pl.BlockSpec((tm, lanes), lambda i: (i, 0)), out_shape=jax.ShapeDtypeStruct((r, lanes), F32),
        compiler_params=_params(("parallel",)),
    )(stacked)


def _row_tile(rows, pref, align):
    best = None
    for t in range(align, min(rows, pref) + 1, align):
        if rows % t == 0:
            best = t
    assert best is not None, (rows, pref, align)
    return best


def _adam(name, w, g, m, v):
    rows, width = w.shape
    tm = _row_tile(rows, max(8, 4096 * LANES // width), 8)
    args = [(t, width, 0) for t in (w, g, m, v)]
    return _rowcall(name, _adam_fn, args, [], [(width, F32)] * 3, tm=tm)


def kernel(x, norm_mix, norm_mlp, norm_final, mlp_w1, mlp_w2, ab_w_in, ab_w_out, rg_conv_w, rg_conv_b, rg_w_a, rg_b_a, rg_w_x, rg_b_x, rg_lambda, hg_lb_logits, hg_norm, gla_w_in, gla_w_out, gla_w_gate_up, gla_b_gate, gla_norm, loss_target, m_norm_mix, m_norm_mlp, m_norm_final, m_mlp_w1, m_mlp_w2, m_ab_w_in, m_ab_w_out, m_rg_conv_w, m_rg_conv_b, m_rg_w_a, m_rg_b_a, m_rg_w_x, m_rg_b_x, m_rg_lambda, m_hg_lb_logits, m_hg_norm, m_gla_w_in, m_gla_w_out, m_gla_w_gate_up, m_gla_b_gate, m_gla_norm, v_norm_mix, v_norm_mlp, v_norm_final, v_mlp_w1, v_mlp_w2, v_ab_w_in, v_ab_w_out, v_rg_conv_w, v_rg_conv_b, v_rg_w_a, v_rg_b_a, v_rg_w_x, v_rg_b_x, v_rg_lambda, v_hg_lb_logits, v_hg_norm, v_gla_w_in, v_gla_w_out, v_gla_w_gate_up, v_gla_b_gate, v_gla_norm):
    w = dict(norm_mix=norm_mix, norm_mlp=norm_mlp, norm_final=norm_final, mlp_w1=mlp_w1, mlp_w2=mlp_w2, ab_w_in=ab_w_in, ab_w_out=ab_w_out, rg_conv_w=rg_conv_w, rg_conv_b=rg_conv_b, rg_w_a=rg_w_a, rg_b_a=rg_b_a, rg_w_x=rg_w_x, rg_b_x=rg_b_x, rg_lambda=rg_lambda, hg_lb_logits=hg_lb_logits, hg_norm=hg_norm, gla_w_in=gla_w_in, gla_w_out=gla_w_out, gla_w_gate_up=gla_w_gate_up, gla_b_gate=gla_b_gate, gla_norm=gla_norm)
    m = dict(norm_mix=m_norm_mix, norm_mlp=m_norm_mlp, norm_final=m_norm_final, mlp_w1=m_mlp_w1, mlp_w2=m_mlp_w2, ab_w_in=m_ab_w_in, ab_w_out=m_ab_w_out, rg_conv_w=m_rg_conv_w, rg_conv_b=m_rg_conv_b, rg_w_a=m_rg_w_a, rg_b_a=m_rg_b_a, rg_w_x=m_rg_w_x, rg_b_x=m_rg_b_x, rg_lambda=m_rg_lambda, hg_lb_logits=m_hg_lb_logits, hg_norm=m_hg_norm, gla_w_in=m_gla_w_in, gla_w_out=m_gla_w_out, gla_w_gate_up=m_gla_w_gate_up, gla_b_gate=m_gla_b_gate, gla_norm=m_gla_norm)
    v = dict(norm_mix=v_norm_mix, norm_mlp=v_norm_mlp, norm_final=v_norm_final, mlp_w1=v_mlp_w1, mlp_w2=v_mlp_w2, ab_w_in=v_ab_w_in, ab_w_out=v_ab_w_out, rg_conv_w=v_rg_conv_w, rg_conv_b=v_rg_conv_b, rg_w_a=v_rg_w_a, rg_b_a=v_rg_b_a, rg_w_x=v_rg_w_x, rg_b_x=v_rg_b_x, rg_lambda=v_rg_lambda, hg_lb_logits=v_hg_lb_logits, hg_norm=v_hg_norm, gla_w_in=v_gla_w_in, gla_w_out=v_gla_w_out, gla_w_gate_up=v_gla_w_gate_up, gla_b_gate=v_gla_b_gate, gla_norm=v_gla_norm)
    chip = 2 * lax.axis_index("x") + lax.axis_index("y")
    core = lax.axis_index("c")
    sharded_shapes = [w[n].shape for n in SMALL_SHARDED]

    slots = [_into_slot(f"cast_{n}{layer}", w[n], chip, N_CHIPS, BF16, 512, layer) for n, layer in MATRICES]
    early = [i for i, (n, _) in enumerate(MATRICES) if n in EARLY_MATRICES]
    rest = [i for i in range(len(MATRICES)) if i not in early]

    def named(indices, arrays):
        big = {}
        for i, t in zip(indices, arrays):
            big.setdefault(MATRICES[i][0], []).append(t)
        return {n: (v if n in ("mlp_w1", "mlp_w2") else v[0]) for n, v in big.items()}

    vectors = _pack([w[n] for n in SMALL_SHARDED])
    vectors = _into_slot("place_vectors", vectors, chip, N_CHIPS, F32, vectors.shape[0])
    *gathered, vectors = _gather_chips("gather_early", [slots[i] for i in early] + [vectors])
    send_sems, recv_sems, *in_flight, token = _gather_start("gather_rest_start", [slots[i] for i in rest], gathered[0])

    def late_weights(after):
        landed = _gather_wait("gather_rest_wait", in_flight, send_sems, recv_sems, after)
        return _prepare_matrices(named(rest, _hand_over("gather_rest_share", list(landed))))

    big = named(early, gathered)
    small_all = _unpack(vectors, sharded_shapes, lead=1)
    full = {n: w[n] for n in SMALL_REPLICATED}
    for n, t in zip(SMALL_SHARDED, small_all):
        full[n] = _join_chips(t, t.ndim - 2)

    def halves(t):
        return t.reshape(N_CHIPS, 2, t.shape[1] // 2, t.shape[2])

    in_flight_grads = {}

    def emit(tag, arrays32, arrays16):
        n = len(arrays16)
        send, recv, *rest = _reduce_start(f"reduce_{tag}_start", [halves(t) for t in arrays16])
        in_flight_grads[tag] = ([halves(t) for t in arrays32], rest[:n], rest[n:2 * n], send, recv)
        return rest[-1]

    loss_part, grad_x, g_kernel = _local_step(
        x[0], loss_target[0], _prepare_weights(big, full), token, late_weights, emit)
    g_big, g_full = _finish_grads(g_kernel)

    small_names = SMALL_REPLICATED + SMALL_SHARDED
    reduced_shapes = [g_full[n].shape for n in small_names] + [loss_part.shape]
    g_small = _pack([g_full[n] for n in small_names] + [loss_part])
    device = 2 * chip + core
    g_small = _into_slot("place_small", g_small, device, N_DEV, F32, g_small.shape[0])
    small_send, small_recv, small_in_flight, small_token = _gather_all_start("reduce_small_start", g_small)

    mine = {}
    for tag, (own, srcs, lands, send, recv) in in_flight_grads.items():
        landed = _reduce_wait(f"reduce_{tag}_wait", srcs, lands, send, recv, small_token)
        mine[tag] = [_reduce_sum(f"reduce_add_{tag}{i}", o, f, chip, core) for i, (o, f) in enumerate(zip(own, landed))]
    ordered = [mine["mlp0"][0], mine["mlp1"][0], mine["mlp0"][1], mine["mlp1"][1], mine["ab"][0], mine["mlp0"][2],
               *mine["gla"]]
    reduced = [t.reshape(2 * t.shape[1], t.shape[2]) for t in _pair_gather("reduce_share", ordered)]
    by_name = {n: [] for n, _ in MATRICES}
    for (n, _), t in zip(MATRICES, reduced):
        by_name[n].append(t)
    grads = {n: jnp.stack(v) for n, v in by_name.items()}

    g_small_all = _gather_all_wait("reduce_small_wait", small_in_flight, small_send, small_recv, reduced[0])
    g_small_red = _sum_blocks("reduce_small_add", g_small_all, g_small_all.shape[1])
    *small_red, loss_sum = _unpack(g_small_red, reduced_shapes)
    loss = loss_sum[0, 0]
    g_small_full = dict(zip(small_names, small_red))
    for n in SMALL_REPLICATED:
        grads[n] = g_small_full[n]
    for n in SMALL_SHARDED:
        width = w[n].shape[-1]
        grads[n] = lax.dynamic_slice_in_dim(g_small_full[n], chip * width, width, axis=g_small_full[n].ndim - 1)

    delta, new_m, new_v = {}, {}, {}
    for n in by_name:
        flat = [t.reshape(-1, t.shape[-1]) for t in (w[n], grads[n], m[n], v[n])]
        for dst, t in zip((delta, new_m, new_v), _adam(f"adam_{n}", *flat)):
            dst[n] = t.reshape(w[n].shape)
    small_shapes = [w[n].shape for n in small_names]
    packs = [_pack([src[n] for n in small_names]) for src in (w, grads, m, v)]
    d_small, m_small, v_small = _adam("adam_small", *packs)
    for dst, buf in ((delta, d_small), (new_m, m_small), (new_v, v_small)):
        dst.update(zip(small_names, _unpack(buf, small_shapes)))

    return (loss, grad_x[None], *[grads[n] for n in WEIGHTS], *[delta[n] for n in WEIGHTS],
            *[new_m[n] for n in WEIGHTS], *[new_v[n] for n in WEIGHTS])
```

```python
import functools

import jax
import jax.numpy as jnp
from jax import lax
from jax.experimental import pallas as pl
from jax.experimental.pallas import tpu as pltpu

F32 = jnp.float32
BF16 = jnp.bfloat16
MESH = pl.DeviceIdType.MESH

LANES = 128
CHUNK = 64
ATTN_SUB = 4
EPS = 1e-6
RG_C = 8.0
N_CHIPS = 4
N_DEV = 8
GLA_IN_WIDTH = 3104
GLA_IN_PAD = 3200
VMEM_LIMIT = 56 * 1024 * 1024

ADAM_LR = 0.001
ADAM_B1 = 0.9
ADAM_B2 = 0.999
ADAM_EPS = 1e-08
ADAM_WD = 0.01
ADAM_STEP = 10


def _raw_dot(a, b, ca, cb):
    return lax.dot_general(a.astype(BF16), b.astype(BF16), (((ca,), (cb,)), ((), ())),
                           preferred_element_type=F32)


def _raw_nn(a, b):
    return _raw_dot(a, b, 1, 0)


def _raw_nt(a, b):
    return _raw_dot(a, b, 1, 1)


def _raw_tn(a, b):
    return _raw_dot(a, b, 0, 0)


@jax.custom_vjp
def _dot_nn(a, b):
    return _raw_nn(a, b)


def _dot_nn_fwd(a, b):
    return _raw_nn(a, b), (a, b)


def _dot_nn_bwd(res, g):
    a, b = res
    return _raw_nt(g, b), _raw_tn(a, g)


_dot_nn.defvjp(_dot_nn_fwd, _dot_nn_bwd)


@jax.custom_vjp
def _dot_nt(a, b):
    return _raw_nt(a, b)


def _dot_nt_fwd(a, b):
    return _raw_nt(a, b), (a, b)


def _dot_nt_bwd(res, g):
    a, b = res
    return _raw_nn(g, b), _raw_tn(g, a)


_dot_nt.defvjp(_dot_nt_fwd, _dot_nt_bwd)


@jax.custom_vjp
def _dot_tn(a, b):
    return _raw_tn(a, b)


def _dot_tn_fwd(a, b):
    return _raw_tn(a, b), (a, b)


def _dot_tn_bwd(res, g):
    a, b = res
    return _raw_nt(b, g), _raw_nn(a, g)


_dot_tn.defvjp(_dot_tn_fwd, _dot_tn_bwd)


def _tile(n, pref):
    if n <= pref:
        return n
    t = (pref // LANES) * LANES
    while t > LANES and n % t:
        t -= LANES
    assert n % t == 0, (n, pref)
    return t


def _params(sem):
    return pltpu.CompilerParams(dimension_semantics=sem, vmem_limit_bytes=VMEM_LIMIT)


def _rowcall(name, fn, rows, pars, row_outs, par_outs=(), tm=512, pin=None):
    if pin is not None:
        inner, pars = fn, list(pars) + [pin]
        fn = lambda *vals: inner(*vals[:-1])
    n_rows = rows[0][0].shape[0]
    tm = min(tm, n_rows)
    assert n_rows % tm == 0
    n_r, n_p, n_ro = len(rows), len(pars), len(row_outs)

    def body(*refs):
        vals = [r[...].astype(F32) for r in refs[:n_r + n_p]]
        outs = fn(*vals)
        o_refs = refs[n_r + n_p:n_r + n_p + n_ro]
        po_refs = refs[n_r + n_p + n_ro:]
        for o_ref, val in zip(o_refs, outs[:n_ro]):
            o_ref[...] = val.astype(o_ref.dtype)
        first = pl.program_id(0) == 0
        for po_ref, val in zip(po_refs, outs[n_ro:]):
            @pl.when(first)
            def _():
                po_ref[...] = val

            @pl.when(jnp.logical_not(first))
            def _():
                po_ref[...] += val

    def const_map(nd):
        return lambda i: (0,) * nd

    def row_spec(w, cb):
        return pl.BlockSpec((tm, w), lambda i: (i, cb))

    in_specs = [row_spec(w, cb) for _, w, cb in rows]
    in_specs += [pl.BlockSpec(p.shape, const_map(p.ndim)) for p in pars]
    out_specs = [pl.BlockSpec((tm, w), lambda i: (i, 0)) for w, _ in row_outs]
    out_specs += [pl.BlockSpec(tuple(s), const_map(len(s))) for s in par_outs]
    out_shape = [jax.ShapeDtypeStruct((n_rows, w), dt) for w, dt in row_outs]
    out_shape += [jax.ShapeDtypeStruct(tuple(s), F32) for s in par_outs]
    return pl.pallas_call(
        body, name=name, grid=(n_rows // tm,), in_specs=in_specs, out_specs=out_specs, out_shape=out_shape,
        compiler_params=_params(("arbitrary",) if par_outs else ("parallel",)),
    )(*[r[0] for r in rows], *pars)


def _vjp_of(fn, n_prim, n_out, n_par, n_pass=0):
    def bwd(*args):
        prim = args[:n_prim]
        cts = args[n_prim:n_prim + n_out]
        passes = args[n_prim + n_out:n_prim + n_out + 2 * n_pass]
        pars = args[n_prim + n_out + 2 * n_pass:]
        _, vjp = jax.vjp(fn, *prim, *pars)
        grads = vjp(tuple(cts))
        sums = tuple(passes[2 * i] + passes[2 * i + 1] for i in range(n_pass))
        return tuple(grads[:n_prim]) + sums + tuple(grads[n_prim:])
    return bwd


def _mm(name, a, b, mode="nn", extras=(), epi=None, out_dtypes=(F32,), a_pro=None, out_split=None,
        epi_pars=(), row_sum=False, pin=None, tm=1024, tn=1024, tk=1024):
    split = b.shape[0] if b.ndim == 3 else None
    b_rows, b_cols = b.shape[-2:]
    if mode == "nn":
        (m, k), n = a.shape, b_cols * (split or 1)
    elif mode == "nt":
        (m, k), n = a.shape, b_rows
        assert k == b_cols * (split or 1)
    else:
        assert split is None
        (k, m), n = a.shape, b_cols
    tm, tk = _tile(m, tm), _tile(k, tk)
    tn = _tile(n // out_split, tn) if out_split else _tile(n, tn)
    if split and mode == "nn":
        tn = _tile(b_cols, tn)
    if split and mode == "nt":
        tk = _tile(b_cols, tk)
    nk = k // tk
    raw = {"nn": _raw_nn, "nt": _raw_nt, "tn": _raw_tn}[mode]
    n_e, n_p, n_o = len(extras), len(epi_pars), len(out_dtypes)
    n_in = n_e + n_p + (0 if pin is None else 1)
    if epi is None:
        epi = lambda acc: (acc,)

    def body(a_ref, b_ref, *rest):
        e_refs, p_refs, o_refs = rest[:n_e], rest[n_e:n_e + n_p], rest[n_in:n_in + n_o]
        kk = pl.program_id(2)
        a_tile = a_ref[...] if a_pro is None else a_pro(a_ref[...].astype(F32))
        part = raw(a_tile, b_ref[...])

        def finish(total):
            res = epi(total, *[e[...].astype(F32) for e in e_refs], *[p[...] for p in p_refs])
            for o_ref, r in zip(o_refs, res):
                o_ref[...] = r.astype(o_ref.dtype)
            if row_sum:
                rest[n_in + n_o][...] = res[n_o]

        if nk == 1:
            finish(part)
            return
        acc = rest[-1]

        @pl.when(kk == 0)
        def _():
            acc[...] = part

        @pl.when((kk > 0) & (kk < nk - 1))
        def _():
            acc[...] += part

        @pl.when(kk == nk - 1)
        def _():
            finish(acc[...] + part)

    a_spec = pl.BlockSpec((tk, tm), lambda i, j, kk: (kk, i)) if mode == "tn" else pl.BlockSpec((tm, tk), lambda i, j, kk: (i, kk))
    if split and mode == "nn":
        per = b_cols // tn
        b_spec = pl.BlockSpec((None, tk, tn), lambda i, j, kk: (j // per, kk, j % per))
    elif split:
        per = b_cols // tk
        b_spec = pl.BlockSpec((None, tn, tk), lambda i, j, kk: (kk // per, j, kk % per))
    elif mode == "nt":
        b_spec = pl.BlockSpec((tn, tk), lambda i, j, kk: (j, kk))
    else:
        b_spec = pl.BlockSpec((tk, tn), lambda i, j, kk: (kk, j))
    mn_spec = pl.BlockSpec((tm, tn), lambda i, j, kk: (i, j))
    if out_split:
        assert not extras
        per_out = n // out_split // tn
        out_spec = pl.BlockSpec((None, tm, tn), lambda i, j, kk: (j // per_out, i, j % per_out))
        out_shapes = [jax.ShapeDtypeStruct((out_split, m, n // out_split), dt) for dt in out_dtypes]
    else:
        out_spec = mn_spec
        out_shapes = [jax.ShapeDtypeStruct((m, n), dt) for dt in out_dtypes]
    out_specs = [out_spec] * n_o
    if row_sum:
        out_specs.append(pl.BlockSpec((None, 1, tn), lambda i, j, kk: (i, 0, j)))
        out_shapes.append(jax.ShapeDtypeStruct((m // tm, 1, n), F32))
    in_specs = [a_spec, b_spec] + [mn_spec] * n_e
    in_specs += [pl.BlockSpec(p.shape, functools.partial(lambda i, j, kk, nd: (0,) * nd, nd=p.ndim)) for p in epi_pars]
    in_specs += [] if pin is None else [pl.BlockSpec(memory_space=pl.ANY)]
    outs = pl.pallas_call(
        body, name=name, grid=(m // tm, n // tn, nk), in_specs=in_specs, out_specs=out_specs, out_shape=out_shapes,
        scratch_shapes=[pltpu.VMEM((tm, tn), F32)] if nk > 1 else [],
        compiler_params=_params(("parallel", "parallel", "arbitrary")),
    )(a, b, *extras, *epi_pars, *([] if pin is None else [pin]))
    return outs[0] if len(outs) == 1 else outs


def _sigmoid(x):
    return jax.nn.sigmoid(x)


def _silu(x):
    return x * _sigmoid(x)


def _softplus(x):
    return jnp.maximum(x, 0.0) + jnp.log1p(jnp.exp(-jnp.abs(x)))


def _rmsnorm_fn(x, gain):
    return (x * lax.rsqrt(jnp.mean(x * x, axis=-1, keepdims=True) + EPS) * gain,)


def _head_norm(o, gain, n_heads):
    w = o.shape[-1] // n_heads
    parts = []
    for h in range(n_heads):
        oh = o[:, h * w:(h + 1) * w]
        parts.append(oh * lax.rsqrt(jnp.mean(oh * oh, axis=-1, keepdims=True) + EPS))
    return jnp.concatenate(parts, axis=-1) * gain


@jax.custom_jvp
def _neg_expm1(x):
    u = jnp.exp(x)
    is_one = u == 1.0
    return jnp.where(is_one, -x, (1.0 - u) * x / jnp.log(jnp.where(is_one, 2.0, u)))


@_neg_expm1.defjvp
def _neg_expm1_jvp(primals, tangents):
    (x,), (t,) = primals, tangents
    return _neg_expm1(x), -jnp.exp(x) * t


def _rg_gates_fn(xc, wa, wx, ba, bx, lam):
    outs = []
    for d in range(2):
        r = _sigmoid(_dot_nn(xc, wa[d]) + ba[d:d + 1])
        i = _sigmoid(_dot_nn(xc, wx[d]) + bx[d:d + 1])
        log_a = -RG_C * r * _softplus(-lam[d:d + 1])
        outs.append(jnp.exp(log_a))
        outs.append(jnp.sqrt(_neg_expm1(2.0 * log_a)) * (i * xc))
    return tuple(outs)


def _hg_pre_fn(q, f_f, f_b, logits):
    mx = jnp.maximum(logits[0:1], logits[1:2])
    e0 = jnp.exp(logits[0:1] - mx)
    e1 = jnp.exp(logits[1:2] - mx)
    lb = e0 / (e0 + e1)
    outs = [_silu(q)]
    for f in (f_f, f_b):
        outs.append((1.0 - lb) * _sigmoid(-f))
        outs.append(jnp.log(lb + (1.0 - lb) * _sigmoid(f)))
    return tuple(outs)


def _post0_fn(hs, ga, o, g, gain):
    ya = hs * jax.nn.gelu(ga, approximate=True)
    yb = _head_norm(o, gain, 4) * _silu(g)
    return (jnp.concatenate([ya, yb], axis=-1),)


def _post0_fwd_fn(h_f, h_b, ga, o_f, o_b, g, gain):
    return _post0_fn(h_f + h_b, ga, o_f + o_b, g, gain)


def _post0_bwd_fn(h_f, h_b, ga, o_f, o_b, g, dmix, gain):
    _, vjp = jax.vjp(_post0_fn, h_f + h_b, ga, o_f + o_b, g, gain)
    return vjp((dmix,))


def _gla_pre_fn(q, lr, w_up, b_gate):
    outs = [q * (128.0 ** -0.5)]
    for d in range(2):
        z = _dot_nn(lr, w_up[d]) + b_gate[d:d + 1]
        outs.append(-_softplus(-z) * (1.0 / 16.0))
    return tuple(outs)


def _gla_post_fn(o, r, gain):
    return (_head_norm(o, gain, 4) * _silu(r),)


def _gla_post_fwd_fn(o_f, o_b, r, gain):
    return _gla_post_fn(o_f + o_b, r, gain)


def _gla_post_bwd_fn(o_f, o_b, r, dmix, gain):
    _, vjp = jax.vjp(_gla_post_fn, o_f + o_b, r, gain)
    return vjp((dmix,))


def _relu2_bwd_epi(acc, hid):
    return (acc * 2.0 * jnp.maximum(hid, 0.0),)


def _relu2(x):
    r = jnp.maximum(x, 0.0)
    return r * r


def _add_epi(acc, res):
    return (acc + res,)


def _loss_head_fn(h, target, gain):
    def f(h, gain):
        y = _rmsnorm_fn(h, gain)[0]
        err = y - target
        return 0.5 * jnp.sum(jnp.mean(err * err, axis=-1, keepdims=True))
    loss, (dh, dgain) = jax.value_and_grad(f, argnums=(0, 1))(h, gain)
    return dh, dh, jnp.full((1, LANES), loss, F32), dgain


def _adam_fn(w, g, m, v):
    m2 = ADAM_B1 * m + (1.0 - ADAM_B1) * g
    v2 = ADAM_B2 * v + (1.0 - ADAM_B2) * (g * g)
    m_hat = m2 / (1.0 - ADAM_B1 ** ADAM_STEP)
    v_hat = v2 / (1.0 - ADAM_B2 ** ADAM_STEP)
    delta = -ADAM_LR * (m_hat / (jnp.sqrt(v_hat) + ADAM_EPS) + ADAM_WD * w)
    return delta, m2, v2


def _shifted(x, t_idx, off):
    n = x.shape[0]
    rolled = pltpu.roll(x, (-off) % n, 0)
    valid = (t_idx + off >= 0) & (t_idx + off < n)
    return jnp.where(valid, rolled, 0.0)


def _conv_fwd(name, src, colblock, w, b):
    n_rows, width = src.shape[0], w.shape[1]

    def body(x_ref, w_ref, b_ref, o_ref):
        x = x_ref[...]
        t_idx = lax.broadcasted_iota(jnp.int32, x.shape, 0)
        acc = b_ref[...] + w_ref[2:3, :] * x
        acc += w_ref[0:1, :] * _shifted(x, t_idx, -2)
        acc += w_ref[1:2, :] * _shifted(x, t_idx, -1)
        acc += w_ref[3:4, :] * _shifted(x, t_idx, 1)
        o_ref[...] = acc

    nb = width // LANES
    return pl.pallas_call(
        body, name=name, grid=(nb,),
        in_specs=[pl.BlockSpec((n_rows, LANES), lambda j: (0, colblock * nb + j)),
                  pl.BlockSpec((4, LANES), lambda j: (0, j)), pl.BlockSpec((1, LANES), lambda j: (0, j))],
        out_specs=pl.BlockSpec((n_rows, LANES), lambda j: (0, j)),
        out_shape=jax.ShapeDtypeStruct((n_rows, width), F32),
        compiler_params=_params(("parallel",)),
    )(src, w, b)


def _conv_bwd(name, src, colblock, w, d):
    n_rows, width = src.shape[0], w.shape[1]

    def body(x_ref, w_ref, d_ref, dx_ref, dw_ref, db_ref):
        x = x_ref[...]
        g = d_ref[...]
        t_idx = lax.broadcasted_iota(jnp.int32, x.shape, 0)
        dx = w_ref[2:3, :] * g
        dx += w_ref[0:1, :] * _shifted(g, t_idx, 2)
        dx += w_ref[1:2, :] * _shifted(g, t_idx, 1)
        dx += w_ref[3:4, :] * _shifted(g, t_idx, -1)
        dx_ref[...] = dx.astype(dx_ref.dtype)
        dw_ref[0:1, :] = jnp.sum(g * _shifted(x, t_idx, -2), axis=0, keepdims=True)
        dw_ref[1:2, :] = jnp.sum(g * _shifted(x, t_idx, -1), axis=0, keepdims=True)
        dw_ref[2:3, :] = jnp.sum(g * x, axis=0, keepdims=True)
        dw_ref[3:4, :] = jnp.sum(g * _shifted(x, t_idx, 1), axis=0, keepdims=True)
        db_ref[...] = jnp.sum(g, axis=0, keepdims=True)

    nb = width // LANES
    return pl.pallas_call(
        body, name=name, grid=(nb,),
        in_specs=[pl.BlockSpec((n_rows, LANES), lambda j: (0, colblock * nb + j)),
                  pl.BlockSpec((4, LANES), lambda j: (0, j)),
                  pl.BlockSpec((n_rows, LANES), lambda j: (0, j))],
        out_specs=[pl.BlockSpec((n_rows, LANES), lambda j: (0, j)), pl.BlockSpec((4, LANES), lambda j: (0, j)),
                   pl.BlockSpec((1, LANES), lambda j: (0, j))],
        out_shape=[jax.ShapeDtypeStruct((n_rows, width), BF16), jax.ShapeDtypeStruct((4, width), F32),
                   jax.ShapeDtypeStruct((1, width), F32)],
        compiler_params=_params(("parallel",)),
    )(src, w, d)


SUBLANES = 8
SCAN_UNROLL = 8


def _shift_rows(x, d, fill):
    n = x.shape[0]
    t = lax.broadcasted_iota(jnp.int32, x.shape, 0)
    valid = (t >= d) if d > 0 else (t < n + d)
    return jnp.where(valid, pltpu.roll(x, d % n, 0), fill)


def _tile_scan(a, u, reverse):
    d = 1
    while d < a.shape[0]:
        s = -d if reverse else d
        a_sh, u_sh = _shift_rows(a, s, 1.0), _shift_rows(u, s, 0.0)
        u = u + a * u_sh
        a = a * a_sh
        d *= 2
    return a, u


def _edge_row(x, reverse):
    return x[0:1, :] if reverse else x[SUBLANES - 1:SUBLANES, :]


def _scan_specs(n_rows, n):
    return [pl.BlockSpec((n_rows, LANES), lambda j: (0, j))] * n


def _scan_tile(a_ref, u_ref, h_ref, i, carry, reverse):
    n_tiles = a_ref.shape[0] // SUBLANES
    tile = (n_tiles - 1 - i) if reverse else i
    rows = pl.ds(pl.multiple_of(tile * SUBLANES, SUBLANES), SUBLANES)
    acc_a, acc_u = _tile_scan(a_ref[rows, :], u_ref[rows, :], reverse)
    h = acc_u + acc_a * carry
    h_ref[rows, :] = h
    return _edge_row(h, reverse)


def _scan_fwd(name, a_f, u_f, a_b, u_b):
    n_rows, width = a_f.shape

    def body(af_ref, uf_ref, ab_ref, ub_ref, hf_ref, hb_ref):
        def step(i, carry):
            return (_scan_tile(af_ref, uf_ref, hf_ref, i, carry[0], False),
                    _scan_tile(ab_ref, ub_ref, hb_ref, i, carry[1], True))
        zero = jnp.zeros((1, LANES), F32)
        lax.fori_loop(0, n_rows // SUBLANES, step, (zero, zero), unroll=SCAN_UNROLL)

    return pl.pallas_call(
        body, name=name, grid=(width // LANES,), in_specs=_scan_specs(n_rows, 4), out_specs=_scan_specs(n_rows, 2),
        out_shape=[jax.ShapeDtypeStruct((n_rows, width), F32)] * 2, compiler_params=_params(("parallel",)),
    )(a_f, u_f, a_b, u_b)


def _scan_bwd_tile(a_ref, h_ref, dh_ref, du_ref, da_ref, i, carry, reverse):
    n_rows = a_ref.shape[0]
    n_tiles = n_rows // SUBLANES
    against = not reverse
    one = -1 if against else 1
    g_in, a_edge = carry
    tile = (n_tiles - 1 - i) if against else i
    start = pl.multiple_of(tile * SUBLANES, SUBLANES)
    rows = pl.ds(start, SUBLANES)
    a_tile = a_ref[rows, :]
    coeff = _shift_rows(a_tile, one, a_edge)
    acc_a, acc_u = _tile_scan(coeff, dh_ref[rows, :], against)
    g = acc_u + acc_a * g_in
    du_ref[rows, :] = g
    outside = (start + SUBLANES) if reverse else (start - 1)
    inside = (outside >= 0) & (outside < n_rows)
    h_edge = jnp.where(inside, h_ref[pl.ds(jnp.clip(outside, 0, n_rows - 1), 1), :], 0.0)
    da_ref[rows, :] = g * _shift_rows(h_ref[rows, :], -one, h_edge)
    return _edge_row(g, against), _edge_row(a_tile, against)


def _scan_bwd(name, a_f, h_f, a_b, h_b, dh):
    n_rows, width = a_f.shape

    def body(af_ref, hf_ref, ab_ref, hb_ref, dh_ref, duf_ref, daf_ref, dub_ref, dab_ref):
        def step(i, carry):
            return (_scan_bwd_tile(af_ref, hf_ref, dh_ref, duf_ref, daf_ref, i, carry[0], False),
                    _scan_bwd_tile(ab_ref, hb_ref, dh_ref, dub_ref, dab_ref, i, carry[1], True))
        zero = jnp.zeros((1, LANES), F32)
        lax.fori_loop(0, n_rows // SUBLANES, step, ((zero, zero), (zero, zero)), unroll=SCAN_UNROLL)

    return pl.pallas_call(
        body, name=name, grid=(width // LANES,), in_specs=_scan_specs(n_rows, 5), out_specs=_scan_specs(n_rows, 4),
        out_shape=[jax.ShapeDtypeStruct((n_rows, width), F32)] * 4, compiler_params=_params(("parallel",)),
    )(a_f, h_f, a_b, h_b, dh)


def _tri_mask(c, reverse):
    row = lax.broadcasted_iota(jnp.int32, (c, c), 0)
    col = lax.broadcasted_iota(jnp.int32, (c, c), 1)
    return (col >= row) if reverse else (col <= row)


def _cumsum_rows(x, reverse):
    tri = _tri_mask(x.shape[0], reverse).astype(BF16)
    hi = x.astype(BF16)
    rest = x - hi.astype(F32)
    mid = rest.astype(BF16)
    lo = (rest - mid.astype(F32)).astype(BF16)
    return _raw_nn(tri, hi) + _raw_nn(tri, mid) + _raw_nn(tri, lo)


@functools.partial(jax.custom_vjp, nondiff_argnums=(1,))
def _cumsum(x, reverse):
    return _cumsum_rows(x, reverse)


def _cumsum_fwd(x, reverse):
    return _cumsum_rows(x, reverse), None


def _cumsum_bwd(reverse, _, g):
    return (_cumsum_rows(g, not reverse),)


_cumsum.defvjp(_cumsum_fwd, _cumsum_bwd)


def _chunks_fn(qs, ks, vs, lfs, sts, reverses):
    n, c = len(qs), qs[0].shape[0]
    every = range(n)
    tris = [_tri_mask(c, r) for r in reverses]
    cums = [_cumsum(lfs[i], reverses[i]) for i in every]
    rid = lax.broadcasted_iota(jnp.int32, cums[0].shape, 0)

    def pick(cum, r):
        return jnp.sum(jnp.where(rid == r, cum, 0.0), axis=0, keepdims=True)

    refs = [pick(cums[i], (c - 1 - c // 2) if reverses[i] else c // 2) for i in every]
    lasts = [pick(cums[i], 0 if reverses[i] else c - 1) for i in every]
    q_in = [qs[i] * jnp.exp(cums[i] - refs[i]) for i in every]
    k_in = [ks[i] * jnp.exp(refs[i] - cums[i]) for i in every]
    scores = [jnp.where(tris[i], _dot_nt(q_in[i], k_in[i]), 0.0) for i in every]
    o_intra = [_dot_nn(scores[i], vs[i]) for i in every]
    q_out = [qs[i] * jnp.exp(cums[i]) for i in every]
    o_inter = [_dot_nt(q_out[i], sts[i]) for i in every]
    k_state = [ks[i] * jnp.exp(lasts[i] - cums[i]) for i in every]
    upd = [_dot_tn(vs[i], k_state[i]) for i in every]
    st_new = [sts[i] * jnp.exp(lasts[i]) + upd[i] for i in every]
    return [o_intra[i] + o_inter[i] for i in every], st_new


def _attn_fwd(name, q, k_f, k_b, v, lf_f, lf_b, n_heads, dk, dv):
    n_rows = q[0].shape[0]
    n_chunks = n_rows // CHUNK
    n_steps = n_chunks // ATTN_SUB
    wk, wv = n_heads * dk, n_heads * dv

    def spec(width, off, rev):
        return pl.BlockSpec((CHUNK * ATTN_SUB, width), lambda n: ((n_steps - 1 - n) if rev else n, off))

    def sspec(rev):
        return pl.BlockSpec((ATTN_SUB, n_heads, dv, dk), lambda n: ((n_steps - 1 - n) if rev else n, 0, 0, 0))

    def body(qf, kf, vf, lff, qb, kb, vb, lfb, of_ref, ob_ref, sf_ref, sb_ref, st):
        @pl.when(pl.program_id(0) == 0)
        def _():
            st[...] = jnp.zeros_like(st)

        ins = ((qf, kf, vf, lff), (qb, kb, vb, lfb))
        chains = [(d, h) for d in range(2) for h in range(n_heads)]
        ck = [slice(h * dk, (h + 1) * dk) for h in range(n_heads)]
        cv = [slice(h * dv, (h + 1) * dv) for h in range(n_heads)]
        sts = [st[d, h] for d, h in chains]
        done = []
        for sub in range(ATTN_SUB):
            local = (sub, ATTN_SUB - 1 - sub)
            rows = [slice(local[d] * CHUNK, (local[d] + 1) * CHUNK) for d in range(2)]
            qs = [ins[d][0][rows[d], ck[h]] for d, h in chains]
            ks = [ins[d][1][rows[d], ck[h]] for d, h in chains]
            vs = [ins[d][2][rows[d], cv[h]] for d, h in chains]
            lfs = [ins[d][3][rows[d], ck[h]] for d, h in chains]
            os_, st_new = _chunks_fn(qs, ks, vs, lfs, sts, [d == 1 for d, _ in chains])
            done.append((local, rows, sts, os_))
            sts = st_new
        for local, rows, entered, os_ in done:
            for i, (d, h) in enumerate(chains):
                (sf_ref, sb_ref)[d][local[d], h] = entered[i].astype(BF16)
                (of_ref, ob_ref)[d][rows[d], cv[h]] = os_[i]
        for i, (d, h) in enumerate(chains):
            st[d, h] = sts[i]

    in_specs = [spec(wk, q[1], False), spec(wk, k_f[1], False), spec(wv, v[1], False), spec(wk, lf_f[1], False),
                spec(wk, q[1], True), spec(wk, k_b[1], True), spec(wv, v[1], True), spec(wk, lf_b[1], True)]
    return pl.pallas_call(
        body, name=name, grid=(n_steps,), in_specs=in_specs,
        out_specs=[spec(wv, 0, False), spec(wv, 0, True), sspec(False), sspec(True)],
        out_shape=[jax.ShapeDtypeStruct((n_rows, wv), F32)] * 2
        + [jax.ShapeDtypeStruct((n_chunks, n_heads, dv, dk), BF16)] * 2,
        scratch_shapes=[pltpu.VMEM((2, n_heads, dv, dk), F32)],
        compiler_params=_params(("arbitrary",)),
    )(q[0], k_f[0], v[0], lf_f[0], q[0], k_b[0], v[0], lf_b[0])


def _attn_bwd(name, q, k_f, k_b, v, lf_f, lf_b, st_f, st_b, do, n_heads, dk, dv, out_dtype=F32):
    n_rows = q[0].shape[0]
    n_chunks = n_rows // CHUNK
    n_steps = n_chunks // ATTN_SUB
    wk, wv = n_heads * dk, n_heads * dv

    def spec(width, off, rev):
        return pl.BlockSpec((CHUNK * ATTN_SUB, width), lambda n: (n if rev else (n_steps - 1 - n), off))

    def sspec(rev):
        return pl.BlockSpec((ATTN_SUB, n_heads, dv, dk), lambda n: (n if rev else (n_steps - 1 - n), 0, 0, 0))

    def body(qf, kf, vf, lff, sf, dof, qb, kb, vb, lfb, sb, dob,
             dqf, dkf, dvf, dlff, dqb, dkb, dvb, dlfb, dst):
        @pl.when(pl.program_id(0) == 0)
        def _():
            dst[...] = jnp.zeros_like(dst)

        ins = ((qf, kf, vf, lff, sf, dof), (qb, kb, vb, lfb, sb, dob))
        outs = ((dqf, dkf, dvf, dlff), (dqb, dkb, dvb, dlfb))
        chains = [(d, h) for d in range(2) for h in range(n_heads)]
        ck = [slice(h * dk, (h + 1) * dk) for h in range(n_heads)]
        cv = [slice(h * dv, (h + 1) * dv) for h in range(n_heads)]
        fn = functools.partial(_chunks_fn, reverses=[d == 1 for d, _ in chains])
        dsts = [dst[d, h] for d, h in chains]
        done = []
        for sub in range(ATTN_SUB):
            local = (ATTN_SUB - 1 - sub, sub)
            rows = [slice(local[d] * CHUNK, (local[d] + 1) * CHUNK) for d in range(2)]
            qs = [ins[d][0][rows[d], ck[h]] for d, h in chains]
            ks = [ins[d][1][rows[d], ck[h]] for d, h in chains]
            vs = [ins[d][2][rows[d], cv[h]] for d, h in chains]
            lfs = [ins[d][3][rows[d], ck[h]] for d, h in chains]
            sts = [ins[d][4][local[d], h].astype(F32) for d, h in chains]
            dos = [ins[d][5][rows[d], cv[h]] for d, h in chains]
            _, vjp = jax.vjp(fn, qs, ks, vs, lfs, sts)
            dqs, dks, dvs, dlfs, dsts = vjp((dos, dsts))
            done.append((rows, dqs, dks, dvs, dlfs))
        for rows, dqs, dks, dvs, dlfs in done:
            for i, (d, h) in enumerate(chains):
                dq_r, dk_r, dv_r, dlf_r = outs[d]
                dq_r[rows[d], ck[h]] = dqs[i].astype(dq_r.dtype)
                dk_r[rows[d], ck[h]] = dks[i].astype(dk_r.dtype)
                dv_r[rows[d], cv[h]] = dvs[i].astype(dv_r.dtype)
                dlf_r[rows[d], ck[h]] = dlfs[i].astype(dlf_r.dtype)
        for i, (d, h) in enumerate(chains):
            dst[d, h] = dsts[i]

    def dir_specs(kk, lf, rev):
        return [spec(wk, q[1], rev), spec(wk, kk[1], rev), spec(wv, v[1], rev), spec(wk, lf[1], rev), sspec(rev),
                spec(wv, 0, rev)]

    def dir_out_specs(rev):
        return [spec(wk, 0, rev), spec(wk, 0, rev), spec(wv, 0, rev), spec(wk, 0, rev)]

    shapes = [jax.ShapeDtypeStruct((n_rows, wk), out_dtype), jax.ShapeDtypeStruct((n_rows, wk), out_dtype),
              jax.ShapeDtypeStruct((n_rows, wv), out_dtype), jax.ShapeDtypeStruct((n_rows, wk), F32)]
    outs = pl.pallas_call(
        body, name=name, grid=(n_steps,), in_specs=dir_specs(k_f, lf_f, False) + dir_specs(k_b, lf_b, True),
        out_specs=dir_out_specs(False) + dir_out_specs(True), out_shape=shapes + shapes,
        scratch_shapes=[pltpu.VMEM((2, n_heads, dv, dk), F32)],
        compiler_params=_params(("arbitrary",)),
    )(q[0], k_f[0], v[0], lf_f[0], st_f, do, q[0], k_b[0], v[0], lf_b[0], st_b, do)
    return outs[:4], outs[4:]


def _row2(v):
    return v.reshape(1, -1)


def _mlp_fwd(tag, h, gain, w1, w2):
    y = _rowcall(f"{tag}_norm", _rmsnorm_fn, [(h, h.shape[1], 0)], [gain], [(h.shape[1], BF16)], tm=512)[0]
    hid = _mm(f"{tag}_up", y, w1, out_dtypes=(BF16,))
    h_out = _mm(f"{tag}_down", hid, w2, a_pro=_relu2, extras=(h,), epi=_add_epi, tk=2048)
    return h_out, (y, hid)


def _dw(name, a, b, **kw):
    return _mm(name, a, b, mode="tn", epi=lambda acc: (acc, acc), out_dtypes=(F32, BF16), **kw)


def _mlp_bwd(tag, h, gain, w1, w2, saved, dh_out):
    y, hid = saved
    dhid = _mm(f"{tag}_dact", dh_out[1], w2, mode="nt", extras=(hid,), epi=_relu2_bwd_epi, out_dtypes=(BF16,))
    dw2 = _dw(f"{tag}_dw2", hid, dh_out[1], a_pro=_relu2, tk=2048)
    dw1 = _dw(f"{tag}_dw1", y, dhid, out_split=N_CHIPS, tk=4096)
    dh, dgain = _dy_norm_bwd(f"{tag}_dy", dhid, w1, h, gain, dh_out[0])
    return dh, dgain, dw1, dw2


def _dy_norm_bwd(name, dz, w, h, gain, dres, pin=None, twice=True, **tiles):
    n_out = 2 if twice else 1

    def epi(dy, h_tile, dres_tile, gain_row):
        _, vjp = jax.vjp(lambda u, v: _rmsnorm_fn(u, v)[0], h_tile, gain_row)
        dh, dgain = vjp(dy)
        return (dh + dres_tile,) * n_out + (dgain,)

    assert h.shape[1] <= 1024
    tiles.setdefault("tm", 1024)
    *dh, dgain_parts = _mm(name, dz, w, mode="nt", extras=(h, dres), epi=epi, epi_pars=(gain,), row_sum=True,
                           out_dtypes=(F32, BF16)[:n_out], pin=pin, **tiles)
    return dh, jnp.sum(dgain_parts, axis=0)


def _local_step(x, target, w, pin=None, late=None, emit=None):
    g = {}
    d_model = x.shape[1]
    rg_w = hg_w = d_model // 2
    pins = []

    def send_off(tag, pairs):
        if emit is not None:
            pins.append(emit(tag, [p[0] for p in pairs], [p[1] for p in pairs]))

    def both(fn, pair):
        return [fn(t) for t in pair]

    def chip_major(t):
        return t.reshape(N_CHIPS, t.shape[0] // N_CHIPS, t.shape[1])

    h_a0 = x
    gain = _row2(w["norm_mix"][0])
    y0 = _rowcall("l0_norm", _rmsnorm_fn, [(h_a0, d_model, 0)], [gain], [(d_model, BF16)], tm=512, pin=pin)[0]
    proj0 = _mm("l0_in", y0, w["ab_w_in"])
    conv_w, conv_b = w["rg_conv_w"], _row2(w["rg_conv_b"])
    xc = _conv_fwd("rg_conv", proj0, 0, conv_w, conv_b)
    gate_pars = [w["rg_wa_bd"], w["rg_wx_bd"], w["rg_b_a"], w["rg_b_x"], w["rg_lambda"]]
    a_f, u_f, a_b, u_b = _rowcall("rg_gates", _rg_gates_fn, [(xc, rg_w, 0)], gate_pars, [(rg_w, F32)] * 4)
    hs_f, hs_b = _scan_fwd("rg_scan", a_f, u_f, a_b, u_b)
    hg_rows = [(proj0, hg_w, 2), (proj0, hg_w, 3), (proj0, hg_w, 4)]
    qh, k_f, lf_f, k_b, lf_b = _rowcall("hg_pre", _hg_pre_fn, hg_rows, [w["hg_lb_logits"]], [(hg_w, F32)] * 5)
    iv = (proj0, 5)
    o_f, o_b, st_f, st_b = _attn_fwd("hg_attn", (qh, 0), (k_f, 0), (k_b, 0), iv, (lf_f, 0), (lf_b, 0), 4, 128, 128)
    post0_rows = [(hs_f, rg_w, 0), (hs_b, rg_w, 0), (proj0, rg_w, 1), (o_f, hg_w, 0), (o_b, hg_w, 0), (proj0, hg_w, 6)]
    hg_gain = _row2(w["hg_norm"])
    mix_in0 = _rowcall("l0_post", _post0_fwd_fn, post0_rows, [hg_gain], [(d_model, BF16)])[0]
    if late is not None:
        w = {**w, **late(mix_in0)}
    h_b0 = _mm("l0_out", mix_in0, w["ab_w_out"], extras=(h_a0,), epi=_add_epi)
    h_c0, mlp0 = _mlp_fwd("mlp0", h_b0, _row2(w["norm_mlp"][0]), w["mlp_w1"][0], w["mlp_w2"][0])

    h_a1 = h_c0
    gain1 = _row2(w["norm_mix"][1])
    y1 = _rowcall("l1_norm", _rmsnorm_fn, [(h_a1, d_model, 0)], [gain1], [(d_model, BF16)], tm=512)[0]
    proj1 = _mm("l1_in", y1, w["gla_w_in_pad"], tn=640)
    gla_pars = [w["gla_w_up_pad"], w["gla_b_gate"]]
    gq, glf_f, glf_b = _rowcall("gla_pre", _gla_pre_fn, [(proj1, 512, 0), (proj1, LANES, 24)], gla_pars, [(512, F32)] * 3)
    gk, gv = (proj1, 1), (proj1, 1)
    go_f, go_b, gst_f, gst_b = _attn_fwd("gla_attn", (gq, 0), gk, gk, gv, (glf_f, 0), (glf_b, 0), 4, 128, 256)
    gla_gain = _row2(w["gla_norm"])
    post1_rows = [(go_f, d_model, 0), (go_b, d_model, 0), (proj1, d_model, 2)]
    mix_in1 = _rowcall("l1_post", _gla_post_fwd_fn, post1_rows, [gla_gain], [(d_model, BF16)])[0]
    h_b1 = _mm("l1_out", mix_in1, w["gla_w_out"], extras=(h_a1,), epi=_add_epi)
    h_c1, mlp1 = _mlp_fwd("mlp1", h_b1, _row2(w["norm_mlp"][1]), w["mlp_w1"][1], w["mlp_w2"][1])

    *dh, loss, g["norm_final"] = _rowcall(
        "loss_head", _loss_head_fn, [(h_c1, d_model, 0), (target, d_model, 0)], [_row2(w["norm_final"])],
        [(d_model, F32), (d_model, BF16)], [(1, LANES), (1, d_model)], tm=512)

    dh, g_nmlp1, g_w1_1, g_w2_1 = _mlp_bwd("mlp1", h_b1, _row2(w["norm_mlp"][1]), w["mlp_w1"][1], w["mlp_w2"][1], mlp1, dh)
    send_off("mlp1", [g_w1_1, both(chip_major, g_w2_1)])
    dmix1 = _mm("l1_dout", dh[1], w["gla_w_out"], mode="nt")
    g_gla_out = _dw("l1_dwout", mix_in1, dh[1])
    g["gla_w_out"] = g_gla_out[0]
    dgo, dr, g["gla_norm"] = _rowcall(
        "l1_dpost", _gla_post_bwd_fn, post1_rows + [(dmix1, d_model, 0)], [gla_gain],
        [(d_model, F32), (d_model, BF16)], [(1, d_model)], pin=pins.pop() if pins else None)
    (dq_f, dk_f, dv_f, dlf_f), (dq_b, dk_b, dv_b, dlf_b) = _attn_bwd(
        "gla_dattn", (gq, 0), gk, gk, gv, (glf_f, 0), (glf_b, 0), gst_f, gst_b, dgo, 4, 128, 256)

    def gla_pre_bwd(q, lr, dq1, dq2, dlf1, dlf2, dk1, dk2, dv1, dv2, w_up, b_gate):
        dlr = jnp.zeros_like(lr)
        dws, dbs = [], []
        for d, dlf in enumerate((dlf1, dlf2)):
            z = _raw_nn(lr, w_up[d]) + b_gate[d:d + 1]
            dz = dlf * _sigmoid(-z) * (1.0 / 16.0)
            dlr = dlr + _raw_nt(dz, w_up[d])
            dws.append(_raw_tn(dz, lr))
            dbs.append(jnp.sum(dz, axis=0, keepdims=True))
        return ((dq1 + dq2) * (128.0 ** -0.5), dk1 + dk2, dv1 + dv2, dlr, dws[0], dws[1], dbs[0], dbs[1])

    rows = [(proj1, 512, 0), (proj1, LANES, 24), (dq_f, 512, 0), (dq_b, 512, 0), (dlf_f, 512, 0), (dlf_b, 512, 0),
            (dk_f, 512, 0), (dk_b, 512, 0), (dv_f, d_model, 0), (dv_b, d_model, 0)]
    dq, dk, dv, dlr, dwt_f, dwt_b, db_f, db_b = _rowcall(
        "gla_dpre", gla_pre_bwd, rows, gla_pars, [(512, BF16), (512, BF16), (d_model, BF16), (LANES, BF16)],
        [(512, LANES), (512, LANES), (1, 512), (1, 512)])
    g["gla_w_up_pad"] = jnp.stack([dwt_f.T, dwt_b.T])
    g["gla_b_gate"] = jnp.concatenate([db_f, db_b], axis=0)
    dproj1 = jnp.concatenate([dq, dk, dv, dr, dlr], axis=1)
    g_gla_in = both(lambda t: _split_chips(t[:, :GLA_IN_WIDTH], 1), _dw("l1_dwin", y1, dproj1, tn=640, tk=4096))
    g["gla_w_in"] = g_gla_in[0]
    send_off("gla", [g_gla_in, both(chip_major, g_gla_out)])
    dh, g_nmix1 = _dy_norm_bwd("l1_dy", dproj1, w["gla_w_in_pad"], h_a1, gain1, dh[0],
                               pin=pins.pop() if pins else None, tk=640)

    dh, g_nmlp0, g_w1_0, g_w2_0 = _mlp_bwd("mlp0", h_b0, _row2(w["norm_mlp"][0]), w["mlp_w1"][0], w["mlp_w2"][0], mlp0, dh)
    g_ab_out = _dw("l0_dwout", mix_in0, dh[1])
    g["ab_w_out"] = g_ab_out[0]
    send_off("mlp0", [g_w1_0, both(chip_major, g_w2_0), both(chip_major, g_ab_out)])
    dmix0 = _mm("l0_dout", dh[1], w["ab_w_out"], mode="nt")
    dhs, dga, do, dg, g["hg_norm"] = _rowcall(
        "l0_dpost", _post0_bwd_fn, post0_rows + [(dmix0, d_model, 0)], [hg_gain],
        [(rg_w, F32), (rg_w, BF16), (hg_w, F32), (hg_w, BF16)], [(1, hg_w)], pin=pins.pop() if pins else None)
    (dqh_f, dk_f, div_f, dlf_f), (dqh_b, dk_b, div_b, dlf_b) = _attn_bwd(
        "hg_dattn", (qh, 0), (k_f, 0), (k_b, 0), iv, (lf_f, 0), (lf_b, 0), st_f, st_b, do, 4, 128, 128)

    def hg_pre_bwd(q, f_f, f_b, dq1, dq2, dk1, dlf1, dk2, dlf2, dv1, dv2, logits):
        _, vjp = jax.vjp(_hg_pre_fn, q, f_f, f_b, logits)
        dq, df_f, df_b, dlogits = vjp((dq1 + dq2, dk1, dlf1, dk2, dlf2))
        return dq, df_f, df_b, dv1 + dv2, dlogits

    rows = hg_rows + [(t, hg_w, 0) for t in (dqh_f, dqh_b, dk_f, dlf_f, dk_b, dlf_b, div_f, div_b)]
    dq, df_f, df_b, div, g["hg_lb_logits"] = _rowcall(
        "hg_dpre", hg_pre_bwd, rows, [w["hg_lb_logits"]], [(hg_w, BF16)] * 4, [(2, hg_w)])
    du_f, da_f, du_b, da_b = _scan_bwd("rg_dscan", a_f, hs_f, a_b, hs_b, dhs)
    gates_bwd = _vjp_of(_rg_gates_fn, 1, 4, 5)
    rows = [(xc, rg_w, 0), (da_f, rg_w, 0), (du_f, rg_w, 0), (da_b, rg_w, 0), (du_b, rg_w, 0)]
    dxc, g["rg_wa_bd"], g["rg_wx_bd"], g["rg_b_a"], g["rg_b_x"], g["rg_lambda"] = _rowcall(
        "rg_dgates", gates_bwd, rows, gate_pars, [(rg_w, F32)],
        [(2, rg_w, rg_w), (2, rg_w, rg_w), (2, rg_w), (2, rg_w), (2, rg_w)])
    dxa, g["rg_conv_w"], g["rg_conv_b"] = _conv_bwd("rg_dconv", proj0, 0, conv_w, dxc)
    dproj0 = jnp.concatenate([dxa, dga, dq, df_f, df_b, div, dg], axis=1)
    g_ab_in = _dw("l0_dwin", y0, dproj0, out_split=N_CHIPS, tk=4096)
    g["ab_w_in"] = g_ab_in[0]
    send_off("ab", [g_ab_in])
    (grad_x,), g_nmix0 = _dy_norm_bwd("l0_dy", dproj0, w["ab_w_in"], h_a0, gain, dh[0],
                                      pin=pins.pop() if pins else None, twice=False)

    g["norm_mix"] = jnp.concatenate([g_nmix0, g_nmix1], axis=0)
    g["norm_mlp"] = jnp.concatenate([g_nmlp0, g_nmlp1], axis=0)
    g["mlp_w1"] = [g_w1_0[0], g_w1_1[0]]
    g["mlp_w2"] = [g_w2_0[0], g_w2_1[0]]
    return loss, grad_x, g


def _block_diag(w):
    d, g, n, _ = w.shape
    eye = jnp.eye(g, dtype=w.dtype)
    return (w[:, :, :, None, :] * eye[None, :, None, :, None]).reshape(d, g * n, g * n)


def _block_diag_extract(wbd, g):
    d, gn, _ = wbd.shape
    n = gn // g
    blocks = wbd.reshape(d, g, n, g, n)
    return jnp.stack([blocks[:, i, :, i, :] for i in range(g)], axis=1)


def _prepare_weights(big, full):
    w = {k: full[k] for k in ("norm_mix", "norm_mlp", "norm_final", "hg_lb_logits")}
    for k in ("rg_conv_w", "rg_conv_b", "rg_b_a", "rg_b_x", "rg_lambda", "hg_norm", "gla_b_gate", "gla_norm"):
        w[k] = full[k][0]
    w["rg_wa_bd"] = _block_diag(full["rg_w_a"][0])
    w["rg_wx_bd"] = _block_diag(full["rg_w_x"][0])
    up = full["gla_w_gate_up"][0]
    rank = up.shape[1]
    pad = jnp.zeros((2, LANES, up.shape[2]), F32)
    w["gla_w_up_pad"] = pad.at[0, 0:rank].set(up[0]).at[1, rank:2 * rank].set(up[1])
    w.update(_prepare_matrices(big))
    return w


def _prepare_matrices(big):
    w = {}
    if "mlp_w1" in big:
        w["mlp_w1"] = list(big["mlp_w1"])
        w["mlp_w2"] = [t.reshape(-1, t.shape[-1]) for t in big["mlp_w2"]]
    if "ab_w_in" in big:
        w["ab_w_in"] = big["ab_w_in"]
    if "ab_w_out" in big:
        w["ab_w_out"] = big["ab_w_out"].reshape(-1, big["ab_w_out"].shape[-1])
    if "gla_w_in" in big:
        w["gla_w_out"] = big["gla_w_out"].reshape(-1, big["gla_w_out"].shape[-1])
        gla_in = _join_chips(big["gla_w_in"], 1)
        w["gla_w_in_pad"] = jnp.pad(gla_in, ((0, 0), (0, GLA_IN_PAD - gla_in.shape[1])))
    return w


def _finish_grads(g, rank=16, rg_blocks=8):
    def chip_major(t):
        return t.reshape(N_CHIPS, t.shape[0] // N_CHIPS, t.shape[1])

    big = {
        "mlp_w1": list(g["mlp_w1"]), "mlp_w2": [chip_major(t) for t in g["mlp_w2"]],
        "ab_w_in": g["ab_w_in"], "ab_w_out": chip_major(g["ab_w_out"]),
        "gla_w_in": g["gla_w_in"], "gla_w_out": chip_major(g["gla_w_out"]),
    }
    small = {
        "norm_mix": g["norm_mix"], "norm_mlp": g["norm_mlp"], "norm_final": g["norm_final"][0],
        "rg_conv_w": g["rg_conv_w"][None], "rg_conv_b": g["rg_conv_b"],
        "rg_w_a": _block_diag_extract(g["rg_wa_bd"], rg_blocks)[None], "rg_b_a": g["rg_b_a"][None],
        "rg_w_x": _block_diag_extract(g["rg_wx_bd"], rg_blocks)[None], "rg_b_x": g["rg_b_x"][None],
        "rg_lambda": g["rg_lambda"][None], "hg_lb_logits": g["hg_lb_logits"], "hg_norm": g["hg_norm"],
        "gla_w_gate_up": jnp.stack([g["gla_w_up_pad"][0, 0:rank], g["gla_w_up_pad"][1, rank:2 * rank]])[None],
        "gla_b_gate": g["gla_b_gate"][None], "gla_norm": g["gla_norm"],
    }
    return big, small


MATRICES = (("mlp_w1", 0), ("mlp_w1", 1), ("mlp_w2", 0), ("mlp_w2", 1), ("ab_w_in", 0), ("ab_w_out", 0),
            ("gla_w_in", 0), ("gla_w_out", 0))
EARLY_MATRICES = ("ab_w_in",)
SMALL_SHARDED = ("rg_conv_w", "rg_b_a", "rg_b_x", "rg_lambda", "gla_w_gate_up", "gla_b_gate", "gla_norm")
SMALL_REPLICATED = ("norm_mix", "norm_mlp", "norm_final", "rg_conv_b", "rg_w_a", "rg_w_x", "hg_lb_logits", "hg_norm")
WEIGHTS = ("norm_mix", "norm_mlp", "norm_final", "mlp_w1", "mlp_w2", "ab_w_in", "ab_w_out", "rg_conv_w", "rg_conv_b",
           "rg_w_a", "rg_b_a", "rg_w_x", "rg_b_x", "rg_lambda", "hg_lb_logits", "hg_norm", "gla_w_in", "gla_w_out",
           "gla_w_gate_up", "gla_b_gate", "gla_norm")
ROW_ALIGN = 16


def _pack(arrays, lead=0):
    head = arrays[0].shape[:lead]
    flat = jnp.concatenate([a.reshape(head + (-1,)) for a in arrays], axis=lead)
    n = flat.shape[-1]
    quantum = LANES * ROW_ALIGN
    padded = -(-n // quantum) * quantum
    if padded != n:
        flat = jnp.pad(flat, [(0, 0)] * lead + [(0, padded - n)])
    return flat.reshape(head + (padded // LANES, LANES))


def _unpack(buf, shapes, lead=0):
    head = buf.shape[:lead]
    flat = buf.reshape(head + (-1,))
    out, off = [], 0
    for s in shapes:
        n = 1
        for v in s:
            n *= v
        out.append(lax.slice_in_dim(flat, off, off + n, axis=lead).reshape(head + tuple(s)))
        off += n
    return out


def _join_chips(gathered, axis):
    t = jnp.moveaxis(gathered, 0, axis)
    return t.reshape(t.shape[:axis] + (t.shape[axis] * t.shape[axis + 1],) + t.shape[axis + 2:])


def _split_chips(full, axis):
    s = full.shape
    t = full.reshape(s[:axis] + (N_CHIPS, s[axis] // N_CHIPS) + s[axis + 1:])
    return jnp.moveaxis(t, axis, 0)


_ANY = pl.BlockSpec(memory_space=pl.ANY)


def _place():
    return lax.axis_index("x"), lax.axis_index("y"), lax.axis_index("c")


def _into_slot(name, src, slot, n_slots, dtype, tm, layer=None):
    r, lanes = src.shape[-2:]
    tm = _row_tile(r, tm, ROW_ALIGN)

    def body(slot_ref, in_ref, o_ref):
        o_ref[...] = in_ref[...].astype(o_ref.dtype)

    if layer is None:
        in_spec = pl.BlockSpec((tm, lanes), lambda i, slot_ref: (i, 0))
    else:
        in_spec = pl.BlockSpec((None, tm, lanes), lambda i, slot_ref: (layer, i, 0))
    grid_spec = pltpu.PrefetchScalarGridSpec(
        num_scalar_prefetch=1, grid=(r // tm,), in_specs=[in_spec],
        out_specs=pl.BlockSpec((None, tm, lanes), lambda i, slot_ref: (slot_ref[0], i, 0)))
    return pl.pallas_call(
        body, name=name, grid_spec=grid_spec, out_shape=jax.ShapeDtypeStruct((n_slots, r, lanes), dtype),
        compiler_params=_params(("parallel",)),
    )(slot.reshape(1).astype(jnp.int32), src)


def _chip_peers():
    x, y, c = _place()
    return 2 * x + y, c, [(1 - x, y), (x, 1 - y), (1 - x, 1 - y)]


def _comm_call(name, body, ins, out_shapes, n_sems, aliases=None):
    return pl.pallas_call(
        body, name=name, in_specs=[_ANY] * len(ins), out_specs=[_ANY] * len(out_shapes), out_shape=out_shapes,
        input_output_aliases=aliases or {},
        scratch_shapes=[pltpu.SemaphoreType.DMA((n_sems,)), pltpu.SemaphoreType.DMA((n_sems,))],
    )(*ins)


def _gather_chips(name, bufs):
    n = len(bufs)

    def body(*refs):
        outs, send_sems, recv_sems = refs[n:2 * n], refs[2 * n], refs[2 * n + 1]
        x, y, c = _place()
        me, _, peers = _chip_peers()

        def rows(a, block, half):
            rh = outs[a].shape[1] // 2
            return outs[a].at[block, pl.ds(half * rh, rh)]

        def copy(a, j, block, half, to, sem):
            return pltpu.make_async_remote_copy(
                src_ref=rows(a, block, half), dst_ref=rows(a, block, half), send_sem=send_sems.at[sem],
                recv_sem=recv_sems.at[sem], device_id=to, device_id_type=MESH)

        def over_ici(a, j, block):
            px, py = peers[j]
            return copy(a, j, block, c, (px, py, c), 6 * a + j)

        def to_sibling(a, j, block, half):
            return copy(a, j, block, half, (x, y, 1 - c), 6 * a + 3 + j)

        sends = [over_ici(a, j, me) for a in range(n) for j in range(3)]
        for cp in sends:
            cp.start()
        for a in range(n):
            for j, (px, py) in enumerate(peers):
                over_ici(a, j, 2 * px + py).wait_recv()
                handed = to_sibling(a, j, 2 * px + py, c)
                handed.start()
                sends.append(handed)
        for a in range(n):
            for j, (px, py) in enumerate(peers):
                to_sibling(a, j, 2 * px + py, 1 - c).wait_recv()
        for cp in sends:
            cp.wait_send()

    shapes = [jax.ShapeDtypeStruct(b.shape, b.dtype) for b in bufs]
    return _comm_call(name, body, bufs, shapes, 6 * n, {a: a for a in range(n)})


_HBM = pl.BlockSpec(memory_space=pltpu.HBM)
_SEM = pl.BlockSpec(memory_space=pltpu.SEMAPHORE)
_EFFECT = pltpu.SideEffectType.DATAFLOW_SIDE_EFFECTING


def _half_rows(ref, block, half):
    rh = ref.shape[1] // 2
    return ref.at[block, pl.ds(half * rh, rh)]


def _gather_start(name, bufs, after):
    n = len(bufs)

    def body(*refs):
        ins, send_sems, recv_sems, token = refs[:n], refs[n + 1], refs[n + 2], refs[-1]
        me, c, peers = _chip_peers()
        for a in range(n):
            mine = _half_rows(ins[a], me, c)
            for j, (px, py) in enumerate(peers):
                pltpu.make_async_remote_copy(
                    src_ref=mine, dst_ref=mine, send_sem=send_sems.at[3 * a + j], recv_sem=recv_sems.at[3 * a + j],
                    device_id=(px, py, c), device_id_type=MESH).start()
        token[...] = jnp.zeros_like(token)

    out_shape = (pltpu.SemaphoreType.DMA((3 * n,)), pltpu.SemaphoreType.DMA((3 * n,)),
                 *[pltpu.HBM(b.shape, b.dtype) for b in bufs], jax.ShapeDtypeStruct((8, LANES), F32))
    return pl.pallas_call(
        body, name=name, out_shape=out_shape, in_specs=[_HBM] * n + [_ANY],
        out_specs=(_SEM, _SEM, *[_HBM] * n, pl.BlockSpec(memory_space=pltpu.VMEM)),
        input_output_aliases={a: 2 + a for a in range(n)},
        compiler_params=pltpu.CompilerParams(has_side_effects=_EFFECT),
    )(*[pltpu.with_memory_space_constraint(b, pltpu.HBM) for b in bufs], after)


def _gather_wait(name, bufs, send_sems, recv_sems, after):
    n = len(bufs)

    def body(*refs):
        ins, send_sems, recv_sems = refs[:n], refs[n], refs[n + 1]
        me, c, peers = _chip_peers()
        for a in range(n):
            for j, (px, py) in enumerate(peers):
                copy = pltpu.make_async_remote_copy(
                    src_ref=_half_rows(ins[a], me, c), dst_ref=_half_rows(ins[a], 2 * px + py, c),
                    send_sem=send_sems.at[3 * a + j], recv_sem=recv_sems.at[3 * a + j],
                    device_id=(px, py, c), device_id_type=MESH)
                copy.wait_send()
                copy.wait_recv()

    return pl.pallas_call(
        body, name=name, out_shape=tuple(pltpu.HBM(b.shape, b.dtype) for b in bufs),
        in_specs=[_HBM] * n + [_SEM, _SEM, _ANY], out_specs=tuple([_HBM] * n),
        input_output_aliases={a: a for a in range(n)},
        compiler_params=pltpu.CompilerParams(has_side_effects=_EFFECT),
    )(*bufs, send_sems, recv_sems, after)


def _hand_over(name, bufs):
    n = len(bufs)

    def body(*refs):
        outs, send_sems, recv_sems = refs[n:2 * n], refs[2 * n], refs[2 * n + 1]
        x, y, c = _place()
        _, _, peers = _chip_peers()

        def copy(a, j, half):
            px, py = peers[j]
            rows = _half_rows(outs[a], 2 * px + py, half)
            return pltpu.make_async_remote_copy(
                src_ref=rows, dst_ref=rows, send_sem=send_sems.at[3 * a + j], recv_sem=recv_sems.at[3 * a + j],
                device_id=(x, y, 1 - c), device_id_type=MESH)

        sends = [copy(a, j, c) for a in range(n) for j in range(3)]
        for cp in sends:
            cp.start()
        for a in range(n):
            for j in range(3):
                copy(a, j, 1 - c).wait_recv()
        for cp in sends:
            cp.wait_send()

    shapes = [jax.ShapeDtypeStruct(b.shape, b.dtype) for b in bufs]
    return _comm_call(name, body, bufs, shapes, 3 * n, {a: a for a in range(n)})


def _pair_gather(name, bufs):
    n = len(bufs)

    def body(*refs):
        ins, outs, send_sems, recv_sems = refs[:n], refs[n:2 * n], refs[2 * n], refs[2 * n + 1]
        x, y, c = _place()

        def copy(a, block):
            return pltpu.make_async_remote_copy(
                src_ref=ins[a].at[block], dst_ref=outs[a].at[block], send_sem=send_sems.at[a],
                recv_sem=recv_sems.at[a], device_id=(x, y, 1 - c), device_id_type=MESH)

        sends = [copy(a, c) for a in range(n)]
        for cp in sends:
            cp.start()
        for a in range(n):
            copy(a, 1 - c).wait_recv()
        for cp in sends:
            cp.wait_send()

    shapes = [jax.ShapeDtypeStruct(b.shape, b.dtype) for b in bufs]
    return _comm_call(name, body, bufs, shapes, n, {a: a for a in range(n)})


def _all_peers():
    x, y, c = _place()
    peers = []
    for mask in range(1, N_DEV):
        fx, fy, fc = (mask >> 2) & 1, (mask >> 1) & 1, mask & 1
        peers.append((jnp.where(fx, 1 - x, x), jnp.where(fy, 1 - y, y), jnp.where(fc, 1 - c, c)))
    return 4 * x + 2 * y + c, peers


def _reduce_copies(srcs, lands, send_sems, recv_sems):
    me, peers = _all_peers()
    sends, arrivals = [], []
    for a in range(len(srcs)):
        for j, (px, py, pc) in enumerate(peers):
            k = (N_DEV - 1) * a + j
            sends.append(pltpu.make_async_remote_copy(
                src_ref=srcs[a].at[2 * px + py, pc], dst_ref=lands[a].at[me], send_sem=send_sems.at[k],
                recv_sem=recv_sems.at[k], device_id=(px, py, pc), device_id_type=MESH))
            arrivals.append(pltpu.make_async_remote_copy(
                src_ref=srcs[a].at[2 * px + py, pc], dst_ref=lands[a].at[4 * px + 2 * py + pc],
                send_sem=send_sems.at[k], recv_sem=recv_sems.at[k], device_id=(px, py, pc), device_id_type=MESH))
    return sends, arrivals


def _reduce_direct(name, srcs, pin=None):
    n = len(srcs)
    extra = [] if pin is None else [pin]

    def body(*refs):
        ins, outs = refs[:n], refs[n + len(extra):2 * n + len(extra)]
        sends, arrivals = _reduce_copies(ins, outs, refs[-2], refs[-1])
        for cp in sends:
            cp.start()
        for cp in arrivals:
            cp.wait_recv()
        for cp in sends:
            cp.wait_send()

    shapes = [jax.ShapeDtypeStruct((N_DEV,) + s.shape[2:], s.dtype) for s in srcs]
    return _comm_call(name, body, list(srcs) + extra, shapes, (N_DEV - 1) * n)


def _reduce_start(name, srcs):
    n = len(srcs)
    lands = [lax.empty((N_DEV,) + s.shape[2:], s.dtype) for s in srcs]

    def body(*refs):
        sends, _ = _reduce_copies(refs[:n], refs[n:2 * n], refs[2 * n], refs[2 * n + 1])
        for cp in sends:
            cp.start()
        refs[-1][...] = jnp.zeros_like(refs[-1])

    bufs = list(srcs) + lands
    n_sems = (N_DEV - 1) * n
    out_shape = (pltpu.SemaphoreType.DMA((n_sems,)), pltpu.SemaphoreType.DMA((n_sems,)),
                 *[pltpu.HBM(b.shape, b.dtype) for b in bufs], jax.ShapeDtypeStruct((8, LANES), F32))
    return pl.pallas_call(
        body, name=name, out_shape=out_shape, in_specs=[_HBM] * (2 * n),
        out_specs=(_SEM, _SEM, *[_HBM] * (2 * n), pl.BlockSpec(memory_space=pltpu.VMEM)),
        input_output_aliases={a: 2 + a for a in range(2 * n)},
        compiler_params=pltpu.CompilerParams(has_side_effects=_EFFECT),
    )(*[pltpu.with_memory_space_constraint(b, pltpu.HBM) for b in bufs])


def _reduce_wait(name, srcs, lands, send_sems, recv_sems, after):
    n = len(srcs)

    def body(*refs):
        sends, arrivals = _reduce_copies(refs[:n], refs[n:2 * n], refs[2 * n], refs[2 * n + 1])
        for cp in sends:
            cp.wait_send()
        for cp in arrivals:
            cp.wait_recv()

    bufs = list(srcs) + list(lands)
    outs = pl.pallas_call(
        body, name=name, out_shape=tuple(pltpu.HBM(b.shape, b.dtype) for b in bufs),
        in_specs=[_HBM] * (2 * n) + [_SEM, _SEM, _ANY], out_specs=tuple([_HBM] * (2 * n)),
        input_output_aliases={a: a for a in range(2 * n)},
        compiler_params=pltpu.CompilerParams(has_side_effects=_EFFECT),
    )(*bufs, send_sems, recv_sems, after)
    return list(outs[n:])


def _reduce_sum(name, own, land, chip, core):
    n, rh, lanes = land.shape
    tm = _row_tile(rh, 1024, ROW_ALIGN)

    def body(idx_ref, own_ref, *rest):
        total = own_ref[...]
        for g_ref in rest[:-1]:
            total = total + g_ref[...].astype(F32)
        rest[-1][...] = total

    def block(k):
        return pl.BlockSpec((None, tm, lanes), lambda i, idx_ref: ((2 * idx_ref[0] + idx_ref[1] + k) % n, i, 0))

    grid_spec = pltpu.PrefetchScalarGridSpec(
        num_scalar_prefetch=1, grid=(rh // tm,),
        in_specs=[pl.BlockSpec((None, None, tm, lanes), lambda i, idx_ref: (idx_ref[0], idx_ref[1], i, 0))]
        + [block(k) for k in range(1, n)],
        out_specs=pl.BlockSpec((None, tm, lanes), lambda i, idx_ref: (idx_ref[1], i, 0)))
    return pl.pallas_call(
        body, name=name, grid_spec=grid_spec, out_shape=jax.ShapeDtypeStruct((2, rh, lanes), F32),
        compiler_params=_params(("parallel",)),
    )(jnp.stack([chip, core]).astype(jnp.int32), own, *[land] * (n - 1))


def _gather_all_start(name, buf):
    def body(in_ref, send_sems, recv_sems, out_ref, token):
        me, peers = _all_peers()
        for j, peer in enumerate(peers):
            pltpu.make_async_remote_copy(
                src_ref=in_ref.at[me], dst_ref=in_ref.at[me], send_sem=send_sems.at[j], recv_sem=recv_sems.at[j],
                device_id=peer, device_id_type=MESH).start()
        token[...] = jnp.zeros_like(token)

    n = N_DEV - 1
    return pl.pallas_call(
        body, name=name, in_specs=[_HBM],
        out_shape=(pltpu.SemaphoreType.DMA((n,)), pltpu.SemaphoreType.DMA((n,)), pltpu.HBM(buf.shape, buf.dtype),
                   jax.ShapeDtypeStruct((8, LANES), F32)),
        out_specs=(_SEM, _SEM, _HBM, pl.BlockSpec(memory_space=pltpu.VMEM)), input_output_aliases={0: 2},
        compiler_params=pltpu.CompilerParams(has_side_effects=_EFFECT),
    )(pltpu.with_memory_space_constraint(buf, pltpu.HBM))


def _gather_all_wait(name, buf, send_sems, recv_sems, after):
    def body(in_ref, send_sems, recv_sems, after_ref, out_ref):
        me, peers = _all_peers()
        for j, (px, py, pc) in enumerate(peers):
            copy = pltpu.make_async_remote_copy(
                src_ref=in_ref.at[me], dst_ref=in_ref.at[4 * px + 2 * py + pc], send_sem=send_sems.at[j],
                recv_sem=recv_sems.at[j], device_id=(px, py, pc), device_id_type=MESH)
            copy.wait_send()
            copy.wait_recv()

    return pl.pallas_call(
        body, name=name, in_specs=[_HBM, _SEM, _SEM, _ANY], out_shape=pltpu.HBM(buf.shape, buf.dtype),
        out_specs=_HBM, input_output_aliases={0: 0},
        compiler_params=pltpu.CompilerParams(has_side_effects=_EFFECT),
    )(buf, send_sems, recv_sems, after)


def _sum_blocks(name, stacked, tm):
    n, r, lanes = stacked.shape

    def body(in_ref, o_ref):
        acc = in_ref[0]
        for j in range(1, n):
            acc = acc + in_ref[j]
        o_ref[...] = acc

    return pl.pallas_call(
        body, name=name, grid=(r // tm,), in_specs=[pl.BlockSpec((n, tm, lanes), lambda i: (0, i, 0))],
        out_specs=pl.BlockSpec((tm, lanes), lambda i: (i, 0)), out_shape=jax.ShapeDtypeStruct((r, lanes), F32),
        compiler_params=_params(("parallel",)),
    )(stacked)


def _row_tile(rows, pref, align):
    best = None
    for t in range(align, min(rows, pref) + 1, align):
        if rows % t == 0:
            best = t
    assert best is not None, (rows, pref, align)
    return best


def _adam(name, w, g, m, v):
    rows, width = w.shape
    tm = _row_tile(rows, max(8, 4096 * LANES // width), 8)
    args = [(t, width, 0) for t in (w, g, m, v)]
    return _rowcall(name, _adam_fn, args, [], [(width, F32)] * 3, tm=tm)


def kernel(x, norm_mix, norm_mlp, norm_final, mlp_w1, mlp_w2, ab_w_in, ab_w_out, rg_conv_w, rg_conv_b, rg_w_a, rg_b_a, rg_w_x, rg_b_x, rg_lambda, hg_lb_logits, hg_norm, gla_w_in, gla_w_out, gla_w_gate_up, gla_b_gate, gla_norm, loss_target, m_norm_mix, m_norm_mlp, m_norm_final, m_mlp_w1, m_mlp_w2, m_ab_w_in, m_ab_w_out, m_rg_conv_w, m_rg_conv_b, m_rg_w_a, m_rg_b_a, m_rg_w_x, m_rg_b_x, m_rg_lambda, m_hg_lb_logits, m_hg_norm, m_gla_w_in, m_gla_w_out, m_gla_w_gate_up, m_gla_b_gate, m_gla_norm, v_norm_mix, v_norm_mlp, v_norm_final, v_mlp_w1, v_mlp_w2, v_ab_w_in, v_ab_w_out, v_rg_conv_w, v_rg_conv_b, v_rg_w_a, v_rg_b_a, v_rg_w_x, v_rg_b_x, v_rg_lambda, v_hg_lb_logits, v_hg_norm, v_gla_w_in, v_gla_w_out, v_gla_w_gate_up, v_gla_b_gate, v_gla_norm):
    w = dict(norm_mix=norm_mix, norm_mlp=norm_mlp, norm_final=norm_final, mlp_w1=mlp_w1, mlp_w2=mlp_w2, ab_w_in=ab_w_in, ab_w_out=ab_w_out, rg_conv_w=rg_conv_w, rg_conv_b=rg_conv_b, rg_w_a=rg_w_a, rg_b_a=rg_b_a, rg_w_x=rg_w_x, rg_b_x=rg_b_x, rg_lambda=rg_lambda, hg_lb_logits=hg_lb_logits, hg_norm=hg_norm, gla_w_in=gla_w_in, gla_w_out=gla_w_out, gla_w_gate_up=gla_w_gate_up, gla_b_gate=gla_b_gate, gla_norm=gla_norm)
    m = dict(norm_mix=m_norm_mix, norm_mlp=m_norm_mlp, norm_final=m_norm_final, mlp_w1=m_mlp_w1, mlp_w2=m_mlp_w2, ab_w_in=m_ab_w_in, ab_w_out=m_ab_w_out, rg_conv_w=m_rg_conv_w, rg_conv_b=m_rg_conv_b, rg_w_a=m_rg_w_a, rg_b_a=m_rg_b_a, rg_w_x=m_rg_w_x, rg_b_x=m_rg_b_x, rg_lambda=m_rg_lambda, hg_lb_logits=m_hg_lb_logits, hg_norm=m_hg_norm, gla_w_in=m_gla_w_in, gla_w_out=m_gla_w_out, gla_w_gate_up=m_gla_w_gate_up, gla_b_gate=m_gla_b_gate, gla_norm=m_gla_norm)
    v = dict(norm_mix=v_norm_mix, norm_mlp=v_norm_mlp, norm_final=v_norm_final, mlp_w1=v_mlp_w1, mlp_w2=v_mlp_w2, ab_w_in=v_ab_w_in, ab_w_out=v_ab_w_out, rg_conv_w=v_rg_conv_w, rg_conv_b=v_rg_conv_b, rg_w_a=v_rg_w_a, rg_b_a=v_rg_b_a, rg_w_x=v_rg_w_x, rg_b_x=v_rg_b_x, rg_lambda=v_rg_lambda, hg_lb_logits=v_hg_lb_logits, hg_norm=v_hg_norm, gla_w_in=v_gla_w_in, gla_w_out=v_gla_w_out, gla_w_gate_up=v_gla_w_gate_up, gla_b_gate=v_gla_b_gate, gla_norm=v_gla_norm)
    chip = 2 * lax.axis_index("x") + lax.axis_index("y")
    core = lax.axis_index("c")
    sharded_shapes = [w[n].shape for n in SMALL_SHARDED]

    slots = [_into_slot(f"cast_{n}{layer}", w[n], chip, N_CHIPS, BF16, 512, layer) for n, layer in MATRICES]
    early = [i for i, (n, _) in enumerate(MATRICES) if n in EARLY_MATRICES]
    rest = [i for i in range(len(MATRICES)) if i not in early]

    def named(indices, arrays):
        big = {}
        for i, t in zip(indices, arrays):
            big.setdefault(MATRICES[i][0], []).append(t)
        return {n: (v if n in ("mlp_w1", "mlp_w2") else v[0]) for n, v in big.items()}

    vectors = _pack([w[n] for n in SMALL_SHARDED])
    vectors = _into_slot("place_vectors", vectors, chip, N_CHIPS, F32, vectors.shape[0])
    *gathered, vectors = _gather_chips("gather_early", [slots[i] for i in early] + [vectors])
    send_sems, recv_sems, *in_flight, token = _gather_start("gather_rest_start", [slots[i] for i in rest], gathered[0])

    def late_weights(after):
        landed = _gather_wait("gather_rest_wait", in_flight, send_sems, recv_sems, after)
        return _prepare_matrices(named(rest, _hand_over("gather_rest_share", list(landed))))

    big = named(early, gathered)
    small_all = _unpack(vectors, sharded_shapes, lead=1)
    full = {n: w[n] for n in SMALL_REPLICATED}
    for n, t in zip(SMALL_SHARDED, small_all):
        full[n] = _join_chips(t, t.ndim - 2)

    def halves(t):
        return t.reshape(N_CHIPS, 2, t.shape[1] // 2, t.shape[2])

    in_flight_grads = {}

    def emit(tag, arrays32, arrays16):
        n = len(arrays16)
        send, recv, *rest = _reduce_start(f"reduce_{tag}_start", [halves(t) for t in arrays16])
        in_flight_grads[tag] = ([halves(t) for t in arrays32], rest[:n], rest[n:2 * n], send, recv)
        return rest[-1]

    loss_part, grad_x, g_kernel = _local_step(
        x[0], loss_target[0], _prepare_weights(big, full), token, late_weights, emit)
    g_big, g_full = _finish_grads(g_kernel)

    small_names = SMALL_REPLICATED + SMALL_SHARDED
    reduced_shapes = [g_full[n].shape for n in small_names] + [loss_part.shape]
    g_small = _pack([g_full[n] for n in small_names] + [loss_part])
    device = 2 * chip + core
    g_small = _into_slot("place_small", g_small, device, N_DEV, F32, g_small.shape[0])
    small_send, small_recv, small_in_flight, small_token = _gather_all_start("reduce_small_start", g_small)

    mine = {}
    for tag, (own, srcs, lands, send, recv) in in_flight_grads.items():
        landed = _reduce_wait(f"reduce_{tag}_wait", srcs, lands, send, recv, small_token)
        mine[tag] = [_reduce_sum(f"reduce_add_{tag}{i}", o, f, chip, core) for i, (o, f) in enumerate(zip(own, landed))]
    ordered = [mine["mlp0"][0], mine["mlp1"][0], mine["mlp0"][1], mine["mlp1"][1], mine["ab"][0], mine["mlp0"][2],
               *mine["gla"]]
    reduced = [t.reshape(2 * t.shape[1], t.shape[2]) for t in _pair_gather("reduce_share", ordered)]
    by_name = {n: [] for n, _ in MATRICES}
    for (n, _), t in zip(MATRICES, reduced):
        by_name[n].append(t)
    grads = {n: jnp.stack(v) for n, v in by_name.items()}

    g_small_all = _gather_all_wait("reduce_small_wait", small_in_flight, small_send, small_recv, reduced[0])
    g_small_red = _sum_blocks("reduce_small_add", g_small_all, g_small_all.shape[1])
    *small_red, loss_sum = _unpack(g_small_red, reduced_shapes)
    loss = loss_sum[0, 0]
    g_small_full = dict(zip(small_names, small_red))
    for n in SMALL_REPLICATED:
        grads[n] = g_small_full[n]
    for n in SMALL_SHARDED:
        width = w[n].shape[-1]
        grads[n] = lax.dynamic_slice_in_dim(g_small_full[n], chip * width, width, axis=g_small_full[n].ndim - 1)

    delta, new_m, new_v = {}, {}, {}
    for n in by_name:
        flat = [t.reshape(-1, t.shape[-1]) for t in (w[n], grads[n], m[n], v[n])]
        for dst, t in zip((delta, new_m, new_v), _adam(f"adam_{n}", *flat)):
            dst[n] = t.reshape(w[n].shape)
    small_shapes = [w[n].shape for n in small_names]
    packs = [_pack([src[n] for n in small_names]) for src in (w, grads, m, v)]
    d_small, m_small, v_small = _adam("adam_small", *packs)
    for dst, buf in ((delta, d_small), (new_m, m_small), (new_v, v_small)):
        dst.update(zip(small_names, _unpack(buf, small_shapes)))

    return (loss, grad_x[None], *[grads[n] for n in WEIGHTS], *[delta[n] for n in WEIGHTS],
            *[new_m[n] for n in WEIGHTS], *[new_v[n] for n in WEIGHTS])
```

```python
import functools

import jax
import jax.numpy as jnp
from jax import lax
from jax.experimental import pallas as pl
from jax.experimental.pallas import tpu as pltpu

F32 = jnp.float32
BF16 = jnp.bfloat16
MESH = pl.DeviceIdType.MESH

LANES = 128
CHUNK = 64
ATTN_SUB = 4
EPS = 1e-6
RG_C = 8.0
N_CHIPS = 4
N_DEV = 8
GLA_IN_WIDTH = 3104
GLA_IN_PAD = 3200
VMEM_LIMIT = 56 * 1024 * 1024

ADAM_LR = 0.001
ADAM_B1 = 0.9
ADAM_B2 = 0.999
ADAM_EPS = 1e-08
ADAM_WD = 0.01
ADAM_STEP = 10


def _raw_dot(a, b, ca, cb):
    return lax.dot_general(a.astype(BF16), b.astype(BF16), (((ca,), (cb,)), ((), ())),
                           preferred_element_type=F32)


def _raw_nn(a, b):
    return _raw_dot(a, b, 1, 0)


def _raw_nt(a, b):
    return _raw_dot(a, b, 1, 1)


def _raw_tn(a, b):
    return _raw_dot(a, b, 0, 0)


@jax.custom_vjp
def _dot_nn(a, b):
    return _raw_nn(a, b)


def _dot_nn_fwd(a, b):
    return _raw_nn(a, b), (a, b)


def _dot_nn_bwd(res, g):
    a, b = res
    return _raw_nt(g, b), _raw_tn(a, g)


_dot_nn.defvjp(_dot_nn_fwd, _dot_nn_bwd)


@jax.custom_vjp
def _dot_nt(a, b):
    return _raw_nt(a, b)


def _dot_nt_fwd(a, b):
    return _raw_nt(a, b), (a, b)


def _dot_nt_bwd(res, g):
    a, b = res
    return _raw_nn(g, b), _raw_tn(g, a)


_dot_nt.defvjp(_dot_nt_fwd, _dot_nt_bwd)


@jax.custom_vjp
def _dot_tn(a, b):
    return _raw_tn(a, b)


def _dot_tn_fwd(a, b):
    return _raw_tn(a, b), (a, b)


def _dot_tn_bwd(res, g):
    a, b = res
    return _raw_nt(b, g), _raw_nn(a, g)


_dot_tn.defvjp(_dot_tn_fwd, _dot_tn_bwd)


def _tile(n, pref):
    if n <= pref:
        return n
    t = (pref // LANES) * LANES
    while t > LANES and n % t:
        t -= LANES
    assert n % t == 0, (n, pref)
    return t


def _params(sem):
    return pltpu.CompilerParams(dimension_semantics=sem, vmem_limit_bytes=VMEM_LIMIT)


def _rowcall(name, fn, rows, pars, row_outs, par_outs=(), tm=512, pin=None):
    if pin is not None:
        inner, pars = fn, list(pars) + [pin]
        fn = lambda *vals: inner(*vals[:-1])
    n_rows = rows[0][0].shape[0]
    tm = min(tm, n_rows)
    assert n_rows % tm == 0
    n_r, n_p, n_ro = len(rows), len(pars), len(row_outs)

    def body(*refs):
        vals = [r[...].astype(F32) for r in refs[:n_r + n_p]]
        outs = fn(*vals)
        o_refs = refs[n_r + n_p:n_r + n_p + n_ro]
        po_refs = refs[n_r + n_p + n_ro:]
        for o_ref, val in zip(o_refs, outs[:n_ro]):
            o_ref[...] = val.astype(o_ref.dtype)
        first = pl.program_id(0) == 0
        for po_ref, val in zip(po_refs, outs[n_ro:]):
            @pl.when(first)
            def _():
                po_ref[...] = val

            @pl.when(jnp.logical_not(first))
            def _():
                po_ref[...] += val

    def const_map(nd):
        return lambda i: (0,) * nd

    def row_spec(w, cb):
        return pl.BlockSpec((tm, w), lambda i: (i, cb))

    in_specs = [row_spec(w, cb) for _, w, cb in rows]
    in_specs += [pl.BlockSpec(p.shape, const_map(p.ndim)) for p in pars]
    out_specs = [pl.BlockSpec((tm, w), lambda i: (i, 0)) for w, _ in row_outs]
    out_specs += [pl.BlockSpec(tuple(s), const_map(len(s))) for s in par_outs]
    out_shape = [jax.ShapeDtypeStruct((n_rows, w), dt) for w, dt in row_outs]
    out_shape += [jax.ShapeDtypeStruct(tuple(s), F32) for s in par_outs]
    return pl.pallas_call(
        body, name=name, grid=(n_rows // tm,), in_specs=in_specs, out_specs=out_specs, out_shape=out_shape,
        compiler_params=_params(("arbitrary",) if par_outs else ("parallel",)),
    )(*[r[0] for r in rows], *pars)


def _vjp_of(fn, n_prim, n_out, n_par, n_pass=0):
    def bwd(*args):
        prim = args[:n_prim]
        cts = args[n_prim:n_prim + n_out]
        passes = args[n_prim + n_out:n_prim + n_out + 2 * n_pass]
        pars = args[n_prim + n_out + 2 * n_pass:]
        _, vjp = jax.vjp(fn, *prim, *pars)
        grads = vjp(tuple(cts))
        sums = tuple(passes[2 * i] + passes[2 * i + 1] for i in range(n_pass))
        return tuple(grads[:n_prim]) + sums + tuple(grads[n_prim:])
    return bwd


def _mm(name, a, b, mode="nn", extras=(), epi=None, out_dtypes=(F32,), a_pro=None, out_split=None,
        epi_pars=(), row_sum=False, pin=None, tm=1024, tn=1024, tk=1024):
    split = b.shape[0] if b.ndim == 3 else None
    b_rows, b_cols = b.shape[-2:]
    if mode == "nn":
        (m, k), n = a.shape, b_cols * (split or 1)
    elif mode == "nt":
        (m, k), n = a.shape, b_rows
        assert k == b_cols * (split or 1)
    else:
        assert split is None
        (k, m), n = a.shape, b_cols
    tm, tk = _tile(m, tm), _tile(k, tk)
    tn = _tile(n // out_split, tn) if out_split else _tile(n, tn)
    if split and mode == "nn":
        tn = _tile(b_cols, tn)
    if split and mode == "nt":
        tk = _tile(b_cols, tk)
    nk = k // tk
    raw = {"nn": _raw_nn, "nt": _raw_nt, "tn": _raw_tn}[mode]
    n_e, n_p, n_o = len(extras), len(epi_pars), len(out_dtypes)
    n_in = n_e + n_p + (0 if pin is None else 1)
    if epi is None:
        epi = lambda acc: (acc,)

    def body(a_ref, b_ref, *rest):
        e_refs, p_refs, o_refs = rest[:n_e], rest[n_e:n_e + n_p], rest[n_in:n_in + n_o]
        kk = pl.program_id(2)
        a_tile = a_ref[...] if a_pro is None else a_pro(a_ref[...].astype(F32))
        part = raw(a_tile, b_ref[...])

        def finish(total):
            res = epi(total, *[e[...].astype(F32) for e in e_refs], *[p[...] for p in p_refs])
            for o_ref, r in zip(o_refs, res):
                o_ref[...] = r.astype(o_ref.dtype)
            if row_sum:
                rest[n_in + n_o][...] = res[n_o]

        if nk == 1:
            finish(part)
            return
        acc = rest[-1]

        @pl.when(kk == 0)
        def _():
            acc[...] = part

        @pl.when((kk > 0) & (kk < nk - 1))
        def _():
            acc[...] += part

        @pl.when(kk == nk - 1)
        def _():
            finish(acc[...] + part)

    a_spec = pl.BlockSpec((tk, tm), lambda i, j, kk: (kk, i)) if mode == "tn" else pl.BlockSpec((tm, tk), lambda i, j, kk: (i, kk))
    if split and mode == "nn":
        per = b_cols // tn
        b_spec = pl.BlockSpec((None, tk, tn), lambda i, j, kk: (j // per, kk, j % per))
    elif split:
        per = b_cols // tk
        b_spec = pl.BlockSpec((None, tn, tk), lambda i, j, kk: (kk // per, j, kk % per))
    elif mode == "nt":
        b_spec = pl.BlockSpec((tn, tk), lambda i, j, kk: (j, kk))
    else:
        b_spec = pl.BlockSpec((tk, tn), lambda i, j, kk: (kk, j))
    mn_spec = pl.BlockSpec((tm, tn), lambda i, j, kk: (i, j))
    if out_split:
        assert not extras
        per_out = n // out_split // tn
        out_spec = pl.BlockSpec((None, tm, tn), lambda i, j, kk: (j // per_out, i, j % per_out))
        out_shapes = [jax.ShapeDtypeStruct((out_split, m, n // out_split), dt) for dt in out_dtypes]
    else:
        out_spec = mn_spec
        out_shapes = [jax.ShapeDtypeStruct((m, n), dt) for dt in out_dtypes]
    out_specs = [out_spec] * n_o
    if row_sum:
        out_specs.append(pl.BlockSpec((None, 1, tn), lambda i, j, kk: (i, 0, j)))
        out_shapes.append(jax.ShapeDtypeStruct((m // tm, 1, n), F32))
    in_specs = [a_spec, b_spec] + [mn_spec] * n_e
    in_specs += [pl.BlockSpec(p.shape, functools.partial(lambda i, j, kk, nd: (0,) * nd, nd=p.ndim)) for p in epi_pars]
    in_specs += [] if pin is None else [pl.BlockSpec(memory_space=pl.ANY)]
    outs = pl.pallas_call(
        body, name=name, grid=(m // tm, n // tn, nk), in_specs=in_specs, out_specs=out_specs, out_shape=out_shapes,
        scratch_shapes=[pltpu.VMEM((tm, tn), F32)] if nk > 1 else [],
        compiler_params=_params(("parallel", "parallel", "arbitrary")),
    )(a, b, *extras, *epi_pars, *([] if pin is None else [pin]))
    return outs[0] if len(outs) == 1 else outs


def _sigmoid(x):
    return jax.nn.sigmoid(x)


def _silu(x):
    return x * _sigmoid(x)


def _softplus(x):
    return jnp.maximum(x, 0.0) + jnp.log1p(jnp.exp(-jnp.abs(x)))


def _rmsnorm_fn(x, gain):
    return (x * lax.rsqrt(jnp.mean(x * x, axis=-1, keepdims=True) + EPS) * gain,)


def _head_norm(o, gain, n_heads):
    w = o.shape[-1] // n_heads
    parts = []
    for h in range(n_heads):
        oh = o[:, h * w:(h + 1) * w]
        parts.append(oh * lax.rsqrt(jnp.mean(oh * oh, axis=-1, keepdims=True) + EPS))
    return jnp.concatenate(parts, axis=-1) * gain


@jax.custom_jvp
def _neg_expm1(x):
    u = jnp.exp(x)
    is_one = u == 1.0
    return jnp.where(is_one, -x, (1.0 - u) * x / jnp.log(jnp.where(is_one, 2.0, u)))


@_neg_expm1.defjvp
def _neg_expm1_jvp(primals, tangents):
    (x,), (t,) = primals, tangents
    return _neg_expm1(x), -jnp.exp(x) * t


def _rg_gates_fn(xc, wa, wx, ba, bx, lam):
    outs = []
    for d in range(2):
        r = _sigmoid(_dot_nn(xc, wa[d]) + ba[d:d + 1])
        i = _sigmoid(_dot_nn(xc, wx[d]) + bx[d:d + 1])
        log_a = -RG_C * r * _softplus(-lam[d:d + 1])
        outs.append(jnp.exp(log_a))
        outs.append(jnp.sqrt(_neg_expm1(2.0 * log_a)) * (i * xc))
    return tuple(outs)


def _hg_pre_fn(q, f_f, f_b, logits):
    mx = jnp.maximum(logits[0:1], logits[1:2])
    e0 = jnp.exp(logits[0:1] - mx)
    e1 = jnp.exp(logits[1:2] - mx)
    lb = e0 / (e0 + e1)
    outs = [_silu(q)]
    for f in (f_f, f_b):
        outs.append((1.0 - lb) * _sigmoid(-f))
        outs.append(jnp.log(lb + (1.0 - lb) * _sigmoid(f)))
    return tuple(outs)


def _post0_fn(hs, ga, o, g, gain):
    ya = hs * jax.nn.gelu(ga, approximate=True)
    yb = _head_norm(o, gain, 4) * _silu(g)
    return (jnp.concatenate([ya, yb], axis=-1),)


def _post0_fwd_fn(h_f, h_b, ga, o_f, o_b, g, gain):
    return _post0_fn(h_f + h_b, ga, o_f + o_b, g, gain)


def _post0_bwd_fn(h_f, h_b, ga, o_f, o_b, g, dmix, gain):
    _, vjp = jax.vjp(_post0_fn, h_f + h_b, ga, o_f + o_b, g, gain)
    return vjp((dmix,))


def _gla_pre_fn(q, lr, w_up, b_gate):
    outs = [q * (128.0 ** -0.5)]
    for d in range(2):
        z = _dot_nn(lr, w_up[d]) + b_gate[d:d + 1]
        outs.append(-_softplus(-z) * (1.0 / 16.0))
    return tuple(outs)


def _gla_post_fn(o, r, gain):
    return (_head_norm(o, gain, 4) * _silu(r),)


def _gla_post_fwd_fn(o_f, o_b, r, gain):
    return _gla_post_fn(o_f + o_b, r, gain)


def _gla_post_bwd_fn(o_f, o_b, r, dmix, gain):
    _, vjp = jax.vjp(_gla_post_fn, o_f + o_b, r, gain)
    return vjp((dmix,))


def _relu2_bwd_epi(acc, hid):
    return (acc * 2.0 * jnp.maximum(hid, 0.0),)


def _relu2(x):
    r = jnp.maximum(x, 0.0)
    return r * r


def _add_epi(acc, res):
    return (acc + res,)


def _loss_head_fn(h, target, gain):
    def f(h, gain):
        y = _rmsnorm_fn(h, gain)[0]
        err = y - target
        return 0.5 * jnp.sum(jnp.mean(err * err, axis=-1, keepdims=True))
    loss, (dh, dgain) = jax.value_and_grad(f, argnums=(0, 1))(h, gain)
    return dh, dh, jnp.full((1, LANES), loss, F32), dgain


def _adam_fn(w, g, m, v):
    m2 = ADAM_B1 * m + (1.0 - ADAM_B1) * g
    v2 = ADAM_B2 * v + (1.0 - ADAM_B2) * (g * g)
    m_hat = m2 / (1.0 - ADAM_B1 ** ADAM_STEP)
    v_hat = v2 / (1.0 - ADAM_B2 ** ADAM_STEP)
    delta = -ADAM_LR * (m_hat / (jnp.sqrt(v_hat) + ADAM_EPS) + ADAM_WD * w)
    return delta, m2, v2


def _shifted(x, t_idx, off):
    n = x.shape[0]
    rolled = pltpu.roll(x, (-off) % n, 0)
    valid = (t_idx + off >= 0) & (t_idx + off < n)
    return jnp.where(valid, rolled, 0.0)


def _conv_fwd(name, src, colblock, w, b):
    n_rows, width = src.shape[0], w.shape[1]

    def body(x_ref, w_ref, b_ref, o_ref):
        x = x_ref[...]
        t_idx = lax.broadcasted_iota(jnp.int32, x.shape, 0)
        acc = b_ref[...] + w_ref[2:3, :] * x
        acc += w_ref[0:1, :] * _shifted(x, t_idx, -2)
        acc += w_ref[1:2, :] * _shifted(x, t_idx, -1)
        acc += w_ref[3:4, :] * _shifted(x, t_idx, 1)
        o_ref[...] = acc

    nb = width // LANES
    return pl.pallas_call(
        body, name=name, grid=(nb,),
        in_specs=[pl.BlockSpec((n_rows, LANES), lambda j: (0, colblock * nb + j)),
                  pl.BlockSpec((4, LANES), lambda j: (0, j)), pl.BlockSpec((1, LANES), lambda j: (0, j))],
        out_specs=pl.BlockSpec((n_rows, LANES), lambda j: (0, j)),
        out_shape=jax.ShapeDtypeStruct((n_rows, width), F32),
        compiler_params=_params(("parallel",)),
    )(src, w, b)


def _conv_bwd(name, src, colblock, w, d):
    n_rows, width = src.shape[0], w.shape[1]

    def body(x_ref, w_ref, d_ref, dx_ref, dw_ref, db_ref):
        x = x_ref[...]
        g = d_ref[...]
        t_idx = lax.broadcasted_iota(jnp.int32, x.shape, 0)
        dx = w_ref[2:3, :] * g
        dx += w_ref[0:1, :] * _shifted(g, t_idx, 2)
        dx += w_ref[1:2, :] * _shifted(g, t_idx, 1)
        dx += w_ref[3:4, :] * _shifted(g, t_idx, -1)
        dx_ref[...] = dx.astype(dx_ref.dtype)
        dw_ref[0:1, :] = jnp.sum(g * _shifted(x, t_idx, -2), axis=0, keepdims=True)
        dw_ref[1:2, :] = jnp.sum(g * _shifted(x, t_idx, -1), axis=0, keepdims=True)
        dw_ref[2:3, :] = jnp.sum(g * x, axis=0, keepdims=True)
        dw_ref[3:4, :] = jnp.sum(g * _shifted(x, t_idx, 1), axis=0, keepdims=True)
        db_ref[...] = jnp.sum(g, axis=0, keepdims=True)

    nb = width // LANES
    return pl.pallas_call(
        body, name=name, grid=(nb,),
        in_specs=[pl.BlockSpec((n_rows, LANES), lambda j: (0, colblock * nb + j)),
                  pl.BlockSpec((4, LANES), lambda j: (0, j)),
                  pl.BlockSpec((n_rows, LANES), lambda j: (0, j))],
        out_specs=[pl.BlockSpec((n_rows, LANES), lambda j: (0, j)), pl.BlockSpec((4, LANES), lambda j: (0, j)),
                   pl.BlockSpec((1, LANES), lambda j: (0, j))],
        out_shape=[jax.ShapeDtypeStruct((n_rows, width), BF16), jax.ShapeDtypeStruct((4, width), F32),
                   jax.ShapeDtypeStruct((1, width), F32)],
        compiler_params=_params(("parallel",)),
    )(src, w, d)


SUBLANES = 8
SCAN_UNROLL = 8


def _shift_rows(x, d, fill):
    n = x.shape[0]
    t = lax.broadcasted_iota(jnp.int32, x.shape, 0)
    valid = (t >= d) if d > 0 else (t < n + d)
    return jnp.where(valid, pltpu.roll(x, d % n, 0), fill)


def _tile_scan(a, u, reverse):
    d = 1
    while d < a.shape[0]:
        s = -d if reverse else d
        a_sh, u_sh = _shift_rows(a, s, 1.0), _shift_rows(u, s, 0.0)
        u = u + a * u_sh
        a = a * a_sh
        d *= 2
    return a, u


def _edge_row(x, reverse):
    return x[0:1, :] if reverse else x[SUBLANES - 1:SUBLANES, :]


def _scan_specs(n_rows, n):
    return [pl.BlockSpec((n_rows, LANES), lambda j: (0, j))] * n


def _scan_tile(a_ref, u_ref, h_ref, i, carry, reverse):
    n_tiles = a_ref.shape[0] // SUBLANES
    tile = (n_tiles - 1 - i) if reverse else i
    rows = pl.ds(pl.multiple_of(tile * SUBLANES, SUBLANES), SUBLANES)
    acc_a, acc_u = _tile_scan(a_ref[rows, :], u_ref[rows, :], reverse)
    h = acc_u + acc_a * carry
    h_ref[rows, :] = h
    return _edge_row(h, reverse)


def _scan_fwd(name, a_f, u_f, a_b, u_b):
    n_rows, width = a_f.shape

    def body(af_ref, uf_ref, ab_ref, ub_ref, hf_ref, hb_ref):
        def step(i, carry):
            return (_scan_tile(af_ref, uf_ref, hf_ref, i, carry[0], False),
                    _scan_tile(ab_ref, ub_ref, hb_ref, i, carry[1], True))
        zero = jnp.zeros((1, LANES), F32)
        lax.fori_loop(0, n_rows // SUBLANES, step, (zero, zero), unroll=SCAN_UNROLL)

    return pl.pallas_call(
        body, name=name, grid=(width // LANES,), in_specs=_scan_specs(n_rows, 4), out_specs=_scan_specs(n_rows, 2),
        out_shape=[jax.ShapeDtypeStruct((n_rows, width), F32)] * 2, compiler_params=_params(("parallel",)),
    )(a_f, u_f, a_b, u_b)


def _scan_bwd_tile(a_ref, h_ref, dh_ref, du_ref, da_ref, i, carry, reverse):
    n_rows = a_ref.shape[0]
    n_tiles = n_rows // SUBLANES
    against = not reverse
    one = -1 if against else 1
    g_in, a_edge = carry
    tile = (n_tiles - 1 - i) if against else i
    start = pl.multiple_of(tile * SUBLANES, SUBLANES)
    rows = pl.ds(start, SUBLANES)
    a_tile = a_ref[rows, :]
    coeff = _shift_rows(a_tile, one, a_edge)
    acc_a, acc_u = _tile_scan(coeff, dh_ref[rows, :], against)
    g = acc_u + acc_a * g_in
    du_ref[rows, :] = g
    outside = (start + SUBLANES) if reverse else (start - 1)
    inside = (outside >= 0) & (outside < n_rows)
    h_edge = jnp.where(inside, h_ref[pl.ds(jnp.clip(outside, 0, n_rows - 1), 1), :], 0.0)
    da_ref[rows, :] = g * _shift_rows(h_ref[rows, :], -one, h_edge)
    return _edge_row(g, against), _edge_row(a_tile, against)


def _scan_bwd(name, a_f, h_f, a_b, h_b, dh):
    n_rows, width = a_f.shape

    def body(af_ref, hf_ref, ab_ref, hb_ref, dh_ref, duf_ref, daf_ref, dub_ref, dab_ref):
        def step(i, carry):
            return (_scan_bwd_tile(af_ref, hf_ref, dh_ref, duf_ref, daf_ref, i, carry[0], False),
                    _scan_bwd_tile(ab_ref, hb_ref, dh_ref, dub_ref, dab_ref, i, carry[1], True))
        zero = jnp.zeros((1, LANES), F32)
        lax.fori_loop(0, n_rows // SUBLANES, step, ((zero, zero), (zero, zero)), unroll=SCAN_UNROLL)

    return pl.pallas_call(
        body, name=name, grid=(width // LANES,), in_specs=_scan_specs(n_rows, 5), out_specs=_scan_specs(n_rows, 4),
        out_shape=[jax.ShapeDtypeStruct((n_rows, width), F32)] * 4, compiler_params=_params(("parallel",)),
    )(a_f, h_f, a_b, h_b, dh)


def _tri_mask(c, reverse):
    row = lax.broadcasted_iota(jnp.int32, (c, c), 0)
    col = lax.broadcasted_iota(jnp.int32, (c, c), 1)
    return (col >= row) if reverse else (col <= row)


def _cumsum_rows(x, reverse):
    tri = _tri_mask(x.shape[0], reverse).astype(BF16)
    hi = x.astype(BF16)
    rest = x - hi.astype(F32)
    mid = rest.astype(BF16)
    lo = (rest - mid.astype(F32)).astype(BF16)
    return _raw_nn(tri, hi) + _raw_nn(tri, mid) + _raw_nn(tri, lo)


@functools.partial(jax.custom_vjp, nondiff_argnums=(1,))
def _cumsum(x, reverse):
    return _cumsum_rows(x, reverse)


def _cumsum_fwd(x, reverse):
    return _cumsum_rows(x, reverse), None


def _cumsum_bwd(reverse, _, g):
    return (_cumsum_rows(g, not reverse),)


_cumsum.defvjp(_cumsum_fwd, _cumsum_bwd)


def _chunks_fn(qs, ks, vs, lfs, sts, reverses):
    n, c = len(qs), qs[0].shape[0]
    every = range(n)
    tris = [_tri_mask(c, r) for r in reverses]
    cums = [_cumsum(lfs[i], reverses[i]) for i in every]
    rid = lax.broadcasted_iota(jnp.int32, cums[0].shape, 0)

    def pick(cum, r):
        return jnp.sum(jnp.where(rid == r, cum, 0.0), axis=0, keepdims=True)

    refs = [pick(cums[i], (c - 1 - c // 2) if reverses[i] else c // 2) for i in every]
    lasts = [pick(cums[i], 0 if reverses[i] else c - 1) for i in every]
    q_in = [qs[i] * jnp.exp(cums[i] - refs[i]) for i in every]
    k_in = [ks[i] * jnp.exp(refs[i] - cums[i]) for i in every]
    scores = [jnp.where(tris[i], _dot_nt(q_in[i], k_in[i]), 0.0) for i in every]
    o_intra = [_dot_nn(scores[i], vs[i]) for i in every]
    q_out = [qs[i] * jnp.exp(cums[i]) for i in every]
    o_inter = [_dot_nt(q_out[i], sts[i]) for i in every]
    k_state = [ks[i] * jnp.exp(lasts[i] - cums[i]) for i in every]
    upd = [_dot_tn(vs[i], k_state[i]) for i in every]
    st_new = [sts[i] * jnp.exp(lasts[i]) + upd[i] for i in every]
    return [o_intra[i] + o_inter[i] for i in every], st_new


def _attn_fwd(name, q, k_f, k_b, v, lf_f, lf_b, n_heads, dk, dv):
    n_rows = q[0].shape[0]
    n_chunks = n_rows // CHUNK
    n_steps = n_chunks // ATTN_SUB
    wk, wv = n_heads * dk, n_heads * dv

    def spec(width, off, rev):
        return pl.BlockSpec((CHUNK * ATTN_SUB, width), lambda n: ((n_steps - 1 - n) if rev else n, off))

    def sspec(rev):
        return pl.BlockSpec((ATTN_SUB, n_heads, dv, dk), lambda n: ((n_steps - 1 - n) if rev else n, 0, 0, 0))

    def body(qf, kf, vf, lff, qb, kb, vb, lfb, of_ref, ob_ref, sf_ref, sb_ref, st):
        @pl.when(pl.program_id(0) == 0)
        def _():
            st[...] = jnp.zeros_like(st)

        ins = ((qf, kf, vf, lff), (qb, kb, vb, lfb))
        chains = [(d, h) for d in range(2) for h in range(n_heads)]
        ck = [slice(h * dk, (h + 1) * dk) for h in range(n_heads)]
        cv = [slice(h * dv, (h + 1) * dv) for h in range(n_heads)]
        sts = [st[d, h] for d, h in chains]
        done = []
        for sub in range(ATTN_SUB):
            local = (sub, ATTN_SUB - 1 - sub)
            rows = [slice(local[d] * CHUNK, (local[d] + 1) * CHUNK) for d in range(2)]
            qs = [ins[d][0][rows[d], ck[h]] for d, h in chains]
            ks = [ins[d][1][rows[d], ck[h]] for d, h in chains]
            vs = [ins[d][2][rows[d], cv[h]] for d, h in chains]
            lfs = [ins[d][3][rows[d], ck[h]] for d, h in chains]
            os_, st_new = _chunks_fn(qs, ks, vs, lfs, sts, [d == 1 for d, _ in chains])
            done.append((local, rows, sts, os_))
            sts = st_new
        for local, rows, entered, os_ in done:
            for i, (d, h) in enumerate(chains):
                (sf_ref, sb_ref)[d][local[d], h] = entered[i].astype(BF16)
                (of_ref, ob_ref)[d][rows[d], cv[h]] = os_[i]
        for i, (d, h) in enumerate(chains):
            st[d, h] = sts[i]

    in_specs = [spec(wk, q[1], False), spec(wk, k_f[1], False), spec(wv, v[1], False), spec(wk, lf_f[1], False),
                spec(wk, q[1], True), spec(wk, k_b[1], True), spec(wv, v[1], True), spec(wk, lf_b[1], True)]
    return pl.pallas_call(
        body, name=name, grid=(n_steps,), in_specs=in_specs,
        out_specs=[spec(wv, 0, False), spec(wv, 0, True), sspec(False), sspec(True)],
        out_shape=[jax.ShapeDtypeStruct((n_rows, wv), F32)] * 2
        + [jax.ShapeDtypeStruct((n_chunks, n_heads, dv, dk), BF16)] * 2,
        scratch_shapes=[pltpu.VMEM((2, n_heads, dv, dk), F32)],
        compiler_params=_params(("arbitrary",)),
    )(q[0], k_f[0], v[0], lf_f[0], q[0], k_b[0], v[0], lf_b[0])


def _attn_bwd(name, q, k_f, k_b, v, lf_f, lf_b, st_f, st_b, do, n_heads, dk, dv, out_dtype=F32):
    n_rows = q[0].shape[0]
    n_chunks = n_rows // CHUNK
    n_steps = n_chunks // ATTN_SUB
    wk, wv = n_heads * dk, n_heads * dv

    def spec(width, off, rev):
        return pl.BlockSpec((CHUNK * ATTN_SUB, width), lambda n: (n if rev else (n_steps - 1 - n), off))

    def sspec(rev):
        return pl.BlockSpec((ATTN_SUB, n_heads, dv, dk), lambda n: (n if rev else (n_steps - 1 - n), 0, 0, 0))

    def body(qf, kf, vf, lff, sf, dof, qb, kb, vb, lfb, sb, dob,
             dqf, dkf, dvf, dlff, dqb, dkb, dvb, dlfb, dst):
        @pl.when(pl.program_id(0) == 0)
        def _():
            dst[...] = jnp.zeros_like(dst)

        ins = ((qf, kf, vf, lff, sf, dof), (qb, kb, vb, lfb, sb, dob))
        outs = ((dqf, dkf, dvf, dlff), (dqb, dkb, dvb, dlfb))
        chains = [(d, h) for d in range(2) for h in range(n_heads)]
        ck = [slice(h * dk, (h + 1) * dk) for h in range(n_heads)]
        cv = [slice(h * dv, (h + 1) * dv) for h in range(n_heads)]
        fn = functools.partial(_chunks_fn, reverses=[d == 1 for d, _ in chains])
        dsts = [dst[d, h] for d, h in chains]
        done = []
        for sub in range(ATTN_SUB):
            local = (ATTN_SUB - 1 - sub, sub)
            rows = [slice(local[d] * CHUNK, (local[d] + 1) * CHUNK) for d in range(2)]
            qs = [ins[d][0][rows[d], ck[h]] for d, h in chains]
            ks = [ins[d][1][rows[d], ck[h]] for d, h in chains]
            vs = [ins[d][2][rows[d], cv[h]] for d, h in chains]
            lfs = [ins[d][3][rows[d], ck[h]] for d, h in chains]
            sts = [ins[d][4][local[d], h].astype(F32) for d, h in chains]
            dos = [ins[d][5][rows[d], cv[h]] for d, h in chains]
            _, vjp = jax.vjp(fn, qs, ks, vs, lfs, sts)
            dqs, dks, dvs, dlfs, dsts = vjp((dos, dsts))
            done.append((rows, dqs, dks, dvs, dlfs))
        for rows, dqs, dks, dvs, dlfs in done:
            for i, (d, h) in enumerate(chains):
                dq_r, dk_r, dv_r, dlf_r = outs[d]
                dq_r[rows[d], ck[h]] = dqs[i].astype(dq_r.dtype)
                dk_r[rows[d], ck[h]] = dks[i].astype(dk_r.dtype)
                dv_r[rows[d], cv[h]] = dvs[i].astype(dv_r.dtype)
                dlf_r[rows[d], ck[h]] = dlfs[i].astype(dlf_r.dtype)
        for i, (d, h) in enumerate(chains):
            dst[d, h] = dsts[i]

    def dir_specs(kk, lf, rev):
        return [spec(wk, q[1], rev), spec(wk, kk[1], rev), spec(wv, v[1], rev), spec(wk, lf[1], rev), sspec(rev),
                spec(wv, 0, rev)]

    def dir_out_specs(rev):
        return [spec(wk, 0, rev), spec(wk, 0, rev), spec(wv, 0, rev), spec(wk, 0, rev)]

    shapes = [jax.ShapeDtypeStruct((n_rows, wk), out_dtype), jax.ShapeDtypeStruct((n_rows, wk), out_dtype),
              jax.ShapeDtypeStruct((n_rows, wv), out_dtype), jax.ShapeDtypeStruct((n_rows, wk), F32)]
    outs = pl.pallas_call(
        body, name=name, grid=(n_steps,), in_specs=dir_specs(k_f, lf_f, False) + dir_specs(k_b, lf_b, True),
        out_specs=dir_out_specs(False) + dir_out_specs(True), out_shape=shapes + shapes,
        scratch_shapes=[pltpu.VMEM((2, n_heads, dv, dk), F32)],
        compiler_params=_params(("arbitrary",)),
    )(q[0], k_f[0], v[0], lf_f[0], st_f, do, q[0], k_b[0], v[0], lf_b[0], st_b, do)
    return outs[:4], outs[4:]


def _row2(v):
    return v.reshape(1, -1)


def _mlp_fwd(tag, h, gain, w1, w2):
    y = _rowcall(f"{tag}_norm", _rmsnorm_fn, [(h, h.shape[1], 0)], [gain], [(h.shape[1], BF16)], tm=512)[0]
    hid = _mm(f"{tag}_up", y, w1, out_dtypes=(BF16,))
    h_out = _mm(f"{tag}_down", hid, w2, a_pro=_relu2, extras=(h,), epi=_add_epi, tk=2048)
    return h_out, (y, hid)


def _dw(name, a, b, **kw):
    return _mm(name, a, b, mode="tn", epi=lambda acc: (acc, acc), out_dtypes=(F32, BF16), **kw)


def _mlp_bwd(tag, h, gain, w1, w2, saved, dh_out):
    y, hid = saved
    dhid = _mm(f"{tag}_dact", dh_out[1], w2, mode="nt", extras=(hid,), epi=_relu2_bwd_epi, out_dtypes=(BF16,))
    dw2 = _dw(f"{tag}_dw2", hid, dh_out[1], a_pro=_relu2, tk=2048)
    dw1 = _dw(f"{tag}_dw1", y, dhid, out_split=N_CHIPS, tk=4096)
    dh, dgain = _dy_norm_bwd(f"{tag}_dy", dhid, w1, h, gain, dh_out[0])
    return dh, dgain, dw1, dw2


def _dy_norm_bwd(name, dz, w, h, gain, dres, pin=None, twice=True, **tiles):
    n_out = 2 if twice else 1

    def epi(dy, h_tile, dres_tile, gain_row):
        _, vjp = jax.vjp(lambda u, v: _rmsnorm_fn(u, v)[0], h_tile, gain_row)
        dh, dgain = vjp(dy)
        return (dh + dres_tile,) * n_out + (dgain,)

    assert h.shape[1] <= 1024
    tiles.setdefault("tm", 1024)
    *dh, dgain_parts = _mm(name, dz, w, mode="nt", extras=(h, dres), epi=epi, epi_pars=(gain,), row_sum=True,
                           out_dtypes=(F32, BF16)[:n_out], pin=pin, **tiles)
    return dh, jnp.sum(dgain_parts, axis=0)


def _local_step(x, target, w, pin=None, late=None, emit=None):
    g = {}
    d_model = x.shape[1]
    rg_w = hg_w = d_model // 2
    pins = []

    def send_off(tag, pairs):
        if emit is not None:
            pins.append(emit(tag, [p[0] for p in pairs], [p[1] for p in pairs]))

    def both(fn, pair):
        return [fn(t) for t in pair]

    def chip_major(t):
        return t.reshape(N_CHIPS, t.shape[0] // N_CHIPS, t.shape[1])

    h_a0 = x
    gain = _row2(w["norm_mix"][0])
    y0 = _rowcall("l0_norm", _rmsnorm_fn, [(h_a0, d_model, 0)], [gain], [(d_model, BF16)], tm=512, pin=pin)[0]
    proj0 = _mm("l0_in", y0, w["ab_w_in"])
    conv_w, conv_b = w["rg_conv_w"], _row2(w["rg_conv_b"])
    xc = _conv_fwd("rg_conv", proj0, 0, conv_w, conv_b)
    gate_pars = [w["rg_wa_bd"], w["rg_wx_bd"], w["rg_b_a"], w["rg_b_x"], w["rg_lambda"]]
    a_f, u_f, a_b, u_b = _rowcall("rg_gates", _rg_gates_fn, [(xc, rg_w, 0)], gate_pars, [(rg_w, F32)] * 4)
    hs_f, hs_b = _scan_fwd("rg_scan", a_f, u_f, a_b, u_b)
    hg_rows = [(proj0, hg_w, 2), (proj0, hg_w, 3), (proj0, hg_w, 4)]
    qh, k_f, lf_f, k_b, lf_b = _rowcall("hg_pre", _hg_pre_fn, hg_rows, [w["hg_lb_logits"]], [(hg_w, F32)] * 5)
    iv = (proj0, 5)
    o_f, o_b, st_f, st_b = _attn_fwd("hg_attn", (qh, 0), (k_f, 0), (k_b, 0), iv, (lf_f, 0), (lf_b, 0), 4, 128, 128)
    post0_rows = [(hs_f, rg_w, 0), (hs_b, rg_w, 0), (proj0, rg_w, 1), (o_f, hg_w, 0), (o_b, hg_w, 0), (proj0, hg_w, 6)]
    hg_gain = _row2(w["hg_norm"])
    mix_in0 = _rowcall("l0_post", _post0_fwd_fn, post0_rows, [hg_gain], [(d_model, BF16)])[0]
    if late is not None:
        w = {**w, **late(mix_in0)}
    h_b0 = _mm("l0_out", mix_in0, w["ab_w_out"], extras=(h_a0,), epi=_add_epi)
    h_c0, mlp0 = _mlp_fwd("mlp0", h_b0, _row2(w["norm_mlp"][0]), w["mlp_w1"][0], w["mlp_w2"][0])

    h_a1 = h_c0
    gain1 = _row2(w["norm_mix"][1])
    y1 = _rowcall("l1_norm", _rmsnorm_fn, [(h_a1, d_model, 0)], [gain1], [(d_model, BF16)], tm=512)[0]
    proj1 = _mm("l1_in", y1, w["gla_w_in_pad"], tm=512, tn=GLA_IN_PAD)
    gla_pars = [w["gla_w_up_pad"], w["gla_b_gate"]]
    gq, glf_f, glf_b = _rowcall("gla_pre", _gla_pre_fn, [(proj1, 512, 0), (proj1, LANES, 24)], gla_pars, [(512, F32)] * 3)
    gk, gv = (proj1, 1), (proj1, 1)
    go_f, go_b, gst_f, gst_b = _attn_fwd("gla_attn", (gq, 0), gk, gk, gv, (glf_f, 0), (glf_b, 0), 4, 128, 256)
    gla_gain = _row2(w["gla_norm"])
    post1_rows = [(go_f, d_model, 0), (go_b, d_model, 0), (proj1, d_model, 2)]
    mix_in1 = _rowcall("l1_post", _gla_post_fwd_fn, post1_rows, [gla_gain], [(d_model, BF16)])[0]
    h_b1 = _mm("l1_out", mix_in1, w["gla_w_out"], extras=(h_a1,), epi=_add_epi)
    h_c1, mlp1 = _mlp_fwd("mlp1", h_b1, _row2(w["norm_mlp"][1]), w["mlp_w1"][1], w["mlp_w2"][1])

    *dh, loss, g["norm_final"] = _rowcall(
        "loss_head", _loss_head_fn, [(h_c1, d_model, 0), (target, d_model, 0)], [_row2(w["norm_final"])],
        [(d_model, F32), (d_model, BF16)], [(1, LANES), (1, d_model)], tm=512)

    dh, g_nmlp1, g_w1_1, g_w2_1 = _mlp_bwd("mlp1", h_b1, _row2(w["norm_mlp"][1]), w["mlp_w1"][1], w["mlp_w2"][1], mlp1, dh)
    send_off("mlp1", [g_w1_1, both(chip_major, g_w2_1)])
    dmix1 = _mm("l1_dout", dh[1], w["gla_w_out"], mode="nt")
    g_gla_out = _dw("l1_dwout", mix_in1, dh[1])
    g["gla_w_out"] = g_gla_out[0]
    dgo, dr, g["gla_norm"] = _rowcall(
        "l1_dpost", _gla_post_bwd_fn, post1_rows + [(dmix1, d_model, 0)], [gla_gain],
        [(d_model, F32), (d_model, BF16)], [(1, d_model)], pin=pins.pop() if pins else None)
    (dq_f, dk_f, dv_f, dlf_f), (dq_b, dk_b, dv_b, dlf_b) = _attn_bwd(
        "gla_dattn", (gq, 0), gk, gk, gv, (glf_f, 0), (glf_b, 0), gst_f, gst_b, dgo, 4, 128, 256)

    def gla_pre_bwd(q, lr, dq1, dq2, dlf1, dlf2, dk1, dk2, dv1, dv2, w_up, b_gate):
        dlr = jnp.zeros_like(lr)
        dws, dbs = [], []
        for d, dlf in enumerate((dlf1, dlf2)):
            z = _raw_nn(lr, w_up[d]) + b_gate[d:d + 1]
            dz = dlf * _sigmoid(-z) * (1.0 / 16.0)
            dlr = dlr + _raw_nt(dz, w_up[d])
            dws.append(_raw_tn(dz, lr))
            dbs.append(jnp.sum(dz, axis=0, keepdims=True))
        return ((dq1 + dq2) * (128.0 ** -0.5), dk1 + dk2, dv1 + dv2, dlr, dws[0], dws[1], dbs[0], dbs[1])

    rows = [(proj1, 512, 0), (proj1, LANES, 24), (dq_f, 512, 0), (dq_b, 512, 0), (dlf_f, 512, 0), (dlf_b, 512, 0),
            (dk_f, 512, 0), (dk_b, 512, 0), (dv_f, d_model, 0), (dv_b, d_model, 0)]
    dq, dk, dv, dlr, dwt_f, dwt_b, db_f, db_b = _rowcall(
        "gla_dpre", gla_pre_bwd, rows, gla_pars, [(512, BF16), (512, BF16), (d_model, BF16), (LANES, BF16)],
        [(512, LANES), (512, LANES), (1, 512), (1, 512)])
    g["gla_w_up_pad"] = jnp.stack([dwt_f.T, dwt_b.T])
    g["gla_b_gate"] = jnp.concatenate([db_f, db_b], axis=0)
    dproj1 = jnp.concatenate([dq, dk, dv, dr, dlr], axis=1)
    g_gla_in = both(lambda t: _split_chips(t[:, :GLA_IN_WIDTH], 1), _dw("l1_dwin", y1, dproj1, tn=640, tk=4096))
    g["gla_w_in"] = g_gla_in[0]
    send_off("gla", [g_gla_in, both(chip_major, g_gla_out)])
    dh, g_nmix1 = _dy_norm_bwd("l1_dy", dproj1, w["gla_w_in_pad"], h_a1, gain1, dh[0],
                               pin=pins.pop() if pins else None, tk=GLA_IN_PAD)

    dh, g_nmlp0, g_w1_0, g_w2_0 = _mlp_bwd("mlp0", h_b0, _row2(w["norm_mlp"][0]), w["mlp_w1"][0], w["mlp_w2"][0], mlp0, dh)
    g_ab_out = _dw("l0_dwout", mix_in0, dh[1])
    g["ab_w_out"] = g_ab_out[0]
    send_off("mlp0", [g_w1_0, both(chip_major, g_w2_0), both(chip_major, g_ab_out)])
    dmix0 = _mm("l0_dout", dh[1], w["ab_w_out"], mode="nt")
    dhs, dga, do, dg, g["hg_norm"] = _rowcall(
        "l0_dpost", _post0_bwd_fn, post0_rows + [(dmix0, d_model, 0)], [hg_gain],
        [(rg_w, F32), (rg_w, BF16), (hg_w, F32), (hg_w, BF16)], [(1, hg_w)], pin=pins.pop() if pins else None)
    (dqh_f, dk_f, div_f, dlf_f), (dqh_b, dk_b, div_b, dlf_b) = _attn_bwd(
        "hg_dattn", (qh, 0), (k_f, 0), (k_b, 0), iv, (lf_f, 0), (lf_b, 0), st_f, st_b, do, 4, 128, 128)

    def hg_pre_bwd(q, f_f, f_b, dq1, dq2, dk1, dlf1, dk2, dlf2, dv1, dv2, logits):
        _, vjp = jax.vjp(_hg_pre_fn, q, f_f, f_b, logits)
        dq, df_f, df_b, dlogits = vjp((dq1 + dq2, dk1, dlf1, dk2, dlf2))
        return dq, df_f, df_b, dv1 + dv2, dlogits

    rows = hg_rows + [(t, hg_w, 0) for t in (dqh_f, dqh_b, dk_f, dlf_f, dk_b, dlf_b, div_f, div_b)]
    dq, df_f, df_b, div, g["hg_lb_logits"] = _rowcall(
        "hg_dpre", hg_pre_bwd, rows, [w["hg_lb_logits"]], [(hg_w, BF16)] * 4, [(2, hg_w)])
    du_f, da_f, du_b, da_b = _scan_bwd("rg_dscan", a_f, hs_f, a_b, hs_b, dhs)
    gates_bwd = _vjp_of(_rg_gates_fn, 1, 4, 5)
    rows = [(xc, rg_w, 0), (da_f, rg_w, 0), (du_f, rg_w, 0), (da_b, rg_w, 0), (du_b, rg_w, 0)]
    dxc, g["rg_wa_bd"], g["rg_wx_bd"], g["rg_b_a"], g["rg_b_x"], g["rg_lambda"] = _rowcall(
        "rg_dgates", gates_bwd, rows, gate_pars, [(rg_w, F32)],
        [(2, rg_w, rg_w), (2, rg_w, rg_w), (2, rg_w), (2, rg_w), (2, rg_w)])
    dxa, g["rg_conv_w"], g["rg_conv_b"] = _conv_bwd("rg_dconv", proj0, 0, conv_w, dxc)
    dproj0 = jnp.concatenate([dxa, dga, dq, df_f, df_b, div, dg], axis=1)
    g_ab_in = _dw("l0_dwin", y0, dproj0, out_split=N_CHIPS, tk=4096)
    g["ab_w_in"] = g_ab_in[0]
    send_off("ab", [g_ab_in])
    (grad_x,), g_nmix0 = _dy_norm_bwd("l0_dy", dproj0, w["ab_w_in"], h_a0, gain, dh[0],
                                      pin=pins.pop() if pins else None, twice=False)

    g["norm_mix"] = jnp.concatenate([g_nmix0, g_nmix1], axis=0)
    g["norm_mlp"] = jnp.concatenate([g_nmlp0, g_nmlp1], axis=0)
    g["mlp_w1"] = [g_w1_0[0], g_w1_1[0]]
    g["mlp_w2"] = [g_w2_0[0], g_w2_1[0]]
    return loss, grad_x, g


def _block_diag(w):
    d, g, n, _ = w.shape
    eye = jnp.eye(g, dtype=w.dtype)
    return (w[:, :, :, None, :] * eye[None, :, None, :, None]).reshape(d, g * n, g * n)


def _block_diag_extract(wbd, g):
    d, gn, _ = wbd.shape
    n = gn // g
    blocks = wbd.reshape(d, g, n, g, n)
    return jnp.stack([blocks[:, i, :, i, :] for i in range(g)], axis=1)


def _prepare_weights(big, full):
    w = {k: full[k] for k in ("norm_mix", "norm_mlp", "norm_final", "hg_lb_logits")}
    for k in ("rg_conv_w", "rg_conv_b", "rg_b_a", "rg_b_x", "rg_lambda", "hg_norm", "gla_b_gate", "gla_norm"):
        w[k] = full[k][0]
    w["rg_wa_bd"] = _block_diag(full["rg_w_a"][0])
    w["rg_wx_bd"] = _block_diag(full["rg_w_x"][0])
    up = full["gla_w_gate_up"][0]
    rank = up.shape[1]
    pad = jnp.zeros((2, LANES, up.shape[2]), F32)
    w["gla_w_up_pad"] = pad.at[0, 0:rank].set(up[0]).at[1, rank:2 * rank].set(up[1])
    w.update(_prepare_matrices(big))
    return w


def _prepare_matrices(big):
    w = {}
    if "mlp_w1" in big:
        w["mlp_w1"] = list(big["mlp_w1"])
        w["mlp_w2"] = [t.reshape(-1, t.shape[-1]) for t in big["mlp_w2"]]
    if "ab_w_in" in big:
        w["ab_w_in"] = big["ab_w_in"]
    if "ab_w_out" in big:
        w["ab_w_out"] = big["ab_w_out"].reshape(-1, big["ab_w_out"].shape[-1])
    if "gla_w_in" in big:
        w["gla_w_out"] = big["gla_w_out"].reshape(-1, big["gla_w_out"].shape[-1])
        gla_in = _join_chips(big["gla_w_in"], 1)
        w["gla_w_in_pad"] = jnp.pad(gla_in, ((0, 0), (0, GLA_IN_PAD - gla_in.shape[1])))
    return w


def _finish_grads(g, rank=16, rg_blocks=8):
    def chip_major(t):
        return t.reshape(N_CHIPS, t.shape[0] // N_CHIPS, t.shape[1])

    big = {
        "mlp_w1": list(g["mlp_w1"]), "mlp_w2": [chip_major(t) for t in g["mlp_w2"]],
        "ab_w_in": g["ab_w_in"], "ab_w_out": chip_major(g["ab_w_out"]),
        "gla_w_in": g["gla_w_in"], "gla_w_out": chip_major(g["gla_w_out"]),
    }
    small = {
        "norm_mix": g["norm_mix"], "norm_mlp": g["norm_mlp"], "norm_final": g["norm_final"][0],
        "rg_conv_w": g["rg_conv_w"][None], "rg_conv_b": g["rg_conv_b"],
        "rg_w_a": _block_diag_extract(g["rg_wa_bd"], rg_blocks)[None], "rg_b_a": g["rg_b_a"][None],
        "rg_w_x": _block_diag_extract(g["rg_wx_bd"], rg_blocks)[None], "rg_b_x": g["rg_b_x"][None],
        "rg_lambda": g["rg_lambda"][None], "hg_lb_logits": g["hg_lb_logits"], "hg_norm": g["hg_norm"],
        "gla_w_gate_up": jnp.stack([g["gla_w_up_pad"][0, 0:rank], g["gla_w_up_pad"][1, rank:2 * rank]])[None],
        "gla_b_gate": g["gla_b_gate"][None], "gla_norm": g["gla_norm"],
    }
    return big, small


MATRICES = (("mlp_w1", 0), ("mlp_w1", 1), ("mlp_w2", 0), ("mlp_w2", 1), ("ab_w_in", 0), ("ab_w_out", 0),
            ("gla_w_in", 0), ("gla_w_out", 0))
EARLY_MATRICES = ("ab_w_in",)
SMALL_SHARDED = ("rg_conv_w", "rg_b_a", "rg_b_x", "rg_lambda", "gla_w_gate_up", "gla_b_gate", "gla_norm")
SMALL_REPLICATED = ("norm_mix", "norm_mlp", "norm_final", "rg_conv_b", "rg_w_a", "rg_w_x", "hg_lb_logits", "hg_norm")
WEIGHTS = ("norm_mix", "norm_mlp", "norm_final", "mlp_w1", "mlp_w2", "ab_w_in", "ab_w_out", "rg_conv_w", "rg_conv_b",
           "rg_w_a", "rg_b_a", "rg_w_x", "rg_b_x", "rg_lambda", "hg_lb_logits", "hg_norm", "gla_w_in", "gla_w_out",
           "gla_w_gate_up", "gla_b_gate", "gla_norm")
ROW_ALIGN = 16


def _pack(arrays, lead=0):
    head = arrays[0].shape[:lead]
    flat = jnp.concatenate([a.reshape(head + (-1,)) for a in arrays], axis=lead)
    n = flat.shape[-1]
    quantum = LANES * ROW_ALIGN
    padded = -(-n // quantum) * quantum
    if padded != n:
        flat = jnp.pad(flat, [(0, 0)] * lead + [(0, padded - n)])
    return flat.reshape(head + (padded // LANES, LANES))


def _unpack(buf, shapes, lead=0):
    head = buf.shape[:lead]
    flat = buf.reshape(head + (-1,))
    out, off = [], 0
    for s in shapes:
        n = 1
        for v in s:
            n *= v
        out.append(lax.slice_in_dim(flat, off, off + n, axis=lead).reshape(head + tuple(s)))
        off += n
    return out


def _join_chips(gathered, axis):
    t = jnp.moveaxis(gathered, 0, axis)
    return t.reshape(t.shape[:axis] + (t.shape[axis] * t.shape[axis + 1],) + t.shape[axis + 2:])


def _split_chips(full, axis):
    s = full.shape
    t = full.reshape(s[:axis] + (N_CHIPS, s[axis] // N_CHIPS) + s[axis + 1:])
    return jnp.moveaxis(t, axis, 0)


_ANY = pl.BlockSpec(memory_space=pl.ANY)


def _place():
    return lax.axis_index("x"), lax.axis_index("y"), lax.axis_index("c")


def _into_slot(name, src, slot, n_slots, dtype, tm, layer=None):
    r, lanes = src.shape[-2:]
    tm = _row_tile(r, tm, ROW_ALIGN)

    def body(slot_ref, in_ref, o_ref):
        o_ref[...] = in_ref[...].astype(o_ref.dtype)

    if layer is None:
        in_spec = pl.BlockSpec((tm, lanes), lambda i, slot_ref: (i, 0))
    else:
        in_spec = pl.BlockSpec((None, tm, lanes), lambda i, slot_ref: (layer, i, 0))
    grid_spec = pltpu.PrefetchScalarGridSpec(
        num_scalar_prefetch=1, grid=(r // tm,), in_specs=[in_spec],
        out_specs=pl.BlockSpec((None, tm, lanes), lambda i, slot_ref: (slot_ref[0], i, 0)))
    return pl.pallas_call(
        body, name=name, grid_spec=grid_spec, out_shape=jax.ShapeDtypeStruct((n_slots, r, lanes), dtype),
        compiler_params=_params(("parallel",)),
    )(slot.reshape(1).astype(jnp.int32), src)


def _chip_peers():
    x, y, c = _place()
    return 2 * x + y, c, [(1 - x, y), (x, 1 - y), (1 - x, 1 - y)]


def _comm_call(name, body, ins, out_shapes, n_sems, aliases=None):
    return pl.pallas_call(
        body, name=name, in_specs=[_ANY] * len(ins), out_specs=[_ANY] * len(out_shapes), out_shape=out_shapes,
        input_output_aliases=aliases or {},
        scratch_shapes=[pltpu.SemaphoreType.DMA((n_sems,)), pltpu.SemaphoreType.DMA((n_sems,))],
    )(*ins)


def _gather_chips(name, bufs):
    n = len(bufs)

    def body(*refs):
        outs, send_sems, recv_sems = refs[n:2 * n], refs[2 * n], refs[2 * n + 1]
        x, y, c = _place()
        me, _, peers = _chip_peers()

        def rows(a, block, half):
            rh = outs[a].shape[1] // 2
            return outs[a].at[block, pl.ds(half * rh, rh)]

        def copy(a, j, block, half, to, sem):
            return pltpu.make_async_remote_copy(
                src_ref=rows(a, block, half), dst_ref=rows(a, block, half), send_sem=send_sems.at[sem],
                recv_sem=recv_sems.at[sem], device_id=to, device_id_type=MESH)

        def over_ici(a, j, block):
            px, py = peers[j]
            return copy(a, j, block, c, (px, py, c), 6 * a + j)

        def to_sibling(a, j, block, half):
            return copy(a, j, block, half, (x, y, 1 - c), 6 * a + 3 + j)

        sends = [over_ici(a, j, me) for a in range(n) for j in range(3)]
        for cp in sends:
            cp.start()
        for a in range(n):
            for j, (px, py) in enumerate(peers):
                over_ici(a, j, 2 * px + py).wait_recv()
                handed = to_sibling(a, j, 2 * px + py, c)
                handed.start()
                sends.append(handed)
        for a in range(n):
            for j, (px, py) in enumerate(peers):
                to_sibling(a, j, 2 * px + py, 1 - c).wait_recv()
        for cp in sends:
            cp.wait_send()

    shapes = [jax.ShapeDtypeStruct(b.shape, b.dtype) for b in bufs]
    return _comm_call(name, body, bufs, shapes, 6 * n, {a: a for a in range(n)})


_HBM = pl.BlockSpec(memory_space=pltpu.HBM)
_SEM = pl.BlockSpec(memory_space=pltpu.SEMAPHORE)
_EFFECT = pltpu.SideEffectType.DATAFLOW_SIDE_EFFECTING


def _half_rows(ref, block, half):
    rh = ref.shape[1] // 2
    return ref.at[block, pl.ds(half * rh, rh)]


def _gather_start(name, bufs, after):
    n = len(bufs)

    def body(*refs):
        ins, send_sems, recv_sems, token = refs[:n], refs[n + 1], refs[n + 2], refs[-1]
        me, c, peers = _chip_peers()
        for a in range(n):
            mine = _half_rows(ins[a], me, c)
            for j, (px, py) in enumerate(peers):
                pltpu.make_async_remote_copy(
                    src_ref=mine, dst_ref=mine, send_sem=send_sems.at[3 * a + j], recv_sem=recv_sems.at[3 * a + j],
                    device_id=(px, py, c), device_id_type=MESH).start()
        token[...] = jnp.zeros_like(token)

    out_shape = (pltpu.SemaphoreType.DMA((3 * n,)), pltpu.SemaphoreType.DMA((3 * n,)),
                 *[pltpu.HBM(b.shape, b.dtype) for b in bufs], jax.ShapeDtypeStruct((8, LANES), F32))
    return pl.pallas_call(
        body, name=name, out_shape=out_shape, in_specs=[_HBM] * n + [_ANY],
        out_specs=(_SEM, _SEM, *[_HBM] * n, pl.BlockSpec(memory_space=pltpu.VMEM)),
        input_output_aliases={a: 2 + a for a in range(n)},
        compiler_params=pltpu.CompilerParams(has_side_effects=_EFFECT),
    )(*[pltpu.with_memory_space_constraint(b, pltpu.HBM) for b in bufs], after)


def _gather_wait(name, bufs, send_sems, recv_sems, after):
    n = len(bufs)

    def body(*refs):
        ins, send_sems, recv_sems = refs[:n], refs[n], refs[n + 1]
        me, c, peers = _chip_peers()
        for a in range(n):
            for j, (px, py) in enumerate(peers):
                copy = pltpu.make_async_remote_copy(
                    src_ref=_half_rows(ins[a], me, c), dst_ref=_half_rows(ins[a], 2 * px + py, c),
                    send_sem=send_sems.at[3 * a + j], recv_sem=recv_sems.at[3 * a + j],
                    device_id=(px, py, c), device_id_type=MESH)
                copy.wait_send()
                copy.wait_recv()

    return pl.pallas_call(
        body, name=name, out_shape=tuple(pltpu.HBM(b.shape, b.dtype) for b in bufs),
        in_specs=[_HBM] * n + [_SEM, _SEM, _ANY], out_specs=tuple([_HBM] * n),
        input_output_aliases={a: a for a in range(n)},
        compiler_params=pltpu.CompilerParams(has_side_effects=_EFFECT),
    )(*bufs, send_sems, recv_sems, after)


def _hand_over(name, bufs):
    n = len(bufs)

    def body(*refs):
        outs, send_sems, recv_sems = refs[n:2 * n], refs[2 * n], refs[2 * n + 1]
        x, y, c = _place()
        _, _, peers = _chip_peers()

        def copy(a, j, half):
            px, py = peers[j]
            rows = _half_rows(outs[a], 2 * px + py, half)
            return pltpu.make_async_remote_copy(
                src_ref=rows, dst_ref=rows, send_sem=send_sems.at[3 * a + j], recv_sem=recv_sems.at[3 * a + j],
                device_id=(x, y, 1 - c), device_id_type=MESH)

        sends = [copy(a, j, c) for a in range(n) for j in range(3)]
        for cp in sends:
            cp.start()
        for a in range(n):
            for j in range(3):
                copy(a, j, 1 - c).wait_recv()
        for cp in sends:
            cp.wait_send()

    shapes = [jax.ShapeDtypeStruct(b.shape, b.dtype) for b in bufs]
    return _comm_call(name, body, bufs, shapes, 3 * n, {a: a for a in range(n)})


def _pair_gather(name, bufs):
    n = len(bufs)

    def body(*refs):
        ins, outs, send_sems, recv_sems = refs[:n], refs[n:2 * n], refs[2 * n], refs[2 * n + 1]
        x, y, c = _place()

        def copy(a, block):
            return pltpu.make_async_remote_copy(
                src_ref=ins[a].at[block], dst_ref=outs[a].at[block], send_sem=send_sems.at[a],
                recv_sem=recv_sems.at[a], device_id=(x, y, 1 - c), device_id_type=MESH)

        sends = [copy(a, c) for a in range(n)]
        for cp in sends:
            cp.start()
        for a in range(n):
            copy(a, 1 - c).wait_recv()
        for cp in sends:
            cp.wait_send()

    shapes = [jax.ShapeDtypeStruct(b.shape, b.dtype) for b in bufs]
    return _comm_call(name, body, bufs, shapes, n, {a: a for a in range(n)})


def _all_peers():
    x, y, c = _place()
    peers = []
    for mask in range(1, N_DEV):
        fx, fy, fc = (mask >> 2) & 1, (mask >> 1) & 1, mask & 1
        peers.append((jnp.where(fx, 1 - x, x), jnp.where(fy, 1 - y, y), jnp.where(fc, 1 - c, c)))
    return 4 * x + 2 * y + c, peers


def _reduce_copies(srcs, lands, send_sems, recv_sems):
    me, peers = _all_peers()
    sends, arrivals = [], []
    for a in range(len(srcs)):
        for j, (px, py, pc) in enumerate(peers):
            k = (N_DEV - 1) * a + j
            sends.append(pltpu.make_async_remote_copy(
                src_ref=srcs[a].at[2 * px + py, pc], dst_ref=lands[a].at[me], send_sem=send_sems.at[k],
                recv_sem=recv_sems.at[k], device_id=(px, py, pc), device_id_type=MESH))
            arrivals.append(pltpu.make_async_remote_copy(
                src_ref=srcs[a].at[2 * px + py, pc], dst_ref=lands[a].at[4 * px + 2 * py + pc],
                send_sem=send_sems.at[k], recv_sem=recv_sems.at[k], device_id=(px, py, pc), device_id_type=MESH))
    return sends, arrivals


def _reduce_direct(name, srcs, pin=None):
    n = len(srcs)
    extra = [] if pin is None else [pin]

    def body(*refs):
        ins, outs = refs[:n], refs[n + len(extra):2 * n + len(extra)]
        sends, arrivals = _reduce_copies(ins, outs, refs[-2], refs[-1])
        for cp in sends:
            cp.start()
        for cp in arrivals:
            cp.wait_recv()
        for cp in sends:
            cp.wait_send()

    shapes = [jax.ShapeDtypeStruct((N_DEV,) + s.shape[2:], s.dtype) for s in srcs]
    return _comm_call(name, body, list(srcs) + extra, shapes, (N_DEV - 1) * n)


def _reduce_start(name, srcs):
    n = len(srcs)
    lands = [lax.empty((N_DEV,) + s.shape[2:], s.dtype) for s in srcs]

    def body(*refs):
        sends, _ = _reduce_copies(refs[:n], refs[n:2 * n], refs[2 * n], refs[2 * n + 1])
        for cp in sends:
            cp.start()
        refs[-1][...] = jnp.zeros_like(refs[-1])

    bufs = list(srcs) + lands
    n_sems = (N_DEV - 1) * n
    out_shape = (pltpu.SemaphoreType.DMA((n_sems,)), pltpu.SemaphoreType.DMA((n_sems,)),
                 *[pltpu.HBM(b.shape, b.dtype) for b in bufs], jax.ShapeDtypeStruct((8, LANES), F32))
    return pl.pallas_call(
        body, name=name, out_shape=out_shape, in_specs=[_HBM] * (2 * n),
        out_specs=(_SEM, _SEM, *[_HBM] * (2 * n), pl.BlockSpec(memory_space=pltpu.VMEM)),
        input_output_aliases={a: 2 + a for a in range(2 * n)},
        compiler_params=pltpu.CompilerParams(has_side_effects=_EFFECT),
    )(*[pltpu.with_memory_space_constraint(b, pltpu.HBM) for b in bufs])


def _reduce_wait(name, srcs, lands, send_sems, recv_sems, after):
    n = len(srcs)

    def body(*refs):
        sends, arrivals = _reduce_copies(refs[:n], refs[n:2 * n], refs[2 * n], refs[2 * n + 1])
        for cp in sends:
            cp.wait_send()
        for cp in arrivals:
            cp.wait_recv()

    bufs = list(srcs) + list(lands)
    outs = pl.pallas_call(
        body, name=name, out_shape=tuple(pltpu.HBM(b.shape, b.dtype) for b in bufs),
        in_specs=[_HBM] * (2 * n) + [_SEM, _SEM, _ANY], out_specs=tuple([_HBM] * (2 * n)),
        input_output_aliases={a: a for a in range(2 * n)},
        compiler_params=pltpu.CompilerParams(has_side_effects=_EFFECT),
    )(*bufs, send_sems, recv_sems, after)
    return list(outs[n:])


def _reduce_sum(name, own, land, chip, core):
    n, rh, lanes = land.shape
    tm = _row_tile(rh, 1024, ROW_ALIGN)

    def body(idx_ref, own_ref, *rest):
        total = own_ref[...]
        for g_ref in rest[:-1]:
            total = total + g_ref[...].astype(F32)
        rest[-1][...] = total

    def block(k):
        return pl.BlockSpec((None, tm, lanes), lambda i, idx_ref: ((2 * idx_ref[0] + idx_ref[1] + k) % n, i, 0))

    grid_spec = pltpu.PrefetchScalarGridSpec(
        num_scalar_prefetch=1, grid=(rh // tm,),
        in_specs=[pl.BlockSpec((None, None, tm, lanes), lambda i, idx_ref: (idx_ref[0], idx_ref[1], i, 0))]
        + [block(k) for k in range(1, n)],
        out_specs=pl.BlockSpec((None, tm, lanes), lambda i, idx_ref: (idx_ref[1], i, 0)))
    return pl.pallas_call(
        body, name=name, grid_spec=grid_spec, out_shape=jax.ShapeDtypeStruct((2, rh, lanes), F32),
        compiler_params=_params(("parallel",)),
    )(jnp.stack([chip, core]).astype(jnp.int32), own, *[land] * (n - 1))


def _gather_all_start(name, buf):
    def body(in_ref, send_sems, recv_sems, out_ref, token):
        me, peers = _all_peers()
        for j, peer in enumerate(peers):
            pltpu.make_async_remote_copy(
                src_ref=in_ref.at[me], dst_ref=in_ref.at[me], send_sem=send_sems.at[j], recv_sem=recv_sems.at[j],
                device_id=peer, device_id_type=MESH).start()
        token[...] = jnp.zeros_like(token)

    n = N_DEV - 1
    return pl.pallas_call(
        body, name=name, in_specs=[_HBM],
        out_shape=(pltpu.SemaphoreType.DMA((n,)), pltpu.SemaphoreType.DMA((n,)), pltpu.HBM(buf.shape, buf.dtype),
                   jax.ShapeDtypeStruct((8, LANES), F32)),
        out_specs=(_SEM, _SEM, _HBM, pl.BlockSpec(memory_space=pltpu.VMEM)), input_output_aliases={0: 2},
        compiler_params=pltpu.CompilerParams(has_side_effects=_EFFECT),
    )(pltpu.with_memory_space_constraint(buf, pltpu.HBM))


def _gather_all_wait(name, buf, send_sems, recv_sems, after):
    def body(in_ref, send_sems, recv_sems, after_ref, out_ref):
        me, peers = _all_peers()
        for j, (px, py, pc) in enumerate(peers):
            copy = pltpu.make_async_remote_copy(
                src_ref=in_ref.at[me], dst_ref=in_ref.at[4 * px + 2 * py + pc], send_sem=send_sems.at[j],
                recv_sem=recv_sems.at[j], device_id=(px, py, pc), device_id_type=MESH)
            copy.wait_send()
            copy.wait_recv()

    return pl.pallas_call(
        body, name=name, in_specs=[_HBM, _SEM, _SEM, _ANY], out_shape=pltpu.HBM(buf.shape, buf.dtype),
        out_specs=_HBM, input_output_aliases={0: 0},
        compiler_params=pltpu.CompilerParams(has_side_effects=_EFFECT),
    )(buf, send_sems, recv_sems, after)


def _sum_blocks(name, stacked, tm):
    n, r, lanes = stacked.shape

    def body(in_ref, o_ref):
        acc = in_ref[0]
        for j in range(1, n):
            acc = acc + in_ref[j]
        o_ref[...] = acc

    return pl.pallas_call(
        body, name=name, grid=(r // tm,), in_specs=[pl.BlockSpec((n, tm, lanes), lambda i: (0, i, 0))],
        out_specs=pl.BlockSpec((tm, lanes), lambda i: (i, 0)), out_shape=jax.ShapeDtypeStruct((r, lanes), F32),
        compiler_params=_params(("parallel",)),
    )(stacked)


def _row_tile(rows, pref, align):
    best = None
    for t in range(align, min(rows, pref) + 1, align):
        if rows % t == 0:
            best = t
    assert best is not None, (rows, pref, align)
    return best


def _adam(name, w, g, m, v):
    rows, width = w.shape
    tm = _row_tile(rows, max(8, 4096 * LANES // width), 8)
    args = [(t, width, 0) for t in (w, g, m, v)]
    return _rowcall(name, _adam_fn, args, [], [(width, F32)] * 3, tm=tm)


def kernel(x, norm_mix, norm_mlp, norm_final, mlp_w1, mlp_w2, ab_w_in, ab_w_out, rg_conv_w, rg_conv_b, rg_w_a, rg_b_a, rg_w_x, rg_b_x, rg_lambda, hg_lb_logits, hg_norm, gla_w_in, gla_w_out, gla_w_gate_up, gla_b_gate, gla_norm, loss_target, m_norm_mix, m_norm_mlp, m_norm_final, m_mlp_w1, m_mlp_w2, m_ab_w_in, m_ab_w_out, m_rg_conv_w, m_rg_conv_b, m_rg_w_a, m_rg_b_a, m_rg_w_x, m_rg_b_x, m_rg_lambda, m_hg_lb_logits, m_hg_norm, m_gla_w_in, m_gla_w_out, m_gla_w_gate_up, m_gla_b_gate, m_gla_norm, v_norm_mix, v_norm_mlp, v_norm_final, v_mlp_w1, v_mlp_w2, v_ab_w_in, v_ab_w_out, v_rg_conv_w, v_rg_conv_b, v_rg_w_a, v_rg_b_a, v_rg_w_x, v_rg_b_x, v_rg_lambda, v_hg_lb_logits, v_hg_norm, v_gla_w_in, v_gla_w_out, v_gla_w_gate_up, v_gla_b_gate, v_gla_norm):
    w = dict(norm_mix=norm_mix, norm_mlp=norm_mlp, norm_final=norm_final, mlp_w1=mlp_w1, mlp_w2=mlp_w2, ab_w_in=ab_w_in, ab_w_out=ab_w_out, rg_conv_w=rg_conv_w, rg_conv_b=rg_conv_b, rg_w_a=rg_w_a, rg_b_a=rg_b_a, rg_w_x=rg_w_x, rg_b_x=rg_b_x, rg_lambda=rg_lambda, hg_lb_logits=hg_lb_logits, hg_norm=hg_norm, gla_w_in=gla_w_in, gla_w_out=gla_w_out, gla_w_gate_up=gla_w_gate_up, gla_b_gate=gla_b_gate, gla_norm=gla_norm)
    m = dict(norm_mix=m_norm_mix, norm_mlp=m_norm_mlp, norm_final=m_norm_final, mlp_w1=m_mlp_w1, mlp_w2=m_mlp_w2, ab_w_in=m_ab_w_in, ab_w_out=m_ab_w_out, rg_conv_w=m_rg_conv_w, rg_conv_b=m_rg_conv_b, rg_w_a=m_rg_w_a, rg_b_a=m_rg_b_a, rg_w_x=m_rg_w_x, rg_b_x=m_rg_b_x, rg_lambda=m_rg_lambda, hg_lb_logits=m_hg_lb_logits, hg_norm=m_hg_norm, gla_w_in=m_gla_w_in, gla_w_out=m_gla_w_out, gla_w_gate_up=m_gla_w_gate_up, gla_b_gate=m_gla_b_gate, gla_norm=m_gla_norm)
    v = dict(norm_mix=v_norm_mix, norm_mlp=v_norm_mlp, norm_final=v_norm_final, mlp_w1=v_mlp_w1, mlp_w2=v_mlp_w2, ab_w_in=v_ab_w_in, ab_w_out=v_ab_w_out, rg_conv_w=v_rg_conv_w, rg_conv_b=v_rg_conv_b, rg_w_a=v_rg_w_a, rg_b_a=v_rg_b_a, rg_w_x=v_rg_w_x, rg_b_x=v_rg_b_x, rg_lambda=v_rg_lambda, hg_lb_logits=v_hg_lb_logits, hg_norm=v_hg_norm, gla_w_in=v_gla_w_in, gla_w_out=v_gla_w_out, gla_w_gate_up=v_gla_w_gate_up, gla_b_gate=v_gla_b_gate, gla_norm=v_gla_norm)
    chip = 2 * lax.axis_index("x") + lax.axis_index("y")
    core = lax.axis_index("c")
    sharded_shapes = [w[n].shape for n in SMALL_SHARDED]

    slots = [_into_slot(f"cast_{n}{layer}", w[n], chip, N_CHIPS, BF16, 512, layer) for n, layer in MATRICES]
    early = [i for i, (n, _) in enumerate(MATRICES) if n in EARLY_MATRICES]
    rest = [i for i in range(len(MATRICES)) if i not in early]

    def named(indices, arrays):
        big = {}
        for i, t in zip(indices, arrays):
            big.setdefault(MATRICES[i][0], []).append(t)
        return {n: (v if n in ("mlp_w1", "mlp_w2") else v[0]) for n, v in big.items()}

    vectors = _pack([w[n] for n in SMALL_SHARDED])
    vectors = _into_slot("place_vectors", vectors, chip, N_CHIPS, F32, vectors.shape[0])
    *gathered, vectors = _gather_chips("gather_early", [slots[i] for i in early] + [vectors])
    send_sems, recv_sems, *in_flight, token = _gather_start("gather_rest_start", [slots[i] for i in rest], gathered[0])

    def late_weights(after):
        landed = _gather_wait("gather_rest_wait", in_flight, send_sems, recv_sems, after)
        return _prepare_matrices(named(rest, _hand_over("gather_rest_share", list(landed))))

    big = named(early, gathered)
    small_all = _unpack(vectors, sharded_shapes, lead=1)
    full = {n: w[n] for n in SMALL_REPLICATED}
    for n, t in zip(SMALL_SHARDED, small_all):
        full[n] = _join_chips(t, t.ndim - 2)

    def halves(t):
        return t.reshape(N_CHIPS, 2, t.shape[1] // 2, t.shape[2])

    in_flight_grads = {}

    def emit(tag, arrays32, arrays16):
        n = len(arrays16)
        send, recv, *rest = _reduce_start(f"reduce_{tag}_start", [halves(t) for t in arrays16])
        in_flight_grads[tag] = ([halves(t) for t in arrays32], rest[:n], rest[n:2 * n], send, recv)
        return rest[-1]

    loss_part, grad_x, g_kernel = _local_step(
        x[0], loss_target[0], _prepare_weights(big, full), token, late_weights, emit)
    g_big, g_full = _finish_grads(g_kernel)

    small_names = SMALL_REPLICATED + SMALL_SHARDED
    reduced_shapes = [g_full[n].shape for n in small_names] + [loss_part.shape]
    g_small = _pack([g_full[n] for n in small_names] + [loss_part])
    device = 2 * chip + core
    g_small = _into_slot("place_small", g_small, device, N_DEV, F32, g_small.shape[0])
    small_send, small_recv, small_in_flight, small_token = _gather_all_start("reduce_small_start", g_small)

    mine = {}
    for tag, (own, srcs, lands, send, recv) in in_flight_grads.items():
        landed = _reduce_wait(f"reduce_{tag}_wait", srcs, lands, send, recv, small_token)
        mine[tag] = [_reduce_sum(f"reduce_add_{tag}{i}", o, f, chip, core) for i, (o, f) in enumerate(zip(own, landed))]
    ordered = [mine["mlp0"][0], mine["mlp1"][0], mine["mlp0"][1], mine["mlp1"][1], mine["ab"][0], mine["mlp0"][2],
               *mine["gla"]]
    reduced = [t.reshape(2 * t.shape[1], t.shape[2]) for t in _pair_gather("reduce_share", ordered)]
    by_name = {n: [] for n, _ in MATRICES}
    for (n, _), t in zip(MATRICES, reduced):
        by_name[n].append(t)
    grads = {n: jnp.stack(v) for n, v in by_name.items()}

    g_small_all = _gather_all_wait("reduce_small_wait", small_in_flight, small_send, small_recv, reduced[0])
    g_small_red = _sum_blocks("reduce_small_add", g_small_all, g_small_all.shape[1])
    *small_red, loss_sum = _unpack(g_small_red, reduced_shapes)
    loss = loss_sum[0, 0]
    g_small_full = dict(zip(small_names, small_red))
    for n in SMALL_REPLICATED:
        grads[n] = g_small_full[n]
    for n in SMALL_SHARDED:
        width = w[n].shape[-1]
        grads[n] = lax.dynamic_slice_in_dim(g_small_full[n], chip * width, width, axis=g_small_full[n].ndim - 1)

    delta, new_m, new_v = {}, {}, {}
    for n in by_name:
        flat = [t.reshape(-1, t.shape[-1]) for t in (w[n], grads[n], m[n], v[n])]
        for dst, t in zip((delta, new_m, new_v), _adam(f"adam_{n}", *flat)):
            dst[n] = t.reshape(w[n].shape)
    small_shapes = [w[n].shape for n in small_names]
    packs = [_pack([src[n] for n in small_names]) for src in (w, grads, m, v)]
    d_small, m_small, v_small = _adam("adam_small", *packs)
    for dst, buf in ((delta, d_small), (new_m, m_small), (new_v, v_small)):
        dst.update(zip(small_names, _unpack(buf, small_shapes)))

    return (loss, grad_x[None], *[grads[n] for n in WEIGHTS], *[delta[n] for n in WEIGHTS],
            *[new_m[n] for n in WEIGHTS], *[new_v[n] for n in WEIGHTS])
```

```python
import functools

import jax
import jax.numpy as jnp
from jax import lax
from jax.experimental import pallas as pl
from jax.experimental.pallas import tpu as pltpu

F32 = jnp.float32
BF16 = jnp.bfloat16
MESH = pl.DeviceIdType.MESH

LANES = 128
CHUNK = 64
ATTN_SUB = 4
EPS = 1e-6
RG_C = 8.0
N_CHIPS = 4
N_DEV = 8
GLA_IN_WIDTH = 3104
GLA_IN_PAD = 3200
VMEM_LIMIT = 56 * 1024 * 1024

ADAM_LR = 0.001
ADAM_B1 = 0.9
ADAM_B2 = 0.999
ADAM_EPS = 1e-08
ADAM_WD = 0.01
ADAM_STEP = 10


def _raw_dot(a, b, ca, cb):
    return lax.dot_general(a.astype(BF16), b.astype(BF16), (((ca,), (cb,)), ((), ())),
                           preferred_element_type=F32)


def _raw_nn(a, b):
    return _raw_dot(a, b, 1, 0)


def _raw_nt(a, b):
    return _raw_dot(a, b, 1, 1)


def _raw_tn(a, b):
    return _raw_dot(a, b, 0, 0)


@jax.custom_vjp
def _dot_nn(a, b):
    return _raw_nn(a, b)


def _dot_nn_fwd(a, b):
    return _raw_nn(a, b), (a, b)


def _dot_nn_bwd(res, g):
    a, b = res
    return _raw_nt(g, b), _raw_tn(a, g)


_dot_nn.defvjp(_dot_nn_fwd, _dot_nn_bwd)


@jax.custom_vjp
def _dot_nt(a, b):
    return _raw_nt(a, b)


def _dot_nt_fwd(a, b):
    return _raw_nt(a, b), (a, b)


def _dot_nt_bwd(res, g):
    a, b = res
    return _raw_nn(g, b), _raw_tn(g, a)


_dot_nt.defvjp(_dot_nt_fwd, _dot_nt_bwd)


@jax.custom_vjp
def _dot_tn(a, b):
    return _raw_tn(a, b)


def _dot_tn_fwd(a, b):
    return _raw_tn(a, b), (a, b)


def _dot_tn_bwd(res, g):
    a, b = res
    return _raw_nt(b, g), _raw_nn(a, g)


_dot_tn.defvjp(_dot_tn_fwd, _dot_tn_bwd)


def _tile(n, pref):
    if n <= pref:
        return n
    t = (pref // LANES) * LANES
    while t > LANES and n % t:
        t -= LANES
    assert n % t == 0, (n, pref)
    return t


def _params(sem):
    return pltpu.CompilerParams(dimension_semantics=sem, vmem_limit_bytes=VMEM_LIMIT)


def _rowcall(name, fn, rows, pars, row_outs, par_outs=(), tm=512, pin=None):
    if pin is not None:
        inner, pars = fn, list(pars) + [pin]
        fn = lambda *vals: inner(*vals[:-1])
    n_rows = rows[0][0].shape[0]
    tm = min(tm, n_rows)
    assert n_rows % tm == 0
    n_r, n_p, n_ro = len(rows), len(pars), len(row_outs)

    def body(*refs):
        vals = [r[...].astype(F32) for r in refs[:n_r + n_p]]
        outs = fn(*vals)
        o_refs = refs[n_r + n_p:n_r + n_p + n_ro]
        po_refs = refs[n_r + n_p + n_ro:]
        for o_ref, val in zip(o_refs, outs[:n_ro]):
            o_ref[...] = val.astype(o_ref.dtype)
        first = pl.program_id(0) == 0
        for po_ref, val in zip(po_refs, outs[n_ro:]):
            @pl.when(first)
            def _():
                po_ref[...] = val

            @pl.when(jnp.logical_not(first))
            def _():
                po_ref[...] += val

    def const_map(nd):
        return lambda i: (0,) * nd

    def row_spec(w, cb):
        return pl.BlockSpec((tm, w), lambda i: (i, cb))

    in_specs = [row_spec(w, cb) for _, w, cb in rows]
    in_specs += [pl.BlockSpec(p.shape, const_map(p.ndim)) for p in pars]
    out_specs = [pl.BlockSpec((tm, w), lambda i: (i, 0)) for w, _ in row_outs]
    out_specs += [pl.BlockSpec(tuple(s), const_map(len(s))) for s in par_outs]
    out_shape = [jax.ShapeDtypeStruct((n_rows, w), dt) for w, dt in row_outs]
    out_shape += [jax.ShapeDtypeStruct(tuple(s), F32) for s in par_outs]
    return pl.pallas_call(
        body, name=name, grid=(n_rows // tm,), in_specs=in_specs, out_specs=out_specs, out_shape=out_shape,
        compiler_params=_params(("arbitrary",) if par_outs else ("parallel",)),
    )(*[r[0] for r in rows], *pars)


def _vjp_of(fn, n_prim, n_out, n_par, n_pass=0):
    def bwd(*args):
        prim = args[:n_prim]
        cts = args[n_prim:n_prim + n_out]
        passes = args[n_prim + n_out:n_prim + n_out + 2 * n_pass]
        pars = args[n_prim + n_out + 2 * n_pass:]
        _, vjp = jax.vjp(fn, *prim, *pars)
        grads = vjp(tuple(cts))
        sums = tuple(passes[2 * i] + passes[2 * i + 1] for i in range(n_pass))
        return tuple(grads[:n_prim]) + sums + tuple(grads[n_prim:])
    return bwd


def _mm(name, a, b, mode="nn", extras=(), epi=None, out_dtypes=(F32,), a_pro=None, out_split=None,
        epi_pars=(), row_sum=False, pin=None, tm=1024, tn=1024, tk=1024):
    split = b.shape[0] if b.ndim == 3 else None
    b_rows, b_cols = b.shape[-2:]
    if mode == "nn":
        (m, k), n = a.shape, b_cols * (split or 1)
    elif mode == "nt":
        (m, k), n = a.shape, b_rows
        assert k == b_cols * (split or 1)
    else:
        assert split is None
        (k, m), n = a.shape, b_cols
    tm, tk = _tile(m, tm), _tile(k, tk)
    tn = _tile(n // out_split, tn) if out_split else _tile(n, tn)
    if split and mode == "nn":
        tn = _tile(b_cols, tn)
    if split and mode == "nt":
        tk = _tile(b_cols, tk)
    nk = k // tk
    raw = {"nn": _raw_nn, "nt": _raw_nt, "tn": _raw_tn}[mode]
    n_e, n_p, n_o = len(extras), len(epi_pars), len(out_dtypes)
    n_in = n_e + n_p + (0 if pin is None else 1)
    if epi is None:
        epi = lambda acc: (acc,)

    def body(a_ref, b_ref, *rest):
        e_refs, p_refs, o_refs = rest[:n_e], rest[n_e:n_e + n_p], rest[n_in:n_in + n_o]
        kk = pl.program_id(2)
        a_tile = a_ref[...] if a_pro is None else a_pro(a_ref[...].astype(F32))
        part = raw(a_tile, b_ref[...])

        def finish(total):
            res = epi(total, *[e[...].astype(F32) for e in e_refs], *[p[...] for p in p_refs])
            for o_ref, r in zip(o_refs, res):
                o_ref[...] = r.astype(o_ref.dtype)
            if row_sum:
                rest[n_in + n_o][...] = res[n_o]

        if nk == 1:
            finish(part)
            return
        acc = rest[-1]

        @pl.when(kk == 0)
        def _():
            acc[...] = part

        @pl.when((kk > 0) & (kk < nk - 1))
        def _():
            acc[...] += part

        @pl.when(kk == nk - 1)
        def _():
            finish(acc[...] + part)

    a_spec = pl.BlockSpec((tk, tm), lambda i, j, kk: (kk, i)) if mode == "tn" else pl.BlockSpec((tm, tk), lambda i, j, kk: (i, kk))
    if split and mode == "nn":
        per = b_cols // tn
        b_spec = pl.BlockSpec((None, tk, tn), lambda i, j, kk: (j // per, kk, j % per))
    elif split:
        per = b_cols // tk
        b_spec = pl.BlockSpec((None, tn, tk), lambda i, j, kk: (kk // per, j, kk % per))
    elif mode == "nt":
        b_spec = pl.BlockSpec((tn, tk), lambda i, j, kk: (j, kk))
    else:
        b_spec = pl.BlockSpec((tk, tn), lambda i, j, kk: (kk, j))
    mn_spec = pl.BlockSpec((tm, tn), lambda i, j, kk: (i, j))
    if out_split:
        assert not extras
        per_out = n // out_split // tn
        out_spec = pl.BlockSpec((None, tm, tn), lambda i, j, kk: (j // per_out, i, j % per_out))
        out_shapes = [jax.ShapeDtypeStruct((out_split, m, n // out_split), dt) for dt in out_dtypes]
    else:
        out_spec = mn_spec
        out_shapes = [jax.ShapeDtypeStruct((m, n), dt) for dt in out_dtypes]
    out_specs = [out_spec] * n_o
    if row_sum:
        out_specs.append(pl.BlockSpec((None, 1, tn), lambda i, j, kk: (i, 0, j)))
        out_shapes.append(jax.ShapeDtypeStruct((m // tm, 1, n), F32))
    in_specs = [a_spec, b_spec] + [mn_spec] * n_e
    in_specs += [pl.BlockSpec(p.shape, functools.partial(lambda i, j, kk, nd: (0,) * nd, nd=p.ndim)) for p in epi_pars]
    in_specs += [] if pin is None else [pl.BlockSpec(memory_space=pl.ANY)]
    outs = pl.pallas_call(
        body, name=name, grid=(m // tm, n // tn, nk), in_specs=in_specs, out_specs=out_specs, out_shape=out_shapes,
        scratch_shapes=[pltpu.VMEM((tm, tn), F32)] if nk > 1 else [],
        compiler_params=_params(("parallel", "parallel", "arbitrary")),
    )(a, b, *extras, *epi_pars, *([] if pin is None else [pin]))
    return outs[0] if len(outs) == 1 else outs


def _sigmoid(x):
    return jax.nn.sigmoid(x)


def _silu(x):
    return x * _sigmoid(x)


def _softplus(x):
    return jnp.maximum(x, 0.0) + jnp.log1p(jnp.exp(-jnp.abs(x)))


def _rmsnorm_fn(x, gain):
    return (x * lax.rsqrt(jnp.mean(x * x, axis=-1, keepdims=True) + EPS) * gain,)


def _head_norm(o, gain, n_heads):
    w = o.shape[-1] // n_heads
    parts = []
    for h in range(n_heads):
        oh = o[:, h * w:(h + 1) * w]
        parts.append(oh * lax.rsqrt(jnp.mean(oh * oh, axis=-1, keepdims=True) + EPS))
    return jnp.concatenate(parts, axis=-1) * gain


@jax.custom_jvp
def _neg_expm1(x):
    u = jnp.exp(x)
    is_one = u == 1.0
    return jnp.where(is_one, -x, (1.0 - u) * x / jnp.log(jnp.where(is_one, 2.0, u)))


@_neg_expm1.defjvp
def _neg_expm1_jvp(primals, tangents):
    (x,), (t,) = primals, tangents
    return _neg_expm1(x), -jnp.exp(x) * t


def _rg_gates_fn(xc, wa, wx, ba, bx, lam):
    outs = []
    for d in range(2):
        r = _sigmoid(_dot_nn(xc, wa[d]) + ba[d:d + 1])
        i = _sigmoid(_dot_nn(xc, wx[d]) + bx[d:d + 1])
        log_a = -RG_C * r * _softplus(-lam[d:d + 1])
        outs.append(jnp.exp(log_a))
        outs.append(jnp.sqrt(_neg_expm1(2.0 * log_a)) * (i * xc))
    return tuple(outs)


def _hg_pre_fn(q, f_f, f_b, logits):
    mx = jnp.maximum(logits[0:1], logits[1:2])
    e0 = jnp.exp(logits[0:1] - mx)
    e1 = jnp.exp(logits[1:2] - mx)
    lb = e0 / (e0 + e1)
    outs = [_silu(q)]
    for f in (f_f, f_b):
        outs.append((1.0 - lb) * _sigmoid(-f))
        outs.append(jnp.log(lb + (1.0 - lb) * _sigmoid(f)))
    return tuple(outs)


def _post0_fn(hs, ga, o, g, gain):
    ya = hs * jax.nn.gelu(ga, approximate=True)
    yb = _head_norm(o, gain, 4) * _silu(g)
    return (jnp.concatenate([ya, yb], axis=-1),)


def _post0_fwd_fn(h_f, h_b, ga, o_f, o_b, g, gain):
    return _post0_fn(h_f + h_b, ga, o_f + o_b, g, gain)


def _post0_bwd_fn(h_f, h_b, ga, o_f, o_b, g, dmix, gain):
    _, vjp = jax.vjp(_post0_fn, h_f + h_b, ga, o_f + o_b, g, gain)
    return vjp((dmix,))


def _gla_pre_fn(q, lr, w_up, b_gate):
    outs = [q * (128.0 ** -0.5)]
    for d in range(2):
        z = _dot_nn(lr, w_up[d]) + b_gate[d:d + 1]
        outs.append(-_softplus(-z) * (1.0 / 16.0))
    return tuple(outs)


def _gla_post_fn(o, r, gain):
    return (_head_norm(o, gain, 4) * _silu(r),)


def _gla_post_fwd_fn(o_f, o_b, r, gain):
    return _gla_post_fn(o_f + o_b, r, gain)


def _gla_post_bwd_fn(o_f, o_b, r, dmix, gain):
    _, vjp = jax.vjp(_gla_post_fn, o_f + o_b, r, gain)
    return vjp((dmix,))


def _relu2_bwd_epi(acc, hid):
    return (acc * 2.0 * jnp.maximum(hid, 0.0),)


def _relu2(x):
    r = jnp.maximum(x, 0.0)
    return r * r


def _add_epi(acc, res):
    return (acc + res,)


def _loss_head_fn(h, target, gain):
    def f(h, gain):
        y = _rmsnorm_fn(h, gain)[0]
        err = y - target
        return 0.5 * jnp.sum(jnp.mean(err * err, axis=-1, keepdims=True))
    loss, (dh, dgain) = jax.value_and_grad(f, argnums=(0, 1))(h, gain)
    return dh, dh, jnp.full((1, LANES), loss, F32), dgain


def _adam_fn(w, g, m, v):
    m2 = ADAM_B1 * m + (1.0 - ADAM_B1) * g
    v2 = ADAM_B2 * v + (1.0 - ADAM_B2) * (g * g)
    m_hat = m2 / (1.0 - ADAM_B1 ** ADAM_STEP)
    v_hat = v2 / (1.0 - ADAM_B2 ** ADAM_STEP)
    delta = -ADAM_LR * (m_hat / (jnp.sqrt(v_hat) + ADAM_EPS) + ADAM_WD * w)
    return delta, m2, v2


def _shifted(x, t_idx, off):
    n = x.shape[0]
    rolled = pltpu.roll(x, (-off) % n, 0)
    valid = (t_idx + off >= 0) & (t_idx + off < n)
    return jnp.where(valid, rolled, 0.0)


def _conv_fwd(name, src, colblock, w, b):
    n_rows, width = src.shape[0], w.shape[1]

    def body(x_ref, w_ref, b_ref, o_ref):
        x = x_ref[...]
        t_idx = lax.broadcasted_iota(jnp.int32, x.shape, 0)
        acc = b_ref[...] + w_ref[2:3, :] * x
        acc += w_ref[0:1, :] * _shifted(x, t_idx, -2)
        acc += w_ref[1:2, :] * _shifted(x, t_idx, -1)
        acc += w_ref[3:4, :] * _shifted(x, t_idx, 1)
        o_ref[...] = acc

    nb = width // LANES
    return pl.pallas_call(
        body, name=name, grid=(nb,),
        in_specs=[pl.BlockSpec((n_rows, LANES), lambda j: (0, colblock * nb + j)),
                  pl.BlockSpec((4, LANES), lambda j: (0, j)), pl.BlockSpec((1, LANES), lambda j: (0, j))],
        out_specs=pl.BlockSpec((n_rows, LANES), lambda j: (0, j)),
        out_shape=jax.ShapeDtypeStruct((n_rows, width), F32),
        compiler_params=_params(("parallel",)),
    )(src, w, b)


def _conv_bwd(name, src, colblock, w, d):
    n_rows, width = src.shape[0], w.shape[1]

    def body(x_ref, w_ref, d_ref, dx_ref, dw_ref, db_ref):
        x = x_ref[...]
        g = d_ref[...]
        t_idx = lax.broadcasted_iota(jnp.int32, x.shape, 0)
        dx = w_ref[2:3, :] * g
        dx += w_ref[0:1, :] * _shifted(g, t_idx, 2)
        dx += w_ref[1:2, :] * _shifted(g, t_idx, 1)
        dx += w_ref[3:4, :] * _shifted(g, t_idx, -1)
        dx_ref[...] = dx.astype(dx_ref.dtype)
        dw_ref[0:1, :] = jnp.sum(g * _shifted(x, t_idx, -2), axis=0, keepdims=True)
        dw_ref[1:2, :] = jnp.sum(g * _shifted(x, t_idx, -1), axis=0, keepdims=True)
        dw_ref[2:3, :] = jnp.sum(g * x, axis=0, keepdims=True)
        dw_ref[3:4, :] = jnp.sum(g * _shifted(x, t_idx, 1), axis=0, keepdims=True)
        db_ref[...] = jnp.sum(g, axis=0, keepdims=True)

    nb = width // LANES
    return pl.pallas_call(
        body, name=name, grid=(nb,),
        in_specs=[pl.BlockSpec((n_rows, LANES), lambda j: (0, colblock * nb + j)),
                  pl.BlockSpec((4, LANES), lambda j: (0, j)),
                  pl.BlockSpec((n_rows, LANES), lambda j: (0, j))],
        out_specs=[pl.BlockSpec((n_rows, LANES), lambda j: (0, j)), pl.BlockSpec((4, LANES), lambda j: (0, j)),
                   pl.BlockSpec((1, LANES), lambda j: (0, j))],
        out_shape=[jax.ShapeDtypeStruct((n_rows, width), BF16), jax.ShapeDtypeStruct((4, width), F32),
                   jax.ShapeDtypeStruct((1, width), F32)],
        compiler_params=_params(("parallel",)),
    )(src, w, d)


SUBLANES = 8
SCAN_UNROLL = 8


def _shift_rows(x, d, fill):
    n = x.shape[0]
    t = lax.broadcasted_iota(jnp.int32, x.shape, 0)
    valid = (t >= d) if d > 0 else (t < n + d)
    return jnp.where(valid, pltpu.roll(x, d % n, 0), fill)


def _tile_scan(a, u, reverse):
    d = 1
    while d < a.shape[0]:
        s = -d if reverse else d
        a_sh, u_sh = _shift_rows(a, s, 1.0), _shift_rows(u, s, 0.0)
        u = u + a * u_sh
        a = a * a_sh
        d *= 2
    return a, u


def _edge_row(x, reverse):
    return x[0:1, :] if reverse else x[SUBLANES - 1:SUBLANES, :]


def _scan_specs(n_rows, n):
    return [pl.BlockSpec((n_rows, LANES), lambda j: (0, j))] * n


def _scan_tile(a_ref, u_ref, h_ref, i, carry, reverse):
    n_tiles = a_ref.shape[0] // SUBLANES
    tile = (n_tiles - 1 - i) if reverse else i
    rows = pl.ds(pl.multiple_of(tile * SUBLANES, SUBLANES), SUBLANES)
    acc_a, acc_u = _tile_scan(a_ref[rows, :], u_ref[rows, :], reverse)
    h = acc_u + acc_a * carry
    h_ref[rows, :] = h
    return _edge_row(h, reverse)


def _scan_fwd(name, a_f, u_f, a_b, u_b):
    n_rows, width = a_f.shape

    def body(af_ref, uf_ref, ab_ref, ub_ref, hf_ref, hb_ref):
        def step(i, carry):
            return (_scan_tile(af_ref, uf_ref, hf_ref, i, carry[0], False),
                    _scan_tile(ab_ref, ub_ref, hb_ref, i, carry[1], True))
        zero = jnp.zeros((1, LANES), F32)
        lax.fori_loop(0, n_rows // SUBLANES, step, (zero, zero), unroll=SCAN_UNROLL)

    return pl.pallas_call(
        body, name=name, grid=(width // LANES,), in_specs=_scan_specs(n_rows, 4), out_specs=_scan_specs(n_rows, 2),
        out_shape=[jax.ShapeDtypeStruct((n_rows, width), F32)] * 2, compiler_params=_params(("parallel",)),
    )(a_f, u_f, a_b, u_b)


def _scan_bwd_tile(a_ref, h_ref, dh_ref, du_ref, da_ref, i, carry, reverse):
    n_rows = a_ref.shape[0]
    n_tiles = n_rows // SUBLANES
    against = not reverse
    one = -1 if against else 1
    g_in, a_edge = carry
    tile = (n_tiles - 1 - i) if against else i
    start = pl.multiple_of(tile * SUBLANES, SUBLANES)
    rows = pl.ds(start, SUBLANES)
    a_tile = a_ref[rows, :]
    coeff = _shift_rows(a_tile, one, a_edge)
    acc_a, acc_u = _tile_scan(coeff, dh_ref[rows, :], against)
    g = acc_u + acc_a * g_in
    du_ref[rows, :] = g
    outside = (start + SUBLANES) if reverse else (start - 1)
    inside = (outside >= 0) & (outside < n_rows)
    h_edge = jnp.where(inside, h_ref[pl.ds(jnp.clip(outside, 0, n_rows - 1), 1), :], 0.0)
    da_ref[rows, :] = g * _shift_rows(h_ref[rows, :], -one, h_edge)
    return _edge_row(g, against), _edge_row(a_tile, against)


def _scan_bwd(name, a_f, h_f, a_b, h_b, dh):
    n_rows, width = a_f.shape

    def body(af_ref, hf_ref, ab_ref, hb_ref, dh_ref, duf_ref, daf_ref, dub_ref, dab_ref):
        def step(i, carry):
            return (_scan_bwd_tile(af_ref, hf_ref, dh_ref, duf_ref, daf_ref, i, carry[0], False),
                    _scan_bwd_tile(ab_ref, hb_ref, dh_ref, dub_ref, dab_ref, i, carry[1], True))
        zero = jnp.zeros((1, LANES), F32)
        lax.fori_loop(0, n_rows // SUBLANES, step, ((zero, zero), (zero, zero)), unroll=SCAN_UNROLL)

    return pl.pallas_call(
        body, name=name, grid=(width // LANES,), in_specs=_scan_specs(n_rows, 5), out_specs=_scan_specs(n_rows, 4),
        out_shape=[jax.ShapeDtypeStruct((n_rows, width), F32)] * 4, compiler_params=_params(("parallel",)),
    )(a_f, h_f, a_b, h_b, dh)


def _tri_mask(c, reverse):
    row = lax.broadcasted_iota(jnp.int32, (c, c), 0)
    col = lax.broadcasted_iota(jnp.int32, (c, c), 1)
    return (col >= row) if reverse else (col <= row)


def _cumsum_rows(x, reverse):
    tri = _tri_mask(x.shape[0], reverse).astype(BF16)
    hi = x.astype(BF16)
    rest = x - hi.astype(F32)
    mid = rest.astype(BF16)
    lo = (rest - mid.astype(F32)).astype(BF16)
    return _raw_nn(tri, hi) + _raw_nn(tri, mid) + _raw_nn(tri, lo)


@functools.partial(jax.custom_vjp, nondiff_argnums=(1,))
def _cumsum(x, reverse):
    return _cumsum_rows(x, reverse)


def _cumsum_fwd(x, reverse):
    return _cumsum_rows(x, reverse), None


def _cumsum_bwd(reverse, _, g):
    return (_cumsum_rows(g, not reverse),)


_cumsum.defvjp(_cumsum_fwd, _cumsum_bwd)


def _chunks_fn(qs, ks, vs, lfs, sts, reverses):
    n, c = len(qs), qs[0].shape[0]
    every = range(n)
    tris = [_tri_mask(c, r) for r in reverses]
    cums = [_cumsum(lfs[i], reverses[i]) for i in every]
    rid = lax.broadcasted_iota(jnp.int32, cums[0].shape, 0)

    def pick(cum, r):
        return jnp.sum(jnp.where(rid == r, cum, 0.0), axis=0, keepdims=True)

    refs = [pick(cums[i], (c - 1 - c // 2) if reverses[i] else c // 2) for i in every]
    lasts = [pick(cums[i], 0 if reverses[i] else c - 1) for i in every]
    q_in = [qs[i] * jnp.exp(cums[i] - refs[i]) for i in every]
    k_in = [ks[i] * jnp.exp(refs[i] - cums[i]) for i in every]
    scores = [jnp.where(tris[i], _dot_nt(q_in[i], k_in[i]), 0.0) for i in every]
    o_intra = [_dot_nn(scores[i], vs[i]) for i in every]
    q_out = [qs[i] * jnp.exp(cums[i]) for i in every]
    o_inter = [_dot_nt(q_out[i], sts[i]) for i in every]
    k_state = [ks[i] * jnp.exp(lasts[i] - cums[i]) for i in every]
    upd = [_dot_tn(vs[i], k_state[i]) for i in every]
    st_new = [sts[i] * jnp.exp(lasts[i]) + upd[i] for i in every]
    return [o_intra[i] + o_inter[i] for i in every], st_new


def _attn_fwd(name, q, k_f, k_b, v, lf_f, lf_b, n_heads, dk, dv):
    n_rows = q[0].shape[0]
    n_chunks = n_rows // CHUNK
    n_steps = n_chunks // ATTN_SUB
    wk, wv = n_heads * dk, n_heads * dv

    def spec(width, off, rev):
        return pl.BlockSpec((CHUNK * ATTN_SUB, width), lambda n: ((n_steps - 1 - n) if rev else n, off))

    def sspec(rev):
        return pl.BlockSpec((ATTN_SUB, n_heads, dv, dk), lambda n: ((n_steps - 1 - n) if rev else n, 0, 0, 0))

    def body(qf, kf, vf, lff, qb, kb, vb, lfb, of_ref, ob_ref, sf_ref, sb_ref, st):
        @pl.when(pl.program_id(0) == 0)
        def _():
            st[...] = jnp.zeros_like(st)

        ins = ((qf, kf, vf, lff), (qb, kb, vb, lfb))
        chains = [(d, h) for d in range(2) for h in range(n_heads)]
        ck = [slice(h * dk, (h + 1) * dk) for h in range(n_heads)]
        cv = [slice(h * dv, (h + 1) * dv) for h in range(n_heads)]
        sts = [st[d, h] for d, h in chains]
        done = []
        for sub in range(ATTN_SUB):
            local = (sub, ATTN_SUB - 1 - sub)
            rows = [slice(local[d] * CHUNK, (local[d] + 1) * CHUNK) for d in range(2)]
            qs = [ins[d][0][rows[d], ck[h]] for d, h in chains]
            ks = [ins[d][1][rows[d], ck[h]] for d, h in chains]
            vs = [ins[d][2][rows[d], cv[h]] for d, h in chains]
            lfs = [ins[d][3][rows[d], ck[h]] for d, h in chains]
            os_, st_new = _chunks_fn(qs, ks, vs, lfs, sts, [d == 1 for d, _ in chains])
            done.append((local, rows, sts, os_))
            sts = st_new
        for local, rows, entered, os_ in done:
            for i, (d, h) in enumerate(chains):
                (sf_ref, sb_ref)[d][local[d], h] = entered[i].astype(BF16)
                (of_ref, ob_ref)[d][rows[d], cv[h]] = os_[i]
        for i, (d, h) in enumerate(chains):
            st[d, h] = sts[i]

    in_specs = [spec(wk, q[1], False), spec(wk, k_f[1], False), spec(wv, v[1], False), spec(wk, lf_f[1], False),
                spec(wk, q[1], True), spec(wk, k_b[1], True), spec(wv, v[1], True), spec(wk, lf_b[1], True)]
    return pl.pallas_call(
        body, name=name, grid=(n_steps,), in_specs=in_specs,
        out_specs=[spec(wv, 0, False), spec(wv, 0, True), sspec(False), sspec(True)],
        out_shape=[jax.ShapeDtypeStruct((n_rows, wv), F32)] * 2
        + [jax.ShapeDtypeStruct((n_chunks, n_heads, dv, dk), BF16)] * 2,
        scratch_shapes=[pltpu.VMEM((2, n_heads, dv, dk), F32)],
        compiler_params=_params(("arbitrary",)),
    )(q[0], k_f[0], v[0], lf_f[0], q[0], k_b[0], v[0], lf_b[0])


def _attn_bwd(name, q, k_f, k_b, v, lf_f, lf_b, st_f, st_b, do, n_heads, dk, dv, out_dtype=F32):
    n_rows = q[0].shape[0]
    n_chunks = n_rows // CHUNK
    n_steps = n_chunks // ATTN_SUB
    wk, wv = n_heads * dk, n_heads * dv

    def spec(width, off, rev):
        return pl.BlockSpec((CHUNK * ATTN_SUB, width), lambda n: (n if rev else (n_steps - 1 - n), off))

    def sspec(rev):
        return pl.BlockSpec((ATTN_SUB, n_heads, dv, dk), lambda n: (n if rev else (n_steps - 1 - n), 0, 0, 0))

    def body(qf, kf, vf, lff, sf, dof, qb, kb, vb, lfb, sb, dob,
             dqf, dkf, dvf, dlff, dqb, dkb, dvb, dlfb, dst):
        @pl.when(pl.program_id(0) == 0)
        def _():
            dst[...] = jnp.zeros_like(dst)

        ins = ((qf, kf, vf, lff, sf, dof), (qb, kb, vb, lfb, sb, dob))
        outs = ((dqf, dkf, dvf, dlff), (dqb, dkb, dvb, dlfb))
        chains = [(d, h) for d in range(2) for h in range(n_heads)]
        ck = [slice(h * dk, (h + 1) * dk) for h in range(n_heads)]
        cv = [slice(h * dv, (h + 1) * dv) for h in range(n_heads)]
        fn = functools.partial(_chunks_fn, reverses=[d == 1 for d, _ in chains])
        dsts = [dst[d, h] for d, h in chains]
        done = []
        for sub in range(ATTN_SUB):
            local = (ATTN_SUB - 1 - sub, sub)
            rows = [slice(local[d] * CHUNK, (local[d] + 1) * CHUNK) for d in range(2)]
            qs = [ins[d][0][rows[d], ck[h]] for d, h in chains]
            ks = [ins[d][1][rows[d], ck[h]] for d, h in chains]
            vs = [ins[d][2][rows[d], cv[h]] for d, h in chains]
            lfs = [ins[d][3][rows[d], ck[h]] for d, h in chains]
            sts = [ins[d][4][local[d], h].astype(F32) for d, h in chains]
            dos = [ins[d][5][rows[d], cv[h]] for d, h in chains]
            _, vjp = jax.vjp(fn, qs, ks, vs, lfs, sts)
            dqs, dks, dvs, dlfs, dsts = vjp((dos, dsts))
            done.append((rows, dqs, dks, dvs, dlfs))
        for rows, dqs, dks, dvs, dlfs in done:
            for i, (d, h) in enumerate(chains):
                dq_r, dk_r, dv_r, dlf_r = outs[d]
                dq_r[rows[d], ck[h]] = dqs[i].astype(dq_r.dtype)
                dk_r[rows[d], ck[h]] = dks[i].astype(dk_r.dtype)
                dv_r[rows[d], cv[h]] = dvs[i].astype(dv_r.dtype)
                dlf_r[rows[d], ck[h]] = dlfs[i].astype(dlf_r.dtype)
        for i, (d, h) in enumerate(chains):
            dst[d, h] = dsts[i]

    def dir_specs(kk, lf, rev):
        return [spec(wk, q[1], rev), spec(wk, kk[1], rev), spec(wv, v[1], rev), spec(wk, lf[1], rev), sspec(rev),
                spec(wv, 0, rev)]

    def dir_out_specs(rev):
        return [spec(wk, 0, rev), spec(wk, 0, rev), spec(wv, 0, rev), spec(wk, 0, rev)]

    shapes = [jax.ShapeDtypeStruct((n_rows, wk), out_dtype), jax.ShapeDtypeStruct((n_rows, wk), out_dtype),
              jax.ShapeDtypeStruct((n_rows, wv), out_dtype), jax.ShapeDtypeStruct((n_rows, wk), F32)]
    outs = pl.pallas_call(
        body, name=name, grid=(n_steps,), in_specs=dir_specs(k_f, lf_f, False) + dir_specs(k_b, lf_b, True),
        out_specs=dir_out_specs(False) + dir_out_specs(True), out_shape=shapes + shapes,
        scratch_shapes=[pltpu.VMEM((2, n_heads, dv, dk), F32)],
        compiler_params=_params(("arbitrary",)),
    )(q[0], k_f[0], v[0], lf_f[0], st_f, do, q[0], k_b[0], v[0], lf_b[0], st_b, do)
    return outs[:4], outs[4:]


def _row2(v):
    return v.reshape(1, -1)


def _mlp_fwd(tag, h, gain, w1, w2):
    y = _rowcall(f"{tag}_norm", _rmsnorm_fn, [(h, h.shape[1], 0)], [gain], [(h.shape[1], BF16)], tm=512)[0]
    hid = _mm(f"{tag}_up", y, w1, out_dtypes=(BF16,), tm=2048)
    h_out = _mm(f"{tag}_down", hid, w2, a_pro=_relu2, extras=(h,), epi=_add_epi, tk=2048)
    return h_out, (y, hid)


def _dw(name, a, b, **kw):
    return _mm(name, a, b, mode="tn", epi=lambda acc: (acc, acc), out_dtypes=(F32, BF16), **kw)


def _mlp_bwd(tag, h, gain, w1, w2, saved, dh_out):
    y, hid = saved
    dhid = _mm(f"{tag}_dact", dh_out[1], w2, mode="nt", extras=(hid,), epi=_relu2_bwd_epi, out_dtypes=(BF16,),
               tm=2048)
    dw2 = _dw(f"{tag}_dw2", hid, dh_out[1], a_pro=_relu2, tk=2048)
    dw1 = _dw(f"{tag}_dw1", y, dhid, out_split=N_CHIPS, tk=4096)
    dh, dgain = _dy_norm_bwd(f"{tag}_dy", dhid, w1, h, gain, dh_out[0])
    return dh, dgain, dw1, dw2


def _dy_norm_bwd(name, dz, w, h, gain, dres, pin=None, twice=True, **tiles):
    n_out = 2 if twice else 1

    def epi(dy, h_tile, dres_tile, gain_row):
        _, vjp = jax.vjp(lambda u, v: _rmsnorm_fn(u, v)[0], h_tile, gain_row)
        dh, dgain = vjp(dy)
        return (dh + dres_tile,) * n_out + (dgain,)

    assert h.shape[1] <= 1024
    tiles.setdefault("tm", 1024)
    *dh, dgain_parts = _mm(name, dz, w, mode="nt", extras=(h, dres), epi=epi, epi_pars=(gain,), row_sum=True,
                           out_dtypes=(F32, BF16)[:n_out], pin=pin, **tiles)
    return dh, jnp.sum(dgain_parts, axis=0)


def _local_step(x, target, w, pin=None, late=None, emit=None):
    g = {}
    d_model = x.shape[1]
    rg_w = hg_w = d_model // 2
    pins = []

    def send_off(tag, pairs):
        if emit is not None:
            pins.append(emit(tag, [p[0] for p in pairs], [p[1] for p in pairs]))

    def both(fn, pair):
        return [fn(t) for t in pair]

    def chip_major(t):
        return t.reshape(N_CHIPS, t.shape[0] // N_CHIPS, t.shape[1])

    h_a0 = x
    gain = _row2(w["norm_mix"][0])
    y0 = _rowcall("l0_norm", _rmsnorm_fn, [(h_a0, d_model, 0)], [gain], [(d_model, BF16)], tm=512, pin=pin)[0]
    proj0 = _mm("l0_in", y0, w["ab_w_in"], tm=2048)
    conv_w, conv_b = w["rg_conv_w"], _row2(w["rg_conv_b"])
    xc = _conv_fwd("rg_conv", proj0, 0, conv_w, conv_b)
    gate_pars = [w["rg_wa_bd"], w["rg_wx_bd"], w["rg_b_a"], w["rg_b_x"], w["rg_lambda"]]
    a_f, u_f, a_b, u_b = _rowcall("rg_gates", _rg_gates_fn, [(xc, rg_w, 0)], gate_pars, [(rg_w, F32)] * 4)
    hs_f, hs_b = _scan_fwd("rg_scan", a_f, u_f, a_b, u_b)
    hg_rows = [(proj0, hg_w, 2), (proj0, hg_w, 3), (proj0, hg_w, 4)]
    qh, k_f, lf_f, k_b, lf_b = _rowcall("hg_pre", _hg_pre_fn, hg_rows, [w["hg_lb_logits"]], [(hg_w, F32)] * 5)
    iv = (proj0, 5)
    o_f, o_b, st_f, st_b = _attn_fwd("hg_attn", (qh, 0), (k_f, 0), (k_b, 0), iv, (lf_f, 0), (lf_b, 0), 4, 128, 128)
    post0_rows = [(hs_f, rg_w, 0), (hs_b, rg_w, 0), (proj0, rg_w, 1), (o_f, hg_w, 0), (o_b, hg_w, 0), (proj0, hg_w, 6)]
    hg_gain = _row2(w["hg_norm"])
    mix_in0 = _rowcall("l0_post", _post0_fwd_fn, post0_rows, [hg_gain], [(d_model, BF16)])[0]
    if late is not None:
        w = {**w, **late(mix_in0)}
    h_b0 = _mm("l0_out", mix_in0, w["ab_w_out"], extras=(h_a0,), epi=_add_epi)
    h_c0, mlp0 = _mlp_fwd("mlp0", h_b0, _row2(w["norm_mlp"][0]), w["mlp_w1"][0], w["mlp_w2"][0])

    h_a1 = h_c0
    gain1 = _row2(w["norm_mix"][1])
    y1 = _rowcall("l1_norm", _rmsnorm_fn, [(h_a1, d_model, 0)], [gain1], [(d_model, BF16)], tm=512)[0]
    proj1 = _mm("l1_in", y1, w["gla_w_in_pad"], tm=512, tn=GLA_IN_PAD)
    gla_pars = [w["gla_w_up_pad"], w["gla_b_gate"]]
    gq, glf_f, glf_b = _rowcall("gla_pre", _gla_pre_fn, [(proj1, 512, 0), (proj1, LANES, 24)], gla_pars, [(512, F32)] * 3)
    gk, gv = (proj1, 1), (proj1, 1)
    go_f, go_b, gst_f, gst_b = _attn_fwd("gla_attn", (gq, 0), gk, gk, gv, (glf_f, 0), (glf_b, 0), 4, 128, 256)
    gla_gain = _row2(w["gla_norm"])
    post1_rows = [(go_f, d_model, 0), (go_b, d_model, 0), (proj1, d_model, 2)]
    mix_in1 = _rowcall("l1_post", _gla_post_fwd_fn, post1_rows, [gla_gain], [(d_model, BF16)])[0]
    h_b1 = _mm("l1_out", mix_in1, w["gla_w_out"], extras=(h_a1,), epi=_add_epi)
    h_c1, mlp1 = _mlp_fwd("mlp1", h_b1, _row2(w["norm_mlp"][1]), w["mlp_w1"][1], w["mlp_w2"][1])

    *dh, loss, g["norm_final"] = _rowcall(
        "loss_head", _loss_head_fn, [(h_c1, d_model, 0), (target, d_model, 0)], [_row2(w["norm_final"])],
        [(d_model, F32), (d_model, BF16)], [(1, LANES), (1, d_model)], tm=512)

    dh, g_nmlp1, g_w1_1, g_w2_1 = _mlp_bwd("mlp1", h_b1, _row2(w["norm_mlp"][1]), w["mlp_w1"][1], w["mlp_w2"][1], mlp1, dh)
    send_off("mlp1", [g_w1_1, both(chip_major, g_w2_1)])
    dmix1 = _mm("l1_dout", dh[1], w["gla_w_out"], mode="nt")
    g_gla_out = _dw("l1_dwout", mix_in1, dh[1])
    g["gla_w_out"] = g_gla_out[0]
    dgo, dr, g["gla_norm"] = _rowcall(
        "l1_dpost", _gla_post_bwd_fn, post1_rows + [(dmix1, d_model, 0)], [gla_gain],
        [(d_model, F32), (d_model, BF16)], [(1, d_model)], pin=pins.pop() if pins else None)
    (dq_f, dk_f, dv_f, dlf_f), (dq_b, dk_b, dv_b, dlf_b) = _attn_bwd(
        "gla_dattn", (gq, 0), gk, gk, gv, (glf_f, 0), (glf_b, 0), gst_f, gst_b, dgo, 4, 128, 256)

    def gla_pre_bwd(q, lr, dq1, dq2, dlf1, dlf2, dk1, dk2, dv1, dv2, w_up, b_gate):
        dlr = jnp.zeros_like(lr)
        dws, dbs = [], []
        for d, dlf in enumerate((dlf1, dlf2)):
            z = _raw_nn(lr, w_up[d]) + b_gate[d:d + 1]
            dz = dlf * _sigmoid(-z) * (1.0 / 16.0)
            dlr = dlr + _raw_nt(dz, w_up[d])
            dws.append(_raw_tn(dz, lr))
            dbs.append(jnp.sum(dz, axis=0, keepdims=True))
        return ((dq1 + dq2) * (128.0 ** -0.5), dk1 + dk2, dv1 + dv2, dlr, dws[0], dws[1], dbs[0], dbs[1])

    rows = [(proj1, 512, 0), (proj1, LANES, 24), (dq_f, 512, 0), (dq_b, 512, 0), (dlf_f, 512, 0), (dlf_b, 512, 0),
            (dk_f, 512, 0), (dk_b, 512, 0), (dv_f, d_model, 0), (dv_b, d_model, 0)]
    dq, dk, dv, dlr, dwt_f, dwt_b, db_f, db_b = _rowcall(
        "gla_dpre", gla_pre_bwd, rows, gla_pars, [(512, BF16), (512, BF16), (d_model, BF16), (LANES, BF16)],
        [(512, LANES), (512, LANES), (1, 512), (1, 512)])
    g["gla_w_up_pad"] = jnp.stack([dwt_f.T, dwt_b.T])
    g["gla_b_gate"] = jnp.concatenate([db_f, db_b], axis=0)
    dproj1 = jnp.concatenate([dq, dk, dv, dr, dlr], axis=1)
    g_gla_in = both(lambda t: _split_chips(t[:, :GLA_IN_WIDTH], 1), _dw("l1_dwin", y1, dproj1, tn=640, tk=4096))
    g["gla_w_in"] = g_gla_in[0]
    send_off("gla", [g_gla_in, both(chip_major, g_gla_out)])
    dh, g_nmix1 = _dy_norm_bwd("l1_dy", dproj1, w["gla_w_in_pad"], h_a1, gain1, dh[0],
                               pin=pins.pop() if pins else None, tk=GLA_IN_PAD)

    dh, g_nmlp0, g_w1_0, g_w2_0 = _mlp_bwd("mlp0", h_b0, _row2(w["norm_mlp"][0]), w["mlp_w1"][0], w["mlp_w2"][0], mlp0, dh)
    g_ab_out = _dw("l0_dwout", mix_in0, dh[1])
    g["ab_w_out"] = g_ab_out[0]
    send_off("mlp0", [g_w1_0, both(chip_major, g_w2_0), both(chip_major, g_ab_out)])
    dmix0 = _mm("l0_dout", dh[1], w["ab_w_out"], mode="nt")
    dhs, dga, do, dg, g["hg_norm"] = _rowcall(
        "l0_dpost", _post0_bwd_fn, post0_rows + [(dmix0, d_model, 0)], [hg_gain],
        [(rg_w, F32), (rg_w, BF16), (hg_w, F32), (hg_w, BF16)], [(1, hg_w)], pin=pins.pop() if pins else None)
    (dqh_f, dk_f, div_f, dlf_f), (dqh_b, dk_b, div_b, dlf_b) = _attn_bwd(
        "hg_dattn", (qh, 0), (k_f, 0), (k_b, 0), iv, (lf_f, 0), (lf_b, 0), st_f, st_b, do, 4, 128, 128)

    def hg_pre_bwd(q, f_f, f_b, dq1, dq2, dk1, dlf1, dk2, dlf2, dv1, dv2, logits):
        _, vjp = jax.vjp(_hg_pre_fn, q, f_f, f_b, logits)
        dq, df_f, df_b, dlogits = vjp((dq1 + dq2, dk1, dlf1, dk2, dlf2))
        return dq, df_f, df_b, dv1 + dv2, dlogits

    rows = hg_rows + [(t, hg_w, 0) for t in (dqh_f, dqh_b, dk_f, dlf_f, dk_b, dlf_b, div_f, div_b)]
    dq, df_f, df_b, div, g["hg_lb_logits"] = _rowcall(
        "hg_dpre", hg_pre_bwd, rows, [w["hg_lb_logits"]], [(hg_w, BF16)] * 4, [(2, hg_w)])
    du_f, da_f, du_b, da_b = _scan_bwd("rg_dscan", a_f, hs_f, a_b, hs_b, dhs)
    gates_bwd = _vjp_of(_rg_gates_fn, 1, 4, 5)
    rows = [(xc, rg_w, 0), (da_f, rg_w, 0), (du_f, rg_w, 0), (da_b, rg_w, 0), (du_b, rg_w, 0)]
    dxc, g["rg_wa_bd"], g["rg_wx_bd"], g["rg_b_a"], g["rg_b_x"], g["rg_lambda"] = _rowcall(
        "rg_dgates", gates_bwd, rows, gate_pars, [(rg_w, F32)],
        [(2, rg_w, rg_w), (2, rg_w, rg_w), (2, rg_w), (2, rg_w), (2, rg_w)])
    dxa, g["rg_conv_w"], g["rg_conv_b"] = _conv_bwd("rg_dconv", proj0, 0, conv_w, dxc)
    dproj0 = jnp.concatenate([dxa, dga, dq, df_f, df_b, div, dg], axis=1)
    g_ab_in = _dw("l0_dwin", y0, dproj0, out_split=N_CHIPS, tk=4096)
    g["ab_w_in"] = g_ab_in[0]
    send_off("ab", [g_ab_in])
    (grad_x,), g_nmix0 = _dy_norm_bwd("l0_dy", dproj0, w["ab_w_in"], h_a0, gain, dh[0],
                                      pin=pins.pop() if pins else None, twice=False)

    g["norm_mix"] = jnp.concatenate([g_nmix0, g_nmix1], axis=0)
    g["norm_mlp"] = jnp.concatenate([g_nmlp0, g_nmlp1], axis=0)
    g["mlp_w1"] = [g_w1_0[0], g_w1_1[0]]
    g["mlp_w2"] = [g_w2_0[0], g_w2_1[0]]
    return loss, grad_x, g


def _block_diag(w):
    d, g, n, _ = w.shape
    eye = jnp.eye(g, dtype=w.dtype)
    return (w[:, :, :, None, :] * eye[None, :, None, :, None]).reshape(d, g * n, g * n)


def _block_diag_extract(wbd, g):
    d, gn, _ = wbd.shape
    n = gn // g
    blocks = wbd.reshape(d, g, n, g, n)
    return jnp.stack([blocks[:, i, :, i, :] for i in range(g)], axis=1)


def _prepare_weights(big, full):
    w = {k: full[k] for k in ("norm_mix", "norm_mlp", "norm_final", "hg_lb_logits")}
    for k in ("rg_conv_w", "rg_conv_b", "rg_b_a", "rg_b_x", "rg_lambda", "hg_norm", "gla_b_gate", "gla_norm"):
        w[k] = full[k][0]
    w["rg_wa_bd"] = _block_diag(full["rg_w_a"][0])
    w["rg_wx_bd"] = _block_diag(full["rg_w_x"][0])
    up = full["gla_w_gate_up"][0]
    rank = up.shape[1]
    pad = jnp.zeros((2, LANES, up.shape[2]), F32)
    w["gla_w_up_pad"] = pad.at[0, 0:rank].set(up[0]).at[1, rank:2 * rank].set(up[1])
    w.update(_prepare_matrices(big))
    return w


def _prepare_matrices(big):
    w = {}
    if "mlp_w1" in big:
        w["mlp_w1"] = list(big["mlp_w1"])
        w["mlp_w2"] = [t.reshape(-1, t.shape[-1]) for t in big["mlp_w2"]]
    if "ab_w_in" in big:
        w["ab_w_in"] = big["ab_w_in"]
    if "ab_w_out" in big:
        w["ab_w_out"] = big["ab_w_out"].reshape(-1, big["ab_w_out"].shape[-1])
    if "gla_w_in" in big:
        w["gla_w_out"] = big["gla_w_out"].reshape(-1, big["gla_w_out"].shape[-1])
        gla_in = _join_chips(big["gla_w_in"], 1)
        w["gla_w_in_pad"] = jnp.pad(gla_in, ((0, 0), (0, GLA_IN_PAD - gla_in.shape[1])))
    return w


def _finish_grads(g, rank=16, rg_blocks=8):
    def chip_major(t):
        return t.reshape(N_CHIPS, t.shape[0] // N_CHIPS, t.shape[1])

    big = {
        "mlp_w1": list(g["mlp_w1"]), "mlp_w2": [chip_major(t) for t in g["mlp_w2"]],
        "ab_w_in": g["ab_w_in"], "ab_w_out": chip_major(g["ab_w_out"]),
        "gla_w_in": g["gla_w_in"], "gla_w_out": chip_major(g["gla_w_out"]),
    }
    small = {
        "norm_mix": g["norm_mix"], "norm_mlp": g["norm_mlp"], "norm_final": g["norm_final"][0],
        "rg_conv_w": g["rg_conv_w"][None], "rg_conv_b": g["rg_conv_b"],
        "rg_w_a": _block_diag_extract(g["rg_wa_bd"], rg_blocks)[None], "rg_b_a": g["rg_b_a"][None],
        "rg_w_x": _block_diag_extract(g["rg_wx_bd"], rg_blocks)[None], "rg_b_x": g["rg_b_x"][None],
        "rg_lambda": g["rg_lambda"][None], "hg_lb_logits": g["hg_lb_logits"], "hg_norm": g["hg_norm"],
        "gla_w_gate_up": jnp.stack([g["gla_w_up_pad"][0, 0:rank], g["gla_w_up_pad"][1, rank:2 * rank]])[None],
        "gla_b_gate": g["gla_b_gate"][None], "gla_norm": g["gla_norm"],
    }
    return big, small


MATRICES = (("mlp_w1", 0), ("mlp_w1", 1), ("mlp_w2", 0), ("mlp_w2", 1), ("ab_w_in", 0), ("ab_w_out", 0),
            ("gla_w_in", 0), ("gla_w_out", 0))
EARLY_MATRICES = ("ab_w_in",)
SMALL_SHARDED = ("rg_conv_w", "rg_b_a", "rg_b_x", "rg_lambda", "gla_w_gate_up", "gla_b_gate", "gla_norm")
SMALL_REPLICATED = ("norm_mix", "norm_mlp", "norm_final", "rg_conv_b", "rg_w_a", "rg_w_x", "hg_lb_logits", "hg_norm")
WEIGHTS = ("norm_mix", "norm_mlp", "norm_final", "mlp_w1", "mlp_w2", "ab_w_in", "ab_w_out", "rg_conv_w", "rg_conv_b",
           "rg_w_a", "rg_b_a", "rg_w_x", "rg_b_x", "rg_lambda", "hg_lb_logits", "hg_norm", "gla_w_in", "gla_w_out",
           "gla_w_gate_up", "gla_b_gate", "gla_norm")
ROW_ALIGN = 16


def _pack(arrays, lead=0):
    head = arrays[0].shape[:lead]
    flat = jnp.concatenate([a.reshape(head + (-1,)) for a in arrays], axis=lead)
    n = flat.shape[-1]
    quantum = LANES * ROW_ALIGN
    padded = -(-n // quantum) * quantum
    if padded != n:
        flat = jnp.pad(flat, [(0, 0)] * lead + [(0, padded - n)])
    return flat.reshape(head + (padded // LANES, LANES))


def _unpack(buf, shapes, lead=0):
    head = buf.shape[:lead]
    flat = buf.reshape(head + (-1,))
    out, off = [], 0
    for s in shapes:
        n = 1
        for v in s:
            n *= v
        out.append(lax.slice_in_dim(flat, off, off + n, axis=lead).reshape(head + tuple(s)))
        off += n
    return out


def _join_chips(gathered, axis):
    t = jnp.moveaxis(gathered, 0, axis)
    return t.reshape(t.shape[:axis] + (t.shape[axis] * t.shape[axis + 1],) + t.shape[axis + 2:])


def _split_chips(full, axis):
    s = full.shape
    t = full.reshape(s[:axis] + (N_CHIPS, s[axis] // N_CHIPS) + s[axis + 1:])
    return jnp.moveaxis(t, axis, 0)


_ANY = pl.BlockSpec(memory_space=pl.ANY)


def _place():
    return lax.axis_index("x"), lax.axis_index("y"), lax.axis_index("c")


def _into_slot(name, src, slot, n_slots, dtype, tm, layer=None):
    r, lanes = src.shape[-2:]
    tm = _row_tile(r, tm, ROW_ALIGN)

    def body(slot_ref, in_ref, o_ref):
        o_ref[...] = in_ref[...].astype(o_ref.dtype)

    if layer is None:
        in_spec = pl.BlockSpec((tm, lanes), lambda i, slot_ref: (i, 0))
    else:
        in_spec = pl.BlockSpec((None, tm, lanes), lambda i, slot_ref: (layer, i, 0))
    grid_spec = pltpu.PrefetchScalarGridSpec(
        num_scalar_prefetch=1, grid=(r // tm,), in_specs=[in_spec],
        out_specs=pl.BlockSpec((None, tm, lanes), lambda i, slot_ref: (slot_ref[0], i, 0)))
    return pl.pallas_call(
        body, name=name, grid_spec=grid_spec, out_shape=jax.ShapeDtypeStruct((n_slots, r, lanes), dtype),
        compiler_params=_params(("parallel",)),
    )(slot.reshape(1).astype(jnp.int32), src)


def _chip_peers():
    x, y, c = _place()
    return 2 * x + y, c, [(1 - x, y), (x, 1 - y), (1 - x, 1 - y)]


def _comm_call(name, body, ins, out_shapes, n_sems, aliases=None):
    return pl.pallas_call(
        body, name=name, in_specs=[_ANY] * len(ins), out_specs=[_ANY] * len(out_shapes), out_shape=out_shapes,
        input_output_aliases=aliases or {},
        scratch_shapes=[pltpu.SemaphoreType.DMA((n_sems,)), pltpu.SemaphoreType.DMA((n_sems,))],
    )(*ins)


def _gather_chips(name, bufs):
    n = len(bufs)

    def body(*refs):
        outs, send_sems, recv_sems = refs[n:2 * n], refs[2 * n], refs[2 * n + 1]
        x, y, c = _place()
        me, _, peers = _chip_peers()

        def rows(a, block, half):
            rh = outs[a].shape[1] // 2
            return outs[a].at[block, pl.ds(half * rh, rh)]

        def copy(a, j, block, half, to, sem):
            return pltpu.make_async_remote_copy(
                src_ref=rows(a, block, half), dst_ref=rows(a, block, half), send_sem=send_sems.at[sem],
                recv_sem=recv_sems.at[sem], device_id=to, device_id_type=MESH)

        def over_ici(a, j, block):
            px, py = peers[j]
            return copy(a, j, block, c, (px, py, c), 6 * a + j)

        def to_sibling(a, j, block, half):
            return copy(a, j, block, half, (x, y, 1 - c), 6 * a + 3 + j)

        sends = [over_ici(a, j, me) for a in range(n) for j in range(3)]
        for cp in sends:
            cp.start()
        for a in range(n):
            for j, (px, py) in enumerate(peers):
                over_ici(a, j, 2 * px + py).wait_recv()
                handed = to_sibling(a, j, 2 * px + py, c)
                handed.start()
                sends.append(handed)
        for a in range(n):
            for j, (px, py) in enumerate(peers):
                to_sibling(a, j, 2 * px + py, 1 - c).wait_recv()
        for cp in sends:
            cp.wait_send()

    shapes = [jax.ShapeDtypeStruct(b.shape, b.dtype) for b in bufs]
    return _comm_call(name, body, bufs, shapes, 6 * n, {a: a for a in range(n)})


_HBM = pl.BlockSpec(memory_space=pltpu.HBM)
_SEM = pl.BlockSpec(memory_space=pltpu.SEMAPHORE)
_EFFECT = pltpu.SideEffectType.DATAFLOW_SIDE_EFFECTING


def _half_rows(ref, block, half):
    rh = ref.shape[1] // 2
    return ref.at[block, pl.ds(half * rh, rh)]


def _gather_start(name, bufs, after):
    n = len(bufs)

    def body(*refs):
        ins, send_sems, recv_sems, token = refs[:n], refs[n + 1], refs[n + 2], refs[-1]
        me, c, peers = _chip_peers()
        for a in range(n):
            mine = _half_rows(ins[a], me, c)
            for j, (px, py) in enumerate(peers):
                pltpu.make_async_remote_copy(
                    src_ref=mine, dst_ref=mine, send_sem=send_sems.at[3 * a + j], recv_sem=recv_sems.at[3 * a + j],
                    device_id=(px, py, c), device_id_type=MESH).start()
        token[...] = jnp.zeros_like(token)

    out_shape = (pltpu.SemaphoreType.DMA((3 * n,)), pltpu.SemaphoreType.DMA((3 * n,)),
                 *[pltpu.HBM(b.shape, b.dtype) for b in bufs], jax.ShapeDtypeStruct((8, LANES), F32))
    return pl.pallas_call(
        body, name=name, out_shape=out_shape, in_specs=[_HBM] * n + [_ANY],
        out_specs=(_SEM, _SEM, *[_HBM] * n, pl.BlockSpec(memory_space=pltpu.VMEM)),
        input_output_aliases={a: 2 + a for a in range(n)},
        compiler_params=pltpu.CompilerParams(has_side_effects=_EFFECT),
    )(*[pltpu.with_memory_space_constraint(b, pltpu.HBM) for b in bufs], after)


def _gather_wait(name, bufs, send_sems, recv_sems, after):
    n = len(bufs)

    def body(*refs):
        ins, send_sems, recv_sems = refs[:n], refs[n], refs[n + 1]
        me, c, peers = _chip_peers()
        for a in range(n):
            for j, (px, py) in enumerate(peers):
                copy = pltpu.make_async_remote_copy(
                    src_ref=_half_rows(ins[a], me, c), dst_ref=_half_rows(ins[a], 2 * px + py, c),
                    send_sem=send_sems.at[3 * a + j], recv_sem=recv_sems.at[3 * a + j],
                    device_id=(px, py, c), device_id_type=MESH)
                copy.wait_send()
                copy.wait_recv()

    return pl.pallas_call(
        body, name=name, out_shape=tuple(pltpu.HBM(b.shape, b.dtype) for b in bufs),
        in_specs=[_HBM] * n + [_SEM, _SEM, _ANY], out_specs=tuple([_HBM] * n),
        input_output_aliases={a: a for a in range(n)},
        compiler_params=pltpu.CompilerParams(has_side_effects=_EFFECT),
    )(*bufs, send_sems, recv_sems, after)


def _hand_over(name, bufs):
    n = len(bufs)

    def body(*refs):
        outs, send_sems, recv_sems = refs[n:2 * n], refs[2 * n], refs[2 * n + 1]
        x, y, c = _place()
        _, _, peers = _chip_peers()

        def copy(a, j, half):
            px, py = peers[j]
            rows = _half_rows(outs[a], 2 * px + py, half)
            return pltpu.make_async_remote_copy(
                src_ref=rows, dst_ref=rows, send_sem=send_sems.at[3 * a + j], recv_sem=recv_sems.at[3 * a + j],
                device_id=(x, y, 1 - c), device_id_type=MESH)

        sends = [copy(a, j, c) for a in range(n) for j in range(3)]
        for cp in sends:
            cp.start()
        for a in range(n):
            for j in range(3):
                copy(a, j, 1 - c).wait_recv()
        for cp in sends:
            cp.wait_send()

    shapes = [jax.ShapeDtypeStruct(b.shape, b.dtype) for b in bufs]
    return _comm_call(name, body, bufs, shapes, 3 * n, {a: a for a in range(n)})


def _pair_gather(name, bufs):
    n = len(bufs)

    def body(*refs):
        ins, outs, send_sems, recv_sems = refs[:n], refs[n:2 * n], refs[2 * n], refs[2 * n + 1]
        x, y, c = _place()

        def copy(a, block):
            return pltpu.make_async_remote_copy(
                src_ref=ins[a].at[block], dst_ref=outs[a].at[block], send_sem=send_sems.at[a],
                recv_sem=recv_sems.at[a], device_id=(x, y, 1 - c), device_id_type=MESH)

        sends = [copy(a, c) for a in range(n)]
        for cp in sends:
            cp.start()
        for a in range(n):
            copy(a, 1 - c).wait_recv()
        for cp in sends:
            cp.wait_send()

    shapes = [jax.ShapeDtypeStruct(b.shape, b.dtype) for b in bufs]
    return _comm_call(name, body, bufs, shapes, n, {a: a for a in range(n)})


def _all_peers():
    x, y, c = _place()
    peers = []
    for mask in range(1, N_DEV):
        fx, fy, fc = (mask >> 2) & 1, (mask >> 1) & 1, mask & 1
        peers.append((jnp.where(fx, 1 - x, x), jnp.where(fy, 1 - y, y), jnp.where(fc, 1 - c, c)))
    return 4 * x + 2 * y + c, peers


def _reduce_copies(srcs, lands, send_sems, recv_sems):
    me, peers = _all_peers()
    sends, arrivals = [], []
    for a in range(len(srcs)):
        for j, (px, py, pc) in enumerate(peers):
            k = (N_DEV - 1) * a + j
            sends.append(pltpu.make_async_remote_copy(
                src_ref=srcs[a].at[2 * px + py, pc], dst_ref=lands[a].at[me], send_sem=send_sems.at[k],
                recv_sem=recv_sems.at[k], device_id=(px, py, pc), device_id_type=MESH))
            arrivals.append(pltpu.make_async_remote_copy(
                src_ref=srcs[a].at[2 * px + py, pc], dst_ref=lands[a].at[4 * px + 2 * py + pc],
                send_sem=send_sems.at[k], recv_sem=recv_sems.at[k], device_id=(px, py, pc), device_id_type=MESH))
    return sends, arrivals


def _reduce_direct(name, srcs, pin=None):
    n = len(srcs)
    extra = [] if pin is None else [pin]

    def body(*refs):
        ins, outs = refs[:n], refs[n + len(extra):2 * n + len(extra)]
        sends, arrivals = _reduce_copies(ins, outs, refs[-2], refs[-1])
        for cp in sends:
            cp.start()
        for cp in arrivals:
            cp.wait_recv()
        for cp in sends:
            cp.wait_send()

    shapes = [jax.ShapeDtypeStruct((N_DEV,) + s.shape[2:], s.dtype) for s in srcs]
    return _comm_call(name, body, list(srcs) + extra, shapes, (N_DEV - 1) * n)


def _reduce_start(name, srcs):
    n = len(srcs)
    lands = [lax.empty((N_DEV,) + s.shape[2:], s.dtype) for s in srcs]

    def body(*refs):
        sends, _ = _reduce_copies(refs[:n], refs[n:2 * n], refs[2 * n], refs[2 * n + 1])
        for cp in sends:
            cp.start()
        refs[-1][...] = jnp.zeros_like(refs[-1])

    bufs = list(srcs) + lands
    n_sems = (N_DEV - 1) * n
    out_shape = (pltpu.SemaphoreType.DMA((n_sems,)), pltpu.SemaphoreType.DMA((n_sems,)),
                 *[pltpu.HBM(b.shape, b.dtype) for b in bufs], jax.ShapeDtypeStruct((8, LANES), F32))
    return pl.pallas_call(
        body, name=name, out_shape=out_shape, in_specs=[_HBM] * (2 * n),
        out_specs=(_SEM, _SEM, *[_HBM] * (2 * n), pl.BlockSpec(memory_space=pltpu.VMEM)),
        input_output_aliases={a: 2 + a for a in range(2 * n)},
        compiler_params=pltpu.CompilerParams(has_side_effects=_EFFECT),
    )(*[pltpu.with_memory_space_constraint(b, pltpu.HBM) for b in bufs])


def _reduce_wait(name, srcs, lands, send_sems, recv_sems, after):
    n = len(srcs)

    def body(*refs):
        sends, arrivals = _reduce_copies(refs[:n], refs[n:2 * n], refs[2 * n], refs[2 * n + 1])
        for cp in sends:
            cp.wait_send()
        for cp in arrivals:
            cp.wait_recv()

    bufs = list(srcs) + list(lands)
    outs = pl.pallas_call(
        body, name=name, out_shape=tuple(pltpu.HBM(b.shape, b.dtype) for b in bufs),
        in_specs=[_HBM] * (2 * n) + [_SEM, _SEM, _ANY], out_specs=tuple([_HBM] * (2 * n)),
        input_output_aliases={a: a for a in range(2 * n)},
        compiler_params=pltpu.CompilerParams(has_side_effects=_EFFECT),
    )(*bufs, send_sems, recv_sems, after)
    return list(outs[n:])


def _reduce_sum(name, own, land, chip, core):
    n, rh, lanes = land.shape
    tm = _row_tile(rh, 1024, ROW_ALIGN)

    def body(idx_ref, own_ref, *rest):
        total = own_ref[...]
        for g_ref in rest[:-1]:
            total = total + g_ref[...].astype(F32)
        rest[-1][...] = total

    def block(k):
        return pl.BlockSpec((None, tm, lanes), lambda i, idx_ref: ((2 * idx_ref[0] + idx_ref[1] + k) % n, i, 0))

    grid_spec = pltpu.PrefetchScalarGridSpec(
        num_scalar_prefetch=1, grid=(rh // tm,),
        in_specs=[pl.BlockSpec((None, None, tm, lanes), lambda i, idx_ref: (idx_ref[0], idx_ref[1], i, 0))]
        + [block(k) for k in range(1, n)],
        out_specs=pl.BlockSpec((None, tm, lanes), lambda i, idx_ref: (idx_ref[1], i, 0)))
    return pl.pallas_call(
        body, name=name, grid_spec=grid_spec, out_shape=jax.ShapeDtypeStruct((2, rh, lanes), F32),
        compiler_params=_params(("parallel",)),
    )(jnp.stack([chip, core]).astype(jnp.int32), own, *[land] * (n - 1))


def _gather_all_start(name, buf):
    def body(in_ref, send_sems, recv_sems, out_ref, token):
        me, peers = _all_peers()
        for j, peer in enumerate(peers):
            pltpu.make_async_remote_copy(
                src_ref=in_ref.at[me], dst_ref=in_ref.at[me], send_sem=send_sems.at[j], recv_sem=recv_sems.at[j],
                device_id=peer, device_id_type=MESH).start()
        token[...] = jnp.zeros_like(token)

    n = N_DEV - 1
    return pl.pallas_call(
        body, name=name, in_specs=[_HBM],
        out_shape=(pltpu.SemaphoreType.DMA((n,)), pltpu.SemaphoreType.DMA((n,)), pltpu.HBM(buf.shape, buf.dtype),
                   jax.ShapeDtypeStruct((8, LANES), F32)),
        out_specs=(_SEM, _SEM, _HBM, pl.BlockSpec(memory_space=pltpu.VMEM)), input_output_aliases={0: 2},
        compiler_params=pltpu.CompilerParams(has_side_effects=_EFFECT),
    )(pltpu.with_memory_space_constraint(buf, pltpu.HBM))


def _gather_all_wait(name, buf, send_sems, recv_sems, after):
    def body(in_ref, send_sems, recv_sems, after_ref, out_ref):
        me, peers = _all_peers()
        for j, (px, py, pc) in enumerate(peers):
            copy = pltpu.make_async_remote_copy(
                src_ref=in_ref.at[me], dst_ref=in_ref.at[4 * px + 2 * py + pc], send_sem=send_sems.at[j],
                recv_sem=recv_sems.at[j], device_id=(px, py, pc), device_id_type=MESH)
            copy.wait_send()
            copy.wait_recv()

    return pl.pallas_call(
        body, name=name, in_specs=[_HBM, _SEM, _SEM, _ANY], out_shape=pltpu.HBM(buf.shape, buf.dtype),
        out_specs=_HBM, input_output_aliases={0: 0},
        compiler_params=pltpu.CompilerParams(has_side_effects=_EFFECT),
    )(buf, send_sems, recv_sems, after)


def _sum_blocks(name, stacked, tm):
    n, r, lanes = stacked.shape

    def body(in_ref, o_ref):
        acc = in_ref[0]
        for j in range(1, n):
            acc = acc + in_ref[j]
        o_ref[...] = acc

    return pl.pallas_call(
        body, name=name, grid=(r // tm,), in_specs=[pl.BlockSpec((n, tm, lanes), lambda i: (0, i, 0))],
        out_specs=pl.BlockSpec((tm, lanes), lambda i: (i, 0)), out_shape=jax.ShapeDtypeStruct((r, lanes), F32),
        compiler_params=_params(("parallel",)),
    )(stacked)


def _row_tile(rows, pref, align):
    best = None
    for t in range(align, min(rows, pref) + 1, align):
        if rows % t == 0:
            best = t
    assert best is not None, (rows, pref, align)
    return best


def _adam(name, w, g, m, v):
    rows, width = w.shape
    tm = _row_tile(rows, max(8, 4096 * LANES // width), 8)
    args = [(t, width, 0) for t in (w, g, m, v)]
    return _rowcall(name, _adam_fn, args, [], [(width, F32)] * 3, tm=tm)


def kernel(x, norm_mix, norm_mlp, norm_final, mlp_w1, mlp_w2, ab_w_in, ab_w_out, rg_conv_w, rg_conv_b, rg_w_a, rg_b_a, rg_w_x, rg_b_x, rg_lambda, hg_lb_logits, hg_norm, gla_w_in, gla_w_out, gla_w_gate_up, gla_b_gate, gla_norm, loss_target, m_norm_mix, m_norm_mlp, m_norm_final, m_mlp_w1, m_mlp_w2, m_ab_w_in, m_ab_w_out, m_rg_conv_w, m_rg_conv_b, m_rg_w_a, m_rg_b_a, m_rg_w_x, m_rg_b_x, m_rg_lambda, m_hg_lb_logits, m_hg_norm, m_gla_w_in, m_gla_w_out, m_gla_w_gate_up, m_gla_b_gate, m_gla_norm, v_norm_mix, v_norm_mlp, v_norm_final, v_mlp_w1, v_mlp_w2, v_ab_w_in, v_ab_w_out, v_rg_conv_w, v_rg_conv_b, v_rg_w_a, v_rg_b_a, v_rg_w_x, v_rg_b_x, v_rg_lambda, v_hg_lb_logits, v_hg_norm, v_gla_w_in, v_gla_w_out, v_gla_w_gate_up, v_gla_b_gate, v_gla_norm):
    w = dict(norm_mix=norm_mix, norm_mlp=norm_mlp, norm_final=norm_final, mlp_w1=mlp_w1, mlp_w2=mlp_w2, ab_w_in=ab_w_in, ab_w_out=ab_w_out, rg_conv_w=rg_conv_w, rg_conv_b=rg_conv_b, rg_w_a=rg_w_a, rg_b_a=rg_b_a, rg_w_x=rg_w_x, rg_b_x=rg_b_x, rg_lambda=rg_lambda, hg_lb_logits=hg_lb_logits, hg_norm=hg_norm, gla_w_in=gla_w_in, gla_w_out=gla_w_out, gla_w_gate_up=gla_w_gate_up, gla_b_gate=gla_b_gate, gla_norm=gla_norm)
    m = dict(norm_mix=m_norm_mix, norm_mlp=m_norm_mlp, norm_final=m_norm_final, mlp_w1=m_mlp_w1, mlp_w2=m_mlp_w2, ab_w_in=m_ab_w_in, ab_w_out=m_ab_w_out, rg_conv_w=m_rg_conv_w, rg_conv_b=m_rg_conv_b, rg_w_a=m_rg_w_a, rg_b_a=m_rg_b_a, rg_w_x=m_rg_w_x, rg_b_x=m_rg_b_x, rg_lambda=m_rg_lambda, hg_lb_logits=m_hg_lb_logits, hg_norm=m_hg_norm, gla_w_in=m_gla_w_in, gla_w_out=m_gla_w_out, gla_w_gate_up=m_gla_w_gate_up, gla_b_gate=m_gla_b_gate, gla_norm=m_gla_norm)
    v = dict(norm_mix=v_norm_mix, norm_mlp=v_norm_mlp, norm_final=v_norm_final, mlp_w1=v_mlp_w1, mlp_w2=v_mlp_w2, ab_w_in=v_ab_w_in, ab_w_out=v_ab_w_out, rg_conv_w=v_rg_conv_w, rg_conv_b=v_rg_conv_b, rg_w_a=v_rg_w_a, rg_b_a=v_rg_b_a, rg_w_x=v_rg_w_x, rg_b_x=v_rg_b_x, rg_lambda=v_rg_lambda, hg_lb_logits=v_hg_lb_logits, hg_norm=v_hg_norm, gla_w_in=v_gla_w_in, gla_w_out=v_gla_w_out, gla_w_gate_up=v_gla_w_gate_up, gla_b_gate=v_gla_b_gate, gla_norm=v_gla_norm)
    chip = 2 * lax.axis_index("x") + lax.axis_index("y")
    core = lax.axis_index("c")
    sharded_shapes = [w[n].shape for n in SMALL_SHARDED]

    slots = [_into_slot(f"cast_{n}{layer}", w[n], chip, N_CHIPS, BF16, 512, layer) for n, layer in MATRICES]
    early = [i for i, (n, _) in enumerate(MATRICES) if n in EARLY_MATRICES]
    rest = [i for i in range(len(MATRICES)) if i not in early]

    def named(indices, arrays):
        big = {}
        for i, t in zip(indices, arrays):
            big.setdefault(MATRICES[i][0], []).append(t)
        return {n: (v if n in ("mlp_w1", "mlp_w2") else v[0]) for n, v in big.items()}

    vectors = _pack([w[n] for n in SMALL_SHARDED])
    vectors = _into_slot("place_vectors", vectors, chip, N_CHIPS, F32, vectors.shape[0])
    *gathered, vectors = _gather_chips("gather_early", [slots[i] for i in early] + [vectors])
    send_sems, recv_sems, *in_flight, token = _gather_start("gather_rest_start", [slots[i] for i in rest], gathered[0])

    def late_weights(after):
        landed = _gather_wait("gather_rest_wait", in_flight, send_sems, recv_sems, after)
        return _prepare_matrices(named(rest, _hand_over("gather_rest_share", list(landed))))

    big = named(early, gathered)
    small_all = _unpack(vectors, sharded_shapes, lead=1)
    full = {n: w[n] for n in SMALL_REPLICATED}
    for n, t in zip(SMALL_SHARDED, small_all):
        full[n] = _join_chips(t, t.ndim - 2)

    def halves(t):
        return t.reshape(N_CHIPS, 2, t.shape[1] // 2, t.shape[2])

    in_flight_grads = {}

    def emit(tag, arrays32, arrays16):
        n = len(arrays16)
        send, recv, *rest = _reduce_start(f"reduce_{tag}_start", [halves(t) for t in arrays16])
        in_flight_grads[tag] = ([halves(t) for t in arrays32], rest[:n], rest[n:2 * n], send, recv)
        return rest[-1]

    loss_part, grad_x, g_kernel = _local_step(
        x[0], loss_target[0], _prepare_weights(big, full), token, late_weights, emit)
    g_big, g_full = _finish_grads(g_kernel)

    small_names = SMALL_REPLICATED + SMALL_SHARDED
    reduced_shapes = [g_full[n].shape for n in small_names] + [loss_part.shape]
    g_small = _pack([g_full[n] for n in small_names] + [loss_part])
    device = 2 * chip + core
    g_small = _into_slot("place_small", g_small, device, N_DEV, F32, g_small.shape[0])
    small_send, small_recv, small_in_flight, small_token = _gather_all_start("reduce_small_start", g_small)

    mine = {}
    for tag, (own, srcs, lands, send, recv) in in_flight_grads.items():
        landed = _reduce_wait(f"reduce_{tag}_wait", srcs, lands, send, recv, small_token)
        mine[tag] = [_reduce_sum(f"reduce_add_{tag}{i}", o, f, chip, core) for i, (o, f) in enumerate(zip(own, landed))]
    ordered = [mine["mlp0"][0], mine["mlp1"][0], mine["mlp0"][1], mine["mlp1"][1], mine["ab"][0], mine["mlp0"][2],
               *mine["gla"]]
    reduced = [t.reshape(2 * t.shape[1], t.shape[2]) for t in _pair_gather("reduce_share", ordered)]
    by_name = {n: [] for n, _ in MATRICES}
    for (n, _), t in zip(MATRICES, reduced):
        by_name[n].append(t)
    grads = {n: jnp.stack(v) for n, v in by_name.items()}

    g_small_all = _gather_all_wait("reduce_small_wait", small_in_flight, small_send, small_recv, reduced[0])
    g_small_red = _sum_blocks("reduce_small_add", g_small_all, g_small_all.shape[1])
    *small_red, loss_sum = _unpack(g_small_red, reduced_shapes)
    loss = loss_sum[0, 0]
    g_small_full = dict(zip(small_names, small_red))
    for n in SMALL_REPLICATED:
        grads[n] = g_small_full[n]
    for n in SMALL_SHARDED:
        width = w[n].shape[-1]
        grads[n] = lax.dynamic_slice_in_dim(g_small_full[n], chip * width, width, axis=g_small_full[n].ndim - 1)

    delta, new_m, new_v = {}, {}, {}
    for n in by_name:
        flat = [t.reshape(-1, t.shape[-1]) for t in (w[n], grads[n], m[n], v[n])]
        for dst, t in zip((delta, new_m, new_v), _adam(f"adam_{n}", *flat)):
            dst[n] = t.reshape(w[n].shape)
    small_shapes = [w[n].shape for n in small_names]
    packs = [_pack([src[n] for n in small_names]) for src in (w, grads, m, v)]
    d_small, m_small, v_small = _adam("adam_small", *packs)
    for dst, buf in ((delta, d_small), (new_m, m_small), (new_v, v_small)):
        dst.update(zip(small_names, _unpack(buf, small_shapes)))

    return (loss, grad_x[None], *[grads[n] for n in WEIGHTS], *[delta[n] for n in WEIGHTS],
            *[new_m[n] for n in WEIGHTS], *[new_v[n] for n in WEIGHTS])
```

```python
import functools

import jax
import jax.numpy as jnp
from jax import lax
from jax.experimental import pallas as pl
from jax.experimental.pallas import tpu as pltpu

F32 = jnp.float32
BF16 = jnp.bfloat16
MESH = pl.DeviceIdType.MESH

LANES = 128
CHUNK = 64
ATTN_SUB = 4
EPS = 1e-6
RG_C = 8.0
N_CHIPS = 4
N_DEV = 8
GLA_IN_WIDTH = 3104
GLA_IN_PAD = 3200
VMEM_LIMIT = 56 * 1024 * 1024

ADAM_LR = 0.001
ADAM_B1 = 0.9
ADAM_B2 = 0.999
ADAM_EPS = 1e-08
ADAM_WD = 0.01
ADAM_STEP = 10


def _raw_dot(a, b, ca, cb):
    return lax.dot_general(a.astype(BF16), b.astype(BF16), (((ca,), (cb,)), ((), ())),
                           preferred_element_type=F32)


def _raw_nn(a, b):
    return _raw_dot(a, b, 1, 0)


def _raw_nt(a, b):
    return _raw_dot(a, b, 1, 1)


def _raw_tn(a, b):
    return _raw_dot(a, b, 0, 0)


@jax.custom_vjp
def _dot_nn(a, b):
    return _raw_nn(a, b)


def _dot_nn_fwd(a, b):
    return _raw_nn(a, b), (a, b)


def _dot_nn_bwd(res, g):
    a, b = res
    return _raw_nt(g, b), _raw_tn(a, g)


_dot_nn.defvjp(_dot_nn_fwd, _dot_nn_bwd)


@jax.custom_vjp
def _dot_nt(a, b):
    return _raw_nt(a, b)


def _dot_nt_fwd(a, b):
    return _raw_nt(a, b), (a, b)


def _dot_nt_bwd(res, g):
    a, b = res
    return _raw_nn(g, b), _raw_tn(g, a)


_dot_nt.defvjp(_dot_nt_fwd, _dot_nt_bwd)


@jax.custom_vjp
def _dot_tn(a, b):
    return _raw_tn(a, b)


def _dot_tn_fwd(a, b):
    return _raw_tn(a, b), (a, b)


def _dot_tn_bwd(res, g):
    a, b = res
    return _raw_nt(b, g), _raw_nn(a, g)


_dot_tn.defvjp(_dot_tn_fwd, _dot_tn_bwd)


def _tile(n, pref):
    if n <= pref:
        return n
    t = (pref // LANES) * LANES
    while t > LANES and n % t:
        t -= LANES
    assert n % t == 0, (n, pref)
    return t


def _params(sem):
    return pltpu.CompilerParams(dimension_semantics=sem, vmem_limit_bytes=VMEM_LIMIT)


def _rowcall(name, fn, rows, pars, row_outs, par_outs=(), tm=512, pin=None):
    if pin is not None:
        inner, pars = fn, list(pars) + [pin]
        fn = lambda *vals: inner(*vals[:-1])
    n_rows = rows[0][0].shape[0]
    tm = min(tm, n_rows)
    assert n_rows % tm == 0
    n_r, n_p, n_ro = len(rows), len(pars), len(row_outs)

    def body(*refs):
        vals = [r[...].astype(F32) for r in refs[:n_r + n_p]]
        outs = fn(*vals)
        o_refs = refs[n_r + n_p:n_r + n_p + n_ro]
        po_refs = refs[n_r + n_p + n_ro:]
        for o_ref, val in zip(o_refs, outs[:n_ro]):
            o_ref[...] = val.astype(o_ref.dtype)
        first = pl.program_id(0) == 0
        for po_ref, val in zip(po_refs, outs[n_ro:]):
            @pl.when(first)
            def _():
                po_ref[...] = val

            @pl.when(jnp.logical_not(first))
            def _():
                po_ref[...] += val

    def const_map(nd):
        return lambda i: (0,) * nd

    def row_spec(w, cb):
        return pl.BlockSpec((tm, w), lambda i: (i, cb))

    in_specs = [row_spec(w, cb) for _, w, cb in rows]
    in_specs += [pl.BlockSpec(p.shape, const_map(p.ndim)) for p in pars]
    out_specs = [pl.BlockSpec((tm, w), lambda i: (i, 0)) for w, _ in row_outs]
    out_specs += [pl.BlockSpec(tuple(s), const_map(len(s))) for s in par_outs]
    out_shape = [jax.ShapeDtypeStruct((n_rows, w), dt) for w, dt in row_outs]
    out_shape += [jax.ShapeDtypeStruct(tuple(s), F32) for s in par_outs]
    return pl.pallas_call(
        body, name=name, grid=(n_rows // tm,), in_specs=in_specs, out_specs=out_specs, out_shape=out_shape,
        compiler_params=_params(("arbitrary",) if par_outs else ("parallel",)),
    )(*[r[0] for r in rows], *pars)


def _vjp_of(fn, n_prim, n_out, n_par, n_pass=0):
    def bwd(*args):
        prim = args[:n_prim]
        cts = args[n_prim:n_prim + n_out]
        passes = args[n_prim + n_out:n_prim + n_out + 2 * n_pass]
        pars = args[n_prim + n_out + 2 * n_pass:]
        _, vjp = jax.vjp(fn, *prim, *pars)
        grads = vjp(tuple(cts))
        sums = tuple(passes[2 * i] + passes[2 * i + 1] for i in range(n_pass))
        return tuple(grads[:n_prim]) + sums + tuple(grads[n_prim:])
    return bwd


def _mm(name, a, b, mode="nn", extras=(), epi=None, out_dtypes=(F32,), a_pro=None, out_split=None,
        epi_pars=(), row_sum=False, pin=None, tm=1024, tn=1024, tk=1024):
    split = b.shape[0] if b.ndim == 3 else None
    b_rows, b_cols = b.shape[-2:]
    if mode == "nn":
        (m, k), n = a.shape, b_cols * (split or 1)
    elif mode == "nt":
        (m, k), n = a.shape, b_rows
        assert k == b_cols * (split or 1)
    else:
        assert split is None
        (k, m), n = a.shape, b_cols
    tm, tk = _tile(m, tm), _tile(k, tk)
    tn = _tile(n // out_split, tn) if out_split else _tile(n, tn)
    if split and mode == "nn":
        tn = _tile(b_cols, tn)
    if split and mode == "nt":
        tk = _tile(b_cols, tk)
    nk = k // tk
    raw = {"nn": _raw_nn, "nt": _raw_nt, "tn": _raw_tn}[mode]
    n_e, n_p, n_o = len(extras), len(epi_pars), len(out_dtypes)
    n_in = n_e + n_p + (0 if pin is None else 1)
    if epi is None:
        epi = lambda acc: (acc,)

    def body(a_ref, b_ref, *rest):
        e_refs, p_refs, o_refs = rest[:n_e], rest[n_e:n_e + n_p], rest[n_in:n_in + n_o]
        kk = pl.program_id(2)
        a_tile = a_ref[...] if a_pro is None else a_pro(a_ref[...].astype(F32))
        part = raw(a_tile, b_ref[...])

        def finish(total):
            res = epi(total, *[e[...].astype(F32) for e in e_refs], *[p[...] for p in p_refs])
            for o_ref, r in zip(o_refs, res):
                o_ref[...] = r.astype(o_ref.dtype)
            if row_sum:
                rest[n_in + n_o][...] = res[n_o]

        if nk == 1:
            finish(part)
            return
        acc = rest[-1]

        @pl.when(kk == 0)
        def _():
            acc[...] = part

        @pl.when((kk > 0) & (kk < nk - 1))
        def _():
            acc[...] += part

        @pl.when(kk == nk - 1)
        def _():
            finish(acc[...] + part)

    a_spec = pl.BlockSpec((tk, tm), lambda i, j, kk: (kk, i)) if mode == "tn" else pl.BlockSpec((tm, tk), lambda i, j, kk: (i, kk))
    if split and mode == "nn":
        per = b_cols // tn
        b_spec = pl.BlockSpec((None, tk, tn), lambda i, j, kk: (j // per, kk, j % per))
    elif split:
        per = b_cols // tk
        b_spec = pl.BlockSpec((None, tn, tk), lambda i, j, kk: (kk // per, j, kk % per))
    elif mode == "nt":
        b_spec = pl.BlockSpec((tn, tk), lambda i, j, kk: (j, kk))
    else:
        b_spec = pl.BlockSpec((tk, tn), lambda i, j, kk: (kk, j))
    mn_spec = pl.BlockSpec((tm, tn), lambda i, j, kk: (i, j))
    if out_split:
        assert not extras
        per_out = n // out_split // tn
        out_spec = pl.BlockSpec((None, tm, tn), lambda i, j, kk: (j // per_out, i, j % per_out))
        out_shapes = [jax.ShapeDtypeStruct((out_split, m, n // out_split), dt) for dt in out_dtypes]
    else:
        out_spec = mn_spec
        out_shapes = [jax.ShapeDtypeStruct((m, n), dt) for dt in out_dtypes]
    out_specs = [out_spec] * n_o
    if row_sum:
        out_specs.append(pl.BlockSpec((None, 1, tn), lambda i, j, kk: (i, 0, j)))
        out_shapes.append(jax.ShapeDtypeStruct((m // tm, 1, n), F32))
    in_specs = [a_spec, b_spec] + [mn_spec] * n_e
    in_specs += [pl.BlockSpec(p.shape, functools.partial(lambda i, j, kk, nd: (0,) * nd, nd=p.ndim)) for p in epi_pars]
    in_specs += [] if pin is None else [pl.BlockSpec(memory_space=pl.ANY)]
    outs = pl.pallas_call(
        body, name=name, grid=(m // tm, n // tn, nk), in_specs=in_specs, out_specs=out_specs, out_shape=out_shapes,
        scratch_shapes=[pltpu.VMEM((tm, tn), F32)] if nk > 1 else [],
        compiler_params=_params(("parallel", "parallel", "arbitrary")),
    )(a, b, *extras, *epi_pars, *([] if pin is None else [pin]))
    return outs[0] if len(outs) == 1 else outs


def _sigmoid(x):
    return jax.nn.sigmoid(x)


def _silu(x):
    return x * _sigmoid(x)


def _softplus(x):
    return jnp.maximum(x, 0.0) + jnp.log1p(jnp.exp(-jnp.abs(x)))


def _rmsnorm_fn(x, gain):
    return (x * lax.rsqrt(jnp.mean(x * x, axis=-1, keepdims=True) + EPS) * gain,)


def _head_norm(o, gain, n_heads):
    w = o.shape[-1] // n_heads
    parts = []
    for h in range(n_heads):
        oh = o[:, h * w:(h + 1) * w]
        parts.append(oh * lax.rsqrt(jnp.mean(oh * oh, axis=-1, keepdims=True) + EPS))
    return jnp.concatenate(parts, axis=-1) * gain


@jax.custom_jvp
def _neg_expm1(x):
    u = jnp.exp(x)
    is_one = u == 1.0
    return jnp.where(is_one, -x, (1.0 - u) * x / jnp.log(jnp.where(is_one, 2.0, u)))


@_neg_expm1.defjvp
def _neg_expm1_jvp(primals, tangents):
    (x,), (t,) = primals, tangents
    return _neg_expm1(x), -jnp.exp(x) * t


def _rg_gates_fn(xc, wa, wx, ba, bx, lam):
    outs = []
    for d in range(2):
        r = _sigmoid(_dot_nn(xc, wa[d]) + ba[d:d + 1])
        i = _sigmoid(_dot_nn(xc, wx[d]) + bx[d:d + 1])
        log_a = -RG_C * r * _softplus(-lam[d:d + 1])
        outs.append(jnp.exp(log_a))
        outs.append(jnp.sqrt(_neg_expm1(2.0 * log_a)) * (i * xc))
    return tuple(outs)


def _hg_pre_fn(q, f_f, f_b, logits):
    mx = jnp.maximum(logits[0:1], logits[1:2])
    e0 = jnp.exp(logits[0:1] - mx)
    e1 = jnp.exp(logits[1:2] - mx)
    lb = e0 / (e0 + e1)
    outs = [_silu(q)]
    for f in (f_f, f_b):
        outs.append((1.0 - lb) * _sigmoid(-f))
        outs.append(jnp.log(lb + (1.0 - lb) * _sigmoid(f)))
    return tuple(outs)


def _post0_fn(hs, ga, o, g, gain):
    ya = hs * jax.nn.gelu(ga, approximate=True)
    yb = _head_norm(o, gain, 4) * _silu(g)
    return (jnp.concatenate([ya, yb], axis=-1),)


def _post0_fwd_fn(h_f, h_b, ga, o_f, o_b, g, gain):
    return _post0_fn(h_f + h_b, ga, o_f + o_b, g, gain)


def _post0_bwd_fn(h_f, h_b, ga, o_f, o_b, g, dmix, gain):
    _, vjp = jax.vjp(_post0_fn, h_f + h_b, ga, o_f + o_b, g, gain)
    return vjp((dmix,))


def _gla_pre_fn(q, lr, w_up, b_gate):
    outs = [q * (128.0 ** -0.5)]
    for d in range(2):
        z = _dot_nn(lr, w_up[d]) + b_gate[d:d + 1]
        outs.append(-_softplus(-z) * (1.0 / 16.0))
    return tuple(outs)


def _gla_post_fn(o, r, gain):
    return (_head_norm(o, gain, 4) * _silu(r),)


def _gla_post_fwd_fn(o_f, o_b, r, gain):
    return _gla_post_fn(o_f + o_b, r, gain)


def _gla_post_bwd_fn(o_f, o_b, r, dmix, gain):
    _, vjp = jax.vjp(_gla_post_fn, o_f + o_b, r, gain)
    return vjp((dmix,))


def _relu2_bwd_epi(acc, hid):
    return (acc * 2.0 * jnp.maximum(hid, 0.0),)


def _relu2(x):
    r = jnp.maximum(x, 0.0)
    return r * r


def _add_epi(acc, res):
    return (acc + res,)


def _loss_head_fn(h, target, gain):
    def f(h, gain):
        y = _rmsnorm_fn(h, gain)[0]
        err = y - target
        return 0.5 * jnp.sum(jnp.mean(err * err, axis=-1, keepdims=True))
    loss, (dh, dgain) = jax.value_and_grad(f, argnums=(0, 1))(h, gain)
    return dh, dh, jnp.full((1, LANES), loss, F32), dgain


def _adam_fn(w, g, m, v):
    m2 = ADAM_B1 * m + (1.0 - ADAM_B1) * g
    v2 = ADAM_B2 * v + (1.0 - ADAM_B2) * (g * g)
    m_hat = m2 / (1.0 - ADAM_B1 ** ADAM_STEP)
    v_hat = v2 / (1.0 - ADAM_B2 ** ADAM_STEP)
    delta = -ADAM_LR * (m_hat / (jnp.sqrt(v_hat) + ADAM_EPS) + ADAM_WD * w)
    return delta, m2, v2


def _shifted(x, t_idx, off):
    n = x.shape[0]
    rolled = pltpu.roll(x, (-off) % n, 0)
    valid = (t_idx + off >= 0) & (t_idx + off < n)
    return jnp.where(valid, rolled, 0.0)


def _conv_fwd(name, src, colblock, w, b):
    n_rows, width = src.shape[0], w.shape[1]

    def body(x_ref, w_ref, b_ref, o_ref):
        x = x_ref[...]
        t_idx = lax.broadcasted_iota(jnp.int32, x.shape, 0)
        acc = b_ref[...] + w_ref[2:3, :] * x
        acc += w_ref[0:1, :] * _shifted(x, t_idx, -2)
        acc += w_ref[1:2, :] * _shifted(x, t_idx, -1)
        acc += w_ref[3:4, :] * _shifted(x, t_idx, 1)
        o_ref[...] = acc

    nb = width // LANES
    return pl.pallas_call(
        body, name=name, grid=(nb,),
        in_specs=[pl.BlockSpec((n_rows, LANES), lambda j: (0, colblock * nb + j)),
                  pl.BlockSpec((4, LANES), lambda j: (0, j)), pl.BlockSpec((1, LANES), lambda j: (0, j))],
        out_specs=pl.BlockSpec((n_rows, LANES), lambda j: (0, j)),
        out_shape=jax.ShapeDtypeStruct((n_rows, width), F32),
        compiler_params=_params(("parallel",)),
    )(src, w, b)


def _conv_bwd(name, src, colblock, w, d):
    n_rows, width = src.shape[0], w.shape[1]

    def body(x_ref, w_ref, d_ref, dx_ref, dw_ref, db_ref):
        x = x_ref[...]
        g = d_ref[...]
        t_idx = lax.broadcasted_iota(jnp.int32, x.shape, 0)
        dx = w_ref[2:3, :] * g
        dx += w_ref[0:1, :] * _shifted(g, t_idx, 2)
        dx += w_ref[1:2, :] * _shifted(g, t_idx, 1)
        dx += w_ref[3:4, :] * _shifted(g, t_idx, -1)
        dx_ref[...] = dx.astype(dx_ref.dtype)
        dw_ref[0:1, :] = jnp.sum(g * _shifted(x, t_idx, -2), axis=0, keepdims=True)
        dw_ref[1:2, :] = jnp.sum(g * _shifted(x, t_idx, -1), axis=0, keepdims=True)
        dw_ref[2:3, :] = jnp.sum(g * x, axis=0, keepdims=True)
        dw_ref[3:4, :] = jnp.sum(g * _shifted(x, t_idx, 1), axis=0, keepdims=True)
        db_ref[...] = jnp.sum(g, axis=0, keepdims=True)

    nb = width // LANES
    return pl.pallas_call(
        body, name=name, grid=(nb,),
        in_specs=[pl.BlockSpec((n_rows, LANES), lambda j: (0, colblock * nb + j)),
                  pl.BlockSpec((4, LANES), lambda j: (0, j)),
                  pl.BlockSpec((n_rows, LANES), lambda j: (0, j))],
        out_specs=[pl.BlockSpec((n_rows, LANES), lambda j: (0, j)), pl.BlockSpec((4, LANES), lambda j: (0, j)),
                   pl.BlockSpec((1, LANES), lambda j: (0, j))],
        out_shape=[jax.ShapeDtypeStruct((n_rows, width), BF16), jax.ShapeDtypeStruct((4, width), F32),
                   jax.ShapeDtypeStruct((1, width), F32)],
        compiler_params=_params(("parallel",)),
    )(src, w, d)


SUBLANES = 8
SCAN_UNROLL = 8


def _shift_rows(x, d, fill):
    n = x.shape[0]
    t = lax.broadcasted_iota(jnp.int32, x.shape, 0)
    valid = (t >= d) if d > 0 else (t < n + d)
    return jnp.where(valid, pltpu.roll(x, d % n, 0), fill)


def _tile_scan(a, u, reverse):
    d = 1
    while d < a.shape[0]:
        s = -d if reverse else d
        a_sh, u_sh = _shift_rows(a, s, 1.0), _shift_rows(u, s, 0.0)
        u = u + a * u_sh
        a = a * a_sh
        d *= 2
    return a, u


def _edge_row(x, reverse):
    return x[0:1, :] if reverse else x[SUBLANES - 1:SUBLANES, :]


def _scan_specs(n_rows, n):
    return [pl.BlockSpec((n_rows, LANES), lambda j: (0, j))] * n


def _scan_tile(a_ref, u_ref, h_ref, i, carry, reverse):
    n_tiles = a_ref.shape[0] // SUBLANES
    tile = (n_tiles - 1 - i) if reverse else i
    rows = pl.ds(pl.multiple_of(tile * SUBLANES, SUBLANES), SUBLANES)
    acc_a, acc_u = _tile_scan(a_ref[rows, :], u_ref[rows, :], reverse)
    h = acc_u + acc_a * carry
    h_ref[rows, :] = h
    return _edge_row(h, reverse)


def _scan_fwd(name, a_f, u_f, a_b, u_b):
    n_rows, width = a_f.shape

    def body(af_ref, uf_ref, ab_ref, ub_ref, hf_ref, hb_ref):
        def step(i, carry):
            return (_scan_tile(af_ref, uf_ref, hf_ref, i, carry[0], False),
                    _scan_tile(ab_ref, ub_ref, hb_ref, i, carry[1], True))
        zero = jnp.zeros((1, LANES), F32)
        lax.fori_loop(0, n_rows // SUBLANES, step, (zero, zero), unroll=SCAN_UNROLL)

    return pl.pallas_call(
        body, name=name, grid=(width // LANES,), in_specs=_scan_specs(n_rows, 4), out_specs=_scan_specs(n_rows, 2),
        out_shape=[jax.ShapeDtypeStruct((n_rows, width), F32)] * 2, compiler_params=_params(("parallel",)),
    )(a_f, u_f, a_b, u_b)


def _scan_bwd_tile(a_ref, h_ref, dh_ref, du_ref, da_ref, i, carry, reverse):
    n_rows = a_ref.shape[0]
    n_tiles = n_rows // SUBLANES
    against = not reverse
    one = -1 if against else 1
    g_in, a_edge = carry
    tile = (n_tiles - 1 - i) if against else i
    start = pl.multiple_of(tile * SUBLANES, SUBLANES)
    rows = pl.ds(start, SUBLANES)
    a_tile = a_ref[rows, :]
    coeff = _shift_rows(a_tile, one, a_edge)
    acc_a, acc_u = _tile_scan(coeff, dh_ref[rows, :], against)
    g = acc_u + acc_a * g_in
    du_ref[rows, :] = g
    outside = (start + SUBLANES) if reverse else (start - 1)
    inside = (outside >= 0) & (outside < n_rows)
    h_edge = jnp.where(inside, h_ref[pl.ds(jnp.clip(outside, 0, n_rows - 1), 1), :], 0.0)
    da_ref[rows, :] = g * _shift_rows(h_ref[rows, :], -one, h_edge)
    return _edge_row(g, against), _edge_row(a_tile, against)


def _scan_bwd(name, a_f, h_f, a_b, h_b, dh):
    n_rows, width = a_f.shape

    def body(af_ref, hf_ref, ab_ref, hb_ref, dh_ref, duf_ref, daf_ref, dub_ref, dab_ref):
        def step(i, carry):
            return (_scan_bwd_tile(af_ref, hf_ref, dh_ref, duf_ref, daf_ref, i, carry[0], False),
                    _scan_bwd_tile(ab_ref, hb_ref, dh_ref, dub_ref, dab_ref, i, carry[1], True))
        zero = jnp.zeros((1, LANES), F32)
        lax.fori_loop(0, n_rows // SUBLANES, step, ((zero, zero), (zero, zero)), unroll=SCAN_UNROLL)

    return pl.pallas_call(
        body, name=name, grid=(width // LANES,), in_specs=_scan_specs(n_rows, 5), out_specs=_scan_specs(n_rows, 4),
        out_shape=[jax.ShapeDtypeStruct((n_rows, width), F32)] * 4, compiler_params=_params(("parallel",)),
    )(a_f, h_f, a_b, h_b, dh)


def _tri_mask(c, reverse):
    row = lax.broadcasted_iota(jnp.int32, (c, c), 0)
    col = lax.broadcasted_iota(jnp.int32, (c, c), 1)
    return (col >= row) if reverse else (col <= row)


def _cumsum_rows(x, reverse):
    tri = _tri_mask(x.shape[0], reverse).astype(BF16)
    hi = x.astype(BF16)
    rest = x - hi.astype(F32)
    mid = rest.astype(BF16)
    lo = (rest - mid.astype(F32)).astype(BF16)
    return _raw_nn(tri, hi) + _raw_nn(tri, mid) + _raw_nn(tri, lo)


@functools.partial(jax.custom_vjp, nondiff_argnums=(1,))
def _cumsum(x, reverse):
    return _cumsum_rows(x, reverse)


def _cumsum_fwd(x, reverse):
    return _cumsum_rows(x, reverse), None


def _cumsum_bwd(reverse, _, g):
    return (_cumsum_rows(g, not reverse),)


_cumsum.defvjp(_cumsum_fwd, _cumsum_bwd)


def _chunks_fn(qs, ks, vs, lfs, sts, reverses):
    n, c = len(qs), qs[0].shape[0]
    every = range(n)
    tris = [_tri_mask(c, r) for r in reverses]
    cums = [_cumsum(lfs[i], reverses[i]) for i in every]
    rid = lax.broadcasted_iota(jnp.int32, cums[0].shape, 0)

    def pick(cum, r):
        return jnp.sum(jnp.where(rid == r, cum, 0.0), axis=0, keepdims=True)

    refs = [pick(cums[i], (c - 1 - c // 2) if reverses[i] else c // 2) for i in every]
    lasts = [pick(cums[i], 0 if reverses[i] else c - 1) for i in every]
    q_in = [qs[i] * jnp.exp(cums[i] - refs[i]) for i in every]
    k_in = [ks[i] * jnp.exp(refs[i] - cums[i]) for i in every]
    scores = [jnp.where(tris[i], _dot_nt(q_in[i], k_in[i]), 0.0) for i in every]
    o_intra = [_dot_nn(scores[i], vs[i]) for i in every]
    q_out = [qs[i] * jnp.exp(cums[i]) for i in every]
    o_inter = [_dot_nt(q_out[i], sts[i]) for i in every]
    k_state = [ks[i] * jnp.exp(lasts[i] - cums[i]) for i in every]
    upd = [_dot_tn(vs[i], k_state[i]) for i in every]
    st_new = [sts[i] * jnp.exp(lasts[i]) + upd[i] for i in every]
    return [o_intra[i] + o_inter[i] for i in every], st_new


def _attn_fwd(name, q, k_f, k_b, v, lf_f, lf_b, n_heads, dk, dv):
    n_rows = q[0].shape[0]
    n_chunks = n_rows // CHUNK
    n_steps = n_chunks // ATTN_SUB
    wk, wv = n_heads * dk, n_heads * dv

    def spec(width, off, rev):
        return pl.BlockSpec((CHUNK * ATTN_SUB, width), lambda n: ((n_steps - 1 - n) if rev else n, off))

    def sspec(rev):
        return pl.BlockSpec((ATTN_SUB, n_heads, dv, dk), lambda n: ((n_steps - 1 - n) if rev else n, 0, 0, 0))

    def body(qf, kf, vf, lff, qb, kb, vb, lfb, of_ref, ob_ref, sf_ref, sb_ref, st):
        @pl.when(pl.program_id(0) == 0)
        def _():
            st[...] = jnp.zeros_like(st)

        ins = ((qf, kf, vf, lff), (qb, kb, vb, lfb))
        chains = [(d, h) for d in range(2) for h in range(n_heads)]
        ck = [slice(h * dk, (h + 1) * dk) for h in range(n_heads)]
        cv = [slice(h * dv, (h + 1) * dv) for h in range(n_heads)]
        sts = [st[d, h] for d, h in chains]
        done = []
        for sub in range(ATTN_SUB):
            local = (sub, ATTN_SUB - 1 - sub)
            rows = [slice(local[d] * CHUNK, (local[d] + 1) * CHUNK) for d in range(2)]
            qs = [ins[d][0][rows[d], ck[h]] for d, h in chains]
            ks = [ins[d][1][rows[d], ck[h]] for d, h in chains]
            vs = [ins[d][2][rows[d], cv[h]] for d, h in chains]
            lfs = [ins[d][3][rows[d], ck[h]] for d, h in chains]
            os_, st_new = _chunks_fn(qs, ks, vs, lfs, sts, [d == 1 for d, _ in chains])
            done.append((local, rows, sts, os_))
            sts = st_new
        for local, rows, entered, os_ in done:
            for i, (d, h) in enumerate(chains):
                (sf_ref, sb_ref)[d][local[d], h] = entered[i].astype(BF16)
                (of_ref, ob_ref)[d][rows[d], cv[h]] = os_[i]
        for i, (d, h) in enumerate(chains):
            st[d, h] = sts[i]

    in_specs = [spec(wk, q[1], False), spec(wk, k_f[1], False), spec(wv, v[1], False), spec(wk, lf_f[1], False),
                spec(wk, q[1], True), spec(wk, k_b[1], True), spec(wv, v[1], True), spec(wk, lf_b[1], True)]
    return pl.pallas_call(
        body, name=name, grid=(n_steps,), in_specs=in_specs,
        out_specs=[spec(wv, 0, False), spec(wv, 0, True), sspec(False), sspec(True)],
        out_shape=[jax.ShapeDtypeStruct((n_rows, wv), F32)] * 2
        + [jax.ShapeDtypeStruct((n_chunks, n_heads, dv, dk), BF16)] * 2,
        scratch_shapes=[pltpu.VMEM((2, n_heads, dv, dk), F32)],
        compiler_params=_params(("arbitrary",)),
    )(q[0], k_f[0], v[0], lf_f[0], q[0], k_b[0], v[0], lf_b[0])


def _attn_bwd(name, q, k_f, k_b, v, lf_f, lf_b, st_f, st_b, do, n_heads, dk, dv, out_dtype=F32):
    n_rows = q[0].shape[0]
    n_chunks = n_rows // CHUNK
    n_steps = n_chunks // ATTN_SUB
    wk, wv = n_heads * dk, n_heads * dv

    def spec(width, off, rev):
        return pl.BlockSpec((CHUNK * ATTN_SUB, width), lambda n: (n if rev else (n_steps - 1 - n), off))

    def sspec(rev):
        return pl.BlockSpec((ATTN_SUB, n_heads, dv, dk), lambda n: (n if rev else (n_steps - 1 - n), 0, 0, 0))

    def body(qf, kf, vf, lff, sf, dof, qb, kb, vb, lfb, sb, dob,
             dqf, dkf, dvf, dlff, dqb, dkb, dvb, dlfb, dst):
        @pl.when(pl.program_id(0) == 0)
        def _():
            dst[...] = jnp.zeros_like(dst)

        ins = ((qf, kf, vf, lff, sf, dof), (qb, kb, vb, lfb, sb, dob))
        outs = ((dqf, dkf, dvf, dlff), (dqb, dkb, dvb, dlfb))
        chains = [(d, h) for d in range(2) for h in range(n_heads)]
        ck = [slice(h * dk, (h + 1) * dk) for h in range(n_heads)]
        cv = [slice(h * dv, (h + 1) * dv) for h in range(n_heads)]
        fn = functools.partial(_chunks_fn, reverses=[d == 1 for d, _ in chains])
        dsts = [dst[d, h] for d, h in chains]
        done = []
        for sub in range(ATTN_SUB):
            local = (ATTN_SUB - 1 - sub, sub)
            rows = [slice(local[d] * CHUNK, (local[d] + 1) * CHUNK) for d in range(2)]
            qs = [ins[d][0][rows[d], ck[h]] for d, h in chains]
            ks = [ins[d][1][rows[d], ck[h]] for d, h in chains]
            vs = [ins[d][2][rows[d], cv[h]] for d, h in chains]
            lfs = [ins[d][3][rows[d], ck[h]] for d, h in chains]
            sts = [ins[d][4][local[d], h].astype(F32) for d, h in chains]
            dos = [ins[d][5][rows[d], cv[h]] for d, h in chains]
            _, vjp = jax.vjp(fn, qs, ks, vs, lfs, sts)
            dqs, dks, dvs, dlfs, dsts = vjp((dos, dsts))
            done.append((rows, dqs, dks, dvs, dlfs))
        for rows, dqs, dks, dvs, dlfs in done:
            for i, (d, h) in enumerate(chains):
                dq_r, dk_r, dv_r, dlf_r = outs[d]
                dq_r[rows[d], ck[h]] = dqs[i].astype(dq_r.dtype)
                dk_r[rows[d], ck[h]] = dks[i].astype(dk_r.dtype)
                dv_r[rows[d], cv[h]] = dvs[i].astype(dv_r.dtype)
                dlf_r[rows[d], ck[h]] = dlfs[i].astype(dlf_r.dtype)
        for i, (d, h) in enumerate(chains):
            dst[d, h] = dsts[i]

    def dir_specs(kk, lf, rev):
        return [spec(wk, q[1], rev), spec(wk, kk[1], rev), spec(wv, v[1], rev), spec(wk, lf[1], rev), sspec(rev),
                spec(wv, 0, rev)]

    def dir_out_specs(rev):
        return [spec(wk, 0, rev), spec(wk, 0, rev), spec(wv, 0, rev), spec(wk, 0, rev)]

    shapes = [jax.ShapeDtypeStruct((n_rows, wk), out_dtype), jax.ShapeDtypeStruct((n_rows, wk), out_dtype),
              jax.ShapeDtypeStruct((n_rows, wv), out_dtype), jax.ShapeDtypeStruct((n_rows, wk), F32)]
    outs = pl.pallas_call(
        body, name=name, grid=(n_steps,), in_specs=dir_specs(k_f, lf_f, False) + dir_specs(k_b, lf_b, True),
        out_specs=dir_out_specs(False) + dir_out_specs(True), out_shape=shapes + shapes,
        scratch_shapes=[pltpu.VMEM((2, n_heads, dv, dk), F32)],
        compiler_params=_params(("arbitrary",)),
    )(q[0], k_f[0], v[0], lf_f[0], st_f, do, q[0], k_b[0], v[0], lf_b[0], st_b, do)
    return outs[:4], outs[4:]


def _row2(v):
    return v.reshape(1, -1)


def _mlp_fwd(tag, h, gain, w1, w2):
    y = _rowcall(f"{tag}_norm", _rmsnorm_fn, [(h, h.shape[1], 0)], [gain], [(h.shape[1], BF16)], tm=512)[0]
    hid = _mm(f"{tag}_up", y, w1, out_dtypes=(BF16,), tm=2048)
    h_out = _mm(f"{tag}_down", hid, w2, a_pro=_relu2, extras=(h,), epi=_add_epi, tk=2048)
    return h_out, (y, hid)


def _dw(name, a, b, **kw):
    return _mm(name, a, b, mode="tn", epi=lambda acc: (acc, acc), out_dtypes=(F32, BF16), **kw)


def _mlp_bwd(tag, h, gain, w1, w2, saved, dh_out):
    y, hid = saved
    dhid = _mm(f"{tag}_dact", dh_out[1], w2, mode="nt", extras=(hid,), epi=_relu2_bwd_epi, out_dtypes=(BF16,),
               tm=2048)
    dw2 = _dw(f"{tag}_dw2", hid, dh_out[1], a_pro=_relu2, tk=2048)
    dw1 = _dw(f"{tag}_dw1", y, dhid, out_split=N_CHIPS, tk=4096)
    dh, dgain = _dy_norm_bwd(f"{tag}_dy", dhid, w1, h, gain, dh_out[0])
    return dh, dgain, dw1, dw2


def _dy_norm_bwd(name, dz, w, h, gain, dres, pin=None, twice=True, **tiles):
    n_out = 2 if twice else 1

    def epi(dy, h_tile, dres_tile, gain_row):
        _, vjp = jax.vjp(lambda u, v: _rmsnorm_fn(u, v)[0], h_tile, gain_row)
        dh, dgain = vjp(dy)
        return (dh + dres_tile,) * n_out + (dgain,)

    assert h.shape[1] <= 1024
    tiles.setdefault("tm", 1024)
    *dh, dgain_parts = _mm(name, dz, w, mode="nt", extras=(h, dres), epi=epi, epi_pars=(gain,), row_sum=True,
                           out_dtypes=(F32, BF16)[:n_out], pin=pin, **tiles)
    return dh, jnp.sum(dgain_parts, axis=0)


def _local_step(x, target, w, pin=None, late=None, emit=None):
    g = {}
    d_model = x.shape[1]
    rg_w = hg_w = d_model // 2
    pins = []

    def send_off(tag, pairs):
        if emit is not None:
            pins.append(emit(tag, [p[0] for p in pairs], [p[1] for p in pairs]))

    def both(fn, pair):
        return [fn(t) for t in pair]

    def chip_major(t):
        return t.reshape(N_CHIPS, t.shape[0] // N_CHIPS, t.shape[1])

    h_a0 = x
    gain = _row2(w["norm_mix"][0])
    y0 = _rowcall("l0_norm", _rmsnorm_fn, [(h_a0, d_model, 0)], [gain], [(d_model, BF16)], tm=512, pin=pin)[0]
    proj0 = _mm("l0_in", y0, w["ab_w_in"], tm=2048)
    conv_w, conv_b = w["rg_conv_w"], _row2(w["rg_conv_b"])
    xc = _conv_fwd("rg_conv", proj0, 0, conv_w, conv_b)
    gate_pars = [w["rg_wa_bd"], w["rg_wx_bd"], w["rg_b_a"], w["rg_b_x"], w["rg_lambda"]]
    a_f, u_f, a_b, u_b = _rowcall("rg_gates", _rg_gates_fn, [(xc, rg_w, 0)], gate_pars, [(rg_w, F32)] * 4)
    hs_f, hs_b = _scan_fwd("rg_scan", a_f, u_f, a_b, u_b)
    hg_rows = [(proj0, hg_w, 2), (proj0, hg_w, 3), (proj0, hg_w, 4)]
    qh, k_f, lf_f, k_b, lf_b = _rowcall("hg_pre", _hg_pre_fn, hg_rows, [w["hg_lb_logits"]], [(hg_w, F32)] * 5)
    iv = (proj0, 5)
    o_f, o_b, st_f, st_b = _attn_fwd("hg_attn", (qh, 0), (k_f, 0), (k_b, 0), iv, (lf_f, 0), (lf_b, 0), 4, 128, 128)
    post0_rows = [(hs_f, rg_w, 0), (hs_b, rg_w, 0), (proj0, rg_w, 1), (o_f, hg_w, 0), (o_b, hg_w, 0), (proj0, hg_w, 6)]
    hg_gain = _row2(w["hg_norm"])
    mix_in0 = _rowcall("l0_post", _post0_fwd_fn, post0_rows, [hg_gain], [(d_model, BF16)])[0]
    if late is not None:
        w = {**w, **late(mix_in0)}
    h_b0 = _mm("l0_out", mix_in0, w["ab_w_out"], extras=(h_a0,), epi=_add_epi)
    h_c0, mlp0 = _mlp_fwd("mlp0", h_b0, _row2(w["norm_mlp"][0]), w["mlp_w1"][0], w["mlp_w2"][0])

    h_a1 = h_c0
    gain1 = _row2(w["norm_mix"][1])
    y1 = _rowcall("l1_norm", _rmsnorm_fn, [(h_a1, d_model, 0)], [gain1], [(d_model, BF16)], tm=512)[0]
    proj1 = _mm("l1_in", y1, w["gla_w_in_pad"], tm=512, tn=GLA_IN_PAD)
    gla_pars = [w["gla_w_up_pad"], w["gla_b_gate"]]
    gq, glf_f, glf_b = _rowcall("gla_pre", _gla_pre_fn, [(proj1, 512, 0), (proj1, LANES, 24)], gla_pars, [(512, F32)] * 3)
    gk, gv = (proj1, 1), (proj1, 1)
    go_f, go_b, gst_f, gst_b = _attn_fwd("gla_attn", (gq, 0), gk, gk, gv, (glf_f, 0), (glf_b, 0), 4, 128, 256)
    gla_gain = _row2(w["gla_norm"])
    post1_rows = [(go_f, d_model, 0), (go_b, d_model, 0), (proj1, d_model, 2)]
    mix_in1 = _rowcall("l1_post", _gla_post_fwd_fn, post1_rows, [gla_gain], [(d_model, BF16)])[0]
    h_b1 = _mm("l1_out", mix_in1, w["gla_w_out"], extras=(h_a1,), epi=_add_epi)
    h_c1, mlp1 = _mlp_fwd("mlp1", h_b1, _row2(w["norm_mlp"][1]), w["mlp_w1"][1], w["mlp_w2"][1])

    *dh, loss, g["norm_final"] = _rowcall(
        "loss_head", _loss_head_fn, [(h_c1, d_model, 0), (target, d_model, 0)], [_row2(w["norm_final"])],
        [(d_model, F32), (d_model, BF16)], [(1, LANES), (1, d_model)], tm=512)

    dh, g_nmlp1, g_w1_1, g_w2_1 = _mlp_bwd("mlp1", h_b1, _row2(w["norm_mlp"][1]), w["mlp_w1"][1], w["mlp_w2"][1], mlp1, dh)
    send_off("mlp1", [g_w1_1, both(chip_major, g_w2_1)])
    dmix1 = _mm("l1_dout", dh[1], w["gla_w_out"], mode="nt")
    g_gla_out = _dw("l1_dwout", mix_in1, dh[1])
    g["gla_w_out"] = g_gla_out[0]
    dgo, dr, g["gla_norm"] = _rowcall(
        "l1_dpost", _gla_post_bwd_fn, post1_rows + [(dmix1, d_model, 0)], [gla_gain],
        [(d_model, F32), (d_model, BF16)], [(1, d_model)], pin=pins.pop() if pins else None)
    (dq_f, dk_f, dv_f, dlf_f), (dq_b, dk_b, dv_b, dlf_b) = _attn_bwd(
        "gla_dattn", (gq, 0), gk, gk, gv, (glf_f, 0), (glf_b, 0), gst_f, gst_b, dgo, 4, 128, 256)

    def gla_pre_bwd(q, lr, dq1, dq2, dlf1, dlf2, dk1, dk2, dv1, dv2, w_up, b_gate):
        dlr = jnp.zeros_like(lr)
        dws, dbs = [], []
        for d, dlf in enumerate((dlf1, dlf2)):
            z = _raw_nn(lr, w_up[d]) + b_gate[d:d + 1]
            dz = dlf * _sigmoid(-z) * (1.0 / 16.0)
            dlr = dlr + _raw_nt(dz, w_up[d])
            dws.append(_raw_tn(dz, lr))
            dbs.append(jnp.sum(dz, axis=0, keepdims=True))
        return ((dq1 + dq2) * (128.0 ** -0.5), dk1 + dk2, dv1 + dv2, dlr, dws[0], dws[1], dbs[0], dbs[1])

    rows = [(proj1, 512, 0), (proj1, LANES, 24), (dq_f, 512, 0), (dq_b, 512, 0), (dlf_f, 512, 0), (dlf_b, 512, 0),
            (dk_f, 512, 0), (dk_b, 512, 0), (dv_f, d_model, 0), (dv_b, d_model, 0)]
    dq, dk, dv, dlr, dwt_f, dwt_b, db_f, db_b = _rowcall(
        "gla_dpre", gla_pre_bwd, rows, gla_pars, [(512, BF16), (512, BF16), (d_model, BF16), (LANES, BF16)],
        [(512, LANES), (512, LANES), (1, 512), (1, 512)])
    g["gla_w_up_pad"] = jnp.stack([dwt_f.T, dwt_b.T])
    g["gla_b_gate"] = jnp.concatenate([db_f, db_b], axis=0)
    dproj1 = jnp.concatenate([dq, dk, dv, dr, dlr], axis=1)
    g_gla_in = both(lambda t: _split_chips(t[:, :GLA_IN_WIDTH], 1), _dw("l1_dwin", y1, dproj1, tn=640, tk=4096))
    g["gla_w_in"] = g_gla_in[0]
    send_off("gla", [g_gla_in, both(chip_major, g_gla_out)])
    dh, g_nmix1 = _dy_norm_bwd("l1_dy", dproj1, w["gla_w_in_pad"], h_a1, gain1, dh[0],
                               pin=pins.pop() if pins else None, tk=GLA_IN_PAD)

    dh, g_nmlp0, g_w1_0, g_w2_0 = _mlp_bwd("mlp0", h_b0, _row2(w["norm_mlp"][0]), w["mlp_w1"][0], w["mlp_w2"][0], mlp0, dh)
    g_ab_out = _dw("l0_dwout", mix_in0, dh[1])
    g["ab_w_out"] = g_ab_out[0]
    send_off("mlp0", [g_w1_0, both(chip_major, g_w2_0), both(chip_major, g_ab_out)])
    dmix0 = _mm("l0_dout", dh[1], w["ab_w_out"], mode="nt")
    dhs, dga, do, dg, g["hg_norm"] = _rowcall(
        "l0_dpost", _post0_bwd_fn, post0_rows + [(dmix0, d_model, 0)], [hg_gain],
        [(rg_w, F32), (rg_w, BF16), (hg_w, F32), (hg_w, BF16)], [(1, hg_w)], pin=pins.pop() if pins else None)
    (dqh_f, dk_f, div_f, dlf_f), (dqh_b, dk_b, div_b, dlf_b) = _attn_bwd(
        "hg_dattn", (qh, 0), (k_f, 0), (k_b, 0), iv, (lf_f, 0), (lf_b, 0), st_f, st_b, do, 4, 128, 128)

    def hg_pre_bwd(q, f_f, f_b, dq1, dq2, dk1, dlf1, dk2, dlf2, dv1, dv2, logits):
        _, vjp = jax.vjp(_hg_pre_fn, q, f_f, f_b, logits)
        dq, df_f, df_b, dlogits = vjp((dq1 + dq2, dk1, dlf1, dk2, dlf2))
        return dq, df_f, df_b, dv1 + dv2, dlogits

    rows = hg_rows + [(t, hg_w, 0) for t in (dqh_f, dqh_b, dk_f, dlf_f, dk_b, dlf_b, div_f, div_b)]
    dq, df_f, df_b, div, g["hg_lb_logits"] = _rowcall(
        "hg_dpre", hg_pre_bwd, rows, [w["hg_lb_logits"]], [(hg_w, BF16)] * 4, [(2, hg_w)])
    du_f, da_f, du_b, da_b = _scan_bwd("rg_dscan", a_f, hs_f, a_b, hs_b, dhs)
    gates_bwd = _vjp_of(_rg_gates_fn, 1, 4, 5)
    rows = [(xc, rg_w, 0), (da_f, rg_w, 0), (du_f, rg_w, 0), (da_b, rg_w, 0), (du_b, rg_w, 0)]
    dxc, g["rg_wa_bd"], g["rg_wx_bd"], g["rg_b_a"], g["rg_b_x"], g["rg_lambda"] = _rowcall(
        "rg_dgates", gates_bwd, rows, gate_pars, [(rg_w, F32)],
        [(2, rg_w, rg_w), (2, rg_w, rg_w), (2, rg_w), (2, rg_w), (2, rg_w)])
    dxa, g["rg_conv_w"], g["rg_conv_b"] = _conv_bwd("rg_dconv", proj0, 0, conv_w, dxc)
    dproj0 = jnp.concatenate([dxa, dga, dq, df_f, df_b, div, dg], axis=1)
    g_ab_in = _dw("l0_dwin", y0, dproj0, out_split=N_CHIPS, tk=4096)
    g["ab_w_in"] = g_ab_in[0]
    send_off("ab", [g_ab_in])
    w_in_flat = _join_chips(w["ab_w_in"], 1)
    (grad_x,), g_nmix0 = _dy_norm_bwd("l0_dy", dproj0, w_in_flat, h_a0, gain, dh[0],
                                      pin=pins.pop() if pins else None, twice=False, tk=w_in_flat.shape[1])

    g["norm_mix"] = jnp.concatenate([g_nmix0, g_nmix1], axis=0)
    g["norm_mlp"] = jnp.concatenate([g_nmlp0, g_nmlp1], axis=0)
    g["mlp_w1"] = [g_w1_0[0], g_w1_1[0]]
    g["mlp_w2"] = [g_w2_0[0], g_w2_1[0]]
    return loss, grad_x, g


def _block_diag(w):
    d, g, n, _ = w.shape
    eye = jnp.eye(g, dtype=w.dtype)
    return (w[:, :, :, None, :] * eye[None, :, None, :, None]).reshape(d, g * n, g * n)


def _block_diag_extract(wbd, g):
    d, gn, _ = wbd.shape
    n = gn // g
    blocks = wbd.reshape(d, g, n, g, n)
    return jnp.stack([blocks[:, i, :, i, :] for i in range(g)], axis=1)


def _prepare_weights(big, full):
    w = {k: full[k] for k in ("norm_mix", "norm_mlp", "norm_final", "hg_lb_logits")}
    for k in ("rg_conv_w", "rg_conv_b", "rg_b_a", "rg_b_x", "rg_lambda", "hg_norm", "gla_b_gate", "gla_norm"):
        w[k] = full[k][0]
    w["rg_wa_bd"] = _block_diag(full["rg_w_a"][0])
    w["rg_wx_bd"] = _block_diag(full["rg_w_x"][0])
    up = full["gla_w_gate_up"][0]
    rank = up.shape[1]
    pad = jnp.zeros((2, LANES, up.shape[2]), F32)
    w["gla_w_up_pad"] = pad.at[0, 0:rank].set(up[0]).at[1, rank:2 * rank].set(up[1])
    w.update(_prepare_matrices(big))
    return w


def _prepare_matrices(big):
    w = {}
    if "mlp_w1" in big:
        w["mlp_w1"] = list(big["mlp_w1"])
        w["mlp_w2"] = [t.reshape(-1, t.shape[-1]) for t in big["mlp_w2"]]
    if "ab_w_in" in big:
        w["ab_w_in"] = big["ab_w_in"]
    if "ab_w_out" in big:
        w["ab_w_out"] = big["ab_w_out"].reshape(-1, big["ab_w_out"].shape[-1])
    if "gla_w_in" in big:
        w["gla_w_out"] = big["gla_w_out"].reshape(-1, big["gla_w_out"].shape[-1])
        gla_in = _join_chips(big["gla_w_in"], 1)
        w["gla_w_in_pad"] = jnp.pad(gla_in, ((0, 0), (0, GLA_IN_PAD - gla_in.shape[1])))
    return w


def _finish_grads(g, rank=16, rg_blocks=8):
    def chip_major(t):
        return t.reshape(N_CHIPS, t.shape[0] // N_CHIPS, t.shape[1])

    big = {
        "mlp_w1": list(g["mlp_w1"]), "mlp_w2": [chip_major(t) for t in g["mlp_w2"]],
        "ab_w_in": g["ab_w_in"], "ab_w_out": chip_major(g["ab_w_out"]),
        "gla_w_in": g["gla_w_in"], "gla_w_out": chip_major(g["gla_w_out"]),
    }
    small = {
        "norm_mix": g["norm_mix"], "norm_mlp": g["norm_mlp"], "norm_final": g["norm_final"][0],
        "rg_conv_w": g["rg_conv_w"][None], "rg_conv_b": g["rg_conv_b"],
        "rg_w_a": _block_diag_extract(g["rg_wa_bd"], rg_blocks)[None], "rg_b_a": g["rg_b_a"][None],
        "rg_w_x": _block_diag_extract(g["rg_wx_bd"], rg_blocks)[None], "rg_b_x": g["rg_b_x"][None],
        "rg_lambda": g["rg_lambda"][None], "hg_lb_logits": g["hg_lb_logits"], "hg_norm": g["hg_norm"],
        "gla_w_gate_up": jnp.stack([g["gla_w_up_pad"][0, 0:rank], g["gla_w_up_pad"][1, rank:2 * rank]])[None],
        "gla_b_gate": g["gla_b_gate"][None], "gla_norm": g["gla_norm"],
    }
    return big, small


MATRICES = (("mlp_w1", 0), ("mlp_w1", 1), ("mlp_w2", 0), ("mlp_w2", 1), ("ab_w_in", 0), ("ab_w_out", 0),
            ("gla_w_in", 0), ("gla_w_out", 0))
EARLY_MATRICES = ("ab_w_in",)
SMALL_SHARDED = ("rg_conv_w", "rg_b_a", "rg_b_x", "rg_lambda", "gla_w_gate_up", "gla_b_gate", "gla_norm")
SMALL_REPLICATED = ("norm_mix", "norm_mlp", "norm_final", "rg_conv_b", "rg_w_a", "rg_w_x", "hg_lb_logits", "hg_norm")
WEIGHTS = ("norm_mix", "norm_mlp", "norm_final", "mlp_w1", "mlp_w2", "ab_w_in", "ab_w_out", "rg_conv_w", "rg_conv_b",
           "rg_w_a", "rg_b_a", "rg_w_x", "rg_b_x", "rg_lambda", "hg_lb_logits", "hg_norm", "gla_w_in", "gla_w_out",
           "gla_w_gate_up", "gla_b_gate", "gla_norm")
ROW_ALIGN = 16


def _pack(arrays, lead=0):
    head = arrays[0].shape[:lead]
    flat = jnp.concatenate([a.reshape(head + (-1,)) for a in arrays], axis=lead)
    n = flat.shape[-1]
    quantum = LANES * ROW_ALIGN
    padded = -(-n // quantum) * quantum
    if padded != n:
        flat = jnp.pad(flat, [(0, 0)] * lead + [(0, padded - n)])
    return flat.reshape(head + (padded // LANES, LANES))


def _unpack(buf, shapes, lead=0):
    head = buf.shape[:lead]
    flat = buf.reshape(head + (-1,))
    out, off = [], 0
    for s in shapes:
        n = 1
        for v in s:
            n *= v
        out.append(lax.slice_in_dim(flat, off, off + n, axis=lead).reshape(head + tuple(s)))
        off += n
    return out


def _join_chips(gathered, axis):
    t = jnp.moveaxis(gathered, 0, axis)
    return t.reshape(t.shape[:axis] + (t.shape[axis] * t.shape[axis + 1],) + t.shape[axis + 2:])


def _split_chips(full, axis):
    s = full.shape
    t = full.reshape(s[:axis] + (N_CHIPS, s[axis] // N_CHIPS) + s[axis + 1:])
    return jnp.moveaxis(t, axis, 0)


_ANY = pl.BlockSpec(memory_space=pl.ANY)


def _place():
    return lax.axis_index("x"), lax.axis_index("y"), lax.axis_index("c")


def _into_slot(name, src, slot, n_slots, dtype, tm, layer=None):
    r, lanes = src.shape[-2:]
    tm = _row_tile(r, tm, ROW_ALIGN)

    def body(slot_ref, in_ref, o_ref):
        o_ref[...] = in_ref[...].astype(o_ref.dtype)

    if layer is None:
        in_spec = pl.BlockSpec((tm, lanes), lambda i, slot_ref: (i, 0))
    else:
        in_spec = pl.BlockSpec((None, tm, lanes), lambda i, slot_ref: (layer, i, 0))
    grid_spec = pltpu.PrefetchScalarGridSpec(
        num_scalar_prefetch=1, grid=(r // tm,), in_specs=[in_spec],
        out_specs=pl.BlockSpec((None, tm, lanes), lambda i, slot_ref: (slot_ref[0], i, 0)))
    return pl.pallas_call(
        body, name=name, grid_spec=grid_spec, out_shape=jax.ShapeDtypeStruct((n_slots, r, lanes), dtype),
        compiler_params=_params(("parallel",)),
    )(slot.reshape(1).astype(jnp.int32), src)


def _chip_peers():
    x, y, c = _place()
    return 2 * x + y, c, [(1 - x, y), (x, 1 - y), (1 - x, 1 - y)]


def _comm_call(name, body, ins, out_shapes, n_sems, aliases=None):
    return pl.pallas_call(
        body, name=name, in_specs=[_ANY] * len(ins), out_specs=[_ANY] * len(out_shapes), out_shape=out_shapes,
        input_output_aliases=aliases or {},
        scratch_shapes=[pltpu.SemaphoreType.DMA((n_sems,)), pltpu.SemaphoreType.DMA((n_sems,))],
    )(*ins)


def _gather_chips(name, bufs):
    n = len(bufs)

    def body(*refs):
        outs, send_sems, recv_sems = refs[n:2 * n], refs[2 * n], refs[2 * n + 1]
        x, y, c = _place()
        me, _, peers = _chip_peers()

        def rows(a, block, half):
            rh = outs[a].shape[1] // 2
            return outs[a].at[block, pl.ds(half * rh, rh)]

        def copy(a, j, block, half, to, sem):
            return pltpu.make_async_remote_copy(
                src_ref=rows(a, block, half), dst_ref=rows(a, block, half), send_sem=send_sems.at[sem],
                recv_sem=recv_sems.at[sem], device_id=to, device_id_type=MESH)

        def over_ici(a, j, block):
            px, py = peers[j]
            return copy(a, j, block, c, (px, py, c), 6 * a + j)

        def to_sibling(a, j, block, half):
            return copy(a, j, block, half, (x, y, 1 - c), 6 * a + 3 + j)

        sends = [over_ici(a, j, me) for a in range(n) for j in range(3)]
        for cp in sends:
            cp.start()
        for a in range(n):
            for j, (px, py) in enumerate(peers):
                over_ici(a, j, 2 * px + py).wait_recv()
                handed = to_sibling(a, j, 2 * px + py, c)
                handed.start()
                sends.append(handed)
        for a in range(n):
            for j, (px, py) in enumerate(peers):
                to_sibling(a, j, 2 * px + py, 1 - c).wait_recv()
        for cp in sends:
            cp.wait_send()

    shapes = [jax.ShapeDtypeStruct(b.shape, b.dtype) for b in bufs]
    return _comm_call(name, body, bufs, shapes, 6 * n, {a: a for a in range(n)})


_HBM = pl.BlockSpec(memory_space=pltpu.HBM)
_SEM = pl.BlockSpec(memory_space=pltpu.SEMAPHORE)
_EFFECT = pltpu.SideEffectType.DATAFLOW_SIDE_EFFECTING


def _half_rows(ref, block, half):
    rh = ref.shape[1] // 2
    return ref.at[block, pl.ds(half * rh, rh)]


def _gather_start(name, bufs, after):
    n = len(bufs)

    def body(*refs):
        ins, send_sems, recv_sems, token = refs[:n], refs[n + 1], refs[n + 2], refs[-1]
        me, c, peers = _chip_peers()
        for a in range(n):
            mine = _half_rows(ins[a], me, c)
            for j, (px, py) in enumerate(peers):
                pltpu.make_async_remote_copy(
                    src_ref=mine, dst_ref=mine, send_sem=send_sems.at[3 * a + j], recv_sem=recv_sems.at[3 * a + j],
                    device_id=(px, py, c), device_id_type=MESH).start()
        token[...] = jnp.zeros_like(token)

    out_shape = (pltpu.SemaphoreType.DMA((3 * n,)), pltpu.SemaphoreType.DMA((3 * n,)),
                 *[pltpu.HBM(b.shape, b.dtype) for b in bufs], jax.ShapeDtypeStruct((8, LANES), F32))
    return pl.pallas_call(
        body, name=name, out_shape=out_shape, in_specs=[_HBM] * n + [_ANY],
        out_specs=(_SEM, _SEM, *[_HBM] * n, pl.BlockSpec(memory_space=pltpu.VMEM)),
        input_output_aliases={a: 2 + a for a in range(n)},
        compiler_params=pltpu.CompilerParams(has_side_effects=_EFFECT),
    )(*[pltpu.with_memory_space_constraint(b, pltpu.HBM) for b in bufs], after)


def _gather_wait(name, bufs, send_sems, recv_sems, after):
    n = len(bufs)

    def body(*refs):
        ins, send_sems, recv_sems = refs[:n], refs[n], refs[n + 1]
        me, c, peers = _chip_peers()
        for a in range(n):
            for j, (px, py) in enumerate(peers):
                copy = pltpu.make_async_remote_copy(
                    src_ref=_half_rows(ins[a], me, c), dst_ref=_half_rows(ins[a], 2 * px + py, c),
                    send_sem=send_sems.at[3 * a + j], recv_sem=recv_sems.at[3 * a + j],
                    device_id=(px, py, c), device_id_type=MESH)
                copy.wait_send()
                copy.wait_recv()

    return pl.pallas_call(
        body, name=name, out_shape=tuple(pltpu.HBM(b.shape, b.dtype) for b in bufs),
        in_specs=[_HBM] * n + [_SEM, _SEM, _ANY], out_specs=tuple([_HBM] * n),
        input_output_aliases={a: a for a in range(n)},
        compiler_params=pltpu.CompilerParams(has_side_effects=_EFFECT),
    )(*bufs, send_sems, recv_sems, after)


def _hand_over(name, bufs):
    n = len(bufs)

    def body(*refs):
        outs, send_sems, recv_sems = refs[n:2 * n], refs[2 * n], refs[2 * n + 1]
        x, y, c = _place()
        _, _, peers = _chip_peers()

        def copy(a, j, half):
            px, py = peers[j]
            rows = _half_rows(outs[a], 2 * px + py, half)
            return pltpu.make_async_remote_copy(
                src_ref=rows, dst_ref=rows, send_sem=send_sems.at[3 * a + j], recv_sem=recv_sems.at[3 * a + j],
                device_id=(x, y, 1 - c), device_id_type=MESH)

        sends = [copy(a, j, c) for a in range(n) for j in range(3)]
        for cp in sends:
            cp.start()
        for a in range(n):
            for j in range(3):
                copy(a, j, 1 - c).wait_recv()
        for cp in sends:
            cp.wait_send()

    shapes = [jax.ShapeDtypeStruct(b.shape, b.dtype) for b in bufs]
    return _comm_call(name, body, bufs, shapes, 3 * n, {a: a for a in range(n)})


def _pair_gather(name, bufs):
    n = len(bufs)

    def body(*refs):
        ins, outs, send_sems, recv_sems = refs[:n], refs[n:2 * n], refs[2 * n], refs[2 * n + 1]
        x, y, c = _place()

        def copy(a, block):
            return pltpu.make_async_remote_copy(
                src_ref=ins[a].at[block], dst_ref=outs[a].at[block], send_sem=send_sems.at[a],
                recv_sem=recv_sems.at[a], device_id=(x, y, 1 - c), device_id_type=MESH)

        sends = [copy(a, c) for a in range(n)]
        for cp in sends:
            cp.start()
        for a in range(n):
            copy(a, 1 - c).wait_recv()
        for cp in sends:
            cp.wait_send()

    shapes = [jax.ShapeDtypeStruct(b.shape, b.dtype) for b in bufs]
    return _comm_call(name, body, bufs, shapes, n, {a: a for a in range(n)})


def _all_peers():
    x, y, c = _place()
    peers = []
    for mask in range(1, N_DEV):
        fx, fy, fc = (mask >> 2) & 1, (mask >> 1) & 1, mask & 1
        peers.append((jnp.where(fx, 1 - x, x), jnp.where(fy, 1 - y, y), jnp.where(fc, 1 - c, c)))
    return 4 * x + 2 * y + c, peers


def _reduce_copies(srcs, lands, send_sems, recv_sems):
    me, peers = _all_peers()
    sends, arrivals = [], []
    for a in range(len(srcs)):
        for j, (px, py, pc) in enumerate(peers):
            k = (N_DEV - 1) * a + j
            sends.append(pltpu.make_async_remote_copy(
                src_ref=srcs[a].at[2 * px + py, pc], dst_ref=lands[a].at[me], send_sem=send_sems.at[k],
                recv_sem=recv_sems.at[k], device_id=(px, py, pc), device_id_type=MESH))
            arrivals.append(pltpu.make_async_remote_copy(
                src_ref=srcs[a].at[2 * px + py, pc], dst_ref=lands[a].at[4 * px + 2 * py + pc],
                send_sem=send_sems.at[k], recv_sem=recv_sems.at[k], device_id=(px, py, pc), device_id_type=MESH))
    return sends, arrivals


def _reduce_direct(name, srcs, pin=None):
    n = len(srcs)
    extra = [] if pin is None else [pin]

    def body(*refs):
        ins, outs = refs[:n], refs[n + len(extra):2 * n + len(extra)]
        sends, arrivals = _reduce_copies(ins, outs, refs[-2], refs[-1])
        for cp in sends:
            cp.start()
        for cp in arrivals:
            cp.wait_recv()
        for cp in sends:
            cp.wait_send()

    shapes = [jax.ShapeDtypeStruct((N_DEV,) + s.shape[2:], s.dtype) for s in srcs]
    return _comm_call(name, body, list(srcs) + extra, shapes, (N_DEV - 1) * n)


def _reduce_start(name, srcs):
    n = len(srcs)
    lands = [lax.empty((N_DEV,) + s.shape[2:], s.dtype) for s in srcs]

    def body(*refs):
        sends, _ = _reduce_copies(refs[:n], refs[n:2 * n], refs[2 * n], refs[2 * n + 1])
        for cp in sends:
            cp.start()
        refs[-1][...] = jnp.zeros_like(refs[-1])

    bufs = list(srcs) + lands
    n_sems = (N_DEV - 1) * n
    out_shape = (pltpu.SemaphoreType.DMA((n_sems,)), pltpu.SemaphoreType.DMA((n_sems,)),
                 *[pltpu.HBM(b.shape, b.dtype) for b in bufs], jax.ShapeDtypeStruct((8, LANES), F32))
    return pl.pallas_call(
        body, name=name, out_shape=out_shape, in_specs=[_HBM] * (2 * n),
        out_specs=(_SEM, _SEM, *[_HBM] * (2 * n), pl.BlockSpec(memory_space=pltpu.VMEM)),
        input_output_aliases={a: 2 + a for a in range(2 * n)},
        compiler_params=pltpu.CompilerParams(has_side_effects=_EFFECT),
    )(*[pltpu.with_memory_space_constraint(b, pltpu.HBM) for b in bufs])


def _reduce_wait(name, srcs, lands, send_sems, recv_sems, after):
    n = len(srcs)

    def body(*refs):
        sends, arrivals = _reduce_copies(refs[:n], refs[n:2 * n], refs[2 * n], refs[2 * n + 1])
        for cp in sends:
            cp.wait_send()
        for cp in arrivals:
            cp.wait_recv()

    bufs = list(srcs) + list(lands)
    outs = pl.pallas_call(
        body, name=name, out_shape=tuple(pltpu.HBM(b.shape, b.dtype) for b in bufs),
        in_specs=[_HBM] * (2 * n) + [_SEM, _SEM, _ANY], out_specs=tuple([_HBM] * (2 * n)),
        input_output_aliases={a: a for a in range(2 * n)},
        compiler_params=pltpu.CompilerParams(has_side_effects=_EFFECT),
    )(*bufs, send_sems, recv_sems, after)
    return list(outs[n:])


def _reduce_sum(name, own, land, chip, core):
    n, rh, lanes = land.shape
    tm = _row_tile(rh, 1024, ROW_ALIGN)

    def body(idx_ref, own_ref, *rest):
        total = own_ref[...]
        for g_ref in rest[:-1]:
            total = total + g_ref[...].astype(F32)
        rest[-1][...] = total

    def block(k):
        return pl.BlockSpec((None, tm, lanes), lambda i, idx_ref: ((2 * idx_ref[0] + idx_ref[1] + k) % n, i, 0))

    grid_spec = pltpu.PrefetchScalarGridSpec(
        num_scalar_prefetch=1, grid=(rh // tm,),
        in_specs=[pl.BlockSpec((None, None, tm, lanes), lambda i, idx_ref: (idx_ref[0], idx_ref[1], i, 0))]
        + [block(k) for k in range(1, n)],
        out_specs=pl.BlockSpec((None, tm, lanes), lambda i, idx_ref: (idx_ref[1], i, 0)))
    return pl.pallas_call(
        body, name=name, grid_spec=grid_spec, out_shape=jax.ShapeDtypeStruct((2, rh, lanes), F32),
        compiler_params=_params(("parallel",)),
    )(jnp.stack([chip, core]).astype(jnp.int32), own, *[land] * (n - 1))


def _gather_all_start(name, buf):
    def body(in_ref, send_sems, recv_sems, out_ref, token):
        me, peers = _all_peers()
        for j, peer in enumerate(peers):
            pltpu.make_async_remote_copy(
                src_ref=in_ref.at[me], dst_ref=in_ref.at[me], send_sem=send_sems.at[j], recv_sem=recv_sems.at[j],
                device_id=peer, device_id_type=MESH).start()
        token[...] = jnp.zeros_like(token)

    n = N_DEV - 1
    return pl.pallas_call(
        body, name=name, in_specs=[_HBM],
        out_shape=(pltpu.SemaphoreType.DMA((n,)), pltpu.SemaphoreType.DMA((n,)), pltpu.HBM(buf.shape, buf.dtype),
                   jax.ShapeDtypeStruct((8, LANES), F32)),
        out_specs=(_SEM, _SEM, _HBM, pl.BlockSpec(memory_space=pltpu.VMEM)), input_output_aliases={0: 2},
        compiler_params=pltpu.CompilerParams(has_side_effects=_EFFECT),
    )(pltpu.with_memory_space_constraint(buf, pltpu.HBM))


def _gather_all_wait(name, buf, send_sems, recv_sems, after):
    def body(in_ref, send_sems, recv_sems, after_ref, out_ref):
        me, peers = _all_peers()
        for j, (px, py, pc) in enumerate(peers):
            copy = pltpu.make_async_remote_copy(
                src_ref=in_ref.at[me], dst_ref=in_ref.at[4 * px + 2 * py + pc], send_sem=send_sems.at[j],
                recv_sem=recv_sems.at[j], device_id=(px, py, pc), device_id_type=MESH)
            copy.wait_send()
            copy.wait_recv()

    return pl.pallas_call(
        body, name=name, in_specs=[_HBM, _SEM, _SEM, _ANY], out_shape=pltpu.HBM(buf.shape, buf.dtype),
        out_specs=_HBM, input_output_aliases={0: 0},
        compiler_params=pltpu.CompilerParams(has_side_effects=_EFFECT),
    )(buf, send_sems, recv_sems, after)


def _sum_blocks(name, stacked, tm):
    n, r, lanes = stacked.shape

    def body(in_ref, o_ref):
        acc = in_ref[0]
        for j in range(1, n):
            acc = acc + in_ref[j]
        o_ref[...] = acc

    return pl.pallas_call(
        body, name=name, grid=(r // tm,), in_specs=[pl.BlockSpec((n, tm, lanes), lambda i: (0, i, 0))],
        out_specs=pl.BlockSpec((tm, lanes), lambda i: (i, 0)), out_shape=jax.ShapeDtypeStruct((r, lanes), F32),
        compiler_params=_params(("parallel",)),
    )(stacked)


def _row_tile(rows, pref, align):
    best = None
    for t in range(align, min(rows, pref) + 1, align):
        if rows % t == 0:
            best = t
    assert best is not None, (rows, pref, align)
    return best


def _adam(name, w, g, m, v):
    rows, width = w.shape
    tm = _row_tile(rows, max(8, 4096 * LANES // width), 8)
    args = [(t, width, 0) for t in (w, g, m, v)]
    return _rowcall(name, _adam_fn, args, [], [(width, F32)] * 3, tm=tm)


def kernel(x, norm_mix, norm_mlp, norm_final, mlp_w1, mlp_w2, ab_w_in, ab_w_out, rg_conv_w, rg_conv_b, rg_w_a, rg_b_a, rg_w_x, rg_b_x, rg_lambda, hg_lb_logits, hg_norm, gla_w_in, gla_w_out, gla_w_gate_up, gla_b_gate, gla_norm, loss_target, m_norm_mix, m_norm_mlp, m_norm_final, m_mlp_w1, m_mlp_w2, m_ab_w_in, m_ab_w_out, m_rg_conv_w, m_rg_conv_b, m_rg_w_a, m_rg_b_a, m_rg_w_x, m_rg_b_x, m_rg_lambda, m_hg_lb_logits, m_hg_norm, m_gla_w_in, m_gla_w_out, m_gla_w_gate_up, m_gla_b_gate, m_gla_norm, v_norm_mix, v_norm_mlp, v_norm_final, v_mlp_w1, v_mlp_w2, v_ab_w_in, v_ab_w_out, v_rg_conv_w, v_rg_conv_b, v_rg_w_a, v_rg_b_a, v_rg_w_x, v_rg_b_x, v_rg_lambda, v_hg_lb_logits, v_hg_norm, v_gla_w_in, v_gla_w_out, v_gla_w_gate_up, v_gla_b_gate, v_gla_norm):
    w = dict(norm_mix=norm_mix, norm_mlp=norm_mlp, norm_final=norm_final, mlp_w1=mlp_w1, mlp_w2=mlp_w2, ab_w_in=ab_w_in, ab_w_out=ab_w_out, rg_conv_w=rg_conv_w, rg_conv_b=rg_conv_b, rg_w_a=rg_w_a, rg_b_a=rg_b_a, rg_w_x=rg_w_x, rg_b_x=rg_b_x, rg_lambda=rg_lambda, hg_lb_logits=hg_lb_logits, hg_norm=hg_norm, gla_w_in=gla_w_in, gla_w_out=gla_w_out, gla_w_gate_up=gla_w_gate_up, gla_b_gate=gla_b_gate, gla_norm=gla_norm)
    m = dict(norm_mix=m_norm_mix, norm_mlp=m_norm_mlp, norm_final=m_norm_final, mlp_w1=m_mlp_w1, mlp_w2=m_mlp_w2, ab_w_in=m_ab_w_in, ab_w_out=m_ab_w_out, rg_conv_w=m_rg_conv_w, rg_conv_b=m_rg_conv_b, rg_w_a=m_rg_w_a, rg_b_a=m_rg_b_a, rg_w_x=m_rg_w_x, rg_b_x=m_rg_b_x, rg_lambda=m_rg_lambda, hg_lb_logits=m_hg_lb_logits, hg_norm=m_hg_norm, gla_w_in=m_gla_w_in, gla_w_out=m_gla_w_out, gla_w_gate_up=m_gla_w_gate_up, gla_b_gate=m_gla_b_gate, gla_norm=m_gla_norm)
    v = dict(norm_mix=v_norm_mix, norm_mlp=v_norm_mlp, norm_final=v_norm_final, mlp_w1=v_mlp_w1, mlp_w2=v_mlp_w2, ab_w_in=v_ab_w_in, ab_w_out=v_ab_w_out, rg_conv_w=v_rg_conv_w, rg_conv_b=v_rg_conv_b, rg_w_a=v_rg_w_a, rg_b_a=v_rg_b_a, rg_w_x=v_rg_w_x, rg_b_x=v_rg_b_x, rg_lambda=v_rg_lambda, hg_lb_logits=v_hg_lb_logits, hg_norm=v_hg_norm, gla_w_in=v_gla_w_in, gla_w_out=v_gla_w_out, gla_w_gate_up=v_gla_w_gate_up, gla_b_gate=v_gla_b_gate, gla_norm=v_gla_norm)
    chip = 2 * lax.axis_index("x") + lax.axis_index("y")
    core = lax.axis_index("c")
    sharded_shapes = [w[n].shape for n in SMALL_SHARDED]

    slots = [_into_slot(f"cast_{n}{layer}", w[n], chip, N_CHIPS, BF16, 512, layer) for n, layer in MATRICES]
    early = [i for i, (n, _) in enumerate(MATRICES) if n in EARLY_MATRICES]
    rest = [i for i in range(len(MATRICES)) if i not in early]

    def named(indices, arrays):
        big = {}
        for i, t in zip(indices, arrays):
            big.setdefault(MATRICES[i][0], []).append(t)
        return {n: (v if n in ("mlp_w1", "mlp_w2") else v[0]) for n, v in big.items()}

    vectors = _pack([w[n] for n in SMALL_SHARDED])
    vectors = _into_slot("place_vectors", vectors, chip, N_CHIPS, F32, vectors.shape[0])
    *gathered, vectors = _gather_chips("gather_early", [slots[i] for i in early] + [vectors])
    send_sems, recv_sems, *in_flight, token = _gather_start("gather_rest_start", [slots[i] for i in rest], gathered[0])

    def late_weights(after):
        landed = _gather_wait("gather_rest_wait", in_flight, send_sems, recv_sems, after)
        return _prepare_matrices(named(rest, _hand_over("gather_rest_share", list(landed))))

    big = named(early, gathered)
    small_all = _unpack(vectors, sharded_shapes, lead=1)
    full = {n: w[n] for n in SMALL_REPLICATED}
    for n, t in zip(SMALL_SHARDED, small_all):
        full[n] = _join_chips(t, t.ndim - 2)

    def halves(t):
        return t.reshape(N_CHIPS, 2, t.shape[1] // 2, t.shape[2])

    in_flight_grads = {}

    def emit(tag, arrays32, arrays16):
        n = len(arrays16)
        send, recv, *rest = _reduce_start(f"reduce_{tag}_start", [halves(t) for t in arrays16])
        in_flight_grads[tag] = ([halves(t) for t in arrays32], rest[:n], rest[n:2 * n], send, recv)
        return rest[-1]

    loss_part, grad_x, g_kernel = _local_step(
        x[0], loss_target[0], _prepare_weights(big, full), token, late_weights, emit)
    g_big, g_full = _finish_grads(g_kernel)

    small_names = SMALL_REPLICATED + SMALL_SHARDED
    reduced_shapes = [g_full[n].shape for n in small_names] + [loss_part.shape]
    g_small = _pack([g_full[n] for n in small_names] + [loss_part])
    device = 2 * chip + core
    g_small = _into_slot("place_small", g_small, device, N_DEV, F32, g_small.shape[0])
    small_send, small_recv, small_in_flight, small_token = _gather_all_start("reduce_small_start", g_small)

    mine = {}
    for tag, (own, srcs, lands, send, recv) in in_flight_grads.items():
        landed = _reduce_wait(f"reduce_{tag}_wait", srcs, lands, send, recv, small_token)
        mine[tag] = [_reduce_sum(f"reduce_add_{tag}{i}", o, f, chip, core) for i, (o, f) in enumerate(zip(own, landed))]
    ordered = [mine["mlp0"][0], mine["mlp1"][0], mine["mlp0"][1], mine["mlp1"][1], mine["ab"][0], mine["mlp0"][2],
               *mine["gla"]]
    reduced = [t.reshape(2 * t.shape[1], t.shape[2]) for t in _pair_gather("reduce_share", ordered)]
    by_name = {n: [] for n, _ in MATRICES}
    for (n, _), t in zip(MATRICES, reduced):
        by_name[n].append(t)
    grads = {n: jnp.stack(v) for n, v in by_name.items()}

    g_small_all = _gather_all_wait("reduce_small_wait", small_in_flight, small_send, small_recv, reduced[0])
    g_small_red = _sum_blocks("reduce_small_add", g_small_all, g_small_all.shape[1])
    *small_red, loss_sum = _unpack(g_small_red, reduced_shapes)
    loss = loss_sum[0, 0]
    g_small_full = dict(zip(small_names, small_red))
    for n in SMALL_REPLICATED:
        grads[n] = g_small_full[n]
    for n in SMALL_SHARDED:
        width = w[n].shape[-1]
        grads[n] = lax.dynamic_slice_in_dim(g_small_full[n], chip * width, width, axis=g_small_full[n].ndim - 1)

    delta, new_m, new_v = {}, {}, {}
    for n in by_name:
        flat = [t.reshape(-1, t.shape[-1]) for t in (w[n], grads[n], m[n], v[n])]
        for dst, t in zip((delta, new_m, new_v), _adam(f"adam_{n}", *flat)):
            dst[n] = t.reshape(w[n].shape)
    small_shapes = [w[n].shape for n in small_names]
    packs = [_pack([src[n] for n in small_names]) for src in (w, grads, m, v)]
    d_small, m_small, v_small = _adam("adam_small", *packs)
    for dst, buf in ((delta, d_small), (new_m, m_small), (new_v, v_small)):
        dst.update(zip(small_names, _unpack(buf, small_shapes)))

    return (loss, grad_x[None], *[grads[n] for n in WEIGHTS], *[delta[n] for n in WEIGHTS],
            *[new_m[n] for n in WEIGHTS], *[new_v[n] for n in WEIGHTS])
```

```python
import functools

import jax
import jax.numpy as jnp
from jax import lax
from jax.experimental import pallas as pl
from jax.experimental.pallas import tpu as pltpu

F32 = jnp.float32
BF16 = jnp.bfloat16
MESH = pl.DeviceIdType.MESH

LANES = 128
CHUNK = 64
ATTN_SUB = 4
EPS = 1e-6
RG_C = 8.0
N_CHIPS = 4
N_DEV = 8
GLA_IN_WIDTH = 3104
GLA_IN_PAD = 3200
VMEM_LIMIT = 56 * 1024 * 1024

ADAM_LR = 0.001
ADAM_B1 = 0.9
ADAM_B2 = 0.999
ADAM_EPS = 1e-08
ADAM_WD = 0.01
ADAM_STEP = 10


def _raw_dot(a, b, ca, cb):
    return lax.dot_general(a.astype(BF16), b.astype(BF16), (((ca,), (cb,)), ((), ())),
                           preferred_element_type=F32)


def _raw_nn(a, b):
    return _raw_dot(a, b, 1, 0)


def _raw_nt(a, b):
    return _raw_dot(a, b, 1, 1)


def _raw_tn(a, b):
    return _raw_dot(a, b, 0, 0)


@jax.custom_vjp
def _dot_nn(a, b):
    return _raw_nn(a, b)


def _dot_nn_fwd(a, b):
    return _raw_nn(a, b), (a, b)


def _dot_nn_bwd(res, g):
    a, b = res
    return _raw_nt(g, b), _raw_tn(a, g)


_dot_nn.defvjp(_dot_nn_fwd, _dot_nn_bwd)


@jax.custom_vjp
def _dot_nt(a, b):
    return _raw_nt(a, b)


def _dot_nt_fwd(a, b):
    return _raw_nt(a, b), (a, b)


def _dot_nt_bwd(res, g):
    a, b = res
    return _raw_nn(g, b), _raw_tn(g, a)


_dot_nt.defvjp(_dot_nt_fwd, _dot_nt_bwd)


@jax.custom_vjp
def _dot_tn(a, b):
    return _raw_tn(a, b)


def _dot_tn_fwd(a, b):
    return _raw_tn(a, b), (a, b)


def _dot_tn_bwd(res, g):
    a, b = res
    return _raw_nt(b, g), _raw_nn(a, g)


_dot_tn.defvjp(_dot_tn_fwd, _dot_tn_bwd)


def _tile(n, pref):
    if n <= pref:
        return n
    t = (pref // LANES) * LANES
    while t > LANES and n % t:
        t -= LANES
    assert n % t == 0, (n, pref)
    return t


def _params(sem):
    return pltpu.CompilerParams(dimension_semantics=sem, vmem_limit_bytes=VMEM_LIMIT)


def _rowcall(name, fn, rows, pars, row_outs, par_outs=(), tm=512, pin=None):
    if pin is not None:
        inner, pars = fn, list(pars) + [pin]
        fn = lambda *vals: inner(*vals[:-1])
    n_rows = rows[0][0].shape[0]
    tm = min(tm, n_rows)
    assert n_rows % tm == 0
    n_r, n_p, n_ro = len(rows), len(pars), len(row_outs)

    def body(*refs):
        vals = [r[...].astype(F32) for r in refs[:n_r + n_p]]
        outs = fn(*vals)
        o_refs = refs[n_r + n_p:n_r + n_p + n_ro]
        po_refs = refs[n_r + n_p + n_ro:]
        for o_ref, val in zip(o_refs, outs[:n_ro]):
            o_ref[...] = val.astype(o_ref.dtype)
        first = pl.program_id(0) == 0
        for po_ref, val in zip(po_refs, outs[n_ro:]):
            @pl.when(first)
            def _():
                po_ref[...] = val

            @pl.when(jnp.logical_not(first))
            def _():
                po_ref[...] += val

    def const_map(nd):
        return lambda i: (0,) * nd

    def row_spec(w, cb):
        return pl.BlockSpec((tm, w), lambda i: (i, cb))

    in_specs = [row_spec(w, cb) for _, w, cb in rows]
    in_specs += [pl.BlockSpec(p.shape, const_map(p.ndim)) for p in pars]
    out_specs = [pl.BlockSpec((tm, w), lambda i: (i, 0)) for w, _ in row_outs]
    out_specs += [pl.BlockSpec(tuple(s), const_map(len(s))) for s in par_outs]
    out_shape = [jax.ShapeDtypeStruct((n_rows, w), dt) for w, dt in row_outs]
    out_shape += [jax.ShapeDtypeStruct(tuple(s), F32) for s in par_outs]
    return pl.pallas_call(
        body, name=name, grid=(n_rows // tm,), in_specs=in_specs, out_specs=out_specs, out_shape=out_shape,
        compiler_params=_params(("arbitrary",) if par_outs else ("parallel",)),
    )(*[r[0] for r in rows], *pars)


def _vjp_of(fn, n_prim, n_out, n_par, n_pass=0):
    def bwd(*args):
        prim = args[:n_prim]
        cts = args[n_prim:n_prim + n_out]
        passes = args[n_prim + n_out:n_prim + n_out + 2 * n_pass]
        pars = args[n_prim + n_out + 2 * n_pass:]
        _, vjp = jax.vjp(fn, *prim, *pars)
        grads = vjp(tuple(cts))
        sums = tuple(passes[2 * i] + passes[2 * i + 1] for i in range(n_pass))
        return tuple(grads[:n_prim]) + sums + tuple(grads[n_prim:])
    return bwd


def _mm(name, a, b, mode="nn", extras=(), epi=None, out_dtypes=(F32,), a_pro=None, out_split=None,
        epi_pars=(), row_sum=False, pin=None, k_whole=False, tm=1024, tn=1024, tk=1024):
    split = b.shape[0] if b.ndim == 3 else None
    b_rows, b_cols = b.shape[-2:]
    if mode == "nn":
        (m, k), n = a.shape, b_cols * (split or 1)
    elif mode == "nt":
        (m, k), n = a.shape, b_rows
        assert k == b_cols * (split or 1)
    else:
        assert split is None
        (k, m), n = a.shape, b_cols
    tm, tk = _tile(m, tm), _tile(k, tk)
    tn = _tile(n // out_split, tn) if out_split else _tile(n, tn)
    if split and mode == "nn":
        tn = _tile(b_cols, tn)
    if split and mode == "nt":
        tk = _tile(b_cols, tk)
    if k_whole:
        assert split and mode == "nt"
        tk = k
    n_b = split if k_whole else 1
    nk = k // tk
    raw = {"nn": _raw_nn, "nt": _raw_nt, "tn": _raw_tn}[mode]
    n_e, n_p, n_o = len(extras), len(epi_pars), len(out_dtypes)
    n_in = n_e + n_p + (0 if pin is None else 1)
    if epi is None:
        epi = lambda acc: (acc,)

    def body(a_ref, *rest):
        b_refs, rest = rest[:n_b], rest[n_b:]
        e_refs, p_refs, o_refs = rest[:n_e], rest[n_e:n_e + n_p], rest[n_in:n_in + n_o]
        kk = pl.program_id(2)
        a_tile = a_ref[...] if a_pro is None else a_pro(a_ref[...].astype(F32))
        part = raw(a_tile[:, :b_cols] if k_whole else a_tile, b_refs[0][...])
        for s in range(1, n_b):
            part = part + raw(a_tile[:, s * b_cols:(s + 1) * b_cols], b_refs[s][...])

        def finish(total):
            res = epi(total, *[e[...].astype(F32) for e in e_refs], *[p[...] for p in p_refs])
            for o_ref, r in zip(o_refs, res):
                o_ref[...] = r.astype(o_ref.dtype)
            if row_sum:
                rest[n_in + n_o][...] = res[n_o]

        if nk == 1:
            finish(part)
            return
        acc = rest[-1]

        @pl.when(kk == 0)
        def _():
            acc[...] = part

        @pl.when((kk > 0) & (kk < nk - 1))
        def _():
            acc[...] += part

        @pl.when(kk == nk - 1)
        def _():
            finish(acc[...] + part)

    a_spec = pl.BlockSpec((tk, tm), lambda i, j, kk: (kk, i)) if mode == "tn" else pl.BlockSpec((tm, tk), lambda i, j, kk: (i, kk))
    if split and mode == "nn":
        per = b_cols // tn
        b_spec = pl.BlockSpec((None, tk, tn), lambda i, j, kk: (j // per, kk, j % per))
    elif split:
        per = b_cols // tk
        b_spec = pl.BlockSpec((None, tn, tk), lambda i, j, kk: (kk // per, j, kk % per))
    elif mode == "nt":
        b_spec = pl.BlockSpec((tn, tk), lambda i, j, kk: (j, kk))
    else:
        b_spec = pl.BlockSpec((tk, tn), lambda i, j, kk: (kk, j))
    mn_spec = pl.BlockSpec((tm, tn), lambda i, j, kk: (i, j))
    if out_split:
        assert not extras
        per_out = n // out_split // tn
        out_spec = pl.BlockSpec((None, tm, tn), lambda i, j, kk: (j // per_out, i, j % per_out))
        out_shapes = [jax.ShapeDtypeStruct((out_split, m, n // out_split), dt) for dt in out_dtypes]
    else:
        out_spec = mn_spec
        out_shapes = [jax.ShapeDtypeStruct((m, n), dt) for dt in out_dtypes]
    out_specs = [out_spec] * n_o
    if row_sum:
        out_specs.append(pl.BlockSpec((None, 1, tn), lambda i, j, kk: (i, 0, j)))
        out_shapes.append(jax.ShapeDtypeStruct((m // tm, 1, n), F32))
    b_specs = [b_spec]
    if k_whole:
        b_specs = [pl.BlockSpec((None, tn, b_cols), functools.partial(lambda i, j, kk, s: (s, j, 0), s=s))
                   for s in range(split)]
    in_specs = [a_spec] + b_specs + [mn_spec] * n_e
    in_specs += [pl.BlockSpec(p.shape, functools.partial(lambda i, j, kk, nd: (0,) * nd, nd=p.ndim)) for p in epi_pars]
    in_specs += [] if pin is None else [pl.BlockSpec(memory_space=pl.ANY)]
    outs = pl.pallas_call(
        body, name=name, grid=(m // tm, n // tn, nk), in_specs=in_specs, out_specs=out_specs, out_shape=out_shapes,
        scratch_shapes=[pltpu.VMEM((tm, tn), F32)] if nk > 1 else [],
        compiler_params=_params(("parallel", "parallel", "arbitrary")),
    )(a, *[b] * n_b, *extras, *epi_pars, *([] if pin is None else [pin]))
    return outs[0] if len(outs) == 1 else outs


def _sigmoid(x):
    return jax.nn.sigmoid(x)


def _silu(x):
    return x * _sigmoid(x)


def _softplus(x):
    return jnp.maximum(x, 0.0) + jnp.log1p(jnp.exp(-jnp.abs(x)))


def _rmsnorm_fn(x, gain):
    return (x * lax.rsqrt(jnp.mean(x * x, axis=-1, keepdims=True) + EPS) * gain,)


def _head_norm(o, gain, n_heads):
    w = o.shape[-1] // n_heads
    parts = []
    for h in range(n_heads):
        oh = o[:, h * w:(h + 1) * w]
        parts.append(oh * lax.rsqrt(jnp.mean(oh * oh, axis=-1, keepdims=True) + EPS))
    return jnp.concatenate(parts, axis=-1) * gain


@jax.custom_jvp
def _neg_expm1(x):
    u = jnp.exp(x)
    is_one = u == 1.0
    return jnp.where(is_one, -x, (1.0 - u) * x / jnp.log(jnp.where(is_one, 2.0, u)))


@_neg_expm1.defjvp
def _neg_expm1_jvp(primals, tangents):
    (x,), (t,) = primals, tangents
    return _neg_expm1(x), -jnp.exp(x) * t


def _rg_gates_fn(xc, wa, wx, ba, bx, lam):
    outs = []
    for d in range(2):
        r = _sigmoid(_dot_nn(xc, wa[d]) + ba[d:d + 1])
        i = _sigmoid(_dot_nn(xc, wx[d]) + bx[d:d + 1])
        log_a = -RG_C * r * _softplus(-lam[d:d + 1])
        outs.append(jnp.exp(log_a))
        outs.append(jnp.sqrt(_neg_expm1(2.0 * log_a)) * (i * xc))
    return tuple(outs)


def _hg_pre_fn(q, f_f, f_b, logits):
    mx = jnp.maximum(logits[0:1], logits[1:2])
    e0 = jnp.exp(logits[0:1] - mx)
    e1 = jnp.exp(logits[1:2] - mx)
    lb = e0 / (e0 + e1)
    outs = [_silu(q)]
    for f in (f_f, f_b):
        outs.append((1.0 - lb) * _sigmoid(-f))
        outs.append(jnp.log(lb + (1.0 - lb) * _sigmoid(f)))
    return tuple(outs)


def _post0_fn(hs, ga, o, g, gain):
    ya = hs * jax.nn.gelu(ga, approximate=True)
    yb = _head_norm(o, gain, 4) * _silu(g)
    return (jnp.concatenate([ya, yb], axis=-1),)


def _post0_fwd_fn(h_f, h_b, ga, o_f, o_b, g, gain):
    return _post0_fn(h_f + h_b, ga, o_f + o_b, g, gain)


def _post0_bwd_fn(h_f, h_b, ga, o_f, o_b, g, dmix, gain):
    _, vjp = jax.vjp(_post0_fn, h_f + h_b, ga, o_f + o_b, g, gain)
    return vjp((dmix,))


def _gla_pre_fn(q, lr, w_up, b_gate):
    outs = [q * (128.0 ** -0.5)]
    for d in range(2):
        z = _dot_nn(lr, w_up[d]) + b_gate[d:d + 1]
        outs.append(-_softplus(-z) * (1.0 / 16.0))
    return tuple(outs)


def _gla_post_fn(o, r, gain):
    return (_head_norm(o, gain, 4) * _silu(r),)


def _gla_post_fwd_fn(o_f, o_b, r, gain):
    return _gla_post_fn(o_f + o_b, r, gain)


def _gla_post_bwd_fn(o_f, o_b, r, dmix, gain):
    _, vjp = jax.vjp(_gla_post_fn, o_f + o_b, r, gain)
    return vjp((dmix,))


def _relu2_bwd_epi(acc, hid):
    return (acc * 2.0 * jnp.maximum(hid, 0.0),)


def _relu2(x):
    r = jnp.maximum(x, 0.0)
    return r * r


def _add_epi(acc, res):
    return (acc + res,)


def _loss_head_fn(h, target, gain):
    def f(h, gain):
        y = _rmsnorm_fn(h, gain)[0]
        err = y - target
        return 0.5 * jnp.sum(jnp.mean(err * err, axis=-1, keepdims=True))
    loss, (dh, dgain) = jax.value_and_grad(f, argnums=(0, 1))(h, gain)
    return dh, dh, jnp.full((1, LANES), loss, F32), dgain


def _adam_fn(w, g, m, v):
    m2 = ADAM_B1 * m + (1.0 - ADAM_B1) * g
    v2 = ADAM_B2 * v + (1.0 - ADAM_B2) * (g * g)
    m_hat = m2 / (1.0 - ADAM_B1 ** ADAM_STEP)
    v_hat = v2 / (1.0 - ADAM_B2 ** ADAM_STEP)
    delta = -ADAM_LR * (m_hat / (jnp.sqrt(v_hat) + ADAM_EPS) + ADAM_WD * w)
    return delta, m2, v2


def _shifted(x, t_idx, off):
    n = x.shape[0]
    rolled = pltpu.roll(x, (-off) % n, 0)
    valid = (t_idx + off >= 0) & (t_idx + off < n)
    return jnp.where(valid, rolled, 0.0)


def _conv_fwd(name, src, colblock, w, b):
    n_rows, width = src.shape[0], w.shape[1]

    def body(x_ref, w_ref, b_ref, o_ref):
        x = x_ref[...]
        t_idx = lax.broadcasted_iota(jnp.int32, x.shape, 0)
        acc = b_ref[...] + w_ref[2:3, :] * x
        acc += w_ref[0:1, :] * _shifted(x, t_idx, -2)
        acc += w_ref[1:2, :] * _shifted(x, t_idx, -1)
        acc += w_ref[3:4, :] * _shifted(x, t_idx, 1)
        o_ref[...] = acc

    nb = width // LANES
    return pl.pallas_call(
        body, name=name, grid=(nb,),
        in_specs=[pl.BlockSpec((n_rows, LANES), lambda j: (0, colblock * nb + j)),
                  pl.BlockSpec((4, LANES), lambda j: (0, j)), pl.BlockSpec((1, LANES), lambda j: (0, j))],
        out_specs=pl.BlockSpec((n_rows, LANES), lambda j: (0, j)),
        out_shape=jax.ShapeDtypeStruct((n_rows, width), F32),
        compiler_params=_params(("parallel",)),
    )(src, w, b)


def _conv_bwd(name, src, colblock, w, d):
    n_rows, width = src.shape[0], w.shape[1]

    def body(x_ref, w_ref, d_ref, dx_ref, dw_ref, db_ref):
        x = x_ref[...]
        g = d_ref[...]
        t_idx = lax.broadcasted_iota(jnp.int32, x.shape, 0)
        dx = w_ref[2:3, :] * g
        dx += w_ref[0:1, :] * _shifted(g, t_idx, 2)
        dx += w_ref[1:2, :] * _shifted(g, t_idx, 1)
        dx += w_ref[3:4, :] * _shifted(g, t_idx, -1)
        dx_ref[...] = dx.astype(dx_ref.dtype)
        dw_ref[0:1, :] = jnp.sum(g * _shifted(x, t_idx, -2), axis=0, keepdims=True)
        dw_ref[1:2, :] = jnp.sum(g * _shifted(x, t_idx, -1), axis=0, keepdims=True)
        dw_ref[2:3, :] = jnp.sum(g * x, axis=0, keepdims=True)
        dw_ref[3:4, :] = jnp.sum(g * _shifted(x, t_idx, 1), axis=0, keepdims=True)
        db_ref[...] = jnp.sum(g, axis=0, keepdims=True)

    nb = width // LANES
    return pl.pallas_call(
        body, name=name, grid=(nb,),
        in_specs=[pl.BlockSpec((n_rows, LANES), lambda j: (0, colblock * nb + j)),
                  pl.BlockSpec((4, LANES), lambda j: (0, j)),
                  pl.BlockSpec((n_rows, LANES), lambda j: (0, j))],
        out_specs=[pl.BlockSpec((n_rows, LANES), lambda j: (0, j)), pl.BlockSpec((4, LANES), lambda j: (0, j)),
                   pl.BlockSpec((1, LANES), lambda j: (0, j))],
        out_shape=[jax.ShapeDtypeStruct((n_rows, width), BF16), jax.ShapeDtypeStruct((4, width), F32),
                   jax.ShapeDtypeStruct((1, width), F32)],
        compiler_params=_params(("parallel",)),
    )(src, w, d)


SUBLANES = 8
SCAN_UNROLL = 8


def _shift_rows(x, d, fill):
    n = x.shape[0]
    t = lax.broadcasted_iota(jnp.int32, x.shape, 0)
    valid = (t >= d) if d > 0 else (t < n + d)
    return jnp.where(valid, pltpu.roll(x, d % n, 0), fill)


def _tile_scan(a, u, reverse):
    d = 1
    while d < a.shape[0]:
        s = -d if reverse else d
        a_sh, u_sh = _shift_rows(a, s, 1.0), _shift_rows(u, s, 0.0)
        u = u + a * u_sh
        a = a * a_sh
        d *= 2
    return a, u


def _edge_row(x, reverse):
    return x[0:1, :] if reverse else x[SUBLANES - 1:SUBLANES, :]


def _scan_specs(n_rows, n):
    return [pl.BlockSpec((n_rows, LANES), lambda j: (0, j))] * n


def _scan_tile(a_ref, u_ref, h_ref, i, carry, reverse):
    n_tiles = a_ref.shape[0] // SUBLANES
    tile = (n_tiles - 1 - i) if reverse else i
    rows = pl.ds(pl.multiple_of(tile * SUBLANES, SUBLANES), SUBLANES)
    acc_a, acc_u = _tile_scan(a_ref[rows, :], u_ref[rows, :], reverse)
    h = acc_u + acc_a * carry
    h_ref[rows, :] = h
    return _edge_row(h, reverse)


def _scan_fwd(name, a_f, u_f, a_b, u_b):
    n_rows, width = a_f.shape

    def body(af_ref, uf_ref, ab_ref, ub_ref, hf_ref, hb_ref):
        def step(i, carry):
            return (_scan_tile(af_ref, uf_ref, hf_ref, i, carry[0], False),
                    _scan_tile(ab_ref, ub_ref, hb_ref, i, carry[1], True))
        zero = jnp.zeros((1, LANES), F32)
        lax.fori_loop(0, n_rows // SUBLANES, step, (zero, zero), unroll=SCAN_UNROLL)

    return pl.pallas_call(
        body, name=name, grid=(width // LANES,), in_specs=_scan_specs(n_rows, 4), out_specs=_scan_specs(n_rows, 2),
        out_shape=[jax.ShapeDtypeStruct((n_rows, width), F32)] * 2, compiler_params=_params(("parallel",)),
    )(a_f, u_f, a_b, u_b)


def _scan_bwd_tile(a_ref, h_ref, dh_ref, du_ref, da_ref, i, carry, reverse):
    n_rows = a_ref.shape[0]
    n_tiles = n_rows // SUBLANES
    against = not reverse
    one = -1 if against else 1
    g_in, a_edge = carry
    tile = (n_tiles - 1 - i) if against else i
    start = pl.multiple_of(tile * SUBLANES, SUBLANES)
    rows = pl.ds(start, SUBLANES)
    a_tile = a_ref[rows, :]
    coeff = _shift_rows(a_tile, one, a_edge)
    acc_a, acc_u = _tile_scan(coeff, dh_ref[rows, :], against)
    g = acc_u + acc_a * g_in
    du_ref[rows, :] = g
    outside = (start + SUBLANES) if reverse else (start - 1)
    inside = (outside >= 0) & (outside < n_rows)
    h_edge = jnp.where(inside, h_ref[pl.ds(jnp.clip(outside, 0, n_rows - 1), 1), :], 0.0)
    da_ref[rows, :] = g * _shift_rows(h_ref[rows, :], -one, h_edge)
    return _edge_row(g, against), _edge_row(a_tile, against)


def _scan_bwd(name, a_f, h_f, a_b, h_b, dh):
    n_rows, width = a_f.shape

    def body(af_ref, hf_ref, ab_ref, hb_ref, dh_ref, duf_ref, daf_ref, dub_ref, dab_ref):
        def step(i, carry):
            return (_scan_bwd_tile(af_ref, hf_ref, dh_ref, duf_ref, daf_ref, i, carry[0], False),
                    _scan_bwd_tile(ab_ref, hb_ref, dh_ref, dub_ref, dab_ref, i, carry[1], True))
        zero = jnp.zeros((1, LANES), F32)
        lax.fori_loop(0, n_rows // SUBLANES, step, ((zero, zero), (zero, zero)), unroll=SCAN_UNROLL)

    return pl.pallas_call(
        body, name=name, grid=(width // LANES,), in_specs=_scan_specs(n_rows, 5), out_specs=_scan_specs(n_rows, 4),
        out_shape=[jax.ShapeDtypeStruct((n_rows, width), F32)] * 4, compiler_params=_params(("parallel",)),
    )(a_f, h_f, a_b, h_b, dh)


def _tri_mask(c, reverse):
    row = lax.broadcasted_iota(jnp.int32, (c, c), 0)
    col = lax.broadcasted_iota(jnp.int32, (c, c), 1)
    return (col >= row) if reverse else (col <= row)


def _cumsum_rows(x, reverse):
    tri = _tri_mask(x.shape[0], reverse).astype(BF16)
    hi = x.astype(BF16)
    rest = x - hi.astype(F32)
    mid = rest.astype(BF16)
    lo = (rest - mid.astype(F32)).astype(BF16)
    return _raw_nn(tri, hi) + _raw_nn(tri, mid) + _raw_nn(tri, lo)


@functools.partial(jax.custom_vjp, nondiff_argnums=(1,))
def _cumsum(x, reverse):
    return _cumsum_rows(x, reverse)


def _cumsum_fwd(x, reverse):
    return _cumsum_rows(x, reverse), None


def _cumsum_bwd(reverse, _, g):
    return (_cumsum_rows(g, not reverse),)


_cumsum.defvjp(_cumsum_fwd, _cumsum_bwd)


def _chunks_fn(qs, ks, vs, lfs, sts, reverses):
    n, c = len(qs), qs[0].shape[0]
    every = range(n)
    tris = [_tri_mask(c, r) for r in reverses]
    cums = [_cumsum(lfs[i], reverses[i]) for i in every]
    rid = lax.broadcasted_iota(jnp.int32, cums[0].shape, 0)

    def pick(cum, r):
        return jnp.sum(jnp.where(rid == r, cum, 0.0), axis=0, keepdims=True)

    refs = [pick(cums[i], (c - 1 - c // 2) if reverses[i] else c // 2) for i in every]
    lasts = [pick(cums[i], 0 if reverses[i] else c - 1) for i in every]
    q_in = [qs[i] * jnp.exp(cums[i] - refs[i]) for i in every]
    k_in = [ks[i] * jnp.exp(refs[i] - cums[i]) for i in every]
    scores = [jnp.where(tris[i], _dot_nt(q_in[i], k_in[i]), 0.0) for i in every]
    o_intra = [_dot_nn(scores[i], vs[i]) for i in every]
    q_out = [qs[i] * jnp.exp(cums[i]) for i in every]
    o_inter = [_dot_nt(q_out[i], sts[i]) for i in every]
    k_state = [ks[i] * jnp.exp(lasts[i] - cums[i]) for i in every]
    upd = [_dot_tn(vs[i], k_state[i]) for i in every]
    st_new = [sts[i] * jnp.exp(lasts[i]) + upd[i] for i in every]
    return [o_intra[i] + o_inter[i] for i in every], st_new


def _attn_fwd(name, q, k_f, k_b, v, lf_f, lf_b, n_heads, dk, dv):
    n_rows = q[0].shape[0]
    n_chunks = n_rows // CHUNK
    n_steps = n_chunks // ATTN_SUB
    wk, wv = n_heads * dk, n_heads * dv

    def spec(width, off, rev):
        return pl.BlockSpec((CHUNK * ATTN_SUB, width), lambda n: ((n_steps - 1 - n) if rev else n, off))

    def sspec(rev):
        return pl.BlockSpec((ATTN_SUB, n_heads, dv, dk), lambda n: ((n_steps - 1 - n) if rev else n, 0, 0, 0))

    def body(qf, kf, vf, lff, qb, kb, vb, lfb, of_ref, ob_ref, sf_ref, sb_ref, st):
        @pl.when(pl.program_id(0) == 0)
        def _():
            st[...] = jnp.zeros_like(st)

        ins = ((qf, kf, vf, lff), (qb, kb, vb, lfb))
        chains = [(d, h) for d in range(2) for h in range(n_heads)]
        ck = [slice(h * dk, (h + 1) * dk) for h in range(n_heads)]
        cv = [slice(h * dv, (h + 1) * dv) for h in range(n_heads)]
        sts = [st[d, h] for d, h in chains]
        done = []
        for sub in range(ATTN_SUB):
            local = (sub, ATTN_SUB - 1 - sub)
            rows = [slice(local[d] * CHUNK, (local[d] + 1) * CHUNK) for d in range(2)]
            qs = [ins[d][0][rows[d], ck[h]] for d, h in chains]
            ks = [ins[d][1][rows[d], ck[h]] for d, h in chains]
            vs = [ins[d][2][rows[d], cv[h]] for d, h in chains]
            lfs = [ins[d][3][rows[d], ck[h]] for d, h in chains]
            os_, st_new = _chunks_fn(qs, ks, vs, lfs, sts, [d == 1 for d, _ in chains])
            done.append((local, rows, sts, os_))
            sts = st_new
        for local, rows, entered, os_ in done:
            for i, (d, h) in enumerate(chains):
                (sf_ref, sb_ref)[d][local[d], h] = entered[i].astype(BF16)
                (of_ref, ob_ref)[d][rows[d], cv[h]] = os_[i]
        for i, (d, h) in enumerate(chains):
            st[d, h] = sts[i]

    in_specs = [spec(wk, q[1], False), spec(wk, k_f[1], False), spec(wv, v[1], False), spec(wk, lf_f[1], False),
                spec(wk, q[1], True), spec(wk, k_b[1], True), spec(wv, v[1], True), spec(wk, lf_b[1], True)]
    return pl.pallas_call(
        body, name=name, grid=(n_steps,), in_specs=in_specs,
        out_specs=[spec(wv, 0, False), spec(wv, 0, True), sspec(False), sspec(True)],
        out_shape=[jax.ShapeDtypeStruct((n_rows, wv), F32)] * 2
        + [jax.ShapeDtypeStruct((n_chunks, n_heads, dv, dk), BF16)] * 2,
        scratch_shapes=[pltpu.VMEM((2, n_heads, dv, dk), F32)],
        compiler_params=_params(("arbitrary",)),
    )(q[0], k_f[0], v[0], lf_f[0], q[0], k_b[0], v[0], lf_b[0])


def _attn_bwd(name, q, k_f, k_b, v, lf_f, lf_b, st_f, st_b, do, n_heads, dk, dv, out_dtype=F32):
    n_rows = q[0].shape[0]
    n_chunks = n_rows // CHUNK
    n_steps = n_chunks // ATTN_SUB
    wk, wv = n_heads * dk, n_heads * dv

    def spec(width, off, rev):
        return pl.BlockSpec((CHUNK * ATTN_SUB, width), lambda n: (n if rev else (n_steps - 1 - n), off))

    def sspec(rev):
        return pl.BlockSpec((ATTN_SUB, n_heads, dv, dk), lambda n: (n if rev else (n_steps - 1 - n), 0, 0, 0))

    def body(qf, kf, vf, lff, sf, dof, qb, kb, vb, lfb, sb, dob,
             dqf, dkf, dvf, dlff, dqb, dkb, dvb, dlfb, dst):
        @pl.when(pl.program_id(0) == 0)
        def _():
            dst[...] = jnp.zeros_like(dst)

        ins = ((qf, kf, vf, lff, sf, dof), (qb, kb, vb, lfb, sb, dob))
        outs = ((dqf, dkf, dvf, dlff), (dqb, dkb, dvb, dlfb))
        chains = [(d, h) for d in range(2) for h in range(n_heads)]
        ck = [slice(h * dk, (h + 1) * dk) for h in range(n_heads)]
        cv = [slice(h * dv, (h + 1) * dv) for h in range(n_heads)]
        fn = functools.partial(_chunks_fn, reverses=[d == 1 for d, _ in chains])
        dsts = [dst[d, h] for d, h in chains]
        done = []
        for sub in range(ATTN_SUB):
            local = (ATTN_SUB - 1 - sub, sub)
            rows = [slice(local[d] * CHUNK, (local[d] + 1) * CHUNK) for d in range(2)]
            qs = [ins[d][0][rows[d], ck[h]] for d, h in chains]
            ks = [ins[d][1][rows[d], ck[h]] for d, h in chains]
            vs = [ins[d][2][rows[d], cv[h]] for d, h in chains]
            lfs = [ins[d][3][rows[d], ck[h]] for d, h in chains]
            sts = [ins[d][4][local[d], h].astype(F32) for d, h in chains]
            dos = [ins[d][5][rows[d], cv[h]] for d, h in chains]
            _, vjp = jax.vjp(fn, qs, ks, vs, lfs, sts)
            dqs, dks, dvs, dlfs, dsts = vjp((dos, dsts))
            done.append((rows, dqs, dks, dvs, dlfs))
        for rows, dqs, dks, dvs, dlfs in done:
            for i, (d, h) in enumerate(chains):
                dq_r, dk_r, dv_r, dlf_r = outs[d]
                dq_r[rows[d], ck[h]] = dqs[i].astype(dq_r.dtype)
                dk_r[rows[d], ck[h]] = dks[i].astype(dk_r.dtype)
                dv_r[rows[d], cv[h]] = dvs[i].astype(dv_r.dtype)
                dlf_r[rows[d], ck[h]] = dlfs[i].astype(dlf_r.dtype)
        for i, (d, h) in enumerate(chains):
            dst[d, h] = dsts[i]

    def dir_specs(kk, lf, rev):
        return [spec(wk, q[1], rev), spec(wk, kk[1], rev), spec(wv, v[1], rev), spec(wk, lf[1], rev), sspec(rev),
                spec(wv, 0, rev)]

    def dir_out_specs(rev):
        return [spec(wk, 0, rev), spec(wk, 0, rev), spec(wv, 0, rev), spec(wk, 0, rev)]

    shapes = [jax.ShapeDtypeStruct((n_rows, wk), out_dtype), jax.ShapeDtypeStruct((n_rows, wk), out_dtype),
              jax.ShapeDtypeStruct((n_rows, wv), out_dtype), jax.ShapeDtypeStruct((n_rows, wk), F32)]
    outs = pl.pallas_call(
        body, name=name, grid=(n_steps,), in_specs=dir_specs(k_f, lf_f, False) + dir_specs(k_b, lf_b, True),
        out_specs=dir_out_specs(False) + dir_out_specs(True), out_shape=shapes + shapes,
        scratch_shapes=[pltpu.VMEM((2, n_heads, dv, dk), F32)],
        compiler_params=_params(("arbitrary",)),
    )(q[0], k_f[0], v[0], lf_f[0], st_f, do, q[0], k_b[0], v[0], lf_b[0], st_b, do)
    return outs[:4], outs[4:]


def _row2(v):
    return v.reshape(1, -1)


def _mlp_fwd(tag, h, gain, w1, w2):
    y = _rowcall(f"{tag}_norm", _rmsnorm_fn, [(h, h.shape[1], 0)], [gain], [(h.shape[1], BF16)], tm=512)[0]
    hid = _mm(f"{tag}_up", y, w1, out_dtypes=(BF16,), tm=2048)
    h_out = _mm(f"{tag}_down", hid, w2, a_pro=_relu2, extras=(h,), epi=_add_epi, tk=2048)
    return h_out, (y, hid)


def _dw(name, a, b, **kw):
    return _mm(name, a, b, mode="tn", epi=lambda acc: (acc, acc), out_dtypes=(F32, BF16), **kw)


def _mlp_bwd(tag, h, gain, w1, w2, saved, dh_out):
    y, hid = saved
    dhid = _mm(f"{tag}_dact", dh_out[1], w2, mode="nt", extras=(hid,), epi=_relu2_bwd_epi, out_dtypes=(BF16,),
               tm=2048)
    dw2 = _dw(f"{tag}_dw2", hid, dh_out[1], a_pro=_relu2, tk=2048)
    dw1 = _dw(f"{tag}_dw1", y, dhid, out_split=N_CHIPS, tk=4096)
    dh, dgain = _dy_norm_bwd(f"{tag}_dy", dhid, w1, h, gain, dh_out[0], k_whole=True, tm=512)
    return dh, dgain, dw1, dw2


def _dy_norm_bwd(name, dz, w, h, gain, dres, pin=None, twice=True, **tiles):
    n_out = 2 if twice else 1

    def epi(dy, h_tile, dres_tile, gain_row):
        _, vjp = jax.vjp(lambda u, v: _rmsnorm_fn(u, v)[0], h_tile, gain_row)
        dh, dgain = vjp(dy)
        return (dh + dres_tile,) * n_out + (dgain,)

    assert h.shape[1] <= 1024
    tiles.setdefault("tm", 1024)
    *dh, dgain_parts = _mm(name, dz, w, mode="nt", extras=(h, dres), epi=epi, epi_pars=(gain,), row_sum=True,
                           out_dtypes=(F32, BF16)[:n_out], pin=pin, **tiles)
    return dh, jnp.sum(dgain_parts, axis=0)


def _local_step(x, target, w, pin=None, late=None, emit=None):
    g = {}
    d_model = x.shape[1]
    rg_w = hg_w = d_model // 2
    pins = []

    def send_off(tag, pairs):
        if emit is not None:
            pins.append(emit(tag, [p[0] for p in pairs], [p[1] for p in pairs]))

    def both(fn, pair):
        return [fn(t) for t in pair]

    def chip_major(t):
        return t.reshape(N_CHIPS, t.shape[0] // N_CHIPS, t.shape[1])

    h_a0 = x
    gain = _row2(w["norm_mix"][0])
    y0 = _rowcall("l0_norm", _rmsnorm_fn, [(h_a0, d_model, 0)], [gain], [(d_model, BF16)], tm=512, pin=pin)[0]
    proj0 = _mm("l0_in", y0, w["ab_w_in"], tm=2048)
    conv_w, conv_b = w["rg_conv_w"], _row2(w["rg_conv_b"])
    xc = _conv_fwd("rg_conv", proj0, 0, conv_w, conv_b)
    gate_pars = [w["rg_wa_bd"], w["rg_wx_bd"], w["rg_b_a"], w["rg_b_x"], w["rg_lambda"]]
    a_f, u_f, a_b, u_b = _rowcall("rg_gates", _rg_gates_fn, [(xc, rg_w, 0)], gate_pars, [(rg_w, F32)] * 4)
    hs_f, hs_b = _scan_fwd("rg_scan", a_f, u_f, a_b, u_b)
    hg_rows = [(proj0, hg_w, 2), (proj0, hg_w, 3), (proj0, hg_w, 4)]
    qh, k_f, lf_f, k_b, lf_b = _rowcall("hg_pre", _hg_pre_fn, hg_rows, [w["hg_lb_logits"]], [(hg_w, F32)] * 5)
    iv = (proj0, 5)
    o_f, o_b, st_f, st_b = _attn_fwd("hg_attn", (qh, 0), (k_f, 0), (k_b, 0), iv, (lf_f, 0), (lf_b, 0), 4, 128, 128)
    post0_rows = [(hs_f, rg_w, 0), (hs_b, rg_w, 0), (proj0, rg_w, 1), (o_f, hg_w, 0), (o_b, hg_w, 0), (proj0, hg_w, 6)]
    hg_gain = _row2(w["hg_norm"])
    mix_in0 = _rowcall("l0_post", _post0_fwd_fn, post0_rows, [hg_gain], [(d_model, BF16)])[0]
    if late is not None:
        w = {**w, **late(mix_in0)}
    h_b0 = _mm("l0_out", mix_in0, w["ab_w_out"], extras=(h_a0,), epi=_add_epi)
    h_c0, mlp0 = _mlp_fwd("mlp0", h_b0, _row2(w["norm_mlp"][0]), w["mlp_w1"][0], w["mlp_w2"][0])

    h_a1 = h_c0
    gain1 = _row2(w["norm_mix"][1])
    y1 = _rowcall("l1_norm", _rmsnorm_fn, [(h_a1, d_model, 0)], [gain1], [(d_model, BF16)], tm=512)[0]
    proj1 = _mm("l1_in", y1, w["gla_w_in_pad"], tm=512, tn=GLA_IN_PAD)
    gla_pars = [w["gla_w_up_pad"], w["gla_b_gate"]]
    gq, glf_f, glf_b = _rowcall("gla_pre", _gla_pre_fn, [(proj1, 512, 0), (proj1, LANES, 24)], gla_pars, [(512, F32)] * 3)
    gk, gv = (proj1, 1), (proj1, 1)
    go_f, go_b, gst_f, gst_b = _attn_fwd("gla_attn", (gq, 0), gk, gk, gv, (glf_f, 0), (glf_b, 0), 4, 128, 256)
    gla_gain = _row2(w["gla_norm"])
    post1_rows = [(go_f, d_model, 0), (go_b, d_model, 0), (proj1, d_model, 2)]
    mix_in1 = _rowcall("l1_post", _gla_post_fwd_fn, post1_rows, [gla_gain], [(d_model, BF16)])[0]
    h_b1 = _mm("l1_out", mix_in1, w["gla_w_out"], extras=(h_a1,), epi=_add_epi)
    h_c1, mlp1 = _mlp_fwd("mlp1", h_b1, _row2(w["norm_mlp"][1]), w["mlp_w1"][1], w["mlp_w2"][1])

    *dh, loss, g["norm_final"] = _rowcall(
        "loss_head", _loss_head_fn, [(h_c1, d_model, 0), (target, d_model, 0)], [_row2(w["norm_final"])],
        [(d_model, F32), (d_model, BF16)], [(1, LANES), (1, d_model)], tm=512)

    dh, g_nmlp1, g_w1_1, g_w2_1 = _mlp_bwd("mlp1", h_b1, _row2(w["norm_mlp"][1]), w["mlp_w1"][1], w["mlp_w2"][1], mlp1, dh)
    send_off("mlp1", [g_w1_1, both(chip_major, g_w2_1)])
    dmix1 = _mm("l1_dout", dh[1], w["gla_w_out"], mode="nt")
    g_gla_out = _dw("l1_dwout", mix_in1, dh[1])
    g["gla_w_out"] = g_gla_out[0]
    dgo, dr, g["gla_norm"] = _rowcall(
        "l1_dpost", _gla_post_bwd_fn, post1_rows + [(dmix1, d_model, 0)], [gla_gain],
        [(d_model, F32), (d_model, BF16)], [(1, d_model)], pin=pins.pop() if pins else None)
    (dq_f, dk_f, dv_f, dlf_f), (dq_b, dk_b, dv_b, dlf_b) = _attn_bwd(
        "gla_dattn", (gq, 0), gk, gk, gv, (glf_f, 0), (glf_b, 0), gst_f, gst_b, dgo, 4, 128, 256)

    def gla_pre_bwd(q, lr, dq1, dq2, dlf1, dlf2, dk1, dk2, dv1, dv2, w_up, b_gate):
        dlr = jnp.zeros_like(lr)
        dws, dbs = [], []
        for d, dlf in enumerate((dlf1, dlf2)):
            z = _raw_nn(lr, w_up[d]) + b_gate[d:d + 1]
            dz = dlf * _sigmoid(-z) * (1.0 / 16.0)
            dlr = dlr + _raw_nt(dz, w_up[d])
            dws.append(_raw_tn(dz, lr))
            dbs.append(jnp.sum(dz, axis=0, keepdims=True))
        return ((dq1 + dq2) * (128.0 ** -0.5), dk1 + dk2, dv1 + dv2, dlr, dws[0], dws[1], dbs[0], dbs[1])

    rows = [(proj1, 512, 0), (proj1, LANES, 24), (dq_f, 512, 0), (dq_b, 512, 0), (dlf_f, 512, 0), (dlf_b, 512, 0),
            (dk_f, 512, 0), (dk_b, 512, 0), (dv_f, d_model, 0), (dv_b, d_model, 0)]
    dq, dk, dv, dlr, dwt_f, dwt_b, db_f, db_b = _rowcall(
        "gla_dpre", gla_pre_bwd, rows, gla_pars, [(512, BF16), (512, BF16), (d_model, BF16), (LANES, BF16)],
        [(512, LANES), (512, LANES), (1, 512), (1, 512)])
    g["gla_w_up_pad"] = jnp.stack([dwt_f.T, dwt_b.T])
    g["gla_b_gate"] = jnp.concatenate([db_f, db_b], axis=0)
    dproj1 = jnp.concatenate([dq, dk, dv, dr, dlr], axis=1)
    g_gla_in = both(lambda t: _split_chips(t[:, :GLA_IN_WIDTH], 1), _dw("l1_dwin", y1, dproj1, tn=640, tk=4096))
    g["gla_w_in"] = g_gla_in[0]
    send_off("gla", [g_gla_in, both(chip_major, g_gla_out)])
    dh, g_nmix1 = _dy_norm_bwd("l1_dy", dproj1, w["gla_w_in_pad"], h_a1, gain1, dh[0],
                               pin=pins.pop() if pins else None, tk=GLA_IN_PAD)

    dh, g_nmlp0, g_w1_0, g_w2_0 = _mlp_bwd("mlp0", h_b0, _row2(w["norm_mlp"][0]), w["mlp_w1"][0], w["mlp_w2"][0], mlp0, dh)
    g_ab_out = _dw("l0_dwout", mix_in0, dh[1])
    g["ab_w_out"] = g_ab_out[0]
    send_off("mlp0", [g_w1_0, both(chip_major, g_w2_0), both(chip_major, g_ab_out)])
    dmix0 = _mm("l0_dout", dh[1], w["ab_w_out"], mode="nt")
    dhs, dga, do, dg, g["hg_norm"] = _rowcall(
        "l0_dpost", _post0_bwd_fn, post0_rows + [(dmix0, d_model, 0)], [hg_gain],
        [(rg_w, F32), (rg_w, BF16), (hg_w, F32), (hg_w, BF16)], [(1, hg_w)], pin=pins.pop() if pins else None)
    (dqh_f, dk_f, div_f, dlf_f), (dqh_b, dk_b, div_b, dlf_b) = _attn_bwd(
        "hg_dattn", (qh, 0), (k_f, 0), (k_b, 0), iv, (lf_f, 0), (lf_b, 0), st_f, st_b, do, 4, 128, 128)

    def hg_pre_bwd(q, f_f, f_b, dq1, dq2, dk1, dlf1, dk2, dlf2, dv1, dv2, logits):
        _, vjp = jax.vjp(_hg_pre_fn, q, f_f, f_b, logits)
        dq, df_f, df_b, dlogits = vjp((dq1 + dq2, dk1, dlf1, dk2, dlf2))
        return dq, df_f, df_b, dv1 + dv2, dlogits

    rows = hg_rows + [(t, hg_w, 0) for t in (dqh_f, dqh_b, dk_f, dlf_f, dk_b, dlf_b, div_f, div_b)]
    dq, df_f, df_b, div, g["hg_lb_logits"] = _rowcall(
        "hg_dpre", hg_pre_bwd, rows, [w["hg_lb_logits"]], [(hg_w, BF16)] * 4, [(2, hg_w)])
    du_f, da_f, du_b, da_b = _scan_bwd("rg_dscan", a_f, hs_f, a_b, hs_b, dhs)
    gates_bwd = _vjp_of(_rg_gates_fn, 1, 4, 5)
    rows = [(xc, rg_w, 0), (da_f, rg_w, 0), (du_f, rg_w, 0), (da_b, rg_w, 0), (du_b, rg_w, 0)]
    dxc, g["rg_wa_bd"], g["rg_wx_bd"], g["rg_b_a"], g["rg_b_x"], g["rg_lambda"] = _rowcall(
        "rg_dgates", gates_bwd, rows, gate_pars, [(rg_w, F32)],
        [(2, rg_w, rg_w), (2, rg_w, rg_w), (2, rg_w), (2, rg_w), (2, rg_w)])
    dxa, g["rg_conv_w"], g["rg_conv_b"] = _conv_bwd("rg_dconv", proj0, 0, conv_w, dxc)
    dproj0 = jnp.concatenate([dxa, dga, dq, df_f, df_b, div, dg], axis=1)
    g_ab_in = _dw("l0_dwin", y0, dproj0, out_split=N_CHIPS, tk=4096)
    g["ab_w_in"] = g_ab_in[0]
    send_off("ab", [g_ab_in])
    (grad_x,), g_nmix0 = _dy_norm_bwd("l0_dy", dproj0, w["ab_w_in"], h_a0, gain, dh[0],
                                      pin=pins.pop() if pins else None, twice=False, k_whole=True)

    g["norm_mix"] = jnp.concatenate([g_nmix0, g_nmix1], axis=0)
    g["norm_mlp"] = jnp.concatenate([g_nmlp0, g_nmlp1], axis=0)
    g["mlp_w1"] = [g_w1_0[0], g_w1_1[0]]
    g["mlp_w2"] = [g_w2_0[0], g_w2_1[0]]
    return loss, grad_x, g


def _block_diag(w):
    d, g, n, _ = w.shape
    eye = jnp.eye(g, dtype=w.dtype)
    return (w[:, :, :, None, :] * eye[None, :, None, :, None]).reshape(d, g * n, g * n)


def _block_diag_extract(wbd, g):
    d, gn, _ = wbd.shape
    n = gn // g
    blocks = wbd.reshape(d, g, n, g, n)
    return jnp.stack([blocks[:, i, :, i, :] for i in range(g)], axis=1)


def _prepare_weights(big, full):
    w = {k: full[k] for k in ("norm_mix", "norm_mlp", "norm_final", "hg_lb_logits")}
    for k in ("rg_conv_w", "rg_conv_b", "rg_b_a", "rg_b_x", "rg_lambda", "hg_norm", "gla_b_gate", "gla_norm"):
        w[k] = full[k][0]
    w["rg_wa_bd"] = _block_diag(full["rg_w_a"][0])
    w["rg_wx_bd"] = _block_diag(full["rg_w_x"][0])
    up = full["gla_w_gate_up"][0]
    rank = up.shape[1]
    pad = jnp.zeros((2, LANES, up.shape[2]), F32)
    w["gla_w_up_pad"] = pad.at[0, 0:rank].set(up[0]).at[1, rank:2 * rank].set(up[1])
    w.update(_prepare_matrices(big))
    return w


def _prepare_matrices(big):
    w = {}
    if "mlp_w1" in big:
        w["mlp_w1"] = list(big["mlp_w1"])
        w["mlp_w2"] = [t.reshape(-1, t.shape[-1]) for t in big["mlp_w2"]]
    if "ab_w_in" in big:
        w["ab_w_in"] = big["ab_w_in"]
    if "ab_w_out" in big:
        w["ab_w_out"] = big["ab_w_out"].reshape(-1, big["ab_w_out"].shape[-1])
    if "gla_w_in" in big:
        w["gla_w_out"] = big["gla_w_out"].reshape(-1, big["gla_w_out"].shape[-1])
        gla_in = _join_chips(big["gla_w_in"], 1)
        w["gla_w_in_pad"] = jnp.pad(gla_in, ((0, 0), (0, GLA_IN_PAD - gla_in.shape[1])))
    return w


def _finish_grads(g, rank=16, rg_blocks=8):
    def chip_major(t):
        return t.reshape(N_CHIPS, t.shape[0] // N_CHIPS, t.shape[1])

    big = {
        "mlp_w1": list(g["mlp_w1"]), "mlp_w2": [chip_major(t) for t in g["mlp_w2"]],
        "ab_w_in": g["ab_w_in"], "ab_w_out": chip_major(g["ab_w_out"]),
        "gla_w_in": g["gla_w_in"], "gla_w_out": chip_major(g["gla_w_out"]),
    }
    small = {
        "norm_mix": g["norm_mix"], "norm_mlp": g["norm_mlp"], "norm_final": g["norm_final"][0],
        "rg_conv_w": g["rg_conv_w"][None], "rg_conv_b": g["rg_conv_b"],
        "rg_w_a": _block_diag_extract(g["rg_wa_bd"], rg_blocks)[None], "rg_b_a": g["rg_b_a"][None],
        "rg_w_x": _block_diag_extract(g["rg_wx_bd"], rg_blocks)[None], "rg_b_x": g["rg_b_x"][None],
        "rg_lambda": g["rg_lambda"][None], "hg_lb_logits": g["hg_lb_logits"], "hg_norm": g["hg_norm"],
        "gla_w_gate_up": jnp.stack([g["gla_w_up_pad"][0, 0:rank], g["gla_w_up_pad"][1, rank:2 * rank]])[None],
        "gla_b_gate": g["gla_b_gate"][None], "gla_norm": g["gla_norm"],
    }
    return big, small


MATRICES = (("mlp_w1", 0), ("mlp_w1", 1), ("mlp_w2", 0), ("mlp_w2", 1), ("ab_w_in", 0), ("ab_w_out", 0),
            ("gla_w_in", 0), ("gla_w_out", 0))
EARLY_MATRICES = ("ab_w_in",)
SMALL_SHARDED = ("rg_conv_w", "rg_b_a", "rg_b_x", "rg_lambda", "gla_w_gate_up", "gla_b_gate", "gla_norm")
SMALL_REPLICATED = ("norm_mix", "norm_mlp", "norm_final", "rg_conv_b", "rg_w_a", "rg_w_x", "hg_lb_logits", "hg_norm")
WEIGHTS = ("norm_mix", "norm_mlp", "norm_final", "mlp_w1", "mlp_w2", "ab_w_in", "ab_w_out", "rg_conv_w", "rg_conv_b",
           "rg_w_a", "rg_b_a", "rg_w_x", "rg_b_x", "rg_lambda", "hg_lb_logits", "hg_norm", "gla_w_in", "gla_w_out",
           "gla_w_gate_up", "gla_b_gate", "gla_norm")
ROW_ALIGN = 16


def _pack(arrays, lead=0):
    head = arrays[0].shape[:lead]
    flat = jnp.concatenate([a.reshape(head + (-1,)) for a in arrays], axis=lead)
    n = flat.shape[-1]
    quantum = LANES * ROW_ALIGN
    padded = -(-n // quantum) * quantum
    if padded != n:
        flat = jnp.pad(flat, [(0, 0)] * lead + [(0, padded - n)])
    return flat.reshape(head + (padded // LANES, LANES))


def _unpack(buf, shapes, lead=0):
    head = buf.shape[:lead]
    flat = buf.reshape(head + (-1,))
    out, off = [], 0
    for s in shapes:
        n = 1
        for v in s:
            n *= v
        out.append(lax.slice_in_dim(flat, off, off + n, axis=lead).reshape(head + tuple(s)))
        off += n
    return out


def _join_chips(gathered, axis):
    t = jnp.moveaxis(gathered, 0, axis)
    return t.reshape(t.shape[:axis] + (t.shape[axis] * t.shape[axis + 1],) + t.shape[axis + 2:])


def _split_chips(full, axis):
    s = full.shape
    t = full.reshape(s[:axis] + (N_CHIPS, s[axis] // N_CHIPS) + s[axis + 1:])
    return jnp.moveaxis(t, axis, 0)


_ANY = pl.BlockSpec(memory_space=pl.ANY)


def _place():
    return lax.axis_index("x"), lax.axis_index("y"), lax.axis_index("c")


def _into_slot(name, src, slot, n_slots, dtype, tm, layer=None):
    r, lanes = src.shape[-2:]
    tm = _row_tile(r, tm, ROW_ALIGN)

    def body(slot_ref, in_ref, o_ref):
        o_ref[...] = in_ref[...].astype(o_ref.dtype)

    if layer is None:
        in_spec = pl.BlockSpec((tm, lanes), lambda i, slot_ref: (i, 0))
    else:
        in_spec = pl.BlockSpec((None, tm, lanes), lambda i, slot_ref: (layer, i, 0))
    grid_spec = pltpu.PrefetchScalarGridSpec(
        num_scalar_prefetch=1, grid=(r // tm,), in_specs=[in_spec],
        out_specs=pl.BlockSpec((None, tm, lanes), lambda i, slot_ref: (slot_ref[0], i, 0)))
    return pl.pallas_call(
        body, name=name, grid_spec=grid_spec, out_shape=jax.ShapeDtypeStruct((n_slots, r, lanes), dtype),
        compiler_params=_params(("parallel",)),
    )(slot.reshape(1).astype(jnp.int32), src)


def _chip_peers():
    x, y, c = _place()
    return 2 * x + y, c, [(1 - x, y), (x, 1 - y), (1 - x, 1 - y)]


def _comm_call(name, body, ins, out_shapes, n_sems, aliases=None):
    return pl.pallas_call(
        body, name=name, in_specs=[_ANY] * len(ins), out_specs=[_ANY] * len(out_shapes), out_shape=out_shapes,
        input_output_aliases=aliases or {},
        scratch_shapes=[pltpu.SemaphoreType.DMA((n_sems,)), pltpu.SemaphoreType.DMA((n_sems,))],
    )(*ins)


def _gather_chips(name, bufs):
    n = len(bufs)

    def body(*refs):
        outs, send_sems, recv_sems = refs[n:2 * n], refs[2 * n], refs[2 * n + 1]
        x, y, c = _place()
        me, _, peers = _chip_peers()

        def rows(a, block, half):
            rh = outs[a].shape[1] // 2
            return outs[a].at[block, pl.ds(half * rh, rh)]

        def copy(a, j, block, half, to, sem):
            return pltpu.make_async_remote_copy(
                src_ref=rows(a, block, half), dst_ref=rows(a, block, half), send_sem=send_sems.at[sem],
                recv_sem=recv_sems.at[sem], device_id=to, device_id_type=MESH)

        def over_ici(a, j, block):
            px, py = peers[j]
            return copy(a, j, block, c, (px, py, c), 6 * a + j)

        def to_sibling(a, j, block, half):
            return copy(a, j, block, half, (x, y, 1 - c), 6 * a + 3 + j)

        sends = [over_ici(a, j, me) for a in range(n) for j in range(3)]
        for cp in sends:
            cp.start()
        for a in range(n):
            for j, (px, py) in enumerate(peers):
                over_ici(a, j, 2 * px + py).wait_recv()
                handed = to_sibling(a, j, 2 * px + py, c)
                handed.start()
                sends.append(handed)
        for a in range(n):
            for j, (px, py) in enumerate(peers):
                to_sibling(a, j, 2 * px + py, 1 - c).wait_recv()
        for cp in sends:
            cp.wait_send()

    shapes = [jax.ShapeDtypeStruct(b.shape, b.dtype) for b in bufs]
    return _comm_call(name, body, bufs, shapes, 6 * n, {a: a for a in range(n)})


_HBM = pl.BlockSpec(memory_space=pltpu.HBM)
_SEM = pl.BlockSpec(memory_space=pltpu.SEMAPHORE)
_EFFECT = pltpu.SideEffectType.DATAFLOW_SIDE_EFFECTING


def _half_rows(ref, block, half):
    rh = ref.shape[1] // 2
    return ref.at[block, pl.ds(half * rh, rh)]


def _gather_start(name, bufs, after):
    n = len(bufs)

    def body(*refs):
        ins, send_sems, recv_sems, token = refs[:n], refs[n + 1], refs[n + 2], refs[-1]
        me, c, peers = _chip_peers()
        for a in range(n):
            mine = _half_rows(ins[a], me, c)
            for j, (px, py) in enumerate(peers):
                pltpu.make_async_remote_copy(
                    src_ref=mine, dst_ref=mine, send_sem=send_sems.at[3 * a + j], recv_sem=recv_sems.at[3 * a + j],
                    device_id=(px, py, c), device_id_type=MESH).start()
        token[...] = jnp.zeros_like(token)

    out_shape = (pltpu.SemaphoreType.DMA((3 * n,)), pltpu.SemaphoreType.DMA((3 * n,)),
                 *[pltpu.HBM(b.shape, b.dtype) for b in bufs], jax.ShapeDtypeStruct((8, LANES), F32))
    return pl.pallas_call(
        body, name=name, out_shape=out_shape, in_specs=[_HBM] * n + [_ANY],
        out_specs=(_SEM, _SEM, *[_HBM] * n, pl.BlockSpec(memory_space=pltpu.VMEM)),
        input_output_aliases={a: 2 + a for a in range(n)},
        compiler_params=pltpu.CompilerParams(has_side_effects=_EFFECT),
    )(*[pltpu.with_memory_space_constraint(b, pltpu.HBM) for b in bufs], after)


def _gather_wait(name, bufs, send_sems, recv_sems, after):
    n = len(bufs)

    def body(*refs):
        ins, send_sems, recv_sems = refs[:n], refs[n], refs[n + 1]
        me, c, peers = _chip_peers()
        for a in range(n):
            for j, (px, py) in enumerate(peers):
                copy = pltpu.make_async_remote_copy(
                    src_ref=_half_rows(ins[a], me, c), dst_ref=_half_rows(ins[a], 2 * px + py, c),
                    send_sem=send_sems.at[3 * a + j], recv_sem=recv_sems.at[3 * a + j],
                    device_id=(px, py, c), device_id_type=MESH)
                copy.wait_send()
                copy.wait_recv()

    return pl.pallas_call(
        body, name=name, out_shape=tuple(pltpu.HBM(b.shape, b.dtype) for b in bufs),
        in_specs=[_HBM] * n + [_SEM, _SEM, _ANY], out_specs=tuple([_HBM] * n),
        input_output_aliases={a: a for a in range(n)},
        compiler_params=pltpu.CompilerParams(has_side_effects=_EFFECT),
    )(*bufs, send_sems, recv_sems, after)


def _hand_over(name, bufs):
    n = len(bufs)

    def body(*refs):
        outs, send_sems, recv_sems = refs[n:2 * n], refs[2 * n], refs[2 * n + 1]
        x, y, c = _place()
        _, _, peers = _chip_peers()

        def copy(a, j, half):
            px, py = peers[j]
            rows = _half_rows(outs[a], 2 * px + py, half)
            return pltpu.make_async_remote_copy(
                src_ref=rows, dst_ref=rows, send_sem=send_sems.at[3 * a + j], recv_sem=recv_sems.at[3 * a + j],
                device_id=(x, y, 1 - c), device_id_type=MESH)

        sends = [copy(a, j, c) for a in range(n) for j in range(3)]
        for cp in sends:
            cp.start()
        for a in range(n):
            for j in range(3):
                copy(a, j, 1 - c).wait_recv()
        for cp in sends:
            cp.wait_send()

    shapes = [jax.ShapeDtypeStruct(b.shape, b.dtype) for b in bufs]
    return _comm_call(name, body, bufs, shapes, 3 * n, {a: a for a in range(n)})


def _pair_gather(name, bufs):
    n = len(bufs)

    def body(*refs):
        ins, outs, send_sems, recv_sems = refs[:n], refs[n:2 * n], refs[2 * n], refs[2 * n + 1]
        x, y, c = _place()

        def copy(a, block):
            return pltpu.make_async_remote_copy(
                src_ref=ins[a].at[block], dst_ref=outs[a].at[block], send_sem=send_sems.at[a],
                recv_sem=recv_sems.at[a], device_id=(x, y, 1 - c), device_id_type=MESH)

        sends = [copy(a, c) for a in range(n)]
        for cp in sends:
            cp.start()
        for a in range(n):
            copy(a, 1 - c).wait_recv()
        for cp in sends:
            cp.wait_send()

    shapes = [jax.ShapeDtypeStruct(b.shape, b.dtype) for b in bufs]
    return _comm_call(name, body, bufs, shapes, n, {a: a for a in range(n)})


def _all_peers():
    x, y, c = _place()
    peers = []
    for mask in range(1, N_DEV):
        fx, fy, fc = (mask >> 2) & 1, (mask >> 1) & 1, mask & 1
        peers.append((jnp.where(fx, 1 - x, x), jnp.where(fy, 1 - y, y), jnp.where(fc, 1 - c, c)))
    return 4 * x + 2 * y + c, peers


def _reduce_copies(srcs, lands, send_sems, recv_sems):
    me, peers = _all_peers()
    sends, arrivals = [], []
    for a in range(len(srcs)):
        for j, (px, py, pc) in enumerate(peers):
            k = (N_DEV - 1) * a + j
            sends.append(pltpu.make_async_remote_copy(
                src_ref=srcs[a].at[2 * px + py, pc], dst_ref=lands[a].at[me], send_sem=send_sems.at[k],
                recv_sem=recv_sems.at[k], device_id=(px, py, pc), device_id_type=MESH))
            arrivals.append(pltpu.make_async_remote_copy(
                src_ref=srcs[a].at[2 * px + py, pc], dst_ref=lands[a].at[4 * px + 2 * py + pc],
                send_sem=send_sems.at[k], recv_sem=recv_sems.at[k], device_id=(px, py, pc), device_id_type=MESH))
    return sends, arrivals


def _reduce_direct(name, srcs, pin=None):
    n = len(srcs)
    extra = [] if pin is None else [pin]

    def body(*refs):
        ins, outs = refs[:n], refs[n + len(extra):2 * n + len(extra)]
        sends, arrivals = _reduce_copies(ins, outs, refs[-2], refs[-1])
        for cp in sends:
            cp.start()
        for cp in arrivals:
            cp.wait_recv()
        for cp in sends:
            cp.wait_send()

    shapes = [jax.ShapeDtypeStruct((N_DEV,) + s.shape[2:], s.dtype) for s in srcs]
    return _comm_call(name, body, list(srcs) + extra, shapes, (N_DEV - 1) * n)


def _reduce_start(name, srcs):
    n = len(srcs)
    lands = [lax.empty((N_DEV,) + s.shape[2:], s.dtype) for s in srcs]

    def body(*refs):
        sends, _ = _reduce_copies(refs[:n], refs[n:2 * n], refs[2 * n], refs[2 * n + 1])
        for cp in sends:
            cp.start()
        refs[-1][...] = jnp.zeros_like(refs[-1])

    bufs = list(srcs) + lands
    n_sems = (N_DEV - 1) * n
    out_shape = (pltpu.SemaphoreType.DMA((n_sems,)), pltpu.SemaphoreType.DMA((n_sems,)),
                 *[pltpu.HBM(b.shape, b.dtype) for b in bufs], jax.ShapeDtypeStruct((8, LANES), F32))
    return pl.pallas_call(
        body, name=name, out_shape=out_shape, in_specs=[_HBM] * (2 * n),
        out_specs=(_SEM, _SEM, *[_HBM] * (2 * n), pl.BlockSpec(memory_space=pltpu.VMEM)),
        input_output_aliases={a: 2 + a for a in range(2 * n)},
        compiler_params=pltpu.CompilerParams(has_side_effects=_EFFECT),
    )(*[pltpu.with_memory_space_constraint(b, pltpu.HBM) for b in bufs])


def _reduce_wait(name, srcs, lands, send_sems, recv_sems, after):
    n = len(srcs)

    def body(*refs):
        sends, arrivals = _reduce_copies(refs[:n], refs[n:2 * n], refs[2 * n], refs[2 * n + 1])
        for cp in sends:
            cp.wait_send()
        for cp in arrivals:
            cp.wait_recv()

    bufs = list(srcs) + list(lands)
    outs = pl.pallas_call(
        body, name=name, out_shape=tuple(pltpu.HBM(b.shape, b.dtype) for b in bufs),
        in_specs=[_HBM] * (2 * n) + [_SEM, _SEM, _ANY], out_specs=tuple([_HBM] * (2 * n)),
        input_output_aliases={a: a for a in range(2 * n)},
        compiler_params=pltpu.CompilerParams(has_side_effects=_EFFECT),
    )(*bufs, send_sems, recv_sems, after)
    return list(outs[n:])


def _reduce_sum(name, own, land, chip, core):
    n, rh, lanes = land.shape
    tm = _row_tile(rh, 1024, ROW_ALIGN)

    def body(idx_ref, own_ref, *rest):
        total = own_ref[...]
        for g_ref in rest[:-1]:
            total = total + g_ref[...].astype(F32)
        rest[-1][...] = total

    def block(k):
        return pl.BlockSpec((None, tm, lanes), lambda i, idx_ref: ((2 * idx_ref[0] + idx_ref[1] + k) % n, i, 0))

    grid_spec = pltpu.PrefetchScalarGridSpec(
        num_scalar_prefetch=1, grid=(rh // tm,),
        in_specs=[pl.BlockSpec((None, None, tm, lanes), lambda i, idx_ref: (idx_ref[0], idx_ref[1], i, 0))]
        + [block(k) for k in range(1, n)],
        out_specs=pl.BlockSpec((None, tm, lanes), lambda i, idx_ref: (idx_ref[1], i, 0)))
    return pl.pallas_call(
        body, name=name, grid_spec=grid_spec, out_shape=jax.ShapeDtypeStruct((2, rh, lanes), F32),
        compiler_params=_params(("parallel",)),
    )(jnp.stack([chip, core]).astype(jnp.int32), own, *[land] * (n - 1))


def _gather_all_start(name, buf):
    def body(in_ref, send_sems, recv_sems, out_ref, token):
        me, peers = _all_peers()
        for j, peer in enumerate(peers):
            pltpu.make_async_remote_copy(
                src_ref=in_ref.at[me], dst_ref=in_ref.at[me], send_sem=send_sems.at[j], recv_sem=recv_sems.at[j],
                device_id=peer, device_id_type=MESH).start()
        token[...] = jnp.zeros_like(token)

    n = N_DEV - 1
    return pl.pallas_call(
        body, name=name, in_specs=[_HBM],
        out_shape=(pltpu.SemaphoreType.DMA((n,)), pltpu.SemaphoreType.DMA((n,)), pltpu.HBM(buf.shape, buf.dtype),
                   jax.ShapeDtypeStruct((8, LANES), F32)),
        out_specs=(_SEM, _SEM, _HBM, pl.BlockSpec(memory_space=pltpu.VMEM)), input_output_aliases={0: 2},
        compiler_params=pltpu.CompilerParams(has_side_effects=_EFFECT),
    )(pltpu.with_memory_space_constraint(buf, pltpu.HBM))


def _gather_all_wait(name, buf, send_sems, recv_sems, after):
    def body(in_ref, send_sems, recv_sems, after_ref, out_ref):
        me, peers = _all_peers()
        for j, (px, py, pc) in enumerate(peers):
            copy = pltpu.make_async_remote_copy(
                src_ref=in_ref.at[me], dst_ref=in_ref.at[4 * px + 2 * py + pc], send_sem=send_sems.at[j],
                recv_sem=recv_sems.at[j], device_id=(px, py, pc), device_id_type=MESH)
            copy.wait_send()
            copy.wait_recv()

    return pl.pallas_call(
        body, name=name, in_specs=[_HBM, _SEM, _SEM, _ANY], out_shape=pltpu.HBM(buf.shape, buf.dtype),
        out_specs=_HBM, input_output_aliases={0: 0},
        compiler_params=pltpu.CompilerParams(has_side_effects=_EFFECT),
    )(buf, send_sems, recv_sems, after)


def _sum_blocks(name, stacked, tm):
    n, r, lanes = stacked.shape

    def body(in_ref, o_ref):
        acc = in_ref[0]
        for j in range(1, n):
            acc = acc + in_ref[j]
        o_ref[...] = acc

    return pl.pallas_call(
        body, name=name, grid=(r // tm,), in_specs=[pl.BlockSpec((n, tm, lanes), lambda i: (0, i, 0))],
        out_specs=pl.BlockSpec((tm, lanes), lambda i: (i, 0)), out_shape=jax.ShapeDtypeStruct((r, lanes), F32),
        compiler_params=_params(("parallel",)),
    )(stacked)


def _row_tile(rows, pref, align):
    best = None
    for t in range(align, min(rows, pref) + 1, align):
        if rows % t == 0:
            best = t
    assert best is not None, (rows, pref, align)
    return best


def _adam(name, w, g, m, v):
    rows, width = w.shape
    tm = _row_tile(rows, max(8, 4096 * LANES // width), 8)
    args = [(t, width, 0) for t in (w, g, m, v)]
    return _rowcall(name, _adam_fn, args, [], [(width, F32)] * 3, tm=tm)


def kernel(x, norm_mix, norm_mlp, norm_final, mlp_w1, mlp_w2, ab_w_in, ab_w_out, rg_conv_w, rg_conv_b, rg_w_a, rg_b_a, rg_w_x, rg_b_x, rg_lambda, hg_lb_logits, hg_norm, gla_w_in, gla_w_out, gla_w_gate_up, gla_b_gate, gla_norm, loss_target, m_norm_mix, m_norm_mlp, m_norm_final, m_mlp_w1, m_mlp_w2, m_ab_w_in, m_ab_w_out, m_rg_conv_w, m_rg_conv_b, m_rg_w_a, m_rg_b_a, m_rg_w_x, m_rg_b_x, m_rg_lambda, m_hg_lb_logits, m_hg_norm, m_gla_w_in, m_gla_w_out, m_gla_w_gate_up, m_gla_b_gate, m_gla_norm, v_norm_mix, v_norm_mlp, v_norm_final, v_mlp_w1, v_mlp_w2, v_ab_w_in, v_ab_w_out, v_rg_conv_w, v_rg_conv_b, v_rg_w_a, v_rg_b_a, v_rg_w_x, v_rg_b_x, v_rg_lambda, v_hg_lb_logits, v_hg_norm, v_gla_w_in, v_gla_w_out, v_gla_w_gate_up, v_gla_b_gate, v_gla_norm):
    w = dict(norm_mix=norm_mix, norm_mlp=norm_mlp, norm_final=norm_final, mlp_w1=mlp_w1, mlp_w2=mlp_w2, ab_w_in=ab_w_in, ab_w_out=ab_w_out, rg_conv_w=rg_conv_w, rg_conv_b=rg_conv_b, rg_w_a=rg_w_a, rg_b_a=rg_b_a, rg_w_x=rg_w_x, rg_b_x=rg_b_x, rg_lambda=rg_lambda, hg_lb_logits=hg_lb_logits, hg_norm=hg_norm, gla_w_in=gla_w_in, gla_w_out=gla_w_out, gla_w_gate_up=gla_w_gate_up, gla_b_gate=gla_b_gate, gla_norm=gla_norm)
    m = dict(norm_mix=m_norm_mix, norm_mlp=m_norm_mlp, norm_final=m_norm_final, mlp_w1=m_mlp_w1, mlp_w2=m_mlp_w2, ab_w_in=m_ab_w_in, ab_w_out=m_ab_w_out, rg_conv_w=m_rg_conv_w, rg_conv_b=m_rg_conv_b, rg_w_a=m_rg_w_a, rg_b_a=m_rg_b_a, rg_w_x=m_rg_w_x, rg_b_x=m_rg_b_x, rg_lambda=m_rg_lambda, hg_lb_logits=m_hg_lb_logits, hg_norm=m_hg_norm, gla_w_in=m_gla_w_in, gla_w_out=m_gla_w_out, gla_w_gate_up=m_gla_w_gate_up, gla_b_gate=m_gla_b_gate, gla_norm=m_gla_norm)
    v = dict(norm_mix=v_norm_mix, norm_mlp=v_norm_mlp, norm_final=v_norm_final, mlp_w1=v_mlp_w1, mlp_w2=v_mlp_w2, ab_w_in=v_ab_w_in, ab_w_out=v_ab_w_out, rg_conv_w=v_rg_conv_w, rg_conv_b=v_rg_conv_b, rg_w_a=v_rg_w_a, rg_b_a=v_rg_b_a, rg_w_x=v_rg_w_x, rg_b_x=v_rg_b_x, rg_lambda=v_rg_lambda, hg_lb_logits=v_hg_lb_logits, hg_norm=v_hg_norm, gla_w_in=v_gla_w_in, gla_w_out=v_gla_w_out, gla_w_gate_up=v_gla_w_gate_up, gla_b_gate=v_gla_b_gate, gla_norm=v_gla_norm)
    chip = 2 * lax.axis_index("x") + lax.axis_index("y")
    core = lax.axis_index("c")
    sharded_shapes = [w[n].shape for n in SMALL_SHARDED]

    slots = [_into_slot(f"cast_{n}{layer}", w[n], chip, N_CHIPS, BF16, 512, layer) for n, layer in MATRICES]
    early = [i for i, (n, _) in enumerate(MATRICES) if n in EARLY_MATRICES]
    rest = [i for i in range(len(MATRICES)) if i not in early]

    def named(indices, arrays):
        big = {}
        for i, t in zip(indices, arrays):
            big.setdefault(MATRICES[i][0], []).append(t)
        return {n: (v if n in ("mlp_w1", "mlp_w2") else v[0]) for n, v in big.items()}

    vectors = _pack([w[n] for n in SMALL_SHARDED])
    vectors = _into_slot("place_vectors", vectors, chip, N_CHIPS, F32, vectors.shape[0])
    *gathered, vectors = _gather_chips("gather_early", [slots[i] for i in early] + [vectors])
    send_sems, recv_sems, *in_flight, token = _gather_start("gather_rest_start", [slots[i] for i in rest], gathered[0])

    def late_weights(after):
        landed = _gather_wait("gather_rest_wait", in_flight, send_sems, recv_sems, after)
        return _prepare_matrices(named(rest, _hand_over("gather_rest_share", list(landed))))

    big = named(early, gathered)
    small_all = _unpack(vectors, sharded_shapes, lead=1)
    full = {n: w[n] for n in SMALL_REPLICATED}
    for n, t in zip(SMALL_SHARDED, small_all):
        full[n] = _join_chips(t, t.ndim - 2)

    def halves(t):
        return t.reshape(N_CHIPS, 2, t.shape[1] // 2, t.shape[2])

    in_flight_grads = {}

    def emit(tag, arrays32, arrays16):
        n = len(arrays16)
        send, recv, *rest = _reduce_start(f"reduce_{tag}_start", [halves(t) for t in arrays16])
        in_flight_grads[tag] = ([halves(t) for t in arrays32], rest[:n], rest[n:2 * n], send, recv)
        return rest[-1]

    loss_part, grad_x, g_kernel = _local_step(
        x[0], loss_target[0], _prepare_weights(big, full), token, late_weights, emit)
    g_big, g_full = _finish_grads(g_kernel)

    small_names = SMALL_REPLICATED + SMALL_SHARDED
    reduced_shapes = [g_full[n].shape for n in small_names] + [loss_part.shape]
    g_small = _pack([g_full[n] for n in small_names] + [loss_part])
    device = 2 * chip + core
    g_small = _into_slot("place_small", g_small, device, N_DEV, F32, g_small.shape[0])
    small_send, small_recv, small_in_flight, small_token = _gather_all_start("reduce_small_start", g_small)

    mine = {}
    for tag, (own, srcs, lands, send, recv) in in_flight_grads.items():
        landed = _reduce_wait(f"reduce_{tag}_wait", srcs, lands, send, recv, small_token)
        mine[tag] = [_reduce_sum(f"reduce_add_{tag}{i}", o, f, chip, core) for i, (o, f) in enumerate(zip(own, landed))]
    ordered = [mine["mlp0"][0], mine["mlp1"][0], mine["mlp0"][1], mine["mlp1"][1], mine["ab"][0], mine["mlp0"][2],
               *mine["gla"]]
    reduced = [t.reshape(2 * t.shape[1], t.shape[2]) for t in _pair_gather("reduce_share", ordered)]
    by_name = {n: [] for n, _ in MATRICES}
    for (n, _), t in zip(MATRICES, reduced):
        by_name[n].append(t)
    grads = {n: jnp.stack(v) for n, v in by_name.items()}

    g_small_all = _gather_all_wait("reduce_small_wait", small_in_flight, small_send, small_recv, reduced[0])
    g_small_red = _sum_blocks("reduce_small_add", g_small_all, g_small_all.shape[1])
    *small_red, loss_sum = _unpack(g_small_red, reduced_shapes)
    loss = loss_sum[0, 0]
    g_small_full = dict(zip(small_names, small_red))
    for n in SMALL_REPLICATED:
        grads[n] = g_small_full[n]
    for n in SMALL_SHARDED:
        width = w[n].shape[-1]
        grads[n] = lax.dynamic_slice_in_dim(g_small_full[n], chip * width, width, axis=g_small_full[n].ndim - 1)

    delta, new_m, new_v = {}, {}, {}
    for n in by_name:
        flat = [t.reshape(-1, t.shape[-1]) for t in (w[n], grads[n], m[n], v[n])]
        for dst, t in zip((delta, new_m, new_v), _adam(f"adam_{n}", *flat)):
            dst[n] = t.reshape(w[n].shape)
    small_shapes = [w[n].shape for n in small_names]
    packs = [_pack([src[n] for n in small_names]) for src in (w, grads, m, v)]
    d_small, m_small, v_small = _adam("adam_small", *packs)
    for dst, buf in ((delta, d_small), (new_m, m_small), (new_v, v_small)):
        dst.update(zip(small_names, _unpack(buf, small_shapes)))

    return (loss, grad_x[None], *[grads[n] for n in WEIGHTS], *[delta[n] for n in WEIGHTS],
            *[new_m[n] for n in WEIGHTS], *[new_v[n] for n in WEIGHTS])
```

```python
import functools

import jax
import jax.numpy as jnp
from jax import lax
from jax.experimental import pallas as pl
from jax.experimental.pallas import tpu as pltpu

F32 = jnp.float32
BF16 = jnp.bfloat16
MESH = pl.DeviceIdType.MESH

LANES = 128
CHUNK = 64
ATTN_SUB = 4
EPS = 1e-6
RG_C = 8.0
N_CHIPS = 4
N_DEV = 8
GLA_IN_WIDTH = 3104
GLA_IN_PAD = 3200
VMEM_LIMIT = 56 * 1024 * 1024

ADAM_LR = 0.001
ADAM_B1 = 0.9
ADAM_B2 = 0.999
ADAM_EPS = 1e-08
ADAM_WD = 0.01
ADAM_STEP = 10


def _raw_dot(a, b, ca, cb):
    return lax.dot_general(a.astype(BF16), b.astype(BF16), (((ca,), (cb,)), ((), ())),
                           preferred_element_type=F32)


def _raw_nn(a, b):
    return _raw_dot(a, b, 1, 0)


def _raw_nt(a, b):
    return _raw_dot(a, b, 1, 1)


def _raw_tn(a, b):
    return _raw_dot(a, b, 0, 0)


@jax.custom_vjp
def _dot_nn(a, b):
    return _raw_nn(a, b)


def _dot_nn_fwd(a, b):
    return _raw_nn(a, b), (a, b)


def _dot_nn_bwd(res, g):
    a, b = res
    return _raw_nt(g, b), _raw_tn(a, g)


_dot_nn.defvjp(_dot_nn_fwd, _dot_nn_bwd)


@jax.custom_vjp
def _dot_nt(a, b):
    return _raw_nt(a, b)


def _dot_nt_fwd(a, b):
    return _raw_nt(a, b), (a, b)


def _dot_nt_bwd(res, g):
    a, b = res
    return _raw_nn(g, b), _raw_tn(g, a)


_dot_nt.defvjp(_dot_nt_fwd, _dot_nt_bwd)


@jax.custom_vjp
def _dot_tn(a, b):
    return _raw_tn(a, b)


def _dot_tn_fwd(a, b):
    return _raw_tn(a, b), (a, b)


def _dot_tn_bwd(res, g):
    a, b = res
    return _raw_nt(b, g), _raw_nn(a, g)


_dot_tn.defvjp(_dot_tn_fwd, _dot_tn_bwd)


def _tile(n, pref):
    if n <= pref:
        return n
    t = (pref // LANES) * LANES
    while t > LANES and n % t:
        t -= LANES
    assert n % t == 0, (n, pref)
    return t


def _params(sem):
    return pltpu.CompilerParams(dimension_semantics=sem, vmem_limit_bytes=VMEM_LIMIT)


def _rowcall(name, fn, rows, pars, row_outs, par_outs=(), tm=512, pin=None):
    if pin is not None:
        inner, pars = fn, list(pars) + [pin]
        fn = lambda *vals: inner(*vals[:-1])
    n_rows = rows[0][0].shape[0]
    tm = min(tm, n_rows)
    assert n_rows % tm == 0
    n_r, n_p, n_ro = len(rows), len(pars), len(row_outs)

    def body(*refs):
        vals = [r[...].astype(F32) for r in refs[:n_r + n_p]]
        outs = fn(*vals)
        o_refs = refs[n_r + n_p:n_r + n_p + n_ro]
        po_refs = refs[n_r + n_p + n_ro:]
        for o_ref, val in zip(o_refs, outs[:n_ro]):
            o_ref[...] = val.astype(o_ref.dtype)
        first = pl.program_id(0) == 0
        for po_ref, val in zip(po_refs, outs[n_ro:]):
            @pl.when(first)
            def _():
                po_ref[...] = val

            @pl.when(jnp.logical_not(first))
            def _():
                po_ref[...] += val

    def const_map(nd):
        return lambda i: (0,) * nd

    def row_spec(w, cb):
        return pl.BlockSpec((tm, w), lambda i: (i, cb))

    in_specs = [row_spec(w, cb) for _, w, cb in rows]
    in_specs += [pl.BlockSpec(p.shape, const_map(p.ndim)) for p in pars]
    out_specs = [pl.BlockSpec((tm, w), lambda i: (i, 0)) for w, _ in row_outs]
    out_specs += [pl.BlockSpec(tuple(s), const_map(len(s))) for s in par_outs]
    out_shape = [jax.ShapeDtypeStruct((n_rows, w), dt) for w, dt in row_outs]
    out_shape += [jax.ShapeDtypeStruct(tuple(s), F32) for s in par_outs]
    return pl.pallas_call(
        body, name=name, grid=(n_rows // tm,), in_specs=in_specs, out_specs=out_specs, out_shape=out_shape,
        compiler_params=_params(("arbitrary",) if par_outs else ("parallel",)),
    )(*[r[0] for r in rows], *pars)


def _vjp_of(fn, n_prim, n_out, n_par, n_pass=0):
    def bwd(*args):
        prim = args[:n_prim]
        cts = args[n_prim:n_prim + n_out]
        passes = args[n_prim + n_out:n_prim + n_out + 2 * n_pass]
        pars = args[n_prim + n_out + 2 * n_pass:]
        _, vjp = jax.vjp(fn, *prim, *pars)
        grads = vjp(tuple(cts))
        sums = tuple(passes[2 * i] + passes[2 * i + 1] for i in range(n_pass))
        return tuple(grads[:n_prim]) + sums + tuple(grads[n_prim:])
    return bwd


def _mm(name, a, b, mode="nn", extras=(), epi=None, out_dtypes=(F32,), a_pro=None, out_split=None,
        epi_pars=(), row_sum=False, pin=None, k_whole=False, tm=1024, tn=1024, tk=1024):
    split = b.shape[0] if b.ndim == 3 else None
    b_rows, b_cols = b.shape[-2:]
    if mode == "nn":
        (m, k), n = a.shape, b_cols * (split or 1)
    elif mode == "nt":
        (m, k), n = a.shape, b_rows
        assert k == b_cols * (split or 1)
    else:
        assert split is None
        (k, m), n = a.shape, b_cols
    tm, tk = _tile(m, tm), _tile(k, tk)
    tn = _tile(n // out_split, tn) if out_split else _tile(n, tn)
    if split and mode == "nn":
        tn = _tile(b_cols, tn)
    if split and mode == "nt":
        tk = _tile(b_cols, tk)
    if k_whole:
        assert split and mode == "nt"
        tk = k
    n_b = split if k_whole else 1
    nk = k // tk
    raw = {"nn": _raw_nn, "nt": _raw_nt, "tn": _raw_tn}[mode]
    n_e, n_p, n_o = len(extras), len(epi_pars), len(out_dtypes)
    n_in = n_e + n_p + (0 if pin is None else 1)
    if epi is None:
        epi = lambda acc: (acc,)

    def body(a_ref, *rest):
        b_refs, rest = rest[:n_b], rest[n_b:]
        e_refs, p_refs, o_refs = rest[:n_e], rest[n_e:n_e + n_p], rest[n_in:n_in + n_o]
        kk = pl.program_id(2)
        a_tile = a_ref[...] if a_pro is None else a_pro(a_ref[...].astype(F32))
        part = raw(a_tile[:, :b_cols] if k_whole else a_tile, b_refs[0][...])
        for s in range(1, n_b):
            part = part + raw(a_tile[:, s * b_cols:(s + 1) * b_cols], b_refs[s][...])

        def finish(total):
            res = epi(total, *[e[...].astype(F32) for e in e_refs], *[p[...] for p in p_refs])
            for o_ref, r in zip(o_refs, res):
                o_ref[...] = r.astype(o_ref.dtype)
            if row_sum:
                rest[n_in + n_o][...] = res[n_o]

        if nk == 1:
            finish(part)
            return
        acc = rest[-1]

        @pl.when(kk == 0)
        def _():
            acc[...] = part

        @pl.when((kk > 0) & (kk < nk - 1))
        def _():
            acc[...] += part

        @pl.when(kk == nk - 1)
        def _():
            finish(acc[...] + part)

    a_spec = pl.BlockSpec((tk, tm), lambda i, j, kk: (kk, i)) if mode == "tn" else pl.BlockSpec((tm, tk), lambda i, j, kk: (i, kk))
    if split and mode == "nn":
        per = b_cols // tn
        b_spec = pl.BlockSpec((None, tk, tn), lambda i, j, kk: (j // per, kk, j % per))
    elif split:
        per = b_cols // tk
        b_spec = pl.BlockSpec((None, tn, tk), lambda i, j, kk: (kk // per, j, kk % per))
    elif mode == "nt":
        b_spec = pl.BlockSpec((tn, tk), lambda i, j, kk: (j, kk))
    else:
        b_spec = pl.BlockSpec((tk, tn), lambda i, j, kk: (kk, j))
    mn_spec = pl.BlockSpec((tm, tn), lambda i, j, kk: (i, j))
    if out_split:
        assert not extras
        per_out = n // out_split // tn
        out_spec = pl.BlockSpec((None, tm, tn), lambda i, j, kk: (j // per_out, i, j % per_out))
        out_shapes = [jax.ShapeDtypeStruct((out_split, m, n // out_split), dt) for dt in out_dtypes]
    else:
        out_spec = mn_spec
        out_shapes = [jax.ShapeDtypeStruct((m, n), dt) for dt in out_dtypes]
    out_specs = [out_spec] * n_o
    if row_sum:
        out_specs.append(pl.BlockSpec((None, 1, tn), lambda i, j, kk: (i, 0, j)))
        out_shapes.append(jax.ShapeDtypeStruct((m // tm, 1, n), F32))
    b_specs = [b_spec]
    if k_whole:
        b_specs = [pl.BlockSpec((None, tn, b_cols), functools.partial(lambda i, j, kk, s: (s, j, 0), s=s))
                   for s in range(split)]
    in_specs = [a_spec] + b_specs + [mn_spec] * n_e
    in_specs += [pl.BlockSpec(p.shape, functools.partial(lambda i, j, kk, nd: (0,) * nd, nd=p.ndim)) for p in epi_pars]
    in_specs += [] if pin is None else [pl.BlockSpec(memory_space=pl.ANY)]
    outs = pl.pallas_call(
        body, name=name, grid=(m // tm, n // tn, nk), in_specs=in_specs, out_specs=out_specs, out_shape=out_shapes,
        scratch_shapes=[pltpu.VMEM((tm, tn), F32)] if nk > 1 else [],
        compiler_params=_params(("parallel", "parallel", "arbitrary")),
    )(a, *[b] * n_b, *extras, *epi_pars, *([] if pin is None else [pin]))
    return outs[0] if len(outs) == 1 else outs


def _sigmoid(x):
    return jax.nn.sigmoid(x)


def _silu(x):
    return x * _sigmoid(x)


def _softplus(x):
    return jnp.maximum(x, 0.0) + jnp.log1p(jnp.exp(-jnp.abs(x)))


def _rmsnorm_fn(x, gain):
    return (x * lax.rsqrt(jnp.mean(x * x, axis=-1, keepdims=True) + EPS) * gain,)


def _head_norm(o, gain, n_heads):
    w = o.shape[-1] // n_heads
    parts = []
    for h in range(n_heads):
        oh = o[:, h * w:(h + 1) * w]
        parts.append(oh * lax.rsqrt(jnp.mean(oh * oh, axis=-1, keepdims=True) + EPS))
    return jnp.concatenate(parts, axis=-1) * gain


@jax.custom_jvp
def _neg_expm1(x):
    u = jnp.exp(x)
    is_one = u == 1.0
    return jnp.where(is_one, -x, (1.0 - u) * x / jnp.log(jnp.where(is_one, 2.0, u)))


@_neg_expm1.defjvp
def _neg_expm1_jvp(primals, tangents):
    (x,), (t,) = primals, tangents
    return _neg_expm1(x), -jnp.exp(x) * t


def _rg_gates_fn(xc, wa, wx, ba, bx, lam):
    outs = []
    for d in range(2):
        r = _sigmoid(_dot_nn(xc, wa[d]) + ba[d:d + 1])
        i = _sigmoid(_dot_nn(xc, wx[d]) + bx[d:d + 1])
        log_a = -RG_C * r * _softplus(-lam[d:d + 1])
        outs.append(jnp.exp(log_a))
        outs.append(jnp.sqrt(_neg_expm1(2.0 * log_a)) * (i * xc))
    return tuple(outs)


def _hg_pre_fn(q, f_f, f_b, logits):
    mx = jnp.maximum(logits[0:1], logits[1:2])
    e0 = jnp.exp(logits[0:1] - mx)
    e1 = jnp.exp(logits[1:2] - mx)
    lb = e0 / (e0 + e1)
    outs = [_silu(q)]
    for f in (f_f, f_b):
        outs.append((1.0 - lb) * _sigmoid(-f))
        outs.append(jnp.log(lb + (1.0 - lb) * _sigmoid(f)))
    return tuple(outs)


def _post0_fn(hs, ga, o, g, gain):
    ya = hs * jax.nn.gelu(ga, approximate=True)
    yb = _head_norm(o, gain, 4) * _silu(g)
    return (jnp.concatenate([ya, yb], axis=-1),)


def _post0_fwd_fn(h_f, h_b, ga, o_f, o_b, g, gain):
    return _post0_fn(h_f + h_b, ga, o_f + o_b, g, gain)


def _post0_bwd_fn(h_f, h_b, ga, o_f, o_b, g, dmix, gain):
    _, vjp = jax.vjp(_post0_fn, h_f + h_b, ga, o_f + o_b, g, gain)
    return vjp((dmix,))


def _gla_pre_fn(q, lr, w_up, b_gate):
    outs = [q * (128.0 ** -0.5)]
    for d in range(2):
        z = _dot_nn(lr, w_up[d]) + b_gate[d:d + 1]
        outs.append(-_softplus(-z) * (1.0 / 16.0))
    return tuple(outs)


def _gla_post_fn(o, r, gain):
    return (_head_norm(o, gain, 4) * _silu(r),)


def _gla_post_fwd_fn(o_f, o_b, r, gain):
    return _gla_post_fn(o_f + o_b, r, gain)


def _gla_post_bwd_fn(o_f, o_b, r, dmix, gain):
    _, vjp = jax.vjp(_gla_post_fn, o_f + o_b, r, gain)
    return vjp((dmix,))


def _relu2_bwd_epi(acc, hid):
    return (acc * 2.0 * jnp.maximum(hid, 0.0),)


def _relu2(x):
    r = jnp.maximum(x, 0.0)
    return r * r


def _add_epi(acc, res):
    return (acc + res,)


def _loss_head_fn(h, target, gain):
    def f(h, gain):
        y = _rmsnorm_fn(h, gain)[0]
        err = y - target
        return 0.5 * jnp.sum(jnp.mean(err * err, axis=-1, keepdims=True))
    loss, (dh, dgain) = jax.value_and_grad(f, argnums=(0, 1))(h, gain)
    return dh, dh, jnp.full((1, LANES), loss, F32), dgain


def _adam_fn(w, g, m, v):
    m2 = ADAM_B1 * m + (1.0 - ADAM_B1) * g
    v2 = ADAM_B2 * v + (1.0 - ADAM_B2) * (g * g)
    m_hat = m2 / (1.0 - ADAM_B1 ** ADAM_STEP)
    v_hat = v2 / (1.0 - ADAM_B2 ** ADAM_STEP)
    delta = -ADAM_LR * (m_hat / (jnp.sqrt(v_hat) + ADAM_EPS) + ADAM_WD * w)
    return delta, m2, v2


def _shifted(x, t_idx, off):
    n = x.shape[0]
    rolled = pltpu.roll(x, (-off) % n, 0)
    valid = (t_idx + off >= 0) & (t_idx + off < n)
    return jnp.where(valid, rolled, 0.0)


def _conv_fwd(name, src, colblock, w, b):
    n_rows, width = src.shape[0], w.shape[1]

    def body(x_ref, w_ref, b_ref, o_ref):
        x = x_ref[...]
        t_idx = lax.broadcasted_iota(jnp.int32, x.shape, 0)
        acc = b_ref[...] + w_ref[2:3, :] * x
        acc += w_ref[0:1, :] * _shifted(x, t_idx, -2)
        acc += w_ref[1:2, :] * _shifted(x, t_idx, -1)
        acc += w_ref[3:4, :] * _shifted(x, t_idx, 1)
        o_ref[...] = acc

    nb = width // LANES
    return pl.pallas_call(
        body, name=name, grid=(nb,),
        in_specs=[pl.BlockSpec((n_rows, LANES), lambda j: (0, colblock * nb + j)),
                  pl.BlockSpec((4, LANES), lambda j: (0, j)), pl.BlockSpec((1, LANES), lambda j: (0, j))],
        out_specs=pl.BlockSpec((n_rows, LANES), lambda j: (0, j)),
        out_shape=jax.ShapeDtypeStruct((n_rows, width), F32),
        compiler_params=_params(("parallel",)),
    )(src, w, b)


def _conv_bwd(name, src, colblock, w, d):
    n_rows, width = src.shape[0], w.shape[1]

    def body(x_ref, w_ref, d_ref, dx_ref, dw_ref, db_ref):
        x = x_ref[...]
        g = d_ref[...]
        t_idx = lax.broadcasted_iota(jnp.int32, x.shape, 0)
        dx = w_ref[2:3, :] * g
        dx += w_ref[0:1, :] * _shifted(g, t_idx, 2)
        dx += w_ref[1:2, :] * _shifted(g, t_idx, 1)
        dx += w_ref[3:4, :] * _shifted(g, t_idx, -1)
        dx_ref[...] = dx.astype(dx_ref.dtype)
        dw_ref[0:1, :] = jnp.sum(g * _shifted(x, t_idx, -2), axis=0, keepdims=True)
        dw_ref[1:2, :] = jnp.sum(g * _shifted(x, t_idx, -1), axis=0, keepdims=True)
        dw_ref[2:3, :] = jnp.sum(g * x, axis=0, keepdims=True)
        dw_ref[3:4, :] = jnp.sum(g * _shifted(x, t_idx, 1), axis=0, keepdims=True)
        db_ref[...] = jnp.sum(g, axis=0, keepdims=True)

    nb = width // LANES
    return pl.pallas_call(
        body, name=name, grid=(nb,),
        in_specs=[pl.BlockSpec((n_rows, LANES), lambda j: (0, colblock * nb + j)),
                  pl.BlockSpec((4, LANES), lambda j: (0, j)),
                  pl.BlockSpec((n_rows, LANES), lambda j: (0, j))],
        out_specs=[pl.BlockSpec((n_rows, LANES), lambda j: (0, j)), pl.BlockSpec((4, LANES), lambda j: (0, j)),
                   pl.BlockSpec((1, LANES), lambda j: (0, j))],
        out_shape=[jax.ShapeDtypeStruct((n_rows, width), BF16), jax.ShapeDtypeStruct((4, width), F32),
                   jax.ShapeDtypeStruct((1, width), F32)],
        compiler_params=_params(("parallel",)),
    )(src, w, d)


SUBLANES = 8
SCAN_UNROLL = 8


def _shift_rows(x, d, fill):
    n = x.shape[0]
    t = lax.broadcasted_iota(jnp.int32, x.shape, 0)
    valid = (t >= d) if d > 0 else (t < n + d)
    return jnp.where(valid, pltpu.roll(x, d % n, 0), fill)


def _tile_scan(a, u, reverse):
    d = 1
    while d < a.shape[0]:
        s = -d if reverse else d
        a_sh, u_sh = _shift_rows(a, s, 1.0), _shift_rows(u, s, 0.0)
        u = u + a * u_sh
        a = a * a_sh
        d *= 2
    return a, u


def _edge_row(x, reverse):
    return x[0:1, :] if reverse else x[SUBLANES - 1:SUBLANES, :]


def _scan_specs(n_rows, n):
    return [pl.BlockSpec((n_rows, LANES), lambda j: (0, j))] * n


def _scan_tile(a_ref, u_ref, h_ref, i, carry, reverse):
    n_tiles = a_ref.shape[0] // SUBLANES
    tile = (n_tiles - 1 - i) if reverse else i
    rows = pl.ds(pl.multiple_of(tile * SUBLANES, SUBLANES), SUBLANES)
    acc_a, acc_u = _tile_scan(a_ref[rows, :], u_ref[rows, :], reverse)
    h = acc_u + acc_a * carry
    h_ref[rows, :] = h
    return _edge_row(h, reverse)


def _scan_fwd(name, a_f, u_f, a_b, u_b):
    n_rows, width = a_f.shape

    def body(af_ref, uf_ref, ab_ref, ub_ref, hf_ref, hb_ref):
        def step(i, carry):
            return (_scan_tile(af_ref, uf_ref, hf_ref, i, carry[0], False),
                    _scan_tile(ab_ref, ub_ref, hb_ref, i, carry[1], True))
        zero = jnp.zeros((1, LANES), F32)
        lax.fori_loop(0, n_rows // SUBLANES, step, (zero, zero), unroll=SCAN_UNROLL)

    return pl.pallas_call(
        body, name=name, grid=(width // LANES,), in_specs=_scan_specs(n_rows, 4), out_specs=_scan_specs(n_rows, 2),
        out_shape=[jax.ShapeDtypeStruct((n_rows, width), F32)] * 2, compiler_params=_params(("parallel",)),
    )(a_f, u_f, a_b, u_b)


def _scan_bwd_tile(a_ref, h_ref, dh_ref, du_ref, da_ref, i, carry, reverse):
    n_rows = a_ref.shape[0]
    n_tiles = n_rows // SUBLANES
    against = not reverse
    one = -1 if against else 1
    g_in, a_edge = carry
    tile = (n_tiles - 1 - i) if against else i
    start = pl.multiple_of(tile * SUBLANES, SUBLANES)
    rows = pl.ds(start, SUBLANES)
    a_tile = a_ref[rows, :]
    coeff = _shift_rows(a_tile, one, a_edge)
    acc_a, acc_u = _tile_scan(coeff, dh_ref[rows, :], against)
    g = acc_u + acc_a * g_in
    du_ref[rows, :] = g
    outside = (start + SUBLANES) if reverse else (start - 1)
    inside = (outside >= 0) & (outside < n_rows)
    h_edge = jnp.where(inside, h_ref[pl.ds(jnp.clip(outside, 0, n_rows - 1), 1), :], 0.0)
    da_ref[rows, :] = g * _shift_rows(h_ref[rows, :], -one, h_edge)
    return _edge_row(g, against), _edge_row(a_tile, against)


def _scan_bwd(name, a_f, h_f, a_b, h_b, dh):
    n_rows, width = a_f.shape

    def body(af_ref, hf_ref, ab_ref, hb_ref, dh_ref, duf_ref, daf_ref, dub_ref, dab_ref):
        def step(i, carry):
            return (_scan_bwd_tile(af_ref, hf_ref, dh_ref, duf_ref, daf_ref, i, carry[0], False),
                    _scan_bwd_tile(ab_ref, hb_ref, dh_ref, dub_ref, dab_ref, i, carry[1], True))
        zero = jnp.zeros((1, LANES), F32)
        lax.fori_loop(0, n_rows // SUBLANES, step, ((zero, zero), (zero, zero)), unroll=SCAN_UNROLL)

    return pl.pallas_call(
        body, name=name, grid=(width // LANES,), in_specs=_scan_specs(n_rows, 5), out_specs=_scan_specs(n_rows, 4),
        out_shape=[jax.ShapeDtypeStruct((n_rows, width), F32)] * 4, compiler_params=_params(("parallel",)),
    )(a_f, h_f, a_b, h_b, dh)


def _tri_mask(c, reverse):
    row = lax.broadcasted_iota(jnp.int32, (c, c), 0)
    col = lax.broadcasted_iota(jnp.int32, (c, c), 1)
    return (col >= row) if reverse else (col <= row)


def _cumsum_rows(x, reverse):
    tri = _tri_mask(x.shape[0], reverse).astype(BF16)
    hi = x.astype(BF16)
    rest = x - hi.astype(F32)
    mid = rest.astype(BF16)
    lo = (rest - mid.astype(F32)).astype(BF16)
    return _raw_nn(tri, hi) + _raw_nn(tri, mid) + _raw_nn(tri, lo)


@functools.partial(jax.custom_vjp, nondiff_argnums=(1,))
def _cumsum(x, reverse):
    return _cumsum_rows(x, reverse)


def _cumsum_fwd(x, reverse):
    return _cumsum_rows(x, reverse), None


def _cumsum_bwd(reverse, _, g):
    return (_cumsum_rows(g, not reverse),)


_cumsum.defvjp(_cumsum_fwd, _cumsum_bwd)


def _chunks_fn(qs, ks, vs, lfs, sts, reverses):
    n, c = len(qs), qs[0].shape[0]
    every = range(n)
    tris = [_tri_mask(c, r) for r in reverses]
    cums = [_cumsum(lfs[i], reverses[i]) for i in every]
    rid = lax.broadcasted_iota(jnp.int32, cums[0].shape, 0)

    def pick(cum, r):
        return jnp.sum(jnp.where(rid == r, cum, 0.0), axis=0, keepdims=True)

    refs = [pick(cums[i], (c - 1 - c // 2) if reverses[i] else c // 2) for i in every]
    lasts = [pick(cums[i], 0 if reverses[i] else c - 1) for i in every]
    q_in = [qs[i] * jnp.exp(cums[i] - refs[i]) for i in every]
    k_in = [ks[i] * jnp.exp(refs[i] - cums[i]) for i in every]
    scores = [jnp.where(tris[i], _dot_nt(q_in[i], k_in[i]), 0.0) for i in every]
    o_intra = [_dot_nn(scores[i], vs[i]) for i in every]
    q_out = [qs[i] * jnp.exp(cums[i]) for i in every]
    o_inter = [_dot_nt(q_out[i], sts[i]) for i in every]
    k_state = [ks[i] * jnp.exp(lasts[i] - cums[i]) for i in every]
    upd = [_dot_tn(vs[i], k_state[i]) for i in every]
    st_new = [sts[i] * jnp.exp(lasts[i]) + upd[i] for i in every]
    return [o_intra[i] + o_inter[i] for i in every], st_new


def _attn_fwd(name, q, k_f, k_b, v, lf_f, lf_b, n_heads, dk, dv):
    n_rows = q[0].shape[0]
    n_chunks = n_rows // CHUNK
    n_steps = n_chunks // ATTN_SUB
    wk, wv = n_heads * dk, n_heads * dv

    def spec(width, off, rev):
        return pl.BlockSpec((CHUNK * ATTN_SUB, width), lambda n: ((n_steps - 1 - n) if rev else n, off))

    def sspec(rev):
        return pl.BlockSpec((ATTN_SUB, n_heads, dv, dk), lambda n: ((n_steps - 1 - n) if rev else n, 0, 0, 0))

    def body(qf, kf, vf, lff, qb, kb, vb, lfb, of_ref, ob_ref, sf_ref, sb_ref, st):
        @pl.when(pl.program_id(0) == 0)
        def _():
            st[...] = jnp.zeros_like(st)

        ins = ((qf, kf, vf, lff), (qb, kb, vb, lfb))
        chains = [(d, h) for d in range(2) for h in range(n_heads)]
        ck = [slice(h * dk, (h + 1) * dk) for h in range(n_heads)]
        cv = [slice(h * dv, (h + 1) * dv) for h in range(n_heads)]
        sts = [st[d, h] for d, h in chains]
        done = []
        for sub in range(ATTN_SUB):
            local = (sub, ATTN_SUB - 1 - sub)
            rows = [slice(local[d] * CHUNK, (local[d] + 1) * CHUNK) for d in range(2)]
            qs = [ins[d][0][rows[d], ck[h]] for d, h in chains]
            ks = [ins[d][1][rows[d], ck[h]] for d, h in chains]
            vs = [ins[d][2][rows[d], cv[h]] for d, h in chains]
            lfs = [ins[d][3][rows[d], ck[h]] for d, h in chains]
            os_, st_new = _chunks_fn(qs, ks, vs, lfs, sts, [d == 1 for d, _ in chains])
            done.append((local, rows, sts, os_))
            sts = st_new
        for local, rows, entered, os_ in done:
            for i, (d, h) in enumerate(chains):
                (sf_ref, sb_ref)[d][local[d], h] = entered[i].astype(BF16)
                (of_ref, ob_ref)[d][rows[d], cv[h]] = os_[i]
        for i, (d, h) in enumerate(chains):
            st[d, h] = sts[i]

    in_specs = [spec(wk, q[1], False), spec(wk, k_f[1], False), spec(wv, v[1], False), spec(wk, lf_f[1], False),
                spec(wk, q[1], True), spec(wk, k_b[1], True), spec(wv, v[1], True), spec(wk, lf_b[1], True)]
    return pl.pallas_call(
        body, name=name, grid=(n_steps,), in_specs=in_specs,
        out_specs=[spec(wv, 0, False), spec(wv, 0, True), sspec(False), sspec(True)],
        out_shape=[jax.ShapeDtypeStruct((n_rows, wv), F32)] * 2
        + [jax.ShapeDtypeStruct((n_chunks, n_heads, dv, dk), BF16)] * 2,
        scratch_shapes=[pltpu.VMEM((2, n_heads, dv, dk), F32)],
        compiler_params=_params(("arbitrary",)),
    )(q[0], k_f[0], v[0], lf_f[0], q[0], k_b[0], v[0], lf_b[0])


def _attn_bwd(name, q, k_f, k_b, v, lf_f, lf_b, st_f, st_b, do, n_heads, dk, dv, out_dtype=F32):
    n_rows = q[0].shape[0]
    n_chunks = n_rows // CHUNK
    n_steps = n_chunks // ATTN_SUB
    wk, wv = n_heads * dk, n_heads * dv

    def spec(width, off, rev):
        return pl.BlockSpec((CHUNK * ATTN_SUB, width), lambda n: (n if rev else (n_steps - 1 - n), off))

    def sspec(rev):
        return pl.BlockSpec((ATTN_SUB, n_heads, dv, dk), lambda n: (n if rev else (n_steps - 1 - n), 0, 0, 0))

    def body(qf, kf, vf, lff, sf, dof, qb, kb, vb, lfb, sb, dob,
             dqf, dkf, dvf, dlff, dqb, dkb, dvb, dlfb, dst):
        @pl.when(pl.program_id(0) == 0)
        def _():
            dst[...] = jnp.zeros_like(dst)

        ins = ((qf, kf, vf, lff, sf, dof), (qb, kb, vb, lfb, sb, dob))
        outs = ((dqf, dkf, dvf, dlff), (dqb, dkb, dvb, dlfb))
        chains = [(d, h) for d in range(2) for h in range(n_heads)]
        ck = [slice(h * dk, (h + 1) * dk) for h in range(n_heads)]
        cv = [slice(h * dv, (h + 1) * dv) for h in range(n_heads)]
        fn = functools.partial(_chunks_fn, reverses=[d == 1 for d, _ in chains])
        dsts = [dst[d, h] for d, h in chains]
        done = []
        for sub in range(ATTN_SUB):
            local = (ATTN_SUB - 1 - sub, sub)
            rows = [slice(local[d] * CHUNK, (local[d] + 1) * CHUNK) for d in range(2)]
            qs = [ins[d][0][rows[d], ck[h]] for d, h in chains]
            ks = [ins[d][1][rows[d], ck[h]] for d, h in chains]
            vs = [ins[d][2][rows[d], cv[h]] for d, h in chains]
            lfs = [ins[d][3][rows[d], ck[h]] for d, h in chains]
            sts = [ins[d][4][local[d], h].astype(F32) for d, h in chains]
            dos = [ins[d][5][rows[d], cv[h]] for d, h in chains]
            _, vjp = jax.vjp(fn, qs, ks, vs, lfs, sts)
            dqs, dks, dvs, dlfs, dsts = vjp((dos, dsts))
            done.append((rows, dqs, dks, dvs, dlfs))
        for rows, dqs, dks, dvs, dlfs in done:
            for i, (d, h) in enumerate(chains):
                dq_r, dk_r, dv_r, dlf_r = outs[d]
                dq_r[rows[d], ck[h]] = dqs[i].astype(dq_r.dtype)
                dk_r[rows[d], ck[h]] = dks[i].astype(dk_r.dtype)
                dv_r[rows[d], cv[h]] = dvs[i].astype(dv_r.dtype)
                dlf_r[rows[d], ck[h]] = dlfs[i].astype(dlf_r.dtype)
        for i, (d, h) in enumerate(chains):
            dst[d, h] = dsts[i]

    def dir_specs(kk, lf, rev):
        return [spec(wk, q[1], rev), spec(wk, kk[1], rev), spec(wv, v[1], rev), spec(wk, lf[1], rev), sspec(rev),
                spec(wv, 0, rev)]

    def dir_out_specs(rev):
        return [spec(wk, 0, rev), spec(wk, 0, rev), spec(wv, 0, rev), spec(wk, 0, rev)]

    shapes = [jax.ShapeDtypeStruct((n_rows, wk), out_dtype), jax.ShapeDtypeStruct((n_rows, wk), out_dtype),
              jax.ShapeDtypeStruct((n_rows, wv), out_dtype), jax.ShapeDtypeStruct((n_rows, wk), F32)]
    outs = pl.pallas_call(
        body, name=name, grid=(n_steps,), in_specs=dir_specs(k_f, lf_f, False) + dir_specs(k_b, lf_b, True),
        out_specs=dir_out_specs(False) + dir_out_specs(True), out_shape=shapes + shapes,
        scratch_shapes=[pltpu.VMEM((2, n_heads, dv, dk), F32)],
        compiler_params=_params(("arbitrary",)),
    )(q[0], k_f[0], v[0], lf_f[0], st_f, do, q[0], k_b[0], v[0], lf_b[0], st_b, do)
    return outs[:4], outs[4:]


def _row2(v):
    return v.reshape(1, -1)


def _add_norm_epi(acc, res, gain):
    h = acc + res
    return h, _rmsnorm_fn(h, gain)[0]


def _residual_mm(name, a, b, res, next_gain, **kw):
    if next_gain is None:
        return _mm(name, a, b, extras=(res,), epi=_add_epi, **kw), None
    assert b.shape[-1] <= 1024
    return _mm(name, a, b, extras=(res,), epi=_add_norm_epi, epi_pars=(next_gain,), out_dtypes=(F32, BF16), **kw)


def _mlp_fwd(tag, h, y, w1, w2, next_gain=None):
    hid = _mm(f"{tag}_up", y, w1, out_dtypes=(BF16,), tm=2048)
    h_out, y_next = _residual_mm(f"{tag}_down", hid, w2, h, next_gain, a_pro=_relu2, tk=2048)
    return h_out, y_next, (y, hid)


def _dw(name, a, b, **kw):
    return _mm(name, a, b, mode="tn", epi=lambda acc: (acc, acc), out_dtypes=(F32, BF16), **kw)


def _mlp_bwd(tag, h, gain, w1, w2, saved, dh_out):
    y, hid = saved
    dhid = _mm(f"{tag}_dact", dh_out[1], w2, mode="nt", extras=(hid,), epi=_relu2_bwd_epi, out_dtypes=(BF16,),
               tm=2048)
    dw2 = _dw(f"{tag}_dw2", hid, dh_out[1], a_pro=_relu2, tk=2048)
    dw1 = _dw(f"{tag}_dw1", y, dhid, out_split=N_CHIPS, tk=4096)
    dh, dgain = _dy_norm_bwd(f"{tag}_dy", dhid, w1, h, gain, dh_out[0], k_whole=True, tm=512)
    return dh, dgain, dw1, dw2


def _dy_norm_bwd(name, dz, w, h, gain, dres, pin=None, twice=True, **tiles):
    n_out = 2 if twice else 1

    def epi(dy, h_tile, dres_tile, gain_row):
        _, vjp = jax.vjp(lambda u, v: _rmsnorm_fn(u, v)[0], h_tile, gain_row)
        dh, dgain = vjp(dy)
        return (dh + dres_tile,) * n_out + (dgain,)

    assert h.shape[1] <= 1024
    tiles.setdefault("tm", 1024)
    *dh, dgain_parts = _mm(name, dz, w, mode="nt", extras=(h, dres), epi=epi, epi_pars=(gain,), row_sum=True,
                           out_dtypes=(F32, BF16)[:n_out], pin=pin, **tiles)
    return dh, jnp.sum(dgain_parts, axis=0)


def _local_step(x, target, w, pin=None, late=None, emit=None):
    g = {}
    d_model = x.shape[1]
    rg_w = hg_w = d_model // 2
    pins = []

    def send_off(tag, pairs):
        if emit is not None:
            pins.append(emit(tag, [p[0] for p in pairs], [p[1] for p in pairs]))

    def both(fn, pair):
        return [fn(t) for t in pair]

    def chip_major(t):
        return t.reshape(N_CHIPS, t.shape[0] // N_CHIPS, t.shape[1])

    h_a0 = x
    gain = _row2(w["norm_mix"][0])
    y0 = _rowcall("l0_norm", _rmsnorm_fn, [(h_a0, d_model, 0)], [gain], [(d_model, BF16)], tm=512, pin=pin)[0]
    proj0 = _mm("l0_in", y0, w["ab_w_in"], tm=2048)
    conv_w, conv_b = w["rg_conv_w"], _row2(w["rg_conv_b"])
    xc = _conv_fwd("rg_conv", proj0, 0, conv_w, conv_b)
    gate_pars = [w["rg_wa_bd"], w["rg_wx_bd"], w["rg_b_a"], w["rg_b_x"], w["rg_lambda"]]
    a_f, u_f, a_b, u_b = _rowcall("rg_gates", _rg_gates_fn, [(xc, rg_w, 0)], gate_pars, [(rg_w, F32)] * 4)
    hs_f, hs_b = _scan_fwd("rg_scan", a_f, u_f, a_b, u_b)
    hg_rows = [(proj0, hg_w, 2), (proj0, hg_w, 3), (proj0, hg_w, 4)]
    qh, k_f, lf_f, k_b, lf_b = _rowcall("hg_pre", _hg_pre_fn, hg_rows, [w["hg_lb_logits"]], [(hg_w, F32)] * 5)
    iv = (proj0, 5)
    o_f, o_b, st_f, st_b = _attn_fwd("hg_attn", (qh, 0), (k_f, 0), (k_b, 0), iv, (lf_f, 0), (lf_b, 0), 4, 128, 128)
    post0_rows = [(hs_f, rg_w, 0), (hs_b, rg_w, 0), (proj0, rg_w, 1), (o_f, hg_w, 0), (o_b, hg_w, 0), (proj0, hg_w, 6)]
    hg_gain = _row2(w["hg_norm"])
    mix_in0 = _rowcall("l0_post", _post0_fwd_fn, post0_rows, [hg_gain], [(d_model, BF16)])[0]
    if late is not None:
        w = {**w, **late(mix_in0)}
    gain1 = _row2(w["norm_mix"][1])
    h_b0, y_mlp0 = _residual_mm("l0_out", mix_in0, w["ab_w_out"], h_a0, _row2(w["norm_mlp"][0]))
    h_c0, y1, mlp0 = _mlp_fwd("mlp0", h_b0, y_mlp0, w["mlp_w1"][0], w["mlp_w2"][0], gain1)

    h_a1 = h_c0
    proj1 = _mm("l1_in", y1, w["gla_w_in_pad"], tm=512, tn=GLA_IN_PAD)
    gla_pars = [w["gla_w_up_pad"], w["gla_b_gate"]]
    gq, glf_f, glf_b = _rowcall("gla_pre", _gla_pre_fn, [(proj1, 512, 0), (proj1, LANES, 24)], gla_pars, [(512, F32)] * 3)
    gk, gv = (proj1, 1), (proj1, 1)
    go_f, go_b, gst_f, gst_b = _attn_fwd("gla_attn", (gq, 0), gk, gk, gv, (glf_f, 0), (glf_b, 0), 4, 128, 256)
    gla_gain = _row2(w["gla_norm"])
    post1_rows = [(go_f, d_model, 0), (go_b, d_model, 0), (proj1, d_model, 2)]
    mix_in1 = _rowcall("l1_post", _gla_post_fwd_fn, post1_rows, [gla_gain], [(d_model, BF16)])[0]
    h_b1, y_mlp1 = _residual_mm("l1_out", mix_in1, w["gla_w_out"], h_a1, _row2(w["norm_mlp"][1]))
    h_c1, _, mlp1 = _mlp_fwd("mlp1", h_b1, y_mlp1, w["mlp_w1"][1], w["mlp_w2"][1])

    *dh, loss, g["norm_final"] = _rowcall(
        "loss_head", _loss_head_fn, [(h_c1, d_model, 0), (target, d_model, 0)], [_row2(w["norm_final"])],
        [(d_model, F32), (d_model, BF16)], [(1, LANES), (1, d_model)], tm=512)

    dh, g_nmlp1, g_w1_1, g_w2_1 = _mlp_bwd("mlp1", h_b1, _row2(w["norm_mlp"][1]), w["mlp_w1"][1], w["mlp_w2"][1], mlp1, dh)
    send_off("mlp1", [g_w1_1, both(chip_major, g_w2_1)])
    dmix1 = _mm("l1_dout", dh[1], w["gla_w_out"], mode="nt")
    g_gla_out = _dw("l1_dwout", mix_in1, dh[1])
    g["gla_w_out"] = g_gla_out[0]
    dgo, dr, g["gla_norm"] = _rowcall(
        "l1_dpost", _gla_post_bwd_fn, post1_rows + [(dmix1, d_model, 0)], [gla_gain],
        [(d_model, F32), (d_model, BF16)], [(1, d_model)], pin=pins.pop() if pins else None)
    (dq_f, dk_f, dv_f, dlf_f), (dq_b, dk_b, dv_b, dlf_b) = _attn_bwd(
        "gla_dattn", (gq, 0), gk, gk, gv, (glf_f, 0), (glf_b, 0), gst_f, gst_b, dgo, 4, 128, 256)

    def gla_pre_bwd(q, lr, dq1, dq2, dlf1, dlf2, dk1, dk2, dv1, dv2, w_up, b_gate):
        dlr = jnp.zeros_like(lr)
        dws, dbs = [], []
        for d, dlf in enumerate((dlf1, dlf2)):
            z = _raw_nn(lr, w_up[d]) + b_gate[d:d + 1]
            dz = dlf * _sigmoid(-z) * (1.0 / 16.0)
            dlr = dlr + _raw_nt(dz, w_up[d])
            dws.append(_raw_tn(dz, lr))
            dbs.append(jnp.sum(dz, axis=0, keepdims=True))
        return ((dq1 + dq2) * (128.0 ** -0.5), dk1 + dk2, dv1 + dv2, dlr, dws[0], dws[1], dbs[0], dbs[1])

    rows = [(proj1, 512, 0), (proj1, LANES, 24), (dq_f, 512, 0), (dq_b, 512, 0), (dlf_f, 512, 0), (dlf_b, 512, 0),
            (dk_f, 512, 0), (dk_b, 512, 0), (dv_f, d_model, 0), (dv_b, d_model, 0)]
    dq, dk, dv, dlr, dwt_f, dwt_b, db_f, db_b = _rowcall(
        "gla_dpre", gla_pre_bwd, rows, gla_pars, [(512, BF16), (512, BF16), (d_model, BF16), (LANES, BF16)],
        [(512, LANES), (512, LANES), (1, 512), (1, 512)])
    g["gla_w_up_pad"] = jnp.stack([dwt_f.T, dwt_b.T])
    g["gla_b_gate"] = jnp.concatenate([db_f, db_b], axis=0)
    dproj1 = jnp.concatenate([dq, dk, dv, dr, dlr], axis=1)
    g_gla_in = both(lambda t: _split_chips(t[:, :GLA_IN_WIDTH], 1), _dw("l1_dwin", y1, dproj1, tn=640, tk=4096))
    g["gla_w_in"] = g_gla_in[0]
    send_off("gla", [g_gla_in, both(chip_major, g_gla_out)])
    dh, g_nmix1 = _dy_norm_bwd("l1_dy", dproj1, w["gla_w_in_pad"], h_a1, gain1, dh[0],
                               pin=pins.pop() if pins else None, tk=GLA_IN_PAD)

    dh, g_nmlp0, g_w1_0, g_w2_0 = _mlp_bwd("mlp0", h_b0, _row2(w["norm_mlp"][0]), w["mlp_w1"][0], w["mlp_w2"][0], mlp0, dh)
    g_ab_out = _dw("l0_dwout", mix_in0, dh[1])
    g["ab_w_out"] = g_ab_out[0]
    send_off("mlp0", [g_w1_0, both(chip_major, g_w2_0), both(chip_major, g_ab_out)])
    dmix0 = _mm("l0_dout", dh[1], w["ab_w_out"], mode="nt")
    dhs, dga, do, dg, g["hg_norm"] = _rowcall(
        "l0_dpost", _post0_bwd_fn, post0_rows + [(dmix0, d_model, 0)], [hg_gain],
        [(rg_w, F32), (rg_w, BF16), (hg_w, F32), (hg_w, BF16)], [(1, hg_w)], pin=pins.pop() if pins else None)
    (dqh_f, dk_f, div_f, dlf_f), (dqh_b, dk_b, div_b, dlf_b) = _attn_bwd(
        "hg_dattn", (qh, 0), (k_f, 0), (k_b, 0), iv, (lf_f, 0), (lf_b, 0), st_f, st_b, do, 4, 128, 128)

    def hg_pre_bwd(q, f_f, f_b, dq1, dq2, dk1, dlf1, dk2, dlf2, dv1, dv2, logits):
        _, vjp = jax.vjp(_hg_pre_fn, q, f_f, f_b, logits)
        dq, df_f, df_b, dlogits = vjp((dq1 + dq2, dk1, dlf1, dk2, dlf2))
        return dq, df_f, df_b, dv1 + dv2, dlogits

    rows = hg_rows + [(t, hg_w, 0) for t in (dqh_f, dqh_b, dk_f, dlf_f, dk_b, dlf_b, div_f, div_b)]
    dq, df_f, df_b, div, g["hg_lb_logits"] = _rowcall(
        "hg_dpre", hg_pre_bwd, rows, [w["hg_lb_logits"]], [(hg_w, BF16)] * 4, [(2, hg_w)])
    du_f, da_f, du_b, da_b = _scan_bwd("rg_dscan", a_f, hs_f, a_b, hs_b, dhs)
    gates_bwd = _vjp_of(_rg_gates_fn, 1, 4, 5)
    rows = [(xc, rg_w, 0), (da_f, rg_w, 0), (du_f, rg_w, 0), (da_b, rg_w, 0), (du_b, rg_w, 0)]
    dxc, g["rg_wa_bd"], g["rg_wx_bd"], g["rg_b_a"], g["rg_b_x"], g["rg_lambda"] = _rowcall(
        "rg_dgates", gates_bwd, rows, gate_pars, [(rg_w, F32)],
        [(2, rg_w, rg_w), (2, rg_w, rg_w), (2, rg_w), (2, rg_w), (2, rg_w)])
    dxa, g["rg_conv_w"], g["rg_conv_b"] = _conv_bwd("rg_dconv", proj0, 0, conv_w, dxc)
    dproj0 = jnp.concatenate([dxa, dga, dq, df_f, df_b, div, dg], axis=1)
    g_ab_in = _dw("l0_dwin", y0, dproj0, out_split=N_CHIPS, tk=4096)
    g["ab_w_in"] = g_ab_in[0]
    send_off("ab", [g_ab_in])
    (grad_x,), g_nmix0 = _dy_norm_bwd("l0_dy", dproj0, w["ab_w_in"], h_a0, gain, dh[0],
                                      pin=pins.pop() if pins else None, twice=False, k_whole=True)

    g["norm_mix"] = jnp.concatenate([g_nmix0, g_nmix1], axis=0)
    g["norm_mlp"] = jnp.concatenate([g_nmlp0, g_nmlp1], axis=0)
    g["mlp_w1"] = [g_w1_0[0], g_w1_1[0]]
    g["mlp_w2"] = [g_w2_0[0], g_w2_1[0]]
    return loss, grad_x, g


def _block_diag(w):
    d, g, n, _ = w.shape
    eye = jnp.eye(g, dtype=w.dtype)
    return (w[:, :, :, None, :] * eye[None, :, None, :, None]).reshape(d, g * n, g * n)


def _block_diag_extract(wbd, g):
    d, gn, _ = wbd.shape
    n = gn // g
    blocks = wbd.reshape(d, g, n, g, n)
    return jnp.stack([blocks[:, i, :, i, :] for i in range(g)], axis=1)


def _prepare_weights(big, full):
    w = {k: full[k] for k in ("norm_mix", "norm_mlp", "norm_final", "hg_lb_logits")}
    for k in ("rg_conv_w", "rg_conv_b", "rg_b_a", "rg_b_x", "rg_lambda", "hg_norm", "gla_b_gate", "gla_norm"):
        w[k] = full[k][0]
    w["rg_wa_bd"] = _block_diag(full["rg_w_a"][0])
    w["rg_wx_bd"] = _block_diag(full["rg_w_x"][0])
    up = full["gla_w_gate_up"][0]
    rank = up.shape[1]
    pad = jnp.zeros((2, LANES, up.shape[2]), F32)
    w["gla_w_up_pad"] = pad.at[0, 0:rank].set(up[0]).at[1, rank:2 * rank].set(up[1])
    w.update(_prepare_matrices(big))
    return w


def _prepare_matrices(big):
    w = {}
    if "mlp_w1" in big:
        w["mlp_w1"] = list(big["mlp_w1"])
        w["mlp_w2"] = [t.reshape(-1, t.shape[-1]) for t in big["mlp_w2"]]
    if "ab_w_in" in big:
        w["ab_w_in"] = big["ab_w_in"]
    if "ab_w_out" in big:
        w["ab_w_out"] = big["ab_w_out"].reshape(-1, big["ab_w_out"].shape[-1])
    if "gla_w_in" in big:
        w["gla_w_out"] = big["gla_w_out"].reshape(-1, big["gla_w_out"].shape[-1])
        gla_in = _join_chips(big["gla_w_in"], 1)
        w["gla_w_in_pad"] = jnp.pad(gla_in, ((0, 0), (0, GLA_IN_PAD - gla_in.shape[1])))
    return w


def _finish_grads(g, rank=16, rg_blocks=8):
    def chip_major(t):
        return t.reshape(N_CHIPS, t.shape[0] // N_CHIPS, t.shape[1])

    big = {
        "mlp_w1": list(g["mlp_w1"]), "mlp_w2": [chip_major(t) for t in g["mlp_w2"]],
        "ab_w_in": g["ab_w_in"], "ab_w_out": chip_major(g["ab_w_out"]),
        "gla_w_in": g["gla_w_in"], "gla_w_out": chip_major(g["gla_w_out"]),
    }
    small = {
        "norm_mix": g["norm_mix"], "norm_mlp": g["norm_mlp"], "norm_final": g["norm_final"][0],
        "rg_conv_w": g["rg_conv_w"][None], "rg_conv_b": g["rg_conv_b"],
        "rg_w_a": _block_diag_extract(g["rg_wa_bd"], rg_blocks)[None], "rg_b_a": g["rg_b_a"][None],
        "rg_w_x": _block_diag_extract(g["rg_wx_bd"], rg_blocks)[None], "rg_b_x": g["rg_b_x"][None],
        "rg_lambda": g["rg_lambda"][None], "hg_lb_logits": g["hg_lb_logits"], "hg_norm": g["hg_norm"],
        "gla_w_gate_up": jnp.stack([g["gla_w_up_pad"][0, 0:rank], g["gla_w_up_pad"][1, rank:2 * rank]])[None],
        "gla_b_gate": g["gla_b_gate"][None], "gla_norm": g["gla_norm"],
    }
    return big, small


MATRICES = (("mlp_w1", 0), ("mlp_w1", 1), ("mlp_w2", 0), ("mlp_w2", 1), ("ab_w_in", 0), ("ab_w_out", 0),
            ("gla_w_in", 0), ("gla_w_out", 0))
EARLY_MATRICES = ("ab_w_in",)
SMALL_SHARDED = ("rg_conv_w", "rg_b_a", "rg_b_x", "rg_lambda", "gla_w_gate_up", "gla_b_gate", "gla_norm")
SMALL_REPLICATED = ("norm_mix", "norm_mlp", "norm_final", "rg_conv_b", "rg_w_a", "rg_w_x", "hg_lb_logits", "hg_norm")
WEIGHTS = ("norm_mix", "norm_mlp", "norm_final", "mlp_w1", "mlp_w2", "ab_w_in", "ab_w_out", "rg_conv_w", "rg_conv_b",
           "rg_w_a", "rg_b_a", "rg_w_x", "rg_b_x", "rg_lambda", "hg_lb_logits", "hg_norm", "gla_w_in", "gla_w_out",
           "gla_w_gate_up", "gla_b_gate", "gla_norm")
ROW_ALIGN = 16


def _pack(arrays, lead=0):
    head = arrays[0].shape[:lead]
    flat = jnp.concatenate([a.reshape(head + (-1,)) for a in arrays], axis=lead)
    n = flat.shape[-1]
    quantum = LANES * ROW_ALIGN
    padded = -(-n // quantum) * quantum
    if padded != n:
        flat = jnp.pad(flat, [(0, 0)] * lead + [(0, padded - n)])
    return flat.reshape(head + (padded // LANES, LANES))


def _unpack(buf, shapes, lead=0):
    head = buf.shape[:lead]
    flat = buf.reshape(head + (-1,))
    out, off = [], 0
    for s in shapes:
        n = 1
        for v in s:
            n *= v
        out.append(lax.slice_in_dim(flat, off, off + n, axis=lead).reshape(head + tuple(s)))
        off += n
    return out


def _join_chips(gathered, axis):
    t = jnp.moveaxis(gathered, 0, axis)
    return t.reshape(t.shape[:axis] + (t.shape[axis] * t.shape[axis + 1],) + t.shape[axis + 2:])


def _split_chips(full, axis):
    s = full.shape
    t = full.reshape(s[:axis] + (N_CHIPS, s[axis] // N_CHIPS) + s[axis + 1:])
    return jnp.moveaxis(t, axis, 0)


_ANY = pl.BlockSpec(memory_space=pl.ANY)


def _place():
    return lax.axis_index("x"), lax.axis_index("y"), lax.axis_index("c")


def _into_slot(name, src, slot, n_slots, dtype, tm, layer=None):
    r, lanes = src.shape[-2:]
    tm = _row_tile(r, tm, ROW_ALIGN)

    def body(slot_ref, in_ref, o_ref):
        o_ref[...] = in_ref[...].astype(o_ref.dtype)

    if layer is None:
        in_spec = pl.BlockSpec((tm, lanes), lambda i, slot_ref: (i, 0))
    else:
        in_spec = pl.BlockSpec((None, tm, lanes), lambda i, slot_ref: (layer, i, 0))
    grid_spec = pltpu.PrefetchScalarGridSpec(
        num_scalar_prefetch=1, grid=(r // tm,), in_specs=[in_spec],
        out_specs=pl.BlockSpec((None, tm, lanes), lambda i, slot_ref: (slot_ref[0], i, 0)))
    return pl.pallas_call(
        body, name=name, grid_spec=grid_spec, out_shape=jax.ShapeDtypeStruct((n_slots, r, lanes), dtype),
        compiler_params=_params(("parallel",)),
    )(slot.reshape(1).astype(jnp.int32), src)


def _chip_peers():
    x, y, c = _place()
    return 2 * x + y, c, [(1 - x, y), (x, 1 - y), (1 - x, 1 - y)]


def _comm_call(name, body, ins, out_shapes, n_sems, aliases=None):
    return pl.pallas_call(
        body, name=name, in_specs=[_ANY] * len(ins), out_specs=[_ANY] * len(out_shapes), out_shape=out_shapes,
        input_output_aliases=aliases or {},
        scratch_shapes=[pltpu.SemaphoreType.DMA((n_sems,)), pltpu.SemaphoreType.DMA((n_sems,))],
    )(*ins)


def _gather_chips(name, bufs):
    n = len(bufs)

    def body(*refs):
        outs, send_sems, recv_sems = refs[n:2 * n], refs[2 * n], refs[2 * n + 1]
        x, y, c = _place()
        me, _, peers = _chip_peers()

        def rows(a, block, half):
            rh = outs[a].shape[1] // 2
            return outs[a].at[block, pl.ds(half * rh, rh)]

        def copy(a, j, block, half, to, sem):
            return pltpu.make_async_remote_copy(
                src_ref=rows(a, block, half), dst_ref=rows(a, block, half), send_sem=send_sems.at[sem],
                recv_sem=recv_sems.at[sem], device_id=to, device_id_type=MESH)

        def over_ici(a, j, block):
            px, py = peers[j]
            return copy(a, j, block, c, (px, py, c), 6 * a + j)

        def to_sibling(a, j, block, half):
            return copy(a, j, block, half, (x, y, 1 - c), 6 * a + 3 + j)

        sends = [over_ici(a, j, me) for a in range(n) for j in range(3)]
        for cp in sends:
            cp.start()
        for a in range(n):
            for j, (px, py) in enumerate(peers):
                over_ici(a, j, 2 * px + py).wait_recv()
                handed = to_sibling(a, j, 2 * px + py, c)
                handed.start()
                sends.append(handed)
        for a in range(n):
            for j, (px, py) in enumerate(peers):
                to_sibling(a, j, 2 * px + py, 1 - c).wait_recv()
        for cp in sends:
            cp.wait_send()

    shapes = [jax.ShapeDtypeStruct(b.shape, b.dtype) for b in bufs]
    return _comm_call(name, body, bufs, shapes, 6 * n, {a: a for a in range(n)})


_HBM = pl.BlockSpec(memory_space=pltpu.HBM)
_SEM = pl.BlockSpec(memory_space=pltpu.SEMAPHORE)
_EFFECT = pltpu.SideEffectType.DATAFLOW_SIDE_EFFECTING


def _half_rows(ref, block, half):
    rh = ref.shape[1] // 2
    return ref.at[block, pl.ds(half * rh, rh)]


def _gather_start(name, bufs, after):
    n = len(bufs)

    def body(*refs):
        ins, send_sems, recv_sems, token = refs[:n], refs[n + 1], refs[n + 2], refs[-1]
        me, c, peers = _chip_peers()
        for a in range(n):
            mine = _half_rows(ins[a], me, c)
            for j, (px, py) in enumerate(peers):
                pltpu.make_async_remote_copy(
                    src_ref=mine, dst_ref=mine, send_sem=send_sems.at[3 * a + j], recv_sem=recv_sems.at[3 * a + j],
                    device_id=(px, py, c), device_id_type=MESH).start()
        token[...] = jnp.zeros_like(token)

    out_shape = (pltpu.SemaphoreType.DMA((3 * n,)), pltpu.SemaphoreType.DMA((3 * n,)),
                 *[pltpu.HBM(b.shape, b.dtype) for b in bufs], jax.ShapeDtypeStruct((8, LANES), F32))
    return pl.pallas_call(
        body, name=name, out_shape=out_shape, in_specs=[_HBM] * n + [_ANY],
        out_specs=(_SEM, _SEM, *[_HBM] * n, pl.BlockSpec(memory_space=pltpu.VMEM)),
        input_output_aliases={a: 2 + a for a in range(n)},
        compiler_params=pltpu.CompilerParams(has_side_effects=_EFFECT),
    )(*[pltpu.with_memory_space_constraint(b, pltpu.HBM) for b in bufs], after)


def _gather_wait(name, bufs, send_sems, recv_sems, after):
    n = len(bufs)

    def body(*refs):
        ins, send_sems, recv_sems = refs[:n], refs[n], refs[n + 1]
        me, c, peers = _chip_peers()
        for a in range(n):
            for j, (px, py) in enumerate(peers):
                copy = pltpu.make_async_remote_copy(
                    src_ref=_half_rows(ins[a], me, c), dst_ref=_half_rows(ins[a], 2 * px + py, c),
                    send_sem=send_sems.at[3 * a + j], recv_sem=recv_sems.at[3 * a + j],
                    device_id=(px, py, c), device_id_type=MESH)
                copy.wait_send()
                copy.wait_recv()

    return pl.pallas_call(
        body, name=name, out_shape=tuple(pltpu.HBM(b.shape, b.dtype) for b in bufs),
        in_specs=[_HBM] * n + [_SEM, _SEM, _ANY], out_specs=tuple([_HBM] * n),
        input_output_aliases={a: a for a in range(n)},
        compiler_params=pltpu.CompilerParams(has_side_effects=_EFFECT),
    )(*bufs, send_sems, recv_sems, after)


def _hand_over(name, bufs):
    n = len(bufs)

    def body(*refs):
        outs, send_sems, recv_sems = refs[n:2 * n], refs[2 * n], refs[2 * n + 1]
        x, y, c = _place()
        _, _, peers = _chip_peers()

        def copy(a, j, half):
            px, py = peers[j]
            rows = _half_rows(outs[a], 2 * px + py, half)
            return pltpu.make_async_remote_copy(
                src_ref=rows, dst_ref=rows, send_sem=send_sems.at[3 * a + j], recv_sem=recv_sems.at[3 * a + j],
                device_id=(x, y, 1 - c), device_id_type=MESH)

        sends = [copy(a, j, c) for a in range(n) for j in range(3)]
        for cp in sends:
            cp.start()
        for a in range(n):
            for j in range(3):
                copy(a, j, 1 - c).wait_recv()
        for cp in sends:
            cp.wait_send()

    shapes = [jax.ShapeDtypeStruct(b.shape, b.dtype) for b in bufs]
    return _comm_call(name, body, bufs, shapes, 3 * n, {a: a for a in range(n)})


def _pair_gather(name, bufs):
    n = len(bufs)

    def body(*refs):
        ins, outs, send_sems, recv_sems = refs[:n], refs[n:2 * n], refs[2 * n], refs[2 * n + 1]
        x, y, c = _place()

        def copy(a, block):
            return pltpu.make_async_remote_copy(
                src_ref=ins[a].at[block], dst_ref=outs[a].at[block], send_sem=send_sems.at[a],
                recv_sem=recv_sems.at[a], device_id=(x, y, 1 - c), device_id_type=MESH)

        sends = [copy(a, c) for a in range(n)]
        for cp in sends:
            cp.start()
        for a in range(n):
            copy(a, 1 - c).wait_recv()
        for cp in sends:
            cp.wait_send()

    shapes = [jax.ShapeDtypeStruct(b.shape, b.dtype) for b in bufs]
    return _comm_call(name, body, bufs, shapes, n, {a: a for a in range(n)})


def _all_peers():
    x, y, c = _place()
    peers = []
    for mask in range(1, N_DEV):
        fx, fy, fc = (mask >> 2) & 1, (mask >> 1) & 1, mask & 1
        peers.append((jnp.where(fx, 1 - x, x), jnp.where(fy, 1 - y, y), jnp.where(fc, 1 - c, c)))
    return 4 * x + 2 * y + c, peers


def _reduce_copies(srcs, lands, send_sems, recv_sems):
    me, peers = _all_peers()
    sends, arrivals = [], []
    for a in range(len(srcs)):
        for j, (px, py, pc) in enumerate(peers):
            k = (N_DEV - 1) * a + j
            sends.append(pltpu.make_async_remote_copy(
                src_ref=srcs[a].at[2 * px + py, pc], dst_ref=lands[a].at[me], send_sem=send_sems.at[k],
                recv_sem=recv_sems.at[k], device_id=(px, py, pc), device_id_type=MESH))
            arrivals.append(pltpu.make_async_remote_copy(
                src_ref=srcs[a].at[2 * px + py, pc], dst_ref=lands[a].at[4 * px + 2 * py + pc],
                send_sem=send_sems.at[k], recv_sem=recv_sems.at[k], device_id=(px, py, pc), device_id_type=MESH))
    return sends, arrivals


def _reduce_direct(name, srcs, pin=None):
    n = len(srcs)
    extra = [] if pin is None else [pin]

    def body(*refs):
        ins, outs = refs[:n], refs[n + len(extra):2 * n + len(extra)]
        sends, arrivals = _reduce_copies(ins, outs, refs[-2], refs[-1])
        for cp in sends:
            cp.start()
        for cp in arrivals:
            cp.wait_recv()
        for cp in sends:
            cp.wait_send()

    shapes = [jax.ShapeDtypeStruct((N_DEV,) + s.shape[2:], s.dtype) for s in srcs]
    return _comm_call(name, body, list(srcs) + extra, shapes, (N_DEV - 1) * n)


def _reduce_start(name, srcs):
    n = len(srcs)
    lands = [lax.empty((N_DEV,) + s.shape[2:], s.dtype) for s in srcs]

    def body(*refs):
        sends, _ = _reduce_copies(refs[:n], refs[n:2 * n], refs[2 * n], refs[2 * n + 1])
        for cp in sends:
            cp.start()
        refs[-1][...] = jnp.zeros_like(refs[-1])

    bufs = list(srcs) + lands
    n_sems = (N_DEV - 1) * n
    out_shape = (pltpu.SemaphoreType.DMA((n_sems,)), pltpu.SemaphoreType.DMA((n_sems,)),
                 *[pltpu.HBM(b.shape, b.dtype) for b in bufs], jax.ShapeDtypeStruct((8, LANES), F32))
    return pl.pallas_call(
        body, name=name, out_shape=out_shape, in_specs=[_HBM] * (2 * n),
        out_specs=(_SEM, _SEM, *[_HBM] * (2 * n), pl.BlockSpec(memory_space=pltpu.VMEM)),
        input_output_aliases={a: 2 + a for a in range(2 * n)},
        compiler_params=pltpu.CompilerParams(has_side_effects=_EFFECT),
    )(*[pltpu.with_memory_space_constraint(b, pltpu.HBM) for b in bufs])


def _reduce_wait(name, srcs, lands, send_sems, recv_sems, after):
    n = len(srcs)

    def body(*refs):
        sends, arrivals = _reduce_copies(refs[:n], refs[n:2 * n], refs[2 * n], refs[2 * n + 1])
        for cp in sends:
            cp.wait_send()
        for cp in arrivals:
            cp.wait_recv()

    bufs = list(srcs) + list(lands)
    outs = pl.pallas_call(
        body, name=name, out_shape=tuple(pltpu.HBM(b.shape, b.dtype) for b in bufs),
        in_specs=[_HBM] * (2 * n) + [_SEM, _SEM, _ANY], out_specs=tuple([_HBM] * (2 * n)),
        input_output_aliases={a: a for a in range(2 * n)},
        compiler_params=pltpu.CompilerParams(has_side_effects=_EFFECT),
    )(*bufs, send_sems, recv_sems, after)
    return list(outs[n:])


def _reduce_sum(name, own, land, chip, core):
    n, rh, lanes = land.shape
    tm = _row_tile(rh, 1024, ROW_ALIGN)

    def body(idx_ref, own_ref, *rest):
        total = own_ref[...]
        for g_ref in rest[:-1]:
            total = total + g_ref[...].astype(F32)
        rest[-1][...] = total

    def block(k):
        return pl.BlockSpec((None, tm, lanes), lambda i, idx_ref: ((2 * idx_ref[0] + idx_ref[1] + k) % n, i, 0))

    grid_spec = pltpu.PrefetchScalarGridSpec(
        num_scalar_prefetch=1, grid=(rh // tm,),
        in_specs=[pl.BlockSpec((None, None, tm, lanes), lambda i, idx_ref: (idx_ref[0], idx_ref[1], i, 0))]
        + [block(k) for k in range(1, n)],
        out_specs=pl.BlockSpec((None, tm, lanes), lambda i, idx_ref: (idx_ref[1], i, 0)))
    return pl.pallas_call(
        body, name=name, grid_spec=grid_spec, out_shape=jax.ShapeDtypeStruct((2, rh, lanes), F32),
        compiler_params=_params(("parallel",)),
    )(jnp.stack([chip, core]).astype(jnp.int32), own, *[land] * (n - 1))


def _gather_all_start(name, buf):
    def body(in_ref, send_sems, recv_sems, out_ref, token):
        me, peers = _all_peers()
        for j, peer in enumerate(peers):
            pltpu.make_async_remote_copy(
                src_ref=in_ref.at[me], dst_ref=in_ref.at[me], send_sem=send_sems.at[j], recv_sem=recv_sems.at[j],
                device_id=peer, device_id_type=MESH).start()
        token[...] = jnp.zeros_like(token)

    n = N_DEV - 1
    return pl.pallas_call(
        body, name=name, in_specs=[_HBM],
        out_shape=(pltpu.SemaphoreType.DMA((n,)), pltpu.SemaphoreType.DMA((n,)), pltpu.HBM(buf.shape, buf.dtype),
                   jax.ShapeDtypeStruct((8, LANES), F32)),
        out_specs=(_SEM, _SEM, _HBM, pl.BlockSpec(memory_space=pltpu.VMEM)), input_output_aliases={0: 2},
        compiler_params=pltpu.CompilerParams(has_side_effects=_EFFECT),
    )(pltpu.with_memory_space_constraint(buf, pltpu.HBM))


def _gather_all_wait(name, buf, send_sems, recv_sems, after):
    def body(in_ref, send_sems, recv_sems, after_ref, out_ref):
        me, peers = _all_peers()
        for j, (px, py, pc) in enumerate(peers):
            copy = pltpu.make_async_remote_copy(
                src_ref=in_ref.at[me], dst_ref=in_ref.at[4 * px + 2 * py + pc], send_sem=send_sems.at[j],
                recv_sem=recv_sems.at[j], device_id=(px, py, pc), device_id_type=MESH)
            copy.wait_send()
            copy.wait_recv()

    return pl.pallas_call(
        body, name=name, in_specs=[_HBM, _SEM, _SEM, _ANY], out_shape=pltpu.HBM(buf.shape, buf.dtype),
        out_specs=_HBM, input_output_aliases={0: 0},
        compiler_params=pltpu.CompilerParams(has_side_effects=_EFFECT),
    )(buf, send_sems, recv_sems, after)


def _sum_blocks(name, stacked, tm):
    n, r, lanes = stacked.shape

    def body(in_ref, o_ref):
        acc = in_ref[0]
        for j in range(1, n):
            acc = acc + in_ref[j]
        o_ref[...] = acc

    return pl.pallas_call(
        body, name=name, grid=(r // tm,), in_specs=[pl.BlockSpec((n, tm, lanes), lambda i: (0, i, 0))],
        out_specs=pl.BlockSpec((tm, lanes), lambda i: (i, 0)), out_shape=jax.ShapeDtypeStruct((r, lanes), F32),
        compiler_params=_params(("parallel",)),
    )(stacked)


def _row_tile(rows, pref, align):
    best = None
    for t in range(align, min(rows, pref) + 1, align):
        if rows % t == 0:
            best = t
    assert best is not None, (rows, pref, align)
    return best


def _adam(name, w, g, m, v):
    rows, width = w.shape
    tm = _row_tile(rows, max(8, 4096 * LANES // width), 8)
    args = [(t, width, 0) for t in (w, g, m, v)]
    return _rowcall(name, _adam_fn, args, [], [(width, F32)] * 3, tm=tm)


def kernel(x, norm_mix, norm_mlp, norm_final, mlp_w1, mlp_w2, ab_w_in, ab_w_out, rg_conv_w, rg_conv_b, rg_w_a, rg_b_a, rg_w_x, rg_b_x, rg_lambda, hg_lb_logits, hg_norm, gla_w_in, gla_w_out, gla_w_gate_up, gla_b_gate, gla_norm, loss_target, m_norm_mix, m_norm_mlp, m_norm_final, m_mlp_w1, m_mlp_w2, m_ab_w_in, m_ab_w_out, m_rg_conv_w, m_rg_conv_b, m_rg_w_a, m_rg_b_a, m_rg_w_x, m_rg_b_x, m_rg_lambda, m_hg_lb_logits, m_hg_norm, m_gla_w_in, m_gla_w_out, m_gla_w_gate_up, m_gla_b_gate, m_gla_norm, v_norm_mix, v_norm_mlp, v_norm_final, v_mlp_w1, v_mlp_w2, v_ab_w_in, v_ab_w_out, v_rg_conv_w, v_rg_conv_b, v_rg_w_a, v_rg_b_a, v_rg_w_x, v_rg_b_x, v_rg_lambda, v_hg_lb_logits, v_hg_norm, v_gla_w_in, v_gla_w_out, v_gla_w_gate_up, v_gla_b_gate, v_gla_norm):
    w = dict(norm_mix=norm_mix, norm_mlp=norm_mlp, norm_final=norm_final, mlp_w1=mlp_w1, mlp_w2=mlp_w2, ab_w_in=ab_w_in, ab_w_out=ab_w_out, rg_conv_w=rg_conv_w, rg_conv_b=rg_conv_b, rg_w_a=rg_w_a, rg_b_a=rg_b_a, rg_w_x=rg_w_x, rg_b_x=rg_b_x, rg_lambda=rg_lambda, hg_lb_logits=hg_lb_logits, hg_norm=hg_norm, gla_w_in=gla_w_in, gla_w_out=gla_w_out, gla_w_gate_up=gla_w_gate_up, gla_b_gate=gla_b_gate, gla_norm=gla_norm)
    m = dict(norm_mix=m_norm_mix, norm_mlp=m_norm_mlp, norm_final=m_norm_final, mlp_w1=m_mlp_w1, mlp_w2=m_mlp_w2, ab_w_in=m_ab_w_in, ab_w_out=m_ab_w_out, rg_conv_w=m_rg_conv_w, rg_conv_b=m_rg_conv_b, rg_w_a=m_rg_w_a, rg_b_a=m_rg_b_a, rg_w_x=m_rg_w_x, rg_b_x=m_rg_b_x, rg_lambda=m_rg_lambda, hg_lb_logits=m_hg_lb_logits, hg_norm=m_hg_norm, gla_w_in=m_gla_w_in, gla_w_out=m_gla_w_out, gla_w_gate_up=m_gla_w_gate_up, gla_b_gate=m_gla_b_gate, gla_norm=m_gla_norm)
    v = dict(norm_mix=v_norm_mix, norm_mlp=v_norm_mlp, norm_final=v_norm_final, mlp_w1=v_mlp_w1, mlp_w2=v_mlp_w2, ab_w_in=v_ab_w_in, ab_w_out=v_ab_w_out, rg_conv_w=v_rg_conv_w, rg_conv_b=v_rg_conv_b, rg_w_a=v_rg_w_a, rg_b_a=v_rg_b_a, rg_w_x=v_rg_w_x, rg_b_x=v_rg_b_x, rg_lambda=v_rg_lambda, hg_lb_logits=v_hg_lb_logits, hg_norm=v_hg_norm, gla_w_in=v_gla_w_in, gla_w_out=v_gla_w_out, gla_w_gate_up=v_gla_w_gate_up, gla_b_gate=v_gla_b_gate, gla_norm=v_gla_norm)
    chip = 2 * lax.axis_index("x") + lax.axis_index("y")
    core = lax.axis_index("c")
    sharded_shapes = [w[n].shape for n in SMALL_SHARDED]

    slots = [_into_slot(f"cast_{n}{layer}", w[n], chip, N_CHIPS, BF16, 512, layer) for n, layer in MATRICES]
    early = [i for i, (n, _) in enumerate(MATRICES) if n in EARLY_MATRICES]
    rest = [i for i in range(len(MATRICES)) if i not in early]

    def named(indices, arrays):
        big = {}
        for i, t in zip(indices, arrays):
            big.setdefault(MATRICES[i][0], []).append(t)
        return {n: (v if n in ("mlp_w1", "mlp_w2") else v[0]) for n, v in big.items()}

    vectors = _pack([w[n] for n in SMALL_SHARDED])
    vectors = _into_slot("place_vectors", vectors, chip, N_CHIPS, F32, vectors.shape[0])
    *gathered, vectors = _gather_chips("gather_early", [slots[i] for i in early] + [vectors])
    send_sems, recv_sems, *in_flight, token = _gather_start("gather_rest_start", [slots[i] for i in rest], gathered[0])

    def late_weights(after):
        landed = _gather_wait("gather_rest_wait", in_flight, send_sems, recv_sems, after)
        return _prepare_matrices(named(rest, _hand_over("gather_rest_share", list(landed))))

    big = named(early, gathered)
    small_all = _unpack(vectors, sharded_shapes, lead=1)
    full = {n: w[n] for n in SMALL_REPLICATED}
    for n, t in zip(SMALL_SHARDED, small_all):
        full[n] = _join_chips(t, t.ndim - 2)

    def halves(t):
        return t.reshape(N_CHIPS, 2, t.shape[1] // 2, t.shape[2])

    in_flight_grads = {}

    def emit(tag, arrays32, arrays16):
        n = len(arrays16)
        send, recv, *rest = _reduce_start(f"reduce_{tag}_start", [halves(t) for t in arrays16])
        in_flight_grads[tag] = ([halves(t) for t in arrays32], rest[:n], rest[n:2 * n], send, recv)
        return rest[-1]

    loss_part, grad_x, g_kernel = _local_step(
        x[0], loss_target[0], _prepare_weights(big, full), token, late_weights, emit)
    g_big, g_full = _finish_grads(g_kernel)

    small_names = SMALL_REPLICATED + SMALL_SHARDED
    reduced_shapes = [g_full[n].shape for n in small_names] + [loss_part.shape]
    g_small = _pack([g_full[n] for n in small_names] + [loss_part])
    device = 2 * chip + core
    g_small = _into_slot("place_small", g_small, device, N_DEV, F32, g_small.shape[0])
    small_send, small_recv, small_in_flight, small_token = _gather_all_start("reduce_small_start", g_small)

    mine = {}
    for tag, (own, srcs, lands, send, recv) in in_flight_grads.items():
        landed = _reduce_wait(f"reduce_{tag}_wait", srcs, lands, send, recv, small_token)
        mine[tag] = [_reduce_sum(f"reduce_add_{tag}{i}", o, f, chip, core) for i, (o, f) in enumerate(zip(own, landed))]
    ordered = [mine["mlp0"][0], mine["mlp1"][0], mine["mlp0"][1], mine["mlp1"][1], mine["ab"][0], mine["mlp0"][2],
               *mine["gla"]]
    reduced = [t.reshape(2 * t.shape[1], t.shape[2]) for t in _pair_gather("reduce_share", ordered)]
    by_name = {n: [] for n, _ in MATRICES}
    for (n, _), t in zip(MATRICES, reduced):
        by_name[n].append(t)
    grads = {n: jnp.stack(v) for n, v in by_name.items()}

    g_small_all = _gather_all_wait("reduce_small_wait", small_in_flight, small_send, small_recv, reduced[0])
    g_small_red = _sum_blocks("reduce_small_add", g_small_all, g_small_all.shape[1])
    *small_red, loss_sum = _unpack(g_small_red, reduced_shapes)
    loss = loss_sum[0, 0]
    g_small_full = dict(zip(small_names, small_red))
    for n in SMALL_REPLICATED:
        grads[n] = g_small_full[n]
    for n in SMALL_SHARDED:
        width = w[n].shape[-1]
        grads[n] = lax.dynamic_slice_in_dim(g_small_full[n], chip * width, width, axis=g_small_full[n].ndim - 1)

    delta, new_m, new_v = {}, {}, {}
    for n in by_name:
        flat = [t.reshape(-1, t.shape[-1]) for t in (w[n], grads[n], m[n], v[n])]
        for dst, t in zip((delta, new_m, new_v), _adam(f"adam_{n}", *flat)):
            dst[n] = t.reshape(w[n].shape)
    small_shapes = [w[n].shape for n in small_names]
    packs = [_pack([src[n] for n in small_names]) for src in (w, grads, m, v)]
    d_small, m_small, v_small = _adam("adam_small", *packs)
    for dst, buf in ((delta, d_small), (new_m, m_small), (new_v, v_small)):
        dst.update(zip(small_names, _unpack(buf, small_shapes)))

    return (loss, grad_x[None], *[grads[n] for n in WEIGHTS], *[delta[n] for n in WEIGHTS],
            *[new_m[n] for n in WEIGHTS], *[new_v[n] for n in WEIGHTS])
```
